```python
import math
import jax, jax.numpy as jnp
from jax import lax
import numpy as np

D_MODEL = 1024
BATCH = 8
SEQ = 2048
DEPTH = 2

GRID_W = 64
NA_HEADS = 8
NA_HEAD_DIM = 64
NA_WIN_ROWS = 8
NA_WIN_COLS = 16
NA_QCOL_BLOCK = NA_WIN_COLS
NA_KCOL_BLOCK = 2 * NA_WIN_COLS
SW_HEADS = 8
SW_KV_HEADS = 2
SW_HEAD_DIM = 64
SW_WINDOW = 128
SW_BLOCK = 128
REL_BUCKETS = 32
REL_MAX_DIST = 128
D_FF = 2816
N_BRANCHES = 2
EPS = 1e-6
NEG = -1e30

NA_WIDTH = NA_HEADS * NA_HEAD_DIM
SW_Q_WIDTH = SW_HEADS * SW_HEAD_DIM
SW_KV_WIDTH = SW_KV_HEADS * SW_HEAD_DIM
IN_WIDTH = 3 * NA_WIDTH + SW_Q_WIDTH + 2 * SW_KV_WIDTH + N_BRANCHES * D_MODEL

kernel_name = "hybrid_natten_swa_macaron_encoder"


def rms_norm(x, g):
    xf = x.astype(jnp.float32)
    y = xf * lax.rsqrt(jnp.mean(xf * xf, axis=-1, keepdims=True) + EPS)
    return (y * g.astype(jnp.float32)).astype(x.dtype)


def swiglu(x, w_gate, w_up, w_down):
    return (jax.nn.silu(x @ w_gate) * (x @ w_up)) @ w_down


def t5_bucket(rel):
    nb = REL_BUCKETS // 2
    max_exact = nb // 2
    n = np.abs(rel)
    large = max_exact + (np.log(np.maximum(n, 1) / max_exact)
                         / np.log(REL_MAX_DIST / max_exact) * (nb - max_exact)).astype(np.int32)
    large = np.minimum(large, nb - 1)
    return ((rel > 0) * nb + np.where(n < max_exact, n, large)).astype(np.int32)


def neighbourhood_attention(q, k, v, rpb):
    B, S, H, dh = q.shape
    rows = S // GRID_W
    kr = min(NA_WIN_ROWS, rows)
    ncb = GRID_W // NA_QCOL_BLOCK
    r = np.arange(rows)
    row_start = np.clip(r - kr // 2, 0, rows - kr)
    key_rows = row_start[:, None] + np.arange(kr)
    cb = np.arange(ncb)
    kcol_start = np.clip(cb * NA_QCOL_BLOCK - NA_WIN_COLS // 2, 0, GRID_W - NA_KCOL_BLOCK)
    key_cols = kcol_start[:, None] + np.arange(NA_KCOL_BLOCK)
    key_idx = (key_rows[:, None, :, None] * GRID_W + key_cols[None, :, None, :])
    key_idx = key_idx.reshape(rows, ncb, kr * NA_KCOL_BLOCK)
    kg = k[:, key_idx]
    vg = v[:, key_idx]
    qb = q.reshape(B, rows, ncb, NA_QCOL_BLOCK, H, dh)
    s = jnp.einsum('brcqhd,brckhd->bhrcqk', qb, kg).astype(jnp.float32) / math.sqrt(dh)
    q_cols = cb[:, None] * NA_QCOL_BLOCK + np.arange(NA_QCOL_BLOCK)
    q_col_start = np.clip(q_cols - NA_WIN_COLS // 2, 0, GRID_W - NA_WIN_COLS)
    kc = key_cols[:, None, :]
    col_ok = (kc >= q_col_start[..., None]) & (kc < q_col_start[..., None] + NA_WIN_COLS)
    row_idx = key_rows - r[:, None] + NA_WIN_ROWS - 1
    col_idx = np.clip(kc - q_cols[..., None] + NA_WIN_COLS - 1, 0, 2 * NA_WIN_COLS - 2)
    bias = rpb[:, row_idx[:, None, None, :, None], col_idx[None, :, :, None, :]]
    bias = bias.reshape(H, rows, ncb, NA_QCOL_BLOCK, kr * NA_KCOL_BLOCK)
    mask = np.broadcast_to(col_ok[None, :, :, None, :],
                           (rows, ncb, NA_QCOL_BLOCK, kr, NA_KCOL_BLOCK))
    mask = mask.reshape(rows, ncb, NA_QCOL_BLOCK, kr * NA_KCOL_BLOCK)
    s = jnp.where(mask, s + bias.astype(jnp.float32), NEG)
    p = jax.nn.softmax(s, axis=-1)
    o = jnp.einsum('bhrcqk,brckhd->brcqhd', p.astype(v.dtype), vg)
    return o.reshape(B, S, H * dh)


def sliding_window_gqa(q, k, v, rel_bias, sink):
    B, S, _, dh = q.shape
    nb = S // SW_BLOCK
    rep = SW_HEADS // SW_KV_HEADS
    qb = q.reshape(B, nb, SW_BLOCK, SW_KV_HEADS, rep, dh)
    pad = ((0, 0), (SW_BLOCK, SW_BLOCK), (0, 0), (0, 0))
    kp = jnp.pad(k, pad).reshape(B, nb + 2, SW_BLOCK, SW_KV_HEADS, dh)
    vp = jnp.pad(v, pad).reshape(B, nb + 2, SW_BLOCK, SW_KV_HEADS, dh)
    kb = jnp.concatenate([kp[:, :-2], kp[:, 1:-1], kp[:, 2:]], axis=2)
    vb = jnp.concatenate([vp[:, :-2], vp[:, 1:-1], vp[:, 2:]], axis=2)
    s = jnp.einsum('bnqgrd,bnkgd->bgrnqk', qb, kb).astype(jnp.float32) / math.sqrt(dh)
    a = np.arange(SW_BLOCK)[:, None]
    j = np.arange(3 * SW_BLOCK)[None, :]
    rel = j - SW_BLOCK - a
    kpos = (np.arange(nb)[:, None, None] - 1) * SW_BLOCK + j[None]
    mask = (np.abs(rel)[None] <= SW_WINDOW) & (kpos >= 0) & (kpos < S)
    bias = rel_bias.astype(jnp.float32).reshape(SW_KV_HEADS, rep, 1, SW_BLOCK, 3 * SW_BLOCK)
    s = jnp.where(mask, s + bias, NEG)
    sink_b = sink.astype(jnp.float32).reshape(SW_KV_HEADS, rep, 1, 1, 1)
    m = jnp.maximum(s.max(axis=-1, keepdims=True), sink_b)
    e = jnp.exp(s - m)
    p = e / (e.sum(axis=-1, keepdims=True) + jnp.exp(sink_b - m))
    o = jnp.einsum('bgrnqk,bnkgd->bnqgrd', p.astype(v.dtype), vb)
    return o.reshape(B, S, SW_HEADS * dh)


def _fwd_setup_inputs(seed: int = 0) -> dict:
    key = jax.random.key(seed)
    ks = jax.random.split(key, 24)
    L, D, F = DEPTH, D_MODEL, D_FF
    nrm = lambda k, shape, fan: jax.random.normal(k, shape, jnp.float32) * fan ** -0.5
    gain = lambda k, shape: 1.0 + 0.05 * jax.random.normal(k, shape, jnp.float32)
    return {
        "x": jax.random.normal(ks[0], (BATCH, SEQ, D), jnp.float32),
        "ffn1_norm": gain(ks[1], (L, D)),
        "ffn1_w_gate": nrm(ks[2], (L, D, F), D),
        "ffn1_w_up": nrm(ks[3], (L, D, F), D),
        "ffn1_w_down": nrm(ks[4], (L, F, D), F),
        "mix_norm": gain(ks[5], (L, D)),
        "w_in": nrm(ks[6], (L, D, IN_WIDTH), D),
        "b_gate": 0.01 * jax.random.normal(ks[7], (L, N_BRANCHES * D), jnp.float32),
        "na_q_norm": gain(ks[8], (L, NA_HEAD_DIM)),
        "na_k_norm": gain(ks[9], (L, NA_HEAD_DIM)),
        "na_rpb": 0.1 * jax.random.normal(ks[10], (L, NA_HEADS, 2 * NA_WIN_ROWS - 1, 2 * NA_WIN_COLS - 1), jnp.float32),
        "sw_q_norm": gain(ks[11], (L, SW_HEAD_DIM)),
        "sw_k_norm": gain(ks[12], (L, SW_HEAD_DIM)),
        "sw_sink": 0.5 * jax.random.normal(ks[13], (L, SW_HEADS), jnp.float32),
        "t5_rel_table": 0.1 * jax.random.normal(ks[14], (REL_BUCKETS, SW_HEADS), jnp.float32),
        "w_branch_na": nrm(ks[15], (L, NA_WIDTH, D), NA_WIDTH),
        "w_branch_sw": nrm(ks[16], (L, SW_Q_WIDTH, D), SW_Q_WIDTH),
        "w_out": nrm(ks[17], (L, D, D), D),
        "ffn2_norm": gain(ks[18], (L, D)),
        "ffn2_w_gate": nrm(ks[19], (L, D, F), D),
        "ffn2_w_up": nrm(ks[20], (L, D, F), D),
        "ffn2_w_down": nrm(ks[21], (L, F, D), F),
    }


def _fwd_reference(x, ffn1_norm, ffn1_w_gate, ffn1_w_up, ffn1_w_down, mix_norm, w_in, b_gate,
              na_q_norm, na_k_norm, na_rpb, sw_q_norm, sw_k_norm, sw_sink, t5_rel_table,
              w_branch_na, w_branch_sw, w_out, ffn2_norm, ffn2_w_gate, ffn2_w_up, ffn2_w_down):
    B, S, D = x.shape
    rel = np.arange(3 * SW_BLOCK)[None, :] - SW_BLOCK - np.arange(SW_BLOCK)[:, None]
    t5_bias = jnp.transpose(t5_rel_table[t5_bucket(rel)], (2, 0, 1))
    splits = np.cumsum([NA_WIDTH, NA_WIDTH, NA_WIDTH, SW_Q_WIDTH, SW_KV_WIDTH, SW_KV_WIDTH])
    for l in range(DEPTH):
        x = x + 0.5 * swiglu(rms_norm(x, ffn1_norm[l]), ffn1_w_gate[l], ffn1_w_up[l], ffn1_w_down[l])
        h = rms_norm(x, mix_norm[l])
        z = h @ w_in[l]
        qa, ka, va, qs, ks_, vs, zg = jnp.split(z, splits, axis=-1)
        qa = rms_norm(qa.reshape(B, S, NA_HEADS, NA_HEAD_DIM), na_q_norm[l])
        ka = rms_norm(ka.reshape(B, S, NA_HEADS, NA_HEAD_DIM), na_k_norm[l])
        va = va.reshape(B, S, NA_HEADS, NA_HEAD_DIM)
        o_na = neighbourhood_attention(qa, ka, va, na_rpb[l])
        qs = rms_norm(qs.reshape(B, S, SW_HEADS, SW_HEAD_DIM), sw_q_norm[l])
        ks_ = rms_norm(ks_.reshape(B, S, SW_KV_HEADS, SW_HEAD_DIM), sw_k_norm[l])
        vs = vs.reshape(B, S, SW_KV_HEADS, SW_HEAD_DIM)
        o_sw = sliding_window_gqa(qs, ks_, vs, t5_bias, sw_sink[l])
        g = jax.nn.sigmoid((zg + b_gate[l]).astype(jnp.float32)).astype(x.dtype)
        g = g.reshape(B, S, N_BRANCHES, D)
        merged = g[:, :, 0] * (o_na @ w_branch_na[l]) + g[:, :, 1] * (o_sw @ w_branch_sw[l])
        x = x + merged @ w_out[l]
        x = x + 0.5 * swiglu(rms_norm(x, ffn2_norm[l]), ffn2_w_gate[l], ffn2_w_up[l], ffn2_w_down[l])
    return x


import jax as _jax
import jax.numpy as _jnp

TWIN_FORMAT = 'train_step'
FWD_PARAMS = ['x', 'ffn1_norm', 'ffn1_w_gate', 'ffn1_w_up', 'ffn1_w_down', 'mix_norm', 'w_in', 'b_gate', 'na_q_norm', 'na_k_norm', 'na_rpb', 'sw_q_norm', 'sw_k_norm', 'sw_sink', 't5_rel_table', 'w_branch_na', 'w_branch_sw', 'w_out', 'ffn2_norm', 'ffn2_w_gate', 'ffn2_w_up', 'ffn2_w_down']
TWIN_WEIGHTS = ['ffn1_norm', 'ffn1_w_gate', 'ffn1_w_up', 'ffn1_w_down', 'mix_norm', 'w_in', 'b_gate', 'na_q_norm', 'na_k_norm', 'na_rpb', 'sw_q_norm', 'sw_k_norm', 'sw_sink', 't5_rel_table', 'w_branch_na', 'w_branch_sw', 'w_out', 'ffn2_norm', 'ffn2_w_gate', 'ffn2_w_up', 'ffn2_w_down']
TWIN_DIFF_INPUT = 'x'
TWIN_INPUTS = ['x', 'ffn1_norm', 'ffn1_w_gate', 'ffn1_w_up', 'ffn1_w_down', 'mix_norm', 'w_in', 'b_gate', 'na_q_norm', 'na_k_norm', 'na_rpb', 'sw_q_norm', 'sw_k_norm', 'sw_sink', 't5_rel_table', 'w_branch_na', 'w_branch_sw', 'w_out', 'ffn2_norm', 'ffn2_w_gate', 'ffn2_w_up', 'ffn2_w_down', 'loss_target', 'm_ffn1_norm', 'm_ffn1_w_gate', 'm_ffn1_w_up', 'm_ffn1_w_down', 'm_mix_norm', 'm_w_in', 'm_b_gate', 'm_na_q_norm', 'm_na_k_norm', 'm_na_rpb', 'm_sw_q_norm', 'm_sw_k_norm', 'm_sw_sink', 'm_t5_rel_table', 'm_w_branch_na', 'm_w_branch_sw', 'm_w_out', 'm_ffn2_norm', 'm_ffn2_w_gate', 'm_ffn2_w_up', 'm_ffn2_w_down', 'v_ffn1_norm', 'v_ffn1_w_gate', 'v_ffn1_w_up', 'v_ffn1_w_down', 'v_mix_norm', 'v_w_in', 'v_b_gate', 'v_na_q_norm', 'v_na_k_norm', 'v_na_rpb', 'v_sw_q_norm', 'v_sw_k_norm', 'v_sw_sink', 'v_t5_rel_table', 'v_w_branch_na', 'v_w_branch_sw', 'v_w_out', 'v_ffn2_norm', 'v_ffn2_w_gate', 'v_ffn2_w_up', 'v_ffn2_w_down']
TWIN_OUTPUTS = ['loss', 'grad_x', 'grad_ffn1_norm', 'grad_ffn1_w_gate', 'grad_ffn1_w_up', 'grad_ffn1_w_down', 'grad_mix_norm', 'grad_w_in', 'grad_b_gate', 'grad_na_q_norm', 'grad_na_k_norm', 'grad_na_rpb', 'grad_sw_q_norm', 'grad_sw_k_norm', 'grad_sw_sink', 'grad_t5_rel_table', 'grad_w_branch_na', 'grad_w_branch_sw', 'grad_w_out', 'grad_ffn2_norm', 'grad_ffn2_w_gate', 'grad_ffn2_w_up', 'grad_ffn2_w_down', 'delta_ffn1_norm', 'delta_ffn1_w_gate', 'delta_ffn1_w_up', 'delta_ffn1_w_down', 'delta_mix_norm', 'delta_w_in', 'delta_b_gate', 'delta_na_q_norm', 'delta_na_k_norm', 'delta_na_rpb', 'delta_sw_q_norm', 'delta_sw_k_norm', 'delta_sw_sink', 'delta_t5_rel_table', 'delta_w_branch_na', 'delta_w_branch_sw', 'delta_w_out', 'delta_ffn2_norm', 'delta_ffn2_w_gate', 'delta_ffn2_w_up', 'delta_ffn2_w_down', 'new_m_ffn1_norm', 'new_m_ffn1_w_gate', 'new_m_ffn1_w_up', 'new_m_ffn1_w_down', 'new_m_mix_norm', 'new_m_w_in', 'new_m_b_gate', 'new_m_na_q_norm', 'new_m_na_k_norm', 'new_m_na_rpb', 'new_m_sw_q_norm', 'new_m_sw_k_norm', 'new_m_sw_sink', 'new_m_t5_rel_table', 'new_m_w_branch_na', 'new_m_w_branch_sw', 'new_m_w_out', 'new_m_ffn2_norm', 'new_m_ffn2_w_gate', 'new_m_ffn2_w_up', 'new_m_ffn2_w_down', 'new_v_ffn1_norm', 'new_v_ffn1_w_gate', 'new_v_ffn1_w_up', 'new_v_ffn1_w_down', 'new_v_mix_norm', 'new_v_w_in', 'new_v_b_gate', 'new_v_na_q_norm', 'new_v_na_k_norm', 'new_v_na_rpb', 'new_v_sw_q_norm', 'new_v_sw_k_norm', 'new_v_sw_sink', 'new_v_t5_rel_table', 'new_v_w_branch_na', 'new_v_w_branch_sw', 'new_v_w_out', 'new_v_ffn2_norm', 'new_v_ffn2_w_gate', 'new_v_ffn2_w_up', 'new_v_ffn2_w_down']
TWIN_LEAF_KINDS = {'loss': 'loss', 'grad_x': 'grad_x', 'grad_ffn1_norm': 'grad_w', 'grad_ffn1_w_gate': 'grad_w', 'grad_ffn1_w_up': 'grad_w', 'grad_ffn1_w_down': 'grad_w', 'grad_mix_norm': 'grad_w', 'grad_w_in': 'grad_w', 'grad_b_gate': 'grad_w', 'grad_na_q_norm': 'grad_w', 'grad_na_k_norm': 'grad_w', 'grad_na_rpb': 'grad_w', 'grad_sw_q_norm': 'grad_w', 'grad_sw_k_norm': 'grad_w', 'grad_sw_sink': 'grad_w', 'grad_t5_rel_table': 'grad_w', 'grad_w_branch_na': 'grad_w', 'grad_w_branch_sw': 'grad_w', 'grad_w_out': 'grad_w', 'grad_ffn2_norm': 'grad_w', 'grad_ffn2_w_gate': 'grad_w', 'grad_ffn2_w_up': 'grad_w', 'grad_ffn2_w_down': 'grad_w', 'delta_ffn1_norm': 'delta_w', 'delta_ffn1_w_gate': 'delta_w', 'delta_ffn1_w_up': 'delta_w', 'delta_ffn1_w_down': 'delta_w', 'delta_mix_norm': 'delta_w', 'delta_w_in': 'delta_w', 'delta_b_gate': 'delta_w', 'delta_na_q_norm': 'delta_w', 'delta_na_k_norm': 'delta_w', 'delta_na_rpb': 'delta_w', 'delta_sw_q_norm': 'delta_w', 'delta_sw_k_norm': 'delta_w', 'delta_sw_sink': 'delta_w', 'delta_t5_rel_table': 'delta_w', 'delta_w_branch_na': 'delta_w', 'delta_w_branch_sw': 'delta_w', 'delta_w_out': 'delta_w', 'delta_ffn2_norm': 'delta_w', 'delta_ffn2_w_gate': 'delta_w', 'delta_ffn2_w_up': 'delta_w', 'delta_ffn2_w_down': 'delta_w', 'new_m_ffn1_norm': 'new_m', 'new_m_ffn1_w_gate': 'new_m', 'new_m_ffn1_w_up': 'new_m', 'new_m_ffn1_w_down': 'new_m', 'new_m_mix_norm': 'new_m', 'new_m_w_in': 'new_m', 'new_m_b_gate': 'new_m', 'new_m_na_q_norm': 'new_m', 'new_m_na_k_norm': 'new_m', 'new_m_na_rpb': 'new_m', 'new_m_sw_q_norm': 'new_m', 'new_m_sw_k_norm': 'new_m', 'new_m_sw_sink': 'new_m', 'new_m_t5_rel_table': 'new_m', 'new_m_w_branch_na': 'new_m', 'new_m_w_branch_sw': 'new_m', 'new_m_w_out': 'new_m', 'new_m_ffn2_norm': 'new_m', 'new_m_ffn2_w_gate': 'new_m', 'new_m_ffn2_w_up': 'new_m', 'new_m_ffn2_w_down': 'new_m', 'new_v_ffn1_norm': 'new_v', 'new_v_ffn1_w_gate': 'new_v', 'new_v_ffn1_w_up': 'new_v', 'new_v_ffn1_w_down': 'new_v', 'new_v_mix_norm': 'new_v', 'new_v_w_in': 'new_v', 'new_v_b_gate': 'new_v', 'new_v_na_q_norm': 'new_v', 'new_v_na_k_norm': 'new_v', 'new_v_na_rpb': 'new_v', 'new_v_sw_q_norm': 'new_v', 'new_v_sw_k_norm': 'new_v', 'new_v_sw_sink': 'new_v', 'new_v_t5_rel_table': 'new_v', 'new_v_w_branch_na': 'new_v', 'new_v_w_branch_sw': 'new_v', 'new_v_w_out': 'new_v', 'new_v_ffn2_norm': 'new_v', 'new_v_ffn2_w_gate': 'new_v', 'new_v_ffn2_w_up': 'new_v', 'new_v_ffn2_w_down': 'new_v'}


def _forward(args):
    return _fwd_reference(*[args[k] for k in FWD_PARAMS])


def _output_shape():
    out = _jax.eval_shape(lambda: _forward(_fwd_setup_inputs(0)))
    return out.shape, out.dtype

N_MICROBATCH = 1
ADAM_LR = 0.001
ADAM_B1 = 0.9
ADAM_B2 = 0.999
ADAM_EPS = 1e-08
ADAM_WD = 0.01
ADAM_STEP = 10
PER_EXAMPLE_BATCH_AXIS = {'x': 0, 'loss_target': 0}
SHARED_INPUTS = []
_WEIGHT_DTYPES = {'ffn1_norm': _jnp.float32, 'ffn1_w_gate': _jnp.float32, 'ffn1_w_up': _jnp.float32, 'ffn1_w_down': _jnp.float32, 'mix_norm': _jnp.float32, 'w_in': _jnp.float32, 'b_gate': _jnp.float32, 'na_q_norm': _jnp.float32, 'na_k_norm': _jnp.float32, 'na_rpb': _jnp.float32, 'sw_q_norm': _jnp.float32, 'sw_k_norm': _jnp.float32, 'sw_sink': _jnp.float32, 't5_rel_table': _jnp.float32, 'w_branch_na': _jnp.float32, 'w_branch_sw': _jnp.float32, 'w_out': _jnp.float32, 'ffn2_norm': _jnp.float32, 'ffn2_w_gate': _jnp.float32, 'ffn2_w_up': _jnp.float32, 'ffn2_w_down': _jnp.float32}
MOMENT_SCALE = {'ffn1_norm': 3.086419e+00, 'ffn1_w_gate': 4.613723e-02, 'ffn1_w_up': 4.969230e-02, 'ffn1_w_down': 8.042248e-02, 'mix_norm': 1.086648e-01, 'w_in': 2.680968e-02, 'b_gate': 1.709689e-02, 'na_q_norm': 7.412283e-01, 'na_k_norm': 7.409943e-01, 'na_rpb': 1.250870e-02, 'sw_q_norm': 6.266867e-01, 'sw_k_norm': 6.225145e-01, 'sw_sink': 1.787585e-02, 't5_rel_table': 4.688944e-02, 'w_branch_na': 2.847203e-02, 'w_branch_sw': 1.932267e-02, 'w_out': 3.356761e-02, 'ffn2_norm': 3.097813e+00, 'ffn2_w_gate': 4.333927e-02, 'ffn2_w_up': 4.782981e-02, 'ffn2_w_down': 7.711231e-02}


def _to_microbatches(a, axis):
    t = _jnp.moveaxis(a, axis, 0)
    t = t.reshape((N_MICROBATCH, t.shape[0] // N_MICROBATCH) + t.shape[1:])
    return _jnp.moveaxis(t, 1, axis + 1)


def setup_inputs(seed: int = 0) -> dict:
    inp = _fwd_setup_inputs(seed)
    key = _jax.random.fold_in(_jax.random.key(seed), 7919)
    shape, _ = _output_shape()
    out = dict(inp)
    out["loss_target"] = _jax.random.normal(_jax.random.fold_in(key, 0), shape, _jnp.float32)
    for i, name in enumerate(TWIN_WEIGHTS):
        w = inp[name].astype(_jnp.float32)
        if MOMENT_SCALE is None:
            s = _jnp.sqrt(_jnp.mean(_jnp.square(w)) + 1e-30)
        else:
            s = MOMENT_SCALE[name]
        km, kv = _jax.random.split(_jax.random.fold_in(key, i + 1))
        out[name] = w
        out["m_" + name] = s * _jax.random.normal(km, w.shape, _jnp.float32)
        out["v_" + name] = (s * s) * _jax.random.uniform(kv, w.shape, _jnp.float32, 0.5, 1.5)
    if N_MICROBATCH > 1:
        for name, axis in PER_EXAMPLE_BATCH_AXIS.items():
            out[name] = _to_microbatches(out[name], axis)
    return {'x': out['x'], 'ffn1_norm': out['ffn1_norm'], 'ffn1_w_gate': out['ffn1_w_gate'], 'ffn1_w_up': out['ffn1_w_up'], 'ffn1_w_down': out['ffn1_w_down'], 'mix_norm': out['mix_norm'], 'w_in': out['w_in'], 'b_gate': out['b_gate'], 'na_q_norm': out['na_q_norm'], 'na_k_norm': out['na_k_norm'], 'na_rpb': out['na_rpb'], 'sw_q_norm': out['sw_q_norm'], 'sw_k_norm': out['sw_k_norm'], 'sw_sink': out['sw_sink'], 't5_rel_table': out['t5_rel_table'], 'w_branch_na': out['w_branch_na'], 'w_branch_sw': out['w_branch_sw'], 'w_out': out['w_out'], 'ffn2_norm': out['ffn2_norm'], 'ffn2_w_gate': out['ffn2_w_gate'], 'ffn2_w_up': out['ffn2_w_up'], 'ffn2_w_down': out['ffn2_w_down'], 'loss_target': out['loss_target'], 'm_ffn1_norm': out['m_ffn1_norm'], 'm_ffn1_w_gate': out['m_ffn1_w_gate'], 'm_ffn1_w_up': out['m_ffn1_w_up'], 'm_ffn1_w_down': out['m_ffn1_w_down'], 'm_mix_norm': out['m_mix_norm'], 'm_w_in': out['m_w_in'], 'm_b_gate': out['m_b_gate'], 'm_na_q_norm': out['m_na_q_norm'], 'm_na_k_norm': out['m_na_k_norm'], 'm_na_rpb': out['m_na_rpb'], 'm_sw_q_norm': out['m_sw_q_norm'], 'm_sw_k_norm': out['m_sw_k_norm'], 'm_sw_sink': out['m_sw_sink'], 'm_t5_rel_table': out['m_t5_rel_table'], 'm_w_branch_na': out['m_w_branch_na'], 'm_w_branch_sw': out['m_w_branch_sw'], 'm_w_out': out['m_w_out'], 'm_ffn2_norm': out['m_ffn2_norm'], 'm_ffn2_w_gate': out['m_ffn2_w_gate'], 'm_ffn2_w_up': out['m_ffn2_w_up'], 'm_ffn2_w_down': out['m_ffn2_w_down'], 'v_ffn1_norm': out['v_ffn1_norm'], 'v_ffn1_w_gate': out['v_ffn1_w_gate'], 'v_ffn1_w_up': out['v_ffn1_w_up'], 'v_ffn1_w_down': out['v_ffn1_w_down'], 'v_mix_norm': out['v_mix_norm'], 'v_w_in': out['v_w_in'], 'v_b_gate': out['v_b_gate'], 'v_na_q_norm': out['v_na_q_norm'], 'v_na_k_norm': out['v_na_k_norm'], 'v_na_rpb': out['v_na_rpb'], 'v_sw_q_norm': out['v_sw_q_norm'], 'v_sw_k_norm': out['v_sw_k_norm'], 'v_sw_sink': out['v_sw_sink'], 'v_t5_rel_table': out['v_t5_rel_table'], 'v_w_branch_na': out['v_w_branch_na'], 'v_w_branch_sw': out['v_w_branch_sw'], 'v_w_out': out['v_w_out'], 'v_ffn2_norm': out['v_ffn2_norm'], 'v_ffn2_w_gate': out['v_ffn2_w_gate'], 'v_ffn2_w_up': out['v_ffn2_w_up'], 'v_ffn2_w_down': out['v_ffn2_w_down']}


def _loss(weights, diff, rest, loss_target):
    with _jax.named_scope("forward"):
        args = {**rest, TWIN_DIFF_INPUT: diff, **{k: w.astype(_WEIGHT_DTYPES[k]) for k, w in weights.items()}}
        y = _forward(args)
    with _jax.named_scope("loss_head"):
        err = _jnp.square(y.astype(_jnp.float32) - loss_target)
        return 0.5 * _jnp.sum(_jnp.mean(err, axis=-1)) if err.ndim else 0.5 * err


def _adamw(w, g, m, v):
    m = ADAM_B1 * m + (1.0 - ADAM_B1) * g
    v = ADAM_B2 * v + (1.0 - ADAM_B2) * _jnp.square(g)
    m_hat = m / (1.0 - ADAM_B1 ** ADAM_STEP)
    v_hat = v / (1.0 - ADAM_B2 ** ADAM_STEP)
    delta = -ADAM_LR * (m_hat / (_jnp.sqrt(v_hat) + ADAM_EPS) + ADAM_WD * w)
    return delta, m, v


def reference(x, ffn1_norm, ffn1_w_gate, ffn1_w_up, ffn1_w_down, mix_norm, w_in, b_gate, na_q_norm, na_k_norm, na_rpb, sw_q_norm, sw_k_norm, sw_sink, t5_rel_table, w_branch_na, w_branch_sw, w_out, ffn2_norm, ffn2_w_gate, ffn2_w_up, ffn2_w_down, loss_target, m_ffn1_norm, m_ffn1_w_gate, m_ffn1_w_up, m_ffn1_w_down, m_mix_norm, m_w_in, m_b_gate, m_na_q_norm, m_na_k_norm, m_na_rpb, m_sw_q_norm, m_sw_k_norm, m_sw_sink, m_t5_rel_table, m_w_branch_na, m_w_branch_sw, m_w_out, m_ffn2_norm, m_ffn2_w_gate, m_ffn2_w_up, m_ffn2_w_down, v_ffn1_norm, v_ffn1_w_gate, v_ffn1_w_up, v_ffn1_w_down, v_mix_norm, v_w_in, v_b_gate, v_na_q_norm, v_na_k_norm, v_na_rpb, v_sw_q_norm, v_sw_k_norm, v_sw_sink, v_t5_rel_table, v_w_branch_na, v_w_branch_sw, v_w_out, v_ffn2_norm, v_ffn2_w_gate, v_ffn2_w_up, v_ffn2_w_down):
    given = dict(x=x, ffn1_norm=ffn1_norm, ffn1_w_gate=ffn1_w_gate, ffn1_w_up=ffn1_w_up, ffn1_w_down=ffn1_w_down, mix_norm=mix_norm, w_in=w_in, b_gate=b_gate, na_q_norm=na_q_norm, na_k_norm=na_k_norm, na_rpb=na_rpb, sw_q_norm=sw_q_norm, sw_k_norm=sw_k_norm, sw_sink=sw_sink, t5_rel_table=t5_rel_table, w_branch_na=w_branch_na, w_branch_sw=w_branch_sw, w_out=w_out, ffn2_norm=ffn2_norm, ffn2_w_gate=ffn2_w_gate, ffn2_w_up=ffn2_w_up, ffn2_w_down=ffn2_w_down, loss_target=loss_target, m_ffn1_norm=m_ffn1_norm, m_ffn1_w_gate=m_ffn1_w_gate, m_ffn1_w_up=m_ffn1_w_up, m_ffn1_w_down=m_ffn1_w_down, m_mix_norm=m_mix_norm, m_w_in=m_w_in, m_b_gate=m_b_gate, m_na_q_norm=m_na_q_norm, m_na_k_norm=m_na_k_norm, m_na_rpb=m_na_rpb, m_sw_q_norm=m_sw_q_norm, m_sw_k_norm=m_sw_k_norm, m_sw_sink=m_sw_sink, m_t5_rel_table=m_t5_rel_table, m_w_branch_na=m_w_branch_na, m_w_branch_sw=m_w_branch_sw, m_w_out=m_w_out, m_ffn2_norm=m_ffn2_norm, m_ffn2_w_gate=m_ffn2_w_gate, m_ffn2_w_up=m_ffn2_w_up, m_ffn2_w_down=m_ffn2_w_down, v_ffn1_norm=v_ffn1_norm, v_ffn1_w_gate=v_ffn1_w_gate, v_ffn1_w_up=v_ffn1_w_up, v_ffn1_w_down=v_ffn1_w_down, v_mix_norm=v_mix_norm, v_w_in=v_w_in, v_b_gate=v_b_gate, v_na_q_norm=v_na_q_norm, v_na_k_norm=v_na_k_norm, v_na_rpb=v_na_rpb, v_sw_q_norm=v_sw_q_norm, v_sw_k_norm=v_sw_k_norm, v_sw_sink=v_sw_sink, v_t5_rel_table=v_t5_rel_table, v_w_branch_na=v_w_branch_na, v_w_branch_sw=v_w_branch_sw, v_w_out=v_w_out, v_ffn2_norm=v_ffn2_norm, v_ffn2_w_gate=v_ffn2_w_gate, v_ffn2_w_up=v_ffn2_w_up, v_ffn2_w_down=v_ffn2_w_down)
    weights = {n: given[n] for n in TWIN_WEIGHTS}
    shared = {n: given[n] for n in SHARED_INPUTS}
    per_example = {n: given[n] for n in ['x']}
    grad_fn = _jax.value_and_grad(_loss, argnums=(0, 1))

    def one_microbatch(ex, loss_target):
        ex = dict(ex)
        diff = ex.pop(TWIN_DIFF_INPUT)
        return grad_fn(weights, diff, {**shared, **ex}, loss_target)

    if N_MICROBATCH == 1:
        loss, (grad_w, grad_x) = one_microbatch(per_example, given["loss_target"])
    else:
        def body(carry, xs):
            loss_sum, grad_sum = carry
            l_k, (gw_k, gx_k) = one_microbatch(xs[0], xs[1])
            with _jax.named_scope("update"):
                return (loss_sum + l_k, _jax.tree.map(_jnp.add, grad_sum, gw_k)), gx_k

        init = (_jnp.zeros((), _jnp.float32), _jax.tree.map(_jnp.zeros_like, weights))
        (loss, grad_w), grad_x = _jax.lax.scan(body, init, (per_example, given["loss_target"]))
    with _jax.named_scope("update"):
        delta_w, new_m, new_v = {}, {}, {}
        for n in TWIN_WEIGHTS:
            delta_w[n], new_m[n], new_v[n] = _adamw(weights[n], grad_w[n], given["m_" + n], given["v_" + n])
    return (loss, grad_x, *[grad_w[n] for n in TWIN_WEIGHTS], *[delta_w[n] for n in TWIN_WEIGHTS],
            *[new_m[n] for n in TWIN_WEIGHTS], *[new_v[n] for n in TWIN_WEIGHTS])
```

```python
import functools
import math

import numpy as np
import jax
import jax.numpy as jnp
from jax import lax
from jax.experimental import pallas as pl
from jax.experimental.pallas import tpu as pltpu

F32 = jnp.float32
BF16 = jnp.bfloat16
MESH = pl.DeviceIdType.MESH

N_DEV = 8
EPS = 1e-6
NEG = -1e30
HEAD_DIM = 64
GRID_W = 64
NA_ROWS = 8
NA_COLS = 16
NA_WIDTH = 512
SW_Q_WIDTH = 512
SW_KV_WIDTH = 128
SW_BLOCK = 128
SW_HEADS = 8
SW_REP = 4
REL_BUCKETS = 32
REL_MAX_DIST = 128
QKV_WIDTH = 3 * NA_WIDTH + SW_Q_WIDTH + 2 * SW_KV_WIDTH
SCALE = 1.0 / math.sqrt(HEAD_DIM)

ADAM_LR = 0.001
ADAM_B1 = 0.9
ADAM_B2 = 0.999
ADAM_EPS = 1e-08
ADAM_WD = 0.01
ADAM_STEP = 10

V7X_VMEM_LIMIT = 56 * 1024 * 1024
LANES = 128
MXU_TILE = 256

NT = (((1,), (1,)), ((), ()))
TN = (((0,), (0,)), ((), ()))


def _params(n_grid=1):
    return pltpu.CompilerParams(dimension_semantics=("arbitrary",) * n_grid,
                                vmem_limit_bytes=V7X_VMEM_LIMIT)


def _row_tile(s):
    for t in (256, 128, 64, 32, 16, 8):
        if s % t == 0:
            return t
    raise ValueError(s)


def _col_chunk(n):
    return MXU_TILE if n % MXU_TILE == 0 else n


def _dot(a, b):
    return jnp.dot(a, b, preferred_element_type=F32)


def _dotg(a, b, dn):
    return lax.dot_general(a, b, dn, preferred_element_type=F32)


def _sigmoid(v):
    return 1.0 / (1.0 + jnp.exp(-v))


def _rstd(xv):
    return lax.rsqrt(jnp.mean(xv * xv, axis=-1, keepdims=True) + EPS)


def _full(shape):
    nd = len(shape)
    return pl.BlockSpec(shape, lambda i, _n=nd: (0,) * _n)


def _rows(tm, width):
    return pl.BlockSpec((tm, width), lambda i: (i, 0))


def _group_mean(v, bd):
    hi = v.astype(BF16)
    lo = (v - hi.astype(F32)).astype(BF16)
    return _dot(hi, bd) + _dot(lo, bd)


def ffn_up(x, gain, wg_t, wu_t, name):
    s, d = x.shape
    f = wg_t.shape[0]
    tm = _row_tile(s)
    fc = _col_chunk(f)

    def body(x_ref, g_ref, wg_ref, wu_ref, xn_ref, hg_ref, hu_ref, act_ref):
        xv = x_ref[...]
        xn = (xv * _rstd(xv) * g_ref[...]).astype(BF16)
        xn_ref[...] = xn
        for c0 in range(0, f, fc):
            hg = _dotg(xn, wg_ref[c0:c0 + fc, :], NT)
            hu = _dotg(xn, wu_ref[c0:c0 + fc, :], NT)
            hg_ref[:, c0:c0 + fc] = hg.astype(BF16)
            hu_ref[:, c0:c0 + fc] = hu.astype(BF16)
            act_ref[:, c0:c0 + fc] = (hg * _sigmoid(hg) * hu).astype(BF16)

    return pl.pallas_call(
        body, name=name, grid=(s // tm,),
        in_specs=[_rows(tm, d), _full((1, d)), _full((f, d)), _full((f, d))],
        out_specs=[_rows(tm, d), _rows(tm, f), _rows(tm, f), _rows(tm, f)],
        out_shape=[jax.ShapeDtypeStruct((s, d), BF16)] + [jax.ShapeDtypeStruct((s, f), BF16)] * 3,
        compiler_params=_params(),
    )(x, gain, wg_t, wu_t)


def ffn_down(x, act, wd, name):
    s, d = x.shape
    f = act.shape[1]
    tm = _row_tile(s)

    def body(x_ref, a_ref, w_ref, o_ref):
        o_ref[...] = x_ref[...] + 0.5 * _dot(a_ref[...], w_ref[...])

    return pl.pallas_call(
        body, name=name, grid=(s // tm,),
        in_specs=[_rows(tm, d), _rows(tm, f), _full((f, d))],
        out_specs=_rows(tm, d),
        out_shape=jax.ShapeDtypeStruct((s, d), F32),
        compiler_params=_params(),
    )(x, act, wd)


def mix_in(x, gain, win_t, b_gate, gq_na, gk_na, gq_sw, gk_sw, bd, name):
    s, d = x.shape
    n_in = win_t.shape[0]
    tm = _row_tile(s)
    gc = _col_chunk(2 * d)

    def body(x_ref, g_ref, w_ref, b_ref, gqa_ref, gka_ref, gqs_ref, gks_ref, bd_ref,
             hn_ref, zq_ref, qa_ref, ka_ref, qs_ref, ks_ref, gt_ref):
        xv = x_ref[...]
        hn = (xv * _rstd(xv) * g_ref[...]).astype(BF16)
        hn_ref[...] = hn

        def proj(c0, c1):
            return _dotg(hn, w_ref[c0:c1, :], NT)

        def headnorm(z, g, bdm):
            return z * lax.rsqrt(_group_mean(z * z, bdm) + EPS) * g

        bd512 = bd_ref[...]
        bd128 = bd_ref[0:SW_KV_WIDTH, 0:SW_KV_WIDTH]
        z = proj(0, 512)
        zq_ref[:, 0:512] = z.astype(BF16)
        qa_ref[...] = (headnorm(z, gqa_ref[...], bd512) * SCALE).astype(BF16)
        z = proj(512, 1024)
        zq_ref[:, 512:1024] = z.astype(BF16)
        ka_ref[...] = headnorm(z, gka_ref[...], bd512).astype(BF16)
        z = proj(1024, 1536)
        zq_ref[:, 1024:1536] = z.astype(BF16)
        z = proj(1536, 2048)
        zq_ref[:, 1536:2048] = z.astype(BF16)
        qs_ref[...] = (headnorm(z, gqs_ref[...], bd512) * SCALE).astype(BF16)
        z = proj(2048, 2176)
        zq_ref[:, 2048:2176] = z.astype(BF16)
        ks_ref[...] = headnorm(z, gks_ref[...], bd128).astype(BF16)
        z = proj(2176, 2304)
        zq_ref[:, 2176:2304] = z.astype(BF16)
        for c0 in range(0, 2 * d, gc):
            zg = proj(QKV_WIDTH + c0, QKV_WIDTH + c0 + gc) + b_ref[:, c0:c0 + gc]
            gt_ref[:, c0:c0 + gc] = _sigmoid(zg).astype(BF16)

    return pl.pallas_call(
        body, name=name, grid=(s // tm,),
        in_specs=[_rows(tm, d), _full((1, d)), _full((n_in, d)), _full((1, 2 * d)),
                  _full((1, 512)), _full((1, 512)), _full((1, 512)), _full((1, 128)), _full((512, 512))],
        out_specs=[_rows(tm, d), _rows(tm, QKV_WIDTH), _rows(tm, 512), _rows(tm, 512), _rows(tm, 512),
                   _rows(tm, 128), _rows(tm, 2 * d)],
        out_shape=[jax.ShapeDtypeStruct((s, d), BF16), jax.ShapeDtypeStruct((s, QKV_WIDTH), BF16),
                   jax.ShapeDtypeStruct((s, 512), BF16), jax.ShapeDtypeStruct((s, 512), BF16),
                   jax.ShapeDtypeStruct((s, 512), BF16), jax.ShapeDtypeStruct((s, 128), BF16),
                   jax.ShapeDtypeStruct((s, 2 * d), BF16)],
        compiler_params=_params(),
    )(x, gain, win_t, b_gate, gq_na, gk_na, gq_sw, gk_sw, bd)


def _na_iotas():
    qc = lax.broadcasted_iota(jnp.int32, (GRID_W, LANES), 0)
    ln = lax.broadcasted_iota(jnp.int32, (GRID_W, LANES), 1)
    low = ln < GRID_W
    kc = jnp.where(low, ln, ln - GRID_W)
    diff = kc - qc + (NA_COLS - 1)
    qcs = jnp.clip(qc - NA_COLS // 2, 0, GRID_W - NA_COLS)
    inwin = (kc >= qcs) & (kc < qcs + NA_COLS)
    return diff, low, inwin


def rpb_expand(rpb_flat, n_heads, name):
    n_ri = 2 * NA_ROWS - 1
    n_ci = 2 * NA_COLS - 1

    def body(rpb_ref, o_ref):
        diff, low, _ = _na_iotas()
        for h in range(n_heads):
            def one(ri, carry, h=h):
                t = jnp.zeros((GRID_W, LANES), F32)
                for c in range(n_ci):
                    lo = rpb_ref[h * n_ri * n_ci + ri * n_ci + c]
                    hi = rpb_ref[h * n_ri * n_ci + (ri + 1) * n_ci + c]
                    t = jnp.where(diff == c, jnp.where(low, lo, hi), t)
                o_ref[h, ri] = t
                return carry
            lax.fori_loop(0, n_ri - 1, one, 0)

    return pl.pallas_call(
        body, name=name,
        in_specs=[pl.BlockSpec(memory_space=pltpu.SMEM)],
        out_specs=pl.BlockSpec(memory_space=pltpu.VMEM),
        out_shape=jax.ShapeDtypeStruct((n_heads, n_ri - 1, GRID_W, LANES), F32),
        compiler_params=pltpu.CompilerParams(vmem_limit_bytes=V7X_VMEM_LIMIT),
    )(rpb_flat)


def rpb_reduce(dt2, name):
    n_heads = dt2.shape[0]
    n_ri = 2 * NA_ROWS - 1
    n_ci = 2 * NA_COLS - 1

    def body(d_ref, o_ref):
        diff, low, _ = _na_iotas()
        low32 = lax.broadcasted_iota(jnp.int32, (32, LANES), 1) < GRID_W
        o_ref[...] = jnp.zeros(o_ref.shape, F32)
        for h in range(n_heads):
            def one(ri, carry, h=h):
                dv = d_ref[h, ri]
                rows = [jnp.sum(jnp.where(diff == c, dv, 0.0), axis=0, keepdims=True) for c in range(n_ci)]
                rows.append(jnp.zeros((1, LANES), F32))
                r = jnp.concatenate(rows, axis=0)
                lo = jnp.sum(jnp.where(low32, r, 0.0), axis=1, keepdims=True)
                hi = jnp.sum(jnp.where(low32, 0.0, r), axis=1, keepdims=True)
                o_ref[h, ri] = o_ref[h, ri] + jnp.broadcast_to(lo, (32, LANES))
                o_ref[h, ri + 1] = o_ref[h, ri + 1] + jnp.broadcast_to(hi, (32, LANES))
                return carry
            lax.fori_loop(0, n_ri - 1, one, 0)

    return pl.pallas_call(
        body, name=name,
        in_specs=[pl.BlockSpec(memory_space=pltpu.VMEM)],
        out_specs=pl.BlockSpec(memory_space=pltpu.VMEM),
        out_shape=jax.ShapeDtypeStruct((n_heads, n_ri, 32, LANES), F32),
        compiler_params=pltpu.CompilerParams(vmem_limit_bytes=V7X_VMEM_LIMIT),
    )(dt2)


def _na_probs(q, k_ref_slices, t2_ref, hh, base, maskadd):
    sc = []
    for c in range(4):
        sc.append(_dotg(q, k_ref_slices[c], NT) + t2_ref[hh, base + 2 * c] + maskadd)
    m = jnp.maximum(jnp.maximum(jnp.max(sc[0], axis=1, keepdims=True), jnp.max(sc[1], axis=1, keepdims=True)),
                    jnp.maximum(jnp.max(sc[2], axis=1, keepdims=True), jnp.max(sc[3], axis=1, keepdims=True)))
    ec = [jnp.exp(v - m) for v in sc]
    den = (jnp.sum(ec[0], axis=1, keepdims=True) + jnp.sum(ec[1], axis=1, keepdims=True)
           + jnp.sum(ec[2], axis=1, keepdims=True) + jnp.sum(ec[3], axis=1, keepdims=True))
    inv = 1.0 / den
    return [v * inv for v in ec]


def na_fwd(qa, ka, zq, t2, name):
    s = qa.shape[0]
    rows = s // GRID_W
    n_pairs = NA_WIDTH // LANES
    v_blk0 = (2 * NA_WIDTH) // LANES

    def body(q_ref, k_ref, v_ref, t2_ref, o_ref):
        _, _, inwin = _na_iotas()
        maskadd = jnp.where(inwin, 0.0, NEG)

        def row(r, carry):
            rs = jnp.clip(r - NA_ROWS // 2, 0, rows - NA_ROWS)
            base = rs - r + (NA_ROWS - 1)
            q0 = pl.multiple_of(r * GRID_W, GRID_W)
            k0 = pl.multiple_of(rs * GRID_W, GRID_W)
            for hh in range(2):
                lanes = slice(HEAD_DIM * hh, HEAD_DIM * (hh + 1))
                q = q_ref[pl.ds(q0, GRID_W), lanes]
                ks = [k_ref[pl.ds(k0 + LANES * c, LANES), lanes] for c in range(4)]
                pc = _na_probs(q, ks, t2_ref, hh, base, maskadd)
                o = jnp.zeros((GRID_W, HEAD_DIM), F32)
                for c in range(4):
                    o = o + _dot(pc[c].astype(BF16), v_ref[pl.ds(k0 + LANES * c, LANES), lanes])
                o_ref[pl.ds(q0, GRID_W), lanes] = o.astype(BF16)
            return carry

        lax.fori_loop(0, rows, row, 0)

    col = lambda off: pl.BlockSpec((s, LANES), lambda p, _o=off: (0, _o + p))
    return pl.pallas_call(
        body, name=name, grid=(n_pairs,),
        in_specs=[col(0), col(0), col(v_blk0),
                  pl.BlockSpec((2, 2 * NA_ROWS - 2, GRID_W, LANES), lambda p: (p, 0, 0, 0))],
        out_specs=col(0),
        out_shape=jax.ShapeDtypeStruct((s, NA_WIDTH), BF16),
        compiler_params=_params(),
    )(qa, ka, zq, t2)


def na_bwd(qa, ka, zq, t2, o_na, do_na, name):
    s = qa.shape[0]
    rows = s // GRID_W
    n_pairs = NA_WIDTH // LANES
    v_blk0 = (2 * NA_WIDTH) // LANES

    def body(q_ref, k_ref, v_ref, t2_ref, o_ref, do_ref, dq_ref, dk_ref, dv_ref, dt2_ref):
        _, _, inwin = _na_iotas()
        maskadd = jnp.where(inwin, 0.0, NEG)
        dk_ref[...] = jnp.zeros(dk_ref.shape, F32)
        dv_ref[...] = jnp.zeros(dv_ref.shape, F32)
        dt2_ref[...] = jnp.zeros(dt2_ref.shape, F32)

        def row(r, carry):
            rs = jnp.clip(r - NA_ROWS // 2, 0, rows - NA_ROWS)
            base = rs - r + (NA_ROWS - 1)
            q0 = pl.multiple_of(r * GRID_W, GRID_W)
            k0 = pl.multiple_of(rs * GRID_W, GRID_W)
            for hh in range(2):
                lanes = slice(HEAD_DIM * hh, HEAD_DIM * (hh + 1))
                q = q_ref[pl.ds(q0, GRID_W), lanes]
                ks = [k_ref[pl.ds(k0 + LANES * c, LANES), lanes] for c in range(4)]
                pc = _na_probs(q, ks, t2_ref, hh, base, maskadd)
                do = do_ref[pl.ds(q0, GRID_W), lanes]
                ov = o_ref[pl.ds(q0, GRID_W), lanes]
                delta = jnp.sum(do.astype(F32) * ov.astype(F32), axis=1, keepdims=True)
                dq = jnp.zeros((GRID_W, HEAD_DIM), F32)
                for c in range(4):
                    kr = pl.ds(k0 + LANES * c, LANES)
                    dp = _dotg(do, v_ref[kr, lanes], NT)
                    ds = pc[c] * (dp - delta)
                    dt2_ref[hh, base + 2 * c] = dt2_ref[hh, base + 2 * c] + ds
                    dsb = ds.astype(BF16)
                    dq = dq + _dot(dsb, ks[c])
                    dk_ref[kr, lanes] = dk_ref[kr, lanes] + _dotg(dsb, q, TN)
                    dv_ref[kr, lanes] = dv_ref[kr, lanes] + _dotg(pc[c].astype(BF16), do, TN)
                dq_ref[pl.ds(q0, GRID_W), lanes] = dq
            return carry

        lax.fori_loop(0, rows, row, 0)

    col = lambda off: pl.BlockSpec((s, LANES), lambda p, _o=off: (0, _o + p))
    t2spec = pl.BlockSpec((2, 2 * NA_ROWS - 2, GRID_W, LANES), lambda p: (p, 0, 0, 0))
    return pl.pallas_call(
        body, name=name, grid=(n_pairs,),
        in_specs=[col(0), col(0), col(v_blk0), t2spec, col(0), col(0)],
        out_specs=[col(0), col(0), col(0), t2spec],
        out_shape=[jax.ShapeDtypeStruct((s, NA_WIDTH), F32)] * 3 + [jax.ShapeDtypeStruct(t2.shape, F32)],
        compiler_params=_params(),
    )(qa, ka, zq, t2, o_na, do_na)


def _t5_bucket_map():
    rel = np.arange(3 * SW_BLOCK)[None, :] - SW_BLOCK - np.arange(SW_BLOCK)[:, None]
    nb = REL_BUCKETS // 2
    max_exact = nb // 2
    n = np.abs(rel)
    large = max_exact + (np.log(np.maximum(n, 1) / max_exact)
                         / np.log(REL_MAX_DIST / max_exact) * (nb - max_exact)).astype(np.int32)
    large = np.minimum(large, nb - 1)
    return ((rel > 0) * nb + np.where(n < max_exact, n, large)).astype(np.int32)


def t5_expand(table, bmap, name):
    def body(tab_ref, bm_ref, o_ref):
        bm = bm_ref[...]
        for h in range(SW_HEADS):
            t = jnp.zeros(bm.shape, F32)
            for b in range(REL_BUCKETS):
                t = jnp.where(bm == b, tab_ref[b, h], t)
            o_ref[h] = t

    return pl.pallas_call(
        body, name=name,
        in_specs=[pl.BlockSpec(memory_space=pltpu.SMEM), pl.BlockSpec(memory_space=pltpu.VMEM)],
        out_specs=pl.BlockSpec(memory_space=pltpu.VMEM),
        out_shape=jax.ShapeDtypeStruct((SW_HEADS,) + bmap.shape, F32),
        compiler_params=pltpu.CompilerParams(vmem_limit_bytes=V7X_VMEM_LIMIT),
    )(table, bmap)


def t5_reduce(dbias_list, bmap, name):
    n = len(dbias_list)

    def body(*refs):
        d_refs, bm_ref, o_ref = refs[:n], refs[n], refs[n + 1]
        bm = bm_ref[...]
        for h in range(SW_HEADS):
            dv = d_refs[0][h]
            for other in d_refs[1:]:
                dv = dv + other[h]
            rows = [jnp.sum(jnp.where(bm == b, dv, 0.0), axis=0, keepdims=True) for b in range(REL_BUCKETS)]
            r = jnp.concatenate(rows, axis=0)
            o_ref[h] = jnp.broadcast_to(jnp.sum(r, axis=1, keepdims=True), (REL_BUCKETS, LANES))

    return pl.pallas_call(
        body, name=name,
        in_specs=[pl.BlockSpec(memory_space=pltpu.VMEM)] * (n + 1),
        out_specs=pl.BlockSpec(memory_space=pltpu.VMEM),
        out_shape=jax.ShapeDtypeStruct((SW_HEADS, REL_BUCKETS, LANES), F32),
        compiler_params=pltpu.CompilerParams(vmem_limit_bytes=V7X_VMEM_LIMIT),
    )(*dbias_list, bmap)


def _sw_mask_iotas():
    a = lax.broadcasted_iota(jnp.int32, (SW_BLOCK, 3 * SW_BLOCK), 0)
    j = lax.broadcasted_iota(jnp.int32, (SW_BLOCK, 3 * SW_BLOCK), 1)
    inwin = jnp.abs(j - SW_BLOCK - a) <= SW_BLOCK
    return j, inwin


def _sw_probs(q, k, bias, madd, sk):
    sc = _dotg(q, k, NT) + bias + madd
    m = jnp.maximum(jnp.max(sc, axis=1, keepdims=True), sk)
    e = jnp.exp(sc - m)
    es = jnp.exp(sk - m)
    inv = 1.0 / (jnp.sum(e, axis=1, keepdims=True) + es)
    return e * inv, es * inv


def sw_fwd(qs, ks, zq, t5b, sink, name):
    s = qs.shape[0]
    nb = s // SW_BLOCK
    v_blk = (3 * NA_WIDTH + SW_Q_WIDTH + SW_KV_WIDTH) // LANES
    pad = s + 2 * SW_BLOCK

    def body(q_ref, k_ref, v_ref, b_ref, sink_ref, o_ref, kp, vp):
        zeros = jnp.zeros((SW_BLOCK, SW_KV_WIDTH), BF16)
        kp[0:SW_BLOCK, :] = zeros
        vp[0:SW_BLOCK, :] = zeros
        kp[SW_BLOCK + s:pad, :] = zeros
        vp[SW_BLOCK + s:pad, :] = zeros
        kp[SW_BLOCK:SW_BLOCK + s, :] = k_ref[...]
        vp[SW_BLOCK:SW_BLOCK + s, :] = v_ref[...]
        j, inwin = _sw_mask_iotas()

        def blk(n, carry):
            kpos = n * SW_BLOCK - SW_BLOCK + j
            madd = jnp.where(inwin & (kpos >= 0) & (kpos < s), 0.0, NEG)
            q0 = pl.multiple_of(n * SW_BLOCK, SW_BLOCK)
            for h in range(SW_HEADS):
                g = h // SW_REP
                q = q_ref[pl.ds(q0, SW_BLOCK), HEAD_DIM * h:HEAD_DIM * (h + 1)]
                k = kp[pl.ds(q0, 3 * SW_BLOCK), HEAD_DIM * g:HEAD_DIM * (g + 1)]
                v = vp[pl.ds(q0, 3 * SW_BLOCK), HEAD_DIM * g:HEAD_DIM * (g + 1)]
                p, _ = _sw_probs(q, k, b_ref[h], madd, sink_ref[h])
                o_ref[pl.ds(q0, SW_BLOCK), HEAD_DIM * h:HEAD_DIM * (h + 1)] = _dot(p.astype(BF16), v).astype(BF16)
            return carry

        lax.fori_loop(0, nb, blk, 0)

    return pl.pallas_call(
        body, name=name, grid=(1,),
        in_specs=[_full((s, SW_Q_WIDTH)), _full((s, SW_KV_WIDTH)),
                  pl.BlockSpec((s, SW_KV_WIDTH), lambda i: (0, v_blk)),
                  _full((SW_HEADS, SW_BLOCK, 3 * SW_BLOCK)), pl.BlockSpec(memory_space=pltpu.SMEM)],
        out_specs=_full((s, SW_Q_WIDTH)),
        out_shape=jax.ShapeDtypeStruct((s, SW_Q_WIDTH), BF16),
        scratch_shapes=[pltpu.VMEM((pad, SW_KV_WIDTH), BF16), pltpu.VMEM((pad, SW_KV_WIDTH), BF16)],
        compiler_params=_params(),
    )(qs, ks, zq, t5b, sink)


def sw_bwd(qs, ks, zq, t5b, sink, o_sw, do_sw, name):
    s = qs.shape[0]
    nb = s // SW_BLOCK
    v_blk = (3 * NA_WIDTH + SW_Q_WIDTH + SW_KV_WIDTH) // LANES
    pad = s + 2 * SW_BLOCK

    def body(q_ref, k_ref, v_ref, b_ref, sink_ref, o_ref, do_ref,
             dq_ref, dk_ref, dv_ref, db_ref, dsk_ref, kp, vp, dkp, dvp):
        zeros = jnp.zeros((SW_BLOCK, SW_KV_WIDTH), BF16)
        kp[0:SW_BLOCK, :] = zeros
        vp[0:SW_BLOCK, :] = zeros
        kp[SW_BLOCK + s:pad, :] = zeros
        vp[SW_BLOCK + s:pad, :] = zeros
        kp[SW_BLOCK:SW_BLOCK + s, :] = k_ref[...]
        vp[SW_BLOCK:SW_BLOCK + s, :] = v_ref[...]
        dkp[...] = jnp.zeros(dkp.shape, F32)
        dvp[...] = jnp.zeros(dvp.shape, F32)
        db_ref[...] = jnp.zeros(db_ref.shape, F32)
        dsk_ref[...] = jnp.zeros(dsk_ref.shape, F32)
        j, inwin = _sw_mask_iotas()

        def blk(n, carry):
            kpos = n * SW_BLOCK - SW_BLOCK + j
            madd = jnp.where(inwin & (kpos >= 0) & (kpos < s), 0.0, NEG)
            q0 = pl.multiple_of(n * SW_BLOCK, SW_BLOCK)
            for g in range(SW_HEADS // SW_REP):
                kl = slice(HEAD_DIM * g, HEAD_DIM * (g + 1))
                k = kp[pl.ds(q0, 3 * SW_BLOCK), kl]
                v = vp[pl.ds(q0, 3 * SW_BLOCK), kl]
                dkw = jnp.zeros((3 * SW_BLOCK, HEAD_DIM), F32)
                dvw = jnp.zeros((3 * SW_BLOCK, HEAD_DIM), F32)
                for r in range(SW_REP):
                    h = g * SW_REP + r
                    hl = slice(HEAD_DIM * h, HEAD_DIM * (h + 1))
                    q = q_ref[pl.ds(q0, SW_BLOCK), hl]
                    p, ps = _sw_probs(q, k, b_ref[h], madd, sink_ref[h])
                    do = do_ref[pl.ds(q0, SW_BLOCK), hl]
                    ov = o_ref[pl.ds(q0, SW_BLOCK), hl]
                    delta = jnp.sum(do.astype(F32) * ov.astype(F32), axis=1, keepdims=True)
                    ds = p * (_dotg(do, v, NT) - delta)
                    db_ref[h] = db_ref[h] + ds
                    dsk_ref[h] = dsk_ref[h] - jnp.broadcast_to(ps * delta, (SW_BLOCK, LANES))
                    dsb = ds.astype(BF16)
                    dq_ref[pl.ds(q0, SW_BLOCK), hl] = _dot(dsb, k)
                    dkw = dkw + _dotg(dsb, q, TN)
                    dvw = dvw + _dotg(p.astype(BF16), do, TN)
                dkp[pl.ds(q0, 3 * SW_BLOCK), kl] = dkp[pl.ds(q0, 3 * SW_BLOCK), kl] + dkw
                dvp[pl.ds(q0, 3 * SW_BLOCK), kl] = dvp[pl.ds(q0, 3 * SW_BLOCK), kl] + dvw
            return carry

        lax.fori_loop(0, nb, blk, 0)
        dk_ref[...] = dkp[SW_BLOCK:SW_BLOCK + s, :]
        dv_ref[...] = dvp[SW_BLOCK:SW_BLOCK + s, :]

    bias_spec = _full((SW_HEADS, SW_BLOCK, 3 * SW_BLOCK))
    return pl.pallas_call(
        body, name=name, grid=(1,),
        in_specs=[_full((s, SW_Q_WIDTH)), _full((s, SW_KV_WIDTH)),
                  pl.BlockSpec((s, SW_KV_WIDTH), lambda i: (0, v_blk)),
                  bias_spec, pl.BlockSpec(memory_space=pltpu.SMEM),
                  _full((s, SW_Q_WIDTH)), _full((s, SW_Q_WIDTH))],
        out_specs=[_full((s, SW_Q_WIDTH)), _full((s, SW_KV_WIDTH)), _full((s, SW_KV_WIDTH)), bias_spec,
                   _full((SW_HEADS, SW_BLOCK, LANES))],
        out_shape=[jax.ShapeDtypeStruct((s, SW_Q_WIDTH), F32), jax.ShapeDtypeStruct((s, SW_KV_WIDTH), F32),
                   jax.ShapeDtypeStruct((s, SW_KV_WIDTH), F32),
                   jax.ShapeDtypeStruct((SW_HEADS, SW_BLOCK, 3 * SW_BLOCK), F32),
                   jax.ShapeDtypeStruct((SW_HEADS, SW_BLOCK, LANES), F32)],
        scratch_shapes=[pltpu.VMEM((pad, SW_KV_WIDTH), BF16), pltpu.VMEM((pad, SW_KV_WIDTH), BF16),
                        pltpu.VMEM((pad, SW_KV_WIDTH), F32), pltpu.VMEM((pad, SW_KV_WIDTH), F32)],
        compiler_params=_params(),
    )(qs, ks, zq, t5b, sink, o_sw, do_sw)


def merge_out(x, o_na, o_sw, gt, wbna_t, wbsw_t, wout, name):
    s, d = x.shape
    tm = _row_tile(s)

    def body(x_ref, ona_ref, osw_ref, gt_ref, wna_ref, wsw_ref, wo_ref, xo_ref, ana_ref, asw_ref, mg_ref):
        a_na = _dotg(ona_ref[...], wna_ref[...], NT)
        a_sw = _dotg(osw_ref[...], wsw_ref[...], NT)
        ana_ref[...] = a_na.astype(BF16)
        asw_ref[...] = a_sw.astype(BF16)
        merged = (gt_ref[:, 0:d].astype(F32) * a_na + gt_ref[:, d:2 * d].astype(F32) * a_sw).astype(BF16)
        mg_ref[...] = merged
        xo_ref[...] = x_ref[...] + _dot(merged, wo_ref[...])

    return pl.pallas_call(
        body, name=name, grid=(s // tm,),
        in_specs=[_rows(tm, d), _rows(tm, 512), _rows(tm, 512), _rows(tm, 2 * d),
                  _full((d, 512)), _full((d, 512)), _full((d, d))],
        out_specs=[_rows(tm, d)] * 4,
        out_shape=[jax.ShapeDtypeStruct((s, d), F32)] + [jax.ShapeDtypeStruct((s, d), BF16)] * 3,
        compiler_params=_params(),
    )(x, o_na, o_sw, gt, wbna_t, wbsw_t, wout)


def mix_bwd_out(dx, gt, a_na, a_sw, wbna_t, wbsw_t, wout, name):
    s, d = dx.shape
    tm = _row_tile(s)

    def body(dx_ref, gt_ref, ana_ref, asw_ref, wna_ref, wsw_ref, wo_ref,
             dxb_ref, dzg_ref, dana_ref, dasw_ref, dona_ref, dosw_ref, dbg_ref):
        @pl.when(pl.program_id(0) == 0)
        def _():
            dbg_ref[...] = jnp.zeros(dbg_ref.shape, F32)

        dxb = dx_ref[...].astype(BF16)
        dxb_ref[...] = dxb
        dm = _dotg(dxb, wo_ref[...], NT)
        for i, (a_ref, da_ref, w_ref, do_ref) in enumerate(
                [(ana_ref, dana_ref, wna_ref, dona_ref), (asw_ref, dasw_ref, wsw_ref, dosw_ref)]):
            gi = gt_ref[:, i * d:(i + 1) * d].astype(F32)
            da = (dm * gi).astype(BF16)
            da_ref[...] = da
            do_ref[...] = _dot(da, w_ref[...]).astype(BF16)
            dzg = dm * a_ref[...].astype(F32) * gi * (1.0 - gi)
            dzg_ref[:, i * d:(i + 1) * d] = dzg.astype(BF16)
            dbg_ref[:, i * d:(i + 1) * d] = dbg_ref[:, i * d:(i + 1) * d] + jnp.sum(dzg, axis=0, keepdims=True)

    return pl.pallas_call(
        body, name=name, grid=(s // tm,),
        in_specs=[_rows(tm, d), _rows(tm, 2 * d), _rows(tm, d), _rows(tm, d),
                  _full((d, 512)), _full((d, 512)), _full((d, d))],
        out_specs=[_rows(tm, d), _rows(tm, 2 * d), _rows(tm, d), _rows(tm, d), _rows(tm, 512), _rows(tm, 512),
                   _full((1, 2 * d))],
        out_shape=[jax.ShapeDtypeStruct((s, d), BF16), jax.ShapeDtypeStruct((s, 2 * d), BF16),
                   jax.ShapeDtypeStruct((s, d), BF16), jax.ShapeDtypeStruct((s, d), BF16),
                   jax.ShapeDtypeStruct((s, 512), BF16), jax.ShapeDtypeStruct((s, 512), BF16),
                   jax.ShapeDtypeStruct((1, 2 * d), F32)],
        compiler_params=_params(),
    )(dx, gt, a_na, a_sw, wbna_t, wbsw_t, wout)


def qk_norm_bwd(dqa, dka, dva, dqs, dks, dvs, zq, dzg, gq_na, gk_na, gq_sw, gk_sw, bd, name):
    s = zq.shape[0]
    d2 = dzg.shape[1]
    n_in = QKV_WIDTH + d2
    tm = _row_tile(s)

    def body(dqa_ref, dka_ref, dva_ref, dqs_ref, dks_ref, dvs_ref, zq_ref, dzg_ref,
             gqa_ref, gka_ref, gqs_ref, gks_ref, bd_ref, dz_ref, dgqa_ref, dgka_ref, dgqs_ref, dgks_ref):
        @pl.when(pl.program_id(0) == 0)
        def _():
            for r in (dgqa_ref, dgka_ref, dgqs_ref, dgks_ref):
                r[...] = jnp.zeros(r.shape, F32)

        bd512 = bd_ref[...]
        bd128 = bd_ref[0:SW_KV_WIDTH, 0:SW_KV_WIDTH]

        def one(c0, c1, dy_ref, g_ref, dg_ref, bdm, scale):
            z = zq_ref[:, c0:c1].astype(F32)
            r = lax.rsqrt(_group_mean(z * z, bdm) + EPS)
            zh = z * r
            dy = dy_ref[...] * scale
            dyg = dy * g_ref[...]
            dz = r * (dyg - zh * _group_mean(dyg * zh, bdm))
            dz_ref[:, c0:c1] = dz.astype(BF16)
            dg_ref[...] = dg_ref[...] + jnp.sum(dy * zh, axis=0, keepdims=True)

        one(0, 512, dqa_ref, gqa_ref, dgqa_ref, bd512, SCALE)
        one(512, 1024, dka_ref, gka_ref, dgka_ref, bd512, 1.0)
        dz_ref[:, 1024:1536] = dva_ref[...].astype(BF16)
        one(1536, 2048, dqs_ref, gqs_ref, dgqs_ref, bd512, SCALE)
        one(2048, 2176, dks_ref, gks_ref, dgks_ref, bd128, 1.0)
        dz_ref[:, 2176:2304] = dvs_ref[...].astype(BF16)
        dz_ref[:, QKV_WIDTH:n_in] = dzg_ref[...]

    return pl.pallas_call(
        body, name=name, grid=(s // tm,),
        in_specs=[_rows(tm, 512), _rows(tm, 512), _rows(tm, 512), _rows(tm, 512), _rows(tm, 128), _rows(tm, 128),
                  _rows(tm, QKV_WIDTH), _rows(tm, d2),
                  _full((1, 512)), _full((1, 512)), _full((1, 512)), _full((1, 128)), _full((512, 512))],
        out_specs=[_rows(tm, n_in), _full((1, 512)), _full((1, 512)), _full((1, 512)), _full((1, 128))],
        out_shape=[jax.ShapeDtypeStruct((s, n_in), BF16)] + [jax.ShapeDtypeStruct((1, 512), F32)] * 3
                  + [jax.ShapeDtypeStruct((1, 128), F32)],
        compiler_params=_params(),
    )(dqa, dka, dva, dqs, dks, dvs, zq, dzg, gq_na, gk_na, gq_sw, gk_sw, bd)


def ffn_bwd_act(dx, wd, hg, hu, name):
    s, d = dx.shape
    f = wd.shape[0]
    tm = _row_tile(s)
    fc = _col_chunk(f)

    def body(dx_ref, w_ref, hg_ref, hu_ref, dxb_ref, dhg_ref, dhu_ref):
        dxb = dx_ref[...].astype(BF16)
        dxb_ref[...] = dxb
        for c0 in range(0, f, fc):
            dact = 0.5 * _dotg(dxb, w_ref[c0:c0 + fc, :], NT)
            hg = hg_ref[:, c0:c0 + fc].astype(F32)
            hu = hu_ref[:, c0:c0 + fc].astype(F32)
            sg = _sigmoid(hg)
            dhu_ref[:, c0:c0 + fc] = (dact * hg * sg).astype(BF16)
            dhg_ref[:, c0:c0 + fc] = (dact * hu * sg * (1.0 + hg * (1.0 - sg))).astype(BF16)

    return pl.pallas_call(
        body, name=name, grid=(s // tm,),
        in_specs=[_rows(tm, d), _full((f, d)), _rows(tm, f), _rows(tm, f)],
        out_specs=[_rows(tm, d), _rows(tm, f), _rows(tm, f)],
        out_shape=[jax.ShapeDtypeStruct((s, d), BF16), jax.ShapeDtypeStruct((s, f), BF16),
                   jax.ShapeDtypeStruct((s, f), BF16)],
        compiler_params=_params(),
    )(dx, wd, hg, hu)


def proj_bwd_norm(acts, weights, x, gain, dx, name):
    s, d = x.shape
    tm = _row_tile(s)
    n = len(acts)

    def body(*refs):
        a_refs, w_refs = refs[:n], refs[n:2 * n]
        x_ref, g_ref, dx_ref, o_ref, dg_ref = refs[2 * n:]

        @pl.when(pl.program_id(0) == 0)
        def _():
            dg_ref[...] = jnp.zeros(dg_ref.shape, F32)

        dxn = _dot(a_refs[0][...], w_refs[0][...])
        for a_ref, w_ref in zip(a_refs[1:], w_refs[1:]):
            dxn = dxn + _dot(a_ref[...], w_ref[...])
        xv = x_ref[...]
        r = _rstd(xv)
        xh = xv * r
        dxh = dxn * g_ref[...]
        o_ref[...] = dx_ref[...] + r * (dxh - xh * jnp.mean(dxh * xh, axis=-1, keepdims=True))
        dg_ref[...] = dg_ref[...] + jnp.sum(dxn * xh, axis=0, keepdims=True)

    return pl.pallas_call(
        body, name=name, grid=(s // tm,),
        in_specs=[_rows(tm, a.shape[1]) for a in acts] + [_full(w.shape) for w in weights]
                 + [_rows(tm, d), _full((1, d)), _rows(tm, d)],
        out_specs=[_rows(tm, d), _full((1, d))],
        out_shape=[jax.ShapeDtypeStruct((s, d), F32), jax.ShapeDtypeStruct((1, d), F32)],
        compiler_params=_params(),
    )(*acts, *weights, x, gain, dx)


def tn_matmul(a, b, scale, name):
    s, n = a.shape
    k = b.shape[1]
    tn = _col_chunk(n)

    def body(a_ref, b_ref, o_ref):
        o_ref[...] = (scale * _dotg(a_ref[...], b_ref[...], TN)).astype(BF16)

    return pl.pallas_call(
        body, name=name, grid=(n // tn,),
        in_specs=[pl.BlockSpec((s, tn), lambda i: (0, i)), _full((s, k))],
        out_specs=pl.BlockSpec((tn, k), lambda i: (i, 0)),
        out_shape=jax.ShapeDtypeStruct((n, k), BF16),
        compiler_params=_params(),
    )(a, b)


def loss_grad(y, target, name):
    s, d = y.shape
    tm = _row_tile(s)

    def body(y_ref, t_ref, dy_ref, acc_ref):
        @pl.when(pl.program_id(0) == 0)
        def _():
            acc_ref[...] = jnp.zeros(acc_ref.shape, F32)

        err = y_ref[...] - t_ref[...]
        dy_ref[...] = err * (1.0 / d)
        e2 = err * err
        part = jnp.sum(e2.reshape(tm // 8, 8, d), axis=0)
        acc = part[:, 0:LANES]
        for c0 in range(LANES, d, LANES):
            acc = acc + part[:, c0:c0 + LANES]
        acc_ref[...] = acc_ref[...] + acc

    return pl.pallas_call(
        body, name=name, grid=(s // tm,),
        in_specs=[_rows(tm, d), _rows(tm, d)],
        out_specs=[_rows(tm, d), _full((8, LANES))],
        out_shape=[jax.ShapeDtypeStruct((s, d), F32), jax.ShapeDtypeStruct((8, LANES), F32)],
        compiler_params=_params(),
    )(y, target)


def _mesh_pos():
    return lax.axis_index("x"), lax.axis_index("y"), lax.axis_index("c")


def gather_weights(shards):
    n = len(shards)

    def body(*refs):
        ins, outs = refs[:n], refs[n:2 * n]
        send_sems, recv_sems, local_sems = refs[2 * n:]
        x, y, c = _mesh_pos()
        me, sibling = (x, y, c), (x, y, 1 - c)
        chips = [(1 - x, y), (x, 1 - y), (1 - x, 1 - y)]

        def slot(a, px, py, pc):
            return outs[a].at[:, 4 * px + 2 * py + pc]

        def copy(a, k, block, to, src=None):
            return pltpu.make_async_remote_copy(
                src_ref=slot(a, *block) if src is None else src, dst_ref=slot(a, *block),
                send_sem=send_sems.at[a, k], recv_sem=recv_sems.at[a, k], device_id=to, device_id_type=MESH)

        mine = [pltpu.make_async_copy(ins[a], slot(a, *me), local_sems.at[a]) for a in range(n)]
        for cp in mine:
            cp.start()
        first = []
        for a in range(n):
            first.append(copy(a, 0, me, sibling, src=ins[a]))
            first += [copy(a, 1 + j, me, (*chip, c), src=ins[a]) for j, chip in enumerate(chips)]
        for cp in first:
            cp.start()
        passed = []
        for j, chip in enumerate(chips):
            for a in range(n):
                copy(a, 1 + j, (*chip, c), me).wait_recv()
                fwd = copy(a, 4 + j, (*chip, c), sibling)
                fwd.start()
                passed.append(fwd)
        for a in range(n):
            copy(a, 0, sibling, me).wait_recv()
            for j, chip in enumerate(chips):
                copy(a, 4 + j, (*chip, 1 - c), me).wait_recv()
        for cp in first + passed:
            cp.wait_send()
        for cp in mine:
            cp.wait()

    any_spec = pl.BlockSpec(memory_space=pl.ANY)
    return pl.pallas_call(
        body, name="gather_weights",
        in_specs=[any_spec] * n, out_specs=[any_spec] * n,
        out_shape=[jax.ShapeDtypeStruct((w.shape[0], N_DEV) + w.shape[1:], w.dtype) for w in shards],
        scratch_shapes=[pltpu.SemaphoreType.DMA((n, 7)), pltpu.SemaphoreType.DMA((n, 7)),
                        pltpu.SemaphoreType.DMA((n,))],
        compiler_params=pltpu.CompilerParams(has_side_effects=True),
    )(*shards)


def scatter_grads(groups, small):
    n = len(groups)
    flat = [g for grp in groups for g in grp]
    offs = np.cumsum([0] + [len(grp) for grp in groups])

    def body(*refs):
        ins = refs[:len(flat)]
        small_ref = refs[len(flat)]
        outs = refs[len(flat) + 1:len(flat) + 1 + n]
        small_out = refs[len(flat) + 1 + n]
        send_sems, recv_sems, local_sems = refs[len(flat) + 2 + n:]
        x, y, c = _mesh_pos()
        me = 4 * x + 2 * y + c
        peers = []
        for rel in range(1, N_DEV):
            px = 1 - x if rel & 4 else x
            py = 1 - y if rel & 2 else y
            pc = 1 - c if rel & 1 else c
            peers.append((px, py, pc))

        started = []
        for a in range(n):
            for w in range(len(groups[a])):
                cp = pltpu.make_async_copy(ins[offs[a] + w].at[me], outs[a].at[me, w], local_sems.at[a])
                cp.start()
        cp_small = pltpu.make_async_copy(small_ref, small_out.at[me], local_sems.at[n])
        cp_small.start()
        for k, peer in enumerate(peers):
            p_id = 4 * peer[0] + 2 * peer[1] + peer[2]
            for a in range(n):
                for w in range(len(groups[a])):
                    pltpu.make_async_remote_copy(
                        src_ref=ins[offs[a] + w].at[p_id], dst_ref=outs[a].at[me, w],
                        send_sem=send_sems.at[a, k], recv_sem=recv_sems.at[a, k],
                        device_id=peer, device_id_type=MESH).start()
            pltpu.make_async_remote_copy(
                src_ref=small_ref, dst_ref=small_out.at[me],
                send_sem=send_sems.at[n, k], recv_sem=recv_sems.at[n, k],
                device_id=peer, device_id_type=MESH).start()
        for k, peer in enumerate(peers):
            for a in range(n):
                pltpu.make_async_remote_copy(
                    src_ref=outs[a].at[me], dst_ref=outs[a].at[me],
                    send_sem=send_sems.at[a, k], recv_sem=recv_sems.at[a, k],
                    device_id=peer, device_id_type=MESH).wait()
            pltpu.make_async_remote_copy(
                src_ref=small_out.at[me], dst_ref=small_out.at[me],
                send_sem=send_sems.at[n, k], recv_sem=recv_sems.at[n, k],
                device_id=peer, device_id_type=MESH).wait()
        for a in range(n):
            pltpu.make_async_copy(outs[a].at[me], outs[a].at[me], local_sems.at[a]).wait()
        cp_small.wait()
        del started

    any_spec = pl.BlockSpec(memory_space=pl.ANY)
    out_shape = [jax.ShapeDtypeStruct((N_DEV, len(grp)) + grp[0].shape[1:], grp[0].dtype) for grp in groups]
    out_shape.append(jax.ShapeDtypeStruct((N_DEV,) + small.shape, small.dtype))
    return pl.pallas_call(
        body, name="scatter_grads",
        in_specs=[any_spec] * (len(flat) + 1), out_specs=[any_spec] * (n + 1),
        out_shape=out_shape,
        scratch_shapes=[pltpu.SemaphoreType.DMA((n + 1, 7)), pltpu.SemaphoreType.DMA((n + 1, 7)),
                        pltpu.SemaphoreType.DMA((n + 1,))],
        compiler_params=pltpu.CompilerParams(has_side_effects=True),
    )(*flat, small)


def sum_sources(recv, name):
    _, w, r, c = recv.shape

    def body(r_ref, o_ref):
        acc = r_ref[0, 0].astype(F32)
        for src in range(1, N_DEV):
            acc = acc + r_ref[src, 0].astype(F32)
        o_ref[0] = acc

    return pl.pallas_call(
        body, name=name, grid=(w,),
        in_specs=[pl.BlockSpec((N_DEV, 1, r, c), lambda i: (0, i, 0, 0))],
        out_specs=pl.BlockSpec((1, r, c), lambda i: (i, 0, 0)),
        out_shape=jax.ShapeDtypeStruct((w, r, c), F32),
        compiler_params=_params(),
    )(recv)


def _adamw_math(w, g, m, v):
    m = ADAM_B1 * m + (1.0 - ADAM_B1) * g
    v = ADAM_B2 * v + (1.0 - ADAM_B2) * (g * g)
    m_hat = m / (1.0 - ADAM_B1 ** ADAM_STEP)
    v_hat = v / (1.0 - ADAM_B2 ** ADAM_STEP)
    delta = -ADAM_LR * (m_hat / (jnp.sqrt(v_hat) + ADAM_EPS) + ADAM_WD * w)
    return delta, m, v


def adamw(w, g, m, v, name):
    shape = w.shape
    c = shape[-1]
    r = int(np.prod(shape[:-1]))
    w2, g2, m2, v2 = (t.reshape(r, c) for t in (w, g, m, v))
    tr = next(t for t in range(min(r, 512), 0, -1) if r % t == 0 and (t % 8 == 0 or t == r))

    def body(w_ref, g_ref, m_ref, v_ref, d_ref, mo_ref, vo_ref):
        d_ref[...], mo_ref[...], vo_ref[...] = _adamw_math(w_ref[...], g_ref[...], m_ref[...], v_ref[...])

    spec = pl.BlockSpec((tr, c), lambda i: (i, 0))
    outs = pl.pallas_call(
        body, name=name, grid=(r // tr,),
        in_specs=[spec] * 4, out_specs=[spec] * 3,
        out_shape=[jax.ShapeDtypeStruct((r, c), F32)] * 3,
        compiler_params=_params(),
    )(w2, g2, m2, v2)
    return tuple(t.reshape(shape) for t in outs)


def adamw_small(w, recv, m, v, name):
    def body(w_ref, r_ref, m_ref, v_ref, g_ref, d_ref, mo_ref, vo_ref):
        g = r_ref[0]
        for src in range(1, N_DEV):
            g = g + r_ref[src]
        g_ref[...] = g
        d_ref[...], mo_ref[...], vo_ref[...] = _adamw_math(w_ref[...], g, m_ref[...], v_ref[...])

    vm = pl.BlockSpec(memory_space=pltpu.VMEM)
    return pl.pallas_call(
        body, name=name, in_specs=[vm] * 4, out_specs=[vm] * 4,
        out_shape=[jax.ShapeDtypeStruct(w.shape, F32)] * 4,
        compiler_params=pltpu.CompilerParams(vmem_limit_bytes=V7X_VMEM_LIMIT),
    )(w, recv, m, v)


SMALL_NAMES = ("ffn1_norm", "mix_norm", "ffn2_norm", "b_gate", "na_q_norm", "na_k_norm", "sw_q_norm", "sw_k_norm",
               "na_rpb", "sw_sink", "t5_rel_table")


def _pack_small(parts):
    flat = jnp.concatenate([parts[k].reshape(-1).astype(F32) for k in SMALL_NAMES])
    n = flat.shape[0]
    rows = -(-n // (8 * LANES)) * 8
    return jnp.pad(flat, (0, rows * LANES - n)).reshape(rows, LANES)


def _unpack_small(packed, like):
    flat = packed.reshape(-1)
    out, off = {}, 0
    for k in SMALL_NAMES:
        n = int(np.prod(like[k].shape))
        out[k] = flat[off:off + n].reshape(like[k].shape)
        off += n
    return out


def kernel(x, ffn1_norm, ffn1_w_gate, ffn1_w_up, ffn1_w_down, mix_norm, w_in, b_gate, na_q_norm, na_k_norm, na_rpb, sw_q_norm, sw_k_norm, sw_sink, t5_rel_table, w_branch_na, w_branch_sw, w_out, ffn2_norm, ffn2_w_gate, ffn2_w_up, ffn2_w_down, loss_target, m_ffn1_norm, m_ffn1_w_gate, m_ffn1_w_up, m_ffn1_w_down, m_mix_norm, m_w_in, m_b_gate, m_na_q_norm, m_na_k_norm, m_na_rpb, m_sw_q_norm, m_sw_k_norm, m_sw_sink, m_t5_rel_table, m_w_branch_na, m_w_branch_sw, m_w_out, m_ffn2_norm, m_ffn2_w_gate, m_ffn2_w_up, m_ffn2_w_down, v_ffn1_norm, v_ffn1_w_gate, v_ffn1_w_up, v_ffn1_w_down, v_mix_norm, v_w_in, v_b_gate, v_na_q_norm, v_na_k_norm, v_na_rpb, v_sw_q_norm, v_sw_k_norm, v_sw_sink, v_t5_rel_table, v_w_branch_na, v_w_branch_sw, v_w_out, v_ffn2_norm, v_ffn2_w_gate, v_ffn2_w_up, v_ffn2_w_down):
    weights = dict(ffn1_norm=ffn1_norm, ffn1_w_gate=ffn1_w_gate, ffn1_w_up=ffn1_w_up, ffn1_w_down=ffn1_w_down,
                   mix_norm=mix_norm, w_in=w_in, b_gate=b_gate, na_q_norm=na_q_norm, na_k_norm=na_k_norm,
                   na_rpb=na_rpb, sw_q_norm=sw_q_norm, sw_k_norm=sw_k_norm, sw_sink=sw_sink,
                   t5_rel_table=t5_rel_table, w_branch_na=w_branch_na, w_branch_sw=w_branch_sw, w_out=w_out,
                   ffn2_norm=ffn2_norm, ffn2_w_gate=ffn2_w_gate, ffn2_w_up=ffn2_w_up, ffn2_w_down=ffn2_w_down)
    mom_m = dict(ffn1_norm=m_ffn1_norm, ffn1_w_gate=m_ffn1_w_gate, ffn1_w_up=m_ffn1_w_up, ffn1_w_down=m_ffn1_w_down,
                 mix_norm=m_mix_norm, w_in=m_w_in, b_gate=m_b_gate, na_q_norm=m_na_q_norm, na_k_norm=m_na_k_norm,
                 na_rpb=m_na_rpb, sw_q_norm=m_sw_q_norm, sw_k_norm=m_sw_k_norm, sw_sink=m_sw_sink,
                 t5_rel_table=m_t5_rel_table, w_branch_na=m_w_branch_na, w_branch_sw=m_w_branch_sw, w_out=m_w_out,
                 ffn2_norm=m_ffn2_norm, ffn2_w_gate=m_ffn2_w_gate, ffn2_w_up=m_ffn2_w_up, ffn2_w_down=m_ffn2_w_down)
    mom_v = dict(ffn1_norm=v_ffn1_norm, ffn1_w_gate=v_ffn1_w_gate, ffn1_w_up=v_ffn1_w_up, ffn1_w_down=v_ffn1_w_down,
                 mix_norm=v_mix_norm, w_in=v_w_in, b_gate=v_b_gate, na_q_norm=v_na_q_norm, na_k_norm=v_na_k_norm,
                 na_rpb=v_na_rpb, sw_q_norm=v_sw_q_norm, sw_k_norm=v_sw_k_norm, sw_sink=v_sw_sink,
                 t5_rel_table=v_t5_rel_table, w_branch_na=v_w_branch_na, w_branch_sw=v_w_branch_sw, w_out=v_w_out,
                 ffn2_norm=v_ffn2_norm, ffn2_w_gate=v_ffn2_w_gate, ffn2_w_up=v_ffn2_w_up, ffn2_w_down=v_ffn2_w_down)
    order = list(weights)

    depth = ffn1_norm.shape[0]
    s, d = x.shape[1], x.shape[2]
    xs = x[0]
    tr = lambda w: jnp.swapaxes(w, -1, -2)

    a_loc = jnp.stack([t for l in range(depth) for t in (
        tr(ffn1_w_gate[l]), tr(ffn1_w_up[l]), ffn1_w_down[l],
        tr(ffn2_w_gate[l]), tr(ffn2_w_up[l]), ffn2_w_down[l])]).astype(BF16)
    b_loc = tr(w_in).astype(BF16)
    c_loc = w_out.astype(BF16)
    d_loc = jnp.stack([t for l in range(depth) for t in (tr(w_branch_na[l]), tr(w_branch_sw[l]))]).astype(BF16)
    a_all, b_all, c_all, d_all = gather_weights([a_loc, b_loc, c_loc, d_loc])
    merge = lambda t: t.reshape(t.shape[0], N_DEV * t.shape[2], t.shape[3])
    a_all, b_all, c_all, d_all = merge(a_all), merge(b_all), merge(c_all), merge(d_all)

    bd = jnp.asarray(np.kron(np.eye(NA_WIDTH // HEAD_DIM), np.full((HEAD_DIM, HEAD_DIM), 1.0 / HEAD_DIM)), BF16)
    bmap = jnp.asarray(_t5_bucket_map())
    tile8 = lambda g: jnp.tile(g, NA_WIDTH // HEAD_DIM).reshape(1, NA_WIDTH)
    tile2 = lambda g: jnp.tile(g, SW_KV_WIDTH // HEAD_DIM).reshape(1, SW_KV_WIDTH)
    t5b = t5_expand(t5_rel_table, bmap, "t5_expand")

    saved = []
    cur = xs
    for l in range(depth):
        sv = {}
        wg1, wu1, wd1, wg2, wu2, wd2 = (a_all[6 * l + i] for i in range(6))
        win_t, wout_l, wna_t, wsw_t = b_all[l], c_all[l], d_all[2 * l], d_all[2 * l + 1]
        sv["x0"] = cur
        sv["xn1"], sv["hg1"], sv["hu1"], sv["act1"] = ffn_up(cur, ffn1_norm[l][None], wg1, wu1, f"ffn1_up_{l}")
        cur = ffn_down(cur, sv["act1"], wd1, f"ffn1_down_{l}")
        sv["x1"] = cur
        sv["gains"] = (tile8(na_q_norm[l]), tile8(na_k_norm[l]), tile8(sw_q_norm[l]), tile2(sw_k_norm[l]))
        sv["hn"], sv["zq"], sv["qa"], sv["ka"], sv["qs"], sv["ks"], sv["gt"] = mix_in(
            cur, mix_norm[l][None], win_t, b_gate[l][None], *sv["gains"], bd, f"mix_in_{l}")
        sv["t2"] = rpb_expand(na_rpb[l].reshape(-1), na_rpb.shape[1], f"rpb_expand_{l}")
        sv["o_na"] = na_fwd(sv["qa"], sv["ka"], sv["zq"], sv["t2"], f"na_fwd_{l}")
        sv["o_sw"] = sw_fwd(sv["qs"], sv["ks"], sv["zq"], t5b, sw_sink[l], f"sw_fwd_{l}")
        cur, sv["a_na"], sv["a_sw"], sv["merged"] = merge_out(
            cur, sv["o_na"], sv["o_sw"], sv["gt"], wna_t, wsw_t, wout_l, f"merge_out_{l}")
        sv["x2"] = cur
        sv["xn2"], sv["hg2"], sv["hu2"], sv["act2"] = ffn_up(cur, ffn2_norm[l][None], wg2, wu2, f"ffn2_up_{l}")
        cur = ffn_down(cur, sv["act2"], wd2, f"ffn2_down_{l}")
        saved.append(sv)

    dx, loss_acc = loss_grad(cur, loss_target[0], "loss_grad")
    loss = lax.psum(jnp.sum(loss_acc) * (0.5 / d), ("x", "y", "c"))

    ga = [None] * (6 * depth)
    gb = [None] * depth
    gc = [None] * depth
    gd = [None] * (2 * depth)
    small = {k: [None] * depth for k in SMALL_NAMES if k != "t5_rel_table"}
    dbias_sw = []
    for l in reversed(range(depth)):
        sv = saved[l]
        wg1, wu1, wd1, wg2, wu2, wd2 = (a_all[6 * l + i] for i in range(6))
        win_t, wout_l, wna_t, wsw_t = b_all[l], c_all[l], d_all[2 * l], d_all[2 * l + 1]
        blocks = ((2, "x2", "xn2", "hg2", "hu2", "act2", wg2, wu2, wd2, "ffn2_norm", 3),
                  (1, "x0", "xn1", "hg1", "hu1", "act1", wg1, wu1, wd1, "ffn1_norm", 0))

        def ffn_backward(dx, blk):
            tag, xk, xnk, hgk, huk, actk, wg, wu, wd, norm_name, slot = blk
            gains = weights[norm_name]
            dxb, dhg, dhu = ffn_bwd_act(dx, wd, sv[hgk], sv[huk], f"ffn{tag}_bwd_act_{l}")
            ga[6 * l + slot + 2] = tn_matmul(sv[actk], dxb, 0.5, f"ffn{tag}_dwd_{l}")
            ga[6 * l + slot + 0] = tn_matmul(dhg, sv[xnk], 1.0, f"ffn{tag}_dwg_{l}")
            ga[6 * l + slot + 1] = tn_matmul(dhu, sv[xnk], 1.0, f"ffn{tag}_dwu_{l}")
            dx, dg = proj_bwd_norm([dhg, dhu], [wg, wu], sv[xk], gains[l][None], dx, f"ffn{tag}_bwd_x_{l}")
            small[norm_name][l] = dg[0]
            return dx

        dx = ffn_backward(dx, blocks[0])
        dxb, dzg, da_na, da_sw, do_na, do_sw, dbg = mix_bwd_out(
            dx, sv["gt"], sv["a_na"], sv["a_sw"], wna_t, wsw_t, wout_l, f"mix_bwd_out_{l}")
        small["b_gate"][l] = dbg[0]
        gc[l] = tn_matmul(sv["merged"], dxb, 1.0, f"dwout_{l}")
        gd[2 * l] = tn_matmul(da_na, sv["o_na"], 1.0, f"dwna_{l}")
        gd[2 * l + 1] = tn_matmul(da_sw, sv["o_sw"], 1.0, f"dwsw_{l}")
        dqa, dka, dva, dt2 = na_bwd(sv["qa"], sv["ka"], sv["zq"], sv["t2"], sv["o_na"], do_na, f"na_bwd_{l}")
        dqs, dks, dvs, dbias, dsink = sw_bwd(sv["qs"], sv["ks"], sv["zq"], t5b, sw_sink[l], sv["o_sw"], do_sw,
                                             f"sw_bwd_{l}")
        dbias_sw.append(dbias)
        small["sw_sink"][l] = jnp.sum(dsink[:, :, 0], axis=1)
        drpb = rpb_reduce(dt2, f"rpb_reduce_{l}")
        small["na_rpb"][l] = drpb[:, :, :2 * NA_COLS - 1, 0]
        dz, dgqa, dgka, dgqs, dgks = qk_norm_bwd(dqa, dka, dva, dqs, dks, dvs, sv["zq"], dzg, *sv["gains"], bd,
                                                 f"qk_norm_bwd_{l}")
        fold = lambda g: jnp.sum(g.reshape(-1, HEAD_DIM), axis=0)
        small["na_q_norm"][l], small["na_k_norm"][l] = fold(dgqa), fold(dgka)
        small["sw_q_norm"][l], small["sw_k_norm"][l] = fold(dgqs), fold(dgks)
        gb[l] = tn_matmul(dz, sv["hn"], 1.0, f"dwin_{l}")
        dx, dg = proj_bwd_norm([dz], [win_t], sv["x1"], mix_norm[l][None], dx, f"mix_bwd_x_{l}")
        small["mix_norm"][l] = dg[0]
        dx = ffn_backward(dx, blocks[1])

    dtab = t5_reduce(dbias_sw, bmap, "t5_reduce")
    small_parts = {k: jnp.stack(v) for k, v in small.items()}
    small_parts["t5_rel_table"] = jnp.transpose(dtab[:, :, 0])
    small_packed = _pack_small(small_parts)

    split = lambda t: t.reshape(N_DEV, t.shape[0] // N_DEV, t.shape[1])
    groups = [[split(t) for t in ga], [split(t) for t in gb], [split(t) for t in gc], [split(t) for t in gd]]
    ra, rb, rc, rd, rs_ = scatter_grads(groups, small_packed)
    sa = sum_sources(ra, "sum_a")
    sb = sum_sources(rb, "sum_b")
    sc = sum_sources(rc, "sum_c")
    sd = sum_sources(rd, "sum_d")

    grads = {}
    pick = lambda i: jnp.stack([sa[6 * l + i] for l in range(depth)])
    grads["ffn1_w_gate"], grads["ffn1_w_up"], grads["ffn1_w_down"] = tr(pick(0)), tr(pick(1)), pick(2)
    grads["ffn2_w_gate"], grads["ffn2_w_up"], grads["ffn2_w_down"] = tr(pick(3)), tr(pick(4)), pick(5)
    grads["w_in"] = tr(sb)
    grads["w_out"] = sc
    grads["w_branch_na"] = tr(jnp.stack([sd[2 * l] for l in range(depth)]))
    grads["w_branch_sw"] = tr(jnp.stack([sd[2 * l + 1] for l in range(depth)]))

    delta, new_m, new_v = {}, {}, {}
    for k in order:
        if k in SMALL_NAMES:
            continue
        delta[k], new_m[k], new_v[k] = adamw(weights[k], grads[k], mom_m[k], mom_v[k], f"adamw_{k}")
    g_s, d_s, m_s, v_s = adamw_small(_pack_small(weights), rs_, _pack_small(mom_m), _pack_small(mom_v), "adamw_small")
    for dst, packed in ((grads, g_s), (delta, d_s), (new_m, m_s), (new_v, v_s)):
        dst.update(_unpack_small(packed, weights))

    return (loss, dx[None], *[grads[k] for k in order], *[delta[k] for k in order],
            *[new_m[k] for k in order], *[new_v[k] for k in order])
```

```python
import functools
import math

import numpy as np
import jax
import jax.numpy as jnp
from jax import lax
from jax.experimental import pallas as pl
from jax.experimental.pallas import tpu as pltpu

F32 = jnp.float32
BF16 = jnp.bfloat16
MESH = pl.DeviceIdType.MESH

N_DEV = 8
EPS = 1e-6
NEG = -1e30
HEAD_DIM = 64
GRID_W = 64
NA_ROWS = 8
NA_COLS = 16
NA_WIDTH = 512
SW_Q_WIDTH = 512
SW_KV_WIDTH = 128
SW_BLOCK = 128
SW_HEADS = 8
SW_REP = 4
REL_BUCKETS = 32
REL_MAX_DIST = 128
QKV_WIDTH = 3 * NA_WIDTH + SW_Q_WIDTH + 2 * SW_KV_WIDTH
SCALE = 1.0 / math.sqrt(HEAD_DIM)

ADAM_LR = 0.001
ADAM_B1 = 0.9
ADAM_B2 = 0.999
ADAM_EPS = 1e-08
ADAM_WD = 0.01
ADAM_STEP = 10

V7X_VMEM_LIMIT = 56 * 1024 * 1024
LANES = 128
MXU_TILE = 256

NT = (((1,), (1,)), ((), ()))
TN = (((0,), (0,)), ((), ()))


def _params(n_grid=1):
    return pltpu.CompilerParams(dimension_semantics=("arbitrary",) * n_grid,
                                vmem_limit_bytes=V7X_VMEM_LIMIT)


def _row_tile(s):
    for t in (256, 128, 64, 32, 16, 8):
        if s % t == 0:
            return t
    raise ValueError(s)


def _col_chunk(n):
    return MXU_TILE if n % MXU_TILE == 0 else n


def _dot(a, b):
    return jnp.dot(a, b, preferred_element_type=F32)


def _dotg(a, b, dn):
    return lax.dot_general(a, b, dn, preferred_element_type=F32)


def _sigmoid(v):
    return 1.0 / (1.0 + jnp.exp(-v))


def _rstd(xv):
    return lax.rsqrt(jnp.mean(xv * xv, axis=-1, keepdims=True) + EPS)


def _full(shape):
    nd = len(shape)
    return pl.BlockSpec(shape, lambda i, _n=nd: (0,) * _n)


def _rows(tm, width):
    return pl.BlockSpec((tm, width), lambda i: (i, 0))


def _mat(stack, idx):
    return pl.BlockSpec((None,) + tuple(stack.shape[1:]), lambda i, _w=idx: (_w, 0, 0))


def _group_mean(v, bd):
    hi = v.astype(BF16)
    lo = (v - hi.astype(F32)).astype(BF16)
    return _dot(hi, bd) + _dot(lo, bd)


def ffn_up(x, gain, wg_t, wu_t, name):
    s, d = x.shape
    f = wg_t[0].shape[1]
    tm = _row_tile(s)
    fc = _col_chunk(f)

    def body(x_ref, g_ref, wg_ref, wu_ref, xn_ref, hg_ref, hu_ref, act_ref):
        xv = x_ref[...]
        xn = (xv * _rstd(xv) * g_ref[...]).astype(BF16)
        xn_ref[...] = xn
        for c0 in range(0, f, fc):
            hg = _dotg(xn, wg_ref[c0:c0 + fc, :], NT)
            hu = _dotg(xn, wu_ref[c0:c0 + fc, :], NT)
            hg_ref[:, c0:c0 + fc] = hg.astype(BF16)
            hu_ref[:, c0:c0 + fc] = hu.astype(BF16)
            act_ref[:, c0:c0 + fc] = (hg * _sigmoid(hg) * hu).astype(BF16)

    return pl.pallas_call(
        body, name=name, grid=(s // tm,),
        in_specs=[_rows(tm, d), _full((1, d)), _mat(*wg_t), _mat(*wu_t)],
        out_specs=[_rows(tm, d), _rows(tm, f), _rows(tm, f), _rows(tm, f)],
        out_shape=[jax.ShapeDtypeStruct((s, d), BF16)] + [jax.ShapeDtypeStruct((s, f), BF16)] * 3,
        compiler_params=_params(),
    )(x, gain, wg_t[0], wu_t[0])


def ffn_down(x, act, wd, name):
    s, d = x.shape
    f = act.shape[1]
    tm = _row_tile(s)

    def body(x_ref, a_ref, w_ref, o_ref):
        o_ref[...] = x_ref[...] + 0.5 * _dot(a_ref[...], w_ref[...])

    return pl.pallas_call(
        body, name=name, grid=(s // tm,),
        in_specs=[_rows(tm, d), _rows(tm, f), _mat(*wd)],
        out_specs=_rows(tm, d),
        out_shape=jax.ShapeDtypeStruct((s, d), F32),
        compiler_params=_params(),
    )(x, act, wd[0])


def mix_in(x, gain, win_t, b_gate, gq_na, gk_na, gq_sw, gk_sw, bd, name):
    s, d = x.shape
    tm = _row_tile(s)
    gc = _col_chunk(2 * d)

    def body(x_ref, g_ref, w_ref, b_ref, gqa_ref, gka_ref, gqs_ref, gks_ref, bd_ref,
             hn_ref, zq_ref, qa_ref, ka_ref, qs_ref, ks_ref, gt_ref):
        xv = x_ref[...]
        hn = (xv * _rstd(xv) * g_ref[...]).astype(BF16)
        hn_ref[...] = hn

        def proj(c0, c1):
            return _dotg(hn, w_ref[c0:c1, :], NT)

        def headnorm(z, g, bdm):
            return z * lax.rsqrt(_group_mean(z * z, bdm) + EPS) * g

        bd512 = bd_ref[...]
        bd128 = bd_ref[0:SW_KV_WIDTH, 0:SW_KV_WIDTH]
        z = proj(0, 512)
        zq_ref[:, 0:512] = z.astype(BF16)
        qa_ref[...] = (headnorm(z, gqa_ref[...], bd512) * SCALE).astype(BF16)
        z = proj(512, 1024)
        zq_ref[:, 512:1024] = z.astype(BF16)
        ka_ref[...] = headnorm(z, gka_ref[...], bd512).astype(BF16)
        z = proj(1024, 1536)
        zq_ref[:, 1024:1536] = z.astype(BF16)
        z = proj(1536, 2048)
        zq_ref[:, 1536:2048] = z.astype(BF16)
        qs_ref[...] = (headnorm(z, gqs_ref[...], bd512) * SCALE).astype(BF16)
        z = proj(2048, 2176)
        zq_ref[:, 2048:2176] = z.astype(BF16)
        ks_ref[...] = headnorm(z, gks_ref[...], bd128).astype(BF16)
        z = proj(2176, 2304)
        zq_ref[:, 2176:2304] = z.astype(BF16)
        for c0 in range(0, 2 * d, gc):
            zg = proj(QKV_WIDTH + c0, QKV_WIDTH + c0 + gc) + b_ref[:, c0:c0 + gc]
            gt_ref[:, c0:c0 + gc] = _sigmoid(zg).astype(BF16)

    return pl.pallas_call(
        body, name=name, grid=(s // tm,),
        in_specs=[_rows(tm, d), _full((1, d)), _mat(*win_t), _full((1, 2 * d)),
                  _full((1, 512)), _full((1, 512)), _full((1, 512)), _full((1, 128)), _full((512, 512))],
        out_specs=[_rows(tm, d), _rows(tm, QKV_WIDTH), _rows(tm, 512), _rows(tm, 512), _rows(tm, 512),
                   _rows(tm, 128), _rows(tm, 2 * d)],
        out_shape=[jax.ShapeDtypeStruct((s, d), BF16), jax.ShapeDtypeStruct((s, QKV_WIDTH), BF16),
                   jax.ShapeDtypeStruct((s, 512), BF16), jax.ShapeDtypeStruct((s, 512), BF16),
                   jax.ShapeDtypeStruct((s, 512), BF16), jax.ShapeDtypeStruct((s, 128), BF16),
                   jax.ShapeDtypeStruct((s, 2 * d), BF16)],
        compiler_params=_params(),
    )(x, gain, win_t[0], b_gate, gq_na, gk_na, gq_sw, gk_sw, bd)


def _na_iotas():
    qc = lax.broadcasted_iota(jnp.int32, (GRID_W, LANES), 0)
    ln = lax.broadcasted_iota(jnp.int32, (GRID_W, LANES), 1)
    low = ln < GRID_W
    kc = jnp.where(low, ln, ln - GRID_W)
    diff = kc - qc + (NA_COLS - 1)
    qcs = jnp.clip(qc - NA_COLS // 2, 0, GRID_W - NA_COLS)
    inwin = (kc >= qcs) & (kc < qcs + NA_COLS)
    return diff, low, inwin


NA_RI = 2 * NA_ROWS - 1
NA_CI = 2 * NA_COLS - 1
NA_T2 = NA_RI + 1


def rpb_expand(rpb_flat, n_heads, name):
    def body(rpb_ref, o_ref):
        diff, low, _ = _na_iotas()
        for h in range(n_heads):
            def one(e, carry, h=h):
                lo_row = jnp.maximum(e - 1, 0)
                hi_row = jnp.minimum(e, NA_RI - 1)
                lo_on = jnp.where(e >= 1, 1.0, 0.0)
                hi_on = jnp.where(e <= NA_RI - 1, 1.0, 0.0)
                t = jnp.zeros((GRID_W, LANES), F32)
                for c in range(NA_CI):
                    lo = rpb_ref[h * NA_RI * NA_CI + lo_row * NA_CI + c] * lo_on
                    hi = rpb_ref[h * NA_RI * NA_CI + hi_row * NA_CI + c] * hi_on
                    t = jnp.where(diff == c, jnp.where(low, lo, hi), t)
                o_ref[h, e] = t
                return carry
            lax.fori_loop(0, NA_T2, one, 0)

    return pl.pallas_call(
        body, name=name,
        in_specs=[pl.BlockSpec(memory_space=pltpu.SMEM)],
        out_specs=pl.BlockSpec(memory_space=pltpu.VMEM),
        out_shape=jax.ShapeDtypeStruct((n_heads, NA_T2, GRID_W, LANES), F32),
        compiler_params=pltpu.CompilerParams(vmem_limit_bytes=V7X_VMEM_LIMIT),
    )(rpb_flat)


def rpb_reduce(dt2, name):
    n_heads = dt2.shape[0]

    def body(d_ref, o_ref):
        diff, low, _ = _na_iotas()
        low32 = lax.broadcasted_iota(jnp.int32, (32, LANES), 1) < GRID_W
        o_ref[...] = jnp.zeros(o_ref.shape, F32)
        for h in range(n_heads):
            def one(e, carry, h=h):
                dv = d_ref[h, e]
                rows = [jnp.sum(jnp.where(diff == c, dv, 0.0), axis=0, keepdims=True) for c in range(NA_CI)]
                rows.append(jnp.zeros((1, LANES), F32))
                r = jnp.concatenate(rows, axis=0)
                lo = jnp.sum(jnp.where(low32, r, 0.0), axis=1, keepdims=True)
                hi = jnp.sum(jnp.where(low32, 0.0, r), axis=1, keepdims=True)
                lo_row = jnp.maximum(e - 1, 0)
                hi_row = jnp.minimum(e, NA_RI - 1)
                o_ref[h, lo_row] = o_ref[h, lo_row] + jnp.broadcast_to(lo, (32, LANES))
                o_ref[h, hi_row] = o_ref[h, hi_row] + jnp.broadcast_to(hi, (32, LANES))
                return carry
            lax.fori_loop(0, NA_T2, one, 0)

    return pl.pallas_call(
        body, name=name,
        in_specs=[pl.BlockSpec(memory_space=pltpu.VMEM)],
        out_specs=pl.BlockSpec(memory_space=pltpu.VMEM),
        out_shape=jax.ShapeDtypeStruct((n_heads, NA_RI, 32, LANES), F32),
        compiler_params=pltpu.CompilerParams(vmem_limit_bytes=V7X_VMEM_LIMIT),
    )(dt2)


NA_TQ = 4
NA_TK = NA_TQ + NA_ROWS
NA_KCH = NA_TK // 2


def _na_tile_geometry(t, rows):
    r = t * NA_TQ
    kbase = jnp.clip(r - NA_ROWS // 2, 0, rows - NA_TK)
    starts = [jnp.clip(r + a - NA_ROWS // 2, 0, rows - NA_ROWS) for a in range(NA_TQ)]
    return r, kbase, starts


def _na_tile_mask(kbase, starts, low, inwin):
    half = jnp.where(low, 0, 1)
    cols = []
    for c in range(NA_KCH):
        krow = kbase + 2 * c + half
        cols.append(jnp.concatenate(
            [jnp.where(inwin & (krow >= st) & (krow < st + NA_ROWS), 0.0, NEG) for st in starts], axis=0))
    return jnp.concatenate(cols, axis=1)


def _na_tile_index(r, kbase, a, c):
    return jnp.clip(kbase + 2 * c - (r + a) + NA_ROWS, 0, NA_T2 - 1)


def _na_tile_probs(q, k, t2_ref, hh, r, kbase, madd):
    bias = jnp.concatenate(
        [jnp.concatenate([t2_ref[hh, _na_tile_index(r, kbase, a, c)] for a in range(NA_TQ)], axis=0)
         for c in range(NA_KCH)], axis=1)
    sc = _dotg(q, k, NT) + bias + madd
    e = jnp.exp(sc - jnp.max(sc, axis=1, keepdims=True))
    return e * (1.0 / jnp.sum(e, axis=1, keepdims=True))


def na_fwd(qa, ka, zq, t2, name):
    s = qa.shape[0]
    rows = s // GRID_W
    n_pairs = NA_WIDTH // LANES
    v_blk0 = (2 * NA_WIDTH) // LANES

    assert rows % NA_TQ == 0 and rows >= NA_TK
    tq, tk = NA_TQ * GRID_W, NA_TK * GRID_W

    def body(q_ref, k_ref, v_ref, t2_ref, o_ref):
        _, low, inwin = _na_iotas()

        def tile(t, carry):
            r, kbase, starts = _na_tile_geometry(t, rows)
            madd = _na_tile_mask(kbase, starts, low, inwin)
            qr = pl.ds(pl.multiple_of(r * GRID_W, tq), tq)
            kr = pl.ds(pl.multiple_of(kbase * GRID_W, tq), tk)
            for hh in range(2):
                lanes = slice(HEAD_DIM * hh, HEAD_DIM * (hh + 1))
                p = _na_tile_probs(q_ref[qr, lanes], k_ref[kr, lanes], t2_ref, hh, r, kbase, madd)
                o_ref[qr, lanes] = _dot(p.astype(BF16), v_ref[kr, lanes]).astype(BF16)
            return carry

        lax.fori_loop(0, rows // NA_TQ, tile, 0)

    col = lambda off: pl.BlockSpec((s, LANES), lambda p, _o=off: (0, _o + p))
    return pl.pallas_call(
        body, name=name, grid=(n_pairs,),
        in_specs=[col(0), col(0), col(v_blk0),
                  pl.BlockSpec((2, NA_T2, GRID_W, LANES), lambda p: (p, 0, 0, 0))],
        out_specs=col(0),
        out_shape=jax.ShapeDtypeStruct((s, NA_WIDTH), BF16),
        compiler_params=_params(),
    )(qa, ka, zq, t2)


def na_bwd(qa, ka, zq, t2, o_na, do_na, name):
    s = qa.shape[0]
    rows = s // GRID_W
    n_pairs = NA_WIDTH // LANES
    v_blk0 = (2 * NA_WIDTH) // LANES

    tq, tk = NA_TQ * GRID_W, NA_TK * GRID_W

    def body(q_ref, k_ref, v_ref, t2_ref, o_ref, do_ref, dq_ref, dk_ref, dv_ref, dt2_ref):
        _, low, inwin = _na_iotas()
        dk_ref[...] = jnp.zeros(dk_ref.shape, F32)
        dv_ref[...] = jnp.zeros(dv_ref.shape, F32)
        dt2_ref[...] = jnp.zeros(dt2_ref.shape, F32)

        def tile(t, carry):
            r, kbase, starts = _na_tile_geometry(t, rows)
            madd = _na_tile_mask(kbase, starts, low, inwin)
            qr = pl.ds(pl.multiple_of(r * GRID_W, tq), tq)
            kr = pl.ds(pl.multiple_of(kbase * GRID_W, tq), tk)
            for hh in range(2):
                lanes = slice(HEAD_DIM * hh, HEAD_DIM * (hh + 1))
                q, k, v = q_ref[qr, lanes], k_ref[kr, lanes], v_ref[kr, lanes]
                p = _na_tile_probs(q, k, t2_ref, hh, r, kbase, madd)
                do = do_ref[qr, lanes]
                delta = jnp.sum(do.astype(F32) * o_ref[qr, lanes].astype(F32), axis=1, keepdims=True)
                ds = p * (_dotg(do, v, NT) - delta)
                for a in range(NA_TQ):
                    for c in range(NA_KCH):
                        e = _na_tile_index(r, kbase, a, c)
                        dt2_ref[hh, e] = dt2_ref[hh, e] + ds[GRID_W * a:GRID_W * (a + 1), LANES * c:LANES * (c + 1)]
                dsb = ds.astype(BF16)
                dq_ref[qr, lanes] = _dot(dsb, k)
                dk_ref[kr, lanes] = dk_ref[kr, lanes] + _dotg(dsb, q, TN)
                dv_ref[kr, lanes] = dv_ref[kr, lanes] + _dotg(p.astype(BF16), do, TN)
            return carry

        lax.fori_loop(0, rows // NA_TQ, tile, 0)

    col = lambda off: pl.BlockSpec((s, LANES), lambda p, _o=off: (0, _o + p))
    t2spec = pl.BlockSpec((2, NA_T2, GRID_W, LANES), lambda p: (p, 0, 0, 0))
    return pl.pallas_call(
        body, name=name, grid=(n_pairs,),
        in_specs=[col(0), col(0), col(v_blk0), t2spec, col(0), col(0)],
        out_specs=[col(0), col(0), col(0), t2spec],
        out_shape=[jax.ShapeDtypeStruct((s, NA_WIDTH), F32)] * 3 + [jax.ShapeDtypeStruct(t2.shape, F32)],
        compiler_params=_params(),
    )(qa, ka, zq, t2, o_na, do_na)


def _t5_bucket_map():
    rel = np.arange(3 * SW_BLOCK)[None, :] - SW_BLOCK - np.arange(SW_BLOCK)[:, None]
    nb = REL_BUCKETS // 2
    max_exact = nb // 2
    n = np.abs(rel)
    large = max_exact + (np.log(np.maximum(n, 1) / max_exact)
                         / np.log(REL_MAX_DIST / max_exact) * (nb - max_exact)).astype(np.int32)
    large = np.minimum(large, nb - 1)
    return ((rel > 0) * nb + np.where(n < max_exact, n, large)).astype(np.int32)


def t5_expand(table, bmap, name):
    def body(tab_ref, bm_ref, o_ref):
        bm = bm_ref[...]
        for h in range(SW_HEADS):
            t = jnp.zeros(bm.shape, F32)
            for b in range(REL_BUCKETS):
                t = jnp.where(bm == b, tab_ref[b, h], t)
            o_ref[h] = t

    return pl.pallas_call(
        body, name=name,
        in_specs=[pl.BlockSpec(memory_space=pltpu.SMEM), pl.BlockSpec(memory_space=pltpu.VMEM)],
        out_specs=pl.BlockSpec(memory_space=pltpu.VMEM),
        out_shape=jax.ShapeDtypeStruct((SW_HEADS,) + bmap.shape, F32),
        compiler_params=pltpu.CompilerParams(vmem_limit_bytes=V7X_VMEM_LIMIT),
    )(table, bmap)


def t5_reduce(dbias_list, bmap, name):
    n = len(dbias_list)

    def body(*refs):
        d_refs, bm_ref, o_ref = refs[:n], refs[n], refs[n + 1]
        bm = bm_ref[...]
        for h in range(SW_HEADS):
            dv = d_refs[0][h]
            for other in d_refs[1:]:
                dv = dv + other[h]
            rows = [jnp.sum(jnp.where(bm == b, dv, 0.0), axis=0, keepdims=True) for b in range(REL_BUCKETS)]
            r = jnp.concatenate(rows, axis=0)
            o_ref[h] = jnp.broadcast_to(jnp.sum(r, axis=1, keepdims=True), (REL_BUCKETS, LANES))

    return pl.pallas_call(
        body, name=name,
        in_specs=[pl.BlockSpec(memory_space=pltpu.VMEM)] * (n + 1),
        out_specs=pl.BlockSpec(memory_space=pltpu.VMEM),
        out_shape=jax.ShapeDtypeStruct((SW_HEADS, REL_BUCKETS, LANES), F32),
        compiler_params=pltpu.CompilerParams(vmem_limit_bytes=V7X_VMEM_LIMIT),
    )(*dbias_list, bmap)


def _sw_mask_iotas():
    a = lax.broadcasted_iota(jnp.int32, (SW_BLOCK, 3 * SW_BLOCK), 0)
    j = lax.broadcasted_iota(jnp.int32, (SW_BLOCK, 3 * SW_BLOCK), 1)
    inwin = jnp.abs(j - SW_BLOCK - a) <= SW_BLOCK
    return j, inwin


def _sw_probs(q, k, bias, madd, sk):
    sc = _dotg(q, k, NT) + bias + madd
    m = jnp.maximum(jnp.max(sc, axis=1, keepdims=True), sk)
    e = jnp.exp(sc - m)
    es = jnp.exp(sk - m)
    inv = 1.0 / (jnp.sum(e, axis=1, keepdims=True) + es)
    return e * inv, es * inv


def sw_fwd(qs, ks, zq, t5b, sink, name):
    s = qs.shape[0]
    nb = s // SW_BLOCK
    v_blk = (3 * NA_WIDTH + SW_Q_WIDTH + SW_KV_WIDTH) // LANES
    pad = s + 2 * SW_BLOCK

    def body(q_ref, k_ref, v_ref, b_ref, sink_ref, o_ref, kp, vp):
        zeros = jnp.zeros((SW_BLOCK, SW_KV_WIDTH), BF16)
        kp[0:SW_BLOCK, :] = zeros
        vp[0:SW_BLOCK, :] = zeros
        kp[SW_BLOCK + s:pad, :] = zeros
        vp[SW_BLOCK + s:pad, :] = zeros
        kp[SW_BLOCK:SW_BLOCK + s, :] = k_ref[...]
        vp[SW_BLOCK:SW_BLOCK + s, :] = v_ref[...]
        j, inwin = _sw_mask_iotas()

        def blk(n, carry):
            kpos = n * SW_BLOCK - SW_BLOCK + j
            madd = jnp.where(inwin & (kpos >= 0) & (kpos < s), 0.0, NEG)
            q0 = pl.multiple_of(n * SW_BLOCK, SW_BLOCK)
            for h in range(SW_HEADS):
                g = h // SW_REP
                q = q_ref[pl.ds(q0, SW_BLOCK), HEAD_DIM * h:HEAD_DIM * (h + 1)]
                k = kp[pl.ds(q0, 3 * SW_BLOCK), HEAD_DIM * g:HEAD_DIM * (g + 1)]
                v = vp[pl.ds(q0, 3 * SW_BLOCK), HEAD_DIM * g:HEAD_DIM * (g + 1)]
                p, _ = _sw_probs(q, k, b_ref[h], madd, sink_ref[h])
                o_ref[pl.ds(q0, SW_BLOCK), HEAD_DIM * h:HEAD_DIM * (h + 1)] = _dot(p.astype(BF16), v).astype(BF16)
            return carry

        lax.fori_loop(0, nb, blk, 0)

    return pl.pallas_call(
        body, name=name, grid=(1,),
        in_specs=[_full((s, SW_Q_WIDTH)), _full((s, SW_KV_WIDTH)),
                  pl.BlockSpec((s, SW_KV_WIDTH), lambda i: (0, v_blk)),
                  _full((SW_HEADS, SW_BLOCK, 3 * SW_BLOCK)), pl.BlockSpec(memory_space=pltpu.SMEM)],
        out_specs=_full((s, SW_Q_WIDTH)),
        out_shape=jax.ShapeDtypeStruct((s, SW_Q_WIDTH), BF16),
        scratch_shapes=[pltpu.VMEM((pad, SW_KV_WIDTH), BF16), pltpu.VMEM((pad, SW_KV_WIDTH), BF16)],
        compiler_params=_params(),
    )(qs, ks, zq, t5b, sink)


def sw_bwd(qs, ks, zq, t5b, sink, o_sw, do_sw, name):
    s = qs.shape[0]
    nb = s // SW_BLOCK
    v_blk = (3 * NA_WIDTH + SW_Q_WIDTH + SW_KV_WIDTH) // LANES
    pad = s + 2 * SW_BLOCK

    def body(q_ref, k_ref, v_ref, b_ref, sink_ref, o_ref, do_ref,
             dq_ref, dk_ref, dv_ref, db_ref, dsk_ref, kp, vp, dkp, dvp):
        zeros = jnp.zeros((SW_BLOCK, SW_KV_WIDTH), BF16)
        kp[0:SW_BLOCK, :] = zeros
        vp[0:SW_BLOCK, :] = zeros
        kp[SW_BLOCK + s:pad, :] = zeros
        vp[SW_BLOCK + s:pad, :] = zeros
        kp[SW_BLOCK:SW_BLOCK + s, :] = k_ref[...]
        vp[SW_BLOCK:SW_BLOCK + s, :] = v_ref[...]
        dkp[...] = jnp.zeros(dkp.shape, F32)
        dvp[...] = jnp.zeros(dvp.shape, F32)
        db_ref[...] = jnp.zeros(db_ref.shape, F32)
        dsk_ref[...] = jnp.zeros(dsk_ref.shape, F32)
        j, inwin = _sw_mask_iotas()

        def blk(n, carry):
            kpos = n * SW_BLOCK - SW_BLOCK + j
            madd = jnp.where(inwin & (kpos >= 0) & (kpos < s), 0.0, NEG)
            q0 = pl.multiple_of(n * SW_BLOCK, SW_BLOCK)
            for g in range(SW_HEADS // SW_REP):
                kl = slice(HEAD_DIM * g, HEAD_DIM * (g + 1))
                k = kp[pl.ds(q0, 3 * SW_BLOCK), kl]
                v = vp[pl.ds(q0, 3 * SW_BLOCK), kl]
                dkw = jnp.zeros((3 * SW_BLOCK, HEAD_DIM), F32)
                dvw = jnp.zeros((3 * SW_BLOCK, HEAD_DIM), F32)
                for r in range(SW_REP):
                    h = g * SW_REP + r
                    hl = slice(HEAD_DIM * h, HEAD_DIM * (h + 1))
                    q = q_ref[pl.ds(q0, SW_BLOCK), hl]
                    p, ps = _sw_probs(q, k, b_ref[h], madd, sink_ref[h])
                    do = do_ref[pl.ds(q0, SW_BLOCK), hl]
                    ov = o_ref[pl.ds(q0, SW_BLOCK), hl]
                    delta = jnp.sum(do.astype(F32) * ov.astype(F32), axis=1, keepdims=True)
                    ds = p * (_dotg(do, v, NT) - delta)
                    db_ref[h] = db_ref[h] + ds
                    dsk_ref[h] = dsk_ref[h] - jnp.broadcast_to(ps * delta, (SW_BLOCK, LANES))
                    dsb = ds.astype(BF16)
                    dq_ref[pl.ds(q0, SW_BLOCK), hl] = _dot(dsb, k)
                    dkw = dkw + _dotg(dsb, q, TN)
                    dvw = dvw + _dotg(p.astype(BF16), do, TN)
                dkp[pl.ds(q0, 3 * SW_BLOCK), kl] = dkp[pl.ds(q0, 3 * SW_BLOCK), kl] + dkw
                dvp[pl.ds(q0, 3 * SW_BLOCK), kl] = dvp[pl.ds(q0, 3 * SW_BLOCK), kl] + dvw
            return carry

        lax.fori_loop(0, nb, blk, 0)
        dk_ref[...] = dkp[SW_BLOCK:SW_BLOCK + s, :]
        dv_ref[...] = dvp[SW_BLOCK:SW_BLOCK + s, :]

    bias_spec = _full((SW_HEADS, SW_BLOCK, 3 * SW_BLOCK))
    return pl.pallas_call(
        body, name=name, grid=(1,),
        in_specs=[_full((s, SW_Q_WIDTH)), _full((s, SW_KV_WIDTH)),
                  pl.BlockSpec((s, SW_KV_WIDTH), lambda i: (0, v_blk)),
                  bias_spec, pl.BlockSpec(memory_space=pltpu.SMEM),
                  _full((s, SW_Q_WIDTH)), _full((s, SW_Q_WIDTH))],
        out_specs=[_full((s, SW_Q_WIDTH)), _full((s, SW_KV_WIDTH)), _full((s, SW_KV_WIDTH)), bias_spec,
                   _full((SW_HEADS, SW_BLOCK, LANES))],
        out_shape=[jax.ShapeDtypeStruct((s, SW_Q_WIDTH), F32), jax.ShapeDtypeStruct((s, SW_KV_WIDTH), F32),
                   jax.ShapeDtypeStruct((s, SW_KV_WIDTH), F32),
                   jax.ShapeDtypeStruct((SW_HEADS, SW_BLOCK, 3 * SW_BLOCK), F32),
                   jax.ShapeDtypeStruct((SW_HEADS, SW_BLOCK, LANES), F32)],
        scratch_shapes=[pltpu.VMEM((pad, SW_KV_WIDTH), BF16), pltpu.VMEM((pad, SW_KV_WIDTH), BF16),
                        pltpu.VMEM((pad, SW_KV_WIDTH), F32), pltpu.VMEM((pad, SW_KV_WIDTH), F32)],
        compiler_params=_params(),
    )(qs, ks, zq, t5b, sink, o_sw, do_sw)


def merge_out(x, o_na, o_sw, gt, wbna_t, wbsw_t, wout, name):
    s, d = x.shape
    tm = _row_tile(s)

    def body(x_ref, ona_ref, osw_ref, gt_ref, wna_ref, wsw_ref, wo_ref, xo_ref, ana_ref, asw_ref, mg_ref):
        a_na = _dotg(ona_ref[...], wna_ref[...], NT)
        a_sw = _dotg(osw_ref[...], wsw_ref[...], NT)
        ana_ref[...] = a_na.astype(BF16)
        asw_ref[...] = a_sw.astype(BF16)
        merged = (gt_ref[:, 0:d].astype(F32) * a_na + gt_ref[:, d:2 * d].astype(F32) * a_sw).astype(BF16)
        mg_ref[...] = merged
        xo_ref[...] = x_ref[...] + _dot(merged, wo_ref[...])

    return pl.pallas_call(
        body, name=name, grid=(s // tm,),
        in_specs=[_rows(tm, d), _rows(tm, 512), _rows(tm, 512), _rows(tm, 2 * d),
                  _mat(*wbna_t), _mat(*wbsw_t), _mat(*wout)],
        out_specs=[_rows(tm, d)] * 4,
        out_shape=[jax.ShapeDtypeStruct((s, d), F32)] + [jax.ShapeDtypeStruct((s, d), BF16)] * 3,
        compiler_params=_params(),
    )(x, o_na, o_sw, gt, wbna_t[0], wbsw_t[0], wout[0])


def mix_bwd_out(dx, gt, a_na, a_sw, wbna_t, wbsw_t, wout, name):
    s, d = dx.shape
    tm = _row_tile(s)

    def body(dx_ref, gt_ref, ana_ref, asw_ref, wna_ref, wsw_ref, wo_ref,
             dxb_ref, dzg_ref, dana_ref, dasw_ref, dona_ref, dosw_ref, dbg_ref):
        @pl.when(pl.program_id(0) == 0)
        def _():
            dbg_ref[...] = jnp.zeros(dbg_ref.shape, F32)

        dxb = dx_ref[...].astype(BF16)
        dxb_ref[...] = dxb
        dm = _dotg(dxb, wo_ref[...], NT)
        for i, (a_ref, da_ref, w_ref, do_ref) in enumerate(
                [(ana_ref, dana_ref, wna_ref, dona_ref), (asw_ref, dasw_ref, wsw_ref, dosw_ref)]):
            gi = gt_ref[:, i * d:(i + 1) * d].astype(F32)
            da = (dm * gi).astype(BF16)
            da_ref[...] = da
            do_ref[...] = _dot(da, w_ref[...]).astype(BF16)
            dzg = dm * a_ref[...].astype(F32) * gi * (1.0 - gi)
            dzg_ref[:, i * d:(i + 1) * d] = dzg.astype(BF16)
            dbg_ref[:, i * d:(i + 1) * d] = dbg_ref[:, i * d:(i + 1) * d] + jnp.sum(dzg, axis=0, keepdims=True)

    return pl.pallas_call(
        body, name=name, grid=(s // tm,),
        in_specs=[_rows(tm, d), _rows(tm, 2 * d), _rows(tm, d), _rows(tm, d),
                  _mat(*wbna_t), _mat(*wbsw_t), _mat(*wout)],
        out_specs=[_rows(tm, d), _rows(tm, 2 * d), _rows(tm, d), _rows(tm, d), _rows(tm, 512), _rows(tm, 512),
                   _full((1, 2 * d))],
        out_shape=[jax.ShapeDtypeStruct((s, d), BF16), jax.ShapeDtypeStruct((s, 2 * d), BF16),
                   jax.ShapeDtypeStruct((s, d), BF16), jax.ShapeDtypeStruct((s, d), BF16),
                   jax.ShapeDtypeStruct((s, 512), BF16), jax.ShapeDtypeStruct((s, 512), BF16),
                   jax.ShapeDtypeStruct((1, 2 * d), F32)],
        compiler_params=_params(),
    )(dx, gt, a_na, a_sw, wbna_t[0], wbsw_t[0], wout[0])


def qk_norm_bwd(dqa, dka, dva, dqs, dks, dvs, zq, dzg, gq_na, gk_na, gq_sw, gk_sw, bd, name):
    s = zq.shape[0]
    d2 = dzg.shape[1]
    n_in = QKV_WIDTH + d2
    tm = _row_tile(s)

    def body(dqa_ref, dka_ref, dva_ref, dqs_ref, dks_ref, dvs_ref, zq_ref, dzg_ref,
             gqa_ref, gka_ref, gqs_ref, gks_ref, bd_ref, dz_ref, dgqa_ref, dgka_ref, dgqs_ref, dgks_ref):
        @pl.when(pl.program_id(0) == 0)
        def _():
            for r in (dgqa_ref, dgka_ref, dgqs_ref, dgks_ref):
                r[...] = jnp.zeros(r.shape, F32)

        bd512 = bd_ref[...]
        bd128 = bd_ref[0:SW_KV_WIDTH, 0:SW_KV_WIDTH]

        def one(c0, c1, dy_ref, g_ref, dg_ref, bdm, scale):
            z = zq_ref[:, c0:c1].astype(F32)
            r = lax.rsqrt(_group_mean(z * z, bdm) + EPS)
            zh = z * r
            dy = dy_ref[...] * scale
            dyg = dy * g_ref[...]
            dz = r * (dyg - zh * _group_mean(dyg * zh, bdm))
            dz_ref[:, c0:c1] = dz.astype(BF16)
            dg_ref[...] = dg_ref[...] + jnp.sum(dy * zh, axis=0, keepdims=True)

        one(0, 512, dqa_ref, gqa_ref, dgqa_ref, bd512, SCALE)
        one(512, 1024, dka_ref, gka_ref, dgka_ref, bd512, 1.0)
        dz_ref[:, 1024:1536] = dva_ref[...].astype(BF16)
        one(1536, 2048, dqs_ref, gqs_ref, dgqs_ref, bd512, SCALE)
        one(2048, 2176, dks_ref, gks_ref, dgks_ref, bd128, 1.0)
        dz_ref[:, 2176:2304] = dvs_ref[...].astype(BF16)
        dz_ref[:, QKV_WIDTH:n_in] = dzg_ref[...]

    return pl.pallas_call(
        body, name=name, grid=(s // tm,),
        in_specs=[_rows(tm, 512), _rows(tm, 512), _rows(tm, 512), _rows(tm, 512), _rows(tm, 128), _rows(tm, 128),
                  _rows(tm, QKV_WIDTH), _rows(tm, d2),
                  _full((1, 512)), _full((1, 512)), _full((1, 512)), _full((1, 128)), _full((512, 512))],
        out_specs=[_rows(tm, n_in), _full((1, 512)), _full((1, 512)), _full((1, 512)), _full((1, 128))],
        out_shape=[jax.ShapeDtypeStruct((s, n_in), BF16)] + [jax.ShapeDtypeStruct((1, 512), F32)] * 3
                  + [jax.ShapeDtypeStruct((1, 128), F32)],
        compiler_params=_params(),
    )(dqa, dka, dva, dqs, dks, dvs, zq, dzg, gq_na, gk_na, gq_sw, gk_sw, bd)


def ffn_bwd_act(dx, wd, hg, hu, name):
    s, d = dx.shape
    f = wd[0].shape[1]
    tm = _row_tile(s)
    fc = _col_chunk(f)

    def body(dx_ref, w_ref, hg_ref, hu_ref, dxb_ref, dhg_ref, dhu_ref):
        dxb = dx_ref[...].astype(BF16)
        dxb_ref[...] = dxb
        for c0 in range(0, f, fc):
            dact = 0.5 * _dotg(dxb, w_ref[c0:c0 + fc, :], NT)
            hg = hg_ref[:, c0:c0 + fc].astype(F32)
            hu = hu_ref[:, c0:c0 + fc].astype(F32)
            sg = _sigmoid(hg)
            dhu_ref[:, c0:c0 + fc] = (dact * hg * sg).astype(BF16)
            dhg_ref[:, c0:c0 + fc] = (dact * hu * sg * (1.0 + hg * (1.0 - sg))).astype(BF16)

    return pl.pallas_call(
        body, name=name, grid=(s // tm,),
        in_specs=[_rows(tm, d), _mat(*wd), _rows(tm, f), _rows(tm, f)],
        out_specs=[_rows(tm, d), _rows(tm, f), _rows(tm, f)],
        out_shape=[jax.ShapeDtypeStruct((s, d), BF16), jax.ShapeDtypeStruct((s, f), BF16),
                   jax.ShapeDtypeStruct((s, f), BF16)],
        compiler_params=_params(),
    )(dx, wd[0], hg, hu)


def proj_bwd_norm(acts, weights, x, gain, dx, name):
    s, d = x.shape
    tm = _row_tile(s)
    n = len(acts)

    def body(*refs):
        a_refs, w_refs = refs[:n], refs[n:2 * n]
        x_ref, g_ref, dx_ref, o_ref, dg_ref = refs[2 * n:]

        @pl.when(pl.program_id(0) == 0)
        def _():
            dg_ref[...] = jnp.zeros(dg_ref.shape, F32)

        dxn = _dot(a_refs[0][...], w_refs[0][...])
        for a_ref, w_ref in zip(a_refs[1:], w_refs[1:]):
            dxn = dxn + _dot(a_ref[...], w_ref[...])
        xv = x_ref[...]
        r = _rstd(xv)
        xh = xv * r
        dxh = dxn * g_ref[...]
        o_ref[...] = dx_ref[...] + r * (dxh - xh * jnp.mean(dxh * xh, axis=-1, keepdims=True))
        dg_ref[...] = dg_ref[...] + jnp.sum(dxn * xh, axis=0, keepdims=True)

    return pl.pallas_call(
        body, name=name, grid=(s // tm,),
        in_specs=[_rows(tm, a.shape[1]) for a in acts] + [_mat(*w) for w in weights]
                 + [_rows(tm, d), _full((1, d)), _rows(tm, d)],
        out_specs=[_rows(tm, d), _full((1, d))],
        out_shape=[jax.ShapeDtypeStruct((s, d), F32), jax.ShapeDtypeStruct((1, d), F32)],
        compiler_params=_params(),
    )(*acts, *[w[0] for w in weights], x, gain, dx)


def tn_matmul(a, b, scale, name):
    s, n = a.shape
    k = b.shape[1]
    tn = _col_chunk(n)

    def body(a_ref, b_ref, o_ref):
        o_ref[...] = (scale * _dotg(a_ref[...], b_ref[...], TN)).astype(BF16)

    return pl.pallas_call(
        body, name=name, grid=(n // tn,),
        in_specs=[pl.BlockSpec((s, tn), lambda i: (0, i)), _full((s, k))],
        out_specs=pl.BlockSpec((tn, k), lambda i: (i, 0)),
        out_shape=jax.ShapeDtypeStruct((n, k), BF16),
        compiler_params=_params(),
    )(a, b)


def loss_grad(y, target, name):
    s, d = y.shape
    tm = _row_tile(s)

    def body(y_ref, t_ref, dy_ref, acc_ref):
        @pl.when(pl.program_id(0) == 0)
        def _():
            acc_ref[...] = jnp.zeros(acc_ref.shape, F32)

        err = y_ref[...] - t_ref[...]
        dy_ref[...] = err * (1.0 / d)
        e2 = err * err
        part = jnp.sum(e2.reshape(tm // 8, 8, d), axis=0)
        acc = part[:, 0:LANES]
        for c0 in range(LANES, d, LANES):
            acc = acc + part[:, c0:c0 + LANES]
        acc_ref[...] = acc_ref[...] + acc

    return pl.pallas_call(
        body, name=name, grid=(s // tm,),
        in_specs=[_rows(tm, d), _rows(tm, d)],
        out_specs=[_rows(tm, d), _full((8, LANES))],
        out_shape=[jax.ShapeDtypeStruct((s, d), F32), jax.ShapeDtypeStruct((8, LANES), F32)],
        compiler_params=_params(),
    )(y, target)


def _mesh_pos():
    return lax.axis_index("x"), lax.axis_index("y"), lax.axis_index("c")


def gather_weights(shards):
    n = len(shards)

    def body(*refs):
        ins, outs = refs[:n], refs[n:2 * n]
        send_sems, recv_sems, local_sems = refs[2 * n:]
        x, y, c = _mesh_pos()
        me, sibling = (x, y, c), (x, y, 1 - c)
        chips = [(1 - x, y), (x, 1 - y), (1 - x, 1 - y)]

        def slot(a, px, py, pc):
            return outs[a].at[:, 4 * px + 2 * py + pc]

        def copy(a, k, block, to, src=None):
            return pltpu.make_async_remote_copy(
                src_ref=slot(a, *block) if src is None else src, dst_ref=slot(a, *block),
                send_sem=send_sems.at[a, k], recv_sem=recv_sems.at[a, k], device_id=to, device_id_type=MESH)

        mine = [pltpu.make_async_copy(ins[a], slot(a, *me), local_sems.at[a]) for a in range(n)]
        for cp in mine:
            cp.start()
        first = []
        for a in range(n):
            first.append(copy(a, 0, me, sibling, src=ins[a]))
            first += [copy(a, 1 + j, me, (*chip, c), src=ins[a]) for j, chip in enumerate(chips)]
        for cp in first:
            cp.start()
        passed = []
        for j, chip in enumerate(chips):
            for a in range(n):
                copy(a, 1 + j, (*chip, c), me).wait_recv()
                fwd = copy(a, 4 + j, (*chip, c), sibling)
                fwd.start()
                passed.append(fwd)
        for a in range(n):
            copy(a, 0, sibling, me).wait_recv()
            for j, chip in enumerate(chips):
                copy(a, 4 + j, (*chip, 1 - c), me).wait_recv()
        for cp in first + passed:
            cp.wait_send()
        for cp in mine:
            cp.wait()

    any_spec = pl.BlockSpec(memory_space=pl.ANY)
    return pl.pallas_call(
        body, name="gather_weights",
        in_specs=[any_spec] * n, out_specs=[any_spec] * n,
        out_shape=[jax.ShapeDtypeStruct((w.shape[0], N_DEV) + w.shape[1:], w.dtype) for w in shards],
        scratch_shapes=[pltpu.SemaphoreType.DMA((n, 7)), pltpu.SemaphoreType.DMA((n, 7)),
                        pltpu.SemaphoreType.DMA((n,))],
        compiler_params=pltpu.CompilerParams(has_side_effects=True),
    )(*shards)


def scatter_grads(groups, small):
    n = len(groups)
    flat = [g for grp in groups for g in grp]
    offs = np.cumsum([0] + [len(grp) for grp in groups])

    def body(*refs):
        ins = refs[:len(flat)]
        small_ref = refs[len(flat)]
        outs = refs[len(flat) + 1:len(flat) + 1 + n]
        small_out = refs[len(flat) + 1 + n]
        send_sems, recv_sems, local_sems = refs[len(flat) + 2 + n:]
        x, y, c = _mesh_pos()
        me = 4 * x + 2 * y + c
        peers = []
        for rel in range(1, N_DEV):
            px = 1 - x if rel & 4 else x
            py = 1 - y if rel & 2 else y
            pc = 1 - c if rel & 1 else c
            peers.append((px, py, pc))

        started = []
        for a in range(n):
            for w in range(len(groups[a])):
                cp = pltpu.make_async_copy(ins[offs[a] + w].at[me], outs[a].at[me, w], local_sems.at[a])
                cp.start()
        cp_small = pltpu.make_async_copy(small_ref, small_out.at[me], local_sems.at[n])
        cp_small.start()
        for k, peer in enumerate(peers):
            p_id = 4 * peer[0] + 2 * peer[1] + peer[2]
            for a in range(n):
                for w in range(len(groups[a])):
                    pltpu.make_async_remote_copy(
                        src_ref=ins[offs[a] + w].at[p_id], dst_ref=outs[a].at[me, w],
                        send_sem=send_sems.at[a, k], recv_sem=recv_sems.at[a, k],
                        device_id=peer, device_id_type=MESH).start()
            pltpu.make_async_remote_copy(
                src_ref=small_ref, dst_ref=small_out.at[me],
                send_sem=send_sems.at[n, k], recv_sem=recv_sems.at[n, k],
                device_id=peer, device_id_type=MESH).start()
        for k, peer in enumerate(peers):
            for a in range(n):
                pltpu.make_async_remote_copy(
                    src_ref=outs[a].at[me], dst_ref=outs[a].at[me],
                    send_sem=send_sems.at[a, k], recv_sem=recv_sems.at[a, k],
                    device_id=peer, device_id_type=MESH).wait()
            pltpu.make_async_remote_copy(
                src_ref=small_out.at[me], dst_ref=small_out.at[me],
                send_sem=send_sems.at[n, k], recv_sem=recv_sems.at[n, k],
                device_id=peer, device_id_type=MESH).wait()
        for a in range(n):
            pltpu.make_async_copy(outs[a].at[me], outs[a].at[me], local_sems.at[a]).wait()
        cp_small.wait()
        del started

    any_spec = pl.BlockSpec(memory_space=pl.ANY)
    out_shape = [jax.ShapeDtypeStruct((N_DEV, len(grp)) + grp[0].shape[1:], grp[0].dtype) for grp in groups]
    out_shape.append(jax.ShapeDtypeStruct((N_DEV,) + small.shape, small.dtype))
    return pl.pallas_call(
        body, name="scatter_grads",
        in_specs=[any_spec] * (len(flat) + 1), out_specs=[any_spec] * (n + 1),
        out_shape=out_shape,
        scratch_shapes=[pltpu.SemaphoreType.DMA((n + 1, 7)), pltpu.SemaphoreType.DMA((n + 1, 7)),
                        pltpu.SemaphoreType.DMA((n + 1,))],
        compiler_params=pltpu.CompilerParams(has_side_effects=True),
    )(*flat, small)


def sum_sources(recv, name):
    _, w, r, c = recv.shape

    def body(r_ref, o_ref):
        acc = r_ref[0, 0].astype(F32)
        for src in range(1, N_DEV):
            acc = acc + r_ref[src, 0].astype(F32)
        o_ref[0] = acc

    return pl.pallas_call(
        body, name=name, grid=(w,),
        in_specs=[pl.BlockSpec((N_DEV, 1, r, c), lambda i: (0, i, 0, 0))],
        out_specs=pl.BlockSpec((1, r, c), lambda i: (i, 0, 0)),
        out_shape=jax.ShapeDtypeStruct((w, r, c), F32),
        compiler_params=_params(),
    )(recv)


def _adamw_math(w, g, m, v):
    m = ADAM_B1 * m + (1.0 - ADAM_B1) * g
    v = ADAM_B2 * v + (1.0 - ADAM_B2) * (g * g)
    m_hat = m / (1.0 - ADAM_B1 ** ADAM_STEP)
    v_hat = v / (1.0 - ADAM_B2 ** ADAM_STEP)
    delta = -ADAM_LR * (m_hat / (jnp.sqrt(v_hat) + ADAM_EPS) + ADAM_WD * w)
    return delta, m, v


def adamw(w, g, m, v, name):
    shape = w.shape
    c = shape[-1]
    r = int(np.prod(shape[:-1]))
    w2, g2, m2, v2 = (t.reshape(r, c) for t in (w, g, m, v))
    tr = next(t for t in range(min(r, 512), 0, -1) if r % t == 0 and (t % 8 == 0 or t == r))

    def body(w_ref, g_ref, m_ref, v_ref, d_ref, mo_ref, vo_ref):
        d_ref[...], mo_ref[...], vo_ref[...] = _adamw_math(w_ref[...], g_ref[...], m_ref[...], v_ref[...])

    spec = pl.BlockSpec((tr, c), lambda i: (i, 0))
    outs = pl.pallas_call(
        body, name=name, grid=(r // tr,),
        in_specs=[spec] * 4, out_specs=[spec] * 3,
        out_shape=[jax.ShapeDtypeStruct((r, c), F32)] * 3,
        compiler_params=_params(),
    )(w2, g2, m2, v2)
    return tuple(t.reshape(shape) for t in outs)


def adamw_small(w, recv, m, v, name):
    def body(w_ref, r_ref, m_ref, v_ref, g_ref, d_ref, mo_ref, vo_ref):
        g = r_ref[0]
        for src in range(1, N_DEV):
            g = g + r_ref[src]
        g_ref[...] = g
        d_ref[...], mo_ref[...], vo_ref[...] = _adamw_math(w_ref[...], g, m_ref[...], v_ref[...])

    vm = pl.BlockSpec(memory_space=pltpu.VMEM)
    return pl.pallas_call(
        body, name=name, in_specs=[vm] * 4, out_specs=[vm] * 4,
        out_shape=[jax.ShapeDtypeStruct(w.shape, F32)] * 4,
        compiler_params=pltpu.CompilerParams(vmem_limit_bytes=V7X_VMEM_LIMIT),
    )(w, recv, m, v)


SMALL_NAMES = ("ffn1_norm", "mix_norm", "ffn2_norm", "b_gate", "na_q_norm", "na_k_norm", "sw_q_norm", "sw_k_norm",
               "na_rpb", "sw_sink", "t5_rel_table")


def _pack_small(parts):
    flat = jnp.concatenate([parts[k].reshape(-1).astype(F32) for k in SMALL_NAMES])
    n = flat.shape[0]
    rows = -(-n // (8 * LANES)) * 8
    return jnp.pad(flat, (0, rows * LANES - n)).reshape(rows, LANES)


def _unpack_small(packed, like):
    flat = packed.reshape(-1)
    out, off = {}, 0
    for k in SMALL_NAMES:
        n = int(np.prod(like[k].shape))
        out[k] = flat[off:off + n].reshape(like[k].shape)
        off += n
    return out


def kernel(x, ffn1_norm, ffn1_w_gate, ffn1_w_up, ffn1_w_down, mix_norm, w_in, b_gate, na_q_norm, na_k_norm, na_rpb, sw_q_norm, sw_k_norm, sw_sink, t5_rel_table, w_branch_na, w_branch_sw, w_out, ffn2_norm, ffn2_w_gate, ffn2_w_up, ffn2_w_down, loss_target, m_ffn1_norm, m_ffn1_w_gate, m_ffn1_w_up, m_ffn1_w_down, m_mix_norm, m_w_in, m_b_gate, m_na_q_norm, m_na_k_norm, m_na_rpb, m_sw_q_norm, m_sw_k_norm, m_sw_sink, m_t5_rel_table, m_w_branch_na, m_w_branch_sw, m_w_out, m_ffn2_norm, m_ffn2_w_gate, m_ffn2_w_up, m_ffn2_w_down, v_ffn1_norm, v_ffn1_w_gate, v_ffn1_w_up, v_ffn1_w_down, v_mix_norm, v_w_in, v_b_gate, v_na_q_norm, v_na_k_norm, v_na_rpb, v_sw_q_norm, v_sw_k_norm, v_sw_sink, v_t5_rel_table, v_w_branch_na, v_w_branch_sw, v_w_out, v_ffn2_norm, v_ffn2_w_gate, v_ffn2_w_up, v_ffn2_w_down):
    weights = dict(ffn1_norm=ffn1_norm, ffn1_w_gate=ffn1_w_gate, ffn1_w_up=ffn1_w_up, ffn1_w_down=ffn1_w_down,
                   mix_norm=mix_norm, w_in=w_in, b_gate=b_gate, na_q_norm=na_q_norm, na_k_norm=na_k_norm,
                   na_rpb=na_rpb, sw_q_norm=sw_q_norm, sw_k_norm=sw_k_norm, sw_sink=sw_sink,
                   t5_rel_table=t5_rel_table, w_branch_na=w_branch_na, w_branch_sw=w_branch_sw, w_out=w_out,
                   ffn2_norm=ffn2_norm, ffn2_w_gate=ffn2_w_gate, ffn2_w_up=ffn2_w_up, ffn2_w_down=ffn2_w_down)
    mom_m = dict(ffn1_norm=m_ffn1_norm, ffn1_w_gate=m_ffn1_w_gate, ffn1_w_up=m_ffn1_w_up, ffn1_w_down=m_ffn1_w_down,
                 mix_norm=m_mix_norm, w_in=m_w_in, b_gate=m_b_gate, na_q_norm=m_na_q_norm, na_k_norm=m_na_k_norm,
                 na_rpb=m_na_rpb, sw_q_norm=m_sw_q_norm, sw_k_norm=m_sw_k_norm, sw_sink=m_sw_sink,
                 t5_rel_table=m_t5_rel_table, w_branch_na=m_w_branch_na, w_branch_sw=m_w_branch_sw, w_out=m_w_out,
                 ffn2_norm=m_ffn2_norm, ffn2_w_gate=m_ffn2_w_gate, ffn2_w_up=m_ffn2_w_up, ffn2_w_down=m_ffn2_w_down)
    mom_v = dict(ffn1_norm=v_ffn1_norm, ffn1_w_gate=v_ffn1_w_gate, ffn1_w_up=v_ffn1_w_up, ffn1_w_down=v_ffn1_w_down,
                 mix_norm=v_mix_norm, w_in=v_w_in, b_gate=v_b_gate, na_q_norm=v_na_q_norm, na_k_norm=v_na_k_norm,
                 na_rpb=v_na_rpb, sw_q_norm=v_sw_q_norm, sw_k_norm=v_sw_k_norm, sw_sink=v_sw_sink,
                 t5_rel_table=v_t5_rel_table, w_branch_na=v_w_branch_na, w_branch_sw=v_w_branch_sw, w_out=v_w_out,
                 ffn2_norm=v_ffn2_norm, ffn2_w_gate=v_ffn2_w_gate, ffn2_w_up=v_ffn2_w_up, ffn2_w_down=v_ffn2_w_down)
    order = list(weights)

    depth = ffn1_norm.shape[0]
    s, d = x.shape[1], x.shape[2]
    xs = x[0]
    tr = lambda w: jnp.swapaxes(w, -1, -2)

    a_loc = jnp.stack([t for l in range(depth) for t in (
        tr(ffn1_w_gate[l]), tr(ffn1_w_up[l]), ffn1_w_down[l],
        tr(ffn2_w_gate[l]), tr(ffn2_w_up[l]), ffn2_w_down[l])]).astype(BF16)
    b_loc = tr(w_in).astype(BF16)
    c_loc = w_out.astype(BF16)
    d_loc = jnp.stack([t for l in range(depth) for t in (tr(w_branch_na[l]), tr(w_branch_sw[l]))]).astype(BF16)
    a_all, b_all, c_all, d_all = gather_weights([a_loc, b_loc, c_loc, d_loc])
    merge = lambda t: t.reshape(t.shape[0], N_DEV * t.shape[2], t.shape[3])
    a_all, b_all, c_all, d_all = merge(a_all), merge(b_all), merge(c_all), merge(d_all)

    bd = jnp.asarray(np.kron(np.eye(NA_WIDTH // HEAD_DIM), np.full((HEAD_DIM, HEAD_DIM), 1.0 / HEAD_DIM)), BF16)
    bmap = jnp.asarray(_t5_bucket_map())
    tile8 = lambda g: jnp.tile(g, NA_WIDTH // HEAD_DIM).reshape(1, NA_WIDTH)
    tile2 = lambda g: jnp.tile(g, SW_KV_WIDTH // HEAD_DIM).reshape(1, SW_KV_WIDTH)
    t5b = t5_expand(t5_rel_table, bmap, "t5_expand")

    saved = []
    cur = xs
    for l in range(depth):
        sv = {}
        wg1, wu1, wd1, wg2, wu2, wd2 = ((a_all, 6 * l + i) for i in range(6))
        win_t, wout_l, wna_t, wsw_t = (b_all, l), (c_all, l), (d_all, 2 * l), (d_all, 2 * l + 1)
        sv["x0"] = cur
        sv["xn1"], sv["hg1"], sv["hu1"], sv["act1"] = ffn_up(cur, ffn1_norm[l][None], wg1, wu1, f"ffn1_up_{l}")
        cur = ffn_down(cur, sv["act1"], wd1, f"ffn1_down_{l}")
        sv["x1"] = cur
        sv["gains"] = (tile8(na_q_norm[l]), tile8(na_k_norm[l]), tile8(sw_q_norm[l]), tile2(sw_k_norm[l]))
        sv["hn"], sv["zq"], sv["qa"], sv["ka"], sv["qs"], sv["ks"], sv["gt"] = mix_in(
            cur, mix_norm[l][None], win_t, b_gate[l][None], *sv["gains"], bd, f"mix_in_{l}")
        sv["t2"] = rpb_expand(na_rpb[l].reshape(-1), na_rpb.shape[1], f"rpb_expand_{l}")
        sv["o_na"] = na_fwd(sv["qa"], sv["ka"], sv["zq"], sv["t2"], f"na_fwd_{l}")
        sv["o_sw"] = sw_fwd(sv["qs"], sv["ks"], sv["zq"], t5b, sw_sink[l], f"sw_fwd_{l}")
        cur, sv["a_na"], sv["a_sw"], sv["merged"] = merge_out(
            cur, sv["o_na"], sv["o_sw"], sv["gt"], wna_t, wsw_t, wout_l, f"merge_out_{l}")
        sv["x2"] = cur
        sv["xn2"], sv["hg2"], sv["hu2"], sv["act2"] = ffn_up(cur, ffn2_norm[l][None], wg2, wu2, f"ffn2_up_{l}")
        cur = ffn_down(cur, sv["act2"], wd2, f"ffn2_down_{l}")
        saved.append(sv)

    dx, loss_acc = loss_grad(cur, loss_target[0], "loss_grad")
    loss = lax.psum(jnp.sum(loss_acc) * (0.5 / d), ("x", "y", "c"))

    ga = [None] * (6 * depth)
    gb = [None] * depth
    gc = [None] * depth
    gd = [None] * (2 * depth)
    small = {k: [None] * depth for k in SMALL_NAMES if k != "t5_rel_table"}
    dbias_sw = []
    for l in reversed(range(depth)):
        sv = saved[l]
        wg1, wu1, wd1, wg2, wu2, wd2 = ((a_all, 6 * l + i) for i in range(6))
        win_t, wout_l, wna_t, wsw_t = (b_all, l), (c_all, l), (d_all, 2 * l), (d_all, 2 * l + 1)
        blocks = ((2, "x2", "xn2", "hg2", "hu2", "act2", wg2, wu2, wd2, "ffn2_norm", 3),
                  (1, "x0", "xn1", "hg1", "hu1", "act1", wg1, wu1, wd1, "ffn1_norm", 0))

        def ffn_backward(dx, blk):
            tag, xk, xnk, hgk, huk, actk, wg, wu, wd, norm_name, slot = blk
            gains = weights[norm_name]
            dxb, dhg, dhu = ffn_bwd_act(dx, wd, sv[hgk], sv[huk], f"ffn{tag}_bwd_act_{l}")
            ga[6 * l + slot + 2] = tn_matmul(sv[actk], dxb, 0.5, f"ffn{tag}_dwd_{l}")
            ga[6 * l + slot + 0] = tn_matmul(dhg, sv[xnk], 1.0, f"ffn{tag}_dwg_{l}")
            ga[6 * l + slot + 1] = tn_matmul(dhu, sv[xnk], 1.0, f"ffn{tag}_dwu_{l}")
            dx, dg = proj_bwd_norm([dhg, dhu], [wg, wu], sv[xk], gains[l][None], dx, f"ffn{tag}_bwd_x_{l}")
            small[norm_name][l] = dg[0]
            return dx

        dx = ffn_backward(dx, blocks[0])
        dxb, dzg, da_na, da_sw, do_na, do_sw, dbg = mix_bwd_out(
            dx, sv["gt"], sv["a_na"], sv["a_sw"], wna_t, wsw_t, wout_l, f"mix_bwd_out_{l}")
        small["b_gate"][l] = dbg[0]
        gc[l] = tn_matmul(sv["merged"], dxb, 1.0, f"dwout_{l}")
        gd[2 * l] = tn_matmul(da_na, sv["o_na"], 1.0, f"dwna_{l}")
        gd[2 * l + 1] = tn_matmul(da_sw, sv["o_sw"], 1.0, f"dwsw_{l}")
        dqa, dka, dva, dt2 = na_bwd(sv["qa"], sv["ka"], sv["zq"], sv["t2"], sv["o_na"], do_na, f"na_bwd_{l}")
        dqs, dks, dvs, dbias, dsink = sw_bwd(sv["qs"], sv["ks"], sv["zq"], t5b, sw_sink[l], sv["o_sw"], do_sw,
                                             f"sw_bwd_{l}")
        dbias_sw.append(dbias)
        small["sw_sink"][l] = jnp.sum(dsink[:, :, 0], axis=1)
        drpb = rpb_reduce(dt2, f"rpb_reduce_{l}")
        small["na_rpb"][l] = drpb[:, :, :2 * NA_COLS - 1, 0]
        dz, dgqa, dgka, dgqs, dgks = qk_norm_bwd(dqa, dka, dva, dqs, dks, dvs, sv["zq"], dzg, *sv["gains"], bd,
                                                 f"qk_norm_bwd_{l}")
        fold = lambda g: jnp.sum(g.reshape(-1, HEAD_DIM), axis=0)
        small["na_q_norm"][l], small["na_k_norm"][l] = fold(dgqa), fold(dgka)
        small["sw_q_norm"][l], small["sw_k_norm"][l] = fold(dgqs), fold(dgks)
        gb[l] = tn_matmul(dz, sv["hn"], 1.0, f"dwin_{l}")
        dx, dg = proj_bwd_norm([dz], [win_t], sv["x1"], mix_norm[l][None], dx, f"mix_bwd_x_{l}")
        small["mix_norm"][l] = dg[0]
        dx = ffn_backward(dx, blocks[1])

    dtab = t5_reduce(dbias_sw, bmap, "t5_reduce")
    small_parts = {k: jnp.stack(v) for k, v in small.items()}
    small_parts["t5_rel_table"] = jnp.transpose(dtab[:, :, 0])
    small_packed = _pack_small(small_parts)

    split = lambda t: t.reshape(N_DEV, t.shape[0] // N_DEV, t.shape[1])
    groups = [[split(t) for t in ga], [split(t) for t in gb], [split(t) for t in gc], [split(t) for t in gd]]
    ra, rb, rc, rd, rs_ = scatter_grads(groups, small_packed)
    sa = sum_sources(ra, "sum_a")
    sb = sum_sources(rb, "sum_b")
    sc = sum_sources(rc, "sum_c")
    sd = sum_sources(rd, "sum_d")

    grads = {}
    pick = lambda i: jnp.stack([sa[6 * l + i] for l in range(depth)])
    grads["ffn1_w_gate"], grads["ffn1_w_up"], grads["ffn1_w_down"] = tr(pick(0)), tr(pick(1)), pick(2)
    grads["ffn2_w_gate"], grads["ffn2_w_up"], grads["ffn2_w_down"] = tr(pick(3)), tr(pick(4)), pick(5)
    grads["w_in"] = tr(sb)
    grads["w_out"] = sc
    grads["w_branch_na"] = tr(jnp.stack([sd[2 * l] for l in range(depth)]))
    grads["w_branch_sw"] = tr(jnp.stack([sd[2 * l + 1] for l in range(depth)]))

    delta, new_m, new_v = {}, {}, {}
    for k in order:
        if k in SMALL_NAMES:
            continue
        delta[k], new_m[k], new_v[k] = adamw(weights[k], grads[k], mom_m[k], mom_v[k], f"adamw_{k}")
    g_s, d_s, m_s, v_s = adamw_small(_pack_small(weights), rs_, _pack_small(mom_m), _pack_small(mom_v), "adamw_small")
    for dst, packed in ((grads, g_s), (delta, d_s), (new_m, m_s), (new_v, v_s)):
        dst.update(_unpack_small(packed, weights))

    return (loss, dx[None], *[grads[k] for k in order], *[delta[k] for k in order],
            *[new_m[k] for k in order], *[new_v[k] for k in order])
```

```python
import functools
import math

import numpy as np
import jax
import jax.numpy as jnp
from jax import lax
from jax.experimental import pallas as pl
from jax.experimental.pallas import tpu as pltpu

F32 = jnp.float32
BF16 = jnp.bfloat16
MESH = pl.DeviceIdType.MESH

N_DEV = 8
EPS = 1e-6
NEG = -1e30
HEAD_DIM = 64
GRID_W = 64
NA_ROWS = 8
NA_COLS = 16
NA_WIDTH = 512
SW_Q_WIDTH = 512
SW_KV_WIDTH = 128
SW_BLOCK = 128
SW_HEADS = 8
SW_REP = 4
REL_BUCKETS = 32
REL_MAX_DIST = 128
QKV_WIDTH = 3 * NA_WIDTH + SW_Q_WIDTH + 2 * SW_KV_WIDTH
SCALE = 1.0 / math.sqrt(HEAD_DIM)

ADAM_LR = 0.001
ADAM_B1 = 0.9
ADAM_B2 = 0.999
ADAM_EPS = 1e-08
ADAM_WD = 0.01
ADAM_STEP = 10

V7X_VMEM_LIMIT = 56 * 1024 * 1024
LANES = 128
MXU_TILE = 256

NT = (((1,), (1,)), ((), ()))
TN = (((0,), (0,)), ((), ()))


def _params(n_grid=1):
    return pltpu.CompilerParams(dimension_semantics=("arbitrary",) * n_grid,
                                vmem_limit_bytes=V7X_VMEM_LIMIT)


def _row_tile(s):
    for t in (256, 128, 64, 32, 16, 8):
        if s % t == 0:
            return t
    raise ValueError(s)


def _col_chunk(n):
    return MXU_TILE if n % MXU_TILE == 0 else n


def _dot(a, b):
    return jnp.dot(a, b, preferred_element_type=F32)


def _dotg(a, b, dn):
    return lax.dot_general(a, b, dn, preferred_element_type=F32)


def _sigmoid(v):
    return 1.0 / (1.0 + jnp.exp(-v))


def _rstd(xv):
    return lax.rsqrt(jnp.mean(xv * xv, axis=-1, keepdims=True) + EPS)


def _full(shape):
    nd = len(shape)
    return pl.BlockSpec(shape, lambda i, _n=nd: (0,) * _n)


def _rows(tm, width):
    return pl.BlockSpec((tm, width), lambda i: (i, 0))


def _mat(stack, idx):
    return pl.BlockSpec((None,) + tuple(stack.shape[1:]), lambda i, _w=idx: (_w, 0, 0))


def _group_mean(v, bd):
    hi = v.astype(BF16)
    lo = (v - hi.astype(F32)).astype(BF16)
    return _dot(hi, bd) + _dot(lo, bd)


def ffn_up(x, gain, wg_t, wu_t, name):
    s, d = x.shape
    f = wg_t[0].shape[1]
    tm = _row_tile(s)
    fc = _col_chunk(f)

    def body(x_ref, g_ref, wg_ref, wu_ref, xn_ref, hg_ref, hu_ref, act_ref):
        xv = x_ref[...]
        xn = (xv * _rstd(xv) * g_ref[...]).astype(BF16)
        xn_ref[...] = xn
        for c0 in range(0, f, fc):
            hg = _dotg(xn, wg_ref[c0:c0 + fc, :], NT)
            hu = _dotg(xn, wu_ref[c0:c0 + fc, :], NT)
            hg_ref[:, c0:c0 + fc] = hg.astype(BF16)
            hu_ref[:, c0:c0 + fc] = hu.astype(BF16)
            act_ref[:, c0:c0 + fc] = (hg * _sigmoid(hg) * hu).astype(BF16)

    return pl.pallas_call(
        body, name=name, grid=(s // tm,),
        in_specs=[_rows(tm, d), _full((1, d)), _mat(*wg_t), _mat(*wu_t)],
        out_specs=[_rows(tm, d), _rows(tm, f), _rows(tm, f), _rows(tm, f)],
        out_shape=[jax.ShapeDtypeStruct((s, d), BF16)] + [jax.ShapeDtypeStruct((s, f), BF16)] * 3,
        compiler_params=_params(),
    )(x, gain, wg_t[0], wu_t[0])


def ffn_down(x, act, wd, name):
    s, d = x.shape
    f = act.shape[1]
    tm = _row_tile(s)

    def body(x_ref, a_ref, w_ref, o_ref):
        o_ref[...] = x_ref[...] + 0.5 * _dot(a_ref[...], w_ref[...])

    return pl.pallas_call(
        body, name=name, grid=(s // tm,),
        in_specs=[_rows(tm, d), _rows(tm, f), _mat(*wd)],
        out_specs=_rows(tm, d),
        out_shape=jax.ShapeDtypeStruct((s, d), F32),
        compiler_params=_params(),
    )(x, act, wd[0])


def mix_in(x, gain, win_t, b_gate, gq_na, gk_na, gq_sw, gk_sw, bd, name):
    s, d = x.shape
    tm = _row_tile(s)
    gc = _col_chunk(2 * d)

    def body(x_ref, g_ref, w_ref, b_ref, gqa_ref, gka_ref, gqs_ref, gks_ref, bd_ref,
             hn_ref, zq_ref, qa_ref, ka_ref, qs_ref, ks_ref, gt_ref):
        xv = x_ref[...]
        hn = (xv * _rstd(xv) * g_ref[...]).astype(BF16)
        hn_ref[...] = hn

        def proj(c0, c1):
            return _dotg(hn, w_ref[c0:c1, :], NT)

        def headnorm(z, g, bdm):
            return z * lax.rsqrt(_group_mean(z * z, bdm) + EPS) * g

        bd512 = bd_ref[...]
        bd128 = bd_ref[0:SW_KV_WIDTH, 0:SW_KV_WIDTH]
        z = proj(0, 512)
        zq_ref[:, 0:512] = z.astype(BF16)
        qa_ref[...] = (headnorm(z, gqa_ref[...], bd512) * SCALE).astype(BF16)
        z = proj(512, 1024)
        zq_ref[:, 512:1024] = z.astype(BF16)
        ka_ref[...] = headnorm(z, gka_ref[...], bd512).astype(BF16)
        z = proj(1024, 1536)
        zq_ref[:, 1024:1536] = z.astype(BF16)
        z = proj(1536, 2048)
        zq_ref[:, 1536:2048] = z.astype(BF16)
        qs_ref[...] = (headnorm(z, gqs_ref[...], bd512) * SCALE).astype(BF16)
        z = proj(2048, 2176)
        zq_ref[:, 2048:2176] = z.astype(BF16)
        ks_ref[...] = headnorm(z, gks_ref[...], bd128).astype(BF16)
        z = proj(2176, 2304)
        zq_ref[:, 2176:2304] = z.astype(BF16)
        for c0 in range(0, 2 * d, gc):
            zg = proj(QKV_WIDTH + c0, QKV_WIDTH + c0 + gc) + b_ref[:, c0:c0 + gc]
            gt_ref[:, c0:c0 + gc] = _sigmoid(zg).astype(BF16)

    return pl.pallas_call(
        body, name=name, grid=(s // tm,),
        in_specs=[_rows(tm, d), _full((1, d)), _mat(*win_t), _full((1, 2 * d)),
                  _full((1, 512)), _full((1, 512)), _full((1, 512)), _full((1, 128)), _full((512, 512))],
        out_specs=[_rows(tm, d), _rows(tm, QKV_WIDTH), _rows(tm, 512), _rows(tm, 512), _rows(tm, 512),
                   _rows(tm, 128), _rows(tm, 2 * d)],
        out_shape=[jax.ShapeDtypeStruct((s, d), BF16), jax.ShapeDtypeStruct((s, QKV_WIDTH), BF16),
                   jax.ShapeDtypeStruct((s, 512), BF16), jax.ShapeDtypeStruct((s, 512), BF16),
                   jax.ShapeDtypeStruct((s, 512), BF16), jax.ShapeDtypeStruct((s, 128), BF16),
                   jax.ShapeDtypeStruct((s, 2 * d), BF16)],
        compiler_params=_params(),
    )(x, gain, win_t[0], b_gate, gq_na, gk_na, gq_sw, gk_sw, bd)


def _na_iotas():
    qc = lax.broadcasted_iota(jnp.int32, (GRID_W, LANES), 0)
    ln = lax.broadcasted_iota(jnp.int32, (GRID_W, LANES), 1)
    low = ln < GRID_W
    kc = jnp.where(low, ln, ln - GRID_W)
    diff = kc - qc + (NA_COLS - 1)
    qcs = jnp.clip(qc - NA_COLS // 2, 0, GRID_W - NA_COLS)
    inwin = (kc >= qcs) & (kc < qcs + NA_COLS)
    return diff, low, inwin


NA_RI = 2 * NA_ROWS - 1
NA_CI = 2 * NA_COLS - 1
NA_T2 = NA_RI + 1


def rpb_expand(rpb_flat, n_heads, name):
    def body(rpb_ref, o_ref):
        diff, low, _ = _na_iotas()
        for h in range(n_heads):
            def one(e, carry, h=h):
                lo_row = jnp.maximum(e - 1, 0)
                hi_row = jnp.minimum(e, NA_RI - 1)
                lo_on = jnp.where(e >= 1, 1.0, 0.0)
                hi_on = jnp.where(e <= NA_RI - 1, 1.0, 0.0)
                t = jnp.zeros((GRID_W, LANES), F32)
                for c in range(NA_CI):
                    lo = rpb_ref[h * NA_RI * NA_CI + lo_row * NA_CI + c] * lo_on
                    hi = rpb_ref[h * NA_RI * NA_CI + hi_row * NA_CI + c] * hi_on
                    t = jnp.where(diff == c, jnp.where(low, lo, hi), t)
                o_ref[h, e] = t
                return carry
            lax.fori_loop(0, NA_T2, one, 0)

    return pl.pallas_call(
        body, name=name,
        in_specs=[pl.BlockSpec(memory_space=pltpu.SMEM)],
        out_specs=pl.BlockSpec(memory_space=pltpu.VMEM),
        out_shape=jax.ShapeDtypeStruct((n_heads, NA_T2, GRID_W, LANES), F32),
        compiler_params=pltpu.CompilerParams(vmem_limit_bytes=V7X_VMEM_LIMIT),
    )(rpb_flat)


def rpb_reduce(dt2, name):
    n_heads = dt2.shape[0]

    def body(d_ref, o_ref):
        diff, low, _ = _na_iotas()
        low32 = lax.broadcasted_iota(jnp.int32, (32, LANES), 1) < GRID_W
        o_ref[...] = jnp.zeros(o_ref.shape, F32)
        for h in range(n_heads):
            def one(e, carry, h=h):
                dv = d_ref[h, e]
                rows = [jnp.sum(jnp.where(diff == c, dv, 0.0), axis=0, keepdims=True) for c in range(NA_CI)]
                rows.append(jnp.zeros((1, LANES), F32))
                r = jnp.concatenate(rows, axis=0)
                lo = jnp.sum(jnp.where(low32, r, 0.0), axis=1, keepdims=True)
                hi = jnp.sum(jnp.where(low32, 0.0, r), axis=1, keepdims=True)
                lo_row = jnp.maximum(e - 1, 0)
                hi_row = jnp.minimum(e, NA_RI - 1)
                o_ref[h, lo_row] = o_ref[h, lo_row] + jnp.broadcast_to(lo, (32, LANES))
                o_ref[h, hi_row] = o_ref[h, hi_row] + jnp.broadcast_to(hi, (32, LANES))
                return carry
            lax.fori_loop(0, NA_T2, one, 0)

    return pl.pallas_call(
        body, name=name,
        in_specs=[pl.BlockSpec(memory_space=pltpu.VMEM)],
        out_specs=pl.BlockSpec(memory_space=pltpu.VMEM),
        out_shape=jax.ShapeDtypeStruct((n_heads, NA_RI, 32, LANES), F32),
        compiler_params=pltpu.CompilerParams(vmem_limit_bytes=V7X_VMEM_LIMIT),
    )(dt2)


NA_TQ = 4
NA_TK = NA_TQ + NA_ROWS
NA_KCH = NA_TK // 2


def _na_tile_geometry(t, rows):
    r = t * NA_TQ
    kbase = jnp.clip(r - NA_ROWS // 2, 0, rows - NA_TK)
    starts = [jnp.clip(r + a - NA_ROWS // 2, 0, rows - NA_ROWS) for a in range(NA_TQ)]
    return r, kbase, starts


def _na_tile_mask(kbase, starts, low, inwin):
    half = jnp.where(low, 0, 1)
    cols = []
    for c in range(NA_KCH):
        krow = kbase + 2 * c + half
        cols.append(jnp.concatenate(
            [jnp.where(inwin & (krow >= st) & (krow < st + NA_ROWS), 0.0, NEG) for st in starts], axis=0))
    return jnp.concatenate(cols, axis=1)


def _na_tile_index(r, kbase, a, c):
    return jnp.clip(kbase + 2 * c - (r + a) + NA_ROWS, 0, NA_T2 - 1)


def _na_tile_probs(q, k, t2_ref, hh, r, kbase, madd):
    bias = jnp.concatenate(
        [jnp.concatenate([t2_ref[hh, _na_tile_index(r, kbase, a, c)] for a in range(NA_TQ)], axis=0)
         for c in range(NA_KCH)], axis=1)
    sc = _dotg(q, k, NT) + bias + madd
    e = jnp.exp(sc - jnp.max(sc, axis=1, keepdims=True))
    return e * (1.0 / jnp.sum(e, axis=1, keepdims=True))


def na_fwd(qa, ka, zq, t2, name):
    s = qa.shape[0]
    rows = s // GRID_W
    n_pairs = NA_WIDTH // LANES
    v_blk0 = (2 * NA_WIDTH) // LANES

    assert rows % NA_TQ == 0 and rows >= NA_TK
    tq, tk = NA_TQ * GRID_W, NA_TK * GRID_W

    def body(q_ref, k_ref, v_ref, t2_ref, o_ref):
        _, low, inwin = _na_iotas()

        def tile(t, carry):
            r, kbase, starts = _na_tile_geometry(t, rows)
            madd = _na_tile_mask(kbase, starts, low, inwin)
            qr = pl.ds(pl.multiple_of(r * GRID_W, tq), tq)
            kr = pl.ds(pl.multiple_of(kbase * GRID_W, tq), tk)
            for hh in range(2):
                lanes = slice(HEAD_DIM * hh, HEAD_DIM * (hh + 1))
                p = _na_tile_probs(q_ref[qr, lanes], k_ref[kr, lanes], t2_ref, hh, r, kbase, madd)
                o_ref[qr, lanes] = _dot(p.astype(BF16), v_ref[kr, lanes]).astype(BF16)
            return carry

        lax.fori_loop(0, rows // NA_TQ, tile, 0)

    col = lambda off: pl.BlockSpec((s, LANES), lambda p, _o=off: (0, _o + p))
    return pl.pallas_call(
        body, name=name, grid=(n_pairs,),
        in_specs=[col(0), col(0), col(v_blk0),
                  pl.BlockSpec((2, NA_T2, GRID_W, LANES), lambda p: (p, 0, 0, 0))],
        out_specs=col(0),
        out_shape=jax.ShapeDtypeStruct((s, NA_WIDTH), BF16),
        compiler_params=_params(),
    )(qa, ka, zq, t2)


def na_bwd(qa, ka, zq, t2, o_na, do_na, name):
    s = qa.shape[0]
    rows = s // GRID_W
    n_pairs = NA_WIDTH // LANES
    v_blk0 = (2 * NA_WIDTH) // LANES

    tq, tk = NA_TQ * GRID_W, NA_TK * GRID_W

    def body(q_ref, k_ref, v_ref, t2_ref, o_ref, do_ref, dq_ref, dk_ref, dv_ref, dt2_ref):
        _, low, inwin = _na_iotas()
        dk_ref[...] = jnp.zeros(dk_ref.shape, F32)
        dv_ref[...] = jnp.zeros(dv_ref.shape, F32)
        dt2_ref[...] = jnp.zeros(dt2_ref.shape, F32)

        def tile(t, carry):
            r, kbase, starts = _na_tile_geometry(t, rows)
            madd = _na_tile_mask(kbase, starts, low, inwin)
            qr = pl.ds(pl.multiple_of(r * GRID_W, tq), tq)
            kr = pl.ds(pl.multiple_of(kbase * GRID_W, tq), tk)
            for hh in range(2):
                lanes = slice(HEAD_DIM * hh, HEAD_DIM * (hh + 1))
                q, k, v = q_ref[qr, lanes], k_ref[kr, lanes], v_ref[kr, lanes]
                p = _na_tile_probs(q, k, t2_ref, hh, r, kbase, madd)
                do = do_ref[qr, lanes]
                delta = jnp.sum(do.astype(F32) * o_ref[qr, lanes].astype(F32), axis=1, keepdims=True)
                ds = p * (_dotg(do, v, NT) - delta)
                for a in range(NA_TQ):
                    for c in range(NA_KCH):
                        e = _na_tile_index(r, kbase, a, c)
                        dt2_ref[hh, e] = dt2_ref[hh, e] + ds[GRID_W * a:GRID_W * (a + 1), LANES * c:LANES * (c + 1)]
                dsb = ds.astype(BF16)
                dq_ref[qr, lanes] = _dot(dsb, k)
                dk_ref[kr, lanes] = dk_ref[kr, lanes] + _dotg(dsb, q, TN)
                dv_ref[kr, lanes] = dv_ref[kr, lanes] + _dotg(p.astype(BF16), do, TN)
            return carry

        lax.fori_loop(0, rows // NA_TQ, tile, 0)

    col = lambda off: pl.BlockSpec((s, LANES), lambda p, _o=off: (0, _o + p))
    t2spec = pl.BlockSpec((2, NA_T2, GRID_W, LANES), lambda p: (p, 0, 0, 0))
    return pl.pallas_call(
        body, name=name, grid=(n_pairs,),
        in_specs=[col(0), col(0), col(v_blk0), t2spec, col(0), col(0)],
        out_specs=[col(0), col(0), col(0), t2spec],
        out_shape=[jax.ShapeDtypeStruct((s, NA_WIDTH), F32)] * 3 + [jax.ShapeDtypeStruct(t2.shape, F32)],
        compiler_params=_params(),
    )(qa, ka, zq, t2, o_na, do_na)


def _t5_bucket_map():
    rel = np.arange(3 * SW_BLOCK)[None, :] - SW_BLOCK - np.arange(SW_BLOCK)[:, None]
    nb = REL_BUCKETS // 2
    max_exact = nb // 2
    n = np.abs(rel)
    large = max_exact + (np.log(np.maximum(n, 1) / max_exact)
                         / np.log(REL_MAX_DIST / max_exact) * (nb - max_exact)).astype(np.int32)
    large = np.minimum(large, nb - 1)
    return ((rel > 0) * nb + np.where(n < max_exact, n, large)).astype(np.int32)


def t5_expand(table, bmap, name):
    def body(tab_ref, bm_ref, o_ref):
        bm = bm_ref[...]
        for h in range(SW_HEADS):
            t = jnp.zeros(bm.shape, F32)
            for b in range(REL_BUCKETS):
                t = jnp.where(bm == b, tab_ref[b, h], t)
            o_ref[h] = t

    return pl.pallas_call(
        body, name=name,
        in_specs=[pl.BlockSpec(memory_space=pltpu.SMEM), pl.BlockSpec(memory_space=pltpu.VMEM)],
        out_specs=pl.BlockSpec(memory_space=pltpu.VMEM),
        out_shape=jax.ShapeDtypeStruct((SW_HEADS,) + bmap.shape, F32),
        compiler_params=pltpu.CompilerParams(vmem_limit_bytes=V7X_VMEM_LIMIT),
    )(table, bmap)


def t5_reduce(dbias_list, bmap, name):
    n = len(dbias_list)

    def body(*refs):
        d_refs, bm_ref, o_ref = refs[:n], refs[n], refs[n + 1]
        bm = bm_ref[...]
        for h in range(SW_HEADS):
            dv = d_refs[0][h]
            for other in d_refs[1:]:
                dv = dv + other[h]
            rows = [jnp.sum(jnp.where(bm == b, dv, 0.0), axis=0, keepdims=True) for b in range(REL_BUCKETS)]
            r = jnp.concatenate(rows, axis=0)
            o_ref[h] = jnp.broadcast_to(jnp.sum(r, axis=1, keepdims=True), (REL_BUCKETS, LANES))

    return pl.pallas_call(
        body, name=name,
        in_specs=[pl.BlockSpec(memory_space=pltpu.VMEM)] * (n + 1),
        out_specs=pl.BlockSpec(memory_space=pltpu.VMEM),
        out_shape=jax.ShapeDtypeStruct((SW_HEADS, REL_BUCKETS, LANES), F32),
        compiler_params=pltpu.CompilerParams(vmem_limit_bytes=V7X_VMEM_LIMIT),
    )(*dbias_list, bmap)


def _sw_mask_iotas():
    a = lax.broadcasted_iota(jnp.int32, (SW_BLOCK, 3 * SW_BLOCK), 0)
    j = lax.broadcasted_iota(jnp.int32, (SW_BLOCK, 3 * SW_BLOCK), 1)
    inwin = jnp.abs(j - SW_BLOCK - a) <= SW_BLOCK
    return j, inwin


def _sw_probs(q, k, bias, madd, sk):
    sc = _dotg(q, k, NT) + bias + madd
    m = jnp.maximum(jnp.max(sc, axis=1, keepdims=True), sk)
    e = jnp.exp(sc - m)
    es = jnp.exp(sk - m)
    inv = 1.0 / (jnp.sum(e, axis=1, keepdims=True) + es)
    return e * inv, es * inv


def sw_fwd(qs, ks, zq, t5b, sink, name):
    s = qs.shape[0]
    nb = s // SW_BLOCK
    v_blk = (3 * NA_WIDTH + SW_Q_WIDTH + SW_KV_WIDTH) // LANES
    pad = s + 2 * SW_BLOCK

    def body(q_ref, k_ref, v_ref, b_ref, sink_ref, o_ref, kp, vp):
        zeros = jnp.zeros((SW_BLOCK, SW_KV_WIDTH), BF16)
        kp[0:SW_BLOCK, :] = zeros
        vp[0:SW_BLOCK, :] = zeros
        kp[SW_BLOCK + s:pad, :] = zeros
        vp[SW_BLOCK + s:pad, :] = zeros
        kp[SW_BLOCK:SW_BLOCK + s, :] = k_ref[...]
        vp[SW_BLOCK:SW_BLOCK + s, :] = v_ref[...]
        j, inwin = _sw_mask_iotas()

        def blk(n, carry):
            kpos = n * SW_BLOCK - SW_BLOCK + j
            madd = jnp.where(inwin & (kpos >= 0) & (kpos < s), 0.0, NEG)
            q0 = pl.multiple_of(n * SW_BLOCK, SW_BLOCK)
            for h in range(SW_HEADS):
                g = h // SW_REP
                q = q_ref[pl.ds(q0, SW_BLOCK), HEAD_DIM * h:HEAD_DIM * (h + 1)]
                k = kp[pl.ds(q0, 3 * SW_BLOCK), HEAD_DIM * g:HEAD_DIM * (g + 1)]
                v = vp[pl.ds(q0, 3 * SW_BLOCK), HEAD_DIM * g:HEAD_DIM * (g + 1)]
                p, _ = _sw_probs(q, k, b_ref[h], madd, sink_ref[h])
                o_ref[pl.ds(q0, SW_BLOCK), HEAD_DIM * h:HEAD_DIM * (h + 1)] = _dot(p.astype(BF16), v).astype(BF16)
            return carry

        lax.fori_loop(0, nb, blk, 0)

    return pl.pallas_call(
        body, name=name, grid=(1,),
        in_specs=[_full((s, SW_Q_WIDTH)), _full((s, SW_KV_WIDTH)),
                  pl.BlockSpec((s, SW_KV_WIDTH), lambda i: (0, v_blk)),
                  _full((SW_HEADS, SW_BLOCK, 3 * SW_BLOCK)), pl.BlockSpec(memory_space=pltpu.SMEM)],
        out_specs=_full((s, SW_Q_WIDTH)),
        out_shape=jax.ShapeDtypeStruct((s, SW_Q_WIDTH), BF16),
        scratch_shapes=[pltpu.VMEM((pad, SW_KV_WIDTH), BF16), pltpu.VMEM((pad, SW_KV_WIDTH), BF16)],
        compiler_params=_params(),
    )(qs, ks, zq, t5b, sink)


def sw_bwd(qs, ks, zq, t5b, sink, o_sw, do_sw, name):
    s = qs.shape[0]
    nb = s // SW_BLOCK
    v_blk = (3 * NA_WIDTH + SW_Q_WIDTH + SW_KV_WIDTH) // LANES
    pad = s + 2 * SW_BLOCK

    def body(q_ref, k_ref, v_ref, b_ref, sink_ref, o_ref, do_ref,
             dq_ref, dk_ref, dv_ref, db_ref, dsk_ref, kp, vp, dkp, dvp):
        zeros = jnp.zeros((SW_BLOCK, SW_KV_WIDTH), BF16)
        kp[0:SW_BLOCK, :] = zeros
        vp[0:SW_BLOCK, :] = zeros
        kp[SW_BLOCK + s:pad, :] = zeros
        vp[SW_BLOCK + s:pad, :] = zeros
        kp[SW_BLOCK:SW_BLOCK + s, :] = k_ref[...]
        vp[SW_BLOCK:SW_BLOCK + s, :] = v_ref[...]
        dkp[...] = jnp.zeros(dkp.shape, F32)
        dvp[...] = jnp.zeros(dvp.shape, F32)
        db_ref[...] = jnp.zeros(db_ref.shape, F32)
        dsk_ref[...] = jnp.zeros(dsk_ref.shape, F32)
        j, inwin = _sw_mask_iotas()

        def blk(n, carry):
            kpos = n * SW_BLOCK - SW_BLOCK + j
            madd = jnp.where(inwin & (kpos >= 0) & (kpos < s), 0.0, NEG)
            q0 = pl.multiple_of(n * SW_BLOCK, SW_BLOCK)
            for g in range(SW_HEADS // SW_REP):
                kl = slice(HEAD_DIM * g, HEAD_DIM * (g + 1))
                k = kp[pl.ds(q0, 3 * SW_BLOCK), kl]
                v = vp[pl.ds(q0, 3 * SW_BLOCK), kl]
                dkw = jnp.zeros((3 * SW_BLOCK, HEAD_DIM), F32)
                dvw = jnp.zeros((3 * SW_BLOCK, HEAD_DIM), F32)
                for r in range(SW_REP):
                    h = g * SW_REP + r
                    hl = slice(HEAD_DIM * h, HEAD_DIM * (h + 1))
                    q = q_ref[pl.ds(q0, SW_BLOCK), hl]
                    p, ps = _sw_probs(q, k, b_ref[h], madd, sink_ref[h])
                    do = do_ref[pl.ds(q0, SW_BLOCK), hl]
                    ov = o_ref[pl.ds(q0, SW_BLOCK), hl]
                    delta = jnp.sum(do.astype(F32) * ov.astype(F32), axis=1, keepdims=True)
                    ds = p * (_dotg(do, v, NT) - delta)
                    db_ref[h] = db_ref[h] + ds
                    dsk_ref[h] = dsk_ref[h] - jnp.broadcast_to(ps * delta, (SW_BLOCK, LANES))
                    dsb = ds.astype(BF16)
                    dq_ref[pl.ds(q0, SW_BLOCK), hl] = _dot(dsb, k)
                    dkw = dkw + _dotg(dsb, q, TN)
                    dvw = dvw + _dotg(p.astype(BF16), do, TN)
                dkp[pl.ds(q0, 3 * SW_BLOCK), kl] = dkp[pl.ds(q0, 3 * SW_BLOCK), kl] + dkw
                dvp[pl.ds(q0, 3 * SW_BLOCK), kl] = dvp[pl.ds(q0, 3 * SW_BLOCK), kl] + dvw
            return carry

        lax.fori_loop(0, nb, blk, 0)
        dk_ref[...] = dkp[SW_BLOCK:SW_BLOCK + s, :]
        dv_ref[...] = dvp[SW_BLOCK:SW_BLOCK + s, :]

    bias_spec = _full((SW_HEADS, SW_BLOCK, 3 * SW_BLOCK))
    return pl.pallas_call(
        body, name=name, grid=(1,),
        in_specs=[_full((s, SW_Q_WIDTH)), _full((s, SW_KV_WIDTH)),
                  pl.BlockSpec((s, SW_KV_WIDTH), lambda i: (0, v_blk)),
                  bias_spec, pl.BlockSpec(memory_space=pltpu.SMEM),
                  _full((s, SW_Q_WIDTH)), _full((s, SW_Q_WIDTH))],
        out_specs=[_full((s, SW_Q_WIDTH)), _full((s, SW_KV_WIDTH)), _full((s, SW_KV_WIDTH)), bias_spec,
                   _full((SW_HEADS, SW_BLOCK, LANES))],
        out_shape=[jax.ShapeDtypeStruct((s, SW_Q_WIDTH), F32), jax.ShapeDtypeStruct((s, SW_KV_WIDTH), F32),
                   jax.ShapeDtypeStruct((s, SW_KV_WIDTH), F32),
                   jax.ShapeDtypeStruct((SW_HEADS, SW_BLOCK, 3 * SW_BLOCK), F32),
                   jax.ShapeDtypeStruct((SW_HEADS, SW_BLOCK, LANES), F32)],
        scratch_shapes=[pltpu.VMEM((pad, SW_KV_WIDTH), BF16), pltpu.VMEM((pad, SW_KV_WIDTH), BF16),
                        pltpu.VMEM((pad, SW_KV_WIDTH), F32), pltpu.VMEM((pad, SW_KV_WIDTH), F32)],
        compiler_params=_params(),
    )(qs, ks, zq, t5b, sink, o_sw, do_sw)


def merge_out(x, o_na, o_sw, gt, wbna_t, wbsw_t, wout, name):
    s, d = x.shape
    tm = _row_tile(s)

    def body(x_ref, ona_ref, osw_ref, gt_ref, wna_ref, wsw_ref, wo_ref, xo_ref, ana_ref, asw_ref, mg_ref):
        a_na = _dotg(ona_ref[...], wna_ref[...], NT)
        a_sw = _dotg(osw_ref[...], wsw_ref[...], NT)
        ana_ref[...] = a_na.astype(BF16)
        asw_ref[...] = a_sw.astype(BF16)
        merged = (gt_ref[:, 0:d].astype(F32) * a_na + gt_ref[:, d:2 * d].astype(F32) * a_sw).astype(BF16)
        mg_ref[...] = merged
        xo_ref[...] = x_ref[...] + _dot(merged, wo_ref[...])

    return pl.pallas_call(
        body, name=name, grid=(s // tm,),
        in_specs=[_rows(tm, d), _rows(tm, 512), _rows(tm, 512), _rows(tm, 2 * d),
                  _mat(*wbna_t), _mat(*wbsw_t), _mat(*wout)],
        out_specs=[_rows(tm, d)] * 4,
        out_shape=[jax.ShapeDtypeStruct((s, d), F32)] + [jax.ShapeDtypeStruct((s, d), BF16)] * 3,
        compiler_params=_params(),
    )(x, o_na, o_sw, gt, wbna_t[0], wbsw_t[0], wout[0])


def mix_bwd_out(dx, gt, a_na, a_sw, wbna_t, wbsw_t, wout, name):
    s, d = dx.shape
    tm = _row_tile(s)

    def body(dx_ref, gt_ref, ana_ref, asw_ref, wna_ref, wsw_ref, wo_ref,
             dxb_ref, dzg_ref, dana_ref, dasw_ref, dona_ref, dosw_ref, dbg_ref):
        @pl.when(pl.program_id(0) == 0)
        def _():
            dbg_ref[...] = jnp.zeros(dbg_ref.shape, F32)

        dxb = dx_ref[...].astype(BF16)
        dxb_ref[...] = dxb
        dm = _dotg(dxb, wo_ref[...], NT)
        for i, (a_ref, da_ref, w_ref, do_ref) in enumerate(
                [(ana_ref, dana_ref, wna_ref, dona_ref), (asw_ref, dasw_ref, wsw_ref, dosw_ref)]):
            gi = gt_ref[:, i * d:(i + 1) * d].astype(F32)
            da = (dm * gi).astype(BF16)
            da_ref[...] = da
            do_ref[...] = _dot(da, w_ref[...]).astype(BF16)
            dzg = dm * a_ref[...].astype(F32) * gi * (1.0 - gi)
            dzg_ref[:, i * d:(i + 1) * d] = dzg.astype(BF16)
            dbg_ref[:, i * d:(i + 1) * d] = dbg_ref[:, i * d:(i + 1) * d] + jnp.sum(dzg, axis=0, keepdims=True)

    return pl.pallas_call(
        body, name=name, grid=(s // tm,),
        in_specs=[_rows(tm, d), _rows(tm, 2 * d), _rows(tm, d), _rows(tm, d),
                  _mat(*wbna_t), _mat(*wbsw_t), _mat(*wout)],
        out_specs=[_rows(tm, d), _rows(tm, 2 * d), _rows(tm, d), _rows(tm, d), _rows(tm, 512), _rows(tm, 512),
                   _full((1, 2 * d))],
        out_shape=[jax.ShapeDtypeStruct((s, d), BF16), jax.ShapeDtypeStruct((s, 2 * d), BF16),
                   jax.ShapeDtypeStruct((s, d), BF16), jax.ShapeDtypeStruct((s, d), BF16),
                   jax.ShapeDtypeStruct((s, 512), BF16), jax.ShapeDtypeStruct((s, 512), BF16),
                   jax.ShapeDtypeStruct((1, 2 * d), F32)],
        compiler_params=_params(),
    )(dx, gt, a_na, a_sw, wbna_t[0], wbsw_t[0], wout[0])


def qk_norm_bwd(dqa, dka, dva, dqs, dks, dvs, zq, dzg, gq_na, gk_na, gq_sw, gk_sw, bd, name):
    s = zq.shape[0]
    d2 = dzg.shape[1]
    n_in = QKV_WIDTH + d2
    tm = _row_tile(s)

    def body(dqa_ref, dka_ref, dva_ref, dqs_ref, dks_ref, dvs_ref, zq_ref, dzg_ref,
             gqa_ref, gka_ref, gqs_ref, gks_ref, bd_ref, dz_ref, dgqa_ref, dgka_ref, dgqs_ref, dgks_ref):
        @pl.when(pl.program_id(0) == 0)
        def _():
            for r in (dgqa_ref, dgka_ref, dgqs_ref, dgks_ref):
                r[...] = jnp.zeros(r.shape, F32)

        bd512 = bd_ref[...]
        bd128 = bd_ref[0:SW_KV_WIDTH, 0:SW_KV_WIDTH]

        def one(c0, c1, dy_ref, g_ref, dg_ref, bdm, scale):
            z = zq_ref[:, c0:c1].astype(F32)
            r = lax.rsqrt(_group_mean(z * z, bdm) + EPS)
            zh = z * r
            dy = dy_ref[...] * scale
            dyg = dy * g_ref[...]
            dz = r * (dyg - zh * _group_mean(dyg * zh, bdm))
            dz_ref[:, c0:c1] = dz.astype(BF16)
            dg_ref[...] = dg_ref[...] + jnp.sum(dy * zh, axis=0, keepdims=True)

        one(0, 512, dqa_ref, gqa_ref, dgqa_ref, bd512, SCALE)
        one(512, 1024, dka_ref, gka_ref, dgka_ref, bd512, 1.0)
        dz_ref[:, 1024:1536] = dva_ref[...].astype(BF16)
        one(1536, 2048, dqs_ref, gqs_ref, dgqs_ref, bd512, SCALE)
        one(2048, 2176, dks_ref, gks_ref, dgks_ref, bd128, 1.0)
        dz_ref[:, 2176:2304] = dvs_ref[...].astype(BF16)
        dz_ref[:, QKV_WIDTH:n_in] = dzg_ref[...]

    return pl.pallas_call(
        body, name=name, grid=(s // tm,),
        in_specs=[_rows(tm, 512), _rows(tm, 512), _rows(tm, 512), _rows(tm, 512), _rows(tm, 128), _rows(tm, 128),
                  _rows(tm, QKV_WIDTH), _rows(tm, d2),
                  _full((1, 512)), _full((1, 512)), _full((1, 512)), _full((1, 128)), _full((512, 512))],
        out_specs=[_rows(tm, n_in), _full((1, 512)), _full((1, 512)), _full((1, 512)), _full((1, 128))],
        out_shape=[jax.ShapeDtypeStruct((s, n_in), BF16)] + [jax.ShapeDtypeStruct((1, 512), F32)] * 3
                  + [jax.ShapeDtypeStruct((1, 128), F32)],
        compiler_params=_params(),
    )(dqa, dka, dva, dqs, dks, dvs, zq, dzg, gq_na, gk_na, gq_sw, gk_sw, bd)


def ffn_bwd_act(dx, wd, hg, hu, name):
    s, d = dx.shape
    f = wd[0].shape[1]
    tm = _row_tile(s)
    fc = _col_chunk(f)

    def body(dx_ref, w_ref, hg_ref, hu_ref, dxb_ref, dhg_ref, dhu_ref):
        dxb = dx_ref[...].astype(BF16)
        dxb_ref[...] = dxb
        for c0 in range(0, f, fc):
            dact = 0.5 * _dotg(dxb, w_ref[c0:c0 + fc, :], NT)
            hg = hg_ref[:, c0:c0 + fc].astype(F32)
            hu = hu_ref[:, c0:c0 + fc].astype(F32)
            sg = _sigmoid(hg)
            dhu_ref[:, c0:c0 + fc] = (dact * hg * sg).astype(BF16)
            dhg_ref[:, c0:c0 + fc] = (dact * hu * sg * (1.0 + hg * (1.0 - sg))).astype(BF16)

    return pl.pallas_call(
        body, name=name, grid=(s // tm,),
        in_specs=[_rows(tm, d), _mat(*wd), _rows(tm, f), _rows(tm, f)],
        out_specs=[_rows(tm, d), _rows(tm, f), _rows(tm, f)],
        out_shape=[jax.ShapeDtypeStruct((s, d), BF16), jax.ShapeDtypeStruct((s, f), BF16),
                   jax.ShapeDtypeStruct((s, f), BF16)],
        compiler_params=_params(),
    )(dx, wd[0], hg, hu)


def proj_bwd_norm(acts, weights, x, gain, dx, name):
    s, d = x.shape
    tm = _row_tile(s)
    n = len(acts)

    def body(*refs):
        a_refs, w_refs = refs[:n], refs[n:2 * n]
        x_ref, g_ref, dx_ref, o_ref, dg_ref = refs[2 * n:]

        @pl.when(pl.program_id(0) == 0)
        def _():
            dg_ref[...] = jnp.zeros(dg_ref.shape, F32)

        dxn = _dot(a_refs[0][...], w_refs[0][...])
        for a_ref, w_ref in zip(a_refs[1:], w_refs[1:]):
            dxn = dxn + _dot(a_ref[...], w_ref[...])
        xv = x_ref[...]
        r = _rstd(xv)
        xh = xv * r
        dxh = dxn * g_ref[...]
        o_ref[...] = dx_ref[...] + r * (dxh - xh * jnp.mean(dxh * xh, axis=-1, keepdims=True))
        dg_ref[...] = dg_ref[...] + jnp.sum(dxn * xh, axis=0, keepdims=True)

    return pl.pallas_call(
        body, name=name, grid=(s // tm,),
        in_specs=[_rows(tm, a.shape[1]) for a in acts] + [_mat(*w) for w in weights]
                 + [_rows(tm, d), _full((1, d)), _rows(tm, d)],
        out_specs=[_rows(tm, d), _full((1, d))],
        out_shape=[jax.ShapeDtypeStruct((s, d), F32), jax.ShapeDtypeStruct((1, d), F32)],
        compiler_params=_params(),
    )(*acts, *[w[0] for w in weights], x, gain, dx)


def tn_matmul(a, b, scale, name):
    s, n = a.shape
    k = b.shape[1]
    tn = _col_chunk(n)

    def body(a_ref, b_ref, o_ref):
        o_ref[...] = (scale * _dotg(a_ref[...], b_ref[...], TN)).astype(BF16)

    return pl.pallas_call(
        body, name=name, grid=(n // tn,),
        in_specs=[pl.BlockSpec((s, tn), lambda i: (0, i)), _full((s, k))],
        out_specs=pl.BlockSpec((tn, k), lambda i: (i, 0)),
        out_shape=jax.ShapeDtypeStruct((n, k), BF16),
        compiler_params=_params(),
    )(a, b)


def loss_grad(y, target, name):
    s, d = y.shape
    tm = _row_tile(s)

    def body(y_ref, t_ref, dy_ref, acc_ref):
        @pl.when(pl.program_id(0) == 0)
        def _():
            acc_ref[...] = jnp.zeros(acc_ref.shape, F32)

        err = y_ref[...] - t_ref[...]
        dy_ref[...] = err * (1.0 / d)
        e2 = err * err
        part = jnp.sum(e2.reshape(tm // 8, 8, d), axis=0)
        acc = part[:, 0:LANES]
        for c0 in range(LANES, d, LANES):
            acc = acc + part[:, c0:c0 + LANES]
        acc_ref[...] = acc_ref[...] + acc

    return pl.pallas_call(
        body, name=name, grid=(s // tm,),
        in_specs=[_rows(tm, d), _rows(tm, d)],
        out_specs=[_rows(tm, d), _full((8, LANES))],
        out_shape=[jax.ShapeDtypeStruct((s, d), F32), jax.ShapeDtypeStruct((8, LANES), F32)],
        compiler_params=_params(),
    )(y, target)


def _mesh_pos():
    return lax.axis_index("x"), lax.axis_index("y"), lax.axis_index("c")


def gather_weights(shards):
    n = len(shards)

    def body(*refs):
        ins, outs = refs[:n], refs[n:2 * n]
        send_sems, recv_sems, local_sems = refs[2 * n:]
        x, y, c = _mesh_pos()
        me, sibling = (x, y, c), (x, y, 1 - c)
        chips = [(1 - x, y), (x, 1 - y), (1 - x, 1 - y)]

        def slot(a, px, py, pc):
            return outs[a].at[:, 4 * px + 2 * py + pc]

        def copy(a, k, block, to, src=None):
            return pltpu.make_async_remote_copy(
                src_ref=slot(a, *block) if src is None else src, dst_ref=slot(a, *block),
                send_sem=send_sems.at[a, k], recv_sem=recv_sems.at[a, k], device_id=to, device_id_type=MESH)

        mine = [pltpu.make_async_copy(ins[a], slot(a, *me), local_sems.at[a]) for a in range(n)]
        for cp in mine:
            cp.start()
        first = []
        for a in range(n):
            first.append(copy(a, 0, me, sibling, src=ins[a]))
            first += [copy(a, 1 + j, me, (*chip, c), src=ins[a]) for j, chip in enumerate(chips)]
        for cp in first:
            cp.start()
        passed = []
        for j, chip in enumerate(chips):
            for a in range(n):
                copy(a, 1 + j, (*chip, c), me).wait_recv()
                fwd = copy(a, 4 + j, (*chip, c), sibling)
                fwd.start()
                passed.append(fwd)
        for a in range(n):
            copy(a, 0, sibling, me).wait_recv()
            for j, chip in enumerate(chips):
                copy(a, 4 + j, (*chip, 1 - c), me).wait_recv()
        for cp in first + passed:
            cp.wait_send()
        for cp in mine:
            cp.wait()

    any_spec = pl.BlockSpec(memory_space=pl.ANY)
    return pl.pallas_call(
        body, name="gather_weights",
        in_specs=[any_spec] * n, out_specs=[any_spec] * n,
        out_shape=[jax.ShapeDtypeStruct((w.shape[0], N_DEV) + w.shape[1:], w.dtype) for w in shards],
        scratch_shapes=[pltpu.SemaphoreType.DMA((n, 7)), pltpu.SemaphoreType.DMA((n, 7)),
                        pltpu.SemaphoreType.DMA((n,))],
        compiler_params=pltpu.CompilerParams(has_side_effects=True),
    )(*shards)


def _peers():
    x, y, c = _mesh_pos()
    peers = []
    for rel in range(1, N_DEV):
        peers.append((1 - x if rel & 4 else x, 1 - y if rel & 2 else y, 1 - c if rel & 1 else c))
    return 4 * x + 2 * y + c, peers


HBM_SPEC = pl.BlockSpec(memory_space=pltpu.HBM)
SEM_SPEC = pl.BlockSpec(memory_space=pltpu.SEMAPHORE)


def scatter_start(groups, name):
    n = len(groups)
    flat = [g for grp in groups for g in grp]
    nf = len(flat)
    offs = np.cumsum([0] + [len(grp) for grp in groups])
    lands = [lax.empty((N_DEV, len(grp)) + grp[0].shape[1:], grp[0].dtype) for grp in groups]

    def body(*refs):
        ins, zones = refs[:nf], refs[nf:nf + n]
        send_sems, recv_sems, local_sems = refs[nf + n:nf + n + 3]
        token = refs[-1]
        me, peers = _peers()
        for a in range(n):
            for w in range(len(groups[a])):
                pltpu.make_async_copy(ins[offs[a] + w].at[me], zones[a].at[me, w], local_sems.at[a]).start()
        for k, peer in enumerate(peers):
            p_id = 4 * peer[0] + 2 * peer[1] + peer[2]
            for a in range(n):
                for w in range(len(groups[a])):
                    pltpu.make_async_remote_copy(
                        src_ref=ins[offs[a] + w].at[p_id], dst_ref=zones[a].at[me, w],
                        send_sem=send_sems.at[7 * a + k], recv_sem=recv_sems.at[7 * a + k],
                        device_id=peer, device_id_type=MESH).start()
        token[...] = jnp.zeros(token.shape, F32)

    hbm = lambda t: pltpu.with_memory_space_constraint(t, pltpu.HBM)
    outs = pl.pallas_call(
        body, name=name,
        in_specs=[HBM_SPEC] * (nf + n),
        out_specs=[SEM_SPEC] * 3 + [HBM_SPEC] * (nf + n) + [pl.BlockSpec(memory_space=pltpu.VMEM)],
        out_shape=[pltpu.SemaphoreType.DMA((7 * n,)), pltpu.SemaphoreType.DMA((7 * n,)), pltpu.SemaphoreType.DMA((n,))]
                  + [pltpu.HBM(t.shape, t.dtype) for t in flat + lands]
                  + [jax.ShapeDtypeStruct((8, LANES), F32)],
        input_output_aliases={i: 3 + i for i in range(nf + n)},
        compiler_params=pltpu.CompilerParams(has_side_effects=pltpu.SideEffectType.DATAFLOW_SIDE_EFFECTING),
    )(*[hbm(t) for t in flat], *[hbm(t) for t in lands])
    sems, thru, token = outs[:3], outs[3:3 + nf + n], outs[-1]
    return (sems, thru, [len(grp) for grp in groups]), token


def scatter_wait(started, after, name):
    (send_sems, recv_sems, local_sems), thru, sizes = started
    n = len(sizes)
    nf = len(thru) - n

    def body(*refs):
        zones = refs[nf:nf + n]
        s_sems, r_sems, l_sems = refs[nf + n:nf + n + 3]
        me, peers = _peers()
        for a in range(n):
            for k, peer in enumerate(peers):
                cp = pltpu.make_async_remote_copy(
                    src_ref=zones[a].at[0], dst_ref=zones[a].at[0],
                    send_sem=s_sems.at[7 * a + k], recv_sem=r_sems.at[7 * a + k], device_id=peer,
                    device_id_type=MESH)
                cp.wait_send()
                cp.wait_recv()
            pltpu.make_async_copy(zones[a].at[0], zones[a].at[0], l_sems.at[a]).wait()

    outs = pl.pallas_call(
        body, name=name,
        in_specs=[HBM_SPEC] * (nf + n) + [SEM_SPEC] * 3 + [pl.BlockSpec(memory_space=pl.ANY)],
        out_specs=[HBM_SPEC] * (nf + n),
        out_shape=[pltpu.HBM(t.shape, t.dtype) for t in thru],
        input_output_aliases={i: i for i in range(nf + n)},
        compiler_params=pltpu.CompilerParams(has_side_effects=pltpu.SideEffectType.DATAFLOW_SIDE_EFFECTING),
    )(*thru, send_sems, recv_sems, local_sems, after)
    return outs[nf:]


def share_small(small):
    def body(s_ref, o_ref, send_sems, recv_sems, local_sem):
        me, peers = _peers()
        mine = pltpu.make_async_copy(s_ref, o_ref.at[me], local_sem)
        mine.start()
        copies = [pltpu.make_async_remote_copy(src_ref=s_ref, dst_ref=o_ref.at[me], send_sem=send_sems.at[k],
                                               recv_sem=recv_sems.at[k], device_id=peer, device_id_type=MESH)
                  for k, peer in enumerate(peers)]
        for cp in copies:
            cp.start()
        for cp in copies:
            cp.wait()
        mine.wait()

    vm = pl.BlockSpec(memory_space=pltpu.VMEM)
    return pl.pallas_call(
        body, name="share_small", in_specs=[vm], out_specs=vm,
        out_shape=jax.ShapeDtypeStruct((N_DEV,) + small.shape, small.dtype),
        scratch_shapes=[pltpu.SemaphoreType.DMA((7,)), pltpu.SemaphoreType.DMA((7,)), pltpu.SemaphoreType.DMA],
    )(small)


def sum_sources(recv, name):
    _, w, r, c = recv.shape

    def body(r_ref, o_ref):
        acc = r_ref[0, 0].astype(F32)
        for src in range(1, N_DEV):
            acc = acc + r_ref[src, 0].astype(F32)
        o_ref[0] = acc

    return pl.pallas_call(
        body, name=name, grid=(w,),
        in_specs=[pl.BlockSpec((N_DEV, 1, r, c), lambda i: (0, i, 0, 0))],
        out_specs=pl.BlockSpec((1, r, c), lambda i: (i, 0, 0)),
        out_shape=jax.ShapeDtypeStruct((w, r, c), F32),
        compiler_params=_params(),
    )(recv)


def _adamw_math(w, g, m, v):
    m = ADAM_B1 * m + (1.0 - ADAM_B1) * g
    v = ADAM_B2 * v + (1.0 - ADAM_B2) * (g * g)
    m_hat = m / (1.0 - ADAM_B1 ** ADAM_STEP)
    v_hat = v / (1.0 - ADAM_B2 ** ADAM_STEP)
    delta = -ADAM_LR * (m_hat / (jnp.sqrt(v_hat) + ADAM_EPS) + ADAM_WD * w)
    return delta, m, v


def adamw(w, g, m, v, name):
    shape = w.shape
    c = shape[-1]
    r = int(np.prod(shape[:-1]))
    w2, g2, m2, v2 = (t.reshape(r, c) for t in (w, g, m, v))
    tr = next(t for t in range(min(r, 512), 0, -1) if r % t == 0 and (t % 8 == 0 or t == r))

    def body(w_ref, g_ref, m_ref, v_ref, d_ref, mo_ref, vo_ref):
        d_ref[...], mo_ref[...], vo_ref[...] = _adamw_math(w_ref[...], g_ref[...], m_ref[...], v_ref[...])

    spec = pl.BlockSpec((tr, c), lambda i: (i, 0))
    outs = pl.pallas_call(
        body, name=name, grid=(r // tr,),
        in_specs=[spec] * 4, out_specs=[spec] * 3,
        out_shape=[jax.ShapeDtypeStruct((r, c), F32)] * 3,
        compiler_params=_params(),
    )(w2, g2, m2, v2)
    return tuple(t.reshape(shape) for t in outs)


def adamw_small(w, recv, m, v, name):
    def body(w_ref, r_ref, m_ref, v_ref, g_ref, d_ref, mo_ref, vo_ref):
        g = r_ref[0]
        for src in range(1, N_DEV):
            g = g + r_ref[src]
        g_ref[...] = g
        d_ref[...], mo_ref[...], vo_ref[...] = _adamw_math(w_ref[...], g, m_ref[...], v_ref[...])

    vm = pl.BlockSpec(memory_space=pltpu.VMEM)
    return pl.pallas_call(
        body, name=name, in_specs=[vm] * 4, out_specs=[vm] * 4,
        out_shape=[jax.ShapeDtypeStruct(w.shape, F32)] * 4,
        compiler_params=pltpu.CompilerParams(vmem_limit_bytes=V7X_VMEM_LIMIT),
    )(w, recv, m, v)


SMALL_NAMES = ("ffn1_norm", "mix_norm", "ffn2_norm", "b_gate", "na_q_norm", "na_k_norm", "sw_q_norm", "sw_k_norm",
               "na_rpb", "sw_sink", "t5_rel_table")


def _pack_small(parts):
    flat = jnp.concatenate([parts[k].reshape(-1).astype(F32) for k in SMALL_NAMES])
    n = flat.shape[0]
    rows = -(-n // (8 * LANES)) * 8
    return jnp.pad(flat, (0, rows * LANES - n)).reshape(rows, LANES)


def _unpack_small(packed, like):
    flat = packed.reshape(-1)
    out, off = {}, 0
    for k in SMALL_NAMES:
        n = int(np.prod(like[k].shape))
        out[k] = flat[off:off + n].reshape(like[k].shape)
        off += n
    return out


def kernel(x, ffn1_norm, ffn1_w_gate, ffn1_w_up, ffn1_w_down, mix_norm, w_in, b_gate, na_q_norm, na_k_norm, na_rpb, sw_q_norm, sw_k_norm, sw_sink, t5_rel_table, w_branch_na, w_branch_sw, w_out, ffn2_norm, ffn2_w_gate, ffn2_w_up, ffn2_w_down, loss_target, m_ffn1_norm, m_ffn1_w_gate, m_ffn1_w_up, m_ffn1_w_down, m_mix_norm, m_w_in, m_b_gate, m_na_q_norm, m_na_k_norm, m_na_rpb, m_sw_q_norm, m_sw_k_norm, m_sw_sink, m_t5_rel_table, m_w_branch_na, m_w_branch_sw, m_w_out, m_ffn2_norm, m_ffn2_w_gate, m_ffn2_w_up, m_ffn2_w_down, v_ffn1_norm, v_ffn1_w_gate, v_ffn1_w_up, v_ffn1_w_down, v_mix_norm, v_w_in, v_b_gate, v_na_q_norm, v_na_k_norm, v_na_rpb, v_sw_q_norm, v_sw_k_norm, v_sw_sink, v_t5_rel_table, v_w_branch_na, v_w_branch_sw, v_w_out, v_ffn2_norm, v_ffn2_w_gate, v_ffn2_w_up, v_ffn2_w_down):
    weights = dict(ffn1_norm=ffn1_norm, ffn1_w_gate=ffn1_w_gate, ffn1_w_up=ffn1_w_up, ffn1_w_down=ffn1_w_down,
                   mix_norm=mix_norm, w_in=w_in, b_gate=b_gate, na_q_norm=na_q_norm, na_k_norm=na_k_norm,
                   na_rpb=na_rpb, sw_q_norm=sw_q_norm, sw_k_norm=sw_k_norm, sw_sink=sw_sink,
                   t5_rel_table=t5_rel_table, w_branch_na=w_branch_na, w_branch_sw=w_branch_sw, w_out=w_out,
                   ffn2_norm=ffn2_norm, ffn2_w_gate=ffn2_w_gate, ffn2_w_up=ffn2_w_up, ffn2_w_down=ffn2_w_down)
    mom_m = dict(ffn1_norm=m_ffn1_norm, ffn1_w_gate=m_ffn1_w_gate, ffn1_w_up=m_ffn1_w_up, ffn1_w_down=m_ffn1_w_down,
                 mix_norm=m_mix_norm, w_in=m_w_in, b_gate=m_b_gate, na_q_norm=m_na_q_norm, na_k_norm=m_na_k_norm,
                 na_rpb=m_na_rpb, sw_q_norm=m_sw_q_norm, sw_k_norm=m_sw_k_norm, sw_sink=m_sw_sink,
                 t5_rel_table=m_t5_rel_table, w_branch_na=m_w_branch_na, w_branch_sw=m_w_branch_sw, w_out=m_w_out,
                 ffn2_norm=m_ffn2_norm, ffn2_w_gate=m_ffn2_w_gate, ffn2_w_up=m_ffn2_w_up, ffn2_w_down=m_ffn2_w_down)
    mom_v = dict(ffn1_norm=v_ffn1_norm, ffn1_w_gate=v_ffn1_w_gate, ffn1_w_up=v_ffn1_w_up, ffn1_w_down=v_ffn1_w_down,
                 mix_norm=v_mix_norm, w_in=v_w_in, b_gate=v_b_gate, na_q_norm=v_na_q_norm, na_k_norm=v_na_k_norm,
                 na_rpb=v_na_rpb, sw_q_norm=v_sw_q_norm, sw_k_norm=v_sw_k_norm, sw_sink=v_sw_sink,
                 t5_rel_table=v_t5_rel_table, w_branch_na=v_w_branch_na, w_branch_sw=v_w_branch_sw, w_out=v_w_out,
                 ffn2_norm=v_ffn2_norm, ffn2_w_gate=v_ffn2_w_gate, ffn2_w_up=v_ffn2_w_up, ffn2_w_down=v_ffn2_w_down)
    order = list(weights)

    depth = ffn1_norm.shape[0]
    s, d = x.shape[1], x.shape[2]
    xs = x[0]
    tr = lambda w: jnp.swapaxes(w, -1, -2)

    a_loc = jnp.stack([t for l in range(depth) for t in (
        tr(ffn1_w_gate[l]), tr(ffn1_w_up[l]), ffn1_w_down[l],
        tr(ffn2_w_gate[l]), tr(ffn2_w_up[l]), ffn2_w_down[l])]).astype(BF16)
    b_loc = tr(w_in).astype(BF16)
    c_loc = w_out.astype(BF16)
    d_loc = jnp.stack([t for l in range(depth) for t in (tr(w_branch_na[l]), tr(w_branch_sw[l]))]).astype(BF16)
    a_all, b_all, c_all, d_all = gather_weights([a_loc, b_loc, c_loc, d_loc])
    merge = lambda t: t.reshape(t.shape[0], N_DEV * t.shape[2], t.shape[3])
    a_all, b_all, c_all, d_all = merge(a_all), merge(b_all), merge(c_all), merge(d_all)

    bd = jnp.asarray(np.kron(np.eye(NA_WIDTH // HEAD_DIM), np.full((HEAD_DIM, HEAD_DIM), 1.0 / HEAD_DIM)), BF16)
    bmap = jnp.asarray(_t5_bucket_map())
    tile8 = lambda g: jnp.tile(g, NA_WIDTH // HEAD_DIM).reshape(1, NA_WIDTH)
    tile2 = lambda g: jnp.tile(g, SW_KV_WIDTH // HEAD_DIM).reshape(1, SW_KV_WIDTH)
    t5b = t5_expand(t5_rel_table, bmap, "t5_expand")

    saved = []
    cur = xs
    for l in range(depth):
        sv = {}
        wg1, wu1, wd1, wg2, wu2, wd2 = ((a_all, 6 * l + i) for i in range(6))
        win_t, wout_l, wna_t, wsw_t = (b_all, l), (c_all, l), (d_all, 2 * l), (d_all, 2 * l + 1)
        sv["x0"] = cur
        sv["xn1"], sv["hg1"], sv["hu1"], sv["act1"] = ffn_up(cur, ffn1_norm[l][None], wg1, wu1, f"ffn1_up_{l}")
        cur = ffn_down(cur, sv["act1"], wd1, f"ffn1_down_{l}")
        sv["x1"] = cur
        sv["gains"] = (tile8(na_q_norm[l]), tile8(na_k_norm[l]), tile8(sw_q_norm[l]), tile2(sw_k_norm[l]))
        sv["hn"], sv["zq"], sv["qa"], sv["ka"], sv["qs"], sv["ks"], sv["gt"] = mix_in(
            cur, mix_norm[l][None], win_t, b_gate[l][None], *sv["gains"], bd, f"mix_in_{l}")
        sv["t2"] = rpb_expand(na_rpb[l].reshape(-1), na_rpb.shape[1], f"rpb_expand_{l}")
        sv["o_na"] = na_fwd(sv["qa"], sv["ka"], sv["zq"], sv["t2"], f"na_fwd_{l}")
        sv["o_sw"] = sw_fwd(sv["qs"], sv["ks"], sv["zq"], t5b, sw_sink[l], f"sw_fwd_{l}")
        cur, sv["a_na"], sv["a_sw"], sv["merged"] = merge_out(
            cur, sv["o_na"], sv["o_sw"], sv["gt"], wna_t, wsw_t, wout_l, f"merge_out_{l}")
        sv["x2"] = cur
        sv["xn2"], sv["hg2"], sv["hu2"], sv["act2"] = ffn_up(cur, ffn2_norm[l][None], wg2, wu2, f"ffn2_up_{l}")
        cur = ffn_down(cur, sv["act2"], wd2, f"ffn2_down_{l}")
        saved.append(sv)

    dx, loss_acc = loss_grad(cur, loss_target[0], "loss_grad")
    loss = lax.psum(jnp.sum(loss_acc) * (0.5 / d), ("x", "y", "c"))

    split = lambda t: t.reshape(N_DEV, t.shape[0] // N_DEV, t.shape[1])
    pending = {}
    small = {k: [None] * depth for k in SMALL_NAMES if k != "t5_rel_table"}
    dbias_sw = []
    for l in reversed(range(depth)):
        sv = saved[l]
        wg1, wu1, wd1, wg2, wu2, wd2 = ((a_all, 6 * l + i) for i in range(6))
        win_t, wout_l, wna_t, wsw_t = (b_all, l), (c_all, l), (d_all, 2 * l), (d_all, 2 * l + 1)
        blocks = ((2, "x2", "xn2", "hg2", "hu2", "act2", wg2, wu2, wd2, "ffn2_norm", 3),
                  (1, "x0", "xn1", "hg1", "hu1", "act1", wg1, wu1, wd1, "ffn1_norm", 0))

        def ffn_backward(dx, blk):
            tag, xk, xnk, hgk, huk, actk, wg, wu, wd, norm_name, slot = blk
            gains = weights[norm_name]
            dxb, dhg, dhu = ffn_bwd_act(dx, wd, sv[hgk], sv[huk], f"ffn{tag}_bwd_act_{l}")
            gwd = tn_matmul(sv[actk], dxb, 0.5, f"ffn{tag}_dwd_{l}")
            gwg = tn_matmul(dhg, sv[xnk], 1.0, f"ffn{tag}_dwg_{l}")
            gwu = tn_matmul(dhu, sv[xnk], 1.0, f"ffn{tag}_dwu_{l}")
            pending[f"ffn{tag}_{l}"], token = scatter_start([[split(gwg), split(gwu), split(gwd)]],
                                                            f"scatter_ffn{tag}_{l}")
            dx, dg = proj_bwd_norm([dhg, dhu], [wg, wu], sv[xk], gains[l][None] + token[0:1, 0:1], dx,
                                   f"ffn{tag}_bwd_x_{l}")
            small[norm_name][l] = dg[0]
            return dx

        dx = ffn_backward(dx, blocks[0])
        dxb, dzg, da_na, da_sw, do_na, do_sw, dbg = mix_bwd_out(
            dx, sv["gt"], sv["a_na"], sv["a_sw"], wna_t, wsw_t, wout_l, f"mix_bwd_out_{l}")
        small["b_gate"][l] = dbg[0]
        gwout = tn_matmul(sv["merged"], dxb, 1.0, f"dwout_{l}")
        gwna = tn_matmul(da_na, sv["o_na"], 1.0, f"dwna_{l}")
        gwsw = tn_matmul(da_sw, sv["o_sw"], 1.0, f"dwsw_{l}")
        dqa, dka, dva, dt2 = na_bwd(sv["qa"], sv["ka"], sv["zq"], sv["t2"], sv["o_na"], do_na, f"na_bwd_{l}")
        dqs, dks, dvs, dbias, dsink = sw_bwd(sv["qs"], sv["ks"], sv["zq"], t5b, sw_sink[l], sv["o_sw"], do_sw,
                                             f"sw_bwd_{l}")
        dbias_sw.append(dbias)
        small["sw_sink"][l] = jnp.sum(dsink[:, :, 0], axis=1)
        drpb = rpb_reduce(dt2, f"rpb_reduce_{l}")
        small["na_rpb"][l] = drpb[:, :, :2 * NA_COLS - 1, 0]
        dz, dgqa, dgka, dgqs, dgks = qk_norm_bwd(dqa, dka, dva, dqs, dks, dvs, sv["zq"], dzg, *sv["gains"], bd,
                                                 f"qk_norm_bwd_{l}")
        fold = lambda g: jnp.sum(g.reshape(-1, HEAD_DIM), axis=0)
        small["na_q_norm"][l], small["na_k_norm"][l] = fold(dgqa), fold(dgka)
        small["sw_q_norm"][l], small["sw_k_norm"][l] = fold(dgqs), fold(dgks)
        gwin = tn_matmul(dz, sv["hn"], 1.0, f"dwin_{l}")
        pending[f"mix_{l}"], token = scatter_start([[split(gwout)], [split(gwna), split(gwsw)], [split(gwin)]],
                                                   f"scatter_mix_{l}")
        dx, dg = proj_bwd_norm([dz], [win_t], sv["x1"], mix_norm[l][None] + token[0:1, 0:1], dx, f"mix_bwd_x_{l}")
        small["mix_norm"][l] = dg[0]
        dx = ffn_backward(dx, blocks[1])

    dtab = t5_reduce(dbias_sw, bmap, "t5_reduce")
    small_parts = {k: jnp.stack(v) for k, v in small.items()}
    small_parts["t5_rel_table"] = jnp.transpose(dtab[:, :, 0])
    small_packed = _pack_small(small_parts)

    rs_ = share_small(small_packed)
    summed = {}
    for key, started in pending.items():
        zones = scatter_wait(started, dx, f"wait_{key}")
        summed[key] = [sum_sources(z, f"sum_{key}_{i}") for i, z in enumerate(zones)]

    grads = {}
    layers = lambda f: jnp.stack([f(l) for l in range(depth)])
    for tag in (1, 2):
        grads[f"ffn{tag}_w_gate"] = tr(layers(lambda l: summed[f"ffn{tag}_{l}"][0][0]))
        grads[f"ffn{tag}_w_up"] = tr(layers(lambda l: summed[f"ffn{tag}_{l}"][0][1]))
        grads[f"ffn{tag}_w_down"] = layers(lambda l: summed[f"ffn{tag}_{l}"][0][2])
    grads["w_out"] = layers(lambda l: summed[f"mix_{l}"][0][0])
    grads["w_branch_na"] = tr(layers(lambda l: summed[f"mix_{l}"][1][0]))
    grads["w_branch_sw"] = tr(layers(lambda l: summed[f"mix_{l}"][1][1]))
    grads["w_in"] = tr(layers(lambda l: summed[f"mix_{l}"][2][0]))

    delta, new_m, new_v = {}, {}, {}
    for k in order:
        if k in SMALL_NAMES:
            continue
        delta[k], new_m[k], new_v[k] = adamw(weights[k], grads[k], mom_m[k], mom_v[k], f"adamw_{k}")
    g_s, d_s, m_s, v_s = adamw_small(_pack_small(weights), rs_, _pack_small(mom_m), _pack_small(mom_v), "adamw_small")
    for dst, packed in ((grads, g_s), (delta, d_s), (new_m, m_s), (new_v, v_s)):
        dst.update(_unpack_small(packed, weights))

    return (loss, dx[None], *[grads[k] for k in order], *[delta[k] for k in order],
            *[new_m[k] for k in order], *[new_v[k] for k in order])
```

```python
import functools
import math

import numpy as np
import jax
import jax.numpy as jnp
from jax import lax
from jax.experimental import pallas as pl
from jax.experimental.pallas import tpu as pltpu

F32 = jnp.float32
BF16 = jnp.bfloat16
MESH = pl.DeviceIdType.MESH

N_DEV = 8
EPS = 1e-6
NEG = -1e30
HEAD_DIM = 64
GRID_W = 64
NA_ROWS = 8
NA_COLS = 16
NA_WIDTH = 512
SW_Q_WIDTH = 512
SW_KV_WIDTH = 128
SW_BLOCK = 128
SW_HEADS = 8
SW_REP = 4
REL_BUCKETS = 32
REL_MAX_DIST = 128
QKV_WIDTH = 3 * NA_WIDTH + SW_Q_WIDTH + 2 * SW_KV_WIDTH
SCALE = 1.0 / math.sqrt(HEAD_DIM)

ADAM_LR = 0.001
ADAM_B1 = 0.9
ADAM_B2 = 0.999
ADAM_EPS = 1e-08
ADAM_WD = 0.01
ADAM_STEP = 10

V7X_VMEM_LIMIT = 56 * 1024 * 1024
LANES = 128
MXU_TILE = 256

NT = (((1,), (1,)), ((), ()))
TN = (((0,), (0,)), ((), ()))


def _params(n_grid=1):
    return pltpu.CompilerParams(dimension_semantics=("arbitrary",) * n_grid,
                                vmem_limit_bytes=V7X_VMEM_LIMIT)


def _row_tile(s):
    for t in (256, 128, 64, 32, 16, 8):
        if s % t == 0:
            return t
    raise ValueError(s)


def _col_chunk(n):
    return MXU_TILE if n % MXU_TILE == 0 else n


def _dot(a, b):
    return jnp.dot(a, b, preferred_element_type=F32)


def _dotg(a, b, dn):
    return lax.dot_general(a, b, dn, preferred_element_type=F32)


def _sigmoid(v):
    return 1.0 / (1.0 + jnp.exp(-v))


def _rstd(xv):
    return lax.rsqrt(jnp.mean(xv * xv, axis=-1, keepdims=True) + EPS)


def _full(shape):
    nd = len(shape)
    return pl.BlockSpec(shape, lambda i, _n=nd: (0,) * _n)


def _rows(tm, width):
    return pl.BlockSpec((tm, width), lambda i: (i, 0))


def _mat(stack, idx):
    return pl.BlockSpec((None,) + tuple(stack.shape[1:]), lambda i, _w=idx: (_w, 0, 0))


def _group_mean(v, bd):
    hi = v.astype(BF16)
    lo = (v - hi.astype(F32)).astype(BF16)
    return _dot(hi, bd) + _dot(lo, bd)


def ffn_up(x, gain, wg_t, wu_t, dep, name):
    s, d = x.shape
    f = wg_t[0].shape[1]
    tm = _row_tile(s)
    fc = _col_chunk(f)

    def body(x_ref, g_ref, wg_ref, wu_ref, dep_ref, xn_ref, hg_ref, hu_ref, act_ref):
        xv = x_ref[...]
        xn = (xv * _rstd(xv) * g_ref[...]).astype(BF16)
        xn_ref[...] = xn
        for c0 in range(0, f, fc):
            hg = _dotg(xn, wg_ref[c0:c0 + fc, :], NT)
            hu = _dotg(xn, wu_ref[c0:c0 + fc, :], NT)
            hg_ref[:, c0:c0 + fc] = hg.astype(BF16)
            hu_ref[:, c0:c0 + fc] = hu.astype(BF16)
            act_ref[:, c0:c0 + fc] = (hg * _sigmoid(hg) * hu).astype(BF16)

    return pl.pallas_call(
        body, name=name, grid=(s // tm,),
        in_specs=[_rows(tm, d), _full((1, d)), _mat(*wg_t), _mat(*wu_t), _full(dep.shape)],
        out_specs=[_rows(tm, d), _rows(tm, f), _rows(tm, f), _rows(tm, f)],
        out_shape=[jax.ShapeDtypeStruct((s, d), BF16)] + [jax.ShapeDtypeStruct((s, f), BF16)] * 3,
        compiler_params=_params(),
    )(x, gain, wg_t[0], wu_t[0], dep)


def ffn_down(x, act, wd, name):
    s, d = x.shape
    f = act.shape[1]
    tm = _row_tile(s)

    def body(x_ref, a_ref, w_ref, o_ref):
        o_ref[...] = x_ref[...] + 0.5 * _dot(a_ref[...], w_ref[...])

    return pl.pallas_call(
        body, name=name, grid=(s // tm,),
        in_specs=[_rows(tm, d), _rows(tm, f), _mat(*wd)],
        out_specs=_rows(tm, d),
        out_shape=jax.ShapeDtypeStruct((s, d), F32),
        compiler_params=_params(),
    )(x, act, wd[0])


def mix_in(x, gain, win_t, b_gate, gq_na, gk_na, gq_sw, gk_sw, bd, name):
    s, d = x.shape
    tm = _row_tile(s)
    gc = _col_chunk(2 * d)

    def body(x_ref, g_ref, w_ref, b_ref, gqa_ref, gka_ref, gqs_ref, gks_ref, bd_ref,
             hn_ref, zq_ref, qa_ref, ka_ref, qs_ref, ks_ref, gt_ref):
        xv = x_ref[...]
        hn = (xv * _rstd(xv) * g_ref[...]).astype(BF16)
        hn_ref[...] = hn

        def proj(c0, c1):
            return _dotg(hn, w_ref[c0:c1, :], NT)

        def headnorm(z, g, bdm):
            return z * lax.rsqrt(_group_mean(z * z, bdm) + EPS) * g

        bd512 = bd_ref[...]
        bd128 = bd_ref[0:SW_KV_WIDTH, 0:SW_KV_WIDTH]
        z = proj(0, 512)
        zq_ref[:, 0:512] = z.astype(BF16)
        qa_ref[...] = (headnorm(z, gqa_ref[...], bd512) * SCALE).astype(BF16)
        z = proj(512, 1024)
        zq_ref[:, 512:1024] = z.astype(BF16)
        ka_ref[...] = headnorm(z, gka_ref[...], bd512).astype(BF16)
        z = proj(1024, 1536)
        zq_ref[:, 1024:1536] = z.astype(BF16)
        z = proj(1536, 2048)
        zq_ref[:, 1536:2048] = z.astype(BF16)
        qs_ref[...] = (headnorm(z, gqs_ref[...], bd512) * SCALE).astype(BF16)
        z = proj(2048, 2176)
        zq_ref[:, 2048:2176] = z.astype(BF16)
        ks_ref[...] = headnorm(z, gks_ref[...], bd128).astype(BF16)
        z = proj(2176, 2304)
        zq_ref[:, 2176:2304] = z.astype(BF16)
        for c0 in range(0, 2 * d, gc):
            zg = proj(QKV_WIDTH + c0, QKV_WIDTH + c0 + gc) + b_ref[:, c0:c0 + gc]
            gt_ref[:, c0:c0 + gc] = _sigmoid(zg).astype(BF16)

    return pl.pallas_call(
        body, name=name, grid=(s // tm,),
        in_specs=[_rows(tm, d), _full((1, d)), _mat(*win_t), _full((1, 2 * d)),
                  _full((1, 512)), _full((1, 512)), _full((1, 512)), _full((1, 128)), _full((512, 512))],
        out_specs=[_rows(tm, d), _rows(tm, QKV_WIDTH), _rows(tm, 512), _rows(tm, 512), _rows(tm, 512),
                   _rows(tm, 128), _rows(tm, 2 * d)],
        out_shape=[jax.ShapeDtypeStruct((s, d), BF16), jax.ShapeDtypeStruct((s, QKV_WIDTH), BF16),
                   jax.ShapeDtypeStruct((s, 512), BF16), jax.ShapeDtypeStruct((s, 512), BF16),
                   jax.ShapeDtypeStruct((s, 512), BF16), jax.ShapeDtypeStruct((s, 128), BF16),
                   jax.ShapeDtypeStruct((s, 2 * d), BF16)],
        compiler_params=_params(),
    )(x, gain, win_t[0], b_gate, gq_na, gk_na, gq_sw, gk_sw, bd)


def _na_iotas():
    qc = lax.broadcasted_iota(jnp.int32, (GRID_W, LANES), 0)
    ln = lax.broadcasted_iota(jnp.int32, (GRID_W, LANES), 1)
    low = ln < GRID_W
    kc = jnp.where(low, ln, ln - GRID_W)
    diff = kc - qc + (NA_COLS - 1)
    qcs = jnp.clip(qc - NA_COLS // 2, 0, GRID_W - NA_COLS)
    inwin = (kc >= qcs) & (kc < qcs + NA_COLS)
    return diff, low, inwin


NA_RI = 2 * NA_ROWS - 1
NA_CI = 2 * NA_COLS - 1
NA_T2 = NA_RI + 1


def rpb_expand(rpb_flat, n_heads, dep, name):
    def body(rpb_ref, dep_ref, o_ref):
        diff, low, _ = _na_iotas()
        for h in range(n_heads):
            def one(e, carry, h=h):
                lo_row = jnp.maximum(e - 1, 0)
                hi_row = jnp.minimum(e, NA_RI - 1)
                lo_on = jnp.where(e >= 1, 1.0, 0.0)
                hi_on = jnp.where(e <= NA_RI - 1, 1.0, 0.0)
                t = jnp.zeros((GRID_W, LANES), F32)
                for c in range(NA_CI):
                    lo = rpb_ref[h * NA_RI * NA_CI + lo_row * NA_CI + c] * lo_on
                    hi = rpb_ref[h * NA_RI * NA_CI + hi_row * NA_CI + c] * hi_on
                    t = jnp.where(diff == c, jnp.where(low, lo, hi), t)
                o_ref[h, e] = t
                return carry
            lax.fori_loop(0, NA_T2, one, 0)

    return pl.pallas_call(
        body, name=name,
        in_specs=[pl.BlockSpec(memory_space=pltpu.SMEM), pl.BlockSpec(memory_space=pltpu.VMEM)],
        out_specs=pl.BlockSpec(memory_space=pltpu.VMEM),
        out_shape=jax.ShapeDtypeStruct((n_heads, NA_T2, GRID_W, LANES), F32),
        compiler_params=pltpu.CompilerParams(vmem_limit_bytes=V7X_VMEM_LIMIT),
    )(rpb_flat, dep)


def rpb_reduce(dt2, name):
    n_heads = dt2.shape[0]

    def body(d_ref, o_ref):
        diff, low, _ = _na_iotas()
        low32 = lax.broadcasted_iota(jnp.int32, (32, LANES), 1) < GRID_W
        o_ref[...] = jnp.zeros(o_ref.shape, F32)
        for h in range(n_heads):
            def one(e, carry, h=h):
                dv = d_ref[h, e]
                rows = [jnp.sum(jnp.where(diff == c, dv, 0.0), axis=0, keepdims=True) for c in range(NA_CI)]
                rows.append(jnp.zeros((1, LANES), F32))
                r = jnp.concatenate(rows, axis=0)
                lo = jnp.sum(jnp.where(low32, r, 0.0), axis=1, keepdims=True)
                hi = jnp.sum(jnp.where(low32, 0.0, r), axis=1, keepdims=True)
                lo_row = jnp.maximum(e - 1, 0)
                hi_row = jnp.minimum(e, NA_RI - 1)
                o_ref[h, lo_row] = o_ref[h, lo_row] + jnp.broadcast_to(lo, (32, LANES))
                o_ref[h, hi_row] = o_ref[h, hi_row] + jnp.broadcast_to(hi, (32, LANES))
                return carry
            lax.fori_loop(0, NA_T2, one, 0)

    return pl.pallas_call(
        body, name=name,
        in_specs=[pl.BlockSpec(memory_space=pltpu.VMEM)],
        out_specs=pl.BlockSpec(memory_space=pltpu.VMEM),
        out_shape=jax.ShapeDtypeStruct((n_heads, NA_RI, 32, LANES), F32),
        compiler_params=pltpu.CompilerParams(vmem_limit_bytes=V7X_VMEM_LIMIT),
    )(dt2)


NA_TQ = 4
NA_TK = NA_TQ + NA_ROWS
NA_KCH = NA_TK // 2


def _na_tile_geometry(t, rows):
    r = t * NA_TQ
    kbase = jnp.clip(r - NA_ROWS // 2, 0, rows - NA_TK)
    starts = [jnp.clip(r + a - NA_ROWS // 2, 0, rows - NA_ROWS) for a in range(NA_TQ)]
    return r, kbase, starts


def _na_tile_mask(kbase, starts, low, inwin):
    half = jnp.where(low, 0, 1)
    cols = []
    for c in range(NA_KCH):
        krow = kbase + 2 * c + half
        cols.append(jnp.concatenate(
            [jnp.where(inwin & (krow >= st) & (krow < st + NA_ROWS), 0.0, NEG) for st in starts], axis=0))
    return jnp.concatenate(cols, axis=1)


def _na_tile_index(r, kbase, a, c):
    return jnp.clip(kbase + 2 * c - (r + a) + NA_ROWS, 0, NA_T2 - 1)


def _na_tile_probs(q, k, t2_ref, hh, r, kbase, madd):
    bias = jnp.concatenate(
        [jnp.concatenate([t2_ref[hh, _na_tile_index(r, kbase, a, c)] for a in range(NA_TQ)], axis=0)
         for c in range(NA_KCH)], axis=1)
    sc = _dotg(q, k, NT) + bias + madd
    e = jnp.exp(sc - jnp.max(sc, axis=1, keepdims=True))
    return e * (1.0 / jnp.sum(e, axis=1, keepdims=True))


def na_fwd(qa, ka, zq, t2, name):
    s = qa.shape[0]
    rows = s // GRID_W
    n_pairs = NA_WIDTH // LANES
    v_blk0 = (2 * NA_WIDTH) // LANES

    assert rows % NA_TQ == 0 and rows >= NA_TK
    tq, tk = NA_TQ * GRID_W, NA_TK * GRID_W

    def body(q_ref, k_ref, v_ref, t2_ref, o_ref):
        _, low, inwin = _na_iotas()

        def tile(t, carry):
            r, kbase, starts = _na_tile_geometry(t, rows)
            madd = _na_tile_mask(kbase, starts, low, inwin)
            qr = pl.ds(pl.multiple_of(r * GRID_W, tq), tq)
            kr = pl.ds(pl.multiple_of(kbase * GRID_W, tq), tk)
            for hh in range(2):
                lanes = slice(HEAD_DIM * hh, HEAD_DIM * (hh + 1))
                p = _na_tile_probs(q_ref[qr, lanes], k_ref[kr, lanes], t2_ref, hh, r, kbase, madd)
                o_ref[qr, lanes] = _dot(p.astype(BF16), v_ref[kr, lanes]).astype(BF16)
            return carry

        lax.fori_loop(0, rows // NA_TQ, tile, 0)

    col = lambda off: pl.BlockSpec((s, LANES), lambda p, _o=off: (0, _o + p))
    return pl.pallas_call(
        body, name=name, grid=(n_pairs,),
        in_specs=[col(0), col(0), col(v_blk0),
                  pl.BlockSpec((2, NA_T2, GRID_W, LANES), lambda p: (p, 0, 0, 0))],
        out_specs=col(0),
        out_shape=jax.ShapeDtypeStruct((s, NA_WIDTH), BF16),
        compiler_params=_params(),
    )(qa, ka, zq, t2)


def na_bwd(qa, ka, zq, t2, o_na, do_na, name):
    s = qa.shape[0]
    rows = s // GRID_W
    n_pairs = NA_WIDTH // LANES
    v_blk0 = (2 * NA_WIDTH) // LANES

    tq, tk = NA_TQ * GRID_W, NA_TK * GRID_W

    def body(q_ref, k_ref, v_ref, t2_ref, o_ref, do_ref, dq_ref, dk_ref, dv_ref, dt2_ref):
        _, low, inwin = _na_iotas()
        dk_ref[...] = jnp.zeros(dk_ref.shape, F32)
        dv_ref[...] = jnp.zeros(dv_ref.shape, F32)
        dt2_ref[...] = jnp.zeros(dt2_ref.shape, F32)

        def tile(t, carry):
            r, kbase, starts = _na_tile_geometry(t, rows)
            madd = _na_tile_mask(kbase, starts, low, inwin)
            qr = pl.ds(pl.multiple_of(r * GRID_W, tq), tq)
            kr = pl.ds(pl.multiple_of(kbase * GRID_W, tq), tk)
            for hh in range(2):
                lanes = slice(HEAD_DIM * hh, HEAD_DIM * (hh + 1))
                q, k, v = q_ref[qr, lanes], k_ref[kr, lanes], v_ref[kr, lanes]
                p = _na_tile_probs(q, k, t2_ref, hh, r, kbase, madd)
                do = do_ref[qr, lanes]
                delta = jnp.sum(do.astype(F32) * o_ref[qr, lanes].astype(F32), axis=1, keepdims=True)
                ds = p * (_dotg(do, v, NT) - delta)
                for a in range(NA_TQ):
                    for c in range(NA_KCH):
                        e = _na_tile_index(r, kbase, a, c)
                        dt2_ref[hh, e] = dt2_ref[hh, e] + ds[GRID_W * a:GRID_W * (a + 1), LANES * c:LANES * (c + 1)]
                dsb = ds.astype(BF16)
                dq_ref[qr, lanes] = _dot(dsb, k)
                dk_ref[kr, lanes] = dk_ref[kr, lanes] + _dotg(dsb, q, TN)
                dv_ref[kr, lanes] = dv_ref[kr, lanes] + _dotg(p.astype(BF16), do, TN)
            return carry

        lax.fori_loop(0, rows // NA_TQ, tile, 0)

    col = lambda off: pl.BlockSpec((s, LANES), lambda p, _o=off: (0, _o + p))
    t2spec = pl.BlockSpec((2, NA_T2, GRID_W, LANES), lambda p: (p, 0, 0, 0))
    return pl.pallas_call(
        body, name=name, grid=(n_pairs,),
        in_specs=[col(0), col(0), col(v_blk0), t2spec, col(0), col(0)],
        out_specs=[col(0), col(0), col(0), t2spec],
        out_shape=[jax.ShapeDtypeStruct((s, NA_WIDTH), F32)] * 3 + [jax.ShapeDtypeStruct(t2.shape, F32)],
        compiler_params=_params(),
    )(qa, ka, zq, t2, o_na, do_na)


def _t5_bucket_map():
    rel = np.arange(3 * SW_BLOCK)[None, :] - SW_BLOCK - np.arange(SW_BLOCK)[:, None]
    nb = REL_BUCKETS // 2
    max_exact = nb // 2
    n = np.abs(rel)
    large = max_exact + (np.log(np.maximum(n, 1) / max_exact)
                         / np.log(REL_MAX_DIST / max_exact) * (nb - max_exact)).astype(np.int32)
    large = np.minimum(large, nb - 1)
    return ((rel > 0) * nb + np.where(n < max_exact, n, large)).astype(np.int32)


def t5_expand(table, bmap, name):
    def body(tab_ref, bm_ref, o_ref):
        bm = bm_ref[...]
        for h in range(SW_HEADS):
            t = jnp.zeros(bm.shape, F32)
            for b in range(REL_BUCKETS):
                t = jnp.where(bm == b, tab_ref[b, h], t)
            o_ref[h] = t

    return pl.pallas_call(
        body, name=name,
        in_specs=[pl.BlockSpec(memory_space=pltpu.SMEM), pl.BlockSpec(memory_space=pltpu.VMEM)],
        out_specs=pl.BlockSpec(memory_space=pltpu.VMEM),
        out_shape=jax.ShapeDtypeStruct((SW_HEADS,) + bmap.shape, F32),
        compiler_params=pltpu.CompilerParams(vmem_limit_bytes=V7X_VMEM_LIMIT),
    )(table, bmap)


def t5_reduce(dbias_list, bmap, name):
    n = len(dbias_list)

    def body(*refs):
        d_refs, bm_ref, o_ref = refs[:n], refs[n], refs[n + 1]
        bm = bm_ref[...]
        for h in range(SW_HEADS):
            dv = d_refs[0][h]
            for other in d_refs[1:]:
                dv = dv + other[h]
            rows = [jnp.sum(jnp.where(bm == b, dv, 0.0), axis=0, keepdims=True) for b in range(REL_BUCKETS)]
            r = jnp.concatenate(rows, axis=0)
            o_ref[h] = jnp.broadcast_to(jnp.sum(r, axis=1, keepdims=True), (REL_BUCKETS, LANES))

    return pl.pallas_call(
        body, name=name,
        in_specs=[pl.BlockSpec(memory_space=pltpu.VMEM)] * (n + 1),
        out_specs=pl.BlockSpec(memory_space=pltpu.VMEM),
        out_shape=jax.ShapeDtypeStruct((SW_HEADS, REL_BUCKETS, LANES), F32),
        compiler_params=pltpu.CompilerParams(vmem_limit_bytes=V7X_VMEM_LIMIT),
    )(*dbias_list, bmap)


def _sw_mask_iotas():
    a = lax.broadcasted_iota(jnp.int32, (SW_BLOCK, 3 * SW_BLOCK), 0)
    j = lax.broadcasted_iota(jnp.int32, (SW_BLOCK, 3 * SW_BLOCK), 1)
    inwin = jnp.abs(j - SW_BLOCK - a) <= SW_BLOCK
    return j, inwin


def _sw_probs(q, k, bias, madd, sk):
    sc = _dotg(q, k, NT) + bias + madd
    m = jnp.maximum(jnp.max(sc, axis=1, keepdims=True), sk)
    e = jnp.exp(sc - m)
    es = jnp.exp(sk - m)
    inv = 1.0 / (jnp.sum(e, axis=1, keepdims=True) + es)
    return e * inv, es * inv


def sw_fwd(qs, ks, zq, t5b, sink, name):
    s = qs.shape[0]
    nb = s // SW_BLOCK
    v_blk = (3 * NA_WIDTH + SW_Q_WIDTH + SW_KV_WIDTH) // LANES
    pad = s + 2 * SW_BLOCK

    def body(q_ref, k_ref, v_ref, b_ref, sink_ref, o_ref, kp, vp):
        zeros = jnp.zeros((SW_BLOCK, SW_KV_WIDTH), BF16)
        kp[0:SW_BLOCK, :] = zeros
        vp[0:SW_BLOCK, :] = zeros
        kp[SW_BLOCK + s:pad, :] = zeros
        vp[SW_BLOCK + s:pad, :] = zeros
        kp[SW_BLOCK:SW_BLOCK + s, :] = k_ref[...]
        vp[SW_BLOCK:SW_BLOCK + s, :] = v_ref[...]
        j, inwin = _sw_mask_iotas()

        def blk(n, carry):
            kpos = n * SW_BLOCK - SW_BLOCK + j
            madd = jnp.where(inwin & (kpos >= 0) & (kpos < s), 0.0, NEG)
            q0 = pl.multiple_of(n * SW_BLOCK, SW_BLOCK)
            for h in range(SW_HEADS):
                g = h // SW_REP
                q = q_ref[pl.ds(q0, SW_BLOCK), HEAD_DIM * h:HEAD_DIM * (h + 1)]
                k = kp[pl.ds(q0, 3 * SW_BLOCK), HEAD_DIM * g:HEAD_DIM * (g + 1)]
                v = vp[pl.ds(q0, 3 * SW_BLOCK), HEAD_DIM * g:HEAD_DIM * (g + 1)]
                p, _ = _sw_probs(q, k, b_ref[h], madd, sink_ref[h])
                o_ref[pl.ds(q0, SW_BLOCK), HEAD_DIM * h:HEAD_DIM * (h + 1)] = _dot(p.astype(BF16), v).astype(BF16)
            return carry

        lax.fori_loop(0, nb, blk, 0)

    return pl.pallas_call(
        body, name=name, grid=(1,),
        in_specs=[_full((s, SW_Q_WIDTH)), _full((s, SW_KV_WIDTH)),
                  pl.BlockSpec((s, SW_KV_WIDTH), lambda i: (0, v_blk)),
                  _full((SW_HEADS, SW_BLOCK, 3 * SW_BLOCK)), pl.BlockSpec(memory_space=pltpu.SMEM)],
        out_specs=_full((s, SW_Q_WIDTH)),
        out_shape=jax.ShapeDtypeStruct((s, SW_Q_WIDTH), BF16),
        scratch_shapes=[pltpu.VMEM((pad, SW_KV_WIDTH), BF16), pltpu.VMEM((pad, SW_KV_WIDTH), BF16)],
        compiler_params=_params(),
    )(qs, ks, zq, t5b, sink)


def sw_bwd(qs, ks, zq, t5b, sink, o_sw, do_sw, name):
    s = qs.shape[0]
    nb = s // SW_BLOCK
    v_blk = (3 * NA_WIDTH + SW_Q_WIDTH + SW_KV_WIDTH) // LANES
    pad = s + 2 * SW_BLOCK

    def body(q_ref, k_ref, v_ref, b_ref, sink_ref, o_ref, do_ref,
             dq_ref, dk_ref, dv_ref, db_ref, dsk_ref, kp, vp, dkp, dvp):
        zeros = jnp.zeros((SW_BLOCK, SW_KV_WIDTH), BF16)
        kp[0:SW_BLOCK, :] = zeros
        vp[0:SW_BLOCK, :] = zeros
        kp[SW_BLOCK + s:pad, :] = zeros
        vp[SW_BLOCK + s:pad, :] = zeros
        kp[SW_BLOCK:SW_BLOCK + s, :] = k_ref[...]
        vp[SW_BLOCK:SW_BLOCK + s, :] = v_ref[...]
        dkp[...] = jnp.zeros(dkp.shape, F32)
        dvp[...] = jnp.zeros(dvp.shape, F32)
        db_ref[...] = jnp.zeros(db_ref.shape, F32)
        dsk_ref[...] = jnp.zeros(dsk_ref.shape, F32)
        j, inwin = _sw_mask_iotas()

        def blk(n, carry):
            kpos = n * SW_BLOCK - SW_BLOCK + j
            madd = jnp.where(inwin & (kpos >= 0) & (kpos < s), 0.0, NEG)
            q0 = pl.multiple_of(n * SW_BLOCK, SW_BLOCK)
            for g in range(SW_HEADS // SW_REP):
                kl = slice(HEAD_DIM * g, HEAD_DIM * (g + 1))
                k = kp[pl.ds(q0, 3 * SW_BLOCK), kl]
                v = vp[pl.ds(q0, 3 * SW_BLOCK), kl]
                dkw = jnp.zeros((3 * SW_BLOCK, HEAD_DIM), F32)
                dvw = jnp.zeros((3 * SW_BLOCK, HEAD_DIM), F32)
                for r in range(SW_REP):
                    h = g * SW_REP + r
                    hl = slice(HEAD_DIM * h, HEAD_DIM * (h + 1))
                    q = q_ref[pl.ds(q0, SW_BLOCK), hl]
                    p, ps = _sw_probs(q, k, b_ref[h], madd, sink_ref[h])
                    do = do_ref[pl.ds(q0, SW_BLOCK), hl]
                    ov = o_ref[pl.ds(q0, SW_BLOCK), hl]
                    delta = jnp.sum(do.astype(F32) * ov.astype(F32), axis=1, keepdims=True)
                    ds = p * (_dotg(do, v, NT) - delta)
                    db_ref[h] = db_ref[h] + ds
                    dsk_ref[h] = dsk_ref[h] - jnp.broadcast_to(ps * delta, (SW_BLOCK, LANES))
                    dsb = ds.astype(BF16)
                    dq_ref[pl.ds(q0, SW_BLOCK), hl] = _dot(dsb, k)
                    dkw = dkw + _dotg(dsb, q, TN)
                    dvw = dvw + _dotg(p.astype(BF16), do, TN)
                dkp[pl.ds(q0, 3 * SW_BLOCK), kl] = dkp[pl.ds(q0, 3 * SW_BLOCK), kl] + dkw
                dvp[pl.ds(q0, 3 * SW_BLOCK), kl] = dvp[pl.ds(q0, 3 * SW_BLOCK), kl] + dvw
            return carry

        lax.fori_loop(0, nb, blk, 0)
        dk_ref[...] = dkp[SW_BLOCK:SW_BLOCK + s, :]
        dv_ref[...] = dvp[SW_BLOCK:SW_BLOCK + s, :]

    bias_spec = _full((SW_HEADS, SW_BLOCK, 3 * SW_BLOCK))
    return pl.pallas_call(
        body, name=name, grid=(1,),
        in_specs=[_full((s, SW_Q_WIDTH)), _full((s, SW_KV_WIDTH)),
                  pl.BlockSpec((s, SW_KV_WIDTH), lambda i: (0, v_blk)),
                  bias_spec, pl.BlockSpec(memory_space=pltpu.SMEM),
                  _full((s, SW_Q_WIDTH)), _full((s, SW_Q_WIDTH))],
        out_specs=[_full((s, SW_Q_WIDTH)), _full((s, SW_KV_WIDTH)), _full((s, SW_KV_WIDTH)), bias_spec,
                   _full((SW_HEADS, SW_BLOCK, LANES))],
        out_shape=[jax.ShapeDtypeStruct((s, SW_Q_WIDTH), F32), jax.ShapeDtypeStruct((s, SW_KV_WIDTH), F32),
                   jax.ShapeDtypeStruct((s, SW_KV_WIDTH), F32),
                   jax.ShapeDtypeStruct((SW_HEADS, SW_BLOCK, 3 * SW_BLOCK), F32),
                   jax.ShapeDtypeStruct((SW_HEADS, SW_BLOCK, LANES), F32)],
        scratch_shapes=[pltpu.VMEM((pad, SW_KV_WIDTH), BF16), pltpu.VMEM((pad, SW_KV_WIDTH), BF16),
                        pltpu.VMEM((pad, SW_KV_WIDTH), F32), pltpu.VMEM((pad, SW_KV_WIDTH), F32)],
        compiler_params=_params(),
    )(qs, ks, zq, t5b, sink, o_sw, do_sw)


def merge_out(x, o_na, o_sw, gt, wbna_t, wbsw_t, wout, name):
    s, d = x.shape
    tm = _row_tile(s)

    def body(x_ref, ona_ref, osw_ref, gt_ref, wna_ref, wsw_ref, wo_ref, xo_ref, ana_ref, asw_ref, mg_ref):
        a_na = _dotg(ona_ref[...], wna_ref[...], NT)
        a_sw = _dotg(osw_ref[...], wsw_ref[...], NT)
        ana_ref[...] = a_na.astype(BF16)
        asw_ref[...] = a_sw.astype(BF16)
        merged = (gt_ref[:, 0:d].astype(F32) * a_na + gt_ref[:, d:2 * d].astype(F32) * a_sw).astype(BF16)
        mg_ref[...] = merged
        xo_ref[...] = x_ref[...] + _dot(merged, wo_ref[...])

    return pl.pallas_call(
        body, name=name, grid=(s // tm,),
        in_specs=[_rows(tm, d), _rows(tm, 512), _rows(tm, 512), _rows(tm, 2 * d),
                  _mat(*wbna_t), _mat(*wbsw_t), _mat(*wout)],
        out_specs=[_rows(tm, d)] * 4,
        out_shape=[jax.ShapeDtypeStruct((s, d), F32)] + [jax.ShapeDtypeStruct((s, d), BF16)] * 3,
        compiler_params=_params(),
    )(x, o_na, o_sw, gt, wbna_t[0], wbsw_t[0], wout[0])


def mix_bwd_out(dx, gt, a_na, a_sw, wbna_t, wbsw_t, wout, name):
    s, d = dx.shape
    tm = _row_tile(s)

    def body(dx_ref, gt_ref, ana_ref, asw_ref, wna_ref, wsw_ref, wo_ref,
             dxb_ref, dzg_ref, dana_ref, dasw_ref, dona_ref, dosw_ref, dbg_ref):
        @pl.when(pl.program_id(0) == 0)
        def _():
            dbg_ref[...] = jnp.zeros(dbg_ref.shape, F32)

        dxb = dx_ref[...].astype(BF16)
        dxb_ref[...] = dxb
        dm = _dotg(dxb, wo_ref[...], NT)
        for i, (a_ref, da_ref, w_ref, do_ref) in enumerate(
                [(ana_ref, dana_ref, wna_ref, dona_ref), (asw_ref, dasw_ref, wsw_ref, dosw_ref)]):
            gi = gt_ref[:, i * d:(i + 1) * d].astype(F32)
            da = (dm * gi).astype(BF16)
            da_ref[...] = da
            do_ref[...] = _dot(da, w_ref[...]).astype(BF16)
            dzg = dm * a_ref[...].astype(F32) * gi * (1.0 - gi)
            dzg_ref[:, i * d:(i + 1) * d] = dzg.astype(BF16)
            dbg_ref[:, i * d:(i + 1) * d] = dbg_ref[:, i * d:(i + 1) * d] + jnp.sum(dzg, axis=0, keepdims=True)

    return pl.pallas_call(
        body, name=name, grid=(s // tm,),
        in_specs=[_rows(tm, d), _rows(tm, 2 * d), _rows(tm, d), _rows(tm, d),
                  _mat(*wbna_t), _mat(*wbsw_t), _mat(*wout)],
        out_specs=[_rows(tm, d), _rows(tm, 2 * d), _rows(tm, d), _rows(tm, d), _rows(tm, 512), _rows(tm, 512),
                   _full((1, 2 * d))],
        out_shape=[jax.ShapeDtypeStruct((s, d), BF16), jax.ShapeDtypeStruct((s, 2 * d), BF16),
                   jax.ShapeDtypeStruct((s, d), BF16), jax.ShapeDtypeStruct((s, d), BF16),
                   jax.ShapeDtypeStruct((s, 512), BF16), jax.ShapeDtypeStruct((s, 512), BF16),
                   jax.ShapeDtypeStruct((1, 2 * d), F32)],
        compiler_params=_params(),
    )(dx, gt, a_na, a_sw, wbna_t[0], wbsw_t[0], wout[0])


def qk_norm_bwd(dqa, dka, dva, dqs, dks, dvs, zq, dzg, gq_na, gk_na, gq_sw, gk_sw, bd, name):
    s = zq.shape[0]
    d2 = dzg.shape[1]
    n_in = QKV_WIDTH + d2
    tm = _row_tile(s)

    def body(dqa_ref, dka_ref, dva_ref, dqs_ref, dks_ref, dvs_ref, zq_ref, dzg_ref,
             gqa_ref, gka_ref, gqs_ref, gks_ref, bd_ref, dz_ref, dgqa_ref, dgka_ref, dgqs_ref, dgks_ref):
        @pl.when(pl.program_id(0) == 0)
        def _():
            for r in (dgqa_ref, dgka_ref, dgqs_ref, dgks_ref):
                r[...] = jnp.zeros(r.shape, F32)

        bd512 = bd_ref[...]
        bd128 = bd_ref[0:SW_KV_WIDTH, 0:SW_KV_WIDTH]

        def one(c0, c1, dy_ref, g_ref, dg_ref, bdm, scale):
            z = zq_ref[:, c0:c1].astype(F32)
            r = lax.rsqrt(_group_mean(z * z, bdm) + EPS)
            zh = z * r
            dy = dy_ref[...] * scale
            dyg = dy * g_ref[...]
            dz = r * (dyg - zh * _group_mean(dyg * zh, bdm))
            dz_ref[:, c0:c1] = dz.astype(BF16)
            dg_ref[...] = dg_ref[...] + jnp.sum(dy * zh, axis=0, keepdims=True)

        one(0, 512, dqa_ref, gqa_ref, dgqa_ref, bd512, SCALE)
        one(512, 1024, dka_ref, gka_ref, dgka_ref, bd512, 1.0)
        dz_ref[:, 1024:1536] = dva_ref[...].astype(BF16)
        one(1536, 2048, dqs_ref, gqs_ref, dgqs_ref, bd512, SCALE)
        one(2048, 2176, dks_ref, gks_ref, dgks_ref, bd128, 1.0)
        dz_ref[:, 2176:2304] = dvs_ref[...].astype(BF16)
        dz_ref[:, QKV_WIDTH:n_in] = dzg_ref[...]

    return pl.pallas_call(
        body, name=name, grid=(s // tm,),
        in_specs=[_rows(tm, 512), _rows(tm, 512), _rows(tm, 512), _rows(tm, 512), _rows(tm, 128), _rows(tm, 128),
                  _rows(tm, QKV_WIDTH), _rows(tm, d2),
                  _full((1, 512)), _full((1, 512)), _full((1, 512)), _full((1, 128)), _full((512, 512))],
        out_specs=[_rows(tm, n_in), _full((1, 512)), _full((1, 512)), _full((1, 512)), _full((1, 128))],
        out_shape=[jax.ShapeDtypeStruct((s, n_in), BF16)] + [jax.ShapeDtypeStruct((1, 512), F32)] * 3
                  + [jax.ShapeDtypeStruct((1, 128), F32)],
        compiler_params=_params(),
    )(dqa, dka, dva, dqs, dks, dvs, zq, dzg, gq_na, gk_na, gq_sw, gk_sw, bd)


def ffn_bwd_act(dx, wd, hg, hu, name):
    s, d = dx.shape
    f = wd[0].shape[1]
    tm = _row_tile(s)
    fc = _col_chunk(f)

    def body(dx_ref, w_ref, hg_ref, hu_ref, dxb_ref, dhg_ref, dhu_ref):
        dxb = dx_ref[...].astype(BF16)
        dxb_ref[...] = dxb
        for c0 in range(0, f, fc):
            dact = 0.5 * _dotg(dxb, w_ref[c0:c0 + fc, :], NT)
            hg = hg_ref[:, c0:c0 + fc].astype(F32)
            hu = hu_ref[:, c0:c0 + fc].astype(F32)
            sg = _sigmoid(hg)
            dhu_ref[:, c0:c0 + fc] = (dact * hg * sg).astype(BF16)
            dhg_ref[:, c0:c0 + fc] = (dact * hu * sg * (1.0 + hg * (1.0 - sg))).astype(BF16)

    return pl.pallas_call(
        body, name=name, grid=(s // tm,),
        in_specs=[_rows(tm, d), _mat(*wd), _rows(tm, f), _rows(tm, f)],
        out_specs=[_rows(tm, d), _rows(tm, f), _rows(tm, f)],
        out_shape=[jax.ShapeDtypeStruct((s, d), BF16), jax.ShapeDtypeStruct((s, f), BF16),
                   jax.ShapeDtypeStruct((s, f), BF16)],
        compiler_params=_params(),
    )(dx, wd[0], hg, hu)


def proj_bwd_norm(acts, weights, x, gain, dx, dep, name):
    s, d = x.shape
    tm = _row_tile(s)
    n = len(acts)

    def body(*refs):
        a_refs, w_refs = refs[:n], refs[n:2 * n]
        x_ref, g_ref, dx_ref, _, o_ref, dg_ref = refs[2 * n:]

        @pl.when(pl.program_id(0) == 0)
        def _():
            dg_ref[...] = jnp.zeros(dg_ref.shape, F32)

        dxn = _dot(a_refs[0][...], w_refs[0][...])
        for a_ref, w_ref in zip(a_refs[1:], w_refs[1:]):
            dxn = dxn + _dot(a_ref[...], w_ref[...])
        xv = x_ref[...]
        r = _rstd(xv)
        xh = xv * r
        dxh = dxn * g_ref[...]
        o_ref[...] = dx_ref[...] + r * (dxh - xh * jnp.mean(dxh * xh, axis=-1, keepdims=True))
        dg_ref[...] = dg_ref[...] + jnp.sum(dxn * xh, axis=0, keepdims=True)

    return pl.pallas_call(
        body, name=name, grid=(s // tm,),
        in_specs=[_rows(tm, a.shape[1]) for a in acts] + [_mat(*w) for w in weights]
                 + [_rows(tm, d), _full((1, d)), _rows(tm, d), _full(dep.shape)],
        out_specs=[_rows(tm, d), _full((1, d))],
        out_shape=[jax.ShapeDtypeStruct((s, d), F32), jax.ShapeDtypeStruct((1, d), F32)],
        compiler_params=_params(),
    )(*acts, *[w[0] for w in weights], x, gain, dx, dep)


def tn_matmul(a, b, scale, name):
    s, n = a.shape
    k = b.shape[1]
    tn = _col_chunk(n)

    def body(a_ref, b_ref, o_ref):
        o_ref[...] = (scale * _dotg(a_ref[...], b_ref[...], TN)).astype(BF16)

    return pl.pallas_call(
        body, name=name, grid=(n // tn,),
        in_specs=[pl.BlockSpec((s, tn), lambda i: (0, i)), _full((s, k))],
        out_specs=pl.BlockSpec((tn, k), lambda i: (i, 0)),
        out_shape=jax.ShapeDtypeStruct((n, k), BF16),
        compiler_params=_params(),
    )(a, b)


def loss_grad(y, target, name):
    s, d = y.shape
    tm = _row_tile(s)

    def body(y_ref, t_ref, dy_ref, acc_ref):
        @pl.when(pl.program_id(0) == 0)
        def _():
            acc_ref[...] = jnp.zeros(acc_ref.shape, F32)

        err = y_ref[...] - t_ref[...]
        dy_ref[...] = err * (1.0 / d)
        e2 = err * err
        part = jnp.sum(e2.reshape(tm // 8, 8, d), axis=0)
        acc = part[:, 0:LANES]
        for c0 in range(LANES, d, LANES):
            acc = acc + part[:, c0:c0 + LANES]
        acc_ref[...] = acc_ref[...] + acc

    return pl.pallas_call(
        body, name=name, grid=(s // tm,),
        in_specs=[_rows(tm, d), _rows(tm, d)],
        out_specs=[_rows(tm, d), _full((8, LANES))],
        out_shape=[jax.ShapeDtypeStruct((s, d), F32), jax.ShapeDtypeStruct((8, LANES), F32)],
        compiler_params=_params(),
    )(y, target)


def _mesh_pos():
    return lax.axis_index("x"), lax.axis_index("y"), lax.axis_index("c")


def gather_weights(shards):
    n = len(shards)

    def body(*refs):
        ins, outs = refs[:n], refs[n:2 * n]
        send_sems, recv_sems, local_sems = refs[2 * n:]
        x, y, c = _mesh_pos()
        me, sibling = (x, y, c), (x, y, 1 - c)
        chips = [(1 - x, y), (x, 1 - y), (1 - x, 1 - y)]

        def slot(a, px, py, pc):
            return outs[a].at[:, 4 * px + 2 * py + pc]

        def copy(a, k, block, to, src=None):
            return pltpu.make_async_remote_copy(
                src_ref=slot(a, *block) if src is None else src, dst_ref=slot(a, *block),
                send_sem=send_sems.at[a, k], recv_sem=recv_sems.at[a, k], device_id=to, device_id_type=MESH)

        mine = [pltpu.make_async_copy(ins[a], slot(a, *me), local_sems.at[a]) for a in range(n)]
        for cp in mine:
            cp.start()
        first = []
        for a in range(n):
            first.append(copy(a, 0, me, sibling, src=ins[a]))
            first += [copy(a, 1 + j, me, (*chip, c), src=ins[a]) for j, chip in enumerate(chips)]
        for cp in first:
            cp.start()
        passed = []
        for j, chip in enumerate(chips):
            for a in range(n):
                copy(a, 1 + j, (*chip, c), me).wait_recv()
                fwd = copy(a, 4 + j, (*chip, c), sibling)
                fwd.start()
                passed.append(fwd)
        for a in range(n):
            copy(a, 0, sibling, me).wait_recv()
            for j, chip in enumerate(chips):
                copy(a, 4 + j, (*chip, 1 - c), me).wait_recv()
        for cp in first + passed:
            cp.wait_send()
        for cp in mine:
            cp.wait()

    any_spec = pl.BlockSpec(memory_space=pl.ANY)
    return pl.pallas_call(
        body, name="gather_weights",
        in_specs=[any_spec] * n, out_specs=[any_spec] * n,
        out_shape=[jax.ShapeDtypeStruct((w.shape[0], N_DEV) + w.shape[1:], w.dtype) for w in shards],
        scratch_shapes=[pltpu.SemaphoreType.DMA((n, 7)), pltpu.SemaphoreType.DMA((n, 7)),
                        pltpu.SemaphoreType.DMA((n,))],
        compiler_params=pltpu.CompilerParams(has_side_effects=True),
    )(*shards)


def _peers():
    x, y, c = _mesh_pos()
    peers = []
    for rel in range(1, N_DEV):
        peers.append((1 - x if rel & 4 else x, 1 - y if rel & 2 else y, 1 - c if rel & 1 else c))
    return 4 * x + 2 * y + c, peers


HBM_SPEC = pl.BlockSpec(memory_space=pltpu.HBM)
SEM_SPEC = pl.BlockSpec(memory_space=pltpu.SEMAPHORE)


def _split_call(body, name, thru, n_sems, extra=(), with_token=True):
    hbm = lambda t: pltpu.with_memory_space_constraint(t, pltpu.HBM)
    effect = pltpu.CompilerParams(has_side_effects=pltpu.SideEffectType.DATAFLOW_SIDE_EFFECTING)
    nt = len(thru)
    thru_shapes = [pltpu.HBM(t.shape, t.dtype) for t in thru]
    if with_token:
        outs = pl.pallas_call(
            body, name=name, in_specs=[HBM_SPEC] * nt,
            out_specs=[SEM_SPEC] * len(n_sems) + [HBM_SPEC] * nt + [pl.BlockSpec(memory_space=pltpu.VMEM)],
            out_shape=[pltpu.SemaphoreType.DMA((k,)) for k in n_sems] + thru_shapes
                      + [jax.ShapeDtypeStruct((8, LANES), F32)],
            input_output_aliases={i: len(n_sems) + i for i in range(nt)}, compiler_params=effect,
        )(*[hbm(t) for t in thru])
        return outs[:len(n_sems)], outs[len(n_sems):-1], outs[-1]
    return pl.pallas_call(
        body, name=name,
        in_specs=[HBM_SPEC] * nt + [SEM_SPEC] * len(n_sems) + [pl.BlockSpec(memory_space=pl.ANY)],
        out_specs=[HBM_SPEC] * nt, out_shape=thru_shapes,
        input_output_aliases={i: i for i in range(nt)}, compiler_params=effect,
    )(*thru, *extra)


def _gather_targets():
    x, y, c = _mesh_pos()
    return 4 * x + 2 * y + c, [(x, y, 1 - c), (1 - x, y, c), (x, 1 - y, c), (1 - x, 1 - y, c)]


def gather_start(shards, name):
    n = len(shards)
    zones = [lax.empty((w.shape[0], N_DEV) + w.shape[1:], w.dtype) for w in shards]

    def body(*refs):
        ins, zs = refs[:n], refs[n:2 * n]
        send_sems, recv_sems, local_sems = refs[2 * n:2 * n + 3]
        token = refs[-1]
        me, targets = _gather_targets()
        for a in range(n):
            pltpu.make_async_copy(ins[a], zs[a].at[:, me], local_sems.at[a]).start()
            for k, to in enumerate(targets):
                pltpu.make_async_remote_copy(
                    src_ref=ins[a], dst_ref=zs[a].at[:, me], send_sem=send_sems.at[4 * a + k],
                    recv_sem=recv_sems.at[4 * a + k], device_id=to, device_id_type=MESH).start()
        token[...] = jnp.zeros(token.shape, F32)

    sems, thru, token = _split_call(body, name, list(shards) + zones, (4 * n, 4 * n, n))
    return (sems, thru, n), token


def gather_wait(started, after, name):
    sems, thru, n = started

    def body(*refs):
        zs = refs[n:2 * n]
        send_sems, recv_sems, local_sems = refs[2 * n:2 * n + 3]
        _, targets = _gather_targets()
        for a in range(n):
            for k, to in enumerate(targets):
                cp = pltpu.make_async_remote_copy(
                    src_ref=zs[a].at[:, 0], dst_ref=zs[a].at[:, 0], send_sem=send_sems.at[4 * a + k],
                    recv_sem=recv_sems.at[4 * a + k], device_id=to, device_id_type=MESH)
                cp.wait_send()
                cp.wait_recv()
            pltpu.make_async_copy(zs[a].at[:, 0], zs[a].at[:, 0], local_sems.at[a]).wait()

    return _split_call(body, name, thru, (4 * n, 4 * n, n), extra=(*sems, after), with_token=False)[n:]


def forward_start(zones, name):
    n = len(zones)

    def body(*refs):
        zs = refs[:n]
        send_sems, recv_sems = refs[n:n + 2]
        token = refs[-1]
        x, y, c = _mesh_pos()
        for a in range(n):
            for j, chip in enumerate([(1 - x, y), (x, 1 - y), (1 - x, 1 - y)]):
                blk = zs[a].at[:, 4 * chip[0] + 2 * chip[1] + c]
                pltpu.make_async_remote_copy(
                    src_ref=blk, dst_ref=blk, send_sem=send_sems.at[3 * a + j], recv_sem=recv_sems.at[3 * a + j],
                    device_id=(x, y, 1 - c), device_id_type=MESH).start()
        token[...] = jnp.zeros(token.shape, F32)

    sems, thru, token = _split_call(body, name, list(zones), (3 * n, 3 * n))
    return (sems, thru, n), token


def forward_wait(started, after, name):
    sems, thru, n = started

    def body(*refs):
        zs = refs[:n]
        send_sems, recv_sems = refs[n:n + 2]
        x, y, c = _mesh_pos()
        for a in range(n):
            for j in range(3):
                cp = pltpu.make_async_remote_copy(
                    src_ref=zs[a].at[:, 0], dst_ref=zs[a].at[:, 0], send_sem=send_sems.at[3 * a + j],
                    recv_sem=recv_sems.at[3 * a + j], device_id=(x, y, 1 - c), device_id_type=MESH)
                cp.wait_send()
                cp.wait_recv()

    return _split_call(body, name, thru, (3 * n, 3 * n), extra=(*sems, after), with_token=False)


def scatter_start(groups, name):
    n = len(groups)
    flat = [g for grp in groups for g in grp]
    nf = len(flat)
    offs = np.cumsum([0] + [len(grp) for grp in groups])
    lands = [lax.empty((N_DEV, len(grp)) + grp[0].shape[1:], grp[0].dtype) for grp in groups]

    def body(*refs):
        ins, zones = refs[:nf], refs[nf:nf + n]
        send_sems, recv_sems, local_sems = refs[nf + n:nf + n + 3]
        token = refs[-1]
        me, peers = _peers()
        for a in range(n):
            for w in range(len(groups[a])):
                pltpu.make_async_copy(ins[offs[a] + w].at[me], zones[a].at[me, w], local_sems.at[a]).start()
        for k, peer in enumerate(peers):
            p_id = 4 * peer[0] + 2 * peer[1] + peer[2]
            for a in range(n):
                for w in range(len(groups[a])):
                    pltpu.make_async_remote_copy(
                        src_ref=ins[offs[a] + w].at[p_id], dst_ref=zones[a].at[me, w],
                        send_sem=send_sems.at[7 * a + k], recv_sem=recv_sems.at[7 * a + k],
                        device_id=peer, device_id_type=MESH).start()
        token[...] = jnp.zeros(token.shape, F32)

    hbm = lambda t: pltpu.with_memory_space_constraint(t, pltpu.HBM)
    outs = pl.pallas_call(
        body, name=name,
        in_specs=[HBM_SPEC] * (nf + n),
        out_specs=[SEM_SPEC] * 3 + [HBM_SPEC] * (nf + n) + [pl.BlockSpec(memory_space=pltpu.VMEM)],
        out_shape=[pltpu.SemaphoreType.DMA((7 * n,)), pltpu.SemaphoreType.DMA((7 * n,)), pltpu.SemaphoreType.DMA((n,))]
                  + [pltpu.HBM(t.shape, t.dtype) for t in flat + lands]
                  + [jax.ShapeDtypeStruct((8, LANES), F32)],
        input_output_aliases={i: 3 + i for i in range(nf + n)},
        compiler_params=pltpu.CompilerParams(has_side_effects=pltpu.SideEffectType.DATAFLOW_SIDE_EFFECTING),
    )(*[hbm(t) for t in flat], *[hbm(t) for t in lands])
    sems, thru, token = outs[:3], outs[3:3 + nf + n], outs[-1]
    return (sems, thru, [len(grp) for grp in groups]), token


def scatter_wait(started, after, name):
    (send_sems, recv_sems, local_sems), thru, sizes = started
    n = len(sizes)
    nf = len(thru) - n

    def body(*refs):
        zones = refs[nf:nf + n]
        s_sems, r_sems, l_sems = refs[nf + n:nf + n + 3]
        me, peers = _peers()
        for a in range(n):
            for k, peer in enumerate(peers):
                cp = pltpu.make_async_remote_copy(
                    src_ref=zones[a].at[0], dst_ref=zones[a].at[0],
                    send_sem=s_sems.at[7 * a + k], recv_sem=r_sems.at[7 * a + k], device_id=peer,
                    device_id_type=MESH)
                cp.wait_send()
                cp.wait_recv()
            pltpu.make_async_copy(zones[a].at[0], zones[a].at[0], l_sems.at[a]).wait()

    outs = pl.pallas_call(
        body, name=name,
        in_specs=[HBM_SPEC] * (nf + n) + [SEM_SPEC] * 3 + [pl.BlockSpec(memory_space=pl.ANY)],
        out_specs=[HBM_SPEC] * (nf + n),
        out_shape=[pltpu.HBM(t.shape, t.dtype) for t in thru],
        input_output_aliases={i: i for i in range(nf + n)},
        compiler_params=pltpu.CompilerParams(has_side_effects=pltpu.SideEffectType.DATAFLOW_SIDE_EFFECTING),
    )(*thru, send_sems, recv_sems, local_sems, after)
    return outs[nf:]


def share_small(small):
    def body(s_ref, o_ref, send_sems, recv_sems, local_sem):
        me, peers = _peers()
        mine = pltpu.make_async_copy(s_ref, o_ref.at[me], local_sem)
        mine.start()
        copies = [pltpu.make_async_remote_copy(src_ref=s_ref, dst_ref=o_ref.at[me], send_sem=send_sems.at[k],
                                               recv_sem=recv_sems.at[k], device_id=peer, device_id_type=MESH)
                  for k, peer in enumerate(peers)]
        for cp in copies:
            cp.start()
        for cp in copies:
            cp.wait()
        mine.wait()

    vm = pl.BlockSpec(memory_space=pltpu.VMEM)
    return pl.pallas_call(
        body, name="share_small", in_specs=[vm], out_specs=vm,
        out_shape=jax.ShapeDtypeStruct((N_DEV,) + small.shape, small.dtype),
        scratch_shapes=[pltpu.SemaphoreType.DMA((7,)), pltpu.SemaphoreType.DMA((7,)), pltpu.SemaphoreType.DMA],
    )(small)


def sum_sources(recv, name):
    _, w, r, c = recv.shape

    def body(r_ref, o_ref):
        acc = r_ref[0, 0].astype(F32)
        for src in range(1, N_DEV):
            acc = acc + r_ref[src, 0].astype(F32)
        o_ref[0] = acc

    return pl.pallas_call(
        body, name=name, grid=(w,),
        in_specs=[pl.BlockSpec((N_DEV, 1, r, c), lambda i: (0, i, 0, 0))],
        out_specs=pl.BlockSpec((1, r, c), lambda i: (i, 0, 0)),
        out_shape=jax.ShapeDtypeStruct((w, r, c), F32),
        compiler_params=_params(),
    )(recv)


def _adamw_math(w, g, m, v):
    m = ADAM_B1 * m + (1.0 - ADAM_B1) * g
    v = ADAM_B2 * v + (1.0 - ADAM_B2) * (g * g)
    m_hat = m / (1.0 - ADAM_B1 ** ADAM_STEP)
    v_hat = v / (1.0 - ADAM_B2 ** ADAM_STEP)
    delta = -ADAM_LR * (m_hat / (jnp.sqrt(v_hat) + ADAM_EPS) + ADAM_WD * w)
    return delta, m, v


def adamw(w, g, m, v, name):
    shape = w.shape
    c = shape[-1]
    r = int(np.prod(shape[:-1]))
    w2, g2, m2, v2 = (t.reshape(r, c) for t in (w, g, m, v))
    tr = next(t for t in range(min(r, 512), 0, -1) if r % t == 0 and (t % 8 == 0 or t == r))

    def body(w_ref, g_ref, m_ref, v_ref, d_ref, mo_ref, vo_ref):
        d_ref[...], mo_ref[...], vo_ref[...] = _adamw_math(w_ref[...], g_ref[...], m_ref[...], v_ref[...])

    spec = pl.BlockSpec((tr, c), lambda i: (i, 0))
    outs = pl.pallas_call(
        body, name=name, grid=(r // tr,),
        in_specs=[spec] * 4, out_specs=[spec] * 3,
        out_shape=[jax.ShapeDtypeStruct((r, c), F32)] * 3,
        compiler_params=_params(),
    )(w2, g2, m2, v2)
    return tuple(t.reshape(shape) for t in outs)


def adamw_small(w, recv, m, v, name):
    def body(w_ref, r_ref, m_ref, v_ref, g_ref, d_ref, mo_ref, vo_ref):
        g = r_ref[0]
        for src in range(1, N_DEV):
            g = g + r_ref[src]
        g_ref[...] = g
        d_ref[...], mo_ref[...], vo_ref[...] = _adamw_math(w_ref[...], g, m_ref[...], v_ref[...])

    vm = pl.BlockSpec(memory_space=pltpu.VMEM)
    return pl.pallas_call(
        body, name=name, in_specs=[vm] * 4, out_specs=[vm] * 4,
        out_shape=[jax.ShapeDtypeStruct(w.shape, F32)] * 4,
        compiler_params=pltpu.CompilerParams(vmem_limit_bytes=V7X_VMEM_LIMIT),
    )(w, recv, m, v)


SMALL_NAMES = ("ffn1_norm", "mix_norm", "ffn2_norm", "b_gate", "na_q_norm", "na_k_norm", "sw_q_norm", "sw_k_norm",
               "na_rpb", "sw_sink", "t5_rel_table")


def _pack_small(parts):
    flat = jnp.concatenate([parts[k].reshape(-1).astype(F32) for k in SMALL_NAMES])
    n = flat.shape[0]
    rows = -(-n // (8 * LANES)) * 8
    return jnp.pad(flat, (0, rows * LANES - n)).reshape(rows, LANES)


def _unpack_small(packed, like):
    flat = packed.reshape(-1)
    out, off = {}, 0
    for k in SMALL_NAMES:
        n = int(np.prod(like[k].shape))
        out[k] = flat[off:off + n].reshape(like[k].shape)
        off += n
    return out


def kernel(x, ffn1_norm, ffn1_w_gate, ffn1_w_up, ffn1_w_down, mix_norm, w_in, b_gate, na_q_norm, na_k_norm, na_rpb, sw_q_norm, sw_k_norm, sw_sink, t5_rel_table, w_branch_na, w_branch_sw, w_out, ffn2_norm, ffn2_w_gate, ffn2_w_up, ffn2_w_down, loss_target, m_ffn1_norm, m_ffn1_w_gate, m_ffn1_w_up, m_ffn1_w_down, m_mix_norm, m_w_in, m_b_gate, m_na_q_norm, m_na_k_norm, m_na_rpb, m_sw_q_norm, m_sw_k_norm, m_sw_sink, m_t5_rel_table, m_w_branch_na, m_w_branch_sw, m_w_out, m_ffn2_norm, m_ffn2_w_gate, m_ffn2_w_up, m_ffn2_w_down, v_ffn1_norm, v_ffn1_w_gate, v_ffn1_w_up, v_ffn1_w_down, v_mix_norm, v_w_in, v_b_gate, v_na_q_norm, v_na_k_norm, v_na_rpb, v_sw_q_norm, v_sw_k_norm, v_sw_sink, v_t5_rel_table, v_w_branch_na, v_w_branch_sw, v_w_out, v_ffn2_norm, v_ffn2_w_gate, v_ffn2_w_up, v_ffn2_w_down):
    weights = dict(ffn1_norm=ffn1_norm, ffn1_w_gate=ffn1_w_gate, ffn1_w_up=ffn1_w_up, ffn1_w_down=ffn1_w_down,
                   mix_norm=mix_norm, w_in=w_in, b_gate=b_gate, na_q_norm=na_q_norm, na_k_norm=na_k_norm,
                   na_rpb=na_rpb, sw_q_norm=sw_q_norm, sw_k_norm=sw_k_norm, sw_sink=sw_sink,
                   t5_rel_table=t5_rel_table, w_branch_na=w_branch_na, w_branch_sw=w_branch_sw, w_out=w_out,
                   ffn2_norm=ffn2_norm, ffn2_w_gate=ffn2_w_gate, ffn2_w_up=ffn2_w_up, ffn2_w_down=ffn2_w_down)
    mom_m = dict(ffn1_norm=m_ffn1_norm, ffn1_w_gate=m_ffn1_w_gate, ffn1_w_up=m_ffn1_w_up, ffn1_w_down=m_ffn1_w_down,
                 mix_norm=m_mix_norm, w_in=m_w_in, b_gate=m_b_gate, na_q_norm=m_na_q_norm, na_k_norm=m_na_k_norm,
                 na_rpb=m_na_rpb, sw_q_norm=m_sw_q_norm, sw_k_norm=m_sw_k_norm, sw_sink=m_sw_sink,
                 t5_rel_table=m_t5_rel_table, w_branch_na=m_w_branch_na, w_branch_sw=m_w_branch_sw, w_out=m_w_out,
                 ffn2_norm=m_ffn2_norm, ffn2_w_gate=m_ffn2_w_gate, ffn2_w_up=m_ffn2_w_up, ffn2_w_down=m_ffn2_w_down)
    mom_v = dict(ffn1_norm=v_ffn1_norm, ffn1_w_gate=v_ffn1_w_gate, ffn1_w_up=v_ffn1_w_up, ffn1_w_down=v_ffn1_w_down,
                 mix_norm=v_mix_norm, w_in=v_w_in, b_gate=v_b_gate, na_q_norm=v_na_q_norm, na_k_norm=v_na_k_norm,
                 na_rpb=v_na_rpb, sw_q_norm=v_sw_q_norm, sw_k_norm=v_sw_k_norm, sw_sink=v_sw_sink,
                 t5_rel_table=v_t5_rel_table, w_branch_na=v_w_branch_na, w_branch_sw=v_w_branch_sw, w_out=v_w_out,
                 ffn2_norm=v_ffn2_norm, ffn2_w_gate=v_ffn2_w_gate, ffn2_w_up=v_ffn2_w_up, ffn2_w_down=v_ffn2_w_down)
    order = list(weights)

    depth = ffn1_norm.shape[0]
    s, d = x.shape[1], x.shape[2]
    xs = x[0]
    tr = lambda w: jnp.swapaxes(w, -1, -2)

    a_loc = jnp.stack([t for l in range(depth) for t in (
        tr(ffn1_w_gate[l]), tr(ffn1_w_up[l]), ffn1_w_down[l],
        tr(ffn2_w_gate[l]), tr(ffn2_w_up[l]), ffn2_w_down[l])]).astype(BF16)
    b_loc = tr(w_in).astype(BF16)
    c_loc = w_out.astype(BF16)
    d_loc = jnp.stack([t for l in range(depth) for t in (tr(w_branch_na[l]), tr(w_branch_sw[l]))]).astype(BF16)
    merge = lambda t: t.reshape(t.shape[0], N_DEV * t.shape[2], t.shape[3])
    no_dep = jnp.zeros((8, LANES), F32)

    first = merge(gather_weights([a_loc[0:3]])[0])
    st_win0, tok_a = gather_start([b_loc[0:1]], "gather_win_0")
    st_rest0, tok_b = gather_start([a_loc[3:6], c_loc[0:1], d_loc[0:2]], "gather_rest_0")
    st_layer, tok_l = {}, [tok_a, tok_b]
    for l in range(1, depth):
        st_layer[l], tok = gather_start(
            [a_loc[6 * l:6 * l + 6], b_loc[l:l + 1], c_loc[l:l + 1], d_loc[2 * l:2 * l + 2]], f"gather_layer_{l}")
        tok_l.append(tok)
    dep = functools.reduce(jnp.add, tok_l)

    bd = jnp.asarray(np.kron(np.eye(NA_WIDTH // HEAD_DIM), np.full((HEAD_DIM, HEAD_DIM), 1.0 / HEAD_DIM)), BF16)
    bmap = jnp.asarray(_t5_bucket_map())
    tile8 = lambda g: jnp.tile(g, NA_WIDTH // HEAD_DIM).reshape(1, NA_WIDTH)
    tile2 = lambda g: jnp.tile(g, SW_KV_WIDTH // HEAD_DIM).reshape(1, SW_KV_WIDTH)
    t5b = t5_expand(t5_rel_table, bmap, "t5_expand")

    saved = []
    layer_w = {0: dict(wg1=(first, 0), wu1=(first, 1), wd1=(first, 2))}
    cur = xs
    for l in range(depth):
        sv = {}
        lw = layer_w[l]
        sv["x0"] = cur
        sv["xn1"], sv["hg1"], sv["hu1"], sv["act1"] = ffn_up(cur, ffn1_norm[l][None], lw["wg1"], lw["wu1"], dep,
                                                             f"ffn1_up_{l}")
        cur = ffn_down(cur, sv["act1"], lw["wd1"], f"ffn1_down_{l}")
        sv["x1"] = cur
        if l == 0:
            zones = gather_wait(st_win0, cur, "gather_win_0_wait")
            fwd, tok = forward_start(zones, "forward_win_0")
            lw["win"] = (merge(forward_wait(fwd, tok, "forward_win_0_wait")[0]), 0)
        sv["gains"] = (tile8(na_q_norm[l]), tile8(na_k_norm[l]), tile8(sw_q_norm[l]), tile2(sw_k_norm[l]))
        sv["hn"], sv["zq"], sv["qa"], sv["ka"], sv["qs"], sv["ks"], sv["gt"] = mix_in(
            cur, mix_norm[l][None], lw["win"], b_gate[l][None], *sv["gains"], bd, f"mix_in_{l}")
        dep = no_dep
        if l == 0:
            zones = gather_wait(st_rest0, sv["hn"], "gather_rest_0_wait")
            fwd, dep = forward_start(zones, "forward_rest_0")
        sv["t2"] = rpb_expand(na_rpb[l].reshape(-1), na_rpb.shape[1], dep, f"rpb_expand_{l}")
        sv["o_na"] = na_fwd(sv["qa"], sv["ka"], sv["zq"], sv["t2"], f"na_fwd_{l}")
        sv["o_sw"] = sw_fwd(sv["qs"], sv["ks"], sv["zq"], t5b, sw_sink[l], f"sw_fwd_{l}")
        if l == 0:
            za, zc, zd = (merge(z) for z in forward_wait(fwd, sv["o_sw"], "forward_rest_0_wait"))
            lw.update(wg2=(za, 0), wu2=(za, 1), wd2=(za, 2), wout=(zc, 0), wna=(zd, 0), wsw=(zd, 1))
        cur, sv["a_na"], sv["a_sw"], sv["merged"] = merge_out(
            cur, sv["o_na"], sv["o_sw"], sv["gt"], lw["wna"], lw["wsw"], lw["wout"], f"merge_out_{l}")
        sv["x2"] = cur
        dep = no_dep
        if l + 1 < depth:
            zones = gather_wait(st_layer[l + 1], cur, f"gather_layer_{l + 1}_wait")
            fwd, dep = forward_start(zones, f"forward_layer_{l + 1}")
        sv["xn2"], sv["hg2"], sv["hu2"], sv["act2"] = ffn_up(cur, ffn2_norm[l][None], lw["wg2"], lw["wu2"], dep,
                                                             f"ffn2_up_{l}")
        cur = ffn_down(cur, sv["act2"], lw["wd2"], f"ffn2_down_{l}")
        if l + 1 < depth:
            za, zb, zc, zd = (merge(z) for z in forward_wait(fwd, cur, f"forward_layer_{l + 1}_wait"))
            layer_w[l + 1] = dict(wg1=(za, 0), wu1=(za, 1), wd1=(za, 2), wg2=(za, 3), wu2=(za, 4), wd2=(za, 5),
                                  win=(zb, 0), wout=(zc, 0), wna=(zd, 0), wsw=(zd, 1))
        dep = no_dep
        saved.append(sv)

    dx, loss_acc = loss_grad(cur, loss_target[0], "loss_grad")
    loss = lax.psum(jnp.sum(loss_acc) * (0.5 / d), ("x", "y", "c"))

    split = lambda t: t.reshape(N_DEV, t.shape[0] // N_DEV, t.shape[1])
    pending = {}
    small = {k: [None] * depth for k in SMALL_NAMES if k != "t5_rel_table"}
    dbias_sw = []
    for l in reversed(range(depth)):
        sv = saved[l]
        lw = layer_w[l]
        wg1, wu1, wd1, wg2, wu2, wd2 = (lw[k] for k in ("wg1", "wu1", "wd1", "wg2", "wu2", "wd2"))
        win_t, wout_l, wna_t, wsw_t = lw["win"], lw["wout"], lw["wna"], lw["wsw"]
        blocks = ((2, "x2", "xn2", "hg2", "hu2", "act2", wg2, wu2, wd2, "ffn2_norm", 3),
                  (1, "x0", "xn1", "hg1", "hu1", "act1", wg1, wu1, wd1, "ffn1_norm", 0))

        def ffn_backward(dx, blk):
            tag, xk, xnk, hgk, huk, actk, wg, wu, wd, norm_name, slot = blk
            gains = weights[norm_name]
            dxb, dhg, dhu = ffn_bwd_act(dx, wd, sv[hgk], sv[huk], f"ffn{tag}_bwd_act_{l}")
            gwd = tn_matmul(sv[actk], dxb, 0.5, f"ffn{tag}_dwd_{l}")
            gwg = tn_matmul(dhg, sv[xnk], 1.0, f"ffn{tag}_dwg_{l}")
            gwu = tn_matmul(dhu, sv[xnk], 1.0, f"ffn{tag}_dwu_{l}")
            pending[f"ffn{tag}_{l}"], token = scatter_start([[split(gwg), split(gwu), split(gwd)]],
                                                            f"scatter_ffn{tag}_{l}")
            dx, dg = proj_bwd_norm([dhg, dhu], [wg, wu], sv[xk], gains[l][None], dx, token, f"ffn{tag}_bwd_x_{l}")
            small[norm_name][l] = dg[0]
            return dx

        dx = ffn_backward(dx, blocks[0])
        dxb, dzg, da_na, da_sw, do_na, do_sw, dbg = mix_bwd_out(
            dx, sv["gt"], sv["a_na"], sv["a_sw"], wna_t, wsw_t, wout_l, f"mix_bwd_out_{l}")
        small["b_gate"][l] = dbg[0]
        gwout = tn_matmul(sv["merged"], dxb, 1.0, f"dwout_{l}")
        gwna = tn_matmul(da_na, sv["o_na"], 1.0, f"dwna_{l}")
        gwsw = tn_matmul(da_sw, sv["o_sw"], 1.0, f"dwsw_{l}")
        dqa, dka, dva, dt2 = na_bwd(sv["qa"], sv["ka"], sv["zq"], sv["t2"], sv["o_na"], do_na, f"na_bwd_{l}")
        dqs, dks, dvs, dbias, dsink = sw_bwd(sv["qs"], sv["ks"], sv["zq"], t5b, sw_sink[l], sv["o_sw"], do_sw,
                                             f"sw_bwd_{l}")
        dbias_sw.append(dbias)
        small["sw_sink"][l] = jnp.sum(dsink[:, :, 0], axis=1)
        drpb = rpb_reduce(dt2, f"rpb_reduce_{l}")
        small["na_rpb"][l] = drpb[:, :, :2 * NA_COLS - 1, 0]
        dz, dgqa, dgka, dgqs, dgks = qk_norm_bwd(dqa, dka, dva, dqs, dks, dvs, sv["zq"], dzg, *sv["gains"], bd,
                                                 f"qk_norm_bwd_{l}")
        fold = lambda g: jnp.sum(g.reshape(-1, HEAD_DIM), axis=0)
        small["na_q_norm"][l], small["na_k_norm"][l] = fold(dgqa), fold(dgka)
        small["sw_q_norm"][l], small["sw_k_norm"][l] = fold(dgqs), fold(dgks)
        gwin = tn_matmul(dz, sv["hn"], 1.0, f"dwin_{l}")
        pending[f"mix_{l}"], token = scatter_start([[split(gwout)], [split(gwna), split(gwsw)], [split(gwin)]],
                                                   f"scatter_mix_{l}")
        dx, dg = proj_bwd_norm([dz], [win_t], sv["x1"], mix_norm[l][None], dx, token, f"mix_bwd_x_{l}")
        small["mix_norm"][l] = dg[0]
        dx = ffn_backward(dx, blocks[1])

    dtab = t5_reduce(dbias_sw, bmap, "t5_reduce")
    small_parts = {k: jnp.stack(v) for k, v in small.items()}
    small_parts["t5_rel_table"] = jnp.transpose(dtab[:, :, 0])
    small_packed = _pack_small(small_parts)

    rs_ = share_small(small_packed)
    summed = {}
    for key, started in pending.items():
        zones = scatter_wait(started, dx, f"wait_{key}")
        summed[key] = [sum_sources(z, f"sum_{key}_{i}") for i, z in enumerate(zones)]

    layers = lambda f: jnp.stack([f(l) for l in range(depth)])
    own = {}
    for tag in (1, 2):
        own[f"ffn{tag}_w_gate"] = (layers(lambda l: summed[f"ffn{tag}_{l}"][0][0]), True)
        own[f"ffn{tag}_w_up"] = (layers(lambda l: summed[f"ffn{tag}_{l}"][0][1]), True)
        own[f"ffn{tag}_w_down"] = (layers(lambda l: summed[f"ffn{tag}_{l}"][0][2]), False)
    own["w_out"] = (layers(lambda l: summed[f"mix_{l}"][0][0]), False)
    own["w_branch_na"] = (layers(lambda l: summed[f"mix_{l}"][1][0]), True)
    own["w_branch_sw"] = (layers(lambda l: summed[f"mix_{l}"][1][1]), True)
    own["w_in"] = (layers(lambda l: summed[f"mix_{l}"][2][0]), True)

    grads, delta, new_m, new_v = {}, {}, {}, {}
    for k, (g, transposed) in own.items():
        view = tr if transposed else (lambda t: t)
        d_k, m_k, v_k = adamw(view(weights[k]), g, view(mom_m[k]), view(mom_v[k]), f"adamw_{k}")
        grads[k], delta[k], new_m[k], new_v[k] = view(g), view(d_k), view(m_k), view(v_k)
    g_s, d_s, m_s, v_s = adamw_small(_pack_small(weights), rs_, _pack_small(mom_m), _pack_small(mom_v), "adamw_small")
    for dst, packed in ((grads, g_s), (delta, d_s), (new_m, m_s), (new_v, v_s)):
        dst.update(_unpack_small(packed, weights))

    return (loss, dx[None], *[grads[k] for k in order], *[delta[k] for k in order],
            *[new_m[k] for k in order], *[new_v[k] for k in order])
```

```python
import functools
import math

import numpy as np
import jax
import jax.numpy as jnp
from jax import lax
from jax.experimental import pallas as pl
from jax.experimental.pallas import tpu as pltpu

F32 = jnp.float32
BF16 = jnp.bfloat16
MESH = pl.DeviceIdType.MESH

N_DEV = 8
EPS = 1e-6
NEG = -1e30
HEAD_DIM = 64
GRID_W = 64
NA_ROWS = 8
NA_COLS = 16
NA_WIDTH = 512
SW_Q_WIDTH = 512
SW_KV_WIDTH = 128
SW_BLOCK = 128
SW_HEADS = 8
SW_REP = 4
REL_BUCKETS = 32
REL_MAX_DIST = 128
QKV_WIDTH = 3 * NA_WIDTH + SW_Q_WIDTH + 2 * SW_KV_WIDTH
SCALE = 1.0 / math.sqrt(HEAD_DIM)

ADAM_LR = 0.001
ADAM_B1 = 0.9
ADAM_B2 = 0.999
ADAM_EPS = 1e-08
ADAM_WD = 0.01
ADAM_STEP = 10

V7X_VMEM_LIMIT = 56 * 1024 * 1024
LANES = 128
MXU_TILE = 256

NT = (((1,), (1,)), ((), ()))
TN = (((0,), (0,)), ((), ()))


def _params(n_grid=1):
    return pltpu.CompilerParams(dimension_semantics=("arbitrary",) * n_grid,
                                vmem_limit_bytes=V7X_VMEM_LIMIT)


def _row_tile(s):
    for t in (256, 128, 64, 32, 16, 8):
        if s % t == 0:
            return t
    raise ValueError(s)


def _col_chunk(n):
    return MXU_TILE if n % MXU_TILE == 0 else n


def _dot(a, b):
    return jnp.dot(a, b, preferred_element_type=F32)


def _dotg(a, b, dn):
    return lax.dot_general(a, b, dn, preferred_element_type=F32)


def _sigmoid(v):
    return 1.0 / (1.0 + jnp.exp(-v))


def _rstd(xv):
    return lax.rsqrt(jnp.mean(xv * xv, axis=-1, keepdims=True) + EPS)


def _full(shape):
    nd = len(shape)
    return pl.BlockSpec(shape, lambda i, _n=nd: (0,) * _n)


def _rows(tm, width):
    return pl.BlockSpec((tm, width), lambda i: (i, 0))


def _mat(stack, idx):
    return pl.BlockSpec((None,) + tuple(stack.shape[1:]), lambda i, _w=idx: (_w, 0, 0))


def _group_mean(v, bd):
    hi = v.astype(BF16)
    lo = (v - hi.astype(F32)).astype(BF16)
    return _dot(hi, bd) + _dot(lo, bd)


def ffn_up(x, gain, wg_t, wu_t, dep, name):
    s, d = x.shape
    f = wg_t[0].shape[1]
    tm = _row_tile(s)
    fc = _col_chunk(f)

    def body(x_ref, g_ref, wg_ref, wu_ref, dep_ref, xn_ref, hg_ref, hu_ref, act_ref):
        xv = x_ref[...]
        xn = (xv * _rstd(xv) * g_ref[...]).astype(BF16)
        xn_ref[...] = xn
        for c0 in range(0, f, fc):
            hg = _dotg(xn, wg_ref[c0:c0 + fc, :], NT)
            hu = _dotg(xn, wu_ref[c0:c0 + fc, :], NT)
            hg_ref[:, c0:c0 + fc] = hg.astype(BF16)
            hu_ref[:, c0:c0 + fc] = hu.astype(BF16)
            act_ref[:, c0:c0 + fc] = (hg * _sigmoid(hg) * hu).astype(BF16)

    return pl.pallas_call(
        body, name=name, grid=(s // tm,),
        in_specs=[_rows(tm, d), _full((1, d)), _mat(*wg_t), _mat(*wu_t), _full(dep.shape)],
        out_specs=[_rows(tm, d), _rows(tm, f), _rows(tm, f), _rows(tm, f)],
        out_shape=[jax.ShapeDtypeStruct((s, d), BF16)] + [jax.ShapeDtypeStruct((s, f), BF16)] * 3,
        compiler_params=_params(),
    )(x, gain, wg_t[0], wu_t[0], dep)


def ffn_down(x, act, wd, dep, name):
    s, d = x.shape
    f = act.shape[1]
    tm = _row_tile(s)

    def body(x_ref, a_ref, w_ref, dep_ref, o_ref):
        o_ref[...] = x_ref[...] + 0.5 * _dot(a_ref[...], w_ref[...])

    return pl.pallas_call(
        body, name=name, grid=(s // tm,),
        in_specs=[_rows(tm, d), _rows(tm, f), _mat(*wd), _full(dep.shape)],
        out_specs=_rows(tm, d),
        out_shape=jax.ShapeDtypeStruct((s, d), F32),
        compiler_params=_params(),
    )(x, act, wd[0], dep)


def mix_in(x, gain, win_t, b_gate, gq_na, gk_na, gq_sw, gk_sw, bd, name):
    s, d = x.shape
    tm = _row_tile(s)
    gc = _col_chunk(2 * d)

    def body(x_ref, g_ref, w_ref, b_ref, gqa_ref, gka_ref, gqs_ref, gks_ref, bd_ref,
             hn_ref, zq_ref, qa_ref, ka_ref, qs_ref, ks_ref, gt_ref):
        xv = x_ref[...]
        hn = (xv * _rstd(xv) * g_ref[...]).astype(BF16)
        hn_ref[...] = hn

        def proj(c0, c1):
            return _dotg(hn, w_ref[c0:c1, :], NT)

        def headnorm(z, g, bdm):
            return z * lax.rsqrt(_group_mean(z * z, bdm) + EPS) * g

        bd512 = bd_ref[...]
        bd128 = bd_ref[0:SW_KV_WIDTH, 0:SW_KV_WIDTH]
        z = proj(0, 512)
        zq_ref[:, 0:512] = z.astype(BF16)
        qa_ref[...] = (headnorm(z, gqa_ref[...], bd512) * SCALE).astype(BF16)
        z = proj(512, 1024)
        zq_ref[:, 512:1024] = z.astype(BF16)
        ka_ref[...] = headnorm(z, gka_ref[...], bd512).astype(BF16)
        z = proj(1024, 1536)
        zq_ref[:, 1024:1536] = z.astype(BF16)
        z = proj(1536, 2048)
        zq_ref[:, 1536:2048] = z.astype(BF16)
        qs_ref[...] = (headnorm(z, gqs_ref[...], bd512) * SCALE).astype(BF16)
        z = proj(2048, 2176)
        zq_ref[:, 2048:2176] = z.astype(BF16)
        ks_ref[...] = headnorm(z, gks_ref[...], bd128).astype(BF16)
        z = proj(2176, 2304)
        zq_ref[:, 2176:2304] = z.astype(BF16)
        for c0 in range(0, 2 * d, gc):
            zg = proj(QKV_WIDTH + c0, QKV_WIDTH + c0 + gc) + b_ref[:, c0:c0 + gc]
            gt_ref[:, c0:c0 + gc] = _sigmoid(zg).astype(BF16)

    return pl.pallas_call(
        body, name=name, grid=(s // tm,),
        in_specs=[_rows(tm, d), _full((1, d)), _mat(*win_t), _full((1, 2 * d)),
                  _full((1, 512)), _full((1, 512)), _full((1, 512)), _full((1, 128)), _full((512, 512))],
        out_specs=[_rows(tm, d), _rows(tm, QKV_WIDTH), _rows(tm, 512), _rows(tm, 512), _rows(tm, 512),
                   _rows(tm, 128), _rows(tm, 2 * d)],
        out_shape=[jax.ShapeDtypeStruct((s, d), BF16), jax.ShapeDtypeStruct((s, QKV_WIDTH), BF16),
                   jax.ShapeDtypeStruct((s, 512), BF16), jax.ShapeDtypeStruct((s, 512), BF16),
                   jax.ShapeDtypeStruct((s, 512), BF16), jax.ShapeDtypeStruct((s, 128), BF16),
                   jax.ShapeDtypeStruct((s, 2 * d), BF16)],
        compiler_params=_params(),
    )(x, gain, win_t[0], b_gate, gq_na, gk_na, gq_sw, gk_sw, bd)


def _na_iotas():
    qc = lax.broadcasted_iota(jnp.int32, (GRID_W, LANES), 0)
    ln = lax.broadcasted_iota(jnp.int32, (GRID_W, LANES), 1)
    low = ln < GRID_W
    kc = jnp.where(low, ln, ln - GRID_W)
    diff = kc - qc + (NA_COLS - 1)
    qcs = jnp.clip(qc - NA_COLS // 2, 0, GRID_W - NA_COLS)
    inwin = (kc >= qcs) & (kc < qcs + NA_COLS)
    return diff, low, inwin


NA_RI = 2 * NA_ROWS - 1
NA_CI = 2 * NA_COLS - 1
NA_T2 = NA_RI + 1


def rpb_expand(rpb_flat, n_heads, dep, name):
    def body(rpb_ref, dep_ref, o_ref):
        diff, low, _ = _na_iotas()
        for h in range(n_heads):
            def one(e, carry, h=h):
                lo_row = jnp.maximum(e - 1, 0)
                hi_row = jnp.minimum(e, NA_RI - 1)
                lo_on = jnp.where(e >= 1, 1.0, 0.0)
                hi_on = jnp.where(e <= NA_RI - 1, 1.0, 0.0)
                t = jnp.zeros((GRID_W, LANES), F32)
                for c in range(NA_CI):
                    lo = rpb_ref[h * NA_RI * NA_CI + lo_row * NA_CI + c] * lo_on
                    hi = rpb_ref[h * NA_RI * NA_CI + hi_row * NA_CI + c] * hi_on
                    t = jnp.where(diff == c, jnp.where(low, lo, hi), t)
                o_ref[h, e] = t
                return carry
            lax.fori_loop(0, NA_T2, one, 0)

    return pl.pallas_call(
        body, name=name,
        in_specs=[pl.BlockSpec(memory_space=pltpu.SMEM), pl.BlockSpec(memory_space=pltpu.VMEM)],
        out_specs=pl.BlockSpec(memory_space=pltpu.VMEM),
        out_shape=jax.ShapeDtypeStruct((n_heads, NA_T2, GRID_W, LANES), F32),
        compiler_params=pltpu.CompilerParams(vmem_limit_bytes=V7X_VMEM_LIMIT),
    )(rpb_flat, dep)


def rpb_reduce(dt2, name):
    n_heads = dt2.shape[0]

    def body(d_ref, o_ref):
        diff, low, _ = _na_iotas()
        low32 = lax.broadcasted_iota(jnp.int32, (32, LANES), 1) < GRID_W
        o_ref[...] = jnp.zeros(o_ref.shape, F32)
        for h in range(n_heads):
            def one(e, carry, h=h):
                dv = d_ref[h, e]
                rows = [jnp.sum(jnp.where(diff == c, dv, 0.0), axis=0, keepdims=True) for c in range(NA_CI)]
                rows.append(jnp.zeros((1, LANES), F32))
                r = jnp.concatenate(rows, axis=0)
                lo = jnp.sum(jnp.where(low32, r, 0.0), axis=1, keepdims=True)
                hi = jnp.sum(jnp.where(low32, 0.0, r), axis=1, keepdims=True)
                lo_row = jnp.maximum(e - 1, 0)
                hi_row = jnp.minimum(e, NA_RI - 1)
                o_ref[h, lo_row] = o_ref[h, lo_row] + jnp.broadcast_to(lo, (32, LANES))
                o_ref[h, hi_row] = o_ref[h, hi_row] + jnp.broadcast_to(hi, (32, LANES))
                return carry
            lax.fori_loop(0, NA_T2, one, 0)

    return pl.pallas_call(
        body, name=name,
        in_specs=[pl.BlockSpec(memory_space=pltpu.VMEM)],
        out_specs=pl.BlockSpec(memory_space=pltpu.VMEM),
        out_shape=jax.ShapeDtypeStruct((n_heads, NA_RI, 32, LANES), F32),
        compiler_params=pltpu.CompilerParams(vmem_limit_bytes=V7X_VMEM_LIMIT),
    )(dt2)


NA_TQ = 4
NA_TK = NA_TQ + NA_ROWS
NA_KCH = NA_TK // 2


def _na_tile_geometry(t, rows):
    r = t * NA_TQ
    kbase = jnp.clip(r - NA_ROWS // 2, 0, rows - NA_TK)
    starts = [jnp.clip(r + a - NA_ROWS // 2, 0, rows - NA_ROWS) for a in range(NA_TQ)]
    return r, kbase, starts


def _na_tile_mask(kbase, starts, low, inwin):
    half = jnp.where(low, 0, 1)
    cols = []
    for c in range(NA_KCH):
        krow = kbase + 2 * c + half
        cols.append(jnp.concatenate(
            [jnp.where(inwin & (krow >= st) & (krow < st + NA_ROWS), 0.0, NEG) for st in starts], axis=0))
    return jnp.concatenate(cols, axis=1)


def _na_tile_index(r, kbase, a, c):
    return jnp.clip(kbase + 2 * c - (r + a) + NA_ROWS, 0, NA_T2 - 1)


def _na_tile_probs(q, k, t2_ref, hh, r, kbase, madd):
    bias = jnp.concatenate(
        [jnp.concatenate([t2_ref[hh, _na_tile_index(r, kbase, a, c)] for a in range(NA_TQ)], axis=0)
         for c in range(NA_KCH)], axis=1)
    sc = _dotg(q, k, NT) + bias + madd
    e = jnp.exp(sc - jnp.max(sc, axis=1, keepdims=True))
    return e * (1.0 / jnp.sum(e, axis=1, keepdims=True))


def na_fwd(qa, ka, zq, t2, name):
    s = qa.shape[0]
    rows = s // GRID_W
    n_pairs = NA_WIDTH // LANES
    v_blk0 = (2 * NA_WIDTH) // LANES

    assert rows % NA_TQ == 0 and rows >= NA_TK
    tq, tk = NA_TQ * GRID_W, NA_TK * GRID_W

    def body(q_ref, k_ref, v_ref, t2_ref, o_ref):
        _, low, inwin = _na_iotas()

        def tile(t, carry):
            r, kbase, starts = _na_tile_geometry(t, rows)
            madd = _na_tile_mask(kbase, starts, low, inwin)
            qr = pl.ds(pl.multiple_of(r * GRID_W, tq), tq)
            kr = pl.ds(pl.multiple_of(kbase * GRID_W, tq), tk)
            for hh in range(2):
                lanes = slice(HEAD_DIM * hh, HEAD_DIM * (hh + 1))
                p = _na_tile_probs(q_ref[qr, lanes], k_ref[kr, lanes], t2_ref, hh, r, kbase, madd)
                o_ref[qr, lanes] = _dot(p.astype(BF16), v_ref[kr, lanes]).astype(BF16)
            return carry

        lax.fori_loop(0, rows // NA_TQ, tile, 0)

    col = lambda off: pl.BlockSpec((s, LANES), lambda p, _o=off: (0, _o + p))
    return pl.pallas_call(
        body, name=name, grid=(n_pairs,),
        in_specs=[col(0), col(0), col(v_blk0),
                  pl.BlockSpec((2, NA_T2, GRID_W, LANES), lambda p: (p, 0, 0, 0))],
        out_specs=col(0),
        out_shape=jax.ShapeDtypeStruct((s, NA_WIDTH), BF16),
        compiler_params=_params(),
    )(qa, ka, zq, t2)


def na_bwd(qa, ka, zq, t2, o_na, do_na, name):
    s = qa.shape[0]
    rows = s // GRID_W
    n_pairs = NA_WIDTH // LANES
    v_blk0 = (2 * NA_WIDTH) // LANES

    tq, tk = NA_TQ * GRID_W, NA_TK * GRID_W

    def body(q_ref, k_ref, v_ref, t2_ref, o_ref, do_ref, dq_ref, dk_ref, dv_ref, dt2_ref):
        _, low, inwin = _na_iotas()
        dk_ref[...] = jnp.zeros(dk_ref.shape, F32)
        dv_ref[...] = jnp.zeros(dv_ref.shape, F32)
        dt2_ref[...] = jnp.zeros(dt2_ref.shape, F32)

        def tile(t, carry):
            r, kbase, starts = _na_tile_geometry(t, rows)
            madd = _na_tile_mask(kbase, starts, low, inwin)
            qr = pl.ds(pl.multiple_of(r * GRID_W, tq), tq)
            kr = pl.ds(pl.multiple_of(kbase * GRID_W, tq), tk)
            for hh in range(2):
                lanes = slice(HEAD_DIM * hh, HEAD_DIM * (hh + 1))
                q, k, v = q_ref[qr, lanes], k_ref[kr, lanes], v_ref[kr, lanes]
                p = _na_tile_probs(q, k, t2_ref, hh, r, kbase, madd)
                do = do_ref[qr, lanes]
                delta = jnp.sum(do.astype(F32) * o_ref[qr, lanes].astype(F32), axis=1, keepdims=True)
                ds = p * (_dotg(do, v, NT) - delta)
                for a in range(NA_TQ):
                    for c in range(NA_KCH):
                        e = _na_tile_index(r, kbase, a, c)
                        dt2_ref[hh, e] = dt2_ref[hh, e] + ds[GRID_W * a:GRID_W * (a + 1), LANES * c:LANES * (c + 1)]
                dsb = ds.astype(BF16)
                dq_ref[qr, lanes] = _dot(dsb, k)
                dk_ref[kr, lanes] = dk_ref[kr, lanes] + _dotg(dsb, q, TN)
                dv_ref[kr, lanes] = dv_ref[kr, lanes] + _dotg(p.astype(BF16), do, TN)
            return carry

        lax.fori_loop(0, rows // NA_TQ, tile, 0)

    col = lambda off: pl.BlockSpec((s, LANES), lambda p, _o=off: (0, _o + p))
    t2spec = pl.BlockSpec((2, NA_T2, GRID_W, LANES), lambda p: (p, 0, 0, 0))
    return pl.pallas_call(
        body, name=name, grid=(n_pairs,),
        in_specs=[col(0), col(0), col(v_blk0), t2spec, col(0), col(0)],
        out_specs=[col(0), col(0), col(0), t2spec],
        out_shape=[jax.ShapeDtypeStruct((s, NA_WIDTH), F32)] * 3 + [jax.ShapeDtypeStruct(t2.shape, F32)],
        compiler_params=_params(),
    )(qa, ka, zq, t2, o_na, do_na)


def _t5_bucket_map():
    rel = np.arange(3 * SW_BLOCK)[None, :] - SW_BLOCK - np.arange(SW_BLOCK)[:, None]
    nb = REL_BUCKETS // 2
    max_exact = nb // 2
    n = np.abs(rel)
    large = max_exact + (np.log(np.maximum(n, 1) / max_exact)
                         / np.log(REL_MAX_DIST / max_exact) * (nb - max_exact)).astype(np.int32)
    large = np.minimum(large, nb - 1)
    return ((rel > 0) * nb + np.where(n < max_exact, n, large)).astype(np.int32)


def t5_expand(table, bmap, name):
    def body(tab_ref, bm_ref, o_ref):
        bm = bm_ref[...]
        for h in range(SW_HEADS):
            t = jnp.zeros(bm.shape, F32)
            for b in range(REL_BUCKETS):
                t = jnp.where(bm == b, tab_ref[b, h], t)
            o_ref[h] = t

    return pl.pallas_call(
        body, name=name,
        in_specs=[pl.BlockSpec(memory_space=pltpu.SMEM), pl.BlockSpec(memory_space=pltpu.VMEM)],
        out_specs=pl.BlockSpec(memory_space=pltpu.VMEM),
        out_shape=jax.ShapeDtypeStruct((SW_HEADS,) + bmap.shape, F32),
        compiler_params=pltpu.CompilerParams(vmem_limit_bytes=V7X_VMEM_LIMIT),
    )(table, bmap)


def t5_reduce(dbias_list, bmap, name):
    n = len(dbias_list)

    def body(*refs):
        d_refs, bm_ref, o_ref = refs[:n], refs[n], refs[n + 1]
        bm = bm_ref[...]
        for h in range(SW_HEADS):
            dv = d_refs[0][h]
            for other in d_refs[1:]:
                dv = dv + other[h]
            rows = [jnp.sum(jnp.where(bm == b, dv, 0.0), axis=0, keepdims=True) for b in range(REL_BUCKETS)]
            r = jnp.concatenate(rows, axis=0)
            o_ref[h] = jnp.broadcast_to(jnp.sum(r, axis=1, keepdims=True), (REL_BUCKETS, LANES))

    return pl.pallas_call(
        body, name=name,
        in_specs=[pl.BlockSpec(memory_space=pltpu.VMEM)] * (n + 1),
        out_specs=pl.BlockSpec(memory_space=pltpu.VMEM),
        out_shape=jax.ShapeDtypeStruct((SW_HEADS, REL_BUCKETS, LANES), F32),
        compiler_params=pltpu.CompilerParams(vmem_limit_bytes=V7X_VMEM_LIMIT),
    )(*dbias_list, bmap)


def _sw_mask_iotas():
    a = lax.broadcasted_iota(jnp.int32, (SW_BLOCK, 3 * SW_BLOCK), 0)
    j = lax.broadcasted_iota(jnp.int32, (SW_BLOCK, 3 * SW_BLOCK), 1)
    inwin = jnp.abs(j - SW_BLOCK - a) <= SW_BLOCK
    return j, inwin


def _sw_probs(q, k, bias, madd, sk):
    sc = _dotg(q, k, NT) + bias + madd
    m = jnp.maximum(jnp.max(sc, axis=1, keepdims=True), sk)
    e = jnp.exp(sc - m)
    es = jnp.exp(sk - m)
    inv = 1.0 / (jnp.sum(e, axis=1, keepdims=True) + es)
    return e * inv, es * inv


def sw_fwd(qs, ks, zq, t5b, sink, dep, name):
    s = qs.shape[0]
    nb = s // SW_BLOCK
    v_blk = (3 * NA_WIDTH + SW_Q_WIDTH + SW_KV_WIDTH) // LANES
    pad = s + 2 * SW_BLOCK

    def body(q_ref, k_ref, v_ref, b_ref, sink_ref, dep_ref, o_ref, kp, vp):
        zeros = jnp.zeros((SW_BLOCK, SW_KV_WIDTH), BF16)
        kp[0:SW_BLOCK, :] = zeros
        vp[0:SW_BLOCK, :] = zeros
        kp[SW_BLOCK + s:pad, :] = zeros
        vp[SW_BLOCK + s:pad, :] = zeros
        kp[SW_BLOCK:SW_BLOCK + s, :] = k_ref[...]
        vp[SW_BLOCK:SW_BLOCK + s, :] = v_ref[...]
        j, inwin = _sw_mask_iotas()

        def blk(n, carry):
            kpos = n * SW_BLOCK - SW_BLOCK + j
            madd = jnp.where(inwin & (kpos >= 0) & (kpos < s), 0.0, NEG)
            q0 = pl.multiple_of(n * SW_BLOCK, SW_BLOCK)
            for h in range(SW_HEADS):
                g = h // SW_REP
                q = q_ref[pl.ds(q0, SW_BLOCK), HEAD_DIM * h:HEAD_DIM * (h + 1)]
                k = kp[pl.ds(q0, 3 * SW_BLOCK), HEAD_DIM * g:HEAD_DIM * (g + 1)]
                v = vp[pl.ds(q0, 3 * SW_BLOCK), HEAD_DIM * g:HEAD_DIM * (g + 1)]
                p, _ = _sw_probs(q, k, b_ref[h], madd, sink_ref[h])
                o_ref[pl.ds(q0, SW_BLOCK), HEAD_DIM * h:HEAD_DIM * (h + 1)] = _dot(p.astype(BF16), v).astype(BF16)
            return carry

        lax.fori_loop(0, nb, blk, 0)

    return pl.pallas_call(
        body, name=name, grid=(1,),
        in_specs=[_full((s, SW_Q_WIDTH)), _full((s, SW_KV_WIDTH)),
                  pl.BlockSpec((s, SW_KV_WIDTH), lambda i: (0, v_blk)),
                  _full((SW_HEADS, SW_BLOCK, 3 * SW_BLOCK)), pl.BlockSpec(memory_space=pltpu.SMEM),
                  _full(dep.shape)],
        out_specs=_full((s, SW_Q_WIDTH)),
        out_shape=jax.ShapeDtypeStruct((s, SW_Q_WIDTH), BF16),
        scratch_shapes=[pltpu.VMEM((pad, SW_KV_WIDTH), BF16), pltpu.VMEM((pad, SW_KV_WIDTH), BF16)],
        compiler_params=_params(),
    )(qs, ks, zq, t5b, sink, dep)


def sw_bwd(qs, ks, zq, t5b, sink, o_sw, do_sw, name):
    s = qs.shape[0]
    nb = s // SW_BLOCK
    v_blk = (3 * NA_WIDTH + SW_Q_WIDTH + SW_KV_WIDTH) // LANES
    pad = s + 2 * SW_BLOCK

    def body(q_ref, k_ref, v_ref, b_ref, sink_ref, o_ref, do_ref,
             dq_ref, dk_ref, dv_ref, db_ref, dsk_ref, kp, vp, dkp, dvp):
        zeros = jnp.zeros((SW_BLOCK, SW_KV_WIDTH), BF16)
        kp[0:SW_BLOCK, :] = zeros
        vp[0:SW_BLOCK, :] = zeros
        kp[SW_BLOCK + s:pad, :] = zeros
        vp[SW_BLOCK + s:pad, :] = zeros
        kp[SW_BLOCK:SW_BLOCK + s, :] = k_ref[...]
        vp[SW_BLOCK:SW_BLOCK + s, :] = v_ref[...]
        dkp[...] = jnp.zeros(dkp.shape, F32)
        dvp[...] = jnp.zeros(dvp.shape, F32)
        db_ref[...] = jnp.zeros(db_ref.shape, F32)
        dsk_ref[...] = jnp.zeros(dsk_ref.shape, F32)
        j, inwin = _sw_mask_iotas()

        def blk(n, carry):
            kpos = n * SW_BLOCK - SW_BLOCK + j
            madd = jnp.where(inwin & (kpos >= 0) & (kpos < s), 0.0, NEG)
            q0 = pl.multiple_of(n * SW_BLOCK, SW_BLOCK)
            for g in range(SW_HEADS // SW_REP):
                kl = slice(HEAD_DIM * g, HEAD_DIM * (g + 1))
                k = kp[pl.ds(q0, 3 * SW_BLOCK), kl]
                v = vp[pl.ds(q0, 3 * SW_BLOCK), kl]
                dkw = jnp.zeros((3 * SW_BLOCK, HEAD_DIM), F32)
                dvw = jnp.zeros((3 * SW_BLOCK, HEAD_DIM), F32)
                for r in range(SW_REP):
                    h = g * SW_REP + r
                    hl = slice(HEAD_DIM * h, HEAD_DIM * (h + 1))
                    q = q_ref[pl.ds(q0, SW_BLOCK), hl]
                    p, ps = _sw_probs(q, k, b_ref[h], madd, sink_ref[h])
                    do = do_ref[pl.ds(q0, SW_BLOCK), hl]
                    ov = o_ref[pl.ds(q0, SW_BLOCK), hl]
                    delta = jnp.sum(do.astype(F32) * ov.astype(F32), axis=1, keepdims=True)
                    ds = p * (_dotg(do, v, NT) - delta)
                    db_ref[h] = db_ref[h] + ds
                    dsk_ref[h] = dsk_ref[h] - jnp.broadcast_to(ps * delta, (SW_BLOCK, LANES))
                    dsb = ds.astype(BF16)
                    dq_ref[pl.ds(q0, SW_BLOCK), hl] = _dot(dsb, k)
                    dkw = dkw + _dotg(dsb, q, TN)
                    dvw = dvw + _dotg(p.astype(BF16), do, TN)
                dkp[pl.ds(q0, 3 * SW_BLOCK), kl] = dkp[pl.ds(q0, 3 * SW_BLOCK), kl] + dkw
                dvp[pl.ds(q0, 3 * SW_BLOCK), kl] = dvp[pl.ds(q0, 3 * SW_BLOCK), kl] + dvw
            return carry

        lax.fori_loop(0, nb, blk, 0)
        dk_ref[...] = dkp[SW_BLOCK:SW_BLOCK + s, :]
        dv_ref[...] = dvp[SW_BLOCK:SW_BLOCK + s, :]

    bias_spec = _full((SW_HEADS, SW_BLOCK, 3 * SW_BLOCK))
    return pl.pallas_call(
        body, name=name, grid=(1,),
        in_specs=[_full((s, SW_Q_WIDTH)), _full((s, SW_KV_WIDTH)),
                  pl.BlockSpec((s, SW_KV_WIDTH), lambda i: (0, v_blk)),
                  bias_spec, pl.BlockSpec(memory_space=pltpu.SMEM),
                  _full((s, SW_Q_WIDTH)), _full((s, SW_Q_WIDTH))],
        out_specs=[_full((s, SW_Q_WIDTH)), _full((s, SW_KV_WIDTH)), _full((s, SW_KV_WIDTH)), bias_spec,
                   _full((SW_HEADS, SW_BLOCK, LANES))],
        out_shape=[jax.ShapeDtypeStruct((s, SW_Q_WIDTH), F32), jax.ShapeDtypeStruct((s, SW_KV_WIDTH), F32),
                   jax.ShapeDtypeStruct((s, SW_KV_WIDTH), F32),
                   jax.ShapeDtypeStruct((SW_HEADS, SW_BLOCK, 3 * SW_BLOCK), F32),
                   jax.ShapeDtypeStruct((SW_HEADS, SW_BLOCK, LANES), F32)],
        scratch_shapes=[pltpu.VMEM((pad, SW_KV_WIDTH), BF16), pltpu.VMEM((pad, SW_KV_WIDTH), BF16),
                        pltpu.VMEM((pad, SW_KV_WIDTH), F32), pltpu.VMEM((pad, SW_KV_WIDTH), F32)],
        compiler_params=_params(),
    )(qs, ks, zq, t5b, sink, o_sw, do_sw)


def merge_out(x, o_na, o_sw, gt, wbna_t, wbsw_t, wout, name):
    s, d = x.shape
    tm = _row_tile(s)

    def body(x_ref, ona_ref, osw_ref, gt_ref, wna_ref, wsw_ref, wo_ref, xo_ref, ana_ref, asw_ref, mg_ref):
        a_na = _dotg(ona_ref[...], wna_ref[...], NT)
        a_sw = _dotg(osw_ref[...], wsw_ref[...], NT)
        ana_ref[...] = a_na.astype(BF16)
        asw_ref[...] = a_sw.astype(BF16)
        merged = (gt_ref[:, 0:d].astype(F32) * a_na + gt_ref[:, d:2 * d].astype(F32) * a_sw).astype(BF16)
        mg_ref[...] = merged
        xo_ref[...] = x_ref[...] + _dot(merged, wo_ref[...])

    return pl.pallas_call(
        body, name=name, grid=(s // tm,),
        in_specs=[_rows(tm, d), _rows(tm, 512), _rows(tm, 512), _rows(tm, 2 * d),
                  _mat(*wbna_t), _mat(*wbsw_t), _mat(*wout)],
        out_specs=[_rows(tm, d)] * 4,
        out_shape=[jax.ShapeDtypeStruct((s, d), F32)] + [jax.ShapeDtypeStruct((s, d), BF16)] * 3,
        compiler_params=_params(),
    )(x, o_na, o_sw, gt, wbna_t[0], wbsw_t[0], wout[0])


def mix_bwd_out(dx, gt, a_na, a_sw, wbna_t, wbsw_t, wout, name):
    s, d = dx.shape
    tm = _row_tile(s)

    def body(dx_ref, gt_ref, ana_ref, asw_ref, wna_ref, wsw_ref, wo_ref,
             dxb_ref, dzg_ref, dana_ref, dasw_ref, dona_ref, dosw_ref, dbg_ref):
        @pl.when(pl.program_id(0) == 0)
        def _():
            dbg_ref[...] = jnp.zeros(dbg_ref.shape, F32)

        dxb = dx_ref[...].astype(BF16)
        dxb_ref[...] = dxb
        dm = _dotg(dxb, wo_ref[...], NT)
        for i, (a_ref, da_ref, w_ref, do_ref) in enumerate(
                [(ana_ref, dana_ref, wna_ref, dona_ref), (asw_ref, dasw_ref, wsw_ref, dosw_ref)]):
            gi = gt_ref[:, i * d:(i + 1) * d].astype(F32)
            da = (dm * gi).astype(BF16)
            da_ref[...] = da
            do_ref[...] = _dot(da, w_ref[...]).astype(BF16)
            dzg = dm * a_ref[...].astype(F32) * gi * (1.0 - gi)
            dzg_ref[:, i * d:(i + 1) * d] = dzg.astype(BF16)
            dbg_ref[:, i * d:(i + 1) * d] = dbg_ref[:, i * d:(i + 1) * d] + jnp.sum(dzg, axis=0, keepdims=True)

    return pl.pallas_call(
        body, name=name, grid=(s // tm,),
        in_specs=[_rows(tm, d), _rows(tm, 2 * d), _rows(tm, d), _rows(tm, d),
                  _mat(*wbna_t), _mat(*wbsw_t), _mat(*wout)],
        out_specs=[_rows(tm, d), _rows(tm, 2 * d), _rows(tm, d), _rows(tm, d), _rows(tm, 512), _rows(tm, 512),
                   _full((1, 2 * d))],
        out_shape=[jax.ShapeDtypeStruct((s, d), BF16), jax.ShapeDtypeStruct((s, 2 * d), BF16),
                   jax.ShapeDtypeStruct((s, d), BF16), jax.ShapeDtypeStruct((s, d), BF16),
                   jax.ShapeDtypeStruct((s, 512), BF16), jax.ShapeDtypeStruct((s, 512), BF16),
                   jax.ShapeDtypeStruct((1, 2 * d), F32)],
        compiler_params=_params(),
    )(dx, gt, a_na, a_sw, wbna_t[0], wbsw_t[0], wout[0])


def qk_norm_bwd(dqa, dka, dva, dqs, dks, dvs, zq, dzg, gq_na, gk_na, gq_sw, gk_sw, bd, name):
    s = zq.shape[0]
    d2 = dzg.shape[1]
    n_in = QKV_WIDTH + d2
    tm = _row_tile(s)

    def body(dqa_ref, dka_ref, dva_ref, dqs_ref, dks_ref, dvs_ref, zq_ref, dzg_ref,
             gqa_ref, gka_ref, gqs_ref, gks_ref, bd_ref, dz_ref, dgqa_ref, dgka_ref, dgqs_ref, dgks_ref):
        @pl.when(pl.program_id(0) == 0)
        def _():
            for r in (dgqa_ref, dgka_ref, dgqs_ref, dgks_ref):
                r[...] = jnp.zeros(r.shape, F32)

        bd512 = bd_ref[...]
        bd128 = bd_ref[0:SW_KV_WIDTH, 0:SW_KV_WIDTH]

        def one(c0, c1, dy_ref, g_ref, dg_ref, bdm, scale):
            z = zq_ref[:, c0:c1].astype(F32)
            r = lax.rsqrt(_group_mean(z * z, bdm) + EPS)
            zh = z * r
            dy = dy_ref[...] * scale
            dyg = dy * g_ref[...]
            dz = r * (dyg - zh * _group_mean(dyg * zh, bdm))
            dz_ref[:, c0:c1] = dz.astype(BF16)
            dg_ref[...] = dg_ref[...] + jnp.sum(dy * zh, axis=0, keepdims=True)

        one(0, 512, dqa_ref, gqa_ref, dgqa_ref, bd512, SCALE)
        one(512, 1024, dka_ref, gka_ref, dgka_ref, bd512, 1.0)
        dz_ref[:, 1024:1536] = dva_ref[...].astype(BF16)
        one(1536, 2048, dqs_ref, gqs_ref, dgqs_ref, bd512, SCALE)
        one(2048, 2176, dks_ref, gks_ref, dgks_ref, bd128, 1.0)
        dz_ref[:, 2176:2304] = dvs_ref[...].astype(BF16)
        dz_ref[:, QKV_WIDTH:n_in] = dzg_ref[...]

    return pl.pallas_call(
        body, name=name, grid=(s // tm,),
        in_specs=[_rows(tm, 512), _rows(tm, 512), _rows(tm, 512), _rows(tm, 512), _rows(tm, 128), _rows(tm, 128),
                  _rows(tm, QKV_WIDTH), _rows(tm, d2),
                  _full((1, 512)), _full((1, 512)), _full((1, 512)), _full((1, 128)), _full((512, 512))],
        out_specs=[_rows(tm, n_in), _full((1, 512)), _full((1, 512)), _full((1, 512)), _full((1, 128))],
        out_shape=[jax.ShapeDtypeStruct((s, n_in), BF16)] + [jax.ShapeDtypeStruct((1, 512), F32)] * 3
                  + [jax.ShapeDtypeStruct((1, 128), F32)],
        compiler_params=_params(),
    )(dqa, dka, dva, dqs, dks, dvs, zq, dzg, gq_na, gk_na, gq_sw, gk_sw, bd)


def ffn_bwd_act(dx, wd, hg, hu, name):
    s, d = dx.shape
    f = wd[0].shape[1]
    tm = _row_tile(s)
    fc = _col_chunk(f)

    def body(dx_ref, w_ref, hg_ref, hu_ref, dxb_ref, dhg_ref, dhu_ref):
        dxb = dx_ref[...].astype(BF16)
        dxb_ref[...] = dxb
        for c0 in range(0, f, fc):
            dact = 0.5 * _dotg(dxb, w_ref[c0:c0 + fc, :], NT)
            hg = hg_ref[:, c0:c0 + fc].astype(F32)
            hu = hu_ref[:, c0:c0 + fc].astype(F32)
            sg = _sigmoid(hg)
            dhu_ref[:, c0:c0 + fc] = (dact * hg * sg).astype(BF16)
            dhg_ref[:, c0:c0 + fc] = (dact * hu * sg * (1.0 + hg * (1.0 - sg))).astype(BF16)

    return pl.pallas_call(
        body, name=name, grid=(s // tm,),
        in_specs=[_rows(tm, d), _mat(*wd), _rows(tm, f), _rows(tm, f)],
        out_specs=[_rows(tm, d), _rows(tm, f), _rows(tm, f)],
        out_shape=[jax.ShapeDtypeStruct((s, d), BF16), jax.ShapeDtypeStruct((s, f), BF16),
                   jax.ShapeDtypeStruct((s, f), BF16)],
        compiler_params=_params(),
    )(dx, wd[0], hg, hu)


def proj_bwd_norm(acts, weights, x, gain, dx, dep, name):
    s, d = x.shape
    tm = _row_tile(s)
    n = len(acts)

    def body(*refs):
        a_refs, w_refs = refs[:n], refs[n:2 * n]
        x_ref, g_ref, dx_ref, _, o_ref, dg_ref = refs[2 * n:]

        @pl.when(pl.program_id(0) == 0)
        def _():
            dg_ref[...] = jnp.zeros(dg_ref.shape, F32)

        dxn = _dot(a_refs[0][...], w_refs[0][...])
        for a_ref, w_ref in zip(a_refs[1:], w_refs[1:]):
            dxn = dxn + _dot(a_ref[...], w_ref[...])
        xv = x_ref[...]
        r = _rstd(xv)
        xh = xv * r
        dxh = dxn * g_ref[...]
        o_ref[...] = dx_ref[...] + r * (dxh - xh * jnp.mean(dxh * xh, axis=-1, keepdims=True))
        dg_ref[...] = dg_ref[...] + jnp.sum(dxn * xh, axis=0, keepdims=True)

    return pl.pallas_call(
        body, name=name, grid=(s // tm,),
        in_specs=[_rows(tm, a.shape[1]) for a in acts] + [_mat(*w) for w in weights]
                 + [_rows(tm, d), _full((1, d)), _rows(tm, d), _full(dep.shape)],
        out_specs=[_rows(tm, d), _full((1, d))],
        out_shape=[jax.ShapeDtypeStruct((s, d), F32), jax.ShapeDtypeStruct((1, d), F32)],
        compiler_params=_params(),
    )(*acts, *[w[0] for w in weights], x, gain, dx, dep)


def tn_matmul(a, b, scale, name):
    s, n = a.shape
    k = b.shape[1]
    tn = _col_chunk(n)

    def body(a_ref, b_ref, o_ref):
        o_ref[...] = (scale * _dotg(a_ref[...], b_ref[...], TN)).astype(BF16)

    return pl.pallas_call(
        body, name=name, grid=(n // tn,),
        in_specs=[pl.BlockSpec((s, tn), lambda i: (0, i)), _full((s, k))],
        out_specs=pl.BlockSpec((tn, k), lambda i: (i, 0)),
        out_shape=jax.ShapeDtypeStruct((n, k), BF16),
        compiler_params=_params(),
    )(a, b)


def loss_grad(y, target, name):
    s, d = y.shape
    tm = _row_tile(s)

    def body(y_ref, t_ref, dy_ref, acc_ref):
        @pl.when(pl.program_id(0) == 0)
        def _():
            acc_ref[...] = jnp.zeros(acc_ref.shape, F32)

        err = y_ref[...] - t_ref[...]
        dy_ref[...] = err * (1.0 / d)
        e2 = err * err
        part = jnp.sum(e2.reshape(tm // 8, 8, d), axis=0)
        acc = part[:, 0:LANES]
        for c0 in range(LANES, d, LANES):
            acc = acc + part[:, c0:c0 + LANES]
        acc_ref[...] = acc_ref[...] + acc

    return pl.pallas_call(
        body, name=name, grid=(s // tm,),
        in_specs=[_rows(tm, d), _rows(tm, d)],
        out_specs=[_rows(tm, d), _full((8, LANES))],
        out_shape=[jax.ShapeDtypeStruct((s, d), F32), jax.ShapeDtypeStruct((8, LANES), F32)],
        compiler_params=_params(),
    )(y, target)


def _mesh_pos():
    return lax.axis_index("x"), lax.axis_index("y"), lax.axis_index("c")


def gather_weights(shards):
    n = len(shards)

    def body(*refs):
        ins, outs = refs[:n], refs[n:2 * n]
        send_sems, recv_sems, local_sems = refs[2 * n:]
        x, y, c = _mesh_pos()
        me, sibling = (x, y, c), (x, y, 1 - c)
        chips = [(1 - x, y), (x, 1 - y), (1 - x, 1 - y)]

        def slot(a, px, py, pc):
            return outs[a].at[:, 4 * px + 2 * py + pc]

        def copy(a, k, block, to, src=None):
            return pltpu.make_async_remote_copy(
                src_ref=slot(a, *block) if src is None else src, dst_ref=slot(a, *block),
                send_sem=send_sems.at[a, k], recv_sem=recv_sems.at[a, k], device_id=to, device_id_type=MESH)

        mine = [pltpu.make_async_copy(ins[a], slot(a, *me), local_sems.at[a]) for a in range(n)]
        for cp in mine:
            cp.start()
        first = []
        for a in range(n):
            first.append(copy(a, 0, me, sibling, src=ins[a]))
            first += [copy(a, 1 + j, me, (*chip, c), src=ins[a]) for j, chip in enumerate(chips)]
        for cp in first:
            cp.start()
        passed = []
        for j, chip in enumerate(chips):
            for a in range(n):
                copy(a, 1 + j, (*chip, c), me).wait_recv()
                fwd = copy(a, 4 + j, (*chip, c), sibling)
                fwd.start()
                passed.append(fwd)
        for a in range(n):
            copy(a, 0, sibling, me).wait_recv()
            for j, chip in enumerate(chips):
                copy(a, 4 + j, (*chip, 1 - c), me).wait_recv()
        for cp in first + passed:
            cp.wait_send()
        for cp in mine:
            cp.wait()

    any_spec = pl.BlockSpec(memory_space=pl.ANY)
    return pl.pallas_call(
        body, name="gather_weights",
        in_specs=[any_spec] * n, out_specs=[any_spec] * n,
        out_shape=[jax.ShapeDtypeStruct((w.shape[0], N_DEV) + w.shape[1:], w.dtype) for w in shards],
        scratch_shapes=[pltpu.SemaphoreType.DMA((n, 7)), pltpu.SemaphoreType.DMA((n, 7)),
                        pltpu.SemaphoreType.DMA((n,))],
        compiler_params=pltpu.CompilerParams(has_side_effects=True),
    )(*shards)


def _peers():
    x, y, c = _mesh_pos()
    peers = []
    for rel in range(1, N_DEV):
        peers.append((1 - x if rel & 4 else x, 1 - y if rel & 2 else y, 1 - c if rel & 1 else c))
    return 4 * x + 2 * y + c, peers


HBM_SPEC = pl.BlockSpec(memory_space=pltpu.HBM)
SEM_SPEC = pl.BlockSpec(memory_space=pltpu.SEMAPHORE)


def _split_call(body, name, thru, n_sems, extra=(), with_token=True):
    hbm = lambda t: pltpu.with_memory_space_constraint(t, pltpu.HBM)
    effect = pltpu.CompilerParams(has_side_effects=pltpu.SideEffectType.DATAFLOW_SIDE_EFFECTING)
    nt = len(thru)
    thru_shapes = [pltpu.HBM(t.shape, t.dtype) for t in thru]
    if with_token:
        (after,) = extra
        outs = pl.pallas_call(
            body, name=name, in_specs=[HBM_SPEC] * nt + [pl.BlockSpec(memory_space=pl.ANY)],
            out_specs=[SEM_SPEC] * len(n_sems) + [HBM_SPEC] * nt + [pl.BlockSpec(memory_space=pltpu.VMEM)],
            out_shape=[pltpu.SemaphoreType.DMA((k,)) for k in n_sems] + thru_shapes
                      + [jax.ShapeDtypeStruct((8, LANES), F32)],
            input_output_aliases={i: len(n_sems) + i for i in range(nt)}, compiler_params=effect,
        )(*[hbm(t) for t in thru], after)
        return outs[:len(n_sems)], outs[len(n_sems):-1], outs[-1]
    return pl.pallas_call(
        body, name=name,
        in_specs=[HBM_SPEC] * nt + [SEM_SPEC] * len(n_sems) + [pl.BlockSpec(memory_space=pl.ANY)],
        out_specs=[HBM_SPEC] * nt, out_shape=thru_shapes,
        input_output_aliases={i: i for i in range(nt)}, compiler_params=effect,
    )(*thru, *extra)


def _gather_targets():
    x, y, c = _mesh_pos()
    return 4 * x + 2 * y + c, [(x, y, 1 - c), (1 - x, y, c), (x, 1 - y, c), (1 - x, 1 - y, c)]


def gather_start(shards, after, name):
    n = len(shards)
    zones = [lax.empty((w.shape[0], N_DEV) + w.shape[1:], w.dtype) for w in shards]

    def body(*refs):
        ins, zs = refs[:n], refs[n:2 * n]
        send_sems, recv_sems, local_sems = refs[2 * n + 1:2 * n + 4]
        token = refs[-1]
        me, targets = _gather_targets()
        for a in range(n):
            pltpu.make_async_copy(ins[a], zs[a].at[:, me], local_sems.at[a]).start()
            for k, to in enumerate(targets):
                pltpu.make_async_remote_copy(
                    src_ref=ins[a], dst_ref=zs[a].at[:, me], send_sem=send_sems.at[4 * a + k],
                    recv_sem=recv_sems.at[4 * a + k], device_id=to, device_id_type=MESH).start()
        token[...] = jnp.zeros(token.shape, F32)

    sems, thru, token = _split_call(body, name, list(shards) + zones, (4 * n, 4 * n, n), extra=(after,))
    return (sems, thru, n), token


def gather_wait(started, after, name):
    sems, thru, n = started

    def body(*refs):
        zs = refs[n:2 * n]
        send_sems, recv_sems, local_sems = refs[2 * n:2 * n + 3]
        _, targets = _gather_targets()
        for a in range(n):
            for k, to in enumerate(targets):
                cp = pltpu.make_async_remote_copy(
                    src_ref=zs[a].at[:, 0], dst_ref=zs[a].at[:, 0], send_sem=send_sems.at[4 * a + k],
                    recv_sem=recv_sems.at[4 * a + k], device_id=to, device_id_type=MESH)
                cp.wait_send()
                cp.wait_recv()
            pltpu.make_async_copy(zs[a].at[:, 0], zs[a].at[:, 0], local_sems.at[a]).wait()

    return _split_call(body, name, thru, (4 * n, 4 * n, n), extra=(*sems, after), with_token=False)[n:]


def forward_start(zones, after, name):
    n = len(zones)

    def body(*refs):
        zs = refs[:n]
        send_sems, recv_sems = refs[n + 1:n + 3]
        token = refs[-1]
        x, y, c = _mesh_pos()
        for a in range(n):
            for j, chip in enumerate([(1 - x, y), (x, 1 - y), (1 - x, 1 - y)]):
                blk = zs[a].at[:, 4 * chip[0] + 2 * chip[1] + c]
                pltpu.make_async_remote_copy(
                    src_ref=blk, dst_ref=blk, send_sem=send_sems.at[3 * a + j], recv_sem=recv_sems.at[3 * a + j],
                    device_id=(x, y, 1 - c), device_id_type=MESH).start()
        token[...] = jnp.zeros(token.shape, F32)

    sems, thru, token = _split_call(body, name, list(zones), (3 * n, 3 * n), extra=(after,))
    return (sems, thru, n), token


def forward_wait(started, after, name):
    sems, thru, n = started

    def body(*refs):
        zs = refs[:n]
        send_sems, recv_sems = refs[n:n + 2]
        x, y, c = _mesh_pos()
        for a in range(n):
            for j in range(3):
                cp = pltpu.make_async_remote_copy(
                    src_ref=zs[a].at[:, 0], dst_ref=zs[a].at[:, 0], send_sem=send_sems.at[3 * a + j],
                    recv_sem=recv_sems.at[3 * a + j], device_id=(x, y, 1 - c), device_id_type=MESH)
                cp.wait_send()
                cp.wait_recv()

    return _split_call(body, name, thru, (3 * n, 3 * n), extra=(*sems, after), with_token=False)


def scatter_start(groups, name):
    n = len(groups)
    flat = [g for grp in groups for g in grp]
    nf = len(flat)
    offs = np.cumsum([0] + [len(grp) for grp in groups])
    lands = [lax.empty((N_DEV, len(grp)) + grp[0].shape[1:], grp[0].dtype) for grp in groups]

    def body(*refs):
        ins, zones = refs[:nf], refs[nf:nf + n]
        send_sems, recv_sems, local_sems = refs[nf + n:nf + n + 3]
        token = refs[-1]
        me, peers = _peers()
        for a in range(n):
            for w in range(len(groups[a])):
                pltpu.make_async_copy(ins[offs[a] + w].at[me], zones[a].at[me, w], local_sems.at[a]).start()
        for k, peer in enumerate(peers):
            p_id = 4 * peer[0] + 2 * peer[1] + peer[2]
            for a in range(n):
                for w in range(len(groups[a])):
                    pltpu.make_async_remote_copy(
                        src_ref=ins[offs[a] + w].at[p_id], dst_ref=zones[a].at[me, w],
                        send_sem=send_sems.at[7 * a + k], recv_sem=recv_sems.at[7 * a + k],
                        device_id=peer, device_id_type=MESH).start()
        token[...] = jnp.zeros(token.shape, F32)

    hbm = lambda t: pltpu.with_memory_space_constraint(t, pltpu.HBM)
    outs = pl.pallas_call(
        body, name=name,
        in_specs=[HBM_SPEC] * (nf + n),
        out_specs=[SEM_SPEC] * 3 + [HBM_SPEC] * (nf + n) + [pl.BlockSpec(memory_space=pltpu.VMEM)],
        out_shape=[pltpu.SemaphoreType.DMA((7 * n,)), pltpu.SemaphoreType.DMA((7 * n,)), pltpu.SemaphoreType.DMA((n,))]
                  + [pltpu.HBM(t.shape, t.dtype) for t in flat + lands]
                  + [jax.ShapeDtypeStruct((8, LANES), F32)],
        input_output_aliases={i: 3 + i for i in range(nf + n)},
        compiler_params=pltpu.CompilerParams(has_side_effects=pltpu.SideEffectType.DATAFLOW_SIDE_EFFECTING),
    )(*[hbm(t) for t in flat], *[hbm(t) for t in lands])
    sems, thru, token = outs[:3], outs[3:3 + nf + n], outs[-1]
    return (sems, thru, [len(grp) for grp in groups]), token


def scatter_wait(started, after, name):
    (send_sems, recv_sems, local_sems), thru, sizes = started
    n = len(sizes)
    nf = len(thru) - n

    def body(*refs):
        zones = refs[nf:nf + n]
        s_sems, r_sems, l_sems = refs[nf + n:nf + n + 3]
        me, peers = _peers()
        for a in range(n):
            for k, peer in enumerate(peers):
                cp = pltpu.make_async_remote_copy(
                    src_ref=zones[a].at[0], dst_ref=zones[a].at[0],
                    send_sem=s_sems.at[7 * a + k], recv_sem=r_sems.at[7 * a + k], device_id=peer,
                    device_id_type=MESH)
                cp.wait_send()
                cp.wait_recv()
            pltpu.make_async_copy(zones[a].at[0], zones[a].at[0], l_sems.at[a]).wait()

    outs = pl.pallas_call(
        body, name=name,
        in_specs=[HBM_SPEC] * (nf + n) + [SEM_SPEC] * 3 + [pl.BlockSpec(memory_space=pl.ANY)],
        out_specs=[HBM_SPEC] * (nf + n),
        out_shape=[pltpu.HBM(t.shape, t.dtype) for t in thru],
        input_output_aliases={i: i for i in range(nf + n)},
        compiler_params=pltpu.CompilerParams(has_side_effects=pltpu.SideEffectType.DATAFLOW_SIDE_EFFECTING),
    )(*thru, send_sems, recv_sems, local_sems, after)
    return outs[nf:]


def share_small(small):
    def body(s_ref, o_ref, send_sems, recv_sems, local_sem):
        me, peers = _peers()
        mine = pltpu.make_async_copy(s_ref, o_ref.at[me], local_sem)
        mine.start()
        copies = [pltpu.make_async_remote_copy(src_ref=s_ref, dst_ref=o_ref.at[me], send_sem=send_sems.at[k],
                                               recv_sem=recv_sems.at[k], device_id=peer, device_id_type=MESH)
                  for k, peer in enumerate(peers)]
        for cp in copies:
            cp.start()
        for cp in copies:
            cp.wait()
        mine.wait()

    vm = pl.BlockSpec(memory_space=pltpu.VMEM)
    return pl.pallas_call(
        body, name="share_small", in_specs=[vm], out_specs=vm,
        out_shape=jax.ShapeDtypeStruct((N_DEV,) + small.shape, small.dtype),
        scratch_shapes=[pltpu.SemaphoreType.DMA((7,)), pltpu.SemaphoreType.DMA((7,)), pltpu.SemaphoreType.DMA],
    )(small)


def sum_sources(recv, name):
    _, w, r, c = recv.shape

    def body(r_ref, o_ref):
        acc = r_ref[0, 0].astype(F32)
        for src in range(1, N_DEV):
            acc = acc + r_ref[src, 0].astype(F32)
        o_ref[0] = acc

    return pl.pallas_call(
        body, name=name, grid=(w,),
        in_specs=[pl.BlockSpec((N_DEV, 1, r, c), lambda i: (0, i, 0, 0))],
        out_specs=pl.BlockSpec((1, r, c), lambda i: (i, 0, 0)),
        out_shape=jax.ShapeDtypeStruct((w, r, c), F32),
        compiler_params=_params(),
    )(recv)


def _adamw_math(w, g, m, v):
    m = ADAM_B1 * m + (1.0 - ADAM_B1) * g
    v = ADAM_B2 * v + (1.0 - ADAM_B2) * (g * g)
    m_hat = m / (1.0 - ADAM_B1 ** ADAM_STEP)
    v_hat = v / (1.0 - ADAM_B2 ** ADAM_STEP)
    delta = -ADAM_LR * (m_hat / (jnp.sqrt(v_hat) + ADAM_EPS) + ADAM_WD * w)
    return delta, m, v


def adamw(w, g, m, v, name):
    shape = w.shape
    c = shape[-1]
    r = int(np.prod(shape[:-1]))
    w2, g2, m2, v2 = (t.reshape(r, c) for t in (w, g, m, v))
    tr = next(t for t in range(min(r, 512), 0, -1) if r % t == 0 and (t % 8 == 0 or t == r))

    def body(w_ref, g_ref, m_ref, v_ref, d_ref, mo_ref, vo_ref):
        d_ref[...], mo_ref[...], vo_ref[...] = _adamw_math(w_ref[...], g_ref[...], m_ref[...], v_ref[...])

    spec = pl.BlockSpec((tr, c), lambda i: (i, 0))
    outs = pl.pallas_call(
        body, name=name, grid=(r // tr,),
        in_specs=[spec] * 4, out_specs=[spec] * 3,
        out_shape=[jax.ShapeDtypeStruct((r, c), F32)] * 3,
        compiler_params=_params(),
    )(w2, g2, m2, v2)
    return tuple(t.reshape(shape) for t in outs)


def adamw_small(w, recv, m, v, name):
    def body(w_ref, r_ref, m_ref, v_ref, g_ref, d_ref, mo_ref, vo_ref):
        g = r_ref[0]
        for src in range(1, N_DEV):
            g = g + r_ref[src]
        g_ref[...] = g
        d_ref[...], mo_ref[...], vo_ref[...] = _adamw_math(w_ref[...], g, m_ref[...], v_ref[...])

    vm = pl.BlockSpec(memory_space=pltpu.VMEM)
    return pl.pallas_call(
        body, name=name, in_specs=[vm] * 4, out_specs=[vm] * 4,
        out_shape=[jax.ShapeDtypeStruct(w.shape, F32)] * 4,
        compiler_params=pltpu.CompilerParams(vmem_limit_bytes=V7X_VMEM_LIMIT),
    )(w, recv, m, v)


SMALL_NAMES = ("ffn1_norm", "mix_norm", "ffn2_norm", "b_gate", "na_q_norm", "na_k_norm", "sw_q_norm", "sw_k_norm",
               "na_rpb", "sw_sink", "t5_rel_table")


def _pack_small(parts):
    flat = jnp.concatenate([parts[k].reshape(-1).astype(F32) for k in SMALL_NAMES])
    n = flat.shape[0]
    rows = -(-n // (8 * LANES)) * 8
    return jnp.pad(flat, (0, rows * LANES - n)).reshape(rows, LANES)


def _unpack_small(packed, like):
    flat = packed.reshape(-1)
    out, off = {}, 0
    for k in SMALL_NAMES:
        n = int(np.prod(like[k].shape))
        out[k] = flat[off:off + n].reshape(like[k].shape)
        off += n
    return out


def kernel(x, ffn1_norm, ffn1_w_gate, ffn1_w_up, ffn1_w_down, mix_norm, w_in, b_gate, na_q_norm, na_k_norm, na_rpb, sw_q_norm, sw_k_norm, sw_sink, t5_rel_table, w_branch_na, w_branch_sw, w_out, ffn2_norm, ffn2_w_gate, ffn2_w_up, ffn2_w_down, loss_target, m_ffn1_norm, m_ffn1_w_gate, m_ffn1_w_up, m_ffn1_w_down, m_mix_norm, m_w_in, m_b_gate, m_na_q_norm, m_na_k_norm, m_na_rpb, m_sw_q_norm, m_sw_k_norm, m_sw_sink, m_t5_rel_table, m_w_branch_na, m_w_branch_sw, m_w_out, m_ffn2_norm, m_ffn2_w_gate, m_ffn2_w_up, m_ffn2_w_down, v_ffn1_norm, v_ffn1_w_gate, v_ffn1_w_up, v_ffn1_w_down, v_mix_norm, v_w_in, v_b_gate, v_na_q_norm, v_na_k_norm, v_na_rpb, v_sw_q_norm, v_sw_k_norm, v_sw_sink, v_t5_rel_table, v_w_branch_na, v_w_branch_sw, v_w_out, v_ffn2_norm, v_ffn2_w_gate, v_ffn2_w_up, v_ffn2_w_down):
    weights = dict(ffn1_norm=ffn1_norm, ffn1_w_gate=ffn1_w_gate, ffn1_w_up=ffn1_w_up, ffn1_w_down=ffn1_w_down,
                   mix_norm=mix_norm, w_in=w_in, b_gate=b_gate, na_q_norm=na_q_norm, na_k_norm=na_k_norm,
                   na_rpb=na_rpb, sw_q_norm=sw_q_norm, sw_k_norm=sw_k_norm, sw_sink=sw_sink,
                   t5_rel_table=t5_rel_table, w_branch_na=w_branch_na, w_branch_sw=w_branch_sw, w_out=w_out,
                   ffn2_norm=ffn2_norm, ffn2_w_gate=ffn2_w_gate, ffn2_w_up=ffn2_w_up, ffn2_w_down=ffn2_w_down)
    mom_m = dict(ffn1_norm=m_ffn1_norm, ffn1_w_gate=m_ffn1_w_gate, ffn1_w_up=m_ffn1_w_up, ffn1_w_down=m_ffn1_w_down,
                 mix_norm=m_mix_norm, w_in=m_w_in, b_gate=m_b_gate, na_q_norm=m_na_q_norm, na_k_norm=m_na_k_norm,
                 na_rpb=m_na_rpb, sw_q_norm=m_sw_q_norm, sw_k_norm=m_sw_k_norm, sw_sink=m_sw_sink,
                 t5_rel_table=m_t5_rel_table, w_branch_na=m_w_branch_na, w_branch_sw=m_w_branch_sw, w_out=m_w_out,
                 ffn2_norm=m_ffn2_norm, ffn2_w_gate=m_ffn2_w_gate, ffn2_w_up=m_ffn2_w_up, ffn2_w_down=m_ffn2_w_down)
    mom_v = dict(ffn1_norm=v_ffn1_norm, ffn1_w_gate=v_ffn1_w_gate, ffn1_w_up=v_ffn1_w_up, ffn1_w_down=v_ffn1_w_down,
                 mix_norm=v_mix_norm, w_in=v_w_in, b_gate=v_b_gate, na_q_norm=v_na_q_norm, na_k_norm=v_na_k_norm,
                 na_rpb=v_na_rpb, sw_q_norm=v_sw_q_norm, sw_k_norm=v_sw_k_norm, sw_sink=v_sw_sink,
                 t5_rel_table=v_t5_rel_table, w_branch_na=v_w_branch_na, w_branch_sw=v_w_branch_sw, w_out=v_w_out,
                 ffn2_norm=v_ffn2_norm, ffn2_w_gate=v_ffn2_w_gate, ffn2_w_up=v_ffn2_w_up, ffn2_w_down=v_ffn2_w_down)
    order = list(weights)

    depth = ffn1_norm.shape[0]
    s, d = x.shape[1], x.shape[2]
    xs = x[0]
    tr = lambda w: jnp.swapaxes(w, -1, -2)

    a_loc = jnp.stack([t for l in range(depth) for t in (
        tr(ffn1_w_gate[l]), tr(ffn1_w_up[l]), ffn1_w_down[l],
        tr(ffn2_w_gate[l]), tr(ffn2_w_up[l]), ffn2_w_down[l])]).astype(BF16)
    b_loc = tr(w_in).astype(BF16)
    c_loc = w_out.astype(BF16)
    d_loc = jnp.stack([t for l in range(depth) for t in (tr(w_branch_na[l]), tr(w_branch_sw[l]))]).astype(BF16)
    merge = lambda t: t.reshape(t.shape[0], N_DEV * t.shape[2], t.shape[3])
    no_dep = jnp.zeros((8, LANES), F32)

    def shards_of(kind, l):
        if kind == "ffn1":
            return [a_loc[6 * l:6 * l + 3]]
        if kind == "win":
            return [b_loc[l:l + 1]]
        return [a_loc[6 * l + 3:6 * l + 6], c_loc[l:l + 1], d_loc[2 * l:2 * l + 2]]

    def start(kind, l, after):
        return gather_start(shards_of(kind, l), after, f"gather_{kind}_{l}")

    def arrive(started, kind, l, after):
        zones = gather_wait(started, after, f"gather_{kind}_{l}_wait")
        return forward_start(zones, no_dep, f"forward_{kind}_{l}")

    def finish(fwd, kind, l, after):
        return [merge(z) for z in forward_wait(fwd, after, f"forward_{kind}_{l}_wait")]

    bd = jnp.asarray(np.kron(np.eye(NA_WIDTH // HEAD_DIM), np.full((HEAD_DIM, HEAD_DIM), 1.0 / HEAD_DIM)), BF16)
    bmap = jnp.asarray(_t5_bucket_map())
    tile8 = lambda g: jnp.tile(g, NA_WIDTH // HEAD_DIM).reshape(1, NA_WIDTH)
    tile2 = lambda g: jnp.tile(g, SW_KV_WIDTH // HEAD_DIM).reshape(1, SW_KV_WIDTH)

    first = merge(gather_weights([a_loc[0:3]])[0])
    st_win, dep = start("win", 0, first)
    t5b = t5_expand(t5_rel_table, bmap, "t5_expand")

    saved = []
    layer_w = {0: dict(wg1=(first, 0), wu1=(first, 1), wd1=(first, 2))}
    cur = xs
    for l in range(depth):
        sv = {}
        lw = layer_w[l]
        sv["x0"] = cur
        sv["xn1"], sv["hg1"], sv["hu1"], sv["act1"] = ffn_up(cur, ffn1_norm[l][None], lw["wg1"], lw["wu1"], dep,
                                                             f"ffn1_up_{l}")
        cur = ffn_down(cur, sv["act1"], lw["wd1"], no_dep, f"ffn1_down_{l}")
        sv["x1"] = cur
        fwd, _ = arrive(st_win, "win", l, cur)
        st_rest, tok = start("rest", l, cur)
        (zb,) = finish(fwd, "win", l, tok)
        lw["win"] = (zb, 0)
        sv["gains"] = (tile8(na_q_norm[l]), tile8(na_k_norm[l]), tile8(sw_q_norm[l]), tile2(sw_k_norm[l]))
        sv["hn"], sv["zq"], sv["qa"], sv["ka"], sv["qs"], sv["ks"], sv["gt"] = mix_in(
            cur, mix_norm[l][None], lw["win"], b_gate[l][None], *sv["gains"], bd, f"mix_in_{l}")
        sv["t2"] = rpb_expand(na_rpb[l].reshape(-1), na_rpb.shape[1], no_dep, f"rpb_expand_{l}")
        sv["o_na"] = na_fwd(sv["qa"], sv["ka"], sv["zq"], sv["t2"], f"na_fwd_{l}")
        dep = no_dep
        if l + 1 < depth:
            st_ffn1, dep = start("ffn1", l + 1, sv["o_na"])
        sv["o_sw"] = sw_fwd(sv["qs"], sv["ks"], sv["zq"], t5b, sw_sink[l], dep, f"sw_fwd_{l}")
        fwd, tok = arrive(st_rest, "rest", l, sv["o_sw"])
        za, zc, zd = finish(fwd, "rest", l, tok)
        lw.update(wg2=(za, 0), wu2=(za, 1), wd2=(za, 2), wout=(zc, 0), wna=(zd, 0), wsw=(zd, 1))
        cur, sv["a_na"], sv["a_sw"], sv["merged"] = merge_out(
            cur, sv["o_na"], sv["o_sw"], sv["gt"], lw["wna"], lw["wsw"], lw["wout"], f"merge_out_{l}")
        sv["x2"] = cur
        dep = no_dep
        if l + 1 < depth:
            st_win, dep = start("win", l + 1, cur)
        sv["xn2"], sv["hg2"], sv["hu2"], sv["act2"] = ffn_up(cur, ffn2_norm[l][None], lw["wg2"], lw["wu2"], dep,
                                                             f"ffn2_up_{l}")
        dep = no_dep
        if l + 1 < depth:
            fwd, dep = arrive(st_ffn1, "ffn1", l + 1, sv["act2"])
        cur = ffn_down(cur, sv["act2"], lw["wd2"], dep, f"ffn2_down_{l}")
        dep = no_dep
        if l + 1 < depth:
            (za,) = finish(fwd, "ffn1", l + 1, cur)
            layer_w[l + 1] = dict(wg1=(za, 0), wu1=(za, 1), wd1=(za, 2))
        saved.append(sv)

    dx, loss_acc = loss_grad(cur, loss_target[0], "loss_grad")
    loss = lax.psum(jnp.sum(loss_acc) * (0.5 / d), ("x", "y", "c"))

    split = lambda t: t.reshape(N_DEV, t.shape[0] // N_DEV, t.shape[1])
    pending = {}
    small = {k: [None] * depth for k in SMALL_NAMES if k != "t5_rel_table"}
    dbias_sw = []
    for l in reversed(range(depth)):
        sv = saved[l]
        lw = layer_w[l]
        wg1, wu1, wd1, wg2, wu2, wd2 = (lw[k] for k in ("wg1", "wu1", "wd1", "wg2", "wu2", "wd2"))
        win_t, wout_l, wna_t, wsw_t = lw["win"], lw["wout"], lw["wna"], lw["wsw"]
        blocks = ((2, "x2", "xn2", "hg2", "hu2", "act2", wg2, wu2, wd2, "ffn2_norm", 3),
                  (1, "x0", "xn1", "hg1", "hu1", "act1", wg1, wu1, wd1, "ffn1_norm", 0))

        def ffn_backward(dx, blk):
            tag, xk, xnk, hgk, huk, actk, wg, wu, wd, norm_name, slot = blk
            gains = weights[norm_name]
            dxb, dhg, dhu = ffn_bwd_act(dx, wd, sv[hgk], sv[huk], f"ffn{tag}_bwd_act_{l}")
            gwd = tn_matmul(sv[actk], dxb, 0.5, f"ffn{tag}_dwd_{l}")
            gwg = tn_matmul(dhg, sv[xnk], 1.0, f"ffn{tag}_dwg_{l}")
            gwu = tn_matmul(dhu, sv[xnk], 1.0, f"ffn{tag}_dwu_{l}")
            pending[f"ffn{tag}_{l}"], token = scatter_start([[split(gwg), split(gwu), split(gwd)]],
                                                            f"scatter_ffn{tag}_{l}")
            dx, dg = proj_bwd_norm([dhg, dhu], [wg, wu], sv[xk], gains[l][None], dx, token, f"ffn{tag}_bwd_x_{l}")
            small[norm_name][l] = dg[0]
            return dx

        dx = ffn_backward(dx, blocks[0])
        dxb, dzg, da_na, da_sw, do_na, do_sw, dbg = mix_bwd_out(
            dx, sv["gt"], sv["a_na"], sv["a_sw"], wna_t, wsw_t, wout_l, f"mix_bwd_out_{l}")
        small["b_gate"][l] = dbg[0]
        gwout = tn_matmul(sv["merged"], dxb, 1.0, f"dwout_{l}")
        gwna = tn_matmul(da_na, sv["o_na"], 1.0, f"dwna_{l}")
        gwsw = tn_matmul(da_sw, sv["o_sw"], 1.0, f"dwsw_{l}")
        dqa, dka, dva, dt2 = na_bwd(sv["qa"], sv["ka"], sv["zq"], sv["t2"], sv["o_na"], do_na, f"na_bwd_{l}")
        dqs, dks, dvs, dbias, dsink = sw_bwd(sv["qs"], sv["ks"], sv["zq"], t5b, sw_sink[l], sv["o_sw"], do_sw,
                                             f"sw_bwd_{l}")
        dbias_sw.append(dbias)
        small["sw_sink"][l] = jnp.sum(dsink[:, :, 0], axis=1)
        drpb = rpb_reduce(dt2, f"rpb_reduce_{l}")
        small["na_rpb"][l] = drpb[:, :, :2 * NA_COLS - 1, 0]
        dz, dgqa, dgka, dgqs, dgks = qk_norm_bwd(dqa, dka, dva, dqs, dks, dvs, sv["zq"], dzg, *sv["gains"], bd,
                                                 f"qk_norm_bwd_{l}")
        fold = lambda g: jnp.sum(g.reshape(-1, HEAD_DIM), axis=0)
        small["na_q_norm"][l], small["na_k_norm"][l] = fold(dgqa), fold(dgka)
        small["sw_q_norm"][l], small["sw_k_norm"][l] = fold(dgqs), fold(dgks)
        gwin = tn_matmul(dz, sv["hn"], 1.0, f"dwin_{l}")
        pending[f"mix_{l}"], token = scatter_start([[split(gwout)], [split(gwna), split(gwsw)], [split(gwin)]],
                                                   f"scatter_mix_{l}")
        dx, dg = proj_bwd_norm([dz], [win_t], sv["x1"], mix_norm[l][None], dx, token, f"mix_bwd_x_{l}")
        small["mix_norm"][l] = dg[0]
        dx = ffn_backward(dx, blocks[1])

    dtab = t5_reduce(dbias_sw, bmap, "t5_reduce")
    small_parts = {k: jnp.stack(v) for k, v in small.items()}
    small_parts["t5_rel_table"] = jnp.transpose(dtab[:, :, 0])
    small_packed = _pack_small(small_parts)

    rs_ = share_small(small_packed)
    summed = {}
    for key, started in pending.items():
        zones = scatter_wait(started, dx, f"wait_{key}")
        summed[key] = [sum_sources(z, f"sum_{key}_{i}") for i, z in enumerate(zones)]

    layers = lambda f: jnp.stack([f(l) for l in range(depth)])
    own = {}
    for tag in (1, 2):
        own[f"ffn{tag}_w_gate"] = (layers(lambda l: summed[f"ffn{tag}_{l}"][0][0]), True)
        own[f"ffn{tag}_w_up"] = (layers(lambda l: summed[f"ffn{tag}_{l}"][0][1]), True)
        own[f"ffn{tag}_w_down"] = (layers(lambda l: summed[f"ffn{tag}_{l}"][0][2]), False)
    own["w_out"] = (layers(lambda l: summed[f"mix_{l}"][0][0]), False)
    own["w_branch_na"] = (layers(lambda l: summed[f"mix_{l}"][1][0]), True)
    own["w_branch_sw"] = (layers(lambda l: summed[f"mix_{l}"][1][1]), True)
    own["w_in"] = (layers(lambda l: summed[f"mix_{l}"][2][0]), True)

    grads, delta, new_m, new_v = {}, {}, {}, {}
    for k, (g, transposed) in own.items():
        view = tr if transposed else (lambda t: t)
        d_k, m_k, v_k = adamw(view(weights[k]), g, view(mom_m[k]), view(mom_v[k]), f"adamw_{k}")
        grads[k], delta[k], new_m[k], new_v[k] = view(g), view(d_k), view(m_k), view(v_k)
    g_s, d_s, m_s, v_s = adamw_small(_pack_small(weights), rs_, _pack_small(mom_m), _pack_small(mom_v), "adamw_small")
    for dst, packed in ((grads, g_s), (delta, d_s), (new_m, m_s), (new_v, v_s)):
        dst.update(_unpack_small(packed, weights))

    return (loss, dx[None], *[grads[k] for k in order], *[delta[k] for k in order],
            *[new_m[k] for k in order], *[new_v[k] for k in order])
```

```python
import functools
import math

import numpy as np
import jax
import jax.numpy as jnp
from jax import lax
from jax.experimental import pallas as pl
from jax.experimental.pallas import tpu as pltpu

F32 = jnp.float32
BF16 = jnp.bfloat16
MESH = pl.DeviceIdType.MESH

N_DEV = 8
EPS = 1e-6
NEG = -1e30
HEAD_DIM = 64
GRID_W = 64
NA_ROWS = 8
NA_COLS = 16
NA_WIDTH = 512
SW_Q_WIDTH = 512
SW_KV_WIDTH = 128
SW_BLOCK = 128
SW_HEADS = 8
SW_REP = 4
REL_BUCKETS = 32
REL_MAX_DIST = 128
QKV_WIDTH = 3 * NA_WIDTH + SW_Q_WIDTH + 2 * SW_KV_WIDTH
SCALE = 1.0 / math.sqrt(HEAD_DIM)

ADAM_LR = 0.001
ADAM_B1 = 0.9
ADAM_B2 = 0.999
ADAM_EPS = 1e-08
ADAM_WD = 0.01
ADAM_STEP = 10

V7X_VMEM_LIMIT = 56 * 1024 * 1024
LANES = 128
MXU_TILE = 256

NT = (((1,), (1,)), ((), ()))
TN = (((0,), (0,)), ((), ()))


def _params(n_grid=1):
    return pltpu.CompilerParams(dimension_semantics=("arbitrary",) * n_grid,
                                vmem_limit_bytes=V7X_VMEM_LIMIT)


def _row_tile(s):
    for t in (512, 256, 128, 64, 32, 16, 8):
        if s % t == 0:
            return t
    raise ValueError(s)


def _tn_tile(n):
    best = max(t for t in range(LANES, min(n, 2304) + 1, LANES) if n % t == 0) if n % LANES == 0 else n
    return best // 2 if best == n and n >= 1024 else best


ONCE = pl.Buffered(1)


def _col_chunk(n):
    return MXU_TILE if n % MXU_TILE == 0 else n


def _dot(a, b):
    return jnp.dot(a, b, preferred_element_type=F32)


def _dotg(a, b, dn):
    return lax.dot_general(a, b, dn, preferred_element_type=F32)


def _sigmoid(v):
    return 1.0 / (1.0 + jnp.exp(-v))


def _rstd(xv):
    return lax.rsqrt(jnp.mean(xv * xv, axis=-1, keepdims=True) + EPS)


def _full(shape):
    nd = len(shape)
    return pl.BlockSpec(shape, lambda i, _n=nd: (0,) * _n)


def _rows(tm, width):
    return pl.BlockSpec((tm, width), lambda i: (i, 0))


def _mat(stack, idx):
    return pl.BlockSpec((None,) + tuple(stack.shape[1:]), lambda i, _w=idx: (_w, 0, 0), pipeline_mode=ONCE)


def _group_mean(v, bd):
    hi = v.astype(BF16)
    lo = (v - hi.astype(F32)).astype(BF16)
    return _dot(hi, bd) + _dot(lo, bd)


def ffn_up(x, gain, wg_t, wu_t, dep, name):
    s, d = x.shape
    f = wg_t[0].shape[1]
    tm = _row_tile(s)
    fc = _col_chunk(f)

    def body(x_ref, g_ref, wg_ref, wu_ref, dep_ref, xn_ref, hg_ref, hu_ref, act_ref):
        xv = x_ref[...]
        xn = (xv * _rstd(xv) * g_ref[...]).astype(BF16)
        xn_ref[...] = xn
        for c0 in range(0, f, fc):
            hg = _dotg(xn, wg_ref[c0:c0 + fc, :], NT)
            hu = _dotg(xn, wu_ref[c0:c0 + fc, :], NT)
            hg_ref[:, c0:c0 + fc] = hg.astype(BF16)
            hu_ref[:, c0:c0 + fc] = hu.astype(BF16)
            act_ref[:, c0:c0 + fc] = (hg * _sigmoid(hg) * hu).astype(BF16)

    return pl.pallas_call(
        body, name=name, grid=(s // tm,),
        in_specs=[_rows(tm, d), _full((1, d)), _mat(*wg_t), _mat(*wu_t), _full(dep.shape)],
        out_specs=[_rows(tm, d), _rows(tm, f), _rows(tm, f), _rows(tm, f)],
        out_shape=[jax.ShapeDtypeStruct((s, d), BF16)] + [jax.ShapeDtypeStruct((s, f), BF16)] * 3,
        compiler_params=_params(),
    )(x, gain, wg_t[0], wu_t[0], dep)


def ffn_down(x, act, wd, dep, name):
    s, d = x.shape
    f = act.shape[1]
    tm = _row_tile(s)

    def body(x_ref, a_ref, w_ref, dep_ref, o_ref):
        o_ref[...] = x_ref[...] + 0.5 * _dot(a_ref[...], w_ref[...])

    return pl.pallas_call(
        body, name=name, grid=(s // tm,),
        in_specs=[_rows(tm, d), _rows(tm, f), _mat(*wd), _full(dep.shape)],
        out_specs=_rows(tm, d),
        out_shape=jax.ShapeDtypeStruct((s, d), F32),
        compiler_params=_params(),
    )(x, act, wd[0], dep)


def mix_in(x, gain, win_t, b_gate, gq_na, gk_na, gq_sw, gk_sw, bd, name):
    s, d = x.shape
    tm = _row_tile(s)
    gc = _col_chunk(2 * d)

    def body(x_ref, g_ref, w_ref, b_ref, gqa_ref, gka_ref, gqs_ref, gks_ref, bd_ref,
             hn_ref, zq_ref, qa_ref, ka_ref, qs_ref, ks_ref, gt_ref):
        xv = x_ref[...]
        hn = (xv * _rstd(xv) * g_ref[...]).astype(BF16)
        hn_ref[...] = hn

        def proj(c0, c1):
            return _dotg(hn, w_ref[c0:c1, :], NT)

        def headnorm(z, g, bdm):
            return z * lax.rsqrt(_group_mean(z * z, bdm) + EPS) * g

        bd512 = bd_ref[...]
        bd128 = bd_ref[0:SW_KV_WIDTH, 0:SW_KV_WIDTH]
        z = proj(0, 512)
        zq_ref[:, 0:512] = z.astype(BF16)
        qa_ref[...] = (headnorm(z, gqa_ref[...], bd512) * SCALE).astype(BF16)
        z = proj(512, 1024)
        zq_ref[:, 512:1024] = z.astype(BF16)
        ka_ref[...] = headnorm(z, gka_ref[...], bd512).astype(BF16)
        z = proj(1024, 1536)
        zq_ref[:, 1024:1536] = z.astype(BF16)
        z = proj(1536, 2048)
        zq_ref[:, 1536:2048] = z.astype(BF16)
        qs_ref[...] = (headnorm(z, gqs_ref[...], bd512) * SCALE).astype(BF16)
        z = proj(2048, 2176)
        zq_ref[:, 2048:2176] = z.astype(BF16)
        ks_ref[...] = headnorm(z, gks_ref[...], bd128).astype(BF16)
        z = proj(2176, 2304)
        zq_ref[:, 2176:2304] = z.astype(BF16)
        for c0 in range(0, 2 * d, gc):
            zg = proj(QKV_WIDTH + c0, QKV_WIDTH + c0 + gc) + b_ref[:, c0:c0 + gc]
            gt_ref[:, c0:c0 + gc] = _sigmoid(zg).astype(BF16)

    return pl.pallas_call(
        body, name=name, grid=(s // tm,),
        in_specs=[_rows(tm, d), _full((1, d)), _mat(*win_t), _full((1, 2 * d)),
                  _full((1, 512)), _full((1, 512)), _full((1, 512)), _full((1, 128)), _full((512, 512))],
        out_specs=[_rows(tm, d), _rows(tm, QKV_WIDTH), _rows(tm, 512), _rows(tm, 512), _rows(tm, 512),
                   _rows(tm, 128), _rows(tm, 2 * d)],
        out_shape=[jax.ShapeDtypeStruct((s, d), BF16), jax.ShapeDtypeStruct((s, QKV_WIDTH), BF16),
                   jax.ShapeDtypeStruct((s, 512), BF16), jax.ShapeDtypeStruct((s, 512), BF16),
                   jax.ShapeDtypeStruct((s, 512), BF16), jax.ShapeDtypeStruct((s, 128), BF16),
                   jax.ShapeDtypeStruct((s, 2 * d), BF16)],
        compiler_params=_params(),
    )(x, gain, win_t[0], b_gate, gq_na, gk_na, gq_sw, gk_sw, bd)


def _na_iotas():
    qc = lax.broadcasted_iota(jnp.int32, (GRID_W, LANES), 0)
    ln = lax.broadcasted_iota(jnp.int32, (GRID_W, LANES), 1)
    low = ln < GRID_W
    kc = jnp.where(low, ln, ln - GRID_W)
    diff = kc - qc + (NA_COLS - 1)
    qcs = jnp.clip(qc - NA_COLS // 2, 0, GRID_W - NA_COLS)
    inwin = (kc >= qcs) & (kc < qcs + NA_COLS)
    return diff, low, inwin


NA_RI = 2 * NA_ROWS - 1
NA_CI = 2 * NA_COLS - 1
NA_T2 = NA_RI + 1


def rpb_expand(rpb_flat, n_heads, dep, name):
    def body(rpb_ref, dep_ref, o_ref):
        diff, low, _ = _na_iotas()
        for h in range(n_heads):
            def one(e, carry, h=h):
                lo_row = jnp.maximum(e - 1, 0)
                hi_row = jnp.minimum(e, NA_RI - 1)
                lo_on = jnp.where(e >= 1, 1.0, 0.0)
                hi_on = jnp.where(e <= NA_RI - 1, 1.0, 0.0)
                t = jnp.zeros((GRID_W, LANES), F32)
                for c in range(NA_CI):
                    lo = rpb_ref[h * NA_RI * NA_CI + lo_row * NA_CI + c] * lo_on
                    hi = rpb_ref[h * NA_RI * NA_CI + hi_row * NA_CI + c] * hi_on
                    t = jnp.where(diff == c, jnp.where(low, lo, hi), t)
                o_ref[h, e] = t
                return carry
            lax.fori_loop(0, NA_T2, one, 0)

    return pl.pallas_call(
        body, name=name,
        in_specs=[pl.BlockSpec(memory_space=pltpu.SMEM), pl.BlockSpec(memory_space=pltpu.VMEM)],
        out_specs=pl.BlockSpec(memory_space=pltpu.VMEM),
        out_shape=jax.ShapeDtypeStruct((n_heads, NA_T2, GRID_W, LANES), F32),
        compiler_params=pltpu.CompilerParams(vmem_limit_bytes=V7X_VMEM_LIMIT),
    )(rpb_flat, dep)


def rpb_reduce(dt2, name):
    n_heads = dt2.shape[0]

    def body(d_ref, o_ref):
        diff, low, _ = _na_iotas()
        low32 = lax.broadcasted_iota(jnp.int32, (32, LANES), 1) < GRID_W
        o_ref[...] = jnp.zeros(o_ref.shape, F32)
        for h in range(n_heads):
            def one(e, carry, h=h):
                dv = d_ref[h, e]
                rows = [jnp.sum(jnp.where(diff == c, dv, 0.0), axis=0, keepdims=True) for c in range(NA_CI)]
                rows.append(jnp.zeros((1, LANES), F32))
                r = jnp.concatenate(rows, axis=0)
                lo = jnp.sum(jnp.where(low32, r, 0.0), axis=1, keepdims=True)
                hi = jnp.sum(jnp.where(low32, 0.0, r), axis=1, keepdims=True)
                lo_row = jnp.maximum(e - 1, 0)
                hi_row = jnp.minimum(e, NA_RI - 1)
                o_ref[h, lo_row] = o_ref[h, lo_row] + jnp.broadcast_to(lo, (32, LANES))
                o_ref[h, hi_row] = o_ref[h, hi_row] + jnp.broadcast_to(hi, (32, LANES))
                return carry
            lax.fori_loop(0, NA_T2, one, 0)

    return pl.pallas_call(
        body, name=name,
        in_specs=[pl.BlockSpec(memory_space=pltpu.VMEM)],
        out_specs=pl.BlockSpec(memory_space=pltpu.VMEM),
        out_shape=jax.ShapeDtypeStruct((n_heads, NA_RI, 32, LANES), F32),
        compiler_params=pltpu.CompilerParams(vmem_limit_bytes=V7X_VMEM_LIMIT),
    )(dt2)


NA_TQ = 4
NA_TK = NA_TQ + NA_ROWS
NA_KCH = NA_TK // 2


def _na_tile_geometry(t, rows):
    r = t * NA_TQ
    kbase = jnp.clip(r - NA_ROWS // 2, 0, rows - NA_TK)
    starts = [jnp.clip(r + a - NA_ROWS // 2, 0, rows - NA_ROWS) for a in range(NA_TQ)]
    return r, kbase, starts


def _na_tile_mask(kbase, starts, low, inwin):
    half = jnp.where(low, 0, 1)
    cols = []
    for c in range(NA_KCH):
        krow = kbase + 2 * c + half
        cols.append(jnp.concatenate(
            [jnp.where(inwin & (krow >= st) & (krow < st + NA_ROWS), 0.0, NEG) for st in starts], axis=0))
    return jnp.concatenate(cols, axis=1)


def _na_tile_index(r, kbase, a, c):
    return jnp.clip(kbase + 2 * c - (r + a) + NA_ROWS, 0, NA_T2 - 1)


def _na_tile_probs(q, k, t2_ref, hh, r, kbase, madd):
    bias = jnp.concatenate(
        [jnp.concatenate([t2_ref[hh, _na_tile_index(r, kbase, a, c)] for a in range(NA_TQ)], axis=0)
         for c in range(NA_KCH)], axis=1)
    sc = _dotg(q, k, NT) + bias + madd
    e = jnp.exp(sc - jnp.max(sc, axis=1, keepdims=True))
    return e * (1.0 / jnp.sum(e, axis=1, keepdims=True))


def na_fwd(qa, ka, zq, t2, name):
    s = qa.shape[0]
    rows = s // GRID_W
    n_pairs = NA_WIDTH // LANES
    v_blk0 = (2 * NA_WIDTH) // LANES

    assert rows % NA_TQ == 0 and rows >= NA_TK
    tq, tk = NA_TQ * GRID_W, NA_TK * GRID_W

    def body(q_ref, k_ref, v_ref, t2_ref, o_ref):
        _, low, inwin = _na_iotas()

        def tile(t, carry):
            r, kbase, starts = _na_tile_geometry(t, rows)
            madd = _na_tile_mask(kbase, starts, low, inwin)
            qr = pl.ds(pl.multiple_of(r * GRID_W, tq), tq)
            kr = pl.ds(pl.multiple_of(kbase * GRID_W, tq), tk)
            for hh in range(2):
                lanes = slice(HEAD_DIM * hh, HEAD_DIM * (hh + 1))
                p = _na_tile_probs(q_ref[qr, lanes], k_ref[kr, lanes], t2_ref, hh, r, kbase, madd)
                o_ref[qr, lanes] = _dot(p.astype(BF16), v_ref[kr, lanes]).astype(BF16)
            return carry

        lax.fori_loop(0, rows // NA_TQ, tile, 0)

    col = lambda off: pl.BlockSpec((s, LANES), lambda p, _o=off: (0, _o + p))
    return pl.pallas_call(
        body, name=name, grid=(n_pairs,),
        in_specs=[col(0), col(0), col(v_blk0),
                  pl.BlockSpec((2, NA_T2, GRID_W, LANES), lambda p: (p, 0, 0, 0))],
        out_specs=col(0),
        out_shape=jax.ShapeDtypeStruct((s, NA_WIDTH), BF16),
        compiler_params=_params(),
    )(qa, ka, zq, t2)


def na_bwd(qa, ka, zq, t2, o_na, do_na, name):
    s = qa.shape[0]
    rows = s // GRID_W
    n_pairs = NA_WIDTH // LANES
    v_blk0 = (2 * NA_WIDTH) // LANES

    tq, tk = NA_TQ * GRID_W, NA_TK * GRID_W

    def body(q_ref, k_ref, v_ref, t2_ref, o_ref, do_ref, dq_ref, dk_ref, dv_ref, dt2_ref):
        _, low, inwin = _na_iotas()
        dk_ref[...] = jnp.zeros(dk_ref.shape, F32)
        dv_ref[...] = jnp.zeros(dv_ref.shape, F32)
        dt2_ref[...] = jnp.zeros(dt2_ref.shape, F32)

        def tile(t, carry):
            r, kbase, starts = _na_tile_geometry(t, rows)
            madd = _na_tile_mask(kbase, starts, low, inwin)
            qr = pl.ds(pl.multiple_of(r * GRID_W, tq), tq)
            kr = pl.ds(pl.multiple_of(kbase * GRID_W, tq), tk)
            for hh in range(2):
                lanes = slice(HEAD_DIM * hh, HEAD_DIM * (hh + 1))
                q, k, v = q_ref[qr, lanes], k_ref[kr, lanes], v_ref[kr, lanes]
                p = _na_tile_probs(q, k, t2_ref, hh, r, kbase, madd)
                do = do_ref[qr, lanes]
                delta = jnp.sum(do.astype(F32) * o_ref[qr, lanes].astype(F32), axis=1, keepdims=True)
                ds = p * (_dotg(do, v, NT) - delta)
                for a in range(NA_TQ):
                    for c in range(NA_KCH):
                        e = _na_tile_index(r, kbase, a, c)
                        dt2_ref[hh, e] = dt2_ref[hh, e] + ds[GRID_W * a:GRID_W * (a + 1), LANES * c:LANES * (c + 1)]
                dsb = ds.astype(BF16)
                dq_ref[qr, lanes] = _dot(dsb, k)
                dk_ref[kr, lanes] = dk_ref[kr, lanes] + _dotg(dsb, q, TN)
                dv_ref[kr, lanes] = dv_ref[kr, lanes] + _dotg(p.astype(BF16), do, TN)
            return carry

        lax.fori_loop(0, rows // NA_TQ, tile, 0)

    col = lambda off: pl.BlockSpec((s, LANES), lambda p, _o=off: (0, _o + p))
    t2spec = pl.BlockSpec((2, NA_T2, GRID_W, LANES), lambda p: (p, 0, 0, 0))
    return pl.pallas_call(
        body, name=name, grid=(n_pairs,),
        in_specs=[col(0), col(0), col(v_blk0), t2spec, col(0), col(0)],
        out_specs=[col(0), col(0), col(0), t2spec],
        out_shape=[jax.ShapeDtypeStruct((s, NA_WIDTH), F32)] * 3 + [jax.ShapeDtypeStruct(t2.shape, F32)],
        compiler_params=_params(),
    )(qa, ka, zq, t2, o_na, do_na)


def _t5_bucket_map():
    rel = np.arange(3 * SW_BLOCK)[None, :] - SW_BLOCK - np.arange(SW_BLOCK)[:, None]
    nb = REL_BUCKETS // 2
    max_exact = nb // 2
    n = np.abs(rel)
    large = max_exact + (np.log(np.maximum(n, 1) / max_exact)
                         / np.log(REL_MAX_DIST / max_exact) * (nb - max_exact)).astype(np.int32)
    large = np.minimum(large, nb - 1)
    return ((rel > 0) * nb + np.where(n < max_exact, n, large)).astype(np.int32)


def t5_expand(table, bmap, name):
    def body(tab_ref, bm_ref, o_ref):
        bm = bm_ref[...]
        for h in range(SW_HEADS):
            t = jnp.zeros(bm.shape, F32)
            for b in range(REL_BUCKETS):
                t = jnp.where(bm == b, tab_ref[b, h], t)
            o_ref[h] = t

    return pl.pallas_call(
        body, name=name,
        in_specs=[pl.BlockSpec(memory_space=pltpu.SMEM), pl.BlockSpec(memory_space=pltpu.VMEM)],
        out_specs=pl.BlockSpec(memory_space=pltpu.VMEM),
        out_shape=jax.ShapeDtypeStruct((SW_HEADS,) + bmap.shape, F32),
        compiler_params=pltpu.CompilerParams(vmem_limit_bytes=V7X_VMEM_LIMIT),
    )(table, bmap)


def t5_reduce(dbias_list, bmap, name):
    n = len(dbias_list)

    def body(*refs):
        d_refs, bm_ref, o_ref = refs[:n], refs[n], refs[n + 1]
        bm = bm_ref[...]
        for h in range(SW_HEADS):
            dv = d_refs[0][h]
            for other in d_refs[1:]:
                dv = dv + other[h]
            rows = [jnp.sum(jnp.where(bm == b, dv, 0.0), axis=0, keepdims=True) for b in range(REL_BUCKETS)]
            r = jnp.concatenate(rows, axis=0)
            o_ref[h] = jnp.broadcast_to(jnp.sum(r, axis=1, keepdims=True), (REL_BUCKETS, LANES))

    return pl.pallas_call(
        body, name=name,
        in_specs=[pl.BlockSpec(memory_space=pltpu.VMEM)] * (n + 1),
        out_specs=pl.BlockSpec(memory_space=pltpu.VMEM),
        out_shape=jax.ShapeDtypeStruct((SW_HEADS, REL_BUCKETS, LANES), F32),
        compiler_params=pltpu.CompilerParams(vmem_limit_bytes=V7X_VMEM_LIMIT),
    )(*dbias_list, bmap)


def _sw_mask_iotas():
    a = lax.broadcasted_iota(jnp.int32, (SW_BLOCK, 3 * SW_BLOCK), 0)
    j = lax.broadcasted_iota(jnp.int32, (SW_BLOCK, 3 * SW_BLOCK), 1)
    inwin = jnp.abs(j - SW_BLOCK - a) <= SW_BLOCK
    return j, inwin


def _sw_probs(q, k, bias, madd, sk):
    sc = _dotg(q, k, NT) + bias + madd
    m = jnp.maximum(jnp.max(sc, axis=1, keepdims=True), sk)
    e = jnp.exp(sc - m)
    es = jnp.exp(sk - m)
    inv = 1.0 / (jnp.sum(e, axis=1, keepdims=True) + es)
    return e * inv, es * inv


def sw_fwd(qs, ks, zq, t5b, sink, dep, name):
    s = qs.shape[0]
    nb = s // SW_BLOCK
    v_blk = (3 * NA_WIDTH + SW_Q_WIDTH + SW_KV_WIDTH) // LANES
    pad = s + 2 * SW_BLOCK

    def body(q_ref, k_ref, v_ref, b_ref, sink_ref, dep_ref, o_ref, kp, vp):
        zeros = jnp.zeros((SW_BLOCK, SW_KV_WIDTH), BF16)
        kp[0:SW_BLOCK, :] = zeros
        vp[0:SW_BLOCK, :] = zeros
        kp[SW_BLOCK + s:pad, :] = zeros
        vp[SW_BLOCK + s:pad, :] = zeros
        kp[SW_BLOCK:SW_BLOCK + s, :] = k_ref[...]
        vp[SW_BLOCK:SW_BLOCK + s, :] = v_ref[...]
        j, inwin = _sw_mask_iotas()

        def blk(n, carry):
            kpos = n * SW_BLOCK - SW_BLOCK + j
            madd = jnp.where(inwin & (kpos >= 0) & (kpos < s), 0.0, NEG)
            q0 = pl.multiple_of(n * SW_BLOCK, SW_BLOCK)
            for h in range(SW_HEADS):
                g = h // SW_REP
                q = q_ref[pl.ds(q0, SW_BLOCK), HEAD_DIM * h:HEAD_DIM * (h + 1)]
                k = kp[pl.ds(q0, 3 * SW_BLOCK), HEAD_DIM * g:HEAD_DIM * (g + 1)]
                v = vp[pl.ds(q0, 3 * SW_BLOCK), HEAD_DIM * g:HEAD_DIM * (g + 1)]
                p, _ = _sw_probs(q, k, b_ref[h], madd, sink_ref[h])
                o_ref[pl.ds(q0, SW_BLOCK), HEAD_DIM * h:HEAD_DIM * (h + 1)] = _dot(p.astype(BF16), v).astype(BF16)
            return carry

        lax.fori_loop(0, nb, blk, 0)

    return pl.pallas_call(
        body, name=name, grid=(1,),
        in_specs=[_full((s, SW_Q_WIDTH)), _full((s, SW_KV_WIDTH)),
                  pl.BlockSpec((s, SW_KV_WIDTH), lambda i: (0, v_blk)),
                  _full((SW_HEADS, SW_BLOCK, 3 * SW_BLOCK)), pl.BlockSpec(memory_space=pltpu.SMEM),
                  _full(dep.shape)],
        out_specs=_full((s, SW_Q_WIDTH)),
        out_shape=jax.ShapeDtypeStruct((s, SW_Q_WIDTH), BF16),
        scratch_shapes=[pltpu.VMEM((pad, SW_KV_WIDTH), BF16), pltpu.VMEM((pad, SW_KV_WIDTH), BF16)],
        compiler_params=_params(),
    )(qs, ks, zq, t5b, sink, dep)


def sw_bwd(qs, ks, zq, t5b, sink, o_sw, do_sw, name):
    s = qs.shape[0]
    nb = s // SW_BLOCK
    v_blk = (3 * NA_WIDTH + SW_Q_WIDTH + SW_KV_WIDTH) // LANES
    pad = s + 2 * SW_BLOCK

    def body(q_ref, k_ref, v_ref, b_ref, sink_ref, o_ref, do_ref,
             dq_ref, dk_ref, dv_ref, db_ref, dsk_ref, kp, vp, dkp, dvp):
        zeros = jnp.zeros((SW_BLOCK, SW_KV_WIDTH), BF16)
        kp[0:SW_BLOCK, :] = zeros
        vp[0:SW_BLOCK, :] = zeros
        kp[SW_BLOCK + s:pad, :] = zeros
        vp[SW_BLOCK + s:pad, :] = zeros
        kp[SW_BLOCK:SW_BLOCK + s, :] = k_ref[...]
        vp[SW_BLOCK:SW_BLOCK + s, :] = v_ref[...]
        dkp[...] = jnp.zeros(dkp.shape, F32)
        dvp[...] = jnp.zeros(dvp.shape, F32)
        db_ref[...] = jnp.zeros(db_ref.shape, F32)
        dsk_ref[...] = jnp.zeros(dsk_ref.shape, F32)
        j, inwin = _sw_mask_iotas()

        def blk(n, carry):
            kpos = n * SW_BLOCK - SW_BLOCK + j
            madd = jnp.where(inwin & (kpos >= 0) & (kpos < s), 0.0, NEG)
            q0 = pl.multiple_of(n * SW_BLOCK, SW_BLOCK)
            for g in range(SW_HEADS // SW_REP):
                kl = slice(HEAD_DIM * g, HEAD_DIM * (g + 1))
                k = kp[pl.ds(q0, 3 * SW_BLOCK), kl]
                v = vp[pl.ds(q0, 3 * SW_BLOCK), kl]
                dkw = jnp.zeros((3 * SW_BLOCK, HEAD_DIM), F32)
                dvw = jnp.zeros((3 * SW_BLOCK, HEAD_DIM), F32)
                for r in range(SW_REP):
                    h = g * SW_REP + r
                    hl = slice(HEAD_DIM * h, HEAD_DIM * (h + 1))
                    q = q_ref[pl.ds(q0, SW_BLOCK), hl]
                    p, ps = _sw_probs(q, k, b_ref[h], madd, sink_ref[h])
                    do = do_ref[pl.ds(q0, SW_BLOCK), hl]
                    ov = o_ref[pl.ds(q0, SW_BLOCK), hl]
                    delta = jnp.sum(do.astype(F32) * ov.astype(F32), axis=1, keepdims=True)
                    ds = p * (_dotg(do, v, NT) - delta)
                    db_ref[h] = db_ref[h] + ds
                    dsk_ref[h] = dsk_ref[h] - jnp.broadcast_to(ps * delta, (SW_BLOCK, LANES))
                    dsb = ds.astype(BF16)
                    dq_ref[pl.ds(q0, SW_BLOCK), hl] = _dot(dsb, k)
                    dkw = dkw + _dotg(dsb, q, TN)
                    dvw = dvw + _dotg(p.astype(BF16), do, TN)
                dkp[pl.ds(q0, 3 * SW_BLOCK), kl] = dkp[pl.ds(q0, 3 * SW_BLOCK), kl] + dkw
                dvp[pl.ds(q0, 3 * SW_BLOCK), kl] = dvp[pl.ds(q0, 3 * SW_BLOCK), kl] + dvw
            return carry

        lax.fori_loop(0, nb, blk, 0)
        dk_ref[...] = dkp[SW_BLOCK:SW_BLOCK + s, :]
        dv_ref[...] = dvp[SW_BLOCK:SW_BLOCK + s, :]

    bias_spec = _full((SW_HEADS, SW_BLOCK, 3 * SW_BLOCK))
    return pl.pallas_call(
        body, name=name, grid=(1,),
        in_specs=[_full((s, SW_Q_WIDTH)), _full((s, SW_KV_WIDTH)),
                  pl.BlockSpec((s, SW_KV_WIDTH), lambda i: (0, v_blk)),
                  bias_spec, pl.BlockSpec(memory_space=pltpu.SMEM),
                  _full((s, SW_Q_WIDTH)), _full((s, SW_Q_WIDTH))],
        out_specs=[_full((s, SW_Q_WIDTH)), _full((s, SW_KV_WIDTH)), _full((s, SW_KV_WIDTH)), bias_spec,
                   _full((SW_HEADS, SW_BLOCK, LANES))],
        out_shape=[jax.ShapeDtypeStruct((s, SW_Q_WIDTH), F32), jax.ShapeDtypeStruct((s, SW_KV_WIDTH), F32),
                   jax.ShapeDtypeStruct((s, SW_KV_WIDTH), F32),
                   jax.ShapeDtypeStruct((SW_HEADS, SW_BLOCK, 3 * SW_BLOCK), F32),
                   jax.ShapeDtypeStruct((SW_HEADS, SW_BLOCK, LANES), F32)],
        scratch_shapes=[pltpu.VMEM((pad, SW_KV_WIDTH), BF16), pltpu.VMEM((pad, SW_KV_WIDTH), BF16),
                        pltpu.VMEM((pad, SW_KV_WIDTH), F32), pltpu.VMEM((pad, SW_KV_WIDTH), F32)],
        compiler_params=_params(),
    )(qs, ks, zq, t5b, sink, o_sw, do_sw)


def merge_out(x, o_na, o_sw, gt, wbna_t, wbsw_t, wout, name):
    s, d = x.shape
    tm = _row_tile(s)

    def body(x_ref, ona_ref, osw_ref, gt_ref, wna_ref, wsw_ref, wo_ref, xo_ref, ana_ref, asw_ref, mg_ref):
        a_na = _dotg(ona_ref[...], wna_ref[...], NT)
        a_sw = _dotg(osw_ref[...], wsw_ref[...], NT)
        ana_ref[...] = a_na.astype(BF16)
        asw_ref[...] = a_sw.astype(BF16)
        merged = (gt_ref[:, 0:d].astype(F32) * a_na + gt_ref[:, d:2 * d].astype(F32) * a_sw).astype(BF16)
        mg_ref[...] = merged
        xo_ref[...] = x_ref[...] + _dot(merged, wo_ref[...])

    return pl.pallas_call(
        body, name=name, grid=(s // tm,),
        in_specs=[_rows(tm, d), _rows(tm, 512), _rows(tm, 512), _rows(tm, 2 * d),
                  _mat(*wbna_t), _mat(*wbsw_t), _mat(*wout)],
        out_specs=[_rows(tm, d)] * 4,
        out_shape=[jax.ShapeDtypeStruct((s, d), F32)] + [jax.ShapeDtypeStruct((s, d), BF16)] * 3,
        compiler_params=_params(),
    )(x, o_na, o_sw, gt, wbna_t[0], wbsw_t[0], wout[0])


def mix_bwd_out(dx, gt, a_na, a_sw, wbna_t, wbsw_t, wout, name):
    s, d = dx.shape
    tm = _row_tile(s)

    def body(dx_ref, gt_ref, ana_ref, asw_ref, wna_ref, wsw_ref, wo_ref,
             dxb_ref, dzg_ref, dana_ref, dasw_ref, dona_ref, dosw_ref, dbg_ref):
        @pl.when(pl.program_id(0) == 0)
        def _():
            dbg_ref[...] = jnp.zeros(dbg_ref.shape, F32)

        dxb = dx_ref[...].astype(BF16)
        dxb_ref[...] = dxb
        dm = _dotg(dxb, wo_ref[...], NT)
        for i, (a_ref, da_ref, w_ref, do_ref) in enumerate(
                [(ana_ref, dana_ref, wna_ref, dona_ref), (asw_ref, dasw_ref, wsw_ref, dosw_ref)]):
            gi = gt_ref[:, i * d:(i + 1) * d].astype(F32)
            da = (dm * gi).astype(BF16)
            da_ref[...] = da
            do_ref[...] = _dot(da, w_ref[...]).astype(BF16)
            dzg = dm * a_ref[...].astype(F32) * gi * (1.0 - gi)
            dzg_ref[:, i * d:(i + 1) * d] = dzg.astype(BF16)
            dbg_ref[:, i * d:(i + 1) * d] = dbg_ref[:, i * d:(i + 1) * d] + jnp.sum(dzg, axis=0, keepdims=True)

    return pl.pallas_call(
        body, name=name, grid=(s // tm,),
        in_specs=[_rows(tm, d), _rows(tm, 2 * d), _rows(tm, d), _rows(tm, d),
                  _mat(*wbna_t), _mat(*wbsw_t), _mat(*wout)],
        out_specs=[_rows(tm, d), _rows(tm, 2 * d), _rows(tm, d), _rows(tm, d), _rows(tm, 512), _rows(tm, 512),
                   _full((1, 2 * d))],
        out_shape=[jax.ShapeDtypeStruct((s, d), BF16), jax.ShapeDtypeStruct((s, 2 * d), BF16),
                   jax.ShapeDtypeStruct((s, d), BF16), jax.ShapeDtypeStruct((s, d), BF16),
                   jax.ShapeDtypeStruct((s, 512), BF16), jax.ShapeDtypeStruct((s, 512), BF16),
                   jax.ShapeDtypeStruct((1, 2 * d), F32)],
        compiler_params=_params(),
    )(dx, gt, a_na, a_sw, wbna_t[0], wbsw_t[0], wout[0])


def qk_norm_bwd(dqa, dka, dva, dqs, dks, dvs, zq, dzg, gq_na, gk_na, gq_sw, gk_sw, bd, name):
    s = zq.shape[0]
    d2 = dzg.shape[1]
    n_in = QKV_WIDTH + d2
    tm = _row_tile(s)

    def body(dqa_ref, dka_ref, dva_ref, dqs_ref, dks_ref, dvs_ref, zq_ref, dzg_ref,
             gqa_ref, gka_ref, gqs_ref, gks_ref, bd_ref, dz_ref, dgqa_ref, dgka_ref, dgqs_ref, dgks_ref):
        @pl.when(pl.program_id(0) == 0)
        def _():
            for r in (dgqa_ref, dgka_ref, dgqs_ref, dgks_ref):
                r[...] = jnp.zeros(r.shape, F32)

        bd512 = bd_ref[...]
        bd128 = bd_ref[0:SW_KV_WIDTH, 0:SW_KV_WIDTH]

        def one(c0, c1, dy_ref, g_ref, dg_ref, bdm, scale):
            z = zq_ref[:, c0:c1].astype(F32)
            r = lax.rsqrt(_group_mean(z * z, bdm) + EPS)
            zh = z * r
            dy = dy_ref[...] * scale
            dyg = dy * g_ref[...]
            dz = r * (dyg - zh * _group_mean(dyg * zh, bdm))
            dz_ref[:, c0:c1] = dz.astype(BF16)
            dg_ref[...] = dg_ref[...] + jnp.sum(dy * zh, axis=0, keepdims=True)

        one(0, 512, dqa_ref, gqa_ref, dgqa_ref, bd512, SCALE)
        one(512, 1024, dka_ref, gka_ref, dgka_ref, bd512, 1.0)
        dz_ref[:, 1024:1536] = dva_ref[...].astype(BF16)
        one(1536, 2048, dqs_ref, gqs_ref, dgqs_ref, bd512, SCALE)
        one(2048, 2176, dks_ref, gks_ref, dgks_ref, bd128, 1.0)
        dz_ref[:, 2176:2304] = dvs_ref[...].astype(BF16)
        dz_ref[:, QKV_WIDTH:n_in] = dzg_ref[...]

    return pl.pallas_call(
        body, name=name, grid=(s // tm,),
        in_specs=[_rows(tm, 512), _rows(tm, 512), _rows(tm, 512), _rows(tm, 512), _rows(tm, 128), _rows(tm, 128),
                  _rows(tm, QKV_WIDTH), _rows(tm, d2),
                  _full((1, 512)), _full((1, 512)), _full((1, 512)), _full((1, 128)), _full((512, 512))],
        out_specs=[_rows(tm, n_in), _full((1, 512)), _full((1, 512)), _full((1, 512)), _full((1, 128))],
        out_shape=[jax.ShapeDtypeStruct((s, n_in), BF16)] + [jax.ShapeDtypeStruct((1, 512), F32)] * 3
                  + [jax.ShapeDtypeStruct((1, 128), F32)],
        compiler_params=_params(),
    )(dqa, dka, dva, dqs, dks, dvs, zq, dzg, gq_na, gk_na, gq_sw, gk_sw, bd)


def ffn_bwd_act(dx, wd, hg, hu, name):
    s, d = dx.shape
    f = wd[0].shape[1]
    tm = _row_tile(s)
    fc = _col_chunk(f)

    def body(dx_ref, w_ref, hg_ref, hu_ref, dxb_ref, dhg_ref, dhu_ref):
        dxb = dx_ref[...].astype(BF16)
        dxb_ref[...] = dxb
        for c0 in range(0, f, fc):
            dact = 0.5 * _dotg(dxb, w_ref[c0:c0 + fc, :], NT)
            hg = hg_ref[:, c0:c0 + fc].astype(F32)
            hu = hu_ref[:, c0:c0 + fc].astype(F32)
            sg = _sigmoid(hg)
            dhu_ref[:, c0:c0 + fc] = (dact * hg * sg).astype(BF16)
            dhg_ref[:, c0:c0 + fc] = (dact * hu * sg * (1.0 + hg * (1.0 - sg))).astype(BF16)

    return pl.pallas_call(
        body, name=name, grid=(s // tm,),
        in_specs=[_rows(tm, d), _mat(*wd), _rows(tm, f), _rows(tm, f)],
        out_specs=[_rows(tm, d), _rows(tm, f), _rows(tm, f)],
        out_shape=[jax.ShapeDtypeStruct((s, d), BF16), jax.ShapeDtypeStruct((s, f), BF16),
                   jax.ShapeDtypeStruct((s, f), BF16)],
        compiler_params=_params(),
    )(dx, wd[0], hg, hu)


def proj_bwd_norm(acts, weights, x, gain, dx, dep, name):
    s, d = x.shape
    tm = _row_tile(s)
    n = len(acts)

    def body(*refs):
        a_refs, w_refs = refs[:n], refs[n:2 * n]
        x_ref, g_ref, dx_ref, _, o_ref, dg_ref = refs[2 * n:]

        @pl.when(pl.program_id(0) == 0)
        def _():
            dg_ref[...] = jnp.zeros(dg_ref.shape, F32)

        dxn = _dot(a_refs[0][...], w_refs[0][...])
        for a_ref, w_ref in zip(a_refs[1:], w_refs[1:]):
            dxn = dxn + _dot(a_ref[...], w_ref[...])
        xv = x_ref[...]
        r = _rstd(xv)
        xh = xv * r
        dxh = dxn * g_ref[...]
        o_ref[...] = dx_ref[...] + r * (dxh - xh * jnp.mean(dxh * xh, axis=-1, keepdims=True))
        dg_ref[...] = dg_ref[...] + jnp.sum(dxn * xh, axis=0, keepdims=True)

    return pl.pallas_call(
        body, name=name, grid=(s // tm,),
        in_specs=[_rows(tm, a.shape[1]) for a in acts] + [_mat(*w) for w in weights]
                 + [_rows(tm, d), _full((1, d)), _rows(tm, d), _full(dep.shape)],
        out_specs=[_rows(tm, d), _full((1, d))],
        out_shape=[jax.ShapeDtypeStruct((s, d), F32), jax.ShapeDtypeStruct((1, d), F32)],
        compiler_params=_params(),
    )(*acts, *[w[0] for w in weights], x, gain, dx, dep)


def tn_matmul(a, b, scale, name):
    s, n = a.shape
    k = b.shape[1]
    tn = _tn_tile(n)

    def body(a_ref, b_ref, o_ref):
        o_ref[...] = (scale * _dotg(a_ref[...], b_ref[...], TN)).astype(BF16)

    return pl.pallas_call(
        body, name=name, grid=(n // tn,),
        in_specs=[pl.BlockSpec((s, tn), lambda i: (0, i)),
                  pl.BlockSpec((s, k), lambda i: (0, 0), pipeline_mode=ONCE)],
        out_specs=pl.BlockSpec((tn, k), lambda i: (i, 0)),
        out_shape=jax.ShapeDtypeStruct((n, k), BF16),
        compiler_params=_params(),
    )(a, b)


def loss_grad(y, target, name):
    s, d = y.shape
    tm = _row_tile(s)

    def body(y_ref, t_ref, dy_ref, acc_ref):
        @pl.when(pl.program_id(0) == 0)
        def _():
            acc_ref[...] = jnp.zeros(acc_ref.shape, F32)

        err = y_ref[...] - t_ref[...]
        dy_ref[...] = err * (1.0 / d)
        e2 = err * err
        part = jnp.sum(e2.reshape(tm // 8, 8, d), axis=0)
        acc = part[:, 0:LANES]
        for c0 in range(LANES, d, LANES):
            acc = acc + part[:, c0:c0 + LANES]
        acc_ref[...] = acc_ref[...] + acc

    return pl.pallas_call(
        body, name=name, grid=(s // tm,),
        in_specs=[_rows(tm, d), _rows(tm, d)],
        out_specs=[_rows(tm, d), _full((8, LANES))],
        out_shape=[jax.ShapeDtypeStruct((s, d), F32), jax.ShapeDtypeStruct((8, LANES), F32)],
        compiler_params=_params(),
    )(y, target)


def _mesh_pos():
    return lax.axis_index("x"), lax.axis_index("y"), lax.axis_index("c")


def gather_weights(shards):
    n = len(shards)

    def body(*refs):
        ins, outs = refs[:n], refs[n:2 * n]
        send_sems, recv_sems, local_sems = refs[2 * n:]
        x, y, c = _mesh_pos()
        me, sibling = (x, y, c), (x, y, 1 - c)
        chips = [(1 - x, y), (x, 1 - y), (1 - x, 1 - y)]

        def slot(a, px, py, pc):
            return outs[a].at[:, 4 * px + 2 * py + pc]

        def copy(a, k, block, to, src=None):
            return pltpu.make_async_remote_copy(
                src_ref=slot(a, *block) if src is None else src, dst_ref=slot(a, *block),
                send_sem=send_sems.at[a, k], recv_sem=recv_sems.at[a, k], device_id=to, device_id_type=MESH)

        mine = [pltpu.make_async_copy(ins[a], slot(a, *me), local_sems.at[a]) for a in range(n)]
        for cp in mine:
            cp.start()
        first = []
        for a in range(n):
            first.append(copy(a, 0, me, sibling, src=ins[a]))
            first += [copy(a, 1 + j, me, (*chip, c), src=ins[a]) for j, chip in enumerate(chips)]
        for cp in first:
            cp.start()
        passed = []
        for j, chip in enumerate(chips):
            for a in range(n):
                copy(a, 1 + j, (*chip, c), me).wait_recv()
                fwd = copy(a, 4 + j, (*chip, c), sibling)
                fwd.start()
                passed.append(fwd)
        for a in range(n):
            copy(a, 0, sibling, me).wait_recv()
            for j, chip in enumerate(chips):
                copy(a, 4 + j, (*chip, 1 - c), me).wait_recv()
        for cp in first + passed:
            cp.wait_send()
        for cp in mine:
            cp.wait()

    any_spec = pl.BlockSpec(memory_space=pl.ANY)
    return pl.pallas_call(
        body, name="gather_weights",
        in_specs=[any_spec] * n, out_specs=[any_spec] * n,
        out_shape=[jax.ShapeDtypeStruct((w.shape[0], N_DEV) + w.shape[1:], w.dtype) for w in shards],
        scratch_shapes=[pltpu.SemaphoreType.DMA((n, 7)), pltpu.SemaphoreType.DMA((n, 7)),
                        pltpu.SemaphoreType.DMA((n,))],
        compiler_params=pltpu.CompilerParams(has_side_effects=True),
    )(*shards)


def _peers():
    x, y, c = _mesh_pos()
    peers = []
    for rel in range(1, N_DEV):
        peers.append((1 - x if rel & 4 else x, 1 - y if rel & 2 else y, 1 - c if rel & 1 else c))
    return 4 * x + 2 * y + c, peers


HBM_SPEC = pl.BlockSpec(memory_space=pltpu.HBM)
SEM_SPEC = pl.BlockSpec(memory_space=pltpu.SEMAPHORE)


def _split_call(body, name, thru, n_sems, extra=(), with_token=True):
    hbm = lambda t: pltpu.with_memory_space_constraint(t, pltpu.HBM)
    effect = pltpu.CompilerParams(has_side_effects=pltpu.SideEffectType.DATAFLOW_SIDE_EFFECTING)
    nt = len(thru)
    thru_shapes = [pltpu.HBM(t.shape, t.dtype) for t in thru]
    if with_token:
        (after,) = extra
        outs = pl.pallas_call(
            body, name=name, in_specs=[HBM_SPEC] * nt + [pl.BlockSpec(memory_space=pl.ANY)],
            out_specs=[SEM_SPEC] * len(n_sems) + [HBM_SPEC] * nt + [pl.BlockSpec(memory_space=pltpu.VMEM)],
            out_shape=[pltpu.SemaphoreType.DMA((k,)) for k in n_sems] + thru_shapes
                      + [jax.ShapeDtypeStruct((8, LANES), F32)],
            input_output_aliases={i: len(n_sems) + i for i in range(nt)}, compiler_params=effect,
        )(*[hbm(t) for t in thru], after)
        return outs[:len(n_sems)], outs[len(n_sems):-1], outs[-1]
    return pl.pallas_call(
        body, name=name,
        in_specs=[HBM_SPEC] * nt + [SEM_SPEC] * len(n_sems) + [pl.BlockSpec(memory_space=pl.ANY)],
        out_specs=[HBM_SPEC] * nt, out_shape=thru_shapes,
        input_output_aliases={i: i for i in range(nt)}, compiler_params=effect,
    )(*thru, *extra)


def _gather_targets():
    x, y, c = _mesh_pos()
    return 4 * x + 2 * y + c, [(x, y, 1 - c), (1 - x, y, c), (x, 1 - y, c), (1 - x, 1 - y, c)]


def gather_start(shards, after, name):
    n = len(shards)
    zones = [lax.empty((w.shape[0], N_DEV) + w.shape[1:], w.dtype) for w in shards]

    def body(*refs):
        ins, zs = refs[:n], refs[n:2 * n]
        send_sems, recv_sems, local_sems = refs[2 * n + 1:2 * n + 4]
        token = refs[-1]
        me, targets = _gather_targets()
        for a in range(n):
            pltpu.make_async_copy(ins[a], zs[a].at[:, me], local_sems.at[a]).start()
            for k, to in enumerate(targets):
                pltpu.make_async_remote_copy(
                    src_ref=ins[a], dst_ref=zs[a].at[:, me], send_sem=send_sems.at[4 * a + k],
                    recv_sem=recv_sems.at[4 * a + k], device_id=to, device_id_type=MESH).start()
        token[...] = jnp.zeros(token.shape, F32)

    sems, thru, token = _split_call(body, name, list(shards) + zones, (4 * n, 4 * n, n), extra=(after,))
    return (sems, thru, n), token


def gather_wait(started, after, name):
    sems, thru, n = started

    def body(*refs):
        zs = refs[n:2 * n]
        send_sems, recv_sems, local_sems = refs[2 * n:2 * n + 3]
        _, targets = _gather_targets()
        for a in range(n):
            for k, to in enumerate(targets):
                cp = pltpu.make_async_remote_copy(
                    src_ref=zs[a].at[:, 0], dst_ref=zs[a].at[:, 0], send_sem=send_sems.at[4 * a + k],
                    recv_sem=recv_sems.at[4 * a + k], device_id=to, device_id_type=MESH)
                cp.wait_send()
                cp.wait_recv()
            pltpu.make_async_copy(zs[a].at[:, 0], zs[a].at[:, 0], local_sems.at[a]).wait()

    return _split_call(body, name, thru, (4 * n, 4 * n, n), extra=(*sems, after), with_token=False)[n:]


def forward_start(zones, after, name):
    n = len(zones)

    def body(*refs):
        zs = refs[:n]
        send_sems, recv_sems = refs[n + 1:n + 3]
        token = refs[-1]
        x, y, c = _mesh_pos()
        for a in range(n):
            for j, chip in enumerate([(1 - x, y), (x, 1 - y), (1 - x, 1 - y)]):
                blk = zs[a].at[:, 4 * chip[0] + 2 * chip[1] + c]
                pltpu.make_async_remote_copy(
                    src_ref=blk, dst_ref=blk, send_sem=send_sems.at[3 * a + j], recv_sem=recv_sems.at[3 * a + j],
                    device_id=(x, y, 1 - c), device_id_type=MESH).start()
        token[...] = jnp.zeros(token.shape, F32)

    sems, thru, token = _split_call(body, name, list(zones), (3 * n, 3 * n), extra=(after,))
    return (sems, thru, n), token


def forward_wait(started, after, name):
    sems, thru, n = started

    def body(*refs):
        zs = refs[:n]
        send_sems, recv_sems = refs[n:n + 2]
        x, y, c = _mesh_pos()
        for a in range(n):
            for j in range(3):
                cp = pltpu.make_async_remote_copy(
                    src_ref=zs[a].at[:, 0], dst_ref=zs[a].at[:, 0], send_sem=send_sems.at[3 * a + j],
                    recv_sem=recv_sems.at[3 * a + j], device_id=(x, y, 1 - c), device_id_type=MESH)
                cp.wait_send()
                cp.wait_recv()

    return _split_call(body, name, thru, (3 * n, 3 * n), extra=(*sems, after), with_token=False)


def scatter_start(groups, name):
    n = len(groups)
    flat = [g for grp in groups for g in grp]
    nf = len(flat)
    offs = np.cumsum([0] + [len(grp) for grp in groups])
    lands = [lax.empty((N_DEV, len(grp)) + grp[0].shape[1:], grp[0].dtype) for grp in groups]

    def body(*refs):
        ins, zones = refs[:nf], refs[nf:nf + n]
        send_sems, recv_sems, local_sems = refs[nf + n:nf + n + 3]
        token = refs[-1]
        me, peers = _peers()
        for a in range(n):
            for w in range(len(groups[a])):
                pltpu.make_async_copy(ins[offs[a] + w].at[me], zones[a].at[me, w], local_sems.at[a]).start()
        for k, peer in enumerate(peers):
            p_id = 4 * peer[0] + 2 * peer[1] + peer[2]
            for a in range(n):
                for w in range(len(groups[a])):
                    pltpu.make_async_remote_copy(
                        src_ref=ins[offs[a] + w].at[p_id], dst_ref=zones[a].at[me, w],
                        send_sem=send_sems.at[7 * a + k], recv_sem=recv_sems.at[7 * a + k],
                        device_id=peer, device_id_type=MESH).start()
        token[...] = jnp.zeros(token.shape, F32)

    hbm = lambda t: pltpu.with_memory_space_constraint(t, pltpu.HBM)
    outs = pl.pallas_call(
        body, name=name,
        in_specs=[HBM_SPEC] * (nf + n),
        out_specs=[SEM_SPEC] * 3 + [HBM_SPEC] * (nf + n) + [pl.BlockSpec(memory_space=pltpu.VMEM)],
        out_shape=[pltpu.SemaphoreType.DMA((7 * n,)), pltpu.SemaphoreType.DMA((7 * n,)), pltpu.SemaphoreType.DMA((n,))]
                  + [pltpu.HBM(t.shape, t.dtype) for t in flat + lands]
                  + [jax.ShapeDtypeStruct((8, LANES), F32)],
        input_output_aliases={i: 3 + i for i in range(nf + n)},
        compiler_params=pltpu.CompilerParams(has_side_effects=pltpu.SideEffectType.DATAFLOW_SIDE_EFFECTING),
    )(*[hbm(t) for t in flat], *[hbm(t) for t in lands])
    sems, thru, token = outs[:3], outs[3:3 + nf + n], outs[-1]
    return (sems, thru, [len(grp) for grp in groups]), token


def scatter_wait(started, after, name):
    (send_sems, recv_sems, local_sems), thru, sizes = started
    n = len(sizes)
    nf = len(thru) - n

    def body(*refs):
        zones = refs[nf:nf + n]
        s_sems, r_sems, l_sems = refs[nf + n:nf + n + 3]
        me, peers = _peers()
        for a in range(n):
            for k, peer in enumerate(peers):
                cp = pltpu.make_async_remote_copy(
                    src_ref=zones[a].at[0], dst_ref=zones[a].at[0],
                    send_sem=s_sems.at[7 * a + k], recv_sem=r_sems.at[7 * a + k], device_id=peer,
                    device_id_type=MESH)
                cp.wait_send()
                cp.wait_recv()
            pltpu.make_async_copy(zones[a].at[0], zones[a].at[0], l_sems.at[a]).wait()

    outs = pl.pallas_call(
        body, name=name,
        in_specs=[HBM_SPEC] * (nf + n) + [SEM_SPEC] * 3 + [pl.BlockSpec(memory_space=pl.ANY)],
        out_specs=[HBM_SPEC] * (nf + n),
        out_shape=[pltpu.HBM(t.shape, t.dtype) for t in thru],
        input_output_aliases={i: i for i in range(nf + n)},
        compiler_params=pltpu.CompilerParams(has_side_effects=pltpu.SideEffectType.DATAFLOW_SIDE_EFFECTING),
    )(*thru, send_sems, recv_sems, local_sems, after)
    return outs[nf:]


def share_small(small, after):
    def body(s_ref, after_ref, o_ref, send_sems, recv_sems, local_sem):
        me, peers = _peers()
        mine = pltpu.make_async_copy(s_ref, o_ref.at[me], local_sem)
        mine.start()
        copies = [pltpu.make_async_remote_copy(src_ref=s_ref, dst_ref=o_ref.at[me], send_sem=send_sems.at[k],
                                               recv_sem=recv_sems.at[k], device_id=peer, device_id_type=MESH)
                  for k, peer in enumerate(peers)]
        for cp in copies:
            cp.start()
        for cp in copies:
            cp.wait()
        mine.wait()

    vm = pl.BlockSpec(memory_space=pltpu.VMEM)
    return pl.pallas_call(
        body, name="share_small", in_specs=[vm, pl.BlockSpec(memory_space=pl.ANY)], out_specs=vm,
        out_shape=jax.ShapeDtypeStruct((N_DEV,) + small.shape, small.dtype),
        scratch_shapes=[pltpu.SemaphoreType.DMA((7,)), pltpu.SemaphoreType.DMA((7,)), pltpu.SemaphoreType.DMA],
    )(small, after)


def sum_sources(recv, name):
    _, w, r, c = recv.shape

    def body(r_ref, o_ref):
        acc = r_ref[0, 0].astype(F32)
        for src in range(1, N_DEV):
            acc = acc + r_ref[src, 0].astype(F32)
        o_ref[0] = acc

    return pl.pallas_call(
        body, name=name, grid=(w,),
        in_specs=[pl.BlockSpec((N_DEV, 1, r, c), lambda i: (0, i, 0, 0))],
        out_specs=pl.BlockSpec((1, r, c), lambda i: (i, 0, 0)),
        out_shape=jax.ShapeDtypeStruct((w, r, c), F32),
        compiler_params=_params(),
    )(recv)


def _adamw_math(w, g, m, v):
    m = ADAM_B1 * m + (1.0 - ADAM_B1) * g
    v = ADAM_B2 * v + (1.0 - ADAM_B2) * (g * g)
    m_hat = m / (1.0 - ADAM_B1 ** ADAM_STEP)
    v_hat = v / (1.0 - ADAM_B2 ** ADAM_STEP)
    delta = -ADAM_LR * (m_hat / (jnp.sqrt(v_hat) + ADAM_EPS) + ADAM_WD * w)
    return delta, m, v


def adamw(w, g, m, v, name):
    shape = w.shape
    c = shape[-1]
    r = int(np.prod(shape[:-1]))
    w2, g2, m2, v2 = (t.reshape(r, c) for t in (w, g, m, v))
    tr = next(t for t in range(min(r, 512), 0, -1) if r % t == 0 and (t % 8 == 0 or t == r))

    def body(w_ref, g_ref, m_ref, v_ref, d_ref, mo_ref, vo_ref):
        d_ref[...], mo_ref[...], vo_ref[...] = _adamw_math(w_ref[...], g_ref[...], m_ref[...], v_ref[...])

    spec = pl.BlockSpec((tr, c), lambda i: (i, 0))
    outs = pl.pallas_call(
        body, name=name, grid=(r // tr,),
        in_specs=[spec] * 4, out_specs=[spec] * 3,
        out_shape=[jax.ShapeDtypeStruct((r, c), F32)] * 3,
        compiler_params=_params(),
    )(w2, g2, m2, v2)
    return tuple(t.reshape(shape) for t in outs)


def adamw_small(w, recv, m, v, name):
    def body(w_ref, r_ref, m_ref, v_ref, g_ref, d_ref, mo_ref, vo_ref):
        g = r_ref[0]
        for src in range(1, N_DEV):
            g = g + r_ref[src]
        g_ref[...] = g
        d_ref[...], mo_ref[...], vo_ref[...] = _adamw_math(w_ref[...], g, m_ref[...], v_ref[...])

    vm = pl.BlockSpec(memory_space=pltpu.VMEM)
    return pl.pallas_call(
        body, name=name, in_specs=[vm] * 4, out_specs=[vm] * 4,
        out_shape=[jax.ShapeDtypeStruct(w.shape, F32)] * 4,
        compiler_params=pltpu.CompilerParams(vmem_limit_bytes=V7X_VMEM_LIMIT),
    )(w, recv, m, v)


SMALL_NAMES = ("ffn1_norm", "mix_norm", "ffn2_norm", "b_gate", "na_q_norm", "na_k_norm", "sw_q_norm", "sw_k_norm",
               "na_rpb", "sw_sink", "t5_rel_table")


def _pack_small(parts):
    flat = jnp.concatenate([parts[k].reshape(-1).astype(F32) for k in SMALL_NAMES])
    n = flat.shape[0]
    rows = -(-n // (8 * LANES)) * 8
    return jnp.pad(flat, (0, rows * LANES - n)).reshape(rows, LANES)


def _unpack_small(packed, like):
    flat = packed.reshape(-1)
    out, off = {}, 0
    for k in SMALL_NAMES:
        n = int(np.prod(like[k].shape))
        out[k] = flat[off:off + n].reshape(like[k].shape)
        off += n
    return out


def kernel(x, ffn1_norm, ffn1_w_gate, ffn1_w_up, ffn1_w_down, mix_norm, w_in, b_gate, na_q_norm, na_k_norm, na_rpb, sw_q_norm, sw_k_norm, sw_sink, t5_rel_table, w_branch_na, w_branch_sw, w_out, ffn2_norm, ffn2_w_gate, ffn2_w_up, ffn2_w_down, loss_target, m_ffn1_norm, m_ffn1_w_gate, m_ffn1_w_up, m_ffn1_w_down, m_mix_norm, m_w_in, m_b_gate, m_na_q_norm, m_na_k_norm, m_na_rpb, m_sw_q_norm, m_sw_k_norm, m_sw_sink, m_t5_rel_table, m_w_branch_na, m_w_branch_sw, m_w_out, m_ffn2_norm, m_ffn2_w_gate, m_ffn2_w_up, m_ffn2_w_down, v_ffn1_norm, v_ffn1_w_gate, v_ffn1_w_up, v_ffn1_w_down, v_mix_norm, v_w_in, v_b_gate, v_na_q_norm, v_na_k_norm, v_na_rpb, v_sw_q_norm, v_sw_k_norm, v_sw_sink, v_t5_rel_table, v_w_branch_na, v_w_branch_sw, v_w_out, v_ffn2_norm, v_ffn2_w_gate, v_ffn2_w_up, v_ffn2_w_down):
    weights = dict(ffn1_norm=ffn1_norm, ffn1_w_gate=ffn1_w_gate, ffn1_w_up=ffn1_w_up, ffn1_w_down=ffn1_w_down,
                   mix_norm=mix_norm, w_in=w_in, b_gate=b_gate, na_q_norm=na_q_norm, na_k_norm=na_k_norm,
                   na_rpb=na_rpb, sw_q_norm=sw_q_norm, sw_k_norm=sw_k_norm, sw_sink=sw_sink,
                   t5_rel_table=t5_rel_table, w_branch_na=w_branch_na, w_branch_sw=w_branch_sw, w_out=w_out,
                   ffn2_norm=ffn2_norm, ffn2_w_gate=ffn2_w_gate, ffn2_w_up=ffn2_w_up, ffn2_w_down=ffn2_w_down)
    mom_m = dict(ffn1_norm=m_ffn1_norm, ffn1_w_gate=m_ffn1_w_gate, ffn1_w_up=m_ffn1_w_up, ffn1_w_down=m_ffn1_w_down,
                 mix_norm=m_mix_norm, w_in=m_w_in, b_gate=m_b_gate, na_q_norm=m_na_q_norm, na_k_norm=m_na_k_norm,
                 na_rpb=m_na_rpb, sw_q_norm=m_sw_q_norm, sw_k_norm=m_sw_k_norm, sw_sink=m_sw_sink,
                 t5_rel_table=m_t5_rel_table, w_branch_na=m_w_branch_na, w_branch_sw=m_w_branch_sw, w_out=m_w_out,
                 ffn2_norm=m_ffn2_norm, ffn2_w_gate=m_ffn2_w_gate, ffn2_w_up=m_ffn2_w_up, ffn2_w_down=m_ffn2_w_down)
    mom_v = dict(ffn1_norm=v_ffn1_norm, ffn1_w_gate=v_ffn1_w_gate, ffn1_w_up=v_ffn1_w_up, ffn1_w_down=v_ffn1_w_down,
                 mix_norm=v_mix_norm, w_in=v_w_in, b_gate=v_b_gate, na_q_norm=v_na_q_norm, na_k_norm=v_na_k_norm,
                 na_rpb=v_na_rpb, sw_q_norm=v_sw_q_norm, sw_k_norm=v_sw_k_norm, sw_sink=v_sw_sink,
                 t5_rel_table=v_t5_rel_table, w_branch_na=v_w_branch_na, w_branch_sw=v_w_branch_sw, w_out=v_w_out,
                 ffn2_norm=v_ffn2_norm, ffn2_w_gate=v_ffn2_w_gate, ffn2_w_up=v_ffn2_w_up, ffn2_w_down=v_ffn2_w_down)
    order = list(weights)

    depth = ffn1_norm.shape[0]
    s, d = x.shape[1], x.shape[2]
    xs = x[0]
    tr = lambda w: jnp.swapaxes(w, -1, -2)

    a_loc = jnp.stack([t for l in range(depth) for t in (
        tr(ffn1_w_gate[l]), tr(ffn1_w_up[l]), ffn1_w_down[l],
        tr(ffn2_w_gate[l]), tr(ffn2_w_up[l]), ffn2_w_down[l])]).astype(BF16)
    b_loc = tr(w_in).astype(BF16)
    c_loc = w_out.astype(BF16)
    d_loc = jnp.stack([t for l in range(depth) for t in (tr(w_branch_na[l]), tr(w_branch_sw[l]))]).astype(BF16)
    merge = lambda t: t.reshape(t.shape[0], N_DEV * t.shape[2], t.shape[3])
    no_dep = jnp.zeros((8, LANES), F32)

    def shards_of(kind, l):
        if kind == "ffn1":
            return [a_loc[6 * l:6 * l + 3]]
        if kind == "win":
            return [b_loc[l:l + 1]]
        return [a_loc[6 * l + 3:6 * l + 6], c_loc[l:l + 1], d_loc[2 * l:2 * l + 2]]

    def start(kind, l, after):
        return gather_start(shards_of(kind, l), after, f"gather_{kind}_{l}")

    def arrive(started, kind, l, after):
        zones = gather_wait(started, after, f"gather_{kind}_{l}_wait")
        return forward_start(zones, no_dep, f"forward_{kind}_{l}")

    def finish(fwd, kind, l, after):
        return [merge(z) for z in forward_wait(fwd, after, f"forward_{kind}_{l}_wait")]

    bd = jnp.asarray(np.kron(np.eye(NA_WIDTH // HEAD_DIM), np.full((HEAD_DIM, HEAD_DIM), 1.0 / HEAD_DIM)), BF16)
    bmap = jnp.asarray(_t5_bucket_map())
    tile8 = lambda g: jnp.tile(g, NA_WIDTH // HEAD_DIM).reshape(1, NA_WIDTH)
    tile2 = lambda g: jnp.tile(g, SW_KV_WIDTH // HEAD_DIM).reshape(1, SW_KV_WIDTH)

    first = merge(gather_weights([a_loc[0:3]])[0])
    st_win, dep = start("win", 0, first)
    t5b = t5_expand(t5_rel_table, bmap, "t5_expand")

    saved = []
    layer_w = {0: dict(wg1=(first, 0), wu1=(first, 1), wd1=(first, 2))}
    cur = xs
    for l in range(depth):
        sv = {}
        lw = layer_w[l]
        sv["x0"] = cur
        sv["xn1"], sv["hg1"], sv["hu1"], sv["act1"] = ffn_up(cur, ffn1_norm[l][None], lw["wg1"], lw["wu1"], dep,
                                                             f"ffn1_up_{l}")
        cur = ffn_down(cur, sv["act1"], lw["wd1"], no_dep, f"ffn1_down_{l}")
        sv["x1"] = cur
        fwd, _ = arrive(st_win, "win", l, cur)
        st_rest, tok = start("rest", l, cur)
        (zb,) = finish(fwd, "win", l, tok)
        lw["win"] = (zb, 0)
        sv["gains"] = (tile8(na_q_norm[l]), tile8(na_k_norm[l]), tile8(sw_q_norm[l]), tile2(sw_k_norm[l]))
        sv["hn"], sv["zq"], sv["qa"], sv["ka"], sv["qs"], sv["ks"], sv["gt"] = mix_in(
            cur, mix_norm[l][None], lw["win"], b_gate[l][None], *sv["gains"], bd, f"mix_in_{l}")
        sv["t2"] = rpb_expand(na_rpb[l].reshape(-1), na_rpb.shape[1], no_dep, f"rpb_expand_{l}")
        sv["o_na"] = na_fwd(sv["qa"], sv["ka"], sv["zq"], sv["t2"], f"na_fwd_{l}")
        dep = no_dep
        if l + 1 < depth:
            st_ffn1, dep = start("ffn1", l + 1, sv["o_na"])
        sv["o_sw"] = sw_fwd(sv["qs"], sv["ks"], sv["zq"], t5b, sw_sink[l], dep, f"sw_fwd_{l}")
        fwd, tok = arrive(st_rest, "rest", l, sv["o_sw"])
        za, zc, zd = finish(fwd, "rest", l, tok)
        lw.update(wg2=(za, 0), wu2=(za, 1), wd2=(za, 2), wout=(zc, 0), wna=(zd, 0), wsw=(zd, 1))
        cur, sv["a_na"], sv["a_sw"], sv["merged"] = merge_out(
            cur, sv["o_na"], sv["o_sw"], sv["gt"], lw["wna"], lw["wsw"], lw["wout"], f"merge_out_{l}")
        sv["x2"] = cur
        dep = no_dep
        if l + 1 < depth:
            st_win, dep = start("win", l + 1, cur)
        sv["xn2"], sv["hg2"], sv["hu2"], sv["act2"] = ffn_up(cur, ffn2_norm[l][None], lw["wg2"], lw["wu2"], dep,
                                                             f"ffn2_up_{l}")
        dep = no_dep
        if l + 1 < depth:
            fwd, dep = arrive(st_ffn1, "ffn1", l + 1, sv["act2"])
        cur = ffn_down(cur, sv["act2"], lw["wd2"], dep, f"ffn2_down_{l}")
        dep = no_dep
        if l + 1 < depth:
            (za,) = finish(fwd, "ffn1", l + 1, cur)
            layer_w[l + 1] = dict(wg1=(za, 0), wu1=(za, 1), wd1=(za, 2))
        saved.append(sv)

    dx, loss_acc = loss_grad(cur, loss_target[0], "loss_grad")
    loss = lax.psum(jnp.sum(loss_acc) * (0.5 / d), ("x", "y", "c"))

    split = lambda t: t.reshape(N_DEV, t.shape[0] // N_DEV, t.shape[1])
    pending = {}
    small = {k: [None] * depth for k in SMALL_NAMES if k != "t5_rel_table"}
    dbias_sw = []
    for l in reversed(range(depth)):
        sv = saved[l]
        lw = layer_w[l]
        wg1, wu1, wd1, wg2, wu2, wd2 = (lw[k] for k in ("wg1", "wu1", "wd1", "wg2", "wu2", "wd2"))
        win_t, wout_l, wna_t, wsw_t = lw["win"], lw["wout"], lw["wna"], lw["wsw"]
        blocks = ((2, "x2", "xn2", "hg2", "hu2", "act2", wg2, wu2, wd2, "ffn2_norm", 3),
                  (1, "x0", "xn1", "hg1", "hu1", "act1", wg1, wu1, wd1, "ffn1_norm", 0))

        def ffn_backward(dx, blk):
            tag, xk, xnk, hgk, huk, actk, wg, wu, wd, norm_name, slot = blk
            gains = weights[norm_name]
            dxb, dhg, dhu = ffn_bwd_act(dx, wd, sv[hgk], sv[huk], f"ffn{tag}_bwd_act_{l}")
            gwd = tn_matmul(sv[actk], dxb, 0.5, f"ffn{tag}_dwd_{l}")
            gwg = tn_matmul(dhg, sv[xnk], 1.0, f"ffn{tag}_dwg_{l}")
            gwu = tn_matmul(dhu, sv[xnk], 1.0, f"ffn{tag}_dwu_{l}")
            pending[f"ffn{tag}_{l}"], token = scatter_start([[split(gwg), split(gwu), split(gwd)]],
                                                            f"scatter_ffn{tag}_{l}")
            dx, dg = proj_bwd_norm([dhg, dhu], [wg, wu], sv[xk], gains[l][None], dx, token, f"ffn{tag}_bwd_x_{l}")
            small[norm_name][l] = dg[0]
            return dx

        dx = ffn_backward(dx, blocks[0])
        dxb, dzg, da_na, da_sw, do_na, do_sw, dbg = mix_bwd_out(
            dx, sv["gt"], sv["a_na"], sv["a_sw"], wna_t, wsw_t, wout_l, f"mix_bwd_out_{l}")
        small["b_gate"][l] = dbg[0]
        gwout = tn_matmul(sv["merged"], dxb, 1.0, f"dwout_{l}")
        gwna = tn_matmul(da_na, sv["o_na"], 1.0, f"dwna_{l}")
        gwsw = tn_matmul(da_sw, sv["o_sw"], 1.0, f"dwsw_{l}")
        dqa, dka, dva, dt2 = na_bwd(sv["qa"], sv["ka"], sv["zq"], sv["t2"], sv["o_na"], do_na, f"na_bwd_{l}")
        dqs, dks, dvs, dbias, dsink = sw_bwd(sv["qs"], sv["ks"], sv["zq"], t5b, sw_sink[l], sv["o_sw"], do_sw,
                                             f"sw_bwd_{l}")
        dbias_sw.append(dbias)
        small["sw_sink"][l] = jnp.sum(dsink[:, :, 0], axis=1)
        drpb = rpb_reduce(dt2, f"rpb_reduce_{l}")
        small["na_rpb"][l] = drpb[:, :, :2 * NA_COLS - 1, 0]
        dz, dgqa, dgka, dgqs, dgks = qk_norm_bwd(dqa, dka, dva, dqs, dks, dvs, sv["zq"], dzg, *sv["gains"], bd,
                                                 f"qk_norm_bwd_{l}")
        fold = lambda g: jnp.sum(g.reshape(-1, HEAD_DIM), axis=0)
        small["na_q_norm"][l], small["na_k_norm"][l] = fold(dgqa), fold(dgka)
        small["sw_q_norm"][l], small["sw_k_norm"][l] = fold(dgqs), fold(dgks)
        gwin = tn_matmul(dz, sv["hn"], 1.0, f"dwin_{l}")
        pending[f"mix_{l}"], token = scatter_start([[split(gwout)], [split(gwna), split(gwsw)], [split(gwin)]],
                                                   f"scatter_mix_{l}")
        dx, dg = proj_bwd_norm([dz], [win_t], sv["x1"], mix_norm[l][None], dx, token, f"mix_bwd_x_{l}")
        small["mix_norm"][l] = dg[0]
        dx = ffn_backward(dx, blocks[1])

    dtab = t5_reduce(dbias_sw, bmap, "t5_reduce")
    small_parts = {k: jnp.stack(v) for k, v in small.items()}
    small_parts["t5_rel_table"] = jnp.transpose(dtab[:, :, 0])
    small_packed = _pack_small(small_parts)

    summed = {}
    layers = lambda f: jnp.stack([f(l) for l in range(depth)])
    grads, delta, new_m, new_v = {}, {}, {}, {}

    def collect(key, after):
        zones = scatter_wait(pending[key], after, f"wait_{key}")
        summed[key] = [sum_sources(z, f"sum_{key}_{i}") for i, z in enumerate(zones)]

    def update(k, g, transposed):
        view = tr if transposed else (lambda t: t)
        d_k, m_k, v_k = adamw(view(weights[k]), g, view(mom_m[k]), view(mom_v[k]), f"adamw_{k}")
        grads[k], delta[k], new_m[k], new_v[k] = view(g), view(d_k), view(m_k), view(v_k)
        return d_k

    last_key = "ffn1_0"
    for key in pending:
        if key != last_key:
            collect(key, dx)
    update("ffn2_w_gate", layers(lambda l: summed[f"ffn2_{l}"][0][0]), True)
    update("ffn2_w_up", layers(lambda l: summed[f"ffn2_{l}"][0][1]), True)
    update("ffn2_w_down", layers(lambda l: summed[f"ffn2_{l}"][0][2]), False)
    update("w_out", layers(lambda l: summed[f"mix_{l}"][0][0]), False)
    update("w_branch_na", layers(lambda l: summed[f"mix_{l}"][1][0]), True)
    update("w_branch_sw", layers(lambda l: summed[f"mix_{l}"][1][1]), True)
    done = update("w_in", layers(lambda l: summed[f"mix_{l}"][2][0]), True)
    collect(last_key, done)
    update("ffn1_w_gate", layers(lambda l: summed[f"ffn1_{l}"][0][0]), True)
    update("ffn1_w_up", layers(lambda l: summed[f"ffn1_{l}"][0][1]), True)
    done = update("ffn1_w_down", layers(lambda l: summed[f"ffn1_{l}"][0][2]), False)
    rs_ = share_small(small_packed, done)
    g_s, d_s, m_s, v_s = adamw_small(_pack_small(weights), rs_, _pack_small(mom_m), _pack_small(mom_v), "adamw_small")
    for dst, packed in ((grads, g_s), (delta, d_s), (new_m, m_s), (new_v, v_s)):
        dst.update(_unpack_small(packed, weights))

    return (loss, dx[None], *[grads[k] for k in order], *[delta[k] for k in order],
            *[new_m[k] for k in order], *[new_v[k] for k in order])
```

```python
import functools
import math

import numpy as np
import jax
import jax.numpy as jnp
from jax import lax
from jax.experimental import pallas as pl
from jax.experimental.pallas import tpu as pltpu

F32 = jnp.float32
BF16 = jnp.bfloat16
MESH = pl.DeviceIdType.MESH

N_DEV = 8
EPS = 1e-6
NEG = -1e30
HEAD_DIM = 64
GRID_W = 64
NA_ROWS = 8
NA_COLS = 16
NA_WIDTH = 512
SW_Q_WIDTH = 512
SW_KV_WIDTH = 128
SW_BLOCK = 128
SW_HEADS = 8
SW_REP = 4
REL_BUCKETS = 32
REL_MAX_DIST = 128
QKV_WIDTH = 3 * NA_WIDTH + SW_Q_WIDTH + 2 * SW_KV_WIDTH
SCALE = 1.0 / math.sqrt(HEAD_DIM)

ADAM_LR = 0.001
ADAM_B1 = 0.9
ADAM_B2 = 0.999
ADAM_EPS = 1e-08
ADAM_WD = 0.01
ADAM_STEP = 10

V7X_VMEM_LIMIT = 56 * 1024 * 1024
LANES = 128
MXU_TILE = 256

NT = (((1,), (1,)), ((), ()))
TN = (((0,), (0,)), ((), ()))


def _params(n_grid=1):
    return pltpu.CompilerParams(dimension_semantics=("arbitrary",) * n_grid,
                                vmem_limit_bytes=V7X_VMEM_LIMIT)


def _row_tile(s):
    for t in (512, 256, 128, 64, 32, 16, 8):
        if s % t == 0:
            return t
    raise ValueError(s)


def _tn_tile(n):
    best = max(t for t in range(LANES, min(n, 2304) + 1, LANES) if n % t == 0) if n % LANES == 0 else n
    return best // 2 if best == n and n >= 1024 else best


ONCE = pl.Buffered(1)


def _col_chunk(n):
    return MXU_TILE if n % MXU_TILE == 0 else n


def _dot(a, b):
    return jnp.dot(a, b, preferred_element_type=F32)


def _dotg(a, b, dn):
    return lax.dot_general(a, b, dn, preferred_element_type=F32)


def _sigmoid(v):
    return 1.0 / (1.0 + jnp.exp(-v))


def _rstd(xv):
    return lax.rsqrt(jnp.mean(xv * xv, axis=-1, keepdims=True) + EPS)


def _full(shape):
    nd = len(shape)
    return pl.BlockSpec(shape, lambda i, _n=nd: (0,) * _n)


def _rows(tm, width):
    return pl.BlockSpec((tm, width), lambda i: (i, 0))


def _mat(stack, idx):
    return pl.BlockSpec((None,) + tuple(stack.shape[1:]), lambda i, _w=idx: (_w, 0, 0), pipeline_mode=ONCE)


def _group_mean(v, bd):
    hi = v.astype(BF16)
    lo = (v - hi.astype(F32)).astype(BF16)
    return _dot(hi, bd) + _dot(lo, bd)


def ffn_up(x, gain, wg_t, wu_t, dep, name):
    s, d = x.shape
    f = wg_t[0].shape[1]
    tm = _row_tile(s)
    fc = _col_chunk(f)

    def body(x_ref, g_ref, wg_ref, wu_ref, dep_ref, xn_ref, hg_ref, hu_ref, act_ref):
        xv = x_ref[...]
        xn = (xv * _rstd(xv) * g_ref[...]).astype(BF16)
        xn_ref[...] = xn
        for c0 in range(0, f, fc):
            hg = _dotg(xn, wg_ref[c0:c0 + fc, :], NT)
            hu = _dotg(xn, wu_ref[c0:c0 + fc, :], NT)
            hg_ref[:, c0:c0 + fc] = hg.astype(BF16)
            hu_ref[:, c0:c0 + fc] = hu.astype(BF16)
            act_ref[:, c0:c0 + fc] = (hg * _sigmoid(hg) * hu).astype(BF16)

    return pl.pallas_call(
        body, name=name, grid=(s // tm,),
        in_specs=[_rows(tm, d), _full((1, d)), _mat(*wg_t), _mat(*wu_t), _full(dep.shape)],
        out_specs=[_rows(tm, d), _rows(tm, f), _rows(tm, f), _rows(tm, f)],
        out_shape=[jax.ShapeDtypeStruct((s, d), BF16)] + [jax.ShapeDtypeStruct((s, f), BF16)] * 3,
        compiler_params=_params(),
    )(x, gain, wg_t[0], wu_t[0], dep)


def ffn_down(x, act, wd, dep, name):
    s, d = x.shape
    f = act.shape[1]
    tm = _row_tile(s)

    def body(x_ref, a_ref, w_ref, dep_ref, o_ref):
        o_ref[...] = x_ref[...] + 0.5 * _dot(a_ref[...], w_ref[...])

    return pl.pallas_call(
        body, name=name, grid=(s // tm,),
        in_specs=[_rows(tm, d), _rows(tm, f), _mat(*wd), _full(dep.shape)],
        out_specs=_rows(tm, d),
        out_shape=jax.ShapeDtypeStruct((s, d), F32),
        compiler_params=_params(),
    )(x, act, wd[0], dep)


def mix_in(x, gain, win_t, b_gate, gq_na, gk_na, gq_sw, gk_sw, bd, name):
    s, d = x.shape
    tm = _row_tile(s)
    gc = _col_chunk(2 * d)

    def body(x_ref, g_ref, w_ref, b_ref, gqa_ref, gka_ref, gqs_ref, gks_ref, bd_ref,
             hn_ref, zq_ref, qa_ref, ka_ref, qs_ref, ks_ref, gt_ref):
        xv = x_ref[...]
        hn = (xv * _rstd(xv) * g_ref[...]).astype(BF16)
        hn_ref[...] = hn

        def proj(c0, c1):
            return _dotg(hn, w_ref[c0:c1, :], NT)

        def headnorm(z, g, bdm):
            return z * lax.rsqrt(_group_mean(z * z, bdm) + EPS) * g

        bd512 = bd_ref[...]
        bd128 = bd_ref[0:SW_KV_WIDTH, 0:SW_KV_WIDTH]
        z = proj(0, 512)
        zq_ref[:, 0:512] = z.astype(BF16)
        qa_ref[...] = (headnorm(z, gqa_ref[...], bd512) * SCALE).astype(BF16)
        z = proj(512, 1024)
        zq_ref[:, 512:1024] = z.astype(BF16)
        ka_ref[...] = headnorm(z, gka_ref[...], bd512).astype(BF16)
        z = proj(1024, 1536)
        zq_ref[:, 1024:1536] = z.astype(BF16)
        z = proj(1536, 2048)
        zq_ref[:, 1536:2048] = z.astype(BF16)
        qs_ref[...] = (headnorm(z, gqs_ref[...], bd512) * SCALE).astype(BF16)
        z = proj(2048, 2176)
        zq_ref[:, 2048:2176] = z.astype(BF16)
        ks_ref[...] = headnorm(z, gks_ref[...], bd128).astype(BF16)
        z = proj(2176, 2304)
        zq_ref[:, 2176:2304] = z.astype(BF16)
        for c0 in range(0, 2 * d, gc):
            zg = proj(QKV_WIDTH + c0, QKV_WIDTH + c0 + gc) + b_ref[:, c0:c0 + gc]
            gt_ref[:, c0:c0 + gc] = _sigmoid(zg).astype(BF16)

    return pl.pallas_call(
        body, name=name, grid=(s // tm,),
        in_specs=[_rows(tm, d), _full((1, d)), _mat(*win_t), _full((1, 2 * d)),
                  _full((1, 512)), _full((1, 512)), _full((1, 512)), _full((1, 128)), _full((512, 512))],
        out_specs=[_rows(tm, d), _rows(tm, QKV_WIDTH), _rows(tm, 512), _rows(tm, 512), _rows(tm, 512),
                   _rows(tm, 128), _rows(tm, 2 * d)],
        out_shape=[jax.ShapeDtypeStruct((s, d), BF16), jax.ShapeDtypeStruct((s, QKV_WIDTH), BF16),
                   jax.ShapeDtypeStruct((s, 512), BF16), jax.ShapeDtypeStruct((s, 512), BF16),
                   jax.ShapeDtypeStruct((s, 512), BF16), jax.ShapeDtypeStruct((s, 128), BF16),
                   jax.ShapeDtypeStruct((s, 2 * d), BF16)],
        compiler_params=_params(),
    )(x, gain, win_t[0], b_gate, gq_na, gk_na, gq_sw, gk_sw, bd)


def _na_iotas():
    qc = lax.broadcasted_iota(jnp.int32, (GRID_W, LANES), 0)
    ln = lax.broadcasted_iota(jnp.int32, (GRID_W, LANES), 1)
    low = ln < GRID_W
    kc = jnp.where(low, ln, ln - GRID_W)
    diff = kc - qc + (NA_COLS - 1)
    qcs = jnp.clip(qc - NA_COLS // 2, 0, GRID_W - NA_COLS)
    inwin = (kc >= qcs) & (kc < qcs + NA_COLS)
    return diff, low, inwin


NA_RI = 2 * NA_ROWS - 1
NA_CI = 2 * NA_COLS - 1
NA_T2 = NA_RI + 1


def rpb_expand(rpb_flat, n_heads, dep, name):
    def body(rpb_ref, dep_ref, o_ref):
        diff, low, _ = _na_iotas()
        for h in range(n_heads):
            def one(e, carry, h=h):
                lo_row = jnp.maximum(e - 1, 0)
                hi_row = jnp.minimum(e, NA_RI - 1)
                lo_on = jnp.where(e >= 1, 1.0, 0.0)
                hi_on = jnp.where(e <= NA_RI - 1, 1.0, 0.0)
                t = jnp.zeros((GRID_W, LANES), F32)
                for c in range(NA_CI):
                    lo = rpb_ref[h * NA_RI * NA_CI + lo_row * NA_CI + c] * lo_on
                    hi = rpb_ref[h * NA_RI * NA_CI + hi_row * NA_CI + c] * hi_on
                    t = jnp.where(diff == c, jnp.where(low, lo, hi), t)
                o_ref[h, e] = t
                return carry
            lax.fori_loop(0, NA_T2, one, 0)

    return pl.pallas_call(
        body, name=name,
        in_specs=[pl.BlockSpec(memory_space=pltpu.SMEM), pl.BlockSpec(memory_space=pltpu.VMEM)],
        out_specs=pl.BlockSpec(memory_space=pltpu.VMEM),
        out_shape=jax.ShapeDtypeStruct((n_heads, NA_T2, GRID_W, LANES), F32),
        compiler_params=pltpu.CompilerParams(vmem_limit_bytes=V7X_VMEM_LIMIT),
    )(rpb_flat, dep)


def rpb_reduce(dt2, name):
    n_heads = dt2.shape[0]

    def body(d_ref, o_ref):
        diff, low, _ = _na_iotas()
        low32 = lax.broadcasted_iota(jnp.int32, (32, LANES), 1) < GRID_W
        o_ref[...] = jnp.zeros(o_ref.shape, F32)
        for h in range(n_heads):
            def one(e, carry, h=h):
                dv = d_ref[h, e]
                rows = [jnp.sum(jnp.where(diff == c, dv, 0.0), axis=0, keepdims=True) for c in range(NA_CI)]
                rows.append(jnp.zeros((1, LANES), F32))
                r = jnp.concatenate(rows, axis=0)
                lo = jnp.sum(jnp.where(low32, r, 0.0), axis=1, keepdims=True)
                hi = jnp.sum(jnp.where(low32, 0.0, r), axis=1, keepdims=True)
                lo_row = jnp.maximum(e - 1, 0)
                hi_row = jnp.minimum(e, NA_RI - 1)
                o_ref[h, lo_row] = o_ref[h, lo_row] + jnp.broadcast_to(lo, (32, LANES))
                o_ref[h, hi_row] = o_ref[h, hi_row] + jnp.broadcast_to(hi, (32, LANES))
                return carry
            lax.fori_loop(0, NA_T2, one, 0)

    return pl.pallas_call(
        body, name=name,
        in_specs=[pl.BlockSpec(memory_space=pltpu.VMEM)],
        out_specs=pl.BlockSpec(memory_space=pltpu.VMEM),
        out_shape=jax.ShapeDtypeStruct((n_heads, NA_RI, 32, LANES), F32),
        compiler_params=pltpu.CompilerParams(vmem_limit_bytes=V7X_VMEM_LIMIT),
    )(dt2)


NA_TQ = 4
NA_TK = NA_TQ + NA_ROWS
NA_KCH = NA_TK // 2


def _na_tile_geometry(t, rows):
    r = t * NA_TQ
    kbase = jnp.clip(r - NA_ROWS // 2, 0, rows - NA_TK)
    starts = [jnp.clip(r + a - NA_ROWS // 2, 0, rows - NA_ROWS) for a in range(NA_TQ)]
    return r, kbase, starts


def _na_tile_mask(kbase, starts, low, inwin):
    half = jnp.where(low, 0, 1)
    cols = []
    for c in range(NA_KCH):
        krow = kbase + 2 * c + half
        cols.append(jnp.concatenate(
            [jnp.where(inwin & (krow >= st) & (krow < st + NA_ROWS), 0.0, NEG) for st in starts], axis=0))
    return jnp.concatenate(cols, axis=1)


def _na_tile_index(r, kbase, a, c):
    return jnp.clip(kbase + 2 * c - (r + a) + NA_ROWS, 0, NA_T2 - 1)


def _na_tile_probs(q, k, t2_ref, hh, r, kbase, madd):
    bias = jnp.concatenate(
        [jnp.concatenate([t2_ref[hh, _na_tile_index(r, kbase, a, c)] for a in range(NA_TQ)], axis=0)
         for c in range(NA_KCH)], axis=1)
    sc = _dotg(q, k, NT) + bias + madd
    e = jnp.exp(sc - jnp.max(sc, axis=1, keepdims=True))
    return e * (1.0 / jnp.sum(e, axis=1, keepdims=True))


def na_fwd(qa, ka, zq, t2, name):
    s = qa.shape[0]
    rows = s // GRID_W
    n_pairs = NA_WIDTH // LANES
    v_blk0 = (2 * NA_WIDTH) // LANES

    assert rows % NA_TQ == 0 and rows >= NA_TK
    tq, tk = NA_TQ * GRID_W, NA_TK * GRID_W

    def body(q_ref, k_ref, v_ref, t2_ref, o_ref):
        _, low, inwin = _na_iotas()

        def tile(t, carry):
            r, kbase, starts = _na_tile_geometry(t, rows)
            madd = _na_tile_mask(kbase, starts, low, inwin)
            qr = pl.ds(pl.multiple_of(r * GRID_W, tq), tq)
            kr = pl.ds(pl.multiple_of(kbase * GRID_W, tq), tk)
            for hh in range(2):
                lanes = slice(HEAD_DIM * hh, HEAD_DIM * (hh + 1))
                p = _na_tile_probs(q_ref[qr, lanes], k_ref[kr, lanes], t2_ref, hh, r, kbase, madd)
                o_ref[qr, lanes] = _dot(p.astype(BF16), v_ref[kr, lanes]).astype(BF16)
            return carry

        lax.fori_loop(0, rows // NA_TQ, tile, 0)

    col = lambda off: pl.BlockSpec((s, LANES), lambda p, _o=off: (0, _o + p))
    return pl.pallas_call(
        body, name=name, grid=(n_pairs,),
        in_specs=[col(0), col(0), col(v_blk0),
                  pl.BlockSpec((2, NA_T2, GRID_W, LANES), lambda p: (p, 0, 0, 0))],
        out_specs=col(0),
        out_shape=jax.ShapeDtypeStruct((s, NA_WIDTH), BF16),
        compiler_params=_params(),
    )(qa, ka, zq, t2)


def na_bwd(qa, ka, zq, t2, o_na, do_na, name):
    s = qa.shape[0]
    rows = s // GRID_W
    n_pairs = NA_WIDTH // LANES
    v_blk0 = (2 * NA_WIDTH) // LANES

    tq, tk = NA_TQ * GRID_W, NA_TK * GRID_W

    def body(q_ref, k_ref, v_ref, t2_ref, o_ref, do_ref, dq_ref, dk_ref, dv_ref, dt2_ref):
        _, low, inwin = _na_iotas()
        dk_ref[...] = jnp.zeros(dk_ref.shape, F32)
        dv_ref[...] = jnp.zeros(dv_ref.shape, F32)
        dt2_ref[...] = jnp.zeros(dt2_ref.shape, F32)

        def tile(t, carry):
            r, kbase, starts = _na_tile_geometry(t, rows)
            madd = _na_tile_mask(kbase, starts, low, inwin)
            qr = pl.ds(pl.multiple_of(r * GRID_W, tq), tq)
            kr = pl.ds(pl.multiple_of(kbase * GRID_W, tq), tk)
            for hh in range(2):
                lanes = slice(HEAD_DIM * hh, HEAD_DIM * (hh + 1))
                q, k, v = q_ref[qr, lanes], k_ref[kr, lanes], v_ref[kr, lanes]
                p = _na_tile_probs(q, k, t2_ref, hh, r, kbase, madd)
                do = do_ref[qr, lanes]
                delta = jnp.sum(do.astype(F32) * o_ref[qr, lanes].astype(F32), axis=1, keepdims=True)
                ds = p * (_dotg(do, v, NT) - delta)
                for a in range(NA_TQ):
                    for c in range(NA_KCH):
                        e = _na_tile_index(r, kbase, a, c)
                        dt2_ref[hh, e] = dt2_ref[hh, e] + ds[GRID_W * a:GRID_W * (a + 1), LANES * c:LANES * (c + 1)]
                dsb = ds.astype(BF16)
                dq_ref[qr, lanes] = _dot(dsb, k)
                dk_ref[kr, lanes] = dk_ref[kr, lanes] + _dotg(dsb, q, TN)
                dv_ref[kr, lanes] = dv_ref[kr, lanes] + _dotg(p.astype(BF16), do, TN)
            return carry

        lax.fori_loop(0, rows // NA_TQ, tile, 0)

    col = lambda off: pl.BlockSpec((s, LANES), lambda p, _o=off: (0, _o + p))
    t2spec = pl.BlockSpec((2, NA_T2, GRID_W, LANES), lambda p: (p, 0, 0, 0))
    return pl.pallas_call(
        body, name=name, grid=(n_pairs,),
        in_specs=[col(0), col(0), col(v_blk0), t2spec, col(0), col(0)],
        out_specs=[col(0), col(0), col(0), t2spec],
        out_shape=[jax.ShapeDtypeStruct((s, NA_WIDTH), F32)] * 3 + [jax.ShapeDtypeStruct(t2.shape, F32)],
        compiler_params=_params(),
    )(qa, ka, zq, t2, o_na, do_na)


def _t5_bucket_map():
    rel = np.arange(3 * SW_BLOCK)[None, :] - SW_BLOCK - np.arange(SW_BLOCK)[:, None]
    nb = REL_BUCKETS // 2
    max_exact = nb // 2
    n = np.abs(rel)
    large = max_exact + (np.log(np.maximum(n, 1) / max_exact)
                         / np.log(REL_MAX_DIST / max_exact) * (nb - max_exact)).astype(np.int32)
    large = np.minimum(large, nb - 1)
    return ((rel > 0) * nb + np.where(n < max_exact, n, large)).astype(np.int32)


def t5_expand(table, bmap, name):
    def body(tab_ref, bm_ref, o_ref):
        bm = bm_ref[...]
        for h in range(SW_HEADS):
            t = jnp.zeros(bm.shape, F32)
            for b in range(REL_BUCKETS):
                t = jnp.where(bm == b, tab_ref[b, h], t)
            o_ref[h] = t

    return pl.pallas_call(
        body, name=name,
        in_specs=[pl.BlockSpec(memory_space=pltpu.SMEM), pl.BlockSpec(memory_space=pltpu.VMEM)],
        out_specs=pl.BlockSpec(memory_space=pltpu.VMEM),
        out_shape=jax.ShapeDtypeStruct((SW_HEADS,) + bmap.shape, F32),
        compiler_params=pltpu.CompilerParams(vmem_limit_bytes=V7X_VMEM_LIMIT),
    )(table, bmap)


def t5_reduce(dbias_list, bmap, name):
    n = len(dbias_list)

    def body(*refs):
        d_refs, bm_ref, o_ref = refs[:n], refs[n], refs[n + 1]
        bm = bm_ref[...]
        for h in range(SW_HEADS):
            dv = d_refs[0][h]
            for other in d_refs[1:]:
                dv = dv + other[h]
            rows = [jnp.sum(jnp.where(bm == b, dv, 0.0), axis=0, keepdims=True) for b in range(REL_BUCKETS)]
            r = jnp.concatenate(rows, axis=0)
            o_ref[h] = jnp.broadcast_to(jnp.sum(r, axis=1, keepdims=True), (REL_BUCKETS, LANES))

    return pl.pallas_call(
        body, name=name,
        in_specs=[pl.BlockSpec(memory_space=pltpu.VMEM)] * (n + 1),
        out_specs=pl.BlockSpec(memory_space=pltpu.VMEM),
        out_shape=jax.ShapeDtypeStruct((SW_HEADS, REL_BUCKETS, LANES), F32),
        compiler_params=pltpu.CompilerParams(vmem_limit_bytes=V7X_VMEM_LIMIT),
    )(*dbias_list, bmap)


def _sw_mask_iotas():
    a = lax.broadcasted_iota(jnp.int32, (SW_BLOCK, 3 * SW_BLOCK), 0)
    j = lax.broadcasted_iota(jnp.int32, (SW_BLOCK, 3 * SW_BLOCK), 1)
    inwin = jnp.abs(j - SW_BLOCK - a) <= SW_BLOCK
    return j, inwin


SW_STACK = SW_HEADS * SW_BLOCK


def _sw_softmax(sc, sk):
    m = jnp.maximum(jnp.max(sc, axis=1, keepdims=True), sk)
    e = jnp.exp(sc - m)
    es = jnp.exp(sk - m)
    inv = 1.0 / (jnp.sum(e, axis=1, keepdims=True) + es)
    return e * inv, es * inv


def _sw_prologue(k_ref, v_ref, kp, vp, sink_ref, s):
    pad = s + 2 * SW_BLOCK
    zeros = jnp.zeros((SW_BLOCK, SW_KV_WIDTH), BF16)
    kp[0:SW_BLOCK, :] = zeros
    vp[0:SW_BLOCK, :] = zeros
    kp[SW_BLOCK + s:pad, :] = zeros
    vp[SW_BLOCK + s:pad, :] = zeros
    kp[SW_BLOCK:SW_BLOCK + s, :] = k_ref[...]
    vp[SW_BLOCK:SW_BLOCK + s, :] = v_ref[...]
    return jnp.concatenate([jnp.full((SW_BLOCK, 1), sink_ref[h], F32) for h in range(SW_HEADS)], axis=0)


def sw_fwd(qs, ks, zq, t5b, sink, dep, name):
    s = qs.shape[0]
    nb = s // SW_BLOCK
    v_blk = (3 * NA_WIDTH + SW_Q_WIDTH + SW_KV_WIDTH) // LANES
    pad = s + 2 * SW_BLOCK

    def body(q_ref, k_ref, v_ref, b_ref, sink_ref, dep_ref, o_ref, kp, vp, s_scr, p_scr):
        sink_col = _sw_prologue(k_ref, v_ref, kp, vp, sink_ref, s)
        j, inwin = _sw_mask_iotas()

        def blk(n, carry):
            kpos = n * SW_BLOCK - SW_BLOCK + j
            madd = jnp.where(inwin & (kpos >= 0) & (kpos < s), 0.0, NEG)
            q0 = pl.multiple_of(n * SW_BLOCK, SW_BLOCK)
            qr, kr = pl.ds(q0, SW_BLOCK), pl.ds(q0, 3 * SW_BLOCK)
            for h in range(SW_HEADS):
                g = h // SW_REP
                s_scr[SW_BLOCK * h:SW_BLOCK * (h + 1), :] = _dotg(
                    q_ref[qr, HEAD_DIM * h:HEAD_DIM * (h + 1)], kp[kr, HEAD_DIM * g:HEAD_DIM * (g + 1)], NT) + madd
            p, _ = _sw_softmax(s_scr[...] + b_ref[...], sink_col)
            p_scr[...] = p.astype(BF16)
            for h in range(SW_HEADS):
                g = h // SW_REP
                o_ref[qr, HEAD_DIM * h:HEAD_DIM * (h + 1)] = _dot(
                    p_scr[SW_BLOCK * h:SW_BLOCK * (h + 1), :], vp[kr, HEAD_DIM * g:HEAD_DIM * (g + 1)]).astype(BF16)
            return carry

        lax.fori_loop(0, nb, blk, 0)

    return pl.pallas_call(
        body, name=name, grid=(1,),
        in_specs=[_full((s, SW_Q_WIDTH)), _full((s, SW_KV_WIDTH)),
                  pl.BlockSpec((s, SW_KV_WIDTH), lambda i: (0, v_blk)),
                  _full((SW_STACK, 3 * SW_BLOCK)), pl.BlockSpec(memory_space=pltpu.SMEM),
                  _full(dep.shape)],
        out_specs=_full((s, SW_Q_WIDTH)),
        out_shape=jax.ShapeDtypeStruct((s, SW_Q_WIDTH), BF16),
        scratch_shapes=[pltpu.VMEM((pad, SW_KV_WIDTH), BF16), pltpu.VMEM((pad, SW_KV_WIDTH), BF16),
                        pltpu.VMEM((SW_STACK, 3 * SW_BLOCK), F32), pltpu.VMEM((SW_STACK, 3 * SW_BLOCK), BF16)],
        compiler_params=_params(),
    )(qs, ks, zq, t5b, sink, dep)


def sw_bwd(qs, ks, zq, t5b, sink, o_sw, do_sw, name):
    s = qs.shape[0]
    nb = s // SW_BLOCK
    v_blk = (3 * NA_WIDTH + SW_Q_WIDTH + SW_KV_WIDTH) // LANES
    pad = s + 2 * SW_BLOCK

    def body(q_ref, k_ref, v_ref, b_ref, sink_ref, o_ref, do_ref,
             dq_ref, dk_ref, dv_ref, db_ref, dsk_ref, kp, vp, dkp, dvp, s_scr, dp_scr, ds_scr, p_scr):
        sink_col = _sw_prologue(k_ref, v_ref, kp, vp, sink_ref, s)
        dkp[...] = jnp.zeros(dkp.shape, F32)
        dvp[...] = jnp.zeros(dvp.shape, F32)
        db_ref[...] = jnp.zeros(db_ref.shape, F32)
        dsk_ref[...] = jnp.zeros(dsk_ref.shape, F32)
        j, inwin = _sw_mask_iotas()

        def blk(n, carry):
            kpos = n * SW_BLOCK - SW_BLOCK + j
            madd = jnp.where(inwin & (kpos >= 0) & (kpos < s), 0.0, NEG)
            q0 = pl.multiple_of(n * SW_BLOCK, SW_BLOCK)
            qr, kr = pl.ds(q0, SW_BLOCK), pl.ds(q0, 3 * SW_BLOCK)
            deltas = []
            for h in range(SW_HEADS):
                g = h // SW_REP
                hl, kl = slice(HEAD_DIM * h, HEAD_DIM * (h + 1)), slice(HEAD_DIM * g, HEAD_DIM * (g + 1))
                rows = slice(SW_BLOCK * h, SW_BLOCK * (h + 1))
                do = do_ref[qr, hl]
                s_scr[rows, :] = _dotg(q_ref[qr, hl], kp[kr, kl], NT) + madd
                dp_scr[rows, :] = _dotg(do, vp[kr, kl], NT)
                deltas.append(jnp.sum(do.astype(F32) * o_ref[qr, hl].astype(F32), axis=1, keepdims=True))
            delta = jnp.concatenate(deltas, axis=0)
            p, ps = _sw_softmax(s_scr[...] + b_ref[...], sink_col)
            ds = p * (dp_scr[...] - delta)
            db_ref[...] = db_ref[...] + ds
            dsk_ref[...] = dsk_ref[...] - jnp.broadcast_to(ps * delta, (SW_STACK, LANES))
            ds_scr[...] = ds.astype(BF16)
            p_scr[...] = p.astype(BF16)
            for g in range(SW_HEADS // SW_REP):
                kl = slice(HEAD_DIM * g, HEAD_DIM * (g + 1))
                k = kp[kr, kl]
                dkw = jnp.zeros((3 * SW_BLOCK, HEAD_DIM), F32)
                dvw = jnp.zeros((3 * SW_BLOCK, HEAD_DIM), F32)
                for r in range(SW_REP):
                    h = g * SW_REP + r
                    hl, rows = slice(HEAD_DIM * h, HEAD_DIM * (h + 1)), slice(SW_BLOCK * h, SW_BLOCK * (h + 1))
                    dsb = ds_scr[rows, :]
                    dq_ref[qr, hl] = _dot(dsb, k)
                    dkw = dkw + _dotg(dsb, q_ref[qr, hl], TN)
                    dvw = dvw + _dotg(p_scr[rows, :], do_ref[qr, hl], TN)
                dkp[kr, kl] = dkp[kr, kl] + dkw
                dvp[kr, kl] = dvp[kr, kl] + dvw
            return carry

        lax.fori_loop(0, nb, blk, 0)
        dk_ref[...] = dkp[SW_BLOCK:SW_BLOCK + s, :]
        dv_ref[...] = dvp[SW_BLOCK:SW_BLOCK + s, :]

    bias_spec = _full((SW_STACK, 3 * SW_BLOCK))
    return pl.pallas_call(
        body, name=name, grid=(1,),
        in_specs=[_full((s, SW_Q_WIDTH)), _full((s, SW_KV_WIDTH)),
                  pl.BlockSpec((s, SW_KV_WIDTH), lambda i: (0, v_blk)),
                  bias_spec, pl.BlockSpec(memory_space=pltpu.SMEM),
                  _full((s, SW_Q_WIDTH)), _full((s, SW_Q_WIDTH))],
        out_specs=[_full((s, SW_Q_WIDTH)), _full((s, SW_KV_WIDTH)), _full((s, SW_KV_WIDTH)), bias_spec,
                   _full((SW_STACK, LANES))],
        out_shape=[jax.ShapeDtypeStruct((s, SW_Q_WIDTH), F32), jax.ShapeDtypeStruct((s, SW_KV_WIDTH), F32),
                   jax.ShapeDtypeStruct((s, SW_KV_WIDTH), F32),
                   jax.ShapeDtypeStruct((SW_STACK, 3 * SW_BLOCK), F32),
                   jax.ShapeDtypeStruct((SW_STACK, LANES), F32)],
        scratch_shapes=[pltpu.VMEM((pad, SW_KV_WIDTH), BF16), pltpu.VMEM((pad, SW_KV_WIDTH), BF16),
                        pltpu.VMEM((pad, SW_KV_WIDTH), F32), pltpu.VMEM((pad, SW_KV_WIDTH), F32),
                        pltpu.VMEM((SW_STACK, 3 * SW_BLOCK), F32), pltpu.VMEM((SW_STACK, 3 * SW_BLOCK), F32),
                        pltpu.VMEM((SW_STACK, 3 * SW_BLOCK), BF16), pltpu.VMEM((SW_STACK, 3 * SW_BLOCK), BF16)],
        compiler_params=_params(),
    )(qs, ks, zq, t5b, sink, o_sw, do_sw)


def merge_out(x, o_na, o_sw, gt, wbna_t, wbsw_t, wout, name):
    s, d = x.shape
    tm = _row_tile(s)

    def body(x_ref, ona_ref, osw_ref, gt_ref, wna_ref, wsw_ref, wo_ref, xo_ref, ana_ref, asw_ref, mg_ref):
        a_na = _dotg(ona_ref[...], wna_ref[...], NT)
        a_sw = _dotg(osw_ref[...], wsw_ref[...], NT)
        ana_ref[...] = a_na.astype(BF16)
        asw_ref[...] = a_sw.astype(BF16)
        merged = (gt_ref[:, 0:d].astype(F32) * a_na + gt_ref[:, d:2 * d].astype(F32) * a_sw).astype(BF16)
        mg_ref[...] = merged
        xo_ref[...] = x_ref[...] + _dot(merged, wo_ref[...])

    return pl.pallas_call(
        body, name=name, grid=(s // tm,),
        in_specs=[_rows(tm, d), _rows(tm, 512), _rows(tm, 512), _rows(tm, 2 * d),
                  _mat(*wbna_t), _mat(*wbsw_t), _mat(*wout)],
        out_specs=[_rows(tm, d)] * 4,
        out_shape=[jax.ShapeDtypeStruct((s, d), F32)] + [jax.ShapeDtypeStruct((s, d), BF16)] * 3,
        compiler_params=_params(),
    )(x, o_na, o_sw, gt, wbna_t[0], wbsw_t[0], wout[0])


def mix_bwd_out(dx, gt, a_na, a_sw, wbna_t, wbsw_t, wout, name):
    s, d = dx.shape
    tm = _row_tile(s)

    def body(dx_ref, gt_ref, ana_ref, asw_ref, wna_ref, wsw_ref, wo_ref,
             dxb_ref, dzg_ref, dana_ref, dasw_ref, dona_ref, dosw_ref, dbg_ref):
        @pl.when(pl.program_id(0) == 0)
        def _():
            dbg_ref[...] = jnp.zeros(dbg_ref.shape, F32)

        dxb = dx_ref[...].astype(BF16)
        dxb_ref[...] = dxb
        dm = _dotg(dxb, wo_ref[...], NT)
        for i, (a_ref, da_ref, w_ref, do_ref) in enumerate(
                [(ana_ref, dana_ref, wna_ref, dona_ref), (asw_ref, dasw_ref, wsw_ref, dosw_ref)]):
            gi = gt_ref[:, i * d:(i + 1) * d].astype(F32)
            da = (dm * gi).astype(BF16)
            da_ref[...] = da
            do_ref[...] = _dot(da, w_ref[...]).astype(BF16)
            dzg = dm * a_ref[...].astype(F32) * gi * (1.0 - gi)
            dzg_ref[:, i * d:(i + 1) * d] = dzg.astype(BF16)
            dbg_ref[:, i * d:(i + 1) * d] = dbg_ref[:, i * d:(i + 1) * d] + jnp.sum(dzg, axis=0, keepdims=True)

    return pl.pallas_call(
        body, name=name, grid=(s // tm,),
        in_specs=[_rows(tm, d), _rows(tm, 2 * d), _rows(tm, d), _rows(tm, d),
                  _mat(*wbna_t), _mat(*wbsw_t), _mat(*wout)],
        out_specs=[_rows(tm, d), _rows(tm, 2 * d), _rows(tm, d), _rows(tm, d), _rows(tm, 512), _rows(tm, 512),
                   _full((1, 2 * d))],
        out_shape=[jax.ShapeDtypeStruct((s, d), BF16), jax.ShapeDtypeStruct((s, 2 * d), BF16),
                   jax.ShapeDtypeStruct((s, d), BF16), jax.ShapeDtypeStruct((s, d), BF16),
                   jax.ShapeDtypeStruct((s, 512), BF16), jax.ShapeDtypeStruct((s, 512), BF16),
                   jax.ShapeDtypeStruct((1, 2 * d), F32)],
        compiler_params=_params(),
    )(dx, gt, a_na, a_sw, wbna_t[0], wbsw_t[0], wout[0])


def qk_norm_bwd(dqa, dka, dva, dqs, dks, dvs, zq, dzg, gq_na, gk_na, gq_sw, gk_sw, bd, name):
    s = zq.shape[0]
    d2 = dzg.shape[1]
    n_in = QKV_WIDTH + d2
    tm = _row_tile(s)

    def body(dqa_ref, dka_ref, dva_ref, dqs_ref, dks_ref, dvs_ref, zq_ref, dzg_ref,
             gqa_ref, gka_ref, gqs_ref, gks_ref, bd_ref, dz_ref, dgqa_ref, dgka_ref, dgqs_ref, dgks_ref):
        @pl.when(pl.program_id(0) == 0)
        def _():
            for r in (dgqa_ref, dgka_ref, dgqs_ref, dgks_ref):
                r[...] = jnp.zeros(r.shape, F32)

        bd512 = bd_ref[...]
        bd128 = bd_ref[0:SW_KV_WIDTH, 0:SW_KV_WIDTH]

        def one(c0, c1, dy_ref, g_ref, dg_ref, bdm, scale):
            z = zq_ref[:, c0:c1].astype(F32)
            r = lax.rsqrt(_group_mean(z * z, bdm) + EPS)
            zh = z * r
            dy = dy_ref[...] * scale
            dyg = dy * g_ref[...]
            dz = r * (dyg - zh * _group_mean(dyg * zh, bdm))
            dz_ref[:, c0:c1] = dz.astype(BF16)
            dg_ref[...] = dg_ref[...] + jnp.sum(dy * zh, axis=0, keepdims=True)

        one(0, 512, dqa_ref, gqa_ref, dgqa_ref, bd512, SCALE)
        one(512, 1024, dka_ref, gka_ref, dgka_ref, bd512, 1.0)
        dz_ref[:, 1024:1536] = dva_ref[...].astype(BF16)
        one(1536, 2048, dqs_ref, gqs_ref, dgqs_ref, bd512, SCALE)
        one(2048, 2176, dks_ref, gks_ref, dgks_ref, bd128, 1.0)
        dz_ref[:, 2176:2304] = dvs_ref[...].astype(BF16)
        dz_ref[:, QKV_WIDTH:n_in] = dzg_ref[...]

    return pl.pallas_call(
        body, name=name, grid=(s // tm,),
        in_specs=[_rows(tm, 512), _rows(tm, 512), _rows(tm, 512), _rows(tm, 512), _rows(tm, 128), _rows(tm, 128),
                  _rows(tm, QKV_WIDTH), _rows(tm, d2),
                  _full((1, 512)), _full((1, 512)), _full((1, 512)), _full((1, 128)), _full((512, 512))],
        out_specs=[_rows(tm, n_in), _full((1, 512)), _full((1, 512)), _full((1, 512)), _full((1, 128))],
        out_shape=[jax.ShapeDtypeStruct((s, n_in), BF16)] + [jax.ShapeDtypeStruct((1, 512), F32)] * 3
                  + [jax.ShapeDtypeStruct((1, 128), F32)],
        compiler_params=_params(),
    )(dqa, dka, dva, dqs, dks, dvs, zq, dzg, gq_na, gk_na, gq_sw, gk_sw, bd)


def ffn_bwd_act(dx, wd, hg, hu, name):
    s, d = dx.shape
    f = wd[0].shape[1]
    tm = _row_tile(s)
    fc = _col_chunk(f)

    def body(dx_ref, w_ref, hg_ref, hu_ref, dxb_ref, dhg_ref, dhu_ref):
        dxb = dx_ref[...].astype(BF16)
        dxb_ref[...] = dxb
        for c0 in range(0, f, fc):
            dact = 0.5 * _dotg(dxb, w_ref[c0:c0 + fc, :], NT)
            hg = hg_ref[:, c0:c0 + fc].astype(F32)
            hu = hu_ref[:, c0:c0 + fc].astype(F32)
            sg = _sigmoid(hg)
            dhu_ref[:, c0:c0 + fc] = (dact * hg * sg).astype(BF16)
            dhg_ref[:, c0:c0 + fc] = (dact * hu * sg * (1.0 + hg * (1.0 - sg))).astype(BF16)

    return pl.pallas_call(
        body, name=name, grid=(s // tm,),
        in_specs=[_rows(tm, d), _mat(*wd), _rows(tm, f), _rows(tm, f)],
        out_specs=[_rows(tm, d), _rows(tm, f), _rows(tm, f)],
        out_shape=[jax.ShapeDtypeStruct((s, d), BF16), jax.ShapeDtypeStruct((s, f), BF16),
                   jax.ShapeDtypeStruct((s, f), BF16)],
        compiler_params=_params(),
    )(dx, wd[0], hg, hu)


def proj_bwd_norm(acts, weights, x, gain, dx, dep, name):
    s, d = x.shape
    tm = _row_tile(s)
    n = len(acts)

    def body(*refs):
        a_refs, w_refs = refs[:n], refs[n:2 * n]
        x_ref, g_ref, dx_ref, _, o_ref, dg_ref = refs[2 * n:]

        @pl.when(pl.program_id(0) == 0)
        def _():
            dg_ref[...] = jnp.zeros(dg_ref.shape, F32)

        dxn = _dot(a_refs[0][...], w_refs[0][...])
        for a_ref, w_ref in zip(a_refs[1:], w_refs[1:]):
            dxn = dxn + _dot(a_ref[...], w_ref[...])
        xv = x_ref[...]
        r = _rstd(xv)
        xh = xv * r
        dxh = dxn * g_ref[...]
        o_ref[...] = dx_ref[...] + r * (dxh - xh * jnp.mean(dxh * xh, axis=-1, keepdims=True))
        dg_ref[...] = dg_ref[...] + jnp.sum(dxn * xh, axis=0, keepdims=True)

    return pl.pallas_call(
        body, name=name, grid=(s // tm,),
        in_specs=[_rows(tm, a.shape[1]) for a in acts] + [_mat(*w) for w in weights]
                 + [_rows(tm, d), _full((1, d)), _rows(tm, d), _full(dep.shape)],
        out_specs=[_rows(tm, d), _full((1, d))],
        out_shape=[jax.ShapeDtypeStruct((s, d), F32), jax.ShapeDtypeStruct((1, d), F32)],
        compiler_params=_params(),
    )(*acts, *[w[0] for w in weights], x, gain, dx, dep)


def tn_matmul(a, b, scale, name):
    s, n = a.shape
    k = b.shape[1]
    tn = _tn_tile(n)

    def body(a_ref, b_ref, o_ref):
        o_ref[...] = (scale * _dotg(a_ref[...], b_ref[...], TN)).astype(BF16)

    return pl.pallas_call(
        body, name=name, grid=(n // tn,),
        in_specs=[pl.BlockSpec((s, tn), lambda i: (0, i)),
                  pl.BlockSpec((s, k), lambda i: (0, 0), pipeline_mode=ONCE)],
        out_specs=pl.BlockSpec((tn, k), lambda i: (i, 0)),
        out_shape=jax.ShapeDtypeStruct((n, k), BF16),
        compiler_params=_params(),
    )(a, b)


def loss_grad(y, target, name):
    s, d = y.shape
    tm = _row_tile(s)

    def body(y_ref, t_ref, dy_ref, acc_ref):
        @pl.when(pl.program_id(0) == 0)
        def _():
            acc_ref[...] = jnp.zeros(acc_ref.shape, F32)

        err = y_ref[...] - t_ref[...]
        dy_ref[...] = err * (1.0 / d)
        e2 = err * err
        part = jnp.sum(e2.reshape(tm // 8, 8, d), axis=0)
        acc = part[:, 0:LANES]
        for c0 in range(LANES, d, LANES):
            acc = acc + part[:, c0:c0 + LANES]
        acc_ref[...] = acc_ref[...] + acc

    return pl.pallas_call(
        body, name=name, grid=(s // tm,),
        in_specs=[_rows(tm, d), _rows(tm, d)],
        out_specs=[_rows(tm, d), _full((8, LANES))],
        out_shape=[jax.ShapeDtypeStruct((s, d), F32), jax.ShapeDtypeStruct((8, LANES), F32)],
        compiler_params=_params(),
    )(y, target)


def _mesh_pos():
    return lax.axis_index("x"), lax.axis_index("y"), lax.axis_index("c")


def gather_weights(shards):
    n = len(shards)

    def body(*refs):
        ins, outs = refs[:n], refs[n:2 * n]
        send_sems, recv_sems, local_sems = refs[2 * n:]
        x, y, c = _mesh_pos()
        me, sibling = (x, y, c), (x, y, 1 - c)
        chips = [(1 - x, y), (x, 1 - y), (1 - x, 1 - y)]

        def slot(a, px, py, pc):
            return outs[a].at[:, 4 * px + 2 * py + pc]

        def copy(a, k, block, to, src=None):
            return pltpu.make_async_remote_copy(
                src_ref=slot(a, *block) if src is None else src, dst_ref=slot(a, *block),
                send_sem=send_sems.at[a, k], recv_sem=recv_sems.at[a, k], device_id=to, device_id_type=MESH)

        mine = [pltpu.make_async_copy(ins[a], slot(a, *me), local_sems.at[a]) for a in range(n)]
        for cp in mine:
            cp.start()
        first = []
        for a in range(n):
            first.append(copy(a, 0, me, sibling, src=ins[a]))
            first += [copy(a, 1 + j, me, (*chip, c), src=ins[a]) for j, chip in enumerate(chips)]
        for cp in first:
            cp.start()
        passed = []
        for j, chip in enumerate(chips):
            for a in range(n):
                copy(a, 1 + j, (*chip, c), me).wait_recv()
                fwd = copy(a, 4 + j, (*chip, c), sibling)
                fwd.start()
                passed.append(fwd)
        for a in range(n):
            copy(a, 0, sibling, me).wait_recv()
            for j, chip in enumerate(chips):
                copy(a, 4 + j, (*chip, 1 - c), me).wait_recv()
        for cp in first + passed:
            cp.wait_send()
        for cp in mine:
            cp.wait()

    any_spec = pl.BlockSpec(memory_space=pl.ANY)
    return pl.pallas_call(
        body, name="gather_weights",
        in_specs=[any_spec] * n, out_specs=[any_spec] * n,
        out_shape=[jax.ShapeDtypeStruct((w.shape[0], N_DEV) + w.shape[1:], w.dtype) for w in shards],
        scratch_shapes=[pltpu.SemaphoreType.DMA((n, 7)), pltpu.SemaphoreType.DMA((n, 7)),
                        pltpu.SemaphoreType.DMA((n,))],
        compiler_params=pltpu.CompilerParams(has_side_effects=True),
    )(*shards)


def _peers():
    x, y, c = _mesh_pos()
    peers = []
    for rel in range(1, N_DEV):
        peers.append((1 - x if rel & 4 else x, 1 - y if rel & 2 else y, 1 - c if rel & 1 else c))
    return 4 * x + 2 * y + c, peers


HBM_SPEC = pl.BlockSpec(memory_space=pltpu.HBM)
SEM_SPEC = pl.BlockSpec(memory_space=pltpu.SEMAPHORE)


def _split_call(body, name, thru, n_sems, extra=(), with_token=True):
    hbm = lambda t: pltpu.with_memory_space_constraint(t, pltpu.HBM)
    effect = pltpu.CompilerParams(has_side_effects=pltpu.SideEffectType.DATAFLOW_SIDE_EFFECTING)
    nt = len(thru)
    thru_shapes = [pltpu.HBM(t.shape, t.dtype) for t in thru]
    if with_token:
        (after,) = extra
        outs = pl.pallas_call(
            body, name=name, in_specs=[HBM_SPEC] * nt + [pl.BlockSpec(memory_space=pl.ANY)],
            out_specs=[SEM_SPEC] * len(n_sems) + [HBM_SPEC] * nt + [pl.BlockSpec(memory_space=pltpu.VMEM)],
            out_shape=[pltpu.SemaphoreType.DMA((k,)) for k in n_sems] + thru_shapes
                      + [jax.ShapeDtypeStruct((8, LANES), F32)],
            input_output_aliases={i: len(n_sems) + i for i in range(nt)}, compiler_params=effect,
        )(*[hbm(t) for t in thru], after)
        return outs[:len(n_sems)], outs[len(n_sems):-1], outs[-1]
    return pl.pallas_call(
        body, name=name,
        in_specs=[HBM_SPEC] * nt + [SEM_SPEC] * len(n_sems) + [pl.BlockSpec(memory_space=pl.ANY)],
        out_specs=[HBM_SPEC] * nt, out_shape=thru_shapes,
        input_output_aliases={i: i for i in range(nt)}, compiler_params=effect,
    )(*thru, *extra)


def _gather_targets():
    x, y, c = _mesh_pos()
    return 4 * x + 2 * y + c, [(x, y, 1 - c), (1 - x, y, c), (x, 1 - y, c), (1 - x, 1 - y, c)]


def gather_start(shards, after, name):
    n = len(shards)
    zones = [lax.empty((w.shape[0], N_DEV) + w.shape[1:], w.dtype) for w in shards]

    def body(*refs):
        ins, zs = refs[:n], refs[n:2 * n]
        send_sems, recv_sems, local_sems = refs[2 * n + 1:2 * n + 4]
        token = refs[-1]
        me, targets = _gather_targets()
        for a in range(n):
            pltpu.make_async_copy(ins[a], zs[a].at[:, me], local_sems.at[a]).start()
            for k, to in enumerate(targets):
                pltpu.make_async_remote_copy(
                    src_ref=ins[a], dst_ref=zs[a].at[:, me], send_sem=send_sems.at[4 * a + k],
                    recv_sem=recv_sems.at[4 * a + k], device_id=to, device_id_type=MESH).start()
        token[...] = jnp.zeros(token.shape, F32)

    sems, thru, token = _split_call(body, name, list(shards) + zones, (4 * n, 4 * n, n), extra=(after,))
    return (sems, thru, n), token


def gather_wait(started, after, name):
    sems, thru, n = started

    def body(*refs):
        zs = refs[n:2 * n]
        send_sems, recv_sems, local_sems = refs[2 * n:2 * n + 3]
        _, targets = _gather_targets()
        for a in range(n):
            for k, to in enumerate(targets):
                cp = pltpu.make_async_remote_copy(
                    src_ref=zs[a].at[:, 0], dst_ref=zs[a].at[:, 0], send_sem=send_sems.at[4 * a + k],
                    recv_sem=recv_sems.at[4 * a + k], device_id=to, device_id_type=MESH)
                cp.wait_send()
                cp.wait_recv()
            pltpu.make_async_copy(zs[a].at[:, 0], zs[a].at[:, 0], local_sems.at[a]).wait()

    return _split_call(body, name, thru, (4 * n, 4 * n, n), extra=(*sems, after), with_token=False)[n:]


def forward_start(zones, after, name):
    n = len(zones)

    def body(*refs):
        zs = refs[:n]
        send_sems, recv_sems = refs[n + 1:n + 3]
        token = refs[-1]
        x, y, c = _mesh_pos()
        for a in range(n):
            for j, chip in enumerate([(1 - x, y), (x, 1 - y), (1 - x, 1 - y)]):
                blk = zs[a].at[:, 4 * chip[0] + 2 * chip[1] + c]
                pltpu.make_async_remote_copy(
                    src_ref=blk, dst_ref=blk, send_sem=send_sems.at[3 * a + j], recv_sem=recv_sems.at[3 * a + j],
                    device_id=(x, y, 1 - c), device_id_type=MESH).start()
        token[...] = jnp.zeros(token.shape, F32)

    sems, thru, token = _split_call(body, name, list(zones), (3 * n, 3 * n), extra=(after,))
    return (sems, thru, n), token


def forward_wait(started, after, name):
    sems, thru, n = started

    def body(*refs):
        zs = refs[:n]
        send_sems, recv_sems = refs[n:n + 2]
        x, y, c = _mesh_pos()
        for a in range(n):
            for j in range(3):
                cp = pltpu.make_async_remote_copy(
                    src_ref=zs[a].at[:, 0], dst_ref=zs[a].at[:, 0], send_sem=send_sems.at[3 * a + j],
                    recv_sem=recv_sems.at[3 * a + j], device_id=(x, y, 1 - c), device_id_type=MESH)
                cp.wait_send()
                cp.wait_recv()

    return _split_call(body, name, thru, (3 * n, 3 * n), extra=(*sems, after), with_token=False)


def scatter_start(groups, name):
    n = len(groups)
    flat = [g for grp in groups for g in grp]
    nf = len(flat)
    offs = np.cumsum([0] + [len(grp) for grp in groups])
    lands = [lax.empty((N_DEV, len(grp)) + grp[0].shape[1:], grp[0].dtype) for grp in groups]

    def body(*refs):
        ins, zones = refs[:nf], refs[nf:nf + n]
        send_sems, recv_sems, local_sems = refs[nf + n:nf + n + 3]
        token = refs[-1]
        me, peers = _peers()
        for a in range(n):
            for w in range(len(groups[a])):
                pltpu.make_async_copy(ins[offs[a] + w].at[me], zones[a].at[me, w], local_sems.at[a]).start()
        for k, peer in enumerate(peers):
            p_id = 4 * peer[0] + 2 * peer[1] + peer[2]
            for a in range(n):
                for w in range(len(groups[a])):
                    pltpu.make_async_remote_copy(
                        src_ref=ins[offs[a] + w].at[p_id], dst_ref=zones[a].at[me, w],
                        send_sem=send_sems.at[7 * a + k], recv_sem=recv_sems.at[7 * a + k],
                        device_id=peer, device_id_type=MESH).start()
        token[...] = jnp.zeros(token.shape, F32)

    hbm = lambda t: pltpu.with_memory_space_constraint(t, pltpu.HBM)
    outs = pl.pallas_call(
        body, name=name,
        in_specs=[HBM_SPEC] * (nf + n),
        out_specs=[SEM_SPEC] * 3 + [HBM_SPEC] * (nf + n) + [pl.BlockSpec(memory_space=pltpu.VMEM)],
        out_shape=[pltpu.SemaphoreType.DMA((7 * n,)), pltpu.SemaphoreType.DMA((7 * n,)), pltpu.SemaphoreType.DMA((n,))]
                  + [pltpu.HBM(t.shape, t.dtype) for t in flat + lands]
                  + [jax.ShapeDtypeStruct((8, LANES), F32)],
        input_output_aliases={i: 3 + i for i in range(nf + n)},
        compiler_params=pltpu.CompilerParams(has_side_effects=pltpu.SideEffectType.DATAFLOW_SIDE_EFFECTING),
    )(*[hbm(t) for t in flat], *[hbm(t) for t in lands])
    sems, thru, token = outs[:3], outs[3:3 + nf + n], outs[-1]
    return (sems, thru, [len(grp) for grp in groups]), token


def scatter_wait(started, after, name):
    (send_sems, recv_sems, local_sems), thru, sizes = started
    n = len(sizes)
    nf = len(thru) - n

    def body(*refs):
        zones = refs[nf:nf + n]
        s_sems, r_sems, l_sems = refs[nf + n:nf + n + 3]
        me, peers = _peers()
        for a in range(n):
            for k, peer in enumerate(peers):
                cp = pltpu.make_async_remote_copy(
                    src_ref=zones[a].at[0], dst_ref=zones[a].at[0],
                    send_sem=s_sems.at[7 * a + k], recv_sem=r_sems.at[7 * a + k], device_id=peer,
                    device_id_type=MESH)
                cp.wait_send()
                cp.wait_recv()
            pltpu.make_async_copy(zones[a].at[0], zones[a].at[0], l_sems.at[a]).wait()

    outs = pl.pallas_call(
        body, name=name,
        in_specs=[HBM_SPEC] * (nf + n) + [SEM_SPEC] * 3 + [pl.BlockSpec(memory_space=pl.ANY)],
        out_specs=[HBM_SPEC] * (nf + n),
        out_shape=[pltpu.HBM(t.shape, t.dtype) for t in thru],
        input_output_aliases={i: i for i in range(nf + n)},
        compiler_params=pltpu.CompilerParams(has_side_effects=pltpu.SideEffectType.DATAFLOW_SIDE_EFFECTING),
    )(*thru, send_sems, recv_sems, local_sems, after)
    return outs[nf:]


def share_small(small, after):
    def body(s_ref, after_ref, o_ref, send_sems, recv_sems, local_sem):
        me, peers = _peers()
        mine = pltpu.make_async_copy(s_ref, o_ref.at[me], local_sem)
        mine.start()
        copies = [pltpu.make_async_remote_copy(src_ref=s_ref, dst_ref=o_ref.at[me], send_sem=send_sems.at[k],
                                               recv_sem=recv_sems.at[k], device_id=peer, device_id_type=MESH)
                  for k, peer in enumerate(peers)]
        for cp in copies:
            cp.start()
        for cp in copies:
            cp.wait()
        mine.wait()

    vm = pl.BlockSpec(memory_space=pltpu.VMEM)
    return pl.pallas_call(
        body, name="share_small", in_specs=[vm, pl.BlockSpec(memory_space=pl.ANY)], out_specs=vm,
        out_shape=jax.ShapeDtypeStruct((N_DEV,) + small.shape, small.dtype),
        scratch_shapes=[pltpu.SemaphoreType.DMA((7,)), pltpu.SemaphoreType.DMA((7,)), pltpu.SemaphoreType.DMA],
    )(small, after)


def sum_sources(recv, name):
    _, w, r, c = recv.shape

    def body(r_ref, o_ref):
        acc = r_ref[0, 0].astype(F32)
        for src in range(1, N_DEV):
            acc = acc + r_ref[src, 0].astype(F32)
        o_ref[0] = acc

    return pl.pallas_call(
        body, name=name, grid=(w,),
        in_specs=[pl.BlockSpec((N_DEV, 1, r, c), lambda i: (0, i, 0, 0))],
        out_specs=pl.BlockSpec((1, r, c), lambda i: (i, 0, 0)),
        out_shape=jax.ShapeDtypeStruct((w, r, c), F32),
        compiler_params=_params(),
    )(recv)


def _adamw_math(w, g, m, v):
    m = ADAM_B1 * m + (1.0 - ADAM_B1) * g
    v = ADAM_B2 * v + (1.0 - ADAM_B2) * (g * g)
    m_hat = m / (1.0 - ADAM_B1 ** ADAM_STEP)
    v_hat = v / (1.0 - ADAM_B2 ** ADAM_STEP)
    delta = -ADAM_LR * (m_hat / (jnp.sqrt(v_hat) + ADAM_EPS) + ADAM_WD * w)
    return delta, m, v


def adamw(w, g, m, v, name):
    shape = w.shape
    c = shape[-1]
    r = int(np.prod(shape[:-1]))
    w2, g2, m2, v2 = (t.reshape(r, c) for t in (w, g, m, v))
    tr = next(t for t in range(min(r, 512), 0, -1) if r % t == 0 and (t % 8 == 0 or t == r))

    def body(w_ref, g_ref, m_ref, v_ref, d_ref, mo_ref, vo_ref):
        d_ref[...], mo_ref[...], vo_ref[...] = _adamw_math(w_ref[...], g_ref[...], m_ref[...], v_ref[...])

    spec = pl.BlockSpec((tr, c), lambda i: (i, 0))
    outs = pl.pallas_call(
        body, name=name, grid=(r // tr,),
        in_specs=[spec] * 4, out_specs=[spec] * 3,
        out_shape=[jax.ShapeDtypeStruct((r, c), F32)] * 3,
        compiler_params=_params(),
    )(w2, g2, m2, v2)
    return tuple(t.reshape(shape) for t in outs)


def adamw_small(w, recv, m, v, name):
    def body(w_ref, r_ref, m_ref, v_ref, g_ref, d_ref, mo_ref, vo_ref):
        g = r_ref[0]
        for src in range(1, N_DEV):
            g = g + r_ref[src]
        g_ref[...] = g
        d_ref[...], mo_ref[...], vo_ref[...] = _adamw_math(w_ref[...], g, m_ref[...], v_ref[...])

    vm = pl.BlockSpec(memory_space=pltpu.VMEM)
    return pl.pallas_call(
        body, name=name, in_specs=[vm] * 4, out_specs=[vm] * 4,
        out_shape=[jax.ShapeDtypeStruct(w.shape, F32)] * 4,
        compiler_params=pltpu.CompilerParams(vmem_limit_bytes=V7X_VMEM_LIMIT),
    )(w, recv, m, v)


SMALL_NAMES = ("ffn1_norm", "mix_norm", "ffn2_norm", "b_gate", "na_q_norm", "na_k_norm", "sw_q_norm", "sw_k_norm",
               "na_rpb", "sw_sink", "t5_rel_table")


def _pack_small(parts):
    flat = jnp.concatenate([parts[k].reshape(-1).astype(F32) for k in SMALL_NAMES])
    n = flat.shape[0]
    rows = -(-n // (8 * LANES)) * 8
    return jnp.pad(flat, (0, rows * LANES - n)).reshape(rows, LANES)


def _unpack_small(packed, like):
    flat = packed.reshape(-1)
    out, off = {}, 0
    for k in SMALL_NAMES:
        n = int(np.prod(like[k].shape))
        out[k] = flat[off:off + n].reshape(like[k].shape)
        off += n
    return out


def kernel(x, ffn1_norm, ffn1_w_gate, ffn1_w_up, ffn1_w_down, mix_norm, w_in, b_gate, na_q_norm, na_k_norm, na_rpb, sw_q_norm, sw_k_norm, sw_sink, t5_rel_table, w_branch_na, w_branch_sw, w_out, ffn2_norm, ffn2_w_gate, ffn2_w_up, ffn2_w_down, loss_target, m_ffn1_norm, m_ffn1_w_gate, m_ffn1_w_up, m_ffn1_w_down, m_mix_norm, m_w_in, m_b_gate, m_na_q_norm, m_na_k_norm, m_na_rpb, m_sw_q_norm, m_sw_k_norm, m_sw_sink, m_t5_rel_table, m_w_branch_na, m_w_branch_sw, m_w_out, m_ffn2_norm, m_ffn2_w_gate, m_ffn2_w_up, m_ffn2_w_down, v_ffn1_norm, v_ffn1_w_gate, v_ffn1_w_up, v_ffn1_w_down, v_mix_norm, v_w_in, v_b_gate, v_na_q_norm, v_na_k_norm, v_na_rpb, v_sw_q_norm, v_sw_k_norm, v_sw_sink, v_t5_rel_table, v_w_branch_na, v_w_branch_sw, v_w_out, v_ffn2_norm, v_ffn2_w_gate, v_ffn2_w_up, v_ffn2_w_down):
    weights = dict(ffn1_norm=ffn1_norm, ffn1_w_gate=ffn1_w_gate, ffn1_w_up=ffn1_w_up, ffn1_w_down=ffn1_w_down,
                   mix_norm=mix_norm, w_in=w_in, b_gate=b_gate, na_q_norm=na_q_norm, na_k_norm=na_k_norm,
                   na_rpb=na_rpb, sw_q_norm=sw_q_norm, sw_k_norm=sw_k_norm, sw_sink=sw_sink,
                   t5_rel_table=t5_rel_table, w_branch_na=w_branch_na, w_branch_sw=w_branch_sw, w_out=w_out,
                   ffn2_norm=ffn2_norm, ffn2_w_gate=ffn2_w_gate, ffn2_w_up=ffn2_w_up, ffn2_w_down=ffn2_w_down)
    mom_m = dict(ffn1_norm=m_ffn1_norm, ffn1_w_gate=m_ffn1_w_gate, ffn1_w_up=m_ffn1_w_up, ffn1_w_down=m_ffn1_w_down,
                 mix_norm=m_mix_norm, w_in=m_w_in, b_gate=m_b_gate, na_q_norm=m_na_q_norm, na_k_norm=m_na_k_norm,
                 na_rpb=m_na_rpb, sw_q_norm=m_sw_q_norm, sw_k_norm=m_sw_k_norm, sw_sink=m_sw_sink,
                 t5_rel_table=m_t5_rel_table, w_branch_na=m_w_branch_na, w_branch_sw=m_w_branch_sw, w_out=m_w_out,
                 ffn2_norm=m_ffn2_norm, ffn2_w_gate=m_ffn2_w_gate, ffn2_w_up=m_ffn2_w_up, ffn2_w_down=m_ffn2_w_down)
    mom_v = dict(ffn1_norm=v_ffn1_norm, ffn1_w_gate=v_ffn1_w_gate, ffn1_w_up=v_ffn1_w_up, ffn1_w_down=v_ffn1_w_down,
                 mix_norm=v_mix_norm, w_in=v_w_in, b_gate=v_b_gate, na_q_norm=v_na_q_norm, na_k_norm=v_na_k_norm,
                 na_rpb=v_na_rpb, sw_q_norm=v_sw_q_norm, sw_k_norm=v_sw_k_norm, sw_sink=v_sw_sink,
                 t5_rel_table=v_t5_rel_table, w_branch_na=v_w_branch_na, w_branch_sw=v_w_branch_sw, w_out=v_w_out,
                 ffn2_norm=v_ffn2_norm, ffn2_w_gate=v_ffn2_w_gate, ffn2_w_up=v_ffn2_w_up, ffn2_w_down=v_ffn2_w_down)
    order = list(weights)

    depth = ffn1_norm.shape[0]
    s, d = x.shape[1], x.shape[2]
    xs = x[0]
    tr = lambda w: jnp.swapaxes(w, -1, -2)

    a_loc = jnp.stack([t for l in range(depth) for t in (
        tr(ffn1_w_gate[l]), tr(ffn1_w_up[l]), ffn1_w_down[l],
        tr(ffn2_w_gate[l]), tr(ffn2_w_up[l]), ffn2_w_down[l])]).astype(BF16)
    b_loc = tr(w_in).astype(BF16)
    c_loc = w_out.astype(BF16)
    d_loc = jnp.stack([t for l in range(depth) for t in (tr(w_branch_na[l]), tr(w_branch_sw[l]))]).astype(BF16)
    merge = lambda t: t.reshape(t.shape[0], N_DEV * t.shape[2], t.shape[3])
    no_dep = jnp.zeros((8, LANES), F32)

    def shards_of(kind, l):
        if kind == "ffn1":
            return [a_loc[6 * l:6 * l + 3]]
        if kind == "win":
            return [b_loc[l:l + 1]]
        return [a_loc[6 * l + 3:6 * l + 6], c_loc[l:l + 1], d_loc[2 * l:2 * l + 2]]

    def start(kind, l, after):
        return gather_start(shards_of(kind, l), after, f"gather_{kind}_{l}")

    def arrive(started, kind, l, after):
        zones = gather_wait(started, after, f"gather_{kind}_{l}_wait")
        return forward_start(zones, no_dep, f"forward_{kind}_{l}")

    def finish(fwd, kind, l, after):
        return [merge(z) for z in forward_wait(fwd, after, f"forward_{kind}_{l}_wait")]

    bd = jnp.asarray(np.kron(np.eye(NA_WIDTH // HEAD_DIM), np.full((HEAD_DIM, HEAD_DIM), 1.0 / HEAD_DIM)), BF16)
    bmap = jnp.asarray(_t5_bucket_map())
    tile8 = lambda g: jnp.tile(g, NA_WIDTH // HEAD_DIM).reshape(1, NA_WIDTH)
    tile2 = lambda g: jnp.tile(g, SW_KV_WIDTH // HEAD_DIM).reshape(1, SW_KV_WIDTH)

    first = merge(gather_weights([a_loc[0:3]])[0])
    st_win, dep = start("win", 0, first)
    t5b = t5_expand(t5_rel_table, bmap, "t5_expand").reshape(SW_STACK, 3 * SW_BLOCK)

    saved = []
    layer_w = {0: dict(wg1=(first, 0), wu1=(first, 1), wd1=(first, 2))}
    cur = xs
    for l in range(depth):
        sv = {}
        lw = layer_w[l]
        sv["x0"] = cur
        sv["xn1"], sv["hg1"], sv["hu1"], sv["act1"] = ffn_up(cur, ffn1_norm[l][None], lw["wg1"], lw["wu1"], dep,
                                                             f"ffn1_up_{l}")
        cur = ffn_down(cur, sv["act1"], lw["wd1"], no_dep, f"ffn1_down_{l}")
        sv["x1"] = cur
        fwd, _ = arrive(st_win, "win", l, cur)
        st_rest, tok = start("rest", l, cur)
        (zb,) = finish(fwd, "win", l, tok)
        lw["win"] = (zb, 0)
        sv["gains"] = (tile8(na_q_norm[l]), tile8(na_k_norm[l]), tile8(sw_q_norm[l]), tile2(sw_k_norm[l]))
        sv["hn"], sv["zq"], sv["qa"], sv["ka"], sv["qs"], sv["ks"], sv["gt"] = mix_in(
            cur, mix_norm[l][None], lw["win"], b_gate[l][None], *sv["gains"], bd, f"mix_in_{l}")
        sv["t2"] = rpb_expand(na_rpb[l].reshape(-1), na_rpb.shape[1], no_dep, f"rpb_expand_{l}")
        sv["o_na"] = na_fwd(sv["qa"], sv["ka"], sv["zq"], sv["t2"], f"na_fwd_{l}")
        dep = no_dep
        if l + 1 < depth:
            st_ffn1, dep = start("ffn1", l + 1, sv["o_na"])
        sv["o_sw"] = sw_fwd(sv["qs"], sv["ks"], sv["zq"], t5b, sw_sink[l], dep, f"sw_fwd_{l}")
        fwd, tok = arrive(st_rest, "rest", l, sv["o_sw"])
        za, zc, zd = finish(fwd, "rest", l, tok)
        lw.update(wg2=(za, 0), wu2=(za, 1), wd2=(za, 2), wout=(zc, 0), wna=(zd, 0), wsw=(zd, 1))
        cur, sv["a_na"], sv["a_sw"], sv["merged"] = merge_out(
            cur, sv["o_na"], sv["o_sw"], sv["gt"], lw["wna"], lw["wsw"], lw["wout"], f"merge_out_{l}")
        sv["x2"] = cur
        dep = no_dep
        if l + 1 < depth:
            st_win, dep = start("win", l + 1, cur)
        sv["xn2"], sv["hg2"], sv["hu2"], sv["act2"] = ffn_up(cur, ffn2_norm[l][None], lw["wg2"], lw["wu2"], dep,
                                                             f"ffn2_up_{l}")
        dep = no_dep
        if l + 1 < depth:
            fwd, dep = arrive(st_ffn1, "ffn1", l + 1, sv["act2"])
        cur = ffn_down(cur, sv["act2"], lw["wd2"], dep, f"ffn2_down_{l}")
        dep = no_dep
        if l + 1 < depth:
            (za,) = finish(fwd, "ffn1", l + 1, cur)
            layer_w[l + 1] = dict(wg1=(za, 0), wu1=(za, 1), wd1=(za, 2))
        saved.append(sv)

    dx, loss_acc = loss_grad(cur, loss_target[0], "loss_grad")
    loss = lax.psum(jnp.sum(loss_acc) * (0.5 / d), ("x", "y", "c"))

    split = lambda t: t.reshape(N_DEV, t.shape[0] // N_DEV, t.shape[1])
    pending = {}
    small = {k: [None] * depth for k in SMALL_NAMES if k != "t5_rel_table"}
    dbias_sw = []
    for l in reversed(range(depth)):
        sv = saved[l]
        lw = layer_w[l]
        wg1, wu1, wd1, wg2, wu2, wd2 = (lw[k] for k in ("wg1", "wu1", "wd1", "wg2", "wu2", "wd2"))
        win_t, wout_l, wna_t, wsw_t = lw["win"], lw["wout"], lw["wna"], lw["wsw"]
        blocks = ((2, "x2", "xn2", "hg2", "hu2", "act2", wg2, wu2, wd2, "ffn2_norm", 3),
                  (1, "x0", "xn1", "hg1", "hu1", "act1", wg1, wu1, wd1, "ffn1_norm", 0))

        def ffn_backward(dx, blk):
            tag, xk, xnk, hgk, huk, actk, wg, wu, wd, norm_name, slot = blk
            gains = weights[norm_name]
            dxb, dhg, dhu = ffn_bwd_act(dx, wd, sv[hgk], sv[huk], f"ffn{tag}_bwd_act_{l}")
            gwd = tn_matmul(sv[actk], dxb, 0.5, f"ffn{tag}_dwd_{l}")
            gwg = tn_matmul(dhg, sv[xnk], 1.0, f"ffn{tag}_dwg_{l}")
            gwu = tn_matmul(dhu, sv[xnk], 1.0, f"ffn{tag}_dwu_{l}")
            pending[f"ffn{tag}_{l}"], token = scatter_start([[split(gwg), split(gwu), split(gwd)]],
                                                            f"scatter_ffn{tag}_{l}")
            dx, dg = proj_bwd_norm([dhg, dhu], [wg, wu], sv[xk], gains[l][None], dx, token, f"ffn{tag}_bwd_x_{l}")
            small[norm_name][l] = dg[0]
            return dx

        dx = ffn_backward(dx, blocks[0])
        dxb, dzg, da_na, da_sw, do_na, do_sw, dbg = mix_bwd_out(
            dx, sv["gt"], sv["a_na"], sv["a_sw"], wna_t, wsw_t, wout_l, f"mix_bwd_out_{l}")
        small["b_gate"][l] = dbg[0]
        gwout = tn_matmul(sv["merged"], dxb, 1.0, f"dwout_{l}")
        gwna = tn_matmul(da_na, sv["o_na"], 1.0, f"dwna_{l}")
        gwsw = tn_matmul(da_sw, sv["o_sw"], 1.0, f"dwsw_{l}")
        dqa, dka, dva, dt2 = na_bwd(sv["qa"], sv["ka"], sv["zq"], sv["t2"], sv["o_na"], do_na, f"na_bwd_{l}")
        dqs, dks, dvs, dbias, dsink = sw_bwd(sv["qs"], sv["ks"], sv["zq"], t5b, sw_sink[l], sv["o_sw"], do_sw,
                                             f"sw_bwd_{l}")
        dbias_sw.append(dbias.reshape(SW_HEADS, SW_BLOCK, 3 * SW_BLOCK))
        small["sw_sink"][l] = jnp.sum(dsink[:, 0].reshape(SW_HEADS, SW_BLOCK), axis=1)
        drpb = rpb_reduce(dt2, f"rpb_reduce_{l}")
        small["na_rpb"][l] = drpb[:, :, :2 * NA_COLS - 1, 0]
        dz, dgqa, dgka, dgqs, dgks = qk_norm_bwd(dqa, dka, dva, dqs, dks, dvs, sv["zq"], dzg, *sv["gains"], bd,
                                                 f"qk_norm_bwd_{l}")
        fold = lambda g: jnp.sum(g.reshape(-1, HEAD_DIM), axis=0)
        small["na_q_norm"][l], small["na_k_norm"][l] = fold(dgqa), fold(dgka)
        small["sw_q_norm"][l], small["sw_k_norm"][l] = fold(dgqs), fold(dgks)
        gwin = tn_matmul(dz, sv["hn"], 1.0, f"dwin_{l}")
        pending[f"mix_{l}"], token = scatter_start([[split(gwout)], [split(gwna), split(gwsw)], [split(gwin)]],
                                                   f"scatter_mix_{l}")
        dx, dg = proj_bwd_norm([dz], [win_t], sv["x1"], mix_norm[l][None], dx, token, f"mix_bwd_x_{l}")
        small["mix_norm"][l] = dg[0]
        dx = ffn_backward(dx, blocks[1])

    dtab = t5_reduce(dbias_sw, bmap, "t5_reduce")
    small_parts = {k: jnp.stack(v) for k, v in small.items()}
    small_parts["t5_rel_table"] = jnp.transpose(dtab[:, :, 0])
    small_packed = _pack_small(small_parts)

    summed = {}
    layers = lambda f: jnp.stack([f(l) for l in range(depth)])
    grads, delta, new_m, new_v = {}, {}, {}, {}

    def collect(key, after):
        zones = scatter_wait(pending[key], after, f"wait_{key}")
        summed[key] = [sum_sources(z, f"sum_{key}_{i}") for i, z in enumerate(zones)]

    def update(k, g, transposed):
        view = tr if transposed else (lambda t: t)
        d_k, m_k, v_k = adamw(view(weights[k]), g, view(mom_m[k]), view(mom_v[k]), f"adamw_{k}")
        grads[k], delta[k], new_m[k], new_v[k] = view(g), view(d_k), view(m_k), view(v_k)
        return d_k

    last_key = "ffn1_0"
    for key in pending:
        if key != last_key:
            collect(key, dx)
    update("ffn2_w_gate", layers(lambda l: summed[f"ffn2_{l}"][0][0]), True)
    update("ffn2_w_up", layers(lambda l: summed[f"ffn2_{l}"][0][1]), True)
    update("ffn2_w_down", layers(lambda l: summed[f"ffn2_{l}"][0][2]), False)
    update("w_out", layers(lambda l: summed[f"mix_{l}"][0][0]), False)
    update("w_branch_na", layers(lambda l: summed[f"mix_{l}"][1][0]), True)
    update("w_branch_sw", layers(lambda l: summed[f"mix_{l}"][1][1]), True)
    done = update("w_in", layers(lambda l: summed[f"mix_{l}"][2][0]), True)
    collect(last_key, done)
    update("ffn1_w_gate", layers(lambda l: summed[f"ffn1_{l}"][0][0]), True)
    update("ffn1_w_up", layers(lambda l: summed[f"ffn1_{l}"][0][1]), True)
    done = update("ffn1_w_down", layers(lambda l: summed[f"ffn1_{l}"][0][2]), False)
    rs_ = share_small(small_packed, done)
    g_s, d_s, m_s, v_s = adamw_small(_pack_small(weights), rs_, _pack_small(mom_m), _pack_small(mom_v), "adamw_small")
    for dst, packed in ((grads, g_s), (delta, d_s), (new_m, m_s), (new_v, v_s)):
        dst.update(_unpack_small(packed, weights))

    return (loss, dx[None], *[grads[k] for k in order], *[delta[k] for k in order],
            *[new_m[k] for k in order], *[new_v[k] for k in order])
```

```python
import functools
import math

import numpy as np
import jax
import jax.numpy as jnp
from jax import lax
from jax.experimental import pallas as pl
from jax.experimental.pallas import tpu as pltpu

F32 = jnp.float32
BF16 = jnp.bfloat16
MESH = pl.DeviceIdType.MESH

N_DEV = 8
EPS = 1e-6
NEG = -1e30
HEAD_DIM = 64
GRID_W = 64
NA_ROWS = 8
NA_COLS = 16
NA_WIDTH = 512
SW_Q_WIDTH = 512
SW_KV_WIDTH = 128
SW_BLOCK = 128
SW_HEADS = 8
SW_REP = 4
REL_BUCKETS = 32
REL_MAX_DIST = 128
QKV_WIDTH = 3 * NA_WIDTH + SW_Q_WIDTH + 2 * SW_KV_WIDTH
SCALE = 1.0 / math.sqrt(HEAD_DIM)

ADAM_LR = 0.001
ADAM_B1 = 0.9
ADAM_B2 = 0.999
ADAM_EPS = 1e-08
ADAM_WD = 0.01
ADAM_STEP = 10

V7X_VMEM_LIMIT = 56 * 1024 * 1024
LANES = 128
MXU_TILE = 256

NT = (((1,), (1,)), ((), ()))
TN = (((0,), (0,)), ((), ()))


def _params(n_grid=1):
    return pltpu.CompilerParams(dimension_semantics=("arbitrary",) * n_grid,
                                vmem_limit_bytes=V7X_VMEM_LIMIT)


def _row_tile(s):
    for t in (512, 256, 128, 64, 32, 16, 8):
        if s % t == 0:
            return t
    raise ValueError(s)


def _tn_tile(n):
    best = max(t for t in range(LANES, min(n, 2304) + 1, LANES) if n % t == 0) if n % LANES == 0 else n
    return best // 2 if best == n and n >= 1024 else best


ONCE = pl.Buffered(1)


def _col_chunk(n):
    return MXU_TILE if n % MXU_TILE == 0 else n


def _dot(a, b):
    return jnp.dot(a, b, preferred_element_type=F32)


def _dotg(a, b, dn):
    return lax.dot_general(a, b, dn, preferred_element_type=F32)


def _sigmoid(v):
    return 1.0 / (1.0 + jnp.exp(-v))


def _rstd(xv):
    return lax.rsqrt(jnp.mean(xv * xv, axis=-1, keepdims=True) + EPS)


def _full(shape):
    nd = len(shape)
    return pl.BlockSpec(shape, lambda i, _n=nd: (0,) * _n)


def _rows(tm, width):
    return pl.BlockSpec((tm, width), lambda i: (i, 0))


def _mat(stack, idx):
    return pl.BlockSpec((None,) + tuple(stack.shape[1:]), lambda i, _w=idx: (_w, 0, 0), pipeline_mode=ONCE)


def _group_mean(v, bd):
    hi = v.astype(BF16)
    lo = (v - hi.astype(F32)).astype(BF16)
    return _dot(hi, bd) + _dot(lo, bd)


def ffn_up(x, gain, wg_t, wu_t, dep, name):
    s, d = x.shape
    f = wg_t[0].shape[1]
    tm = _row_tile(s)
    fc = _col_chunk(f)

    def body(x_ref, g_ref, wg_ref, wu_ref, dep_ref, xn_ref, hg_ref, hu_ref, act_ref):
        xv = x_ref[...]
        xn = (xv * _rstd(xv) * g_ref[...]).astype(BF16)
        xn_ref[...] = xn
        for c0 in range(0, f, fc):
            hg = _dotg(xn, wg_ref[c0:c0 + fc, :], NT)
            hu = _dotg(xn, wu_ref[c0:c0 + fc, :], NT)
            hg_ref[:, c0:c0 + fc] = hg.astype(BF16)
            hu_ref[:, c0:c0 + fc] = hu.astype(BF16)
            act_ref[:, c0:c0 + fc] = (hg * _sigmoid(hg) * hu).astype(BF16)

    return pl.pallas_call(
        body, name=name, grid=(s // tm,),
        in_specs=[_rows(tm, d), _full((1, d)), _mat(*wg_t), _mat(*wu_t), _full(dep.shape)],
        out_specs=[_rows(tm, d), _rows(tm, f), _rows(tm, f), _rows(tm, f)],
        out_shape=[jax.ShapeDtypeStruct((s, d), BF16)] + [jax.ShapeDtypeStruct((s, f), BF16)] * 3,
        compiler_params=_params(),
    )(x, gain, wg_t[0], wu_t[0], dep)


def ffn_down(x, act, wd, dep, name):
    s, d = x.shape
    f = act.shape[1]
    tm = _row_tile(s)

    def body(x_ref, a_ref, w_ref, dep_ref, o_ref):
        o_ref[...] = x_ref[...] + 0.5 * _dot(a_ref[...], w_ref[...])

    return pl.pallas_call(
        body, name=name, grid=(s // tm,),
        in_specs=[_rows(tm, d), _rows(tm, f), _mat(*wd), _full(dep.shape)],
        out_specs=_rows(tm, d),
        out_shape=jax.ShapeDtypeStruct((s, d), F32),
        compiler_params=_params(),
    )(x, act, wd[0], dep)


def mix_in(x, gain, win_t, b_gate, gq_na, gk_na, gq_sw, gk_sw, bd, name):
    s, d = x.shape
    tm = _row_tile(s)
    gc = _col_chunk(2 * d)

    def body(x_ref, g_ref, w_ref, b_ref, gqa_ref, gka_ref, gqs_ref, gks_ref, bd_ref,
             hn_ref, zq_ref, qa_ref, ka_ref, qs_ref, ks_ref, gt_ref):
        xv = x_ref[...]
        hn = (xv * _rstd(xv) * g_ref[...]).astype(BF16)
        hn_ref[...] = hn

        def proj(c0, c1):
            return _dotg(hn, w_ref[c0:c1, :], NT)

        def headnorm(z, g, bdm):
            return z * lax.rsqrt(_group_mean(z * z, bdm) + EPS) * g

        bd512 = bd_ref[...]
        bd128 = bd_ref[0:SW_KV_WIDTH, 0:SW_KV_WIDTH]
        z = proj(0, 512)
        zq_ref[:, 0:512] = z.astype(BF16)
        qa_ref[...] = (headnorm(z, gqa_ref[...], bd512) * SCALE).astype(BF16)
        z = proj(512, 1024)
        zq_ref[:, 512:1024] = z.astype(BF16)
        ka_ref[...] = headnorm(z, gka_ref[...], bd512).astype(BF16)
        z = proj(1024, 1536)
        zq_ref[:, 1024:1536] = z.astype(BF16)
        z = proj(1536, 2048)
        zq_ref[:, 1536:2048] = z.astype(BF16)
        qs_ref[...] = (headnorm(z, gqs_ref[...], bd512) * SCALE).astype(BF16)
        z = proj(2048, 2176)
        zq_ref[:, 2048:2176] = z.astype(BF16)
        ks_ref[...] = headnorm(z, gks_ref[...], bd128).astype(BF16)
        z = proj(2176, 2304)
        zq_ref[:, 2176:2304] = z.astype(BF16)
        for c0 in range(0, 2 * d, gc):
            zg = proj(QKV_WIDTH + c0, QKV_WIDTH + c0 + gc) + b_ref[:, c0:c0 + gc]
            gt_ref[:, c0:c0 + gc] = _sigmoid(zg).astype(BF16)

    return pl.pallas_call(
        body, name=name, grid=(s // tm,),
        in_specs=[_rows(tm, d), _full((1, d)), _mat(*win_t), _full((1, 2 * d)),
                  _full((1, 512)), _full((1, 512)), _full((1, 512)), _full((1, 128)), _full((512, 512))],
        out_specs=[_rows(tm, d), _rows(tm, QKV_WIDTH), _rows(tm, 512), _rows(tm, 512), _rows(tm, 512),
                   _rows(tm, 128), _rows(tm, 2 * d)],
        out_shape=[jax.ShapeDtypeStruct((s, d), BF16), jax.ShapeDtypeStruct((s, QKV_WIDTH), BF16),
                   jax.ShapeDtypeStruct((s, 512), BF16), jax.ShapeDtypeStruct((s, 512), BF16),
                   jax.ShapeDtypeStruct((s, 512), BF16), jax.ShapeDtypeStruct((s, 128), BF16),
                   jax.ShapeDtypeStruct((s, 2 * d), BF16)],
        compiler_params=_params(),
    )(x, gain, win_t[0], b_gate, gq_na, gk_na, gq_sw, gk_sw, bd)


def _na_iotas():
    qc = lax.broadcasted_iota(jnp.int32, (GRID_W, LANES), 0)
    ln = lax.broadcasted_iota(jnp.int32, (GRID_W, LANES), 1)
    low = ln < GRID_W
    kc = jnp.where(low, ln, ln - GRID_W)
    diff = kc - qc + (NA_COLS - 1)
    qcs = jnp.clip(qc - NA_COLS // 2, 0, GRID_W - NA_COLS)
    inwin = (kc >= qcs) & (kc < qcs + NA_COLS)
    return diff, low, inwin


NA_RI = 2 * NA_ROWS - 1
NA_CI = 2 * NA_COLS - 1
NA_T2 = NA_RI + 1


def _rpb_rows(rpb):
    h = rpb.shape[0]
    padded = jnp.pad(rpb, ((0, 0), (1, 1), (0, GRID_W - NA_CI)))
    return jnp.concatenate([padded[:, :NA_T2], padded[:, 1:NA_T2 + 1]], axis=2).reshape(h, NA_T2, LANES)


def _rpb_from_rows(rows):
    return rows[:, 1:, :NA_CI] + rows[:, :NA_RI, GRID_W:GRID_W + NA_CI]


def rpb_expand(rows, dep, name):
    n_heads = rows.shape[0]

    def body(r_ref, dep_ref, o_ref):
        for h in range(n_heads):
            for e in range(NA_T2):
                line = jnp.broadcast_to(r_ref[h, e:e + 1, :], (GRID_W, LANES))
                o_ref[h, e] = pltpu.roll(line, LANES - (NA_COLS - 1), 1, stride=1, stride_axis=0)

    return pl.pallas_call(
        body, name=name,
        in_specs=[pl.BlockSpec(memory_space=pltpu.VMEM), pl.BlockSpec(memory_space=pltpu.VMEM)],
        out_specs=pl.BlockSpec(memory_space=pltpu.VMEM),
        out_shape=jax.ShapeDtypeStruct((n_heads, NA_T2, GRID_W, LANES), F32),
        compiler_params=pltpu.CompilerParams(vmem_limit_bytes=V7X_VMEM_LIMIT),
    )(rows, dep)


def rpb_reduce(dt2, name):
    n_heads = dt2.shape[0]
    flip = jnp.asarray(np.eye(GRID_W)[::-1], BF16)

    def body(d_ref, j_ref, o_ref):
        jm = j_ref[...]
        for h in range(n_heads):
            for e in range(NA_T2):
                dv = d_ref[h, e]
                hi = dv.astype(BF16)
                mid = (dv - hi.astype(F32)).astype(BF16)
                lo = (dv - hi.astype(F32) - mid.astype(F32)).astype(BF16)
                rev = _dot(jm, hi) + _dot(jm, mid) + _dot(jm, lo)
                back = pltpu.roll(rev, LANES + (NA_COLS - 1) - (GRID_W - 1), 1, stride=1, stride_axis=0)
                o_ref[h, e:e + 1, :] = jnp.sum(back, axis=0, keepdims=True)

    return pl.pallas_call(
        body, name=name,
        in_specs=[pl.BlockSpec(memory_space=pltpu.VMEM)] * 2,
        out_specs=pl.BlockSpec(memory_space=pltpu.VMEM),
        out_shape=jax.ShapeDtypeStruct((n_heads, NA_T2, LANES), F32),
        compiler_params=pltpu.CompilerParams(vmem_limit_bytes=V7X_VMEM_LIMIT),
    )(dt2, flip)


NA_TQ = 4
NA_TK = NA_TQ + NA_ROWS
NA_KCH = NA_TK // 2


def _na_tile_geometry(t, rows):
    r = t * NA_TQ
    kbase = jnp.clip(r - NA_ROWS // 2, 0, rows - NA_TK)
    starts = [jnp.clip(r + a - NA_ROWS // 2, 0, rows - NA_ROWS) for a in range(NA_TQ)]
    return r, kbase, starts


def _na_tile_mask(kbase, starts, low, inwin):
    half = jnp.where(low, 0, 1)
    cols = []
    for c in range(NA_KCH):
        krow = kbase + 2 * c + half
        cols.append(jnp.concatenate(
            [jnp.where(inwin & (krow >= st) & (krow < st + NA_ROWS), 0.0, NEG) for st in starts], axis=0))
    return jnp.concatenate(cols, axis=1)


def _na_tile_index(r, kbase, a, c):
    return jnp.clip(kbase + 2 * c - (r + a) + NA_ROWS, 0, NA_T2 - 1)


def _na_tile_scores(q, k, t2_ref, hh, r, kbase, madd):
    bias = jnp.concatenate(
        [jnp.concatenate([t2_ref[hh, _na_tile_index(r, kbase, a, c)] for a in range(NA_TQ)], axis=0)
         for c in range(NA_KCH)], axis=1)
    return _dotg(q, k, NT) + bias + madd


def _softmax_rows(sc):
    e = jnp.exp(sc - jnp.max(sc, axis=1, keepdims=True))
    return e * (1.0 / jnp.sum(e, axis=1, keepdims=True))


def na_fwd(qa, ka, zq, t2, name):
    s = qa.shape[0]
    rows = s // GRID_W
    n_pairs = NA_WIDTH // LANES
    v_blk0 = (2 * NA_WIDTH) // LANES

    assert rows % NA_TQ == 0 and rows >= NA_TK
    tq, tk = NA_TQ * GRID_W, NA_TK * GRID_W

    def body(q_ref, k_ref, v_ref, t2_ref, o_ref, s_scr, p_scr):
        _, low, inwin = _na_iotas()

        def tile(t, carry):
            r, kbase, starts = _na_tile_geometry(t, rows)
            madd = _na_tile_mask(kbase, starts, low, inwin)
            qr = pl.ds(pl.multiple_of(r * GRID_W, tq), tq)
            kr = pl.ds(pl.multiple_of(kbase * GRID_W, tq), tk)
            for hh in range(2):
                lanes = slice(HEAD_DIM * hh, HEAD_DIM * (hh + 1))
                s_scr[tq * hh:tq * (hh + 1), :] = _na_tile_scores(q_ref[qr, lanes], k_ref[kr, lanes], t2_ref, hh, r,
                                                                  kbase, madd)
            p_scr[...] = _softmax_rows(s_scr[...]).astype(BF16)
            for hh in range(2):
                lanes = slice(HEAD_DIM * hh, HEAD_DIM * (hh + 1))
                o_ref[qr, lanes] = _dot(p_scr[tq * hh:tq * (hh + 1), :], v_ref[kr, lanes]).astype(BF16)
            return carry

        lax.fori_loop(0, rows // NA_TQ, tile, 0)

    col = lambda off: pl.BlockSpec((s, LANES), lambda p, _o=off: (0, _o + p))
    return pl.pallas_call(
        body, name=name, grid=(n_pairs,),
        in_specs=[col(0), col(0), col(v_blk0),
                  pl.BlockSpec((2, NA_T2, GRID_W, LANES), lambda p: (p, 0, 0, 0))],
        out_specs=col(0),
        out_shape=jax.ShapeDtypeStruct((s, NA_WIDTH), BF16),
        scratch_shapes=[pltpu.VMEM((2 * tq, tk), F32), pltpu.VMEM((2 * tq, tk), BF16)],
        compiler_params=_params(),
    )(qa, ka, zq, t2)


def na_bwd(qa, ka, zq, t2, o_na, do_na, name):
    s = qa.shape[0]
    rows = s // GRID_W
    n_pairs = NA_WIDTH // LANES
    v_blk0 = (2 * NA_WIDTH) // LANES

    tq, tk = NA_TQ * GRID_W, NA_TK * GRID_W

    def body(q_ref, k_ref, v_ref, t2_ref, o_ref, do_ref, dq_ref, dk_ref, dv_ref, dt2_ref, s_scr, dp_scr, ds_scr, p_scr):
        _, low, inwin = _na_iotas()
        dk_ref[...] = jnp.zeros(dk_ref.shape, F32)
        dv_ref[...] = jnp.zeros(dv_ref.shape, F32)
        dt2_ref[...] = jnp.zeros(dt2_ref.shape, F32)

        def tile(t, carry):
            r, kbase, starts = _na_tile_geometry(t, rows)
            madd = _na_tile_mask(kbase, starts, low, inwin)
            qr = pl.ds(pl.multiple_of(r * GRID_W, tq), tq)
            kr = pl.ds(pl.multiple_of(kbase * GRID_W, tq), tk)
            deltas = []
            for hh in range(2):
                lanes, srows = slice(HEAD_DIM * hh, HEAD_DIM * (hh + 1)), slice(tq * hh, tq * (hh + 1))
                do = do_ref[qr, lanes]
                s_scr[srows, :] = _na_tile_scores(q_ref[qr, lanes], k_ref[kr, lanes], t2_ref, hh, r, kbase, madd)
                dp_scr[srows, :] = _dotg(do, v_ref[kr, lanes], NT)
                deltas.append(jnp.sum(do.astype(F32) * o_ref[qr, lanes].astype(F32), axis=1, keepdims=True))
            p = _softmax_rows(s_scr[...])
            ds = p * (dp_scr[...] - jnp.concatenate(deltas, axis=0))
            p_scr[...] = p.astype(BF16)
            ds_scr[...] = ds.astype(BF16)
            for hh in range(2):
                for a in range(NA_TQ):
                    for c in range(NA_KCH):
                        e = _na_tile_index(r, kbase, a, c)
                        q0 = tq * hh + GRID_W * a
                        dt2_ref[hh, e] = dt2_ref[hh, e] + ds[q0:q0 + GRID_W, LANES * c:LANES * (c + 1)]
            for hh in range(2):
                lanes, srows = slice(HEAD_DIM * hh, HEAD_DIM * (hh + 1)), slice(tq * hh, tq * (hh + 1))
                dsb = ds_scr[srows, :]
                dq_ref[qr, lanes] = _dot(dsb, k_ref[kr, lanes])
                dk_ref[kr, lanes] = dk_ref[kr, lanes] + _dotg(dsb, q_ref[qr, lanes], TN)
                dv_ref[kr, lanes] = dv_ref[kr, lanes] + _dotg(p_scr[srows, :], do_ref[qr, lanes], TN)
            return carry

        lax.fori_loop(0, rows // NA_TQ, tile, 0)

    col = lambda off: pl.BlockSpec((s, LANES), lambda p, _o=off: (0, _o + p))
    t2spec = pl.BlockSpec((2, NA_T2, GRID_W, LANES), lambda p: (p, 0, 0, 0))
    return pl.pallas_call(
        body, name=name, grid=(n_pairs,),
        in_specs=[col(0), col(0), col(v_blk0), t2spec, col(0), col(0)],
        out_specs=[col(0), col(0), col(0), t2spec],
        out_shape=[jax.ShapeDtypeStruct((s, NA_WIDTH), F32)] * 3 + [jax.ShapeDtypeStruct(t2.shape, F32)],
        scratch_shapes=[pltpu.VMEM((2 * tq, tk), F32), pltpu.VMEM((2 * tq, tk), F32),
                        pltpu.VMEM((2 * tq, tk), BF16), pltpu.VMEM((2 * tq, tk), BF16)],
        compiler_params=_params(),
    )(qa, ka, zq, t2, o_na, do_na)


def _t5_bucket_map():
    rel = np.arange(3 * SW_BLOCK)[None, :] - SW_BLOCK - np.arange(SW_BLOCK)[:, None]
    nb = REL_BUCKETS // 2
    max_exact = nb // 2
    n = np.abs(rel)
    large = max_exact + (np.log(np.maximum(n, 1) / max_exact)
                         / np.log(REL_MAX_DIST / max_exact) * (nb - max_exact)).astype(np.int32)
    large = np.minimum(large, nb - 1)
    return ((rel > 0) * nb + np.where(n < max_exact, n, large)).astype(np.int32)


def t5_expand(table, bmap, dep, name):
    def body(tab_ref, bm_ref, dep_ref, o_ref):
        bm = bm_ref[...]
        for h in range(SW_HEADS):
            t = jnp.zeros(bm.shape, F32)
            for b in range(REL_BUCKETS):
                t = jnp.where(bm == b, tab_ref[b, h], t)
            o_ref[h] = t

    return pl.pallas_call(
        body, name=name,
        in_specs=[pl.BlockSpec(memory_space=pltpu.SMEM), pl.BlockSpec(memory_space=pltpu.VMEM),
                  pl.BlockSpec(memory_space=pltpu.VMEM)],
        out_specs=pl.BlockSpec(memory_space=pltpu.VMEM),
        out_shape=jax.ShapeDtypeStruct((SW_HEADS,) + bmap.shape, F32),
        compiler_params=pltpu.CompilerParams(vmem_limit_bytes=V7X_VMEM_LIMIT),
    )(table, bmap, dep)


def t5_reduce(dbias_list, bmap, name):
    n = len(dbias_list)

    def body(*refs):
        d_refs, bm_ref, o_ref = refs[:n], refs[n], refs[n + 1]
        bm = bm_ref[...]
        for h in range(SW_HEADS):
            dv = d_refs[0][h]
            for other in d_refs[1:]:
                dv = dv + other[h]
            rows = [jnp.sum(jnp.where(bm == b, dv, 0.0), axis=0, keepdims=True) for b in range(REL_BUCKETS)]
            r = jnp.concatenate(rows, axis=0)
            o_ref[h] = jnp.broadcast_to(jnp.sum(r, axis=1, keepdims=True), (REL_BUCKETS, LANES))

    return pl.pallas_call(
        body, name=name,
        in_specs=[pl.BlockSpec(memory_space=pltpu.VMEM)] * (n + 1),
        out_specs=pl.BlockSpec(memory_space=pltpu.VMEM),
        out_shape=jax.ShapeDtypeStruct((SW_HEADS, REL_BUCKETS, LANES), F32),
        compiler_params=pltpu.CompilerParams(vmem_limit_bytes=V7X_VMEM_LIMIT),
    )(*dbias_list, bmap)


def _sw_mask_iotas():
    a = lax.broadcasted_iota(jnp.int32, (SW_BLOCK, 3 * SW_BLOCK), 0)
    j = lax.broadcasted_iota(jnp.int32, (SW_BLOCK, 3 * SW_BLOCK), 1)
    inwin = jnp.abs(j - SW_BLOCK - a) <= SW_BLOCK
    return j, inwin


SW_STACK = SW_HEADS * SW_BLOCK


def _sw_softmax(sc, sk):
    m = jnp.maximum(jnp.max(sc, axis=1, keepdims=True), sk)
    e = jnp.exp(sc - m)
    es = jnp.exp(sk - m)
    inv = 1.0 / (jnp.sum(e, axis=1, keepdims=True) + es)
    return e * inv, es * inv


def _sw_prologue(k_ref, v_ref, kp, vp, sink_ref, s):
    pad = s + 2 * SW_BLOCK
    zeros = jnp.zeros((SW_BLOCK, SW_KV_WIDTH), BF16)
    kp[0:SW_BLOCK, :] = zeros
    vp[0:SW_BLOCK, :] = zeros
    kp[SW_BLOCK + s:pad, :] = zeros
    vp[SW_BLOCK + s:pad, :] = zeros
    kp[SW_BLOCK:SW_BLOCK + s, :] = k_ref[...]
    vp[SW_BLOCK:SW_BLOCK + s, :] = v_ref[...]
    return jnp.concatenate([jnp.full((SW_BLOCK, 1), sink_ref[h], F32) for h in range(SW_HEADS)], axis=0)


def sw_fwd(qs, ks, zq, t5b, sink, dep, name):
    s = qs.shape[0]
    nb = s // SW_BLOCK
    v_blk = (3 * NA_WIDTH + SW_Q_WIDTH + SW_KV_WIDTH) // LANES
    pad = s + 2 * SW_BLOCK

    def body(q_ref, k_ref, v_ref, b_ref, sink_ref, dep_ref, o_ref, kp, vp, s_scr, p_scr):
        sink_col = _sw_prologue(k_ref, v_ref, kp, vp, sink_ref, s)
        j, inwin = _sw_mask_iotas()

        def blk(n, carry):
            kpos = n * SW_BLOCK - SW_BLOCK + j
            madd = jnp.where(inwin & (kpos >= 0) & (kpos < s), 0.0, NEG)
            q0 = pl.multiple_of(n * SW_BLOCK, SW_BLOCK)
            qr, kr = pl.ds(q0, SW_BLOCK), pl.ds(q0, 3 * SW_BLOCK)
            for h in range(SW_HEADS):
                g = h // SW_REP
                s_scr[SW_BLOCK * h:SW_BLOCK * (h + 1), :] = _dotg(
                    q_ref[qr, HEAD_DIM * h:HEAD_DIM * (h + 1)], kp[kr, HEAD_DIM * g:HEAD_DIM * (g + 1)], NT) + madd
            p, _ = _sw_softmax(s_scr[...] + b_ref[...], sink_col)
            p_scr[...] = p.astype(BF16)
            for h in range(SW_HEADS):
                g = h // SW_REP
                o_ref[qr, HEAD_DIM * h:HEAD_DIM * (h + 1)] = _dot(
                    p_scr[SW_BLOCK * h:SW_BLOCK * (h + 1), :], vp[kr, HEAD_DIM * g:HEAD_DIM * (g + 1)]).astype(BF16)
            return carry

        lax.fori_loop(0, nb, blk, 0)

    return pl.pallas_call(
        body, name=name, grid=(1,),
        in_specs=[_full((s, SW_Q_WIDTH)), _full((s, SW_KV_WIDTH)),
                  pl.BlockSpec((s, SW_KV_WIDTH), lambda i: (0, v_blk)),
                  _full((SW_STACK, 3 * SW_BLOCK)), pl.BlockSpec(memory_space=pltpu.SMEM),
                  _full(dep.shape)],
        out_specs=_full((s, SW_Q_WIDTH)),
        out_shape=jax.ShapeDtypeStruct((s, SW_Q_WIDTH), BF16),
        scratch_shapes=[pltpu.VMEM((pad, SW_KV_WIDTH), BF16), pltpu.VMEM((pad, SW_KV_WIDTH), BF16),
                        pltpu.VMEM((SW_STACK, 3 * SW_BLOCK), F32), pltpu.VMEM((SW_STACK, 3 * SW_BLOCK), BF16)],
        compiler_params=_params(),
    )(qs, ks, zq, t5b, sink, dep)


def sw_bwd(qs, ks, zq, t5b, sink, o_sw, do_sw, name):
    s = qs.shape[0]
    nb = s // SW_BLOCK
    v_blk = (3 * NA_WIDTH + SW_Q_WIDTH + SW_KV_WIDTH) // LANES
    pad = s + 2 * SW_BLOCK

    def body(q_ref, k_ref, v_ref, b_ref, sink_ref, o_ref, do_ref,
             dq_ref, dk_ref, dv_ref, db_ref, dsk_ref, kp, vp, dkp, dvp, s_scr, dp_scr, ds_scr, p_scr):
        sink_col = _sw_prologue(k_ref, v_ref, kp, vp, sink_ref, s)
        dkp[...] = jnp.zeros(dkp.shape, F32)
        dvp[...] = jnp.zeros(dvp.shape, F32)
        db_ref[...] = jnp.zeros(db_ref.shape, F32)
        dsk_ref[...] = jnp.zeros(dsk_ref.shape, F32)
        j, inwin = _sw_mask_iotas()

        def blk(n, carry):
            kpos = n * SW_BLOCK - SW_BLOCK + j
            madd = jnp.where(inwin & (kpos >= 0) & (kpos < s), 0.0, NEG)
            q0 = pl.multiple_of(n * SW_BLOCK, SW_BLOCK)
            qr, kr = pl.ds(q0, SW_BLOCK), pl.ds(q0, 3 * SW_BLOCK)
            deltas = []
            for h in range(SW_HEADS):
                g = h // SW_REP
                hl, kl = slice(HEAD_DIM * h, HEAD_DIM * (h + 1)), slice(HEAD_DIM * g, HEAD_DIM * (g + 1))
                rows = slice(SW_BLOCK * h, SW_BLOCK * (h + 1))
                do = do_ref[qr, hl]
                s_scr[rows, :] = _dotg(q_ref[qr, hl], kp[kr, kl], NT) + madd
                dp_scr[rows, :] = _dotg(do, vp[kr, kl], NT)
                deltas.append(jnp.sum(do.astype(F32) * o_ref[qr, hl].astype(F32), axis=1, keepdims=True))
            delta = jnp.concatenate(deltas, axis=0)
            p, ps = _sw_softmax(s_scr[...] + b_ref[...], sink_col)
            ds = p * (dp_scr[...] - delta)
            db_ref[...] = db_ref[...] + ds
            dsk_ref[...] = dsk_ref[...] - jnp.broadcast_to(ps * delta, (SW_STACK, LANES))
            ds_scr[...] = ds.astype(BF16)
            p_scr[...] = p.astype(BF16)
            for g in range(SW_HEADS // SW_REP):
                kl = slice(HEAD_DIM * g, HEAD_DIM * (g + 1))
                k = kp[kr, kl]
                dkw = jnp.zeros((3 * SW_BLOCK, HEAD_DIM), F32)
                dvw = jnp.zeros((3 * SW_BLOCK, HEAD_DIM), F32)
                for r in range(SW_REP):
                    h = g * SW_REP + r
                    hl, rows = slice(HEAD_DIM * h, HEAD_DIM * (h + 1)), slice(SW_BLOCK * h, SW_BLOCK * (h + 1))
                    dsb = ds_scr[rows, :]
                    dq_ref[qr, hl] = _dot(dsb, k)
                    dkw = dkw + _dotg(dsb, q_ref[qr, hl], TN)
                    dvw = dvw + _dotg(p_scr[rows, :], do_ref[qr, hl], TN)
                dkp[kr, kl] = dkp[kr, kl] + dkw
                dvp[kr, kl] = dvp[kr, kl] + dvw
            return carry

        lax.fori_loop(0, nb, blk, 0)
        dk_ref[...] = dkp[SW_BLOCK:SW_BLOCK + s, :]
        dv_ref[...] = dvp[SW_BLOCK:SW_BLOCK + s, :]

    bias_spec = _full((SW_STACK, 3 * SW_BLOCK))
    return pl.pallas_call(
        body, name=name, grid=(1,),
        in_specs=[_full((s, SW_Q_WIDTH)), _full((s, SW_KV_WIDTH)),
                  pl.BlockSpec((s, SW_KV_WIDTH), lambda i: (0, v_blk)),
                  bias_spec, pl.BlockSpec(memory_space=pltpu.SMEM),
                  _full((s, SW_Q_WIDTH)), _full((s, SW_Q_WIDTH))],
        out_specs=[_full((s, SW_Q_WIDTH)), _full((s, SW_KV_WIDTH)), _full((s, SW_KV_WIDTH)), bias_spec,
                   _full((SW_STACK, LANES))],
        out_shape=[jax.ShapeDtypeStruct((s, SW_Q_WIDTH), F32), jax.ShapeDtypeStruct((s, SW_KV_WIDTH), F32),
                   jax.ShapeDtypeStruct((s, SW_KV_WIDTH), F32),
                   jax.ShapeDtypeStruct((SW_STACK, 3 * SW_BLOCK), F32),
                   jax.ShapeDtypeStruct((SW_STACK, LANES), F32)],
        scratch_shapes=[pltpu.VMEM((pad, SW_KV_WIDTH), BF16), pltpu.VMEM((pad, SW_KV_WIDTH), BF16),
                        pltpu.VMEM((pad, SW_KV_WIDTH), F32), pltpu.VMEM((pad, SW_KV_WIDTH), F32),
                        pltpu.VMEM((SW_STACK, 3 * SW_BLOCK), F32), pltpu.VMEM((SW_STACK, 3 * SW_BLOCK), F32),
                        pltpu.VMEM((SW_STACK, 3 * SW_BLOCK), BF16), pltpu.VMEM((SW_STACK, 3 * SW_BLOCK), BF16)],
        compiler_params=_params(),
    )(qs, ks, zq, t5b, sink, o_sw, do_sw)


def merge_out(x, o_na, o_sw, gt, wbna_t, wbsw_t, wout, name):
    s, d = x.shape
    tm = _row_tile(s)

    def body(x_ref, ona_ref, osw_ref, gt_ref, wna_ref, wsw_ref, wo_ref, xo_ref, ana_ref, asw_ref, mg_ref):
        a_na = _dotg(ona_ref[...], wna_ref[...], NT)
        a_sw = _dotg(osw_ref[...], wsw_ref[...], NT)
        ana_ref[...] = a_na.astype(BF16)
        asw_ref[...] = a_sw.astype(BF16)
        merged = (gt_ref[:, 0:d].astype(F32) * a_na + gt_ref[:, d:2 * d].astype(F32) * a_sw).astype(BF16)
        mg_ref[...] = merged
        xo_ref[...] = x_ref[...] + _dot(merged, wo_ref[...])

    return pl.pallas_call(
        body, name=name, grid=(s // tm,),
        in_specs=[_rows(tm, d), _rows(tm, 512), _rows(tm, 512), _rows(tm, 2 * d),
                  _mat(*wbna_t), _mat(*wbsw_t), _mat(*wout)],
        out_specs=[_rows(tm, d)] * 4,
        out_shape=[jax.ShapeDtypeStruct((s, d), F32)] + [jax.ShapeDtypeStruct((s, d), BF16)] * 3,
        compiler_params=_params(),
    )(x, o_na, o_sw, gt, wbna_t[0], wbsw_t[0], wout[0])


def mix_bwd_out(dx, gt, a_na, a_sw, wbna_t, wbsw_t, wout, name):
    s, d = dx.shape
    tm = _row_tile(s)

    def body(dx_ref, gt_ref, ana_ref, asw_ref, wna_ref, wsw_ref, wo_ref,
             dxb_ref, dzg_ref, dana_ref, dasw_ref, dona_ref, dosw_ref, dbg_ref):
        @pl.when(pl.program_id(0) == 0)
        def _():
            dbg_ref[...] = jnp.zeros(dbg_ref.shape, F32)

        dxb = dx_ref[...].astype(BF16)
        dxb_ref[...] = dxb
        dm = _dotg(dxb, wo_ref[...], NT)
        for i, (a_ref, da_ref, w_ref, do_ref) in enumerate(
                [(ana_ref, dana_ref, wna_ref, dona_ref), (asw_ref, dasw_ref, wsw_ref, dosw_ref)]):
            gi = gt_ref[:, i * d:(i + 1) * d].astype(F32)
            da = (dm * gi).astype(BF16)
            da_ref[...] = da
            do_ref[...] = _dot(da, w_ref[...]).astype(BF16)
            dzg = dm * a_ref[...].astype(F32) * gi * (1.0 - gi)
            dzg_ref[:, i * d:(i + 1) * d] = dzg.astype(BF16)
            dbg_ref[:, i * d:(i + 1) * d] = dbg_ref[:, i * d:(i + 1) * d] + jnp.sum(dzg, axis=0, keepdims=True)

    return pl.pallas_call(
        body, name=name, grid=(s // tm,),
        in_specs=[_rows(tm, d), _rows(tm, 2 * d), _rows(tm, d), _rows(tm, d),
                  _mat(*wbna_t), _mat(*wbsw_t), _mat(*wout)],
        out_specs=[_rows(tm, d), _rows(tm, 2 * d), _rows(tm, d), _rows(tm, d), _rows(tm, 512), _rows(tm, 512),
                   _full((1, 2 * d))],
        out_shape=[jax.ShapeDtypeStruct((s, d), BF16), jax.ShapeDtypeStruct((s, 2 * d), BF16),
                   jax.ShapeDtypeStruct((s, d), BF16), jax.ShapeDtypeStruct((s, d), BF16),
                   jax.ShapeDtypeStruct((s, 512), BF16), jax.ShapeDtypeStruct((s, 512), BF16),
                   jax.ShapeDtypeStruct((1, 2 * d), F32)],
        compiler_params=_params(),
    )(dx, gt, a_na, a_sw, wbna_t[0], wbsw_t[0], wout[0])


def qk_norm_bwd(dqa, dka, dva, dqs, dks, dvs, zq, dzg, gq_na, gk_na, gq_sw, gk_sw, bd, name):
    s = zq.shape[0]
    d2 = dzg.shape[1]
    n_in = QKV_WIDTH + d2
    tm = _row_tile(s)

    def body(dqa_ref, dka_ref, dva_ref, dqs_ref, dks_ref, dvs_ref, zq_ref, dzg_ref,
             gqa_ref, gka_ref, gqs_ref, gks_ref, bd_ref, dz_ref, dgqa_ref, dgka_ref, dgqs_ref, dgks_ref):
        @pl.when(pl.program_id(0) == 0)
        def _():
            for r in (dgqa_ref, dgka_ref, dgqs_ref, dgks_ref):
                r[...] = jnp.zeros(r.shape, F32)

        bd512 = bd_ref[...]
        bd128 = bd_ref[0:SW_KV_WIDTH, 0:SW_KV_WIDTH]

        def one(c0, c1, dy_ref, g_ref, dg_ref, bdm, scale):
            z = zq_ref[:, c0:c1].astype(F32)
            r = lax.rsqrt(_group_mean(z * z, bdm) + EPS)
            zh = z * r
            dy = dy_ref[...] * scale
            dyg = dy * g_ref[...]
            dz = r * (dyg - zh * _group_mean(dyg * zh, bdm))
            dz_ref[:, c0:c1] = dz.astype(BF16)
            dg_ref[...] = dg_ref[...] + jnp.sum(dy * zh, axis=0, keepdims=True)

        one(0, 512, dqa_ref, gqa_ref, dgqa_ref, bd512, SCALE)
        one(512, 1024, dka_ref, gka_ref, dgka_ref, bd512, 1.0)
        dz_ref[:, 1024:1536] = dva_ref[...].astype(BF16)
        one(1536, 2048, dqs_ref, gqs_ref, dgqs_ref, bd512, SCALE)
        one(2048, 2176, dks_ref, gks_ref, dgks_ref, bd128, 1.0)
        dz_ref[:, 2176:2304] = dvs_ref[...].astype(BF16)
        dz_ref[:, QKV_WIDTH:n_in] = dzg_ref[...]

    return pl.pallas_call(
        body, name=name, grid=(s // tm,),
        in_specs=[_rows(tm, 512), _rows(tm, 512), _rows(tm, 512), _rows(tm, 512), _rows(tm, 128), _rows(tm, 128),
                  _rows(tm, QKV_WIDTH), _rows(tm, d2),
                  _full((1, 512)), _full((1, 512)), _full((1, 512)), _full((1, 128)), _full((512, 512))],
        out_specs=[_rows(tm, n_in), _full((1, 512)), _full((1, 512)), _full((1, 512)), _full((1, 128))],
        out_shape=[jax.ShapeDtypeStruct((s, n_in), BF16)] + [jax.ShapeDtypeStruct((1, 512), F32)] * 3
                  + [jax.ShapeDtypeStruct((1, 128), F32)],
        compiler_params=_params(),
    )(dqa, dka, dva, dqs, dks, dvs, zq, dzg, gq_na, gk_na, gq_sw, gk_sw, bd)


def ffn_bwd_act(dx, wd, hg, hu, name):
    s, d = dx.shape
    f = wd[0].shape[1]
    tm = _row_tile(s)
    fc = _col_chunk(f)

    def body(dx_ref, w_ref, hg_ref, hu_ref, dxb_ref, dhg_ref, dhu_ref):
        dxb = dx_ref[...].astype(BF16)
        dxb_ref[...] = dxb
        for c0 in range(0, f, fc):
            dact = 0.5 * _dotg(dxb, w_ref[c0:c0 + fc, :], NT)
            hg = hg_ref[:, c0:c0 + fc].astype(F32)
            hu = hu_ref[:, c0:c0 + fc].astype(F32)
            sg = _sigmoid(hg)
            dhu_ref[:, c0:c0 + fc] = (dact * hg * sg).astype(BF16)
            dhg_ref[:, c0:c0 + fc] = (dact * hu * sg * (1.0 + hg * (1.0 - sg))).astype(BF16)

    return pl.pallas_call(
        body, name=name, grid=(s // tm,),
        in_specs=[_rows(tm, d), _mat(*wd), _rows(tm, f), _rows(tm, f)],
        out_specs=[_rows(tm, d), _rows(tm, f), _rows(tm, f)],
        out_shape=[jax.ShapeDtypeStruct((s, d), BF16), jax.ShapeDtypeStruct((s, f), BF16),
                   jax.ShapeDtypeStruct((s, f), BF16)],
        compiler_params=_params(),
    )(dx, wd[0], hg, hu)


def proj_bwd_norm(acts, weights, x, gain, dx, dep, name):
    s, d = x.shape
    tm = _row_tile(s)
    n = len(acts)

    def body(*refs):
        a_refs, w_refs = refs[:n], refs[n:2 * n]
        x_ref, g_ref, dx_ref, _, o_ref, dg_ref = refs[2 * n:]

        @pl.when(pl.program_id(0) == 0)
        def _():
            dg_ref[...] = jnp.zeros(dg_ref.shape, F32)

        dxn = _dot(a_refs[0][...], w_refs[0][...])
        for a_ref, w_ref in zip(a_refs[1:], w_refs[1:]):
            dxn = dxn + _dot(a_ref[...], w_ref[...])
        xv = x_ref[...]
        r = _rstd(xv)
        xh = xv * r
        dxh = dxn * g_ref[...]
        o_ref[...] = dx_ref[...] + r * (dxh - xh * jnp.mean(dxh * xh, axis=-1, keepdims=True))
        dg_ref[...] = dg_ref[...] + jnp.sum(dxn * xh, axis=0, keepdims=True)

    return pl.pallas_call(
        body, name=name, grid=(s // tm,),
        in_specs=[_rows(tm, a.shape[1]) for a in acts] + [_mat(*w) for w in weights]
                 + [_rows(tm, d), _full((1, d)), _rows(tm, d), _full(dep.shape)],
        out_specs=[_rows(tm, d), _full((1, d))],
        out_shape=[jax.ShapeDtypeStruct((s, d), F32), jax.ShapeDtypeStruct((1, d), F32)],
        compiler_params=_params(),
    )(*acts, *[w[0] for w in weights], x, gain, dx, dep)


def tn_matmul(a, b, scale, name):
    s, n = a.shape
    k = b.shape[1]
    tn = _tn_tile(n)

    def body(a_ref, b_ref, o_ref):
        o_ref[...] = (scale * _dotg(a_ref[...], b_ref[...], TN)).astype(BF16)

    return pl.pallas_call(
        body, name=name, grid=(n // tn,),
        in_specs=[pl.BlockSpec((s, tn), lambda i: (0, i)),
                  pl.BlockSpec((s, k), lambda i: (0, 0), pipeline_mode=ONCE)],
        out_specs=pl.BlockSpec((tn, k), lambda i: (i, 0)),
        out_shape=jax.ShapeDtypeStruct((n, k), BF16),
        compiler_params=_params(),
    )(a, b)


def loss_grad(y, target, name):
    s, d = y.shape
    tm = _row_tile(s)

    def body(y_ref, t_ref, dy_ref, acc_ref):
        @pl.when(pl.program_id(0) == 0)
        def _():
            acc_ref[...] = jnp.zeros(acc_ref.shape, F32)

        err = y_ref[...] - t_ref[...]
        dy_ref[...] = err * (1.0 / d)
        e2 = err * err
        part = jnp.sum(e2.reshape(tm // 8, 8, d), axis=0)
        acc = part[:, 0:LANES]
        for c0 in range(LANES, d, LANES):
            acc = acc + part[:, c0:c0 + LANES]
        acc_ref[...] = acc_ref[...] + acc

    return pl.pallas_call(
        body, name=name, grid=(s // tm,),
        in_specs=[_rows(tm, d), _rows(tm, d)],
        out_specs=[_rows(tm, d), _full((8, LANES))],
        out_shape=[jax.ShapeDtypeStruct((s, d), F32), jax.ShapeDtypeStruct((8, LANES), F32)],
        compiler_params=_params(),
    )(y, target)


def _mesh_pos():
    return lax.axis_index("x"), lax.axis_index("y"), lax.axis_index("c")


def _peers():
    x, y, c = _mesh_pos()
    peers = []
    for rel in range(1, N_DEV):
        peers.append((1 - x if rel & 4 else x, 1 - y if rel & 2 else y, 1 - c if rel & 1 else c))
    return 4 * x + 2 * y + c, peers


HBM_SPEC = pl.BlockSpec(memory_space=pltpu.HBM)
SEM_SPEC = pl.BlockSpec(memory_space=pltpu.SEMAPHORE)


def _split_call(body, name, thru, n_sems, extra=(), with_token=True):
    hbm = lambda t: pltpu.with_memory_space_constraint(t, pltpu.HBM)
    effect = pltpu.CompilerParams(has_side_effects=pltpu.SideEffectType.DATAFLOW_SIDE_EFFECTING)
    nt = len(thru)
    thru_shapes = [pltpu.HBM(t.shape, t.dtype) for t in thru]
    if with_token:
        (after,) = extra
        outs = pl.pallas_call(
            body, name=name, in_specs=[HBM_SPEC] * nt + [pl.BlockSpec(memory_space=pl.ANY)],
            out_specs=[SEM_SPEC] * len(n_sems) + [HBM_SPEC] * nt + [pl.BlockSpec(memory_space=pltpu.VMEM)],
            out_shape=[pltpu.SemaphoreType.DMA((k,)) for k in n_sems] + thru_shapes
                      + [jax.ShapeDtypeStruct((8, LANES), F32)],
            input_output_aliases={i: len(n_sems) + i for i in range(nt)}, compiler_params=effect,
        )(*[hbm(t) for t in thru], after)
        return outs[:len(n_sems)], outs[len(n_sems):-1], outs[-1]
    return pl.pallas_call(
        body, name=name,
        in_specs=[HBM_SPEC] * nt + [SEM_SPEC] * len(n_sems) + [pl.BlockSpec(memory_space=pl.ANY)],
        out_specs=[HBM_SPEC] * nt, out_shape=thru_shapes,
        input_output_aliases={i: i for i in range(nt)}, compiler_params=effect,
    )(*thru, *extra)


def _gather_targets():
    x, y, c = _mesh_pos()
    return 4 * x + 2 * y + c, [(x, y, 1 - c), (1 - x, y, c), (x, 1 - y, c), (1 - x, 1 - y, c)]


def gather_start(shards, after, name):
    n = len(shards)
    zones = [lax.empty((w.shape[0], N_DEV) + w.shape[1:], w.dtype) for w in shards]

    def body(*refs):
        ins, zs = refs[:n], refs[n:2 * n]
        send_sems, recv_sems, local_sems = refs[2 * n + 1:2 * n + 4]
        token = refs[-1]
        me, targets = _gather_targets()
        for a in range(n):
            pltpu.make_async_copy(ins[a], zs[a].at[:, me], local_sems.at[a]).start()
            for k, to in enumerate(targets):
                pltpu.make_async_remote_copy(
                    src_ref=ins[a], dst_ref=zs[a].at[:, me], send_sem=send_sems.at[4 * a + k],
                    recv_sem=recv_sems.at[4 * a + k], device_id=to, device_id_type=MESH).start()
        token[...] = jnp.zeros(token.shape, F32)

    sems, thru, token = _split_call(body, name, list(shards) + zones, (4 * n, 4 * n, n), extra=(after,))
    return (sems, thru, n), token


def gather_wait(started, after, name):
    sems, thru, n = started

    def body(*refs):
        zs = refs[n:2 * n]
        send_sems, recv_sems, local_sems = refs[2 * n:2 * n + 3]
        _, targets = _gather_targets()
        for a in range(n):
            for k, to in enumerate(targets):
                cp = pltpu.make_async_remote_copy(
                    src_ref=zs[a].at[:, 0], dst_ref=zs[a].at[:, 0], send_sem=send_sems.at[4 * a + k],
                    recv_sem=recv_sems.at[4 * a + k], device_id=to, device_id_type=MESH)
                cp.wait_send()
                cp.wait_recv()
            pltpu.make_async_copy(zs[a].at[:, 0], zs[a].at[:, 0], local_sems.at[a]).wait()

    return _split_call(body, name, thru, (4 * n, 4 * n, n), extra=(*sems, after), with_token=False)[n:]


def forward_start(zones, after, name):
    n = len(zones)

    def body(*refs):
        zs = refs[:n]
        send_sems, recv_sems = refs[n + 1:n + 3]
        token = refs[-1]
        x, y, c = _mesh_pos()
        for a in range(n):
            for j, chip in enumerate([(1 - x, y), (x, 1 - y), (1 - x, 1 - y)]):
                blk = zs[a].at[:, 4 * chip[0] + 2 * chip[1] + c]
                pltpu.make_async_remote_copy(
                    src_ref=blk, dst_ref=blk, send_sem=send_sems.at[3 * a + j], recv_sem=recv_sems.at[3 * a + j],
                    device_id=(x, y, 1 - c), device_id_type=MESH).start()
        token[...] = jnp.zeros(token.shape, F32)

    sems, thru, token = _split_call(body, name, list(zones), (3 * n, 3 * n), extra=(after,))
    return (sems, thru, n), token


def forward_wait(started, after, name):
    sems, thru, n = started

    def body(*refs):
        zs = refs[:n]
        send_sems, recv_sems = refs[n:n + 2]
        x, y, c = _mesh_pos()
        for a in range(n):
            for j in range(3):
                cp = pltpu.make_async_remote_copy(
                    src_ref=zs[a].at[:, 0], dst_ref=zs[a].at[:, 0], send_sem=send_sems.at[3 * a + j],
                    recv_sem=recv_sems.at[3 * a + j], device_id=(x, y, 1 - c), device_id_type=MESH)
                cp.wait_send()
                cp.wait_recv()

    return _split_call(body, name, thru, (3 * n, 3 * n), extra=(*sems, after), with_token=False)


def scatter_start(groups, name):
    n = len(groups)
    flat = [g for grp in groups for g in grp]
    nf = len(flat)
    offs = np.cumsum([0] + [len(grp) for grp in groups])
    lands = [lax.empty((N_DEV, len(grp)) + grp[0].shape[1:], grp[0].dtype) for grp in groups]

    def body(*refs):
        ins, zones = refs[:nf], refs[nf:nf + n]
        send_sems, recv_sems, local_sems = refs[nf + n:nf + n + 3]
        token = refs[-1]
        me, peers = _peers()
        for a in range(n):
            for w in range(len(groups[a])):
                pltpu.make_async_copy(ins[offs[a] + w].at[me], zones[a].at[me, w], local_sems.at[a]).start()
        for k, peer in enumerate(peers):
            p_id = 4 * peer[0] + 2 * peer[1] + peer[2]
            for a in range(n):
                for w in range(len(groups[a])):
                    pltpu.make_async_remote_copy(
                        src_ref=ins[offs[a] + w].at[p_id], dst_ref=zones[a].at[me, w],
                        send_sem=send_sems.at[7 * a + k], recv_sem=recv_sems.at[7 * a + k],
                        device_id=peer, device_id_type=MESH).start()
        token[...] = jnp.zeros(token.shape, F32)

    hbm = lambda t: pltpu.with_memory_space_constraint(t, pltpu.HBM)
    outs = pl.pallas_call(
        body, name=name,
        in_specs=[HBM_SPEC] * (nf + n),
        out_specs=[SEM_SPEC] * 3 + [HBM_SPEC] * (nf + n) + [pl.BlockSpec(memory_space=pltpu.VMEM)],
        out_shape=[pltpu.SemaphoreType.DMA((7 * n,)), pltpu.SemaphoreType.DMA((7 * n,)), pltpu.SemaphoreType.DMA((n,))]
                  + [pltpu.HBM(t.shape, t.dtype) for t in flat + lands]
                  + [jax.ShapeDtypeStruct((8, LANES), F32)],
        input_output_aliases={i: 3 + i for i in range(nf + n)},
        compiler_params=pltpu.CompilerParams(has_side_effects=pltpu.SideEffectType.DATAFLOW_SIDE_EFFECTING),
    )(*[hbm(t) for t in flat], *[hbm(t) for t in lands])
    sems, thru, token = outs[:3], outs[3:3 + nf + n], outs[-1]
    return (sems, thru, [len(grp) for grp in groups]), token


def scatter_wait(started, after, name):
    (send_sems, recv_sems, local_sems), thru, sizes = started
    n = len(sizes)
    nf = len(thru) - n

    def body(*refs):
        zones = refs[nf:nf + n]
        s_sems, r_sems, l_sems = refs[nf + n:nf + n + 3]
        me, peers = _peers()
        for a in range(n):
            for k, peer in enumerate(peers):
                cp = pltpu.make_async_remote_copy(
                    src_ref=zones[a].at[0], dst_ref=zones[a].at[0],
                    send_sem=s_sems.at[7 * a + k], recv_sem=r_sems.at[7 * a + k], device_id=peer,
                    device_id_type=MESH)
                cp.wait_send()
                cp.wait_recv()
            pltpu.make_async_copy(zones[a].at[0], zones[a].at[0], l_sems.at[a]).wait()

    outs = pl.pallas_call(
        body, name=name,
        in_specs=[HBM_SPEC] * (nf + n) + [SEM_SPEC] * 3 + [pl.BlockSpec(memory_space=pl.ANY)],
        out_specs=[HBM_SPEC] * (nf + n),
        out_shape=[pltpu.HBM(t.shape, t.dtype) for t in thru],
        input_output_aliases={i: i for i in range(nf + n)},
        compiler_params=pltpu.CompilerParams(has_side_effects=pltpu.SideEffectType.DATAFLOW_SIDE_EFFECTING),
    )(*thru, send_sems, recv_sems, local_sems, after)
    return outs[nf:]


def share_small(small, after):
    def body(s_ref, after_ref, o_ref, send_sems, recv_sems, local_sem):
        me, peers = _peers()
        mine = pltpu.make_async_copy(s_ref, o_ref.at[me], local_sem)
        mine.start()
        copies = [pltpu.make_async_remote_copy(src_ref=s_ref, dst_ref=o_ref.at[me], send_sem=send_sems.at[k],
                                               recv_sem=recv_sems.at[k], device_id=peer, device_id_type=MESH)
                  for k, peer in enumerate(peers)]
        for cp in copies:
            cp.start()
        for cp in copies:
            cp.wait()
        mine.wait()

    vm = pl.BlockSpec(memory_space=pltpu.VMEM)
    return pl.pallas_call(
        body, name="share_small", in_specs=[vm, pl.BlockSpec(memory_space=pl.ANY)], out_specs=vm,
        out_shape=jax.ShapeDtypeStruct((N_DEV,) + small.shape, small.dtype),
        scratch_shapes=[pltpu.SemaphoreType.DMA((7,)), pltpu.SemaphoreType.DMA((7,)), pltpu.SemaphoreType.DMA],
    )(small, after)


def sum_sources(recv, name):
    _, w, r, c = recv.shape

    def body(r_ref, o_ref):
        acc = r_ref[0, 0].astype(F32)
        for src in range(1, N_DEV):
            acc = acc + r_ref[src, 0].astype(F32)
        o_ref[0] = acc

    return pl.pallas_call(
        body, name=name, grid=(w,),
        in_specs=[pl.BlockSpec((N_DEV, 1, r, c), lambda i: (0, i, 0, 0))],
        out_specs=pl.BlockSpec((1, r, c), lambda i: (i, 0, 0)),
        out_shape=jax.ShapeDtypeStruct((w, r, c), F32),
        compiler_params=_params(),
    )(recv)


def _adamw_math(w, g, m, v):
    m = ADAM_B1 * m + (1.0 - ADAM_B1) * g
    v = ADAM_B2 * v + (1.0 - ADAM_B2) * (g * g)
    m_hat = m / (1.0 - ADAM_B1 ** ADAM_STEP)
    v_hat = v / (1.0 - ADAM_B2 ** ADAM_STEP)
    delta = -ADAM_LR * (m_hat / (jnp.sqrt(v_hat) + ADAM_EPS) + ADAM_WD * w)
    return delta, m, v


def adamw(w, g, m, v, name):
    shape = w.shape
    c = shape[-1]
    r = int(np.prod(shape[:-1]))
    w2, g2, m2, v2 = (t.reshape(r, c) for t in (w, g, m, v))
    tr = next(t for t in range(min(r, 512), 0, -1) if r % t == 0 and (t % 8 == 0 or t == r))

    def body(w_ref, g_ref, m_ref, v_ref, d_ref, mo_ref, vo_ref):
        d_ref[...], mo_ref[...], vo_ref[...] = _adamw_math(w_ref[...], g_ref[...], m_ref[...], v_ref[...])

    spec = pl.BlockSpec((tr, c), lambda i: (i, 0))
    outs = pl.pallas_call(
        body, name=name, grid=(r // tr,),
        in_specs=[spec] * 4, out_specs=[spec] * 3,
        out_shape=[jax.ShapeDtypeStruct((r, c), F32)] * 3,
        compiler_params=_params(),
    )(w2, g2, m2, v2)
    return tuple(t.reshape(shape) for t in outs)


def adamw_small(w, recv, m, v, name):
    def body(w_ref, r_ref, m_ref, v_ref, g_ref, d_ref, mo_ref, vo_ref):
        g = r_ref[0]
        for src in range(1, N_DEV):
            g = g + r_ref[src]
        g_ref[...] = g
        d_ref[...], mo_ref[...], vo_ref[...] = _adamw_math(w_ref[...], g, m_ref[...], v_ref[...])

    vm = pl.BlockSpec(memory_space=pltpu.VMEM)
    return pl.pallas_call(
        body, name=name, in_specs=[vm] * 4, out_specs=[vm] * 4,
        out_shape=[jax.ShapeDtypeStruct(w.shape, F32)] * 4,
        compiler_params=pltpu.CompilerParams(vmem_limit_bytes=V7X_VMEM_LIMIT),
    )(w, recv, m, v)


SMALL_NAMES = ("ffn1_norm", "mix_norm", "ffn2_norm", "b_gate", "na_q_norm", "na_k_norm", "sw_q_norm", "sw_k_norm",
               "na_rpb", "sw_sink", "t5_rel_table")


def _pack_small(parts):
    flat = jnp.concatenate([parts[k].reshape(-1).astype(F32) for k in SMALL_NAMES])
    n = flat.shape[0]
    rows = -(-n // (8 * LANES)) * 8
    return jnp.pad(flat, (0, rows * LANES - n)).reshape(rows, LANES)


def _unpack_small(packed, like):
    flat = packed.reshape(-1)
    out, off = {}, 0
    for k in SMALL_NAMES:
        n = int(np.prod(like[k].shape))
        out[k] = flat[off:off + n].reshape(like[k].shape)
        off += n
    return out


def kernel(x, ffn1_norm, ffn1_w_gate, ffn1_w_up, ffn1_w_down, mix_norm, w_in, b_gate, na_q_norm, na_k_norm, na_rpb, sw_q_norm, sw_k_norm, sw_sink, t5_rel_table, w_branch_na, w_branch_sw, w_out, ffn2_norm, ffn2_w_gate, ffn2_w_up, ffn2_w_down, loss_target, m_ffn1_norm, m_ffn1_w_gate, m_ffn1_w_up, m_ffn1_w_down, m_mix_norm, m_w_in, m_b_gate, m_na_q_norm, m_na_k_norm, m_na_rpb, m_sw_q_norm, m_sw_k_norm, m_sw_sink, m_t5_rel_table, m_w_branch_na, m_w_branch_sw, m_w_out, m_ffn2_norm, m_ffn2_w_gate, m_ffn2_w_up, m_ffn2_w_down, v_ffn1_norm, v_ffn1_w_gate, v_ffn1_w_up, v_ffn1_w_down, v_mix_norm, v_w_in, v_b_gate, v_na_q_norm, v_na_k_norm, v_na_rpb, v_sw_q_norm, v_sw_k_norm, v_sw_sink, v_t5_rel_table, v_w_branch_na, v_w_branch_sw, v_w_out, v_ffn2_norm, v_ffn2_w_gate, v_ffn2_w_up, v_ffn2_w_down):
    weights = dict(ffn1_norm=ffn1_norm, ffn1_w_gate=ffn1_w_gate, ffn1_w_up=ffn1_w_up, ffn1_w_down=ffn1_w_down,
                   mix_norm=mix_norm, w_in=w_in, b_gate=b_gate, na_q_norm=na_q_norm, na_k_norm=na_k_norm,
                   na_rpb=na_rpb, sw_q_norm=sw_q_norm, sw_k_norm=sw_k_norm, sw_sink=sw_sink,
                   t5_rel_table=t5_rel_table, w_branch_na=w_branch_na, w_branch_sw=w_branch_sw, w_out=w_out,
                   ffn2_norm=ffn2_norm, ffn2_w_gate=ffn2_w_gate, ffn2_w_up=ffn2_w_up, ffn2_w_down=ffn2_w_down)
    mom_m = dict(ffn1_norm=m_ffn1_norm, ffn1_w_gate=m_ffn1_w_gate, ffn1_w_up=m_ffn1_w_up, ffn1_w_down=m_ffn1_w_down,
                 mix_norm=m_mix_norm, w_in=m_w_in, b_gate=m_b_gate, na_q_norm=m_na_q_norm, na_k_norm=m_na_k_norm,
                 na_rpb=m_na_rpb, sw_q_norm=m_sw_q_norm, sw_k_norm=m_sw_k_norm, sw_sink=m_sw_sink,
                 t5_rel_table=m_t5_rel_table, w_branch_na=m_w_branch_na, w_branch_sw=m_w_branch_sw, w_out=m_w_out,
                 ffn2_norm=m_ffn2_norm, ffn2_w_gate=m_ffn2_w_gate, ffn2_w_up=m_ffn2_w_up, ffn2_w_down=m_ffn2_w_down)
    mom_v = dict(ffn1_norm=v_ffn1_norm, ffn1_w_gate=v_ffn1_w_gate, ffn1_w_up=v_ffn1_w_up, ffn1_w_down=v_ffn1_w_down,
                 mix_norm=v_mix_norm, w_in=v_w_in, b_gate=v_b_gate, na_q_norm=v_na_q_norm, na_k_norm=v_na_k_norm,
                 na_rpb=v_na_rpb, sw_q_norm=v_sw_q_norm, sw_k_norm=v_sw_k_norm, sw_sink=v_sw_sink,
                 t5_rel_table=v_t5_rel_table, w_branch_na=v_w_branch_na, w_branch_sw=v_w_branch_sw, w_out=v_w_out,
                 ffn2_norm=v_ffn2_norm, ffn2_w_gate=v_ffn2_w_gate, ffn2_w_up=v_ffn2_w_up, ffn2_w_down=v_ffn2_w_down)
    order = list(weights)

    depth = ffn1_norm.shape[0]
    s, d = x.shape[1], x.shape[2]
    xs = x[0]
    tr = lambda w: jnp.swapaxes(w, -1, -2)

    merge = lambda t: t.reshape(t.shape[0], N_DEV * t.shape[2], t.shape[3])
    no_dep = jnp.zeros((8, LANES), F32)

    def shards_of(kind, l):
        stack = lambda *ws: jnp.stack(ws).astype(BF16)
        if kind == "ffn1":
            return [stack(tr(ffn1_w_gate[l]), tr(ffn1_w_up[l]), ffn1_w_down[l])]
        if kind == "win":
            return [stack(tr(w_in[l]))]
        return [stack(tr(ffn2_w_gate[l]), tr(ffn2_w_up[l]), ffn2_w_down[l]), stack(w_out[l]),
                stack(tr(w_branch_na[l]), tr(w_branch_sw[l]))]

    def start(kind, l, after):
        return gather_start(shards_of(kind, l), after, f"gather_{kind}_{l}")

    def arrive(started, kind, l, after):
        zones = gather_wait(started, after, f"gather_{kind}_{l}_wait")
        return forward_start(zones, no_dep, f"forward_{kind}_{l}")

    def finish(fwd, kind, l, after):
        return [merge(z) for z in forward_wait(fwd, after, f"forward_{kind}_{l}_wait")]

    bd = jnp.asarray(np.kron(np.eye(NA_WIDTH // HEAD_DIM), np.full((HEAD_DIM, HEAD_DIM), 1.0 / HEAD_DIM)), BF16)
    bmap = jnp.asarray(_t5_bucket_map())
    tile8 = lambda g: jnp.tile(g, NA_WIDTH // HEAD_DIM).reshape(1, NA_WIDTH)
    tile2 = lambda g: jnp.tile(g, SW_KV_WIDTH // HEAD_DIM).reshape(1, SW_KV_WIDTH)

    st_first, tok = start("ffn1", 0, no_dep)
    t5b = t5_expand(t5_rel_table, bmap, tok, "t5_expand").reshape(SW_STACK, 3 * SW_BLOCK)
    fwd, _ = arrive(st_first, "ffn1", 0, t5b)
    st_win, dep = start("win", 0, t5b)
    (first,) = finish(fwd, "ffn1", 0, dep)

    saved = []
    layer_w = {0: dict(wg1=(first, 0), wu1=(first, 1), wd1=(first, 2))}
    cur = xs
    for l in range(depth):
        sv = {}
        lw = layer_w[l]
        sv["x0"] = cur
        sv["xn1"], sv["hg1"], sv["hu1"], sv["act1"] = ffn_up(cur, ffn1_norm[l][None], lw["wg1"], lw["wu1"], dep,
                                                             f"ffn1_up_{l}")
        cur = ffn_down(cur, sv["act1"], lw["wd1"], no_dep, f"ffn1_down_{l}")
        sv["x1"] = cur
        fwd, _ = arrive(st_win, "win", l, cur)
        st_rest, tok = start("rest", l, cur)
        (zb,) = finish(fwd, "win", l, tok)
        lw["win"] = (zb, 0)
        sv["gains"] = (tile8(na_q_norm[l]), tile8(na_k_norm[l]), tile8(sw_q_norm[l]), tile2(sw_k_norm[l]))
        sv["hn"], sv["zq"], sv["qa"], sv["ka"], sv["qs"], sv["ks"], sv["gt"] = mix_in(
            cur, mix_norm[l][None], lw["win"], b_gate[l][None], *sv["gains"], bd, f"mix_in_{l}")
        sv["t2"] = rpb_expand(_rpb_rows(na_rpb[l]), no_dep, f"rpb_expand_{l}")
        sv["o_na"] = na_fwd(sv["qa"], sv["ka"], sv["zq"], sv["t2"], f"na_fwd_{l}")
        dep = no_dep
        if l + 1 < depth:
            st_ffn1, dep = start("ffn1", l + 1, sv["o_na"])
        sv["o_sw"] = sw_fwd(sv["qs"], sv["ks"], sv["zq"], t5b, sw_sink[l], dep, f"sw_fwd_{l}")
        fwd, tok = arrive(st_rest, "rest", l, sv["o_sw"])
        za, zc, zd = finish(fwd, "rest", l, tok)
        lw.update(wg2=(za, 0), wu2=(za, 1), wd2=(za, 2), wout=(zc, 0), wna=(zd, 0), wsw=(zd, 1))
        cur, sv["a_na"], sv["a_sw"], sv["merged"] = merge_out(
            cur, sv["o_na"], sv["o_sw"], sv["gt"], lw["wna"], lw["wsw"], lw["wout"], f"merge_out_{l}")
        sv["x2"] = cur
        dep = no_dep
        if l + 1 < depth:
            st_win, dep = start("win", l + 1, cur)
        sv["xn2"], sv["hg2"], sv["hu2"], sv["act2"] = ffn_up(cur, ffn2_norm[l][None], lw["wg2"], lw["wu2"], dep,
                                                             f"ffn2_up_{l}")
        dep = no_dep
        if l + 1 < depth:
            fwd, dep = arrive(st_ffn1, "ffn1", l + 1, sv["act2"])
        cur = ffn_down(cur, sv["act2"], lw["wd2"], dep, f"ffn2_down_{l}")
        dep = no_dep
        if l + 1 < depth:
            (za,) = finish(fwd, "ffn1", l + 1, cur)
            layer_w[l + 1] = dict(wg1=(za, 0), wu1=(za, 1), wd1=(za, 2))
        saved.append(sv)

    dx, loss_acc = loss_grad(cur, loss_target[0], "loss_grad")
    loss = lax.psum(jnp.sum(loss_acc) * (0.5 / d), ("x", "y", "c"))

    split = lambda t: t.reshape(N_DEV, t.shape[0] // N_DEV, t.shape[1])
    pending = {}
    small = {k: [None] * depth for k in SMALL_NAMES if k != "t5_rel_table"}
    dbias_sw = []
    for l in reversed(range(depth)):
        sv = saved[l]
        lw = layer_w[l]
        wg1, wu1, wd1, wg2, wu2, wd2 = (lw[k] for k in ("wg1", "wu1", "wd1", "wg2", "wu2", "wd2"))
        win_t, wout_l, wna_t, wsw_t = lw["win"], lw["wout"], lw["wna"], lw["wsw"]
        blocks = ((2, "x2", "xn2", "hg2", "hu2", "act2", wg2, wu2, wd2, "ffn2_norm", 3),
                  (1, "x0", "xn1", "hg1", "hu1", "act1", wg1, wu1, wd1, "ffn1_norm", 0))

        def ffn_backward(dx, blk):
            tag, xk, xnk, hgk, huk, actk, wg, wu, wd, norm_name, slot = blk
            gains = weights[norm_name]
            dxb, dhg, dhu = ffn_bwd_act(dx, wd, sv[hgk], sv[huk], f"ffn{tag}_bwd_act_{l}")
            gwd = tn_matmul(sv[actk], dxb, 0.5, f"ffn{tag}_dwd_{l}")
            gwg = tn_matmul(dhg, sv[xnk], 1.0, f"ffn{tag}_dwg_{l}")
            gwu = tn_matmul(dhu, sv[xnk], 1.0, f"ffn{tag}_dwu_{l}")
            pending[f"ffn{tag}_{l}"], token = scatter_start([[split(gwg), split(gwu), split(gwd)]],
                                                            f"scatter_ffn{tag}_{l}")
            dx, dg = proj_bwd_norm([dhg, dhu], [wg, wu], sv[xk], gains[l][None], dx, token, f"ffn{tag}_bwd_x_{l}")
            small[norm_name][l] = dg[0]
            return dx

        dx = ffn_backward(dx, blocks[0])
        dxb, dzg, da_na, da_sw, do_na, do_sw, dbg = mix_bwd_out(
            dx, sv["gt"], sv["a_na"], sv["a_sw"], wna_t, wsw_t, wout_l, f"mix_bwd_out_{l}")
        small["b_gate"][l] = dbg[0]
        gwout = tn_matmul(sv["merged"], dxb, 1.0, f"dwout_{l}")
        gwna = tn_matmul(da_na, sv["o_na"], 1.0, f"dwna_{l}")
        gwsw = tn_matmul(da_sw, sv["o_sw"], 1.0, f"dwsw_{l}")
        dqa, dka, dva, dt2 = na_bwd(sv["qa"], sv["ka"], sv["zq"], sv["t2"], sv["o_na"], do_na, f"na_bwd_{l}")
        dqs, dks, dvs, dbias, dsink = sw_bwd(sv["qs"], sv["ks"], sv["zq"], t5b, sw_sink[l], sv["o_sw"], do_sw,
                                             f"sw_bwd_{l}")
        dbias_sw.append(dbias.reshape(SW_HEADS, SW_BLOCK, 3 * SW_BLOCK))
        small["sw_sink"][l] = jnp.sum(dsink[:, 0].reshape(SW_HEADS, SW_BLOCK), axis=1)
        small["na_rpb"][l] = _rpb_from_rows(rpb_reduce(dt2, f"rpb_reduce_{l}"))
        dz, dgqa, dgka, dgqs, dgks = qk_norm_bwd(dqa, dka, dva, dqs, dks, dvs, sv["zq"], dzg, *sv["gains"], bd,
                                                 f"qk_norm_bwd_{l}")
        fold = lambda g: jnp.sum(g.reshape(-1, HEAD_DIM), axis=0)
        small["na_q_norm"][l], small["na_k_norm"][l] = fold(dgqa), fold(dgka)
        small["sw_q_norm"][l], small["sw_k_norm"][l] = fold(dgqs), fold(dgks)
        gwin = tn_matmul(dz, sv["hn"], 1.0, f"dwin_{l}")
        pending[f"mix_{l}"], token = scatter_start([[split(gwout)], [split(gwna), split(gwsw)], [split(gwin)]],
                                                   f"scatter_mix_{l}")
        dx, dg = proj_bwd_norm([dz], [win_t], sv["x1"], mix_norm[l][None], dx, token, f"mix_bwd_x_{l}")
        small["mix_norm"][l] = dg[0]
        dx = ffn_backward(dx, blocks[1])

    dtab = t5_reduce(dbias_sw, bmap, "t5_reduce")
    small_parts = {k: jnp.stack(v) for k, v in small.items()}
    small_parts["t5_rel_table"] = jnp.transpose(dtab[:, :, 0])
    small_packed = _pack_small(small_parts)

    summed = {}
    layers = lambda f: jnp.stack([f(l) for l in range(depth)])
    grads, delta, new_m, new_v = {}, {}, {}, {}

    def collect(key, after):
        zones = scatter_wait(pending[key], after, f"wait_{key}")
        summed[key] = [sum_sources(z, f"sum_{key}_{i}") for i, z in enumerate(zones)]

    def update(k, g, transposed):
        view = tr if transposed else (lambda t: t)
        d_k, m_k, v_k = adamw(view(weights[k]), g, view(mom_m[k]), view(mom_v[k]), f"adamw_{k}")
        grads[k], delta[k], new_m[k], new_v[k] = view(g), view(d_k), view(m_k), view(v_k)
        return d_k

    last_key = "ffn1_0"
    for key in pending:
        if key != last_key:
            collect(key, dx)
    update("ffn2_w_gate", layers(lambda l: summed[f"ffn2_{l}"][0][0]), True)
    update("ffn2_w_up", layers(lambda l: summed[f"ffn2_{l}"][0][1]), True)
    update("ffn2_w_down", layers(lambda l: summed[f"ffn2_{l}"][0][2]), False)
    update("w_out", layers(lambda l: summed[f"mix_{l}"][0][0]), False)
    update("w_branch_na", layers(lambda l: summed[f"mix_{l}"][1][0]), True)
    update("w_branch_sw", layers(lambda l: summed[f"mix_{l}"][1][1]), True)
    done = update("w_in", layers(lambda l: summed[f"mix_{l}"][2][0]), True)
    collect(last_key, done)
    update("ffn1_w_gate", layers(lambda l: summed[f"ffn1_{l}"][0][0]), True)
    update("ffn1_w_up", layers(lambda l: summed[f"ffn1_{l}"][0][1]), True)
    done = update("ffn1_w_down", layers(lambda l: summed[f"ffn1_{l}"][0][2]), False)
    rs_ = share_small(small_packed, done)
    g_s, d_s, m_s, v_s = adamw_small(_pack_small(weights), rs_, _pack_small(mom_m), _pack_small(mom_v), "adamw_small")
    for dst, packed in ((grads, g_s), (delta, d_s), (new_m, m_s), (new_v, v_s)):
        dst.update(_unpack_small(packed, weights))

    return (loss, dx[None], *[grads[k] for k in order], *[delta[k] for k in order],
            *[new_m[k] for k in order], *[new_v[k] for k in order])
```

```python
import functools
import math

import numpy as np
import jax
import jax.numpy as jnp
from jax import lax
from jax.experimental import pallas as pl
from jax.experimental.pallas import tpu as pltpu

F32 = jnp.float32
BF16 = jnp.bfloat16
MESH = pl.DeviceIdType.MESH

N_DEV = 8
EPS = 1e-6
NEG = -1e30
HEAD_DIM = 64
GRID_W = 64
NA_ROWS = 8
NA_COLS = 16
NA_WIDTH = 512
SW_Q_WIDTH = 512
SW_KV_WIDTH = 128
SW_BLOCK = 128
SW_HEADS = 8
SW_REP = 4
REL_BUCKETS = 32
REL_MAX_DIST = 128
QKV_WIDTH = 3 * NA_WIDTH + SW_Q_WIDTH + 2 * SW_KV_WIDTH
SCALE = 1.0 / math.sqrt(HEAD_DIM)

ADAM_LR = 0.001
ADAM_B1 = 0.9
ADAM_B2 = 0.999
ADAM_EPS = 1e-08
ADAM_WD = 0.01
ADAM_STEP = 10

V7X_VMEM_LIMIT = 56 * 1024 * 1024
LANES = 128
MXU_TILE = 256

NT = (((1,), (1,)), ((), ()))
TN = (((0,), (0,)), ((), ()))


def _params(n_grid=1):
    return pltpu.CompilerParams(dimension_semantics=("arbitrary",) * n_grid,
                                vmem_limit_bytes=V7X_VMEM_LIMIT)


def _row_tile(s):
    for t in (512, 256, 128, 64, 32, 16, 8):
        if s % t == 0:
            return t
    raise ValueError(s)


def _tn_tile(n):
    best = max(t for t in range(LANES, min(n, 2304) + 1, LANES) if n % t == 0) if n % LANES == 0 else n
    return best // 2 if best == n and n >= 1024 else best


ONCE = pl.Buffered(1)


def _col_chunk(n):
    return MXU_TILE if n % MXU_TILE == 0 else n


def _dot(a, b):
    return jnp.dot(a, b, preferred_element_type=F32)


def _dotg(a, b, dn):
    return lax.dot_general(a, b, dn, preferred_element_type=F32)


def _sigmoid(v):
    return 1.0 / (1.0 + jnp.exp(-v))


def _rstd(xv):
    return lax.rsqrt(jnp.mean(xv * xv, axis=-1, keepdims=True) + EPS)


def _full(shape):
    nd = len(shape)
    return pl.BlockSpec(shape, lambda i, _n=nd: (0,) * _n)


def _rows(tm, width):
    return pl.BlockSpec((tm, width), lambda i: (i, 0))


def _mat(stack, idx):
    return pl.BlockSpec((None,) + tuple(stack.shape[1:]), lambda i, _w=idx: (_w, 0, 0), pipeline_mode=ONCE)


def _group_mean(v, bd):
    hi = v.astype(BF16)
    lo = (v - hi.astype(F32)).astype(BF16)
    return _dot(hi, bd) + _dot(lo, bd)


def ffn_up(x, gain, wg_t, wu_t, dep, name):
    s, d = x.shape
    f = wg_t[0].shape[1]
    tm = _row_tile(s)
    fc = _col_chunk(f)

    def body(x_ref, g_ref, wg_ref, wu_ref, dep_ref, xn_ref, hg_ref, hu_ref, act_ref):
        xv = x_ref[...]
        xn = (xv * _rstd(xv) * g_ref[...]).astype(BF16)
        xn_ref[...] = xn
        for c0 in range(0, f, fc):
            hg = _dotg(xn, wg_ref[c0:c0 + fc, :], NT)
            hu = _dotg(xn, wu_ref[c0:c0 + fc, :], NT)
            hg_ref[:, c0:c0 + fc] = hg.astype(BF16)
            hu_ref[:, c0:c0 + fc] = hu.astype(BF16)
            act_ref[:, c0:c0 + fc] = (hg * _sigmoid(hg) * hu).astype(BF16)

    return pl.pallas_call(
        body, name=name, grid=(s // tm,),
        in_specs=[_rows(tm, d), _full((1, d)), _mat(*wg_t), _mat(*wu_t), _full(dep.shape)],
        out_specs=[_rows(tm, d), _rows(tm, f), _rows(tm, f), _rows(tm, f)],
        out_shape=[jax.ShapeDtypeStruct((s, d), BF16)] + [jax.ShapeDtypeStruct((s, f), BF16)] * 3,
        compiler_params=_params(),
    )(x, gain, wg_t[0], wu_t[0], dep)


def ffn_down(x, act, wd, dep, name):
    s, d = x.shape
    f = act.shape[1]
    tm = _row_tile(s)

    def body(x_ref, a_ref, w_ref, dep_ref, o_ref):
        o_ref[...] = x_ref[...] + 0.5 * _dot(a_ref[...], w_ref[...])

    return pl.pallas_call(
        body, name=name, grid=(s // tm,),
        in_specs=[_rows(tm, d), _rows(tm, f), _mat(*wd), _full(dep.shape)],
        out_specs=_rows(tm, d),
        out_shape=jax.ShapeDtypeStruct((s, d), F32),
        compiler_params=_params(),
    )(x, act, wd[0], dep)


def mix_in(x, gain, win_t, b_gate, gq_na, gk_na, gq_sw, gk_sw, bd, name):
    s, d = x.shape
    tm = _row_tile(s)
    gc = _col_chunk(2 * d)

    def body(x_ref, g_ref, w_ref, b_ref, gqa_ref, gka_ref, gqs_ref, gks_ref, bd_ref,
             hn_ref, zq_ref, qa_ref, ka_ref, qs_ref, ks_ref, gt_ref):
        xv = x_ref[...]
        hn = (xv * _rstd(xv) * g_ref[...]).astype(BF16)
        hn_ref[...] = hn

        def proj(c0, c1):
            return _dotg(hn, w_ref[c0:c1, :], NT)

        def headnorm(z, g, bdm):
            return z * lax.rsqrt(_group_mean(z * z, bdm) + EPS) * g

        bd512 = bd_ref[...]
        bd128 = bd_ref[0:SW_KV_WIDTH, 0:SW_KV_WIDTH]
        z = proj(0, 512)
        zq_ref[:, 0:512] = z.astype(BF16)
        qa_ref[...] = (headnorm(z, gqa_ref[...], bd512) * SCALE).astype(BF16)
        z = proj(512, 1024)
        zq_ref[:, 512:1024] = z.astype(BF16)
        ka_ref[...] = headnorm(z, gka_ref[...], bd512).astype(BF16)
        z = proj(1024, 1536)
        zq_ref[:, 1024:1536] = z.astype(BF16)
        z = proj(1536, 2048)
        zq_ref[:, 1536:2048] = z.astype(BF16)
        qs_ref[...] = (headnorm(z, gqs_ref[...], bd512) * SCALE).astype(BF16)
        z = proj(2048, 2176)
        zq_ref[:, 2048:2176] = z.astype(BF16)
        ks_ref[...] = headnorm(z, gks_ref[...], bd128).astype(BF16)
        z = proj(2176, 2304)
        zq_ref[:, 2176:2304] = z.astype(BF16)
        for c0 in range(0, 2 * d, gc):
            zg = proj(QKV_WIDTH + c0, QKV_WIDTH + c0 + gc) + b_ref[:, c0:c0 + gc]
            gt_ref[:, c0:c0 + gc] = _sigmoid(zg).astype(BF16)

    return pl.pallas_call(
        body, name=name, grid=(s // tm,),
        in_specs=[_rows(tm, d), _full((1, d)), _mat(*win_t), _full((1, 2 * d)),
                  _full((1, 512)), _full((1, 512)), _full((1, 512)), _full((1, 128)), _full((512, 512))],
        out_specs=[_rows(tm, d), _rows(tm, QKV_WIDTH), _rows(tm, 512), _rows(tm, 512), _rows(tm, 512),
                   _rows(tm, 128), _rows(tm, 2 * d)],
        out_shape=[jax.ShapeDtypeStruct((s, d), BF16), jax.ShapeDtypeStruct((s, QKV_WIDTH), BF16),
                   jax.ShapeDtypeStruct((s, 512), BF16), jax.ShapeDtypeStruct((s, 512), BF16),
                   jax.ShapeDtypeStruct((s, 512), BF16), jax.ShapeDtypeStruct((s, 128), BF16),
                   jax.ShapeDtypeStruct((s, 2 * d), BF16)],
        compiler_params=_params(),
    )(x, gain, win_t[0], b_gate, gq_na, gk_na, gq_sw, gk_sw, bd)


def _na_iotas():
    qc = lax.broadcasted_iota(jnp.int32, (GRID_W, LANES), 0)
    ln = lax.broadcasted_iota(jnp.int32, (GRID_W, LANES), 1)
    low = ln < GRID_W
    kc = jnp.where(low, ln, ln - GRID_W)
    diff = kc - qc + (NA_COLS - 1)
    qcs = jnp.clip(qc - NA_COLS // 2, 0, GRID_W - NA_COLS)
    inwin = (kc >= qcs) & (kc < qcs + NA_COLS)
    return diff, low, inwin


NA_RI = 2 * NA_ROWS - 1
NA_CI = 2 * NA_COLS - 1
NA_T2 = NA_RI + 1


def _rpb_rows(rpb):
    h = rpb.shape[0]
    padded = jnp.pad(rpb, ((0, 0), (1, 1), (0, GRID_W - NA_CI)))
    return jnp.concatenate([padded[:, :NA_T2], padded[:, 1:NA_T2 + 1]], axis=2).reshape(h, NA_T2, LANES)


def _rpb_from_rows(rows):
    return rows[:, 1:, :NA_CI] + rows[:, :NA_RI, GRID_W:GRID_W + NA_CI]


def rpb_expand(rows, dep, name):
    n_heads = rows.shape[0]

    def body(r_ref, dep_ref, o_ref):
        for h in range(n_heads):
            for e in range(NA_T2):
                line = jnp.broadcast_to(r_ref[h, e:e + 1, :], (GRID_W, LANES))
                o_ref[h, e] = pltpu.roll(line, LANES - (NA_COLS - 1), 1, stride=1, stride_axis=0)

    return pl.pallas_call(
        body, name=name,
        in_specs=[pl.BlockSpec(memory_space=pltpu.VMEM), pl.BlockSpec(memory_space=pltpu.VMEM)],
        out_specs=pl.BlockSpec(memory_space=pltpu.VMEM),
        out_shape=jax.ShapeDtypeStruct((n_heads, NA_T2, GRID_W, LANES), F32),
        compiler_params=pltpu.CompilerParams(vmem_limit_bytes=V7X_VMEM_LIMIT),
    )(rows, dep)


def rpb_reduce(dt2, name):
    n_heads = dt2.shape[0]
    flip = jnp.asarray(np.eye(GRID_W)[::-1], BF16)

    def body(d_ref, j_ref, o_ref):
        jm = j_ref[...]
        for h in range(n_heads):
            for e in range(NA_T2):
                dv = d_ref[h, e]
                hi = dv.astype(BF16)
                mid = (dv - hi.astype(F32)).astype(BF16)
                lo = (dv - hi.astype(F32) - mid.astype(F32)).astype(BF16)
                rev = _dot(jm, hi) + _dot(jm, mid) + _dot(jm, lo)
                back = pltpu.roll(rev, LANES + (NA_COLS - 1) - (GRID_W - 1), 1, stride=1, stride_axis=0)
                o_ref[h, e:e + 1, :] = jnp.sum(back, axis=0, keepdims=True)

    return pl.pallas_call(
        body, name=name,
        in_specs=[pl.BlockSpec(memory_space=pltpu.VMEM)] * 2,
        out_specs=pl.BlockSpec(memory_space=pltpu.VMEM),
        out_shape=jax.ShapeDtypeStruct((n_heads, NA_T2, LANES), F32),
        compiler_params=pltpu.CompilerParams(vmem_limit_bytes=V7X_VMEM_LIMIT),
    )(dt2, flip)


NA_TQ = 4
NA_TK = NA_TQ + NA_ROWS
NA_KCH = NA_TK // 2


def _na_tile_geometry(t, rows):
    r = t * NA_TQ
    kbase = jnp.clip(r - NA_ROWS // 2, 0, rows - NA_TK)
    starts = [jnp.clip(r + a - NA_ROWS // 2, 0, rows - NA_ROWS) for a in range(NA_TQ)]
    return r, kbase, starts


def _na_tile_mask(kbase, starts, low, inwin):
    half = jnp.where(low, 0, 1)
    cols = []
    for c in range(NA_KCH):
        krow = kbase + 2 * c + half
        cols.append(jnp.concatenate(
            [jnp.where(inwin & (krow >= st) & (krow < st + NA_ROWS), 0.0, NEG) for st in starts], axis=0))
    return jnp.concatenate(cols, axis=1)


def _na_tile_index(r, kbase, a, c):
    return jnp.clip(kbase + 2 * c - (r + a) + NA_ROWS, 0, NA_T2 - 1)


def _na_tile_scores(q, k, t2_ref, hh, r, kbase, madd):
    bias = jnp.concatenate(
        [jnp.concatenate([t2_ref[hh, _na_tile_index(r, kbase, a, c)] for a in range(NA_TQ)], axis=0)
         for c in range(NA_KCH)], axis=1)
    return _dotg(q, k, NT) + bias + madd


def _softmax_rows(sc):
    e = jnp.exp(sc - jnp.max(sc, axis=1, keepdims=True))
    return e * (1.0 / jnp.sum(e, axis=1, keepdims=True))


def na_fwd(qa, ka, zq, t2, name):
    s = qa.shape[0]
    rows = s // GRID_W
    n_pairs = NA_WIDTH // LANES
    v_blk0 = (2 * NA_WIDTH) // LANES

    assert rows % NA_TQ == 0 and rows >= NA_TK
    tq, tk = NA_TQ * GRID_W, NA_TK * GRID_W

    def body(q_ref, k_ref, v_ref, t2_ref, o_ref, s_scr, p_scr):
        _, low, inwin = _na_iotas()

        def tile(t, carry):
            r, kbase, starts = _na_tile_geometry(t, rows)
            madd = _na_tile_mask(kbase, starts, low, inwin)
            qr = pl.ds(pl.multiple_of(r * GRID_W, tq), tq)
            kr = pl.ds(pl.multiple_of(kbase * GRID_W, tq), tk)
            for hh in range(2):
                lanes = slice(HEAD_DIM * hh, HEAD_DIM * (hh + 1))
                s_scr[tq * hh:tq * (hh + 1), :] = _na_tile_scores(q_ref[qr, lanes], k_ref[kr, lanes], t2_ref, hh, r,
                                                                  kbase, madd)
            p_scr[...] = _softmax_rows(s_scr[...]).astype(BF16)
            for hh in range(2):
                lanes = slice(HEAD_DIM * hh, HEAD_DIM * (hh + 1))
                o_ref[qr, lanes] = _dot(p_scr[tq * hh:tq * (hh + 1), :], v_ref[kr, lanes]).astype(BF16)
            return carry

        lax.fori_loop(0, rows // NA_TQ, tile, 0)

    col = lambda off: pl.BlockSpec((s, LANES), lambda p, _o=off: (0, _o + p))
    return pl.pallas_call(
        body, name=name, grid=(n_pairs,),
        in_specs=[col(0), col(0), col(v_blk0),
                  pl.BlockSpec((2, NA_T2, GRID_W, LANES), lambda p: (p, 0, 0, 0))],
        out_specs=col(0),
        out_shape=jax.ShapeDtypeStruct((s, NA_WIDTH), BF16),
        scratch_shapes=[pltpu.VMEM((2 * tq, tk), F32), pltpu.VMEM((2 * tq, tk), BF16)],
        compiler_params=_params(),
    )(qa, ka, zq, t2)


def na_bwd(qa, ka, zq, t2, o_na, do_na, name):
    s = qa.shape[0]
    rows = s // GRID_W
    n_pairs = NA_WIDTH // LANES
    v_blk0 = (2 * NA_WIDTH) // LANES

    tq, tk = NA_TQ * GRID_W, NA_TK * GRID_W

    def body(q_ref, k_ref, v_ref, t2_ref, o_ref, do_ref, dq_ref, dk_ref, dv_ref, dt2_ref):
        _, low, inwin = _na_iotas()
        dk_ref[...] = jnp.zeros(dk_ref.shape, F32)
        dv_ref[...] = jnp.zeros(dv_ref.shape, F32)
        dt2_ref[...] = jnp.zeros(dt2_ref.shape, F32)

        def tile(t, carry):
            r, kbase, starts = _na_tile_geometry(t, rows)
            madd = _na_tile_mask(kbase, starts, low, inwin)
            qr = pl.ds(pl.multiple_of(r * GRID_W, tq), tq)
            kr = pl.ds(pl.multiple_of(kbase * GRID_W, tq), tk)
            for hh in range(2):
                lanes = slice(HEAD_DIM * hh, HEAD_DIM * (hh + 1))
                q, k, v = q_ref[qr, lanes], k_ref[kr, lanes], v_ref[kr, lanes]
                p = _softmax_rows(_na_tile_scores(q, k, t2_ref, hh, r, kbase, madd))
                do = do_ref[qr, lanes]
                delta = jnp.sum(do.astype(F32) * o_ref[qr, lanes].astype(F32), axis=1, keepdims=True)
                ds = p * (_dotg(do, v, NT) - delta)
                for a in range(NA_TQ):
                    for c in range(NA_KCH):
                        e = _na_tile_index(r, kbase, a, c)
                        dt2_ref[hh, e] = dt2_ref[hh, e] + ds[GRID_W * a:GRID_W * (a + 1), LANES * c:LANES * (c + 1)]
                dsb = ds.astype(BF16)
                dq_ref[qr, lanes] = _dot(dsb, k)
                dk_ref[kr, lanes] = dk_ref[kr, lanes] + _dotg(dsb, q, TN)
                dv_ref[kr, lanes] = dv_ref[kr, lanes] + _dotg(p.astype(BF16), do, TN)
            return carry

        lax.fori_loop(0, rows // NA_TQ, tile, 0)

    col = lambda off: pl.BlockSpec((s, LANES), lambda p, _o=off: (0, _o + p))
    t2spec = pl.BlockSpec((2, NA_T2, GRID_W, LANES), lambda p: (p, 0, 0, 0))
    return pl.pallas_call(
        body, name=name, grid=(n_pairs,),
        in_specs=[col(0), col(0), col(v_blk0), t2spec, col(0), col(0)],
        out_specs=[col(0), col(0), col(0), t2spec],
        out_shape=[jax.ShapeDtypeStruct((s, NA_WIDTH), F32)] * 3 + [jax.ShapeDtypeStruct(t2.shape, F32)],
        compiler_params=_params(),
    )(qa, ka, zq, t2, o_na, do_na)


def _t5_bucket_map():
    rel = np.arange(3 * SW_BLOCK)[None, :] - SW_BLOCK - np.arange(SW_BLOCK)[:, None]
    nb = REL_BUCKETS // 2
    max_exact = nb // 2
    n = np.abs(rel)
    large = max_exact + (np.log(np.maximum(n, 1) / max_exact)
                         / np.log(REL_MAX_DIST / max_exact) * (nb - max_exact)).astype(np.int32)
    large = np.minimum(large, nb - 1)
    return ((rel > 0) * nb + np.where(n < max_exact, n, large)).astype(np.int32)


def t5_expand(table, bmap, dep, name):
    def body(tab_ref, bm_ref, dep_ref, o_ref):
        bm = bm_ref[...]
        for h in range(SW_HEADS):
            t = jnp.zeros(bm.shape, F32)
            for b in range(REL_BUCKETS):
                t = jnp.where(bm == b, tab_ref[b, h], t)
            o_ref[h] = t

    return pl.pallas_call(
        body, name=name,
        in_specs=[pl.BlockSpec(memory_space=pltpu.SMEM), pl.BlockSpec(memory_space=pltpu.VMEM),
                  pl.BlockSpec(memory_space=pltpu.VMEM)],
        out_specs=pl.BlockSpec(memory_space=pltpu.VMEM),
        out_shape=jax.ShapeDtypeStruct((SW_HEADS,) + bmap.shape, F32),
        compiler_params=pltpu.CompilerParams(vmem_limit_bytes=V7X_VMEM_LIMIT),
    )(table, bmap, dep)


def t5_reduce(dbias_list, bmap, name):
    n = len(dbias_list)

    def body(*refs):
        d_refs, bm_ref, o_ref = refs[:n], refs[n], refs[n + 1]
        bm = bm_ref[...]
        for h in range(SW_HEADS):
            dv = d_refs[0][h]
            for other in d_refs[1:]:
                dv = dv + other[h]
            rows = [jnp.sum(jnp.where(bm == b, dv, 0.0), axis=0, keepdims=True) for b in range(REL_BUCKETS)]
            r = jnp.concatenate(rows, axis=0)
            o_ref[h] = jnp.broadcast_to(jnp.sum(r, axis=1, keepdims=True), (REL_BUCKETS, LANES))

    return pl.pallas_call(
        body, name=name,
        in_specs=[pl.BlockSpec(memory_space=pltpu.VMEM)] * (n + 1),
        out_specs=pl.BlockSpec(memory_space=pltpu.VMEM),
        out_shape=jax.ShapeDtypeStruct((SW_HEADS, REL_BUCKETS, LANES), F32),
        compiler_params=pltpu.CompilerParams(vmem_limit_bytes=V7X_VMEM_LIMIT),
    )(*dbias_list, bmap)


def _sw_mask_iotas():
    a = lax.broadcasted_iota(jnp.int32, (SW_BLOCK, 3 * SW_BLOCK), 0)
    j = lax.broadcasted_iota(jnp.int32, (SW_BLOCK, 3 * SW_BLOCK), 1)
    inwin = jnp.abs(j - SW_BLOCK - a) <= SW_BLOCK
    return j, inwin


SW_STACK = SW_HEADS * SW_BLOCK


def _sw_softmax(sc, sk):
    m = jnp.maximum(jnp.max(sc, axis=1, keepdims=True), sk)
    e = jnp.exp(sc - m)
    es = jnp.exp(sk - m)
    inv = 1.0 / (jnp.sum(e, axis=1, keepdims=True) + es)
    return e * inv, es * inv


def _sw_prologue(k_ref, v_ref, kp, vp, sink_ref, s):
    pad = s + 2 * SW_BLOCK
    zeros = jnp.zeros((SW_BLOCK, SW_KV_WIDTH), BF16)
    kp[0:SW_BLOCK, :] = zeros
    vp[0:SW_BLOCK, :] = zeros
    kp[SW_BLOCK + s:pad, :] = zeros
    vp[SW_BLOCK + s:pad, :] = zeros
    kp[SW_BLOCK:SW_BLOCK + s, :] = k_ref[...]
    vp[SW_BLOCK:SW_BLOCK + s, :] = v_ref[...]
    return jnp.concatenate([jnp.full((SW_BLOCK, 1), sink_ref[h], F32) for h in range(SW_HEADS)], axis=0)


def sw_fwd(qs, ks, zq, t5b, sink, dep, name):
    s = qs.shape[0]
    nb = s // SW_BLOCK
    v_blk = (3 * NA_WIDTH + SW_Q_WIDTH + SW_KV_WIDTH) // LANES
    pad = s + 2 * SW_BLOCK

    def body(q_ref, k_ref, v_ref, b_ref, sink_ref, dep_ref, o_ref, kp, vp, s_scr, p_scr):
        sink_col = _sw_prologue(k_ref, v_ref, kp, vp, sink_ref, s)
        j, inwin = _sw_mask_iotas()

        def blk(n, carry):
            kpos = n * SW_BLOCK - SW_BLOCK + j
            madd = jnp.where(inwin & (kpos >= 0) & (kpos < s), 0.0, NEG)
            q0 = pl.multiple_of(n * SW_BLOCK, SW_BLOCK)
            qr, kr = pl.ds(q0, SW_BLOCK), pl.ds(q0, 3 * SW_BLOCK)
            for h in range(SW_HEADS):
                g = h // SW_REP
                s_scr[SW_BLOCK * h:SW_BLOCK * (h + 1), :] = _dotg(
                    q_ref[qr, HEAD_DIM * h:HEAD_DIM * (h + 1)], kp[kr, HEAD_DIM * g:HEAD_DIM * (g + 1)], NT) + madd
            p, _ = _sw_softmax(s_scr[...] + b_ref[...], sink_col)
            p_scr[...] = p.astype(BF16)
            for h in range(SW_HEADS):
                g = h // SW_REP
                o_ref[qr, HEAD_DIM * h:HEAD_DIM * (h + 1)] = _dot(
                    p_scr[SW_BLOCK * h:SW_BLOCK * (h + 1), :], vp[kr, HEAD_DIM * g:HEAD_DIM * (g + 1)]).astype(BF16)
            return carry

        lax.fori_loop(0, nb, blk, 0)

    return pl.pallas_call(
        body, name=name, grid=(1,),
        in_specs=[_full((s, SW_Q_WIDTH)), _full((s, SW_KV_WIDTH)),
                  pl.BlockSpec((s, SW_KV_WIDTH), lambda i: (0, v_blk)),
                  _full((SW_STACK, 3 * SW_BLOCK)), pl.BlockSpec(memory_space=pltpu.SMEM),
                  _full(dep.shape)],
        out_specs=_full((s, SW_Q_WIDTH)),
        out_shape=jax.ShapeDtypeStruct((s, SW_Q_WIDTH), BF16),
        scratch_shapes=[pltpu.VMEM((pad, SW_KV_WIDTH), BF16), pltpu.VMEM((pad, SW_KV_WIDTH), BF16),
                        pltpu.VMEM((SW_STACK, 3 * SW_BLOCK), F32), pltpu.VMEM((SW_STACK, 3 * SW_BLOCK), BF16)],
        compiler_params=_params(),
    )(qs, ks, zq, t5b, sink, dep)


def sw_bwd(qs, ks, zq, t5b, sink, o_sw, do_sw, name):
    s = qs.shape[0]
    nb = s // SW_BLOCK
    v_blk = (3 * NA_WIDTH + SW_Q_WIDTH + SW_KV_WIDTH) // LANES
    pad = s + 2 * SW_BLOCK

    def body(q_ref, k_ref, v_ref, b_ref, sink_ref, o_ref, do_ref,
             dq_ref, dk_ref, dv_ref, db_ref, dsk_ref, kp, vp, dkp, dvp, s_scr, dp_scr, ds_scr, p_scr):
        sink_col = _sw_prologue(k_ref, v_ref, kp, vp, sink_ref, s)
        dkp[...] = jnp.zeros(dkp.shape, F32)
        dvp[...] = jnp.zeros(dvp.shape, F32)
        db_ref[...] = jnp.zeros(db_ref.shape, F32)
        dsk_ref[...] = jnp.zeros(dsk_ref.shape, F32)
        j, inwin = _sw_mask_iotas()

        def blk(n, carry):
            kpos = n * SW_BLOCK - SW_BLOCK + j
            madd = jnp.where(inwin & (kpos >= 0) & (kpos < s), 0.0, NEG)
            q0 = pl.multiple_of(n * SW_BLOCK, SW_BLOCK)
            qr, kr = pl.ds(q0, SW_BLOCK), pl.ds(q0, 3 * SW_BLOCK)
            deltas = []
            for h in range(SW_HEADS):
                g = h // SW_REP
                hl, kl = slice(HEAD_DIM * h, HEAD_DIM * (h + 1)), slice(HEAD_DIM * g, HEAD_DIM * (g + 1))
                rows = slice(SW_BLOCK * h, SW_BLOCK * (h + 1))
                do = do_ref[qr, hl]
                s_scr[rows, :] = _dotg(q_ref[qr, hl], kp[kr, kl], NT) + madd
                dp_scr[rows, :] = _dotg(do, vp[kr, kl], NT)
                deltas.append(jnp.sum(do.astype(F32) * o_ref[qr, hl].astype(F32), axis=1, keepdims=True))
            delta = jnp.concatenate(deltas, axis=0)
            p, ps = _sw_softmax(s_scr[...] + b_ref[...], sink_col)
            ds = p * (dp_scr[...] - delta)
            db_ref[...] = db_ref[...] + ds
            dsk_ref[...] = dsk_ref[...] - jnp.broadcast_to(ps * delta, (SW_STACK, LANES))
            ds_scr[...] = ds.astype(BF16)
            p_scr[...] = p.astype(BF16)
            for g in range(SW_HEADS // SW_REP):
                kl = slice(HEAD_DIM * g, HEAD_DIM * (g + 1))
                k = kp[kr, kl]
                dkw = jnp.zeros((3 * SW_BLOCK, HEAD_DIM), F32)
                dvw = jnp.zeros((3 * SW_BLOCK, HEAD_DIM), F32)
                for r in range(SW_REP):
                    h = g * SW_REP + r
                    hl, rows = slice(HEAD_DIM * h, HEAD_DIM * (h + 1)), slice(SW_BLOCK * h, SW_BLOCK * (h + 1))
                    dsb = ds_scr[rows, :]
                    dq_ref[qr, hl] = _dot(dsb, k)
                    dkw = dkw + _dotg(dsb, q_ref[qr, hl], TN)
                    dvw = dvw + _dotg(p_scr[rows, :], do_ref[qr, hl], TN)
                dkp[kr, kl] = dkp[kr, kl] + dkw
                dvp[kr, kl] = dvp[kr, kl] + dvw
            return carry

        lax.fori_loop(0, nb, blk, 0)
        dk_ref[...] = dkp[SW_BLOCK:SW_BLOCK + s, :]
        dv_ref[...] = dvp[SW_BLOCK:SW_BLOCK + s, :]

    bias_spec = _full((SW_STACK, 3 * SW_BLOCK))
    return pl.pallas_call(
        body, name=name, grid=(1,),
        in_specs=[_full((s, SW_Q_WIDTH)), _full((s, SW_KV_WIDTH)),
                  pl.BlockSpec((s, SW_KV_WIDTH), lambda i: (0, v_blk)),
                  bias_spec, pl.BlockSpec(memory_space=pltpu.SMEM),
                  _full((s, SW_Q_WIDTH)), _full((s, SW_Q_WIDTH))],
        out_specs=[_full((s, SW_Q_WIDTH)), _full((s, SW_KV_WIDTH)), _full((s, SW_KV_WIDTH)), bias_spec,
                   _full((SW_STACK, LANES))],
        out_shape=[jax.ShapeDtypeStruct((s, SW_Q_WIDTH), F32), jax.ShapeDtypeStruct((s, SW_KV_WIDTH), F32),
                   jax.ShapeDtypeStruct((s, SW_KV_WIDTH), F32),
                   jax.ShapeDtypeStruct((SW_STACK, 3 * SW_BLOCK), F32),
                   jax.ShapeDtypeStruct((SW_STACK, LANES), F32)],
        scratch_shapes=[pltpu.VMEM((pad, SW_KV_WIDTH), BF16), pltpu.VMEM((pad, SW_KV_WIDTH), BF16),
                        pltpu.VMEM((pad, SW_KV_WIDTH), F32), pltpu.VMEM((pad, SW_KV_WIDTH), F32),
                        pltpu.VMEM((SW_STACK, 3 * SW_BLOCK), F32), pltpu.VMEM((SW_STACK, 3 * SW_BLOCK), F32),
                        pltpu.VMEM((SW_STACK, 3 * SW_BLOCK), BF16), pltpu.VMEM((SW_STACK, 3 * SW_BLOCK), BF16)],
        compiler_params=_params(),
    )(qs, ks, zq, t5b, sink, o_sw, do_sw)


def merge_out(x, o_na, o_sw, gt, wbna_t, wbsw_t, wout, name):
    s, d = x.shape
    tm = _row_tile(s)

    def body(x_ref, ona_ref, osw_ref, gt_ref, wna_ref, wsw_ref, wo_ref, xo_ref, ana_ref, asw_ref, mg_ref):
        a_na = _dotg(ona_ref[...], wna_ref[...], NT)
        a_sw = _dotg(osw_ref[...], wsw_ref[...], NT)
        ana_ref[...] = a_na.astype(BF16)
        asw_ref[...] = a_sw.astype(BF16)
        merged = (gt_ref[:, 0:d].astype(F32) * a_na + gt_ref[:, d:2 * d].astype(F32) * a_sw).astype(BF16)
        mg_ref[...] = merged
        xo_ref[...] = x_ref[...] + _dot(merged, wo_ref[...])

    return pl.pallas_call(
        body, name=name, grid=(s // tm,),
        in_specs=[_rows(tm, d), _rows(tm, 512), _rows(tm, 512), _rows(tm, 2 * d),
                  _mat(*wbna_t), _mat(*wbsw_t), _mat(*wout)],
        out_specs=[_rows(tm, d)] * 4,
        out_shape=[jax.ShapeDtypeStruct((s, d), F32)] + [jax.ShapeDtypeStruct((s, d), BF16)] * 3,
        compiler_params=_params(),
    )(x, o_na, o_sw, gt, wbna_t[0], wbsw_t[0], wout[0])


def mix_bwd_out(dx, gt, a_na, a_sw, wbna_t, wbsw_t, wout, name):
    s, d = dx.shape
    tm = _row_tile(s)

    def body(dx_ref, gt_ref, ana_ref, asw_ref, wna_ref, wsw_ref, wo_ref,
             dxb_ref, dzg_ref, dana_ref, dasw_ref, dona_ref, dosw_ref, dbg_ref):
        @pl.when(pl.program_id(0) == 0)
        def _():
            dbg_ref[...] = jnp.zeros(dbg_ref.shape, F32)

        dxb = dx_ref[...].astype(BF16)
        dxb_ref[...] = dxb
        dm = _dotg(dxb, wo_ref[...], NT)
        for i, (a_ref, da_ref, w_ref, do_ref) in enumerate(
                [(ana_ref, dana_ref, wna_ref, dona_ref), (asw_ref, dasw_ref, wsw_ref, dosw_ref)]):
            gi = gt_ref[:, i * d:(i + 1) * d].astype(F32)
            da = (dm * gi).astype(BF16)
            da_ref[...] = da
            do_ref[...] = _dot(da, w_ref[...]).astype(BF16)
            dzg = dm * a_ref[...].astype(F32) * gi * (1.0 - gi)
            dzg_ref[:, i * d:(i + 1) * d] = dzg.astype(BF16)
            dbg_ref[:, i * d:(i + 1) * d] = dbg_ref[:, i * d:(i + 1) * d] + jnp.sum(dzg, axis=0, keepdims=True)

    return pl.pallas_call(
        body, name=name, grid=(s // tm,),
        in_specs=[_rows(tm, d), _rows(tm, 2 * d), _rows(tm, d), _rows(tm, d),
                  _mat(*wbna_t), _mat(*wbsw_t), _mat(*wout)],
        out_specs=[_rows(tm, d), _rows(tm, 2 * d), _rows(tm, d), _rows(tm, d), _rows(tm, 512), _rows(tm, 512),
                   _full((1, 2 * d))],
        out_shape=[jax.ShapeDtypeStruct((s, d), BF16), jax.ShapeDtypeStruct((s, 2 * d), BF16),
                   jax.ShapeDtypeStruct((s, d), BF16), jax.ShapeDtypeStruct((s, d), BF16),
                   jax.ShapeDtypeStruct((s, 512), BF16), jax.ShapeDtypeStruct((s, 512), BF16),
                   jax.ShapeDtypeStruct((1, 2 * d), F32)],
        compiler_params=_params(),
    )(dx, gt, a_na, a_sw, wbna_t[0], wbsw_t[0], wout[0])


def qk_norm_bwd(dqa, dka, dva, dqs, dks, dvs, zq, dzg, gq_na, gk_na, gq_sw, gk_sw, bd, name):
    s = zq.shape[0]
    d2 = dzg.shape[1]
    n_in = QKV_WIDTH + d2
    tm = _row_tile(s)

    def body(dqa_ref, dka_ref, dva_ref, dqs_ref, dks_ref, dvs_ref, zq_ref, dzg_ref,
             gqa_ref, gka_ref, gqs_ref, gks_ref, bd_ref, dz_ref, dgqa_ref, dgka_ref, dgqs_ref, dgks_ref):
        @pl.when(pl.program_id(0) == 0)
        def _():
            for r in (dgqa_ref, dgka_ref, dgqs_ref, dgks_ref):
                r[...] = jnp.zeros(r.shape, F32)

        bd512 = bd_ref[...]
        bd128 = bd_ref[0:SW_KV_WIDTH, 0:SW_KV_WIDTH]

        def one(c0, c1, dy_ref, g_ref, dg_ref, bdm, scale):
            z = zq_ref[:, c0:c1].astype(F32)
            r = lax.rsqrt(_group_mean(z * z, bdm) + EPS)
            zh = z * r
            dy = dy_ref[...] * scale
            dyg = dy * g_ref[...]
            dz = r * (dyg - zh * _group_mean(dyg * zh, bdm))
            dz_ref[:, c0:c1] = dz.astype(BF16)
            dg_ref[...] = dg_ref[...] + jnp.sum(dy * zh, axis=0, keepdims=True)

        one(0, 512, dqa_ref, gqa_ref, dgqa_ref, bd512, SCALE)
        one(512, 1024, dka_ref, gka_ref, dgka_ref, bd512, 1.0)
        dz_ref[:, 1024:1536] = dva_ref[...].astype(BF16)
        one(1536, 2048, dqs_ref, gqs_ref, dgqs_ref, bd512, SCALE)
        one(2048, 2176, dks_ref, gks_ref, dgks_ref, bd128, 1.0)
        dz_ref[:, 2176:2304] = dvs_ref[...].astype(BF16)
        dz_ref[:, QKV_WIDTH:n_in] = dzg_ref[...]

    return pl.pallas_call(
        body, name=name, grid=(s // tm,),
        in_specs=[_rows(tm, 512), _rows(tm, 512), _rows(tm, 512), _rows(tm, 512), _rows(tm, 128), _rows(tm, 128),
                  _rows(tm, QKV_WIDTH), _rows(tm, d2),
                  _full((1, 512)), _full((1, 512)), _full((1, 512)), _full((1, 128)), _full((512, 512))],
        out_specs=[_rows(tm, n_in), _full((1, 512)), _full((1, 512)), _full((1, 512)), _full((1, 128))],
        out_shape=[jax.ShapeDtypeStruct((s, n_in), BF16)] + [jax.ShapeDtypeStruct((1, 512), F32)] * 3
                  + [jax.ShapeDtypeStruct((1, 128), F32)],
        compiler_params=_params(),
    )(dqa, dka, dva, dqs, dks, dvs, zq, dzg, gq_na, gk_na, gq_sw, gk_sw, bd)


def ffn_bwd_act(dx, wd, hg, hu, name):
    s, d = dx.shape
    f = wd[0].shape[1]
    tm = _row_tile(s)
    fc = _col_chunk(f)

    def body(dx_ref, w_ref, hg_ref, hu_ref, dxb_ref, dhg_ref, dhu_ref):
        dxb = dx_ref[...].astype(BF16)
        dxb_ref[...] = dxb
        for c0 in range(0, f, fc):
            dact = 0.5 * _dotg(dxb, w_ref[c0:c0 + fc, :], NT)
            hg = hg_ref[:, c0:c0 + fc].astype(F32)
            hu = hu_ref[:, c0:c0 + fc].astype(F32)
            sg = _sigmoid(hg)
            dhu_ref[:, c0:c0 + fc] = (dact * hg * sg).astype(BF16)
            dhg_ref[:, c0:c0 + fc] = (dact * hu * sg * (1.0 + hg * (1.0 - sg))).astype(BF16)

    return pl.pallas_call(
        body, name=name, grid=(s // tm,),
        in_specs=[_rows(tm, d), _mat(*wd), _rows(tm, f), _rows(tm, f)],
        out_specs=[_rows(tm, d), _rows(tm, f), _rows(tm, f)],
        out_shape=[jax.ShapeDtypeStruct((s, d), BF16), jax.ShapeDtypeStruct((s, f), BF16),
                   jax.ShapeDtypeStruct((s, f), BF16)],
        compiler_params=_params(),
    )(dx, wd[0], hg, hu)


def proj_bwd_norm(acts, weights, x, gain, dx, dep, name):
    s, d = x.shape
    tm = _row_tile(s)
    n = len(acts)

    def body(*refs):
        a_refs, w_refs = refs[:n], refs[n:2 * n]
        x_ref, g_ref, dx_ref, _, o_ref, dg_ref = refs[2 * n:]

        @pl.when(pl.program_id(0) == 0)
        def _():
            dg_ref[...] = jnp.zeros(dg_ref.shape, F32)

        dxn = _dot(a_refs[0][...], w_refs[0][...])
        for a_ref, w_ref in zip(a_refs[1:], w_refs[1:]):
            dxn = dxn + _dot(a_ref[...], w_ref[...])
        xv = x_ref[...]
        r = _rstd(xv)
        xh = xv * r
        dxh = dxn * g_ref[...]
        o_ref[...] = dx_ref[...] + r * (dxh - xh * jnp.mean(dxh * xh, axis=-1, keepdims=True))
        dg_ref[...] = dg_ref[...] + jnp.sum(dxn * xh, axis=0, keepdims=True)

    return pl.pallas_call(
        body, name=name, grid=(s // tm,),
        in_specs=[_rows(tm, a.shape[1]) for a in acts] + [_mat(*w) for w in weights]
                 + [_rows(tm, d), _full((1, d)), _rows(tm, d), _full(dep.shape)],
        out_specs=[_rows(tm, d), _full((1, d))],
        out_shape=[jax.ShapeDtypeStruct((s, d), F32), jax.ShapeDtypeStruct((1, d), F32)],
        compiler_params=_params(),
    )(*acts, *[w[0] for w in weights], x, gain, dx, dep)


def tn_matmul(a, b, scale, name):
    s, n = a.shape
    k = b.shape[1]
    tn = _tn_tile(n)

    def body(a_ref, b_ref, o_ref):
        o_ref[...] = (scale * _dotg(a_ref[...], b_ref[...], TN)).astype(BF16)

    return pl.pallas_call(
        body, name=name, grid=(n // tn,),
        in_specs=[pl.BlockSpec((s, tn), lambda i: (0, i)),
                  pl.BlockSpec((s, k), lambda i: (0, 0), pipeline_mode=ONCE)],
        out_specs=pl.BlockSpec((tn, k), lambda i: (i, 0)),
        out_shape=jax.ShapeDtypeStruct((n, k), BF16),
        compiler_params=_params(),
    )(a, b)


def loss_grad(y, target, name):
    s, d = y.shape
    tm = _row_tile(s)

    def body(y_ref, t_ref, dy_ref, acc_ref):
        @pl.when(pl.program_id(0) == 0)
        def _():
            acc_ref[...] = jnp.zeros(acc_ref.shape, F32)

        err = y_ref[...] - t_ref[...]
        dy_ref[...] = err * (1.0 / d)
        e2 = err * err
        part = jnp.sum(e2.reshape(tm // 8, 8, d), axis=0)
        acc = part[:, 0:LANES]
        for c0 in range(LANES, d, LANES):
            acc = acc + part[:, c0:c0 + LANES]
        acc_ref[...] = acc_ref[...] + acc

    return pl.pallas_call(
        body, name=name, grid=(s // tm,),
        in_specs=[_rows(tm, d), _rows(tm, d)],
        out_specs=[_rows(tm, d), _full((8, LANES))],
        out_shape=[jax.ShapeDtypeStruct((s, d), F32), jax.ShapeDtypeStruct((8, LANES), F32)],
        compiler_params=_params(),
    )(y, target)


def _mesh_pos():
    return lax.axis_index("x"), lax.axis_index("y"), lax.axis_index("c")


def _peers():
    x, y, c = _mesh_pos()
    peers = []
    for rel in range(1, N_DEV):
        peers.append((1 - x if rel & 4 else x, 1 - y if rel & 2 else y, 1 - c if rel & 1 else c))
    return 4 * x + 2 * y + c, peers


HBM_SPEC = pl.BlockSpec(memory_space=pltpu.HBM)
SEM_SPEC = pl.BlockSpec(memory_space=pltpu.SEMAPHORE)


def _split_call(body, name, thru, n_sems, extra=(), with_token=True):
    hbm = lambda t: pltpu.with_memory_space_constraint(t, pltpu.HBM)
    effect = pltpu.CompilerParams(has_side_effects=pltpu.SideEffectType.DATAFLOW_SIDE_EFFECTING)
    nt = len(thru)
    thru_shapes = [pltpu.HBM(t.shape, t.dtype) for t in thru]
    if with_token:
        (after,) = extra
        outs = pl.pallas_call(
            body, name=name, in_specs=[HBM_SPEC] * nt + [pl.BlockSpec(memory_space=pl.ANY)],
            out_specs=[SEM_SPEC] * len(n_sems) + [HBM_SPEC] * nt + [pl.BlockSpec(memory_space=pltpu.VMEM)],
            out_shape=[pltpu.SemaphoreType.DMA((k,)) for k in n_sems] + thru_shapes
                      + [jax.ShapeDtypeStruct((8, LANES), F32)],
            input_output_aliases={i: len(n_sems) + i for i in range(nt)}, compiler_params=effect,
        )(*[hbm(t) for t in thru], after)
        return outs[:len(n_sems)], outs[len(n_sems):-1], outs[-1]
    return pl.pallas_call(
        body, name=name,
        in_specs=[HBM_SPEC] * nt + [SEM_SPEC] * len(n_sems) + [pl.BlockSpec(memory_space=pl.ANY)],
        out_specs=[HBM_SPEC] * nt, out_shape=thru_shapes,
        input_output_aliases={i: i for i in range(nt)}, compiler_params=effect,
    )(*thru, *extra)


def _gather_targets():
    x, y, c = _mesh_pos()
    return 4 * x + 2 * y + c, [(x, y, 1 - c), (1 - x, y, c), (x, 1 - y, c), (1 - x, 1 - y, c)]


def gather_start(shards, after, name):
    n = len(shards)
    zones = [lax.empty((w.shape[0], N_DEV) + w.shape[1:], w.dtype) for w in shards]

    def body(*refs):
        ins, zs = refs[:n], refs[n:2 * n]
        send_sems, recv_sems, local_sems = refs[2 * n + 1:2 * n + 4]
        token = refs[-1]
        me, targets = _gather_targets()
        for a in range(n):
            pltpu.make_async_copy(ins[a], zs[a].at[:, me], local_sems.at[a]).start()
            for k, to in enumerate(targets):
                pltpu.make_async_remote_copy(
                    src_ref=ins[a], dst_ref=zs[a].at[:, me], send_sem=send_sems.at[4 * a + k],
                    recv_sem=recv_sems.at[4 * a + k], device_id=to, device_id_type=MESH).start()
        token[...] = jnp.zeros(token.shape, F32)

    sems, thru, token = _split_call(body, name, list(shards) + zones, (4 * n, 4 * n, n), extra=(after,))
    return (sems, thru, n), token


def gather_wait(started, after, name):
    sems, thru, n = started

    def body(*refs):
        zs = refs[n:2 * n]
        send_sems, recv_sems, local_sems = refs[2 * n:2 * n + 3]
        _, targets = _gather_targets()
        for a in range(n):
            for k, to in enumerate(targets):
                cp = pltpu.make_async_remote_copy(
                    src_ref=zs[a].at[:, 0], dst_ref=zs[a].at[:, 0], send_sem=send_sems.at[4 * a + k],
                    recv_sem=recv_sems.at[4 * a + k], device_id=to, device_id_type=MESH)
                cp.wait_send()
                cp.wait_recv()
            pltpu.make_async_copy(zs[a].at[:, 0], zs[a].at[:, 0], local_sems.at[a]).wait()

    return _split_call(body, name, thru, (4 * n, 4 * n, n), extra=(*sems, after), with_token=False)[n:]


def forward_start(zones, after, name):
    n = len(zones)

    def body(*refs):
        zs = refs[:n]
        send_sems, recv_sems = refs[n + 1:n + 3]
        token = refs[-1]
        x, y, c = _mesh_pos()
        for a in range(n):
            for j, chip in enumerate([(1 - x, y), (x, 1 - y), (1 - x, 1 - y)]):
                blk = zs[a].at[:, 4 * chip[0] + 2 * chip[1] + c]
                pltpu.make_async_remote_copy(
                    src_ref=blk, dst_ref=blk, send_sem=send_sems.at[3 * a + j], recv_sem=recv_sems.at[3 * a + j],
                    device_id=(x, y, 1 - c), device_id_type=MESH).start()
        token[...] = jnp.zeros(token.shape, F32)

    sems, thru, token = _split_call(body, name, list(zones), (3 * n, 3 * n), extra=(after,))
    return (sems, thru, n), token


def forward_wait(started, after, name):
    sems, thru, n = started

    def body(*refs):
        zs = refs[:n]
        send_sems, recv_sems = refs[n:n + 2]
        x, y, c = _mesh_pos()
        for a in range(n):
            for j in range(3):
                cp = pltpu.make_async_remote_copy(
                    src_ref=zs[a].at[:, 0], dst_ref=zs[a].at[:, 0], send_sem=send_sems.at[3 * a + j],
                    recv_sem=recv_sems.at[3 * a + j], device_id=(x, y, 1 - c), device_id_type=MESH)
                cp.wait_send()
                cp.wait_recv()

    return _split_call(body, name, thru, (3 * n, 3 * n), extra=(*sems, after), with_token=False)


def scatter_start(groups, name):
    n = len(groups)
    flat = [g for grp in groups for g in grp]
    nf = len(flat)
    offs = np.cumsum([0] + [len(grp) for grp in groups])
    lands = [lax.empty((N_DEV, len(grp)) + grp[0].shape[1:], grp[0].dtype) for grp in groups]

    def body(*refs):
        ins, zones = refs[:nf], refs[nf:nf + n]
        send_sems, recv_sems, local_sems = refs[nf + n:nf + n + 3]
        token = refs[-1]
        me, peers = _peers()
        for a in range(n):
            for w in range(len(groups[a])):
                pltpu.make_async_copy(ins[offs[a] + w].at[me], zones[a].at[me, w], local_sems.at[a]).start()
        for k, peer in enumerate(peers):
            p_id = 4 * peer[0] + 2 * peer[1] + peer[2]
            for a in range(n):
                for w in range(len(groups[a])):
                    pltpu.make_async_remote_copy(
                        src_ref=ins[offs[a] + w].at[p_id], dst_ref=zones[a].at[me, w],
                        send_sem=send_sems.at[7 * a + k], recv_sem=recv_sems.at[7 * a + k],
                        device_id=peer, device_id_type=MESH).start()
        token[...] = jnp.zeros(token.shape, F32)

    hbm = lambda t: pltpu.with_memory_space_constraint(t, pltpu.HBM)
    outs = pl.pallas_call(
        body, name=name,
        in_specs=[HBM_SPEC] * (nf + n),
        out_specs=[SEM_SPEC] * 3 + [HBM_SPEC] * (nf + n) + [pl.BlockSpec(memory_space=pltpu.VMEM)],
        out_shape=[pltpu.SemaphoreType.DMA((7 * n,)), pltpu.SemaphoreType.DMA((7 * n,)), pltpu.SemaphoreType.DMA((n,))]
                  + [pltpu.HBM(t.shape, t.dtype) for t in flat + lands]
                  + [jax.ShapeDtypeStruct((8, LANES), F32)],
        input_output_aliases={i: 3 + i for i in range(nf + n)},
        compiler_params=pltpu.CompilerParams(has_side_effects=pltpu.SideEffectType.DATAFLOW_SIDE_EFFECTING),
    )(*[hbm(t) for t in flat], *[hbm(t) for t in lands])
    sems, thru, token = outs[:3], outs[3:3 + nf + n], outs[-1]
    return (sems, thru, [len(grp) for grp in groups]), token


def scatter_wait(started, after, name):
    (send_sems, recv_sems, local_sems), thru, sizes = started
    n = len(sizes)
    nf = len(thru) - n

    def body(*refs):
        zones = refs[nf:nf + n]
        s_sems, r_sems, l_sems = refs[nf + n:nf + n + 3]
        me, peers = _peers()
        for a in range(n):
            for k, peer in enumerate(peers):
                cp = pltpu.make_async_remote_copy(
                    src_ref=zones[a].at[0], dst_ref=zones[a].at[0],
                    send_sem=s_sems.at[7 * a + k], recv_sem=r_sems.at[7 * a + k], device_id=peer,
                    device_id_type=MESH)
                cp.wait_send()
                cp.wait_recv()
            pltpu.make_async_copy(zones[a].at[0], zones[a].at[0], l_sems.at[a]).wait()

    outs = pl.pallas_call(
        body, name=name,
        in_specs=[HBM_SPEC] * (nf + n) + [SEM_SPEC] * 3 + [pl.BlockSpec(memory_space=pl.ANY)],
        out_specs=[HBM_SPEC] * (nf + n),
        out_shape=[pltpu.HBM(t.shape, t.dtype) for t in thru],
        input_output_aliases={i: i for i in range(nf + n)},
        compiler_params=pltpu.CompilerParams(has_side_effects=pltpu.SideEffectType.DATAFLOW_SIDE_EFFECTING),
    )(*thru, send_sems, recv_sems, local_sems, after)
    return outs[nf:]


def pair_start(grads, after, name):
    nw = len(grads)
    land = lax.empty((4, nw) + grads[0].shape[1:], grads[0].dtype)

    def body(*refs):
        ins, zone = refs[:nw], refs[nw]
        send_sems, recv_sems = refs[nw + 2:nw + 4]
        x, y, c = _mesh_pos()
        for j in range(4):
            for w in range(nw):
                pltpu.make_async_remote_copy(
                    src_ref=ins[w].at[2 * j + (1 - c)], dst_ref=zone.at[j, w], send_sem=send_sems.at[0],
                    recv_sem=recv_sems.at[0], device_id=(x, y, 1 - c), device_id_type=MESH).start()
        refs[-1][...] = jnp.zeros(refs[-1].shape, F32)

    sems, thru, token = _split_call(body, name, list(grads) + [land], (1, 1), extra=(after,))
    return (sems, thru, nw), token


def pair_wait(started, after, name):
    sems, thru, nw = started

    def body(*refs):
        zone = refs[nw]
        send_sems, recv_sems = refs[nw + 1:nw + 3]
        x, y, c = _mesh_pos()
        cp = pltpu.make_async_remote_copy(src_ref=zone, dst_ref=zone, send_sem=send_sems.at[0],
                                          recv_sem=recv_sems.at[0], device_id=(x, y, 1 - c), device_id_type=MESH)
        cp.wait_send()
        cp.wait_recv()

    outs = _split_call(body, name, thru, (1, 1), extra=(*sems, after), with_token=False)
    return outs[:nw], outs[nw]


def pair_sum(grads, land, name):
    nw = len(grads)
    _, r, c_dim = grads[0].shape

    def body(*refs):
        g_refs, l_ref, o_ref = refs[:nw], refs[nw], refs[nw + 1]
        core = lax.axis_index("c")
        for w in range(nw):
            o_ref[0, w] = (g_refs[w][0, core].astype(F32) + l_ref[0, w].astype(F32)).astype(BF16)

    return pl.pallas_call(
        body, name=name, grid=(4,),
        in_specs=[pl.BlockSpec((1, 2, r, c_dim), lambda j: (j, 0, 0, 0))] * nw
                 + [pl.BlockSpec((1, nw, r, c_dim), lambda j: (j, 0, 0, 0))],
        out_specs=pl.BlockSpec((1, nw, r, c_dim), lambda j: (j, 0, 0, 0)),
        out_shape=jax.ShapeDtypeStruct((4, nw, r, c_dim), BF16),
        compiler_params=_params(),
    )(*[g.reshape(4, 2, r, c_dim) for g in grads], land)


def _other_chips():
    x, y, c = _mesh_pos()
    chips = []
    for rel in range(1, 4):
        px, py = (1 - x if rel & 2 else x), (1 - y if rel & 1 else y)
        chips.append((px, py, 2 * px + py))
    return 2 * x + y, c, chips


def chip_start(pair_sums, after, name):
    land = lax.empty(pair_sums.shape, pair_sums.dtype)

    def body(*refs):
        h_ref, zone = refs[0], refs[1]
        send_sems, recv_sems, local_sem = refs[3:6]
        mine, c, chips = _other_chips()
        pltpu.make_async_copy(h_ref.at[mine], zone.at[mine], local_sem.at[0]).start()
        for k, (px, py, j) in enumerate(chips):
            pltpu.make_async_remote_copy(
                src_ref=h_ref.at[j], dst_ref=zone.at[mine], send_sem=send_sems.at[k], recv_sem=recv_sems.at[k],
                device_id=(px, py, c), device_id_type=MESH).start()
        refs[-1][...] = jnp.zeros(refs[-1].shape, F32)

    sems, thru, token = _split_call(body, name, [pair_sums, land], (3, 3, 1), extra=(after,))
    return (sems, thru), token


def chip_wait(started, after, name):
    sems, thru = started

    def body(*refs):
        zone = refs[1]
        send_sems, recv_sems, local_sem = refs[2:5]
        _, c, chips = _other_chips()
        for k, (px, py, _) in enumerate(chips):
            cp = pltpu.make_async_remote_copy(
                src_ref=zone.at[0], dst_ref=zone.at[0], send_sem=send_sems.at[k], recv_sem=recv_sems.at[k],
                device_id=(px, py, c), device_id_type=MESH)
            cp.wait_send()
            cp.wait_recv()
        pltpu.make_async_copy(zone.at[0], zone.at[0], local_sem.at[0]).wait()

    return _split_call(body, name, thru, (3, 3, 1), extra=(*sems, after), with_token=False)[1]


def share_small(small, after):
    def body(s_ref, after_ref, o_ref, send_sems, recv_sems, local_sem):
        me, peers = _peers()
        mine = pltpu.make_async_copy(s_ref, o_ref.at[me], local_sem)
        mine.start()
        copies = [pltpu.make_async_remote_copy(src_ref=s_ref, dst_ref=o_ref.at[me], send_sem=send_sems.at[k],
                                               recv_sem=recv_sems.at[k], device_id=peer, device_id_type=MESH)
                  for k, peer in enumerate(peers)]
        for cp in copies:
            cp.start()
        for cp in copies:
            cp.wait()
        mine.wait()

    vm = pl.BlockSpec(memory_space=pltpu.VMEM)
    return pl.pallas_call(
        body, name="share_small", in_specs=[vm, pl.BlockSpec(memory_space=pl.ANY)], out_specs=vm,
        out_shape=jax.ShapeDtypeStruct((N_DEV,) + small.shape, small.dtype),
        scratch_shapes=[pltpu.SemaphoreType.DMA((7,)), pltpu.SemaphoreType.DMA((7,)), pltpu.SemaphoreType.DMA],
    )(small, after)


def sum_sources(recv, name):
    n_src, w, r, c = recv.shape

    def body(r_ref, o_ref):
        acc = r_ref[0, 0].astype(F32)
        for src in range(1, n_src):
            acc = acc + r_ref[src, 0].astype(F32)
        o_ref[0] = acc

    return pl.pallas_call(
        body, name=name, grid=(w,),
        in_specs=[pl.BlockSpec((n_src, 1, r, c), lambda i: (0, i, 0, 0))],
        out_specs=pl.BlockSpec((1, r, c), lambda i: (i, 0, 0)),
        out_shape=jax.ShapeDtypeStruct((w, r, c), F32),
        compiler_params=_params(),
    )(recv)


def _adamw_math(w, g, m, v):
    m = ADAM_B1 * m + (1.0 - ADAM_B1) * g
    v = ADAM_B2 * v + (1.0 - ADAM_B2) * (g * g)
    m_hat = m / (1.0 - ADAM_B1 ** ADAM_STEP)
    v_hat = v / (1.0 - ADAM_B2 ** ADAM_STEP)
    delta = -ADAM_LR * (m_hat / (jnp.sqrt(v_hat) + ADAM_EPS) + ADAM_WD * w)
    return delta, m, v


def adamw(w, g, m, v, name):
    shape = w.shape
    c = shape[-1]
    r = int(np.prod(shape[:-1]))
    w2, g2, m2, v2 = (t.reshape(r, c) for t in (w, g, m, v))
    tr = next(t for t in range(min(r, 512), 0, -1) if r % t == 0 and (t % 8 == 0 or t == r))

    def body(w_ref, g_ref, m_ref, v_ref, d_ref, mo_ref, vo_ref):
        d_ref[...], mo_ref[...], vo_ref[...] = _adamw_math(w_ref[...], g_ref[...], m_ref[...], v_ref[...])

    spec = pl.BlockSpec((tr, c), lambda i: (i, 0))
    outs = pl.pallas_call(
        body, name=name, grid=(r // tr,),
        in_specs=[spec] * 4, out_specs=[spec] * 3,
        out_shape=[jax.ShapeDtypeStruct((r, c), F32)] * 3,
        compiler_params=_params(),
    )(w2, g2, m2, v2)
    return tuple(t.reshape(shape) for t in outs)


def adamw_small(w, recv, m, v, name):
    def body(w_ref, r_ref, m_ref, v_ref, g_ref, d_ref, mo_ref, vo_ref):
        g = r_ref[0]
        for src in range(1, N_DEV):
            g = g + r_ref[src]
        g_ref[...] = g
        d_ref[...], mo_ref[...], vo_ref[...] = _adamw_math(w_ref[...], g, m_ref[...], v_ref[...])

    vm = pl.BlockSpec(memory_space=pltpu.VMEM)
    return pl.pallas_call(
        body, name=name, in_specs=[vm] * 4, out_specs=[vm] * 4,
        out_shape=[jax.ShapeDtypeStruct(w.shape, F32)] * 4,
        compiler_params=pltpu.CompilerParams(vmem_limit_bytes=V7X_VMEM_LIMIT),
    )(w, recv, m, v)


SMALL_NAMES = ("ffn1_norm", "mix_norm", "ffn2_norm", "b_gate", "na_q_norm", "na_k_norm", "sw_q_norm", "sw_k_norm",
               "na_rpb", "sw_sink", "t5_rel_table")


def _pack_small(parts):
    flat = jnp.concatenate([parts[k].reshape(-1).astype(F32) for k in SMALL_NAMES])
    n = flat.shape[0]
    rows = -(-n // (8 * LANES)) * 8
    return jnp.pad(flat, (0, rows * LANES - n)).reshape(rows, LANES)


def _unpack_small(packed, like):
    flat = packed.reshape(-1)
    out, off = {}, 0
    for k in SMALL_NAMES:
        n = int(np.prod(like[k].shape))
        out[k] = flat[off:off + n].reshape(like[k].shape)
        off += n
    return out


def kernel(x, ffn1_norm, ffn1_w_gate, ffn1_w_up, ffn1_w_down, mix_norm, w_in, b_gate, na_q_norm, na_k_norm, na_rpb, sw_q_norm, sw_k_norm, sw_sink, t5_rel_table, w_branch_na, w_branch_sw, w_out, ffn2_norm, ffn2_w_gate, ffn2_w_up, ffn2_w_down, loss_target, m_ffn1_norm, m_ffn1_w_gate, m_ffn1_w_up, m_ffn1_w_down, m_mix_norm, m_w_in, m_b_gate, m_na_q_norm, m_na_k_norm, m_na_rpb, m_sw_q_norm, m_sw_k_norm, m_sw_sink, m_t5_rel_table, m_w_branch_na, m_w_branch_sw, m_w_out, m_ffn2_norm, m_ffn2_w_gate, m_ffn2_w_up, m_ffn2_w_down, v_ffn1_norm, v_ffn1_w_gate, v_ffn1_w_up, v_ffn1_w_down, v_mix_norm, v_w_in, v_b_gate, v_na_q_norm, v_na_k_norm, v_na_rpb, v_sw_q_norm, v_sw_k_norm, v_sw_sink, v_t5_rel_table, v_w_branch_na, v_w_branch_sw, v_w_out, v_ffn2_norm, v_ffn2_w_gate, v_ffn2_w_up, v_ffn2_w_down):
    weights = dict(ffn1_norm=ffn1_norm, ffn1_w_gate=ffn1_w_gate, ffn1_w_up=ffn1_w_up, ffn1_w_down=ffn1_w_down,
                   mix_norm=mix_norm, w_in=w_in, b_gate=b_gate, na_q_norm=na_q_norm, na_k_norm=na_k_norm,
                   na_rpb=na_rpb, sw_q_norm=sw_q_norm, sw_k_norm=sw_k_norm, sw_sink=sw_sink,
                   t5_rel_table=t5_rel_table, w_branch_na=w_branch_na, w_branch_sw=w_branch_sw, w_out=w_out,
                   ffn2_norm=ffn2_norm, ffn2_w_gate=ffn2_w_gate, ffn2_w_up=ffn2_w_up, ffn2_w_down=ffn2_w_down)
    mom_m = dict(ffn1_norm=m_ffn1_norm, ffn1_w_gate=m_ffn1_w_gate, ffn1_w_up=m_ffn1_w_up, ffn1_w_down=m_ffn1_w_down,
                 mix_norm=m_mix_norm, w_in=m_w_in, b_gate=m_b_gate, na_q_norm=m_na_q_norm, na_k_norm=m_na_k_norm,
                 na_rpb=m_na_rpb, sw_q_norm=m_sw_q_norm, sw_k_norm=m_sw_k_norm, sw_sink=m_sw_sink,
                 t5_rel_table=m_t5_rel_table, w_branch_na=m_w_branch_na, w_branch_sw=m_w_branch_sw, w_out=m_w_out,
                 ffn2_norm=m_ffn2_norm, ffn2_w_gate=m_ffn2_w_gate, ffn2_w_up=m_ffn2_w_up, ffn2_w_down=m_ffn2_w_down)
    mom_v = dict(ffn1_norm=v_ffn1_norm, ffn1_w_gate=v_ffn1_w_gate, ffn1_w_up=v_ffn1_w_up, ffn1_w_down=v_ffn1_w_down,
                 mix_norm=v_mix_norm, w_in=v_w_in, b_gate=v_b_gate, na_q_norm=v_na_q_norm, na_k_norm=v_na_k_norm,
                 na_rpb=v_na_rpb, sw_q_norm=v_sw_q_norm, sw_k_norm=v_sw_k_norm, sw_sink=v_sw_sink,
                 t5_rel_table=v_t5_rel_table, w_branch_na=v_w_branch_na, w_branch_sw=v_w_branch_sw, w_out=v_w_out,
                 ffn2_norm=v_ffn2_norm, ffn2_w_gate=v_ffn2_w_gate, ffn2_w_up=v_ffn2_w_up, ffn2_w_down=v_ffn2_w_down)
    order = list(weights)

    depth = ffn1_norm.shape[0]
    s, d = x.shape[1], x.shape[2]
    xs = x[0]
    tr = lambda w: jnp.swapaxes(w, -1, -2)

    merge = lambda t: t.reshape(t.shape[0], N_DEV * t.shape[2], t.shape[3])
    no_dep = jnp.zeros((8, LANES), F32)

    def shards_of(kind, l):
        stack = lambda *ws: jnp.stack(ws).astype(BF16)
        if kind == "ffn1":
            return [stack(tr(ffn1_w_gate[l]), tr(ffn1_w_up[l]), ffn1_w_down[l])]
        if kind == "win":
            return [stack(tr(w_in[l]))]
        return [stack(tr(ffn2_w_gate[l]), tr(ffn2_w_up[l]), ffn2_w_down[l]), stack(w_out[l]),
                stack(tr(w_branch_na[l]), tr(w_branch_sw[l]))]

    def start(kind, l, after):
        return gather_start(shards_of(kind, l), after, f"gather_{kind}_{l}")

    def arrive(started, kind, l, after):
        zones = gather_wait(started, after, f"gather_{kind}_{l}_wait")
        return forward_start(zones, no_dep, f"forward_{kind}_{l}")

    def finish(fwd, kind, l, after):
        return [merge(z) for z in forward_wait(fwd, after, f"forward_{kind}_{l}_wait")]

    bd = jnp.asarray(np.kron(np.eye(NA_WIDTH // HEAD_DIM), np.full((HEAD_DIM, HEAD_DIM), 1.0 / HEAD_DIM)), BF16)
    bmap = jnp.asarray(_t5_bucket_map())
    tile8 = lambda g: jnp.tile(g, NA_WIDTH // HEAD_DIM).reshape(1, NA_WIDTH)
    tile2 = lambda g: jnp.tile(g, SW_KV_WIDTH // HEAD_DIM).reshape(1, SW_KV_WIDTH)

    st_first, tok = start("ffn1", 0, no_dep)
    t5b = t5_expand(t5_rel_table, bmap, tok, "t5_expand").reshape(SW_STACK, 3 * SW_BLOCK)
    fwd, _ = arrive(st_first, "ffn1", 0, t5b)
    st_win, dep = start("win", 0, t5b)
    (first,) = finish(fwd, "ffn1", 0, dep)

    saved = []
    layer_w = {0: dict(wg1=(first, 0), wu1=(first, 1), wd1=(first, 2))}
    cur = xs
    for l in range(depth):
        sv = {}
        lw = layer_w[l]
        sv["x0"] = cur
        sv["xn1"], sv["hg1"], sv["hu1"], sv["act1"] = ffn_up(cur, ffn1_norm[l][None], lw["wg1"], lw["wu1"], dep,
                                                             f"ffn1_up_{l}")
        cur = ffn_down(cur, sv["act1"], lw["wd1"], no_dep, f"ffn1_down_{l}")
        sv["x1"] = cur
        fwd, _ = arrive(st_win, "win", l, cur)
        st_rest, tok = start("rest", l, cur)
        (zb,) = finish(fwd, "win", l, tok)
        lw["win"] = (zb, 0)
        sv["gains"] = (tile8(na_q_norm[l]), tile8(na_k_norm[l]), tile8(sw_q_norm[l]), tile2(sw_k_norm[l]))
        sv["hn"], sv["zq"], sv["qa"], sv["ka"], sv["qs"], sv["ks"], sv["gt"] = mix_in(
            cur, mix_norm[l][None], lw["win"], b_gate[l][None], *sv["gains"], bd, f"mix_in_{l}")
        sv["t2"] = rpb_expand(_rpb_rows(na_rpb[l]), no_dep, f"rpb_expand_{l}")
        sv["o_na"] = na_fwd(sv["qa"], sv["ka"], sv["zq"], sv["t2"], f"na_fwd_{l}")
        dep = no_dep
        if l + 1 < depth:
            st_ffn1, dep = start("ffn1", l + 1, sv["o_na"])
        sv["o_sw"] = sw_fwd(sv["qs"], sv["ks"], sv["zq"], t5b, sw_sink[l], dep, f"sw_fwd_{l}")
        fwd, tok = arrive(st_rest, "rest", l, sv["o_sw"])
        za, zc, zd = finish(fwd, "rest", l, tok)
        lw.update(wg2=(za, 0), wu2=(za, 1), wd2=(za, 2), wout=(zc, 0), wna=(zd, 0), wsw=(zd, 1))
        cur, sv["a_na"], sv["a_sw"], sv["merged"] = merge_out(
            cur, sv["o_na"], sv["o_sw"], sv["gt"], lw["wna"], lw["wsw"], lw["wout"], f"merge_out_{l}")
        sv["x2"] = cur
        dep = no_dep
        if l + 1 < depth:
            st_win, dep = start("win", l + 1, cur)
        sv["xn2"], sv["hg2"], sv["hu2"], sv["act2"] = ffn_up(cur, ffn2_norm[l][None], lw["wg2"], lw["wu2"], dep,
                                                             f"ffn2_up_{l}")
        dep = no_dep
        if l + 1 < depth:
            fwd, dep = arrive(st_ffn1, "ffn1", l + 1, sv["act2"])
        cur = ffn_down(cur, sv["act2"], lw["wd2"], dep, f"ffn2_down_{l}")
        dep = no_dep
        if l + 1 < depth:
            (za,) = finish(fwd, "ffn1", l + 1, cur)
            layer_w[l + 1] = dict(wg1=(za, 0), wu1=(za, 1), wd1=(za, 2))
        saved.append(sv)

    dx, loss_acc = loss_grad(cur, loss_target[0], "loss_grad")
    loss = lax.psum(jnp.sum(loss_acc) * (0.5 / d), ("x", "y", "c"))

    split = lambda t: t.reshape(N_DEV, t.shape[0] // N_DEV, t.shape[1])
    pending = {}
    last_key = "ffn1_0"
    small = {k: [None] * depth for k in SMALL_NAMES if k != "t5_rel_table"}
    dbias_sw = []
    for l in reversed(range(depth)):
        sv = saved[l]
        lw = layer_w[l]
        wg1, wu1, wd1, wg2, wu2, wd2 = (lw[k] for k in ("wg1", "wu1", "wd1", "wg2", "wu2", "wd2"))
        win_t, wout_l, wna_t, wsw_t = lw["win"], lw["wout"], lw["wna"], lw["wsw"]
        blocks = ((2, "x2", "xn2", "hg2", "hu2", "act2", wg2, wu2, wd2, "ffn2_norm", 3),
                  (1, "x0", "xn1", "hg1", "hu1", "act1", wg1, wu1, wd1, "ffn1_norm", 0))

        def ffn_backward(dx, blk):
            tag, xk, xnk, hgk, huk, actk, wg, wu, wd, norm_name, slot = blk
            gains = weights[norm_name]
            dxb, dhg, dhu = ffn_bwd_act(dx, wd, sv[hgk], sv[huk], f"ffn{tag}_bwd_act_{l}")
            gwd = tn_matmul(sv[actk], dxb, 0.5, f"ffn{tag}_dwd_{l}")
            gwg = tn_matmul(dhg, sv[xnk], 1.0, f"ffn{tag}_dwg_{l}")
            gwu = tn_matmul(dhu, sv[xnk], 1.0, f"ffn{tag}_dwu_{l}")
            key = f"ffn{tag}_{l}"
            blocks_of = [split(gwg), split(gwu), split(gwd)]
            if key == last_key:
                paired, token = pair_start(blocks_of, dxb, f"pair_{key}")
            else:
                pending[key], token = scatter_start([blocks_of], f"scatter_{key}")
            dx, dg = proj_bwd_norm([dhg, dhu], [wg, wu], sv[xk], gains[l][None], dx, token, f"ffn{tag}_bwd_x_{l}")
            if key == last_key:
                thru, land = pair_wait(paired, dx, f"pair_{key}_wait")
                pending[key], _ = chip_start(pair_sum(thru, land, f"pair_sum_{key}"), dg, f"chips_{key}")
            small[norm_name][l] = dg[0]
            return dx

        dx = ffn_backward(dx, blocks[0])
        dxb, dzg, da_na, da_sw, do_na, do_sw, dbg = mix_bwd_out(
            dx, sv["gt"], sv["a_na"], sv["a_sw"], wna_t, wsw_t, wout_l, f"mix_bwd_out_{l}")
        small["b_gate"][l] = dbg[0]
        gwout = tn_matmul(sv["merged"], dxb, 1.0, f"dwout_{l}")
        gwna = tn_matmul(da_na, sv["o_na"], 1.0, f"dwna_{l}")
        gwsw = tn_matmul(da_sw, sv["o_sw"], 1.0, f"dwsw_{l}")
        dqa, dka, dva, dt2 = na_bwd(sv["qa"], sv["ka"], sv["zq"], sv["t2"], sv["o_na"], do_na, f"na_bwd_{l}")
        dqs, dks, dvs, dbias, dsink = sw_bwd(sv["qs"], sv["ks"], sv["zq"], t5b, sw_sink[l], sv["o_sw"], do_sw,
                                             f"sw_bwd_{l}")
        dbias_sw.append(dbias.reshape(SW_HEADS, SW_BLOCK, 3 * SW_BLOCK))
        small["sw_sink"][l] = jnp.sum(dsink[:, 0].reshape(SW_HEADS, SW_BLOCK), axis=1)
        small["na_rpb"][l] = _rpb_from_rows(rpb_reduce(dt2, f"rpb_reduce_{l}"))
        dz, dgqa, dgka, dgqs, dgks = qk_norm_bwd(dqa, dka, dva, dqs, dks, dvs, sv["zq"], dzg, *sv["gains"], bd,
                                                 f"qk_norm_bwd_{l}")
        fold = lambda g: jnp.sum(g.reshape(-1, HEAD_DIM), axis=0)
        small["na_q_norm"][l], small["na_k_norm"][l] = fold(dgqa), fold(dgka)
        small["sw_q_norm"][l], small["sw_k_norm"][l] = fold(dgqs), fold(dgks)
        gwin = tn_matmul(dz, sv["hn"], 1.0, f"dwin_{l}")
        pending[f"mix_{l}"], token = scatter_start([[split(gwout)], [split(gwna), split(gwsw)], [split(gwin)]],
                                                   f"scatter_mix_{l}")
        dx, dg = proj_bwd_norm([dz], [win_t], sv["x1"], mix_norm[l][None], dx, token, f"mix_bwd_x_{l}")
        small["mix_norm"][l] = dg[0]
        dx = ffn_backward(dx, blocks[1])

    dtab = t5_reduce(dbias_sw, bmap, "t5_reduce")
    small_parts = {k: jnp.stack(v) for k, v in small.items()}
    small_parts["t5_rel_table"] = jnp.transpose(dtab[:, :, 0])
    small_packed = _pack_small(small_parts)

    summed = {}
    layers = lambda f: jnp.stack([f(l) for l in range(depth)])
    grads, delta, new_m, new_v = {}, {}, {}, {}

    def collect(key, after):
        if key == last_key:
            zones = [chip_wait(pending[key], after, f"wait_{key}")]
        else:
            zones = scatter_wait(pending[key], after, f"wait_{key}")
        summed[key] = [sum_sources(z, f"sum_{key}_{i}") for i, z in enumerate(zones)]

    def update(k, g, transposed):
        view = tr if transposed else (lambda t: t)
        d_k, m_k, v_k = adamw(view(weights[k]), g, view(mom_m[k]), view(mom_v[k]), f"adamw_{k}")
        grads[k], delta[k], new_m[k], new_v[k] = view(g), view(d_k), view(m_k), view(v_k)
        return d_k

    for key in pending:
        if key != last_key:
            collect(key, dx)
    update("ffn2_w_gate", layers(lambda l: summed[f"ffn2_{l}"][0][0]), True)
    update("ffn2_w_up", layers(lambda l: summed[f"ffn2_{l}"][0][1]), True)
    update("ffn2_w_down", layers(lambda l: summed[f"ffn2_{l}"][0][2]), False)
    update("w_out", layers(lambda l: summed[f"mix_{l}"][0][0]), False)
    update("w_branch_na", layers(lambda l: summed[f"mix_{l}"][1][0]), True)
    update("w_branch_sw", layers(lambda l: summed[f"mix_{l}"][1][1]), True)
    done = update("w_in", layers(lambda l: summed[f"mix_{l}"][2][0]), True)
    collect(last_key, done)
    update("ffn1_w_gate", layers(lambda l: summed[f"ffn1_{l}"][0][0]), True)
    update("ffn1_w_up", layers(lambda l: summed[f"ffn1_{l}"][0][1]), True)
    done = update("ffn1_w_down", layers(lambda l: summed[f"ffn1_{l}"][0][2]), False)
    rs_ = share_small(small_packed, done)
    g_s, d_s, m_s, v_s = adamw_small(_pack_small(weights), rs_, _pack_small(mom_m), _pack_small(mom_v), "adamw_small")
    for dst, packed in ((grads, g_s), (delta, d_s), (new_m, m_s), (new_v, v_s)):
        dst.update(_unpack_small(packed, weights))

    return (loss, dx[None], *[grads[k] for k in order], *[delta[k] for k in order],
            *[new_m[k] for k in order], *[new_v[k] for k in order])
```

```python
import functools
import math

import numpy as np
import jax
import jax.numpy as jnp
from jax import lax
from jax.experimental import pallas as pl
from jax.experimental.pallas import tpu as pltpu

F32 = jnp.float32
BF16 = jnp.bfloat16
MESH = pl.DeviceIdType.MESH

N_DEV = 8
EPS = 1e-6
NEG = -1e30
HEAD_DIM = 64
GRID_W = 64
NA_ROWS = 8
NA_COLS = 16
NA_WIDTH = 512
SW_Q_WIDTH = 512
SW_KV_WIDTH = 128
SW_BLOCK = 128
SW_HEADS = 8
SW_REP = 4
REL_BUCKETS = 32
REL_MAX_DIST = 128
QKV_WIDTH = 3 * NA_WIDTH + SW_Q_WIDTH + 2 * SW_KV_WIDTH
SCALE = 1.0 / math.sqrt(HEAD_DIM)

ADAM_LR = 0.001
ADAM_B1 = 0.9
ADAM_B2 = 0.999
ADAM_EPS = 1e-08
ADAM_WD = 0.01
ADAM_STEP = 10

V7X_VMEM_LIMIT = 56 * 1024 * 1024
LANES = 128
MXU_TILE = 256

NT = (((1,), (1,)), ((), ()))
TN = (((0,), (0,)), ((), ()))


def _params(n_grid=1):
    return pltpu.CompilerParams(dimension_semantics=("arbitrary",) * n_grid,
                                vmem_limit_bytes=V7X_VMEM_LIMIT)


def _row_tile(s):
    for t in (512, 256, 128, 64, 32, 16, 8):
        if s % t == 0:
            return t
    raise ValueError(s)


def _tn_tile(n):
    best = max(t for t in range(LANES, min(n, 2304) + 1, LANES) if n % t == 0) if n % LANES == 0 else n
    return best // 2 if best == n and n >= 1024 else best


ONCE = pl.Buffered(1)


def _col_chunk(n):
    return MXU_TILE if n % MXU_TILE == 0 else n


def _dot(a, b):
    return jnp.dot(a, b, preferred_element_type=F32)


def _dotg(a, b, dn):
    return lax.dot_general(a, b, dn, preferred_element_type=F32)


def _sigmoid(v):
    return 1.0 / (1.0 + jnp.exp(-v))


def _rstd(xv):
    return lax.rsqrt(jnp.mean(xv * xv, axis=-1, keepdims=True) + EPS)


def _full(shape):
    nd = len(shape)
    return pl.BlockSpec(shape, lambda i, _n=nd: (0,) * _n)


def _rows(tm, width):
    return pl.BlockSpec((tm, width), lambda i: (i, 0))


def _mat(stack, idx):
    return pl.BlockSpec((None,) + tuple(stack.shape[1:]), lambda i, _w=idx: (_w, 0, 0), pipeline_mode=ONCE)


def _group_mean(v, bd):
    hi = v.astype(BF16)
    lo = (v - hi.astype(F32)).astype(BF16)
    return _dot(hi, bd) + _dot(lo, bd)


def ffn_up(x, gain, wg_t, wu_t, dep, name):
    s, d = x.shape
    f = wg_t[0].shape[1]
    tm = _row_tile(s)
    fc = _col_chunk(f)

    def body(x_ref, g_ref, wg_ref, wu_ref, dep_ref, xn_ref, hg_ref, hu_ref, act_ref):
        xv = x_ref[...]
        xn = (xv * _rstd(xv) * g_ref[...]).astype(BF16)
        xn_ref[...] = xn
        for c0 in range(0, f, fc):
            hg = _dotg(xn, wg_ref[c0:c0 + fc, :], NT)
            hu = _dotg(xn, wu_ref[c0:c0 + fc, :], NT)
            hg_ref[:, c0:c0 + fc] = hg.astype(BF16)
            hu_ref[:, c0:c0 + fc] = hu.astype(BF16)
            act_ref[:, c0:c0 + fc] = (hg * _sigmoid(hg) * hu).astype(BF16)

    return pl.pallas_call(
        body, name=name, grid=(s // tm,),
        in_specs=[_rows(tm, d), _full((1, d)), _mat(*wg_t), _mat(*wu_t), _full(dep.shape)],
        out_specs=[_rows(tm, d), _rows(tm, f), _rows(tm, f), _rows(tm, f)],
        out_shape=[jax.ShapeDtypeStruct((s, d), BF16)] + [jax.ShapeDtypeStruct((s, f), BF16)] * 3,
        compiler_params=_params(),
    )(x, gain, wg_t[0], wu_t[0], dep)


def ffn_down(x, act, wd, dep, name):
    s, d = x.shape
    f = act.shape[1]
    tm = _row_tile(s)

    def body(x_ref, a_ref, w_ref, dep_ref, o_ref):
        o_ref[...] = x_ref[...] + 0.5 * _dot(a_ref[...], w_ref[...])

    return pl.pallas_call(
        body, name=name, grid=(s // tm,),
        in_specs=[_rows(tm, d), _rows(tm, f), _mat(*wd), _full(dep.shape)],
        out_specs=_rows(tm, d),
        out_shape=jax.ShapeDtypeStruct((s, d), F32),
        compiler_params=_params(),
    )(x, act, wd[0], dep)


def mix_in(x, gain, win_t, b_gate, gq_na, gk_na, gq_sw, gk_sw, bd, name):
    s, d = x.shape
    tm = _row_tile(s)
    gc = _col_chunk(2 * d)

    def body(x_ref, g_ref, w_ref, b_ref, gqa_ref, gka_ref, gqs_ref, gks_ref, bd_ref,
             hn_ref, zq_ref, qa_ref, ka_ref, qs_ref, ks_ref, gt_ref):
        xv = x_ref[...]
        hn = (xv * _rstd(xv) * g_ref[...]).astype(BF16)
        hn_ref[...] = hn

        def proj(c0, c1):
            return _dotg(hn, w_ref[c0:c1, :], NT)

        def headnorm(z, g, bdm):
            return z * lax.rsqrt(_group_mean(z * z, bdm) + EPS) * g

        bd512 = bd_ref[...]
        bd128 = bd_ref[0:SW_KV_WIDTH, 0:SW_KV_WIDTH]
        z = proj(0, 512)
        zq_ref[:, 0:512] = z.astype(BF16)
        qa_ref[...] = (headnorm(z, gqa_ref[...], bd512) * SCALE).astype(BF16)
        z = proj(512, 1024)
        zq_ref[:, 512:1024] = z.astype(BF16)
        ka_ref[...] = headnorm(z, gka_ref[...], bd512).astype(BF16)
        z = proj(1024, 1536)
        zq_ref[:, 1024:1536] = z.astype(BF16)
        z = proj(1536, 2048)
        zq_ref[:, 1536:2048] = z.astype(BF16)
        qs_ref[...] = (headnorm(z, gqs_ref[...], bd512) * SCALE).astype(BF16)
        z = proj(2048, 2176)
        zq_ref[:, 2048:2176] = z.astype(BF16)
        ks_ref[...] = headnorm(z, gks_ref[...], bd128).astype(BF16)
        z = proj(2176, 2304)
        zq_ref[:, 2176:2304] = z.astype(BF16)
        for c0 in range(0, 2 * d, gc):
            zg = proj(QKV_WIDTH + c0, QKV_WIDTH + c0 + gc) + b_ref[:, c0:c0 + gc]
            gt_ref[:, c0:c0 + gc] = _sigmoid(zg).astype(BF16)

    return pl.pallas_call(
        body, name=name, grid=(s // tm,),
        in_specs=[_rows(tm, d), _full((1, d)), _mat(*win_t), _full((1, 2 * d)),
                  _full((1, 512)), _full((1, 512)), _full((1, 512)), _full((1, 128)), _full((512, 512))],
        out_specs=[_rows(tm, d), _rows(tm, QKV_WIDTH), _rows(tm, 512), _rows(tm, 512), _rows(tm, 512),
                   _rows(tm, 128), _rows(tm, 2 * d)],
        out_shape=[jax.ShapeDtypeStruct((s, d), BF16), jax.ShapeDtypeStruct((s, QKV_WIDTH), BF16),
                   jax.ShapeDtypeStruct((s, 512), BF16), jax.ShapeDtypeStruct((s, 512), BF16),
                   jax.ShapeDtypeStruct((s, 512), BF16), jax.ShapeDtypeStruct((s, 128), BF16),
                   jax.ShapeDtypeStruct((s, 2 * d), BF16)],
        compiler_params=_params(),
    )(x, gain, win_t[0], b_gate, gq_na, gk_na, gq_sw, gk_sw, bd)


def _na_iotas():
    qc = lax.broadcasted_iota(jnp.int32, (GRID_W, LANES), 0)
    ln = lax.broadcasted_iota(jnp.int32, (GRID_W, LANES), 1)
    low = ln < GRID_W
    kc = jnp.where(low, ln, ln - GRID_W)
    diff = kc - qc + (NA_COLS - 1)
    qcs = jnp.clip(qc - NA_COLS // 2, 0, GRID_W - NA_COLS)
    inwin = (kc >= qcs) & (kc < qcs + NA_COLS)
    return diff, low, inwin


NA_RI = 2 * NA_ROWS - 1
NA_CI = 2 * NA_COLS - 1
NA_T2 = NA_RI + 1


def _rpb_rows(rpb):
    h = rpb.shape[0]
    padded = jnp.pad(rpb, ((0, 0), (1, 1), (0, GRID_W - NA_CI)))
    return jnp.concatenate([padded[:, :NA_T2], padded[:, 1:NA_T2 + 1]], axis=2).reshape(h, NA_T2, LANES)


def _rpb_from_rows(rows):
    return rows[:, 1:, :NA_CI] + rows[:, :NA_RI, GRID_W:GRID_W + NA_CI]


def rpb_expand(rows, dep, name):
    n_heads = rows.shape[0]

    def body(r_ref, dep_ref, o_ref):
        for h in range(n_heads):
            for e in range(NA_T2):
                line = jnp.broadcast_to(r_ref[h, e:e + 1, :], (GRID_W, LANES))
                o_ref[h, e] = pltpu.roll(line, LANES - (NA_COLS - 1), 1, stride=1, stride_axis=0)

    return pl.pallas_call(
        body, name=name,
        in_specs=[pl.BlockSpec(memory_space=pltpu.VMEM), pl.BlockSpec(memory_space=pltpu.VMEM)],
        out_specs=pl.BlockSpec(memory_space=pltpu.VMEM),
        out_shape=jax.ShapeDtypeStruct((n_heads, NA_T2, GRID_W, LANES), F32),
        compiler_params=pltpu.CompilerParams(vmem_limit_bytes=V7X_VMEM_LIMIT),
    )(rows, dep)


def rpb_reduce(dt2, name):
    n_heads = dt2.shape[0]
    flip = jnp.asarray(np.eye(GRID_W)[::-1], BF16)

    def body(d_ref, j_ref, o_ref):
        jm = j_ref[...]
        for h in range(n_heads):
            for e in range(NA_T2):
                dv = d_ref[h, e]
                hi = dv.astype(BF16)
                mid = (dv - hi.astype(F32)).astype(BF16)
                lo = (dv - hi.astype(F32) - mid.astype(F32)).astype(BF16)
                rev = _dot(jm, hi) + _dot(jm, mid) + _dot(jm, lo)
                back = pltpu.roll(rev, LANES + (NA_COLS - 1) - (GRID_W - 1), 1, stride=1, stride_axis=0)
                o_ref[h, e:e + 1, :] = jnp.sum(back, axis=0, keepdims=True)

    return pl.pallas_call(
        body, name=name,
        in_specs=[pl.BlockSpec(memory_space=pltpu.VMEM)] * 2,
        out_specs=pl.BlockSpec(memory_space=pltpu.VMEM),
        out_shape=jax.ShapeDtypeStruct((n_heads, NA_T2, LANES), F32),
        compiler_params=pltpu.CompilerParams(vmem_limit_bytes=V7X_VMEM_LIMIT),
    )(dt2, flip)


NA_TQ = 4
NA_TK = NA_TQ + NA_ROWS
NA_KCH = NA_TK // 2


def _na_tile_geometry(t, rows):
    r = t * NA_TQ
    kbase = jnp.clip(r - NA_ROWS // 2, 0, rows - NA_TK)
    starts = [jnp.clip(r + a - NA_ROWS // 2, 0, rows - NA_ROWS) for a in range(NA_TQ)]
    return r, kbase, starts


def _na_tile_mask(kbase, starts, low, inwin):
    half = jnp.where(low, 0, 1)
    cols = []
    for c in range(NA_KCH):
        krow = kbase + 2 * c + half
        cols.append(jnp.concatenate(
            [jnp.where(inwin & (krow >= st) & (krow < st + NA_ROWS), 0.0, NEG) for st in starts], axis=0))
    return jnp.concatenate(cols, axis=1)


def _na_tile_index(r, kbase, a, c):
    return jnp.clip(kbase + 2 * c - (r + a) + NA_ROWS, 0, NA_T2 - 1)


def _na_tile_scores(q, k, t2_ref, hh, r, kbase, madd):
    bias = jnp.concatenate(
        [jnp.concatenate([t2_ref[hh, _na_tile_index(r, kbase, a, c)] for a in range(NA_TQ)], axis=0)
         for c in range(NA_KCH)], axis=1)
    return _dotg(q, k, NT) + bias + madd


def _softmax_rows(sc):
    e = jnp.exp(sc - jnp.max(sc, axis=1, keepdims=True))
    return e * (1.0 / jnp.sum(e, axis=1, keepdims=True))


def na_fwd(qa, ka, zq, t2, name):
    s = qa.shape[0]
    rows = s // GRID_W
    n_pairs = NA_WIDTH // LANES
    v_blk0 = (2 * NA_WIDTH) // LANES

    assert rows % NA_TQ == 0 and rows >= NA_TK
    tq, tk = NA_TQ * GRID_W, NA_TK * GRID_W

    def body(q_ref, k_ref, v_ref, t2_ref, o_ref, s_scr, p_scr):
        _, low, inwin = _na_iotas()

        def tile(t, carry):
            r, kbase, starts = _na_tile_geometry(t, rows)
            madd = _na_tile_mask(kbase, starts, low, inwin)
            qr = pl.ds(pl.multiple_of(r * GRID_W, tq), tq)
            kr = pl.ds(pl.multiple_of(kbase * GRID_W, tq), tk)
            for hh in range(2):
                lanes = slice(HEAD_DIM * hh, HEAD_DIM * (hh + 1))
                s_scr[tq * hh:tq * (hh + 1), :] = _na_tile_scores(q_ref[qr, lanes], k_ref[kr, lanes], t2_ref, hh, r,
                                                                  kbase, madd)
            p_scr[...] = _softmax_rows(s_scr[...]).astype(BF16)
            for hh in range(2):
                lanes = slice(HEAD_DIM * hh, HEAD_DIM * (hh + 1))
                o_ref[qr, lanes] = _dot(p_scr[tq * hh:tq * (hh + 1), :], v_ref[kr, lanes]).astype(BF16)
            return carry

        lax.fori_loop(0, rows // NA_TQ, tile, 0)

    col = lambda off: pl.BlockSpec((s, LANES), lambda p, _o=off: (0, _o + p))
    return pl.pallas_call(
        body, name=name, grid=(n_pairs,),
        in_specs=[col(0), col(0), col(v_blk0),
                  pl.BlockSpec((2, NA_T2, GRID_W, LANES), lambda p: (p, 0, 0, 0))],
        out_specs=col(0),
        out_shape=jax.ShapeDtypeStruct((s, NA_WIDTH), BF16),
        scratch_shapes=[pltpu.VMEM((2 * tq, tk), F32), pltpu.VMEM((2 * tq, tk), BF16)],
        compiler_params=_params(),
    )(qa, ka, zq, t2)


def na_bwd(qa, ka, zq, t2, o_na, do_na, name):
    s = qa.shape[0]
    rows = s // GRID_W
    n_pairs = NA_WIDTH // LANES
    v_blk0 = (2 * NA_WIDTH) // LANES

    tq, tk = NA_TQ * GRID_W, NA_TK * GRID_W

    def body(q_ref, k_ref, v_ref, t2_ref, o_ref, do_ref, dq_ref, dk_ref, dv_ref, dt2_ref):
        _, low, inwin = _na_iotas()
        dk_ref[...] = jnp.zeros(dk_ref.shape, F32)
        dv_ref[...] = jnp.zeros(dv_ref.shape, F32)
        dt2_ref[...] = jnp.zeros(dt2_ref.shape, F32)

        def tile(t, carry):
            r, kbase, starts = _na_tile_geometry(t, rows)
            madd = _na_tile_mask(kbase, starts, low, inwin)
            qr = pl.ds(pl.multiple_of(r * GRID_W, tq), tq)
            kr = pl.ds(pl.multiple_of(kbase * GRID_W, tq), tk)
            for hh in range(2):
                lanes = slice(HEAD_DIM * hh, HEAD_DIM * (hh + 1))
                q, k, v = q_ref[qr, lanes], k_ref[kr, lanes], v_ref[kr, lanes]
                p = _softmax_rows(_na_tile_scores(q, k, t2_ref, hh, r, kbase, madd))
                do = do_ref[qr, lanes]
                delta = jnp.sum(do.astype(F32) * o_ref[qr, lanes].astype(F32), axis=1, keepdims=True)
                ds = p * (_dotg(do, v, NT) - delta)
                for a in range(NA_TQ):
                    for c in range(NA_KCH):
                        e = _na_tile_index(r, kbase, a, c)
                        dt2_ref[hh, e] = dt2_ref[hh, e] + ds[GRID_W * a:GRID_W * (a + 1), LANES * c:LANES * (c + 1)]
                dsb = ds.astype(BF16)
                dq_ref[qr, lanes] = _dot(dsb, k)
                dk_ref[kr, lanes] = dk_ref[kr, lanes] + _dotg(dsb, q, TN)
                dv_ref[kr, lanes] = dv_ref[kr, lanes] + _dotg(p.astype(BF16), do, TN)
            return carry

        lax.fori_loop(0, rows // NA_TQ, tile, 0)

    col = lambda off: pl.BlockSpec((s, LANES), lambda p, _o=off: (0, _o + p))
    t2spec = pl.BlockSpec((2, NA_T2, GRID_W, LANES), lambda p: (p, 0, 0, 0))
    return pl.pallas_call(
        body, name=name, grid=(n_pairs,),
        in_specs=[col(0), col(0), col(v_blk0), t2spec, col(0), col(0)],
        out_specs=[col(0), col(0), col(0), t2spec],
        out_shape=[jax.ShapeDtypeStruct((s, NA_WIDTH), F32)] * 3 + [jax.ShapeDtypeStruct(t2.shape, F32)],
        compiler_params=_params(),
    )(qa, ka, zq, t2, o_na, do_na)


def _t5_bucket_map():
    rel = np.arange(3 * SW_BLOCK)[None, :] - SW_BLOCK - np.arange(SW_BLOCK)[:, None]
    nb = REL_BUCKETS // 2
    max_exact = nb // 2
    n = np.abs(rel)
    large = max_exact + (np.log(np.maximum(n, 1) / max_exact)
                         / np.log(REL_MAX_DIST / max_exact) * (nb - max_exact)).astype(np.int32)
    large = np.minimum(large, nb - 1)
    return ((rel > 0) * nb + np.where(n < max_exact, n, large)).astype(np.int32)


def t5_expand(table, bmap, dep, name):
    def body(tab_ref, bm_ref, dep_ref, o_ref):
        bm = bm_ref[...]
        for h in range(SW_HEADS):
            t = jnp.zeros(bm.shape, F32)
            for b in range(REL_BUCKETS):
                t = jnp.where(bm == b, tab_ref[b, h], t)
            o_ref[h] = t

    return pl.pallas_call(
        body, name=name,
        in_specs=[pl.BlockSpec(memory_space=pltpu.SMEM), pl.BlockSpec(memory_space=pltpu.VMEM),
                  pl.BlockSpec(memory_space=pltpu.VMEM)],
        out_specs=pl.BlockSpec(memory_space=pltpu.VMEM),
        out_shape=jax.ShapeDtypeStruct((SW_HEADS,) + bmap.shape, F32),
        compiler_params=pltpu.CompilerParams(vmem_limit_bytes=V7X_VMEM_LIMIT),
    )(table, bmap, dep)


def t5_reduce(dbias_list, bmap, name):
    n = len(dbias_list)

    def body(*refs):
        d_refs, bm_ref, o_ref = refs[:n], refs[n], refs[n + 1]
        bm = bm_ref[...]
        for h in range(SW_HEADS):
            dv = d_refs[0][h]
            for other in d_refs[1:]:
                dv = dv + other[h]
            rows = [jnp.sum(jnp.where(bm == b, dv, 0.0), axis=0, keepdims=True) for b in range(REL_BUCKETS)]
            r = jnp.concatenate(rows, axis=0)
            o_ref[h] = jnp.broadcast_to(jnp.sum(r, axis=1, keepdims=True), (REL_BUCKETS, LANES))

    return pl.pallas_call(
        body, name=name,
        in_specs=[pl.BlockSpec(memory_space=pltpu.VMEM)] * (n + 1),
        out_specs=pl.BlockSpec(memory_space=pltpu.VMEM),
        out_shape=jax.ShapeDtypeStruct((SW_HEADS, REL_BUCKETS, LANES), F32),
        compiler_params=pltpu.CompilerParams(vmem_limit_bytes=V7X_VMEM_LIMIT),
    )(*dbias_list, bmap)


def _sw_mask_iotas():
    a = lax.broadcasted_iota(jnp.int32, (SW_BLOCK, 3 * SW_BLOCK), 0)
    j = lax.broadcasted_iota(jnp.int32, (SW_BLOCK, 3 * SW_BLOCK), 1)
    inwin = jnp.abs(j - SW_BLOCK - a) <= SW_BLOCK
    return j, inwin


SW_STACK = SW_HEADS * SW_BLOCK


def _sw_softmax(sc, sk):
    m = jnp.maximum(jnp.max(sc, axis=1, keepdims=True), sk)
    e = jnp.exp(sc - m)
    es = jnp.exp(sk - m)
    inv = 1.0 / (jnp.sum(e, axis=1, keepdims=True) + es)
    return e * inv, es * inv


def _sw_prologue(k_ref, v_ref, kp, vp, sink_ref, s):
    pad = s + 2 * SW_BLOCK
    zeros = jnp.zeros((SW_BLOCK, SW_KV_WIDTH), BF16)
    kp[0:SW_BLOCK, :] = zeros
    vp[0:SW_BLOCK, :] = zeros
    kp[SW_BLOCK + s:pad, :] = zeros
    vp[SW_BLOCK + s:pad, :] = zeros
    kp[SW_BLOCK:SW_BLOCK + s, :] = k_ref[...]
    vp[SW_BLOCK:SW_BLOCK + s, :] = v_ref[...]
    return jnp.concatenate([jnp.full((SW_BLOCK, 1), sink_ref[h], F32) for h in range(SW_HEADS)], axis=0)


def sw_fwd(qs, ks, zq, t5b, sink, dep, name):
    s = qs.shape[0]
    nb = s // SW_BLOCK
    v_blk = (3 * NA_WIDTH + SW_Q_WIDTH + SW_KV_WIDTH) // LANES
    pad = s + 2 * SW_BLOCK

    def body(q_ref, k_ref, v_ref, b_ref, sink_ref, dep_ref, o_ref, kp, vp, s_scr, p_scr):
        sink_col = _sw_prologue(k_ref, v_ref, kp, vp, sink_ref, s)
        j, inwin = _sw_mask_iotas()

        def blk(n, carry):
            kpos = n * SW_BLOCK - SW_BLOCK + j
            madd = jnp.where(inwin & (kpos >= 0) & (kpos < s), 0.0, NEG)
            q0 = pl.multiple_of(n * SW_BLOCK, SW_BLOCK)
            qr, kr = pl.ds(q0, SW_BLOCK), pl.ds(q0, 3 * SW_BLOCK)
            for h in range(SW_HEADS):
                g = h // SW_REP
                s_scr[SW_BLOCK * h:SW_BLOCK * (h + 1), :] = _dotg(
                    q_ref[qr, HEAD_DIM * h:HEAD_DIM * (h + 1)], kp[kr, HEAD_DIM * g:HEAD_DIM * (g + 1)], NT) + madd
            p, _ = _sw_softmax(s_scr[...] + b_ref[...], sink_col)
            p_scr[...] = p.astype(BF16)
            for h in range(SW_HEADS):
                g = h // SW_REP
                o_ref[qr, HEAD_DIM * h:HEAD_DIM * (h + 1)] = _dot(
                    p_scr[SW_BLOCK * h:SW_BLOCK * (h + 1), :], vp[kr, HEAD_DIM * g:HEAD_DIM * (g + 1)]).astype(BF16)
            return carry

        lax.fori_loop(0, nb, blk, 0)

    return pl.pallas_call(
        body, name=name, grid=(1,),
        in_specs=[_full((s, SW_Q_WIDTH)), _full((s, SW_KV_WIDTH)),
                  pl.BlockSpec((s, SW_KV_WIDTH), lambda i: (0, v_blk)),
                  _full((SW_STACK, 3 * SW_BLOCK)), pl.BlockSpec(memory_space=pltpu.SMEM),
                  _full(dep.shape)],
        out_specs=_full((s, SW_Q_WIDTH)),
        out_shape=jax.ShapeDtypeStruct((s, SW_Q_WIDTH), BF16),
        scratch_shapes=[pltpu.VMEM((pad, SW_KV_WIDTH), BF16), pltpu.VMEM((pad, SW_KV_WIDTH), BF16),
                        pltpu.VMEM((SW_STACK, 3 * SW_BLOCK), F32), pltpu.VMEM((SW_STACK, 3 * SW_BLOCK), BF16)],
        compiler_params=_params(),
    )(qs, ks, zq, t5b, sink, dep)


def sw_bwd(qs, ks, zq, t5b, sink, o_sw, do_sw, name):
    s = qs.shape[0]
    nb = s // SW_BLOCK
    v_blk = (3 * NA_WIDTH + SW_Q_WIDTH + SW_KV_WIDTH) // LANES
    pad = s + 2 * SW_BLOCK

    def body(q_ref, k_ref, v_ref, b_ref, sink_ref, o_ref, do_ref,
             dq_ref, dk_ref, dv_ref, db_ref, dsk_ref, kp, vp, dkp, dvp, s_scr, dp_scr, ds_scr, p_scr):
        sink_col = _sw_prologue(k_ref, v_ref, kp, vp, sink_ref, s)
        dkp[...] = jnp.zeros(dkp.shape, F32)
        dvp[...] = jnp.zeros(dvp.shape, F32)
        db_ref[...] = jnp.zeros(db_ref.shape, F32)
        dsk_ref[...] = jnp.zeros(dsk_ref.shape, F32)
        j, inwin = _sw_mask_iotas()

        def blk(n, carry):
            kpos = n * SW_BLOCK - SW_BLOCK + j
            madd = jnp.where(inwin & (kpos >= 0) & (kpos < s), 0.0, NEG)
            q0 = pl.multiple_of(n * SW_BLOCK, SW_BLOCK)
            qr, kr = pl.ds(q0, SW_BLOCK), pl.ds(q0, 3 * SW_BLOCK)
            deltas = []
            for h in range(SW_HEADS):
                g = h // SW_REP
                hl, kl = slice(HEAD_DIM * h, HEAD_DIM * (h + 1)), slice(HEAD_DIM * g, HEAD_DIM * (g + 1))
                rows = slice(SW_BLOCK * h, SW_BLOCK * (h + 1))
                do = do_ref[qr, hl]
                s_scr[rows, :] = _dotg(q_ref[qr, hl], kp[kr, kl], NT) + madd
                dp_scr[rows, :] = _dotg(do, vp[kr, kl], NT)
                deltas.append(jnp.sum(do.astype(F32) * o_ref[qr, hl].astype(F32), axis=1, keepdims=True))
            delta = jnp.concatenate(deltas, axis=0)
            p, ps = _sw_softmax(s_scr[...] + b_ref[...], sink_col)
            ds = p * (dp_scr[...] - delta)
            db_ref[...] = db_ref[...] + ds
            dsk_ref[...] = dsk_ref[...] - jnp.broadcast_to(ps * delta, (SW_STACK, LANES))
            ds_scr[...] = ds.astype(BF16)
            p_scr[...] = p.astype(BF16)
            for g in range(SW_HEADS // SW_REP):
                kl = slice(HEAD_DIM * g, HEAD_DIM * (g + 1))
                k = kp[kr, kl]
                dkw = jnp.zeros((3 * SW_BLOCK, HEAD_DIM), F32)
                dvw = jnp.zeros((3 * SW_BLOCK, HEAD_DIM), F32)
                for r in range(SW_REP):
                    h = g * SW_REP + r
                    hl, rows = slice(HEAD_DIM * h, HEAD_DIM * (h + 1)), slice(SW_BLOCK * h, SW_BLOCK * (h + 1))
                    dsb = ds_scr[rows, :]
                    dq_ref[qr, hl] = _dot(dsb, k)
                    dkw = dkw + _dotg(dsb, q_ref[qr, hl], TN)
                    dvw = dvw + _dotg(p_scr[rows, :], do_ref[qr, hl], TN)
                dkp[kr, kl] = dkp[kr, kl] + dkw
                dvp[kr, kl] = dvp[kr, kl] + dvw
            return carry

        lax.fori_loop(0, nb, blk, 0)
        dk_ref[...] = dkp[SW_BLOCK:SW_BLOCK + s, :]
        dv_ref[...] = dvp[SW_BLOCK:SW_BLOCK + s, :]

    bias_spec = _full((SW_STACK, 3 * SW_BLOCK))
    return pl.pallas_call(
        body, name=name, grid=(1,),
        in_specs=[_full((s, SW_Q_WIDTH)), _full((s, SW_KV_WIDTH)),
                  pl.BlockSpec((s, SW_KV_WIDTH), lambda i: (0, v_blk)),
                  bias_spec, pl.BlockSpec(memory_space=pltpu.SMEM),
                  _full((s, SW_Q_WIDTH)), _full((s, SW_Q_WIDTH))],
        out_specs=[_full((s, SW_Q_WIDTH)), _full((s, SW_KV_WIDTH)), _full((s, SW_KV_WIDTH)), bias_spec,
                   _full((SW_STACK, LANES))],
        out_shape=[jax.ShapeDtypeStruct((s, SW_Q_WIDTH), F32), jax.ShapeDtypeStruct((s, SW_KV_WIDTH), F32),
                   jax.ShapeDtypeStruct((s, SW_KV_WIDTH), F32),
                   jax.ShapeDtypeStruct((SW_STACK, 3 * SW_BLOCK), F32),
                   jax.ShapeDtypeStruct((SW_STACK, LANES), F32)],
        scratch_shapes=[pltpu.VMEM((pad, SW_KV_WIDTH), BF16), pltpu.VMEM((pad, SW_KV_WIDTH), BF16),
                        pltpu.VMEM((pad, SW_KV_WIDTH), F32), pltpu.VMEM((pad, SW_KV_WIDTH), F32),
                        pltpu.VMEM((SW_STACK, 3 * SW_BLOCK), F32), pltpu.VMEM((SW_STACK, 3 * SW_BLOCK), F32),
                        pltpu.VMEM((SW_STACK, 3 * SW_BLOCK), BF16), pltpu.VMEM((SW_STACK, 3 * SW_BLOCK), BF16)],
        compiler_params=_params(),
    )(qs, ks, zq, t5b, sink, o_sw, do_sw)


def merge_out(x, o_na, o_sw, gt, wbna_t, wbsw_t, wout, name):
    s, d = x.shape
    tm = _row_tile(s)

    def body(x_ref, ona_ref, osw_ref, gt_ref, wna_ref, wsw_ref, wo_ref, xo_ref, ana_ref, asw_ref, mg_ref):
        a_na = _dotg(ona_ref[...], wna_ref[...], NT)
        a_sw = _dotg(osw_ref[...], wsw_ref[...], NT)
        ana_ref[...] = a_na.astype(BF16)
        asw_ref[...] = a_sw.astype(BF16)
        merged = (gt_ref[:, 0:d].astype(F32) * a_na + gt_ref[:, d:2 * d].astype(F32) * a_sw).astype(BF16)
        mg_ref[...] = merged
        xo_ref[...] = x_ref[...] + _dot(merged, wo_ref[...])

    return pl.pallas_call(
        body, name=name, grid=(s // tm,),
        in_specs=[_rows(tm, d), _rows(tm, 512), _rows(tm, 512), _rows(tm, 2 * d),
                  _mat(*wbna_t), _mat(*wbsw_t), _mat(*wout)],
        out_specs=[_rows(tm, d)] * 4,
        out_shape=[jax.ShapeDtypeStruct((s, d), F32)] + [jax.ShapeDtypeStruct((s, d), BF16)] * 3,
        compiler_params=_params(),
    )(x, o_na, o_sw, gt, wbna_t[0], wbsw_t[0], wout[0])


def mix_bwd_out(dx, gt, a_na, a_sw, wbna_t, wbsw_t, wout, name):
    s, d = dx.shape
    tm = _row_tile(s)

    def body(dx_ref, gt_ref, ana_ref, asw_ref, wna_ref, wsw_ref, wo_ref,
             dxb_ref, dzg_ref, dana_ref, dasw_ref, dona_ref, dosw_ref, dbg_ref):
        @pl.when(pl.program_id(0) == 0)
        def _():
            dbg_ref[...] = jnp.zeros(dbg_ref.shape, F32)

        dxb = dx_ref[...].astype(BF16)
        dxb_ref[...] = dxb
        dm = _dotg(dxb, wo_ref[...], NT)
        for i, (a_ref, da_ref, w_ref, do_ref) in enumerate(
                [(ana_ref, dana_ref, wna_ref, dona_ref), (asw_ref, dasw_ref, wsw_ref, dosw_ref)]):
            gi = gt_ref[:, i * d:(i + 1) * d].astype(F32)
            da = (dm * gi).astype(BF16)
            da_ref[...] = da
            do_ref[...] = _dot(da, w_ref[...]).astype(BF16)
            dzg = dm * a_ref[...].astype(F32) * gi * (1.0 - gi)
            dzg_ref[:, i * d:(i + 1) * d] = dzg.astype(BF16)
            dbg_ref[:, i * d:(i + 1) * d] = dbg_ref[:, i * d:(i + 1) * d] + jnp.sum(dzg, axis=0, keepdims=True)

    return pl.pallas_call(
        body, name=name, grid=(s // tm,),
        in_specs=[_rows(tm, d), _rows(tm, 2 * d), _rows(tm, d), _rows(tm, d),
                  _mat(*wbna_t), _mat(*wbsw_t), _mat(*wout)],
        out_specs=[_rows(tm, d), _rows(tm, 2 * d), _rows(tm, d), _rows(tm, d), _rows(tm, 512), _rows(tm, 512),
                   _full((1, 2 * d))],
        out_shape=[jax.ShapeDtypeStruct((s, d), BF16), jax.ShapeDtypeStruct((s, 2 * d), BF16),
                   jax.ShapeDtypeStruct((s, d), BF16), jax.ShapeDtypeStruct((s, d), BF16),
                   jax.ShapeDtypeStruct((s, 512), BF16), jax.ShapeDtypeStruct((s, 512), BF16),
                   jax.ShapeDtypeStruct((1, 2 * d), F32)],
        compiler_params=_params(),
    )(dx, gt, a_na, a_sw, wbna_t[0], wbsw_t[0], wout[0])


def qk_norm_bwd(dqa, dka, dva, dqs, dks, dvs, zq, dzg, gq_na, gk_na, gq_sw, gk_sw, bd, name):
    s = zq.shape[0]
    d2 = dzg.shape[1]
    n_in = QKV_WIDTH + d2
    tm = _row_tile(s)

    def body(dqa_ref, dka_ref, dva_ref, dqs_ref, dks_ref, dvs_ref, zq_ref, dzg_ref,
             gqa_ref, gka_ref, gqs_ref, gks_ref, bd_ref, dz_ref, dgqa_ref, dgka_ref, dgqs_ref, dgks_ref):
        @pl.when(pl.program_id(0) == 0)
        def _():
            for r in (dgqa_ref, dgka_ref, dgqs_ref, dgks_ref):
                r[...] = jnp.zeros(r.shape, F32)

        bd512 = bd_ref[...]
        bd128 = bd_ref[0:SW_KV_WIDTH, 0:SW_KV_WIDTH]

        def one(c0, c1, dy_ref, g_ref, dg_ref, bdm, scale):
            z = zq_ref[:, c0:c1].astype(F32)
            r = lax.rsqrt(_group_mean(z * z, bdm) + EPS)
            zh = z * r
            dy = dy_ref[...] * scale
            dyg = dy * g_ref[...]
            dz = r * (dyg - zh * _group_mean(dyg * zh, bdm))
            dz_ref[:, c0:c1] = dz.astype(BF16)
            dg_ref[...] = dg_ref[...] + jnp.sum(dy * zh, axis=0, keepdims=True)

        one(0, 512, dqa_ref, gqa_ref, dgqa_ref, bd512, SCALE)
        one(512, 1024, dka_ref, gka_ref, dgka_ref, bd512, 1.0)
        dz_ref[:, 1024:1536] = dva_ref[...].astype(BF16)
        one(1536, 2048, dqs_ref, gqs_ref, dgqs_ref, bd512, SCALE)
        one(2048, 2176, dks_ref, gks_ref, dgks_ref, bd128, 1.0)
        dz_ref[:, 2176:2304] = dvs_ref[...].astype(BF16)
        dz_ref[:, QKV_WIDTH:n_in] = dzg_ref[...]

    return pl.pallas_call(
        body, name=name, grid=(s // tm,),
        in_specs=[_rows(tm, 512), _rows(tm, 512), _rows(tm, 512), _rows(tm, 512), _rows(tm, 128), _rows(tm, 128),
                  _rows(tm, QKV_WIDTH), _rows(tm, d2),
                  _full((1, 512)), _full((1, 512)), _full((1, 512)), _full((1, 128)), _full((512, 512))],
        out_specs=[_rows(tm, n_in), _full((1, 512)), _full((1, 512)), _full((1, 512)), _full((1, 128))],
        out_shape=[jax.ShapeDtypeStruct((s, n_in), BF16)] + [jax.ShapeDtypeStruct((1, 512), F32)] * 3
                  + [jax.ShapeDtypeStruct((1, 128), F32)],
        compiler_params=_params(),
    )(dqa, dka, dva, dqs, dks, dvs, zq, dzg, gq_na, gk_na, gq_sw, gk_sw, bd)


def ffn_bwd_act(dx, wd, hg, hu, name):
    s, d = dx.shape
    f = wd[0].shape[1]
    tm = _row_tile(s)
    fc = _col_chunk(f)

    def body(dx_ref, w_ref, hg_ref, hu_ref, dxb_ref, dhg_ref, dhu_ref):
        dxb = dx_ref[...].astype(BF16)
        dxb_ref[...] = dxb
        for c0 in range(0, f, fc):
            dact = 0.5 * _dotg(dxb, w_ref[c0:c0 + fc, :], NT)
            hg = hg_ref[:, c0:c0 + fc].astype(F32)
            hu = hu_ref[:, c0:c0 + fc].astype(F32)
            sg = _sigmoid(hg)
            dhu_ref[:, c0:c0 + fc] = (dact * hg * sg).astype(BF16)
            dhg_ref[:, c0:c0 + fc] = (dact * hu * sg * (1.0 + hg * (1.0 - sg))).astype(BF16)

    return pl.pallas_call(
        body, name=name, grid=(s // tm,),
        in_specs=[_rows(tm, d), _mat(*wd), _rows(tm, f), _rows(tm, f)],
        out_specs=[_rows(tm, d), _rows(tm, f), _rows(tm, f)],
        out_shape=[jax.ShapeDtypeStruct((s, d), BF16), jax.ShapeDtypeStruct((s, f), BF16),
                   jax.ShapeDtypeStruct((s, f), BF16)],
        compiler_params=_params(),
    )(dx, wd[0], hg, hu)


def proj_bwd_norm(acts, weights, x, gain, dx, dep, name):
    s, d = x.shape
    tm = _row_tile(s)
    n = len(acts)

    def body(*refs):
        a_refs, w_refs = refs[:n], refs[n:2 * n]
        x_ref, g_ref, dx_ref, _, o_ref, dg_ref = refs[2 * n:]

        @pl.when(pl.program_id(0) == 0)
        def _():
            dg_ref[...] = jnp.zeros(dg_ref.shape, F32)

        dxn = _dot(a_refs[0][...], w_refs[0][...])
        for a_ref, w_ref in zip(a_refs[1:], w_refs[1:]):
            dxn = dxn + _dot(a_ref[...], w_ref[...])
        xv = x_ref[...]
        r = _rstd(xv)
        xh = xv * r
        dxh = dxn * g_ref[...]
        o_ref[...] = dx_ref[...] + r * (dxh - xh * jnp.mean(dxh * xh, axis=-1, keepdims=True))
        dg_ref[...] = dg_ref[...] + jnp.sum(dxn * xh, axis=0, keepdims=True)

    return pl.pallas_call(
        body, name=name, grid=(s // tm,),
        in_specs=[_rows(tm, a.shape[1]) for a in acts] + [_mat(*w) for w in weights]
                 + [_rows(tm, d), _full((1, d)), _rows(tm, d), _full(dep.shape)],
        out_specs=[_rows(tm, d), _full((1, d))],
        out_shape=[jax.ShapeDtypeStruct((s, d), F32), jax.ShapeDtypeStruct((1, d), F32)],
        compiler_params=_params(),
    )(*acts, *[w[0] for w in weights], x, gain, dx, dep)


def tn_matmul(a, b, scale, name):
    s, n = a.shape
    k = b.shape[1]
    tn = _tn_tile(n)

    def body(a_ref, b_ref, o_ref):
        o_ref[...] = (scale * _dotg(a_ref[...], b_ref[...], TN)).astype(BF16)

    return pl.pallas_call(
        body, name=name, grid=(n // tn,),
        in_specs=[pl.BlockSpec((s, tn), lambda i: (0, i)),
                  pl.BlockSpec((s, k), lambda i: (0, 0), pipeline_mode=ONCE)],
        out_specs=pl.BlockSpec((tn, k), lambda i: (i, 0)),
        out_shape=jax.ShapeDtypeStruct((n, k), BF16),
        compiler_params=_params(),
    )(a, b)


def loss_grad(y, target, name):
    s, d = y.shape
    tm = _row_tile(s)

    def body(y_ref, t_ref, dy_ref, acc_ref):
        @pl.when(pl.program_id(0) == 0)
        def _():
            acc_ref[...] = jnp.zeros(acc_ref.shape, F32)

        err = y_ref[...] - t_ref[...]
        dy_ref[...] = err * (1.0 / d)
        e2 = err * err
        part = jnp.sum(e2.reshape(tm // 8, 8, d), axis=0)
        acc = part[:, 0:LANES]
        for c0 in range(LANES, d, LANES):
            acc = acc + part[:, c0:c0 + LANES]
        acc_ref[...] = acc_ref[...] + acc

    return pl.pallas_call(
        body, name=name, grid=(s // tm,),
        in_specs=[_rows(tm, d), _rows(tm, d)],
        out_specs=[_rows(tm, d), _full((8, LANES))],
        out_shape=[jax.ShapeDtypeStruct((s, d), F32), jax.ShapeDtypeStruct((8, LANES), F32)],
        compiler_params=_params(),
    )(y, target)


def _mesh_pos():
    return lax.axis_index("x"), lax.axis_index("y"), lax.axis_index("c")


def _peers():
    x, y, c = _mesh_pos()
    peers = []
    for rel in range(1, N_DEV):
        peers.append((1 - x if rel & 4 else x, 1 - y if rel & 2 else y, 1 - c if rel & 1 else c))
    return 4 * x + 2 * y + c, peers


HBM_SPEC = pl.BlockSpec(memory_space=pltpu.HBM)
SEM_SPEC = pl.BlockSpec(memory_space=pltpu.SEMAPHORE)


def _split_call(body, name, thru, n_sems, extra=(), with_token=True):
    hbm = lambda t: pltpu.with_memory_space_constraint(t, pltpu.HBM)
    effect = pltpu.CompilerParams(has_side_effects=pltpu.SideEffectType.DATAFLOW_SIDE_EFFECTING)
    nt = len(thru)
    thru_shapes = [pltpu.HBM(t.shape, t.dtype) for t in thru]
    if with_token:
        (after,) = extra
        outs = pl.pallas_call(
            body, name=name, in_specs=[HBM_SPEC] * nt + [pl.BlockSpec(memory_space=pl.ANY)],
            out_specs=[SEM_SPEC] * len(n_sems) + [HBM_SPEC] * nt + [pl.BlockSpec(memory_space=pltpu.VMEM)],
            out_shape=[pltpu.SemaphoreType.DMA((k,)) for k in n_sems] + thru_shapes
                      + [jax.ShapeDtypeStruct((8, LANES), F32)],
            input_output_aliases={i: len(n_sems) + i for i in range(nt)}, compiler_params=effect,
        )(*[hbm(t) for t in thru], after)
        return outs[:len(n_sems)], outs[len(n_sems):-1], outs[-1]
    return pl.pallas_call(
        body, name=name,
        in_specs=[HBM_SPEC] * nt + [SEM_SPEC] * len(n_sems) + [pl.BlockSpec(memory_space=pl.ANY)],
        out_specs=[HBM_SPEC] * nt, out_shape=thru_shapes,
        input_output_aliases={i: i for i in range(nt)}, compiler_params=effect,
    )(*thru, *extra)


def _gather_targets():
    x, y, c = _mesh_pos()
    return 4 * x + 2 * y + c, [(x, y, 1 - c), (1 - x, y, c), (x, 1 - y, c), (1 - x, 1 - y, c)]


def gather_start(shards, after, name):
    n = len(shards)
    zones = [lax.empty((w.shape[0], N_DEV) + w.shape[1:], w.dtype) for w in shards]

    def body(*refs):
        ins, zs = refs[:n], refs[n:2 * n]
        send_sems, recv_sems, local_sems = refs[2 * n + 1:2 * n + 4]
        token = refs[-1]
        me, targets = _gather_targets()
        for a in range(n):
            pltpu.make_async_copy(ins[a], zs[a].at[:, me], local_sems.at[a]).start()
            for k, to in enumerate(targets):
                pltpu.make_async_remote_copy(
                    src_ref=ins[a], dst_ref=zs[a].at[:, me], send_sem=send_sems.at[4 * a + k],
                    recv_sem=recv_sems.at[4 * a + k], device_id=to, device_id_type=MESH).start()
        token[...] = jnp.zeros(token.shape, F32)

    sems, thru, token = _split_call(body, name, list(shards) + zones, (4 * n, 4 * n, n), extra=(after,))
    return (sems, thru, n), token


def gather_wait(started, after, name):
    sems, thru, n = started

    def body(*refs):
        zs = refs[n:2 * n]
        send_sems, recv_sems, local_sems = refs[2 * n:2 * n + 3]
        _, targets = _gather_targets()
        for a in range(n):
            for k, to in enumerate(targets):
                cp = pltpu.make_async_remote_copy(
                    src_ref=zs[a].at[:, 0], dst_ref=zs[a].at[:, 0], send_sem=send_sems.at[4 * a + k],
                    recv_sem=recv_sems.at[4 * a + k], device_id=to, device_id_type=MESH)
                cp.wait_send()
                cp.wait_recv()
            pltpu.make_async_copy(zs[a].at[:, 0], zs[a].at[:, 0], local_sems.at[a]).wait()

    return _split_call(body, name, thru, (4 * n, 4 * n, n), extra=(*sems, after), with_token=False)[n:]


def forward_start(zones, after, name):
    n = len(zones)

    def body(*refs):
        zs = refs[:n]
        send_sems, recv_sems = refs[n + 1:n + 3]
        token = refs[-1]
        x, y, c = _mesh_pos()
        for a in range(n):
            for j, chip in enumerate([(1 - x, y), (x, 1 - y), (1 - x, 1 - y)]):
                blk = zs[a].at[:, 4 * chip[0] + 2 * chip[1] + c]
                pltpu.make_async_remote_copy(
                    src_ref=blk, dst_ref=blk, send_sem=send_sems.at[3 * a + j], recv_sem=recv_sems.at[3 * a + j],
                    device_id=(x, y, 1 - c), device_id_type=MESH).start()
        token[...] = jnp.zeros(token.shape, F32)

    sems, thru, token = _split_call(body, name, list(zones), (3 * n, 3 * n), extra=(after,))
    return (sems, thru, n), token


def forward_wait(started, after, name):
    sems, thru, n = started

    def body(*refs):
        zs = refs[:n]
        send_sems, recv_sems = refs[n:n + 2]
        x, y, c = _mesh_pos()
        for a in range(n):
            for j in range(3):
                cp = pltpu.make_async_remote_copy(
                    src_ref=zs[a].at[:, 0], dst_ref=zs[a].at[:, 0], send_sem=send_sems.at[3 * a + j],
                    recv_sem=recv_sems.at[3 * a + j], device_id=(x, y, 1 - c), device_id_type=MESH)
                cp.wait_send()
                cp.wait_recv()

    return _split_call(body, name, thru, (3 * n, 3 * n), extra=(*sems, after), with_token=False)


def scatter_start(groups, name):
    n = len(groups)
    flat = [g for grp in groups for g in grp]
    nf = len(flat)
    offs = np.cumsum([0] + [len(grp) for grp in groups])
    lands = [lax.empty((N_DEV, len(grp)) + grp[0].shape[1:], grp[0].dtype) for grp in groups]

    def body(*refs):
        ins, zones = refs[:nf], refs[nf:nf + n]
        send_sems, recv_sems, local_sems = refs[nf + n:nf + n + 3]
        token = refs[-1]
        me, peers = _peers()
        for a in range(n):
            for w in range(len(groups[a])):
                pltpu.make_async_copy(ins[offs[a] + w].at[me], zones[a].at[me, w], local_sems.at[a]).start()
        for k, peer in enumerate(peers):
            p_id = 4 * peer[0] + 2 * peer[1] + peer[2]
            for a in range(n):
                for w in range(len(groups[a])):
                    pltpu.make_async_remote_copy(
                        src_ref=ins[offs[a] + w].at[p_id], dst_ref=zones[a].at[me, w],
                        send_sem=send_sems.at[7 * a + k], recv_sem=recv_sems.at[7 * a + k],
                        device_id=peer, device_id_type=MESH).start()
        token[...] = jnp.zeros(token.shape, F32)

    hbm = lambda t: pltpu.with_memory_space_constraint(t, pltpu.HBM)
    outs = pl.pallas_call(
        body, name=name,
        in_specs=[HBM_SPEC] * (nf + n),
        out_specs=[SEM_SPEC] * 3 + [HBM_SPEC] * (nf + n) + [pl.BlockSpec(memory_space=pltpu.VMEM)],
        out_shape=[pltpu.SemaphoreType.DMA((7 * n,)), pltpu.SemaphoreType.DMA((7 * n,)), pltpu.SemaphoreType.DMA((n,))]
                  + [pltpu.HBM(t.shape, t.dtype) for t in flat + lands]
                  + [jax.ShapeDtypeStruct((8, LANES), F32)],
        input_output_aliases={i: 3 + i for i in range(nf + n)},
        compiler_params=pltpu.CompilerParams(has_side_effects=pltpu.SideEffectType.DATAFLOW_SIDE_EFFECTING),
    )(*[hbm(t) for t in flat], *[hbm(t) for t in lands])
    sems, thru, token = outs[:3], outs[3:3 + nf + n], outs[-1]
    return (sems, thru, [len(grp) for grp in groups]), token


def scatter_wait(started, after, name):
    (send_sems, recv_sems, local_sems), thru, sizes = started
    n = len(sizes)
    nf = len(thru) - n

    def body(*refs):
        zones = refs[nf:nf + n]
        s_sems, r_sems, l_sems = refs[nf + n:nf + n + 3]
        me, peers = _peers()
        for a in range(n):
            for k, peer in enumerate(peers):
                cp = pltpu.make_async_remote_copy(
                    src_ref=zones[a].at[0], dst_ref=zones[a].at[0],
                    send_sem=s_sems.at[7 * a + k], recv_sem=r_sems.at[7 * a + k], device_id=peer,
                    device_id_type=MESH)
                cp.wait_send()
                cp.wait_recv()
            pltpu.make_async_copy(zones[a].at[0], zones[a].at[0], l_sems.at[a]).wait()

    outs = pl.pallas_call(
        body, name=name,
        in_specs=[HBM_SPEC] * (nf + n) + [SEM_SPEC] * 3 + [pl.BlockSpec(memory_space=pl.ANY)],
        out_specs=[HBM_SPEC] * (nf + n),
        out_shape=[pltpu.HBM(t.shape, t.dtype) for t in thru],
        input_output_aliases={i: i for i in range(nf + n)},
        compiler_params=pltpu.CompilerParams(has_side_effects=pltpu.SideEffectType.DATAFLOW_SIDE_EFFECTING),
    )(*thru, send_sems, recv_sems, local_sems, after)
    return outs[nf:]


def pair_start(grads, after, name):
    nw = len(grads)
    land = lax.empty((4, nw) + grads[0].shape[1:], grads[0].dtype)

    def body(*refs):
        ins, zone = refs[:nw], refs[nw]
        send_sems, recv_sems = refs[nw + 2:nw + 4]
        x, y, c = _mesh_pos()
        for j in range(4):
            for w in range(nw):
                pltpu.make_async_remote_copy(
                    src_ref=ins[w].at[2 * j + (1 - c)], dst_ref=zone.at[j, w], send_sem=send_sems.at[0],
                    recv_sem=recv_sems.at[0], device_id=(x, y, 1 - c), device_id_type=MESH).start()
        refs[-1][...] = jnp.zeros(refs[-1].shape, F32)

    sems, thru, token = _split_call(body, name, list(grads) + [land], (1, 1), extra=(after,))
    return (sems, thru, nw), token


def pair_wait(started, after, name):
    sems, thru, nw = started

    def body(*refs):
        zone = refs[nw]
        send_sems, recv_sems = refs[nw + 1:nw + 3]
        x, y, c = _mesh_pos()
        cp = pltpu.make_async_remote_copy(src_ref=zone, dst_ref=zone, send_sem=send_sems.at[0],
                                          recv_sem=recv_sems.at[0], device_id=(x, y, 1 - c), device_id_type=MESH)
        cp.wait_send()
        cp.wait_recv()

    outs = _split_call(body, name, thru, (1, 1), extra=(*sems, after), with_token=False)
    return outs[:nw], outs[nw]


def pair_sum(grads, land, name):
    nw = len(grads)
    _, r, c_dim = grads[0].shape

    def body(*refs):
        g_refs, l_ref, o_ref = refs[:nw], refs[nw], refs[nw + 1]
        core = lax.axis_index("c")
        for w in range(nw):
            o_ref[0, w] = (g_refs[w][0, core].astype(F32) + l_ref[0, w].astype(F32)).astype(BF16)

    return pl.pallas_call(
        body, name=name, grid=(4,),
        in_specs=[pl.BlockSpec((1, 2, r, c_dim), lambda j: (j, 0, 0, 0))] * nw
                 + [pl.BlockSpec((1, nw, r, c_dim), lambda j: (j, 0, 0, 0))],
        out_specs=pl.BlockSpec((1, nw, r, c_dim), lambda j: (j, 0, 0, 0)),
        out_shape=jax.ShapeDtypeStruct((4, nw, r, c_dim), BF16),
        compiler_params=_params(),
    )(*[g.reshape(4, 2, r, c_dim) for g in grads], land)


def _other_chips():
    x, y, c = _mesh_pos()
    chips = []
    for rel in range(1, 4):
        px, py = (1 - x if rel & 2 else x), (1 - y if rel & 1 else y)
        chips.append((px, py, 2 * px + py))
    return 2 * x + y, c, chips


def chip_start(pair_sums, after, name):
    land = lax.empty(pair_sums.shape, pair_sums.dtype)

    def body(*refs):
        h_ref, zone = refs[0], refs[1]
        send_sems, recv_sems, local_sem = refs[3:6]
        mine, c, chips = _other_chips()
        pltpu.make_async_copy(h_ref.at[mine], zone.at[mine], local_sem.at[0]).start()
        for k, (px, py, j) in enumerate(chips):
            pltpu.make_async_remote_copy(
                src_ref=h_ref.at[j], dst_ref=zone.at[mine], send_sem=send_sems.at[k], recv_sem=recv_sems.at[k],
                device_id=(px, py, c), device_id_type=MESH).start()
        refs[-1][...] = jnp.zeros(refs[-1].shape, F32)

    sems, thru, token = _split_call(body, name, [pair_sums, land], (3, 3, 1), extra=(after,))
    return (sems, thru), token


def chip_wait(started, after, name):
    sems, thru = started

    def body(*refs):
        zone = refs[1]
        send_sems, recv_sems, local_sem = refs[2:5]
        _, c, chips = _other_chips()
        for k, (px, py, _) in enumerate(chips):
            cp = pltpu.make_async_remote_copy(
                src_ref=zone.at[0], dst_ref=zone.at[0], send_sem=send_sems.at[k], recv_sem=recv_sems.at[k],
                device_id=(px, py, c), device_id_type=MESH)
            cp.wait_send()
            cp.wait_recv()
        pltpu.make_async_copy(zone.at[0], zone.at[0], local_sem.at[0]).wait()

    return _split_call(body, name, thru, (3, 3, 1), extra=(*sems, after), with_token=False)[1]


def share_small(parts, after):
    n = len(parts)

    def body(*refs):
        ins, outs = refs[:n], refs[n + 1:2 * n + 1]
        send_sems, recv_sems, local_sems = refs[2 * n + 1:]
        me, peers = _peers()
        copies = []
        for i in range(n):
            copies.append(pltpu.make_async_copy(ins[i], outs[i].at[me], local_sems.at[i]))
            copies += [pltpu.make_async_remote_copy(
                src_ref=ins[i], dst_ref=outs[i].at[me], send_sem=send_sems.at[7 * i + k],
                recv_sem=recv_sems.at[7 * i + k], device_id=peer, device_id_type=MESH)
                for k, peer in enumerate(peers)]
        for cp in copies:
            cp.start()
        for cp in copies:
            cp.wait()

    vm = pl.BlockSpec(memory_space=pltpu.VMEM)
    return pl.pallas_call(
        body, name="share_small", in_specs=[vm] * n + [pl.BlockSpec(memory_space=pl.ANY)], out_specs=[vm] * n,
        out_shape=[jax.ShapeDtypeStruct((N_DEV,) + p.shape, p.dtype) for p in parts],
        scratch_shapes=[pltpu.SemaphoreType.DMA((7 * n,)), pltpu.SemaphoreType.DMA((7 * n,)),
                        pltpu.SemaphoreType.DMA((n,))],
    )(*parts, after)


def sum_sources(recv, name):
    n_src, w, r, c = recv.shape

    def body(r_ref, o_ref):
        acc = r_ref[0, 0].astype(F32)
        for src in range(1, n_src):
            acc = acc + r_ref[src, 0].astype(F32)
        o_ref[0] = acc

    return pl.pallas_call(
        body, name=name, grid=(w,),
        in_specs=[pl.BlockSpec((n_src, 1, r, c), lambda i: (0, i, 0, 0))],
        out_specs=pl.BlockSpec((1, r, c), lambda i: (i, 0, 0)),
        out_shape=jax.ShapeDtypeStruct((w, r, c), F32),
        compiler_params=_params(),
    )(recv)


def _adamw_math(w, g, m, v):
    m = ADAM_B1 * m + (1.0 - ADAM_B1) * g
    v = ADAM_B2 * v + (1.0 - ADAM_B2) * (g * g)
    m_hat = m / (1.0 - ADAM_B1 ** ADAM_STEP)
    v_hat = v / (1.0 - ADAM_B2 ** ADAM_STEP)
    delta = -ADAM_LR * (m_hat / (jnp.sqrt(v_hat) + ADAM_EPS) + ADAM_WD * w)
    return delta, m, v


def adamw(w, g, m, v, name):
    shape = w.shape
    c = shape[-1]
    r = int(np.prod(shape[:-1]))
    w2, g2, m2, v2 = (t.reshape(r, c) for t in (w, g, m, v))
    tr = next(t for t in range(min(r, 512), 0, -1) if r % t == 0 and (t % 8 == 0 or t == r))

    def body(w_ref, g_ref, m_ref, v_ref, d_ref, mo_ref, vo_ref):
        d_ref[...], mo_ref[...], vo_ref[...] = _adamw_math(w_ref[...], g_ref[...], m_ref[...], v_ref[...])

    spec = pl.BlockSpec((tr, c), lambda i: (i, 0))
    outs = pl.pallas_call(
        body, name=name, grid=(r // tr,),
        in_specs=[spec] * 4, out_specs=[spec] * 3,
        out_shape=[jax.ShapeDtypeStruct((r, c), F32)] * 3,
        compiler_params=_params(),
    )(w2, g2, m2, v2)
    return tuple(t.reshape(shape) for t in outs)


def adamw_small(ws, recvs, ms, vs, name):
    n = len(ws)

    def body(*refs):
        w_refs, r_refs, m_refs, v_refs = (refs[i * n:(i + 1) * n] for i in range(4))
        g_refs, d_refs, mo_refs, vo_refs = (refs[(4 + i) * n:(5 + i) * n] for i in range(4))
        for i in range(n):
            g = r_refs[i][0]
            for src in range(1, N_DEV):
                g = g + r_refs[i][src]
            g_refs[i][...] = g
            d_refs[i][...], mo_refs[i][...], vo_refs[i][...] = _adamw_math(w_refs[i][...], g, m_refs[i][...],
                                                                            v_refs[i][...])

    vm = pl.BlockSpec(memory_space=pltpu.VMEM)
    outs = pl.pallas_call(
        body, name=name, in_specs=[vm] * (4 * n), out_specs=[vm] * (4 * n),
        out_shape=[jax.ShapeDtypeStruct(w.shape, F32) for w in ws] * 4,
        compiler_params=pltpu.CompilerParams(vmem_limit_bytes=V7X_VMEM_LIMIT),
    )(*ws, *recvs, *ms, *vs)
    return [outs[i * n:(i + 1) * n] for i in range(4)]


SMALL_NAMES = ("ffn1_norm", "mix_norm", "ffn2_norm", "b_gate", "na_q_norm", "na_k_norm", "sw_q_norm", "sw_k_norm",
               "na_rpb", "sw_sink", "t5_rel_table")


def kernel(x, ffn1_norm, ffn1_w_gate, ffn1_w_up, ffn1_w_down, mix_norm, w_in, b_gate, na_q_norm, na_k_norm, na_rpb, sw_q_norm, sw_k_norm, sw_sink, t5_rel_table, w_branch_na, w_branch_sw, w_out, ffn2_norm, ffn2_w_gate, ffn2_w_up, ffn2_w_down, loss_target, m_ffn1_norm, m_ffn1_w_gate, m_ffn1_w_up, m_ffn1_w_down, m_mix_norm, m_w_in, m_b_gate, m_na_q_norm, m_na_k_norm, m_na_rpb, m_sw_q_norm, m_sw_k_norm, m_sw_sink, m_t5_rel_table, m_w_branch_na, m_w_branch_sw, m_w_out, m_ffn2_norm, m_ffn2_w_gate, m_ffn2_w_up, m_ffn2_w_down, v_ffn1_norm, v_ffn1_w_gate, v_ffn1_w_up, v_ffn1_w_down, v_mix_norm, v_w_in, v_b_gate, v_na_q_norm, v_na_k_norm, v_na_rpb, v_sw_q_norm, v_sw_k_norm, v_sw_sink, v_t5_rel_table, v_w_branch_na, v_w_branch_sw, v_w_out, v_ffn2_norm, v_ffn2_w_gate, v_ffn2_w_up, v_ffn2_w_down):
    weights = dict(ffn1_norm=ffn1_norm, ffn1_w_gate=ffn1_w_gate, ffn1_w_up=ffn1_w_up, ffn1_w_down=ffn1_w_down,
                   mix_norm=mix_norm, w_in=w_in, b_gate=b_gate, na_q_norm=na_q_norm, na_k_norm=na_k_norm,
                   na_rpb=na_rpb, sw_q_norm=sw_q_norm, sw_k_norm=sw_k_norm, sw_sink=sw_sink,
                   t5_rel_table=t5_rel_table, w_branch_na=w_branch_na, w_branch_sw=w_branch_sw, w_out=w_out,
                   ffn2_norm=ffn2_norm, ffn2_w_gate=ffn2_w_gate, ffn2_w_up=ffn2_w_up, ffn2_w_down=ffn2_w_down)
    mom_m = dict(ffn1_norm=m_ffn1_norm, ffn1_w_gate=m_ffn1_w_gate, ffn1_w_up=m_ffn1_w_up, ffn1_w_down=m_ffn1_w_down,
                 mix_norm=m_mix_norm, w_in=m_w_in, b_gate=m_b_gate, na_q_norm=m_na_q_norm, na_k_norm=m_na_k_norm,
                 na_rpb=m_na_rpb, sw_q_norm=m_sw_q_norm, sw_k_norm=m_sw_k_norm, sw_sink=m_sw_sink,
                 t5_rel_table=m_t5_rel_table, w_branch_na=m_w_branch_na, w_branch_sw=m_w_branch_sw, w_out=m_w_out,
                 ffn2_norm=m_ffn2_norm, ffn2_w_gate=m_ffn2_w_gate, ffn2_w_up=m_ffn2_w_up, ffn2_w_down=m_ffn2_w_down)
    mom_v = dict(ffn1_norm=v_ffn1_norm, ffn1_w_gate=v_ffn1_w_gate, ffn1_w_up=v_ffn1_w_up, ffn1_w_down=v_ffn1_w_down,
                 mix_norm=v_mix_norm, w_in=v_w_in, b_gate=v_b_gate, na_q_norm=v_na_q_norm, na_k_norm=v_na_k_norm,
                 na_rpb=v_na_rpb, sw_q_norm=v_sw_q_norm, sw_k_norm=v_sw_k_norm, sw_sink=v_sw_sink,
                 t5_rel_table=v_t5_rel_table, w_branch_na=v_w_branch_na, w_branch_sw=v_w_branch_sw, w_out=v_w_out,
                 ffn2_norm=v_ffn2_norm, ffn2_w_gate=v_ffn2_w_gate, ffn2_w_up=v_ffn2_w_up, ffn2_w_down=v_ffn2_w_down)
    order = list(weights)

    depth = ffn1_norm.shape[0]
    s, d = x.shape[1], x.shape[2]
    xs = x[0]
    tr = lambda w: jnp.swapaxes(w, -1, -2)

    merge = lambda t: t.reshape(t.shape[0], N_DEV * t.shape[2], t.shape[3])
    no_dep = jnp.zeros((8, LANES), F32)

    def shards_of(kind, l):
        stack = lambda *ws: jnp.stack(ws).astype(BF16)
        if kind == "ffn1":
            return [stack(tr(ffn1_w_gate[l]), tr(ffn1_w_up[l]), ffn1_w_down[l])]
        if kind == "win":
            return [stack(tr(w_in[l]))]
        return [stack(tr(ffn2_w_gate[l]), tr(ffn2_w_up[l]), ffn2_w_down[l]), stack(w_out[l]),
                stack(tr(w_branch_na[l]), tr(w_branch_sw[l]))]

    def start(kind, l, after):
        return gather_start(shards_of(kind, l), after, f"gather_{kind}_{l}")

    def arrive(started, kind, l, after):
        zones = gather_wait(started, after, f"gather_{kind}_{l}_wait")
        return forward_start(zones, no_dep, f"forward_{kind}_{l}")

    def finish(fwd, kind, l, after):
        return [merge(z) for z in forward_wait(fwd, after, f"forward_{kind}_{l}_wait")]

    bd = jnp.asarray(np.kron(np.eye(NA_WIDTH // HEAD_DIM), np.full((HEAD_DIM, HEAD_DIM), 1.0 / HEAD_DIM)), BF16)
    bmap = jnp.asarray(_t5_bucket_map())
    tile8 = lambda g: jnp.tile(g, NA_WIDTH // HEAD_DIM).reshape(1, NA_WIDTH)
    tile2 = lambda g: jnp.tile(g, SW_KV_WIDTH // HEAD_DIM).reshape(1, SW_KV_WIDTH)

    st_first, tok = start("ffn1", 0, no_dep)
    t5b = t5_expand(t5_rel_table, bmap, tok, "t5_expand").reshape(SW_STACK, 3 * SW_BLOCK)
    fwd, _ = arrive(st_first, "ffn1", 0, t5b)
    st_win, dep = start("win", 0, t5b)
    (first,) = finish(fwd, "ffn1", 0, dep)

    saved = []
    layer_w = {0: dict(wg1=(first, 0), wu1=(first, 1), wd1=(first, 2))}
    cur = xs
    for l in range(depth):
        sv = {}
        lw = layer_w[l]
        sv["x0"] = cur
        sv["xn1"], sv["hg1"], sv["hu1"], sv["act1"] = ffn_up(cur, ffn1_norm[l][None], lw["wg1"], lw["wu1"], dep,
                                                             f"ffn1_up_{l}")
        cur = ffn_down(cur, sv["act1"], lw["wd1"], no_dep, f"ffn1_down_{l}")
        sv["x1"] = cur
        fwd, _ = arrive(st_win, "win", l, cur)
        st_rest, tok = start("rest", l, cur)
        (zb,) = finish(fwd, "win", l, tok)
        lw["win"] = (zb, 0)
        sv["gains"] = (tile8(na_q_norm[l]), tile8(na_k_norm[l]), tile8(sw_q_norm[l]), tile2(sw_k_norm[l]))
        sv["hn"], sv["zq"], sv["qa"], sv["ka"], sv["qs"], sv["ks"], sv["gt"] = mix_in(
            cur, mix_norm[l][None], lw["win"], b_gate[l][None], *sv["gains"], bd, f"mix_in_{l}")
        sv["t2"] = rpb_expand(_rpb_rows(na_rpb[l]), no_dep, f"rpb_expand_{l}")
        sv["o_na"] = na_fwd(sv["qa"], sv["ka"], sv["zq"], sv["t2"], f"na_fwd_{l}")
        dep = no_dep
        if l + 1 < depth:
            st_ffn1, dep = start("ffn1", l + 1, sv["o_na"])
        sv["o_sw"] = sw_fwd(sv["qs"], sv["ks"], sv["zq"], t5b, sw_sink[l], dep, f"sw_fwd_{l}")
        fwd, tok = arrive(st_rest, "rest", l, sv["o_sw"])
        za, zc, zd = finish(fwd, "rest", l, tok)
        lw.update(wg2=(za, 0), wu2=(za, 1), wd2=(za, 2), wout=(zc, 0), wna=(zd, 0), wsw=(zd, 1))
        cur, sv["a_na"], sv["a_sw"], sv["merged"] = merge_out(
            cur, sv["o_na"], sv["o_sw"], sv["gt"], lw["wna"], lw["wsw"], lw["wout"], f"merge_out_{l}")
        sv["x2"] = cur
        dep = no_dep
        if l + 1 < depth:
            st_win, dep = start("win", l + 1, cur)
        sv["xn2"], sv["hg2"], sv["hu2"], sv["act2"] = ffn_up(cur, ffn2_norm[l][None], lw["wg2"], lw["wu2"], dep,
                                                             f"ffn2_up_{l}")
        dep = no_dep
        if l + 1 < depth:
            fwd, dep = arrive(st_ffn1, "ffn1", l + 1, sv["act2"])
        cur = ffn_down(cur, sv["act2"], lw["wd2"], dep, f"ffn2_down_{l}")
        dep = no_dep
        if l + 1 < depth:
            (za,) = finish(fwd, "ffn1", l + 1, cur)
            layer_w[l + 1] = dict(wg1=(za, 0), wu1=(za, 1), wd1=(za, 2))
        saved.append(sv)

    dx, loss_acc = loss_grad(cur, loss_target[0], "loss_grad")
    loss = lax.psum(jnp.sum(loss_acc) * (0.5 / d), ("x", "y", "c"))

    split = lambda t: t.reshape(N_DEV, t.shape[0] // N_DEV, t.shape[1])
    pending = {}
    last_key = "ffn1_0"
    small = {k: [None] * depth for k in SMALL_NAMES if k != "t5_rel_table"}
    dbias_sw = []
    for l in reversed(range(depth)):
        sv = saved[l]
        lw = layer_w[l]
        wg1, wu1, wd1, wg2, wu2, wd2 = (lw[k] for k in ("wg1", "wu1", "wd1", "wg2", "wu2", "wd2"))
        win_t, wout_l, wna_t, wsw_t = lw["win"], lw["wout"], lw["wna"], lw["wsw"]
        blocks = ((2, "x2", "xn2", "hg2", "hu2", "act2", wg2, wu2, wd2, "ffn2_norm", 3),
                  (1, "x0", "xn1", "hg1", "hu1", "act1", wg1, wu1, wd1, "ffn1_norm", 0))

        def ffn_backward(dx, blk):
            tag, xk, xnk, hgk, huk, actk, wg, wu, wd, norm_name, slot = blk
            gains = weights[norm_name]
            dxb, dhg, dhu = ffn_bwd_act(dx, wd, sv[hgk], sv[huk], f"ffn{tag}_bwd_act_{l}")
            gwd = tn_matmul(sv[actk], dxb, 0.5, f"ffn{tag}_dwd_{l}")
            gwg = tn_matmul(dhg, sv[xnk], 1.0, f"ffn{tag}_dwg_{l}")
            gwu = tn_matmul(dhu, sv[xnk], 1.0, f"ffn{tag}_dwu_{l}")
            key = f"ffn{tag}_{l}"
            blocks_of = [split(gwg), split(gwu), split(gwd)]
            if key == last_key:
                paired, token = pair_start(blocks_of, dxb, f"pair_{key}")
            else:
                pending[key], token = scatter_start([blocks_of], f"scatter_{key}")
            dx, dg = proj_bwd_norm([dhg, dhu], [wg, wu], sv[xk], gains[l][None], dx, token, f"ffn{tag}_bwd_x_{l}")
            if key == last_key:
                thru, land = pair_wait(paired, dx, f"pair_{key}_wait")
                pending[key], _ = chip_start(pair_sum(thru, land, f"pair_sum_{key}"), dg, f"chips_{key}")
            small[norm_name][l] = dg[0]
            return dx

        dx = ffn_backward(dx, blocks[0])
        dxb, dzg, da_na, da_sw, do_na, do_sw, dbg = mix_bwd_out(
            dx, sv["gt"], sv["a_na"], sv["a_sw"], wna_t, wsw_t, wout_l, f"mix_bwd_out_{l}")
        small["b_gate"][l] = dbg[0]
        gwout = tn_matmul(sv["merged"], dxb, 1.0, f"dwout_{l}")
        gwna = tn_matmul(da_na, sv["o_na"], 1.0, f"dwna_{l}")
        gwsw = tn_matmul(da_sw, sv["o_sw"], 1.0, f"dwsw_{l}")
        dqa, dka, dva, dt2 = na_bwd(sv["qa"], sv["ka"], sv["zq"], sv["t2"], sv["o_na"], do_na, f"na_bwd_{l}")
        dqs, dks, dvs, dbias, dsink = sw_bwd(sv["qs"], sv["ks"], sv["zq"], t5b, sw_sink[l], sv["o_sw"], do_sw,
                                             f"sw_bwd_{l}")
        dbias_sw.append(dbias.reshape(SW_HEADS, SW_BLOCK, 3 * SW_BLOCK))
        small["sw_sink"][l] = jnp.sum(dsink[:, 0].reshape(SW_HEADS, SW_BLOCK), axis=1)
        small["na_rpb"][l] = _rpb_from_rows(rpb_reduce(dt2, f"rpb_reduce_{l}"))
        dz, dgqa, dgka, dgqs, dgks = qk_norm_bwd(dqa, dka, dva, dqs, dks, dvs, sv["zq"], dzg, *sv["gains"], bd,
                                                 f"qk_norm_bwd_{l}")
        fold = lambda g: jnp.sum(g.reshape(-1, HEAD_DIM), axis=0)
        small["na_q_norm"][l], small["na_k_norm"][l] = fold(dgqa), fold(dgka)
        small["sw_q_norm"][l], small["sw_k_norm"][l] = fold(dgqs), fold(dgks)
        gwin = tn_matmul(dz, sv["hn"], 1.0, f"dwin_{l}")
        pending[f"mix_{l}"], token = scatter_start([[split(gwout)], [split(gwna), split(gwsw)], [split(gwin)]],
                                                   f"scatter_mix_{l}")
        dx, dg = proj_bwd_norm([dz], [win_t], sv["x1"], mix_norm[l][None], dx, token, f"mix_bwd_x_{l}")
        small["mix_norm"][l] = dg[0]
        dx = ffn_backward(dx, blocks[1])

    dtab = t5_reduce(dbias_sw, bmap, "t5_reduce")
    small_parts = {k: jnp.stack(v) for k, v in small.items()}
    small_parts["t5_rel_table"] = jnp.transpose(dtab[:, :, 0])

    summed = {}
    layers = lambda f: jnp.stack([f(l) for l in range(depth)])
    grads, delta, new_m, new_v = {}, {}, {}, {}

    def collect(key, after):
        if key == last_key:
            zones = [chip_wait(pending[key], after, f"wait_{key}")]
        else:
            zones = scatter_wait(pending[key], after, f"wait_{key}")
        summed[key] = [sum_sources(z, f"sum_{key}_{i}") for i, z in enumerate(zones)]

    def update(k, g, transposed):
        view = tr if transposed else (lambda t: t)
        d_k, m_k, v_k = adamw(view(weights[k]), g, view(mom_m[k]), view(mom_v[k]), f"adamw_{k}")
        grads[k], delta[k], new_m[k], new_v[k] = view(g), view(d_k), view(m_k), view(v_k)
        return d_k

    for key in pending:
        if key != last_key:
            collect(key, dx)
    update("ffn2_w_gate", layers(lambda l: summed[f"ffn2_{l}"][0][0]), True)
    update("ffn2_w_up", layers(lambda l: summed[f"ffn2_{l}"][0][1]), True)
    update("ffn2_w_down", layers(lambda l: summed[f"ffn2_{l}"][0][2]), False)
    update("w_out", layers(lambda l: summed[f"mix_{l}"][0][0]), False)
    update("w_branch_na", layers(lambda l: summed[f"mix_{l}"][1][0]), True)
    update("w_branch_sw", layers(lambda l: summed[f"mix_{l}"][1][1]), True)
    done = update("w_in", layers(lambda l: summed[f"mix_{l}"][2][0]), True)
    collect(last_key, done)
    update("ffn1_w_gate", layers(lambda l: summed[f"ffn1_{l}"][0][0]), True)
    update("ffn1_w_up", layers(lambda l: summed[f"ffn1_{l}"][0][1]), True)
    done = update("ffn1_w_down", layers(lambda l: summed[f"ffn1_{l}"][0][2]), False)
    recvs = share_small([small_parts[k] for k in SMALL_NAMES], done)
    results = adamw_small([weights[k] for k in SMALL_NAMES], recvs, [mom_m[k] for k in SMALL_NAMES],
                          [mom_v[k] for k in SMALL_NAMES], "adamw_small")
    for dst, outs in zip((grads, delta, new_m, new_v), results):
        dst.update(dict(zip(SMALL_NAMES, outs)))

    return (loss, dx[None], *[grads[k] for k in order], *[delta[k] for k in order],
            *[new_m[k] for k in order], *[new_v[k] for k in order])
```

```python
import functools
import math

import numpy as np
import jax
import jax.numpy as jnp
from jax import lax
from jax.experimental import pallas as pl
from jax.experimental.pallas import tpu as pltpu

F32 = jnp.float32
BF16 = jnp.bfloat16
MESH = pl.DeviceIdType.MESH

N_DEV = 8
EPS = 1e-6
NEG = -1e30
HEAD_DIM = 64
GRID_W = 64
NA_ROWS = 8
NA_COLS = 16
NA_WIDTH = 512
SW_Q_WIDTH = 512
SW_KV_WIDTH = 128
SW_BLOCK = 128
SW_HEADS = 8
SW_REP = 4
REL_BUCKETS = 32
REL_MAX_DIST = 128
QKV_WIDTH = 3 * NA_WIDTH + SW_Q_WIDTH + 2 * SW_KV_WIDTH
SCALE = 1.0 / math.sqrt(HEAD_DIM)

ADAM_LR = 0.001
ADAM_B1 = 0.9
ADAM_B2 = 0.999
ADAM_EPS = 1e-08
ADAM_WD = 0.01
ADAM_STEP = 10

V7X_VMEM_LIMIT = 56 * 1024 * 1024
LANES = 128
MXU_TILE = 256

NT = (((1,), (1,)), ((), ()))
TN = (((0,), (0,)), ((), ()))


def _params(n_grid=1):
    return pltpu.CompilerParams(dimension_semantics=("arbitrary",) * n_grid,
                                vmem_limit_bytes=V7X_VMEM_LIMIT)


def _row_tile(s):
    for t in (512, 256, 128, 64, 32, 16, 8):
        if s % t == 0:
            return t
    raise ValueError(s)


def _tn_tile(n):
    best = max(t for t in range(LANES, min(n, 2304) + 1, LANES) if n % t == 0) if n % LANES == 0 else n
    return best // 2 if best == n and n >= 1024 else best


ONCE = pl.Buffered(1)


def _col_chunk(n):
    return MXU_TILE if n % MXU_TILE == 0 else n


def _dot(a, b):
    return jnp.dot(a, b, preferred_element_type=F32)


def _dotg(a, b, dn):
    return lax.dot_general(a, b, dn, preferred_element_type=F32)


def _sigmoid(v):
    return 1.0 / (1.0 + jnp.exp(-v))


def _rstd(xv):
    return lax.rsqrt(jnp.mean(xv * xv, axis=-1, keepdims=True) + EPS)


def _full(shape):
    nd = len(shape)
    return pl.BlockSpec(shape, lambda i, _n=nd: (0,) * _n)


def _rows(tm, width):
    return pl.BlockSpec((tm, width), lambda i: (i, 0))


def _mat(stack, idx):
    return pl.BlockSpec((None,) + tuple(stack.shape[1:]), lambda i, _w=idx: (_w, 0, 0), pipeline_mode=ONCE)


def _group_mean(v, bd):
    hi = v.astype(BF16)
    lo = (v - hi.astype(F32)).astype(BF16)
    return _dot(hi, bd) + _dot(lo, bd)


def ffn_up(x, gain, wg_t, wu_t, dep, name):
    s, d = x.shape
    f = wg_t[0].shape[1]
    tm = _row_tile(s)
    fc = _col_chunk(f)

    def body(x_ref, g_ref, wg_ref, wu_ref, dep_ref, xn_ref, hg_ref, hu_ref, act_ref):
        xv = x_ref[...]
        xn = (xv * _rstd(xv) * g_ref[...]).astype(BF16)
        xn_ref[...] = xn
        for c0 in range(0, f, fc):
            hg = _dotg(xn, wg_ref[c0:c0 + fc, :], NT)
            hu = _dotg(xn, wu_ref[c0:c0 + fc, :], NT)
            hg_ref[:, c0:c0 + fc] = hg.astype(BF16)
            hu_ref[:, c0:c0 + fc] = hu.astype(BF16)
            act_ref[:, c0:c0 + fc] = (hg * _sigmoid(hg) * hu).astype(BF16)

    return pl.pallas_call(
        body, name=name, grid=(s // tm,),
        in_specs=[_rows(tm, d), _full((1, d)), _mat(*wg_t), _mat(*wu_t), _full(dep.shape)],
        out_specs=[_rows(tm, d), _rows(tm, f), _rows(tm, f), _rows(tm, f)],
        out_shape=[jax.ShapeDtypeStruct((s, d), BF16)] + [jax.ShapeDtypeStruct((s, f), BF16)] * 3,
        compiler_params=_params(),
    )(x, gain, wg_t[0], wu_t[0], dep)


def ffn_down(x, act, wd, dep, name):
    s, d = x.shape
    f = act.shape[1]
    tm = _row_tile(s)

    def body(x_ref, a_ref, w_ref, dep_ref, o_ref):
        o_ref[...] = x_ref[...] + 0.5 * _dot(a_ref[...], w_ref[...])

    return pl.pallas_call(
        body, name=name, grid=(s // tm,),
        in_specs=[_rows(tm, d), _rows(tm, f), _mat(*wd), _full(dep.shape)],
        out_specs=_rows(tm, d),
        out_shape=jax.ShapeDtypeStruct((s, d), F32),
        compiler_params=_params(),
    )(x, act, wd[0], dep)


def mix_in(x, gain, win_t, b_gate, gq_na, gk_na, gq_sw, gk_sw, bd, name):
    s, d = x.shape
    tm = _row_tile(s)
    gc = _col_chunk(2 * d)

    def body(x_ref, g_ref, w_ref, b_ref, gqa_ref, gka_ref, gqs_ref, gks_ref, bd_ref,
             hn_ref, zq_ref, qa_ref, ka_ref, qs_ref, ks_ref, gt_ref):
        xv = x_ref[...]
        hn = (xv * _rstd(xv) * g_ref[...]).astype(BF16)
        hn_ref[...] = hn

        def proj(c0, c1):
            return _dotg(hn, w_ref[c0:c1, :], NT)

        def headnorm(z, g, bdm):
            return z * lax.rsqrt(_group_mean(z * z, bdm) + EPS) * g

        bd512 = bd_ref[...]
        bd128 = bd_ref[0:SW_KV_WIDTH, 0:SW_KV_WIDTH]
        z = proj(0, 512)
        zq_ref[:, 0:512] = z.astype(BF16)
        qa_ref[...] = (headnorm(z, gqa_ref[...], bd512) * SCALE).astype(BF16)
        z = proj(512, 1024)
        zq_ref[:, 512:1024] = z.astype(BF16)
        ka_ref[...] = headnorm(z, gka_ref[...], bd512).astype(BF16)
        z = proj(1024, 1536)
        zq_ref[:, 1024:1536] = z.astype(BF16)
        z = proj(1536, 2048)
        zq_ref[:, 1536:2048] = z.astype(BF16)
        qs_ref[...] = (headnorm(z, gqs_ref[...], bd512) * SCALE).astype(BF16)
        z = proj(2048, 2176)
        zq_ref[:, 2048:2176] = z.astype(BF16)
        ks_ref[...] = headnorm(z, gks_ref[...], bd128).astype(BF16)
        z = proj(2176, 2304)
        zq_ref[:, 2176:2304] = z.astype(BF16)
        for c0 in range(0, 2 * d, gc):
            zg = proj(QKV_WIDTH + c0, QKV_WIDTH + c0 + gc) + b_ref[:, c0:c0 + gc]
            gt_ref[:, c0:c0 + gc] = _sigmoid(zg).astype(BF16)

    return pl.pallas_call(
        body, name=name, grid=(s // tm,),
        in_specs=[_rows(tm, d), _full((1, d)), _mat(*win_t), _full((1, 2 * d)),
                  _full((1, 512)), _full((1, 512)), _full((1, 512)), _full((1, 128)), _full((512, 512))],
        out_specs=[_rows(tm, d), _rows(tm, QKV_WIDTH), _rows(tm, 512), _rows(tm, 512), _rows(tm, 512),
                   _rows(tm, 128), _rows(tm, 2 * d)],
        out_shape=[jax.ShapeDtypeStruct((s, d), BF16), jax.ShapeDtypeStruct((s, QKV_WIDTH), BF16),
                   jax.ShapeDtypeStruct((s, 512), BF16), jax.ShapeDtypeStruct((s, 512), BF16),
                   jax.ShapeDtypeStruct((s, 512), BF16), jax.ShapeDtypeStruct((s, 128), BF16),
                   jax.ShapeDtypeStruct((s, 2 * d), BF16)],
        compiler_params=_params(),
    )(x, gain, win_t[0], b_gate, gq_na, gk_na, gq_sw, gk_sw, bd)


def _na_iotas():
    qc = lax.broadcasted_iota(jnp.int32, (GRID_W, LANES), 0)
    ln = lax.broadcasted_iota(jnp.int32, (GRID_W, LANES), 1)
    low = ln < GRID_W
    kc = jnp.where(low, ln, ln - GRID_W)
    diff = kc - qc + (NA_COLS - 1)
    qcs = jnp.clip(qc - NA_COLS // 2, 0, GRID_W - NA_COLS)
    inwin = (kc >= qcs) & (kc < qcs + NA_COLS)
    return diff, low, inwin


NA_RI = 2 * NA_ROWS - 1
NA_CI = 2 * NA_COLS - 1
NA_T2 = NA_RI + 1


def _rpb_rows(rpb):
    h = rpb.shape[0]
    padded = jnp.pad(rpb, ((0, 0), (1, 1), (0, GRID_W - NA_CI)))
    return jnp.concatenate([padded[:, :NA_T2], padded[:, 1:NA_T2 + 1]], axis=2).reshape(h, NA_T2, LANES)


def _rpb_from_rows(rows):
    return rows[:, 1:, :NA_CI] + rows[:, :NA_RI, GRID_W:GRID_W + NA_CI]


def rpb_expand(rows, dep, name):
    n_heads = rows.shape[0]

    def body(r_ref, dep_ref, o_ref):
        for h in range(n_heads):
            for e in range(NA_T2):
                line = jnp.broadcast_to(r_ref[h, e:e + 1, :], (GRID_W, LANES))
                o_ref[h, e] = pltpu.roll(line, LANES - (NA_COLS - 1), 1, stride=1, stride_axis=0)

    return pl.pallas_call(
        body, name=name,
        in_specs=[pl.BlockSpec(memory_space=pltpu.VMEM), pl.BlockSpec(memory_space=pltpu.VMEM)],
        out_specs=pl.BlockSpec(memory_space=pltpu.VMEM),
        out_shape=jax.ShapeDtypeStruct((n_heads, NA_T2, GRID_W, LANES), F32),
        compiler_params=pltpu.CompilerParams(vmem_limit_bytes=V7X_VMEM_LIMIT),
    )(rows, dep)


def rpb_reduce(dt2, name):
    n_heads = dt2.shape[0]
    flip = jnp.asarray(np.eye(GRID_W)[::-1], BF16)

    def body(d_ref, j_ref, o_ref):
        jm = j_ref[...]
        for h in range(n_heads):
            for e in range(NA_T2):
                dv = d_ref[h, e]
                hi = dv.astype(BF16)
                mid = (dv - hi.astype(F32)).astype(BF16)
                lo = (dv - hi.astype(F32) - mid.astype(F32)).astype(BF16)
                rev = _dot(jm, hi) + _dot(jm, mid) + _dot(jm, lo)
                back = pltpu.roll(rev, LANES + (NA_COLS - 1) - (GRID_W - 1), 1, stride=1, stride_axis=0)
                o_ref[h, e:e + 1, :] = jnp.sum(back, axis=0, keepdims=True)

    return pl.pallas_call(
        body, name=name,
        in_specs=[pl.BlockSpec(memory_space=pltpu.VMEM)] * 2,
        out_specs=pl.BlockSpec(memory_space=pltpu.VMEM),
        out_shape=jax.ShapeDtypeStruct((n_heads, NA_T2, LANES), F32),
        compiler_params=pltpu.CompilerParams(vmem_limit_bytes=V7X_VMEM_LIMIT),
    )(dt2, flip)


NA_TQ = 4
NA_TK = NA_TQ + NA_ROWS
NA_KCH = NA_TK // 2


def _na_tile_geometry(t, rows):
    r = t * NA_TQ
    kbase = jnp.clip(r - NA_ROWS // 2, 0, rows - NA_TK)
    starts = [jnp.clip(r + a - NA_ROWS // 2, 0, rows - NA_ROWS) for a in range(NA_TQ)]
    return r, kbase, starts


def _na_tile_mask(kbase, starts, low, inwin):
    half = jnp.where(low, 0, 1)
    cols = []
    for c in range(NA_KCH):
        krow = kbase + 2 * c + half
        cols.append(jnp.concatenate(
            [jnp.where(inwin & (krow >= st) & (krow < st + NA_ROWS), 0.0, NEG) for st in starts], axis=0))
    return jnp.concatenate(cols, axis=1)


def _na_tile_index(r, kbase, a, c):
    return jnp.clip(kbase + 2 * c - (r + a) + NA_ROWS, 0, NA_T2 - 1)


def _na_tile_scores(q, k, t2_ref, hh, r, kbase, madd):
    bias = jnp.concatenate(
        [jnp.concatenate([t2_ref[hh, _na_tile_index(r, kbase, a, c)] for a in range(NA_TQ)], axis=0)
         for c in range(NA_KCH)], axis=1)
    return _dotg(q, k, NT) + bias + madd


def _softmax_rows(sc):
    e = jnp.exp(sc - jnp.max(sc, axis=1, keepdims=True))
    return e * (1.0 / jnp.sum(e, axis=1, keepdims=True))


def na_fwd(qa, ka, zq, t2, name):
    s = qa.shape[0]
    rows = s // GRID_W
    n_pairs = NA_WIDTH // LANES
    v_blk0 = (2 * NA_WIDTH) // LANES

    assert rows % NA_TQ == 0 and rows >= NA_TK
    tq, tk = NA_TQ * GRID_W, NA_TK * GRID_W

    def body(q_ref, k_ref, v_ref, t2_ref, o_ref, s_scr, p_scr):
        _, low, inwin = _na_iotas()

        def tile(t, carry):
            r, kbase, starts = _na_tile_geometry(t, rows)
            madd = _na_tile_mask(kbase, starts, low, inwin)
            qr = pl.ds(pl.multiple_of(r * GRID_W, tq), tq)
            kr = pl.ds(pl.multiple_of(kbase * GRID_W, tq), tk)
            for hh in range(2):
                lanes = slice(HEAD_DIM * hh, HEAD_DIM * (hh + 1))
                s_scr[tq * hh:tq * (hh + 1), :] = _na_tile_scores(q_ref[qr, lanes], k_ref[kr, lanes], t2_ref, hh, r,
                                                                  kbase, madd)
            p_scr[...] = _softmax_rows(s_scr[...]).astype(BF16)
            for hh in range(2):
                lanes = slice(HEAD_DIM * hh, HEAD_DIM * (hh + 1))
                o_ref[qr, lanes] = _dot(p_scr[tq * hh:tq * (hh + 1), :], v_ref[kr, lanes]).astype(BF16)
            return carry

        lax.fori_loop(0, rows // NA_TQ, tile, 0)

    col = lambda off: pl.BlockSpec((s, LANES), lambda p, _o=off: (0, _o + p))
    return pl.pallas_call(
        body, name=name, grid=(n_pairs,),
        in_specs=[col(0), col(0), col(v_blk0),
                  pl.BlockSpec((2, NA_T2, GRID_W, LANES), lambda p: (p, 0, 0, 0))],
        out_specs=col(0),
        out_shape=jax.ShapeDtypeStruct((s, NA_WIDTH), BF16),
        scratch_shapes=[pltpu.VMEM((2 * tq, tk), F32), pltpu.VMEM((2 * tq, tk), BF16)],
        compiler_params=_params(),
    )(qa, ka, zq, t2)


def na_bwd(qa, ka, zq, t2, o_na, do_na, name):
    s = qa.shape[0]
    rows = s // GRID_W
    n_pairs = NA_WIDTH // LANES
    v_blk0 = (2 * NA_WIDTH) // LANES

    tq, tk = NA_TQ * GRID_W, NA_TK * GRID_W

    def body(q_ref, k_ref, v_ref, t2_ref, o_ref, do_ref, dq_ref, dk_ref, dv_ref, dt2_ref):
        _, low, inwin = _na_iotas()
        dk_ref[...] = jnp.zeros(dk_ref.shape, F32)
        dv_ref[...] = jnp.zeros(dv_ref.shape, F32)
        dt2_ref[...] = jnp.zeros(dt2_ref.shape, F32)

        def tile(t, carry):
            r, kbase, starts = _na_tile_geometry(t, rows)
            madd = _na_tile_mask(kbase, starts, low, inwin)
            qr = pl.ds(pl.multiple_of(r * GRID_W, tq), tq)
            kr = pl.ds(pl.multiple_of(kbase * GRID_W, tq), tk)
            for hh in range(2):
                lanes = slice(HEAD_DIM * hh, HEAD_DIM * (hh + 1))
                q, k, v = q_ref[qr, lanes], k_ref[kr, lanes], v_ref[kr, lanes]
                p = _softmax_rows(_na_tile_scores(q, k, t2_ref, hh, r, kbase, madd))
                do = do_ref[qr, lanes]
                delta = jnp.sum(do.astype(F32) * o_ref[qr, lanes].astype(F32), axis=1, keepdims=True)
                ds = p * (_dotg(do, v, NT) - delta)
                for a in range(NA_TQ):
                    for c in range(NA_KCH):
                        e = _na_tile_index(r, kbase, a, c)
                        dt2_ref[hh, e] = dt2_ref[hh, e] + ds[GRID_W * a:GRID_W * (a + 1), LANES * c:LANES * (c + 1)]
                dsb = ds.astype(BF16)
                dq_ref[qr, lanes] = _dot(dsb, k)
                dk_ref[kr, lanes] = dk_ref[kr, lanes] + _dotg(dsb, q, TN)
                dv_ref[kr, lanes] = dv_ref[kr, lanes] + _dotg(p.astype(BF16), do, TN)
            return carry

        lax.fori_loop(0, rows // NA_TQ, tile, 0)

    col = lambda off: pl.BlockSpec((s, LANES), lambda p, _o=off: (0, _o + p))
    t2spec = pl.BlockSpec((2, NA_T2, GRID_W, LANES), lambda p: (p, 0, 0, 0))
    return pl.pallas_call(
        body, name=name, grid=(n_pairs,),
        in_specs=[col(0), col(0), col(v_blk0), t2spec, col(0), col(0)],
        out_specs=[col(0), col(0), col(0), t2spec],
        out_shape=[jax.ShapeDtypeStruct((s, NA_WIDTH), F32)] * 3 + [jax.ShapeDtypeStruct(t2.shape, F32)],
        compiler_params=_params(),
    )(qa, ka, zq, t2, o_na, do_na)


def _t5_bucket_map():
    rel = np.arange(3 * SW_BLOCK)[None, :] - SW_BLOCK - np.arange(SW_BLOCK)[:, None]
    nb = REL_BUCKETS // 2
    max_exact = nb // 2
    n = np.abs(rel)
    large = max_exact + (np.log(np.maximum(n, 1) / max_exact)
                         / np.log(REL_MAX_DIST / max_exact) * (nb - max_exact)).astype(np.int32)
    large = np.minimum(large, nb - 1)
    return ((rel > 0) * nb + np.where(n < max_exact, n, large)).astype(np.int32)


def t5_expand(table, bmap, dep, name):
    def body(tab_ref, bm_ref, dep_ref, o_ref):
        bm = bm_ref[...]
        for h in range(SW_HEADS):
            t = jnp.zeros(bm.shape, F32)
            for b in range(REL_BUCKETS):
                t = jnp.where(bm == b, tab_ref[b, h], t)
            o_ref[h] = t

    return pl.pallas_call(
        body, name=name,
        in_specs=[pl.BlockSpec(memory_space=pltpu.SMEM), pl.BlockSpec(memory_space=pltpu.VMEM),
                  pl.BlockSpec(memory_space=pltpu.VMEM)],
        out_specs=pl.BlockSpec(memory_space=pltpu.VMEM),
        out_shape=jax.ShapeDtypeStruct((SW_HEADS,) + bmap.shape, F32),
        compiler_params=pltpu.CompilerParams(vmem_limit_bytes=V7X_VMEM_LIMIT),
    )(table, bmap, dep)


def t5_reduce(dbias_list, bmap, name):
    n = len(dbias_list)

    def body(*refs):
        d_refs, bm_ref, o_ref = refs[:n], refs[n], refs[n + 1]
        bm = bm_ref[...]
        for h in range(SW_HEADS):
            dv = d_refs[0][h]
            for other in d_refs[1:]:
                dv = dv + other[h]
            rows = [jnp.sum(jnp.where(bm == b, dv, 0.0), axis=0, keepdims=True) for b in range(REL_BUCKETS)]
            r = jnp.concatenate(rows, axis=0)
            o_ref[h] = jnp.broadcast_to(jnp.sum(r, axis=1, keepdims=True), (REL_BUCKETS, LANES))

    return pl.pallas_call(
        body, name=name,
        in_specs=[pl.BlockSpec(memory_space=pltpu.VMEM)] * (n + 1),
        out_specs=pl.BlockSpec(memory_space=pltpu.VMEM),
        out_shape=jax.ShapeDtypeStruct((SW_HEADS, REL_BUCKETS, LANES), F32),
        compiler_params=pltpu.CompilerParams(vmem_limit_bytes=V7X_VMEM_LIMIT),
    )(*dbias_list, bmap)


def _sw_mask_iotas():
    a = lax.broadcasted_iota(jnp.int32, (SW_BLOCK, 3 * SW_BLOCK), 0)
    j = lax.broadcasted_iota(jnp.int32, (SW_BLOCK, 3 * SW_BLOCK), 1)
    inwin = jnp.abs(j - SW_BLOCK - a) <= SW_BLOCK
    return j, inwin


SW_STACK = SW_HEADS * SW_BLOCK


def _sw_softmax(sc, sk):
    m = jnp.maximum(jnp.max(sc, axis=1, keepdims=True), sk)
    e = jnp.exp(sc - m)
    es = jnp.exp(sk - m)
    inv = 1.0 / (jnp.sum(e, axis=1, keepdims=True) + es)
    return e * inv, es * inv


def _sw_prologue(k_ref, v_ref, kp, vp, sink_ref, s):
    pad = s + 2 * SW_BLOCK
    zeros = jnp.zeros((SW_BLOCK, SW_KV_WIDTH), BF16)
    kp[0:SW_BLOCK, :] = zeros
    vp[0:SW_BLOCK, :] = zeros
    kp[SW_BLOCK + s:pad, :] = zeros
    vp[SW_BLOCK + s:pad, :] = zeros
    kp[SW_BLOCK:SW_BLOCK + s, :] = k_ref[...]
    vp[SW_BLOCK:SW_BLOCK + s, :] = v_ref[...]
    return jnp.concatenate([jnp.full((SW_BLOCK, 1), sink_ref[h], F32) for h in range(SW_HEADS)], axis=0)


def sw_fwd(qs, ks, zq, t5b, sink, dep, name):
    s = qs.shape[0]
    nb = s // SW_BLOCK
    v_blk = (3 * NA_WIDTH + SW_Q_WIDTH + SW_KV_WIDTH) // LANES
    pad = s + 2 * SW_BLOCK

    def body(q_ref, k_ref, v_ref, b_ref, sink_ref, dep_ref, o_ref, kp, vp, s_scr, p_scr):
        sink_col = _sw_prologue(k_ref, v_ref, kp, vp, sink_ref, s)
        j, inwin = _sw_mask_iotas()

        def blk(n, carry):
            kpos = n * SW_BLOCK - SW_BLOCK + j
            madd = jnp.where(inwin & (kpos >= 0) & (kpos < s), 0.0, NEG)
            q0 = pl.multiple_of(n * SW_BLOCK, SW_BLOCK)
            qr, kr = pl.ds(q0, SW_BLOCK), pl.ds(q0, 3 * SW_BLOCK)
            for h in range(SW_HEADS):
                g = h // SW_REP
                s_scr[SW_BLOCK * h:SW_BLOCK * (h + 1), :] = _dotg(
                    q_ref[qr, HEAD_DIM * h:HEAD_DIM * (h + 1)], kp[kr, HEAD_DIM * g:HEAD_DIM * (g + 1)], NT) + madd
            p, _ = _sw_softmax(s_scr[...] + b_ref[...], sink_col)
            p_scr[...] = p.astype(BF16)
            for h in range(SW_HEADS):
                g = h // SW_REP
                o_ref[qr, HEAD_DIM * h:HEAD_DIM * (h + 1)] = _dot(
                    p_scr[SW_BLOCK * h:SW_BLOCK * (h + 1), :], vp[kr, HEAD_DIM * g:HEAD_DIM * (g + 1)]).astype(BF16)
            return carry

        lax.fori_loop(0, nb, blk, 0)

    return pl.pallas_call(
        body, name=name, grid=(1,),
        in_specs=[_full((s, SW_Q_WIDTH)), _full((s, SW_KV_WIDTH)),
                  pl.BlockSpec((s, SW_KV_WIDTH), lambda i: (0, v_blk)),
                  _full((SW_STACK, 3 * SW_BLOCK)), pl.BlockSpec(memory_space=pltpu.SMEM),
                  _full(dep.shape)],
        out_specs=_full((s, SW_Q_WIDTH)),
        out_shape=jax.ShapeDtypeStruct((s, SW_Q_WIDTH), BF16),
        scratch_shapes=[pltpu.VMEM((pad, SW_KV_WIDTH), BF16), pltpu.VMEM((pad, SW_KV_WIDTH), BF16),
                        pltpu.VMEM((SW_STACK, 3 * SW_BLOCK), F32), pltpu.VMEM((SW_STACK, 3 * SW_BLOCK), BF16)],
        compiler_params=_params(),
    )(qs, ks, zq, t5b, sink, dep)


def sw_bwd(qs, ks, zq, t5b, sink, o_sw, do_sw, name):
    s = qs.shape[0]
    nb = s // SW_BLOCK
    v_blk = (3 * NA_WIDTH + SW_Q_WIDTH + SW_KV_WIDTH) // LANES
    pad = s + 2 * SW_BLOCK

    def body(q_ref, k_ref, v_ref, b_ref, sink_ref, o_ref, do_ref,
             dq_ref, dk_ref, dv_ref, db_ref, dsk_ref, kp, vp, dkp, dvp, s_scr, dp_scr, ds_scr, p_scr):
        sink_col = _sw_prologue(k_ref, v_ref, kp, vp, sink_ref, s)
        dkp[...] = jnp.zeros(dkp.shape, F32)
        dvp[...] = jnp.zeros(dvp.shape, F32)
        db_ref[...] = jnp.zeros(db_ref.shape, F32)
        dsk_ref[...] = jnp.zeros(dsk_ref.shape, F32)
        j, inwin = _sw_mask_iotas()

        def blk(n, carry):
            kpos = n * SW_BLOCK - SW_BLOCK + j
            madd = jnp.where(inwin & (kpos >= 0) & (kpos < s), 0.0, NEG)
            q0 = pl.multiple_of(n * SW_BLOCK, SW_BLOCK)
            qr, kr = pl.ds(q0, SW_BLOCK), pl.ds(q0, 3 * SW_BLOCK)
            deltas = []
            for h in range(SW_HEADS):
                g = h // SW_REP
                hl, kl = slice(HEAD_DIM * h, HEAD_DIM * (h + 1)), slice(HEAD_DIM * g, HEAD_DIM * (g + 1))
                rows = slice(SW_BLOCK * h, SW_BLOCK * (h + 1))
                do = do_ref[qr, hl]
                s_scr[rows, :] = _dotg(q_ref[qr, hl], kp[kr, kl], NT) + madd
                dp_scr[rows, :] = _dotg(do, vp[kr, kl], NT)
                deltas.append(jnp.sum(do.astype(F32) * o_ref[qr, hl].astype(F32), axis=1, keepdims=True))
            delta = jnp.concatenate(deltas, axis=0)
            p, ps = _sw_softmax(s_scr[...] + b_ref[...], sink_col)
            ds = p * (dp_scr[...] - delta)
            db_ref[...] = db_ref[...] + ds
            dsk_ref[...] = dsk_ref[...] - jnp.broadcast_to(ps * delta, (SW_STACK, LANES))
            ds_scr[...] = ds.astype(BF16)
            p_scr[...] = p.astype(BF16)
            for g in range(SW_HEADS // SW_REP):
                kl = slice(HEAD_DIM * g, HEAD_DIM * (g + 1))
                k = kp[kr, kl]
                dkw = jnp.zeros((3 * SW_BLOCK, HEAD_DIM), F32)
                dvw = jnp.zeros((3 * SW_BLOCK, HEAD_DIM), F32)
                for r in range(SW_REP):
                    h = g * SW_REP + r
                    hl, rows = slice(HEAD_DIM * h, HEAD_DIM * (h + 1)), slice(SW_BLOCK * h, SW_BLOCK * (h + 1))
                    dsb = ds_scr[rows, :]
                    dq_ref[qr, hl] = _dot(dsb, k)
                    dkw = dkw + _dotg(dsb, q_ref[qr, hl], TN)
                    dvw = dvw + _dotg(p_scr[rows, :], do_ref[qr, hl], TN)
                dkp[kr, kl] = dkp[kr, kl] + dkw
                dvp[kr, kl] = dvp[kr, kl] + dvw
            return carry

        lax.fori_loop(0, nb, blk, 0)
        dk_ref[...] = dkp[SW_BLOCK:SW_BLOCK + s, :]
        dv_ref[...] = dvp[SW_BLOCK:SW_BLOCK + s, :]

    bias_spec = _full((SW_STACK, 3 * SW_BLOCK))
    return pl.pallas_call(
        body, name=name, grid=(1,),
        in_specs=[_full((s, SW_Q_WIDTH)), _full((s, SW_KV_WIDTH)),
                  pl.BlockSpec((s, SW_KV_WIDTH), lambda i: (0, v_blk)),
                  bias_spec, pl.BlockSpec(memory_space=pltpu.SMEM),
                  _full((s, SW_Q_WIDTH)), _full((s, SW_Q_WIDTH))],
        out_specs=[_full((s, SW_Q_WIDTH)), _full((s, SW_KV_WIDTH)), _full((s, SW_KV_WIDTH)), bias_spec,
                   _full((SW_STACK, LANES))],
        out_shape=[jax.ShapeDtypeStruct((s, SW_Q_WIDTH), F32), jax.ShapeDtypeStruct((s, SW_KV_WIDTH), F32),
                   jax.ShapeDtypeStruct((s, SW_KV_WIDTH), F32),
                   jax.ShapeDtypeStruct((SW_STACK, 3 * SW_BLOCK), F32),
                   jax.ShapeDtypeStruct((SW_STACK, LANES), F32)],
        scratch_shapes=[pltpu.VMEM((pad, SW_KV_WIDTH), BF16), pltpu.VMEM((pad, SW_KV_WIDTH), BF16),
                        pltpu.VMEM((pad, SW_KV_WIDTH), F32), pltpu.VMEM((pad, SW_KV_WIDTH), F32),
                        pltpu.VMEM((SW_STACK, 3 * SW_BLOCK), F32), pltpu.VMEM((SW_STACK, 3 * SW_BLOCK), F32),
                        pltpu.VMEM((SW_STACK, 3 * SW_BLOCK), BF16), pltpu.VMEM((SW_STACK, 3 * SW_BLOCK), BF16)],
        compiler_params=_params(),
    )(qs, ks, zq, t5b, sink, o_sw, do_sw)


def merge_out(x, o_na, o_sw, gt, wbna_t, wbsw_t, wout, name):
    s, d = x.shape
    tm = _row_tile(s)

    def body(x_ref, ona_ref, osw_ref, gt_ref, wna_ref, wsw_ref, wo_ref, xo_ref, ana_ref, asw_ref, mg_ref):
        a_na = _dotg(ona_ref[...], wna_ref[...], NT)
        a_sw = _dotg(osw_ref[...], wsw_ref[...], NT)
        ana_ref[...] = a_na.astype(BF16)
        asw_ref[...] = a_sw.astype(BF16)
        merged = (gt_ref[:, 0:d].astype(F32) * a_na + gt_ref[:, d:2 * d].astype(F32) * a_sw).astype(BF16)
        mg_ref[...] = merged
        xo_ref[...] = x_ref[...] + _dot(merged, wo_ref[...])

    return pl.pallas_call(
        body, name=name, grid=(s // tm,),
        in_specs=[_rows(tm, d), _rows(tm, 512), _rows(tm, 512), _rows(tm, 2 * d),
                  _mat(*wbna_t), _mat(*wbsw_t), _mat(*wout)],
        out_specs=[_rows(tm, d)] * 4,
        out_shape=[jax.ShapeDtypeStruct((s, d), F32)] + [jax.ShapeDtypeStruct((s, d), BF16)] * 3,
        compiler_params=_params(),
    )(x, o_na, o_sw, gt, wbna_t[0], wbsw_t[0], wout[0])


def mix_bwd_out(dx, gt, a_na, a_sw, wbna_t, wbsw_t, wout, dep, name):
    s, d = dx.shape
    tm = _row_tile(s)

    def body(dx_ref, gt_ref, ana_ref, asw_ref, wna_ref, wsw_ref, wo_ref, dep_ref,
             dxb_ref, dzg_ref, dana_ref, dasw_ref, dona_ref, dosw_ref, dbg_ref):
        @pl.when(pl.program_id(0) == 0)
        def _():
            dbg_ref[...] = jnp.zeros(dbg_ref.shape, F32)

        dxb = dx_ref[...].astype(BF16)
        dxb_ref[...] = dxb
        dm = _dotg(dxb, wo_ref[...], NT)
        for i, (a_ref, da_ref, w_ref, do_ref) in enumerate(
                [(ana_ref, dana_ref, wna_ref, dona_ref), (asw_ref, dasw_ref, wsw_ref, dosw_ref)]):
            gi = gt_ref[:, i * d:(i + 1) * d].astype(F32)
            da = (dm * gi).astype(BF16)
            da_ref[...] = da
            do_ref[...] = _dot(da, w_ref[...]).astype(BF16)
            dzg = dm * a_ref[...].astype(F32) * gi * (1.0 - gi)
            dzg_ref[:, i * d:(i + 1) * d] = dzg.astype(BF16)
            dbg_ref[:, i * d:(i + 1) * d] = dbg_ref[:, i * d:(i + 1) * d] + jnp.sum(dzg, axis=0, keepdims=True)

    return pl.pallas_call(
        body, name=name, grid=(s // tm,),
        in_specs=[_rows(tm, d), _rows(tm, 2 * d), _rows(tm, d), _rows(tm, d),
                  _mat(*wbna_t), _mat(*wbsw_t), _mat(*wout), _full(dep.shape)],
        out_specs=[_rows(tm, d), _rows(tm, 2 * d), _rows(tm, d), _rows(tm, d), _rows(tm, 512), _rows(tm, 512),
                   _full((1, 2 * d))],
        out_shape=[jax.ShapeDtypeStruct((s, d), BF16), jax.ShapeDtypeStruct((s, 2 * d), BF16),
                   jax.ShapeDtypeStruct((s, d), BF16), jax.ShapeDtypeStruct((s, d), BF16),
                   jax.ShapeDtypeStruct((s, 512), BF16), jax.ShapeDtypeStruct((s, 512), BF16),
                   jax.ShapeDtypeStruct((1, 2 * d), F32)],
        compiler_params=_params(),
    )(dx, gt, a_na, a_sw, wbna_t[0], wbsw_t[0], wout[0], dep)


def qk_norm_bwd(dqa, dka, dva, dqs, dks, dvs, zq, dzg, gq_na, gk_na, gq_sw, gk_sw, bd, name):
    s = zq.shape[0]
    d2 = dzg.shape[1]
    n_in = QKV_WIDTH + d2
    tm = _row_tile(s)

    def body(dqa_ref, dka_ref, dva_ref, dqs_ref, dks_ref, dvs_ref, zq_ref, dzg_ref,
             gqa_ref, gka_ref, gqs_ref, gks_ref, bd_ref, dz_ref, dgqa_ref, dgka_ref, dgqs_ref, dgks_ref):
        @pl.when(pl.program_id(0) == 0)
        def _():
            for r in (dgqa_ref, dgka_ref, dgqs_ref, dgks_ref):
                r[...] = jnp.zeros(r.shape, F32)

        bd512 = bd_ref[...]
        bd128 = bd_ref[0:SW_KV_WIDTH, 0:SW_KV_WIDTH]

        def one(c0, c1, dy_ref, g_ref, dg_ref, bdm, scale):
            z = zq_ref[:, c0:c1].astype(F32)
            r = lax.rsqrt(_group_mean(z * z, bdm) + EPS)
            zh = z * r
            dy = dy_ref[...] * scale
            dyg = dy * g_ref[...]
            dz = r * (dyg - zh * _group_mean(dyg * zh, bdm))
            dz_ref[:, c0:c1] = dz.astype(BF16)
            dg_ref[...] = dg_ref[...] + jnp.sum(dy * zh, axis=0, keepdims=True)

        one(0, 512, dqa_ref, gqa_ref, dgqa_ref, bd512, SCALE)
        one(512, 1024, dka_ref, gka_ref, dgka_ref, bd512, 1.0)
        dz_ref[:, 1024:1536] = dva_ref[...].astype(BF16)
        one(1536, 2048, dqs_ref, gqs_ref, dgqs_ref, bd512, SCALE)
        one(2048, 2176, dks_ref, gks_ref, dgks_ref, bd128, 1.0)
        dz_ref[:, 2176:2304] = dvs_ref[...].astype(BF16)
        dz_ref[:, QKV_WIDTH:n_in] = dzg_ref[...]

    return pl.pallas_call(
        body, name=name, grid=(s // tm,),
        in_specs=[_rows(tm, 512), _rows(tm, 512), _rows(tm, 512), _rows(tm, 512), _rows(tm, 128), _rows(tm, 128),
                  _rows(tm, QKV_WIDTH), _rows(tm, d2),
                  _full((1, 512)), _full((1, 512)), _full((1, 512)), _full((1, 128)), _full((512, 512))],
        out_specs=[_rows(tm, n_in), _full((1, 512)), _full((1, 512)), _full((1, 512)), _full((1, 128))],
        out_shape=[jax.ShapeDtypeStruct((s, n_in), BF16)] + [jax.ShapeDtypeStruct((1, 512), F32)] * 3
                  + [jax.ShapeDtypeStruct((1, 128), F32)],
        compiler_params=_params(),
    )(dqa, dka, dva, dqs, dks, dvs, zq, dzg, gq_na, gk_na, gq_sw, gk_sw, bd)


def ffn_bwd_act(dx, wd, hg, hu, name):
    s, d = dx.shape
    f = wd[0].shape[1]
    tm = _row_tile(s)
    fc = _col_chunk(f)

    def body(dx_ref, w_ref, hg_ref, hu_ref, dxb_ref, dhg_ref, dhu_ref):
        dxb = dx_ref[...].astype(BF16)
        dxb_ref[...] = dxb
        for c0 in range(0, f, fc):
            dact = 0.5 * _dotg(dxb, w_ref[c0:c0 + fc, :], NT)
            hg = hg_ref[:, c0:c0 + fc].astype(F32)
            hu = hu_ref[:, c0:c0 + fc].astype(F32)
            sg = _sigmoid(hg)
            dhu_ref[:, c0:c0 + fc] = (dact * hg * sg).astype(BF16)
            dhg_ref[:, c0:c0 + fc] = (dact * hu * sg * (1.0 + hg * (1.0 - sg))).astype(BF16)

    return pl.pallas_call(
        body, name=name, grid=(s // tm,),
        in_specs=[_rows(tm, d), _mat(*wd), _rows(tm, f), _rows(tm, f)],
        out_specs=[_rows(tm, d), _rows(tm, f), _rows(tm, f)],
        out_shape=[jax.ShapeDtypeStruct((s, d), BF16), jax.ShapeDtypeStruct((s, f), BF16),
                   jax.ShapeDtypeStruct((s, f), BF16)],
        compiler_params=_params(),
    )(dx, wd[0], hg, hu)


def proj_bwd_norm(acts, weights, x, gain, dx, dep, name):
    s, d = x.shape
    tm = _row_tile(s)
    n = len(acts)

    def body(*refs):
        a_refs, w_refs = refs[:n], refs[n:2 * n]
        x_ref, g_ref, dx_ref, _, o_ref, dg_ref = refs[2 * n:]

        @pl.when(pl.program_id(0) == 0)
        def _():
            dg_ref[...] = jnp.zeros(dg_ref.shape, F32)

        dxn = _dot(a_refs[0][...], w_refs[0][...])
        for a_ref, w_ref in zip(a_refs[1:], w_refs[1:]):
            dxn = dxn + _dot(a_ref[...], w_ref[...])
        xv = x_ref[...]
        r = _rstd(xv)
        xh = xv * r
        dxh = dxn * g_ref[...]
        o_ref[...] = dx_ref[...] + r * (dxh - xh * jnp.mean(dxh * xh, axis=-1, keepdims=True))
        dg_ref[...] = dg_ref[...] + jnp.sum(dxn * xh, axis=0, keepdims=True)

    return pl.pallas_call(
        body, name=name, grid=(s // tm,),
        in_specs=[_rows(tm, a.shape[1]) for a in acts] + [_mat(*w) for w in weights]
                 + [_rows(tm, d), _full((1, d)), _rows(tm, d), _full(dep.shape)],
        out_specs=[_rows(tm, d), _full((1, d))],
        out_shape=[jax.ShapeDtypeStruct((s, d), F32), jax.ShapeDtypeStruct((1, d), F32)],
        compiler_params=_params(),
    )(*acts, *[w[0] for w in weights], x, gain, dx, dep)


def tn_matmul(a, b, scale, name):
    s, n = a.shape
    k = b.shape[1]
    tn = _tn_tile(n)

    def body(a_ref, b_ref, o_ref):
        o_ref[...] = (scale * _dotg(a_ref[...], b_ref[...], TN)).astype(BF16)

    return pl.pallas_call(
        body, name=name, grid=(n // tn,),
        in_specs=[pl.BlockSpec((s, tn), lambda i: (0, i)),
                  pl.BlockSpec((s, k), lambda i: (0, 0), pipeline_mode=ONCE)],
        out_specs=pl.BlockSpec((tn, k), lambda i: (i, 0)),
        out_shape=jax.ShapeDtypeStruct((n, k), BF16),
        compiler_params=_params(),
    )(a, b)


def loss_grad(y, target, name):
    s, d = y.shape
    tm = _row_tile(s)

    def body(y_ref, t_ref, dy_ref, acc_ref):
        @pl.when(pl.program_id(0) == 0)
        def _():
            acc_ref[...] = jnp.zeros(acc_ref.shape, F32)

        err = y_ref[...] - t_ref[...]
        dy_ref[...] = err * (1.0 / d)
        e2 = err * err
        part = jnp.sum(e2.reshape(tm // 8, 8, d), axis=0)
        acc = part[:, 0:LANES]
        for c0 in range(LANES, d, LANES):
            acc = acc + part[:, c0:c0 + LANES]
        acc_ref[...] = acc_ref[...] + acc

    return pl.pallas_call(
        body, name=name, grid=(s // tm,),
        in_specs=[_rows(tm, d), _rows(tm, d)],
        out_specs=[_rows(tm, d), _full((8, LANES))],
        out_shape=[jax.ShapeDtypeStruct((s, d), F32), jax.ShapeDtypeStruct((8, LANES), F32)],
        compiler_params=_params(),
    )(y, target)


def _mesh_pos():
    return lax.axis_index("x"), lax.axis_index("y"), lax.axis_index("c")


def _peers():
    x, y, c = _mesh_pos()
    peers = []
    for rel in range(1, N_DEV):
        peers.append((1 - x if rel & 4 else x, 1 - y if rel & 2 else y, 1 - c if rel & 1 else c))
    return 4 * x + 2 * y + c, peers


HBM_SPEC = pl.BlockSpec(memory_space=pltpu.HBM)
SEM_SPEC = pl.BlockSpec(memory_space=pltpu.SEMAPHORE)


def _split_call(body, name, thru, n_sems, extra=(), with_token=True):
    hbm = lambda t: pltpu.with_memory_space_constraint(t, pltpu.HBM)
    effect = pltpu.CompilerParams(has_side_effects=pltpu.SideEffectType.DATAFLOW_SIDE_EFFECTING)
    nt = len(thru)
    thru_shapes = [pltpu.HBM(t.shape, t.dtype) for t in thru]
    if with_token:
        (after,) = extra
        outs = pl.pallas_call(
            body, name=name, in_specs=[HBM_SPEC] * nt + [pl.BlockSpec(memory_space=pl.ANY)],
            out_specs=[SEM_SPEC] * len(n_sems) + [HBM_SPEC] * nt + [pl.BlockSpec(memory_space=pltpu.VMEM)],
            out_shape=[pltpu.SemaphoreType.DMA((k,)) for k in n_sems] + thru_shapes
                      + [jax.ShapeDtypeStruct((8, LANES), F32)],
            input_output_aliases={i: len(n_sems) + i for i in range(nt)}, compiler_params=effect,
        )(*[hbm(t) for t in thru], after)
        return outs[:len(n_sems)], outs[len(n_sems):-1], outs[-1]
    return pl.pallas_call(
        body, name=name,
        in_specs=[HBM_SPEC] * nt + [SEM_SPEC] * len(n_sems) + [pl.BlockSpec(memory_space=pl.ANY)],
        out_specs=[HBM_SPEC] * nt, out_shape=thru_shapes,
        input_output_aliases={i: i for i in range(nt)}, compiler_params=effect,
    )(*thru, *extra)


def _gather_targets():
    x, y, c = _mesh_pos()
    return 4 * x + 2 * y + c, [(x, y, 1 - c), (1 - x, y, c), (x, 1 - y, c), (1 - x, 1 - y, c)]


def gather_start(shards, after, name):
    n = len(shards)
    zones = [lax.empty((w.shape[0], N_DEV) + w.shape[1:], w.dtype) for w in shards]

    def body(*refs):
        ins, zs = refs[:n], refs[n:2 * n]
        send_sems, recv_sems, local_sems = refs[2 * n + 1:2 * n + 4]
        token = refs[-1]
        me, targets = _gather_targets()
        for a in range(n):
            pltpu.make_async_copy(ins[a], zs[a].at[:, me], local_sems.at[a]).start()
            for k, to in enumerate(targets):
                pltpu.make_async_remote_copy(
                    src_ref=ins[a], dst_ref=zs[a].at[:, me], send_sem=send_sems.at[4 * a + k],
                    recv_sem=recv_sems.at[4 * a + k], device_id=to, device_id_type=MESH).start()
        token[...] = jnp.zeros(token.shape, F32)

    sems, thru, token = _split_call(body, name, list(shards) + zones, (4 * n, 4 * n, n), extra=(after,))
    return (sems, thru, n), token


def gather_wait(started, after, name):
    sems, thru, n = started

    def body(*refs):
        zs = refs[n:2 * n]
        send_sems, recv_sems, local_sems = refs[2 * n:2 * n + 3]
        _, targets = _gather_targets()
        for a in range(n):
            for k, to in enumerate(targets):
                cp = pltpu.make_async_remote_copy(
                    src_ref=zs[a].at[:, 0], dst_ref=zs[a].at[:, 0], send_sem=send_sems.at[4 * a + k],
                    recv_sem=recv_sems.at[4 * a + k], device_id=to, device_id_type=MESH)
                cp.wait_send()
                cp.wait_recv()
            pltpu.make_async_copy(zs[a].at[:, 0], zs[a].at[:, 0], local_sems.at[a]).wait()

    return _split_call(body, name, thru, (4 * n, 4 * n, n), extra=(*sems, after), with_token=False)[n:]


def forward_start(zones, after, name):
    n = len(zones)

    def body(*refs):
        zs = refs[:n]
        send_sems, recv_sems = refs[n + 1:n + 3]
        token = refs[-1]
        x, y, c = _mesh_pos()
        for a in range(n):
            for j, chip in enumerate([(1 - x, y), (x, 1 - y), (1 - x, 1 - y)]):
                blk = zs[a].at[:, 4 * chip[0] + 2 * chip[1] + c]
                pltpu.make_async_remote_copy(
                    src_ref=blk, dst_ref=blk, send_sem=send_sems.at[3 * a + j], recv_sem=recv_sems.at[3 * a + j],
                    device_id=(x, y, 1 - c), device_id_type=MESH).start()
        token[...] = jnp.zeros(token.shape, F32)

    sems, thru, token = _split_call(body, name, list(zones), (3 * n, 3 * n), extra=(after,))
    return (sems, thru, n), token


def forward_wait(started, after, name):
    sems, thru, n = started

    def body(*refs):
        zs = refs[:n]
        send_sems, recv_sems = refs[n:n + 2]
        x, y, c = _mesh_pos()
        for a in range(n):
            for j in range(3):
                cp = pltpu.make_async_remote_copy(
                    src_ref=zs[a].at[:, 0], dst_ref=zs[a].at[:, 0], send_sem=send_sems.at[3 * a + j],
                    recv_sem=recv_sems.at[3 * a + j], device_id=(x, y, 1 - c), device_id_type=MESH)
                cp.wait_send()
                cp.wait_recv()

    return _split_call(body, name, thru, (3 * n, 3 * n), extra=(*sems, after), with_token=False)


def scatter_start(groups, name):
    n = len(groups)
    flat = [g for grp in groups for g in grp]
    nf = len(flat)
    offs = np.cumsum([0] + [len(grp) for grp in groups])
    lands = [lax.empty((N_DEV, len(grp)) + grp[0].shape[1:], grp[0].dtype) for grp in groups]

    def body(*refs):
        ins, zones = refs[:nf], refs[nf:nf + n]
        send_sems, recv_sems, local_sems = refs[nf + n:nf + n + 3]
        token = refs[-1]
        me, peers = _peers()
        for a in range(n):
            for w in range(len(groups[a])):
                pltpu.make_async_copy(ins[offs[a] + w].at[me], zones[a].at[me, w], local_sems.at[a]).start()
        for k, peer in enumerate(peers):
            p_id = 4 * peer[0] + 2 * peer[1] + peer[2]
            for a in range(n):
                for w in range(len(groups[a])):
                    pltpu.make_async_remote_copy(
                        src_ref=ins[offs[a] + w].at[p_id], dst_ref=zones[a].at[me, w],
                        send_sem=send_sems.at[7 * a + k], recv_sem=recv_sems.at[7 * a + k],
                        device_id=peer, device_id_type=MESH).start()
        token[...] = jnp.zeros(token.shape, F32)

    hbm = lambda t: pltpu.with_memory_space_constraint(t, pltpu.HBM)
    outs = pl.pallas_call(
        body, name=name,
        in_specs=[HBM_SPEC] * (nf + n),
        out_specs=[SEM_SPEC] * 3 + [HBM_SPEC] * (nf + n) + [pl.BlockSpec(memory_space=pltpu.VMEM)],
        out_shape=[pltpu.SemaphoreType.DMA((7 * n,)), pltpu.SemaphoreType.DMA((7 * n,)), pltpu.SemaphoreType.DMA((n,))]
                  + [pltpu.HBM(t.shape, t.dtype) for t in flat + lands]
                  + [jax.ShapeDtypeStruct((8, LANES), F32)],
        input_output_aliases={i: 3 + i for i in range(nf + n)},
        compiler_params=pltpu.CompilerParams(has_side_effects=pltpu.SideEffectType.DATAFLOW_SIDE_EFFECTING),
    )(*[hbm(t) for t in flat], *[hbm(t) for t in lands])
    sems, thru, token = outs[:3], outs[3:3 + nf + n], outs[-1]
    return (sems, thru, [len(grp) for grp in groups]), token


def scatter_wait(started, after, name):
    (send_sems, recv_sems, local_sems), thru, sizes = started
    n = len(sizes)
    nf = len(thru) - n

    def body(*refs):
        zones = refs[nf:nf + n]
        s_sems, r_sems, l_sems = refs[nf + n:nf + n + 3]
        me, peers = _peers()
        for a in range(n):
            for k, peer in enumerate(peers):
                cp = pltpu.make_async_remote_copy(
                    src_ref=zones[a].at[0], dst_ref=zones[a].at[0],
                    send_sem=s_sems.at[7 * a + k], recv_sem=r_sems.at[7 * a + k], device_id=peer,
                    device_id_type=MESH)
                cp.wait_send()
                cp.wait_recv()
            pltpu.make_async_copy(zones[a].at[0], zones[a].at[0], l_sems.at[a]).wait()

    outs = pl.pallas_call(
        body, name=name,
        in_specs=[HBM_SPEC] * (nf + n) + [SEM_SPEC] * 3 + [pl.BlockSpec(memory_space=pl.ANY)],
        out_specs=[HBM_SPEC] * (nf + n),
        out_shape=[pltpu.HBM(t.shape, t.dtype) for t in thru],
        input_output_aliases={i: i for i in range(nf + n)},
        compiler_params=pltpu.CompilerParams(has_side_effects=pltpu.SideEffectType.DATAFLOW_SIDE_EFFECTING),
    )(*thru, send_sems, recv_sems, local_sems, after)
    return outs[nf:]


def pair_start(grads, after, name):
    nw = len(grads)
    land = lax.empty((4, nw) + grads[0].shape[1:], grads[0].dtype)

    def body(*refs):
        ins, zone = refs[:nw], refs[nw]
        send_sems, recv_sems = refs[nw + 2:nw + 4]
        x, y, c = _mesh_pos()
        for j in range(4):
            for w in range(nw):
                pltpu.make_async_remote_copy(
                    src_ref=ins[w].at[2 * j + (1 - c)], dst_ref=zone.at[j, w], send_sem=send_sems.at[0],
                    recv_sem=recv_sems.at[0], device_id=(x, y, 1 - c), device_id_type=MESH).start()
        refs[-1][...] = jnp.zeros(refs[-1].shape, F32)

    sems, thru, token = _split_call(body, name, list(grads) + [land], (1, 1), extra=(after,))
    return (sems, thru, nw), token


def pair_wait(started, after, name):
    sems, thru, nw = started

    def body(*refs):
        zone = refs[nw]
        send_sems, recv_sems = refs[nw + 1:nw + 3]
        x, y, c = _mesh_pos()
        cp = pltpu.make_async_remote_copy(src_ref=zone, dst_ref=zone, send_sem=send_sems.at[0],
                                          recv_sem=recv_sems.at[0], device_id=(x, y, 1 - c), device_id_type=MESH)
        cp.wait_send()
        cp.wait_recv()

    outs = _split_call(body, name, thru, (1, 1), extra=(*sems, after), with_token=False)
    return outs[:nw], outs[nw]


def pair_sum(grads, land, name):
    nw = len(grads)
    _, r, c_dim = grads[0].shape

    def body(*refs):
        g_refs, l_ref, o_ref = refs[:nw], refs[nw], refs[nw + 1]
        core = lax.axis_index("c")
        for w in range(nw):
            o_ref[0, w] = (g_refs[w][0, core].astype(F32) + l_ref[0, w].astype(F32)).astype(BF16)

    return pl.pallas_call(
        body, name=name, grid=(4,),
        in_specs=[pl.BlockSpec((1, 2, r, c_dim), lambda j: (j, 0, 0, 0))] * nw
                 + [pl.BlockSpec((1, nw, r, c_dim), lambda j: (j, 0, 0, 0))],
        out_specs=pl.BlockSpec((1, nw, r, c_dim), lambda j: (j, 0, 0, 0)),
        out_shape=jax.ShapeDtypeStruct((4, nw, r, c_dim), BF16),
        compiler_params=_params(),
    )(*[g.reshape(4, 2, r, c_dim) for g in grads], land)


def _other_chips():
    x, y, c = _mesh_pos()
    chips = []
    for rel in range(1, 4):
        px, py = (1 - x if rel & 2 else x), (1 - y if rel & 1 else y)
        chips.append((px, py, 2 * px + py))
    return 2 * x + y, c, chips


def chip_start(pair_sums, after, name):
    land = lax.empty(pair_sums.shape, pair_sums.dtype)

    def body(*refs):
        h_ref, zone = refs[0], refs[1]
        send_sems, recv_sems, local_sem = refs[3:6]
        mine, c, chips = _other_chips()
        pltpu.make_async_copy(h_ref.at[mine], zone.at[mine], local_sem.at[0]).start()
        for k, (px, py, j) in enumerate(chips):
            pltpu.make_async_remote_copy(
                src_ref=h_ref.at[j], dst_ref=zone.at[mine], send_sem=send_sems.at[k], recv_sem=recv_sems.at[k],
                device_id=(px, py, c), device_id_type=MESH).start()
        refs[-1][...] = jnp.zeros(refs[-1].shape, F32)

    sems, thru, token = _split_call(body, name, [pair_sums, land], (3, 3, 1), extra=(after,))
    return (sems, thru), token


def chip_wait(started, after, name):
    sems, thru = started

    def body(*refs):
        zone = refs[1]
        send_sems, recv_sems, local_sem = refs[2:5]
        _, c, chips = _other_chips()
        for k, (px, py, _) in enumerate(chips):
            cp = pltpu.make_async_remote_copy(
                src_ref=zone.at[0], dst_ref=zone.at[0], send_sem=send_sems.at[k], recv_sem=recv_sems.at[k],
                device_id=(px, py, c), device_id_type=MESH)
            cp.wait_send()
            cp.wait_recv()
        pltpu.make_async_copy(zone.at[0], zone.at[0], local_sem.at[0]).wait()

    return _split_call(body, name, thru, (3, 3, 1), extra=(*sems, after), with_token=False)[1]


def share_small(parts, after):
    n = len(parts)

    def body(*refs):
        ins, outs = refs[:n], refs[n + 1:2 * n + 1]
        send_sems, recv_sems, local_sems = refs[2 * n + 1:]
        me, peers = _peers()
        copies = []
        for i in range(n):
            copies.append(pltpu.make_async_copy(ins[i], outs[i].at[me], local_sems.at[i]))
            copies += [pltpu.make_async_remote_copy(
                src_ref=ins[i], dst_ref=outs[i].at[me], send_sem=send_sems.at[7 * i + k],
                recv_sem=recv_sems.at[7 * i + k], device_id=peer, device_id_type=MESH)
                for k, peer in enumerate(peers)]
        for cp in copies:
            cp.start()
        for cp in copies:
            cp.wait()

    vm = pl.BlockSpec(memory_space=pltpu.VMEM)
    return pl.pallas_call(
        body, name="share_small", in_specs=[vm] * n + [pl.BlockSpec(memory_space=pl.ANY)], out_specs=[vm] * n,
        out_shape=[jax.ShapeDtypeStruct((N_DEV,) + p.shape, p.dtype) for p in parts],
        scratch_shapes=[pltpu.SemaphoreType.DMA((7 * n,)), pltpu.SemaphoreType.DMA((7 * n,)),
                        pltpu.SemaphoreType.DMA((n,))],
    )(*parts, after)


def sum_sources(recv, name):
    n_src, w, r, c = recv.shape

    def body(r_ref, o_ref):
        acc = r_ref[0, 0].astype(F32)
        for src in range(1, n_src):
            acc = acc + r_ref[src, 0].astype(F32)
        o_ref[0] = acc

    return pl.pallas_call(
        body, name=name, grid=(w,),
        in_specs=[pl.BlockSpec((n_src, 1, r, c), lambda i: (0, i, 0, 0))],
        out_specs=pl.BlockSpec((1, r, c), lambda i: (i, 0, 0)),
        out_shape=jax.ShapeDtypeStruct((w, r, c), F32),
        compiler_params=_params(),
    )(recv)


def _adamw_math(w, g, m, v):
    m = ADAM_B1 * m + (1.0 - ADAM_B1) * g
    v = ADAM_B2 * v + (1.0 - ADAM_B2) * (g * g)
    m_hat = m / (1.0 - ADAM_B1 ** ADAM_STEP)
    v_hat = v / (1.0 - ADAM_B2 ** ADAM_STEP)
    delta = -ADAM_LR * (m_hat / (jnp.sqrt(v_hat) + ADAM_EPS) + ADAM_WD * w)
    return delta, m, v


def adamw(w, g, m, v, name):
    shape = w.shape
    c = shape[-1]
    r = int(np.prod(shape[:-1]))
    w2, g2, m2, v2 = (t.reshape(r, c) for t in (w, g, m, v))
    tr = next(t for t in range(min(r, 512), 0, -1) if r % t == 0 and (t % 8 == 0 or t == r))

    def body(w_ref, g_ref, m_ref, v_ref, d_ref, mo_ref, vo_ref):
        d_ref[...], mo_ref[...], vo_ref[...] = _adamw_math(w_ref[...], g_ref[...], m_ref[...], v_ref[...])

    spec = pl.BlockSpec((tr, c), lambda i: (i, 0))
    outs = pl.pallas_call(
        body, name=name, grid=(r // tr,),
        in_specs=[spec] * 4, out_specs=[spec] * 3,
        out_shape=[jax.ShapeDtypeStruct((r, c), F32)] * 3,
        compiler_params=_params(),
    )(w2, g2, m2, v2)
    return tuple(t.reshape(shape) for t in outs)


def adamw_small(ws, recvs, ms, vs, name):
    n = len(ws)

    def body(*refs):
        w_refs, r_refs, m_refs, v_refs = (refs[i * n:(i + 1) * n] for i in range(4))
        g_refs, d_refs, mo_refs, vo_refs = (refs[(4 + i) * n:(5 + i) * n] for i in range(4))
        for i in range(n):
            g = r_refs[i][0]
            for src in range(1, N_DEV):
                g = g + r_refs[i][src]
            g_refs[i][...] = g
            d_refs[i][...], mo_refs[i][...], vo_refs[i][...] = _adamw_math(w_refs[i][...], g, m_refs[i][...],
                                                                            v_refs[i][...])

    vm = pl.BlockSpec(memory_space=pltpu.VMEM)
    outs = pl.pallas_call(
        body, name=name, in_specs=[vm] * (4 * n), out_specs=[vm] * (4 * n),
        out_shape=[jax.ShapeDtypeStruct(w.shape, F32) for w in ws] * 4,
        compiler_params=pltpu.CompilerParams(vmem_limit_bytes=V7X_VMEM_LIMIT),
    )(*ws, *recvs, *ms, *vs)
    return [outs[i * n:(i + 1) * n] for i in range(4)]


SMALL_NAMES = ("ffn1_norm", "mix_norm", "ffn2_norm", "b_gate", "na_q_norm", "na_k_norm", "sw_q_norm", "sw_k_norm",
               "na_rpb", "sw_sink", "t5_rel_table")


def kernel(x, ffn1_norm, ffn1_w_gate, ffn1_w_up, ffn1_w_down, mix_norm, w_in, b_gate, na_q_norm, na_k_norm, na_rpb, sw_q_norm, sw_k_norm, sw_sink, t5_rel_table, w_branch_na, w_branch_sw, w_out, ffn2_norm, ffn2_w_gate, ffn2_w_up, ffn2_w_down, loss_target, m_ffn1_norm, m_ffn1_w_gate, m_ffn1_w_up, m_ffn1_w_down, m_mix_norm, m_w_in, m_b_gate, m_na_q_norm, m_na_k_norm, m_na_rpb, m_sw_q_norm, m_sw_k_norm, m_sw_sink, m_t5_rel_table, m_w_branch_na, m_w_branch_sw, m_w_out, m_ffn2_norm, m_ffn2_w_gate, m_ffn2_w_up, m_ffn2_w_down, v_ffn1_norm, v_ffn1_w_gate, v_ffn1_w_up, v_ffn1_w_down, v_mix_norm, v_w_in, v_b_gate, v_na_q_norm, v_na_k_norm, v_na_rpb, v_sw_q_norm, v_sw_k_norm, v_sw_sink, v_t5_rel_table, v_w_branch_na, v_w_branch_sw, v_w_out, v_ffn2_norm, v_ffn2_w_gate, v_ffn2_w_up, v_ffn2_w_down):
    weights = dict(ffn1_norm=ffn1_norm, ffn1_w_gate=ffn1_w_gate, ffn1_w_up=ffn1_w_up, ffn1_w_down=ffn1_w_down,
                   mix_norm=mix_norm, w_in=w_in, b_gate=b_gate, na_q_norm=na_q_norm, na_k_norm=na_k_norm,
                   na_rpb=na_rpb, sw_q_norm=sw_q_norm, sw_k_norm=sw_k_norm, sw_sink=sw_sink,
                   t5_rel_table=t5_rel_table, w_branch_na=w_branch_na, w_branch_sw=w_branch_sw, w_out=w_out,
                   ffn2_norm=ffn2_norm, ffn2_w_gate=ffn2_w_gate, ffn2_w_up=ffn2_w_up, ffn2_w_down=ffn2_w_down)
    mom_m = dict(ffn1_norm=m_ffn1_norm, ffn1_w_gate=m_ffn1_w_gate, ffn1_w_up=m_ffn1_w_up, ffn1_w_down=m_ffn1_w_down,
                 mix_norm=m_mix_norm, w_in=m_w_in, b_gate=m_b_gate, na_q_norm=m_na_q_norm, na_k_norm=m_na_k_norm,
                 na_rpb=m_na_rpb, sw_q_norm=m_sw_q_norm, sw_k_norm=m_sw_k_norm, sw_sink=m_sw_sink,
                 t5_rel_table=m_t5_rel_table, w_branch_na=m_w_branch_na, w_branch_sw=m_w_branch_sw, w_out=m_w_out,
                 ffn2_norm=m_ffn2_norm, ffn2_w_gate=m_ffn2_w_gate, ffn2_w_up=m_ffn2_w_up, ffn2_w_down=m_ffn2_w_down)
    mom_v = dict(ffn1_norm=v_ffn1_norm, ffn1_w_gate=v_ffn1_w_gate, ffn1_w_up=v_ffn1_w_up, ffn1_w_down=v_ffn1_w_down,
                 mix_norm=v_mix_norm, w_in=v_w_in, b_gate=v_b_gate, na_q_norm=v_na_q_norm, na_k_norm=v_na_k_norm,
                 na_rpb=v_na_rpb, sw_q_norm=v_sw_q_norm, sw_k_norm=v_sw_k_norm, sw_sink=v_sw_sink,
                 t5_rel_table=v_t5_rel_table, w_branch_na=v_w_branch_na, w_branch_sw=v_w_branch_sw, w_out=v_w_out,
                 ffn2_norm=v_ffn2_norm, ffn2_w_gate=v_ffn2_w_gate, ffn2_w_up=v_ffn2_w_up, ffn2_w_down=v_ffn2_w_down)
    order = list(weights)

    depth = ffn1_norm.shape[0]
    s, d = x.shape[1], x.shape[2]
    xs = x[0]
    tr = lambda w: jnp.swapaxes(w, -1, -2)

    merge = lambda t: t.reshape(t.shape[0], N_DEV * t.shape[2], t.shape[3])
    no_dep = jnp.zeros((8, LANES), F32)

    def shards_of(kind, l):
        stack = lambda *ws: jnp.stack(ws).astype(BF16)
        if kind == "ffn1":
            return [stack(tr(ffn1_w_gate[l]), tr(ffn1_w_up[l]), ffn1_w_down[l])]
        if kind == "win":
            return [stack(tr(w_in[l]))]
        return [stack(tr(ffn2_w_gate[l]), tr(ffn2_w_up[l]), ffn2_w_down[l]), stack(w_out[l]),
                stack(tr(w_branch_na[l]), tr(w_branch_sw[l]))]

    def start(kind, l, after):
        return gather_start(shards_of(kind, l), after, f"gather_{kind}_{l}")

    def arrive(started, kind, l, after):
        zones = gather_wait(started, after, f"gather_{kind}_{l}_wait")
        return forward_start(zones, no_dep, f"forward_{kind}_{l}")

    def finish(fwd, kind, l, after):
        return [merge(z) for z in forward_wait(fwd, after, f"forward_{kind}_{l}_wait")]

    bd = jnp.asarray(np.kron(np.eye(NA_WIDTH // HEAD_DIM), np.full((HEAD_DIM, HEAD_DIM), 1.0 / HEAD_DIM)), BF16)
    bmap = jnp.asarray(_t5_bucket_map())
    tile8 = lambda g: jnp.tile(g, NA_WIDTH // HEAD_DIM).reshape(1, NA_WIDTH)
    tile2 = lambda g: jnp.tile(g, SW_KV_WIDTH // HEAD_DIM).reshape(1, SW_KV_WIDTH)

    st_first, tok = start("ffn1", 0, no_dep)
    t5b = t5_expand(t5_rel_table, bmap, tok, "t5_expand").reshape(SW_STACK, 3 * SW_BLOCK)
    fwd, _ = arrive(st_first, "ffn1", 0, t5b)
    st_win, dep = start("win", 0, t5b)
    (first,) = finish(fwd, "ffn1", 0, dep)

    saved = []
    layer_w = {0: dict(wg1=(first, 0), wu1=(first, 1), wd1=(first, 2))}
    cur = xs
    for l in range(depth):
        sv = {}
        lw = layer_w[l]
        sv["x0"] = cur
        sv["xn1"], sv["hg1"], sv["hu1"], sv["act1"] = ffn_up(cur, ffn1_norm[l][None], lw["wg1"], lw["wu1"], dep,
                                                             f"ffn1_up_{l}")
        cur = ffn_down(cur, sv["act1"], lw["wd1"], no_dep, f"ffn1_down_{l}")
        sv["x1"] = cur
        fwd, _ = arrive(st_win, "win", l, cur)
        st_rest, tok = start("rest", l, cur)
        (zb,) = finish(fwd, "win", l, tok)
        lw["win"] = (zb, 0)
        sv["gains"] = (tile8(na_q_norm[l]), tile8(na_k_norm[l]), tile8(sw_q_norm[l]), tile2(sw_k_norm[l]))
        sv["hn"], sv["zq"], sv["qa"], sv["ka"], sv["qs"], sv["ks"], sv["gt"] = mix_in(
            cur, mix_norm[l][None], lw["win"], b_gate[l][None], *sv["gains"], bd, f"mix_in_{l}")
        sv["t2"] = rpb_expand(_rpb_rows(na_rpb[l]), no_dep, f"rpb_expand_{l}")
        sv["o_na"] = na_fwd(sv["qa"], sv["ka"], sv["zq"], sv["t2"], f"na_fwd_{l}")
        dep = no_dep
        if l + 1 < depth:
            st_ffn1, dep = start("ffn1", l + 1, sv["o_na"])
        sv["o_sw"] = sw_fwd(sv["qs"], sv["ks"], sv["zq"], t5b, sw_sink[l], dep, f"sw_fwd_{l}")
        fwd, tok = arrive(st_rest, "rest", l, sv["o_sw"])
        za, zc, zd = finish(fwd, "rest", l, tok)
        lw.update(wg2=(za, 0), wu2=(za, 1), wd2=(za, 2), wout=(zc, 0), wna=(zd, 0), wsw=(zd, 1))
        cur, sv["a_na"], sv["a_sw"], sv["merged"] = merge_out(
            cur, sv["o_na"], sv["o_sw"], sv["gt"], lw["wna"], lw["wsw"], lw["wout"], f"merge_out_{l}")
        sv["x2"] = cur
        dep = no_dep
        if l + 1 < depth:
            st_win, dep = start("win", l + 1, cur)
        sv["xn2"], sv["hg2"], sv["hu2"], sv["act2"] = ffn_up(cur, ffn2_norm[l][None], lw["wg2"], lw["wu2"], dep,
                                                             f"ffn2_up_{l}")
        dep = no_dep
        if l + 1 < depth:
            fwd, dep = arrive(st_ffn1, "ffn1", l + 1, sv["act2"])
        cur = ffn_down(cur, sv["act2"], lw["wd2"], dep, f"ffn2_down_{l}")
        dep = no_dep
        if l + 1 < depth:
            (za,) = finish(fwd, "ffn1", l + 1, cur)
            layer_w[l + 1] = dict(wg1=(za, 0), wu1=(za, 1), wd1=(za, 2))
        saved.append(sv)

    dx, loss_acc = loss_grad(cur, loss_target[0], "loss_grad")
    loss = lax.psum(jnp.sum(loss_acc) * (0.5 / d), ("x", "y", "c"))

    split = lambda t: t.reshape(N_DEV, t.shape[0] // N_DEV, t.shape[1])
    pending = {}
    last_key = "ffn1_0"
    two_level = {"ffn2_0", last_key}
    small = {k: [None] * depth for k in SMALL_NAMES if k != "t5_rel_table"}
    dbias_sw = []
    for l in reversed(range(depth)):
        sv = saved[l]
        lw = layer_w[l]
        wg1, wu1, wd1, wg2, wu2, wd2 = (lw[k] for k in ("wg1", "wu1", "wd1", "wg2", "wu2", "wd2"))
        win_t, wout_l, wna_t, wsw_t = lw["win"], lw["wout"], lw["wna"], lw["wsw"]
        blocks = ((2, "x2", "xn2", "hg2", "hu2", "act2", wg2, wu2, wd2, "ffn2_norm", 3),
                  (1, "x0", "xn1", "hg1", "hu1", "act1", wg1, wu1, wd1, "ffn1_norm", 0))

        def ffn_backward(dx, blk):
            tag, xk, xnk, hgk, huk, actk, wg, wu, wd, norm_name, slot = blk
            gains = weights[norm_name]
            dxb, dhg, dhu = ffn_bwd_act(dx, wd, sv[hgk], sv[huk], f"ffn{tag}_bwd_act_{l}")
            gwd = tn_matmul(sv[actk], dxb, 0.5, f"ffn{tag}_dwd_{l}")
            gwg = tn_matmul(dhg, sv[xnk], 1.0, f"ffn{tag}_dwg_{l}")
            gwu = tn_matmul(dhu, sv[xnk], 1.0, f"ffn{tag}_dwu_{l}")
            key = f"ffn{tag}_{l}"
            blocks_of = [split(gwg), split(gwu), split(gwd)]
            if key in two_level:
                paired, token = pair_start(blocks_of, dxb, f"pair_{key}")
            else:
                pending[key], token = scatter_start([blocks_of], f"scatter_{key}")
            dx, dg = proj_bwd_norm([dhg, dhu], [wg, wu], sv[xk], gains[l][None], dx, token, f"ffn{tag}_bwd_x_{l}")
            token = no_dep
            if key in two_level:
                thru, land = pair_wait(paired, dx, f"pair_{key}_wait")
                pending[key], token = chip_start(pair_sum(thru, land, f"pair_sum_{key}"), dg, f"chips_{key}")
            small[norm_name][l] = dg[0]
            return dx, token

        dx, token = ffn_backward(dx, blocks[0])
        dxb, dzg, da_na, da_sw, do_na, do_sw, dbg = mix_bwd_out(
            dx, sv["gt"], sv["a_na"], sv["a_sw"], wna_t, wsw_t, wout_l, token, f"mix_bwd_out_{l}")
        small["b_gate"][l] = dbg[0]
        gwout = tn_matmul(sv["merged"], dxb, 1.0, f"dwout_{l}")
        gwna = tn_matmul(da_na, sv["o_na"], 1.0, f"dwna_{l}")
        gwsw = tn_matmul(da_sw, sv["o_sw"], 1.0, f"dwsw_{l}")
        dqa, dka, dva, dt2 = na_bwd(sv["qa"], sv["ka"], sv["zq"], sv["t2"], sv["o_na"], do_na, f"na_bwd_{l}")
        dqs, dks, dvs, dbias, dsink = sw_bwd(sv["qs"], sv["ks"], sv["zq"], t5b, sw_sink[l], sv["o_sw"], do_sw,
                                             f"sw_bwd_{l}")
        dbias_sw.append(dbias.reshape(SW_HEADS, SW_BLOCK, 3 * SW_BLOCK))
        small["sw_sink"][l] = jnp.sum(dsink[:, 0].reshape(SW_HEADS, SW_BLOCK), axis=1)
        small["na_rpb"][l] = _rpb_from_rows(rpb_reduce(dt2, f"rpb_reduce_{l}"))
        dz, dgqa, dgka, dgqs, dgks = qk_norm_bwd(dqa, dka, dva, dqs, dks, dvs, sv["zq"], dzg, *sv["gains"], bd,
                                                 f"qk_norm_bwd_{l}")
        fold = lambda g: jnp.sum(g.reshape(-1, HEAD_DIM), axis=0)
        small["na_q_norm"][l], small["na_k_norm"][l] = fold(dgqa), fold(dgka)
        small["sw_q_norm"][l], small["sw_k_norm"][l] = fold(dgqs), fold(dgks)
        gwin = tn_matmul(dz, sv["hn"], 1.0, f"dwin_{l}")
        pending[f"mix_{l}"], token = scatter_start([[split(gwout)], [split(gwna), split(gwsw)], [split(gwin)]],
                                                   f"scatter_mix_{l}")
        dx, dg = proj_bwd_norm([dz], [win_t], sv["x1"], mix_norm[l][None], dx, token, f"mix_bwd_x_{l}")
        small["mix_norm"][l] = dg[0]
        dx, _ = ffn_backward(dx, blocks[1])

    dtab = t5_reduce(dbias_sw, bmap, "t5_reduce")
    small_parts = {k: jnp.stack(v) for k, v in small.items()}
    small_parts["t5_rel_table"] = jnp.transpose(dtab[:, :, 0])

    summed = {}
    layers = lambda f: jnp.stack([f(l) for l in range(depth)])
    grads, delta, new_m, new_v = {}, {}, {}, {}

    def collect(key, after):
        if key in two_level:
            zones = [chip_wait(pending[key], after, f"wait_{key}")]
        else:
            zones = scatter_wait(pending[key], after, f"wait_{key}")
        summed[key] = [sum_sources(z, f"sum_{key}_{i}") for i, z in enumerate(zones)]

    def update(k, g, transposed):
        view = tr if transposed else (lambda t: t)
        d_k, m_k, v_k = adamw(view(weights[k]), g, view(mom_m[k]), view(mom_v[k]), f"adamw_{k}")
        grads[k], delta[k], new_m[k], new_v[k] = view(g), view(d_k), view(m_k), view(v_k)
        return d_k

    for key in pending:
        if key != last_key:
            collect(key, dx)
    update("ffn2_w_gate", layers(lambda l: summed[f"ffn2_{l}"][0][0]), True)
    update("ffn2_w_up", layers(lambda l: summed[f"ffn2_{l}"][0][1]), True)
    update("ffn2_w_down", layers(lambda l: summed[f"ffn2_{l}"][0][2]), False)
    update("w_out", layers(lambda l: summed[f"mix_{l}"][0][0]), False)
    update("w_branch_na", layers(lambda l: summed[f"mix_{l}"][1][0]), True)
    update("w_branch_sw", layers(lambda l: summed[f"mix_{l}"][1][1]), True)
    done = update("w_in", layers(lambda l: summed[f"mix_{l}"][2][0]), True)
    collect(last_key, done)
    update("ffn1_w_gate", layers(lambda l: summed[f"ffn1_{l}"][0][0]), True)
    update("ffn1_w_up", layers(lambda l: summed[f"ffn1_{l}"][0][1]), True)
    done = update("ffn1_w_down", layers(lambda l: summed[f"ffn1_{l}"][0][2]), False)
    recvs = share_small([small_parts[k] for k in SMALL_NAMES], done)
    results = adamw_small([weights[k] for k in SMALL_NAMES], recvs, [mom_m[k] for k in SMALL_NAMES],
                          [mom_v[k] for k in SMALL_NAMES], "adamw_small")
    for dst, outs in zip((grads, delta, new_m, new_v), results):
        dst.update(dict(zip(SMALL_NAMES, outs)))

    return (loss, dx[None], *[grads[k] for k in order], *[delta[k] for k in order],
            *[new_m[k] for k in order], *[new_v[k] for k in order])
```

```python
import functools
import math

import numpy as np
import jax
import jax.numpy as jnp
from jax import lax
from jax.experimental import pallas as pl
from jax.experimental.pallas import tpu as pltpu

F32 = jnp.float32
BF16 = jnp.bfloat16
MESH = pl.DeviceIdType.MESH

N_DEV = 8
EPS = 1e-6
NEG = -1e30
HEAD_DIM = 64
GRID_W = 64
NA_ROWS = 8
NA_COLS = 16
NA_WIDTH = 512
SW_Q_WIDTH = 512
SW_KV_WIDTH = 128
SW_BLOCK = 128
SW_HEADS = 8
SW_REP = 4
REL_BUCKETS = 32
REL_MAX_DIST = 128
QKV_WIDTH = 3 * NA_WIDTH + SW_Q_WIDTH + 2 * SW_KV_WIDTH
SCALE = 1.0 / math.sqrt(HEAD_DIM)

ADAM_LR = 0.001
ADAM_B1 = 0.9
ADAM_B2 = 0.999
ADAM_EPS = 1e-08
ADAM_WD = 0.01
ADAM_STEP = 10

V7X_VMEM_LIMIT = 56 * 1024 * 1024
LANES = 128
MXU_TILE = 256

NT = (((1,), (1,)), ((), ()))
TN = (((0,), (0,)), ((), ()))


def _params(n_grid=1):
    return pltpu.CompilerParams(dimension_semantics=("arbitrary",) * n_grid,
                                vmem_limit_bytes=V7X_VMEM_LIMIT)


def _row_tile(s):
    for t in (512, 256, 128, 64, 32, 16, 8):
        if s % t == 0:
            return t
    raise ValueError(s)


def _tn_tile(n):
    best = max(t for t in range(LANES, min(n, 2304) + 1, LANES) if n % t == 0) if n % LANES == 0 else n
    return best // 2 if best == n and n >= 1024 else best


ONCE = pl.Buffered(1)


def _col_chunk(n):
    return MXU_TILE if n % MXU_TILE == 0 else n


def _dot(a, b):
    return jnp.dot(a, b, preferred_element_type=F32)


def _dotg(a, b, dn):
    return lax.dot_general(a, b, dn, preferred_element_type=F32)


def _sigmoid(v):
    return 1.0 / (1.0 + jnp.exp(-v))


def _rstd(xv):
    return lax.rsqrt(jnp.mean(xv * xv, axis=-1, keepdims=True) + EPS)


def _full(shape):
    nd = len(shape)
    return pl.BlockSpec(shape, lambda i, _n=nd: (0,) * _n)


def _rows(tm, width):
    return pl.BlockSpec((tm, width), lambda i: (i, 0))


def _mat(stack, idx):
    return pl.BlockSpec((None,) + tuple(stack.shape[1:]), lambda i, _w=idx: (_w, 0, 0), pipeline_mode=ONCE)


def _group_mean(v, bd):
    hi = v.astype(BF16)
    lo = (v - hi.astype(F32)).astype(BF16)
    return _dot(hi, bd) + _dot(lo, bd)


def ffn_up(x, gain, wg_t, wu_t, dep, name):
    s, d = x.shape
    f = wg_t[0].shape[1]
    tm = _row_tile(s)
    fc = _col_chunk(f)

    def body(x_ref, g_ref, wg_ref, wu_ref, dep_ref, xn_ref, hg_ref, hu_ref, act_ref):
        xv = x_ref[...]
        xn = (xv * _rstd(xv) * g_ref[...]).astype(BF16)
        xn_ref[...] = xn
        for c0 in range(0, f, fc):
            hg = _dotg(xn, wg_ref[c0:c0 + fc, :], NT)
            hu = _dotg(xn, wu_ref[c0:c0 + fc, :], NT)
            hg_ref[:, c0:c0 + fc] = hg.astype(BF16)
            hu_ref[:, c0:c0 + fc] = hu.astype(BF16)
            act_ref[:, c0:c0 + fc] = (hg * _sigmoid(hg) * hu).astype(BF16)

    return pl.pallas_call(
        body, name=name, grid=(s // tm,),
        in_specs=[_rows(tm, d), _full((1, d)), _mat(*wg_t), _mat(*wu_t), _full(dep.shape)],
        out_specs=[_rows(tm, d), _rows(tm, f), _rows(tm, f), _rows(tm, f)],
        out_shape=[jax.ShapeDtypeStruct((s, d), BF16)] + [jax.ShapeDtypeStruct((s, f), BF16)] * 3,
        compiler_params=_params(),
    )(x, gain, wg_t[0], wu_t[0], dep)


def ffn_down(x, act, wd, dep, name):
    s, d = x.shape
    f = act.shape[1]
    tm = _row_tile(s)

    def body(x_ref, a_ref, w_ref, dep_ref, o_ref):
        o_ref[...] = x_ref[...] + 0.5 * _dot(a_ref[...], w_ref[...])

    return pl.pallas_call(
        body, name=name, grid=(s // tm,),
        in_specs=[_rows(tm, d), _rows(tm, f), _mat(*wd), _full(dep.shape)],
        out_specs=_rows(tm, d),
        out_shape=jax.ShapeDtypeStruct((s, d), F32),
        compiler_params=_params(),
    )(x, act, wd[0], dep)


def mix_in(x, gain, win_t, b_gate, gq_na, gk_na, gq_sw, gk_sw, bd, name):
    s, d = x.shape
    tm = _row_tile(s)
    gc = _col_chunk(2 * d)

    def body(x_ref, g_ref, w_ref, b_ref, gqa_ref, gka_ref, gqs_ref, gks_ref, bd_ref,
             hn_ref, zq_ref, qa_ref, ka_ref, qs_ref, ks_ref, gt_ref):
        xv = x_ref[...]
        hn = (xv * _rstd(xv) * g_ref[...]).astype(BF16)
        hn_ref[...] = hn

        def proj(c0, c1):
            return _dotg(hn, w_ref[c0:c1, :], NT)

        def headnorm(z, g, bdm):
            return z * lax.rsqrt(_group_mean(z * z, bdm) + EPS) * g

        bd512 = bd_ref[...]
        bd128 = bd_ref[0:SW_KV_WIDTH, 0:SW_KV_WIDTH]
        z = proj(0, 512)
        zq_ref[:, 0:512] = z.astype(BF16)
        qa_ref[...] = (headnorm(z, gqa_ref[...], bd512) * SCALE).astype(BF16)
        z = proj(512, 1024)
        zq_ref[:, 512:1024] = z.astype(BF16)
        ka_ref[...] = headnorm(z, gka_ref[...], bd512).astype(BF16)
        z = proj(1024, 1536)
        zq_ref[:, 1024:1536] = z.astype(BF16)
        z = proj(1536, 2048)
        zq_ref[:, 1536:2048] = z.astype(BF16)
        qs_ref[...] = (headnorm(z, gqs_ref[...], bd512) * SCALE).astype(BF16)
        z = proj(2048, 2176)
        zq_ref[:, 2048:2176] = z.astype(BF16)
        ks_ref[...] = headnorm(z, gks_ref[...], bd128).astype(BF16)
        z = proj(2176, 2304)
        zq_ref[:, 2176:2304] = z.astype(BF16)
        for c0 in range(0, 2 * d, gc):
            zg = proj(QKV_WIDTH + c0, QKV_WIDTH + c0 + gc) + b_ref[:, c0:c0 + gc]
            gt_ref[:, c0:c0 + gc] = _sigmoid(zg).astype(BF16)

    return pl.pallas_call(
        body, name=name, grid=(s // tm,),
        in_specs=[_rows(tm, d), _full((1, d)), _mat(*win_t), _full((1, 2 * d)),
                  _full((1, 512)), _full((1, 512)), _full((1, 512)), _full((1, 128)), _full((512, 512))],
        out_specs=[_rows(tm, d), _rows(tm, QKV_WIDTH), _rows(tm, 512), _rows(tm, 512), _rows(tm, 512),
                   _rows(tm, 128), _rows(tm, 2 * d)],
        out_shape=[jax.ShapeDtypeStruct((s, d), BF16), jax.ShapeDtypeStruct((s, QKV_WIDTH), BF16),
                   jax.ShapeDtypeStruct((s, 512), BF16), jax.ShapeDtypeStruct((s, 512), BF16),
                   jax.ShapeDtypeStruct((s, 512), BF16), jax.ShapeDtypeStruct((s, 128), BF16),
                   jax.ShapeDtypeStruct((s, 2 * d), BF16)],
        compiler_params=_params(),
    )(x, gain, win_t[0], b_gate, gq_na, gk_na, gq_sw, gk_sw, bd)


def _na_iotas():
    qc = lax.broadcasted_iota(jnp.int32, (GRID_W, LANES), 0)
    ln = lax.broadcasted_iota(jnp.int32, (GRID_W, LANES), 1)
    low = ln < GRID_W
    kc = jnp.where(low, ln, ln - GRID_W)
    diff = kc - qc + (NA_COLS - 1)
    qcs = jnp.clip(qc - NA_COLS // 2, 0, GRID_W - NA_COLS)
    inwin = (kc >= qcs) & (kc < qcs + NA_COLS)
    return diff, low, inwin


NA_RI = 2 * NA_ROWS - 1
NA_CI = 2 * NA_COLS - 1
NA_T2 = NA_RI + 1


def _rpb_rows(rpb):
    h = rpb.shape[0]
    padded = jnp.pad(rpb, ((0, 0), (1, 1), (0, GRID_W - NA_CI)))
    return jnp.concatenate([padded[:, :NA_T2], padded[:, 1:NA_T2 + 1]], axis=2).reshape(h, NA_T2, LANES)


def _rpb_from_rows(rows):
    return rows[:, 1:, :NA_CI] + rows[:, :NA_RI, GRID_W:GRID_W + NA_CI]


def rpb_expand(rows, dep, name):
    n_heads = rows.shape[0]

    def body(r_ref, dep_ref, o_ref):
        for h in range(n_heads):
            for e in range(NA_T2):
                line = jnp.broadcast_to(r_ref[h, e:e + 1, :], (GRID_W, LANES))
                o_ref[h, e] = pltpu.roll(line, LANES - (NA_COLS - 1), 1, stride=1, stride_axis=0)

    return pl.pallas_call(
        body, name=name,
        in_specs=[pl.BlockSpec(memory_space=pltpu.VMEM), pl.BlockSpec(memory_space=pltpu.VMEM)],
        out_specs=pl.BlockSpec(memory_space=pltpu.VMEM),
        out_shape=jax.ShapeDtypeStruct((n_heads, NA_T2, GRID_W, LANES), F32),
        compiler_params=pltpu.CompilerParams(vmem_limit_bytes=V7X_VMEM_LIMIT),
    )(rows, dep)


def rpb_reduce(dt2, name):
    n_heads = dt2.shape[0]
    flip = jnp.asarray(np.eye(GRID_W)[::-1], BF16)

    def body(d_ref, j_ref, o_ref):
        jm = j_ref[...]
        for h in range(n_heads):
            for e in range(NA_T2):
                dv = d_ref[h, e]
                hi = dv.astype(BF16)
                mid = (dv - hi.astype(F32)).astype(BF16)
                lo = (dv - hi.astype(F32) - mid.astype(F32)).astype(BF16)
                rev = _dot(jm, hi) + _dot(jm, mid) + _dot(jm, lo)
                back = pltpu.roll(rev, LANES + (NA_COLS - 1) - (GRID_W - 1), 1, stride=1, stride_axis=0)
                o_ref[h, e:e + 1, :] = jnp.sum(back, axis=0, keepdims=True)

    return pl.pallas_call(
        body, name=name,
        in_specs=[pl.BlockSpec(memory_space=pltpu.VMEM)] * 2,
        out_specs=pl.BlockSpec(memory_space=pltpu.VMEM),
        out_shape=jax.ShapeDtypeStruct((n_heads, NA_T2, LANES), F32),
        compiler_params=pltpu.CompilerParams(vmem_limit_bytes=V7X_VMEM_LIMIT),
    )(dt2, flip)


NA_TQ = 4
NA_TK = NA_TQ + NA_ROWS
NA_KCH = NA_TK // 2


def _na_tile_geometry(t, rows):
    r = t * NA_TQ
    kbase = jnp.clip(r - NA_ROWS // 2, 0, rows - NA_TK)
    starts = [jnp.clip(r + a - NA_ROWS // 2, 0, rows - NA_ROWS) for a in range(NA_TQ)]
    return r, kbase, starts


def _na_tile_mask(kbase, starts, low, inwin):
    half = jnp.where(low, 0, 1)
    cols = []
    for c in range(NA_KCH):
        krow = kbase + 2 * c + half
        cols.append(jnp.concatenate(
            [jnp.where(inwin & (krow >= st) & (krow < st + NA_ROWS), 0.0, NEG) for st in starts], axis=0))
    return jnp.concatenate(cols, axis=1)


def _na_tile_index(r, kbase, a, c):
    return jnp.clip(kbase + 2 * c - (r + a) + NA_ROWS, 0, NA_T2 - 1)


def _na_tile_scores(q, k, t2_ref, hh, r, kbase, madd):
    bias = jnp.concatenate(
        [jnp.concatenate([t2_ref[hh, _na_tile_index(r, kbase, a, c)] for a in range(NA_TQ)], axis=0)
         for c in range(NA_KCH)], axis=1)
    return _dotg(q, k, NT) + bias + madd


def _softmax_rows(sc):
    e = jnp.exp(sc - jnp.max(sc, axis=1, keepdims=True))
    return e * (1.0 / jnp.sum(e, axis=1, keepdims=True))


def na_fwd(qa, ka, zq, t2, name):
    s = qa.shape[0]
    rows = s // GRID_W
    n_pairs = NA_WIDTH // LANES
    v_blk0 = (2 * NA_WIDTH) // LANES

    assert rows % NA_TQ == 0 and rows >= NA_TK
    tq, tk = NA_TQ * GRID_W, NA_TK * GRID_W

    def body(q_ref, k_ref, v_ref, t2_ref, o_ref, s_scr, p_scr):
        _, low, inwin = _na_iotas()

        def tile(t, carry):
            r, kbase, starts = _na_tile_geometry(t, rows)
            madd = _na_tile_mask(kbase, starts, low, inwin)
            qr = pl.ds(pl.multiple_of(r * GRID_W, tq), tq)
            kr = pl.ds(pl.multiple_of(kbase * GRID_W, tq), tk)
            for hh in range(2):
                lanes = slice(HEAD_DIM * hh, HEAD_DIM * (hh + 1))
                s_scr[tq * hh:tq * (hh + 1), :] = _na_tile_scores(q_ref[qr, lanes], k_ref[kr, lanes], t2_ref, hh, r,
                                                                  kbase, madd)
            p_scr[...] = _softmax_rows(s_scr[...]).astype(BF16)
            for hh in range(2):
                lanes = slice(HEAD_DIM * hh, HEAD_DIM * (hh + 1))
                o_ref[qr, lanes] = _dot(p_scr[tq * hh:tq * (hh + 1), :], v_ref[kr, lanes]).astype(BF16)
            return carry

        lax.fori_loop(0, rows // NA_TQ, tile, 0)

    col = lambda off: pl.BlockSpec((s, LANES), lambda p, _o=off: (0, _o + p))
    return pl.pallas_call(
        body, name=name, grid=(n_pairs,),
        in_specs=[col(0), col(0), col(v_blk0),
                  pl.BlockSpec((2, NA_T2, GRID_W, LANES), lambda p: (p, 0, 0, 0))],
        out_specs=col(0),
        out_shape=jax.ShapeDtypeStruct((s, NA_WIDTH), BF16),
        scratch_shapes=[pltpu.VMEM((2 * tq, tk), F32), pltpu.VMEM((2 * tq, tk), BF16)],
        compiler_params=_params(),
    )(qa, ka, zq, t2)


def na_bwd(qa, ka, zq, t2, o_na, do_na, name):
    s = qa.shape[0]
    rows = s // GRID_W
    n_pairs = NA_WIDTH // LANES
    v_blk0 = (2 * NA_WIDTH) // LANES

    tq, tk = NA_TQ * GRID_W, NA_TK * GRID_W

    def body(q_ref, k_ref, v_ref, t2_ref, o_ref, do_ref, dq_ref, dk_ref, dv_ref, dt2_ref):
        _, low, inwin = _na_iotas()
        dk_ref[...] = jnp.zeros(dk_ref.shape, F32)
        dv_ref[...] = jnp.zeros(dv_ref.shape, F32)
        dt2_ref[...] = jnp.zeros(dt2_ref.shape, F32)

        def tile(t, carry):
            r, kbase, starts = _na_tile_geometry(t, rows)
            madd = _na_tile_mask(kbase, starts, low, inwin)
            qr = pl.ds(pl.multiple_of(r * GRID_W, tq), tq)
            kr = pl.ds(pl.multiple_of(kbase * GRID_W, tq), tk)
            for hh in range(2):
                lanes = slice(HEAD_DIM * hh, HEAD_DIM * (hh + 1))
                q, k, v = q_ref[qr, lanes], k_ref[kr, lanes], v_ref[kr, lanes]
                p = _softmax_rows(_na_tile_scores(q, k, t2_ref, hh, r, kbase, madd))
                do = do_ref[qr, lanes]
                delta = jnp.sum(do.astype(F32) * o_ref[qr, lanes].astype(F32), axis=1, keepdims=True)
                ds = p * (_dotg(do, v, NT) - delta)
                for a in range(NA_TQ):
                    for c in range(NA_KCH):
                        e = _na_tile_index(r, kbase, a, c)
                        dt2_ref[hh, e] = dt2_ref[hh, e] + ds[GRID_W * a:GRID_W * (a + 1), LANES * c:LANES * (c + 1)]
                dsb = ds.astype(BF16)
                dq_ref[qr, lanes] = _dot(dsb, k)
                dk_ref[kr, lanes] = dk_ref[kr, lanes] + _dotg(dsb, q, TN)
                dv_ref[kr, lanes] = dv_ref[kr, lanes] + _dotg(p.astype(BF16), do, TN)
            return carry

        lax.fori_loop(0, rows // NA_TQ, tile, 0)

    col = lambda off: pl.BlockSpec((s, LANES), lambda p, _o=off: (0, _o + p))
    t2spec = pl.BlockSpec((2, NA_T2, GRID_W, LANES), lambda p: (p, 0, 0, 0))
    return pl.pallas_call(
        body, name=name, grid=(n_pairs,),
        in_specs=[col(0), col(0), col(v_blk0), t2spec, col(0), col(0)],
        out_specs=[col(0), col(0), col(0), t2spec],
        out_shape=[jax.ShapeDtypeStruct((s, NA_WIDTH), F32)] * 3 + [jax.ShapeDtypeStruct(t2.shape, F32)],
        compiler_params=_params(),
    )(qa, ka, zq, t2, o_na, do_na)


def _t5_bucket_map():
    rel = np.arange(3 * SW_BLOCK)[None, :] - SW_BLOCK - np.arange(SW_BLOCK)[:, None]
    nb = REL_BUCKETS // 2
    max_exact = nb // 2
    n = np.abs(rel)
    large = max_exact + (np.log(np.maximum(n, 1) / max_exact)
                         / np.log(REL_MAX_DIST / max_exact) * (nb - max_exact)).astype(np.int32)
    large = np.minimum(large, nb - 1)
    return ((rel > 0) * nb + np.where(n < max_exact, n, large)).astype(np.int32)


def t5_expand(table, bmap, dep, name):
    def body(tab_ref, bm_ref, dep_ref, o_ref):
        bm = bm_ref[...]
        for h in range(SW_HEADS):
            t = jnp.zeros(bm.shape, F32)
            for b in range(REL_BUCKETS):
                t = jnp.where(bm == b, tab_ref[b, h], t)
            o_ref[h] = t

    return pl.pallas_call(
        body, name=name,
        in_specs=[pl.BlockSpec(memory_space=pltpu.SMEM), pl.BlockSpec(memory_space=pltpu.VMEM),
                  pl.BlockSpec(memory_space=pltpu.VMEM)],
        out_specs=pl.BlockSpec(memory_space=pltpu.VMEM),
        out_shape=jax.ShapeDtypeStruct((SW_HEADS,) + bmap.shape, F32),
        compiler_params=pltpu.CompilerParams(vmem_limit_bytes=V7X_VMEM_LIMIT),
    )(table, bmap, dep)


def t5_reduce(dbias_list, bmap, name):
    n = len(dbias_list)

    def body(*refs):
        d_refs, bm_ref, o_ref = refs[:n], refs[n], refs[n + 1]
        bm = bm_ref[...]
        for h in range(SW_HEADS):
            dv = d_refs[0][h]
            for other in d_refs[1:]:
                dv = dv + other[h]
            rows = [jnp.sum(jnp.where(bm == b, dv, 0.0), axis=0, keepdims=True) for b in range(REL_BUCKETS)]
            r = jnp.concatenate(rows, axis=0)
            o_ref[h] = jnp.broadcast_to(jnp.sum(r, axis=1, keepdims=True), (REL_BUCKETS, LANES))

    return pl.pallas_call(
        body, name=name,
        in_specs=[pl.BlockSpec(memory_space=pltpu.VMEM)] * (n + 1),
        out_specs=pl.BlockSpec(memory_space=pltpu.VMEM),
        out_shape=jax.ShapeDtypeStruct((SW_HEADS, REL_BUCKETS, LANES), F32),
        compiler_params=pltpu.CompilerParams(vmem_limit_bytes=V7X_VMEM_LIMIT),
    )(*dbias_list, bmap)


def _sw_mask_iotas():
    a = lax.broadcasted_iota(jnp.int32, (SW_BLOCK, 3 * SW_BLOCK), 0)
    j = lax.broadcasted_iota(jnp.int32, (SW_BLOCK, 3 * SW_BLOCK), 1)
    inwin = jnp.abs(j - SW_BLOCK - a) <= SW_BLOCK
    return j, inwin


SW_STACK = SW_HEADS * SW_BLOCK


def _sw_softmax(sc, sk):
    m = jnp.maximum(jnp.max(sc, axis=1, keepdims=True), sk)
    e = jnp.exp(sc - m)
    es = jnp.exp(sk - m)
    inv = 1.0 / (jnp.sum(e, axis=1, keepdims=True) + es)
    return e * inv, es * inv


def _sw_prologue(k_ref, v_ref, kp, vp, sink_ref, s):
    pad = s + 2 * SW_BLOCK
    zeros = jnp.zeros((SW_BLOCK, SW_KV_WIDTH), BF16)
    kp[0:SW_BLOCK, :] = zeros
    vp[0:SW_BLOCK, :] = zeros
    kp[SW_BLOCK + s:pad, :] = zeros
    vp[SW_BLOCK + s:pad, :] = zeros
    kp[SW_BLOCK:SW_BLOCK + s, :] = k_ref[...]
    vp[SW_BLOCK:SW_BLOCK + s, :] = v_ref[...]
    return jnp.concatenate([jnp.full((SW_BLOCK, 1), sink_ref[h], F32) for h in range(SW_HEADS)], axis=0)


def sw_fwd(qs, ks, zq, t5b, sink, dep, name):
    s = qs.shape[0]
    nb = s // SW_BLOCK
    v_blk = (3 * NA_WIDTH + SW_Q_WIDTH + SW_KV_WIDTH) // LANES
    pad = s + 2 * SW_BLOCK

    def body(q_ref, k_ref, v_ref, b_ref, sink_ref, dep_ref, o_ref, kp, vp, s_scr, p_scr):
        sink_col = _sw_prologue(k_ref, v_ref, kp, vp, sink_ref, s)
        j, inwin = _sw_mask_iotas()

        def blk(n, carry):
            kpos = n * SW_BLOCK - SW_BLOCK + j
            madd = jnp.where(inwin & (kpos >= 0) & (kpos < s), 0.0, NEG)
            q0 = pl.multiple_of(n * SW_BLOCK, SW_BLOCK)
            qr, kr = pl.ds(q0, SW_BLOCK), pl.ds(q0, 3 * SW_BLOCK)
            for h in range(SW_HEADS):
                g = h // SW_REP
                s_scr[SW_BLOCK * h:SW_BLOCK * (h + 1), :] = _dotg(
                    q_ref[qr, HEAD_DIM * h:HEAD_DIM * (h + 1)], kp[kr, HEAD_DIM * g:HEAD_DIM * (g + 1)], NT) + madd
            p, _ = _sw_softmax(s_scr[...] + b_ref[...], sink_col)
            p_scr[...] = p.astype(BF16)
            for h in range(SW_HEADS):
                g = h // SW_REP
                o_ref[qr, HEAD_DIM * h:HEAD_DIM * (h + 1)] = _dot(
                    p_scr[SW_BLOCK * h:SW_BLOCK * (h + 1), :], vp[kr, HEAD_DIM * g:HEAD_DIM * (g + 1)]).astype(BF16)
            return carry

        lax.fori_loop(0, nb, blk, 0)

    return pl.pallas_call(
        body, name=name, grid=(1,),
        in_specs=[_full((s, SW_Q_WIDTH)), _full((s, SW_KV_WIDTH)),
                  pl.BlockSpec((s, SW_KV_WIDTH), lambda i: (0, v_blk)),
                  _full((SW_STACK, 3 * SW_BLOCK)), pl.BlockSpec(memory_space=pltpu.SMEM),
                  _full(dep.shape)],
        out_specs=_full((s, SW_Q_WIDTH)),
        out_shape=jax.ShapeDtypeStruct((s, SW_Q_WIDTH), BF16),
        scratch_shapes=[pltpu.VMEM((pad, SW_KV_WIDTH), BF16), pltpu.VMEM((pad, SW_KV_WIDTH), BF16),
                        pltpu.VMEM((SW_STACK, 3 * SW_BLOCK), F32), pltpu.VMEM((SW_STACK, 3 * SW_BLOCK), BF16)],
        compiler_params=_params(),
    )(qs, ks, zq, t5b, sink, dep)


def sw_bwd(qs, ks, zq, t5b, sink, o_sw, do_sw, name):
    s = qs.shape[0]
    nb = s // SW_BLOCK
    v_blk = (3 * NA_WIDTH + SW_Q_WIDTH + SW_KV_WIDTH) // LANES
    pad = s + 2 * SW_BLOCK

    def body(q_ref, k_ref, v_ref, b_ref, sink_ref, o_ref, do_ref,
             dq_ref, dk_ref, dv_ref, db_ref, dsk_ref, kp, vp, dkp, dvp, s_scr, dp_scr, ds_scr, p_scr):
        sink_col = _sw_prologue(k_ref, v_ref, kp, vp, sink_ref, s)
        dkp[...] = jnp.zeros(dkp.shape, F32)
        dvp[...] = jnp.zeros(dvp.shape, F32)
        db_ref[...] = jnp.zeros(db_ref.shape, F32)
        dsk_ref[...] = jnp.zeros(dsk_ref.shape, F32)
        j, inwin = _sw_mask_iotas()

        def blk(n, carry):
            kpos = n * SW_BLOCK - SW_BLOCK + j
            madd = jnp.where(inwin & (kpos >= 0) & (kpos < s), 0.0, NEG)
            q0 = pl.multiple_of(n * SW_BLOCK, SW_BLOCK)
            qr, kr = pl.ds(q0, SW_BLOCK), pl.ds(q0, 3 * SW_BLOCK)
            deltas = []
            for h in range(SW_HEADS):
                g = h // SW_REP
                hl, kl = slice(HEAD_DIM * h, HEAD_DIM * (h + 1)), slice(HEAD_DIM * g, HEAD_DIM * (g + 1))
                rows = slice(SW_BLOCK * h, SW_BLOCK * (h + 1))
                do = do_ref[qr, hl]
                s_scr[rows, :] = _dotg(q_ref[qr, hl], kp[kr, kl], NT) + madd
                dp_scr[rows, :] = _dotg(do, vp[kr, kl], NT)
                deltas.append(jnp.sum(do.astype(F32) * o_ref[qr, hl].astype(F32), axis=1, keepdims=True))
            delta = jnp.concatenate(deltas, axis=0)
            p, ps = _sw_softmax(s_scr[...] + b_ref[...], sink_col)
            ds = p * (dp_scr[...] - delta)
            db_ref[...] = db_ref[...] + ds
            dsk_ref[...] = dsk_ref[...] - jnp.broadcast_to(ps * delta, (SW_STACK, LANES))
            ds_scr[...] = ds.astype(BF16)
            p_scr[...] = p.astype(BF16)
            for g in range(SW_HEADS // SW_REP):
                kl = slice(HEAD_DIM * g, HEAD_DIM * (g + 1))
                k = kp[kr, kl]
                dkw = jnp.zeros((3 * SW_BLOCK, HEAD_DIM), F32)
                dvw = jnp.zeros((3 * SW_BLOCK, HEAD_DIM), F32)
                for r in range(SW_REP):
                    h = g * SW_REP + r
                    hl, rows = slice(HEAD_DIM * h, HEAD_DIM * (h + 1)), slice(SW_BLOCK * h, SW_BLOCK * (h + 1))
                    dsb = ds_scr[rows, :]
                    dq_ref[qr, hl] = _dot(dsb, k)
                    dkw = dkw + _dotg(dsb, q_ref[qr, hl], TN)
                    dvw = dvw + _dotg(p_scr[rows, :], do_ref[qr, hl], TN)
                dkp[kr, kl] = dkp[kr, kl] + dkw
                dvp[kr, kl] = dvp[kr, kl] + dvw
            return carry

        lax.fori_loop(0, nb, blk, 0)
        dk_ref[...] = dkp[SW_BLOCK:SW_BLOCK + s, :]
        dv_ref[...] = dvp[SW_BLOCK:SW_BLOCK + s, :]

    bias_spec = _full((SW_STACK, 3 * SW_BLOCK))
    return pl.pallas_call(
        body, name=name, grid=(1,),
        in_specs=[_full((s, SW_Q_WIDTH)), _full((s, SW_KV_WIDTH)),
                  pl.BlockSpec((s, SW_KV_WIDTH), lambda i: (0, v_blk)),
                  bias_spec, pl.BlockSpec(memory_space=pltpu.SMEM),
                  _full((s, SW_Q_WIDTH)), _full((s, SW_Q_WIDTH))],
        out_specs=[_full((s, SW_Q_WIDTH)), _full((s, SW_KV_WIDTH)), _full((s, SW_KV_WIDTH)), bias_spec,
                   _full((SW_STACK, LANES))],
        out_shape=[jax.ShapeDtypeStruct((s, SW_Q_WIDTH), F32), jax.ShapeDtypeStruct((s, SW_KV_WIDTH), F32),
                   jax.ShapeDtypeStruct((s, SW_KV_WIDTH), F32),
                   jax.ShapeDtypeStruct((SW_STACK, 3 * SW_BLOCK), F32),
                   jax.ShapeDtypeStruct((SW_STACK, LANES), F32)],
        scratch_shapes=[pltpu.VMEM((pad, SW_KV_WIDTH), BF16), pltpu.VMEM((pad, SW_KV_WIDTH), BF16),
                        pltpu.VMEM((pad, SW_KV_WIDTH), F32), pltpu.VMEM((pad, SW_KV_WIDTH), F32),
                        pltpu.VMEM((SW_STACK, 3 * SW_BLOCK), F32), pltpu.VMEM((SW_STACK, 3 * SW_BLOCK), F32),
                        pltpu.VMEM((SW_STACK, 3 * SW_BLOCK), BF16), pltpu.VMEM((SW_STACK, 3 * SW_BLOCK), BF16)],
        compiler_params=_params(),
    )(qs, ks, zq, t5b, sink, o_sw, do_sw)


def merge_out(x, o_na, o_sw, gt, wbna_t, wbsw_t, wout, name):
    s, d = x.shape
    tm = _row_tile(s)

    def body(x_ref, ona_ref, osw_ref, gt_ref, wna_ref, wsw_ref, wo_ref, xo_ref, ana_ref, asw_ref, mg_ref):
        a_na = _dotg(ona_ref[...], wna_ref[...], NT)
        a_sw = _dotg(osw_ref[...], wsw_ref[...], NT)
        ana_ref[...] = a_na.astype(BF16)
        asw_ref[...] = a_sw.astype(BF16)
        merged = (gt_ref[:, 0:d].astype(F32) * a_na + gt_ref[:, d:2 * d].astype(F32) * a_sw).astype(BF16)
        mg_ref[...] = merged
        xo_ref[...] = x_ref[...] + _dot(merged, wo_ref[...])

    return pl.pallas_call(
        body, name=name, grid=(s // tm,),
        in_specs=[_rows(tm, d), _rows(tm, 512), _rows(tm, 512), _rows(tm, 2 * d),
                  _mat(*wbna_t), _mat(*wbsw_t), _mat(*wout)],
        out_specs=[_rows(tm, d)] * 4,
        out_shape=[jax.ShapeDtypeStruct((s, d), F32)] + [jax.ShapeDtypeStruct((s, d), BF16)] * 3,
        compiler_params=_params(),
    )(x, o_na, o_sw, gt, wbna_t[0], wbsw_t[0], wout[0])


def mix_bwd_out(dx, gt, a_na, a_sw, wbna_t, wbsw_t, wout, dep, name):
    s, d = dx.shape
    tm = _row_tile(s)

    def body(dx_ref, gt_ref, ana_ref, asw_ref, wna_ref, wsw_ref, wo_ref, dep_ref,
             dxb_ref, dzg_ref, dana_ref, dasw_ref, dona_ref, dosw_ref, dbg_ref):
        @pl.when(pl.program_id(0) == 0)
        def _():
            dbg_ref[...] = jnp.zeros(dbg_ref.shape, F32)

        dxb = dx_ref[...].astype(BF16)
        dxb_ref[...] = dxb
        dm = _dotg(dxb, wo_ref[...], NT)
        for i, (a_ref, da_ref, w_ref, do_ref) in enumerate(
                [(ana_ref, dana_ref, wna_ref, dona_ref), (asw_ref, dasw_ref, wsw_ref, dosw_ref)]):
            gi = gt_ref[:, i * d:(i + 1) * d].astype(F32)
            da = (dm * gi).astype(BF16)
            da_ref[...] = da
            do_ref[...] = _dot(da, w_ref[...]).astype(BF16)
            dzg = dm * a_ref[...].astype(F32) * gi * (1.0 - gi)
            dzg_ref[:, i * d:(i + 1) * d] = dzg.astype(BF16)
            dbg_ref[:, i * d:(i + 1) * d] = dbg_ref[:, i * d:(i + 1) * d] + jnp.sum(dzg, axis=0, keepdims=True)

    return pl.pallas_call(
        body, name=name, grid=(s // tm,),
        in_specs=[_rows(tm, d), _rows(tm, 2 * d), _rows(tm, d), _rows(tm, d),
                  _mat(*wbna_t), _mat(*wbsw_t), _mat(*wout), _full(dep.shape)],
        out_specs=[_rows(tm, d), _rows(tm, 2 * d), _rows(tm, d), _rows(tm, d), _rows(tm, 512), _rows(tm, 512),
                   _full((1, 2 * d))],
        out_shape=[jax.ShapeDtypeStruct((s, d), BF16), jax.ShapeDtypeStruct((s, 2 * d), BF16),
                   jax.ShapeDtypeStruct((s, d), BF16), jax.ShapeDtypeStruct((s, d), BF16),
                   jax.ShapeDtypeStruct((s, 512), BF16), jax.ShapeDtypeStruct((s, 512), BF16),
                   jax.ShapeDtypeStruct((1, 2 * d), F32)],
        compiler_params=_params(),
    )(dx, gt, a_na, a_sw, wbna_t[0], wbsw_t[0], wout[0], dep)


def qk_norm_bwd(dqa, dka, dva, dqs, dks, dvs, zq, dzg, gq_na, gk_na, gq_sw, gk_sw, bd, name):
    s = zq.shape[0]
    d2 = dzg.shape[1]
    n_in = QKV_WIDTH + d2
    tm = _row_tile(s)

    def body(dqa_ref, dka_ref, dva_ref, dqs_ref, dks_ref, dvs_ref, zq_ref, dzg_ref,
             gqa_ref, gka_ref, gqs_ref, gks_ref, bd_ref, dz_ref, dgqa_ref, dgka_ref, dgqs_ref, dgks_ref):
        @pl.when(pl.program_id(0) == 0)
        def _():
            for r in (dgqa_ref, dgka_ref, dgqs_ref, dgks_ref):
                r[...] = jnp.zeros(r.shape, F32)

        bd512 = bd_ref[...]
        bd128 = bd_ref[0:SW_KV_WIDTH, 0:SW_KV_WIDTH]

        def one(c0, c1, dy_ref, g_ref, dg_ref, bdm, scale):
            z = zq_ref[:, c0:c1].astype(F32)
            r = lax.rsqrt(_group_mean(z * z, bdm) + EPS)
            zh = z * r
            dy = dy_ref[...] * scale
            dyg = dy * g_ref[...]
            dz = r * (dyg - zh * _group_mean(dyg * zh, bdm))
            dz_ref[:, c0:c1] = dz.astype(BF16)
            dg_ref[...] = dg_ref[...] + jnp.sum(dy * zh, axis=0, keepdims=True)

        one(0, 512, dqa_ref, gqa_ref, dgqa_ref, bd512, SCALE)
        one(512, 1024, dka_ref, gka_ref, dgka_ref, bd512, 1.0)
        dz_ref[:, 1024:1536] = dva_ref[...].astype(BF16)
        one(1536, 2048, dqs_ref, gqs_ref, dgqs_ref, bd512, SCALE)
        one(2048, 2176, dks_ref, gks_ref, dgks_ref, bd128, 1.0)
        dz_ref[:, 2176:2304] = dvs_ref[...].astype(BF16)
        dz_ref[:, QKV_WIDTH:n_in] = dzg_ref[...]

    return pl.pallas_call(
        body, name=name, grid=(s // tm,),
        in_specs=[_rows(tm, 512), _rows(tm, 512), _rows(tm, 512), _rows(tm, 512), _rows(tm, 128), _rows(tm, 128),
                  _rows(tm, QKV_WIDTH), _rows(tm, d2),
                  _full((1, 512)), _full((1, 512)), _full((1, 512)), _full((1, 128)), _full((512, 512))],
        out_specs=[_rows(tm, n_in), _full((1, 512)), _full((1, 512)), _full((1, 512)), _full((1, 128))],
        out_shape=[jax.ShapeDtypeStruct((s, n_in), BF16)] + [jax.ShapeDtypeStruct((1, 512), F32)] * 3
                  + [jax.ShapeDtypeStruct((1, 128), F32)],
        compiler_params=_params(),
    )(dqa, dka, dva, dqs, dks, dvs, zq, dzg, gq_na, gk_na, gq_sw, gk_sw, bd)


def ffn_bwd_act(dx, wd, hg, hu, name):
    s, d = dx.shape
    f = wd[0].shape[1]
    tm = _row_tile(s)
    fc = _col_chunk(f)

    def body(dx_ref, w_ref, hg_ref, hu_ref, dxb_ref, dhg_ref, dhu_ref):
        dxb = dx_ref[...].astype(BF16)
        dxb_ref[...] = dxb
        for c0 in range(0, f, fc):
            dact = 0.5 * _dotg(dxb, w_ref[c0:c0 + fc, :], NT)
            hg = hg_ref[:, c0:c0 + fc].astype(F32)
            hu = hu_ref[:, c0:c0 + fc].astype(F32)
            sg = _sigmoid(hg)
            dhu_ref[:, c0:c0 + fc] = (dact * hg * sg).astype(BF16)
            dhg_ref[:, c0:c0 + fc] = (dact * hu * sg * (1.0 + hg * (1.0 - sg))).astype(BF16)

    return pl.pallas_call(
        body, name=name, grid=(s // tm,),
        in_specs=[_rows(tm, d), _mat(*wd), _rows(tm, f), _rows(tm, f)],
        out_specs=[_rows(tm, d), _rows(tm, f), _rows(tm, f)],
        out_shape=[jax.ShapeDtypeStruct((s, d), BF16), jax.ShapeDtypeStruct((s, f), BF16),
                   jax.ShapeDtypeStruct((s, f), BF16)],
        compiler_params=_params(),
    )(dx, wd[0], hg, hu)


def proj_bwd_norm(acts, weights, x, gain, dx, dep, name):
    s, d = x.shape
    tm = _row_tile(s)
    n = len(acts)

    def body(*refs):
        a_refs, w_refs = refs[:n], refs[n:2 * n]
        x_ref, g_ref, dx_ref, _, o_ref, dg_ref = refs[2 * n:]

        @pl.when(pl.program_id(0) == 0)
        def _():
            dg_ref[...] = jnp.zeros(dg_ref.shape, F32)

        dxn = _dot(a_refs[0][...], w_refs[0][...])
        for a_ref, w_ref in zip(a_refs[1:], w_refs[1:]):
            dxn = dxn + _dot(a_ref[...], w_ref[...])
        xv = x_ref[...]
        r = _rstd(xv)
        xh = xv * r
        dxh = dxn * g_ref[...]
        o_ref[...] = dx_ref[...] + r * (dxh - xh * jnp.mean(dxh * xh, axis=-1, keepdims=True))
        dg_ref[...] = dg_ref[...] + jnp.sum(dxn * xh, axis=0, keepdims=True)

    return pl.pallas_call(
        body, name=name, grid=(s // tm,),
        in_specs=[_rows(tm, a.shape[1]) for a in acts] + [_mat(*w) for w in weights]
                 + [_rows(tm, d), _full((1, d)), _rows(tm, d), _full(dep.shape)],
        out_specs=[_rows(tm, d), _full((1, d))],
        out_shape=[jax.ShapeDtypeStruct((s, d), F32), jax.ShapeDtypeStruct((1, d), F32)],
        compiler_params=_params(),
    )(*acts, *[w[0] for w in weights], x, gain, dx, dep)


def tn_matmul(a, b, scale, name):
    s, n = a.shape
    k = b.shape[1]
    tn = _tn_tile(n)

    def body(a_ref, b_ref, o_ref):
        o_ref[...] = (scale * _dotg(a_ref[...], b_ref[...], TN)).astype(BF16)

    return pl.pallas_call(
        body, name=name, grid=(n // tn,),
        in_specs=[pl.BlockSpec((s, tn), lambda i: (0, i)),
                  pl.BlockSpec((s, k), lambda i: (0, 0), pipeline_mode=ONCE)],
        out_specs=pl.BlockSpec((tn, k), lambda i: (i, 0)),
        out_shape=jax.ShapeDtypeStruct((n, k), BF16),
        compiler_params=_params(),
    )(a, b)


def loss_grad(y, target, name):
    s, d = y.shape
    tm = _row_tile(s)

    def body(y_ref, t_ref, dy_ref, acc_ref):
        @pl.when(pl.program_id(0) == 0)
        def _():
            acc_ref[...] = jnp.zeros(acc_ref.shape, F32)

        err = y_ref[...] - t_ref[...]
        dy_ref[...] = err * (1.0 / d)
        e2 = err * err
        part = jnp.sum(e2.reshape(tm // 8, 8, d), axis=0)
        acc = part[:, 0:LANES]
        for c0 in range(LANES, d, LANES):
            acc = acc + part[:, c0:c0 + LANES]
        acc_ref[...] = acc_ref[...] + acc

    return pl.pallas_call(
        body, name=name, grid=(s // tm,),
        in_specs=[_rows(tm, d), _rows(tm, d)],
        out_specs=[_rows(tm, d), _full((8, LANES))],
        out_shape=[jax.ShapeDtypeStruct((s, d), F32), jax.ShapeDtypeStruct((8, LANES), F32)],
        compiler_params=_params(),
    )(y, target)


def _mesh_pos():
    return lax.axis_index("x"), lax.axis_index("y"), lax.axis_index("c")


def _peers():
    x, y, c = _mesh_pos()
    peers = []
    for rel in range(1, N_DEV):
        peers.append((1 - x if rel & 4 else x, 1 - y if rel & 2 else y, 1 - c if rel & 1 else c))
    return 4 * x + 2 * y + c, peers


HBM_SPEC = pl.BlockSpec(memory_space=pltpu.HBM)
SEM_SPEC = pl.BlockSpec(memory_space=pltpu.SEMAPHORE)


def _split_call(body, name, thru, n_sems, extra=(), with_token=True):
    hbm = lambda t: pltpu.with_memory_space_constraint(t, pltpu.HBM)
    effect = pltpu.CompilerParams(has_side_effects=pltpu.SideEffectType.DATAFLOW_SIDE_EFFECTING)
    nt = len(thru)
    thru_shapes = [pltpu.HBM(t.shape, t.dtype) for t in thru]
    if with_token:
        (after,) = extra
        outs = pl.pallas_call(
            body, name=name, in_specs=[HBM_SPEC] * nt + [pl.BlockSpec(memory_space=pl.ANY)],
            out_specs=[SEM_SPEC] * len(n_sems) + [HBM_SPEC] * nt + [pl.BlockSpec(memory_space=pltpu.VMEM)],
            out_shape=[pltpu.SemaphoreType.DMA((k,)) for k in n_sems] + thru_shapes
                      + [jax.ShapeDtypeStruct((8, LANES), F32)],
            input_output_aliases={i: len(n_sems) + i for i in range(nt)}, compiler_params=effect,
        )(*[hbm(t) for t in thru], after)
        return outs[:len(n_sems)], outs[len(n_sems):-1], outs[-1]
    return pl.pallas_call(
        body, name=name,
        in_specs=[HBM_SPEC] * nt + [SEM_SPEC] * len(n_sems) + [pl.BlockSpec(memory_space=pl.ANY)],
        out_specs=[HBM_SPEC] * nt, out_shape=thru_shapes,
        input_output_aliases={i: i for i in range(nt)}, compiler_params=effect,
    )(*thru, *extra)


def _gather_targets():
    x, y, c = _mesh_pos()
    return 4 * x + 2 * y + c, [(x, y, 1 - c), (1 - x, y, c), (x, 1 - y, c), (1 - x, 1 - y, c)]


def gather_start(shards, after, name):
    n = len(shards)
    zones = [lax.empty((w.shape[0], N_DEV) + w.shape[1:], w.dtype) for w in shards]

    def body(*refs):
        ins, zs = refs[:n], refs[n:2 * n]
        send_sems, recv_sems, local_sems = refs[2 * n + 1:2 * n + 4]
        token = refs[-1]
        me, targets = _gather_targets()
        for a in range(n):
            pltpu.make_async_copy(ins[a], zs[a].at[:, me], local_sems.at[a]).start()
            for k, to in enumerate(targets):
                pltpu.make_async_remote_copy(
                    src_ref=ins[a], dst_ref=zs[a].at[:, me], send_sem=send_sems.at[4 * a + k],
                    recv_sem=recv_sems.at[4 * a + k], device_id=to, device_id_type=MESH).start()
        token[...] = jnp.zeros(token.shape, F32)

    sems, thru, token = _split_call(body, name, list(shards) + zones, (4 * n, 4 * n, n), extra=(after,))
    return (sems, thru, n), token


def gather_wait(started, after, name):
    sems, thru, n = started

    def body(*refs):
        zs = refs[n:2 * n]
        send_sems, recv_sems, local_sems = refs[2 * n:2 * n + 3]
        _, targets = _gather_targets()
        for a in range(n):
            for k, to in enumerate(targets):
                cp = pltpu.make_async_remote_copy(
                    src_ref=zs[a].at[:, 0], dst_ref=zs[a].at[:, 0], send_sem=send_sems.at[4 * a + k],
                    recv_sem=recv_sems.at[4 * a + k], device_id=to, device_id_type=MESH)
                cp.wait_send()
                cp.wait_recv()
            pltpu.make_async_copy(zs[a].at[:, 0], zs[a].at[:, 0], local_sems.at[a]).wait()

    return _split_call(body, name, thru, (4 * n, 4 * n, n), extra=(*sems, after), with_token=False)[n:]


def forward_start(zones, after, name):
    n = len(zones)

    def body(*refs):
        zs = refs[:n]
        send_sems, recv_sems = refs[n + 1:n + 3]
        token = refs[-1]
        x, y, c = _mesh_pos()
        for a in range(n):
            for j, chip in enumerate([(1 - x, y), (x, 1 - y), (1 - x, 1 - y)]):
                blk = zs[a].at[:, 4 * chip[0] + 2 * chip[1] + c]
                pltpu.make_async_remote_copy(
                    src_ref=blk, dst_ref=blk, send_sem=send_sems.at[3 * a + j], recv_sem=recv_sems.at[3 * a + j],
                    device_id=(x, y, 1 - c), device_id_type=MESH).start()
        token[...] = jnp.zeros(token.shape, F32)

    sems, thru, token = _split_call(body, name, list(zones), (3 * n, 3 * n), extra=(after,))
    return (sems, thru, n), token


def forward_wait(started, after, name):
    sems, thru, n = started

    def body(*refs):
        zs = refs[:n]
        send_sems, recv_sems = refs[n:n + 2]
        x, y, c = _mesh_pos()
        for a in range(n):
            for j in range(3):
                cp = pltpu.make_async_remote_copy(
                    src_ref=zs[a].at[:, 0], dst_ref=zs[a].at[:, 0], send_sem=send_sems.at[3 * a + j],
                    recv_sem=recv_sems.at[3 * a + j], device_id=(x, y, 1 - c), device_id_type=MESH)
                cp.wait_send()
                cp.wait_recv()

    return _split_call(body, name, thru, (3 * n, 3 * n), extra=(*sems, after), with_token=False)


def scatter_start(groups, name):
    n = len(groups)
    flat = [g for grp in groups for g in grp]
    nf = len(flat)
    offs = np.cumsum([0] + [len(grp) for grp in groups])
    lands = [lax.empty((N_DEV, len(grp)) + grp[0].shape[1:], grp[0].dtype) for grp in groups]

    def body(*refs):
        ins, zones = refs[:nf], refs[nf:nf + n]
        send_sems, recv_sems, local_sems = refs[nf + n:nf + n + 3]
        token = refs[-1]
        me, peers = _peers()
        for a in range(n):
            for w in range(len(groups[a])):
                pltpu.make_async_copy(ins[offs[a] + w].at[me], zones[a].at[me, w], local_sems.at[a]).start()
        for k, peer in enumerate(peers):
            p_id = 4 * peer[0] + 2 * peer[1] + peer[2]
            for a in range(n):
                for w in range(len(groups[a])):
                    pltpu.make_async_remote_copy(
                        src_ref=ins[offs[a] + w].at[p_id], dst_ref=zones[a].at[me, w],
                        send_sem=send_sems.at[7 * a + k], recv_sem=recv_sems.at[7 * a + k],
                        device_id=peer, device_id_type=MESH).start()
        token[...] = jnp.zeros(token.shape, F32)

    hbm = lambda t: pltpu.with_memory_space_constraint(t, pltpu.HBM)
    outs = pl.pallas_call(
        body, name=name,
        in_specs=[HBM_SPEC] * (nf + n),
        out_specs=[SEM_SPEC] * 3 + [HBM_SPEC] * (nf + n) + [pl.BlockSpec(memory_space=pltpu.VMEM)],
        out_shape=[pltpu.SemaphoreType.DMA((7 * n,)), pltpu.SemaphoreType.DMA((7 * n,)), pltpu.SemaphoreType.DMA((n,))]
                  + [pltpu.HBM(t.shape, t.dtype) for t in flat + lands]
                  + [jax.ShapeDtypeStruct((8, LANES), F32)],
        input_output_aliases={i: 3 + i for i in range(nf + n)},
        compiler_params=pltpu.CompilerParams(has_side_effects=pltpu.SideEffectType.DATAFLOW_SIDE_EFFECTING),
    )(*[hbm(t) for t in flat], *[hbm(t) for t in lands])
    sems, thru, token = outs[:3], outs[3:3 + nf + n], outs[-1]
    return (sems, thru, [len(grp) for grp in groups]), token


def scatter_wait(started, after, name):
    (send_sems, recv_sems, local_sems), thru, sizes = started
    n = len(sizes)
    nf = len(thru) - n

    def body(*refs):
        zones = refs[nf:nf + n]
        s_sems, r_sems, l_sems = refs[nf + n:nf + n + 3]
        me, peers = _peers()
        for a in range(n):
            for k, peer in enumerate(peers):
                cp = pltpu.make_async_remote_copy(
                    src_ref=zones[a].at[0], dst_ref=zones[a].at[0],
                    send_sem=s_sems.at[7 * a + k], recv_sem=r_sems.at[7 * a + k], device_id=peer,
                    device_id_type=MESH)
                cp.wait_send()
                cp.wait_recv()
            pltpu.make_async_copy(zones[a].at[0], zones[a].at[0], l_sems.at[a]).wait()

    outs = pl.pallas_call(
        body, name=name,
        in_specs=[HBM_SPEC] * (nf + n) + [SEM_SPEC] * 3 + [pl.BlockSpec(memory_space=pl.ANY)],
        out_specs=[HBM_SPEC] * (nf + n),
        out_shape=[pltpu.HBM(t.shape, t.dtype) for t in thru],
        input_output_aliases={i: i for i in range(nf + n)},
        compiler_params=pltpu.CompilerParams(has_side_effects=pltpu.SideEffectType.DATAFLOW_SIDE_EFFECTING),
    )(*thru, send_sems, recv_sems, local_sems, after)
    return outs[nf:]


def pair_start(grads, after, name):
    nw = len(grads)
    land = lax.empty((4, nw) + grads[0].shape[1:], grads[0].dtype)

    def body(*refs):
        ins, zone = refs[:nw], refs[nw]
        send_sems, recv_sems = refs[nw + 2:nw + 4]
        x, y, c = _mesh_pos()
        for j in range(4):
            for w in range(nw):
                pltpu.make_async_remote_copy(
                    src_ref=ins[w].at[2 * j + (1 - c)], dst_ref=zone.at[j, w], send_sem=send_sems.at[0],
                    recv_sem=recv_sems.at[0], device_id=(x, y, 1 - c), device_id_type=MESH).start()
        refs[-1][...] = jnp.zeros(refs[-1].shape, F32)

    sems, thru, token = _split_call(body, name, list(grads) + [land], (1, 1), extra=(after,))
    return (sems, thru, nw), token


def pair_wait(started, after, name):
    sems, thru, nw = started

    def body(*refs):
        zone = refs[nw]
        send_sems, recv_sems = refs[nw + 1:nw + 3]
        x, y, c = _mesh_pos()
        cp = pltpu.make_async_remote_copy(src_ref=zone, dst_ref=zone, send_sem=send_sems.at[0],
                                          recv_sem=recv_sems.at[0], device_id=(x, y, 1 - c), device_id_type=MESH)
        cp.wait_send()
        cp.wait_recv()

    outs = _split_call(body, name, thru, (1, 1), extra=(*sems, after), with_token=False)
    return outs[:nw], outs[nw]


def pair_sum(grads, land, name):
    nw = len(grads)
    _, r, c_dim = grads[0].shape

    def body(*refs):
        g_refs, l_ref, o_ref = refs[:nw], refs[nw], refs[nw + 1]
        core = lax.axis_index("c")
        for w in range(nw):
            o_ref[0, w] = (g_refs[w][0, core].astype(F32) + l_ref[0, w].astype(F32)).astype(BF16)

    return pl.pallas_call(
        body, name=name, grid=(4,),
        in_specs=[pl.BlockSpec((1, 2, r, c_dim), lambda j: (j, 0, 0, 0))] * nw
                 + [pl.BlockSpec((1, nw, r, c_dim), lambda j: (j, 0, 0, 0))],
        out_specs=pl.BlockSpec((1, nw, r, c_dim), lambda j: (j, 0, 0, 0)),
        out_shape=jax.ShapeDtypeStruct((4, nw, r, c_dim), BF16),
        compiler_params=_params(),
    )(*[g.reshape(4, 2, r, c_dim) for g in grads], land)


def _other_chips():
    x, y, c = _mesh_pos()
    chips = []
    for rel in range(1, 4):
        px, py = (1 - x if rel & 2 else x), (1 - y if rel & 1 else y)
        chips.append((px, py, 2 * px + py))
    return 2 * x + y, c, chips


def chip_start(pair_sums, after, name):
    land = lax.empty(pair_sums.shape, pair_sums.dtype)

    def body(*refs):
        h_ref, zone = refs[0], refs[1]
        send_sems, recv_sems, local_sem = refs[3:6]
        mine, c, chips = _other_chips()
        pltpu.make_async_copy(h_ref.at[mine], zone.at[mine], local_sem.at[0]).start()
        for k, (px, py, j) in enumerate(chips):
            pltpu.make_async_remote_copy(
                src_ref=h_ref.at[j], dst_ref=zone.at[mine], send_sem=send_sems.at[k], recv_sem=recv_sems.at[k],
                device_id=(px, py, c), device_id_type=MESH).start()
        refs[-1][...] = jnp.zeros(refs[-1].shape, F32)

    sems, thru, token = _split_call(body, name, [pair_sums, land], (3, 3, 1), extra=(after,))
    return (sems, thru), token


def chip_wait(started, after, name):
    sems, thru = started

    def body(*refs):
        zone = refs[1]
        send_sems, recv_sems, local_sem = refs[2:5]
        _, c, chips = _other_chips()
        for k, (px, py, _) in enumerate(chips):
            cp = pltpu.make_async_remote_copy(
                src_ref=zone.at[0], dst_ref=zone.at[0], send_sem=send_sems.at[k], recv_sem=recv_sems.at[k],
                device_id=(px, py, c), device_id_type=MESH)
            cp.wait_send()
            cp.wait_recv()
        pltpu.make_async_copy(zone.at[0], zone.at[0], local_sem.at[0]).wait()

    return _split_call(body, name, thru, (3, 3, 1), extra=(*sems, after), with_token=False)[1]


def share_small(parts, after):
    n = len(parts)

    def body(*refs):
        ins, outs = refs[:n], refs[n + 1:2 * n + 1]
        send_sems, recv_sems, local_sems = refs[2 * n + 1:]
        me, peers = _peers()
        copies = []
        for i in range(n):
            copies.append(pltpu.make_async_copy(ins[i], outs[i].at[me], local_sems.at[i]))
            copies += [pltpu.make_async_remote_copy(
                src_ref=ins[i], dst_ref=outs[i].at[me], send_sem=send_sems.at[7 * i + k],
                recv_sem=recv_sems.at[7 * i + k], device_id=peer, device_id_type=MESH)
                for k, peer in enumerate(peers)]
        for cp in copies:
            cp.start()
        for cp in copies:
            cp.wait()

    vm = pl.BlockSpec(memory_space=pltpu.VMEM)
    return pl.pallas_call(
        body, name="share_small", in_specs=[vm] * n + [pl.BlockSpec(memory_space=pl.ANY)], out_specs=[vm] * n,
        out_shape=[jax.ShapeDtypeStruct((N_DEV,) + p.shape, p.dtype) for p in parts],
        scratch_shapes=[pltpu.SemaphoreType.DMA((7 * n,)), pltpu.SemaphoreType.DMA((7 * n,)),
                        pltpu.SemaphoreType.DMA((n,))],
    )(*parts, after)


def sum_sources(recv, name):
    n_src, w, r, c = recv.shape

    def body(r_ref, o_ref):
        acc = r_ref[0, 0].astype(F32)
        for src in range(1, n_src):
            acc = acc + r_ref[src, 0].astype(F32)
        o_ref[0] = acc

    return pl.pallas_call(
        body, name=name, grid=(w,),
        in_specs=[pl.BlockSpec((n_src, 1, r, c), lambda i: (0, i, 0, 0))],
        out_specs=pl.BlockSpec((1, r, c), lambda i: (i, 0, 0)),
        out_shape=jax.ShapeDtypeStruct((w, r, c), F32),
        compiler_params=_params(),
    )(recv)


def _adamw_math(w, g, m, v):
    m = ADAM_B1 * m + (1.0 - ADAM_B1) * g
    v = ADAM_B2 * v + (1.0 - ADAM_B2) * (g * g)
    m_hat = m / (1.0 - ADAM_B1 ** ADAM_STEP)
    v_hat = v / (1.0 - ADAM_B2 ** ADAM_STEP)
    delta = -ADAM_LR * (m_hat / (jnp.sqrt(v_hat) + ADAM_EPS) + ADAM_WD * w)
    return delta, m, v


def adamw(w, g, m, v, after, name):
    shape = w.shape
    c = shape[-1]
    r = int(np.prod(shape[:-1]))
    w2, g2, m2, v2 = (t.reshape(r, c) for t in (w, g, m, v))
    tr = next(t for t in range(min(r, 512), 0, -1) if r % t == 0 and (t % 8 == 0 or t == r))

    def body(w_ref, g_ref, m_ref, v_ref, after_ref, d_ref, mo_ref, vo_ref):
        d_ref[...], mo_ref[...], vo_ref[...] = _adamw_math(w_ref[...], g_ref[...], m_ref[...], v_ref[...])

    spec = pl.BlockSpec((tr, c), lambda i: (i, 0))
    outs = pl.pallas_call(
        body, name=name, grid=(r // tr,),
        in_specs=[spec] * 4 + [pl.BlockSpec(memory_space=pl.ANY)], out_specs=[spec] * 3,
        out_shape=[jax.ShapeDtypeStruct((r, c), F32)] * 3,
        compiler_params=_params(),
    )(w2, g2, m2, v2, after)
    return tuple(t.reshape(shape) for t in outs)


def adamw_small(ws, recvs, ms, vs, name):
    n = len(ws)

    def body(*refs):
        w_refs, r_refs, m_refs, v_refs = (refs[i * n:(i + 1) * n] for i in range(4))
        g_refs, d_refs, mo_refs, vo_refs = (refs[(4 + i) * n:(5 + i) * n] for i in range(4))
        for i in range(n):
            g = r_refs[i][0]
            for src in range(1, N_DEV):
                g = g + r_refs[i][src]
            g_refs[i][...] = g
            d_refs[i][...], mo_refs[i][...], vo_refs[i][...] = _adamw_math(w_refs[i][...], g, m_refs[i][...],
                                                                            v_refs[i][...])

    vm = pl.BlockSpec(memory_space=pltpu.VMEM)
    outs = pl.pallas_call(
        body, name=name, in_specs=[vm] * (4 * n), out_specs=[vm] * (4 * n),
        out_shape=[jax.ShapeDtypeStruct(w.shape, F32) for w in ws] * 4,
        compiler_params=pltpu.CompilerParams(vmem_limit_bytes=V7X_VMEM_LIMIT),
    )(*ws, *recvs, *ms, *vs)
    return [outs[i * n:(i + 1) * n] for i in range(4)]


SMALL_NAMES = ("ffn1_norm", "mix_norm", "ffn2_norm", "b_gate", "na_q_norm", "na_k_norm", "sw_q_norm", "sw_k_norm",
               "na_rpb", "sw_sink", "t5_rel_table")


def kernel(x, ffn1_norm, ffn1_w_gate, ffn1_w_up, ffn1_w_down, mix_norm, w_in, b_gate, na_q_norm, na_k_norm, na_rpb, sw_q_norm, sw_k_norm, sw_sink, t5_rel_table, w_branch_na, w_branch_sw, w_out, ffn2_norm, ffn2_w_gate, ffn2_w_up, ffn2_w_down, loss_target, m_ffn1_norm, m_ffn1_w_gate, m_ffn1_w_up, m_ffn1_w_down, m_mix_norm, m_w_in, m_b_gate, m_na_q_norm, m_na_k_norm, m_na_rpb, m_sw_q_norm, m_sw_k_norm, m_sw_sink, m_t5_rel_table, m_w_branch_na, m_w_branch_sw, m_w_out, m_ffn2_norm, m_ffn2_w_gate, m_ffn2_w_up, m_ffn2_w_down, v_ffn1_norm, v_ffn1_w_gate, v_ffn1_w_up, v_ffn1_w_down, v_mix_norm, v_w_in, v_b_gate, v_na_q_norm, v_na_k_norm, v_na_rpb, v_sw_q_norm, v_sw_k_norm, v_sw_sink, v_t5_rel_table, v_w_branch_na, v_w_branch_sw, v_w_out, v_ffn2_norm, v_ffn2_w_gate, v_ffn2_w_up, v_ffn2_w_down):
    weights = dict(ffn1_norm=ffn1_norm, ffn1_w_gate=ffn1_w_gate, ffn1_w_up=ffn1_w_up, ffn1_w_down=ffn1_w_down,
                   mix_norm=mix_norm, w_in=w_in, b_gate=b_gate, na_q_norm=na_q_norm, na_k_norm=na_k_norm,
                   na_rpb=na_rpb, sw_q_norm=sw_q_norm, sw_k_norm=sw_k_norm, sw_sink=sw_sink,
                   t5_rel_table=t5_rel_table, w_branch_na=w_branch_na, w_branch_sw=w_branch_sw, w_out=w_out,
                   ffn2_norm=ffn2_norm, ffn2_w_gate=ffn2_w_gate, ffn2_w_up=ffn2_w_up, ffn2_w_down=ffn2_w_down)
    mom_m = dict(ffn1_norm=m_ffn1_norm, ffn1_w_gate=m_ffn1_w_gate, ffn1_w_up=m_ffn1_w_up, ffn1_w_down=m_ffn1_w_down,
                 mix_norm=m_mix_norm, w_in=m_w_in, b_gate=m_b_gate, na_q_norm=m_na_q_norm, na_k_norm=m_na_k_norm,
                 na_rpb=m_na_rpb, sw_q_norm=m_sw_q_norm, sw_k_norm=m_sw_k_norm, sw_sink=m_sw_sink,
                 t5_rel_table=m_t5_rel_table, w_branch_na=m_w_branch_na, w_branch_sw=m_w_branch_sw, w_out=m_w_out,
                 ffn2_norm=m_ffn2_norm, ffn2_w_gate=m_ffn2_w_gate, ffn2_w_up=m_ffn2_w_up, ffn2_w_down=m_ffn2_w_down)
    mom_v = dict(ffn1_norm=v_ffn1_norm, ffn1_w_gate=v_ffn1_w_gate, ffn1_w_up=v_ffn1_w_up, ffn1_w_down=v_ffn1_w_down,
                 mix_norm=v_mix_norm, w_in=v_w_in, b_gate=v_b_gate, na_q_norm=v_na_q_norm, na_k_norm=v_na_k_norm,
                 na_rpb=v_na_rpb, sw_q_norm=v_sw_q_norm, sw_k_norm=v_sw_k_norm, sw_sink=v_sw_sink,
                 t5_rel_table=v_t5_rel_table, w_branch_na=v_w_branch_na, w_branch_sw=v_w_branch_sw, w_out=v_w_out,
                 ffn2_norm=v_ffn2_norm, ffn2_w_gate=v_ffn2_w_gate, ffn2_w_up=v_ffn2_w_up, ffn2_w_down=v_ffn2_w_down)
    order = list(weights)

    depth = ffn1_norm.shape[0]
    s, d = x.shape[1], x.shape[2]
    xs = x[0]
    tr = lambda w: jnp.swapaxes(w, -1, -2)

    merge = lambda t: t.reshape(t.shape[0], N_DEV * t.shape[2], t.shape[3])
    no_dep = jnp.zeros((8, LANES), F32)

    def shards_of(kind, l):
        stack = lambda *ws: jnp.stack(ws).astype(BF16)
        if kind == "ffn1":
            return [stack(tr(ffn1_w_gate[l]), tr(ffn1_w_up[l]), ffn1_w_down[l])]
        if kind == "win":
            return [stack(tr(w_in[l]))]
        return [stack(tr(ffn2_w_gate[l]), tr(ffn2_w_up[l]), ffn2_w_down[l]), stack(w_out[l]),
                stack(tr(w_branch_na[l]), tr(w_branch_sw[l]))]

    def start(kind, l, after):
        return gather_start(shards_of(kind, l), after, f"gather_{kind}_{l}")

    def arrive(started, kind, l, after):
        zones = gather_wait(started, after, f"gather_{kind}_{l}_wait")
        return forward_start(zones, no_dep, f"forward_{kind}_{l}")

    def finish(fwd, kind, l, after):
        return [merge(z) for z in forward_wait(fwd, after, f"forward_{kind}_{l}_wait")]

    bd = jnp.asarray(np.kron(np.eye(NA_WIDTH // HEAD_DIM), np.full((HEAD_DIM, HEAD_DIM), 1.0 / HEAD_DIM)), BF16)
    bmap = jnp.asarray(_t5_bucket_map())
    tile8 = lambda g: jnp.tile(g, NA_WIDTH // HEAD_DIM).reshape(1, NA_WIDTH)
    tile2 = lambda g: jnp.tile(g, SW_KV_WIDTH // HEAD_DIM).reshape(1, SW_KV_WIDTH)

    st_first, tok = start("ffn1", 0, no_dep)
    t5b = t5_expand(t5_rel_table, bmap, tok, "t5_expand").reshape(SW_STACK, 3 * SW_BLOCK)
    fwd, _ = arrive(st_first, "ffn1", 0, t5b)
    st_win, dep = start("win", 0, t5b)
    (first,) = finish(fwd, "ffn1", 0, dep)

    saved = []
    layer_w = {0: dict(wg1=(first, 0), wu1=(first, 1), wd1=(first, 2))}
    cur = xs
    for l in range(depth):
        sv = {}
        lw = layer_w[l]
        sv["x0"] = cur
        sv["xn1"], sv["hg1"], sv["hu1"], sv["act1"] = ffn_up(cur, ffn1_norm[l][None], lw["wg1"], lw["wu1"], dep,
                                                             f"ffn1_up_{l}")
        cur = ffn_down(cur, sv["act1"], lw["wd1"], no_dep, f"ffn1_down_{l}")
        sv["x1"] = cur
        fwd, _ = arrive(st_win, "win", l, cur)
        st_rest, tok = start("rest", l, cur)
        (zb,) = finish(fwd, "win", l, tok)
        lw["win"] = (zb, 0)
        sv["gains"] = (tile8(na_q_norm[l]), tile8(na_k_norm[l]), tile8(sw_q_norm[l]), tile2(sw_k_norm[l]))
        sv["hn"], sv["zq"], sv["qa"], sv["ka"], sv["qs"], sv["ks"], sv["gt"] = mix_in(
            cur, mix_norm[l][None], lw["win"], b_gate[l][None], *sv["gains"], bd, f"mix_in_{l}")
        sv["t2"] = rpb_expand(_rpb_rows(na_rpb[l]), no_dep, f"rpb_expand_{l}")
        sv["o_na"] = na_fwd(sv["qa"], sv["ka"], sv["zq"], sv["t2"], f"na_fwd_{l}")
        dep = no_dep
        if l + 1 < depth:
            st_ffn1, dep = start("ffn1", l + 1, sv["o_na"])
        sv["o_sw"] = sw_fwd(sv["qs"], sv["ks"], sv["zq"], t5b, sw_sink[l], dep, f"sw_fwd_{l}")
        fwd, tok = arrive(st_rest, "rest", l, sv["o_sw"])
        za, zc, zd = finish(fwd, "rest", l, tok)
        lw.update(wg2=(za, 0), wu2=(za, 1), wd2=(za, 2), wout=(zc, 0), wna=(zd, 0), wsw=(zd, 1))
        cur, sv["a_na"], sv["a_sw"], sv["merged"] = merge_out(
            cur, sv["o_na"], sv["o_sw"], sv["gt"], lw["wna"], lw["wsw"], lw["wout"], f"merge_out_{l}")
        sv["x2"] = cur
        dep = no_dep
        if l + 1 < depth:
            st_win, dep = start("win", l + 1, cur)
        sv["xn2"], sv["hg2"], sv["hu2"], sv["act2"] = ffn_up(cur, ffn2_norm[l][None], lw["wg2"], lw["wu2"], dep,
                                                             f"ffn2_up_{l}")
        dep = no_dep
        if l + 1 < depth:
            fwd, dep = arrive(st_ffn1, "ffn1", l + 1, sv["act2"])
        cur = ffn_down(cur, sv["act2"], lw["wd2"], dep, f"ffn2_down_{l}")
        dep = no_dep
        if l + 1 < depth:
            (za,) = finish(fwd, "ffn1", l + 1, cur)
            layer_w[l + 1] = dict(wg1=(za, 0), wu1=(za, 1), wd1=(za, 2))
        saved.append(sv)

    dx, loss_acc = loss_grad(cur, loss_target[0], "loss_grad")
    loss = lax.psum(jnp.sum(loss_acc) * (0.5 / d), ("x", "y", "c"))

    split = lambda t: t.reshape(N_DEV, t.shape[0] // N_DEV, t.shape[1])
    pending = {}
    last_key = "ffn1_0"
    two_level = {last_key}
    small = {k: [None] * depth for k in SMALL_NAMES if k != "t5_rel_table"}
    dbias_sw = []
    for l in reversed(range(depth)):
        sv = saved[l]
        lw = layer_w[l]
        wg1, wu1, wd1, wg2, wu2, wd2 = (lw[k] for k in ("wg1", "wu1", "wd1", "wg2", "wu2", "wd2"))
        win_t, wout_l, wna_t, wsw_t = lw["win"], lw["wout"], lw["wna"], lw["wsw"]
        blocks = ((2, "x2", "xn2", "hg2", "hu2", "act2", wg2, wu2, wd2, "ffn2_norm", 3),
                  (1, "x0", "xn1", "hg1", "hu1", "act1", wg1, wu1, wd1, "ffn1_norm", 0))

        def ffn_backward(dx, blk):
            tag, xk, xnk, hgk, huk, actk, wg, wu, wd, norm_name, slot = blk
            gains = weights[norm_name]
            dxb, dhg, dhu = ffn_bwd_act(dx, wd, sv[hgk], sv[huk], f"ffn{tag}_bwd_act_{l}")
            gwd = tn_matmul(sv[actk], dxb, 0.5, f"ffn{tag}_dwd_{l}")
            gwg = tn_matmul(dhg, sv[xnk], 1.0, f"ffn{tag}_dwg_{l}")
            gwu = tn_matmul(dhu, sv[xnk], 1.0, f"ffn{tag}_dwu_{l}")
            key = f"ffn{tag}_{l}"
            blocks_of = [split(gwg), split(gwu), split(gwd)]
            if key in two_level:
                paired, token = pair_start(blocks_of, dxb, f"pair_{key}")
            else:
                pending[key], token = scatter_start([blocks_of], f"scatter_{key}")
            dx, dg = proj_bwd_norm([dhg, dhu], [wg, wu], sv[xk], gains[l][None], dx, token, f"ffn{tag}_bwd_x_{l}")
            token = no_dep
            if key in two_level:
                thru, land = pair_wait(paired, dx, f"pair_{key}_wait")
                pending[key], token = chip_start(pair_sum(thru, land, f"pair_sum_{key}"), dg, f"chips_{key}")
            small[norm_name][l] = dg[0]
            return dx, token

        dx, token = ffn_backward(dx, blocks[0])
        dxb, dzg, da_na, da_sw, do_na, do_sw, dbg = mix_bwd_out(
            dx, sv["gt"], sv["a_na"], sv["a_sw"], wna_t, wsw_t, wout_l, token, f"mix_bwd_out_{l}")
        small["b_gate"][l] = dbg[0]
        gwout = tn_matmul(sv["merged"], dxb, 1.0, f"dwout_{l}")
        gwna = tn_matmul(da_na, sv["o_na"], 1.0, f"dwna_{l}")
        gwsw = tn_matmul(da_sw, sv["o_sw"], 1.0, f"dwsw_{l}")
        dqa, dka, dva, dt2 = na_bwd(sv["qa"], sv["ka"], sv["zq"], sv["t2"], sv["o_na"], do_na, f"na_bwd_{l}")
        dqs, dks, dvs, dbias, dsink = sw_bwd(sv["qs"], sv["ks"], sv["zq"], t5b, sw_sink[l], sv["o_sw"], do_sw,
                                             f"sw_bwd_{l}")
        dbias_sw.append(dbias.reshape(SW_HEADS, SW_BLOCK, 3 * SW_BLOCK))
        small["sw_sink"][l] = jnp.sum(dsink[:, 0].reshape(SW_HEADS, SW_BLOCK), axis=1)
        small["na_rpb"][l] = _rpb_from_rows(rpb_reduce(dt2, f"rpb_reduce_{l}"))
        dz, dgqa, dgka, dgqs, dgks = qk_norm_bwd(dqa, dka, dva, dqs, dks, dvs, sv["zq"], dzg, *sv["gains"], bd,
                                                 f"qk_norm_bwd_{l}")
        fold = lambda g: jnp.sum(g.reshape(-1, HEAD_DIM), axis=0)
        small["na_q_norm"][l], small["na_k_norm"][l] = fold(dgqa), fold(dgka)
        small["sw_q_norm"][l], small["sw_k_norm"][l] = fold(dgqs), fold(dgks)
        gwin = tn_matmul(dz, sv["hn"], 1.0, f"dwin_{l}")
        pending[f"mix_{l}"], token = scatter_start([[split(gwout)], [split(gwna), split(gwsw)], [split(gwin)]],
                                                   f"scatter_mix_{l}")
        dx, dg = proj_bwd_norm([dz], [win_t], sv["x1"], mix_norm[l][None], dx, token, f"mix_bwd_x_{l}")
        small["mix_norm"][l] = dg[0]
        dx, tail = ffn_backward(dx, blocks[1])

    dtab = t5_reduce(dbias_sw, bmap, "t5_reduce")
    small_parts = {k: jnp.stack(v) for k, v in small.items()}
    small_parts["t5_rel_table"] = jnp.transpose(dtab[:, :, 0])

    summed = {}
    layers = lambda f: jnp.stack([f(l) for l in range(depth)])
    grads, delta, new_m, new_v = {}, {}, {}, {}

    def collect(key, after):
        if key in two_level:
            zones = [chip_wait(pending[key], after, f"wait_{key}")]
        else:
            zones = scatter_wait(pending[key], after, f"wait_{key}")
        summed[key] = [sum_sources(z, f"sum_{key}_{i}") for i, z in enumerate(zones)]

    chain = [tail]

    def update(k, g, transposed):
        view = tr if transposed else (lambda t: t)
        d_k, m_k, v_k = adamw(view(weights[k]), g, view(mom_m[k]), view(mom_v[k]), chain[0], f"adamw_{k}")
        grads[k], delta[k], new_m[k], new_v[k] = view(g), view(d_k), view(m_k), view(v_k)
        chain[0] = d_k

    for key in pending:
        if key != last_key:
            collect(key, tail)
    update("ffn2_w_gate", layers(lambda l: summed[f"ffn2_{l}"][0][0]), True)
    update("ffn2_w_up", layers(lambda l: summed[f"ffn2_{l}"][0][1]), True)
    update("ffn2_w_down", layers(lambda l: summed[f"ffn2_{l}"][0][2]), False)
    update("w_out", layers(lambda l: summed[f"mix_{l}"][0][0]), False)
    update("w_branch_na", layers(lambda l: summed[f"mix_{l}"][1][0]), True)
    update("w_branch_sw", layers(lambda l: summed[f"mix_{l}"][1][1]), True)
    update("w_in", layers(lambda l: summed[f"mix_{l}"][2][0]), True)
    collect(last_key, chain[0])
    update("ffn1_w_gate", layers(lambda l: summed[f"ffn1_{l}"][0][0]), True)
    update("ffn1_w_up", layers(lambda l: summed[f"ffn1_{l}"][0][1]), True)
    update("ffn1_w_down", layers(lambda l: summed[f"ffn1_{l}"][0][2]), False)
    recvs = share_small([small_parts[k] for k in SMALL_NAMES], chain[0])
    results = adamw_small([weights[k] for k in SMALL_NAMES], recvs, [mom_m[k] for k in SMALL_NAMES],
                          [mom_v[k] for k in SMALL_NAMES], "adamw_small")
    for dst, outs in zip((grads, delta, new_m, new_v), results):
        dst.update(dict(zip(SMALL_NAMES, outs)))

    return (loss, dx[None], *[grads[k] for k in order], *[delta[k] for k in order],
            *[new_m[k] for k in order], *[new_v[k] for k in order])
```

```python
import functools
import math

import numpy as np
import jax
import jax.numpy as jnp
from jax import lax
from jax.experimental import pallas as pl
from jax.experimental.pallas import tpu as pltpu

F32 = jnp.float32
BF16 = jnp.bfloat16
MESH = pl.DeviceIdType.MESH

N_DEV = 8
EPS = 1e-6
NEG = -1e30
HEAD_DIM = 64
GRID_W = 64
NA_ROWS = 8
NA_COLS = 16
NA_WIDTH = 512
SW_Q_WIDTH = 512
SW_KV_WIDTH = 128
SW_BLOCK = 128
SW_HEADS = 8
SW_REP = 4
REL_BUCKETS = 32
REL_MAX_DIST = 128
QKV_WIDTH = 3 * NA_WIDTH + SW_Q_WIDTH + 2 * SW_KV_WIDTH
SCALE = 1.0 / math.sqrt(HEAD_DIM)

ADAM_LR = 0.001
ADAM_B1 = 0.9
ADAM_B2 = 0.999
ADAM_EPS = 1e-08
ADAM_WD = 0.01
ADAM_STEP = 10

V7X_VMEM_LIMIT = 56 * 1024 * 1024
LANES = 128
MXU_TILE = 256

NT = (((1,), (1,)), ((), ()))
TN = (((0,), (0,)), ((), ()))


def _params(n_grid=1):
    return pltpu.CompilerParams(dimension_semantics=("arbitrary",) * n_grid,
                                vmem_limit_bytes=V7X_VMEM_LIMIT)


def _row_tile(s):
    for t in (512, 256, 128, 64, 32, 16, 8):
        if s % t == 0:
            return t
    raise ValueError(s)


def _tn_tile(n):
    best = max(t for t in range(LANES, min(n, 2304) + 1, LANES) if n % t == 0) if n % LANES == 0 else n
    return best // 2 if best == n and n >= 1024 else best


ONCE = pl.Buffered(1)


def _col_chunk(n):
    return MXU_TILE if n % MXU_TILE == 0 else n


def _dot(a, b):
    return jnp.dot(a, b, preferred_element_type=F32)


def _dotg(a, b, dn):
    return lax.dot_general(a, b, dn, preferred_element_type=F32)


def _sigmoid(v):
    return 1.0 / (1.0 + jnp.exp(-v))


def _rstd(xv):
    return lax.rsqrt(jnp.mean(xv * xv, axis=-1, keepdims=True) + EPS)


def _full(shape):
    nd = len(shape)
    return pl.BlockSpec(shape, lambda i, _n=nd: (0,) * _n)


def _rows(tm, width):
    return pl.BlockSpec((tm, width), lambda i: (i, 0))


def _mat(stack, idx):
    return pl.BlockSpec((None,) + tuple(stack.shape[1:]), lambda i, _w=idx: (_w, 0, 0), pipeline_mode=ONCE)


def _group_mean(v, bd):
    hi = v.astype(BF16)
    lo = (v - hi.astype(F32)).astype(BF16)
    return _dot(hi, bd) + _dot(lo, bd)


def ffn_up(x, gain, wg_t, wu_t, dep, name):
    s, d = x.shape
    f = wg_t[0].shape[1]
    tm = _row_tile(s)
    fc = _col_chunk(f)

    def body(x_ref, g_ref, wg_ref, wu_ref, dep_ref, xn_ref, hg_ref, hu_ref, act_ref):
        xv = x_ref[...]
        xn = (xv * _rstd(xv) * g_ref[...]).astype(BF16)
        xn_ref[...] = xn
        for c0 in range(0, f, fc):
            hg = _dotg(xn, wg_ref[c0:c0 + fc, :], NT)
            hu = _dotg(xn, wu_ref[c0:c0 + fc, :], NT)
            hg_ref[:, c0:c0 + fc] = hg.astype(BF16)
            hu_ref[:, c0:c0 + fc] = hu.astype(BF16)
            act_ref[:, c0:c0 + fc] = (hg * _sigmoid(hg) * hu).astype(BF16)

    return pl.pallas_call(
        body, name=name, grid=(s // tm,),
        in_specs=[_rows(tm, d), _full((1, d)), _mat(*wg_t), _mat(*wu_t), _full(dep.shape)],
        out_specs=[_rows(tm, d), _rows(tm, f), _rows(tm, f), _rows(tm, f)],
        out_shape=[jax.ShapeDtypeStruct((s, d), BF16)] + [jax.ShapeDtypeStruct((s, f), BF16)] * 3,
        compiler_params=_params(),
    )(x, gain, wg_t[0], wu_t[0], dep)


def ffn_down(x, act, wd, dep, name):
    s, d = x.shape
    f = act.shape[1]
    tm = _row_tile(s)

    def body(x_ref, a_ref, w_ref, dep_ref, o_ref):
        o_ref[...] = x_ref[...] + 0.5 * _dot(a_ref[...], w_ref[...])

    return pl.pallas_call(
        body, name=name, grid=(s // tm,),
        in_specs=[_rows(tm, d), _rows(tm, f), _mat(*wd), _full(dep.shape)],
        out_specs=_rows(tm, d),
        out_shape=jax.ShapeDtypeStruct((s, d), F32),
        compiler_params=_params(),
    )(x, act, wd[0], dep)


def mix_in(x, gain, win_t, b_gate, gq_na, gk_na, gq_sw, gk_sw, bd, name):
    s, d = x.shape
    tm = _row_tile(s)
    gc = _col_chunk(2 * d)

    def body(x_ref, g_ref, w_ref, b_ref, gqa_ref, gka_ref, gqs_ref, gks_ref, bd_ref,
             hn_ref, zq_ref, qa_ref, ka_ref, qs_ref, ks_ref, gt_ref):
        xv = x_ref[...]
        hn = (xv * _rstd(xv) * g_ref[...]).astype(BF16)
        hn_ref[...] = hn

        def proj(c0, c1):
            return _dotg(hn, w_ref[c0:c1, :], NT)

        def headnorm(z, g, bdm):
            return z * lax.rsqrt(_group_mean(z * z, bdm) + EPS) * g

        bd512 = bd_ref[...]
        bd128 = bd_ref[0:SW_KV_WIDTH, 0:SW_KV_WIDTH]
        z = proj(0, 512)
        zq_ref[:, 0:512] = z.astype(BF16)
        qa_ref[...] = (headnorm(z, gqa_ref[...], bd512) * SCALE).astype(BF16)
        z = proj(512, 1024)
        zq_ref[:, 512:1024] = z.astype(BF16)
        ka_ref[...] = headnorm(z, gka_ref[...], bd512).astype(BF16)
        z = proj(1024, 1536)
        zq_ref[:, 1024:1536] = z.astype(BF16)
        z = proj(1536, 2048)
        zq_ref[:, 1536:2048] = z.astype(BF16)
        qs_ref[...] = (headnorm(z, gqs_ref[...], bd512) * SCALE).astype(BF16)
        z = proj(2048, 2176)
        zq_ref[:, 2048:2176] = z.astype(BF16)
        ks_ref[...] = headnorm(z, gks_ref[...], bd128).astype(BF16)
        z = proj(2176, 2304)
        zq_ref[:, 2176:2304] = z.astype(BF16)
        for c0 in range(0, 2 * d, gc):
            zg = proj(QKV_WIDTH + c0, QKV_WIDTH + c0 + gc) + b_ref[:, c0:c0 + gc]
            gt_ref[:, c0:c0 + gc] = _sigmoid(zg).astype(BF16)

    return pl.pallas_call(
        body, name=name, grid=(s // tm,),
        in_specs=[_rows(tm, d), _full((1, d)), _mat(*win_t), _full((1, 2 * d)),
                  _full((1, 512)), _full((1, 512)), _full((1, 512)), _full((1, 128)), _full((512, 512))],
        out_specs=[_rows(tm, d), _rows(tm, QKV_WIDTH), _rows(tm, 512), _rows(tm, 512), _rows(tm, 512),
                   _rows(tm, 128), _rows(tm, 2 * d)],
        out_shape=[jax.ShapeDtypeStruct((s, d), BF16), jax.ShapeDtypeStruct((s, QKV_WIDTH), BF16),
                   jax.ShapeDtypeStruct((s, 512), BF16), jax.ShapeDtypeStruct((s, 512), BF16),
                   jax.ShapeDtypeStruct((s, 512), BF16), jax.ShapeDtypeStruct((s, 128), BF16),
                   jax.ShapeDtypeStruct((s, 2 * d), BF16)],
        compiler_params=_params(),
    )(x, gain, win_t[0], b_gate, gq_na, gk_na, gq_sw, gk_sw, bd)


def _na_iotas():
    qc = lax.broadcasted_iota(jnp.int32, (GRID_W, LANES), 0)
    ln = lax.broadcasted_iota(jnp.int32, (GRID_W, LANES), 1)
    low = ln < GRID_W
    kc = jnp.where(low, ln, ln - GRID_W)
    diff = kc - qc + (NA_COLS - 1)
    qcs = jnp.clip(qc - NA_COLS // 2, 0, GRID_W - NA_COLS)
    inwin = (kc >= qcs) & (kc < qcs + NA_COLS)
    return diff, low, inwin


NA_RI = 2 * NA_ROWS - 1
NA_CI = 2 * NA_COLS - 1
NA_T2 = NA_RI + 1


def _rpb_rows(rpb):
    h = rpb.shape[0]
    padded = jnp.pad(rpb, ((0, 0), (1, 1), (0, GRID_W - NA_CI)))
    return jnp.concatenate([padded[:, :NA_T2], padded[:, 1:NA_T2 + 1]], axis=2).reshape(h, NA_T2, LANES)


def _rpb_from_rows(rows):
    return rows[:, 1:, :NA_CI] + rows[:, :NA_RI, GRID_W:GRID_W + NA_CI]


def rpb_expand(rows, dep, name):
    n_heads = rows.shape[0]

    def body(r_ref, dep_ref, o_ref):
        for h in range(n_heads):
            for e in range(NA_T2):
                line = jnp.broadcast_to(r_ref[h, e:e + 1, :], (GRID_W, LANES))
                o_ref[h, e] = pltpu.roll(line, LANES - (NA_COLS - 1), 1, stride=1, stride_axis=0)

    return pl.pallas_call(
        body, name=name,
        in_specs=[pl.BlockSpec(memory_space=pltpu.VMEM), pl.BlockSpec(memory_space=pltpu.VMEM)],
        out_specs=pl.BlockSpec(memory_space=pltpu.VMEM),
        out_shape=jax.ShapeDtypeStruct((n_heads, NA_T2, GRID_W, LANES), F32),
        compiler_params=pltpu.CompilerParams(vmem_limit_bytes=V7X_VMEM_LIMIT),
    )(rows, dep)


def rpb_reduce(dt2, name):
    n_heads = dt2.shape[0]
    flip = jnp.asarray(np.eye(GRID_W)[::-1], BF16)

    def body(d_ref, j_ref, o_ref):
        jm = j_ref[...]
        for h in range(n_heads):
            for e in range(NA_T2):
                dv = d_ref[h, e]
                hi = dv.astype(BF16)
                mid = (dv - hi.astype(F32)).astype(BF16)
                lo = (dv - hi.astype(F32) - mid.astype(F32)).astype(BF16)
                rev = _dot(jm, hi) + _dot(jm, mid) + _dot(jm, lo)
                back = pltpu.roll(rev, LANES + (NA_COLS - 1) - (GRID_W - 1), 1, stride=1, stride_axis=0)
                o_ref[h, e:e + 1, :] = jnp.sum(back, axis=0, keepdims=True)

    return pl.pallas_call(
        body, name=name,
        in_specs=[pl.BlockSpec(memory_space=pltpu.VMEM)] * 2,
        out_specs=pl.BlockSpec(memory_space=pltpu.VMEM),
        out_shape=jax.ShapeDtypeStruct((n_heads, NA_T2, LANES), F32),
        compiler_params=pltpu.CompilerParams(vmem_limit_bytes=V7X_VMEM_LIMIT),
    )(dt2, flip)


NA_TQ = 4
NA_TK = NA_TQ + NA_ROWS
NA_KCH = NA_TK // 2


def _na_tile_geometry(t, rows):
    r = t * NA_TQ
    kbase = jnp.clip(r - NA_ROWS // 2, 0, rows - NA_TK)
    starts = [jnp.clip(r + a - NA_ROWS // 2, 0, rows - NA_ROWS) for a in range(NA_TQ)]
    return r, kbase, starts


def _na_tile_mask(kbase, starts, low, inwin):
    half = jnp.where(low, 0, 1)
    cols = []
    for c in range(NA_KCH):
        krow = kbase + 2 * c + half
        cols.append(jnp.concatenate(
            [jnp.where(inwin & (krow >= st) & (krow < st + NA_ROWS), 0.0, NEG) for st in starts], axis=0))
    return jnp.concatenate(cols, axis=1)


def _na_tile_index(r, kbase, a, c):
    return jnp.clip(kbase + 2 * c - (r + a) + NA_ROWS, 0, NA_T2 - 1)


def _na_tile_scores(q, k, t2_ref, hh, r, kbase, madd):
    bias = jnp.concatenate(
        [jnp.concatenate([t2_ref[hh, _na_tile_index(r, kbase, a, c)] for a in range(NA_TQ)], axis=0)
         for c in range(NA_KCH)], axis=1)
    return _dotg(q, k, NT) + bias + madd


def _softmax_rows(sc):
    e = jnp.exp(sc - jnp.max(sc, axis=1, keepdims=True))
    return e * (1.0 / jnp.sum(e, axis=1, keepdims=True))


def na_fwd(qa, ka, zq, t2, name):
    s = qa.shape[0]
    rows = s // GRID_W
    n_pairs = NA_WIDTH // LANES
    v_blk0 = (2 * NA_WIDTH) // LANES

    assert rows % NA_TQ == 0 and rows >= NA_TK
    tq, tk = NA_TQ * GRID_W, NA_TK * GRID_W

    def body(q_ref, k_ref, v_ref, t2_ref, o_ref, s_scr, p_scr):
        _, low, inwin = _na_iotas()

        def tile(t, carry):
            r, kbase, starts = _na_tile_geometry(t, rows)
            madd = _na_tile_mask(kbase, starts, low, inwin)
            qr = pl.ds(pl.multiple_of(r * GRID_W, tq), tq)
            kr = pl.ds(pl.multiple_of(kbase * GRID_W, tq), tk)
            for hh in range(2):
                lanes = slice(HEAD_DIM * hh, HEAD_DIM * (hh + 1))
                s_scr[tq * hh:tq * (hh + 1), :] = _na_tile_scores(q_ref[qr, lanes], k_ref[kr, lanes], t2_ref, hh, r,
                                                                  kbase, madd)
            p_scr[...] = _softmax_rows(s_scr[...]).astype(BF16)
            for hh in range(2):
                lanes = slice(HEAD_DIM * hh, HEAD_DIM * (hh + 1))
                o_ref[qr, lanes] = _dot(p_scr[tq * hh:tq * (hh + 1), :], v_ref[kr, lanes]).astype(BF16)
            return carry

        lax.fori_loop(0, rows // NA_TQ, tile, 0)

    col = lambda off: pl.BlockSpec((s, LANES), lambda p, _o=off: (0, _o + p))
    return pl.pallas_call(
        body, name=name, grid=(n_pairs,),
        in_specs=[col(0), col(0), col(v_blk0),
                  pl.BlockSpec((2, NA_T2, GRID_W, LANES), lambda p: (p, 0, 0, 0))],
        out_specs=col(0),
        out_shape=jax.ShapeDtypeStruct((s, NA_WIDTH), BF16),
        scratch_shapes=[pltpu.VMEM((2 * tq, tk), F32), pltpu.VMEM((2 * tq, tk), BF16)],
        compiler_params=_params(),
    )(qa, ka, zq, t2)


def na_bwd(qa, ka, zq, t2, o_na, do_na, name):
    s = qa.shape[0]
    rows = s // GRID_W
    n_pairs = NA_WIDTH // LANES
    v_blk0 = (2 * NA_WIDTH) // LANES

    tq, tk = NA_TQ * GRID_W, NA_TK * GRID_W

    def body(q_ref, k_ref, v_ref, t2_ref, o_ref, do_ref, dq_ref, dk_ref, dv_ref, dt2_ref):
        _, low, inwin = _na_iotas()
        dk_ref[...] = jnp.zeros(dk_ref.shape, F32)
        dv_ref[...] = jnp.zeros(dv_ref.shape, F32)
        dt2_ref[...] = jnp.zeros(dt2_ref.shape, F32)

        def tile(t, carry):
            r, kbase, starts = _na_tile_geometry(t, rows)
            madd = _na_tile_mask(kbase, starts, low, inwin)
            qr = pl.ds(pl.multiple_of(r * GRID_W, tq), tq)
            kr = pl.ds(pl.multiple_of(kbase * GRID_W, tq), tk)
            for hh in range(2):
                lanes = slice(HEAD_DIM * hh, HEAD_DIM * (hh + 1))
                q, k, v = q_ref[qr, lanes], k_ref[kr, lanes], v_ref[kr, lanes]
                p = _softmax_rows(_na_tile_scores(q, k, t2_ref, hh, r, kbase, madd))
                do = do_ref[qr, lanes]
                delta = jnp.sum(do.astype(F32) * o_ref[qr, lanes].astype(F32), axis=1, keepdims=True)
                ds = p * (_dotg(do, v, NT) - delta)
                for a in range(NA_TQ):
                    for c in range(NA_KCH):
                        e = _na_tile_index(r, kbase, a, c)
                        dt2_ref[hh, e] = dt2_ref[hh, e] + ds[GRID_W * a:GRID_W * (a + 1), LANES * c:LANES * (c + 1)]
                dsb = ds.astype(BF16)
                dq_ref[qr, lanes] = _dot(dsb, k)
                dk_ref[kr, lanes] = dk_ref[kr, lanes] + _dotg(dsb, q, TN)
                dv_ref[kr, lanes] = dv_ref[kr, lanes] + _dotg(p.astype(BF16), do, TN)
            return carry

        lax.fori_loop(0, rows // NA_TQ, tile, 0)

    col = lambda off: pl.BlockSpec((s, LANES), lambda p, _o=off: (0, _o + p))
    t2spec = pl.BlockSpec((2, NA_T2, GRID_W, LANES), lambda p: (p, 0, 0, 0))
    return pl.pallas_call(
        body, name=name, grid=(n_pairs,),
        in_specs=[col(0), col(0), col(v_blk0), t2spec, col(0), col(0)],
        out_specs=[col(0), col(0), col(0), t2spec],
        out_shape=[jax.ShapeDtypeStruct((s, NA_WIDTH), F32)] * 3 + [jax.ShapeDtypeStruct(t2.shape, F32)],
        compiler_params=_params(),
    )(qa, ka, zq, t2, o_na, do_na)


def _t5_bucket_map():
    rel = np.arange(3 * SW_BLOCK)[None, :] - SW_BLOCK - np.arange(SW_BLOCK)[:, None]
    nb = REL_BUCKETS // 2
    max_exact = nb // 2
    n = np.abs(rel)
    large = max_exact + (np.log(np.maximum(n, 1) / max_exact)
                         / np.log(REL_MAX_DIST / max_exact) * (nb - max_exact)).astype(np.int32)
    large = np.minimum(large, nb - 1)
    return ((rel > 0) * nb + np.where(n < max_exact, n, large)).astype(np.int32)


def t5_expand(table, bmap, dep, name):
    def body(tab_ref, bm_ref, dep_ref, o_ref):
        bm = bm_ref[...]
        for h in range(SW_HEADS):
            t = jnp.zeros(bm.shape, F32)
            for b in range(REL_BUCKETS):
                t = jnp.where(bm == b, tab_ref[b, h], t)
            o_ref[h] = t

    return pl.pallas_call(
        body, name=name,
        in_specs=[pl.BlockSpec(memory_space=pltpu.SMEM), pl.BlockSpec(memory_space=pltpu.VMEM),
                  pl.BlockSpec(memory_space=pltpu.VMEM)],
        out_specs=pl.BlockSpec(memory_space=pltpu.VMEM),
        out_shape=jax.ShapeDtypeStruct((SW_HEADS,) + bmap.shape, F32),
        compiler_params=pltpu.CompilerParams(vmem_limit_bytes=V7X_VMEM_LIMIT),
    )(table, bmap, dep)


def t5_reduce(dbias_list, bmap, name):
    n = len(dbias_list)

    def body(*refs):
        d_refs, bm_ref, o_ref = refs[:n], refs[n], refs[n + 1]
        bm = bm_ref[...]
        for h in range(SW_HEADS):
            dv = d_refs[0][h]
            for other in d_refs[1:]:
                dv = dv + other[h]
            rows = [jnp.sum(jnp.where(bm == b, dv, 0.0), axis=0, keepdims=True) for b in range(REL_BUCKETS)]
            r = jnp.concatenate(rows, axis=0)
            o_ref[h] = jnp.broadcast_to(jnp.sum(r, axis=1, keepdims=True), (REL_BUCKETS, LANES))

    return pl.pallas_call(
        body, name=name,
        in_specs=[pl.BlockSpec(memory_space=pltpu.VMEM)] * (n + 1),
        out_specs=pl.BlockSpec(memory_space=pltpu.VMEM),
        out_shape=jax.ShapeDtypeStruct((SW_HEADS, REL_BUCKETS, LANES), F32),
        compiler_params=pltpu.CompilerParams(vmem_limit_bytes=V7X_VMEM_LIMIT),
    )(*dbias_list, bmap)


def _sw_mask_iotas():
    a = lax.broadcasted_iota(jnp.int32, (SW_BLOCK, 3 * SW_BLOCK), 0)
    j = lax.broadcasted_iota(jnp.int32, (SW_BLOCK, 3 * SW_BLOCK), 1)
    inwin = jnp.abs(j - SW_BLOCK - a) <= SW_BLOCK
    return j, inwin


SW_STACK = SW_HEADS * SW_BLOCK


def _sw_softmax(sc, sk):
    m = jnp.maximum(jnp.max(sc, axis=1, keepdims=True), sk)
    e = jnp.exp(sc - m)
    es = jnp.exp(sk - m)
    inv = 1.0 / (jnp.sum(e, axis=1, keepdims=True) + es)
    return e * inv, es * inv


def _sw_prologue(k_ref, v_ref, kp, vp, sink_ref, s):
    pad = s + 2 * SW_BLOCK
    zeros = jnp.zeros((SW_BLOCK, SW_KV_WIDTH), BF16)
    kp[0:SW_BLOCK, :] = zeros
    vp[0:SW_BLOCK, :] = zeros
    kp[SW_BLOCK + s:pad, :] = zeros
    vp[SW_BLOCK + s:pad, :] = zeros
    kp[SW_BLOCK:SW_BLOCK + s, :] = k_ref[...]
    vp[SW_BLOCK:SW_BLOCK + s, :] = v_ref[...]
    return jnp.concatenate([jnp.full((SW_BLOCK, 1), sink_ref[h], F32) for h in range(SW_HEADS)], axis=0)


def sw_fwd(qs, ks, zq, t5b, sink, dep, name):
    s = qs.shape[0]
    nb = s // SW_BLOCK
    v_blk = (3 * NA_WIDTH + SW_Q_WIDTH + SW_KV_WIDTH) // LANES
    pad = s + 2 * SW_BLOCK

    def body(q_ref, k_ref, v_ref, b_ref, sink_ref, dep_ref, o_ref, kp, vp, s_scr, p_scr):
        sink_col = _sw_prologue(k_ref, v_ref, kp, vp, sink_ref, s)
        j, inwin = _sw_mask_iotas()

        def blk(n, carry):
            kpos = n * SW_BLOCK - SW_BLOCK + j
            madd = jnp.where(inwin & (kpos >= 0) & (kpos < s), 0.0, NEG)
            q0 = pl.multiple_of(n * SW_BLOCK, SW_BLOCK)
            qr, kr = pl.ds(q0, SW_BLOCK), pl.ds(q0, 3 * SW_BLOCK)
            for h in range(SW_HEADS):
                g = h // SW_REP
                s_scr[SW_BLOCK * h:SW_BLOCK * (h + 1), :] = _dotg(
                    q_ref[qr, HEAD_DIM * h:HEAD_DIM * (h + 1)], kp[kr, HEAD_DIM * g:HEAD_DIM * (g + 1)], NT) + madd
            p, _ = _sw_softmax(s_scr[...] + b_ref[...], sink_col)
            p_scr[...] = p.astype(BF16)
            for h in range(SW_HEADS):
                g = h // SW_REP
                o_ref[qr, HEAD_DIM * h:HEAD_DIM * (h + 1)] = _dot(
                    p_scr[SW_BLOCK * h:SW_BLOCK * (h + 1), :], vp[kr, HEAD_DIM * g:HEAD_DIM * (g + 1)]).astype(BF16)
            return carry

        lax.fori_loop(0, nb, blk, 0)

    return pl.pallas_call(
        body, name=name, grid=(1,),
        in_specs=[_full((s, SW_Q_WIDTH)), _full((s, SW_KV_WIDTH)),
                  pl.BlockSpec((s, SW_KV_WIDTH), lambda i: (0, v_blk)),
                  _full((SW_STACK, 3 * SW_BLOCK)), pl.BlockSpec(memory_space=pltpu.SMEM),
                  _full(dep.shape)],
        out_specs=_full((s, SW_Q_WIDTH)),
        out_shape=jax.ShapeDtypeStruct((s, SW_Q_WIDTH), BF16),
        scratch_shapes=[pltpu.VMEM((pad, SW_KV_WIDTH), BF16), pltpu.VMEM((pad, SW_KV_WIDTH), BF16),
                        pltpu.VMEM((SW_STACK, 3 * SW_BLOCK), F32), pltpu.VMEM((SW_STACK, 3 * SW_BLOCK), BF16)],
        compiler_params=_params(),
    )(qs, ks, zq, t5b, sink, dep)


def sw_bwd(qs, ks, zq, t5b, sink, o_sw, do_sw, name):
    s = qs.shape[0]
    nb = s // SW_BLOCK
    v_blk = (3 * NA_WIDTH + SW_Q_WIDTH + SW_KV_WIDTH) // LANES
    pad = s + 2 * SW_BLOCK

    def body(q_ref, k_ref, v_ref, b_ref, sink_ref, o_ref, do_ref,
             dq_ref, dk_ref, dv_ref, db_ref, dsk_ref, kp, vp, dkp, dvp, s_scr, dp_scr, ds_scr, p_scr):
        sink_col = _sw_prologue(k_ref, v_ref, kp, vp, sink_ref, s)
        dkp[...] = jnp.zeros(dkp.shape, F32)
        dvp[...] = jnp.zeros(dvp.shape, F32)
        db_ref[...] = jnp.zeros(db_ref.shape, F32)
        dsk_ref[...] = jnp.zeros(dsk_ref.shape, F32)
        j, inwin = _sw_mask_iotas()

        def blk(n, carry):
            kpos = n * SW_BLOCK - SW_BLOCK + j
            madd = jnp.where(inwin & (kpos >= 0) & (kpos < s), 0.0, NEG)
            q0 = pl.multiple_of(n * SW_BLOCK, SW_BLOCK)
            qr, kr = pl.ds(q0, SW_BLOCK), pl.ds(q0, 3 * SW_BLOCK)
            deltas = []
            for h in range(SW_HEADS):
                g = h // SW_REP
                hl, kl = slice(HEAD_DIM * h, HEAD_DIM * (h + 1)), slice(HEAD_DIM * g, HEAD_DIM * (g + 1))
                rows = slice(SW_BLOCK * h, SW_BLOCK * (h + 1))
                do = do_ref[qr, hl]
                s_scr[rows, :] = _dotg(q_ref[qr, hl], kp[kr, kl], NT) + madd
                dp_scr[rows, :] = _dotg(do, vp[kr, kl], NT)
                deltas.append(jnp.sum(do.astype(F32) * o_ref[qr, hl].astype(F32), axis=1, keepdims=True))
            delta = jnp.concatenate(deltas, axis=0)
            p, ps = _sw_softmax(s_scr[...] + b_ref[...], sink_col)
            ds = p * (dp_scr[...] - delta)
            db_ref[...] = db_ref[...] + ds
            dsk_ref[...] = dsk_ref[...] - jnp.broadcast_to(ps * delta, (SW_STACK, LANES))
            ds_scr[...] = ds.astype(BF16)
            p_scr[...] = p.astype(BF16)
            for g in range(SW_HEADS // SW_REP):
                kl = slice(HEAD_DIM * g, HEAD_DIM * (g + 1))
                k = kp[kr, kl]
                dkw = jnp.zeros((3 * SW_BLOCK, HEAD_DIM), F32)
                dvw = jnp.zeros((3 * SW_BLOCK, HEAD_DIM), F32)
                for r in range(SW_REP):
                    h = g * SW_REP + r
                    hl, rows = slice(HEAD_DIM * h, HEAD_DIM * (h + 1)), slice(SW_BLOCK * h, SW_BLOCK * (h + 1))
                    dsb = ds_scr[rows, :]
                    dq_ref[qr, hl] = _dot(dsb, k)
                    dkw = dkw + _dotg(dsb, q_ref[qr, hl], TN)
                    dvw = dvw + _dotg(p_scr[rows, :], do_ref[qr, hl], TN)
                dkp[kr, kl] = dkp[kr, kl] + dkw
                dvp[kr, kl] = dvp[kr, kl] + dvw
            return carry

        lax.fori_loop(0, nb, blk, 0)
        dk_ref[...] = dkp[SW_BLOCK:SW_BLOCK + s, :]
        dv_ref[...] = dvp[SW_BLOCK:SW_BLOCK + s, :]

    bias_spec = _full((SW_STACK, 3 * SW_BLOCK))
    return pl.pallas_call(
        body, name=name, grid=(1,),
        in_specs=[_full((s, SW_Q_WIDTH)), _full((s, SW_KV_WIDTH)),
                  pl.BlockSpec((s, SW_KV_WIDTH), lambda i: (0, v_blk)),
                  bias_spec, pl.BlockSpec(memory_space=pltpu.SMEM),
                  _full((s, SW_Q_WIDTH)), _full((s, SW_Q_WIDTH))],
        out_specs=[_full((s, SW_Q_WIDTH)), _full((s, SW_KV_WIDTH)), _full((s, SW_KV_WIDTH)), bias_spec,
                   _full((SW_STACK, LANES))],
        out_shape=[jax.ShapeDtypeStruct((s, SW_Q_WIDTH), F32), jax.ShapeDtypeStruct((s, SW_KV_WIDTH), F32),
                   jax.ShapeDtypeStruct((s, SW_KV_WIDTH), F32),
                   jax.ShapeDtypeStruct((SW_STACK, 3 * SW_BLOCK), F32),
                   jax.ShapeDtypeStruct((SW_STACK, LANES), F32)],
        scratch_shapes=[pltpu.VMEM((pad, SW_KV_WIDTH), BF16), pltpu.VMEM((pad, SW_KV_WIDTH), BF16),
                        pltpu.VMEM((pad, SW_KV_WIDTH), F32), pltpu.VMEM((pad, SW_KV_WIDTH), F32),
                        pltpu.VMEM((SW_STACK, 3 * SW_BLOCK), F32), pltpu.VMEM((SW_STACK, 3 * SW_BLOCK), F32),
                        pltpu.VMEM((SW_STACK, 3 * SW_BLOCK), BF16), pltpu.VMEM((SW_STACK, 3 * SW_BLOCK), BF16)],
        compiler_params=_params(),
    )(qs, ks, zq, t5b, sink, o_sw, do_sw)


def merge_out(x, o_na, o_sw, gt, wbna_t, wbsw_t, wout, name):
    s, d = x.shape
    tm = _row_tile(s)

    def body(x_ref, ona_ref, osw_ref, gt_ref, wna_ref, wsw_ref, wo_ref, xo_ref, ana_ref, asw_ref, mg_ref):
        a_na = _dotg(ona_ref[...], wna_ref[...], NT)
        a_sw = _dotg(osw_ref[...], wsw_ref[...], NT)
        ana_ref[...] = a_na.astype(BF16)
        asw_ref[...] = a_sw.astype(BF16)
        merged = (gt_ref[:, 0:d].astype(F32) * a_na + gt_ref[:, d:2 * d].astype(F32) * a_sw).astype(BF16)
        mg_ref[...] = merged
        xo_ref[...] = x_ref[...] + _dot(merged, wo_ref[...])

    return pl.pallas_call(
        body, name=name, grid=(s // tm,),
        in_specs=[_rows(tm, d), _rows(tm, 512), _rows(tm, 512), _rows(tm, 2 * d),
                  _mat(*wbna_t), _mat(*wbsw_t), _mat(*wout)],
        out_specs=[_rows(tm, d)] * 4,
        out_shape=[jax.ShapeDtypeStruct((s, d), F32)] + [jax.ShapeDtypeStruct((s, d), BF16)] * 3,
        compiler_params=_params(),
    )(x, o_na, o_sw, gt, wbna_t[0], wbsw_t[0], wout[0])


def mix_bwd_out(dx, gt, a_na, a_sw, wbna_t, wbsw_t, wout, dep, name):
    s, d = dx.shape
    tm = _row_tile(s)

    def body(dx_ref, gt_ref, ana_ref, asw_ref, wna_ref, wsw_ref, wo_ref, dep_ref,
             dxb_ref, dzg_ref, dana_ref, dasw_ref, dona_ref, dosw_ref, dbg_ref):
        @pl.when(pl.program_id(0) == 0)
        def _():
            dbg_ref[...] = jnp.zeros(dbg_ref.shape, F32)

        dxb = dx_ref[...].astype(BF16)
        dxb_ref[...] = dxb
        dm = _dotg(dxb, wo_ref[...], NT)
        for i, (a_ref, da_ref, w_ref, do_ref) in enumerate(
                [(ana_ref, dana_ref, wna_ref, dona_ref), (asw_ref, dasw_ref, wsw_ref, dosw_ref)]):
            gi = gt_ref[:, i * d:(i + 1) * d].astype(F32)
            da = (dm * gi).astype(BF16)
            da_ref[...] = da
            do_ref[...] = _dot(da, w_ref[...]).astype(BF16)
            dzg = dm * a_ref[...].astype(F32) * gi * (1.0 - gi)
            dzg_ref[:, i * d:(i + 1) * d] = dzg.astype(BF16)
            dbg_ref[:, i * d:(i + 1) * d] = dbg_ref[:, i * d:(i + 1) * d] + jnp.sum(dzg, axis=0, keepdims=True)

    return pl.pallas_call(
        body, name=name, grid=(s // tm,),
        in_specs=[_rows(tm, d), _rows(tm, 2 * d), _rows(tm, d), _rows(tm, d),
                  _mat(*wbna_t), _mat(*wbsw_t), _mat(*wout), _full(dep.shape)],
        out_specs=[_rows(tm, d), _rows(tm, 2 * d), _rows(tm, d), _rows(tm, d), _rows(tm, 512), _rows(tm, 512),
                   _full((1, 2 * d))],
        out_shape=[jax.ShapeDtypeStruct((s, d), BF16), jax.ShapeDtypeStruct((s, 2 * d), BF16),
                   jax.ShapeDtypeStruct((s, d), BF16), jax.ShapeDtypeStruct((s, d), BF16),
                   jax.ShapeDtypeStruct((s, 512), BF16), jax.ShapeDtypeStruct((s, 512), BF16),
                   jax.ShapeDtypeStruct((1, 2 * d), F32)],
        compiler_params=_params(),
    )(dx, gt, a_na, a_sw, wbna_t[0], wbsw_t[0], wout[0], dep)


def qk_norm_bwd(dqa, dka, dva, dqs, dks, dvs, zq, dzg, gq_na, gk_na, gq_sw, gk_sw, bd, name):
    s = zq.shape[0]
    d2 = dzg.shape[1]
    n_in = QKV_WIDTH + d2
    tm = _row_tile(s)

    def body(dqa_ref, dka_ref, dva_ref, dqs_ref, dks_ref, dvs_ref, zq_ref, dzg_ref,
             gqa_ref, gka_ref, gqs_ref, gks_ref, bd_ref, dz_ref, dgqa_ref, dgka_ref, dgqs_ref, dgks_ref):
        @pl.when(pl.program_id(0) == 0)
        def _():
            for r in (dgqa_ref, dgka_ref, dgqs_ref, dgks_ref):
                r[...] = jnp.zeros(r.shape, F32)

        bd512 = bd_ref[...]
        bd128 = bd_ref[0:SW_KV_WIDTH, 0:SW_KV_WIDTH]

        def one(c0, c1, dy_ref, g_ref, dg_ref, bdm, scale):
            z = zq_ref[:, c0:c1].astype(F32)
            r = lax.rsqrt(_group_mean(z * z, bdm) + EPS)
            zh = z * r
            dy = dy_ref[...] * scale
            dyg = dy * g_ref[...]
            dz = r * (dyg - zh * _group_mean(dyg * zh, bdm))
            dz_ref[:, c0:c1] = dz.astype(BF16)
            dg_ref[...] = dg_ref[...] + jnp.sum(dy * zh, axis=0, keepdims=True)

        one(0, 512, dqa_ref, gqa_ref, dgqa_ref, bd512, SCALE)
        one(512, 1024, dka_ref, gka_ref, dgka_ref, bd512, 1.0)
        dz_ref[:, 1024:1536] = dva_ref[...].astype(BF16)
        one(1536, 2048, dqs_ref, gqs_ref, dgqs_ref, bd512, SCALE)
        one(2048, 2176, dks_ref, gks_ref, dgks_ref, bd128, 1.0)
        dz_ref[:, 2176:2304] = dvs_ref[...].astype(BF16)
        dz_ref[:, QKV_WIDTH:n_in] = dzg_ref[...]

    return pl.pallas_call(
        body, name=name, grid=(s // tm,),
        in_specs=[_rows(tm, 512), _rows(tm, 512), _rows(tm, 512), _rows(tm, 512), _rows(tm, 128), _rows(tm, 128),
                  _rows(tm, QKV_WIDTH), _rows(tm, d2),
                  _full((1, 512)), _full((1, 512)), _full((1, 512)), _full((1, 128)), _full((512, 512))],
        out_specs=[_rows(tm, n_in), _full((1, 512)), _full((1, 512)), _full((1, 512)), _full((1, 128))],
        out_shape=[jax.ShapeDtypeStruct((s, n_in), BF16)] + [jax.ShapeDtypeStruct((1, 512), F32)] * 3
                  + [jax.ShapeDtypeStruct((1, 128), F32)],
        compiler_params=_params(),
    )(dqa, dka, dva, dqs, dks, dvs, zq, dzg, gq_na, gk_na, gq_sw, gk_sw, bd)


def ffn_bwd_act(dx, wd, hg, hu, name):
    s, d = dx.shape
    f = wd[0].shape[1]
    tm = _row_tile(s)
    fc = _col_chunk(f)

    def body(dx_ref, w_ref, hg_ref, hu_ref, dxb_ref, dhg_ref, dhu_ref):
        dxb = dx_ref[...].astype(BF16)
        dxb_ref[...] = dxb
        for c0 in range(0, f, fc):
            dact = 0.5 * _dotg(dxb, w_ref[c0:c0 + fc, :], NT)
            hg = hg_ref[:, c0:c0 + fc].astype(F32)
            hu = hu_ref[:, c0:c0 + fc].astype(F32)
            sg = _sigmoid(hg)
            dhu_ref[:, c0:c0 + fc] = (dact * hg * sg).astype(BF16)
            dhg_ref[:, c0:c0 + fc] = (dact * hu * sg * (1.0 + hg * (1.0 - sg))).astype(BF16)

    return pl.pallas_call(
        body, name=name, grid=(s // tm,),
        in_specs=[_rows(tm, d), _mat(*wd), _rows(tm, f), _rows(tm, f)],
        out_specs=[_rows(tm, d), _rows(tm, f), _rows(tm, f)],
        out_shape=[jax.ShapeDtypeStruct((s, d), BF16), jax.ShapeDtypeStruct((s, f), BF16),
                   jax.ShapeDtypeStruct((s, f), BF16)],
        compiler_params=_params(),
    )(dx, wd[0], hg, hu)


def proj_bwd_norm(acts, weights, x, gain, dx, dep, name):
    s, d = x.shape
    tm = _row_tile(s)
    n = len(acts)

    def body(*refs):
        a_refs, w_refs = refs[:n], refs[n:2 * n]
        x_ref, g_ref, dx_ref, _, o_ref, dg_ref = refs[2 * n:]

        @pl.when(pl.program_id(0) == 0)
        def _():
            dg_ref[...] = jnp.zeros(dg_ref.shape, F32)

        dxn = _dot(a_refs[0][...], w_refs[0][...])
        for a_ref, w_ref in zip(a_refs[1:], w_refs[1:]):
            dxn = dxn + _dot(a_ref[...], w_ref[...])
        xv = x_ref[...]
        r = _rstd(xv)
        xh = xv * r
        dxh = dxn * g_ref[...]
        o_ref[...] = dx_ref[...] + r * (dxh - xh * jnp.mean(dxh * xh, axis=-1, keepdims=True))
        dg_ref[...] = dg_ref[...] + jnp.sum(dxn * xh, axis=0, keepdims=True)

    return pl.pallas_call(
        body, name=name, grid=(s // tm,),
        in_specs=[_rows(tm, a.shape[1]) for a in acts] + [_mat(*w) for w in weights]
                 + [_rows(tm, d), _full((1, d)), _rows(tm, d), _full(dep.shape)],
        out_specs=[_rows(tm, d), _full((1, d))],
        out_shape=[jax.ShapeDtypeStruct((s, d), F32), jax.ShapeDtypeStruct((1, d), F32)],
        compiler_params=_params(),
    )(*acts, *[w[0] for w in weights], x, gain, dx, dep)


def tn_matmul(a, b, scale, name):
    s, n = a.shape
    k = b.shape[1]
    tn = _tn_tile(n)

    def body(a_ref, b_ref, o_ref):
        o_ref[...] = (scale * _dotg(a_ref[...], b_ref[...], TN)).astype(BF16)

    return pl.pallas_call(
        body, name=name, grid=(n // tn,),
        in_specs=[pl.BlockSpec((s, tn), lambda i: (0, i)),
                  pl.BlockSpec((s, k), lambda i: (0, 0), pipeline_mode=ONCE)],
        out_specs=pl.BlockSpec((tn, k), lambda i: (i, 0)),
        out_shape=jax.ShapeDtypeStruct((n, k), BF16),
        compiler_params=_params(),
    )(a, b)


def loss_grad(y, target, name):
    s, d = y.shape
    tm = _row_tile(s)

    def body(y_ref, t_ref, dy_ref, acc_ref):
        @pl.when(pl.program_id(0) == 0)
        def _():
            acc_ref[...] = jnp.zeros(acc_ref.shape, F32)

        err = y_ref[...] - t_ref[...]
        dy_ref[...] = err * (1.0 / d)
        e2 = err * err
        part = jnp.sum(e2.reshape(tm // 8, 8, d), axis=0)
        acc = part[:, 0:LANES]
        for c0 in range(LANES, d, LANES):
            acc = acc + part[:, c0:c0 + LANES]
        acc_ref[...] = acc_ref[...] + acc

    return pl.pallas_call(
        body, name=name, grid=(s // tm,),
        in_specs=[_rows(tm, d), _rows(tm, d)],
        out_specs=[_rows(tm, d), _full((8, LANES))],
        out_shape=[jax.ShapeDtypeStruct((s, d), F32), jax.ShapeDtypeStruct((8, LANES), F32)],
        compiler_params=_params(),
    )(y, target)


def _mesh_pos():
    return lax.axis_index("x"), lax.axis_index("y"), lax.axis_index("c")


def _peers():
    x, y, c = _mesh_pos()
    peers = []
    for rel in range(1, N_DEV):
        peers.append((1 - x if rel & 4 else x, 1 - y if rel & 2 else y, 1 - c if rel & 1 else c))
    return 4 * x + 2 * y + c, peers


HBM_SPEC = pl.BlockSpec(memory_space=pltpu.HBM)
SEM_SPEC = pl.BlockSpec(memory_space=pltpu.SEMAPHORE)


def _split_call(body, name, thru, n_sems, extra=(), with_token=True):
    hbm = lambda t: pltpu.with_memory_space_constraint(t, pltpu.HBM)
    effect = pltpu.CompilerParams(has_side_effects=pltpu.SideEffectType.DATAFLOW_SIDE_EFFECTING)
    nt = len(thru)
    thru_shapes = [pltpu.HBM(t.shape, t.dtype) for t in thru]
    if with_token:
        (after,) = extra
        outs = pl.pallas_call(
            body, name=name, in_specs=[HBM_SPEC] * nt + [pl.BlockSpec(memory_space=pl.ANY)],
            out_specs=[SEM_SPEC] * len(n_sems) + [HBM_SPEC] * nt + [pl.BlockSpec(memory_space=pltpu.VMEM)],
            out_shape=[pltpu.SemaphoreType.DMA((k,)) for k in n_sems] + thru_shapes
                      + [jax.ShapeDtypeStruct((8, LANES), F32)],
            input_output_aliases={i: len(n_sems) + i for i in range(nt)}, compiler_params=effect,
        )(*[hbm(t) for t in thru], after)
        return outs[:len(n_sems)], outs[len(n_sems):-1], outs[-1]
    return pl.pallas_call(
        body, name=name,
        in_specs=[HBM_SPEC] * nt + [SEM_SPEC] * len(n_sems) + [pl.BlockSpec(memory_space=pl.ANY)],
        out_specs=[HBM_SPEC] * nt, out_shape=thru_shapes,
        input_output_aliases={i: i for i in range(nt)}, compiler_params=effect,
    )(*thru, *extra)


def _gather_targets():
    x, y, c = _mesh_pos()
    return 4 * x + 2 * y + c, [(x, y, 1 - c), (1 - x, y, c), (x, 1 - y, c), (1 - x, 1 - y, c)]


def gather_start(shards, after, name):
    n = len(shards)
    zones = [lax.empty((w.shape[0], N_DEV) + w.shape[1:], w.dtype) for w in shards]

    def body(*refs):
        ins, zs = refs[:n], refs[n:2 * n]
        send_sems, recv_sems, local_sems = refs[2 * n + 1:2 * n + 4]
        token = refs[-1]
        me, targets = _gather_targets()
        for a in range(n):
            pltpu.make_async_copy(ins[a], zs[a].at[:, me], local_sems.at[a]).start()
            for k, to in enumerate(targets):
                pltpu.make_async_remote_copy(
                    src_ref=ins[a], dst_ref=zs[a].at[:, me], send_sem=send_sems.at[4 * a + k],
                    recv_sem=recv_sems.at[4 * a + k], device_id=to, device_id_type=MESH).start()
        token[...] = jnp.zeros(token.shape, F32)

    sems, thru, token = _split_call(body, name, list(shards) + zones, (4 * n, 4 * n, n), extra=(after,))
    return (sems, thru, n), token


def gather_wait(started, after, name):
    sems, thru, n = started

    def body(*refs):
        zs = refs[n:2 * n]
        send_sems, recv_sems, local_sems = refs[2 * n:2 * n + 3]
        _, targets = _gather_targets()
        for a in range(n):
            for k, to in enumerate(targets):
                cp = pltpu.make_async_remote_copy(
                    src_ref=zs[a].at[:, 0], dst_ref=zs[a].at[:, 0], send_sem=send_sems.at[4 * a + k],
                    recv_sem=recv_sems.at[4 * a + k], device_id=to, device_id_type=MESH)
                cp.wait_send()
                cp.wait_recv()
            pltpu.make_async_copy(zs[a].at[:, 0], zs[a].at[:, 0], local_sems.at[a]).wait()

    return _split_call(body, name, thru, (4 * n, 4 * n, n), extra=(*sems, after), with_token=False)[n:]


def forward_start(zones, after, name):
    n = len(zones)

    def body(*refs):
        zs = refs[:n]
        send_sems, recv_sems = refs[n + 1:n + 3]
        token = refs[-1]
        x, y, c = _mesh_pos()
        for a in range(n):
            for j, chip in enumerate([(1 - x, y), (x, 1 - y), (1 - x, 1 - y)]):
                blk = zs[a].at[:, 4 * chip[0] + 2 * chip[1] + c]
                pltpu.make_async_remote_copy(
                    src_ref=blk, dst_ref=blk, send_sem=send_sems.at[3 * a + j], recv_sem=recv_sems.at[3 * a + j],
                    device_id=(x, y, 1 - c), device_id_type=MESH).start()
        token[...] = jnp.zeros(token.shape, F32)

    sems, thru, token = _split_call(body, name, list(zones), (3 * n, 3 * n), extra=(after,))
    return (sems, thru, n), token


def forward_wait(started, after, name):
    sems, thru, n = started

    def body(*refs):
        zs = refs[:n]
        send_sems, recv_sems = refs[n:n + 2]
        x, y, c = _mesh_pos()
        for a in range(n):
            for j in range(3):
                cp = pltpu.make_async_remote_copy(
                    src_ref=zs[a].at[:, 0], dst_ref=zs[a].at[:, 0], send_sem=send_sems.at[3 * a + j],
                    recv_sem=recv_sems.at[3 * a + j], device_id=(x, y, 1 - c), device_id_type=MESH)
                cp.wait_send()
                cp.wait_recv()

    return _split_call(body, name, thru, (3 * n, 3 * n), extra=(*sems, after), with_token=False)


def scatter_start(groups, name):
    n = len(groups)
    flat = [g for grp in groups for g in grp]
    nf = len(flat)
    offs = np.cumsum([0] + [len(grp) for grp in groups])
    lands = [lax.empty((N_DEV, len(grp)) + grp[0].shape[1:], grp[0].dtype) for grp in groups]

    def body(*refs):
        ins, zones = refs[:nf], refs[nf:nf + n]
        send_sems, recv_sems, local_sems = refs[nf + n:nf + n + 3]
        token = refs[-1]
        me, peers = _peers()
        for a in range(n):
            for w in range(len(groups[a])):
                pltpu.make_async_copy(ins[offs[a] + w].at[me], zones[a].at[me, w], local_sems.at[a]).start()
        for k, peer in enumerate(peers):
            p_id = 4 * peer[0] + 2 * peer[1] + peer[2]
            for a in range(n):
                for w in range(len(groups[a])):
                    pltpu.make_async_remote_copy(
                        src_ref=ins[offs[a] + w].at[p_id], dst_ref=zones[a].at[me, w],
                        send_sem=send_sems.at[7 * a + k], recv_sem=recv_sems.at[7 * a + k],
                        device_id=peer, device_id_type=MESH).start()
        token[...] = jnp.zeros(token.shape, F32)

    hbm = lambda t: pltpu.with_memory_space_constraint(t, pltpu.HBM)
    outs = pl.pallas_call(
        body, name=name,
        in_specs=[HBM_SPEC] * (nf + n),
        out_specs=[SEM_SPEC] * 3 + [HBM_SPEC] * (nf + n) + [pl.BlockSpec(memory_space=pltpu.VMEM)],
        out_shape=[pltpu.SemaphoreType.DMA((7 * n,)), pltpu.SemaphoreType.DMA((7 * n,)), pltpu.SemaphoreType.DMA((n,))]
                  + [pltpu.HBM(t.shape, t.dtype) for t in flat + lands]
                  + [jax.ShapeDtypeStruct((8, LANES), F32)],
        input_output_aliases={i: 3 + i for i in range(nf + n)},
        compiler_params=pltpu.CompilerParams(has_side_effects=pltpu.SideEffectType.DATAFLOW_SIDE_EFFECTING),
    )(*[hbm(t) for t in flat], *[hbm(t) for t in lands])
    sems, thru, token = outs[:3], outs[3:3 + nf + n], outs[-1]
    return (sems, thru, [len(grp) for grp in groups]), token


def scatter_wait(started, after, name):
    (send_sems, recv_sems, local_sems), thru, sizes = started
    n = len(sizes)
    nf = len(thru) - n

    def body(*refs):
        zones = refs[nf:nf + n]
        s_sems, r_sems, l_sems = refs[nf + n:nf + n + 3]
        me, peers = _peers()
        for a in range(n):
            for k, peer in enumerate(peers):
                cp = pltpu.make_async_remote_copy(
                    src_ref=zones[a].at[0], dst_ref=zones[a].at[0],
                    send_sem=s_sems.at[7 * a + k], recv_sem=r_sems.at[7 * a + k], device_id=peer,
                    device_id_type=MESH)
                cp.wait_send()
                cp.wait_recv()
            pltpu.make_async_copy(zones[a].at[0], zones[a].at[0], l_sems.at[a]).wait()

    outs = pl.pallas_call(
        body, name=name,
        in_specs=[HBM_SPEC] * (nf + n) + [SEM_SPEC] * 3 + [pl.BlockSpec(memory_space=pl.ANY)],
        out_specs=[HBM_SPEC] * (nf + n),
        out_shape=[pltpu.HBM(t.shape, t.dtype) for t in thru],
        input_output_aliases={i: i for i in range(nf + n)},
        compiler_params=pltpu.CompilerParams(has_side_effects=pltpu.SideEffectType.DATAFLOW_SIDE_EFFECTING),
    )(*thru, send_sems, recv_sems, local_sems, after)
    return outs[nf:]


def pair_start(grads, after, name):
    nw = len(grads)
    land = lax.empty((4, nw) + grads[0].shape[1:], grads[0].dtype)

    def body(*refs):
        ins, zone = refs[:nw], refs[nw]
        send_sems, recv_sems = refs[nw + 2:nw + 4]
        x, y, c = _mesh_pos()
        for j in range(4):
            for w in range(nw):
                pltpu.make_async_remote_copy(
                    src_ref=ins[w].at[2 * j + (1 - c)], dst_ref=zone.at[j, w], send_sem=send_sems.at[0],
                    recv_sem=recv_sems.at[0], device_id=(x, y, 1 - c), device_id_type=MESH).start()
        refs[-1][...] = jnp.zeros(refs[-1].shape, F32)

    sems, thru, token = _split_call(body, name, list(grads) + [land], (1, 1), extra=(after,))
    return (sems, thru, nw), token


def pair_wait(started, after, name):
    sems, thru, nw = started

    def body(*refs):
        zone = refs[nw]
        send_sems, recv_sems = refs[nw + 1:nw + 3]
        x, y, c = _mesh_pos()
        cp = pltpu.make_async_remote_copy(src_ref=zone, dst_ref=zone, send_sem=send_sems.at[0],
                                          recv_sem=recv_sems.at[0], device_id=(x, y, 1 - c), device_id_type=MESH)
        cp.wait_send()
        cp.wait_recv()

    outs = _split_call(body, name, thru, (1, 1), extra=(*sems, after), with_token=False)
    return outs[:nw], outs[nw]


def pair_sum(grads, land, name):
    nw = len(grads)
    _, r, c_dim = grads[0].shape

    def body(*refs):
        g_refs, l_ref, o_ref = refs[:nw], refs[nw], refs[nw + 1]
        core = lax.axis_index("c")
        for w in range(nw):
            o_ref[0, w] = (g_refs[w][0, core].astype(F32) + l_ref[0, w].astype(F32)).astype(BF16)

    return pl.pallas_call(
        body, name=name, grid=(4,),
        in_specs=[pl.BlockSpec((1, 2, r, c_dim), lambda j: (j, 0, 0, 0))] * nw
                 + [pl.BlockSpec((1, nw, r, c_dim), lambda j: (j, 0, 0, 0))],
        out_specs=pl.BlockSpec((1, nw, r, c_dim), lambda j: (j, 0, 0, 0)),
        out_shape=jax.ShapeDtypeStruct((4, nw, r, c_dim), BF16),
        compiler_params=_params(),
    )(*[g.reshape(4, 2, r, c_dim) for g in grads], land)


def _other_chips():
    x, y, c = _mesh_pos()
    chips = []
    for rel in range(1, 4):
        px, py = (1 - x if rel & 2 else x), (1 - y if rel & 1 else y)
        chips.append((px, py, 2 * px + py))
    return 2 * x + y, c, chips


def chip_start(pair_sums, after, name):
    land = lax.empty(pair_sums.shape, pair_sums.dtype)

    def body(*refs):
        h_ref, zone = refs[0], refs[1]
        send_sems, recv_sems, local_sem = refs[3:6]
        mine, c, chips = _other_chips()
        pltpu.make_async_copy(h_ref.at[mine], zone.at[mine], local_sem.at[0]).start()
        for k, (px, py, j) in enumerate(chips):
            pltpu.make_async_remote_copy(
                src_ref=h_ref.at[j], dst_ref=zone.at[mine], send_sem=send_sems.at[k], recv_sem=recv_sems.at[k],
                device_id=(px, py, c), device_id_type=MESH).start()
        refs[-1][...] = jnp.zeros(refs[-1].shape, F32)

    sems, thru, token = _split_call(body, name, [pair_sums, land], (3, 3, 1), extra=(after,))
    return (sems, thru), token


def chip_wait(started, after, name):
    sems, thru = started

    def body(*refs):
        zone = refs[1]
        send_sems, recv_sems, local_sem = refs[2:5]
        _, c, chips = _other_chips()
        for k, (px, py, _) in enumerate(chips):
            cp = pltpu.make_async_remote_copy(
                src_ref=zone.at[0], dst_ref=zone.at[0], send_sem=send_sems.at[k], recv_sem=recv_sems.at[k],
                device_id=(px, py, c), device_id_type=MESH)
            cp.wait_send()
            cp.wait_recv()
        pltpu.make_async_copy(zone.at[0], zone.at[0], local_sem.at[0]).wait()

    return _split_call(body, name, thru, (3, 3, 1), extra=(*sems, after), with_token=False)[1]


def share_small(parts, after):
    n = len(parts)

    def body(*refs):
        ins, outs = refs[:n], refs[n + 1:2 * n + 1]
        send_sems, recv_sems, local_sems = refs[2 * n + 1:]
        me, peers = _peers()
        copies = []
        for i in range(n):
            copies.append(pltpu.make_async_copy(ins[i], outs[i].at[me], local_sems.at[i]))
            copies += [pltpu.make_async_remote_copy(
                src_ref=ins[i], dst_ref=outs[i].at[me], send_sem=send_sems.at[7 * i + k],
                recv_sem=recv_sems.at[7 * i + k], device_id=peer, device_id_type=MESH)
                for k, peer in enumerate(peers)]
        for cp in copies:
            cp.start()
        for cp in copies:
            cp.wait()

    vm = pl.BlockSpec(memory_space=pltpu.VMEM)
    return pl.pallas_call(
        body, name="share_small", in_specs=[vm] * n + [pl.BlockSpec(memory_space=pl.ANY)], out_specs=[vm] * n,
        out_shape=[jax.ShapeDtypeStruct((N_DEV,) + p.shape, p.dtype) for p in parts],
        scratch_shapes=[pltpu.SemaphoreType.DMA((7 * n,)), pltpu.SemaphoreType.DMA((7 * n,)),
                        pltpu.SemaphoreType.DMA((n,))],
    )(*parts, after)


def _adamw_math(w, g, m, v):
    m = ADAM_B1 * m + (1.0 - ADAM_B1) * g
    v = ADAM_B2 * v + (1.0 - ADAM_B2) * (g * g)
    m_hat = m / (1.0 - ADAM_B1 ** ADAM_STEP)
    v_hat = v / (1.0 - ADAM_B2 ** ADAM_STEP)
    delta = -ADAM_LR * (m_hat / (jnp.sqrt(v_hat) + ADAM_EPS) + ADAM_WD * w)
    return delta, m, v


def adamw_layer(zone, w_idx, layer, w, m, v, prev, after, name):
    n_src, _, r, c = zone.shape
    depth = w.shape[0]
    if prev is None:
        prev = tuple(lax.empty((depth, r, c), F32) for _ in range(4))
    tr = r // 2 if r % 16 == 0 else r

    def body(z_ref, w_ref, m_ref, v_ref, *rest):
        g_ref, d_ref, mo_ref, vo_ref = rest[5:]
        g = z_ref[0].astype(F32)
        for src in range(1, n_src):
            g = g + z_ref[src].astype(F32)
        g_ref[...] = g
        d_ref[...], mo_ref[...], vo_ref[...] = _adamw_math(w_ref[...], g, m_ref[...], v_ref[...])

    rows = pl.BlockSpec((None, tr, c), lambda i: (layer, i, 0))
    anywhere = pl.BlockSpec(memory_space=pl.ANY)
    return pl.pallas_call(
        body, name=name, grid=(r // tr,),
        in_specs=[pl.BlockSpec((n_src, None, tr, c), lambda i: (0, w_idx, i, 0)), rows, rows, rows]
                 + [anywhere] * 5,
        out_specs=[rows] * 4,
        out_shape=[jax.ShapeDtypeStruct((depth, r, c), F32)] * 4,
        input_output_aliases={4 + k: k for k in range(4)},
        compiler_params=_params(),
    )(zone, w, m, v, *prev, after)


def adamw_small(ws, recvs, ms, vs, name):
    n = len(ws)

    def body(*refs):
        w_refs, r_refs, m_refs, v_refs = (refs[i * n:(i + 1) * n] for i in range(4))
        g_refs, d_refs, mo_refs, vo_refs = (refs[(4 + i) * n:(5 + i) * n] for i in range(4))
        for i in range(n):
            g = r_refs[i][0]
            for src in range(1, N_DEV):
                g = g + r_refs[i][src]
            g_refs[i][...] = g
            d_refs[i][...], mo_refs[i][...], vo_refs[i][...] = _adamw_math(w_refs[i][...], g, m_refs[i][...],
                                                                            v_refs[i][...])

    vm = pl.BlockSpec(memory_space=pltpu.VMEM)
    outs = pl.pallas_call(
        body, name=name, in_specs=[vm] * (4 * n), out_specs=[vm] * (4 * n),
        out_shape=[jax.ShapeDtypeStruct(w.shape, F32) for w in ws] * 4,
        compiler_params=pltpu.CompilerParams(vmem_limit_bytes=V7X_VMEM_LIMIT),
    )(*ws, *recvs, *ms, *vs)
    return [outs[i * n:(i + 1) * n] for i in range(4)]


SMALL_NAMES = ("ffn1_norm", "mix_norm", "ffn2_norm", "b_gate", "na_q_norm", "na_k_norm", "sw_q_norm", "sw_k_norm",
               "na_rpb", "sw_sink", "t5_rel_table")


def kernel(x, ffn1_norm, ffn1_w_gate, ffn1_w_up, ffn1_w_down, mix_norm, w_in, b_gate, na_q_norm, na_k_norm, na_rpb, sw_q_norm, sw_k_norm, sw_sink, t5_rel_table, w_branch_na, w_branch_sw, w_out, ffn2_norm, ffn2_w_gate, ffn2_w_up, ffn2_w_down, loss_target, m_ffn1_norm, m_ffn1_w_gate, m_ffn1_w_up, m_ffn1_w_down, m_mix_norm, m_w_in, m_b_gate, m_na_q_norm, m_na_k_norm, m_na_rpb, m_sw_q_norm, m_sw_k_norm, m_sw_sink, m_t5_rel_table, m_w_branch_na, m_w_branch_sw, m_w_out, m_ffn2_norm, m_ffn2_w_gate, m_ffn2_w_up, m_ffn2_w_down, v_ffn1_norm, v_ffn1_w_gate, v_ffn1_w_up, v_ffn1_w_down, v_mix_norm, v_w_in, v_b_gate, v_na_q_norm, v_na_k_norm, v_na_rpb, v_sw_q_norm, v_sw_k_norm, v_sw_sink, v_t5_rel_table, v_w_branch_na, v_w_branch_sw, v_w_out, v_ffn2_norm, v_ffn2_w_gate, v_ffn2_w_up, v_ffn2_w_down):
    weights = dict(ffn1_norm=ffn1_norm, ffn1_w_gate=ffn1_w_gate, ffn1_w_up=ffn1_w_up, ffn1_w_down=ffn1_w_down,
                   mix_norm=mix_norm, w_in=w_in, b_gate=b_gate, na_q_norm=na_q_norm, na_k_norm=na_k_norm,
                   na_rpb=na_rpb, sw_q_norm=sw_q_norm, sw_k_norm=sw_k_norm, sw_sink=sw_sink,
                   t5_rel_table=t5_rel_table, w_branch_na=w_branch_na, w_branch_sw=w_branch_sw, w_out=w_out,
                   ffn2_norm=ffn2_norm, ffn2_w_gate=ffn2_w_gate, ffn2_w_up=ffn2_w_up, ffn2_w_down=ffn2_w_down)
    mom_m = dict(ffn1_norm=m_ffn1_norm, ffn1_w_gate=m_ffn1_w_gate, ffn1_w_up=m_ffn1_w_up, ffn1_w_down=m_ffn1_w_down,
                 mix_norm=m_mix_norm, w_in=m_w_in, b_gate=m_b_gate, na_q_norm=m_na_q_norm, na_k_norm=m_na_k_norm,
                 na_rpb=m_na_rpb, sw_q_norm=m_sw_q_norm, sw_k_norm=m_sw_k_norm, sw_sink=m_sw_sink,
                 t5_rel_table=m_t5_rel_table, w_branch_na=m_w_branch_na, w_branch_sw=m_w_branch_sw, w_out=m_w_out,
                 ffn2_norm=m_ffn2_norm, ffn2_w_gate=m_ffn2_w_gate, ffn2_w_up=m_ffn2_w_up, ffn2_w_down=m_ffn2_w_down)
    mom_v = dict(ffn1_norm=v_ffn1_norm, ffn1_w_gate=v_ffn1_w_gate, ffn1_w_up=v_ffn1_w_up, ffn1_w_down=v_ffn1_w_down,
                 mix_norm=v_mix_norm, w_in=v_w_in, b_gate=v_b_gate, na_q_norm=v_na_q_norm, na_k_norm=v_na_k_norm,
                 na_rpb=v_na_rpb, sw_q_norm=v_sw_q_norm, sw_k_norm=v_sw_k_norm, sw_sink=v_sw_sink,
                 t5_rel_table=v_t5_rel_table, w_branch_na=v_w_branch_na, w_branch_sw=v_w_branch_sw, w_out=v_w_out,
                 ffn2_norm=v_ffn2_norm, ffn2_w_gate=v_ffn2_w_gate, ffn2_w_up=v_ffn2_w_up, ffn2_w_down=v_ffn2_w_down)
    order = list(weights)

    depth = ffn1_norm.shape[0]
    s, d = x.shape[1], x.shape[2]
    xs = x[0]
    tr = lambda w: jnp.swapaxes(w, -1, -2)

    merge = lambda t: t.reshape(t.shape[0], N_DEV * t.shape[2], t.shape[3])
    no_dep = jnp.zeros((8, LANES), F32)

    def shards_of(kind, l):
        stack = lambda *ws: jnp.stack(ws).astype(BF16)
        if kind == "ffn1":
            return [stack(tr(ffn1_w_gate[l]), tr(ffn1_w_up[l]), ffn1_w_down[l])]
        if kind == "win":
            return [stack(tr(w_in[l]))]
        return [stack(tr(ffn2_w_gate[l]), tr(ffn2_w_up[l]), ffn2_w_down[l]), stack(w_out[l]),
                stack(tr(w_branch_na[l]), tr(w_branch_sw[l]))]

    def start(kind, l, after):
        return gather_start(shards_of(kind, l), after, f"gather_{kind}_{l}")

    def arrive(started, kind, l, after):
        zones = gather_wait(started, after, f"gather_{kind}_{l}_wait")
        return forward_start(zones, no_dep, f"forward_{kind}_{l}")

    def finish(fwd, kind, l, after):
        return [merge(z) for z in forward_wait(fwd, after, f"forward_{kind}_{l}_wait")]

    bd = jnp.asarray(np.kron(np.eye(NA_WIDTH // HEAD_DIM), np.full((HEAD_DIM, HEAD_DIM), 1.0 / HEAD_DIM)), BF16)
    bmap = jnp.asarray(_t5_bucket_map())
    tile8 = lambda g: jnp.tile(g, NA_WIDTH // HEAD_DIM).reshape(1, NA_WIDTH)
    tile2 = lambda g: jnp.tile(g, SW_KV_WIDTH // HEAD_DIM).reshape(1, SW_KV_WIDTH)

    st_first, tok = start("ffn1", 0, no_dep)
    t5b = t5_expand(t5_rel_table, bmap, tok, "t5_expand").reshape(SW_STACK, 3 * SW_BLOCK)
    fwd, _ = arrive(st_first, "ffn1", 0, t5b)
    st_win, dep = start("win", 0, t5b)
    (first,) = finish(fwd, "ffn1", 0, dep)

    saved = []
    layer_w = {0: dict(wg1=(first, 0), wu1=(first, 1), wd1=(first, 2))}
    cur = xs
    for l in range(depth):
        sv = {}
        lw = layer_w[l]
        sv["x0"] = cur
        sv["xn1"], sv["hg1"], sv["hu1"], sv["act1"] = ffn_up(cur, ffn1_norm[l][None], lw["wg1"], lw["wu1"], dep,
                                                             f"ffn1_up_{l}")
        cur = ffn_down(cur, sv["act1"], lw["wd1"], no_dep, f"ffn1_down_{l}")
        sv["x1"] = cur
        fwd, _ = arrive(st_win, "win", l, cur)
        st_rest, tok = start("rest", l, cur)
        (zb,) = finish(fwd, "win", l, tok)
        lw["win"] = (zb, 0)
        sv["gains"] = (tile8(na_q_norm[l]), tile8(na_k_norm[l]), tile8(sw_q_norm[l]), tile2(sw_k_norm[l]))
        sv["hn"], sv["zq"], sv["qa"], sv["ka"], sv["qs"], sv["ks"], sv["gt"] = mix_in(
            cur, mix_norm[l][None], lw["win"], b_gate[l][None], *sv["gains"], bd, f"mix_in_{l}")
        sv["t2"] = rpb_expand(_rpb_rows(na_rpb[l]), no_dep, f"rpb_expand_{l}")
        sv["o_na"] = na_fwd(sv["qa"], sv["ka"], sv["zq"], sv["t2"], f"na_fwd_{l}")
        dep = no_dep
        if l + 1 < depth:
            st_ffn1, dep = start("ffn1", l + 1, sv["o_na"])
        sv["o_sw"] = sw_fwd(sv["qs"], sv["ks"], sv["zq"], t5b, sw_sink[l], dep, f"sw_fwd_{l}")
        fwd, tok = arrive(st_rest, "rest", l, sv["o_sw"])
        za, zc, zd = finish(fwd, "rest", l, tok)
        lw.update(wg2=(za, 0), wu2=(za, 1), wd2=(za, 2), wout=(zc, 0), wna=(zd, 0), wsw=(zd, 1))
        cur, sv["a_na"], sv["a_sw"], sv["merged"] = merge_out(
            cur, sv["o_na"], sv["o_sw"], sv["gt"], lw["wna"], lw["wsw"], lw["wout"], f"merge_out_{l}")
        sv["x2"] = cur
        dep = no_dep
        if l + 1 < depth:
            st_win, dep = start("win", l + 1, cur)
        sv["xn2"], sv["hg2"], sv["hu2"], sv["act2"] = ffn_up(cur, ffn2_norm[l][None], lw["wg2"], lw["wu2"], dep,
                                                             f"ffn2_up_{l}")
        dep = no_dep
        if l + 1 < depth:
            fwd, dep = arrive(st_ffn1, "ffn1", l + 1, sv["act2"])
        cur = ffn_down(cur, sv["act2"], lw["wd2"], dep, f"ffn2_down_{l}")
        dep = no_dep
        if l + 1 < depth:
            (za,) = finish(fwd, "ffn1", l + 1, cur)
            layer_w[l + 1] = dict(wg1=(za, 0), wu1=(za, 1), wd1=(za, 2))
        saved.append(sv)

    dx, loss_acc = loss_grad(cur, loss_target[0], "loss_grad")
    loss = lax.psum(jnp.sum(loss_acc) * (0.5 / d), ("x", "y", "c"))

    split = lambda t: t.reshape(N_DEV, t.shape[0] // N_DEV, t.shape[1])
    pending = {}
    last_key = "ffn1_0"
    two_level = {last_key}
    small = {k: [None] * depth for k in SMALL_NAMES if k != "t5_rel_table"}
    dbias_sw = []
    for l in reversed(range(depth)):
        sv = saved[l]
        lw = layer_w[l]
        wg1, wu1, wd1, wg2, wu2, wd2 = (lw[k] for k in ("wg1", "wu1", "wd1", "wg2", "wu2", "wd2"))
        win_t, wout_l, wna_t, wsw_t = lw["win"], lw["wout"], lw["wna"], lw["wsw"]
        blocks = ((2, "x2", "xn2", "hg2", "hu2", "act2", wg2, wu2, wd2, "ffn2_norm", 3),
                  (1, "x0", "xn1", "hg1", "hu1", "act1", wg1, wu1, wd1, "ffn1_norm", 0))

        def ffn_backward(dx, blk):
            tag, xk, xnk, hgk, huk, actk, wg, wu, wd, norm_name, slot = blk
            gains = weights[norm_name]
            dxb, dhg, dhu = ffn_bwd_act(dx, wd, sv[hgk], sv[huk], f"ffn{tag}_bwd_act_{l}")
            gwd = tn_matmul(sv[actk], dxb, 0.5, f"ffn{tag}_dwd_{l}")
            gwg = tn_matmul(dhg, sv[xnk], 1.0, f"ffn{tag}_dwg_{l}")
            gwu = tn_matmul(dhu, sv[xnk], 1.0, f"ffn{tag}_dwu_{l}")
            key = f"ffn{tag}_{l}"
            blocks_of = [split(gwg), split(gwu), split(gwd)]
            if key in two_level:
                paired, token = pair_start(blocks_of, dxb, f"pair_{key}")
            else:
                pending[key], token = scatter_start([blocks_of], f"scatter_{key}")
            dx, dg = proj_bwd_norm([dhg, dhu], [wg, wu], sv[xk], gains[l][None], dx, token, f"ffn{tag}_bwd_x_{l}")
            token = no_dep
            if key in two_level:
                thru, land = pair_wait(paired, dx, f"pair_{key}_wait")
                pending[key], token = chip_start(pair_sum(thru, land, f"pair_sum_{key}"), dg, f"chips_{key}")
            small[norm_name][l] = dg[0]
            return dx, token

        dx, token = ffn_backward(dx, blocks[0])
        dxb, dzg, da_na, da_sw, do_na, do_sw, dbg = mix_bwd_out(
            dx, sv["gt"], sv["a_na"], sv["a_sw"], wna_t, wsw_t, wout_l, token, f"mix_bwd_out_{l}")
        small["b_gate"][l] = dbg[0]
        gwout = tn_matmul(sv["merged"], dxb, 1.0, f"dwout_{l}")
        gwna = tn_matmul(da_na, sv["o_na"], 1.0, f"dwna_{l}")
        gwsw = tn_matmul(da_sw, sv["o_sw"], 1.0, f"dwsw_{l}")
        dqa, dka, dva, dt2 = na_bwd(sv["qa"], sv["ka"], sv["zq"], sv["t2"], sv["o_na"], do_na, f"na_bwd_{l}")
        dqs, dks, dvs, dbias, dsink = sw_bwd(sv["qs"], sv["ks"], sv["zq"], t5b, sw_sink[l], sv["o_sw"], do_sw,
                                             f"sw_bwd_{l}")
        dbias_sw.append(dbias.reshape(SW_HEADS, SW_BLOCK, 3 * SW_BLOCK))
        small["sw_sink"][l] = jnp.sum(dsink[:, 0].reshape(SW_HEADS, SW_BLOCK), axis=1)
        small["na_rpb"][l] = _rpb_from_rows(rpb_reduce(dt2, f"rpb_reduce_{l}"))
        dz, dgqa, dgka, dgqs, dgks = qk_norm_bwd(dqa, dka, dva, dqs, dks, dvs, sv["zq"], dzg, *sv["gains"], bd,
                                                 f"qk_norm_bwd_{l}")
        fold = lambda g: jnp.sum(g.reshape(-1, HEAD_DIM), axis=0)
        small["na_q_norm"][l], small["na_k_norm"][l] = fold(dgqa), fold(dgka)
        small["sw_q_norm"][l], small["sw_k_norm"][l] = fold(dgqs), fold(dgks)
        gwin = tn_matmul(dz, sv["hn"], 1.0, f"dwin_{l}")
        pending[f"mix_{l}"], token = scatter_start([[split(gwout)], [split(gwna), split(gwsw)], [split(gwin)]],
                                                   f"scatter_mix_{l}")
        dx, dg = proj_bwd_norm([dz], [win_t], sv["x1"], mix_norm[l][None], dx, token, f"mix_bwd_x_{l}")
        small["mix_norm"][l] = dg[0]
        dx, tail = ffn_backward(dx, blocks[1])

    dtab = t5_reduce(dbias_sw, bmap, "t5_reduce")
    small_parts = {k: jnp.stack(v) for k, v in small.items()}
    small_parts["t5_rel_table"] = jnp.transpose(dtab[:, :, 0])

    grads, delta, new_m, new_v = {}, {}, {}, {}
    state = {}
    chain = [tail]
    members = {"ffn": lambda t: [(f"ffn{t}_w_gate", 0, 0, True), (f"ffn{t}_w_up", 0, 1, True),
                                 (f"ffn{t}_w_down", 0, 2, False)],
               "mix": lambda t: [("w_out", 0, 0, False), ("w_branch_na", 1, 0, True), ("w_branch_sw", 1, 1, True),
                                 ("w_in", 2, 0, True)]}

    def collect(key):
        if key in two_level:
            zones = [chip_wait(pending[key], chain[0], f"wait_{key}")]
        else:
            zones = scatter_wait(pending[key], chain[0], f"wait_{key}")
        kind, l = key.split("_")
        for k, zi, wi, transposed in members[kind[:3]](kind[3:]):
            view = tr if transposed else (lambda t: t)
            state[k] = adamw_layer(zones[zi], wi, int(l), view(weights[k]), view(mom_m[k]), view(mom_v[k]),
                                   state.get(k), chain[0], f"adamw_{k}_{l}")
            chain[0] = state[k][1]
            if all(f"{kind}_{j}" in done for j in range(depth) if j != int(l)):
                grads[k], delta[k], new_m[k], new_v[k] = (view(t) for t in state[k])
        done.add(key)

    done = set()
    for key in pending:
        if key != last_key:
            collect(key)
    collect(last_key)
    recvs = share_small([small_parts[k] for k in SMALL_NAMES], chain[0])
    results = adamw_small([weights[k] for k in SMALL_NAMES], recvs, [mom_m[k] for k in SMALL_NAMES],
                          [mom_v[k] for k in SMALL_NAMES], "adamw_small")
    for dst, outs in zip((grads, delta, new_m, new_v), results):
        dst.update(dict(zip(SMALL_NAMES, outs)))

    return (loss, dx[None], *[grads[k] for k in order], *[delta[k] for k in order],
            *[new_m[k] for k in order], *[new_v[k] for k in order])
```

```python
import functools
import math

import numpy as np
import jax
import jax.numpy as jnp
from jax import lax
from jax.experimental import pallas as pl
from jax.experimental.pallas import tpu as pltpu

F32 = jnp.float32
BF16 = jnp.bfloat16
MESH = pl.DeviceIdType.MESH

N_DEV = 8
EPS = 1e-6
NEG = -1e30
HEAD_DIM = 64
GRID_W = 64
NA_ROWS = 8
NA_COLS = 16
NA_WIDTH = 512
SW_Q_WIDTH = 512
SW_KV_WIDTH = 128
SW_BLOCK = 128
SW_HEADS = 8
SW_REP = 4
REL_BUCKETS = 32
REL_MAX_DIST = 128
QKV_WIDTH = 3 * NA_WIDTH + SW_Q_WIDTH + 2 * SW_KV_WIDTH
SCALE = 1.0 / math.sqrt(HEAD_DIM)

ADAM_LR = 0.001
ADAM_B1 = 0.9
ADAM_B2 = 0.999
ADAM_EPS = 1e-08
ADAM_WD = 0.01
ADAM_STEP = 10

V7X_VMEM_LIMIT = 56 * 1024 * 1024
LANES = 128
MXU_TILE = 256

NT = (((1,), (1,)), ((), ()))
TN = (((0,), (0,)), ((), ()))


def _params(n_grid=1):
    return pltpu.CompilerParams(dimension_semantics=("arbitrary",) * n_grid,
                                vmem_limit_bytes=V7X_VMEM_LIMIT)


def _row_tile(s):
    for t in (512, 256, 128, 64, 32, 16, 8):
        if s % t == 0:
            return t
    raise ValueError(s)


def _tn_tile(n):
    best = max(t for t in range(LANES, min(n, 2304) + 1, LANES) if n % t == 0) if n % LANES == 0 else n
    return best // 2 if best == n and n >= 1024 else best


ONCE = pl.Buffered(1)


def _col_chunk(n):
    return MXU_TILE if n % MXU_TILE == 0 else n


def _dot(a, b):
    return jnp.dot(a, b, preferred_element_type=F32)


def _dotg(a, b, dn):
    return lax.dot_general(a, b, dn, preferred_element_type=F32)


def _sigmoid(v):
    return 1.0 / (1.0 + jnp.exp(-v))


def _rstd(xv):
    return lax.rsqrt(jnp.mean(xv * xv, axis=-1, keepdims=True) + EPS)


def _full(shape):
    nd = len(shape)
    return pl.BlockSpec(shape, lambda i, _n=nd: (0,) * _n)


def _rows(tm, width):
    return pl.BlockSpec((tm, width), lambda i: (i, 0))


def _mat(stack, idx):
    return pl.BlockSpec((None,) + tuple(stack.shape[1:]), lambda i, _w=idx: (_w, 0, 0), pipeline_mode=ONCE)


def _group_mean(v, bd):
    hi = v.astype(BF16)
    lo = (v - hi.astype(F32)).astype(BF16)
    return _dot(hi, bd) + _dot(lo, bd)


def ffn_up(x, gain, wg_t, wu_t, dep, name):
    s, d = x.shape
    f = wg_t[0].shape[1]
    tm = _row_tile(s)
    fc = _col_chunk(f)

    def body(x_ref, g_ref, wg_ref, wu_ref, dep_ref, xn_ref, dg_ref, du_ref, act_ref):
        xv = x_ref[...]
        xn = (xv * _rstd(xv) * g_ref[...]).astype(BF16)
        xn_ref[...] = xn
        for c0 in range(0, f, fc):
            hg = _dotg(xn, wg_ref[c0:c0 + fc, :], NT)
            hu = _dotg(xn, wu_ref[c0:c0 + fc, :], NT)
            sg = _sigmoid(hg)
            silu = hg * sg
            du_ref[:, c0:c0 + fc] = silu.astype(BF16)
            dg_ref[:, c0:c0 + fc] = (hu * (sg + silu * (1.0 - sg))).astype(BF16)
            act_ref[:, c0:c0 + fc] = (silu * hu).astype(BF16)

    return pl.pallas_call(
        body, name=name, grid=(s // tm,),
        in_specs=[_rows(tm, d), _full((1, d)), _mat(*wg_t), _mat(*wu_t), _full(dep.shape)],
        out_specs=[_rows(tm, d), _rows(tm, f), _rows(tm, f), _rows(tm, f)],
        out_shape=[jax.ShapeDtypeStruct((s, d), BF16)] + [jax.ShapeDtypeStruct((s, f), BF16)] * 3,
        compiler_params=_params(),
    )(x, gain, wg_t[0], wu_t[0], dep)


def ffn_down(x, act, wd, dep, name):
    s, d = x.shape
    f = act.shape[1]
    tm = _row_tile(s)

    def body(x_ref, a_ref, w_ref, dep_ref, o_ref):
        o_ref[...] = x_ref[...] + 0.5 * _dot(a_ref[...], w_ref[...])

    return pl.pallas_call(
        body, name=name, grid=(s // tm,),
        in_specs=[_rows(tm, d), _rows(tm, f), _mat(*wd), _full(dep.shape)],
        out_specs=_rows(tm, d),
        out_shape=jax.ShapeDtypeStruct((s, d), F32),
        compiler_params=_params(),
    )(x, act, wd[0], dep)


def mix_in(x, gain, win_t, b_gate, gq_na, gk_na, gq_sw, gk_sw, bd, name):
    s, d = x.shape
    tm = _row_tile(s)
    gc = _col_chunk(2 * d)

    def body(x_ref, g_ref, w_ref, b_ref, gqa_ref, gka_ref, gqs_ref, gks_ref, bd_ref,
             hn_ref, zq_ref, qa_ref, ka_ref, qs_ref, ks_ref, gt_ref):
        xv = x_ref[...]
        hn = (xv * _rstd(xv) * g_ref[...]).astype(BF16)
        hn_ref[...] = hn

        def proj(c0, c1):
            return _dotg(hn, w_ref[c0:c1, :], NT)

        def headnorm(z, g, bdm):
            return z * lax.rsqrt(_group_mean(z * z, bdm) + EPS) * g

        bd512 = bd_ref[...]
        bd128 = bd_ref[0:SW_KV_WIDTH, 0:SW_KV_WIDTH]
        z = proj(0, 512)
        zq_ref[:, 0:512] = z.astype(BF16)
        qa_ref[...] = (headnorm(z, gqa_ref[...], bd512) * SCALE).astype(BF16)
        z = proj(512, 1024)
        zq_ref[:, 512:1024] = z.astype(BF16)
        ka_ref[...] = headnorm(z, gka_ref[...], bd512).astype(BF16)
        z = proj(1024, 1536)
        zq_ref[:, 1024:1536] = z.astype(BF16)
        z = proj(1536, 2048)
        zq_ref[:, 1536:2048] = z.astype(BF16)
        qs_ref[...] = (headnorm(z, gqs_ref[...], bd512) * SCALE).astype(BF16)
        z = proj(2048, 2176)
        zq_ref[:, 2048:2176] = z.astype(BF16)
        ks_ref[...] = headnorm(z, gks_ref[...], bd128).astype(BF16)
        z = proj(2176, 2304)
        zq_ref[:, 2176:2304] = z.astype(BF16)
        for c0 in range(0, 2 * d, gc):
            zg = proj(QKV_WIDTH + c0, QKV_WIDTH + c0 + gc) + b_ref[:, c0:c0 + gc]
            gt_ref[:, c0:c0 + gc] = _sigmoid(zg).astype(BF16)

    return pl.pallas_call(
        body, name=name, grid=(s // tm,),
        in_specs=[_rows(tm, d), _full((1, d)), _mat(*win_t), _full((1, 2 * d)),
                  _full((1, 512)), _full((1, 512)), _full((1, 512)), _full((1, 128)), _full((512, 512))],
        out_specs=[_rows(tm, d), _rows(tm, QKV_WIDTH), _rows(tm, 512), _rows(tm, 512), _rows(tm, 512),
                   _rows(tm, 128), _rows(tm, 2 * d)],
        out_shape=[jax.ShapeDtypeStruct((s, d), BF16), jax.ShapeDtypeStruct((s, QKV_WIDTH), BF16),
                   jax.ShapeDtypeStruct((s, 512), BF16), jax.ShapeDtypeStruct((s, 512), BF16),
                   jax.ShapeDtypeStruct((s, 512), BF16), jax.ShapeDtypeStruct((s, 128), BF16),
                   jax.ShapeDtypeStruct((s, 2 * d), BF16)],
        compiler_params=_params(),
    )(x, gain, win_t[0], b_gate, gq_na, gk_na, gq_sw, gk_sw, bd)


def _na_iotas():
    qc = lax.broadcasted_iota(jnp.int32, (GRID_W, LANES), 0)
    ln = lax.broadcasted_iota(jnp.int32, (GRID_W, LANES), 1)
    low = ln < GRID_W
    kc = jnp.where(low, ln, ln - GRID_W)
    diff = kc - qc + (NA_COLS - 1)
    qcs = jnp.clip(qc - NA_COLS // 2, 0, GRID_W - NA_COLS)
    inwin = (kc >= qcs) & (kc < qcs + NA_COLS)
    return diff, low, inwin


NA_RI = 2 * NA_ROWS - 1
NA_CI = 2 * NA_COLS - 1
NA_T2 = NA_RI + 1


def _rpb_rows(rpb):
    h = rpb.shape[0]
    padded = jnp.pad(rpb, ((0, 0), (1, 1), (0, GRID_W - NA_CI)))
    return jnp.concatenate([padded[:, :NA_T2], padded[:, 1:NA_T2 + 1]], axis=2).reshape(h, NA_T2, LANES)


def _rpb_from_rows(rows):
    return rows[:, 1:, :NA_CI] + rows[:, :NA_RI, GRID_W:GRID_W + NA_CI]


def rpb_expand(rows, dep, name):
    n_heads = rows.shape[0]

    def body(r_ref, dep_ref, o_ref):
        for h in range(n_heads):
            for e in range(NA_T2):
                line = jnp.broadcast_to(r_ref[h, e:e + 1, :], (GRID_W, LANES))
                o_ref[h, e] = pltpu.roll(line, LANES - (NA_COLS - 1), 1, stride=1, stride_axis=0)

    return pl.pallas_call(
        body, name=name,
        in_specs=[pl.BlockSpec(memory_space=pltpu.VMEM), pl.BlockSpec(memory_space=pltpu.VMEM)],
        out_specs=pl.BlockSpec(memory_space=pltpu.VMEM),
        out_shape=jax.ShapeDtypeStruct((n_heads, NA_T2, GRID_W, LANES), F32),
        compiler_params=pltpu.CompilerParams(vmem_limit_bytes=V7X_VMEM_LIMIT),
    )(rows, dep)


def rpb_reduce(dt2, name):
    n_heads = dt2.shape[0]
    flip = jnp.asarray(np.eye(GRID_W)[::-1], BF16)

    def body(d_ref, j_ref, o_ref):
        jm = j_ref[...]
        for h in range(n_heads):
            for e in range(NA_T2):
                dv = d_ref[h, e]
                hi = dv.astype(BF16)
                mid = (dv - hi.astype(F32)).astype(BF16)
                lo = (dv - hi.astype(F32) - mid.astype(F32)).astype(BF16)
                rev = _dot(jm, hi) + _dot(jm, mid) + _dot(jm, lo)
                back = pltpu.roll(rev, LANES + (NA_COLS - 1) - (GRID_W - 1), 1, stride=1, stride_axis=0)
                o_ref[h, e:e + 1, :] = jnp.sum(back, axis=0, keepdims=True)

    return pl.pallas_call(
        body, name=name,
        in_specs=[pl.BlockSpec(memory_space=pltpu.VMEM)] * 2,
        out_specs=pl.BlockSpec(memory_space=pltpu.VMEM),
        out_shape=jax.ShapeDtypeStruct((n_heads, NA_T2, LANES), F32),
        compiler_params=pltpu.CompilerParams(vmem_limit_bytes=V7X_VMEM_LIMIT),
    )(dt2, flip)


NA_TQ = 4
NA_TK = NA_TQ + NA_ROWS
NA_KCH = NA_TK // 2


def _na_tile_geometry(t, rows):
    r = t * NA_TQ
    kbase = jnp.clip(r - NA_ROWS // 2, 0, rows - NA_TK)
    starts = [jnp.clip(r + a - NA_ROWS // 2, 0, rows - NA_ROWS) for a in range(NA_TQ)]
    return r, kbase, starts


def _na_tile_mask(kbase, starts, low, inwin):
    half = jnp.where(low, 0, 1)
    cols = []
    for c in range(NA_KCH):
        krow = kbase + 2 * c + half
        cols.append(jnp.concatenate(
            [jnp.where(inwin & (krow >= st) & (krow < st + NA_ROWS), 0.0, NEG) for st in starts], axis=0))
    return jnp.concatenate(cols, axis=1)


def _na_tile_index(r, kbase, a, c):
    return jnp.clip(kbase + 2 * c - (r + a) + NA_ROWS, 0, NA_T2 - 1)


def _na_tile_scores(q, k, t2_ref, hh, r, kbase, madd):
    bias = jnp.concatenate(
        [jnp.concatenate([t2_ref[hh, _na_tile_index(r, kbase, a, c)] for a in range(NA_TQ)], axis=0)
         for c in range(NA_KCH)], axis=1)
    return _dotg(q, k, NT) + bias + madd


def _softmax_rows(sc):
    e = jnp.exp(sc - jnp.max(sc, axis=1, keepdims=True))
    return e * (1.0 / jnp.sum(e, axis=1, keepdims=True))


def na_fwd(qa, ka, zq, t2, name):
    s = qa.shape[0]
    rows = s // GRID_W
    n_pairs = NA_WIDTH // LANES
    v_blk0 = (2 * NA_WIDTH) // LANES

    assert rows % NA_TQ == 0 and rows >= NA_TK
    tq, tk = NA_TQ * GRID_W, NA_TK * GRID_W

    def body(q_ref, k_ref, v_ref, t2_ref, o_ref, s_scr, p_scr):
        _, low, inwin = _na_iotas()

        def tile(t, carry):
            r, kbase, starts = _na_tile_geometry(t, rows)
            madd = _na_tile_mask(kbase, starts, low, inwin)
            qr = pl.ds(pl.multiple_of(r * GRID_W, tq), tq)
            kr = pl.ds(pl.multiple_of(kbase * GRID_W, tq), tk)
            for hh in range(2):
                lanes = slice(HEAD_DIM * hh, HEAD_DIM * (hh + 1))
                s_scr[tq * hh:tq * (hh + 1), :] = _na_tile_scores(q_ref[qr, lanes], k_ref[kr, lanes], t2_ref, hh, r,
                                                                  kbase, madd)
            p_scr[...] = _softmax_rows(s_scr[...]).astype(BF16)
            for hh in range(2):
                lanes = slice(HEAD_DIM * hh, HEAD_DIM * (hh + 1))
                o_ref[qr, lanes] = _dot(p_scr[tq * hh:tq * (hh + 1), :], v_ref[kr, lanes]).astype(BF16)
            return carry

        lax.fori_loop(0, rows // NA_TQ, tile, 0)

    col = lambda off: pl.BlockSpec((s, LANES), lambda p, _o=off: (0, _o + p))
    return pl.pallas_call(
        body, name=name, grid=(n_pairs,),
        in_specs=[col(0), col(0), col(v_blk0),
                  pl.BlockSpec((2, NA_T2, GRID_W, LANES), lambda p: (p, 0, 0, 0))],
        out_specs=col(0),
        out_shape=jax.ShapeDtypeStruct((s, NA_WIDTH), BF16),
        scratch_shapes=[pltpu.VMEM((2 * tq, tk), F32), pltpu.VMEM((2 * tq, tk), BF16)],
        compiler_params=_params(),
    )(qa, ka, zq, t2)


def na_bwd(qa, ka, zq, t2, o_na, do_na, name):
    s = qa.shape[0]
    rows = s // GRID_W
    n_pairs = NA_WIDTH // LANES
    v_blk0 = (2 * NA_WIDTH) // LANES

    tq, tk = NA_TQ * GRID_W, NA_TK * GRID_W

    def body(q_ref, k_ref, v_ref, t2_ref, o_ref, do_ref, dq_ref, dk_ref, dv_ref, dt2_ref):
        _, low, inwin = _na_iotas()
        dk_ref[...] = jnp.zeros(dk_ref.shape, F32)
        dv_ref[...] = jnp.zeros(dv_ref.shape, F32)
        dt2_ref[...] = jnp.zeros(dt2_ref.shape, F32)

        def tile(t, carry):
            r, kbase, starts = _na_tile_geometry(t, rows)
            madd = _na_tile_mask(kbase, starts, low, inwin)
            qr = pl.ds(pl.multiple_of(r * GRID_W, tq), tq)
            kr = pl.ds(pl.multiple_of(kbase * GRID_W, tq), tk)
            for hh in range(2):
                lanes = slice(HEAD_DIM * hh, HEAD_DIM * (hh + 1))
                q, k, v = q_ref[qr, lanes], k_ref[kr, lanes], v_ref[kr, lanes]
                p = _softmax_rows(_na_tile_scores(q, k, t2_ref, hh, r, kbase, madd))
                do = do_ref[qr, lanes]
                delta = jnp.sum(do.astype(F32) * o_ref[qr, lanes].astype(F32), axis=1, keepdims=True)
                ds = p * (_dotg(do, v, NT) - delta)
                for a in range(NA_TQ):
                    for c in range(NA_KCH):
                        e = _na_tile_index(r, kbase, a, c)
                        dt2_ref[hh, e] = dt2_ref[hh, e] + ds[GRID_W * a:GRID_W * (a + 1), LANES * c:LANES * (c + 1)]
                dsb = ds.astype(BF16)
                dq_ref[qr, lanes] = _dot(dsb, k)
                dk_ref[kr, lanes] = dk_ref[kr, lanes] + _dotg(dsb, q, TN)
                dv_ref[kr, lanes] = dv_ref[kr, lanes] + _dotg(p.astype(BF16), do, TN)
            return carry

        lax.fori_loop(0, rows // NA_TQ, tile, 0)

    col = lambda off: pl.BlockSpec((s, LANES), lambda p, _o=off: (0, _o + p))
    t2spec = pl.BlockSpec((2, NA_T2, GRID_W, LANES), lambda p: (p, 0, 0, 0))
    return pl.pallas_call(
        body, name=name, grid=(n_pairs,),
        in_specs=[col(0), col(0), col(v_blk0), t2spec, col(0), col(0)],
        out_specs=[col(0), col(0), col(0), t2spec],
        out_shape=[jax.ShapeDtypeStruct((s, NA_WIDTH), F32)] * 3 + [jax.ShapeDtypeStruct(t2.shape, F32)],
        compiler_params=_params(),
    )(qa, ka, zq, t2, o_na, do_na)


def _t5_bucket_map():
    rel = np.arange(3 * SW_BLOCK)[None, :] - SW_BLOCK - np.arange(SW_BLOCK)[:, None]
    nb = REL_BUCKETS // 2
    max_exact = nb // 2
    n = np.abs(rel)
    large = max_exact + (np.log(np.maximum(n, 1) / max_exact)
                         / np.log(REL_MAX_DIST / max_exact) * (nb - max_exact)).astype(np.int32)
    large = np.minimum(large, nb - 1)
    return ((rel > 0) * nb + np.where(n < max_exact, n, large)).astype(np.int32)


def t5_expand(table, bmap, dep, name):
    def body(tab_ref, bm_ref, dep_ref, o_ref):
        bm = bm_ref[...]
        for h in range(SW_HEADS):
            t = jnp.zeros(bm.shape, F32)
            for b in range(REL_BUCKETS):
                t = jnp.where(bm == b, tab_ref[b, h], t)
            o_ref[h] = t

    return pl.pallas_call(
        body, name=name,
        in_specs=[pl.BlockSpec(memory_space=pltpu.SMEM), pl.BlockSpec(memory_space=pltpu.VMEM),
                  pl.BlockSpec(memory_space=pltpu.VMEM)],
        out_specs=pl.BlockSpec(memory_space=pltpu.VMEM),
        out_shape=jax.ShapeDtypeStruct((SW_HEADS,) + bmap.shape, F32),
        compiler_params=pltpu.CompilerParams(vmem_limit_bytes=V7X_VMEM_LIMIT),
    )(table, bmap, dep)


def t5_reduce(dbias_list, bmap, name):
    n = len(dbias_list)

    def body(*refs):
        d_refs, bm_ref, o_ref = refs[:n], refs[n], refs[n + 1]
        bm = bm_ref[...]
        for h in range(SW_HEADS):
            dv = d_refs[0][h]
            for other in d_refs[1:]:
                dv = dv + other[h]
            rows = [jnp.sum(jnp.where(bm == b, dv, 0.0), axis=0, keepdims=True) for b in range(REL_BUCKETS)]
            r = jnp.concatenate(rows, axis=0)
            o_ref[h] = jnp.broadcast_to(jnp.sum(r, axis=1, keepdims=True), (REL_BUCKETS, LANES))

    return pl.pallas_call(
        body, name=name,
        in_specs=[pl.BlockSpec(memory_space=pltpu.VMEM)] * (n + 1),
        out_specs=pl.BlockSpec(memory_space=pltpu.VMEM),
        out_shape=jax.ShapeDtypeStruct((SW_HEADS, REL_BUCKETS, LANES), F32),
        compiler_params=pltpu.CompilerParams(vmem_limit_bytes=V7X_VMEM_LIMIT),
    )(*dbias_list, bmap)


def _sw_mask_iotas():
    a = lax.broadcasted_iota(jnp.int32, (SW_BLOCK, 3 * SW_BLOCK), 0)
    j = lax.broadcasted_iota(jnp.int32, (SW_BLOCK, 3 * SW_BLOCK), 1)
    inwin = jnp.abs(j - SW_BLOCK - a) <= SW_BLOCK
    return j, inwin


SW_STACK = SW_HEADS * SW_BLOCK


def _sw_softmax(sc, sk):
    m = jnp.maximum(jnp.max(sc, axis=1, keepdims=True), sk)
    e = jnp.exp(sc - m)
    es = jnp.exp(sk - m)
    inv = 1.0 / (jnp.sum(e, axis=1, keepdims=True) + es)
    return e * inv, es * inv


def _sw_prologue(k_ref, v_ref, kp, vp, sink_ref, s):
    pad = s + 2 * SW_BLOCK
    zeros = jnp.zeros((SW_BLOCK, SW_KV_WIDTH), BF16)
    kp[0:SW_BLOCK, :] = zeros
    vp[0:SW_BLOCK, :] = zeros
    kp[SW_BLOCK + s:pad, :] = zeros
    vp[SW_BLOCK + s:pad, :] = zeros
    kp[SW_BLOCK:SW_BLOCK + s, :] = k_ref[...]
    vp[SW_BLOCK:SW_BLOCK + s, :] = v_ref[...]
    return jnp.concatenate([jnp.full((SW_BLOCK, 1), sink_ref[h], F32) for h in range(SW_HEADS)], axis=0)


def sw_fwd(qs, ks, zq, t5b, sink, dep, name):
    s = qs.shape[0]
    nb = s // SW_BLOCK
    v_blk = (3 * NA_WIDTH + SW_Q_WIDTH + SW_KV_WIDTH) // LANES
    pad = s + 2 * SW_BLOCK

    def body(q_ref, k_ref, v_ref, b_ref, sink_ref, dep_ref, o_ref, kp, vp, s_scr, p_scr):
        sink_col = _sw_prologue(k_ref, v_ref, kp, vp, sink_ref, s)
        j, inwin = _sw_mask_iotas()

        def blk(n, carry):
            kpos = n * SW_BLOCK - SW_BLOCK + j
            madd = jnp.where(inwin & (kpos >= 0) & (kpos < s), 0.0, NEG)
            q0 = pl.multiple_of(n * SW_BLOCK, SW_BLOCK)
            qr, kr = pl.ds(q0, SW_BLOCK), pl.ds(q0, 3 * SW_BLOCK)
            for h in range(SW_HEADS):
                g = h // SW_REP
                s_scr[SW_BLOCK * h:SW_BLOCK * (h + 1), :] = _dotg(
                    q_ref[qr, HEAD_DIM * h:HEAD_DIM * (h + 1)], kp[kr, HEAD_DIM * g:HEAD_DIM * (g + 1)], NT) + madd
            p, _ = _sw_softmax(s_scr[...] + b_ref[...], sink_col)
            p_scr[...] = p.astype(BF16)
            for h in range(SW_HEADS):
                g = h // SW_REP
                o_ref[qr, HEAD_DIM * h:HEAD_DIM * (h + 1)] = _dot(
                    p_scr[SW_BLOCK * h:SW_BLOCK * (h + 1), :], vp[kr, HEAD_DIM * g:HEAD_DIM * (g + 1)]).astype(BF16)
            return carry

        lax.fori_loop(0, nb, blk, 0)

    return pl.pallas_call(
        body, name=name, grid=(1,),
        in_specs=[_full((s, SW_Q_WIDTH)), _full((s, SW_KV_WIDTH)),
                  pl.BlockSpec((s, SW_KV_WIDTH), lambda i: (0, v_blk)),
                  _full((SW_STACK, 3 * SW_BLOCK)), pl.BlockSpec(memory_space=pltpu.SMEM),
                  _full(dep.shape)],
        out_specs=_full((s, SW_Q_WIDTH)),
        out_shape=jax.ShapeDtypeStruct((s, SW_Q_WIDTH), BF16),
        scratch_shapes=[pltpu.VMEM((pad, SW_KV_WIDTH), BF16), pltpu.VMEM((pad, SW_KV_WIDTH), BF16),
                        pltpu.VMEM((SW_STACK, 3 * SW_BLOCK), F32), pltpu.VMEM((SW_STACK, 3 * SW_BLOCK), BF16)],
        compiler_params=_params(),
    )(qs, ks, zq, t5b, sink, dep)


def sw_bwd(qs, ks, zq, t5b, sink, o_sw, do_sw, name):
    s = qs.shape[0]
    nb = s // SW_BLOCK
    v_blk = (3 * NA_WIDTH + SW_Q_WIDTH + SW_KV_WIDTH) // LANES
    pad = s + 2 * SW_BLOCK

    def body(q_ref, k_ref, v_ref, b_ref, sink_ref, o_ref, do_ref,
             dq_ref, dk_ref, dv_ref, db_ref, dsk_ref, kp, vp, dkp, dvp, s_scr, dp_scr, ds_scr, p_scr):
        sink_col = _sw_prologue(k_ref, v_ref, kp, vp, sink_ref, s)
        dkp[...] = jnp.zeros(dkp.shape, F32)
        dvp[...] = jnp.zeros(dvp.shape, F32)
        db_ref[...] = jnp.zeros(db_ref.shape, F32)
        dsk_ref[...] = jnp.zeros(dsk_ref.shape, F32)
        j, inwin = _sw_mask_iotas()

        def blk(n, carry):
            kpos = n * SW_BLOCK - SW_BLOCK + j
            madd = jnp.where(inwin & (kpos >= 0) & (kpos < s), 0.0, NEG)
            q0 = pl.multiple_of(n * SW_BLOCK, SW_BLOCK)
            qr, kr = pl.ds(q0, SW_BLOCK), pl.ds(q0, 3 * SW_BLOCK)
            deltas = []
            for h in range(SW_HEADS):
                g = h // SW_REP
                hl, kl = slice(HEAD_DIM * h, HEAD_DIM * (h + 1)), slice(HEAD_DIM * g, HEAD_DIM * (g + 1))
                rows = slice(SW_BLOCK * h, SW_BLOCK * (h + 1))
                do = do_ref[qr, hl]
                s_scr[rows, :] = _dotg(q_ref[qr, hl], kp[kr, kl], NT) + madd
                dp_scr[rows, :] = _dotg(do, vp[kr, kl], NT)
                deltas.append(jnp.sum(do.astype(F32) * o_ref[qr, hl].astype(F32), axis=1, keepdims=True))
            delta = jnp.concatenate(deltas, axis=0)
            p, ps = _sw_softmax(s_scr[...] + b_ref[...], sink_col)
            ds = p * (dp_scr[...] - delta)
            db_ref[...] = db_ref[...] + ds
            dsk_ref[...] = dsk_ref[...] - jnp.broadcast_to(ps * delta, (SW_STACK, LANES))
            ds_scr[...] = ds.astype(BF16)
            p_scr[...] = p.astype(BF16)
            for g in range(SW_HEADS // SW_REP):
                kl = slice(HEAD_DIM * g, HEAD_DIM * (g + 1))
                k = kp[kr, kl]
                dkw = jnp.zeros((3 * SW_BLOCK, HEAD_DIM), F32)
                dvw = jnp.zeros((3 * SW_BLOCK, HEAD_DIM), F32)
                for r in range(SW_REP):
                    h = g * SW_REP + r
                    hl, rows = slice(HEAD_DIM * h, HEAD_DIM * (h + 1)), slice(SW_BLOCK * h, SW_BLOCK * (h + 1))
                    dsb = ds_scr[rows, :]
                    dq_ref[qr, hl] = _dot(dsb, k)
                    dkw = dkw + _dotg(dsb, q_ref[qr, hl], TN)
                    dvw = dvw + _dotg(p_scr[rows, :], do_ref[qr, hl], TN)
                dkp[kr, kl] = dkp[kr, kl] + dkw
                dvp[kr, kl] = dvp[kr, kl] + dvw
            return carry

        lax.fori_loop(0, nb, blk, 0)
        dk_ref[...] = dkp[SW_BLOCK:SW_BLOCK + s, :]
        dv_ref[...] = dvp[SW_BLOCK:SW_BLOCK + s, :]

    bias_spec = _full((SW_STACK, 3 * SW_BLOCK))
    return pl.pallas_call(
        body, name=name, grid=(1,),
        in_specs=[_full((s, SW_Q_WIDTH)), _full((s, SW_KV_WIDTH)),
                  pl.BlockSpec((s, SW_KV_WIDTH), lambda i: (0, v_blk)),
                  bias_spec, pl.BlockSpec(memory_space=pltpu.SMEM),
                  _full((s, SW_Q_WIDTH)), _full((s, SW_Q_WIDTH))],
        out_specs=[_full((s, SW_Q_WIDTH)), _full((s, SW_KV_WIDTH)), _full((s, SW_KV_WIDTH)), bias_spec,
                   _full((SW_STACK, LANES))],
        out_shape=[jax.ShapeDtypeStruct((s, SW_Q_WIDTH), F32), jax.ShapeDtypeStruct((s, SW_KV_WIDTH), F32),
                   jax.ShapeDtypeStruct((s, SW_KV_WIDTH), F32),
                   jax.ShapeDtypeStruct((SW_STACK, 3 * SW_BLOCK), F32),
                   jax.ShapeDtypeStruct((SW_STACK, LANES), F32)],
        scratch_shapes=[pltpu.VMEM((pad, SW_KV_WIDTH), BF16), pltpu.VMEM((pad, SW_KV_WIDTH), BF16),
                        pltpu.VMEM((pad, SW_KV_WIDTH), F32), pltpu.VMEM((pad, SW_KV_WIDTH), F32),
                        pltpu.VMEM((SW_STACK, 3 * SW_BLOCK), F32), pltpu.VMEM((SW_STACK, 3 * SW_BLOCK), F32),
                        pltpu.VMEM((SW_STACK, 3 * SW_BLOCK), BF16), pltpu.VMEM((SW_STACK, 3 * SW_BLOCK), BF16)],
        compiler_params=_params(),
    )(qs, ks, zq, t5b, sink, o_sw, do_sw)


def merge_out(x, o_na, o_sw, gt, wbna_t, wbsw_t, wout, name):
    s, d = x.shape
    tm = _row_tile(s)

    def body(x_ref, ona_ref, osw_ref, gt_ref, wna_ref, wsw_ref, wo_ref, xo_ref, ana_ref, asw_ref, mg_ref):
        a_na = _dotg(ona_ref[...], wna_ref[...], NT)
        a_sw = _dotg(osw_ref[...], wsw_ref[...], NT)
        g_na, g_sw = gt_ref[:, 0:d].astype(F32), gt_ref[:, d:2 * d].astype(F32)
        ana_ref[...] = (a_na * g_na * (1.0 - g_na)).astype(BF16)
        asw_ref[...] = (a_sw * g_sw * (1.0 - g_sw)).astype(BF16)
        merged = (g_na * a_na + g_sw * a_sw).astype(BF16)
        mg_ref[...] = merged
        xo_ref[...] = x_ref[...] + _dot(merged, wo_ref[...])

    return pl.pallas_call(
        body, name=name, grid=(s // tm,),
        in_specs=[_rows(tm, d), _rows(tm, 512), _rows(tm, 512), _rows(tm, 2 * d),
                  _mat(*wbna_t), _mat(*wbsw_t), _mat(*wout)],
        out_specs=[_rows(tm, d)] * 4,
        out_shape=[jax.ShapeDtypeStruct((s, d), F32)] + [jax.ShapeDtypeStruct((s, d), BF16)] * 3,
        compiler_params=_params(),
    )(x, o_na, o_sw, gt, wbna_t[0], wbsw_t[0], wout[0])


def mix_bwd_out(dx, gt, a_na, a_sw, wbna_t, wbsw_t, wout, dep, name):
    s, d = dx.shape
    tm = _row_tile(s)

    def body(dx_ref, gt_ref, ana_ref, asw_ref, wna_ref, wsw_ref, wo_ref, dep_ref,
             dxb_ref, dzg_ref, dana_ref, dasw_ref, dona_ref, dosw_ref, dbg_ref):
        @pl.when(pl.program_id(0) == 0)
        def _():
            dbg_ref[...] = jnp.zeros(dbg_ref.shape, F32)

        dxb = dx_ref[...].astype(BF16)
        dxb_ref[...] = dxb
        dm = _dotg(dxb, wo_ref[...], NT)
        for i, (a_ref, da_ref, w_ref, do_ref) in enumerate(
                [(ana_ref, dana_ref, wna_ref, dona_ref), (asw_ref, dasw_ref, wsw_ref, dosw_ref)]):
            gi = gt_ref[:, i * d:(i + 1) * d].astype(F32)
            da = (dm * gi).astype(BF16)
            da_ref[...] = da
            do_ref[...] = _dot(da, w_ref[...]).astype(BF16)
            dzg = dm * a_ref[...].astype(F32)
            dzg_ref[:, i * d:(i + 1) * d] = dzg.astype(BF16)
            dbg_ref[:, i * d:(i + 1) * d] = dbg_ref[:, i * d:(i + 1) * d] + jnp.sum(dzg, axis=0, keepdims=True)

    return pl.pallas_call(
        body, name=name, grid=(s // tm,),
        in_specs=[_rows(tm, d), _rows(tm, 2 * d), _rows(tm, d), _rows(tm, d),
                  _mat(*wbna_t), _mat(*wbsw_t), _mat(*wout), _full(dep.shape)],
        out_specs=[_rows(tm, d), _rows(tm, 2 * d), _rows(tm, d), _rows(tm, d), _rows(tm, 512), _rows(tm, 512),
                   _full((1, 2 * d))],
        out_shape=[jax.ShapeDtypeStruct((s, d), BF16), jax.ShapeDtypeStruct((s, 2 * d), BF16),
                   jax.ShapeDtypeStruct((s, d), BF16), jax.ShapeDtypeStruct((s, d), BF16),
                   jax.ShapeDtypeStruct((s, 512), BF16), jax.ShapeDtypeStruct((s, 512), BF16),
                   jax.ShapeDtypeStruct((1, 2 * d), F32)],
        compiler_params=_params(),
    )(dx, gt, a_na, a_sw, wbna_t[0], wbsw_t[0], wout[0], dep)


def qk_norm_bwd(dqa, dka, dva, dqs, dks, dvs, zq, dzg, gq_na, gk_na, gq_sw, gk_sw, bd, name):
    s = zq.shape[0]
    d2 = dzg.shape[1]
    n_in = QKV_WIDTH + d2
    tm = _row_tile(s)

    def body(dqa_ref, dka_ref, dva_ref, dqs_ref, dks_ref, dvs_ref, zq_ref, dzg_ref,
             gqa_ref, gka_ref, gqs_ref, gks_ref, bd_ref, dz_ref, dgqa_ref, dgka_ref, dgqs_ref, dgks_ref):
        @pl.when(pl.program_id(0) == 0)
        def _():
            for r in (dgqa_ref, dgka_ref, dgqs_ref, dgks_ref):
                r[...] = jnp.zeros(r.shape, F32)

        bd512 = bd_ref[...]
        bd128 = bd_ref[0:SW_KV_WIDTH, 0:SW_KV_WIDTH]

        def one(c0, c1, dy_ref, g_ref, dg_ref, bdm, scale):
            z = zq_ref[:, c0:c1].astype(F32)
            r = lax.rsqrt(_group_mean(z * z, bdm) + EPS)
            zh = z * r
            dy = dy_ref[...] * scale
            dyg = dy * g_ref[...]
            dz = r * (dyg - zh * _group_mean(dyg * zh, bdm))
            dz_ref[:, c0:c1] = dz.astype(BF16)
            dg_ref[...] = dg_ref[...] + jnp.sum(dy * zh, axis=0, keepdims=True)

        one(0, 512, dqa_ref, gqa_ref, dgqa_ref, bd512, SCALE)
        one(512, 1024, dka_ref, gka_ref, dgka_ref, bd512, 1.0)
        dz_ref[:, 1024:1536] = dva_ref[...].astype(BF16)
        one(1536, 2048, dqs_ref, gqs_ref, dgqs_ref, bd512, SCALE)
        one(2048, 2176, dks_ref, gks_ref, dgks_ref, bd128, 1.0)
        dz_ref[:, 2176:2304] = dvs_ref[...].astype(BF16)
        dz_ref[:, QKV_WIDTH:n_in] = dzg_ref[...]

    return pl.pallas_call(
        body, name=name, grid=(s // tm,),
        in_specs=[_rows(tm, 512), _rows(tm, 512), _rows(tm, 512), _rows(tm, 512), _rows(tm, 128), _rows(tm, 128),
                  _rows(tm, QKV_WIDTH), _rows(tm, d2),
                  _full((1, 512)), _full((1, 512)), _full((1, 512)), _full((1, 128)), _full((512, 512))],
        out_specs=[_rows(tm, n_in), _full((1, 512)), _full((1, 512)), _full((1, 512)), _full((1, 128))],
        out_shape=[jax.ShapeDtypeStruct((s, n_in), BF16)] + [jax.ShapeDtypeStruct((1, 512), F32)] * 3
                  + [jax.ShapeDtypeStruct((1, 128), F32)],
        compiler_params=_params(),
    )(dqa, dka, dva, dqs, dks, dvs, zq, dzg, gq_na, gk_na, gq_sw, gk_sw, bd)


def ffn_bwd_act(dx, wd, hg, hu, name):
    s, d = dx.shape
    f = wd[0].shape[1]
    tm = _row_tile(s)
    fc = _col_chunk(f)

    def body(dx_ref, w_ref, hg_ref, hu_ref, dxb_ref, dhg_ref, dhu_ref):
        dxv = dx_ref[...]
        dxb_ref[...] = dxv.astype(BF16)
        half = (0.5 * dxv).astype(BF16)
        for c0 in range(0, f, fc):
            dact = _dotg(half, w_ref[c0:c0 + fc, :], NT)
            dhu_ref[:, c0:c0 + fc] = (dact * hu_ref[:, c0:c0 + fc].astype(F32)).astype(BF16)
            dhg_ref[:, c0:c0 + fc] = (dact * hg_ref[:, c0:c0 + fc].astype(F32)).astype(BF16)

    return pl.pallas_call(
        body, name=name, grid=(s // tm,),
        in_specs=[_rows(tm, d), _mat(*wd), _rows(tm, f), _rows(tm, f)],
        out_specs=[_rows(tm, d), _rows(tm, f), _rows(tm, f)],
        out_shape=[jax.ShapeDtypeStruct((s, d), BF16), jax.ShapeDtypeStruct((s, f), BF16),
                   jax.ShapeDtypeStruct((s, f), BF16)],
        compiler_params=_params(),
    )(dx, wd[0], hg, hu)


def proj_bwd_norm(acts, weights, x, gain, dx, dep, name):
    s, d = x.shape
    tm = _row_tile(s)
    n = len(acts)

    def body(*refs):
        a_refs, w_refs = refs[:n], refs[n:2 * n]
        x_ref, g_ref, dx_ref, _, o_ref, dg_ref = refs[2 * n:]

        @pl.when(pl.program_id(0) == 0)
        def _():
            dg_ref[...] = jnp.zeros(dg_ref.shape, F32)

        dxn = _dot(a_refs[0][...], w_refs[0][...])
        for a_ref, w_ref in zip(a_refs[1:], w_refs[1:]):
            dxn = dxn + _dot(a_ref[...], w_ref[...])
        xv = x_ref[...]
        r = _rstd(xv)
        xh = xv * r
        dxh = dxn * g_ref[...]
        o_ref[...] = dx_ref[...] + r * (dxh - xh * jnp.mean(dxh * xh, axis=-1, keepdims=True))
        dg_ref[...] = dg_ref[...] + jnp.sum(dxn * xh, axis=0, keepdims=True)

    return pl.pallas_call(
        body, name=name, grid=(s // tm,),
        in_specs=[_rows(tm, a.shape[1]) for a in acts] + [_mat(*w) for w in weights]
                 + [_rows(tm, d), _full((1, d)), _rows(tm, d), _full(dep.shape)],
        out_specs=[_rows(tm, d), _full((1, d))],
        out_shape=[jax.ShapeDtypeStruct((s, d), F32), jax.ShapeDtypeStruct((1, d), F32)],
        compiler_params=_params(),
    )(*acts, *[w[0] for w in weights], x, gain, dx, dep)


def tn_matmul(a, b, scale, name):
    s, n = a.shape
    k = b.shape[1]
    tn = _tn_tile(n)

    def body(a_ref, b_ref, o_ref):
        o_ref[...] = (scale * _dotg(a_ref[...], b_ref[...], TN)).astype(BF16)

    return pl.pallas_call(
        body, name=name, grid=(n // tn,),
        in_specs=[pl.BlockSpec((s, tn), lambda i: (0, i)),
                  pl.BlockSpec((s, k), lambda i: (0, 0), pipeline_mode=ONCE)],
        out_specs=pl.BlockSpec((tn, k), lambda i: (i, 0)),
        out_shape=jax.ShapeDtypeStruct((n, k), BF16),
        compiler_params=_params(),
    )(a, b)


def loss_grad(y, target, name):
    s, d = y.shape
    tm = _row_tile(s)

    def body(y_ref, t_ref, dy_ref, acc_ref):
        @pl.when(pl.program_id(0) == 0)
        def _():
            acc_ref[...] = jnp.zeros(acc_ref.shape, F32)

        err = y_ref[...] - t_ref[...]
        dy_ref[...] = err * (1.0 / d)
        e2 = err * err
        part = jnp.sum(e2.reshape(tm // 8, 8, d), axis=0)
        acc = part[:, 0:LANES]
        for c0 in range(LANES, d, LANES):
            acc = acc + part[:, c0:c0 + LANES]
        acc_ref[...] = acc_ref[...] + acc

    return pl.pallas_call(
        body, name=name, grid=(s // tm,),
        in_specs=[_rows(tm, d), _rows(tm, d)],
        out_specs=[_rows(tm, d), _full((8, LANES))],
        out_shape=[jax.ShapeDtypeStruct((s, d), F32), jax.ShapeDtypeStruct((8, LANES), F32)],
        compiler_params=_params(),
    )(y, target)


def _mesh_pos():
    return lax.axis_index("x"), lax.axis_index("y"), lax.axis_index("c")


def _peers():
    x, y, c = _mesh_pos()
    peers = []
    for rel in range(1, N_DEV):
        peers.append((1 - x if rel & 4 else x, 1 - y if rel & 2 else y, 1 - c if rel & 1 else c))
    return 4 * x + 2 * y + c, peers


HBM_SPEC = pl.BlockSpec(memory_space=pltpu.HBM)
SEM_SPEC = pl.BlockSpec(memory_space=pltpu.SEMAPHORE)


def _split_call(body, name, thru, n_sems, extra=(), with_token=True):
    hbm = lambda t: pltpu.with_memory_space_constraint(t, pltpu.HBM)
    effect = pltpu.CompilerParams(has_side_effects=pltpu.SideEffectType.DATAFLOW_SIDE_EFFECTING)
    nt = len(thru)
    thru_shapes = [pltpu.HBM(t.shape, t.dtype) for t in thru]
    if with_token:
        (after,) = extra
        outs = pl.pallas_call(
            body, name=name, in_specs=[HBM_SPEC] * nt + [pl.BlockSpec(memory_space=pl.ANY)],
            out_specs=[SEM_SPEC] * len(n_sems) + [HBM_SPEC] * nt + [pl.BlockSpec(memory_space=pltpu.VMEM)],
            out_shape=[pltpu.SemaphoreType.DMA((k,)) for k in n_sems] + thru_shapes
                      + [jax.ShapeDtypeStruct((8, LANES), F32)],
            input_output_aliases={i: len(n_sems) + i for i in range(nt)}, compiler_params=effect,
        )(*[hbm(t) for t in thru], after)
        return outs[:len(n_sems)], outs[len(n_sems):-1], outs[-1]
    return pl.pallas_call(
        body, name=name,
        in_specs=[HBM_SPEC] * nt + [SEM_SPEC] * len(n_sems) + [pl.BlockSpec(memory_space=pl.ANY)],
        out_specs=[HBM_SPEC] * nt, out_shape=thru_shapes,
        input_output_aliases={i: i for i in range(nt)}, compiler_params=effect,
    )(*thru, *extra)


def _gather_targets():
    x, y, c = _mesh_pos()
    return 4 * x + 2 * y + c, [(x, y, 1 - c), (1 - x, y, c), (x, 1 - y, c), (1 - x, 1 - y, c)]


def gather_start(shards, after, name):
    n = len(shards)
    zones = [lax.empty((w.shape[0], N_DEV) + w.shape[1:], w.dtype) for w in shards]

    def body(*refs):
        ins, zs = refs[:n], refs[n:2 * n]
        send_sems, recv_sems, local_sems = refs[2 * n + 1:2 * n + 4]
        token = refs[-1]
        me, targets = _gather_targets()
        for a in range(n):
            pltpu.make_async_copy(ins[a], zs[a].at[:, me], local_sems.at[a]).start()
            for k, to in enumerate(targets):
                pltpu.make_async_remote_copy(
                    src_ref=ins[a], dst_ref=zs[a].at[:, me], send_sem=send_sems.at[4 * a + k],
                    recv_sem=recv_sems.at[4 * a + k], device_id=to, device_id_type=MESH).start()
        token[...] = jnp.zeros(token.shape, F32)

    sems, thru, token = _split_call(body, name, list(shards) + zones, (4 * n, 4 * n, n), extra=(after,))
    return (sems, thru, n), token


def gather_wait(started, after, name):
    sems, thru, n = started

    def body(*refs):
        zs = refs[n:2 * n]
        send_sems, recv_sems, local_sems = refs[2 * n:2 * n + 3]
        _, targets = _gather_targets()
        for a in range(n):
            for k, to in enumerate(targets):
                cp = pltpu.make_async_remote_copy(
                    src_ref=zs[a].at[:, 0], dst_ref=zs[a].at[:, 0], send_sem=send_sems.at[4 * a + k],
                    recv_sem=recv_sems.at[4 * a + k], device_id=to, device_id_type=MESH)
                cp.wait_send()
                cp.wait_recv()
            pltpu.make_async_copy(zs[a].at[:, 0], zs[a].at[:, 0], local_sems.at[a]).wait()

    return _split_call(body, name, thru, (4 * n, 4 * n, n), extra=(*sems, after), with_token=False)[n:]


def forward_start(zones, after, name):
    n = len(zones)

    def body(*refs):
        zs = refs[:n]
        send_sems, recv_sems = refs[n + 1:n + 3]
        token = refs[-1]
        x, y, c = _mesh_pos()
        for a in range(n):
            for j, chip in enumerate([(1 - x, y), (x, 1 - y), (1 - x, 1 - y)]):
                blk = zs[a].at[:, 4 * chip[0] + 2 * chip[1] + c]
                pltpu.make_async_remote_copy(
                    src_ref=blk, dst_ref=blk, send_sem=send_sems.at[3 * a + j], recv_sem=recv_sems.at[3 * a + j],
                    device_id=(x, y, 1 - c), device_id_type=MESH).start()
        token[...] = jnp.zeros(token.shape, F32)

    sems, thru, token = _split_call(body, name, list(zones), (3 * n, 3 * n), extra=(after,))
    return (sems, thru, n), token


def forward_wait(started, after, name):
    sems, thru, n = started

    def body(*refs):
        zs = refs[:n]
        send_sems, recv_sems = refs[n:n + 2]
        x, y, c = _mesh_pos()
        for a in range(n):
            for j in range(3):
                cp = pltpu.make_async_remote_copy(
                    src_ref=zs[a].at[:, 0], dst_ref=zs[a].at[:, 0], send_sem=send_sems.at[3 * a + j],
                    recv_sem=recv_sems.at[3 * a + j], device_id=(x, y, 1 - c), device_id_type=MESH)
                cp.wait_send()
                cp.wait_recv()

    return _split_call(body, name, thru, (3 * n, 3 * n), extra=(*sems, after), with_token=False)


def scatter_start(groups, name):
    n = len(groups)
    flat = [g for grp in groups for g in grp]
    nf = len(flat)
    offs = np.cumsum([0] + [len(grp) for grp in groups])
    lands = [lax.empty((N_DEV, len(grp)) + grp[0].shape[1:], grp[0].dtype) for grp in groups]

    def body(*refs):
        ins, zones = refs[:nf], refs[nf:nf + n]
        send_sems, recv_sems, local_sems = refs[nf + n:nf + n + 3]
        token = refs[-1]
        me, peers = _peers()
        for a in range(n):
            for w in range(len(groups[a])):
                pltpu.make_async_copy(ins[offs[a] + w].at[me], zones[a].at[me, w], local_sems.at[a]).start()
        for k, peer in enumerate(peers):
            p_id = 4 * peer[0] + 2 * peer[1] + peer[2]
            for a in range(n):
                for w in range(len(groups[a])):
                    pltpu.make_async_remote_copy(
                        src_ref=ins[offs[a] + w].at[p_id], dst_ref=zones[a].at[me, w],
                        send_sem=send_sems.at[7 * a + k], recv_sem=recv_sems.at[7 * a + k],
                        device_id=peer, device_id_type=MESH).start()
        token[...] = jnp.zeros(token.shape, F32)

    hbm = lambda t: pltpu.with_memory_space_constraint(t, pltpu.HBM)
    outs = pl.pallas_call(
        body, name=name,
        in_specs=[HBM_SPEC] * (nf + n),
        out_specs=[SEM_SPEC] * 3 + [HBM_SPEC] * (nf + n) + [pl.BlockSpec(memory_space=pltpu.VMEM)],
        out_shape=[pltpu.SemaphoreType.DMA((7 * n,)), pltpu.SemaphoreType.DMA((7 * n,)), pltpu.SemaphoreType.DMA((n,))]
                  + [pltpu.HBM(t.shape, t.dtype) for t in flat + lands]
                  + [jax.ShapeDtypeStruct((8, LANES), F32)],
        input_output_aliases={i: 3 + i for i in range(nf + n)},
        compiler_params=pltpu.CompilerParams(has_side_effects=pltpu.SideEffectType.DATAFLOW_SIDE_EFFECTING),
    )(*[hbm(t) for t in flat], *[hbm(t) for t in lands])
    sems, thru, token = outs[:3], outs[3:3 + nf + n], outs[-1]
    return (sems, thru, [len(grp) for grp in groups]), token


def scatter_wait(started, after, name):
    (send_sems, recv_sems, local_sems), thru, sizes = started
    n = len(sizes)
    nf = len(thru) - n

    def body(*refs):
        zones = refs[nf:nf + n]
        s_sems, r_sems, l_sems = refs[nf + n:nf + n + 3]
        me, peers = _peers()
        for a in range(n):
            for k, peer in enumerate(peers):
                cp = pltpu.make_async_remote_copy(
                    src_ref=zones[a].at[0], dst_ref=zones[a].at[0],
                    send_sem=s_sems.at[7 * a + k], recv_sem=r_sems.at[7 * a + k], device_id=peer,
                    device_id_type=MESH)
                cp.wait_send()
                cp.wait_recv()
            pltpu.make_async_copy(zones[a].at[0], zones[a].at[0], l_sems.at[a]).wait()

    outs = pl.pallas_call(
        body, name=name,
        in_specs=[HBM_SPEC] * (nf + n) + [SEM_SPEC] * 3 + [pl.BlockSpec(memory_space=pl.ANY)],
        out_specs=[HBM_SPEC] * (nf + n),
        out_shape=[pltpu.HBM(t.shape, t.dtype) for t in thru],
        input_output_aliases={i: i for i in range(nf + n)},
        compiler_params=pltpu.CompilerParams(has_side_effects=pltpu.SideEffectType.DATAFLOW_SIDE_EFFECTING),
    )(*thru, send_sems, recv_sems, local_sems, after)
    return outs[nf:]


def pair_start(grads, after, name):
    nw = len(grads)
    land = lax.empty((4, nw) + grads[0].shape[1:], grads[0].dtype)

    def body(*refs):
        ins, zone = refs[:nw], refs[nw]
        send_sems, recv_sems = refs[nw + 2:nw + 4]
        x, y, c = _mesh_pos()
        for j in range(4):
            for w in range(nw):
                pltpu.make_async_remote_copy(
                    src_ref=ins[w].at[2 * j + (1 - c)], dst_ref=zone.at[j, w], send_sem=send_sems.at[0],
                    recv_sem=recv_sems.at[0], device_id=(x, y, 1 - c), device_id_type=MESH).start()
        refs[-1][...] = jnp.zeros(refs[-1].shape, F32)

    sems, thru, token = _split_call(body, name, list(grads) + [land], (1, 1), extra=(after,))
    return (sems, thru, nw), token


def pair_wait(started, after, name):
    sems, thru, nw = started

    def body(*refs):
        zone = refs[nw]
        send_sems, recv_sems = refs[nw + 1:nw + 3]
        x, y, c = _mesh_pos()
        cp = pltpu.make_async_remote_copy(src_ref=zone, dst_ref=zone, send_sem=send_sems.at[0],
                                          recv_sem=recv_sems.at[0], device_id=(x, y, 1 - c), device_id_type=MESH)
        cp.wait_send()
        cp.wait_recv()

    outs = _split_call(body, name, thru, (1, 1), extra=(*sems, after), with_token=False)
    return outs[:nw], outs[nw]


def pair_sum(grads, land, name):
    nw = len(grads)
    _, r, c_dim = grads[0].shape

    def body(*refs):
        g_refs, l_ref, o_ref = refs[:nw], refs[nw], refs[nw + 1]
        core = lax.axis_index("c")
        for w in range(nw):
            o_ref[0, w] = (g_refs[w][0, core].astype(F32) + l_ref[0, w].astype(F32)).astype(BF16)

    return pl.pallas_call(
        body, name=name, grid=(4,),
        in_specs=[pl.BlockSpec((1, 2, r, c_dim), lambda j: (j, 0, 0, 0))] * nw
                 + [pl.BlockSpec((1, nw, r, c_dim), lambda j: (j, 0, 0, 0))],
        out_specs=pl.BlockSpec((1, nw, r, c_dim), lambda j: (j, 0, 0, 0)),
        out_shape=jax.ShapeDtypeStruct((4, nw, r, c_dim), BF16),
        compiler_params=_params(),
    )(*[g.reshape(4, 2, r, c_dim) for g in grads], land)


def _other_chips():
    x, y, c = _mesh_pos()
    chips = []
    for rel in range(1, 4):
        px, py = (1 - x if rel & 2 else x), (1 - y if rel & 1 else y)
        chips.append((px, py, 2 * px + py))
    return 2 * x + y, c, chips


def chip_start(pair_sums, after, name):
    land = lax.empty(pair_sums.shape, pair_sums.dtype)

    def body(*refs):
        h_ref, zone = refs[0], refs[1]
        send_sems, recv_sems, local_sem = refs[3:6]
        mine, c, chips = _other_chips()
        pltpu.make_async_copy(h_ref.at[mine], zone.at[mine], local_sem.at[0]).start()
        for k, (px, py, j) in enumerate(chips):
            pltpu.make_async_remote_copy(
                src_ref=h_ref.at[j], dst_ref=zone.at[mine], send_sem=send_sems.at[k], recv_sem=recv_sems.at[k],
                device_id=(px, py, c), device_id_type=MESH).start()
        refs[-1][...] = jnp.zeros(refs[-1].shape, F32)

    sems, thru, token = _split_call(body, name, [pair_sums, land], (3, 3, 1), extra=(after,))
    return (sems, thru), token


def chip_wait(started, after, name):
    sems, thru = started

    def body(*refs):
        zone = refs[1]
        send_sems, recv_sems, local_sem = refs[2:5]
        _, c, chips = _other_chips()
        for k, (px, py, _) in enumerate(chips):
            cp = pltpu.make_async_remote_copy(
                src_ref=zone.at[0], dst_ref=zone.at[0], send_sem=send_sems.at[k], recv_sem=recv_sems.at[k],
                device_id=(px, py, c), device_id_type=MESH)
            cp.wait_send()
            cp.wait_recv()
        pltpu.make_async_copy(zone.at[0], zone.at[0], local_sem.at[0]).wait()

    return _split_call(body, name, thru, (3, 3, 1), extra=(*sems, after), with_token=False)[1]


def share_small(parts, after):
    n = len(parts)

    def body(*refs):
        ins, outs = refs[:n], refs[n + 1:2 * n + 1]
        send_sems, recv_sems, local_sems = refs[2 * n + 1:]
        me, peers = _peers()
        copies = []
        for i in range(n):
            copies.append(pltpu.make_async_copy(ins[i], outs[i].at[me], local_sems.at[i]))
            copies += [pltpu.make_async_remote_copy(
                src_ref=ins[i], dst_ref=outs[i].at[me], send_sem=send_sems.at[7 * i + k],
                recv_sem=recv_sems.at[7 * i + k], device_id=peer, device_id_type=MESH)
                for k, peer in enumerate(peers)]
        for cp in copies:
            cp.start()
        for cp in copies:
            cp.wait()

    vm = pl.BlockSpec(memory_space=pltpu.VMEM)
    return pl.pallas_call(
        body, name="share_small", in_specs=[vm] * n + [pl.BlockSpec(memory_space=pl.ANY)], out_specs=[vm] * n,
        out_shape=[jax.ShapeDtypeStruct((N_DEV,) + p.shape, p.dtype) for p in parts],
        scratch_shapes=[pltpu.SemaphoreType.DMA((7 * n,)), pltpu.SemaphoreType.DMA((7 * n,)),
                        pltpu.SemaphoreType.DMA((n,))],
    )(*parts, after)


def _adamw_math(w, g, m, v):
    m = ADAM_B1 * m + (1.0 - ADAM_B1) * g
    v = ADAM_B2 * v + (1.0 - ADAM_B2) * (g * g)
    m_hat = m / (1.0 - ADAM_B1 ** ADAM_STEP)
    v_hat = v / (1.0 - ADAM_B2 ** ADAM_STEP)
    delta = -ADAM_LR * (m_hat / (jnp.sqrt(v_hat) + ADAM_EPS) + ADAM_WD * w)
    return delta, m, v


def adamw_layer(zone, w_idx, layer, w, m, v, prev, after, name):
    n_src, _, r, c = zone.shape
    depth = w.shape[0]
    if prev is None:
        prev = tuple(lax.empty((depth, r, c), F32) for _ in range(4))
    tr = r // 2 if r % 16 == 0 else r

    def body(z_ref, w_ref, m_ref, v_ref, *rest):
        g_ref, d_ref, mo_ref, vo_ref = rest[5:]
        g = z_ref[0].astype(F32)
        for src in range(1, n_src):
            g = g + z_ref[src].astype(F32)
        g_ref[...] = g
        d_ref[...], mo_ref[...], vo_ref[...] = _adamw_math(w_ref[...], g, m_ref[...], v_ref[...])

    rows = pl.BlockSpec((None, tr, c), lambda i: (layer, i, 0))
    anywhere = pl.BlockSpec(memory_space=pl.ANY)
    return pl.pallas_call(
        body, name=name, grid=(r // tr,),
        in_specs=[pl.BlockSpec((n_src, None, tr, c), lambda i: (0, w_idx, i, 0)), rows, rows, rows]
                 + [anywhere] * 5,
        out_specs=[rows] * 4,
        out_shape=[jax.ShapeDtypeStruct((depth, r, c), F32)] * 4,
        input_output_aliases={4 + k: k for k in range(4)},
        compiler_params=_params(),
    )(zone, w, m, v, *prev, after)


def adamw_small(ws, recvs, ms, vs, name):
    n = len(ws)

    def body(*refs):
        w_refs, r_refs, m_refs, v_refs = (refs[i * n:(i + 1) * n] for i in range(4))
        g_refs, d_refs, mo_refs, vo_refs = (refs[(4 + i) * n:(5 + i) * n] for i in range(4))
        for i in range(n):
            g = r_refs[i][0]
            for src in range(1, N_DEV):
                g = g + r_refs[i][src]
            g_refs[i][...] = g
            d_refs[i][...], mo_refs[i][...], vo_refs[i][...] = _adamw_math(w_refs[i][...], g, m_refs[i][...],
                                                                            v_refs[i][...])

    vm = pl.BlockSpec(memory_space=pltpu.VMEM)
    outs = pl.pallas_call(
        body, name=name, in_specs=[vm] * (4 * n), out_specs=[vm] * (4 * n),
        out_shape=[jax.ShapeDtypeStruct(w.shape, F32) for w in ws] * 4,
        compiler_params=pltpu.CompilerParams(vmem_limit_bytes=V7X_VMEM_LIMIT),
    )(*ws, *recvs, *ms, *vs)
    return [outs[i * n:(i + 1) * n] for i in range(4)]


SMALL_NAMES = ("ffn1_norm", "mix_norm", "ffn2_norm", "b_gate", "na_q_norm", "na_k_norm", "sw_q_norm", "sw_k_norm",
               "na_rpb", "sw_sink", "t5_rel_table")


def kernel(x, ffn1_norm, ffn1_w_gate, ffn1_w_up, ffn1_w_down, mix_norm, w_in, b_gate, na_q_norm, na_k_norm, na_rpb, sw_q_norm, sw_k_norm, sw_sink, t5_rel_table, w_branch_na, w_branch_sw, w_out, ffn2_norm, ffn2_w_gate, ffn2_w_up, ffn2_w_down, loss_target, m_ffn1_norm, m_ffn1_w_gate, m_ffn1_w_up, m_ffn1_w_down, m_mix_norm, m_w_in, m_b_gate, m_na_q_norm, m_na_k_norm, m_na_rpb, m_sw_q_norm, m_sw_k_norm, m_sw_sink, m_t5_rel_table, m_w_branch_na, m_w_branch_sw, m_w_out, m_ffn2_norm, m_ffn2_w_gate, m_ffn2_w_up, m_ffn2_w_down, v_ffn1_norm, v_ffn1_w_gate, v_ffn1_w_up, v_ffn1_w_down, v_mix_norm, v_w_in, v_b_gate, v_na_q_norm, v_na_k_norm, v_na_rpb, v_sw_q_norm, v_sw_k_norm, v_sw_sink, v_t5_rel_table, v_w_branch_na, v_w_branch_sw, v_w_out, v_ffn2_norm, v_ffn2_w_gate, v_ffn2_w_up, v_ffn2_w_down):
    weights = dict(ffn1_norm=ffn1_norm, ffn1_w_gate=ffn1_w_gate, ffn1_w_up=ffn1_w_up, ffn1_w_down=ffn1_w_down,
                   mix_norm=mix_norm, w_in=w_in, b_gate=b_gate, na_q_norm=na_q_norm, na_k_norm=na_k_norm,
                   na_rpb=na_rpb, sw_q_norm=sw_q_norm, sw_k_norm=sw_k_norm, sw_sink=sw_sink,
                   t5_rel_table=t5_rel_table, w_branch_na=w_branch_na, w_branch_sw=w_branch_sw, w_out=w_out,
                   ffn2_norm=ffn2_norm, ffn2_w_gate=ffn2_w_gate, ffn2_w_up=ffn2_w_up, ffn2_w_down=ffn2_w_down)
    mom_m = dict(ffn1_norm=m_ffn1_norm, ffn1_w_gate=m_ffn1_w_gate, ffn1_w_up=m_ffn1_w_up, ffn1_w_down=m_ffn1_w_down,
                 mix_norm=m_mix_norm, w_in=m_w_in, b_gate=m_b_gate, na_q_norm=m_na_q_norm, na_k_norm=m_na_k_norm,
                 na_rpb=m_na_rpb, sw_q_norm=m_sw_q_norm, sw_k_norm=m_sw_k_norm, sw_sink=m_sw_sink,
                 t5_rel_table=m_t5_rel_table, w_branch_na=m_w_branch_na, w_branch_sw=m_w_branch_sw, w_out=m_w_out,
                 ffn2_norm=m_ffn2_norm, ffn2_w_gate=m_ffn2_w_gate, ffn2_w_up=m_ffn2_w_up, ffn2_w_down=m_ffn2_w_down)
    mom_v = dict(ffn1_norm=v_ffn1_norm, ffn1_w_gate=v_ffn1_w_gate, ffn1_w_up=v_ffn1_w_up, ffn1_w_down=v_ffn1_w_down,
                 mix_norm=v_mix_norm, w_in=v_w_in, b_gate=v_b_gate, na_q_norm=v_na_q_norm, na_k_norm=v_na_k_norm,
                 na_rpb=v_na_rpb, sw_q_norm=v_sw_q_norm, sw_k_norm=v_sw_k_norm, sw_sink=v_sw_sink,
                 t5_rel_table=v_t5_rel_table, w_branch_na=v_w_branch_na, w_branch_sw=v_w_branch_sw, w_out=v_w_out,
                 ffn2_norm=v_ffn2_norm, ffn2_w_gate=v_ffn2_w_gate, ffn2_w_up=v_ffn2_w_up, ffn2_w_down=v_ffn2_w_down)
    order = list(weights)

    depth = ffn1_norm.shape[0]
    s, d = x.shape[1], x.shape[2]
    xs = x[0]
    tr = lambda w: jnp.swapaxes(w, -1, -2)

    merge = lambda t: t.reshape(t.shape[0], N_DEV * t.shape[2], t.shape[3])
    no_dep = jnp.zeros((8, LANES), F32)

    def shards_of(kind, l):
        stack = lambda *ws: jnp.stack(ws).astype(BF16)
        if kind == "ffn1":
            return [stack(tr(ffn1_w_gate[l]), tr(ffn1_w_up[l]), ffn1_w_down[l])]
        if kind == "win":
            return [stack(tr(w_in[l]))]
        return [stack(tr(ffn2_w_gate[l]), tr(ffn2_w_up[l]), ffn2_w_down[l]), stack(w_out[l]),
                stack(tr(w_branch_na[l]), tr(w_branch_sw[l]))]

    def start(kind, l, after):
        return gather_start(shards_of(kind, l), after, f"gather_{kind}_{l}")

    def arrive(started, kind, l, after):
        zones = gather_wait(started, after, f"gather_{kind}_{l}_wait")
        return forward_start(zones, no_dep, f"forward_{kind}_{l}")

    def finish(fwd, kind, l, after):
        return [merge(z) for z in forward_wait(fwd, after, f"forward_{kind}_{l}_wait")]

    bd = jnp.asarray(np.kron(np.eye(NA_WIDTH // HEAD_DIM), np.full((HEAD_DIM, HEAD_DIM), 1.0 / HEAD_DIM)), BF16)
    bmap = jnp.asarray(_t5_bucket_map())
    tile8 = lambda g: jnp.tile(g, NA_WIDTH // HEAD_DIM).reshape(1, NA_WIDTH)
    tile2 = lambda g: jnp.tile(g, SW_KV_WIDTH // HEAD_DIM).reshape(1, SW_KV_WIDTH)

    st_first, tok = start("ffn1", 0, no_dep)
    t5b = t5_expand(t5_rel_table, bmap, tok, "t5_expand").reshape(SW_STACK, 3 * SW_BLOCK)
    fwd, _ = arrive(st_first, "ffn1", 0, t5b)
    st_win, dep = start("win", 0, t5b)
    (first,) = finish(fwd, "ffn1", 0, dep)

    saved = []
    layer_w = {0: dict(wg1=(first, 0), wu1=(first, 1), wd1=(first, 2))}
    cur = xs
    for l in range(depth):
        sv = {}
        lw = layer_w[l]
        sv["x0"] = cur
        sv["xn1"], sv["hg1"], sv["hu1"], sv["act1"] = ffn_up(cur, ffn1_norm[l][None], lw["wg1"], lw["wu1"], dep,
                                                             f"ffn1_up_{l}")
        cur = ffn_down(cur, sv["act1"], lw["wd1"], no_dep, f"ffn1_down_{l}")
        sv["x1"] = cur
        fwd, _ = arrive(st_win, "win", l, cur)
        st_rest, tok = start("rest", l, cur)
        (zb,) = finish(fwd, "win", l, tok)
        lw["win"] = (zb, 0)
        sv["gains"] = (tile8(na_q_norm[l]), tile8(na_k_norm[l]), tile8(sw_q_norm[l]), tile2(sw_k_norm[l]))
        sv["hn"], sv["zq"], sv["qa"], sv["ka"], sv["qs"], sv["ks"], sv["gt"] = mix_in(
            cur, mix_norm[l][None], lw["win"], b_gate[l][None], *sv["gains"], bd, f"mix_in_{l}")
        sv["t2"] = rpb_expand(_rpb_rows(na_rpb[l]), no_dep, f"rpb_expand_{l}")
        sv["o_na"] = na_fwd(sv["qa"], sv["ka"], sv["zq"], sv["t2"], f"na_fwd_{l}")
        dep = no_dep
        if l + 1 < depth:
            st_ffn1, dep = start("ffn1", l + 1, sv["o_na"])
        sv["o_sw"] = sw_fwd(sv["qs"], sv["ks"], sv["zq"], t5b, sw_sink[l], dep, f"sw_fwd_{l}")
        fwd, tok = arrive(st_rest, "rest", l, sv["o_sw"])
        za, zc, zd = finish(fwd, "rest", l, tok)
        lw.update(wg2=(za, 0), wu2=(za, 1), wd2=(za, 2), wout=(zc, 0), wna=(zd, 0), wsw=(zd, 1))
        cur, sv["a_na"], sv["a_sw"], sv["merged"] = merge_out(
            cur, sv["o_na"], sv["o_sw"], sv["gt"], lw["wna"], lw["wsw"], lw["wout"], f"merge_out_{l}")
        sv["x2"] = cur
        dep = no_dep
        if l + 1 < depth:
            st_win, dep = start("win", l + 1, cur)
        sv["xn2"], sv["hg2"], sv["hu2"], sv["act2"] = ffn_up(cur, ffn2_norm[l][None], lw["wg2"], lw["wu2"], dep,
                                                             f"ffn2_up_{l}")
        dep = no_dep
        if l + 1 < depth:
            fwd, dep = arrive(st_ffn1, "ffn1", l + 1, sv["act2"])
        cur = ffn_down(cur, sv["act2"], lw["wd2"], dep, f"ffn2_down_{l}")
        dep = no_dep
        if l + 1 < depth:
            (za,) = finish(fwd, "ffn1", l + 1, cur)
            layer_w[l + 1] = dict(wg1=(za, 0), wu1=(za, 1), wd1=(za, 2))
        saved.append(sv)

    dx, loss_acc = loss_grad(cur, loss_target[0], "loss_grad")
    loss = lax.psum(jnp.sum(loss_acc) * (0.5 / d), ("x", "y", "c"))

    split = lambda t: t.reshape(N_DEV, t.shape[0] // N_DEV, t.shape[1])
    pending = {}
    last_key = "ffn1_0"
    two_level = {last_key}
    small = {k: [None] * depth for k in SMALL_NAMES if k != "t5_rel_table"}
    dbias_sw = []
    for l in reversed(range(depth)):
        sv = saved[l]
        lw = layer_w[l]
        wg1, wu1, wd1, wg2, wu2, wd2 = (lw[k] for k in ("wg1", "wu1", "wd1", "wg2", "wu2", "wd2"))
        win_t, wout_l, wna_t, wsw_t = lw["win"], lw["wout"], lw["wna"], lw["wsw"]
        blocks = ((2, "x2", "xn2", "hg2", "hu2", "act2", wg2, wu2, wd2, "ffn2_norm", 3),
                  (1, "x0", "xn1", "hg1", "hu1", "act1", wg1, wu1, wd1, "ffn1_norm", 0))

        def ffn_backward(dx, blk):
            tag, xk, xnk, hgk, huk, actk, wg, wu, wd, norm_name, slot = blk
            gains = weights[norm_name]
            dxb, dhg, dhu = ffn_bwd_act(dx, wd, sv[hgk], sv[huk], f"ffn{tag}_bwd_act_{l}")
            gwd = tn_matmul(sv[actk], dxb, 0.5, f"ffn{tag}_dwd_{l}")
            gwg = tn_matmul(dhg, sv[xnk], 1.0, f"ffn{tag}_dwg_{l}")
            gwu = tn_matmul(dhu, sv[xnk], 1.0, f"ffn{tag}_dwu_{l}")
            key = f"ffn{tag}_{l}"
            blocks_of = [split(gwg), split(gwu), split(gwd)]
            if key in two_level:
                paired, token = pair_start(blocks_of, dxb, f"pair_{key}")
            else:
                pending[key], token = scatter_start([blocks_of], f"scatter_{key}")
            dx, dg = proj_bwd_norm([dhg, dhu], [wg, wu], sv[xk], gains[l][None], dx, token, f"ffn{tag}_bwd_x_{l}")
            token = no_dep
            if key in two_level:
                thru, land = pair_wait(paired, dx, f"pair_{key}_wait")
                pending[key], token = chip_start(pair_sum(thru, land, f"pair_sum_{key}"), dg, f"chips_{key}")
            small[norm_name][l] = dg[0]
            return dx, token

        dx, token = ffn_backward(dx, blocks[0])
        dxb, dzg, da_na, da_sw, do_na, do_sw, dbg = mix_bwd_out(
            dx, sv["gt"], sv["a_na"], sv["a_sw"], wna_t, wsw_t, wout_l, token, f"mix_bwd_out_{l}")
        small["b_gate"][l] = dbg[0]
        gwout = tn_matmul(sv["merged"], dxb, 1.0, f"dwout_{l}")
        gwna = tn_matmul(da_na, sv["o_na"], 1.0, f"dwna_{l}")
        gwsw = tn_matmul(da_sw, sv["o_sw"], 1.0, f"dwsw_{l}")
        dqa, dka, dva, dt2 = na_bwd(sv["qa"], sv["ka"], sv["zq"], sv["t2"], sv["o_na"], do_na, f"na_bwd_{l}")
        dqs, dks, dvs, dbias, dsink = sw_bwd(sv["qs"], sv["ks"], sv["zq"], t5b, sw_sink[l], sv["o_sw"], do_sw,
                                             f"sw_bwd_{l}")
        dbias_sw.append(dbias.reshape(SW_HEADS, SW_BLOCK, 3 * SW_BLOCK))
        small["sw_sink"][l] = jnp.sum(dsink[:, 0].reshape(SW_HEADS, SW_BLOCK), axis=1)
        small["na_rpb"][l] = _rpb_from_rows(rpb_reduce(dt2, f"rpb_reduce_{l}"))
        dz, dgqa, dgka, dgqs, dgks = qk_norm_bwd(dqa, dka, dva, dqs, dks, dvs, sv["zq"], dzg, *sv["gains"], bd,
                                                 f"qk_norm_bwd_{l}")
        fold = lambda g: jnp.sum(g.reshape(-1, HEAD_DIM), axis=0)
        small["na_q_norm"][l], small["na_k_norm"][l] = fold(dgqa), fold(dgka)
        small["sw_q_norm"][l], small["sw_k_norm"][l] = fold(dgqs), fold(dgks)
        gwin = tn_matmul(dz, sv["hn"], 1.0, f"dwin_{l}")
        pending[f"mix_{l}"], token = scatter_start([[split(gwout)], [split(gwna), split(gwsw)], [split(gwin)]],
                                                   f"scatter_mix_{l}")
        dx, dg = proj_bwd_norm([dz], [win_t], sv["x1"], mix_norm[l][None], dx, token, f"mix_bwd_x_{l}")
        small["mix_norm"][l] = dg[0]
        dx, tail = ffn_backward(dx, blocks[1])

    dtab = t5_reduce(dbias_sw, bmap, "t5_reduce")
    small_parts = {k: jnp.stack(v) for k, v in small.items()}
    small_parts["t5_rel_table"] = jnp.transpose(dtab[:, :, 0])

    grads, delta, new_m, new_v = {}, {}, {}, {}
    state = {}
    chain = [tail]
    members = {"ffn": lambda t: [(f"ffn{t}_w_gate", 0, 0, True), (f"ffn{t}_w_up", 0, 1, True),
                                 (f"ffn{t}_w_down", 0, 2, False)],
               "mix": lambda t: [("w_out", 0, 0, False), ("w_branch_na", 1, 0, True), ("w_branch_sw", 1, 1, True),
                                 ("w_in", 2, 0, True)]}

    def collect(key):
        if key in two_level:
            zones = [chip_wait(pending[key], chain[0], f"wait_{key}")]
        else:
            zones = scatter_wait(pending[key], chain[0], f"wait_{key}")
        kind, l = key.split("_")
        for k, zi, wi, transposed in members[kind[:3]](kind[3:]):
            view = tr if transposed else (lambda t: t)
            state[k] = adamw_layer(zones[zi], wi, int(l), view(weights[k]), view(mom_m[k]), view(mom_v[k]),
                                   state.get(k), chain[0], f"adamw_{k}_{l}")
            chain[0] = state[k][1]
            if all(f"{kind}_{j}" in done for j in range(depth) if j != int(l)):
                grads[k], delta[k], new_m[k], new_v[k] = (view(t) for t in state[k])
        done.add(key)

    done = set()
    for key in pending:
        if key != last_key:
            collect(key)
    collect(last_key)
    recvs = share_small([small_parts[k] for k in SMALL_NAMES], chain[0])
    results = adamw_small([weights[k] for k in SMALL_NAMES], recvs, [mom_m[k] for k in SMALL_NAMES],
                          [mom_v[k] for k in SMALL_NAMES], "adamw_small")
    for dst, outs in zip((grads, delta, new_m, new_v), results):
        dst.update(dict(zip(SMALL_NAMES, outs)))

    return (loss, dx[None], *[grads[k] for k in order], *[delta[k] for k in order],
            *[new_m[k] for k in order], *[new_v[k] for k in order])
```

```python
import functools
import math

import numpy as np
import jax
import jax.numpy as jnp
from jax import lax
from jax.experimental import pallas as pl
from jax.experimental.pallas import tpu as pltpu

F32 = jnp.float32
BF16 = jnp.bfloat16
MESH = pl.DeviceIdType.MESH

N_DEV = 8
EPS = 1e-6
NEG = -1e30
HEAD_DIM = 64
GRID_W = 64
NA_ROWS = 8
NA_COLS = 16
NA_WIDTH = 512
SW_Q_WIDTH = 512
SW_KV_WIDTH = 128
SW_BLOCK = 128
SW_HEADS = 8
SW_REP = 4
REL_BUCKETS = 32
REL_MAX_DIST = 128
QKV_WIDTH = 3 * NA_WIDTH + SW_Q_WIDTH + 2 * SW_KV_WIDTH
SCALE = 1.0 / math.sqrt(HEAD_DIM)

ADAM_LR = 0.001
ADAM_B1 = 0.9
ADAM_B2 = 0.999
ADAM_EPS = 1e-08
ADAM_WD = 0.01
ADAM_STEP = 10

V7X_VMEM_LIMIT = 56 * 1024 * 1024
LANES = 128
MXU_TILE = 256

NT = (((1,), (1,)), ((), ()))
TN = (((0,), (0,)), ((), ()))


def _params(n_grid=1):
    return pltpu.CompilerParams(dimension_semantics=("arbitrary",) * n_grid,
                                vmem_limit_bytes=V7X_VMEM_LIMIT)


def _row_tile(s):
    for t in (512, 256, 128, 64, 32, 16, 8):
        if s % t == 0:
            return t
    raise ValueError(s)


def _tn_tile(n):
    best = max(t for t in range(LANES, min(n, 2304) + 1, LANES) if n % t == 0) if n % LANES == 0 else n
    return best // 2 if best == n and n >= 1024 else best


ONCE = pl.Buffered(1)


def _col_chunk(n):
    return MXU_TILE if n % MXU_TILE == 0 else n


def _dot(a, b):
    return jnp.dot(a, b, preferred_element_type=F32)


def _dotg(a, b, dn):
    return lax.dot_general(a, b, dn, preferred_element_type=F32)


def _sigmoid(v):
    return 1.0 / (1.0 + jnp.exp(-v))


def _rstd(xv):
    return lax.rsqrt(jnp.mean(xv * xv, axis=-1, keepdims=True) + EPS)


def _full(shape):
    nd = len(shape)
    return pl.BlockSpec(shape, lambda i, _n=nd: (0,) * _n)


def _rows(tm, width):
    return pl.BlockSpec((tm, width), lambda i: (i, 0))


def _mat(stack, idx):
    return pl.BlockSpec((None,) + tuple(stack.shape[1:]), lambda i, _w=idx: (_w, 0, 0), pipeline_mode=ONCE)


def _group_mean(v, bd):
    hi = v.astype(BF16)
    lo = (v - hi.astype(F32)).astype(BF16)
    return _dot(hi, bd) + _dot(lo, bd)


def _swiglu_tile(xn, wg_ref, wu_ref, dg_ref, du_ref, act_ref, fc):
    for c0 in range(0, wg_ref.shape[0], fc):
        hg = _dotg(xn, wg_ref[c0:c0 + fc, :], NT)
        hu = _dotg(xn, wu_ref[c0:c0 + fc, :], NT)
        sg = _sigmoid(hg)
        silu = hg * sg
        du_ref[:, c0:c0 + fc] = silu.astype(BF16)
        dg_ref[:, c0:c0 + fc] = (hu * (sg + silu * (1.0 - sg))).astype(BF16)
        act_ref[:, c0:c0 + fc] = (silu * hu).astype(BF16)


def ffn_up(x, gain, wg_t, wu_t, dep, name):
    s, d = x.shape
    f = wg_t[0].shape[1]
    tm = _row_tile(s)
    fc = _col_chunk(f)

    def body(x_ref, g_ref, wg_ref, wu_ref, dep_ref, xn_ref, dg_ref, du_ref, act_ref):
        xv = x_ref[...]
        xn = (xv * _rstd(xv) * g_ref[...]).astype(BF16)
        xn_ref[...] = xn
        _swiglu_tile(xn, wg_ref, wu_ref, dg_ref, du_ref, act_ref, fc)

    return pl.pallas_call(
        body, name=name, grid=(s // tm,),
        in_specs=[_rows(tm, d), _full((1, d)), _mat(*wg_t), _mat(*wu_t), _full(dep.shape)],
        out_specs=[_rows(tm, d), _rows(tm, f), _rows(tm, f), _rows(tm, f)],
        out_shape=[jax.ShapeDtypeStruct((s, d), BF16)] + [jax.ShapeDtypeStruct((s, f), BF16)] * 3,
        compiler_params=_params(),
    )(x, gain, wg_t[0], wu_t[0], dep)


def ffn_both(x, gain, wg_t, wu_t, wd, dep, name):
    s, d = x.shape
    f = wg_t[0].shape[1]
    tm = min(_row_tile(s), 256)
    fc = _col_chunk(f)

    def body(x_ref, g_ref, wg_ref, wu_ref, wd_ref, dep_ref, xo_ref, xn_ref, dg_ref, du_ref, act_ref):
        xv = x_ref[...]
        xn = (xv * _rstd(xv) * g_ref[...]).astype(BF16)
        xn_ref[...] = xn
        _swiglu_tile(xn, wg_ref, wu_ref, dg_ref, du_ref, act_ref, fc)
        xo_ref[...] = xv + 0.5 * _dot(act_ref[...], wd_ref[...])

    return pl.pallas_call(
        body, name=name, grid=(s // tm,),
        in_specs=[_rows(tm, d), _full((1, d)), _mat(*wg_t), _mat(*wu_t), _mat(*wd), _full(dep.shape)],
        out_specs=[_rows(tm, d), _rows(tm, d), _rows(tm, f), _rows(tm, f), _rows(tm, f)],
        out_shape=[jax.ShapeDtypeStruct((s, d), F32), jax.ShapeDtypeStruct((s, d), BF16)]
                  + [jax.ShapeDtypeStruct((s, f), BF16)] * 3,
        compiler_params=_params(),
    )(x, gain, wg_t[0], wu_t[0], wd[0], dep)


def ffn_down(x, act, wd, dep, name, target=None):
    s, d = x.shape
    f = act.shape[1]
    tm = _row_tile(s)

    def body(x_ref, a_ref, w_ref, dep_ref, *rest):
        y = x_ref[...] + 0.5 * _dot(a_ref[...], w_ref[...])
        if target is None:
            rest[0][...] = y
            return
        t_ref, dy_ref, acc_ref = rest

        @pl.when(pl.program_id(0) == 0)
        def _():
            acc_ref[...] = jnp.zeros(acc_ref.shape, F32)

        err = y - t_ref[...]
        dy_ref[...] = err * (1.0 / d)
        part = jnp.sum((err * err).reshape(tm // 8, 8, d), axis=0)
        acc = part[:, 0:LANES]
        for c0 in range(LANES, d, LANES):
            acc = acc + part[:, c0:c0 + LANES]
        acc_ref[...] = acc_ref[...] + acc

    ins = [_rows(tm, d), _rows(tm, f), _mat(*wd), _full(dep.shape)]
    if target is None:
        return pl.pallas_call(
            body, name=name, grid=(s // tm,), in_specs=ins, out_specs=_rows(tm, d),
            out_shape=jax.ShapeDtypeStruct((s, d), F32), compiler_params=_params(),
        )(x, act, wd[0], dep)
    return pl.pallas_call(
        body, name=name, grid=(s // tm,), in_specs=ins + [_rows(tm, d)],
        out_specs=[_rows(tm, d), _full((8, LANES))],
        out_shape=[jax.ShapeDtypeStruct((s, d), F32), jax.ShapeDtypeStruct((8, LANES), F32)],
        compiler_params=_params(),
    )(x, act, wd[0], dep, target)


def mix_in(x, gain, win_t, b_gate, gq_na, gk_na, gq_sw, gk_sw, bd, name):
    s, d = x.shape
    tm = _row_tile(s)
    gc = _col_chunk(2 * d)

    def body(x_ref, g_ref, w_ref, b_ref, gqa_ref, gka_ref, gqs_ref, gks_ref, bd_ref,
             hn_ref, zq_ref, qa_ref, ka_ref, qs_ref, ks_ref, gt_ref):
        xv = x_ref[...]
        hn = (xv * _rstd(xv) * g_ref[...]).astype(BF16)
        hn_ref[...] = hn

        def proj(c0, c1):
            return _dotg(hn, w_ref[c0:c1, :], NT)

        def headnorm(z, g, bdm):
            return z * lax.rsqrt(_group_mean(z * z, bdm) + EPS) * g

        bd512 = bd_ref[...]
        bd128 = bd_ref[0:SW_KV_WIDTH, 0:SW_KV_WIDTH]
        z = proj(0, 512)
        zq_ref[:, 0:512] = z.astype(BF16)
        qa_ref[...] = (headnorm(z, gqa_ref[...], bd512) * SCALE).astype(BF16)
        z = proj(512, 1024)
        zq_ref[:, 512:1024] = z.astype(BF16)
        ka_ref[...] = headnorm(z, gka_ref[...], bd512).astype(BF16)
        z = proj(1024, 1536)
        zq_ref[:, 1024:1536] = z.astype(BF16)
        z = proj(1536, 2048)
        zq_ref[:, 1536:2048] = z.astype(BF16)
        qs_ref[...] = (headnorm(z, gqs_ref[...], bd512) * SCALE).astype(BF16)
        z = proj(2048, 2176)
        zq_ref[:, 2048:2176] = z.astype(BF16)
        ks_ref[...] = headnorm(z, gks_ref[...], bd128).astype(BF16)
        z = proj(2176, 2304)
        zq_ref[:, 2176:2304] = z.astype(BF16)
        for c0 in range(0, 2 * d, gc):
            zg = proj(QKV_WIDTH + c0, QKV_WIDTH + c0 + gc) + b_ref[:, c0:c0 + gc]
            gt_ref[:, c0:c0 + gc] = _sigmoid(zg).astype(BF16)

    return pl.pallas_call(
        body, name=name, grid=(s // tm,),
        in_specs=[_rows(tm, d), _full((1, d)), _mat(*win_t), _full((1, 2 * d)),
                  _full((1, 512)), _full((1, 512)), _full((1, 512)), _full((1, 128)), _full((512, 512))],
        out_specs=[_rows(tm, d), _rows(tm, QKV_WIDTH), _rows(tm, 512), _rows(tm, 512), _rows(tm, 512),
                   _rows(tm, 128), _rows(tm, 2 * d)],
        out_shape=[jax.ShapeDtypeStruct((s, d), BF16), jax.ShapeDtypeStruct((s, QKV_WIDTH), BF16),
                   jax.ShapeDtypeStruct((s, 512), BF16), jax.ShapeDtypeStruct((s, 512), BF16),
                   jax.ShapeDtypeStruct((s, 512), BF16), jax.ShapeDtypeStruct((s, 128), BF16),
                   jax.ShapeDtypeStruct((s, 2 * d), BF16)],
        compiler_params=_params(),
    )(x, gain, win_t[0], b_gate, gq_na, gk_na, gq_sw, gk_sw, bd)


def _na_iotas():
    qc = lax.broadcasted_iota(jnp.int32, (GRID_W, LANES), 0)
    ln = lax.broadcasted_iota(jnp.int32, (GRID_W, LANES), 1)
    low = ln < GRID_W
    kc = jnp.where(low, ln, ln - GRID_W)
    diff = kc - qc + (NA_COLS - 1)
    qcs = jnp.clip(qc - NA_COLS // 2, 0, GRID_W - NA_COLS)
    inwin = (kc >= qcs) & (kc < qcs + NA_COLS)
    return diff, low, inwin


NA_RI = 2 * NA_ROWS - 1
NA_CI = 2 * NA_COLS - 1
NA_T2 = NA_RI + 1


def _rpb_rows(rpb):
    h = rpb.shape[0]
    padded = jnp.pad(rpb, ((0, 0), (1, 1), (0, GRID_W - NA_CI)))
    return jnp.concatenate([padded[:, :NA_T2], padded[:, 1:NA_T2 + 1]], axis=2).reshape(h, NA_T2, LANES)


def _rpb_from_rows(rows):
    return rows[:, 1:, :NA_CI] + rows[:, :NA_RI, GRID_W:GRID_W + NA_CI]


def rpb_expand(rows, dep, name):
    n_heads = rows.shape[0]

    def body(r_ref, dep_ref, o_ref):
        for h in range(n_heads):
            for e in range(NA_T2):
                line = jnp.broadcast_to(r_ref[h, e:e + 1, :], (GRID_W, LANES))
                o_ref[h, e] = pltpu.roll(line, LANES - (NA_COLS - 1), 1, stride=1, stride_axis=0)

    return pl.pallas_call(
        body, name=name,
        in_specs=[pl.BlockSpec(memory_space=pltpu.VMEM), pl.BlockSpec(memory_space=pltpu.VMEM)],
        out_specs=pl.BlockSpec(memory_space=pltpu.VMEM),
        out_shape=jax.ShapeDtypeStruct((n_heads, NA_T2, GRID_W, LANES), F32),
        compiler_params=pltpu.CompilerParams(vmem_limit_bytes=V7X_VMEM_LIMIT),
    )(rows, dep)


def rpb_reduce(dt2, name):
    n_heads = dt2.shape[0]
    flip = jnp.asarray(np.eye(GRID_W)[::-1], BF16)

    def body(d_ref, j_ref, o_ref):
        jm = j_ref[...]
        for h in range(n_heads):
            for e in range(NA_T2):
                dv = d_ref[h, e]
                hi = dv.astype(BF16)
                mid = (dv - hi.astype(F32)).astype(BF16)
                lo = (dv - hi.astype(F32) - mid.astype(F32)).astype(BF16)
                rev = _dot(jm, hi) + _dot(jm, mid) + _dot(jm, lo)
                back = pltpu.roll(rev, LANES + (NA_COLS - 1) - (GRID_W - 1), 1, stride=1, stride_axis=0)
                o_ref[h, e:e + 1, :] = jnp.sum(back, axis=0, keepdims=True)

    return pl.pallas_call(
        body, name=name,
        in_specs=[pl.BlockSpec(memory_space=pltpu.VMEM)] * 2,
        out_specs=pl.BlockSpec(memory_space=pltpu.VMEM),
        out_shape=jax.ShapeDtypeStruct((n_heads, NA_T2, LANES), F32),
        compiler_params=pltpu.CompilerParams(vmem_limit_bytes=V7X_VMEM_LIMIT),
    )(dt2, flip)


NA_TQ = 4
NA_TK = NA_TQ + NA_ROWS
NA_KCH = NA_TK // 2


def _na_tile_geometry(t, rows):
    r = t * NA_TQ
    kbase = jnp.clip(r - NA_ROWS // 2, 0, rows - NA_TK)
    starts = [jnp.clip(r + a - NA_ROWS // 2, 0, rows - NA_ROWS) for a in range(NA_TQ)]
    return r, kbase, starts


def _na_tile_mask(kbase, starts, low, inwin):
    half = jnp.where(low, 0, 1)
    cols = []
    for c in range(NA_KCH):
        krow = kbase + 2 * c + half
        cols.append(jnp.concatenate(
            [jnp.where(inwin & (krow >= st) & (krow < st + NA_ROWS), 0.0, NEG) for st in starts], axis=0))
    return jnp.concatenate(cols, axis=1)


def _na_tile_index(r, kbase, a, c):
    return jnp.clip(kbase + 2 * c - (r + a) + NA_ROWS, 0, NA_T2 - 1)


def _na_tile_scores(q, k, t2_ref, hh, r, kbase, madd):
    bias = jnp.concatenate(
        [jnp.concatenate([t2_ref[hh, _na_tile_index(r, kbase, a, c)] for a in range(NA_TQ)], axis=0)
         for c in range(NA_KCH)], axis=1)
    return _dotg(q, k, NT) + bias + madd


def _softmax_rows(sc):
    e = jnp.exp(sc - jnp.max(sc, axis=1, keepdims=True))
    return e * (1.0 / jnp.sum(e, axis=1, keepdims=True))


def na_fwd(qa, ka, zq, t2, name):
    s = qa.shape[0]
    rows = s // GRID_W
    n_pairs = NA_WIDTH // LANES
    v_blk0 = (2 * NA_WIDTH) // LANES

    assert rows % NA_TQ == 0 and rows >= NA_TK
    tq, tk = NA_TQ * GRID_W, NA_TK * GRID_W

    def body(q_ref, k_ref, v_ref, t2_ref, o_ref, s_scr, p_scr):
        _, low, inwin = _na_iotas()

        def tile(t, carry):
            r, kbase, starts = _na_tile_geometry(t, rows)
            madd = _na_tile_mask(kbase, starts, low, inwin)
            qr = pl.ds(pl.multiple_of(r * GRID_W, tq), tq)
            kr = pl.ds(pl.multiple_of(kbase * GRID_W, tq), tk)
            for hh in range(2):
                lanes = slice(HEAD_DIM * hh, HEAD_DIM * (hh + 1))
                s_scr[tq * hh:tq * (hh + 1), :] = _na_tile_scores(q_ref[qr, lanes], k_ref[kr, lanes], t2_ref, hh, r,
                                                                  kbase, madd)
            p_scr[...] = _softmax_rows(s_scr[...]).astype(BF16)
            for hh in range(2):
                lanes = slice(HEAD_DIM * hh, HEAD_DIM * (hh + 1))
                o_ref[qr, lanes] = _dot(p_scr[tq * hh:tq * (hh + 1), :], v_ref[kr, lanes]).astype(BF16)
            return carry

        lax.fori_loop(0, rows // NA_TQ, tile, 0)

    col = lambda off: pl.BlockSpec((s, LANES), lambda p, _o=off: (0, _o + p))
    return pl.pallas_call(
        body, name=name, grid=(n_pairs,),
        in_specs=[col(0), col(0), col(v_blk0),
                  pl.BlockSpec((2, NA_T2, GRID_W, LANES), lambda p: (p, 0, 0, 0))],
        out_specs=col(0),
        out_shape=jax.ShapeDtypeStruct((s, NA_WIDTH), BF16),
        scratch_shapes=[pltpu.VMEM((2 * tq, tk), F32), pltpu.VMEM((2 * tq, tk), BF16)],
        compiler_params=_params(),
    )(qa, ka, zq, t2)


def na_bwd(qa, ka, zq, t2, o_na, do_na, name):
    s = qa.shape[0]
    rows = s // GRID_W
    n_pairs = NA_WIDTH // LANES
    v_blk0 = (2 * NA_WIDTH) // LANES

    tq, tk = NA_TQ * GRID_W, NA_TK * GRID_W

    def body(q_ref, k_ref, v_ref, t2_ref, o_ref, do_ref, dq_ref, dk_ref, dv_ref, dt2_ref):
        _, low, inwin = _na_iotas()
        dk_ref[...] = jnp.zeros(dk_ref.shape, F32)
        dv_ref[...] = jnp.zeros(dv_ref.shape, F32)
        dt2_ref[...] = jnp.zeros(dt2_ref.shape, F32)

        def tile(t, carry):
            r, kbase, starts = _na_tile_geometry(t, rows)
            madd = _na_tile_mask(kbase, starts, low, inwin)
            qr = pl.ds(pl.multiple_of(r * GRID_W, tq), tq)
            kr = pl.ds(pl.multiple_of(kbase * GRID_W, tq), tk)
            for hh in range(2):
                lanes = slice(HEAD_DIM * hh, HEAD_DIM * (hh + 1))
                q, k, v = q_ref[qr, lanes], k_ref[kr, lanes], v_ref[kr, lanes]
                p = _softmax_rows(_na_tile_scores(q, k, t2_ref, hh, r, kbase, madd))
                do = do_ref[qr, lanes]
                delta = jnp.sum(do.astype(F32) * o_ref[qr, lanes].astype(F32), axis=1, keepdims=True)
                ds = p * (_dotg(do, v, NT) - delta)
                for a in range(NA_TQ):
                    for c in range(NA_KCH):
                        e = _na_tile_index(r, kbase, a, c)
                        dt2_ref[hh, e] = dt2_ref[hh, e] + ds[GRID_W * a:GRID_W * (a + 1), LANES * c:LANES * (c + 1)]
                dsb = ds.astype(BF16)
                dq_ref[qr, lanes] = _dot(dsb, k)
                dk_ref[kr, lanes] = dk_ref[kr, lanes] + _dotg(dsb, q, TN)
                dv_ref[kr, lanes] = dv_ref[kr, lanes] + _dotg(p.astype(BF16), do, TN)
            return carry

        lax.fori_loop(0, rows // NA_TQ, tile, 0)

    col = lambda off: pl.BlockSpec((s, LANES), lambda p, _o=off: (0, _o + p))
    t2spec = pl.BlockSpec((2, NA_T2, GRID_W, LANES), lambda p: (p, 0, 0, 0))
    return pl.pallas_call(
        body, name=name, grid=(n_pairs,),
        in_specs=[col(0), col(0), col(v_blk0), t2spec, col(0), col(0)],
        out_specs=[col(0), col(0), col(0), t2spec],
        out_shape=[jax.ShapeDtypeStruct((s, NA_WIDTH), F32)] * 3 + [jax.ShapeDtypeStruct(t2.shape, F32)],
        compiler_params=_params(),
    )(qa, ka, zq, t2, o_na, do_na)


def _t5_bucket_map():
    rel = np.arange(3 * SW_BLOCK)[None, :] - SW_BLOCK - np.arange(SW_BLOCK)[:, None]
    nb = REL_BUCKETS // 2
    max_exact = nb // 2
    n = np.abs(rel)
    large = max_exact + (np.log(np.maximum(n, 1) / max_exact)
                         / np.log(REL_MAX_DIST / max_exact) * (nb - max_exact)).astype(np.int32)
    large = np.minimum(large, nb - 1)
    return ((rel > 0) * nb + np.where(n < max_exact, n, large)).astype(np.int32)


def t5_expand(table, bmap, dep, name):
    def body(tab_ref, bm_ref, dep_ref, o_ref):
        bm = bm_ref[...]
        for h in range(SW_HEADS):
            t = jnp.zeros(bm.shape, F32)
            for b in range(REL_BUCKETS):
                t = jnp.where(bm == b, tab_ref[b, h], t)
            o_ref[h] = t

    return pl.pallas_call(
        body, name=name,
        in_specs=[pl.BlockSpec(memory_space=pltpu.SMEM), pl.BlockSpec(memory_space=pltpu.VMEM),
                  pl.BlockSpec(memory_space=pltpu.VMEM)],
        out_specs=pl.BlockSpec(memory_space=pltpu.VMEM),
        out_shape=jax.ShapeDtypeStruct((SW_HEADS,) + bmap.shape, F32),
        compiler_params=pltpu.CompilerParams(vmem_limit_bytes=V7X_VMEM_LIMIT),
    )(table, bmap, dep)


def t5_reduce(dbias_list, bmap, name):
    n = len(dbias_list)

    def body(*refs):
        d_refs, bm_ref, o_ref = refs[:n], refs[n], refs[n + 1]
        bm = bm_ref[...]
        for h in range(SW_HEADS):
            dv = d_refs[0][h]
            for other in d_refs[1:]:
                dv = dv + other[h]
            rows = [jnp.sum(jnp.where(bm == b, dv, 0.0), axis=0, keepdims=True) for b in range(REL_BUCKETS)]
            r = jnp.concatenate(rows, axis=0)
            o_ref[h] = jnp.broadcast_to(jnp.sum(r, axis=1, keepdims=True), (REL_BUCKETS, LANES))

    return pl.pallas_call(
        body, name=name,
        in_specs=[pl.BlockSpec(memory_space=pltpu.VMEM)] * (n + 1),
        out_specs=pl.BlockSpec(memory_space=pltpu.VMEM),
        out_shape=jax.ShapeDtypeStruct((SW_HEADS, REL_BUCKETS, LANES), F32),
        compiler_params=pltpu.CompilerParams(vmem_limit_bytes=V7X_VMEM_LIMIT),
    )(*dbias_list, bmap)


def _sw_mask_iotas():
    a = lax.broadcasted_iota(jnp.int32, (SW_BLOCK, 3 * SW_BLOCK), 0)
    j = lax.broadcasted_iota(jnp.int32, (SW_BLOCK, 3 * SW_BLOCK), 1)
    inwin = jnp.abs(j - SW_BLOCK - a) <= SW_BLOCK
    return j, inwin


SW_STACK = SW_HEADS * SW_BLOCK


def _sw_softmax(sc, sk):
    m = jnp.maximum(jnp.max(sc, axis=1, keepdims=True), sk)
    e = jnp.exp(sc - m)
    es = jnp.exp(sk - m)
    inv = 1.0 / (jnp.sum(e, axis=1, keepdims=True) + es)
    return e * inv, es * inv


def _sw_prologue(k_ref, v_ref, kp, vp, sink_ref, s):
    pad = s + 2 * SW_BLOCK
    zeros = jnp.zeros((SW_BLOCK, SW_KV_WIDTH), BF16)
    kp[0:SW_BLOCK, :] = zeros
    vp[0:SW_BLOCK, :] = zeros
    kp[SW_BLOCK + s:pad, :] = zeros
    vp[SW_BLOCK + s:pad, :] = zeros
    kp[SW_BLOCK:SW_BLOCK + s, :] = k_ref[...]
    vp[SW_BLOCK:SW_BLOCK + s, :] = v_ref[...]
    return jnp.concatenate([jnp.full((SW_BLOCK, 1), sink_ref[h], F32) for h in range(SW_HEADS)], axis=0)


def sw_fwd(qs, ks, zq, t5b, sink, dep, name):
    s = qs.shape[0]
    nb = s // SW_BLOCK
    v_blk = (3 * NA_WIDTH + SW_Q_WIDTH + SW_KV_WIDTH) // LANES
    pad = s + 2 * SW_BLOCK

    def body(q_ref, k_ref, v_ref, b_ref, sink_ref, dep_ref, o_ref, kp, vp, s_scr, p_scr):
        sink_col = _sw_prologue(k_ref, v_ref, kp, vp, sink_ref, s)
        j, inwin = _sw_mask_iotas()

        def blk(n, carry):
            kpos = n * SW_BLOCK - SW_BLOCK + j
            madd = jnp.where(inwin & (kpos >= 0) & (kpos < s), 0.0, NEG)
            q0 = pl.multiple_of(n * SW_BLOCK, SW_BLOCK)
            qr, kr = pl.ds(q0, SW_BLOCK), pl.ds(q0, 3 * SW_BLOCK)
            for h in range(SW_HEADS):
                g = h // SW_REP
                s_scr[SW_BLOCK * h:SW_BLOCK * (h + 1), :] = _dotg(
                    q_ref[qr, HEAD_DIM * h:HEAD_DIM * (h + 1)], kp[kr, HEAD_DIM * g:HEAD_DIM * (g + 1)], NT) + madd
            p, _ = _sw_softmax(s_scr[...] + b_ref[...], sink_col)
            p_scr[...] = p.astype(BF16)
            for h in range(SW_HEADS):
                g = h // SW_REP
                o_ref[qr, HEAD_DIM * h:HEAD_DIM * (h + 1)] = _dot(
                    p_scr[SW_BLOCK * h:SW_BLOCK * (h + 1), :], vp[kr, HEAD_DIM * g:HEAD_DIM * (g + 1)]).astype(BF16)
            return carry

        lax.fori_loop(0, nb, blk, 0)

    return pl.pallas_call(
        body, name=name, grid=(1,),
        in_specs=[_full((s, SW_Q_WIDTH)), _full((s, SW_KV_WIDTH)),
                  pl.BlockSpec((s, SW_KV_WIDTH), lambda i: (0, v_blk)),
                  _full((SW_STACK, 3 * SW_BLOCK)), pl.BlockSpec(memory_space=pltpu.SMEM),
                  _full(dep.shape)],
        out_specs=_full((s, SW_Q_WIDTH)),
        out_shape=jax.ShapeDtypeStruct((s, SW_Q_WIDTH), BF16),
        scratch_shapes=[pltpu.VMEM((pad, SW_KV_WIDTH), BF16), pltpu.VMEM((pad, SW_KV_WIDTH), BF16),
                        pltpu.VMEM((SW_STACK, 3 * SW_BLOCK), F32), pltpu.VMEM((SW_STACK, 3 * SW_BLOCK), BF16)],
        compiler_params=_params(),
    )(qs, ks, zq, t5b, sink, dep)


def sw_bwd(qs, ks, zq, t5b, sink, o_sw, do_sw, name):
    s = qs.shape[0]
    nb = s // SW_BLOCK
    v_blk = (3 * NA_WIDTH + SW_Q_WIDTH + SW_KV_WIDTH) // LANES
    pad = s + 2 * SW_BLOCK

    def body(q_ref, k_ref, v_ref, b_ref, sink_ref, o_ref, do_ref,
             dq_ref, dk_ref, dv_ref, db_ref, dsk_ref, kp, vp, dkp, dvp, s_scr, dp_scr, ds_scr, p_scr):
        sink_col = _sw_prologue(k_ref, v_ref, kp, vp, sink_ref, s)
        dkp[...] = jnp.zeros(dkp.shape, F32)
        dvp[...] = jnp.zeros(dvp.shape, F32)
        db_ref[...] = jnp.zeros(db_ref.shape, F32)
        dsk_ref[...] = jnp.zeros(dsk_ref.shape, F32)
        j, inwin = _sw_mask_iotas()

        def blk(n, carry):
            kpos = n * SW_BLOCK - SW_BLOCK + j
            madd = jnp.where(inwin & (kpos >= 0) & (kpos < s), 0.0, NEG)
            q0 = pl.multiple_of(n * SW_BLOCK, SW_BLOCK)
            qr, kr = pl.ds(q0, SW_BLOCK), pl.ds(q0, 3 * SW_BLOCK)
            deltas = []
            for h in range(SW_HEADS):
                g = h // SW_REP
                hl, kl = slice(HEAD_DIM * h, HEAD_DIM * (h + 1)), slice(HEAD_DIM * g, HEAD_DIM * (g + 1))
                rows = slice(SW_BLOCK * h, SW_BLOCK * (h + 1))
                do = do_ref[qr, hl]
                s_scr[rows, :] = _dotg(q_ref[qr, hl], kp[kr, kl], NT) + madd
                dp_scr[rows, :] = _dotg(do, vp[kr, kl], NT)
                deltas.append(jnp.sum(do.astype(F32) * o_ref[qr, hl].astype(F32), axis=1, keepdims=True))
            delta = jnp.concatenate(deltas, axis=0)
            p, ps = _sw_softmax(s_scr[...] + b_ref[...], sink_col)
            ds = p * (dp_scr[...] - delta)
            db_ref[...] = db_ref[...] + ds
            dsk_ref[...] = dsk_ref[...] - jnp.broadcast_to(ps * delta, (SW_STACK, LANES))
            ds_scr[...] = ds.astype(BF16)
            p_scr[...] = p.astype(BF16)
            for g in range(SW_HEADS // SW_REP):
                kl = slice(HEAD_DIM * g, HEAD_DIM * (g + 1))
                k = kp[kr, kl]
                dkw = jnp.zeros((3 * SW_BLOCK, HEAD_DIM), F32)
                dvw = jnp.zeros((3 * SW_BLOCK, HEAD_DIM), F32)
                for r in range(SW_REP):
                    h = g * SW_REP + r
                    hl, rows = slice(HEAD_DIM * h, HEAD_DIM * (h + 1)), slice(SW_BLOCK * h, SW_BLOCK * (h + 1))
                    dsb = ds_scr[rows, :]
                    dq_ref[qr, hl] = _dot(dsb, k)
                    dkw = dkw + _dotg(dsb, q_ref[qr, hl], TN)
                    dvw = dvw + _dotg(p_scr[rows, :], do_ref[qr, hl], TN)
                dkp[kr, kl] = dkp[kr, kl] + dkw
                dvp[kr, kl] = dvp[kr, kl] + dvw
            return carry

        lax.fori_loop(0, nb, blk, 0)
        dk_ref[...] = dkp[SW_BLOCK:SW_BLOCK + s, :]
        dv_ref[...] = dvp[SW_BLOCK:SW_BLOCK + s, :]

    bias_spec = _full((SW_STACK, 3 * SW_BLOCK))
    return pl.pallas_call(
        body, name=name, grid=(1,),
        in_specs=[_full((s, SW_Q_WIDTH)), _full((s, SW_KV_WIDTH)),
                  pl.BlockSpec((s, SW_KV_WIDTH), lambda i: (0, v_blk)),
                  bias_spec, pl.BlockSpec(memory_space=pltpu.SMEM),
                  _full((s, SW_Q_WIDTH)), _full((s, SW_Q_WIDTH))],
        out_specs=[_full((s, SW_Q_WIDTH)), _full((s, SW_KV_WIDTH)), _full((s, SW_KV_WIDTH)), bias_spec,
                   _full((SW_STACK, LANES))],
        out_shape=[jax.ShapeDtypeStruct((s, SW_Q_WIDTH), F32), jax.ShapeDtypeStruct((s, SW_KV_WIDTH), F32),
                   jax.ShapeDtypeStruct((s, SW_KV_WIDTH), F32),
                   jax.ShapeDtypeStruct((SW_STACK, 3 * SW_BLOCK), F32),
                   jax.ShapeDtypeStruct((SW_STACK, LANES), F32)],
        scratch_shapes=[pltpu.VMEM((pad, SW_KV_WIDTH), BF16), pltpu.VMEM((pad, SW_KV_WIDTH), BF16),
                        pltpu.VMEM((pad, SW_KV_WIDTH), F32), pltpu.VMEM((pad, SW_KV_WIDTH), F32),
                        pltpu.VMEM((SW_STACK, 3 * SW_BLOCK), F32), pltpu.VMEM((SW_STACK, 3 * SW_BLOCK), F32),
                        pltpu.VMEM((SW_STACK, 3 * SW_BLOCK), BF16), pltpu.VMEM((SW_STACK, 3 * SW_BLOCK), BF16)],
        compiler_params=_params(),
    )(qs, ks, zq, t5b, sink, o_sw, do_sw)


def merge_out(x, o_na, o_sw, gt, wbna_t, wbsw_t, wout, name):
    s, d = x.shape
    tm = _row_tile(s)

    def body(x_ref, ona_ref, osw_ref, gt_ref, wna_ref, wsw_ref, wo_ref, xo_ref, ana_ref, asw_ref, mg_ref):
        a_na = _dotg(ona_ref[...], wna_ref[...], NT)
        a_sw = _dotg(osw_ref[...], wsw_ref[...], NT)
        g_na, g_sw = gt_ref[:, 0:d].astype(F32), gt_ref[:, d:2 * d].astype(F32)
        ana_ref[...] = (a_na * g_na * (1.0 - g_na)).astype(BF16)
        asw_ref[...] = (a_sw * g_sw * (1.0 - g_sw)).astype(BF16)
        merged = (g_na * a_na + g_sw * a_sw).astype(BF16)
        mg_ref[...] = merged
        xo_ref[...] = x_ref[...] + _dot(merged, wo_ref[...])

    return pl.pallas_call(
        body, name=name, grid=(s // tm,),
        in_specs=[_rows(tm, d), _rows(tm, 512), _rows(tm, 512), _rows(tm, 2 * d),
                  _mat(*wbna_t), _mat(*wbsw_t), _mat(*wout)],
        out_specs=[_rows(tm, d)] * 4,
        out_shape=[jax.ShapeDtypeStruct((s, d), F32)] + [jax.ShapeDtypeStruct((s, d), BF16)] * 3,
        compiler_params=_params(),
    )(x, o_na, o_sw, gt, wbna_t[0], wbsw_t[0], wout[0])


def mix_bwd_out(dx, gt, a_na, a_sw, wbna_t, wbsw_t, wout, dep, name):
    s, d = dx.shape
    tm = _row_tile(s)

    def body(dx_ref, gt_ref, ana_ref, asw_ref, wna_ref, wsw_ref, wo_ref, dep_ref,
             dxb_ref, dzg_ref, dana_ref, dasw_ref, dona_ref, dosw_ref, dbg_ref):
        @pl.when(pl.program_id(0) == 0)
        def _():
            dbg_ref[...] = jnp.zeros(dbg_ref.shape, F32)

        dxb = dx_ref[...].astype(BF16)
        dxb_ref[...] = dxb
        dm = _dotg(dxb, wo_ref[...], NT)
        for i, (a_ref, da_ref, w_ref, do_ref) in enumerate(
                [(ana_ref, dana_ref, wna_ref, dona_ref), (asw_ref, dasw_ref, wsw_ref, dosw_ref)]):
            gi = gt_ref[:, i * d:(i + 1) * d].astype(F32)
            da = (dm * gi).astype(BF16)
            da_ref[...] = da
            do_ref[...] = _dot(da, w_ref[...]).astype(BF16)
            dzg = dm * a_ref[...].astype(F32)
            dzg_ref[:, i * d:(i + 1) * d] = dzg.astype(BF16)
            dbg_ref[:, i * d:(i + 1) * d] = dbg_ref[:, i * d:(i + 1) * d] + jnp.sum(dzg, axis=0, keepdims=True)

    return pl.pallas_call(
        body, name=name, grid=(s // tm,),
        in_specs=[_rows(tm, d), _rows(tm, 2 * d), _rows(tm, d), _rows(tm, d),
                  _mat(*wbna_t), _mat(*wbsw_t), _mat(*wout), _full(dep.shape)],
        out_specs=[_rows(tm, d), _rows(tm, 2 * d), _rows(tm, d), _rows(tm, d), _rows(tm, 512), _rows(tm, 512),
                   _full((1, 2 * d))],
        out_shape=[jax.ShapeDtypeStruct((s, d), BF16), jax.ShapeDtypeStruct((s, 2 * d), BF16),
                   jax.ShapeDtypeStruct((s, d), BF16), jax.ShapeDtypeStruct((s, d), BF16),
                   jax.ShapeDtypeStruct((s, 512), BF16), jax.ShapeDtypeStruct((s, 512), BF16),
                   jax.ShapeDtypeStruct((1, 2 * d), F32)],
        compiler_params=_params(),
    )(dx, gt, a_na, a_sw, wbna_t[0], wbsw_t[0], wout[0], dep)


def qk_norm_bwd(dqa, dka, dva, dqs, dks, dvs, zq, dzg, gq_na, gk_na, gq_sw, gk_sw, bd, name):
    s = zq.shape[0]
    d2 = dzg.shape[1]
    n_in = QKV_WIDTH + d2
    tm = _row_tile(s)

    def body(dqa_ref, dka_ref, dva_ref, dqs_ref, dks_ref, dvs_ref, zq_ref, dzg_ref,
             gqa_ref, gka_ref, gqs_ref, gks_ref, bd_ref, dz_ref, dgqa_ref, dgka_ref, dgqs_ref, dgks_ref):
        @pl.when(pl.program_id(0) == 0)
        def _():
            for r in (dgqa_ref, dgka_ref, dgqs_ref, dgks_ref):
                r[...] = jnp.zeros(r.shape, F32)

        bd512 = bd_ref[...]
        bd128 = bd_ref[0:SW_KV_WIDTH, 0:SW_KV_WIDTH]

        def one(c0, c1, dy_ref, g_ref, dg_ref, bdm, scale):
            z = zq_ref[:, c0:c1].astype(F32)
            r = lax.rsqrt(_group_mean(z * z, bdm) + EPS)
            zh = z * r
            dy = dy_ref[...] * scale
            dyg = dy * g_ref[...]
            dz = r * (dyg - zh * _group_mean(dyg * zh, bdm))
            dz_ref[:, c0:c1] = dz.astype(BF16)
            dg_ref[...] = dg_ref[...] + jnp.sum(dy * zh, axis=0, keepdims=True)

        one(0, 512, dqa_ref, gqa_ref, dgqa_ref, bd512, SCALE)
        one(512, 1024, dka_ref, gka_ref, dgka_ref, bd512, 1.0)
        dz_ref[:, 1024:1536] = dva_ref[...].astype(BF16)
        one(1536, 2048, dqs_ref, gqs_ref, dgqs_ref, bd512, SCALE)
        one(2048, 2176, dks_ref, gks_ref, dgks_ref, bd128, 1.0)
        dz_ref[:, 2176:2304] = dvs_ref[...].astype(BF16)
        dz_ref[:, QKV_WIDTH:n_in] = dzg_ref[...]

    return pl.pallas_call(
        body, name=name, grid=(s // tm,),
        in_specs=[_rows(tm, 512), _rows(tm, 512), _rows(tm, 512), _rows(tm, 512), _rows(tm, 128), _rows(tm, 128),
                  _rows(tm, QKV_WIDTH), _rows(tm, d2),
                  _full((1, 512)), _full((1, 512)), _full((1, 512)), _full((1, 128)), _full((512, 512))],
        out_specs=[_rows(tm, n_in), _full((1, 512)), _full((1, 512)), _full((1, 512)), _full((1, 128))],
        out_shape=[jax.ShapeDtypeStruct((s, n_in), BF16)] + [jax.ShapeDtypeStruct((1, 512), F32)] * 3
                  + [jax.ShapeDtypeStruct((1, 128), F32)],
        compiler_params=_params(),
    )(dqa, dka, dva, dqs, dks, dvs, zq, dzg, gq_na, gk_na, gq_sw, gk_sw, bd)


def ffn_bwd_act(dx, wd, hg, hu, name):
    s, d = dx.shape
    f = wd[0].shape[1]
    tm = _row_tile(s)
    fc = _col_chunk(f)

    def body(dx_ref, w_ref, hg_ref, hu_ref, dxb_ref, dhg_ref, dhu_ref):
        dxv = dx_ref[...]
        dxb_ref[...] = dxv.astype(BF16)
        half = (0.5 * dxv).astype(BF16)
        for c0 in range(0, f, fc):
            dact = _dotg(half, w_ref[c0:c0 + fc, :], NT)
            dhu_ref[:, c0:c0 + fc] = (dact * hu_ref[:, c0:c0 + fc].astype(F32)).astype(BF16)
            dhg_ref[:, c0:c0 + fc] = (dact * hg_ref[:, c0:c0 + fc].astype(F32)).astype(BF16)

    return pl.pallas_call(
        body, name=name, grid=(s // tm,),
        in_specs=[_rows(tm, d), _mat(*wd), _rows(tm, f), _rows(tm, f)],
        out_specs=[_rows(tm, d), _rows(tm, f), _rows(tm, f)],
        out_shape=[jax.ShapeDtypeStruct((s, d), BF16), jax.ShapeDtypeStruct((s, f), BF16),
                   jax.ShapeDtypeStruct((s, f), BF16)],
        compiler_params=_params(),
    )(dx, wd[0], hg, hu)


def proj_bwd_norm(acts, weights, x, gain, dx, dep, name):
    s, d = x.shape
    tm = _row_tile(s)
    n = len(acts)

    def body(*refs):
        a_refs, w_refs = refs[:n], refs[n:2 * n]
        x_ref, g_ref, dx_ref, _, o_ref, dg_ref = refs[2 * n:]

        @pl.when(pl.program_id(0) == 0)
        def _():
            dg_ref[...] = jnp.zeros(dg_ref.shape, F32)

        dxn = _dot(a_refs[0][...], w_refs[0][...])
        for a_ref, w_ref in zip(a_refs[1:], w_refs[1:]):
            dxn = dxn + _dot(a_ref[...], w_ref[...])
        xv = x_ref[...]
        r = _rstd(xv)
        xh = xv * r
        dxh = dxn * g_ref[...]
        o_ref[...] = dx_ref[...] + r * (dxh - xh * jnp.mean(dxh * xh, axis=-1, keepdims=True))
        dg_ref[...] = dg_ref[...] + jnp.sum(dxn * xh, axis=0, keepdims=True)

    return pl.pallas_call(
        body, name=name, grid=(s // tm,),
        in_specs=[_rows(tm, a.shape[1]) for a in acts] + [_mat(*w) for w in weights]
                 + [_rows(tm, d), _full((1, d)), _rows(tm, d), _full(dep.shape)],
        out_specs=[_rows(tm, d), _full((1, d))],
        out_shape=[jax.ShapeDtypeStruct((s, d), F32), jax.ShapeDtypeStruct((1, d), F32)],
        compiler_params=_params(),
    )(*acts, *[w[0] for w in weights], x, gain, dx, dep)


def tn_matmul(lhs, b, scale, name):
    nl = len(lhs)
    s, n = lhs[0].shape
    k = b.shape[1]
    tn = _tn_tile(n)

    def body(*refs):
        b_ref = refs[nl]
        for a_ref, o_ref in zip(refs[:nl], refs[nl + 1:]):
            o_ref[...] = (scale * _dotg(a_ref[...], b_ref[...], TN)).astype(BF16)

    return pl.pallas_call(
        body, name=name, grid=(n // tn,),
        in_specs=[pl.BlockSpec((s, tn), lambda i: (0, i))] * nl
                 + [pl.BlockSpec((s, k), lambda i: (0, 0), pipeline_mode=ONCE)],
        out_specs=[pl.BlockSpec((tn, k), lambda i: (i, 0))] * nl,
        out_shape=[jax.ShapeDtypeStruct((n, k), BF16)] * nl,
        compiler_params=_params(),
    )(*lhs, b)


def _mesh_pos():
    return lax.axis_index("x"), lax.axis_index("y"), lax.axis_index("c")


def _peers():
    x, y, c = _mesh_pos()
    peers = []
    for rel in range(1, N_DEV):
        peers.append((1 - x if rel & 4 else x, 1 - y if rel & 2 else y, 1 - c if rel & 1 else c))
    return 4 * x + 2 * y + c, peers


HBM_SPEC = pl.BlockSpec(memory_space=pltpu.HBM)
SEM_SPEC = pl.BlockSpec(memory_space=pltpu.SEMAPHORE)


def _split_call(body, name, thru, n_sems, extra=(), with_token=True):
    hbm = lambda t: pltpu.with_memory_space_constraint(t, pltpu.HBM)
    effect = pltpu.CompilerParams(has_side_effects=pltpu.SideEffectType.DATAFLOW_SIDE_EFFECTING)
    nt = len(thru)
    thru_shapes = [pltpu.HBM(t.shape, t.dtype) for t in thru]
    if with_token:
        (after,) = extra
        outs = pl.pallas_call(
            body, name=name, in_specs=[HBM_SPEC] * nt + [pl.BlockSpec(memory_space=pl.ANY)],
            out_specs=[SEM_SPEC] * len(n_sems) + [HBM_SPEC] * nt + [pl.BlockSpec(memory_space=pltpu.VMEM)],
            out_shape=[pltpu.SemaphoreType.DMA((k,)) for k in n_sems] + thru_shapes
                      + [jax.ShapeDtypeStruct((8, LANES), F32)],
            input_output_aliases={i: len(n_sems) + i for i in range(nt)}, compiler_params=effect,
        )(*[hbm(t) for t in thru], after)
        return outs[:len(n_sems)], outs[len(n_sems):-1], outs[-1]
    return pl.pallas_call(
        body, name=name,
        in_specs=[HBM_SPEC] * nt + [SEM_SPEC] * len(n_sems) + [pl.BlockSpec(memory_space=pl.ANY)],
        out_specs=[HBM_SPEC] * nt, out_shape=thru_shapes,
        input_output_aliases={i: i for i in range(nt)}, compiler_params=effect,
    )(*thru, *extra)


def _gather_targets():
    x, y, c = _mesh_pos()
    return 4 * x + 2 * y + c, [(x, y, 1 - c), (1 - x, y, c), (x, 1 - y, c), (1 - x, 1 - y, c)]


def gather_start(shards, after, name):
    n = len(shards)
    zones = [lax.empty((w.shape[0], N_DEV) + w.shape[1:], w.dtype) for w in shards]

    def body(*refs):
        ins, zs = refs[:n], refs[n:2 * n]
        send_sems, recv_sems, local_sems = refs[2 * n + 1:2 * n + 4]
        token = refs[-1]
        me, targets = _gather_targets()
        for a in range(n):
            pltpu.make_async_copy(ins[a], zs[a].at[:, me], local_sems.at[a]).start()
            for k, to in enumerate(targets):
                pltpu.make_async_remote_copy(
                    src_ref=ins[a], dst_ref=zs[a].at[:, me], send_sem=send_sems.at[4 * a + k],
                    recv_sem=recv_sems.at[4 * a + k], device_id=to, device_id_type=MESH).start()
        token[...] = jnp.zeros(token.shape, F32)

    sems, thru, token = _split_call(body, name, list(shards) + zones, (4 * n, 4 * n, n), extra=(after,))
    return (sems, thru, n), token


def gather_wait(started, after, name):
    sems, thru, n = started

    def body(*refs):
        zs = refs[n:2 * n]
        send_sems, recv_sems, local_sems = refs[2 * n:2 * n + 3]
        _, targets = _gather_targets()
        for a in range(n):
            for k, to in enumerate(targets):
                cp = pltpu.make_async_remote_copy(
                    src_ref=zs[a].at[:, 0], dst_ref=zs[a].at[:, 0], send_sem=send_sems.at[4 * a + k],
                    recv_sem=recv_sems.at[4 * a + k], device_id=to, device_id_type=MESH)
                cp.wait_send()
                cp.wait_recv()
            pltpu.make_async_copy(zs[a].at[:, 0], zs[a].at[:, 0], local_sems.at[a]).wait()

    return _split_call(body, name, thru, (4 * n, 4 * n, n), extra=(*sems, after), with_token=False)[n:]


def forward_start(zones, after, name):
    n = len(zones)

    def body(*refs):
        zs = refs[:n]
        send_sems, recv_sems = refs[n + 1:n + 3]
        token = refs[-1]
        x, y, c = _mesh_pos()
        for a in range(n):
            for j, chip in enumerate([(1 - x, y), (x, 1 - y), (1 - x, 1 - y)]):
                blk = zs[a].at[:, 4 * chip[0] + 2 * chip[1] + c]
                pltpu.make_async_remote_copy(
                    src_ref=blk, dst_ref=blk, send_sem=send_sems.at[3 * a + j], recv_sem=recv_sems.at[3 * a + j],
                    device_id=(x, y, 1 - c), device_id_type=MESH).start()
        token[...] = jnp.zeros(token.shape, F32)

    sems, thru, token = _split_call(body, name, list(zones), (3 * n, 3 * n), extra=(after,))
    return (sems, thru, n), token


def forward_wait(started, after, name):
    sems, thru, n = started

    def body(*refs):
        zs = refs[:n]
        send_sems, recv_sems = refs[n:n + 2]
        x, y, c = _mesh_pos()
        for a in range(n):
            for j in range(3):
                cp = pltpu.make_async_remote_copy(
                    src_ref=zs[a].at[:, 0], dst_ref=zs[a].at[:, 0], send_sem=send_sems.at[3 * a + j],
                    recv_sem=recv_sems.at[3 * a + j], device_id=(x, y, 1 - c), device_id_type=MESH)
                cp.wait_send()
                cp.wait_recv()

    return _split_call(body, name, thru, (3 * n, 3 * n), extra=(*sems, after), with_token=False)


def scatter_start(groups, name):
    n = len(groups)
    flat = [g for grp in groups for g in grp]
    nf = len(flat)
    offs = np.cumsum([0] + [len(grp) for grp in groups])
    lands = [lax.empty((N_DEV, len(grp)) + grp[0].shape[1:], grp[0].dtype) for grp in groups]

    def body(*refs):
        ins, zones = refs[:nf], refs[nf:nf + n]
        send_sems, recv_sems, local_sems = refs[nf + n:nf + n + 3]
        token = refs[-1]
        me, peers = _peers()
        for a in range(n):
            for w in range(len(groups[a])):
                pltpu.make_async_copy(ins[offs[a] + w].at[me], zones[a].at[me, w], local_sems.at[a]).start()
        for k, peer in enumerate(peers):
            p_id = 4 * peer[0] + 2 * peer[1] + peer[2]
            for a in range(n):
                for w in range(len(groups[a])):
                    pltpu.make_async_remote_copy(
                        src_ref=ins[offs[a] + w].at[p_id], dst_ref=zones[a].at[me, w],
                        send_sem=send_sems.at[7 * a + k], recv_sem=recv_sems.at[7 * a + k],
                        device_id=peer, device_id_type=MESH).start()
        token[...] = jnp.zeros(token.shape, F32)

    hbm = lambda t: pltpu.with_memory_space_constraint(t, pltpu.HBM)
    outs = pl.pallas_call(
        body, name=name,
        in_specs=[HBM_SPEC] * (nf + n),
        out_specs=[SEM_SPEC] * 3 + [HBM_SPEC] * (nf + n) + [pl.BlockSpec(memory_space=pltpu.VMEM)],
        out_shape=[pltpu.SemaphoreType.DMA((7 * n,)), pltpu.SemaphoreType.DMA((7 * n,)), pltpu.SemaphoreType.DMA((n,))]
                  + [pltpu.HBM(t.shape, t.dtype) for t in flat + lands]
                  + [jax.ShapeDtypeStruct((8, LANES), F32)],
        input_output_aliases={i: 3 + i for i in range(nf + n)},
        compiler_params=pltpu.CompilerParams(has_side_effects=pltpu.SideEffectType.DATAFLOW_SIDE_EFFECTING),
    )(*[hbm(t) for t in flat], *[hbm(t) for t in lands])
    sems, thru, token = outs[:3], outs[3:3 + nf + n], outs[-1]
    return (sems, thru, [len(grp) for grp in groups]), token


def scatter_wait(started, after, name):
    (send_sems, recv_sems, local_sems), thru, sizes = started
    n = len(sizes)
    nf = len(thru) - n

    def body(*refs):
        zones = refs[nf:nf + n]
        s_sems, r_sems, l_sems = refs[nf + n:nf + n + 3]
        me, peers = _peers()
        for a in range(n):
            for k, peer in enumerate(peers):
                cp = pltpu.make_async_remote_copy(
                    src_ref=zones[a].at[0], dst_ref=zones[a].at[0],
                    send_sem=s_sems.at[7 * a + k], recv_sem=r_sems.at[7 * a + k], device_id=peer,
                    device_id_type=MESH)
                cp.wait_send()
                cp.wait_recv()
            pltpu.make_async_copy(zones[a].at[0], zones[a].at[0], l_sems.at[a]).wait()

    outs = pl.pallas_call(
        body, name=name,
        in_specs=[HBM_SPEC] * (nf + n) + [SEM_SPEC] * 3 + [pl.BlockSpec(memory_space=pl.ANY)],
        out_specs=[HBM_SPEC] * (nf + n),
        out_shape=[pltpu.HBM(t.shape, t.dtype) for t in thru],
        input_output_aliases={i: i for i in range(nf + n)},
        compiler_params=pltpu.CompilerParams(has_side_effects=pltpu.SideEffectType.DATAFLOW_SIDE_EFFECTING),
    )(*thru, send_sems, recv_sems, local_sems, after)
    return outs[nf:]


def pair_start(grads, after, name):
    nw = len(grads)
    land = lax.empty((4, nw) + grads[0].shape[1:], grads[0].dtype)

    def body(*refs):
        ins, zone = refs[:nw], refs[nw]
        send_sems, recv_sems = refs[nw + 2:nw + 4]
        x, y, c = _mesh_pos()
        for j in range(4):
            for w in range(nw):
                pltpu.make_async_remote_copy(
                    src_ref=ins[w].at[2 * j + (1 - c)], dst_ref=zone.at[j, w], send_sem=send_sems.at[0],
                    recv_sem=recv_sems.at[0], device_id=(x, y, 1 - c), device_id_type=MESH).start()
        refs[-1][...] = jnp.zeros(refs[-1].shape, F32)

    sems, thru, token = _split_call(body, name, list(grads) + [land], (1, 1), extra=(after,))
    return (sems, thru, nw), token


def pair_wait(started, after, name):
    sems, thru, nw = started

    def body(*refs):
        zone = refs[nw]
        send_sems, recv_sems = refs[nw + 1:nw + 3]
        x, y, c = _mesh_pos()
        cp = pltpu.make_async_remote_copy(src_ref=zone, dst_ref=zone, send_sem=send_sems.at[0],
                                          recv_sem=recv_sems.at[0], device_id=(x, y, 1 - c), device_id_type=MESH)
        cp.wait_send()
        cp.wait_recv()

    outs = _split_call(body, name, thru, (1, 1), extra=(*sems, after), with_token=False)
    return outs[:nw], outs[nw]


def pair_sum(grads, land, name):
    nw = len(grads)
    _, r, c_dim = grads[0].shape

    def body(*refs):
        g_refs, l_ref, o_ref = refs[:nw], refs[nw], refs[nw + 1]
        core = lax.axis_index("c")
        for w in range(nw):
            o_ref[0, w] = (g_refs[w][0, core].astype(F32) + l_ref[0, w].astype(F32)).astype(BF16)

    return pl.pallas_call(
        body, name=name, grid=(4,),
        in_specs=[pl.BlockSpec((1, 2, r, c_dim), lambda j: (j, 0, 0, 0))] * nw
                 + [pl.BlockSpec((1, nw, r, c_dim), lambda j: (j, 0, 0, 0))],
        out_specs=pl.BlockSpec((1, nw, r, c_dim), lambda j: (j, 0, 0, 0)),
        out_shape=jax.ShapeDtypeStruct((4, nw, r, c_dim), BF16),
        compiler_params=_params(),
    )(*[g.reshape(4, 2, r, c_dim) for g in grads], land)


def _other_chips():
    x, y, c = _mesh_pos()
    chips = []
    for rel in range(1, 4):
        px, py = (1 - x if rel & 2 else x), (1 - y if rel & 1 else y)
        chips.append((px, py, 2 * px + py))
    return 2 * x + y, c, chips


def chip_start(pair_sums, after, name):
    land = lax.empty(pair_sums.shape, pair_sums.dtype)

    def body(*refs):
        h_ref, zone = refs[0], refs[1]
        send_sems, recv_sems, local_sem = refs[3:6]
        mine, c, chips = _other_chips()
        pltpu.make_async_copy(h_ref.at[mine], zone.at[mine], local_sem.at[0]).start()
        for k, (px, py, j) in enumerate(chips):
            pltpu.make_async_remote_copy(
                src_ref=h_ref.at[j], dst_ref=zone.at[mine], send_sem=send_sems.at[k], recv_sem=recv_sems.at[k],
                device_id=(px, py, c), device_id_type=MESH).start()
        refs[-1][...] = jnp.zeros(refs[-1].shape, F32)

    sems, thru, token = _split_call(body, name, [pair_sums, land], (3, 3, 1), extra=(after,))
    return (sems, thru), token


def chip_wait(started, after, name):
    sems, thru = started

    def body(*refs):
        zone = refs[1]
        send_sems, recv_sems, local_sem = refs[2:5]
        _, c, chips = _other_chips()
        for k, (px, py, _) in enumerate(chips):
            cp = pltpu.make_async_remote_copy(
                src_ref=zone.at[0], dst_ref=zone.at[0], send_sem=send_sems.at[k], recv_sem=recv_sems.at[k],
                device_id=(px, py, c), device_id_type=MESH)
            cp.wait_send()
            cp.wait_recv()
        pltpu.make_async_copy(zone.at[0], zone.at[0], local_sem.at[0]).wait()

    return _split_call(body, name, thru, (3, 3, 1), extra=(*sems, after), with_token=False)[1]


def share_small(parts, after):
    n = len(parts)

    def body(*refs):
        ins, outs = refs[:n], refs[n + 1:2 * n + 1]
        send_sems, recv_sems, local_sems = refs[2 * n + 1:]
        me, peers = _peers()
        copies = []
        for i in range(n):
            copies.append(pltpu.make_async_copy(ins[i], outs[i].at[me], local_sems.at[i]))
            copies += [pltpu.make_async_remote_copy(
                src_ref=ins[i], dst_ref=outs[i].at[me], send_sem=send_sems.at[7 * i + k],
                recv_sem=recv_sems.at[7 * i + k], device_id=peer, device_id_type=MESH)
                for k, peer in enumerate(peers)]
        for cp in copies:
            cp.start()
        for cp in copies:
            cp.wait()

    vm = pl.BlockSpec(memory_space=pltpu.VMEM)
    return pl.pallas_call(
        body, name="share_small", in_specs=[vm] * n + [pl.BlockSpec(memory_space=pl.ANY)], out_specs=[vm] * n,
        out_shape=[jax.ShapeDtypeStruct((N_DEV,) + p.shape, p.dtype) for p in parts],
        scratch_shapes=[pltpu.SemaphoreType.DMA((7 * n,)), pltpu.SemaphoreType.DMA((7 * n,)),
                        pltpu.SemaphoreType.DMA((n,))],
    )(*parts, after)


def _adamw_math(w, g, m, v):
    m = ADAM_B1 * m + (1.0 - ADAM_B1) * g
    v = ADAM_B2 * v + (1.0 - ADAM_B2) * (g * g)
    m_hat = m / (1.0 - ADAM_B1 ** ADAM_STEP)
    v_hat = v / (1.0 - ADAM_B2 ** ADAM_STEP)
    delta = -ADAM_LR * (m_hat / (jnp.sqrt(v_hat) + ADAM_EPS) + ADAM_WD * w)
    return delta, m, v


def adamw_layer(zone, w_idx, layer, w, m, v, prev, after, name):
    n_src, _, r, c = zone.shape
    depth = w.shape[0]
    if prev is None:
        prev = tuple(lax.empty((depth, r, c), F32) for _ in range(4))
    tr = r // 2 if r % 16 == 0 else r

    def body(z_ref, w_ref, m_ref, v_ref, *rest):
        g_ref, d_ref, mo_ref, vo_ref = rest[5:]
        g = z_ref[0].astype(F32)
        for src in range(1, n_src):
            g = g + z_ref[src].astype(F32)
        g_ref[...] = g
        d_ref[...], mo_ref[...], vo_ref[...] = _adamw_math(w_ref[...], g, m_ref[...], v_ref[...])

    rows = pl.BlockSpec((None, tr, c), lambda i: (layer, i, 0))
    anywhere = pl.BlockSpec(memory_space=pl.ANY)
    return pl.pallas_call(
        body, name=name, grid=(r // tr,),
        in_specs=[pl.BlockSpec((n_src, None, tr, c), lambda i: (0, w_idx, i, 0)), rows, rows, rows]
                 + [anywhere] * 5,
        out_specs=[rows] * 4,
        out_shape=[jax.ShapeDtypeStruct((depth, r, c), F32)] * 4,
        input_output_aliases={4 + k: k for k in range(4)},
        compiler_params=_params(),
    )(zone, w, m, v, *prev, after)


def adamw_small(ws, recvs, ms, vs, name):
    n = len(ws)

    def body(*refs):
        w_refs, r_refs, m_refs, v_refs = (refs[i * n:(i + 1) * n] for i in range(4))
        g_refs, d_refs, mo_refs, vo_refs = (refs[(4 + i) * n:(5 + i) * n] for i in range(4))
        for i in range(n):
            g = r_refs[i][0]
            for src in range(1, N_DEV):
                g = g + r_refs[i][src]
            g_refs[i][...] = g
            d_refs[i][...], mo_refs[i][...], vo_refs[i][...] = _adamw_math(w_refs[i][...], g, m_refs[i][...],
                                                                            v_refs[i][...])

    vm = pl.BlockSpec(memory_space=pltpu.VMEM)
    outs = pl.pallas_call(
        body, name=name, in_specs=[vm] * (4 * n), out_specs=[vm] * (4 * n),
        out_shape=[jax.ShapeDtypeStruct(w.shape, F32) for w in ws] * 4,
        compiler_params=pltpu.CompilerParams(vmem_limit_bytes=V7X_VMEM_LIMIT),
    )(*ws, *recvs, *ms, *vs)
    return [outs[i * n:(i + 1) * n] for i in range(4)]


SMALL_NAMES = ("ffn1_norm", "mix_norm", "ffn2_norm", "b_gate", "na_q_norm", "na_k_norm", "sw_q_norm", "sw_k_norm",
               "na_rpb", "sw_sink", "t5_rel_table")


def kernel(x, ffn1_norm, ffn1_w_gate, ffn1_w_up, ffn1_w_down, mix_norm, w_in, b_gate, na_q_norm, na_k_norm, na_rpb, sw_q_norm, sw_k_norm, sw_sink, t5_rel_table, w_branch_na, w_branch_sw, w_out, ffn2_norm, ffn2_w_gate, ffn2_w_up, ffn2_w_down, loss_target, m_ffn1_norm, m_ffn1_w_gate, m_ffn1_w_up, m_ffn1_w_down, m_mix_norm, m_w_in, m_b_gate, m_na_q_norm, m_na_k_norm, m_na_rpb, m_sw_q_norm, m_sw_k_norm, m_sw_sink, m_t5_rel_table, m_w_branch_na, m_w_branch_sw, m_w_out, m_ffn2_norm, m_ffn2_w_gate, m_ffn2_w_up, m_ffn2_w_down, v_ffn1_norm, v_ffn1_w_gate, v_ffn1_w_up, v_ffn1_w_down, v_mix_norm, v_w_in, v_b_gate, v_na_q_norm, v_na_k_norm, v_na_rpb, v_sw_q_norm, v_sw_k_norm, v_sw_sink, v_t5_rel_table, v_w_branch_na, v_w_branch_sw, v_w_out, v_ffn2_norm, v_ffn2_w_gate, v_ffn2_w_up, v_ffn2_w_down):
    weights = dict(ffn1_norm=ffn1_norm, ffn1_w_gate=ffn1_w_gate, ffn1_w_up=ffn1_w_up, ffn1_w_down=ffn1_w_down,
                   mix_norm=mix_norm, w_in=w_in, b_gate=b_gate, na_q_norm=na_q_norm, na_k_norm=na_k_norm,
                   na_rpb=na_rpb, sw_q_norm=sw_q_norm, sw_k_norm=sw_k_norm, sw_sink=sw_sink,
                   t5_rel_table=t5_rel_table, w_branch_na=w_branch_na, w_branch_sw=w_branch_sw, w_out=w_out,
                   ffn2_norm=ffn2_norm, ffn2_w_gate=ffn2_w_gate, ffn2_w_up=ffn2_w_up, ffn2_w_down=ffn2_w_down)
    mom_m = dict(ffn1_norm=m_ffn1_norm, ffn1_w_gate=m_ffn1_w_gate, ffn1_w_up=m_ffn1_w_up, ffn1_w_down=m_ffn1_w_down,
                 mix_norm=m_mix_norm, w_in=m_w_in, b_gate=m_b_gate, na_q_norm=m_na_q_norm, na_k_norm=m_na_k_norm,
                 na_rpb=m_na_rpb, sw_q_norm=m_sw_q_norm, sw_k_norm=m_sw_k_norm, sw_sink=m_sw_sink,
                 t5_rel_table=m_t5_rel_table, w_branch_na=m_w_branch_na, w_branch_sw=m_w_branch_sw, w_out=m_w_out,
                 ffn2_norm=m_ffn2_norm, ffn2_w_gate=m_ffn2_w_gate, ffn2_w_up=m_ffn2_w_up, ffn2_w_down=m_ffn2_w_down)
    mom_v = dict(ffn1_norm=v_ffn1_norm, ffn1_w_gate=v_ffn1_w_gate, ffn1_w_up=v_ffn1_w_up, ffn1_w_down=v_ffn1_w_down,
                 mix_norm=v_mix_norm, w_in=v_w_in, b_gate=v_b_gate, na_q_norm=v_na_q_norm, na_k_norm=v_na_k_norm,
                 na_rpb=v_na_rpb, sw_q_norm=v_sw_q_norm, sw_k_norm=v_sw_k_norm, sw_sink=v_sw_sink,
                 t5_rel_table=v_t5_rel_table, w_branch_na=v_w_branch_na, w_branch_sw=v_w_branch_sw, w_out=v_w_out,
                 ffn2_norm=v_ffn2_norm, ffn2_w_gate=v_ffn2_w_gate, ffn2_w_up=v_ffn2_w_up, ffn2_w_down=v_ffn2_w_down)
    order = list(weights)

    depth = ffn1_norm.shape[0]
    s, d = x.shape[1], x.shape[2]
    xs = x[0]
    tr = lambda w: jnp.swapaxes(w, -1, -2)

    merge = lambda t: t.reshape(t.shape[0], N_DEV * t.shape[2], t.shape[3])
    no_dep = jnp.zeros((8, LANES), F32)

    def shards_of(kind, l):
        stack = lambda *ws: jnp.stack(ws).astype(BF16)
        if kind == "ffn1":
            return [stack(tr(ffn1_w_gate[l]), tr(ffn1_w_up[l]), ffn1_w_down[l])]
        if kind == "win":
            return [stack(tr(w_in[l]))]
        return [stack(tr(ffn2_w_gate[l]), tr(ffn2_w_up[l]), ffn2_w_down[l]), stack(w_out[l]),
                stack(tr(w_branch_na[l]), tr(w_branch_sw[l]))]

    def start(kind, l, after):
        return gather_start(shards_of(kind, l), after, f"gather_{kind}_{l}")

    def arrive(started, kind, l, after):
        zones = gather_wait(started, after, f"gather_{kind}_{l}_wait")
        return forward_start(zones, no_dep, f"forward_{kind}_{l}")

    def finish(fwd, kind, l, after):
        return [merge(z) for z in forward_wait(fwd, after, f"forward_{kind}_{l}_wait")]

    bd = jnp.asarray(np.kron(np.eye(NA_WIDTH // HEAD_DIM), np.full((HEAD_DIM, HEAD_DIM), 1.0 / HEAD_DIM)), BF16)
    bmap = jnp.asarray(_t5_bucket_map())
    tile8 = lambda g: jnp.tile(g, NA_WIDTH // HEAD_DIM).reshape(1, NA_WIDTH)
    tile2 = lambda g: jnp.tile(g, SW_KV_WIDTH // HEAD_DIM).reshape(1, SW_KV_WIDTH)

    st_first, tok = start("ffn1", 0, no_dep)
    t5b = t5_expand(t5_rel_table, bmap, tok, "t5_expand").reshape(SW_STACK, 3 * SW_BLOCK)
    t2_tables = [rpb_expand(_rpb_rows(na_rpb[l]), tok, f"rpb_expand_{l}") for l in range(depth)]
    fwd, _ = arrive(st_first, "ffn1", 0, t2_tables[-1])
    st_win, dep = start("win", 0, t5b)
    (first,) = finish(fwd, "ffn1", 0, dep)

    saved = []
    layer_w = {0: dict(wg1=(first, 0), wu1=(first, 1), wd1=(first, 2))}
    cur = xs
    for l in range(depth):
        sv = {}
        lw = layer_w[l]
        sv["x0"] = cur
        cur, sv["xn1"], sv["hg1"], sv["hu1"], sv["act1"] = ffn_both(
            cur, ffn1_norm[l][None], lw["wg1"], lw["wu1"], lw["wd1"], dep, f"ffn1_{l}")
        sv["x1"] = cur
        fwd, _ = arrive(st_win, "win", l, cur)
        st_rest, tok = start("rest", l, cur)
        (zb,) = finish(fwd, "win", l, tok)
        lw["win"] = (zb, 0)
        sv["gains"] = (tile8(na_q_norm[l]), tile8(na_k_norm[l]), tile8(sw_q_norm[l]), tile2(sw_k_norm[l]))
        sv["hn"], sv["zq"], sv["qa"], sv["ka"], sv["qs"], sv["ks"], sv["gt"] = mix_in(
            cur, mix_norm[l][None], lw["win"], b_gate[l][None], *sv["gains"], bd, f"mix_in_{l}")
        sv["t2"] = t2_tables[l]
        sv["o_na"] = na_fwd(sv["qa"], sv["ka"], sv["zq"], sv["t2"], f"na_fwd_{l}")
        dep = no_dep
        if l + 1 < depth:
            st_ffn1, dep = start("ffn1", l + 1, sv["o_na"])
        sv["o_sw"] = sw_fwd(sv["qs"], sv["ks"], sv["zq"], t5b, sw_sink[l], dep, f"sw_fwd_{l}")
        fwd, tok = arrive(st_rest, "rest", l, sv["o_sw"])
        za, zc, zd = finish(fwd, "rest", l, tok)
        lw.update(wg2=(za, 0), wu2=(za, 1), wd2=(za, 2), wout=(zc, 0), wna=(zd, 0), wsw=(zd, 1))
        cur, sv["a_na"], sv["a_sw"], sv["merged"] = merge_out(
            cur, sv["o_na"], sv["o_sw"], sv["gt"], lw["wna"], lw["wsw"], lw["wout"], f"merge_out_{l}")
        sv["x2"] = cur
        dep = no_dep
        if l + 1 < depth:
            st_win, dep = start("win", l + 1, cur)
        sv["xn2"], sv["hg2"], sv["hu2"], sv["act2"] = ffn_up(cur, ffn2_norm[l][None], lw["wg2"], lw["wu2"], dep,
                                                             f"ffn2_up_{l}")
        dep = no_dep
        if l + 1 < depth:
            fwd, dep = arrive(st_ffn1, "ffn1", l + 1, sv["act2"])
        if l + 1 < depth:
            cur = ffn_down(cur, sv["act2"], lw["wd2"], dep, f"ffn2_down_{l}")
            (za,) = finish(fwd, "ffn1", l + 1, cur)
            layer_w[l + 1] = dict(wg1=(za, 0), wu1=(za, 1), wd1=(za, 2))
        else:
            dx, loss_acc = ffn_down(cur, sv["act2"], lw["wd2"], dep, f"ffn2_down_{l}", target=loss_target[0])
        dep = no_dep
        saved.append(sv)

    loss = lax.psum(jnp.sum(loss_acc) * (0.5 / d), ("x", "y", "c"))

    split = lambda t: t.reshape(N_DEV, t.shape[0] // N_DEV, t.shape[1])
    pending = {}
    last_key = "ffn1_0"
    two_level = {last_key}
    small = {k: [None] * depth for k in SMALL_NAMES if k != "t5_rel_table"}
    dbias_sw = []
    for l in reversed(range(depth)):
        sv = saved[l]
        lw = layer_w[l]
        wg1, wu1, wd1, wg2, wu2, wd2 = (lw[k] for k in ("wg1", "wu1", "wd1", "wg2", "wu2", "wd2"))
        win_t, wout_l, wna_t, wsw_t = lw["win"], lw["wout"], lw["wna"], lw["wsw"]
        blocks = ((2, "x2", "xn2", "hg2", "hu2", "act2", wg2, wu2, wd2, "ffn2_norm", 3),
                  (1, "x0", "xn1", "hg1", "hu1", "act1", wg1, wu1, wd1, "ffn1_norm", 0))

        def ffn_backward(dx, blk):
            tag, xk, xnk, hgk, huk, actk, wg, wu, wd, norm_name, slot = blk
            gains = weights[norm_name]
            dxb, dhg, dhu = ffn_bwd_act(dx, wd, sv[hgk], sv[huk], f"ffn{tag}_bwd_act_{l}")
            (gwd,) = tn_matmul([sv[actk]], dxb, 0.5, f"ffn{tag}_dwd_{l}")
            gwg, gwu = tn_matmul([dhg, dhu], sv[xnk], 1.0, f"ffn{tag}_dwgu_{l}")
            key = f"ffn{tag}_{l}"
            blocks_of = [split(gwg), split(gwu), split(gwd)]
            if key in two_level:
                paired, token = pair_start(blocks_of, dxb, f"pair_{key}")
            else:
                pending[key], token = scatter_start([blocks_of], f"scatter_{key}")
            dx, dg = proj_bwd_norm([dhg, dhu], [wg, wu], sv[xk], gains[l][None], dx, token, f"ffn{tag}_bwd_x_{l}")
            token = no_dep
            if key in two_level:
                thru, land = pair_wait(paired, dx, f"pair_{key}_wait")
                pending[key], token = chip_start(pair_sum(thru, land, f"pair_sum_{key}"), dg, f"chips_{key}")
            small[norm_name][l] = dg[0]
            return dx, token

        dx, token = ffn_backward(dx, blocks[0])
        dxb, dzg, da_na, da_sw, do_na, do_sw, dbg = mix_bwd_out(
            dx, sv["gt"], sv["a_na"], sv["a_sw"], wna_t, wsw_t, wout_l, token, f"mix_bwd_out_{l}")
        small["b_gate"][l] = dbg[0]
        (gwout,) = tn_matmul([sv["merged"]], dxb, 1.0, f"dwout_{l}")
        (gwna,) = tn_matmul([da_na], sv["o_na"], 1.0, f"dwna_{l}")
        (gwsw,) = tn_matmul([da_sw], sv["o_sw"], 1.0, f"dwsw_{l}")
        dqa, dka, dva, dt2 = na_bwd(sv["qa"], sv["ka"], sv["zq"], sv["t2"], sv["o_na"], do_na, f"na_bwd_{l}")
        dqs, dks, dvs, dbias, dsink = sw_bwd(sv["qs"], sv["ks"], sv["zq"], t5b, sw_sink[l], sv["o_sw"], do_sw,
                                             f"sw_bwd_{l}")
        dbias_sw.append(dbias.reshape(SW_HEADS, SW_BLOCK, 3 * SW_BLOCK))
        small["sw_sink"][l] = jnp.sum(dsink[:, 0].reshape(SW_HEADS, SW_BLOCK), axis=1)
        small["na_rpb"][l] = _rpb_from_rows(rpb_reduce(dt2, f"rpb_reduce_{l}"))
        dz, dgqa, dgka, dgqs, dgks = qk_norm_bwd(dqa, dka, dva, dqs, dks, dvs, sv["zq"], dzg, *sv["gains"], bd,
                                                 f"qk_norm_bwd_{l}")
        fold = lambda g: jnp.sum(g.reshape(-1, HEAD_DIM), axis=0)
        small["na_q_norm"][l], small["na_k_norm"][l] = fold(dgqa), fold(dgka)
        small["sw_q_norm"][l], small["sw_k_norm"][l] = fold(dgqs), fold(dgks)
        (gwin,) = tn_matmul([dz], sv["hn"], 1.0, f"dwin_{l}")
        pending[f"mix_{l}"], token = scatter_start([[split(gwout)], [split(gwna), split(gwsw)], [split(gwin)]],
                                                   f"scatter_mix_{l}")
        dx, dg = proj_bwd_norm([dz], [win_t], sv["x1"], mix_norm[l][None], dx, token, f"mix_bwd_x_{l}")
        small["mix_norm"][l] = dg[0]
        dx, tail = ffn_backward(dx, blocks[1])

    dtab = t5_reduce(dbias_sw, bmap, "t5_reduce")
    small_parts = {k: jnp.stack(v) for k, v in small.items()}
    small_parts["t5_rel_table"] = jnp.transpose(dtab[:, :, 0])

    grads, delta, new_m, new_v = {}, {}, {}, {}
    state = {}
    chain = [tail]
    members = {"ffn": lambda t: [(f"ffn{t}_w_gate", 0, 0, True), (f"ffn{t}_w_up", 0, 1, True),
                                 (f"ffn{t}_w_down", 0, 2, False)],
               "mix": lambda t: [("w_out", 0, 0, False), ("w_branch_na", 1, 0, True), ("w_branch_sw", 1, 1, True),
                                 ("w_in", 2, 0, True)]}

    def collect(key):
        if key in two_level:
            zones = [chip_wait(pending[key], chain[0], f"wait_{key}")]
        else:
            zones = scatter_wait(pending[key], chain[0], f"wait_{key}")
        kind, l = key.split("_")
        for k, zi, wi, transposed in members[kind[:3]](kind[3:]):
            view = tr if transposed else (lambda t: t)
            state[k] = adamw_layer(zones[zi], wi, int(l), view(weights[k]), view(mom_m[k]), view(mom_v[k]),
                                   state.get(k), chain[0], f"adamw_{k}_{l}")
            chain[0] = state[k][1]
            if all(f"{kind}_{j}" in done for j in range(depth) if j != int(l)):
                grads[k], delta[k], new_m[k], new_v[k] = (view(t) for t in state[k])
        done.add(key)

    done = set()
    for key in pending:
        if key != last_key:
            collect(key)
    collect(last_key)
    recvs = share_small([small_parts[k] for k in SMALL_NAMES], chain[0])
    results = adamw_small([weights[k] for k in SMALL_NAMES], recvs, [mom_m[k] for k in SMALL_NAMES],
                          [mom_v[k] for k in SMALL_NAMES], "adamw_small")
    for dst, outs in zip((grads, delta, new_m, new_v), results):
        dst.update(dict(zip(SMALL_NAMES, outs)))

    return (loss, dx[None], *[grads[k] for k in order], *[delta[k] for k in order],
            *[new_m[k] for k in order], *[new_v[k] for k in order])
```

```python
import functools
import math

import numpy as np
import jax
import jax.numpy as jnp
from jax import lax
from jax.experimental import pallas as pl
from jax.experimental.pallas import tpu as pltpu

F32 = jnp.float32
BF16 = jnp.bfloat16
MESH = pl.DeviceIdType.MESH

N_DEV = 8
EPS = 1e-6
NEG = -1e30
HEAD_DIM = 64
GRID_W = 64
NA_ROWS = 8
NA_COLS = 16
NA_WIDTH = 512
SW_Q_WIDTH = 512
SW_KV_WIDTH = 128
SW_BLOCK = 128
SW_HEADS = 8
SW_REP = 4
REL_BUCKETS = 32
REL_MAX_DIST = 128
QKV_WIDTH = 3 * NA_WIDTH + SW_Q_WIDTH + 2 * SW_KV_WIDTH
SCALE = 1.0 / math.sqrt(HEAD_DIM)

ADAM_LR = 0.001
ADAM_B1 = 0.9
ADAM_B2 = 0.999
ADAM_EPS = 1e-08
ADAM_WD = 0.01
ADAM_STEP = 10

V7X_VMEM_LIMIT = 56 * 1024 * 1024
LANES = 128
MXU_TILE = 256

NT = (((1,), (1,)), ((), ()))
TN = (((0,), (0,)), ((), ()))


def _params(n_grid=1):
    return pltpu.CompilerParams(dimension_semantics=("arbitrary",) * n_grid,
                                vmem_limit_bytes=V7X_VMEM_LIMIT)


def _row_tile(s):
    for t in (512, 256, 128, 64, 32, 16, 8):
        if s % t == 0:
            return t
    raise ValueError(s)


def _tn_tile(n):
    best = max(t for t in range(LANES, min(n, 2304) + 1, LANES) if n % t == 0) if n % LANES == 0 else n
    return best // 2 if best == n and n >= 1024 else best


ONCE = pl.Buffered(1)


def _col_chunk(n):
    return MXU_TILE if n % MXU_TILE == 0 else n


def _dot(a, b):
    return jnp.dot(a, b, preferred_element_type=F32)


def _dotg(a, b, dn):
    return lax.dot_general(a, b, dn, preferred_element_type=F32)


def _sigmoid(v):
    return 1.0 / (1.0 + jnp.exp(-v))


def _rstd(xv):
    return lax.rsqrt(jnp.mean(xv * xv, axis=-1, keepdims=True) + EPS)


def _full(shape):
    nd = len(shape)
    return pl.BlockSpec(shape, lambda i, _n=nd: (0,) * _n)


def _rows(tm, width):
    return pl.BlockSpec((tm, width), lambda i: (i, 0))


def _mat(stack, idx):
    return pl.BlockSpec((None,) + tuple(stack.shape[1:]), lambda i, _w=idx: (_w, 0, 0), pipeline_mode=ONCE)


def _group_mean(v, bd):
    hi = v.astype(BF16)
    lo = (v - hi.astype(F32)).astype(BF16)
    return _dot(hi, bd) + _dot(lo, bd)


def _swiglu_tile(xn, wg_ref, wu_ref, dg_ref, du_ref, act_ref, fc):
    for c0 in range(0, wg_ref.shape[0], fc):
        hg = _dotg(xn, wg_ref[c0:c0 + fc, :], NT)
        hu = _dotg(xn, wu_ref[c0:c0 + fc, :], NT)
        sg = _sigmoid(hg)
        silu = hg * sg
        du_ref[:, c0:c0 + fc] = silu.astype(BF16)
        dg_ref[:, c0:c0 + fc] = (hu * (sg + silu * (1.0 - sg))).astype(BF16)
        act_ref[:, c0:c0 + fc] = (silu * hu).astype(BF16)


def ffn_up(x, gain, wg_t, wu_t, dep, name):
    s, d = x.shape
    f = wg_t[0].shape[1]
    tm = _row_tile(s)
    fc = _col_chunk(f)

    def body(x_ref, g_ref, wg_ref, wu_ref, dep_ref, xn_ref, dg_ref, du_ref, act_ref):
        xv = x_ref[...]
        xn = (xv * _rstd(xv) * g_ref[...]).astype(BF16)
        xn_ref[...] = xn
        _swiglu_tile(xn, wg_ref, wu_ref, dg_ref, du_ref, act_ref, fc)

    return pl.pallas_call(
        body, name=name, grid=(s // tm,),
        in_specs=[_rows(tm, d), _full((1, d)), _mat(*wg_t), _mat(*wu_t), _full(dep.shape)],
        out_specs=[_rows(tm, d), _rows(tm, f), _rows(tm, f), _rows(tm, f)],
        out_shape=[jax.ShapeDtypeStruct((s, d), BF16)] + [jax.ShapeDtypeStruct((s, f), BF16)] * 3,
        compiler_params=_params(),
    )(x, gain, wg_t[0], wu_t[0], dep)


def ffn_both(x, gain, wg_t, wu_t, wd, dep, name):
    s, d = x.shape
    f = wg_t[0].shape[1]
    tm = min(_row_tile(s), 256)
    fc = _col_chunk(f)

    def body(x_ref, g_ref, wg_ref, wu_ref, wd_ref, dep_ref, xo_ref, xn_ref, dg_ref, du_ref, act_ref):
        xv = x_ref[...]
        xn = (xv * _rstd(xv) * g_ref[...]).astype(BF16)
        xn_ref[...] = xn
        _swiglu_tile(xn, wg_ref, wu_ref, dg_ref, du_ref, act_ref, fc)
        xo_ref[...] = xv + 0.5 * _dot(act_ref[...], wd_ref[...])

    return pl.pallas_call(
        body, name=name, grid=(s // tm,),
        in_specs=[_rows(tm, d), _full((1, d)), _mat(*wg_t), _mat(*wu_t), _mat(*wd), _full(dep.shape)],
        out_specs=[_rows(tm, d), _rows(tm, d), _rows(tm, f), _rows(tm, f), _rows(tm, f)],
        out_shape=[jax.ShapeDtypeStruct((s, d), F32), jax.ShapeDtypeStruct((s, d), BF16)]
                  + [jax.ShapeDtypeStruct((s, f), BF16)] * 3,
        compiler_params=_params(),
    )(x, gain, wg_t[0], wu_t[0], wd[0], dep)


def ffn_down(x, act, wd, dep, name, target=None):
    s, d = x.shape
    f = act.shape[1]
    tm = _row_tile(s)

    def body(x_ref, a_ref, w_ref, dep_ref, *rest):
        y = x_ref[...] + 0.5 * _dot(a_ref[...], w_ref[...])
        if target is None:
            rest[0][...] = y
            return
        t_ref, dy_ref, acc_ref = rest

        @pl.when(pl.program_id(0) == 0)
        def _():
            acc_ref[...] = jnp.zeros(acc_ref.shape, F32)

        err = y - t_ref[...]
        dy_ref[...] = err * (1.0 / d)
        part = jnp.sum((err * err).reshape(tm // 8, 8, d), axis=0)
        acc = part[:, 0:LANES]
        for c0 in range(LANES, d, LANES):
            acc = acc + part[:, c0:c0 + LANES]
        acc_ref[...] = acc_ref[...] + acc

    ins = [_rows(tm, d), _rows(tm, f), _mat(*wd), _full(dep.shape)]
    if target is None:
        return pl.pallas_call(
            body, name=name, grid=(s // tm,), in_specs=ins, out_specs=_rows(tm, d),
            out_shape=jax.ShapeDtypeStruct((s, d), F32), compiler_params=_params(),
        )(x, act, wd[0], dep)
    return pl.pallas_call(
        body, name=name, grid=(s // tm,), in_specs=ins + [_rows(tm, d)],
        out_specs=[_rows(tm, d), _full((8, LANES))],
        out_shape=[jax.ShapeDtypeStruct((s, d), F32), jax.ShapeDtypeStruct((8, LANES), F32)],
        compiler_params=_params(),
    )(x, act, wd[0], dep, target)


def mix_in(x, gain, win_t, b_gate, gq_na, gk_na, gq_sw, gk_sw, bd, name):
    s, d = x.shape
    tm = _row_tile(s)
    gc = _col_chunk(2 * d)

    def body(x_ref, g_ref, w_ref, b_ref, gqa_ref, gka_ref, gqs_ref, gks_ref, bd_ref,
             hn_ref, zq_ref, qa_ref, ka_ref, qs_ref, ks_ref, gt_ref):
        xv = x_ref[...]
        hn = (xv * _rstd(xv) * g_ref[...]).astype(BF16)
        hn_ref[...] = hn

        def proj(c0, c1):
            return _dotg(hn, w_ref[c0:c1, :], NT)

        def headnorm(z, g, bdm):
            return z * lax.rsqrt(_group_mean(z * z, bdm) + EPS) * g

        bd512 = bd_ref[...]
        bd128 = bd_ref[0:SW_KV_WIDTH, 0:SW_KV_WIDTH]
        z = proj(0, 512)
        zq_ref[:, 0:512] = z.astype(BF16)
        qa_ref[...] = (headnorm(z, gqa_ref[...], bd512) * SCALE).astype(BF16)
        z = proj(512, 1024)
        zq_ref[:, 512:1024] = z.astype(BF16)
        ka_ref[...] = headnorm(z, gka_ref[...], bd512).astype(BF16)
        z = proj(1024, 1536)
        zq_ref[:, 1024:1536] = z.astype(BF16)
        z = proj(1536, 2048)
        zq_ref[:, 1536:2048] = z.astype(BF16)
        qs_ref[...] = (headnorm(z, gqs_ref[...], bd512) * SCALE).astype(BF16)
        z = proj(2048, 2176)
        zq_ref[:, 2048:2176] = z.astype(BF16)
        ks_ref[...] = headnorm(z, gks_ref[...], bd128).astype(BF16)
        z = proj(2176, 2304)
        zq_ref[:, 2176:2304] = z.astype(BF16)
        for c0 in range(0, 2 * d, gc):
            zg = proj(QKV_WIDTH + c0, QKV_WIDTH + c0 + gc) + b_ref[:, c0:c0 + gc]
            gt_ref[:, c0:c0 + gc] = _sigmoid(zg).astype(BF16)

    return pl.pallas_call(
        body, name=name, grid=(s // tm,),
        in_specs=[_rows(tm, d), _full((1, d)), _mat(*win_t), _full((1, 2 * d)),
                  _full((1, 512)), _full((1, 512)), _full((1, 512)), _full((1, 128)), _full((512, 512))],
        out_specs=[_rows(tm, d), _rows(tm, QKV_WIDTH), _rows(tm, 512), _rows(tm, 512), _rows(tm, 512),
                   _rows(tm, 128), _rows(tm, 2 * d)],
        out_shape=[jax.ShapeDtypeStruct((s, d), BF16), jax.ShapeDtypeStruct((s, QKV_WIDTH), BF16),
                   jax.ShapeDtypeStruct((s, 512), BF16), jax.ShapeDtypeStruct((s, 512), BF16),
                   jax.ShapeDtypeStruct((s, 512), BF16), jax.ShapeDtypeStruct((s, 128), BF16),
                   jax.ShapeDtypeStruct((s, 2 * d), BF16)],
        compiler_params=_params(),
    )(x, gain, win_t[0], b_gate, gq_na, gk_na, gq_sw, gk_sw, bd)


def _na_iotas():
    qc = lax.broadcasted_iota(jnp.int32, (GRID_W, LANES), 0)
    ln = lax.broadcasted_iota(jnp.int32, (GRID_W, LANES), 1)
    low = ln < GRID_W
    kc = jnp.where(low, ln, ln - GRID_W)
    diff = kc - qc + (NA_COLS - 1)
    qcs = jnp.clip(qc - NA_COLS // 2, 0, GRID_W - NA_COLS)
    inwin = (kc >= qcs) & (kc < qcs + NA_COLS)
    return diff, low, inwin


NA_RI = 2 * NA_ROWS - 1
NA_CI = 2 * NA_COLS - 1
NA_T2 = NA_RI + 1


def _rpb_rows(rpb):
    h = rpb.shape[0]
    padded = jnp.pad(rpb, ((0, 0), (1, 1), (0, GRID_W - NA_CI)))
    return jnp.concatenate([padded[:, :NA_T2], padded[:, 1:NA_T2 + 1]], axis=2).reshape(h, NA_T2, LANES)


def _rpb_from_rows(rows):
    return rows[:, 1:, :NA_CI] + rows[:, :NA_RI, GRID_W:GRID_W + NA_CI]


def rpb_expand(rows, dep, name):
    n_heads = rows.shape[0]

    def body(r_ref, dep_ref, o_ref):
        for h in range(n_heads):
            for e in range(NA_T2):
                line = jnp.broadcast_to(r_ref[h, e:e + 1, :], (GRID_W, LANES))
                o_ref[h, e] = pltpu.roll(line, LANES - (NA_COLS - 1), 1, stride=1, stride_axis=0)

    return pl.pallas_call(
        body, name=name,
        in_specs=[pl.BlockSpec(memory_space=pltpu.VMEM), pl.BlockSpec(memory_space=pltpu.VMEM)],
        out_specs=pl.BlockSpec(memory_space=pltpu.VMEM),
        out_shape=jax.ShapeDtypeStruct((n_heads, NA_T2, GRID_W, LANES), F32),
        compiler_params=pltpu.CompilerParams(vmem_limit_bytes=V7X_VMEM_LIMIT),
    )(rows, dep)


def rpb_reduce(dt2, name):
    n_heads = dt2.shape[0]
    flip = jnp.asarray(np.eye(GRID_W)[::-1], BF16)

    def body(d_ref, j_ref, o_ref):
        jm = j_ref[...]
        for h in range(n_heads):
            for e in range(NA_T2):
                dv = d_ref[h, e]
                hi = dv.astype(BF16)
                mid = (dv - hi.astype(F32)).astype(BF16)
                lo = (dv - hi.astype(F32) - mid.astype(F32)).astype(BF16)
                rev = _dot(jm, hi) + _dot(jm, mid) + _dot(jm, lo)
                back = pltpu.roll(rev, LANES + (NA_COLS - 1) - (GRID_W - 1), 1, stride=1, stride_axis=0)
                o_ref[h, e:e + 1, :] = jnp.sum(back, axis=0, keepdims=True)

    return pl.pallas_call(
        body, name=name,
        in_specs=[pl.BlockSpec(memory_space=pltpu.VMEM)] * 2,
        out_specs=pl.BlockSpec(memory_space=pltpu.VMEM),
        out_shape=jax.ShapeDtypeStruct((n_heads, NA_T2, LANES), F32),
        compiler_params=pltpu.CompilerParams(vmem_limit_bytes=V7X_VMEM_LIMIT),
    )(dt2, flip)


NA_TQ = 4
NA_TK = NA_TQ + NA_ROWS
NA_KCH = NA_TK // 2


def _na_tile_geometry(t, rows):
    r = t * NA_TQ
    kbase = jnp.clip(r - NA_ROWS // 2, 0, rows - NA_TK)
    starts = [jnp.clip(r + a - NA_ROWS // 2, 0, rows - NA_ROWS) for a in range(NA_TQ)]
    return r, kbase, starts


def _na_tile_mask(kbase, starts, low, inwin):
    half = jnp.where(low, 0, 1)
    cols = []
    for c in range(NA_KCH):
        krow = kbase + 2 * c + half
        cols.append(jnp.concatenate(
            [jnp.where(inwin & (krow >= st) & (krow < st + NA_ROWS), 0.0, NEG) for st in starts], axis=0))
    return jnp.concatenate(cols, axis=1)


def _na_tile_index(r, kbase, a, c):
    return jnp.clip(kbase + 2 * c - (r + a) + NA_ROWS, 0, NA_T2 - 1)


def _na_tile_scores(q, k, t2_ref, hh, r, kbase, madd):
    bias = jnp.concatenate(
        [jnp.concatenate([t2_ref[hh, _na_tile_index(r, kbase, a, c)] for a in range(NA_TQ)], axis=0)
         for c in range(NA_KCH)], axis=1)
    return _dotg(q, k, NT) + bias + madd


def _softmax_rows(sc):
    e = jnp.exp(sc - jnp.max(sc, axis=1, keepdims=True))
    return e * (1.0 / jnp.sum(e, axis=1, keepdims=True))


def na_fwd(qa, ka, zq, t2, name):
    s = qa.shape[0]
    rows = s // GRID_W
    n_pairs = NA_WIDTH // LANES
    v_blk0 = (2 * NA_WIDTH) // LANES

    assert rows % NA_TQ == 0 and rows >= NA_TK
    tq, tk = NA_TQ * GRID_W, NA_TK * GRID_W

    def body(q_ref, k_ref, v_ref, t2_ref, o_ref, s_scr, p_scr):
        _, low, inwin = _na_iotas()

        def tile(t, carry):
            r, kbase, starts = _na_tile_geometry(t, rows)
            madd = _na_tile_mask(kbase, starts, low, inwin)
            qr = pl.ds(pl.multiple_of(r * GRID_W, tq), tq)
            kr = pl.ds(pl.multiple_of(kbase * GRID_W, tq), tk)
            for hh in range(2):
                lanes = slice(HEAD_DIM * hh, HEAD_DIM * (hh + 1))
                s_scr[tq * hh:tq * (hh + 1), :] = _na_tile_scores(q_ref[qr, lanes], k_ref[kr, lanes], t2_ref, hh, r,
                                                                  kbase, madd)
            p_scr[...] = _softmax_rows(s_scr[...]).astype(BF16)
            for hh in range(2):
                lanes = slice(HEAD_DIM * hh, HEAD_DIM * (hh + 1))
                o_ref[qr, lanes] = _dot(p_scr[tq * hh:tq * (hh + 1), :], v_ref[kr, lanes]).astype(BF16)
            return carry

        lax.fori_loop(0, rows // NA_TQ, tile, 0)

    col = lambda off: pl.BlockSpec((s, LANES), lambda p, _o=off: (0, _o + p))
    return pl.pallas_call(
        body, name=name, grid=(n_pairs,),
        in_specs=[col(0), col(0), col(v_blk0),
                  pl.BlockSpec((2, NA_T2, GRID_W, LANES), lambda p: (p, 0, 0, 0))],
        out_specs=col(0),
        out_shape=jax.ShapeDtypeStruct((s, NA_WIDTH), BF16),
        scratch_shapes=[pltpu.VMEM((2 * tq, tk), F32), pltpu.VMEM((2 * tq, tk), BF16)],
        compiler_params=_params(),
    )(qa, ka, zq, t2)


def na_bwd(qa, ka, zq, t2, o_na, do_na, name):
    s = qa.shape[0]
    rows = s // GRID_W
    n_pairs = NA_WIDTH // LANES
    v_blk0 = (2 * NA_WIDTH) // LANES

    tq, tk = NA_TQ * GRID_W, NA_TK * GRID_W

    def body(q_ref, k_ref, v_ref, t2_ref, o_ref, do_ref, dq_ref, dk_ref, dv_ref, dt2_ref):
        _, low, inwin = _na_iotas()
        dk_ref[...] = jnp.zeros(dk_ref.shape, F32)
        dv_ref[...] = jnp.zeros(dv_ref.shape, F32)
        dt2_ref[...] = jnp.zeros(dt2_ref.shape, F32)

        def tile(t, carry):
            r, kbase, starts = _na_tile_geometry(t, rows)
            madd = _na_tile_mask(kbase, starts, low, inwin)
            qr = pl.ds(pl.multiple_of(r * GRID_W, tq), tq)
            kr = pl.ds(pl.multiple_of(kbase * GRID_W, tq), tk)
            for hh in range(2):
                lanes = slice(HEAD_DIM * hh, HEAD_DIM * (hh + 1))
                q, k, v = q_ref[qr, lanes], k_ref[kr, lanes], v_ref[kr, lanes]
                p = _softmax_rows(_na_tile_scores(q, k, t2_ref, hh, r, kbase, madd))
                do = do_ref[qr, lanes]
                delta = jnp.sum(do.astype(F32) * o_ref[qr, lanes].astype(F32), axis=1, keepdims=True)
                ds = p * (_dotg(do, v, NT) - delta)
                shared = {}
                for a in range(NA_TQ):
                    for c in range(NA_KCH):
                        shared.setdefault(2 * c - a, []).append(
                            ds[GRID_W * a:GRID_W * (a + 1), LANES * c:LANES * (c + 1)])
                for offset, parts in shared.items():
                    e = jnp.clip(offset + kbase - r + NA_ROWS, 0, NA_T2 - 1)
                    dt2_ref[hh, e] = dt2_ref[hh, e] + functools.reduce(jnp.add, parts)
                dsb = ds.astype(BF16)
                dq_ref[qr, lanes] = _dot(dsb, k)
                dk_ref[kr, lanes] = dk_ref[kr, lanes] + _dotg(dsb, q, TN)
                dv_ref[kr, lanes] = dv_ref[kr, lanes] + _dotg(p.astype(BF16), do, TN)
            return carry

        lax.fori_loop(0, rows // NA_TQ, tile, 0)

    col = lambda off: pl.BlockSpec((s, LANES), lambda p, _o=off: (0, _o + p))
    t2spec = pl.BlockSpec((2, NA_T2, GRID_W, LANES), lambda p: (p, 0, 0, 0))
    return pl.pallas_call(
        body, name=name, grid=(n_pairs,),
        in_specs=[col(0), col(0), col(v_blk0), t2spec, col(0), col(0)],
        out_specs=[col(0), col(0), col(0), t2spec],
        out_shape=[jax.ShapeDtypeStruct((s, NA_WIDTH), F32)] * 3 + [jax.ShapeDtypeStruct(t2.shape, F32)],
        compiler_params=_params(),
    )(qa, ka, zq, t2, o_na, do_na)


def _t5_bucket_map():
    rel = np.arange(3 * SW_BLOCK)[None, :] - SW_BLOCK - np.arange(SW_BLOCK)[:, None]
    nb = REL_BUCKETS // 2
    max_exact = nb // 2
    n = np.abs(rel)
    large = max_exact + (np.log(np.maximum(n, 1) / max_exact)
                         / np.log(REL_MAX_DIST / max_exact) * (nb - max_exact)).astype(np.int32)
    large = np.minimum(large, nb - 1)
    return ((rel > 0) * nb + np.where(n < max_exact, n, large)).astype(np.int32)


def t5_expand(table, bmap, dep, name):
    def body(tab_ref, bm_ref, dep_ref, o_ref):
        bm = bm_ref[...]
        for h in range(SW_HEADS):
            t = jnp.zeros(bm.shape, F32)
            for b in range(REL_BUCKETS):
                t = jnp.where(bm == b, tab_ref[b, h], t)
            o_ref[h] = t

    return pl.pallas_call(
        body, name=name,
        in_specs=[pl.BlockSpec(memory_space=pltpu.SMEM), pl.BlockSpec(memory_space=pltpu.VMEM),
                  pl.BlockSpec(memory_space=pltpu.VMEM)],
        out_specs=pl.BlockSpec(memory_space=pltpu.VMEM),
        out_shape=jax.ShapeDtypeStruct((SW_HEADS,) + bmap.shape, F32),
        compiler_params=pltpu.CompilerParams(vmem_limit_bytes=V7X_VMEM_LIMIT),
    )(table, bmap, dep)


def t5_reduce(dbias_list, bmap, name):
    n = len(dbias_list)

    def body(*refs):
        d_refs, bm_ref, o_ref = refs[:n], refs[n], refs[n + 1]
        bm = bm_ref[...]
        for h in range(SW_HEADS):
            dv = d_refs[0][h]
            for other in d_refs[1:]:
                dv = dv + other[h]
            rows = [jnp.sum(jnp.where(bm == b, dv, 0.0), axis=0, keepdims=True) for b in range(REL_BUCKETS)]
            r = jnp.concatenate(rows, axis=0)
            o_ref[h] = jnp.broadcast_to(jnp.sum(r, axis=1, keepdims=True), (REL_BUCKETS, LANES))

    return pl.pallas_call(
        body, name=name,
        in_specs=[pl.BlockSpec(memory_space=pltpu.VMEM)] * (n + 1),
        out_specs=pl.BlockSpec(memory_space=pltpu.VMEM),
        out_shape=jax.ShapeDtypeStruct((SW_HEADS, REL_BUCKETS, LANES), F32),
        compiler_params=pltpu.CompilerParams(vmem_limit_bytes=V7X_VMEM_LIMIT),
    )(*dbias_list, bmap)


def _sw_mask_iotas():
    a = lax.broadcasted_iota(jnp.int32, (SW_BLOCK, 3 * SW_BLOCK), 0)
    j = lax.broadcasted_iota(jnp.int32, (SW_BLOCK, 3 * SW_BLOCK), 1)
    inwin = jnp.abs(j - SW_BLOCK - a) <= SW_BLOCK
    return j, inwin


SW_STACK = SW_HEADS * SW_BLOCK


def _sw_softmax(sc, sk):
    m = jnp.maximum(jnp.max(sc, axis=1, keepdims=True), sk)
    e = jnp.exp(sc - m)
    es = jnp.exp(sk - m)
    inv = 1.0 / (jnp.sum(e, axis=1, keepdims=True) + es)
    return e * inv, es * inv


def _sw_prologue(k_ref, v_ref, kp, vp, sink_ref, s):
    pad = s + 2 * SW_BLOCK
    zeros = jnp.zeros((SW_BLOCK, SW_KV_WIDTH), BF16)
    kp[0:SW_BLOCK, :] = zeros
    vp[0:SW_BLOCK, :] = zeros
    kp[SW_BLOCK + s:pad, :] = zeros
    vp[SW_BLOCK + s:pad, :] = zeros
    kp[SW_BLOCK:SW_BLOCK + s, :] = k_ref[...]
    vp[SW_BLOCK:SW_BLOCK + s, :] = v_ref[...]
    return jnp.concatenate([jnp.full((SW_BLOCK, 1), sink_ref[h], F32) for h in range(SW_HEADS)], axis=0)


def sw_fwd(qs, ks, zq, t5b, sink, dep, name):
    s = qs.shape[0]
    nb = s // SW_BLOCK
    v_blk = (3 * NA_WIDTH + SW_Q_WIDTH + SW_KV_WIDTH) // LANES
    pad = s + 2 * SW_BLOCK

    def body(q_ref, k_ref, v_ref, b_ref, sink_ref, dep_ref, o_ref, kp, vp, s_scr, p_scr):
        sink_col = _sw_prologue(k_ref, v_ref, kp, vp, sink_ref, s)
        j, inwin = _sw_mask_iotas()

        def blk(n, carry):
            kpos = n * SW_BLOCK - SW_BLOCK + j
            madd = jnp.where(inwin & (kpos >= 0) & (kpos < s), 0.0, NEG)
            q0 = pl.multiple_of(n * SW_BLOCK, SW_BLOCK)
            qr, kr = pl.ds(q0, SW_BLOCK), pl.ds(q0, 3 * SW_BLOCK)
            for h in range(SW_HEADS):
                g = h // SW_REP
                s_scr[SW_BLOCK * h:SW_BLOCK * (h + 1), :] = _dotg(
                    q_ref[qr, HEAD_DIM * h:HEAD_DIM * (h + 1)], kp[kr, HEAD_DIM * g:HEAD_DIM * (g + 1)], NT) + madd
            p, _ = _sw_softmax(s_scr[...] + b_ref[...], sink_col)
            p_scr[...] = p.astype(BF16)
            for h in range(SW_HEADS):
                g = h // SW_REP
                o_ref[qr, HEAD_DIM * h:HEAD_DIM * (h + 1)] = _dot(
                    p_scr[SW_BLOCK * h:SW_BLOCK * (h + 1), :], vp[kr, HEAD_DIM * g:HEAD_DIM * (g + 1)]).astype(BF16)
            return carry

        lax.fori_loop(0, nb, blk, 0)

    return pl.pallas_call(
        body, name=name, grid=(1,),
        in_specs=[_full((s, SW_Q_WIDTH)), _full((s, SW_KV_WIDTH)),
                  pl.BlockSpec((s, SW_KV_WIDTH), lambda i: (0, v_blk)),
                  _full((SW_STACK, 3 * SW_BLOCK)), pl.BlockSpec(memory_space=pltpu.SMEM),
                  _full(dep.shape)],
        out_specs=_full((s, SW_Q_WIDTH)),
        out_shape=jax.ShapeDtypeStruct((s, SW_Q_WIDTH), BF16),
        scratch_shapes=[pltpu.VMEM((pad, SW_KV_WIDTH), BF16), pltpu.VMEM((pad, SW_KV_WIDTH), BF16),
                        pltpu.VMEM((SW_STACK, 3 * SW_BLOCK), F32), pltpu.VMEM((SW_STACK, 3 * SW_BLOCK), BF16)],
        compiler_params=_params(),
    )(qs, ks, zq, t5b, sink, dep)


def sw_bwd(qs, ks, zq, t5b, sink, o_sw, do_sw, name):
    s = qs.shape[0]
    nb = s // SW_BLOCK
    v_blk = (3 * NA_WIDTH + SW_Q_WIDTH + SW_KV_WIDTH) // LANES
    pad = s + 2 * SW_BLOCK

    def body(q_ref, k_ref, v_ref, b_ref, sink_ref, o_ref, do_ref,
             dq_ref, dk_ref, dv_ref, db_ref, dsk_ref, kp, vp, dkp, dvp, s_scr, dp_scr, ds_scr, p_scr):
        sink_col = _sw_prologue(k_ref, v_ref, kp, vp, sink_ref, s)
        dkp[...] = jnp.zeros(dkp.shape, F32)
        dvp[...] = jnp.zeros(dvp.shape, F32)
        db_ref[...] = jnp.zeros(db_ref.shape, F32)
        dsk_ref[...] = jnp.zeros(dsk_ref.shape, F32)
        j, inwin = _sw_mask_iotas()

        def blk(n, carry):
            kpos = n * SW_BLOCK - SW_BLOCK + j
            madd = jnp.where(inwin & (kpos >= 0) & (kpos < s), 0.0, NEG)
            q0 = pl.multiple_of(n * SW_BLOCK, SW_BLOCK)
            qr, kr = pl.ds(q0, SW_BLOCK), pl.ds(q0, 3 * SW_BLOCK)
            deltas = []
            for h in range(SW_HEADS):
                g = h // SW_REP
                hl, kl = slice(HEAD_DIM * h, HEAD_DIM * (h + 1)), slice(HEAD_DIM * g, HEAD_DIM * (g + 1))
                rows = slice(SW_BLOCK * h, SW_BLOCK * (h + 1))
                do = do_ref[qr, hl]
                s_scr[rows, :] = _dotg(q_ref[qr, hl], kp[kr, kl], NT) + madd
                dp_scr[rows, :] = _dotg(do, vp[kr, kl], NT)
                deltas.append(jnp.sum(do.astype(F32) * o_ref[qr, hl].astype(F32), axis=1, keepdims=True))
            delta = jnp.concatenate(deltas, axis=0)
            p, ps = _sw_softmax(s_scr[...] + b_ref[...], sink_col)
            ds = p * (dp_scr[...] - delta)
            db_ref[...] = db_ref[...] + ds
            dsk_ref[...] = dsk_ref[...] - jnp.broadcast_to(ps * delta, (SW_STACK, LANES))
            ds_scr[...] = ds.astype(BF16)
            p_scr[...] = p.astype(BF16)
            for g in range(SW_HEADS // SW_REP):
                kl = slice(HEAD_DIM * g, HEAD_DIM * (g + 1))
                k = kp[kr, kl]
                dkw = jnp.zeros((3 * SW_BLOCK, HEAD_DIM), F32)
                dvw = jnp.zeros((3 * SW_BLOCK, HEAD_DIM), F32)
                for r in range(SW_REP):
                    h = g * SW_REP + r
                    hl, rows = slice(HEAD_DIM * h, HEAD_DIM * (h + 1)), slice(SW_BLOCK * h, SW_BLOCK * (h + 1))
                    dsb = ds_scr[rows, :]
                    dq_ref[qr, hl] = _dot(dsb, k)
                    dkw = dkw + _dotg(dsb, q_ref[qr, hl], TN)
                    dvw = dvw + _dotg(p_scr[rows, :], do_ref[qr, hl], TN)
                dkp[kr, kl] = dkp[kr, kl] + dkw
                dvp[kr, kl] = dvp[kr, kl] + dvw
            return carry

        lax.fori_loop(0, nb, blk, 0)
        dk_ref[...] = dkp[SW_BLOCK:SW_BLOCK + s, :]
        dv_ref[...] = dvp[SW_BLOCK:SW_BLOCK + s, :]

    bias_spec = _full((SW_STACK, 3 * SW_BLOCK))
    return pl.pallas_call(
        body, name=name, grid=(1,),
        in_specs=[_full((s, SW_Q_WIDTH)), _full((s, SW_KV_WIDTH)),
                  pl.BlockSpec((s, SW_KV_WIDTH), lambda i: (0, v_blk)),
                  bias_spec, pl.BlockSpec(memory_space=pltpu.SMEM),
                  _full((s, SW_Q_WIDTH)), _full((s, SW_Q_WIDTH))],
        out_specs=[_full((s, SW_Q_WIDTH)), _full((s, SW_KV_WIDTH)), _full((s, SW_KV_WIDTH)), bias_spec,
                   _full((SW_STACK, LANES))],
        out_shape=[jax.ShapeDtypeStruct((s, SW_Q_WIDTH), F32), jax.ShapeDtypeStruct((s, SW_KV_WIDTH), F32),
                   jax.ShapeDtypeStruct((s, SW_KV_WIDTH), F32),
                   jax.ShapeDtypeStruct((SW_STACK, 3 * SW_BLOCK), F32),
                   jax.ShapeDtypeStruct((SW_STACK, LANES), F32)],
        scratch_shapes=[pltpu.VMEM((pad, SW_KV_WIDTH), BF16), pltpu.VMEM((pad, SW_KV_WIDTH), BF16),
                        pltpu.VMEM((pad, SW_KV_WIDTH), F32), pltpu.VMEM((pad, SW_KV_WIDTH), F32),
                        pltpu.VMEM((SW_STACK, 3 * SW_BLOCK), F32), pltpu.VMEM((SW_STACK, 3 * SW_BLOCK), F32),
                        pltpu.VMEM((SW_STACK, 3 * SW_BLOCK), BF16), pltpu.VMEM((SW_STACK, 3 * SW_BLOCK), BF16)],
        compiler_params=_params(),
    )(qs, ks, zq, t5b, sink, o_sw, do_sw)


def merge_out(x, o_na, o_sw, gt, wbna_t, wbsw_t, wout, name):
    s, d = x.shape
    tm = _row_tile(s)

    def body(x_ref, ona_ref, osw_ref, gt_ref, wna_ref, wsw_ref, wo_ref, xo_ref, ana_ref, asw_ref, mg_ref):
        a_na = _dotg(ona_ref[...], wna_ref[...], NT)
        a_sw = _dotg(osw_ref[...], wsw_ref[...], NT)
        g_na, g_sw = gt_ref[:, 0:d].astype(F32), gt_ref[:, d:2 * d].astype(F32)
        ana_ref[...] = (a_na * g_na * (1.0 - g_na)).astype(BF16)
        asw_ref[...] = (a_sw * g_sw * (1.0 - g_sw)).astype(BF16)
        merged = (g_na * a_na + g_sw * a_sw).astype(BF16)
        mg_ref[...] = merged
        xo_ref[...] = x_ref[...] + _dot(merged, wo_ref[...])

    return pl.pallas_call(
        body, name=name, grid=(s // tm,),
        in_specs=[_rows(tm, d), _rows(tm, 512), _rows(tm, 512), _rows(tm, 2 * d),
                  _mat(*wbna_t), _mat(*wbsw_t), _mat(*wout)],
        out_specs=[_rows(tm, d)] * 4,
        out_shape=[jax.ShapeDtypeStruct((s, d), F32)] + [jax.ShapeDtypeStruct((s, d), BF16)] * 3,
        compiler_params=_params(),
    )(x, o_na, o_sw, gt, wbna_t[0], wbsw_t[0], wout[0])


def mix_bwd_out(dx, gt, a_na, a_sw, wbna_t, wbsw_t, wout, dep, name):
    s, d = dx.shape
    tm = _row_tile(s)

    def body(dx_ref, gt_ref, ana_ref, asw_ref, wna_ref, wsw_ref, wo_ref, dep_ref,
             dxb_ref, dzg_ref, dana_ref, dasw_ref, dona_ref, dosw_ref, dbg_ref):
        @pl.when(pl.program_id(0) == 0)
        def _():
            dbg_ref[...] = jnp.zeros(dbg_ref.shape, F32)

        dxb = dx_ref[...].astype(BF16)
        dxb_ref[...] = dxb
        dm = _dotg(dxb, wo_ref[...], NT)
        for i, (a_ref, da_ref, w_ref, do_ref) in enumerate(
                [(ana_ref, dana_ref, wna_ref, dona_ref), (asw_ref, dasw_ref, wsw_ref, dosw_ref)]):
            gi = gt_ref[:, i * d:(i + 1) * d].astype(F32)
            da = (dm * gi).astype(BF16)
            da_ref[...] = da
            do_ref[...] = _dot(da, w_ref[...]).astype(BF16)
            dzg = dm * a_ref[...].astype(F32)
            dzg_ref[:, i * d:(i + 1) * d] = dzg.astype(BF16)
            dbg_ref[:, i * d:(i + 1) * d] = dbg_ref[:, i * d:(i + 1) * d] + jnp.sum(dzg, axis=0, keepdims=True)

    return pl.pallas_call(
        body, name=name, grid=(s // tm,),
        in_specs=[_rows(tm, d), _rows(tm, 2 * d), _rows(tm, d), _rows(tm, d),
                  _mat(*wbna_t), _mat(*wbsw_t), _mat(*wout), _full(dep.shape)],
        out_specs=[_rows(tm, d), _rows(tm, 2 * d), _rows(tm, d), _rows(tm, d), _rows(tm, 512), _rows(tm, 512),
                   _full((1, 2 * d))],
        out_shape=[jax.ShapeDtypeStruct((s, d), BF16), jax.ShapeDtypeStruct((s, 2 * d), BF16),
                   jax.ShapeDtypeStruct((s, d), BF16), jax.ShapeDtypeStruct((s, d), BF16),
                   jax.ShapeDtypeStruct((s, 512), BF16), jax.ShapeDtypeStruct((s, 512), BF16),
                   jax.ShapeDtypeStruct((1, 2 * d), F32)],
        compiler_params=_params(),
    )(dx, gt, a_na, a_sw, wbna_t[0], wbsw_t[0], wout[0], dep)


def qk_norm_bwd(dqa, dka, dva, dqs, dks, dvs, zq, dzg, gq_na, gk_na, gq_sw, gk_sw, bd, name):
    s = zq.shape[0]
    d2 = dzg.shape[1]
    n_in = QKV_WIDTH + d2
    tm = _row_tile(s)

    def body(dqa_ref, dka_ref, dva_ref, dqs_ref, dks_ref, dvs_ref, zq_ref, dzg_ref,
             gqa_ref, gka_ref, gqs_ref, gks_ref, bd_ref, dz_ref, dgqa_ref, dgka_ref, dgqs_ref, dgks_ref):
        @pl.when(pl.program_id(0) == 0)
        def _():
            for r in (dgqa_ref, dgka_ref, dgqs_ref, dgks_ref):
                r[...] = jnp.zeros(r.shape, F32)

        bd512 = bd_ref[...]
        bd128 = bd_ref[0:SW_KV_WIDTH, 0:SW_KV_WIDTH]

        def one(c0, c1, dy_ref, g_ref, dg_ref, bdm, scale):
            z = zq_ref[:, c0:c1].astype(F32)
            r = lax.rsqrt(_group_mean(z * z, bdm) + EPS)
            zh = z * r
            dy = dy_ref[...] * scale
            dyg = dy * g_ref[...]
            dz = r * (dyg - zh * _group_mean(dyg * zh, bdm))
            dz_ref[:, c0:c1] = dz.astype(BF16)
            dg_ref[...] = dg_ref[...] + jnp.sum(dy * zh, axis=0, keepdims=True)

        one(0, 512, dqa_ref, gqa_ref, dgqa_ref, bd512, SCALE)
        one(512, 1024, dka_ref, gka_ref, dgka_ref, bd512, 1.0)
        dz_ref[:, 1024:1536] = dva_ref[...].astype(BF16)
        one(1536, 2048, dqs_ref, gqs_ref, dgqs_ref, bd512, SCALE)
        one(2048, 2176, dks_ref, gks_ref, dgks_ref, bd128, 1.0)
        dz_ref[:, 2176:2304] = dvs_ref[...].astype(BF16)
        dz_ref[:, QKV_WIDTH:n_in] = dzg_ref[...]

    return pl.pallas_call(
        body, name=name, grid=(s // tm,),
        in_specs=[_rows(tm, 512), _rows(tm, 512), _rows(tm, 512), _rows(tm, 512), _rows(tm, 128), _rows(tm, 128),
                  _rows(tm, QKV_WIDTH), _rows(tm, d2),
                  _full((1, 512)), _full((1, 512)), _full((1, 512)), _full((1, 128)), _full((512, 512))],
        out_specs=[_rows(tm, n_in), _full((1, 512)), _full((1, 512)), _full((1, 512)), _full((1, 128))],
        out_shape=[jax.ShapeDtypeStruct((s, n_in), BF16)] + [jax.ShapeDtypeStruct((1, 512), F32)] * 3
                  + [jax.ShapeDtypeStruct((1, 128), F32)],
        compiler_params=_params(),
    )(dqa, dka, dva, dqs, dks, dvs, zq, dzg, gq_na, gk_na, gq_sw, gk_sw, bd)


def ffn_bwd_act(dx, wd, hg, hu, name):
    s, d = dx.shape
    f = wd[0].shape[1]
    tm = _row_tile(s)
    fc = _col_chunk(f)

    def body(dx_ref, w_ref, hg_ref, hu_ref, dxb_ref, dhg_ref, dhu_ref):
        dxv = dx_ref[...]
        dxb_ref[...] = dxv.astype(BF16)
        half = (0.5 * dxv).astype(BF16)
        for c0 in range(0, f, fc):
            dact = _dotg(half, w_ref[c0:c0 + fc, :], NT)
            dhu_ref[:, c0:c0 + fc] = (dact * hu_ref[:, c0:c0 + fc].astype(F32)).astype(BF16)
            dhg_ref[:, c0:c0 + fc] = (dact * hg_ref[:, c0:c0 + fc].astype(F32)).astype(BF16)

    return pl.pallas_call(
        body, name=name, grid=(s // tm,),
        in_specs=[_rows(tm, d), _mat(*wd), _rows(tm, f), _rows(tm, f)],
        out_specs=[_rows(tm, d), _rows(tm, f), _rows(tm, f)],
        out_shape=[jax.ShapeDtypeStruct((s, d), BF16), jax.ShapeDtypeStruct((s, f), BF16),
                   jax.ShapeDtypeStruct((s, f), BF16)],
        compiler_params=_params(),
    )(dx, wd[0], hg, hu)


def proj_bwd_norm(acts, weights, x, gain, dx, dep, name):
    s, d = x.shape
    tm = _row_tile(s)
    n = len(acts)

    def body(*refs):
        a_refs, w_refs = refs[:n], refs[n:2 * n]
        x_ref, g_ref, dx_ref, _, o_ref, dg_ref = refs[2 * n:]

        @pl.when(pl.program_id(0) == 0)
        def _():
            dg_ref[...] = jnp.zeros(dg_ref.shape, F32)

        dxn = _dot(a_refs[0][...], w_refs[0][...])
        for a_ref, w_ref in zip(a_refs[1:], w_refs[1:]):
            dxn = dxn + _dot(a_ref[...], w_ref[...])
        xv = x_ref[...]
        r = _rstd(xv)
        xh = xv * r
        dxh = dxn * g_ref[...]
        o_ref[...] = dx_ref[...] + r * (dxh - xh * jnp.mean(dxh * xh, axis=-1, keepdims=True))
        dg_ref[...] = dg_ref[...] + jnp.sum(dxn * xh, axis=0, keepdims=True)

    return pl.pallas_call(
        body, name=name, grid=(s // tm,),
        in_specs=[_rows(tm, a.shape[1]) for a in acts] + [_mat(*w) for w in weights]
                 + [_rows(tm, d), _full((1, d)), _rows(tm, d), _full(dep.shape)],
        out_specs=[_rows(tm, d), _full((1, d))],
        out_shape=[jax.ShapeDtypeStruct((s, d), F32), jax.ShapeDtypeStruct((1, d), F32)],
        compiler_params=_params(),
    )(*acts, *[w[0] for w in weights], x, gain, dx, dep)


def tn_matmul(products, name):
    s, n = products[0][0].shape
    tn = _tn_tile(n) if len(products) == 1 else _col_chunk(n)
    rhs = []
    for _, b, _ in products:
        if not any(b is seen for seen in rhs):
            rhs.append(b)
    which = [next(i for i, seen in enumerate(rhs) if b is seen) for _, b, _ in products]
    npr, nr = len(products), len(rhs)

    def body(*refs):
        a_refs, b_refs, o_refs = refs[:npr], refs[npr:npr + nr], refs[npr + nr:]
        for i, (_, _, scale) in enumerate(products):
            o_refs[i][...] = (scale * _dotg(a_refs[i][...], b_refs[which[i]][...], TN)).astype(BF16)

    return pl.pallas_call(
        body, name=name, grid=(n // tn,),
        in_specs=[pl.BlockSpec((s, tn), lambda i: (0, i))] * npr
                 + [pl.BlockSpec(b.shape, lambda i: (0, 0), pipeline_mode=ONCE) for b in rhs],
        out_specs=[pl.BlockSpec((tn, b.shape[1]), lambda i: (i, 0)) for _, b, _ in products],
        out_shape=[jax.ShapeDtypeStruct((n, b.shape[1]), BF16) for _, b, _ in products],
        compiler_params=_params(),
    )(*[a for a, _, _ in products], *rhs)


def _mesh_pos():
    return lax.axis_index("x"), lax.axis_index("y"), lax.axis_index("c")


def _peers():
    x, y, c = _mesh_pos()
    peers = []
    for rel in range(1, N_DEV):
        peers.append((1 - x if rel & 4 else x, 1 - y if rel & 2 else y, 1 - c if rel & 1 else c))
    return 4 * x + 2 * y + c, peers


HBM_SPEC = pl.BlockSpec(memory_space=pltpu.HBM)
SEM_SPEC = pl.BlockSpec(memory_space=pltpu.SEMAPHORE)


def _split_call(body, name, thru, n_sems, extra=(), with_token=True):
    hbm = lambda t: pltpu.with_memory_space_constraint(t, pltpu.HBM)
    effect = pltpu.CompilerParams(has_side_effects=pltpu.SideEffectType.DATAFLOW_SIDE_EFFECTING)
    nt = len(thru)
    thru_shapes = [pltpu.HBM(t.shape, t.dtype) for t in thru]
    if with_token:
        (after,) = extra
        outs = pl.pallas_call(
            body, name=name, in_specs=[HBM_SPEC] * nt + [pl.BlockSpec(memory_space=pl.ANY)],
            out_specs=[SEM_SPEC] * len(n_sems) + [HBM_SPEC] * nt + [pl.BlockSpec(memory_space=pltpu.VMEM)],
            out_shape=[pltpu.SemaphoreType.DMA((k,)) for k in n_sems] + thru_shapes
                      + [jax.ShapeDtypeStruct((8, LANES), F32)],
            input_output_aliases={i: len(n_sems) + i for i in range(nt)}, compiler_params=effect,
        )(*[hbm(t) for t in thru], after)
        return outs[:len(n_sems)], outs[len(n_sems):-1], outs[-1]
    return pl.pallas_call(
        body, name=name,
        in_specs=[HBM_SPEC] * nt + [SEM_SPEC] * len(n_sems) + [pl.BlockSpec(memory_space=pl.ANY)],
        out_specs=[HBM_SPEC] * nt, out_shape=thru_shapes,
        input_output_aliases={i: i for i in range(nt)}, compiler_params=effect,
    )(*thru, *extra)


def _gather_targets():
    x, y, c = _mesh_pos()
    return 4 * x + 2 * y + c, [(x, y, 1 - c), (1 - x, y, c), (x, 1 - y, c), (1 - x, 1 - y, c)]


def gather_start(shards, after, name):
    n = len(shards)
    zones = [lax.empty((w.shape[0], N_DEV) + w.shape[1:], w.dtype) for w in shards]

    def body(*refs):
        ins, zs = refs[:n], refs[n:2 * n]
        send_sems, recv_sems, local_sems = refs[2 * n + 1:2 * n + 4]
        token = refs[-1]
        me, targets = _gather_targets()
        for a in range(n):
            pltpu.make_async_copy(ins[a], zs[a].at[:, me], local_sems.at[a]).start()
            for k, to in enumerate(targets):
                pltpu.make_async_remote_copy(
                    src_ref=ins[a], dst_ref=zs[a].at[:, me], send_sem=send_sems.at[4 * a + k],
                    recv_sem=recv_sems.at[4 * a + k], device_id=to, device_id_type=MESH).start()
        token[...] = jnp.zeros(token.shape, F32)

    sems, thru, token = _split_call(body, name, list(shards) + zones, (4 * n, 4 * n, n), extra=(after,))
    return (sems, thru, n), token


def gather_wait(started, after, name):
    sems, thru, n = started

    def body(*refs):
        zs = refs[n:2 * n]
        send_sems, recv_sems, local_sems = refs[2 * n:2 * n + 3]
        _, targets = _gather_targets()
        for a in range(n):
            for k, to in enumerate(targets):
                cp = pltpu.make_async_remote_copy(
                    src_ref=zs[a].at[:, 0], dst_ref=zs[a].at[:, 0], send_sem=send_sems.at[4 * a + k],
                    recv_sem=recv_sems.at[4 * a + k], device_id=to, device_id_type=MESH)
                cp.wait_send()
                cp.wait_recv()
            pltpu.make_async_copy(zs[a].at[:, 0], zs[a].at[:, 0], local_sems.at[a]).wait()

    return _split_call(body, name, thru, (4 * n, 4 * n, n), extra=(*sems, after), with_token=False)[n:]


def forward_start(zones, after, name):
    n = len(zones)

    def body(*refs):
        zs = refs[:n]
        send_sems, recv_sems = refs[n + 1:n + 3]
        token = refs[-1]
        x, y, c = _mesh_pos()
        for a in range(n):
            for j, chip in enumerate([(1 - x, y), (x, 1 - y), (1 - x, 1 - y)]):
                blk = zs[a].at[:, 4 * chip[0] + 2 * chip[1] + c]
                pltpu.make_async_remote_copy(
                    src_ref=blk, dst_ref=blk, send_sem=send_sems.at[3 * a + j], recv_sem=recv_sems.at[3 * a + j],
                    device_id=(x, y, 1 - c), device_id_type=MESH).start()
        token[...] = jnp.zeros(token.shape, F32)

    sems, thru, token = _split_call(body, name, list(zones), (3 * n, 3 * n), extra=(after,))
    return (sems, thru, n), token


def forward_wait(started, after, name):
    sems, thru, n = started

    def body(*refs):
        zs = refs[:n]
        send_sems, recv_sems = refs[n:n + 2]
        x, y, c = _mesh_pos()
        for a in range(n):
            for j in range(3):
                cp = pltpu.make_async_remote_copy(
                    src_ref=zs[a].at[:, 0], dst_ref=zs[a].at[:, 0], send_sem=send_sems.at[3 * a + j],
                    recv_sem=recv_sems.at[3 * a + j], device_id=(x, y, 1 - c), device_id_type=MESH)
                cp.wait_send()
                cp.wait_recv()

    return _split_call(body, name, thru, (3 * n, 3 * n), extra=(*sems, after), with_token=False)


def scatter_start(groups, name):
    n = len(groups)
    flat = [g for grp in groups for g in grp]
    nf = len(flat)
    offs = np.cumsum([0] + [len(grp) for grp in groups])
    lands = [lax.empty((N_DEV, len(grp)) + grp[0].shape[1:], grp[0].dtype) for grp in groups]

    def body(*refs):
        ins, zones = refs[:nf], refs[nf:nf + n]
        send_sems, recv_sems, local_sems = refs[nf + n:nf + n + 3]
        token = refs[-1]
        me, peers = _peers()
        for a in range(n):
            for w in range(len(groups[a])):
                pltpu.make_async_copy(ins[offs[a] + w].at[me], zones[a].at[me, w], local_sems.at[a]).start()
        for k, peer in enumerate(peers):
            p_id = 4 * peer[0] + 2 * peer[1] + peer[2]
            for a in range(n):
                for w in range(len(groups[a])):
                    pltpu.make_async_remote_copy(
                        src_ref=ins[offs[a] + w].at[p_id], dst_ref=zones[a].at[me, w],
                        send_sem=send_sems.at[7 * a + k], recv_sem=recv_sems.at[7 * a + k],
                        device_id=peer, device_id_type=MESH).start()
        token[...] = jnp.zeros(token.shape, F32)

    hbm = lambda t: pltpu.with_memory_space_constraint(t, pltpu.HBM)
    outs = pl.pallas_call(
        body, name=name,
        in_specs=[HBM_SPEC] * (nf + n),
        out_specs=[SEM_SPEC] * 3 + [HBM_SPEC] * (nf + n) + [pl.BlockSpec(memory_space=pltpu.VMEM)],
        out_shape=[pltpu.SemaphoreType.DMA((7 * n,)), pltpu.SemaphoreType.DMA((7 * n,)), pltpu.SemaphoreType.DMA((n,))]
                  + [pltpu.HBM(t.shape, t.dtype) for t in flat + lands]
                  + [jax.ShapeDtypeStruct((8, LANES), F32)],
        input_output_aliases={i: 3 + i for i in range(nf + n)},
        compiler_params=pltpu.CompilerParams(has_side_effects=pltpu.SideEffectType.DATAFLOW_SIDE_EFFECTING),
    )(*[hbm(t) for t in flat], *[hbm(t) for t in lands])
    sems, thru, token = outs[:3], outs[3:3 + nf + n], outs[-1]
    return (sems, thru, [len(grp) for grp in groups]), token


def scatter_wait(started, after, name):
    (send_sems, recv_sems, local_sems), thru, sizes = started
    n = len(sizes)
    nf = len(thru) - n

    def body(*refs):
        zones = refs[nf:nf + n]
        s_sems, r_sems, l_sems = refs[nf + n:nf + n + 3]
        me, peers = _peers()
        for a in range(n):
            for k, peer in enumerate(peers):
                cp = pltpu.make_async_remote_copy(
                    src_ref=zones[a].at[0], dst_ref=zones[a].at[0],
                    send_sem=s_sems.at[7 * a + k], recv_sem=r_sems.at[7 * a + k], device_id=peer,
                    device_id_type=MESH)
                cp.wait_send()
                cp.wait_recv()
            pltpu.make_async_copy(zones[a].at[0], zones[a].at[0], l_sems.at[a]).wait()

    outs = pl.pallas_call(
        body, name=name,
        in_specs=[HBM_SPEC] * (nf + n) + [SEM_SPEC] * 3 + [pl.BlockSpec(memory_space=pl.ANY)],
        out_specs=[HBM_SPEC] * (nf + n),
        out_shape=[pltpu.HBM(t.shape, t.dtype) for t in thru],
        input_output_aliases={i: i for i in range(nf + n)},
        compiler_params=pltpu.CompilerParams(has_side_effects=pltpu.SideEffectType.DATAFLOW_SIDE_EFFECTING),
    )(*thru, send_sems, recv_sems, local_sems, after)
    return outs[nf:]


def pair_start(grads, after, name):
    nw = len(grads)
    land = lax.empty((4, nw) + grads[0].shape[1:], grads[0].dtype)

    def body(*refs):
        ins, zone = refs[:nw], refs[nw]
        send_sems, recv_sems = refs[nw + 2:nw + 4]
        x, y, c = _mesh_pos()
        for j in range(4):
            for w in range(nw):
                pltpu.make_async_remote_copy(
                    src_ref=ins[w].at[2 * j + (1 - c)], dst_ref=zone.at[j, w], send_sem=send_sems.at[0],
                    recv_sem=recv_sems.at[0], device_id=(x, y, 1 - c), device_id_type=MESH).start()
        refs[-1][...] = jnp.zeros(refs[-1].shape, F32)

    sems, thru, token = _split_call(body, name, list(grads) + [land], (1, 1), extra=(after,))
    return (sems, thru, nw), token


def pair_wait(started, after, name):
    sems, thru, nw = started

    def body(*refs):
        zone = refs[nw]
        send_sems, recv_sems = refs[nw + 1:nw + 3]
        x, y, c = _mesh_pos()
        cp = pltpu.make_async_remote_copy(src_ref=zone, dst_ref=zone, send_sem=send_sems.at[0],
                                          recv_sem=recv_sems.at[0], device_id=(x, y, 1 - c), device_id_type=MESH)
        cp.wait_send()
        cp.wait_recv()

    outs = _split_call(body, name, thru, (1, 1), extra=(*sems, after), with_token=False)
    return outs[:nw], outs[nw]


def pair_sum(grads, land, name):
    nw = len(grads)
    _, r, c_dim = grads[0].shape

    def body(*refs):
        g_refs, l_ref, o_ref = refs[:nw], refs[nw], refs[nw + 1]
        core = lax.axis_index("c")
        for w in range(nw):
            o_ref[0, w] = (g_refs[w][0, core].astype(F32) + l_ref[0, w].astype(F32)).astype(BF16)

    return pl.pallas_call(
        body, name=name, grid=(4,),
        in_specs=[pl.BlockSpec((1, 2, r, c_dim), lambda j: (j, 0, 0, 0))] * nw
                 + [pl.BlockSpec((1, nw, r, c_dim), lambda j: (j, 0, 0, 0))],
        out_specs=pl.BlockSpec((1, nw, r, c_dim), lambda j: (j, 0, 0, 0)),
        out_shape=jax.ShapeDtypeStruct((4, nw, r, c_dim), BF16),
        compiler_params=_params(),
    )(*[g.reshape(4, 2, r, c_dim) for g in grads], land)


def _other_chips():
    x, y, c = _mesh_pos()
    chips = []
    for rel in range(1, 4):
        px, py = (1 - x if rel & 2 else x), (1 - y if rel & 1 else y)
        chips.append((px, py, 2 * px + py))
    return 2 * x + y, c, chips


def chip_start(pair_sums, after, name):
    land = lax.empty(pair_sums.shape, pair_sums.dtype)

    def body(*refs):
        h_ref, zone = refs[0], refs[1]
        send_sems, recv_sems, local_sem = refs[3:6]
        mine, c, chips = _other_chips()
        pltpu.make_async_copy(h_ref.at[mine], zone.at[mine], local_sem.at[0]).start()
        for k, (px, py, j) in enumerate(chips):
            pltpu.make_async_remote_copy(
                src_ref=h_ref.at[j], dst_ref=zone.at[mine], send_sem=send_sems.at[k], recv_sem=recv_sems.at[k],
                device_id=(px, py, c), device_id_type=MESH).start()
        refs[-1][...] = jnp.zeros(refs[-1].shape, F32)

    sems, thru, token = _split_call(body, name, [pair_sums, land], (3, 3, 1), extra=(after,))
    return (sems, thru), token


def chip_wait(started, after, name):
    sems, thru = started

    def body(*refs):
        zone = refs[1]
        send_sems, recv_sems, local_sem = refs[2:5]
        _, c, chips = _other_chips()
        for k, (px, py, _) in enumerate(chips):
            cp = pltpu.make_async_remote_copy(
                src_ref=zone.at[0], dst_ref=zone.at[0], send_sem=send_sems.at[k], recv_sem=recv_sems.at[k],
                device_id=(px, py, c), device_id_type=MESH)
            cp.wait_send()
            cp.wait_recv()
        pltpu.make_async_copy(zone.at[0], zone.at[0], local_sem.at[0]).wait()

    return _split_call(body, name, thru, (3, 3, 1), extra=(*sems, after), with_token=False)[1]


def share_small(parts, after):
    n = len(parts)

    def body(*refs):
        ins, outs = refs[:n], refs[n + 1:2 * n + 1]
        send_sems, recv_sems, local_sems = refs[2 * n + 1:]
        me, peers = _peers()
        copies = []
        for i in range(n):
            copies.append(pltpu.make_async_copy(ins[i], outs[i].at[me], local_sems.at[i]))
            copies += [pltpu.make_async_remote_copy(
                src_ref=ins[i], dst_ref=outs[i].at[me], send_sem=send_sems.at[7 * i + k],
                recv_sem=recv_sems.at[7 * i + k], device_id=peer, device_id_type=MESH)
                for k, peer in enumerate(peers)]
        for cp in copies:
            cp.start()
        for cp in copies:
            cp.wait()

    vm = pl.BlockSpec(memory_space=pltpu.VMEM)
    return pl.pallas_call(
        body, name="share_small", in_specs=[vm] * n + [pl.BlockSpec(memory_space=pl.ANY)], out_specs=[vm] * n,
        out_shape=[jax.ShapeDtypeStruct((N_DEV,) + p.shape, p.dtype) for p in parts],
        scratch_shapes=[pltpu.SemaphoreType.DMA((7 * n,)), pltpu.SemaphoreType.DMA((7 * n,)),
                        pltpu.SemaphoreType.DMA((n,))],
    )(*parts, after)


def _adamw_math(w, g, m, v):
    m = ADAM_B1 * m + (1.0 - ADAM_B1) * g
    v = ADAM_B2 * v + (1.0 - ADAM_B2) * (g * g)
    m_hat = m / (1.0 - ADAM_B1 ** ADAM_STEP)
    v_hat = v / (1.0 - ADAM_B2 ** ADAM_STEP)
    delta = -ADAM_LR * (m_hat / (jnp.sqrt(v_hat) + ADAM_EPS) + ADAM_WD * w)
    return delta, m, v


def adamw_layer(zone, w_idx, layer, w, m, v, prev, after, name):
    n_src, _, r, c = zone.shape
    depth = w.shape[0]
    if prev is None:
        prev = tuple(lax.empty((depth, r, c), F32) for _ in range(4))
    tr = r // 2 if r % 16 == 0 else r

    def body(z_ref, w_ref, m_ref, v_ref, *rest):
        g_ref, d_ref, mo_ref, vo_ref = rest[5:]
        g = z_ref[0].astype(F32)
        for src in range(1, n_src):
            g = g + z_ref[src].astype(F32)
        g_ref[...] = g
        d_ref[...], mo_ref[...], vo_ref[...] = _adamw_math(w_ref[...], g, m_ref[...], v_ref[...])

    rows = pl.BlockSpec((None, tr, c), lambda i: (layer, i, 0))
    anywhere = pl.BlockSpec(memory_space=pl.ANY)
    return pl.pallas_call(
        body, name=name, grid=(r // tr,),
        in_specs=[pl.BlockSpec((n_src, None, tr, c), lambda i: (0, w_idx, i, 0)), rows, rows, rows]
                 + [anywhere] * 5,
        out_specs=[rows] * 4,
        out_shape=[jax.ShapeDtypeStruct((depth, r, c), F32)] * 4,
        input_output_aliases={4 + k: k for k in range(4)},
        compiler_params=_params(),
    )(zone, w, m, v, *prev, after)


def adamw_small(ws, recvs, ms, vs, name):
    n = len(ws)

    def body(*refs):
        w_refs, r_refs, m_refs, v_refs = (refs[i * n:(i + 1) * n] for i in range(4))
        g_refs, d_refs, mo_refs, vo_refs = (refs[(4 + i) * n:(5 + i) * n] for i in range(4))
        for i in range(n):
            g = r_refs[i][0]
            for src in range(1, N_DEV):
                g = g + r_refs[i][src]
            g_refs[i][...] = g
            d_refs[i][...], mo_refs[i][...], vo_refs[i][...] = _adamw_math(w_refs[i][...], g, m_refs[i][...],
                                                                            v_refs[i][...])

    vm = pl.BlockSpec(memory_space=pltpu.VMEM)
    outs = pl.pallas_call(
        body, name=name, in_specs=[vm] * (4 * n), out_specs=[vm] * (4 * n),
        out_shape=[jax.ShapeDtypeStruct(w.shape, F32) for w in ws] * 4,
        compiler_params=pltpu.CompilerParams(vmem_limit_bytes=V7X_VMEM_LIMIT),
    )(*ws, *recvs, *ms, *vs)
    return [outs[i * n:(i + 1) * n] for i in range(4)]


SMALL_NAMES = ("ffn1_norm", "mix_norm", "ffn2_norm", "b_gate", "na_q_norm", "na_k_norm", "sw_q_norm", "sw_k_norm",
               "na_rpb", "sw_sink", "t5_rel_table")


def kernel(x, ffn1_norm, ffn1_w_gate, ffn1_w_up, ffn1_w_down, mix_norm, w_in, b_gate, na_q_norm, na_k_norm, na_rpb, sw_q_norm, sw_k_norm, sw_sink, t5_rel_table, w_branch_na, w_branch_sw, w_out, ffn2_norm, ffn2_w_gate, ffn2_w_up, ffn2_w_down, loss_target, m_ffn1_norm, m_ffn1_w_gate, m_ffn1_w_up, m_ffn1_w_down, m_mix_norm, m_w_in, m_b_gate, m_na_q_norm, m_na_k_norm, m_na_rpb, m_sw_q_norm, m_sw_k_norm, m_sw_sink, m_t5_rel_table, m_w_branch_na, m_w_branch_sw, m_w_out, m_ffn2_norm, m_ffn2_w_gate, m_ffn2_w_up, m_ffn2_w_down, v_ffn1_norm, v_ffn1_w_gate, v_ffn1_w_up, v_ffn1_w_down, v_mix_norm, v_w_in, v_b_gate, v_na_q_norm, v_na_k_norm, v_na_rpb, v_sw_q_norm, v_sw_k_norm, v_sw_sink, v_t5_rel_table, v_w_branch_na, v_w_branch_sw, v_w_out, v_ffn2_norm, v_ffn2_w_gate, v_ffn2_w_up, v_ffn2_w_down):
    weights = dict(ffn1_norm=ffn1_norm, ffn1_w_gate=ffn1_w_gate, ffn1_w_up=ffn1_w_up, ffn1_w_down=ffn1_w_down,
                   mix_norm=mix_norm, w_in=w_in, b_gate=b_gate, na_q_norm=na_q_norm, na_k_norm=na_k_norm,
                   na_rpb=na_rpb, sw_q_norm=sw_q_norm, sw_k_norm=sw_k_norm, sw_sink=sw_sink,
                   t5_rel_table=t5_rel_table, w_branch_na=w_branch_na, w_branch_sw=w_branch_sw, w_out=w_out,
                   ffn2_norm=ffn2_norm, ffn2_w_gate=ffn2_w_gate, ffn2_w_up=ffn2_w_up, ffn2_w_down=ffn2_w_down)
    mom_m = dict(ffn1_norm=m_ffn1_norm, ffn1_w_gate=m_ffn1_w_gate, ffn1_w_up=m_ffn1_w_up, ffn1_w_down=m_ffn1_w_down,
                 mix_norm=m_mix_norm, w_in=m_w_in, b_gate=m_b_gate, na_q_norm=m_na_q_norm, na_k_norm=m_na_k_norm,
                 na_rpb=m_na_rpb, sw_q_norm=m_sw_q_norm, sw_k_norm=m_sw_k_norm, sw_sink=m_sw_sink,
                 t5_rel_table=m_t5_rel_table, w_branch_na=m_w_branch_na, w_branch_sw=m_w_branch_sw, w_out=m_w_out,
                 ffn2_norm=m_ffn2_norm, ffn2_w_gate=m_ffn2_w_gate, ffn2_w_up=m_ffn2_w_up, ffn2_w_down=m_ffn2_w_down)
    mom_v = dict(ffn1_norm=v_ffn1_norm, ffn1_w_gate=v_ffn1_w_gate, ffn1_w_up=v_ffn1_w_up, ffn1_w_down=v_ffn1_w_down,
                 mix_norm=v_mix_norm, w_in=v_w_in, b_gate=v_b_gate, na_q_norm=v_na_q_norm, na_k_norm=v_na_k_norm,
                 na_rpb=v_na_rpb, sw_q_norm=v_sw_q_norm, sw_k_norm=v_sw_k_norm, sw_sink=v_sw_sink,
                 t5_rel_table=v_t5_rel_table, w_branch_na=v_w_branch_na, w_branch_sw=v_w_branch_sw, w_out=v_w_out,
                 ffn2_norm=v_ffn2_norm, ffn2_w_gate=v_ffn2_w_gate, ffn2_w_up=v_ffn2_w_up, ffn2_w_down=v_ffn2_w_down)
    order = list(weights)

    depth = ffn1_norm.shape[0]
    s, d = x.shape[1], x.shape[2]
    xs = x[0]
    tr = lambda w: jnp.swapaxes(w, -1, -2)

    merge = lambda t: t.reshape(t.shape[0], N_DEV * t.shape[2], t.shape[3])
    no_dep = jnp.zeros((8, LANES), F32)

    def shards_of(kind, l):
        stack = lambda *ws: jnp.stack(ws).astype(BF16)
        if kind == "ffn1":
            return [stack(tr(ffn1_w_gate[l]), tr(ffn1_w_up[l]), ffn1_w_down[l])]
        if kind == "win":
            return [stack(tr(w_in[l]))]
        return [stack(tr(ffn2_w_gate[l]), tr(ffn2_w_up[l]), ffn2_w_down[l]), stack(w_out[l]),
                stack(tr(w_branch_na[l]), tr(w_branch_sw[l]))]

    def start(kind, l, after):
        return gather_start(shards_of(kind, l), after, f"gather_{kind}_{l}")

    def arrive(started, kind, l, after):
        zones = gather_wait(started, after, f"gather_{kind}_{l}_wait")
        return forward_start(zones, no_dep, f"forward_{kind}_{l}")

    def finish(fwd, kind, l, after):
        return [merge(z) for z in forward_wait(fwd, after, f"forward_{kind}_{l}_wait")]

    bd = jnp.asarray(np.kron(np.eye(NA_WIDTH // HEAD_DIM), np.full((HEAD_DIM, HEAD_DIM), 1.0 / HEAD_DIM)), BF16)
    bmap = jnp.asarray(_t5_bucket_map())
    tile8 = lambda g: jnp.tile(g, NA_WIDTH // HEAD_DIM).reshape(1, NA_WIDTH)
    tile2 = lambda g: jnp.tile(g, SW_KV_WIDTH // HEAD_DIM).reshape(1, SW_KV_WIDTH)

    st_first, tok = start("ffn1", 0, no_dep)
    t5b = t5_expand(t5_rel_table, bmap, tok, "t5_expand").reshape(SW_STACK, 3 * SW_BLOCK)
    t2_tables = [rpb_expand(_rpb_rows(na_rpb[l]), tok, f"rpb_expand_{l}") for l in range(depth)]
    fwd, _ = arrive(st_first, "ffn1", 0, t2_tables[-1])
    st_win, dep = start("win", 0, t5b)
    (first,) = finish(fwd, "ffn1", 0, dep)

    saved = []
    layer_w = {0: dict(wg1=(first, 0), wu1=(first, 1), wd1=(first, 2))}
    cur = xs
    for l in range(depth):
        sv = {}
        lw = layer_w[l]
        sv["x0"] = cur
        cur, sv["xn1"], sv["hg1"], sv["hu1"], sv["act1"] = ffn_both(
            cur, ffn1_norm[l][None], lw["wg1"], lw["wu1"], lw["wd1"], dep, f"ffn1_{l}")
        sv["x1"] = cur
        fwd, _ = arrive(st_win, "win", l, cur)
        st_rest, tok = start("rest", l, cur)
        (zb,) = finish(fwd, "win", l, tok)
        lw["win"] = (zb, 0)
        sv["gains"] = (tile8(na_q_norm[l]), tile8(na_k_norm[l]), tile8(sw_q_norm[l]), tile2(sw_k_norm[l]))
        sv["hn"], sv["zq"], sv["qa"], sv["ka"], sv["qs"], sv["ks"], sv["gt"] = mix_in(
            cur, mix_norm[l][None], lw["win"], b_gate[l][None], *sv["gains"], bd, f"mix_in_{l}")
        sv["t2"] = t2_tables[l]
        sv["o_na"] = na_fwd(sv["qa"], sv["ka"], sv["zq"], sv["t2"], f"na_fwd_{l}")
        dep = no_dep
        if l + 1 < depth:
            st_ffn1, dep = start("ffn1", l + 1, sv["o_na"])
        sv["o_sw"] = sw_fwd(sv["qs"], sv["ks"], sv["zq"], t5b, sw_sink[l], dep, f"sw_fwd_{l}")
        fwd, tok = arrive(st_rest, "rest", l, sv["o_sw"])
        za, zc, zd = finish(fwd, "rest", l, tok)
        lw.update(wg2=(za, 0), wu2=(za, 1), wd2=(za, 2), wout=(zc, 0), wna=(zd, 0), wsw=(zd, 1))
        cur, sv["a_na"], sv["a_sw"], sv["merged"] = merge_out(
            cur, sv["o_na"], sv["o_sw"], sv["gt"], lw["wna"], lw["wsw"], lw["wout"], f"merge_out_{l}")
        sv["x2"] = cur
        dep = no_dep
        if l + 1 < depth:
            st_win, dep = start("win", l + 1, cur)
        sv["xn2"], sv["hg2"], sv["hu2"], sv["act2"] = ffn_up(cur, ffn2_norm[l][None], lw["wg2"], lw["wu2"], dep,
                                                             f"ffn2_up_{l}")
        dep = no_dep
        if l + 1 < depth:
            fwd, dep = arrive(st_ffn1, "ffn1", l + 1, sv["act2"])
        if l + 1 < depth:
            cur = ffn_down(cur, sv["act2"], lw["wd2"], dep, f"ffn2_down_{l}")
            (za,) = finish(fwd, "ffn1", l + 1, cur)
            layer_w[l + 1] = dict(wg1=(za, 0), wu1=(za, 1), wd1=(za, 2))
        else:
            dx, loss_acc = ffn_down(cur, sv["act2"], lw["wd2"], dep, f"ffn2_down_{l}", target=loss_target[0])
        dep = no_dep
        saved.append(sv)

    loss = lax.psum(jnp.sum(loss_acc) * (0.5 / d), ("x", "y", "c"))

    split = lambda t: t.reshape(N_DEV, t.shape[0] // N_DEV, t.shape[1])
    pending = {}
    last_key = "ffn1_0"
    two_level = {last_key}
    small = {k: [None] * depth for k in SMALL_NAMES if k != "t5_rel_table"}
    dbias_sw = []
    for l in reversed(range(depth)):
        sv = saved[l]
        lw = layer_w[l]
        wg1, wu1, wd1, wg2, wu2, wd2 = (lw[k] for k in ("wg1", "wu1", "wd1", "wg2", "wu2", "wd2"))
        win_t, wout_l, wna_t, wsw_t = lw["win"], lw["wout"], lw["wna"], lw["wsw"]
        blocks = ((2, "x2", "xn2", "hg2", "hu2", "act2", wg2, wu2, wd2, "ffn2_norm", 3),
                  (1, "x0", "xn1", "hg1", "hu1", "act1", wg1, wu1, wd1, "ffn1_norm", 0))

        def ffn_backward(dx, blk):
            tag, xk, xnk, hgk, huk, actk, wg, wu, wd, norm_name, slot = blk
            gains = weights[norm_name]
            dxb, dhg, dhu = ffn_bwd_act(dx, wd, sv[hgk], sv[huk], f"ffn{tag}_bwd_act_{l}")
            gwg, gwu, gwd = tn_matmul([(dhg, sv[xnk], 1.0), (dhu, sv[xnk], 1.0), (sv[actk], dxb, 0.5)],
                                      f"ffn{tag}_dw_{l}")
            key = f"ffn{tag}_{l}"
            blocks_of = [split(gwg), split(gwu), split(gwd)]
            if key in two_level:
                paired, token = pair_start(blocks_of, dxb, f"pair_{key}")
            else:
                pending[key], token = scatter_start([blocks_of], f"scatter_{key}")
            dx, dg = proj_bwd_norm([dhg, dhu], [wg, wu], sv[xk], gains[l][None], dx, token, f"ffn{tag}_bwd_x_{l}")
            token = no_dep
            if key in two_level:
                thru, land = pair_wait(paired, dx, f"pair_{key}_wait")
                pending[key], token = chip_start(pair_sum(thru, land, f"pair_sum_{key}"), dg, f"chips_{key}")
            small[norm_name][l] = dg[0]
            return dx, token

        dx, token = ffn_backward(dx, blocks[0])
        dxb, dzg, da_na, da_sw, do_na, do_sw, dbg = mix_bwd_out(
            dx, sv["gt"], sv["a_na"], sv["a_sw"], wna_t, wsw_t, wout_l, token, f"mix_bwd_out_{l}")
        small["b_gate"][l] = dbg[0]
        gwout, gwna, gwsw = tn_matmul([(sv["merged"], dxb, 1.0), (da_na, sv["o_na"], 1.0), (da_sw, sv["o_sw"], 1.0)],
                                      f"mix_dw_{l}")
        dqa, dka, dva, dt2 = na_bwd(sv["qa"], sv["ka"], sv["zq"], sv["t2"], sv["o_na"], do_na, f"na_bwd_{l}")
        dqs, dks, dvs, dbias, dsink = sw_bwd(sv["qs"], sv["ks"], sv["zq"], t5b, sw_sink[l], sv["o_sw"], do_sw,
                                             f"sw_bwd_{l}")
        dbias_sw.append(dbias.reshape(SW_HEADS, SW_BLOCK, 3 * SW_BLOCK))
        small["sw_sink"][l] = jnp.sum(dsink[:, 0].reshape(SW_HEADS, SW_BLOCK), axis=1)
        small["na_rpb"][l] = _rpb_from_rows(rpb_reduce(dt2, f"rpb_reduce_{l}"))
        dz, dgqa, dgka, dgqs, dgks = qk_norm_bwd(dqa, dka, dva, dqs, dks, dvs, sv["zq"], dzg, *sv["gains"], bd,
                                                 f"qk_norm_bwd_{l}")
        fold = lambda g: jnp.sum(g.reshape(-1, HEAD_DIM), axis=0)
        small["na_q_norm"][l], small["na_k_norm"][l] = fold(dgqa), fold(dgka)
        small["sw_q_norm"][l], small["sw_k_norm"][l] = fold(dgqs), fold(dgks)
        (gwin,) = tn_matmul([(dz, sv["hn"], 1.0)], f"dwin_{l}")
        pending[f"mix_{l}"], token = scatter_start([[split(gwout)], [split(gwna), split(gwsw)], [split(gwin)]],
                                                   f"scatter_mix_{l}")
        dx, dg = proj_bwd_norm([dz], [win_t], sv["x1"], mix_norm[l][None], dx, token, f"mix_bwd_x_{l}")
        small["mix_norm"][l] = dg[0]
        dx, tail = ffn_backward(dx, blocks[1])

    dtab = t5_reduce(dbias_sw, bmap, "t5_reduce")
    small_parts = {k: jnp.stack(v) for k, v in small.items()}
    small_parts["t5_rel_table"] = jnp.transpose(dtab[:, :, 0])

    grads, delta, new_m, new_v = {}, {}, {}, {}
    state = {}
    chain = [tail]
    members = {"ffn": lambda t: [(f"ffn{t}_w_gate", 0, 0, True), (f"ffn{t}_w_up", 0, 1, True),
                                 (f"ffn{t}_w_down", 0, 2, False)],
               "mix": lambda t: [("w_out", 0, 0, False), ("w_branch_na", 1, 0, True), ("w_branch_sw", 1, 1, True),
                                 ("w_in", 2, 0, True)]}

    def collect(key):
        if key in two_level:
            zones = [chip_wait(pending[key], chain[0], f"wait_{key}")]
        else:
            zones = scatter_wait(pending[key], chain[0], f"wait_{key}")
        kind, l = key.split("_")
        for k, zi, wi, transposed in members[kind[:3]](kind[3:]):
            view = tr if transposed else (lambda t: t)
            state[k] = adamw_layer(zones[zi], wi, int(l), view(weights[k]), view(mom_m[k]), view(mom_v[k]),
                                   state.get(k), chain[0], f"adamw_{k}_{l}")
            chain[0] = state[k][1]
            if all(f"{kind}_{j}" in done for j in range(depth) if j != int(l)):
                grads[k], delta[k], new_m[k], new_v[k] = (view(t) for t in state[k])
        done.add(key)

    done = set()
    for key in pending:
        if key != last_key:
            collect(key)
    collect(last_key)
    recvs = share_small([small_parts[k] for k in SMALL_NAMES], chain[0])
    results = adamw_small([weights[k] for k in SMALL_NAMES], recvs, [mom_m[k] for k in SMALL_NAMES],
                          [mom_v[k] for k in SMALL_NAMES], "adamw_small")
    for dst, outs in zip((grads, delta, new_m, new_v), results):
        dst.update(dict(zip(SMALL_NAMES, outs)))

    return (loss, dx[None], *[grads[k] for k in order], *[delta[k] for k in order],
            *[new_m[k] for k in order], *[new_v[k] for k in order])
```

```python
import functools
import math

import numpy as np
import jax
import jax.numpy as jnp
from jax import lax
from jax.experimental import pallas as pl
from jax.experimental.pallas import tpu as pltpu

F32 = jnp.float32
BF16 = jnp.bfloat16
MESH = pl.DeviceIdType.MESH

N_DEV = 8
EPS = 1e-6
NEG = -1e30
HEAD_DIM = 64
GRID_W = 64
NA_ROWS = 8
NA_COLS = 16
NA_WIDTH = 512
SW_Q_WIDTH = 512
SW_KV_WIDTH = 128
SW_BLOCK = 128
SW_HEADS = 8
SW_REP = 4
REL_BUCKETS = 32
REL_MAX_DIST = 128
QKV_WIDTH = 3 * NA_WIDTH + SW_Q_WIDTH + 2 * SW_KV_WIDTH
SCALE = 1.0 / math.sqrt(HEAD_DIM)

ADAM_LR = 0.001
ADAM_B1 = 0.9
ADAM_B2 = 0.999
ADAM_EPS = 1e-08
ADAM_WD = 0.01
ADAM_STEP = 10

V7X_VMEM_LIMIT = 56 * 1024 * 1024
LANES = 128
MXU_TILE = 256

NT = (((1,), (1,)), ((), ()))
TN = (((0,), (0,)), ((), ()))


def _params(n_grid=1):
    return pltpu.CompilerParams(dimension_semantics=("arbitrary",) * n_grid,
                                vmem_limit_bytes=V7X_VMEM_LIMIT)


def _row_tile(s):
    for t in (512, 256, 128, 64, 32, 16, 8):
        if s % t == 0:
            return t
    raise ValueError(s)


def _tn_tile(n):
    best = max(t for t in range(LANES, min(n, 2304) + 1, LANES) if n % t == 0) if n % LANES == 0 else n
    return best // 2 if best == n and n >= 1024 else best


ONCE = pl.Buffered(1)


def _col_chunk(n):
    return MXU_TILE if n % MXU_TILE == 0 else n


def _dot(a, b):
    return jnp.dot(a, b, preferred_element_type=F32)


def _dotg(a, b, dn):
    return lax.dot_general(a, b, dn, preferred_element_type=F32)


def _sigmoid(v):
    return 1.0 / (1.0 + jnp.exp(-v))


def _rstd(xv):
    return lax.rsqrt(jnp.mean(xv * xv, axis=-1, keepdims=True) + EPS)


def _full(shape):
    nd = len(shape)
    return pl.BlockSpec(shape, lambda i, _n=nd: (0,) * _n)


def _rows(tm, width):
    return pl.BlockSpec((tm, width), lambda i: (i, 0))


def _mat(stack, idx):
    return pl.BlockSpec((None,) + tuple(stack.shape[1:]), lambda i, _w=idx: (_w, 0, 0), pipeline_mode=ONCE)


def _group_mean(v, bd):
    hi = v.astype(BF16)
    lo = (v - hi.astype(F32)).astype(BF16)
    return _dot(hi, bd) + _dot(lo, bd)


def _swiglu_tile(xn, wg_ref, wu_ref, dg_ref, du_ref, act_ref, fc):
    for c0 in range(0, wg_ref.shape[0], fc):
        hg = _dotg(xn, wg_ref[c0:c0 + fc, :], NT)
        hu = _dotg(xn, wu_ref[c0:c0 + fc, :], NT)
        sg = _sigmoid(hg)
        silu = hg * sg
        du_ref[:, c0:c0 + fc] = silu.astype(BF16)
        dg_ref[:, c0:c0 + fc] = (hu * (sg + silu * (1.0 - sg))).astype(BF16)
        act_ref[:, c0:c0 + fc] = (silu * hu).astype(BF16)


def ffn_up(x, gain, wg_t, wu_t, dep, name):
    s, d = x.shape
    f = wg_t[0].shape[1]
    tm = _row_tile(s)
    fc = _col_chunk(f)

    def body(x_ref, g_ref, wg_ref, wu_ref, dep_ref, xn_ref, dg_ref, du_ref, act_ref):
        xv = x_ref[...]
        xn = (xv * _rstd(xv) * g_ref[...]).astype(BF16)
        xn_ref[...] = xn
        _swiglu_tile(xn, wg_ref, wu_ref, dg_ref, du_ref, act_ref, fc)

    return pl.pallas_call(
        body, name=name, grid=(s // tm,),
        in_specs=[_rows(tm, d), _full((1, d)), _mat(*wg_t), _mat(*wu_t), _full(dep.shape)],
        out_specs=[_rows(tm, d), _rows(tm, f), _rows(tm, f), _rows(tm, f)],
        out_shape=[jax.ShapeDtypeStruct((s, d), BF16)] + [jax.ShapeDtypeStruct((s, f), BF16)] * 3,
        compiler_params=_params(),
    )(x, gain, wg_t[0], wu_t[0], dep)


def ffn_both(x, gain, wg_t, wu_t, wd, dep, name):
    s, d = x.shape
    f = wg_t[0].shape[1]
    tm = min(_row_tile(s), 256)
    fc = _col_chunk(f)

    def body(x_ref, g_ref, wg_ref, wu_ref, wd_ref, dep_ref, xo_ref, xn_ref, dg_ref, du_ref, act_ref):
        xv = x_ref[...]
        xn = (xv * _rstd(xv) * g_ref[...]).astype(BF16)
        xn_ref[...] = xn
        _swiglu_tile(xn, wg_ref, wu_ref, dg_ref, du_ref, act_ref, fc)
        xo_ref[...] = xv + 0.5 * _dot(act_ref[...], wd_ref[...])

    return pl.pallas_call(
        body, name=name, grid=(s // tm,),
        in_specs=[_rows(tm, d), _full((1, d)), _mat(*wg_t), _mat(*wu_t), _mat(*wd), _full(dep.shape)],
        out_specs=[_rows(tm, d), _rows(tm, d), _rows(tm, f), _rows(tm, f), _rows(tm, f)],
        out_shape=[jax.ShapeDtypeStruct((s, d), F32), jax.ShapeDtypeStruct((s, d), BF16)]
                  + [jax.ShapeDtypeStruct((s, f), BF16)] * 3,
        compiler_params=_params(),
    )(x, gain, wg_t[0], wu_t[0], wd[0], dep)


def ffn_down(x, act, wd, dep, name, target=None):
    s, d = x.shape
    f = act.shape[1]
    tm = _row_tile(s)

    def body(x_ref, a_ref, w_ref, dep_ref, *rest):
        y = x_ref[...] + 0.5 * _dot(a_ref[...], w_ref[...])
        if target is None:
            rest[0][...] = y
            return
        t_ref, dy_ref, acc_ref = rest

        @pl.when(pl.program_id(0) == 0)
        def _():
            acc_ref[...] = jnp.zeros(acc_ref.shape, F32)

        err = y - t_ref[...]
        dy_ref[...] = err * (1.0 / d)
        part = jnp.sum((err * err).reshape(tm // 8, 8, d), axis=0)
        acc = part[:, 0:LANES]
        for c0 in range(LANES, d, LANES):
            acc = acc + part[:, c0:c0 + LANES]
        acc_ref[...] = acc_ref[...] + acc

    ins = [_rows(tm, d), _rows(tm, f), _mat(*wd), _full(dep.shape)]
    if target is None:
        return pl.pallas_call(
            body, name=name, grid=(s // tm,), in_specs=ins, out_specs=_rows(tm, d),
            out_shape=jax.ShapeDtypeStruct((s, d), F32), compiler_params=_params(),
        )(x, act, wd[0], dep)
    return pl.pallas_call(
        body, name=name, grid=(s // tm,), in_specs=ins + [_rows(tm, d)],
        out_specs=[_rows(tm, d), _full((8, LANES))],
        out_shape=[jax.ShapeDtypeStruct((s, d), F32), jax.ShapeDtypeStruct((8, LANES), F32)],
        compiler_params=_params(),
    )(x, act, wd[0], dep, target)


def mix_in(x, gain, win_t, b_gate, gq_na, gk_na, gq_sw, gk_sw, bd, name):
    s, d = x.shape
    tm = _row_tile(s)
    gc = _col_chunk(2 * d)

    def body(x_ref, g_ref, w_ref, b_ref, gqa_ref, gka_ref, gqs_ref, gks_ref, bd_ref,
             hn_ref, zq_ref, qa_ref, ka_ref, qs_ref, ks_ref, gt_ref):
        xv = x_ref[...]
        hn = (xv * _rstd(xv) * g_ref[...]).astype(BF16)
        hn_ref[...] = hn

        def proj(c0, c1):
            return _dotg(hn, w_ref[c0:c1, :], NT)

        def headnorm(z, g, bdm):
            return z * lax.rsqrt(_group_mean(z * z, bdm) + EPS) * g

        bd512 = bd_ref[...]
        bd128 = bd_ref[0:SW_KV_WIDTH, 0:SW_KV_WIDTH]
        z = proj(0, 512)
        zq_ref[:, 0:512] = z.astype(BF16)
        qa_ref[...] = (headnorm(z, gqa_ref[...], bd512) * SCALE).astype(BF16)
        z = proj(512, 1024)
        zq_ref[:, 512:1024] = z.astype(BF16)
        ka_ref[...] = headnorm(z, gka_ref[...], bd512).astype(BF16)
        z = proj(1024, 1536)
        zq_ref[:, 1024:1536] = z.astype(BF16)
        z = proj(1536, 2048)
        zq_ref[:, 1536:2048] = z.astype(BF16)
        qs_ref[...] = (headnorm(z, gqs_ref[...], bd512) * SCALE).astype(BF16)
        z = proj(2048, 2176)
        zq_ref[:, 2048:2176] = z.astype(BF16)
        ks_ref[...] = headnorm(z, gks_ref[...], bd128).astype(BF16)
        z = proj(2176, 2304)
        zq_ref[:, 2176:2304] = z.astype(BF16)
        for c0 in range(0, 2 * d, gc):
            zg = proj(QKV_WIDTH + c0, QKV_WIDTH + c0 + gc) + b_ref[:, c0:c0 + gc]
            gt_ref[:, c0:c0 + gc] = _sigmoid(zg).astype(BF16)

    return pl.pallas_call(
        body, name=name, grid=(s // tm,),
        in_specs=[_rows(tm, d), _full((1, d)), _mat(*win_t), _full((1, 2 * d)),
                  _full((1, 512)), _full((1, 512)), _full((1, 512)), _full((1, 128)), _full((512, 512))],
        out_specs=[_rows(tm, d), _rows(tm, QKV_WIDTH), _rows(tm, 512), _rows(tm, 512), _rows(tm, 512),
                   _rows(tm, 128), _rows(tm, 2 * d)],
        out_shape=[jax.ShapeDtypeStruct((s, d), BF16), jax.ShapeDtypeStruct((s, QKV_WIDTH), BF16),
                   jax.ShapeDtypeStruct((s, 512), BF16), jax.ShapeDtypeStruct((s, 512), BF16),
                   jax.ShapeDtypeStruct((s, 512), BF16), jax.ShapeDtypeStruct((s, 128), BF16),
                   jax.ShapeDtypeStruct((s, 2 * d), BF16)],
        compiler_params=_params(),
    )(x, gain, win_t[0], b_gate, gq_na, gk_na, gq_sw, gk_sw, bd)


def _na_iotas():
    qc = lax.broadcasted_iota(jnp.int32, (GRID_W, LANES), 0)
    ln = lax.broadcasted_iota(jnp.int32, (GRID_W, LANES), 1)
    low = ln < GRID_W
    kc = jnp.where(low, ln, ln - GRID_W)
    diff = kc - qc + (NA_COLS - 1)
    qcs = jnp.clip(qc - NA_COLS // 2, 0, GRID_W - NA_COLS)
    inwin = (kc >= qcs) & (kc < qcs + NA_COLS)
    return diff, low, inwin


NA_RI = 2 * NA_ROWS - 1
NA_CI = 2 * NA_COLS - 1
NA_T2 = NA_RI + 1


def _rpb_rows(rpb):
    h = rpb.shape[0]
    padded = jnp.pad(rpb, ((0, 0), (1, 1), (0, GRID_W - NA_CI)))
    return jnp.concatenate([padded[:, :NA_T2], padded[:, 1:NA_T2 + 1]], axis=2).reshape(h, NA_T2, LANES)


def _rpb_from_rows(rows):
    return rows[:, 1:, :NA_CI] + rows[:, :NA_RI, GRID_W:GRID_W + NA_CI]


def rpb_expand(rows, dep, name):
    n_heads = rows.shape[0]

    def body(r_ref, dep_ref, o_ref):
        for h in range(n_heads):
            for e in range(NA_T2):
                line = jnp.broadcast_to(r_ref[h, e:e + 1, :], (GRID_W, LANES))
                o_ref[h, e] = pltpu.roll(line, LANES - (NA_COLS - 1), 1, stride=1, stride_axis=0)

    return pl.pallas_call(
        body, name=name,
        in_specs=[pl.BlockSpec(memory_space=pltpu.VMEM), pl.BlockSpec(memory_space=pltpu.VMEM)],
        out_specs=pl.BlockSpec(memory_space=pltpu.VMEM),
        out_shape=jax.ShapeDtypeStruct((n_heads, NA_T2, GRID_W, LANES), F32),
        compiler_params=pltpu.CompilerParams(vmem_limit_bytes=V7X_VMEM_LIMIT),
    )(rows, dep)


def rpb_reduce(dt2, name):
    n_heads = dt2.shape[0]
    flip = jnp.asarray(np.eye(GRID_W)[::-1], BF16)

    def body(d_ref, j_ref, o_ref):
        jm = j_ref[...]
        for h in range(n_heads):
            for e in range(NA_T2):
                dv = d_ref[h, e]
                hi = dv.astype(BF16)
                mid = (dv - hi.astype(F32)).astype(BF16)
                lo = (dv - hi.astype(F32) - mid.astype(F32)).astype(BF16)
                rev = _dot(jm, hi) + _dot(jm, mid) + _dot(jm, lo)
                back = pltpu.roll(rev, LANES + (NA_COLS - 1) - (GRID_W - 1), 1, stride=1, stride_axis=0)
                o_ref[h, e:e + 1, :] = jnp.sum(back, axis=0, keepdims=True)

    return pl.pallas_call(
        body, name=name,
        in_specs=[pl.BlockSpec(memory_space=pltpu.VMEM)] * 2,
        out_specs=pl.BlockSpec(memory_space=pltpu.VMEM),
        out_shape=jax.ShapeDtypeStruct((n_heads, NA_T2, LANES), F32),
        compiler_params=pltpu.CompilerParams(vmem_limit_bytes=V7X_VMEM_LIMIT),
    )(dt2, flip)


NA_TQ = 4
NA_TK = NA_TQ + NA_ROWS
NA_KCH = NA_TK // 2


def _na_tile_geometry(t, rows):
    r = t * NA_TQ
    kbase = jnp.clip(r - NA_ROWS // 2, 0, rows - NA_TK)
    starts = [jnp.clip(r + a - NA_ROWS // 2, 0, rows - NA_ROWS) for a in range(NA_TQ)]
    return r, kbase, starts


def _na_tile_mask(kbase, starts, low, inwin):
    half = jnp.where(low, 0, 1)
    cols = []
    for c in range(NA_KCH):
        krow = kbase + 2 * c + half
        cols.append(jnp.concatenate(
            [jnp.where(inwin & (krow >= st) & (krow < st + NA_ROWS), 0.0, NEG) for st in starts], axis=0))
    return jnp.concatenate(cols, axis=1)


def _na_tile_index(r, kbase, a, c):
    return jnp.clip(kbase + 2 * c - (r + a) + NA_ROWS, 0, NA_T2 - 1)


def _na_tile_scores(q, k, t2_ref, hh, r, kbase, madd):
    bias = jnp.concatenate(
        [jnp.concatenate([t2_ref[hh, _na_tile_index(r, kbase, a, c)] for a in range(NA_TQ)], axis=0)
         for c in range(NA_KCH)], axis=1)
    return _dotg(q, k, NT) + bias + madd


def _softmax_rows(sc):
    e = jnp.exp(sc - jnp.max(sc, axis=1, keepdims=True))
    return e * (1.0 / jnp.sum(e, axis=1, keepdims=True))


def na_fwd(qa, ka, zq, t2, name):
    s = qa.shape[0]
    rows = s // GRID_W
    n_pairs = NA_WIDTH // LANES
    v_blk0 = (2 * NA_WIDTH) // LANES

    assert rows % NA_TQ == 0 and rows >= NA_TK
    tq, tk = NA_TQ * GRID_W, NA_TK * GRID_W

    def body(q_ref, k_ref, v_ref, t2_ref, o_ref, s_scr, p_scr):
        _, low, inwin = _na_iotas()

        def tile(t, carry):
            r, kbase, starts = _na_tile_geometry(t, rows)
            madd = _na_tile_mask(kbase, starts, low, inwin)
            qr = pl.ds(pl.multiple_of(r * GRID_W, tq), tq)
            kr = pl.ds(pl.multiple_of(kbase * GRID_W, tq), tk)
            for hh in range(2):
                lanes = slice(HEAD_DIM * hh, HEAD_DIM * (hh + 1))
                s_scr[tq * hh:tq * (hh + 1), :] = _na_tile_scores(q_ref[qr, lanes], k_ref[kr, lanes], t2_ref, hh, r,
                                                                  kbase, madd)
            p_scr[...] = _softmax_rows(s_scr[...]).astype(BF16)
            for hh in range(2):
                lanes = slice(HEAD_DIM * hh, HEAD_DIM * (hh + 1))
                o_ref[qr, lanes] = _dot(p_scr[tq * hh:tq * (hh + 1), :], v_ref[kr, lanes]).astype(BF16)
            return carry

        lax.fori_loop(0, rows // NA_TQ, tile, 0)

    col = lambda off: pl.BlockSpec((s, LANES), lambda p, _o=off: (0, _o + p))
    return pl.pallas_call(
        body, name=name, grid=(n_pairs,),
        in_specs=[col(0), col(0), col(v_blk0),
                  pl.BlockSpec((2, NA_T2, GRID_W, LANES), lambda p: (p, 0, 0, 0))],
        out_specs=col(0),
        out_shape=jax.ShapeDtypeStruct((s, NA_WIDTH), BF16),
        scratch_shapes=[pltpu.VMEM((2 * tq, tk), F32), pltpu.VMEM((2 * tq, tk), BF16)],
        compiler_params=_params(),
    )(qa, ka, zq, t2)


def na_bwd(qa, ka, zq, t2, o_na, do_na, name):
    s = qa.shape[0]
    rows = s // GRID_W
    n_pairs = NA_WIDTH // LANES
    v_blk0 = (2 * NA_WIDTH) // LANES

    tq, tk = NA_TQ * GRID_W, NA_TK * GRID_W

    def body(q_ref, k_ref, v_ref, t2_ref, o_ref, do_ref, dq_ref, dk_ref, dv_ref, dt2_ref):
        _, low, inwin = _na_iotas()
        dk_ref[...] = jnp.zeros(dk_ref.shape, F32)
        dv_ref[...] = jnp.zeros(dv_ref.shape, F32)
        dt2_ref[...] = jnp.zeros(dt2_ref.shape, F32)

        def tile(t, carry):
            r, kbase, starts = _na_tile_geometry(t, rows)
            madd = _na_tile_mask(kbase, starts, low, inwin)
            qr = pl.ds(pl.multiple_of(r * GRID_W, tq), tq)
            kr = pl.ds(pl.multiple_of(kbase * GRID_W, tq), tk)
            for hh in range(2):
                lanes = slice(HEAD_DIM * hh, HEAD_DIM * (hh + 1))
                q, k, v = q_ref[qr, lanes], k_ref[kr, lanes], v_ref[kr, lanes]
                p = _softmax_rows(_na_tile_scores(q, k, t2_ref, hh, r, kbase, madd))
                do = do_ref[qr, lanes]
                delta = jnp.sum(do.astype(F32) * o_ref[qr, lanes].astype(F32), axis=1, keepdims=True)
                ds = p * (_dotg(do, v, NT) - delta)
                shared = {}
                for a in range(NA_TQ):
                    for c in range(NA_KCH):
                        shared.setdefault(2 * c - a, []).append(
                            ds[GRID_W * a:GRID_W * (a + 1), LANES * c:LANES * (c + 1)])
                for offset, parts in shared.items():
                    e = jnp.clip(offset + kbase - r + NA_ROWS, 0, NA_T2 - 1)
                    dt2_ref[hh, e] = dt2_ref[hh, e] + functools.reduce(jnp.add, parts)
                dsb = ds.astype(BF16)
                dq_ref[qr, lanes] = _dot(dsb, k)
                dk_ref[kr, lanes] = dk_ref[kr, lanes] + _dotg(dsb, q, TN)
                dv_ref[kr, lanes] = dv_ref[kr, lanes] + _dotg(p.astype(BF16), do, TN)
            return carry

        lax.fori_loop(0, rows // NA_TQ, tile, 0)

    col = lambda off: pl.BlockSpec((s, LANES), lambda p, _o=off: (0, _o + p))
    t2spec = pl.BlockSpec((2, NA_T2, GRID_W, LANES), lambda p: (p, 0, 0, 0))
    return pl.pallas_call(
        body, name=name, grid=(n_pairs,),
        in_specs=[col(0), col(0), col(v_blk0), t2spec, col(0), col(0)],
        out_specs=[col(0), col(0), col(0), t2spec],
        out_shape=[jax.ShapeDtypeStruct((s, NA_WIDTH), F32)] * 3 + [jax.ShapeDtypeStruct(t2.shape, F32)],
        compiler_params=_params(),
    )(qa, ka, zq, t2, o_na, do_na)


def _t5_bucket_map():
    rel = np.arange(3 * SW_BLOCK)[None, :] - SW_BLOCK - np.arange(SW_BLOCK)[:, None]
    nb = REL_BUCKETS // 2
    max_exact = nb // 2
    n = np.abs(rel)
    large = max_exact + (np.log(np.maximum(n, 1) / max_exact)
                         / np.log(REL_MAX_DIST / max_exact) * (nb - max_exact)).astype(np.int32)
    large = np.minimum(large, nb - 1)
    return ((rel > 0) * nb + np.where(n < max_exact, n, large)).astype(np.int32)


def t5_expand(table, bmap, dep, name):
    def body(tab_ref, bm_ref, dep_ref, o_ref):
        bm = bm_ref[...]
        for h in range(SW_HEADS):
            t = jnp.zeros(bm.shape, F32)
            for b in range(REL_BUCKETS):
                t = jnp.where(bm == b, tab_ref[b, h], t)
            o_ref[h] = t

    return pl.pallas_call(
        body, name=name,
        in_specs=[pl.BlockSpec(memory_space=pltpu.SMEM), pl.BlockSpec(memory_space=pltpu.VMEM),
                  pl.BlockSpec(memory_space=pltpu.VMEM)],
        out_specs=pl.BlockSpec(memory_space=pltpu.VMEM),
        out_shape=jax.ShapeDtypeStruct((SW_HEADS,) + bmap.shape, F32),
        compiler_params=pltpu.CompilerParams(vmem_limit_bytes=V7X_VMEM_LIMIT),
    )(table, bmap, dep)


def t5_reduce(dbias_list, bmap, name):
    n = len(dbias_list)

    def body(*refs):
        d_refs, bm_ref, o_ref = refs[:n], refs[n], refs[n + 1]
        bm = bm_ref[...]
        for h in range(SW_HEADS):
            dv = d_refs[0][h]
            for other in d_refs[1:]:
                dv = dv + other[h]
            rows = [jnp.sum(jnp.where(bm == b, dv, 0.0), axis=0, keepdims=True) for b in range(REL_BUCKETS)]
            r = jnp.concatenate(rows, axis=0)
            o_ref[h] = jnp.broadcast_to(jnp.sum(r, axis=1, keepdims=True), (REL_BUCKETS, LANES))

    return pl.pallas_call(
        body, name=name,
        in_specs=[pl.BlockSpec(memory_space=pltpu.VMEM)] * (n + 1),
        out_specs=pl.BlockSpec(memory_space=pltpu.VMEM),
        out_shape=jax.ShapeDtypeStruct((SW_HEADS, REL_BUCKETS, LANES), F32),
        compiler_params=pltpu.CompilerParams(vmem_limit_bytes=V7X_VMEM_LIMIT),
    )(*dbias_list, bmap)


def _sw_mask_iotas():
    a = lax.broadcasted_iota(jnp.int32, (SW_BLOCK, 3 * SW_BLOCK), 0)
    j = lax.broadcasted_iota(jnp.int32, (SW_BLOCK, 3 * SW_BLOCK), 1)
    inwin = jnp.abs(j - SW_BLOCK - a) <= SW_BLOCK
    return j, inwin


SW_STACK = SW_HEADS * SW_BLOCK


def _sw_softmax(sc, sk):
    m = jnp.maximum(jnp.max(sc, axis=1, keepdims=True), sk)
    e = jnp.exp(sc - m)
    es = jnp.exp(sk - m)
    inv = 1.0 / (jnp.sum(e, axis=1, keepdims=True) + es)
    return e * inv, es * inv


def _sw_prologue(k_ref, v_ref, kp, vp, sink_ref, s):
    pad = s + 2 * SW_BLOCK
    zeros = jnp.zeros((SW_BLOCK, SW_KV_WIDTH), BF16)
    kp[0:SW_BLOCK, :] = zeros
    vp[0:SW_BLOCK, :] = zeros
    kp[SW_BLOCK + s:pad, :] = zeros
    vp[SW_BLOCK + s:pad, :] = zeros
    kp[SW_BLOCK:SW_BLOCK + s, :] = k_ref[...]
    vp[SW_BLOCK:SW_BLOCK + s, :] = v_ref[...]
    return jnp.concatenate([jnp.full((SW_BLOCK, 1), sink_ref[h], F32) for h in range(SW_HEADS)], axis=0)


def sw_fwd(qs, ks, zq, t5b, sink, dep, name):
    s = qs.shape[0]
    nb = s // SW_BLOCK
    v_blk = (3 * NA_WIDTH + SW_Q_WIDTH + SW_KV_WIDTH) // LANES
    pad = s + 2 * SW_BLOCK

    def body(q_ref, k_ref, v_ref, b_ref, sink_ref, dep_ref, o_ref, kp, vp, s_scr, p_scr):
        sink_col = _sw_prologue(k_ref, v_ref, kp, vp, sink_ref, s)
        j, inwin = _sw_mask_iotas()

        def blk(n, carry):
            kpos = n * SW_BLOCK - SW_BLOCK + j
            madd = jnp.where(inwin & (kpos >= 0) & (kpos < s), 0.0, NEG)
            q0 = pl.multiple_of(n * SW_BLOCK, SW_BLOCK)
            qr, kr = pl.ds(q0, SW_BLOCK), pl.ds(q0, 3 * SW_BLOCK)
            for h in range(SW_HEADS):
                g = h // SW_REP
                s_scr[SW_BLOCK * h:SW_BLOCK * (h + 1), :] = _dotg(
                    q_ref[qr, HEAD_DIM * h:HEAD_DIM * (h + 1)], kp[kr, HEAD_DIM * g:HEAD_DIM * (g + 1)], NT) + madd
            p, _ = _sw_softmax(s_scr[...] + b_ref[...], sink_col)
            p_scr[...] = p.astype(BF16)
            for h in range(SW_HEADS):
                g = h // SW_REP
                o_ref[qr, HEAD_DIM * h:HEAD_DIM * (h + 1)] = _dot(
                    p_scr[SW_BLOCK * h:SW_BLOCK * (h + 1), :], vp[kr, HEAD_DIM * g:HEAD_DIM * (g + 1)]).astype(BF16)
            return carry

        lax.fori_loop(0, nb, blk, 0)

    return pl.pallas_call(
        body, name=name, grid=(1,),
        in_specs=[_full((s, SW_Q_WIDTH)), _full((s, SW_KV_WIDTH)),
                  pl.BlockSpec((s, SW_KV_WIDTH), lambda i: (0, v_blk)),
                  _full((SW_STACK, 3 * SW_BLOCK)), pl.BlockSpec(memory_space=pltpu.SMEM),
                  _full(dep.shape)],
        out_specs=_full((s, SW_Q_WIDTH)),
        out_shape=jax.ShapeDtypeStruct((s, SW_Q_WIDTH), BF16),
        scratch_shapes=[pltpu.VMEM((pad, SW_KV_WIDTH), BF16), pltpu.VMEM((pad, SW_KV_WIDTH), BF16),
                        pltpu.VMEM((SW_STACK, 3 * SW_BLOCK), F32), pltpu.VMEM((SW_STACK, 3 * SW_BLOCK), BF16)],
        compiler_params=_params(),
    )(qs, ks, zq, t5b, sink, dep)


def sw_bwd(qs, ks, zq, t5b, sink, o_sw, do_sw, name):
    s = qs.shape[0]
    nb = s // SW_BLOCK
    v_blk = (3 * NA_WIDTH + SW_Q_WIDTH + SW_KV_WIDTH) // LANES
    pad = s + 2 * SW_BLOCK

    def body(q_ref, k_ref, v_ref, b_ref, sink_ref, o_ref, do_ref,
             dq_ref, dk_ref, dv_ref, db_ref, dsk_ref, kp, vp, dkp, dvp, s_scr, dp_scr, ds_scr, p_scr):
        sink_col = _sw_prologue(k_ref, v_ref, kp, vp, sink_ref, s)
        dkp[...] = jnp.zeros(dkp.shape, F32)
        dvp[...] = jnp.zeros(dvp.shape, F32)
        db_ref[...] = jnp.zeros(db_ref.shape, F32)
        dsk_ref[...] = jnp.zeros(dsk_ref.shape, F32)
        j, inwin = _sw_mask_iotas()

        def blk(n, carry):
            kpos = n * SW_BLOCK - SW_BLOCK + j
            madd = jnp.where(inwin & (kpos >= 0) & (kpos < s), 0.0, NEG)
            q0 = pl.multiple_of(n * SW_BLOCK, SW_BLOCK)
            qr, kr = pl.ds(q0, SW_BLOCK), pl.ds(q0, 3 * SW_BLOCK)
            deltas = []
            for h in range(SW_HEADS):
                g = h // SW_REP
                hl, kl = slice(HEAD_DIM * h, HEAD_DIM * (h + 1)), slice(HEAD_DIM * g, HEAD_DIM * (g + 1))
                rows = slice(SW_BLOCK * h, SW_BLOCK * (h + 1))
                do = do_ref[qr, hl]
                s_scr[rows, :] = _dotg(q_ref[qr, hl], kp[kr, kl], NT) + madd
                dp_scr[rows, :] = _dotg(do, vp[kr, kl], NT)
                deltas.append(jnp.sum(do.astype(F32) * o_ref[qr, hl].astype(F32), axis=1, keepdims=True))
            delta = jnp.concatenate(deltas, axis=0)
            p, ps = _sw_softmax(s_scr[...] + b_ref[...], sink_col)
            ds = p * (dp_scr[...] - delta)
            db_ref[...] = db_ref[...] + ds
            dsk_ref[...] = dsk_ref[...] - jnp.broadcast_to(ps * delta, (SW_STACK, LANES))
            ds_scr[...] = ds.astype(BF16)
            p_scr[...] = p.astype(BF16)
            for g in range(SW_HEADS // SW_REP):
                kl = slice(HEAD_DIM * g, HEAD_DIM * (g + 1))
                k = kp[kr, kl]
                dkw = jnp.zeros((3 * SW_BLOCK, HEAD_DIM), F32)
                dvw = jnp.zeros((3 * SW_BLOCK, HEAD_DIM), F32)
                for r in range(SW_REP):
                    h = g * SW_REP + r
                    hl, rows = slice(HEAD_DIM * h, HEAD_DIM * (h + 1)), slice(SW_BLOCK * h, SW_BLOCK * (h + 1))
                    dsb = ds_scr[rows, :]
                    dq_ref[qr, hl] = _dot(dsb, k)
                    dkw = dkw + _dotg(dsb, q_ref[qr, hl], TN)
                    dvw = dvw + _dotg(p_scr[rows, :], do_ref[qr, hl], TN)
                dkp[kr, kl] = dkp[kr, kl] + dkw
                dvp[kr, kl] = dvp[kr, kl] + dvw
            return carry

        lax.fori_loop(0, nb, blk, 0)
        dk_ref[...] = dkp[SW_BLOCK:SW_BLOCK + s, :]
        dv_ref[...] = dvp[SW_BLOCK:SW_BLOCK + s, :]

    bias_spec = _full((SW_STACK, 3 * SW_BLOCK))
    return pl.pallas_call(
        body, name=name, grid=(1,),
        in_specs=[_full((s, SW_Q_WIDTH)), _full((s, SW_KV_WIDTH)),
                  pl.BlockSpec((s, SW_KV_WIDTH), lambda i: (0, v_blk)),
                  bias_spec, pl.BlockSpec(memory_space=pltpu.SMEM),
                  _full((s, SW_Q_WIDTH)), _full((s, SW_Q_WIDTH))],
        out_specs=[_full((s, SW_Q_WIDTH)), _full((s, SW_KV_WIDTH)), _full((s, SW_KV_WIDTH)), bias_spec,
                   _full((SW_STACK, LANES))],
        out_shape=[jax.ShapeDtypeStruct((s, SW_Q_WIDTH), F32), jax.ShapeDtypeStruct((s, SW_KV_WIDTH), F32),
                   jax.ShapeDtypeStruct((s, SW_KV_WIDTH), F32),
                   jax.ShapeDtypeStruct((SW_STACK, 3 * SW_BLOCK), F32),
                   jax.ShapeDtypeStruct((SW_STACK, LANES), F32)],
        scratch_shapes=[pltpu.VMEM((pad, SW_KV_WIDTH), BF16), pltpu.VMEM((pad, SW_KV_WIDTH), BF16),
                        pltpu.VMEM((pad, SW_KV_WIDTH), F32), pltpu.VMEM((pad, SW_KV_WIDTH), F32),
                        pltpu.VMEM((SW_STACK, 3 * SW_BLOCK), F32), pltpu.VMEM((SW_STACK, 3 * SW_BLOCK), F32),
                        pltpu.VMEM((SW_STACK, 3 * SW_BLOCK), BF16), pltpu.VMEM((SW_STACK, 3 * SW_BLOCK), BF16)],
        compiler_params=_params(),
    )(qs, ks, zq, t5b, sink, o_sw, do_sw)


def merge_out(x, o_na, o_sw, gt, wbna_t, wbsw_t, wout, name):
    s, d = x.shape
    tm = _row_tile(s)

    def body(x_ref, ona_ref, osw_ref, gt_ref, wna_ref, wsw_ref, wo_ref, xo_ref, ana_ref, asw_ref, mg_ref):
        a_na = _dotg(ona_ref[...], wna_ref[...], NT)
        a_sw = _dotg(osw_ref[...], wsw_ref[...], NT)
        g_na, g_sw = gt_ref[:, 0:d].astype(F32), gt_ref[:, d:2 * d].astype(F32)
        ana_ref[...] = (a_na * g_na * (1.0 - g_na)).astype(BF16)
        asw_ref[...] = (a_sw * g_sw * (1.0 - g_sw)).astype(BF16)
        merged = (g_na * a_na + g_sw * a_sw).astype(BF16)
        mg_ref[...] = merged
        xo_ref[...] = x_ref[...] + _dot(merged, wo_ref[...])

    return pl.pallas_call(
        body, name=name, grid=(s // tm,),
        in_specs=[_rows(tm, d), _rows(tm, 512), _rows(tm, 512), _rows(tm, 2 * d),
                  _mat(*wbna_t), _mat(*wbsw_t), _mat(*wout)],
        out_specs=[_rows(tm, d)] * 4,
        out_shape=[jax.ShapeDtypeStruct((s, d), F32)] + [jax.ShapeDtypeStruct((s, d), BF16)] * 3,
        compiler_params=_params(),
    )(x, o_na, o_sw, gt, wbna_t[0], wbsw_t[0], wout[0])


def mix_bwd_out(dx, gt, a_na, a_sw, wbna_t, wbsw_t, wout, dep, name):
    s, d = dx.shape
    tm = _row_tile(s)

    def body(dx_ref, gt_ref, ana_ref, asw_ref, wna_ref, wsw_ref, wo_ref, dep_ref,
             dxb_ref, dzg_ref, dana_ref, dasw_ref, dona_ref, dosw_ref, dbg_ref):
        @pl.when(pl.program_id(0) == 0)
        def _():
            dbg_ref[...] = jnp.zeros(dbg_ref.shape, F32)

        dxb = dx_ref[...].astype(BF16)
        dxb_ref[...] = dxb
        dm = _dotg(dxb, wo_ref[...], NT)
        for i, (a_ref, da_ref, w_ref, do_ref) in enumerate(
                [(ana_ref, dana_ref, wna_ref, dona_ref), (asw_ref, dasw_ref, wsw_ref, dosw_ref)]):
            gi = gt_ref[:, i * d:(i + 1) * d].astype(F32)
            da = (dm * gi).astype(BF16)
            da_ref[...] = da
            do_ref[...] = _dot(da, w_ref[...]).astype(BF16)
            dzg = dm * a_ref[...].astype(F32)
            dzg_ref[:, i * d:(i + 1) * d] = dzg.astype(BF16)
            dbg_ref[:, i * d:(i + 1) * d] = dbg_ref[:, i * d:(i + 1) * d] + jnp.sum(dzg, axis=0, keepdims=True)

    return pl.pallas_call(
        body, name=name, grid=(s // tm,),
        in_specs=[_rows(tm, d), _rows(tm, 2 * d), _rows(tm, d), _rows(tm, d),
                  _mat(*wbna_t), _mat(*wbsw_t), _mat(*wout), _full(dep.shape)],
        out_specs=[_rows(tm, d), _rows(tm, 2 * d), _rows(tm, d), _rows(tm, d), _rows(tm, 512), _rows(tm, 512),
                   _full((1, 2 * d))],
        out_shape=[jax.ShapeDtypeStruct((s, d), BF16), jax.ShapeDtypeStruct((s, 2 * d), BF16),
                   jax.ShapeDtypeStruct((s, d), BF16), jax.ShapeDtypeStruct((s, d), BF16),
                   jax.ShapeDtypeStruct((s, 512), BF16), jax.ShapeDtypeStruct((s, 512), BF16),
                   jax.ShapeDtypeStruct((1, 2 * d), F32)],
        compiler_params=_params(),
    )(dx, gt, a_na, a_sw, wbna_t[0], wbsw_t[0], wout[0], dep)


def qk_norm_bwd(dqa, dka, dva, dqs, dks, dvs, zq, dzg, gq_na, gk_na, gq_sw, gk_sw, bd, name):
    s = zq.shape[0]
    d2 = dzg.shape[1]
    n_in = QKV_WIDTH + d2
    tm = _row_tile(s)

    def body(dqa_ref, dka_ref, dva_ref, dqs_ref, dks_ref, dvs_ref, zq_ref, dzg_ref,
             gqa_ref, gka_ref, gqs_ref, gks_ref, bd_ref, dz_ref, dgqa_ref, dgka_ref, dgqs_ref, dgks_ref):
        @pl.when(pl.program_id(0) == 0)
        def _():
            for r in (dgqa_ref, dgka_ref, dgqs_ref, dgks_ref):
                r[...] = jnp.zeros(r.shape, F32)

        bd512 = bd_ref[...]
        bd128 = bd_ref[0:SW_KV_WIDTH, 0:SW_KV_WIDTH]

        def one(c0, c1, dy_ref, g_ref, dg_ref, bdm, scale):
            z = zq_ref[:, c0:c1].astype(F32)
            r = lax.rsqrt(_group_mean(z * z, bdm) + EPS)
            zh = z * r
            dy = dy_ref[...] * scale
            dyg = dy * g_ref[...]
            dz = r * (dyg - zh * _group_mean(dyg * zh, bdm))
            dz_ref[:, c0:c1] = dz.astype(BF16)
            dg_ref[...] = dg_ref[...] + jnp.sum(dy * zh, axis=0, keepdims=True)

        one(0, 512, dqa_ref, gqa_ref, dgqa_ref, bd512, SCALE)
        one(512, 1024, dka_ref, gka_ref, dgka_ref, bd512, 1.0)
        dz_ref[:, 1024:1536] = dva_ref[...].astype(BF16)
        one(1536, 2048, dqs_ref, gqs_ref, dgqs_ref, bd512, SCALE)
        one(2048, 2176, dks_ref, gks_ref, dgks_ref, bd128, 1.0)
        dz_ref[:, 2176:2304] = dvs_ref[...].astype(BF16)
        dz_ref[:, QKV_WIDTH:n_in] = dzg_ref[...]

    return pl.pallas_call(
        body, name=name, grid=(s // tm,),
        in_specs=[_rows(tm, 512), _rows(tm, 512), _rows(tm, 512), _rows(tm, 512), _rows(tm, 128), _rows(tm, 128),
                  _rows(tm, QKV_WIDTH), _rows(tm, d2),
                  _full((1, 512)), _full((1, 512)), _full((1, 512)), _full((1, 128)), _full((512, 512))],
        out_specs=[_rows(tm, n_in), _full((1, 512)), _full((1, 512)), _full((1, 512)), _full((1, 128))],
        out_shape=[jax.ShapeDtypeStruct((s, n_in), BF16)] + [jax.ShapeDtypeStruct((1, 512), F32)] * 3
                  + [jax.ShapeDtypeStruct((1, 128), F32)],
        compiler_params=_params(),
    )(dqa, dka, dva, dqs, dks, dvs, zq, dzg, gq_na, gk_na, gq_sw, gk_sw, bd)


def ffn_bwd_act(dx, wd, hg, hu, name):
    s, d = dx.shape
    f = wd[0].shape[1]
    tm = _row_tile(s)
    fc = _col_chunk(f)

    def body(dx_ref, w_ref, hg_ref, hu_ref, dxb_ref, dhg_ref, dhu_ref):
        dxv = dx_ref[...]
        dxb_ref[...] = dxv.astype(BF16)
        half = (0.5 * dxv).astype(BF16)
        for c0 in range(0, f, fc):
            dact = _dotg(half, w_ref[c0:c0 + fc, :], NT)
            dhu_ref[:, c0:c0 + fc] = (dact * hu_ref[:, c0:c0 + fc].astype(F32)).astype(BF16)
            dhg_ref[:, c0:c0 + fc] = (dact * hg_ref[:, c0:c0 + fc].astype(F32)).astype(BF16)

    return pl.pallas_call(
        body, name=name, grid=(s // tm,),
        in_specs=[_rows(tm, d), _mat(*wd), _rows(tm, f), _rows(tm, f)],
        out_specs=[_rows(tm, d), _rows(tm, f), _rows(tm, f)],
        out_shape=[jax.ShapeDtypeStruct((s, d), BF16), jax.ShapeDtypeStruct((s, f), BF16),
                   jax.ShapeDtypeStruct((s, f), BF16)],
        compiler_params=_params(),
    )(dx, wd[0], hg, hu)


def proj_bwd_norm(acts, weights, x, gain, dx, dep, name):
    s, d = x.shape
    tm = min(_row_tile(s), 256)
    n = len(acts)

    def body(*refs):
        a_refs, w_refs = refs[:n], refs[n:2 * n]
        x_ref, g_ref, dx_ref, _, o_ref, dg_ref = refs[2 * n:]

        @pl.when(pl.program_id(0) == 0)
        def _():
            dg_ref[...] = jnp.zeros(dg_ref.shape, F32)

        dxn = _dot(a_refs[0][...], w_refs[0][...])
        for a_ref, w_ref in zip(a_refs[1:], w_refs[1:]):
            dxn = dxn + _dot(a_ref[...], w_ref[...])
        xv = x_ref[...]
        r = _rstd(xv)
        xh = xv * r
        dxh = dxn * g_ref[...]
        o_ref[...] = dx_ref[...] + r * (dxh - xh * jnp.mean(dxh * xh, axis=-1, keepdims=True))
        dg_ref[...] = dg_ref[...] + jnp.sum(dxn * xh, axis=0, keepdims=True)

    return pl.pallas_call(
        body, name=name, grid=(s // tm,),
        in_specs=[_rows(tm, a.shape[1]) for a in acts] + [_mat(*w) for w in weights]
                 + [_rows(tm, d), _full((1, d)), _rows(tm, d), _full(dep.shape)],
        out_specs=[_rows(tm, d), _full((1, d))],
        out_shape=[jax.ShapeDtypeStruct((s, d), F32), jax.ShapeDtypeStruct((1, d), F32)],
        compiler_params=_params(),
    )(*acts, *[w[0] for w in weights], x, gain, dx, dep)


def tn_matmul(products, name):
    s, n = products[0][0].shape
    tn = _tn_tile(n) if len(products) == 1 else _col_chunk(n)
    rhs = []
    for _, b, _ in products:
        if not any(b is seen for seen in rhs):
            rhs.append(b)
    which = [next(i for i, seen in enumerate(rhs) if b is seen) for _, b, _ in products]
    npr, nr = len(products), len(rhs)

    def body(*refs):
        a_refs, b_refs, o_refs = refs[:npr], refs[npr:npr + nr], refs[npr + nr:]
        for i, (_, _, scale) in enumerate(products):
            o_refs[i][...] = (scale * _dotg(a_refs[i][...], b_refs[which[i]][...], TN)).astype(BF16)

    return pl.pallas_call(
        body, name=name, grid=(n // tn,),
        in_specs=[pl.BlockSpec((s, tn), lambda i: (0, i))] * npr
                 + [pl.BlockSpec(b.shape, lambda i: (0, 0), pipeline_mode=ONCE) for b in rhs],
        out_specs=[pl.BlockSpec((tn, b.shape[1]), lambda i: (i, 0)) for _, b, _ in products],
        out_shape=[jax.ShapeDtypeStruct((n, b.shape[1]), BF16) for _, b, _ in products],
        compiler_params=_params(),
    )(*[a for a, _, _ in products], *rhs)


def _mesh_pos():
    return lax.axis_index("x"), lax.axis_index("y"), lax.axis_index("c")


def _peers():
    x, y, c = _mesh_pos()
    peers = []
    for rel in range(1, N_DEV):
        peers.append((1 - x if rel & 4 else x, 1 - y if rel & 2 else y, 1 - c if rel & 1 else c))
    return 4 * x + 2 * y + c, peers


HBM_SPEC = pl.BlockSpec(memory_space=pltpu.HBM)
SEM_SPEC = pl.BlockSpec(memory_space=pltpu.SEMAPHORE)


def _split_call(body, name, thru, n_sems, extra=(), with_token=True):
    hbm = lambda t: pltpu.with_memory_space_constraint(t, pltpu.HBM)
    effect = pltpu.CompilerParams(has_side_effects=pltpu.SideEffectType.DATAFLOW_SIDE_EFFECTING)
    nt = len(thru)
    thru_shapes = [pltpu.HBM(t.shape, t.dtype) for t in thru]
    if with_token:
        (after,) = extra
        outs = pl.pallas_call(
            body, name=name, in_specs=[HBM_SPEC] * nt + [pl.BlockSpec(memory_space=pl.ANY)],
            out_specs=[SEM_SPEC] * len(n_sems) + [HBM_SPEC] * nt + [pl.BlockSpec(memory_space=pltpu.VMEM)],
            out_shape=[pltpu.SemaphoreType.DMA((k,)) for k in n_sems] + thru_shapes
                      + [jax.ShapeDtypeStruct((8, LANES), F32)],
            input_output_aliases={i: len(n_sems) + i for i in range(nt)}, compiler_params=effect,
        )(*[hbm(t) for t in thru], after)
        return outs[:len(n_sems)], outs[len(n_sems):-1], outs[-1]
    return pl.pallas_call(
        body, name=name,
        in_specs=[HBM_SPEC] * nt + [SEM_SPEC] * len(n_sems) + [pl.BlockSpec(memory_space=pl.ANY)],
        out_specs=[HBM_SPEC] * nt, out_shape=thru_shapes,
        input_output_aliases={i: i for i in range(nt)}, compiler_params=effect,
    )(*thru, *extra)


def _gather_targets():
    x, y, c = _mesh_pos()
    return 4 * x + 2 * y + c, [(x, y, 1 - c), (1 - x, y, c), (x, 1 - y, c), (1 - x, 1 - y, c)]


def gather_start(shards, after, name):
    n = len(shards)
    zones = [lax.empty((w.shape[0], N_DEV) + w.shape[1:], w.dtype) for w in shards]

    def body(*refs):
        ins, zs = refs[:n], refs[n:2 * n]
        send_sems, recv_sems, local_sems = refs[2 * n + 1:2 * n + 4]
        token = refs[-1]
        me, targets = _gather_targets()
        for a in range(n):
            pltpu.make_async_copy(ins[a], zs[a].at[:, me], local_sems.at[a]).start()
            for k, to in enumerate(targets):
                pltpu.make_async_remote_copy(
                    src_ref=ins[a], dst_ref=zs[a].at[:, me], send_sem=send_sems.at[4 * a + k],
                    recv_sem=recv_sems.at[4 * a + k], device_id=to, device_id_type=MESH).start()
        token[...] = jnp.zeros(token.shape, F32)

    sems, thru, token = _split_call(body, name, list(shards) + zones, (4 * n, 4 * n, n), extra=(after,))
    return (sems, thru, n), token


def gather_wait(started, after, name):
    sems, thru, n = started

    def body(*refs):
        zs = refs[n:2 * n]
        send_sems, recv_sems, local_sems = refs[2 * n:2 * n + 3]
        _, targets = _gather_targets()
        for a in range(n):
            for k, to in enumerate(targets):
                cp = pltpu.make_async_remote_copy(
                    src_ref=zs[a].at[:, 0], dst_ref=zs[a].at[:, 0], send_sem=send_sems.at[4 * a + k],
                    recv_sem=recv_sems.at[4 * a + k], device_id=to, device_id_type=MESH)
                cp.wait_send()
                cp.wait_recv()
            pltpu.make_async_copy(zs[a].at[:, 0], zs[a].at[:, 0], local_sems.at[a]).wait()

    return _split_call(body, name, thru, (4 * n, 4 * n, n), extra=(*sems, after), with_token=False)[n:]


def forward_start(zones, after, name):
    n = len(zones)

    def body(*refs):
        zs = refs[:n]
        send_sems, recv_sems = refs[n + 1:n + 3]
        token = refs[-1]
        x, y, c = _mesh_pos()
        for a in range(n):
            for j, chip in enumerate([(1 - x, y), (x, 1 - y), (1 - x, 1 - y)]):
                blk = zs[a].at[:, 4 * chip[0] + 2 * chip[1] + c]
                pltpu.make_async_remote_copy(
                    src_ref=blk, dst_ref=blk, send_sem=send_sems.at[3 * a + j], recv_sem=recv_sems.at[3 * a + j],
                    device_id=(x, y, 1 - c), device_id_type=MESH).start()
        token[...] = jnp.zeros(token.shape, F32)

    sems, thru, token = _split_call(body, name, list(zones), (3 * n, 3 * n), extra=(after,))
    return (sems, thru, n), token


def forward_wait(started, after, name):
    sems, thru, n = started

    def body(*refs):
        zs = refs[:n]
        send_sems, recv_sems = refs[n:n + 2]
        x, y, c = _mesh_pos()
        for a in range(n):
            for j in range(3):
                cp = pltpu.make_async_remote_copy(
                    src_ref=zs[a].at[:, 0], dst_ref=zs[a].at[:, 0], send_sem=send_sems.at[3 * a + j],
                    recv_sem=recv_sems.at[3 * a + j], device_id=(x, y, 1 - c), device_id_type=MESH)
                cp.wait_send()
                cp.wait_recv()

    return _split_call(body, name, thru, (3 * n, 3 * n), extra=(*sems, after), with_token=False)


def scatter_start(groups, name):
    n = len(groups)
    flat = [g for grp in groups for g in grp]
    nf = len(flat)
    offs = np.cumsum([0] + [len(grp) for grp in groups])
    lands = [lax.empty((N_DEV, len(grp)) + grp[0].shape[1:], grp[0].dtype) for grp in groups]

    def body(*refs):
        ins, zones = refs[:nf], refs[nf:nf + n]
        send_sems, recv_sems, local_sems = refs[nf + n:nf + n + 3]
        token = refs[-1]
        me, peers = _peers()
        for a in range(n):
            for w in range(len(groups[a])):
                pltpu.make_async_copy(ins[offs[a] + w].at[me], zones[a].at[me, w], local_sems.at[a]).start()
        for k, peer in enumerate(peers):
            p_id = 4 * peer[0] + 2 * peer[1] + peer[2]
            for a in range(n):
                for w in range(len(groups[a])):
                    pltpu.make_async_remote_copy(
                        src_ref=ins[offs[a] + w].at[p_id], dst_ref=zones[a].at[me, w],
                        send_sem=send_sems.at[7 * a + k], recv_sem=recv_sems.at[7 * a + k],
                        device_id=peer, device_id_type=MESH).start()
        token[...] = jnp.zeros(token.shape, F32)

    hbm = lambda t: pltpu.with_memory_space_constraint(t, pltpu.HBM)
    outs = pl.pallas_call(
        body, name=name,
        in_specs=[HBM_SPEC] * (nf + n),
        out_specs=[SEM_SPEC] * 3 + [HBM_SPEC] * (nf + n) + [pl.BlockSpec(memory_space=pltpu.VMEM)],
        out_shape=[pltpu.SemaphoreType.DMA((7 * n,)), pltpu.SemaphoreType.DMA((7 * n,)), pltpu.SemaphoreType.DMA((n,))]
                  + [pltpu.HBM(t.shape, t.dtype) for t in flat + lands]
                  + [jax.ShapeDtypeStruct((8, LANES), F32)],
        input_output_aliases={i: 3 + i for i in range(nf + n)},
        compiler_params=pltpu.CompilerParams(has_side_effects=pltpu.SideEffectType.DATAFLOW_SIDE_EFFECTING),
    )(*[hbm(t) for t in flat], *[hbm(t) for t in lands])
    sems, thru, token = outs[:3], outs[3:3 + nf + n], outs[-1]
    return (sems, thru, [len(grp) for grp in groups]), token


def scatter_wait(started, after, name):
    (send_sems, recv_sems, local_sems), thru, sizes = started
    n = len(sizes)
    nf = len(thru) - n

    def body(*refs):
        zones = refs[nf:nf + n]
        s_sems, r_sems, l_sems = refs[nf + n:nf + n + 3]
        me, peers = _peers()
        for a in range(n):
            for k, peer in enumerate(peers):
                cp = pltpu.make_async_remote_copy(
                    src_ref=zones[a].at[0], dst_ref=zones[a].at[0],
                    send_sem=s_sems.at[7 * a + k], recv_sem=r_sems.at[7 * a + k], device_id=peer,
                    device_id_type=MESH)
                cp.wait_send()
                cp.wait_recv()
            pltpu.make_async_copy(zones[a].at[0], zones[a].at[0], l_sems.at[a]).wait()

    outs = pl.pallas_call(
        body, name=name,
        in_specs=[HBM_SPEC] * (nf + n) + [SEM_SPEC] * 3 + [pl.BlockSpec(memory_space=pl.ANY)],
        out_specs=[HBM_SPEC] * (nf + n),
        out_shape=[pltpu.HBM(t.shape, t.dtype) for t in thru],
        input_output_aliases={i: i for i in range(nf + n)},
        compiler_params=pltpu.CompilerParams(has_side_effects=pltpu.SideEffectType.DATAFLOW_SIDE_EFFECTING),
    )(*thru, send_sems, recv_sems, local_sems, after)
    return outs[nf:]


def pair_start(grads, after, name):
    nw = len(grads)
    land = lax.empty((4, nw) + grads[0].shape[1:], grads[0].dtype)

    def body(*refs):
        ins, zone = refs[:nw], refs[nw]
        send_sems, recv_sems = refs[nw + 2:nw + 4]
        x, y, c = _mesh_pos()
        for j in range(4):
            for w in range(nw):
                pltpu.make_async_remote_copy(
                    src_ref=ins[w].at[2 * j + (1 - c)], dst_ref=zone.at[j, w], send_sem=send_sems.at[0],
                    recv_sem=recv_sems.at[0], device_id=(x, y, 1 - c), device_id_type=MESH).start()
        refs[-1][...] = jnp.zeros(refs[-1].shape, F32)

    sems, thru, token = _split_call(body, name, list(grads) + [land], (1, 1), extra=(after,))
    return (sems, thru, nw), token


def pair_wait(started, after, name):
    sems, thru, nw = started

    def body(*refs):
        zone = refs[nw]
        send_sems, recv_sems = refs[nw + 1:nw + 3]
        x, y, c = _mesh_pos()
        cp = pltpu.make_async_remote_copy(src_ref=zone, dst_ref=zone, send_sem=send_sems.at[0],
                                          recv_sem=recv_sems.at[0], device_id=(x, y, 1 - c), device_id_type=MESH)
        cp.wait_send()
        cp.wait_recv()

    outs = _split_call(body, name, thru, (1, 1), extra=(*sems, after), with_token=False)
    return outs[:nw], outs[nw]


def pair_sum(grads, land, name):
    nw = len(grads)
    _, r, c_dim = grads[0].shape

    def body(*refs):
        g_refs, l_ref, o_ref = refs[:nw], refs[nw], refs[nw + 1]
        core = lax.axis_index("c")
        for w in range(nw):
            o_ref[0, w] = (g_refs[w][0, core].astype(F32) + l_ref[0, w].astype(F32)).astype(BF16)

    return pl.pallas_call(
        body, name=name, grid=(4,),
        in_specs=[pl.BlockSpec((1, 2, r, c_dim), lambda j: (j, 0, 0, 0))] * nw
                 + [pl.BlockSpec((1, nw, r, c_dim), lambda j: (j, 0, 0, 0))],
        out_specs=pl.BlockSpec((1, nw, r, c_dim), lambda j: (j, 0, 0, 0)),
        out_shape=jax.ShapeDtypeStruct((4, nw, r, c_dim), BF16),
        compiler_params=_params(),
    )(*[g.reshape(4, 2, r, c_dim) for g in grads], land)


def _other_chips():
    x, y, c = _mesh_pos()
    chips = []
    for rel in range(1, 4):
        px, py = (1 - x if rel & 2 else x), (1 - y if rel & 1 else y)
        chips.append((px, py, 2 * px + py))
    return 2 * x + y, c, chips


def chip_start(pair_sums, after, name):
    land = lax.empty(pair_sums.shape, pair_sums.dtype)

    def body(*refs):
        h_ref, zone = refs[0], refs[1]
        send_sems, recv_sems, local_sem = refs[3:6]
        mine, c, chips = _other_chips()
        pltpu.make_async_copy(h_ref.at[mine], zone.at[mine], local_sem.at[0]).start()
        for k, (px, py, j) in enumerate(chips):
            pltpu.make_async_remote_copy(
                src_ref=h_ref.at[j], dst_ref=zone.at[mine], send_sem=send_sems.at[k], recv_sem=recv_sems.at[k],
                device_id=(px, py, c), device_id_type=MESH).start()
        refs[-1][...] = jnp.zeros(refs[-1].shape, F32)

    sems, thru, token = _split_call(body, name, [pair_sums, land], (3, 3, 1), extra=(after,))
    return (sems, thru), token


def chip_wait(started, after, name):
    sems, thru = started

    def body(*refs):
        zone = refs[1]
        send_sems, recv_sems, local_sem = refs[2:5]
        _, c, chips = _other_chips()
        for k, (px, py, _) in enumerate(chips):
            cp = pltpu.make_async_remote_copy(
                src_ref=zone.at[0], dst_ref=zone.at[0], send_sem=send_sems.at[k], recv_sem=recv_sems.at[k],
                device_id=(px, py, c), device_id_type=MESH)
            cp.wait_send()
            cp.wait_recv()
        pltpu.make_async_copy(zone.at[0], zone.at[0], local_sem.at[0]).wait()

    return _split_call(body, name, thru, (3, 3, 1), extra=(*sems, after), with_token=False)[1]


def share_small(parts, after):
    n = len(parts)

    def body(*refs):
        ins, outs = refs[:n], refs[n + 1:2 * n + 1]
        send_sems, recv_sems, local_sems = refs[2 * n + 1:]
        me, peers = _peers()
        copies = []
        for i in range(n):
            copies.append(pltpu.make_async_copy(ins[i], outs[i].at[me], local_sems.at[i]))
            copies += [pltpu.make_async_remote_copy(
                src_ref=ins[i], dst_ref=outs[i].at[me], send_sem=send_sems.at[7 * i + k],
                recv_sem=recv_sems.at[7 * i + k], device_id=peer, device_id_type=MESH)
                for k, peer in enumerate(peers)]
        for cp in copies:
            cp.start()
        for cp in copies:
            cp.wait()

    vm = pl.BlockSpec(memory_space=pltpu.VMEM)
    return pl.pallas_call(
        body, name="share_small", in_specs=[vm] * n + [pl.BlockSpec(memory_space=pl.ANY)], out_specs=[vm] * n,
        out_shape=[jax.ShapeDtypeStruct((N_DEV,) + p.shape, p.dtype) for p in parts],
        scratch_shapes=[pltpu.SemaphoreType.DMA((7 * n,)), pltpu.SemaphoreType.DMA((7 * n,)),
                        pltpu.SemaphoreType.DMA((n,))],
    )(*parts, after)


def _adamw_math(w, g, m, v):
    m = ADAM_B1 * m + (1.0 - ADAM_B1) * g
    v = ADAM_B2 * v + (1.0 - ADAM_B2) * (g * g)
    m_hat = m / (1.0 - ADAM_B1 ** ADAM_STEP)
    v_hat = v / (1.0 - ADAM_B2 ** ADAM_STEP)
    delta = -ADAM_LR * (m_hat / (jnp.sqrt(v_hat) + ADAM_EPS) + ADAM_WD * w)
    return delta, m, v


def adamw_layer(zone, w_idx, layer, w, m, v, prev, after, name):
    n_src, _, r, c = zone.shape
    depth = w.shape[0]
    if prev is None:
        prev = tuple(lax.empty((depth, r, c), F32) for _ in range(4))
    tr = r // 2 if r % 16 == 0 else r

    def body(z_ref, w_ref, m_ref, v_ref, *rest):
        g_ref, d_ref, mo_ref, vo_ref = rest[5:]
        g = z_ref[0].astype(F32)
        for src in range(1, n_src):
            g = g + z_ref[src].astype(F32)
        g_ref[...] = g
        d_ref[...], mo_ref[...], vo_ref[...] = _adamw_math(w_ref[...], g, m_ref[...], v_ref[...])

    rows = pl.BlockSpec((None, tr, c), lambda i: (layer, i, 0))
    anywhere = pl.BlockSpec(memory_space=pl.ANY)
    return pl.pallas_call(
        body, name=name, grid=(r // tr,),
        in_specs=[pl.BlockSpec((n_src, None, tr, c), lambda i: (0, w_idx, i, 0)), rows, rows, rows]
                 + [anywhere] * 5,
        out_specs=[rows] * 4,
        out_shape=[jax.ShapeDtypeStruct((depth, r, c), F32)] * 4,
        input_output_aliases={4 + k: k for k in range(4)},
        compiler_params=_params(),
    )(zone, w, m, v, *prev, after)


def adamw_small(ws, recvs, ms, vs, name):
    n = len(ws)

    def body(*refs):
        w_refs, r_refs, m_refs, v_refs = (refs[i * n:(i + 1) * n] for i in range(4))
        g_refs, d_refs, mo_refs, vo_refs = (refs[(4 + i) * n:(5 + i) * n] for i in range(4))
        for i in range(n):
            g = r_refs[i][0]
            for src in range(1, N_DEV):
                g = g + r_refs[i][src]
            g_refs[i][...] = g
            d_refs[i][...], mo_refs[i][...], vo_refs[i][...] = _adamw_math(w_refs[i][...], g, m_refs[i][...],
                                                                            v_refs[i][...])

    vm = pl.BlockSpec(memory_space=pltpu.VMEM)
    outs = pl.pallas_call(
        body, name=name, in_specs=[vm] * (4 * n), out_specs=[vm] * (4 * n),
        out_shape=[jax.ShapeDtypeStruct(w.shape, F32) for w in ws] * 4,
        compiler_params=pltpu.CompilerParams(vmem_limit_bytes=V7X_VMEM_LIMIT),
    )(*ws, *recvs, *ms, *vs)
    return [outs[i * n:(i + 1) * n] for i in range(4)]


SMALL_NAMES = ("ffn1_norm", "mix_norm", "ffn2_norm", "b_gate", "na_q_norm", "na_k_norm", "sw_q_norm", "sw_k_norm",
               "na_rpb", "sw_sink", "t5_rel_table")


def kernel(x, ffn1_norm, ffn1_w_gate, ffn1_w_up, ffn1_w_down, mix_norm, w_in, b_gate, na_q_norm, na_k_norm, na_rpb, sw_q_norm, sw_k_norm, sw_sink, t5_rel_table, w_branch_na, w_branch_sw, w_out, ffn2_norm, ffn2_w_gate, ffn2_w_up, ffn2_w_down, loss_target, m_ffn1_norm, m_ffn1_w_gate, m_ffn1_w_up, m_ffn1_w_down, m_mix_norm, m_w_in, m_b_gate, m_na_q_norm, m_na_k_norm, m_na_rpb, m_sw_q_norm, m_sw_k_norm, m_sw_sink, m_t5_rel_table, m_w_branch_na, m_w_branch_sw, m_w_out, m_ffn2_norm, m_ffn2_w_gate, m_ffn2_w_up, m_ffn2_w_down, v_ffn1_norm, v_ffn1_w_gate, v_ffn1_w_up, v_ffn1_w_down, v_mix_norm, v_w_in, v_b_gate, v_na_q_norm, v_na_k_norm, v_na_rpb, v_sw_q_norm, v_sw_k_norm, v_sw_sink, v_t5_rel_table, v_w_branch_na, v_w_branch_sw, v_w_out, v_ffn2_norm, v_ffn2_w_gate, v_ffn2_w_up, v_ffn2_w_down):
    weights = dict(ffn1_norm=ffn1_norm, ffn1_w_gate=ffn1_w_gate, ffn1_w_up=ffn1_w_up, ffn1_w_down=ffn1_w_down,
                   mix_norm=mix_norm, w_in=w_in, b_gate=b_gate, na_q_norm=na_q_norm, na_k_norm=na_k_norm,
                   na_rpb=na_rpb, sw_q_norm=sw_q_norm, sw_k_norm=sw_k_norm, sw_sink=sw_sink,
                   t5_rel_table=t5_rel_table, w_branch_na=w_branch_na, w_branch_sw=w_branch_sw, w_out=w_out,
                   ffn2_norm=ffn2_norm, ffn2_w_gate=ffn2_w_gate, ffn2_w_up=ffn2_w_up, ffn2_w_down=ffn2_w_down)
    mom_m = dict(ffn1_norm=m_ffn1_norm, ffn1_w_gate=m_ffn1_w_gate, ffn1_w_up=m_ffn1_w_up, ffn1_w_down=m_ffn1_w_down,
                 mix_norm=m_mix_norm, w_in=m_w_in, b_gate=m_b_gate, na_q_norm=m_na_q_norm, na_k_norm=m_na_k_norm,
                 na_rpb=m_na_rpb, sw_q_norm=m_sw_q_norm, sw_k_norm=m_sw_k_norm, sw_sink=m_sw_sink,
                 t5_rel_table=m_t5_rel_table, w_branch_na=m_w_branch_na, w_branch_sw=m_w_branch_sw, w_out=m_w_out,
                 ffn2_norm=m_ffn2_norm, ffn2_w_gate=m_ffn2_w_gate, ffn2_w_up=m_ffn2_w_up, ffn2_w_down=m_ffn2_w_down)
    mom_v = dict(ffn1_norm=v_ffn1_norm, ffn1_w_gate=v_ffn1_w_gate, ffn1_w_up=v_ffn1_w_up, ffn1_w_down=v_ffn1_w_down,
                 mix_norm=v_mix_norm, w_in=v_w_in, b_gate=v_b_gate, na_q_norm=v_na_q_norm, na_k_norm=v_na_k_norm,
                 na_rpb=v_na_rpb, sw_q_norm=v_sw_q_norm, sw_k_norm=v_sw_k_norm, sw_sink=v_sw_sink,
                 t5_rel_table=v_t5_rel_table, w_branch_na=v_w_branch_na, w_branch_sw=v_w_branch_sw, w_out=v_w_out,
                 ffn2_norm=v_ffn2_norm, ffn2_w_gate=v_ffn2_w_gate, ffn2_w_up=v_ffn2_w_up, ffn2_w_down=v_ffn2_w_down)
    order = list(weights)

    depth = ffn1_norm.shape[0]
    s, d = x.shape[1], x.shape[2]
    xs = x[0]
    tr = lambda w: jnp.swapaxes(w, -1, -2)

    merge = lambda t: t.reshape(t.shape[0], N_DEV * t.shape[2], t.shape[3])
    no_dep = jnp.zeros((8, LANES), F32)

    def shards_of(kind, l):
        stack = lambda *ws: jnp.stack(ws).astype(BF16)
        if kind == "ffn1":
            return [stack(tr(ffn1_w_gate[l]), tr(ffn1_w_up[l]), ffn1_w_down[l])]
        if kind == "win":
            return [stack(tr(w_in[l]))]
        return [stack(tr(ffn2_w_gate[l]), tr(ffn2_w_up[l]), ffn2_w_down[l]), stack(w_out[l]),
                stack(tr(w_branch_na[l]), tr(w_branch_sw[l]))]

    def start(kind, l, after):
        return gather_start(shards_of(kind, l), after, f"gather_{kind}_{l}")

    def arrive(started, kind, l, after):
        zones = gather_wait(started, after, f"gather_{kind}_{l}_wait")
        return forward_start(zones, no_dep, f"forward_{kind}_{l}")

    def finish(fwd, kind, l, after):
        return [merge(z) for z in forward_wait(fwd, after, f"forward_{kind}_{l}_wait")]

    bd = jnp.asarray(np.kron(np.eye(NA_WIDTH // HEAD_DIM), np.full((HEAD_DIM, HEAD_DIM), 1.0 / HEAD_DIM)), BF16)
    bmap = jnp.asarray(_t5_bucket_map())
    tile8 = lambda g: jnp.tile(g, NA_WIDTH // HEAD_DIM).reshape(1, NA_WIDTH)
    tile2 = lambda g: jnp.tile(g, SW_KV_WIDTH // HEAD_DIM).reshape(1, SW_KV_WIDTH)

    st_first, tok = start("ffn1", 0, no_dep)
    t5b = t5_expand(t5_rel_table, bmap, tok, "t5_expand").reshape(SW_STACK, 3 * SW_BLOCK)
    t2_tables = [rpb_expand(_rpb_rows(na_rpb[l]), tok, f"rpb_expand_{l}") for l in range(depth)]
    fwd, _ = arrive(st_first, "ffn1", 0, t2_tables[-1])
    st_win, dep = start("win", 0, t5b)
    (first,) = finish(fwd, "ffn1", 0, dep)

    saved = []
    layer_w = {0: dict(wg1=(first, 0), wu1=(first, 1), wd1=(first, 2))}
    cur = xs
    for l in range(depth):
        sv = {}
        lw = layer_w[l]
        sv["x0"] = cur
        cur, sv["xn1"], sv["hg1"], sv["hu1"], sv["act1"] = ffn_both(
            cur, ffn1_norm[l][None], lw["wg1"], lw["wu1"], lw["wd1"], dep, f"ffn1_{l}")
        sv["x1"] = cur
        fwd, _ = arrive(st_win, "win", l, cur)
        st_rest, tok = start("rest", l, cur)
        (zb,) = finish(fwd, "win", l, tok)
        lw["win"] = (zb, 0)
        sv["gains"] = (tile8(na_q_norm[l]), tile8(na_k_norm[l]), tile8(sw_q_norm[l]), tile2(sw_k_norm[l]))
        sv["hn"], sv["zq"], sv["qa"], sv["ka"], sv["qs"], sv["ks"], sv["gt"] = mix_in(
            cur, mix_norm[l][None], lw["win"], b_gate[l][None], *sv["gains"], bd, f"mix_in_{l}")
        sv["t2"] = t2_tables[l]
        sv["o_na"] = na_fwd(sv["qa"], sv["ka"], sv["zq"], sv["t2"], f"na_fwd_{l}")
        dep = no_dep
        if l + 1 < depth:
            st_ffn1, dep = start("ffn1", l + 1, sv["o_na"])
        sv["o_sw"] = sw_fwd(sv["qs"], sv["ks"], sv["zq"], t5b, sw_sink[l], dep, f"sw_fwd_{l}")
        fwd, tok = arrive(st_rest, "rest", l, sv["o_sw"])
        za, zc, zd = finish(fwd, "rest", l, tok)
        lw.update(wg2=(za, 0), wu2=(za, 1), wd2=(za, 2), wout=(zc, 0), wna=(zd, 0), wsw=(zd, 1))
        cur, sv["a_na"], sv["a_sw"], sv["merged"] = merge_out(
            cur, sv["o_na"], sv["o_sw"], sv["gt"], lw["wna"], lw["wsw"], lw["wout"], f"merge_out_{l}")
        sv["x2"] = cur
        dep = no_dep
        if l + 1 < depth:
            st_win, dep = start("win", l + 1, cur)
        sv["xn2"], sv["hg2"], sv["hu2"], sv["act2"] = ffn_up(cur, ffn2_norm[l][None], lw["wg2"], lw["wu2"], dep,
                                                             f"ffn2_up_{l}")
        dep = no_dep
        if l + 1 < depth:
            fwd, dep = arrive(st_ffn1, "ffn1", l + 1, sv["act2"])
        if l + 1 < depth:
            cur = ffn_down(cur, sv["act2"], lw["wd2"], dep, f"ffn2_down_{l}")
            (za,) = finish(fwd, "ffn1", l + 1, cur)
            layer_w[l + 1] = dict(wg1=(za, 0), wu1=(za, 1), wd1=(za, 2))
        else:
            dx, loss_acc = ffn_down(cur, sv["act2"], lw["wd2"], dep, f"ffn2_down_{l}", target=loss_target[0])
        dep = no_dep
        saved.append(sv)

    loss = lax.psum(jnp.sum(loss_acc) * (0.5 / d), ("x", "y", "c"))

    split = lambda t: t.reshape(N_DEV, t.shape[0] // N_DEV, t.shape[1])
    pending = {}
    last_key = "ffn1_0"
    two_level = {last_key}
    small = {k: [None] * depth for k in SMALL_NAMES if k != "t5_rel_table"}
    dbias_sw = []
    for l in reversed(range(depth)):
        sv = saved[l]
        lw = layer_w[l]
        wg1, wu1, wd1, wg2, wu2, wd2 = (lw[k] for k in ("wg1", "wu1", "wd1", "wg2", "wu2", "wd2"))
        win_t, wout_l, wna_t, wsw_t = lw["win"], lw["wout"], lw["wna"], lw["wsw"]
        blocks = ((2, "x2", "xn2", "hg2", "hu2", "act2", wg2, wu2, wd2, "ffn2_norm", 3),
                  (1, "x0", "xn1", "hg1", "hu1", "act1", wg1, wu1, wd1, "ffn1_norm", 0))

        def ffn_backward(dx, blk):
            tag, xk, xnk, hgk, huk, actk, wg, wu, wd, norm_name, slot = blk
            gains = weights[norm_name]
            dxb, dhg, dhu = ffn_bwd_act(dx, wd, sv[hgk], sv[huk], f"ffn{tag}_bwd_act_{l}")
            gwg, gwu, gwd = tn_matmul([(dhg, sv[xnk], 1.0), (dhu, sv[xnk], 1.0), (sv[actk], dxb, 0.5)],
                                      f"ffn{tag}_dw_{l}")
            key = f"ffn{tag}_{l}"
            blocks_of = [split(gwg), split(gwu), split(gwd)]
            if key in two_level:
                paired, token = pair_start(blocks_of, dxb, f"pair_{key}")
            else:
                pending[key], token = scatter_start([blocks_of], f"scatter_{key}")
            dx, dg = proj_bwd_norm([dhg, dhu], [wg, wu], sv[xk], gains[l][None], dx, token, f"ffn{tag}_bwd_x_{l}")
            token = no_dep
            if key in two_level:
                thru, land = pair_wait(paired, dx, f"pair_{key}_wait")
                pending[key], token = chip_start(pair_sum(thru, land, f"pair_sum_{key}"), dg, f"chips_{key}")
            small[norm_name][l] = dg[0]
            return dx, token

        dx, token = ffn_backward(dx, blocks[0])
        dxb, dzg, da_na, da_sw, do_na, do_sw, dbg = mix_bwd_out(
            dx, sv["gt"], sv["a_na"], sv["a_sw"], wna_t, wsw_t, wout_l, token, f"mix_bwd_out_{l}")
        small["b_gate"][l] = dbg[0]
        gwout, gwna, gwsw = tn_matmul([(sv["merged"], dxb, 1.0), (da_na, sv["o_na"], 1.0), (da_sw, sv["o_sw"], 1.0)],
                                      f"mix_dw_{l}")
        dqa, dka, dva, dt2 = na_bwd(sv["qa"], sv["ka"], sv["zq"], sv["t2"], sv["o_na"], do_na, f"na_bwd_{l}")
        dqs, dks, dvs, dbias, dsink = sw_bwd(sv["qs"], sv["ks"], sv["zq"], t5b, sw_sink[l], sv["o_sw"], do_sw,
                                             f"sw_bwd_{l}")
        dbias_sw.append(dbias.reshape(SW_HEADS, SW_BLOCK, 3 * SW_BLOCK))
        small["sw_sink"][l] = jnp.sum(dsink[:, 0].reshape(SW_HEADS, SW_BLOCK), axis=1)
        small["na_rpb"][l] = _rpb_from_rows(rpb_reduce(dt2, f"rpb_reduce_{l}"))
        dz, dgqa, dgka, dgqs, dgks = qk_norm_bwd(dqa, dka, dva, dqs, dks, dvs, sv["zq"], dzg, *sv["gains"], bd,
                                                 f"qk_norm_bwd_{l}")
        fold = lambda g: jnp.sum(g.reshape(-1, HEAD_DIM), axis=0)
        small["na_q_norm"][l], small["na_k_norm"][l] = fold(dgqa), fold(dgka)
        small["sw_q_norm"][l], small["sw_k_norm"][l] = fold(dgqs), fold(dgks)
        (gwin,) = tn_matmul([(dz, sv["hn"], 1.0)], f"dwin_{l}")
        pending[f"mix_{l}"], token = scatter_start([[split(gwout)], [split(gwna), split(gwsw)], [split(gwin)]],
                                                   f"scatter_mix_{l}")
        dx, dg = proj_bwd_norm([dz], [win_t], sv["x1"], mix_norm[l][None], dx, token, f"mix_bwd_x_{l}")
        small["mix_norm"][l] = dg[0]
        dx, tail = ffn_backward(dx, blocks[1])

    dtab = t5_reduce(dbias_sw, bmap, "t5_reduce")
    small_parts = {k: jnp.stack(v) for k, v in small.items()}
    small_parts["t5_rel_table"] = jnp.transpose(dtab[:, :, 0])

    grads, delta, new_m, new_v = {}, {}, {}, {}
    state = {}
    chain = [tail]
    members = {"ffn": lambda t: [(f"ffn{t}_w_gate", 0, 0, True), (f"ffn{t}_w_up", 0, 1, True),
                                 (f"ffn{t}_w_down", 0, 2, False)],
               "mix": lambda t: [("w_out", 0, 0, False), ("w_branch_na", 1, 0, True), ("w_branch_sw", 1, 1, True),
                                 ("w_in", 2, 0, True)]}

    def collect(key):
        if key in two_level:
            zones = [chip_wait(pending[key], chain[0], f"wait_{key}")]
        else:
            zones = scatter_wait(pending[key], chain[0], f"wait_{key}")
        kind, l = key.split("_")
        for k, zi, wi, transposed in members[kind[:3]](kind[3:]):
            view = tr if transposed else (lambda t: t)
            state[k] = adamw_layer(zones[zi], wi, int(l), view(weights[k]), view(mom_m[k]), view(mom_v[k]),
                                   state.get(k), chain[0], f"adamw_{k}_{l}")
            chain[0] = state[k][1]
            if all(f"{kind}_{j}" in done for j in range(depth) if j != int(l)):
                grads[k], delta[k], new_m[k], new_v[k] = (view(t) for t in state[k])
        done.add(key)

    done = set()
    for key in pending:
        if key != last_key:
            collect(key)
    collect(last_key)
    recvs = share_small([small_parts[k] for k in SMALL_NAMES], chain[0])
    results = adamw_small([weights[k] for k in SMALL_NAMES], recvs, [mom_m[k] for k in SMALL_NAMES],
                          [mom_v[k] for k in SMALL_NAMES], "adamw_small")
    for dst, outs in zip((grads, delta, new_m, new_v), results):
        dst.update(dict(zip(SMALL_NAMES, outs)))

    return (loss, dx[None], *[grads[k] for k in order], *[delta[k] for k in order],
            *[new_m[k] for k in order], *[new_v[k] for k in order])
```

```python
import functools
import math

import numpy as np
import jax
import jax.numpy as jnp
from jax import lax
from jax.experimental import pallas as pl
from jax.experimental.pallas import tpu as pltpu

F32 = jnp.float32
BF16 = jnp.bfloat16
MESH = pl.DeviceIdType.MESH

N_DEV = 8
EPS = 1e-6
NEG = -1e30
HEAD_DIM = 64
GRID_W = 64
NA_ROWS = 8
NA_COLS = 16
NA_WIDTH = 512
SW_Q_WIDTH = 512
SW_KV_WIDTH = 128
SW_BLOCK = 128
SW_HEADS = 8
SW_REP = 4
REL_BUCKETS = 32
REL_MAX_DIST = 128
QKV_WIDTH = 3 * NA_WIDTH + SW_Q_WIDTH + 2 * SW_KV_WIDTH
SCALE = 1.0 / math.sqrt(HEAD_DIM)

ADAM_LR = 0.001
ADAM_B1 = 0.9
ADAM_B2 = 0.999
ADAM_EPS = 1e-08
ADAM_WD = 0.01
ADAM_STEP = 10

V7X_VMEM_LIMIT = 56 * 1024 * 1024
LANES = 128
MXU_TILE = 256

NT = (((1,), (1,)), ((), ()))
TN = (((0,), (0,)), ((), ()))


def _params(n_grid=1):
    return pltpu.CompilerParams(dimension_semantics=("arbitrary",) * n_grid,
                                vmem_limit_bytes=V7X_VMEM_LIMIT)


def _row_tile(s):
    for t in (512, 256, 128, 64, 32, 16, 8):
        if s % t == 0:
            return t
    raise ValueError(s)


def _tn_tile(n):
    best = max(t for t in range(LANES, min(n, 2304) + 1, LANES) if n % t == 0) if n % LANES == 0 else n
    return best // 2 if best == n and n >= 1024 else best


ONCE = pl.Buffered(1)


def _col_chunk(n):
    return MXU_TILE if n % MXU_TILE == 0 else n


def _stream_pieces(rows):
    if rows % MXU_TILE:
        return [(0, rows)]
    tiles = rows // MXU_TILE
    n = min(4, tiles)
    sizes = [(tiles // n + (1 if i < tiles % n else 0)) * MXU_TILE for i in range(n)]
    return [(sum(sizes[:i]), sizes[i]) for i in range(n)]


def _dot(a, b):
    return jnp.dot(a, b, preferred_element_type=F32)


def _dotg(a, b, dn):
    return lax.dot_general(a, b, dn, preferred_element_type=F32)


def _sigmoid(v):
    return 1.0 / (1.0 + jnp.exp(-v))


def _rstd(xv):
    return lax.rsqrt(jnp.mean(xv * xv, axis=-1, keepdims=True) + EPS)


def _full(shape):
    nd = len(shape)
    return pl.BlockSpec(shape, lambda i, _n=nd: (0,) * _n)


def _rows(tm, width):
    return pl.BlockSpec((tm, width), lambda i: (i, 0))


def _mat(stack, idx):
    return pl.BlockSpec((None,) + tuple(stack.shape[1:]), lambda i, _w=idx: (_w, 0, 0), pipeline_mode=ONCE)


def _group_mean(v, bd):
    hi = v.astype(BF16)
    lo = (v - hi.astype(F32)).astype(BF16)
    return _dot(hi, bd) + _dot(lo, bd)


def _swiglu_tile(xn, wg_ref, wu_ref, dg_ref, du_ref, act_ref, fc):
    for c0 in range(0, wg_ref.shape[0], fc):
        hg = _dotg(xn, wg_ref[c0:c0 + fc, :], NT)
        hu = _dotg(xn, wu_ref[c0:c0 + fc, :], NT)
        sg = _sigmoid(hg)
        silu = hg * sg
        du_ref[:, c0:c0 + fc] = silu.astype(BF16)
        dg_ref[:, c0:c0 + fc] = (hu * (sg + silu * (1.0 - sg))).astype(BF16)
        act_ref[:, c0:c0 + fc] = (silu * hu).astype(BF16)


def ffn_up(x, gain, wg_t, wu_t, dep, name):
    s, d = x.shape
    f = wg_t[0].shape[1]
    tm = _row_tile(s)
    fc = _col_chunk(f)

    def body(x_ref, g_ref, wg_ref, wu_ref, dep_ref, xn_ref, dg_ref, du_ref, act_ref):
        xv = x_ref[...]
        xn = (xv * _rstd(xv) * g_ref[...]).astype(BF16)
        xn_ref[...] = xn
        _swiglu_tile(xn, wg_ref, wu_ref, dg_ref, du_ref, act_ref, fc)

    return pl.pallas_call(
        body, name=name, grid=(s // tm,),
        in_specs=[_rows(tm, d), _full((1, d)), _mat(*wg_t), _mat(*wu_t), _full(dep.shape)],
        out_specs=[_rows(tm, d), _rows(tm, f), _rows(tm, f), _rows(tm, f)],
        out_shape=[jax.ShapeDtypeStruct((s, d), BF16)] + [jax.ShapeDtypeStruct((s, f), BF16)] * 3,
        compiler_params=_params(),
    )(x, gain, wg_t[0], wu_t[0], dep)


def ffn_both(x, gain, wg_t, wu_t, wd, dep, name):
    s, d = x.shape
    f = wg_t[0].shape[1]
    tm = min(_row_tile(s), 256)
    fc = _col_chunk(f)

    def body(x_ref, g_ref, wg_ref, wu_ref, wd_ref, dep_ref, xo_ref, xn_ref, dg_ref, du_ref, act_ref):
        xv = x_ref[...]
        xn = (xv * _rstd(xv) * g_ref[...]).astype(BF16)
        xn_ref[...] = xn
        _swiglu_tile(xn, wg_ref, wu_ref, dg_ref, du_ref, act_ref, fc)
        xo_ref[...] = xv + 0.5 * _dot(act_ref[...], wd_ref[...])

    return pl.pallas_call(
        body, name=name, grid=(s // tm,),
        in_specs=[_rows(tm, d), _full((1, d)), _mat(*wg_t), _mat(*wu_t), _mat(*wd), _full(dep.shape)],
        out_specs=[_rows(tm, d), _rows(tm, d), _rows(tm, f), _rows(tm, f), _rows(tm, f)],
        out_shape=[jax.ShapeDtypeStruct((s, d), F32), jax.ShapeDtypeStruct((s, d), BF16)]
                  + [jax.ShapeDtypeStruct((s, f), BF16)] * 3,
        compiler_params=_params(),
    )(x, gain, wg_t[0], wu_t[0], wd[0], dep)


def ffn_down(x, act, wd, dep, name, target=None):
    s, d = x.shape
    f = act.shape[1]
    tm = _row_tile(s)

    def body(x_ref, a_ref, w_ref, dep_ref, *rest):
        y = x_ref[...] + 0.5 * _dot(a_ref[...], w_ref[...])
        if target is None:
            rest[0][...] = y
            return
        t_ref, dy_ref, acc_ref = rest

        @pl.when(pl.program_id(0) == 0)
        def _():
            acc_ref[...] = jnp.zeros(acc_ref.shape, F32)

        err = y - t_ref[...]
        dy_ref[...] = err * (1.0 / d)
        part = jnp.sum((err * err).reshape(tm // 8, 8, d), axis=0)
        acc = part[:, 0:LANES]
        for c0 in range(LANES, d, LANES):
            acc = acc + part[:, c0:c0 + LANES]
        acc_ref[...] = acc_ref[...] + acc

    ins = [_rows(tm, d), _rows(tm, f), _mat(*wd), _full(dep.shape)]
    if target is None:
        return pl.pallas_call(
            body, name=name, grid=(s // tm,), in_specs=ins, out_specs=_rows(tm, d),
            out_shape=jax.ShapeDtypeStruct((s, d), F32), compiler_params=_params(),
        )(x, act, wd[0], dep)
    return pl.pallas_call(
        body, name=name, grid=(s // tm,), in_specs=ins + [_rows(tm, d)],
        out_specs=[_rows(tm, d), _full((8, LANES))],
        out_shape=[jax.ShapeDtypeStruct((s, d), F32), jax.ShapeDtypeStruct((8, LANES), F32)],
        compiler_params=_params(),
    )(x, act, wd[0], dep, target)


def mix_in(x, gain, win_t, b_gate, gq_na, gk_na, gq_sw, gk_sw, bd, name):
    s, d = x.shape
    tm = _row_tile(s)
    gc = _col_chunk(2 * d)

    def body(x_ref, g_ref, w_ref, b_ref, gqa_ref, gka_ref, gqs_ref, gks_ref, bd_ref,
             hn_ref, zq_ref, qa_ref, ka_ref, qs_ref, ks_ref, gt_ref):
        xv = x_ref[...]
        hn = (xv * _rstd(xv) * g_ref[...]).astype(BF16)
        hn_ref[...] = hn

        def proj(c0, c1):
            return _dotg(hn, w_ref[c0:c1, :], NT)

        def headnorm(z, g, bdm):
            return z * lax.rsqrt(_group_mean(z * z, bdm) + EPS) * g

        bd512 = bd_ref[...]
        bd128 = bd_ref[0:SW_KV_WIDTH, 0:SW_KV_WIDTH]
        z = proj(0, 512)
        zq_ref[:, 0:512] = z.astype(BF16)
        qa_ref[...] = (headnorm(z, gqa_ref[...], bd512) * SCALE).astype(BF16)
        z = proj(512, 1024)
        zq_ref[:, 512:1024] = z.astype(BF16)
        ka_ref[...] = headnorm(z, gka_ref[...], bd512).astype(BF16)
        z = proj(1024, 1536)
        zq_ref[:, 1024:1536] = z.astype(BF16)
        z = proj(1536, 2048)
        zq_ref[:, 1536:2048] = z.astype(BF16)
        qs_ref[...] = (headnorm(z, gqs_ref[...], bd512) * SCALE).astype(BF16)
        z = proj(2048, 2176)
        zq_ref[:, 2048:2176] = z.astype(BF16)
        ks_ref[...] = headnorm(z, gks_ref[...], bd128).astype(BF16)
        z = proj(2176, 2304)
        zq_ref[:, 2176:2304] = z.astype(BF16)
        for c0 in range(0, 2 * d, gc):
            zg = proj(QKV_WIDTH + c0, QKV_WIDTH + c0 + gc) + b_ref[:, c0:c0 + gc]
            gt_ref[:, c0:c0 + gc] = _sigmoid(zg).astype(BF16)

    return pl.pallas_call(
        body, name=name, grid=(s // tm,),
        in_specs=[_rows(tm, d), _full((1, d)), _mat(*win_t), _full((1, 2 * d)),
                  _full((1, 512)), _full((1, 512)), _full((1, 512)), _full((1, 128)), _full((512, 512))],
        out_specs=[_rows(tm, d), _rows(tm, QKV_WIDTH), _rows(tm, 512), _rows(tm, 512), _rows(tm, 512),
                   _rows(tm, 128), _rows(tm, 2 * d)],
        out_shape=[jax.ShapeDtypeStruct((s, d), BF16), jax.ShapeDtypeStruct((s, QKV_WIDTH), BF16),
                   jax.ShapeDtypeStruct((s, 512), BF16), jax.ShapeDtypeStruct((s, 512), BF16),
                   jax.ShapeDtypeStruct((s, 512), BF16), jax.ShapeDtypeStruct((s, 128), BF16),
                   jax.ShapeDtypeStruct((s, 2 * d), BF16)],
        compiler_params=_params(),
    )(x, gain, win_t[0], b_gate, gq_na, gk_na, gq_sw, gk_sw, bd)


def _na_iotas():
    qc = lax.broadcasted_iota(jnp.int32, (GRID_W, LANES), 0)
    ln = lax.broadcasted_iota(jnp.int32, (GRID_W, LANES), 1)
    low = ln < GRID_W
    kc = jnp.where(low, ln, ln - GRID_W)
    diff = kc - qc + (NA_COLS - 1)
    qcs = jnp.clip(qc - NA_COLS // 2, 0, GRID_W - NA_COLS)
    inwin = (kc >= qcs) & (kc < qcs + NA_COLS)
    return diff, low, inwin


NA_RI = 2 * NA_ROWS - 1
NA_CI = 2 * NA_COLS - 1
NA_T2 = NA_RI + 1


def _rpb_rows(rpb):
    h = rpb.shape[0]
    padded = jnp.pad(rpb, ((0, 0), (1, 1), (0, GRID_W - NA_CI)))
    return jnp.concatenate([padded[:, :NA_T2], padded[:, 1:NA_T2 + 1]], axis=2).reshape(h, NA_T2, LANES)


def _rpb_from_rows(rows):
    return rows[:, 1:, :NA_CI] + rows[:, :NA_RI, GRID_W:GRID_W + NA_CI]


def rpb_expand(rows, dep, name):
    n_heads = rows.shape[0]

    def body(r_ref, dep_ref, o_ref):
        for h in range(n_heads):
            for e in range(NA_T2):
                line = jnp.broadcast_to(r_ref[h, e:e + 1, :], (GRID_W, LANES))
                o_ref[h, e] = pltpu.roll(line, LANES - (NA_COLS - 1), 1, stride=1, stride_axis=0)

    return pl.pallas_call(
        body, name=name,
        in_specs=[pl.BlockSpec(memory_space=pltpu.VMEM), pl.BlockSpec(memory_space=pltpu.VMEM)],
        out_specs=pl.BlockSpec(memory_space=pltpu.VMEM),
        out_shape=jax.ShapeDtypeStruct((n_heads, NA_T2, GRID_W, LANES), F32),
        compiler_params=pltpu.CompilerParams(vmem_limit_bytes=V7X_VMEM_LIMIT),
    )(rows, dep)


def rpb_reduce(dt2, name):
    n_heads = dt2.shape[0]
    flip = jnp.asarray(np.eye(GRID_W)[::-1], BF16)

    def body(d_ref, j_ref, o_ref):
        jm = j_ref[...]
        for h in range(n_heads):
            for e in range(NA_T2):
                dv = d_ref[h, e]
                hi = dv.astype(BF16)
                mid = (dv - hi.astype(F32)).astype(BF16)
                lo = (dv - hi.astype(F32) - mid.astype(F32)).astype(BF16)
                rev = _dot(jm, hi) + _dot(jm, mid) + _dot(jm, lo)
                back = pltpu.roll(rev, LANES + (NA_COLS - 1) - (GRID_W - 1), 1, stride=1, stride_axis=0)
                o_ref[h, e:e + 1, :] = jnp.sum(back, axis=0, keepdims=True)

    return pl.pallas_call(
        body, name=name,
        in_specs=[pl.BlockSpec(memory_space=pltpu.VMEM)] * 2,
        out_specs=pl.BlockSpec(memory_space=pltpu.VMEM),
        out_shape=jax.ShapeDtypeStruct((n_heads, NA_T2, LANES), F32),
        compiler_params=pltpu.CompilerParams(vmem_limit_bytes=V7X_VMEM_LIMIT),
    )(dt2, flip)


NA_TQ = 4
NA_TK = NA_TQ + NA_ROWS
NA_KCH = NA_TK // 2


def _na_tile_geometry(t, rows):
    r = t * NA_TQ
    kbase = jnp.clip(r - NA_ROWS // 2, 0, rows - NA_TK)
    starts = [jnp.clip(r + a - NA_ROWS // 2, 0, rows - NA_ROWS) for a in range(NA_TQ)]
    return r, kbase, starts


def _na_tile_mask(kbase, starts, low, inwin):
    half = jnp.where(low, 0, 1)
    cols = []
    for c in range(NA_KCH):
        krow = kbase + 2 * c + half
        cols.append(jnp.concatenate(
            [jnp.where(inwin & (krow >= st) & (krow < st + NA_ROWS), 0.0, NEG) for st in starts], axis=0))
    return jnp.concatenate(cols, axis=1)


def _na_tile_index(r, kbase, a, c):
    return jnp.clip(kbase + 2 * c - (r + a) + NA_ROWS, 0, NA_T2 - 1)


def _na_tile_scores(q, k, t2_ref, hh, r, kbase, madd):
    bias = jnp.concatenate(
        [jnp.concatenate([t2_ref[hh, _na_tile_index(r, kbase, a, c)] for a in range(NA_TQ)], axis=0)
         for c in range(NA_KCH)], axis=1)
    return _dotg(q, k, NT) + bias + madd


def _softmax_rows(sc):
    e = jnp.exp(sc - jnp.max(sc, axis=1, keepdims=True))
    return e * (1.0 / jnp.sum(e, axis=1, keepdims=True))


def na_fwd(qa, ka, zq, t2, name):
    s = qa.shape[0]
    rows = s // GRID_W
    n_pairs = NA_WIDTH // LANES
    v_blk0 = (2 * NA_WIDTH) // LANES

    assert rows % NA_TQ == 0 and rows >= NA_TK
    tq, tk = NA_TQ * GRID_W, NA_TK * GRID_W

    def body(q_ref, k_ref, v_ref, t2_ref, o_ref, s_scr, p_scr):
        _, low, inwin = _na_iotas()

        def tile(t, carry):
            r, kbase, starts = _na_tile_geometry(t, rows)
            madd = _na_tile_mask(kbase, starts, low, inwin)
            qr = pl.ds(pl.multiple_of(r * GRID_W, tq), tq)
            kr = pl.ds(pl.multiple_of(kbase * GRID_W, tq), tk)
            for hh in range(2):
                lanes = slice(HEAD_DIM * hh, HEAD_DIM * (hh + 1))
                s_scr[tq * hh:tq * (hh + 1), :] = _na_tile_scores(q_ref[qr, lanes], k_ref[kr, lanes], t2_ref, hh, r,
                                                                  kbase, madd)
            p_scr[...] = _softmax_rows(s_scr[...]).astype(BF16)
            for hh in range(2):
                lanes = slice(HEAD_DIM * hh, HEAD_DIM * (hh + 1))
                o_ref[qr, lanes] = _dot(p_scr[tq * hh:tq * (hh + 1), :], v_ref[kr, lanes]).astype(BF16)
            return carry

        lax.fori_loop(0, rows // NA_TQ, tile, 0)

    col = lambda off: pl.BlockSpec((s, LANES), lambda p, _o=off: (0, _o + p))
    return pl.pallas_call(
        body, name=name, grid=(n_pairs,),
        in_specs=[col(0), col(0), col(v_blk0),
                  pl.BlockSpec((2, NA_T2, GRID_W, LANES), lambda p: (p, 0, 0, 0))],
        out_specs=col(0),
        out_shape=jax.ShapeDtypeStruct((s, NA_WIDTH), BF16),
        scratch_shapes=[pltpu.VMEM((2 * tq, tk), F32), pltpu.VMEM((2 * tq, tk), BF16)],
        compiler_params=_params(),
    )(qa, ka, zq, t2)


def na_bwd(qa, ka, zq, t2, o_na, do_na, name):
    s = qa.shape[0]
    rows = s // GRID_W
    n_pairs = NA_WIDTH // LANES
    v_blk0 = (2 * NA_WIDTH) // LANES

    tq, tk = NA_TQ * GRID_W, NA_TK * GRID_W

    def body(q_ref, k_ref, v_ref, t2_ref, o_ref, do_ref, dq_ref, dk_ref, dv_ref, dt2_ref):
        _, low, inwin = _na_iotas()
        dk_ref[...] = jnp.zeros(dk_ref.shape, F32)
        dv_ref[...] = jnp.zeros(dv_ref.shape, F32)
        dt2_ref[...] = jnp.zeros(dt2_ref.shape, F32)

        def tile(t, carry):
            r, kbase, starts = _na_tile_geometry(t, rows)
            madd = _na_tile_mask(kbase, starts, low, inwin)
            qr = pl.ds(pl.multiple_of(r * GRID_W, tq), tq)
            kr = pl.ds(pl.multiple_of(kbase * GRID_W, tq), tk)
            for hh in range(2):
                lanes = slice(HEAD_DIM * hh, HEAD_DIM * (hh + 1))
                q, k, v = q_ref[qr, lanes], k_ref[kr, lanes], v_ref[kr, lanes]
                p = _softmax_rows(_na_tile_scores(q, k, t2_ref, hh, r, kbase, madd))
                do = do_ref[qr, lanes]
                delta = jnp.sum(do.astype(F32) * o_ref[qr, lanes].astype(F32), axis=1, keepdims=True)
                ds = p * (_dotg(do, v, NT) - delta)
                shared = {}
                for a in range(NA_TQ):
                    for c in range(NA_KCH):
                        shared.setdefault(2 * c - a, []).append(
                            ds[GRID_W * a:GRID_W * (a + 1), LANES * c:LANES * (c + 1)])
                for offset, parts in shared.items():
                    e = jnp.clip(offset + kbase - r + NA_ROWS, 0, NA_T2 - 1)
                    dt2_ref[hh, e] = dt2_ref[hh, e] + functools.reduce(jnp.add, parts)
                dsb = ds.astype(BF16)
                dq_ref[qr, lanes] = _dot(dsb, k)
                dk_ref[kr, lanes] = dk_ref[kr, lanes] + _dotg(dsb, q, TN)
                dv_ref[kr, lanes] = dv_ref[kr, lanes] + _dotg(p.astype(BF16), do, TN)
            return carry

        lax.fori_loop(0, rows // NA_TQ, tile, 0)

    col = lambda off: pl.BlockSpec((s, LANES), lambda p, _o=off: (0, _o + p))
    t2spec = pl.BlockSpec((2, NA_T2, GRID_W, LANES), lambda p: (p, 0, 0, 0))
    return pl.pallas_call(
        body, name=name, grid=(n_pairs,),
        in_specs=[col(0), col(0), col(v_blk0), t2spec, col(0), col(0)],
        out_specs=[col(0), col(0), col(0), t2spec],
        out_shape=[jax.ShapeDtypeStruct((s, NA_WIDTH), F32)] * 3 + [jax.ShapeDtypeStruct(t2.shape, F32)],
        compiler_params=_params(),
    )(qa, ka, zq, t2, o_na, do_na)


def _t5_bucket_map():
    rel = np.arange(3 * SW_BLOCK)[None, :] - SW_BLOCK - np.arange(SW_BLOCK)[:, None]
    nb = REL_BUCKETS // 2
    max_exact = nb // 2
    n = np.abs(rel)
    large = max_exact + (np.log(np.maximum(n, 1) / max_exact)
                         / np.log(REL_MAX_DIST / max_exact) * (nb - max_exact)).astype(np.int32)
    large = np.minimum(large, nb - 1)
    return ((rel > 0) * nb + np.where(n < max_exact, n, large)).astype(np.int32)


def t5_expand(table, bmap, dep, name):
    def body(tab_ref, bm_ref, dep_ref, o_ref):
        bm = bm_ref[...]
        for h in range(SW_HEADS):
            t = jnp.zeros(bm.shape, F32)
            for b in range(REL_BUCKETS):
                t = jnp.where(bm == b, tab_ref[b, h], t)
            o_ref[h] = t

    return pl.pallas_call(
        body, name=name,
        in_specs=[pl.BlockSpec(memory_space=pltpu.SMEM), pl.BlockSpec(memory_space=pltpu.VMEM),
                  pl.BlockSpec(memory_space=pltpu.VMEM)],
        out_specs=pl.BlockSpec(memory_space=pltpu.VMEM),
        out_shape=jax.ShapeDtypeStruct((SW_HEADS,) + bmap.shape, F32),
        compiler_params=pltpu.CompilerParams(vmem_limit_bytes=V7X_VMEM_LIMIT),
    )(table, bmap, dep)


def t5_reduce(dbias_list, bmap, name):
    n = len(dbias_list)

    def body(*refs):
        d_refs, bm_ref, o_ref = refs[:n], refs[n], refs[n + 1]
        bm = bm_ref[...]
        for h in range(SW_HEADS):
            dv = d_refs[0][h]
            for other in d_refs[1:]:
                dv = dv + other[h]
            rows = [jnp.sum(jnp.where(bm == b, dv, 0.0), axis=0, keepdims=True) for b in range(REL_BUCKETS)]
            r = jnp.concatenate(rows, axis=0)
            o_ref[h] = jnp.broadcast_to(jnp.sum(r, axis=1, keepdims=True), (REL_BUCKETS, LANES))

    return pl.pallas_call(
        body, name=name,
        in_specs=[pl.BlockSpec(memory_space=pltpu.VMEM)] * (n + 1),
        out_specs=pl.BlockSpec(memory_space=pltpu.VMEM),
        out_shape=jax.ShapeDtypeStruct((SW_HEADS, REL_BUCKETS, LANES), F32),
        compiler_params=pltpu.CompilerParams(vmem_limit_bytes=V7X_VMEM_LIMIT),
    )(*dbias_list, bmap)


def _sw_mask_iotas():
    a = lax.broadcasted_iota(jnp.int32, (SW_BLOCK, 3 * SW_BLOCK), 0)
    j = lax.broadcasted_iota(jnp.int32, (SW_BLOCK, 3 * SW_BLOCK), 1)
    inwin = jnp.abs(j - SW_BLOCK - a) <= SW_BLOCK
    return j, inwin


SW_STACK = SW_HEADS * SW_BLOCK


def _sw_softmax(sc, sk):
    m = jnp.maximum(jnp.max(sc, axis=1, keepdims=True), sk)
    e = jnp.exp(sc - m)
    es = jnp.exp(sk - m)
    inv = 1.0 / (jnp.sum(e, axis=1, keepdims=True) + es)
    return e * inv, es * inv


def _sw_prologue(k_ref, v_ref, kp, vp, sink_ref, s):
    pad = s + 2 * SW_BLOCK
    zeros = jnp.zeros((SW_BLOCK, SW_KV_WIDTH), BF16)
    kp[0:SW_BLOCK, :] = zeros
    vp[0:SW_BLOCK, :] = zeros
    kp[SW_BLOCK + s:pad, :] = zeros
    vp[SW_BLOCK + s:pad, :] = zeros
    kp[SW_BLOCK:SW_BLOCK + s, :] = k_ref[...]
    vp[SW_BLOCK:SW_BLOCK + s, :] = v_ref[...]
    return jnp.concatenate([jnp.full((SW_BLOCK, 1), sink_ref[h], F32) for h in range(SW_HEADS)], axis=0)


def sw_fwd(qs, ks, zq, t5b, sink, dep, name):
    s = qs.shape[0]
    nb = s // SW_BLOCK
    v_blk = (3 * NA_WIDTH + SW_Q_WIDTH + SW_KV_WIDTH) // LANES
    pad = s + 2 * SW_BLOCK

    def body(q_ref, k_ref, v_ref, b_ref, sink_ref, dep_ref, o_ref, kp, vp, s_scr, p_scr):
        sink_col = _sw_prologue(k_ref, v_ref, kp, vp, sink_ref, s)
        j, inwin = _sw_mask_iotas()

        def blk(n, carry):
            kpos = n * SW_BLOCK - SW_BLOCK + j
            madd = jnp.where(inwin & (kpos >= 0) & (kpos < s), 0.0, NEG)
            q0 = pl.multiple_of(n * SW_BLOCK, SW_BLOCK)
            qr, kr = pl.ds(q0, SW_BLOCK), pl.ds(q0, 3 * SW_BLOCK)
            for h in range(SW_HEADS):
                g = h // SW_REP
                s_scr[SW_BLOCK * h:SW_BLOCK * (h + 1), :] = _dotg(
                    q_ref[qr, HEAD_DIM * h:HEAD_DIM * (h + 1)], kp[kr, HEAD_DIM * g:HEAD_DIM * (g + 1)], NT) + madd
            p, _ = _sw_softmax(s_scr[...] + b_ref[...], sink_col)
            p_scr[...] = p.astype(BF16)
            for h in range(SW_HEADS):
                g = h // SW_REP
                o_ref[qr, HEAD_DIM * h:HEAD_DIM * (h + 1)] = _dot(
                    p_scr[SW_BLOCK * h:SW_BLOCK * (h + 1), :], vp[kr, HEAD_DIM * g:HEAD_DIM * (g + 1)]).astype(BF16)
            return carry

        lax.fori_loop(0, nb, blk, 0)

    return pl.pallas_call(
        body, name=name, grid=(1,),
        in_specs=[_full((s, SW_Q_WIDTH)), _full((s, SW_KV_WIDTH)),
                  pl.BlockSpec((s, SW_KV_WIDTH), lambda i: (0, v_blk)),
                  _full((SW_STACK, 3 * SW_BLOCK)), pl.BlockSpec(memory_space=pltpu.SMEM),
                  _full(dep.shape)],
        out_specs=_full((s, SW_Q_WIDTH)),
        out_shape=jax.ShapeDtypeStruct((s, SW_Q_WIDTH), BF16),
        scratch_shapes=[pltpu.VMEM((pad, SW_KV_WIDTH), BF16), pltpu.VMEM((pad, SW_KV_WIDTH), BF16),
                        pltpu.VMEM((SW_STACK, 3 * SW_BLOCK), F32), pltpu.VMEM((SW_STACK, 3 * SW_BLOCK), BF16)],
        compiler_params=_params(),
    )(qs, ks, zq, t5b, sink, dep)


def sw_bwd(qs, ks, zq, t5b, sink, o_sw, do_sw, name):
    s = qs.shape[0]
    nb = s // SW_BLOCK
    v_blk = (3 * NA_WIDTH + SW_Q_WIDTH + SW_KV_WIDTH) // LANES
    pad = s + 2 * SW_BLOCK

    def body(q_ref, k_ref, v_ref, b_ref, sink_ref, o_ref, do_ref,
             dq_ref, dk_ref, dv_ref, db_ref, dsk_ref, kp, vp, dkp, dvp, s_scr, dp_scr, ds_scr, p_scr):
        sink_col = _sw_prologue(k_ref, v_ref, kp, vp, sink_ref, s)
        dkp[...] = jnp.zeros(dkp.shape, F32)
        dvp[...] = jnp.zeros(dvp.shape, F32)
        db_ref[...] = jnp.zeros(db_ref.shape, F32)
        dsk_ref[...] = jnp.zeros(dsk_ref.shape, F32)
        j, inwin = _sw_mask_iotas()

        def blk(n, carry):
            kpos = n * SW_BLOCK - SW_BLOCK + j
            madd = jnp.where(inwin & (kpos >= 0) & (kpos < s), 0.0, NEG)
            q0 = pl.multiple_of(n * SW_BLOCK, SW_BLOCK)
            qr, kr = pl.ds(q0, SW_BLOCK), pl.ds(q0, 3 * SW_BLOCK)
            deltas = []
            for h in range(SW_HEADS):
                g = h // SW_REP
                hl, kl = slice(HEAD_DIM * h, HEAD_DIM * (h + 1)), slice(HEAD_DIM * g, HEAD_DIM * (g + 1))
                rows = slice(SW_BLOCK * h, SW_BLOCK * (h + 1))
                do = do_ref[qr, hl]
                s_scr[rows, :] = _dotg(q_ref[qr, hl], kp[kr, kl], NT) + madd
                dp_scr[rows, :] = _dotg(do, vp[kr, kl], NT)
                deltas.append(jnp.sum(do.astype(F32) * o_ref[qr, hl].astype(F32), axis=1, keepdims=True))
            delta = jnp.concatenate(deltas, axis=0)
            p, ps = _sw_softmax(s_scr[...] + b_ref[...], sink_col)
            ds = p * (dp_scr[...] - delta)
            db_ref[...] = db_ref[...] + ds
            dsk_ref[...] = dsk_ref[...] - jnp.broadcast_to(ps * delta, (SW_STACK, LANES))
            ds_scr[...] = ds.astype(BF16)
            p_scr[...] = p.astype(BF16)
            for g in range(SW_HEADS // SW_REP):
                kl = slice(HEAD_DIM * g, HEAD_DIM * (g + 1))
                k = kp[kr, kl]
                dkw = jnp.zeros((3 * SW_BLOCK, HEAD_DIM), F32)
                dvw = jnp.zeros((3 * SW_BLOCK, HEAD_DIM), F32)
                for r in range(SW_REP):
                    h = g * SW_REP + r
                    hl, rows = slice(HEAD_DIM * h, HEAD_DIM * (h + 1)), slice(SW_BLOCK * h, SW_BLOCK * (h + 1))
                    dsb = ds_scr[rows, :]
                    dq_ref[qr, hl] = _dot(dsb, k)
                    dkw = dkw + _dotg(dsb, q_ref[qr, hl], TN)
                    dvw = dvw + _dotg(p_scr[rows, :], do_ref[qr, hl], TN)
                dkp[kr, kl] = dkp[kr, kl] + dkw
                dvp[kr, kl] = dvp[kr, kl] + dvw
            return carry

        lax.fori_loop(0, nb, blk, 0)
        dk_ref[...] = dkp[SW_BLOCK:SW_BLOCK + s, :]
        dv_ref[...] = dvp[SW_BLOCK:SW_BLOCK + s, :]

    bias_spec = _full((SW_STACK, 3 * SW_BLOCK))
    return pl.pallas_call(
        body, name=name, grid=(1,),
        in_specs=[_full((s, SW_Q_WIDTH)), _full((s, SW_KV_WIDTH)),
                  pl.BlockSpec((s, SW_KV_WIDTH), lambda i: (0, v_blk)),
                  bias_spec, pl.BlockSpec(memory_space=pltpu.SMEM),
                  _full((s, SW_Q_WIDTH)), _full((s, SW_Q_WIDTH))],
        out_specs=[_full((s, SW_Q_WIDTH)), _full((s, SW_KV_WIDTH)), _full((s, SW_KV_WIDTH)), bias_spec,
                   _full((SW_STACK, LANES))],
        out_shape=[jax.ShapeDtypeStruct((s, SW_Q_WIDTH), F32), jax.ShapeDtypeStruct((s, SW_KV_WIDTH), F32),
                   jax.ShapeDtypeStruct((s, SW_KV_WIDTH), F32),
                   jax.ShapeDtypeStruct((SW_STACK, 3 * SW_BLOCK), F32),
                   jax.ShapeDtypeStruct((SW_STACK, LANES), F32)],
        scratch_shapes=[pltpu.VMEM((pad, SW_KV_WIDTH), BF16), pltpu.VMEM((pad, SW_KV_WIDTH), BF16),
                        pltpu.VMEM((pad, SW_KV_WIDTH), F32), pltpu.VMEM((pad, SW_KV_WIDTH), F32),
                        pltpu.VMEM((SW_STACK, 3 * SW_BLOCK), F32), pltpu.VMEM((SW_STACK, 3 * SW_BLOCK), F32),
                        pltpu.VMEM((SW_STACK, 3 * SW_BLOCK), BF16), pltpu.VMEM((SW_STACK, 3 * SW_BLOCK), BF16)],
        compiler_params=_params(),
    )(qs, ks, zq, t5b, sink, o_sw, do_sw)


def merge_out(x, o_na, o_sw, gt, wbna_t, wbsw_t, wout, name):
    s, d = x.shape
    tm = _row_tile(s)

    def body(x_ref, ona_ref, osw_ref, gt_ref, wna_ref, wsw_ref, wo_ref, xo_ref, ana_ref, asw_ref, mg_ref):
        a_na = _dotg(ona_ref[...], wna_ref[...], NT)
        a_sw = _dotg(osw_ref[...], wsw_ref[...], NT)
        g_na, g_sw = gt_ref[:, 0:d].astype(F32), gt_ref[:, d:2 * d].astype(F32)
        ana_ref[...] = (a_na * g_na * (1.0 - g_na)).astype(BF16)
        asw_ref[...] = (a_sw * g_sw * (1.0 - g_sw)).astype(BF16)
        merged = (g_na * a_na + g_sw * a_sw).astype(BF16)
        mg_ref[...] = merged
        xo_ref[...] = x_ref[...] + _dot(merged, wo_ref[...])

    return pl.pallas_call(
        body, name=name, grid=(s // tm,),
        in_specs=[_rows(tm, d), _rows(tm, 512), _rows(tm, 512), _rows(tm, 2 * d),
                  _mat(*wbna_t), _mat(*wbsw_t), _mat(*wout)],
        out_specs=[_rows(tm, d)] * 4,
        out_shape=[jax.ShapeDtypeStruct((s, d), F32)] + [jax.ShapeDtypeStruct((s, d), BF16)] * 3,
        compiler_params=_params(),
    )(x, o_na, o_sw, gt, wbna_t[0], wbsw_t[0], wout[0])


def mix_bwd_out(dx, gt, a_na, a_sw, wbna_t, wbsw_t, wout, dep, name):
    s, d = dx.shape
    tm = _row_tile(s)

    def body(dx_ref, gt_ref, ana_ref, asw_ref, wna_ref, wsw_ref, wo_ref, dep_ref,
             dxb_ref, dzg_ref, dana_ref, dasw_ref, dona_ref, dosw_ref, dbg_ref):
        @pl.when(pl.program_id(0) == 0)
        def _():
            dbg_ref[...] = jnp.zeros(dbg_ref.shape, F32)

        dxb = dx_ref[...].astype(BF16)
        dxb_ref[...] = dxb
        dm = _dotg(dxb, wo_ref[...], NT)
        for i, (a_ref, da_ref, w_ref, do_ref) in enumerate(
                [(ana_ref, dana_ref, wna_ref, dona_ref), (asw_ref, dasw_ref, wsw_ref, dosw_ref)]):
            gi = gt_ref[:, i * d:(i + 1) * d].astype(F32)
            da = (dm * gi).astype(BF16)
            da_ref[...] = da
            do_ref[...] = _dot(da, w_ref[...]).astype(BF16)
            dzg = dm * a_ref[...].astype(F32)
            dzg_ref[:, i * d:(i + 1) * d] = dzg.astype(BF16)
            dbg_ref[:, i * d:(i + 1) * d] = dbg_ref[:, i * d:(i + 1) * d] + jnp.sum(dzg, axis=0, keepdims=True)

    return pl.pallas_call(
        body, name=name, grid=(s // tm,),
        in_specs=[_rows(tm, d), _rows(tm, 2 * d), _rows(tm, d), _rows(tm, d),
                  _mat(*wbna_t), _mat(*wbsw_t), _mat(*wout), _full(dep.shape)],
        out_specs=[_rows(tm, d), _rows(tm, 2 * d), _rows(tm, d), _rows(tm, d), _rows(tm, 512), _rows(tm, 512),
                   _full((1, 2 * d))],
        out_shape=[jax.ShapeDtypeStruct((s, d), BF16), jax.ShapeDtypeStruct((s, 2 * d), BF16),
                   jax.ShapeDtypeStruct((s, d), BF16), jax.ShapeDtypeStruct((s, d), BF16),
                   jax.ShapeDtypeStruct((s, 512), BF16), jax.ShapeDtypeStruct((s, 512), BF16),
                   jax.ShapeDtypeStruct((1, 2 * d), F32)],
        compiler_params=_params(),
    )(dx, gt, a_na, a_sw, wbna_t[0], wbsw_t[0], wout[0], dep)


def qk_norm_bwd(dqa, dka, dva, dqs, dks, dvs, zq, dzg, gq_na, gk_na, gq_sw, gk_sw, bd, name):
    s = zq.shape[0]
    d2 = dzg.shape[1]
    n_in = QKV_WIDTH + d2
    tm = _row_tile(s)

    def body(dqa_ref, dka_ref, dva_ref, dqs_ref, dks_ref, dvs_ref, zq_ref, dzg_ref,
             gqa_ref, gka_ref, gqs_ref, gks_ref, bd_ref, dz_ref, dgqa_ref, dgka_ref, dgqs_ref, dgks_ref):
        @pl.when(pl.program_id(0) == 0)
        def _():
            for r in (dgqa_ref, dgka_ref, dgqs_ref, dgks_ref):
                r[...] = jnp.zeros(r.shape, F32)

        bd512 = bd_ref[...]
        bd128 = bd_ref[0:SW_KV_WIDTH, 0:SW_KV_WIDTH]

        def one(c0, c1, dy_ref, g_ref, dg_ref, bdm, scale):
            z = zq_ref[:, c0:c1].astype(F32)
            r = lax.rsqrt(_group_mean(z * z, bdm) + EPS)
            zh = z * r
            dy = dy_ref[...] * scale
            dyg = dy * g_ref[...]
            dz = r * (dyg - zh * _group_mean(dyg * zh, bdm))
            dz_ref[:, c0:c1] = dz.astype(BF16)
            dg_ref[...] = dg_ref[...] + jnp.sum(dy * zh, axis=0, keepdims=True)

        one(0, 512, dqa_ref, gqa_ref, dgqa_ref, bd512, SCALE)
        one(512, 1024, dka_ref, gka_ref, dgka_ref, bd512, 1.0)
        dz_ref[:, 1024:1536] = dva_ref[...].astype(BF16)
        one(1536, 2048, dqs_ref, gqs_ref, dgqs_ref, bd512, SCALE)
        one(2048, 2176, dks_ref, gks_ref, dgks_ref, bd128, 1.0)
        dz_ref[:, 2176:2304] = dvs_ref[...].astype(BF16)
        dz_ref[:, QKV_WIDTH:n_in] = dzg_ref[...]

    return pl.pallas_call(
        body, name=name, grid=(s // tm,),
        in_specs=[_rows(tm, 512), _rows(tm, 512), _rows(tm, 512), _rows(tm, 512), _rows(tm, 128), _rows(tm, 128),
                  _rows(tm, QKV_WIDTH), _rows(tm, d2),
                  _full((1, 512)), _full((1, 512)), _full((1, 512)), _full((1, 128)), _full((512, 512))],
        out_specs=[_rows(tm, n_in), _full((1, 512)), _full((1, 512)), _full((1, 512)), _full((1, 128))],
        out_shape=[jax.ShapeDtypeStruct((s, n_in), BF16)] + [jax.ShapeDtypeStruct((1, 512), F32)] * 3
                  + [jax.ShapeDtypeStruct((1, 128), F32)],
        compiler_params=_params(),
    )(dqa, dka, dva, dqs, dks, dvs, zq, dzg, gq_na, gk_na, gq_sw, gk_sw, bd)


def ffn_bwd_act(dx, wd, hg, hu, name):
    s, d = dx.shape
    f = wd[0].shape[1]
    tm = _row_tile(s)
    fc = _col_chunk(f)

    def body(dx_ref, w_ref, hg_ref, hu_ref, dxb_ref, dhg_ref, dhu_ref):
        dxv = dx_ref[...]
        dxb_ref[...] = dxv.astype(BF16)
        half = (0.5 * dxv).astype(BF16)
        for c0 in range(0, f, fc):
            dact = _dotg(half, w_ref[c0:c0 + fc, :], NT)
            dhu_ref[:, c0:c0 + fc] = (dact * hu_ref[:, c0:c0 + fc].astype(F32)).astype(BF16)
            dhg_ref[:, c0:c0 + fc] = (dact * hg_ref[:, c0:c0 + fc].astype(F32)).astype(BF16)

    return pl.pallas_call(
        body, name=name, grid=(s // tm,),
        in_specs=[_rows(tm, d), _mat(*wd), _rows(tm, f), _rows(tm, f)],
        out_specs=[_rows(tm, d), _rows(tm, f), _rows(tm, f)],
        out_shape=[jax.ShapeDtypeStruct((s, d), BF16), jax.ShapeDtypeStruct((s, f), BF16),
                   jax.ShapeDtypeStruct((s, f), BF16)],
        compiler_params=_params(),
    )(dx, wd[0], hg, hu)


def proj_bwd_norm(acts, weights, x, gain, dx, dep, name):
    s, d = x.shape
    tm = min(_row_tile(s), 256)
    n = len(acts)
    pieces = [(i, c0, kc) for i, w in enumerate(weights) for c0, kc in _stream_pieces(w[0].shape[1])]

    def body(*refs):
        a_refs, w_hbm = refs[:n], refs[n:2 * n]
        x_ref, g_ref, dx_ref, _, o_ref, dg_ref = refs[2 * n:2 * n + 6]
        w_vm, sems = refs[2 * n + 6:3 * n + 6], refs[3 * n + 6]
        first = pl.program_id(0) == 0

        def piece(j):
            i, c0, kc = pieces[j]
            rows = pl.ds(c0, kc)
            return pltpu.make_async_copy(w_hbm[i].at[weights[i][1], rows, :], w_vm[i].at[rows, :], sems.at[j])

        @pl.when(first)
        def _():
            dg_ref[...] = jnp.zeros(dg_ref.shape, F32)
            for j in range(len(pieces)):
                piece(j).start()

        dxn = jnp.zeros((tm, d), F32)
        for j, (i, c0, kc) in enumerate(pieces):
            @pl.when(first)
            def _(j=j):
                piece(j).wait()
            dxn = dxn + _dot(a_refs[i][:, c0:c0 + kc], w_vm[i][c0:c0 + kc, :])
        xv = x_ref[...]
        r = _rstd(xv)
        xh = xv * r
        dxh = dxn * g_ref[...]
        o_ref[...] = dx_ref[...] + r * (dxh - xh * jnp.mean(dxh * xh, axis=-1, keepdims=True))
        dg_ref[...] = dg_ref[...] + jnp.sum(dxn * xh, axis=0, keepdims=True)

    return pl.pallas_call(
        body, name=name, grid=(s // tm,),
        in_specs=[_rows(tm, a.shape[1]) for a in acts] + [pl.BlockSpec(memory_space=pl.ANY)] * n
                 + [_rows(tm, d), _full((1, d)), _rows(tm, d), _full(dep.shape)],
        out_specs=[_rows(tm, d), _full((1, d))],
        out_shape=[jax.ShapeDtypeStruct((s, d), F32), jax.ShapeDtypeStruct((1, d), F32)],
        scratch_shapes=[pltpu.VMEM(w[0].shape[1:], BF16) for w in weights]
                       + [pltpu.SemaphoreType.DMA((len(pieces),))],
        compiler_params=_params(),
    )(*acts, *[w[0] for w in weights], x, gain, dx, dep)


def tn_matmul(products, name):
    s, n = products[0][0].shape
    tn = _tn_tile(n) if len(products) == 1 else _col_chunk(n)
    rhs = []
    for _, b, _ in products:
        if not any(b is seen for seen in rhs):
            rhs.append(b)
    which = [next(i for i, seen in enumerate(rhs) if b is seen) for _, b, _ in products]
    npr, nr = len(products), len(rhs)

    def body(*refs):
        a_refs, b_refs, o_refs = refs[:npr], refs[npr:npr + nr], refs[npr + nr:]
        for i, (_, _, scale) in enumerate(products):
            o_refs[i][...] = (scale * _dotg(a_refs[i][...], b_refs[which[i]][...], TN)).astype(BF16)

    return pl.pallas_call(
        body, name=name, grid=(n // tn,),
        in_specs=[pl.BlockSpec((s, tn), lambda i: (0, i))] * npr
                 + [pl.BlockSpec(b.shape, lambda i: (0, 0), pipeline_mode=ONCE) for b in rhs],
        out_specs=[pl.BlockSpec((tn, b.shape[1]), lambda i: (i, 0)) for _, b, _ in products],
        out_shape=[jax.ShapeDtypeStruct((n, b.shape[1]), BF16) for _, b, _ in products],
        compiler_params=_params(),
    )(*[a for a, _, _ in products], *rhs)


def _mesh_pos():
    return lax.axis_index("x"), lax.axis_index("y"), lax.axis_index("c")


def _peers():
    x, y, c = _mesh_pos()
    peers = []
    for rel in range(1, N_DEV):
        peers.append((1 - x if rel & 4 else x, 1 - y if rel & 2 else y, 1 - c if rel & 1 else c))
    return 4 * x + 2 * y + c, peers


HBM_SPEC = pl.BlockSpec(memory_space=pltpu.HBM)
SEM_SPEC = pl.BlockSpec(memory_space=pltpu.SEMAPHORE)


def _split_call(body, name, thru, n_sems, extra=(), with_token=True):
    hbm = lambda t: pltpu.with_memory_space_constraint(t, pltpu.HBM)
    effect = pltpu.CompilerParams(has_side_effects=pltpu.SideEffectType.DATAFLOW_SIDE_EFFECTING)
    nt = len(thru)
    thru_shapes = [pltpu.HBM(t.shape, t.dtype) for t in thru]
    if with_token:
        (after,) = extra
        outs = pl.pallas_call(
            body, name=name, in_specs=[HBM_SPEC] * nt + [pl.BlockSpec(memory_space=pl.ANY)],
            out_specs=[SEM_SPEC] * len(n_sems) + [HBM_SPEC] * nt + [pl.BlockSpec(memory_space=pltpu.VMEM)],
            out_shape=[pltpu.SemaphoreType.DMA((k,)) for k in n_sems] + thru_shapes
                      + [jax.ShapeDtypeStruct((8, LANES), F32)],
            input_output_aliases={i: len(n_sems) + i for i in range(nt)}, compiler_params=effect,
        )(*[hbm(t) for t in thru], after)
        return outs[:len(n_sems)], outs[len(n_sems):-1], outs[-1]
    return pl.pallas_call(
        body, name=name,
        in_specs=[HBM_SPEC] * nt + [SEM_SPEC] * len(n_sems) + [pl.BlockSpec(memory_space=pl.ANY)],
        out_specs=[HBM_SPEC] * nt, out_shape=thru_shapes,
        input_output_aliases={i: i for i in range(nt)}, compiler_params=effect,
    )(*thru, *extra)


def _gather_targets():
    x, y, c = _mesh_pos()
    return 4 * x + 2 * y + c, [(x, y, 1 - c), (1 - x, y, c), (x, 1 - y, c), (1 - x, 1 - y, c)]


def gather_start(shards, after, name):
    n = len(shards)
    zones = [lax.empty((w.shape[0], N_DEV) + w.shape[1:], w.dtype) for w in shards]

    def body(*refs):
        ins, zs = refs[:n], refs[n:2 * n]
        send_sems, recv_sems, local_sems = refs[2 * n + 1:2 * n + 4]
        token = refs[-1]
        me, targets = _gather_targets()
        for a in range(n):
            pltpu.make_async_copy(ins[a], zs[a].at[:, me], local_sems.at[a]).start()
            for k, to in enumerate(targets):
                pltpu.make_async_remote_copy(
                    src_ref=ins[a], dst_ref=zs[a].at[:, me], send_sem=send_sems.at[4 * a + k],
                    recv_sem=recv_sems.at[4 * a + k], device_id=to, device_id_type=MESH).start()
        token[...] = jnp.zeros(token.shape, F32)

    sems, thru, token = _split_call(body, name, list(shards) + zones, (4 * n, 4 * n, n), extra=(after,))
    return (sems, thru, n), token


def gather_wait(started, after, name):
    sems, thru, n = started

    def body(*refs):
        zs = refs[n:2 * n]
        send_sems, recv_sems, local_sems = refs[2 * n:2 * n + 3]
        _, targets = _gather_targets()
        for a in range(n):
            for k, to in enumerate(targets):
                cp = pltpu.make_async_remote_copy(
                    src_ref=zs[a].at[:, 0], dst_ref=zs[a].at[:, 0], send_sem=send_sems.at[4 * a + k],
                    recv_sem=recv_sems.at[4 * a + k], device_id=to, device_id_type=MESH)
                cp.wait_send()
                cp.wait_recv()
            pltpu.make_async_copy(zs[a].at[:, 0], zs[a].at[:, 0], local_sems.at[a]).wait()

    return _split_call(body, name, thru, (4 * n, 4 * n, n), extra=(*sems, after), with_token=False)[n:]


def forward_start(zones, after, name):
    n = len(zones)

    def body(*refs):
        zs = refs[:n]
        send_sems, recv_sems = refs[n + 1:n + 3]
        token = refs[-1]
        x, y, c = _mesh_pos()
        for a in range(n):
            for j, chip in enumerate([(1 - x, y), (x, 1 - y), (1 - x, 1 - y)]):
                blk = zs[a].at[:, 4 * chip[0] + 2 * chip[1] + c]
                pltpu.make_async_remote_copy(
                    src_ref=blk, dst_ref=blk, send_sem=send_sems.at[3 * a + j], recv_sem=recv_sems.at[3 * a + j],
                    device_id=(x, y, 1 - c), device_id_type=MESH).start()
        token[...] = jnp.zeros(token.shape, F32)

    sems, thru, token = _split_call(body, name, list(zones), (3 * n, 3 * n), extra=(after,))
    return (sems, thru, n), token


def forward_wait(started, after, name):
    sems, thru, n = started

    def body(*refs):
        zs = refs[:n]
        send_sems, recv_sems = refs[n:n + 2]
        x, y, c = _mesh_pos()
        for a in range(n):
            for j in range(3):
                cp = pltpu.make_async_remote_copy(
                    src_ref=zs[a].at[:, 0], dst_ref=zs[a].at[:, 0], send_sem=send_sems.at[3 * a + j],
                    recv_sem=recv_sems.at[3 * a + j], device_id=(x, y, 1 - c), device_id_type=MESH)
                cp.wait_send()
                cp.wait_recv()

    return _split_call(body, name, thru, (3 * n, 3 * n), extra=(*sems, after), with_token=False)


def scatter_start(groups, name):
    n = len(groups)
    flat = [g for grp in groups for g in grp]
    nf = len(flat)
    offs = np.cumsum([0] + [len(grp) for grp in groups])
    lands = [lax.empty((N_DEV, len(grp)) + grp[0].shape[1:], grp[0].dtype) for grp in groups]

    def body(*refs):
        ins, zones = refs[:nf], refs[nf:nf + n]
        send_sems, recv_sems, local_sems = refs[nf + n:nf + n + 3]
        token = refs[-1]
        me, peers = _peers()
        for a in range(n):
            for w in range(len(groups[a])):
                pltpu.make_async_copy(ins[offs[a] + w].at[me], zones[a].at[me, w], local_sems.at[a]).start()
        for k, peer in enumerate(peers):
            p_id = 4 * peer[0] + 2 * peer[1] + peer[2]
            for a in range(n):
                for w in range(len(groups[a])):
                    pltpu.make_async_remote_copy(
                        src_ref=ins[offs[a] + w].at[p_id], dst_ref=zones[a].at[me, w],
                        send_sem=send_sems.at[7 * a + k], recv_sem=recv_sems.at[7 * a + k],
                        device_id=peer, device_id_type=MESH).start()
        token[...] = jnp.zeros(token.shape, F32)

    hbm = lambda t: pltpu.with_memory_space_constraint(t, pltpu.HBM)
    outs = pl.pallas_call(
        body, name=name,
        in_specs=[HBM_SPEC] * (nf + n),
        out_specs=[SEM_SPEC] * 3 + [HBM_SPEC] * (nf + n) + [pl.BlockSpec(memory_space=pltpu.VMEM)],
        out_shape=[pltpu.SemaphoreType.DMA((7 * n,)), pltpu.SemaphoreType.DMA((7 * n,)), pltpu.SemaphoreType.DMA((n,))]
                  + [pltpu.HBM(t.shape, t.dtype) for t in flat + lands]
                  + [jax.ShapeDtypeStruct((8, LANES), F32)],
        input_output_aliases={i: 3 + i for i in range(nf + n)},
        compiler_params=pltpu.CompilerParams(has_side_effects=pltpu.SideEffectType.DATAFLOW_SIDE_EFFECTING),
    )(*[hbm(t) for t in flat], *[hbm(t) for t in lands])
    sems, thru, token = outs[:3], outs[3:3 + nf + n], outs[-1]
    return (sems, thru, [len(grp) for grp in groups]), token


def scatter_wait(started, after, name):
    (send_sems, recv_sems, local_sems), thru, sizes = started
    n = len(sizes)
    nf = len(thru) - n

    def body(*refs):
        zones = refs[nf:nf + n]
        s_sems, r_sems, l_sems = refs[nf + n:nf + n + 3]
        me, peers = _peers()
        for a in range(n):
            for k, peer in enumerate(peers):
                cp = pltpu.make_async_remote_copy(
                    src_ref=zones[a].at[0], dst_ref=zones[a].at[0],
                    send_sem=s_sems.at[7 * a + k], recv_sem=r_sems.at[7 * a + k], device_id=peer,
                    device_id_type=MESH)
                cp.wait_send()
                cp.wait_recv()
            pltpu.make_async_copy(zones[a].at[0], zones[a].at[0], l_sems.at[a]).wait()

    outs = pl.pallas_call(
        body, name=name,
        in_specs=[HBM_SPEC] * (nf + n) + [SEM_SPEC] * 3 + [pl.BlockSpec(memory_space=pl.ANY)],
        out_specs=[HBM_SPEC] * (nf + n),
        out_shape=[pltpu.HBM(t.shape, t.dtype) for t in thru],
        input_output_aliases={i: i for i in range(nf + n)},
        compiler_params=pltpu.CompilerParams(has_side_effects=pltpu.SideEffectType.DATAFLOW_SIDE_EFFECTING),
    )(*thru, send_sems, recv_sems, local_sems, after)
    return outs[nf:]


def pair_start(grads, after, name):
    nw = len(grads)
    land = lax.empty((4, nw) + grads[0].shape[1:], grads[0].dtype)

    def body(*refs):
        ins, zone = refs[:nw], refs[nw]
        send_sems, recv_sems = refs[nw + 2:nw + 4]
        x, y, c = _mesh_pos()
        for j in range(4):
            for w in range(nw):
                pltpu.make_async_remote_copy(
                    src_ref=ins[w].at[2 * j + (1 - c)], dst_ref=zone.at[j, w], send_sem=send_sems.at[0],
                    recv_sem=recv_sems.at[0], device_id=(x, y, 1 - c), device_id_type=MESH).start()
        refs[-1][...] = jnp.zeros(refs[-1].shape, F32)

    sems, thru, token = _split_call(body, name, list(grads) + [land], (1, 1), extra=(after,))
    return (sems, thru, nw), token


def pair_wait(started, after, name):
    sems, thru, nw = started

    def body(*refs):
        zone = refs[nw]
        send_sems, recv_sems = refs[nw + 1:nw + 3]
        x, y, c = _mesh_pos()
        cp = pltpu.make_async_remote_copy(src_ref=zone, dst_ref=zone, send_sem=send_sems.at[0],
                                          recv_sem=recv_sems.at[0], device_id=(x, y, 1 - c), device_id_type=MESH)
        cp.wait_send()
        cp.wait_recv()

    outs = _split_call(body, name, thru, (1, 1), extra=(*sems, after), with_token=False)
    return outs[:nw], outs[nw]


def pair_sum(grads, land, name):
    nw = len(grads)
    _, r, c_dim = grads[0].shape

    def body(*refs):
        g_refs, l_ref, o_ref = refs[:nw], refs[nw], refs[nw + 1]
        core = lax.axis_index("c")
        for w in range(nw):
            o_ref[0, w] = (g_refs[w][0, core].astype(F32) + l_ref[0, w].astype(F32)).astype(BF16)

    return pl.pallas_call(
        body, name=name, grid=(4,),
        in_specs=[pl.BlockSpec((1, 2, r, c_dim), lambda j: (j, 0, 0, 0))] * nw
                 + [pl.BlockSpec((1, nw, r, c_dim), lambda j: (j, 0, 0, 0))],
        out_specs=pl.BlockSpec((1, nw, r, c_dim), lambda j: (j, 0, 0, 0)),
        out_shape=jax.ShapeDtypeStruct((4, nw, r, c_dim), BF16),
        compiler_params=_params(),
    )(*[g.reshape(4, 2, r, c_dim) for g in grads], land)


def _other_chips():
    x, y, c = _mesh_pos()
    chips = []
    for rel in range(1, 4):
        px, py = (1 - x if rel & 2 else x), (1 - y if rel & 1 else y)
        chips.append((px, py, 2 * px + py))
    return 2 * x + y, c, chips


def chip_start(pair_sums, after, name):
    land = lax.empty(pair_sums.shape, pair_sums.dtype)

    def body(*refs):
        h_ref, zone = refs[0], refs[1]
        send_sems, recv_sems, local_sem = refs[3:6]
        mine, c, chips = _other_chips()
        pltpu.make_async_copy(h_ref.at[mine], zone.at[mine], local_sem.at[0]).start()
        for k, (px, py, j) in enumerate(chips):
            pltpu.make_async_remote_copy(
                src_ref=h_ref.at[j], dst_ref=zone.at[mine], send_sem=send_sems.at[k], recv_sem=recv_sems.at[k],
                device_id=(px, py, c), device_id_type=MESH).start()
        refs[-1][...] = jnp.zeros(refs[-1].shape, F32)

    sems, thru, token = _split_call(body, name, [pair_sums, land], (3, 3, 1), extra=(after,))
    return (sems, thru), token


def chip_wait(started, after, name):
    sems, thru = started

    def body(*refs):
        zone = refs[1]
        send_sems, recv_sems, local_sem = refs[2:5]
        _, c, chips = _other_chips()
        for k, (px, py, _) in enumerate(chips):
            cp = pltpu.make_async_remote_copy(
                src_ref=zone.at[0], dst_ref=zone.at[0], send_sem=send_sems.at[k], recv_sem=recv_sems.at[k],
                device_id=(px, py, c), device_id_type=MESH)
            cp.wait_send()
            cp.wait_recv()
        pltpu.make_async_copy(zone.at[0], zone.at[0], local_sem.at[0]).wait()

    return _split_call(body, name, thru, (3, 3, 1), extra=(*sems, after), with_token=False)[1]


def share_small(parts, after):
    n = len(parts)

    def body(*refs):
        ins, outs = refs[:n], refs[n + 1:2 * n + 1]
        send_sems, recv_sems, local_sems = refs[2 * n + 1:]
        me, peers = _peers()
        copies = []
        for i in range(n):
            copies.append(pltpu.make_async_copy(ins[i], outs[i].at[me], local_sems.at[i]))
            copies += [pltpu.make_async_remote_copy(
                src_ref=ins[i], dst_ref=outs[i].at[me], send_sem=send_sems.at[7 * i + k],
                recv_sem=recv_sems.at[7 * i + k], device_id=peer, device_id_type=MESH)
                for k, peer in enumerate(peers)]
        for cp in copies:
            cp.start()
        for cp in copies:
            cp.wait()

    vm = pl.BlockSpec(memory_space=pltpu.VMEM)
    return pl.pallas_call(
        body, name="share_small", in_specs=[vm] * n + [pl.BlockSpec(memory_space=pl.ANY)], out_specs=[vm] * n,
        out_shape=[jax.ShapeDtypeStruct((N_DEV,) + p.shape, p.dtype) for p in parts],
        scratch_shapes=[pltpu.SemaphoreType.DMA((7 * n,)), pltpu.SemaphoreType.DMA((7 * n,)),
                        pltpu.SemaphoreType.DMA((n,))],
    )(*parts, after)


def _adamw_math(w, g, m, v):
    m = ADAM_B1 * m + (1.0 - ADAM_B1) * g
    v = ADAM_B2 * v + (1.0 - ADAM_B2) * (g * g)
    m_hat = m / (1.0 - ADAM_B1 ** ADAM_STEP)
    v_hat = v / (1.0 - ADAM_B2 ** ADAM_STEP)
    delta = -ADAM_LR * (m_hat / (jnp.sqrt(v_hat) + ADAM_EPS) + ADAM_WD * w)
    return delta, m, v


def adamw_layer(zone, w_idx, layer, w, m, v, prev, after, name):
    n_src, _, r, c = zone.shape
    depth = w.shape[0]
    if prev is None:
        prev = tuple(lax.empty((depth, r, c), F32) for _ in range(4))
    tr = r // 2 if r % 16 == 0 else r

    def body(z_ref, w_ref, m_ref, v_ref, *rest):
        g_ref, d_ref, mo_ref, vo_ref = rest[5:]
        g = z_ref[0].astype(F32)
        for src in range(1, n_src):
            g = g + z_ref[src].astype(F32)
        g_ref[...] = g
        d_ref[...], mo_ref[...], vo_ref[...] = _adamw_math(w_ref[...], g, m_ref[...], v_ref[...])

    rows = pl.BlockSpec((None, tr, c), lambda i: (layer, i, 0))
    anywhere = pl.BlockSpec(memory_space=pl.ANY)
    return pl.pallas_call(
        body, name=name, grid=(r // tr,),
        in_specs=[pl.BlockSpec((n_src, None, tr, c), lambda i: (0, w_idx, i, 0)), rows, rows, rows]
                 + [anywhere] * 5,
        out_specs=[rows] * 4,
        out_shape=[jax.ShapeDtypeStruct((depth, r, c), F32)] * 4,
        input_output_aliases={4 + k: k for k in range(4)},
        compiler_params=_params(),
    )(zone, w, m, v, *prev, after)


def adamw_small(ws, recvs, ms, vs, name):
    n = len(ws)

    def body(*refs):
        w_refs, r_refs, m_refs, v_refs = (refs[i * n:(i + 1) * n] for i in range(4))
        g_refs, d_refs, mo_refs, vo_refs = (refs[(4 + i) * n:(5 + i) * n] for i in range(4))
        for i in range(n):
            g = r_refs[i][0]
            for src in range(1, N_DEV):
                g = g + r_refs[i][src]
            g_refs[i][...] = g
            d_refs[i][...], mo_refs[i][...], vo_refs[i][...] = _adamw_math(w_refs[i][...], g, m_refs[i][...],
                                                                            v_refs[i][...])

    vm = pl.BlockSpec(memory_space=pltpu.VMEM)
    outs = pl.pallas_call(
        body, name=name, in_specs=[vm] * (4 * n), out_specs=[vm] * (4 * n),
        out_shape=[jax.ShapeDtypeStruct(w.shape, F32) for w in ws] * 4,
        compiler_params=pltpu.CompilerParams(vmem_limit_bytes=V7X_VMEM_LIMIT),
    )(*ws, *recvs, *ms, *vs)
    return [outs[i * n:(i + 1) * n] for i in range(4)]


SMALL_NAMES = ("ffn1_norm", "mix_norm", "ffn2_norm", "b_gate", "na_q_norm", "na_k_norm", "sw_q_norm", "sw_k_norm",
               "na_rpb", "sw_sink", "t5_rel_table")


def kernel(x, ffn1_norm, ffn1_w_gate, ffn1_w_up, ffn1_w_down, mix_norm, w_in, b_gate, na_q_norm, na_k_norm, na_rpb, sw_q_norm, sw_k_norm, sw_sink, t5_rel_table, w_branch_na, w_branch_sw, w_out, ffn2_norm, ffn2_w_gate, ffn2_w_up, ffn2_w_down, loss_target, m_ffn1_norm, m_ffn1_w_gate, m_ffn1_w_up, m_ffn1_w_down, m_mix_norm, m_w_in, m_b_gate, m_na_q_norm, m_na_k_norm, m_na_rpb, m_sw_q_norm, m_sw_k_norm, m_sw_sink, m_t5_rel_table, m_w_branch_na, m_w_branch_sw, m_w_out, m_ffn2_norm, m_ffn2_w_gate, m_ffn2_w_up, m_ffn2_w_down, v_ffn1_norm, v_ffn1_w_gate, v_ffn1_w_up, v_ffn1_w_down, v_mix_norm, v_w_in, v_b_gate, v_na_q_norm, v_na_k_norm, v_na_rpb, v_sw_q_norm, v_sw_k_norm, v_sw_sink, v_t5_rel_table, v_w_branch_na, v_w_branch_sw, v_w_out, v_ffn2_norm, v_ffn2_w_gate, v_ffn2_w_up, v_ffn2_w_down):
    weights = dict(ffn1_norm=ffn1_norm, ffn1_w_gate=ffn1_w_gate, ffn1_w_up=ffn1_w_up, ffn1_w_down=ffn1_w_down,
                   mix_norm=mix_norm, w_in=w_in, b_gate=b_gate, na_q_norm=na_q_norm, na_k_norm=na_k_norm,
                   na_rpb=na_rpb, sw_q_norm=sw_q_norm, sw_k_norm=sw_k_norm, sw_sink=sw_sink,
                   t5_rel_table=t5_rel_table, w_branch_na=w_branch_na, w_branch_sw=w_branch_sw, w_out=w_out,
                   ffn2_norm=ffn2_norm, ffn2_w_gate=ffn2_w_gate, ffn2_w_up=ffn2_w_up, ffn2_w_down=ffn2_w_down)
    mom_m = dict(ffn1_norm=m_ffn1_norm, ffn1_w_gate=m_ffn1_w_gate, ffn1_w_up=m_ffn1_w_up, ffn1_w_down=m_ffn1_w_down,
                 mix_norm=m_mix_norm, w_in=m_w_in, b_gate=m_b_gate, na_q_norm=m_na_q_norm, na_k_norm=m_na_k_norm,
                 na_rpb=m_na_rpb, sw_q_norm=m_sw_q_norm, sw_k_norm=m_sw_k_norm, sw_sink=m_sw_sink,
                 t5_rel_table=m_t5_rel_table, w_branch_na=m_w_branch_na, w_branch_sw=m_w_branch_sw, w_out=m_w_out,
                 ffn2_norm=m_ffn2_norm, ffn2_w_gate=m_ffn2_w_gate, ffn2_w_up=m_ffn2_w_up, ffn2_w_down=m_ffn2_w_down)
    mom_v = dict(ffn1_norm=v_ffn1_norm, ffn1_w_gate=v_ffn1_w_gate, ffn1_w_up=v_ffn1_w_up, ffn1_w_down=v_ffn1_w_down,
                 mix_norm=v_mix_norm, w_in=v_w_in, b_gate=v_b_gate, na_q_norm=v_na_q_norm, na_k_norm=v_na_k_norm,
                 na_rpb=v_na_rpb, sw_q_norm=v_sw_q_norm, sw_k_norm=v_sw_k_norm, sw_sink=v_sw_sink,
                 t5_rel_table=v_t5_rel_table, w_branch_na=v_w_branch_na, w_branch_sw=v_w_branch_sw, w_out=v_w_out,
                 ffn2_norm=v_ffn2_norm, ffn2_w_gate=v_ffn2_w_gate, ffn2_w_up=v_ffn2_w_up, ffn2_w_down=v_ffn2_w_down)
    order = list(weights)

    depth = ffn1_norm.shape[0]
    s, d = x.shape[1], x.shape[2]
    xs = x[0]
    tr = lambda w: jnp.swapaxes(w, -1, -2)

    merge = lambda t: t.reshape(t.shape[0], N_DEV * t.shape[2], t.shape[3])
    no_dep = jnp.zeros((8, LANES), F32)

    def shards_of(kind, l):
        stack = lambda *ws: jnp.stack(ws).astype(BF16)
        if kind == "ffn1":
            return [stack(tr(ffn1_w_gate[l]), tr(ffn1_w_up[l]), ffn1_w_down[l])]
        if kind == "win":
            return [stack(tr(w_in[l]))]
        return [stack(tr(ffn2_w_gate[l]), tr(ffn2_w_up[l]), ffn2_w_down[l]), stack(w_out[l]),
                stack(tr(w_branch_na[l]), tr(w_branch_sw[l]))]

    def start(kind, l, after):
        return gather_start(shards_of(kind, l), after, f"gather_{kind}_{l}")

    def arrive(started, kind, l, after):
        zones = gather_wait(started, after, f"gather_{kind}_{l}_wait")
        return forward_start(zones, no_dep, f"forward_{kind}_{l}")

    def finish(fwd, kind, l, after):
        return [merge(z) for z in forward_wait(fwd, after, f"forward_{kind}_{l}_wait")]

    bd = jnp.asarray(np.kron(np.eye(NA_WIDTH // HEAD_DIM), np.full((HEAD_DIM, HEAD_DIM), 1.0 / HEAD_DIM)), BF16)
    bmap = jnp.asarray(_t5_bucket_map())
    tile8 = lambda g: jnp.tile(g, NA_WIDTH // HEAD_DIM).reshape(1, NA_WIDTH)
    tile2 = lambda g: jnp.tile(g, SW_KV_WIDTH // HEAD_DIM).reshape(1, SW_KV_WIDTH)

    st_first, tok = start("ffn1", 0, no_dep)
    t5b = t5_expand(t5_rel_table, bmap, tok, "t5_expand").reshape(SW_STACK, 3 * SW_BLOCK)
    t2_tables = [rpb_expand(_rpb_rows(na_rpb[l]), tok, f"rpb_expand_{l}") for l in range(depth)]
    fwd, _ = arrive(st_first, "ffn1", 0, t2_tables[-1])
    st_win, dep = start("win", 0, t5b)
    (first,) = finish(fwd, "ffn1", 0, dep)

    saved = []
    layer_w = {0: dict(wg1=(first, 0), wu1=(first, 1), wd1=(first, 2))}
    cur = xs
    for l in range(depth):
        sv = {}
        lw = layer_w[l]
        sv["x0"] = cur
        cur, sv["xn1"], sv["hg1"], sv["hu1"], sv["act1"] = ffn_both(
            cur, ffn1_norm[l][None], lw["wg1"], lw["wu1"], lw["wd1"], dep, f"ffn1_{l}")
        sv["x1"] = cur
        fwd, _ = arrive(st_win, "win", l, cur)
        st_rest, tok = start("rest", l, cur)
        (zb,) = finish(fwd, "win", l, tok)
        lw["win"] = (zb, 0)
        sv["gains"] = (tile8(na_q_norm[l]), tile8(na_k_norm[l]), tile8(sw_q_norm[l]), tile2(sw_k_norm[l]))
        sv["hn"], sv["zq"], sv["qa"], sv["ka"], sv["qs"], sv["ks"], sv["gt"] = mix_in(
            cur, mix_norm[l][None], lw["win"], b_gate[l][None], *sv["gains"], bd, f"mix_in_{l}")
        sv["t2"] = t2_tables[l]
        sv["o_na"] = na_fwd(sv["qa"], sv["ka"], sv["zq"], sv["t2"], f"na_fwd_{l}")
        dep = no_dep
        if l + 1 < depth:
            st_ffn1, dep = start("ffn1", l + 1, sv["o_na"])
        sv["o_sw"] = sw_fwd(sv["qs"], sv["ks"], sv["zq"], t5b, sw_sink[l], dep, f"sw_fwd_{l}")
        fwd, tok = arrive(st_rest, "rest", l, sv["o_sw"])
        za, zc, zd = finish(fwd, "rest", l, tok)
        lw.update(wg2=(za, 0), wu2=(za, 1), wd2=(za, 2), wout=(zc, 0), wna=(zd, 0), wsw=(zd, 1))
        cur, sv["a_na"], sv["a_sw"], sv["merged"] = merge_out(
            cur, sv["o_na"], sv["o_sw"], sv["gt"], lw["wna"], lw["wsw"], lw["wout"], f"merge_out_{l}")
        sv["x2"] = cur
        dep = no_dep
        if l + 1 < depth:
            st_win, dep = start("win", l + 1, cur)
        sv["xn2"], sv["hg2"], sv["hu2"], sv["act2"] = ffn_up(cur, ffn2_norm[l][None], lw["wg2"], lw["wu2"], dep,
                                                             f"ffn2_up_{l}")
        dep = no_dep
        if l + 1 < depth:
            fwd, dep = arrive(st_ffn1, "ffn1", l + 1, sv["act2"])
        if l + 1 < depth:
            cur = ffn_down(cur, sv["act2"], lw["wd2"], dep, f"ffn2_down_{l}")
            (za,) = finish(fwd, "ffn1", l + 1, cur)
            layer_w[l + 1] = dict(wg1=(za, 0), wu1=(za, 1), wd1=(za, 2))
        else:
            dx, loss_acc = ffn_down(cur, sv["act2"], lw["wd2"], dep, f"ffn2_down_{l}", target=loss_target[0])
        dep = no_dep
        saved.append(sv)

    loss = lax.psum(jnp.sum(loss_acc) * (0.5 / d), ("x", "y", "c"))

    split = lambda t: t.reshape(N_DEV, t.shape[0] // N_DEV, t.shape[1])
    pending = {}
    last_key = "ffn1_0"
    two_level = {last_key}
    small = {k: [None] * depth for k in SMALL_NAMES if k != "t5_rel_table"}
    dbias_sw = []
    for l in reversed(range(depth)):
        sv = saved[l]
        lw = layer_w[l]
        wg1, wu1, wd1, wg2, wu2, wd2 = (lw[k] for k in ("wg1", "wu1", "wd1", "wg2", "wu2", "wd2"))
        win_t, wout_l, wna_t, wsw_t = lw["win"], lw["wout"], lw["wna"], lw["wsw"]
        blocks = ((2, "x2", "xn2", "hg2", "hu2", "act2", wg2, wu2, wd2, "ffn2_norm", 3),
                  (1, "x0", "xn1", "hg1", "hu1", "act1", wg1, wu1, wd1, "ffn1_norm", 0))

        def ffn_backward(dx, blk):
            tag, xk, xnk, hgk, huk, actk, wg, wu, wd, norm_name, slot = blk
            gains = weights[norm_name]
            dxb, dhg, dhu = ffn_bwd_act(dx, wd, sv[hgk], sv[huk], f"ffn{tag}_bwd_act_{l}")
            gwg, gwu, gwd = tn_matmul([(dhg, sv[xnk], 1.0), (dhu, sv[xnk], 1.0), (sv[actk], dxb, 0.5)],
                                      f"ffn{tag}_dw_{l}")
            key = f"ffn{tag}_{l}"
            blocks_of = [split(gwg), split(gwu), split(gwd)]
            if key in two_level:
                paired, token = pair_start(blocks_of, dxb, f"pair_{key}")
            else:
                pending[key], token = scatter_start([blocks_of], f"scatter_{key}")
            dx, dg = proj_bwd_norm([dhg, dhu], [wg, wu], sv[xk], gains[l][None], dx, token, f"ffn{tag}_bwd_x_{l}")
            token = no_dep
            if key in two_level:
                thru, land = pair_wait(paired, dx, f"pair_{key}_wait")
                pending[key], token = chip_start(pair_sum(thru, land, f"pair_sum_{key}"), dg, f"chips_{key}")
            small[norm_name][l] = dg[0]
            return dx, token

        dx, token = ffn_backward(dx, blocks[0])
        dxb, dzg, da_na, da_sw, do_na, do_sw, dbg = mix_bwd_out(
            dx, sv["gt"], sv["a_na"], sv["a_sw"], wna_t, wsw_t, wout_l, token, f"mix_bwd_out_{l}")
        small["b_gate"][l] = dbg[0]
        gwout, gwna, gwsw = tn_matmul([(sv["merged"], dxb, 1.0), (da_na, sv["o_na"], 1.0), (da_sw, sv["o_sw"], 1.0)],
                                      f"mix_dw_{l}")
        dqa, dka, dva, dt2 = na_bwd(sv["qa"], sv["ka"], sv["zq"], sv["t2"], sv["o_na"], do_na, f"na_bwd_{l}")
        dqs, dks, dvs, dbias, dsink = sw_bwd(sv["qs"], sv["ks"], sv["zq"], t5b, sw_sink[l], sv["o_sw"], do_sw,
                                             f"sw_bwd_{l}")
        dbias_sw.append(dbias.reshape(SW_HEADS, SW_BLOCK, 3 * SW_BLOCK))
        small["sw_sink"][l] = jnp.sum(dsink[:, 0].reshape(SW_HEADS, SW_BLOCK), axis=1)
        small["na_rpb"][l] = _rpb_from_rows(rpb_reduce(dt2, f"rpb_reduce_{l}"))
        dz, dgqa, dgka, dgqs, dgks = qk_norm_bwd(dqa, dka, dva, dqs, dks, dvs, sv["zq"], dzg, *sv["gains"], bd,
                                                 f"qk_norm_bwd_{l}")
        fold = lambda g: jnp.sum(g.reshape(-1, HEAD_DIM), axis=0)
        small["na_q_norm"][l], small["na_k_norm"][l] = fold(dgqa), fold(dgka)
        small["sw_q_norm"][l], small["sw_k_norm"][l] = fold(dgqs), fold(dgks)
        (gwin,) = tn_matmul([(dz, sv["hn"], 1.0)], f"dwin_{l}")
        pending[f"mix_{l}"], token = scatter_start([[split(gwout)], [split(gwna), split(gwsw)], [split(gwin)]],
                                                   f"scatter_mix_{l}")
        dx, dg = proj_bwd_norm([dz], [win_t], sv["x1"], mix_norm[l][None], dx, token, f"mix_bwd_x_{l}")
        small["mix_norm"][l] = dg[0]
        dx, tail = ffn_backward(dx, blocks[1])

    dtab = t5_reduce(dbias_sw, bmap, "t5_reduce")
    small_parts = {k: jnp.stack(v) for k, v in small.items()}
    small_parts["t5_rel_table"] = jnp.transpose(dtab[:, :, 0])

    grads, delta, new_m, new_v = {}, {}, {}, {}
    state = {}
    chain = [tail]
    members = {"ffn": lambda t: [(f"ffn{t}_w_gate", 0, 0, True), (f"ffn{t}_w_up", 0, 1, True),
                                 (f"ffn{t}_w_down", 0, 2, False)],
               "mix": lambda t: [("w_out", 0, 0, False), ("w_branch_na", 1, 0, True), ("w_branch_sw", 1, 1, True),
                                 ("w_in", 2, 0, True)]}

    def collect(key):
        if key in two_level:
            zones = [chip_wait(pending[key], chain[0], f"wait_{key}")]
        else:
            zones = scatter_wait(pending[key], chain[0], f"wait_{key}")
        kind, l = key.split("_")
        for k, zi, wi, transposed in members[kind[:3]](kind[3:]):
            view = tr if transposed else (lambda t: t)
            state[k] = adamw_layer(zones[zi], wi, int(l), view(weights[k]), view(mom_m[k]), view(mom_v[k]),
                                   state.get(k), chain[0], f"adamw_{k}_{l}")
            chain[0] = state[k][1]
            if all(f"{kind}_{j}" in done for j in range(depth) if j != int(l)):
                grads[k], delta[k], new_m[k], new_v[k] = (view(t) for t in state[k])
        done.add(key)

    done = set()
    for key in pending:
        if key != last_key:
            collect(key)
    collect(last_key)
    recvs = share_small([small_parts[k] for k in SMALL_NAMES], chain[0])
    results = adamw_small([weights[k] for k in SMALL_NAMES], recvs, [mom_m[k] for k in SMALL_NAMES],
                          [mom_v[k] for k in SMALL_NAMES], "adamw_small")
    for dst, outs in zip((grads, delta, new_m, new_v), results):
        dst.update(dict(zip(SMALL_NAMES, outs)))

    return (loss, dx[None], *[grads[k] for k in order], *[delta[k] for k in order],
            *[new_m[k] for k in order], *[new_v[k] for k in order])
```

```python
import functools
import math

import numpy as np
import jax
import jax.numpy as jnp
from jax import lax
from jax.experimental import pallas as pl
from jax.experimental.pallas import tpu as pltpu

F32 = jnp.float32
BF16 = jnp.bfloat16
MESH = pl.DeviceIdType.MESH

N_DEV = 8
EPS = 1e-6
NEG = -1e30
HEAD_DIM = 64
GRID_W = 64
NA_ROWS = 8
NA_COLS = 16
NA_WIDTH = 512
SW_Q_WIDTH = 512
SW_KV_WIDTH = 128
SW_BLOCK = 128
SW_HEADS = 8
SW_REP = 4
REL_BUCKETS = 32
REL_MAX_DIST = 128
QKV_WIDTH = 3 * NA_WIDTH + SW_Q_WIDTH + 2 * SW_KV_WIDTH
SCALE = 1.0 / math.sqrt(HEAD_DIM)

ADAM_LR = 0.001
ADAM_B1 = 0.9
ADAM_B2 = 0.999
ADAM_EPS = 1e-08
ADAM_WD = 0.01
ADAM_STEP = 10

V7X_VMEM_LIMIT = 56 * 1024 * 1024
LANES = 128
MXU_TILE = 256

NT = (((1,), (1,)), ((), ()))
TN = (((0,), (0,)), ((), ()))


def _params(n_grid=1):
    return pltpu.CompilerParams(dimension_semantics=("arbitrary",) * n_grid,
                                vmem_limit_bytes=V7X_VMEM_LIMIT)


def _row_tile(s):
    for t in (512, 256, 128, 64, 32, 16, 8):
        if s % t == 0:
            return t
    raise ValueError(s)


def _tn_tile(n):
    best = max(t for t in range(LANES, min(n, 2304) + 1, LANES) if n % t == 0) if n % LANES == 0 else n
    return best // 2 if best == n and n >= 1024 else best


ONCE = pl.Buffered(1)


def _col_chunk(n):
    return MXU_TILE if n % MXU_TILE == 0 else n


def _dot(a, b):
    return jnp.dot(a, b, preferred_element_type=F32)


def _dotg(a, b, dn):
    return lax.dot_general(a, b, dn, preferred_element_type=F32)


def _sigmoid(v):
    return 1.0 / (1.0 + jnp.exp(-v))


def _rstd(xv):
    return lax.rsqrt(jnp.mean(xv * xv, axis=-1, keepdims=True) + EPS)


def _full(shape):
    nd = len(shape)
    return pl.BlockSpec(shape, lambda i, _n=nd: (0,) * _n)


def _rows(tm, width):
    return pl.BlockSpec((tm, width), lambda i: (i, 0))


def _mat(stack, idx):
    return pl.BlockSpec((None,) + tuple(stack.shape[1:]), lambda i, _w=idx: (_w, 0, 0), pipeline_mode=ONCE)


def _group_mean(v, bd):
    w = bd.shape[0]
    if v.shape[1] > w:
        return jnp.concatenate([_group_mean(v[:, c0:c0 + w], bd) for c0 in range(0, v.shape[1], w)], axis=1)
    hi = v.astype(BF16)
    lo = (v - hi.astype(F32)).astype(BF16)
    return _dot(hi, bd) + _dot(lo, bd)


def _swiglu_tile(xn, wg_ref, wu_ref, dg_ref, du_ref, act_ref, fc):
    for c0 in range(0, wg_ref.shape[0], fc):
        hg = _dotg(xn, wg_ref[c0:c0 + fc, :], NT)
        hu = _dotg(xn, wu_ref[c0:c0 + fc, :], NT)
        sg = _sigmoid(hg)
        silu = hg * sg
        du_ref[:, c0:c0 + fc] = silu.astype(BF16)
        dg_ref[:, c0:c0 + fc] = (hu * (sg + silu * (1.0 - sg))).astype(BF16)
        act_ref[:, c0:c0 + fc] = (silu * hu).astype(BF16)


def ffn_up(x, gain, wg_t, wu_t, dep, name):
    s, d = x.shape
    f = wg_t[0].shape[1]
    tm = _row_tile(s)
    fc = _col_chunk(f)

    def body(x_ref, g_ref, wg_ref, wu_ref, dep_ref, xn_ref, dg_ref, du_ref, act_ref):
        xv = x_ref[...]
        xn = (xv * _rstd(xv) * g_ref[...]).astype(BF16)
        xn_ref[...] = xn
        _swiglu_tile(xn, wg_ref, wu_ref, dg_ref, du_ref, act_ref, fc)

    return pl.pallas_call(
        body, name=name, grid=(s // tm,),
        in_specs=[_rows(tm, d), _full((1, d)), _mat(*wg_t), _mat(*wu_t), _full(dep.shape)],
        out_specs=[_rows(tm, d), _rows(tm, f), _rows(tm, f), _rows(tm, f)],
        out_shape=[jax.ShapeDtypeStruct((s, d), BF16)] + [jax.ShapeDtypeStruct((s, f), BF16)] * 3,
        compiler_params=_params(),
    )(x, gain, wg_t[0], wu_t[0], dep)


def ffn_both(x, gain, wg_t, wu_t, wd, dep, name):
    s, d = x.shape
    f = wg_t[0].shape[1]
    tm = min(_row_tile(s), 256)
    fc = _col_chunk(f)

    def body(x_ref, g_ref, wg_ref, wu_ref, wd_ref, dep_ref, xo_ref, xn_ref, dg_ref, du_ref, act_ref):
        xv = x_ref[...]
        xn = (xv * _rstd(xv) * g_ref[...]).astype(BF16)
        xn_ref[...] = xn
        _swiglu_tile(xn, wg_ref, wu_ref, dg_ref, du_ref, act_ref, fc)
        xo_ref[...] = xv + 0.5 * _dot(act_ref[...], wd_ref[...])

    return pl.pallas_call(
        body, name=name, grid=(s // tm,),
        in_specs=[_rows(tm, d), _full((1, d)), _mat(*wg_t), _mat(*wu_t), _mat(*wd), _full(dep.shape)],
        out_specs=[_rows(tm, d), _rows(tm, d), _rows(tm, f), _rows(tm, f), _rows(tm, f)],
        out_shape=[jax.ShapeDtypeStruct((s, d), F32), jax.ShapeDtypeStruct((s, d), BF16)]
                  + [jax.ShapeDtypeStruct((s, f), BF16)] * 3,
        compiler_params=_params(),
    )(x, gain, wg_t[0], wu_t[0], wd[0], dep)


def ffn_down(x, act, wd, dep, name, target=None):
    s, d = x.shape
    f = act.shape[1]
    tm = _row_tile(s)

    def body(x_ref, a_ref, w_ref, dep_ref, *rest):
        y = x_ref[...] + 0.5 * _dot(a_ref[...], w_ref[...])
        if target is None:
            rest[0][...] = y
            return
        t_ref, dy_ref, acc_ref = rest

        @pl.when(pl.program_id(0) == 0)
        def _():
            acc_ref[...] = jnp.zeros(acc_ref.shape, F32)

        err = y - t_ref[...]
        dy_ref[...] = err * (1.0 / d)
        part = jnp.sum((err * err).reshape(tm // 8, 8, d), axis=0)
        acc = part[:, 0:LANES]
        for c0 in range(LANES, d, LANES):
            acc = acc + part[:, c0:c0 + LANES]
        acc_ref[...] = acc_ref[...] + acc

    ins = [_rows(tm, d), _rows(tm, f), _mat(*wd), _full(dep.shape)]
    if target is None:
        return pl.pallas_call(
            body, name=name, grid=(s // tm,), in_specs=ins, out_specs=_rows(tm, d),
            out_shape=jax.ShapeDtypeStruct((s, d), F32), compiler_params=_params(),
        )(x, act, wd[0], dep)
    return pl.pallas_call(
        body, name=name, grid=(s // tm,), in_specs=ins + [_rows(tm, d)],
        out_specs=[_rows(tm, d), _full((8, LANES))],
        out_shape=[jax.ShapeDtypeStruct((s, d), F32), jax.ShapeDtypeStruct((8, LANES), F32)],
        compiler_params=_params(),
    )(x, act, wd[0], dep, target)


def mix_in(x, gain, win_t, b_gate, gq_na, gk_na, gq_sw, gk_sw, bd, name):
    s, d = x.shape
    tm = _row_tile(s)
    gc = _col_chunk(2 * d)

    def body(x_ref, g_ref, w_ref, b_ref, gqa_ref, gka_ref, gqs_ref, gks_ref, bd_ref,
             hn_ref, zq_ref, qa_ref, ka_ref, qs_ref, ks_ref, gt_ref):
        xv = x_ref[...]
        hn = (xv * _rstd(xv) * g_ref[...]).astype(BF16)
        hn_ref[...] = hn

        def proj(c0, c1):
            return _dotg(hn, w_ref[c0:c1, :], NT)

        def headnorm(z, g, bdm):
            return z * lax.rsqrt(_group_mean(z * z, bdm) + EPS) * g

        bd512 = bd_ref[...]
        bd128 = bd_ref[0:SW_KV_WIDTH, 0:SW_KV_WIDTH]
        z = proj(0, 512)
        zq_ref[:, 0:512] = z.astype(BF16)
        qa_ref[...] = (headnorm(z, gqa_ref[...], bd512) * SCALE).astype(BF16)
        z = proj(512, 1024)
        zq_ref[:, 512:1024] = z.astype(BF16)
        ka_ref[...] = headnorm(z, gka_ref[...], bd512).astype(BF16)
        z = proj(1024, 1536)
        zq_ref[:, 1024:1536] = z.astype(BF16)
        z = proj(1536, 2048)
        zq_ref[:, 1536:2048] = z.astype(BF16)
        qs_ref[...] = (headnorm(z, gqs_ref[...], bd512) * SCALE).astype(BF16)
        z = proj(2048, 2176)
        zq_ref[:, 2048:2176] = z.astype(BF16)
        ks_ref[...] = headnorm(z, gks_ref[...], bd128).astype(BF16)
        z = proj(2176, 2304)
        zq_ref[:, 2176:2304] = z.astype(BF16)
        for c0 in range(0, 2 * d, gc):
            zg = proj(QKV_WIDTH + c0, QKV_WIDTH + c0 + gc) + b_ref[:, c0:c0 + gc]
            gt_ref[:, c0:c0 + gc] = _sigmoid(zg).astype(BF16)

    return pl.pallas_call(
        body, name=name, grid=(s // tm,),
        in_specs=[_rows(tm, d), _full((1, d)), _mat(*win_t), _full((1, 2 * d)),
                  _full((1, 512)), _full((1, 512)), _full((1, 512)), _full((1, 128)), _full((MXU_TILE, MXU_TILE))],
        out_specs=[_rows(tm, d), _rows(tm, QKV_WIDTH), _rows(tm, 512), _rows(tm, 512), _rows(tm, 512),
                   _rows(tm, 128), _rows(tm, 2 * d)],
        out_shape=[jax.ShapeDtypeStruct((s, d), BF16), jax.ShapeDtypeStruct((s, QKV_WIDTH), BF16),
                   jax.ShapeDtypeStruct((s, 512), BF16), jax.ShapeDtypeStruct((s, 512), BF16),
                   jax.ShapeDtypeStruct((s, 512), BF16), jax.ShapeDtypeStruct((s, 128), BF16),
                   jax.ShapeDtypeStruct((s, 2 * d), BF16)],
        compiler_params=_params(),
    )(x, gain, win_t[0], b_gate, gq_na, gk_na, gq_sw, gk_sw, bd)


def _na_iotas():
    qc = lax.broadcasted_iota(jnp.int32, (GRID_W, LANES), 0)
    ln = lax.broadcasted_iota(jnp.int32, (GRID_W, LANES), 1)
    low = ln < GRID_W
    kc = jnp.where(low, ln, ln - GRID_W)
    diff = kc - qc + (NA_COLS - 1)
    qcs = jnp.clip(qc - NA_COLS // 2, 0, GRID_W - NA_COLS)
    inwin = (kc >= qcs) & (kc < qcs + NA_COLS)
    return diff, low, inwin


NA_RI = 2 * NA_ROWS - 1
NA_CI = 2 * NA_COLS - 1
NA_T2 = NA_RI + 1


def _rpb_rows(rpb):
    h = rpb.shape[0]
    padded = jnp.pad(rpb, ((0, 0), (1, 1), (0, GRID_W - NA_CI)))
    return jnp.concatenate([padded[:, :NA_T2], padded[:, 1:NA_T2 + 1]], axis=2).reshape(h, NA_T2, LANES)


def _rpb_from_rows(rows):
    return rows[:, 1:, :NA_CI] + rows[:, :NA_RI, GRID_W:GRID_W + NA_CI]


def rpb_expand(rows, dep, name):
    n_heads = rows.shape[0]

    def body(r_ref, dep_ref, o_ref):
        for h in range(n_heads):
            for e in range(NA_T2):
                line = jnp.broadcast_to(r_ref[h, e:e + 1, :], (GRID_W, LANES))
                o_ref[h, e] = pltpu.roll(line, LANES - (NA_COLS - 1), 1, stride=1, stride_axis=0)

    return pl.pallas_call(
        body, name=name,
        in_specs=[pl.BlockSpec(memory_space=pltpu.VMEM), pl.BlockSpec(memory_space=pltpu.VMEM)],
        out_specs=pl.BlockSpec(memory_space=pltpu.VMEM),
        out_shape=jax.ShapeDtypeStruct((n_heads, NA_T2, GRID_W, LANES), F32),
        compiler_params=pltpu.CompilerParams(vmem_limit_bytes=V7X_VMEM_LIMIT),
    )(rows, dep)


def rpb_reduce(dt2, name):
    n_heads = dt2.shape[0]
    flip = jnp.asarray(np.eye(GRID_W)[::-1], BF16)

    def body(d_ref, j_ref, o_ref):
        jm = j_ref[...]
        for h in range(n_heads):
            for e in range(NA_T2):
                dv = d_ref[h, e]
                hi = dv.astype(BF16)
                mid = (dv - hi.astype(F32)).astype(BF16)
                lo = (dv - hi.astype(F32) - mid.astype(F32)).astype(BF16)
                rev = _dot(jm, hi) + _dot(jm, mid) + _dot(jm, lo)
                back = pltpu.roll(rev, LANES + (NA_COLS - 1) - (GRID_W - 1), 1, stride=1, stride_axis=0)
                o_ref[h, e:e + 1, :] = jnp.sum(back, axis=0, keepdims=True)

    return pl.pallas_call(
        body, name=name,
        in_specs=[pl.BlockSpec(memory_space=pltpu.VMEM)] * 2,
        out_specs=pl.BlockSpec(memory_space=pltpu.VMEM),
        out_shape=jax.ShapeDtypeStruct((n_heads, NA_T2, LANES), F32),
        compiler_params=pltpu.CompilerParams(vmem_limit_bytes=V7X_VMEM_LIMIT),
    )(dt2, flip)


NA_TQ = 4
NA_TK = NA_TQ + NA_ROWS
NA_KCH = NA_TK // 2


def _na_tile_geometry(t, rows):
    r = t * NA_TQ
    kbase = jnp.clip(r - NA_ROWS // 2, 0, rows - NA_TK)
    starts = [jnp.clip(r + a - NA_ROWS // 2, 0, rows - NA_ROWS) for a in range(NA_TQ)]
    return r, kbase, starts


def _na_tile_mask(kbase, starts, low, inwin):
    half = jnp.where(low, 0, 1)
    cols = []
    for c in range(NA_KCH):
        krow = kbase + 2 * c + half
        cols.append(jnp.concatenate(
            [jnp.where(inwin & (krow >= st) & (krow < st + NA_ROWS), 0.0, NEG) for st in starts], axis=0))
    return jnp.concatenate(cols, axis=1)


def _na_tile_index(r, kbase, a, c):
    return jnp.clip(kbase + 2 * c - (r + a) + NA_ROWS, 0, NA_T2 - 1)


def _na_tile_scores(q, k, t2_ref, hh, r, kbase, madd):
    bias = jnp.concatenate(
        [jnp.concatenate([t2_ref[hh, _na_tile_index(r, kbase, a, c)] for a in range(NA_TQ)], axis=0)
         for c in range(NA_KCH)], axis=1)
    return _dotg(q, k, NT) + bias + madd


def _softmax_rows(sc):
    e = jnp.exp(sc - jnp.max(sc, axis=1, keepdims=True))
    return e * (1.0 / jnp.sum(e, axis=1, keepdims=True))


def na_fwd(qa, ka, zq, t2, name):
    s = qa.shape[0]
    rows = s // GRID_W
    n_pairs = NA_WIDTH // LANES
    v_blk0 = (2 * NA_WIDTH) // LANES

    assert rows % NA_TQ == 0 and rows >= NA_TK
    tq, tk = NA_TQ * GRID_W, NA_TK * GRID_W

    def body(q_ref, k_ref, v_ref, t2_ref, o_ref, s_scr, p_scr):
        _, low, inwin = _na_iotas()

        def tile(t, carry):
            r, kbase, starts = _na_tile_geometry(t, rows)
            madd = _na_tile_mask(kbase, starts, low, inwin)
            qr = pl.ds(pl.multiple_of(r * GRID_W, tq), tq)
            kr = pl.ds(pl.multiple_of(kbase * GRID_W, tq), tk)
            for hh in range(2):
                lanes = slice(HEAD_DIM * hh, HEAD_DIM * (hh + 1))
                s_scr[tq * hh:tq * (hh + 1), :] = _na_tile_scores(q_ref[qr, lanes], k_ref[kr, lanes], t2_ref, hh, r,
                                                                  kbase, madd)
            p_scr[...] = _softmax_rows(s_scr[...]).astype(BF16)
            for hh in range(2):
                lanes = slice(HEAD_DIM * hh, HEAD_DIM * (hh + 1))
                o_ref[qr, lanes] = _dot(p_scr[tq * hh:tq * (hh + 1), :], v_ref[kr, lanes]).astype(BF16)
            return carry

        lax.fori_loop(0, rows // NA_TQ, tile, 0)

    col = lambda off: pl.BlockSpec((s, LANES), lambda p, _o=off: (0, _o + p))
    return pl.pallas_call(
        body, name=name, grid=(n_pairs,),
        in_specs=[col(0), col(0), col(v_blk0),
                  pl.BlockSpec((2, NA_T2, GRID_W, LANES), lambda p: (p, 0, 0, 0))],
        out_specs=col(0),
        out_shape=jax.ShapeDtypeStruct((s, NA_WIDTH), BF16),
        scratch_shapes=[pltpu.VMEM((2 * tq, tk), F32), pltpu.VMEM((2 * tq, tk), BF16)],
        compiler_params=_params(),
    )(qa, ka, zq, t2)


def na_bwd(qa, ka, zq, t2, o_na, do_na, name):
    s = qa.shape[0]
    rows = s // GRID_W
    n_pairs = NA_WIDTH // LANES
    v_blk0 = (2 * NA_WIDTH) // LANES

    tq, tk = NA_TQ * GRID_W, NA_TK * GRID_W

    def body(q_ref, k_ref, v_ref, t2_ref, o_ref, do_ref, dq_ref, dk_ref, dv_ref, dt2_ref):
        _, low, inwin = _na_iotas()
        dk_ref[...] = jnp.zeros(dk_ref.shape, F32)
        dv_ref[...] = jnp.zeros(dv_ref.shape, F32)
        dt2_ref[...] = jnp.zeros(dt2_ref.shape, F32)

        def tile(t, carry):
            r, kbase, starts = _na_tile_geometry(t, rows)
            madd = _na_tile_mask(kbase, starts, low, inwin)
            qr = pl.ds(pl.multiple_of(r * GRID_W, tq), tq)
            kr = pl.ds(pl.multiple_of(kbase * GRID_W, tq), tk)
            for hh in range(2):
                lanes = slice(HEAD_DIM * hh, HEAD_DIM * (hh + 1))
                q, k, v = q_ref[qr, lanes], k_ref[kr, lanes], v_ref[kr, lanes]
                p = _softmax_rows(_na_tile_scores(q, k, t2_ref, hh, r, kbase, madd))
                do = do_ref[qr, lanes]
                delta = jnp.sum(do.astype(F32) * o_ref[qr, lanes].astype(F32), axis=1, keepdims=True)
                ds = p * (_dotg(do, v, NT) - delta)
                shared = {}
                for a in range(NA_TQ):
                    for c in range(NA_KCH):
                        shared.setdefault(2 * c - a, []).append(
                            ds[GRID_W * a:GRID_W * (a + 1), LANES * c:LANES * (c + 1)])
                for offset, parts in shared.items():
                    e = jnp.clip(offset + kbase - r + NA_ROWS, 0, NA_T2 - 1)
                    dt2_ref[hh, e] = dt2_ref[hh, e] + functools.reduce(jnp.add, parts)
                dsb = ds.astype(BF16)
                dq_ref[qr, lanes] = _dot(dsb, k)
                dk_ref[kr, lanes] = dk_ref[kr, lanes] + _dotg(dsb, q, TN)
                dv_ref[kr, lanes] = dv_ref[kr, lanes] + _dotg(p.astype(BF16), do, TN)
            return carry

        lax.fori_loop(0, rows // NA_TQ, tile, 0)

    col = lambda off: pl.BlockSpec((s, LANES), lambda p, _o=off: (0, _o + p))
    t2spec = pl.BlockSpec((2, NA_T2, GRID_W, LANES), lambda p: (p, 0, 0, 0))
    return pl.pallas_call(
        body, name=name, grid=(n_pairs,),
        in_specs=[col(0), col(0), col(v_blk0), t2spec, col(0), col(0)],
        out_specs=[col(0), col(0), col(0), t2spec],
        out_shape=[jax.ShapeDtypeStruct((s, NA_WIDTH), F32)] * 3 + [jax.ShapeDtypeStruct(t2.shape, F32)],
        compiler_params=_params(),
    )(qa, ka, zq, t2, o_na, do_na)


def _t5_bucket_map():
    rel = np.arange(3 * SW_BLOCK)[None, :] - SW_BLOCK - np.arange(SW_BLOCK)[:, None]
    nb = REL_BUCKETS // 2
    max_exact = nb // 2
    n = np.abs(rel)
    large = max_exact + (np.log(np.maximum(n, 1) / max_exact)
                         / np.log(REL_MAX_DIST / max_exact) * (nb - max_exact)).astype(np.int32)
    large = np.minimum(large, nb - 1)
    return ((rel > 0) * nb + np.where(n < max_exact, n, large)).astype(np.int32)


def t5_expand(table, bmap, dep, name):
    def body(tab_ref, bm_ref, dep_ref, o_ref):
        bm = bm_ref[...]
        for h in range(SW_HEADS):
            t = jnp.zeros(bm.shape, F32)
            for b in range(REL_BUCKETS):
                t = jnp.where(bm == b, tab_ref[b, h], t)
            o_ref[h] = t

    return pl.pallas_call(
        body, name=name,
        in_specs=[pl.BlockSpec(memory_space=pltpu.SMEM), pl.BlockSpec(memory_space=pltpu.VMEM),
                  pl.BlockSpec(memory_space=pltpu.VMEM)],
        out_specs=pl.BlockSpec(memory_space=pltpu.VMEM),
        out_shape=jax.ShapeDtypeStruct((SW_HEADS,) + bmap.shape, F32),
        compiler_params=pltpu.CompilerParams(vmem_limit_bytes=V7X_VMEM_LIMIT),
    )(table, bmap, dep)


def t5_reduce(dbias_list, bmap, name):
    n = len(dbias_list)

    def body(*refs):
        d_refs, bm_ref, o_ref = refs[:n], refs[n], refs[n + 1]
        bm = bm_ref[...]
        for h in range(SW_HEADS):
            dv = d_refs[0][h]
            for other in d_refs[1:]:
                dv = dv + other[h]
            rows = [jnp.sum(jnp.where(bm == b, dv, 0.0), axis=0, keepdims=True) for b in range(REL_BUCKETS)]
            r = jnp.concatenate(rows, axis=0)
            o_ref[h] = jnp.broadcast_to(jnp.sum(r, axis=1, keepdims=True), (REL_BUCKETS, LANES))

    return pl.pallas_call(
        body, name=name,
        in_specs=[pl.BlockSpec(memory_space=pltpu.VMEM)] * (n + 1),
        out_specs=pl.BlockSpec(memory_space=pltpu.VMEM),
        out_shape=jax.ShapeDtypeStruct((SW_HEADS, REL_BUCKETS, LANES), F32),
        compiler_params=pltpu.CompilerParams(vmem_limit_bytes=V7X_VMEM_LIMIT),
    )(*dbias_list, bmap)


def _sw_mask_iotas():
    a = lax.broadcasted_iota(jnp.int32, (SW_BLOCK, 3 * SW_BLOCK), 0)
    j = lax.broadcasted_iota(jnp.int32, (SW_BLOCK, 3 * SW_BLOCK), 1)
    inwin = jnp.abs(j - SW_BLOCK - a) <= SW_BLOCK
    return j, inwin


SW_STACK = SW_HEADS * SW_BLOCK


def _sw_softmax(sc, sk):
    m = jnp.maximum(jnp.max(sc, axis=1, keepdims=True), sk)
    e = jnp.exp(sc - m)
    es = jnp.exp(sk - m)
    inv = 1.0 / (jnp.sum(e, axis=1, keepdims=True) + es)
    return e * inv, es * inv


def _sw_prologue(k_ref, v_ref, kp, vp, sink_ref, s):
    pad = s + 2 * SW_BLOCK
    zeros = jnp.zeros((SW_BLOCK, SW_KV_WIDTH), BF16)
    kp[0:SW_BLOCK, :] = zeros
    vp[0:SW_BLOCK, :] = zeros
    kp[SW_BLOCK + s:pad, :] = zeros
    vp[SW_BLOCK + s:pad, :] = zeros
    kp[SW_BLOCK:SW_BLOCK + s, :] = k_ref[...]
    vp[SW_BLOCK:SW_BLOCK + s, :] = v_ref[...]
    return jnp.concatenate([jnp.full((SW_BLOCK, 1), sink_ref[h], F32) for h in range(SW_HEADS)], axis=0)


def sw_fwd(qs, ks, zq, t5b, sink, dep, name):
    s = qs.shape[0]
    nb = s // SW_BLOCK
    v_blk = (3 * NA_WIDTH + SW_Q_WIDTH + SW_KV_WIDTH) // LANES
    pad = s + 2 * SW_BLOCK

    def body(q_ref, k_ref, v_ref, b_ref, sink_ref, dep_ref, o_ref, kp, vp, s_scr, p_scr):
        sink_col = _sw_prologue(k_ref, v_ref, kp, vp, sink_ref, s)
        j, inwin = _sw_mask_iotas()

        def blk(n, carry):
            kpos = n * SW_BLOCK - SW_BLOCK + j
            madd = jnp.where(inwin & (kpos >= 0) & (kpos < s), 0.0, NEG)
            q0 = pl.multiple_of(n * SW_BLOCK, SW_BLOCK)
            qr, kr = pl.ds(q0, SW_BLOCK), pl.ds(q0, 3 * SW_BLOCK)
            for h in range(SW_HEADS):
                g = h // SW_REP
                s_scr[SW_BLOCK * h:SW_BLOCK * (h + 1), :] = _dotg(
                    q_ref[qr, HEAD_DIM * h:HEAD_DIM * (h + 1)], kp[kr, HEAD_DIM * g:HEAD_DIM * (g + 1)], NT) + madd
            p, _ = _sw_softmax(s_scr[...] + b_ref[...], sink_col)
            p_scr[...] = p.astype(BF16)
            for h in range(SW_HEADS):
                g = h // SW_REP
                o_ref[qr, HEAD_DIM * h:HEAD_DIM * (h + 1)] = _dot(
                    p_scr[SW_BLOCK * h:SW_BLOCK * (h + 1), :], vp[kr, HEAD_DIM * g:HEAD_DIM * (g + 1)]).astype(BF16)
            return carry

        lax.fori_loop(0, nb, blk, 0)

    return pl.pallas_call(
        body, name=name, grid=(1,),
        in_specs=[_full((s, SW_Q_WIDTH)), _full((s, SW_KV_WIDTH)),
                  pl.BlockSpec((s, SW_KV_WIDTH), lambda i: (0, v_blk)),
                  _full((SW_STACK, 3 * SW_BLOCK)), pl.BlockSpec(memory_space=pltpu.SMEM),
                  _full(dep.shape)],
        out_specs=_full((s, SW_Q_WIDTH)),
        out_shape=jax.ShapeDtypeStruct((s, SW_Q_WIDTH), BF16),
        scratch_shapes=[pltpu.VMEM((pad, SW_KV_WIDTH), BF16), pltpu.VMEM((pad, SW_KV_WIDTH), BF16),
                        pltpu.VMEM((SW_STACK, 3 * SW_BLOCK), F32), pltpu.VMEM((SW_STACK, 3 * SW_BLOCK), BF16)],
        compiler_params=_params(),
    )(qs, ks, zq, t5b, sink, dep)


def sw_bwd(qs, ks, zq, t5b, sink, o_sw, do_sw, name):
    s = qs.shape[0]
    nb = s // SW_BLOCK
    v_blk = (3 * NA_WIDTH + SW_Q_WIDTH + SW_KV_WIDTH) // LANES
    pad = s + 2 * SW_BLOCK

    def body(q_ref, k_ref, v_ref, b_ref, sink_ref, o_ref, do_ref,
             dq_ref, dk_ref, dv_ref, db_ref, dsk_ref, kp, vp, dkp, dvp, s_scr, dp_scr, ds_scr, p_scr):
        sink_col = _sw_prologue(k_ref, v_ref, kp, vp, sink_ref, s)
        dkp[...] = jnp.zeros(dkp.shape, F32)
        dvp[...] = jnp.zeros(dvp.shape, F32)
        db_ref[...] = jnp.zeros(db_ref.shape, F32)
        dsk_ref[...] = jnp.zeros(dsk_ref.shape, F32)
        j, inwin = _sw_mask_iotas()

        def blk(n, carry):
            kpos = n * SW_BLOCK - SW_BLOCK + j
            madd = jnp.where(inwin & (kpos >= 0) & (kpos < s), 0.0, NEG)
            q0 = pl.multiple_of(n * SW_BLOCK, SW_BLOCK)
            qr, kr = pl.ds(q0, SW_BLOCK), pl.ds(q0, 3 * SW_BLOCK)
            deltas = []
            for h in range(SW_HEADS):
                g = h // SW_REP
                hl, kl = slice(HEAD_DIM * h, HEAD_DIM * (h + 1)), slice(HEAD_DIM * g, HEAD_DIM * (g + 1))
                rows = slice(SW_BLOCK * h, SW_BLOCK * (h + 1))
                do = do_ref[qr, hl]
                s_scr[rows, :] = _dotg(q_ref[qr, hl], kp[kr, kl], NT) + madd
                dp_scr[rows, :] = _dotg(do, vp[kr, kl], NT)
                deltas.append(jnp.sum(do.astype(F32) * o_ref[qr, hl].astype(F32), axis=1, keepdims=True))
            delta = jnp.concatenate(deltas, axis=0)
            p, ps = _sw_softmax(s_scr[...] + b_ref[...], sink_col)
            ds = p * (dp_scr[...] - delta)
            db_ref[...] = db_ref[...] + ds
            dsk_ref[...] = dsk_ref[...] - jnp.broadcast_to(ps * delta, (SW_STACK, LANES))
            ds_scr[...] = ds.astype(BF16)
            p_scr[...] = p.astype(BF16)
            for g in range(SW_HEADS // SW_REP):
                kl = slice(HEAD_DIM * g, HEAD_DIM * (g + 1))
                k = kp[kr, kl]
                dkw = jnp.zeros((3 * SW_BLOCK, HEAD_DIM), F32)
                dvw = jnp.zeros((3 * SW_BLOCK, HEAD_DIM), F32)
                for r in range(SW_REP):
                    h = g * SW_REP + r
                    hl, rows = slice(HEAD_DIM * h, HEAD_DIM * (h + 1)), slice(SW_BLOCK * h, SW_BLOCK * (h + 1))
                    dsb = ds_scr[rows, :]
                    dq_ref[qr, hl] = _dot(dsb, k)
                    dkw = dkw + _dotg(dsb, q_ref[qr, hl], TN)
                    dvw = dvw + _dotg(p_scr[rows, :], do_ref[qr, hl], TN)
                dkp[kr, kl] = dkp[kr, kl] + dkw
                dvp[kr, kl] = dvp[kr, kl] + dvw
            return carry

        lax.fori_loop(0, nb, blk, 0)
        dk_ref[...] = dkp[SW_BLOCK:SW_BLOCK + s, :]
        dv_ref[...] = dvp[SW_BLOCK:SW_BLOCK + s, :]

    bias_spec = _full((SW_STACK, 3 * SW_BLOCK))
    return pl.pallas_call(
        body, name=name, grid=(1,),
        in_specs=[_full((s, SW_Q_WIDTH)), _full((s, SW_KV_WIDTH)),
                  pl.BlockSpec((s, SW_KV_WIDTH), lambda i: (0, v_blk)),
                  bias_spec, pl.BlockSpec(memory_space=pltpu.SMEM),
                  _full((s, SW_Q_WIDTH)), _full((s, SW_Q_WIDTH))],
        out_specs=[_full((s, SW_Q_WIDTH)), _full((s, SW_KV_WIDTH)), _full((s, SW_KV_WIDTH)), bias_spec,
                   _full((SW_STACK, LANES))],
        out_shape=[jax.ShapeDtypeStruct((s, SW_Q_WIDTH), F32), jax.ShapeDtypeStruct((s, SW_KV_WIDTH), F32),
                   jax.ShapeDtypeStruct((s, SW_KV_WIDTH), F32),
                   jax.ShapeDtypeStruct((SW_STACK, 3 * SW_BLOCK), F32),
                   jax.ShapeDtypeStruct((SW_STACK, LANES), F32)],
        scratch_shapes=[pltpu.VMEM((pad, SW_KV_WIDTH), BF16), pltpu.VMEM((pad, SW_KV_WIDTH), BF16),
                        pltpu.VMEM((pad, SW_KV_WIDTH), F32), pltpu.VMEM((pad, SW_KV_WIDTH), F32),
                        pltpu.VMEM((SW_STACK, 3 * SW_BLOCK), F32), pltpu.VMEM((SW_STACK, 3 * SW_BLOCK), F32),
                        pltpu.VMEM((SW_STACK, 3 * SW_BLOCK), BF16), pltpu.VMEM((SW_STACK, 3 * SW_BLOCK), BF16)],
        compiler_params=_params(),
    )(qs, ks, zq, t5b, sink, o_sw, do_sw)


def merge_out(x, o_na, o_sw, gt, wbna_t, wbsw_t, wout, name):
    s, d = x.shape
    tm = _row_tile(s)

    def body(x_ref, ona_ref, osw_ref, gt_ref, wna_ref, wsw_ref, wo_ref, xo_ref, ana_ref, asw_ref, mg_ref):
        a_na = _dotg(ona_ref[...], wna_ref[...], NT)
        a_sw = _dotg(osw_ref[...], wsw_ref[...], NT)
        g_na, g_sw = gt_ref[:, 0:d].astype(F32), gt_ref[:, d:2 * d].astype(F32)
        ana_ref[...] = (a_na * g_na * (1.0 - g_na)).astype(BF16)
        asw_ref[...] = (a_sw * g_sw * (1.0 - g_sw)).astype(BF16)
        merged = (g_na * a_na + g_sw * a_sw).astype(BF16)
        mg_ref[...] = merged
        xo_ref[...] = x_ref[...] + _dot(merged, wo_ref[...])

    return pl.pallas_call(
        body, name=name, grid=(s // tm,),
        in_specs=[_rows(tm, d), _rows(tm, 512), _rows(tm, 512), _rows(tm, 2 * d),
                  _mat(*wbna_t), _mat(*wbsw_t), _mat(*wout)],
        out_specs=[_rows(tm, d)] * 4,
        out_shape=[jax.ShapeDtypeStruct((s, d), F32)] + [jax.ShapeDtypeStruct((s, d), BF16)] * 3,
        compiler_params=_params(),
    )(x, o_na, o_sw, gt, wbna_t[0], wbsw_t[0], wout[0])


def mix_bwd_out(dx, gt, a_na, a_sw, wbna_t, wbsw_t, wout, dep, name):
    s, d = dx.shape
    tm = _row_tile(s)

    def body(dx_ref, gt_ref, ana_ref, asw_ref, wna_ref, wsw_ref, wo_ref, dep_ref,
             dxb_ref, dzg_ref, dana_ref, dasw_ref, dona_ref, dosw_ref, dbg_ref):
        @pl.when(pl.program_id(0) == 0)
        def _():
            dbg_ref[...] = jnp.zeros(dbg_ref.shape, F32)

        dxb = dx_ref[...].astype(BF16)
        dxb_ref[...] = dxb
        dm = _dotg(dxb, wo_ref[...], NT)
        for i, (a_ref, da_ref, w_ref, do_ref) in enumerate(
                [(ana_ref, dana_ref, wna_ref, dona_ref), (asw_ref, dasw_ref, wsw_ref, dosw_ref)]):
            gi = gt_ref[:, i * d:(i + 1) * d].astype(F32)
            da = (dm * gi).astype(BF16)
            da_ref[...] = da
            do_ref[...] = _dot(da, w_ref[...]).astype(BF16)
            dzg = dm * a_ref[...].astype(F32)
            dzg_ref[:, i * d:(i + 1) * d] = dzg.astype(BF16)
            dbg_ref[:, i * d:(i + 1) * d] = dbg_ref[:, i * d:(i + 1) * d] + jnp.sum(dzg, axis=0, keepdims=True)

    return pl.pallas_call(
        body, name=name, grid=(s // tm,),
        in_specs=[_rows(tm, d), _rows(tm, 2 * d), _rows(tm, d), _rows(tm, d),
                  _mat(*wbna_t), _mat(*wbsw_t), _mat(*wout), _full(dep.shape)],
        out_specs=[_rows(tm, d), _rows(tm, 2 * d), _rows(tm, d), _rows(tm, d), _rows(tm, 512), _rows(tm, 512),
                   _full((1, 2 * d))],
        out_shape=[jax.ShapeDtypeStruct((s, d), BF16), jax.ShapeDtypeStruct((s, 2 * d), BF16),
                   jax.ShapeDtypeStruct((s, d), BF16), jax.ShapeDtypeStruct((s, d), BF16),
                   jax.ShapeDtypeStruct((s, 512), BF16), jax.ShapeDtypeStruct((s, 512), BF16),
                   jax.ShapeDtypeStruct((1, 2 * d), F32)],
        compiler_params=_params(),
    )(dx, gt, a_na, a_sw, wbna_t[0], wbsw_t[0], wout[0], dep)


def qk_norm_bwd(dqa, dka, dva, dqs, dks, dvs, zq, dzg, gq_na, gk_na, gq_sw, gk_sw, bd, name):
    s = zq.shape[0]
    d2 = dzg.shape[1]
    n_in = QKV_WIDTH + d2
    tm = _row_tile(s)

    def body(dqa_ref, dka_ref, dva_ref, dqs_ref, dks_ref, dvs_ref, zq_ref, dzg_ref,
             gqa_ref, gka_ref, gqs_ref, gks_ref, bd_ref, dz_ref, dgqa_ref, dgka_ref, dgqs_ref, dgks_ref):
        @pl.when(pl.program_id(0) == 0)
        def _():
            for r in (dgqa_ref, dgka_ref, dgqs_ref, dgks_ref):
                r[...] = jnp.zeros(r.shape, F32)

        bd512 = bd_ref[...]
        bd128 = bd_ref[0:SW_KV_WIDTH, 0:SW_KV_WIDTH]

        def one(c0, c1, dy_ref, g_ref, dg_ref, bdm, scale):
            z = zq_ref[:, c0:c1].astype(F32)
            r = lax.rsqrt(_group_mean(z * z, bdm) + EPS)
            zh = z * r
            dy = dy_ref[...] * scale
            dyg = dy * g_ref[...]
            dz = r * (dyg - zh * _group_mean(dyg * zh, bdm))
            dz_ref[:, c0:c1] = dz.astype(BF16)
            dg_ref[...] = dg_ref[...] + jnp.sum(dy * zh, axis=0, keepdims=True)

        one(0, 512, dqa_ref, gqa_ref, dgqa_ref, bd512, SCALE)
        one(512, 1024, dka_ref, gka_ref, dgka_ref, bd512, 1.0)
        dz_ref[:, 1024:1536] = dva_ref[...].astype(BF16)
        one(1536, 2048, dqs_ref, gqs_ref, dgqs_ref, bd512, SCALE)
        one(2048, 2176, dks_ref, gks_ref, dgks_ref, bd128, 1.0)
        dz_ref[:, 2176:2304] = dvs_ref[...].astype(BF16)
        dz_ref[:, QKV_WIDTH:n_in] = dzg_ref[...]

    return pl.pallas_call(
        body, name=name, grid=(s // tm,),
        in_specs=[_rows(tm, 512), _rows(tm, 512), _rows(tm, 512), _rows(tm, 512), _rows(tm, 128), _rows(tm, 128),
                  _rows(tm, QKV_WIDTH), _rows(tm, d2),
                  _full((1, 512)), _full((1, 512)), _full((1, 512)), _full((1, 128)), _full((MXU_TILE, MXU_TILE))],
        out_specs=[_rows(tm, n_in), _full((1, 512)), _full((1, 512)), _full((1, 512)), _full((1, 128))],
        out_shape=[jax.ShapeDtypeStruct((s, n_in), BF16)] + [jax.ShapeDtypeStruct((1, 512), F32)] * 3
                  + [jax.ShapeDtypeStruct((1, 128), F32)],
        compiler_params=_params(),
    )(dqa, dka, dva, dqs, dks, dvs, zq, dzg, gq_na, gk_na, gq_sw, gk_sw, bd)


def ffn_bwd_act(dx, wd, hg, hu, name):
    s, d = dx.shape
    f = wd[0].shape[1]
    tm = _row_tile(s)
    fc = _col_chunk(f)

    def body(dx_ref, w_ref, hg_ref, hu_ref, dxb_ref, dhg_ref, dhu_ref):
        dxv = dx_ref[...]
        dxb_ref[...] = dxv.astype(BF16)
        half = (0.5 * dxv).astype(BF16)
        for c0 in range(0, f, fc):
            dact = _dotg(half, w_ref[c0:c0 + fc, :], NT)
            dhu_ref[:, c0:c0 + fc] = (dact * hu_ref[:, c0:c0 + fc].astype(F32)).astype(BF16)
            dhg_ref[:, c0:c0 + fc] = (dact * hg_ref[:, c0:c0 + fc].astype(F32)).astype(BF16)

    return pl.pallas_call(
        body, name=name, grid=(s // tm,),
        in_specs=[_rows(tm, d), _mat(*wd), _rows(tm, f), _rows(tm, f)],
        out_specs=[_rows(tm, d), _rows(tm, f), _rows(tm, f)],
        out_shape=[jax.ShapeDtypeStruct((s, d), BF16), jax.ShapeDtypeStruct((s, f), BF16),
                   jax.ShapeDtypeStruct((s, f), BF16)],
        compiler_params=_params(),
    )(dx, wd[0], hg, hu)


def proj_bwd_norm(acts, weights, x, gain, dx, dep, name):
    s, d = x.shape
    tm = min(_row_tile(s), 256)
    n = len(acts)

    def body(*refs):
        a_refs, w_refs = refs[:n], refs[n:2 * n]
        x_ref, g_ref, dx_ref, _, o_ref, dg_ref = refs[2 * n:]

        @pl.when(pl.program_id(0) == 0)
        def _():
            dg_ref[...] = jnp.zeros(dg_ref.shape, F32)

        dxn = _dot(a_refs[0][...], w_refs[0][...])
        for a_ref, w_ref in zip(a_refs[1:], w_refs[1:]):
            dxn = dxn + _dot(a_ref[...], w_ref[...])
        xv = x_ref[...]
        r = _rstd(xv)
        xh = xv * r
        dxh = dxn * g_ref[...]
        o_ref[...] = dx_ref[...] + r * (dxh - xh * jnp.mean(dxh * xh, axis=-1, keepdims=True))
        dg_ref[...] = dg_ref[...] + jnp.sum(dxn * xh, axis=0, keepdims=True)

    return pl.pallas_call(
        body, name=name, grid=(s // tm,),
        in_specs=[_rows(tm, a.shape[1]) for a in acts] + [_mat(*w) for w in weights]
                 + [_rows(tm, d), _full((1, d)), _rows(tm, d), _full(dep.shape)],
        out_specs=[_rows(tm, d), _full((1, d))],
        out_shape=[jax.ShapeDtypeStruct((s, d), F32), jax.ShapeDtypeStruct((1, d), F32)],
        compiler_params=_params(),
    )(*acts, *[w[0] for w in weights], x, gain, dx, dep)


def tn_matmul(products, name):
    s, n = products[0][0].shape
    tn = _tn_tile(n) if len(products) == 1 else _col_chunk(n)
    rhs = []
    for _, b, _ in products:
        if not any(b is seen for seen in rhs):
            rhs.append(b)
    which = [next(i for i, seen in enumerate(rhs) if b is seen) for _, b, _ in products]
    npr, nr = len(products), len(rhs)

    def body(*refs):
        a_refs, b_refs, o_refs = refs[:npr], refs[npr:npr + nr], refs[npr + nr:]
        for i, (_, _, scale) in enumerate(products):
            o_refs[i][...] = (scale * _dotg(a_refs[i][...], b_refs[which[i]][...], TN)).astype(BF16)

    return pl.pallas_call(
        body, name=name, grid=(n // tn,),
        in_specs=[pl.BlockSpec((s, tn), lambda i: (0, i))] * npr
                 + [pl.BlockSpec(b.shape, lambda i: (0, 0), pipeline_mode=ONCE) for b in rhs],
        out_specs=[pl.BlockSpec((tn, b.shape[1]), lambda i: (i, 0)) for _, b, _ in products],
        out_shape=[jax.ShapeDtypeStruct((n, b.shape[1]), BF16) for _, b, _ in products],
        compiler_params=_params(),
    )(*[a for a, _, _ in products], *rhs)


def _mesh_pos():
    return lax.axis_index("x"), lax.axis_index("y"), lax.axis_index("c")


def _peers():
    x, y, c = _mesh_pos()
    peers = []
    for rel in range(1, N_DEV):
        peers.append((1 - x if rel & 4 else x, 1 - y if rel & 2 else y, 1 - c if rel & 1 else c))
    return 4 * x + 2 * y + c, peers


HBM_SPEC = pl.BlockSpec(memory_space=pltpu.HBM)
SEM_SPEC = pl.BlockSpec(memory_space=pltpu.SEMAPHORE)


def _split_call(body, name, thru, n_sems, extra=(), with_token=True):
    hbm = lambda t: pltpu.with_memory_space_constraint(t, pltpu.HBM)
    effect = pltpu.CompilerParams(has_side_effects=pltpu.SideEffectType.DATAFLOW_SIDE_EFFECTING)
    nt = len(thru)
    thru_shapes = [pltpu.HBM(t.shape, t.dtype) for t in thru]
    if with_token:
        (after,) = extra
        outs = pl.pallas_call(
            body, name=name, in_specs=[HBM_SPEC] * nt + [pl.BlockSpec(memory_space=pl.ANY)],
            out_specs=[SEM_SPEC] * len(n_sems) + [HBM_SPEC] * nt + [pl.BlockSpec(memory_space=pltpu.VMEM)],
            out_shape=[pltpu.SemaphoreType.DMA((k,)) for k in n_sems] + thru_shapes
                      + [jax.ShapeDtypeStruct((8, LANES), F32)],
            input_output_aliases={i: len(n_sems) + i for i in range(nt)}, compiler_params=effect,
        )(*[hbm(t) for t in thru], after)
        return outs[:len(n_sems)], outs[len(n_sems):-1], outs[-1]
    return pl.pallas_call(
        body, name=name,
        in_specs=[HBM_SPEC] * nt + [SEM_SPEC] * len(n_sems) + [pl.BlockSpec(memory_space=pl.ANY)],
        out_specs=[HBM_SPEC] * nt, out_shape=thru_shapes,
        input_output_aliases={i: i for i in range(nt)}, compiler_params=effect,
    )(*thru, *extra)


def _gather_targets():
    x, y, c = _mesh_pos()
    return 4 * x + 2 * y + c, [(x, y, 1 - c), (1 - x, y, c), (x, 1 - y, c), (1 - x, 1 - y, c)]


def gather_start(shards, after, name):
    n = len(shards)
    zones = [lax.empty((w.shape[0], N_DEV) + w.shape[1:], w.dtype) for w in shards]

    def body(*refs):
        ins, zs = refs[:n], refs[n:2 * n]
        send_sems, recv_sems, local_sems = refs[2 * n + 1:2 * n + 4]
        token = refs[-1]
        me, targets = _gather_targets()
        for a in range(n):
            pltpu.make_async_copy(ins[a], zs[a].at[:, me], local_sems.at[a]).start()
            for k, to in enumerate(targets):
                pltpu.make_async_remote_copy(
                    src_ref=ins[a], dst_ref=zs[a].at[:, me], send_sem=send_sems.at[4 * a + k],
                    recv_sem=recv_sems.at[4 * a + k], device_id=to, device_id_type=MESH).start()
        token[...] = jnp.zeros(token.shape, F32)

    sems, thru, token = _split_call(body, name, list(shards) + zones, (4 * n, 4 * n, n), extra=(after,))
    return (sems, thru, n), token


def gather_wait(started, after, name):
    sems, thru, n = started

    def body(*refs):
        zs = refs[n:2 * n]
        send_sems, recv_sems, local_sems = refs[2 * n:2 * n + 3]
        _, targets = _gather_targets()
        for a in range(n):
            for k, to in enumerate(targets):
                cp = pltpu.make_async_remote_copy(
                    src_ref=zs[a].at[:, 0], dst_ref=zs[a].at[:, 0], send_sem=send_sems.at[4 * a + k],
                    recv_sem=recv_sems.at[4 * a + k], device_id=to, device_id_type=MESH)
                cp.wait_send()
                cp.wait_recv()
            pltpu.make_async_copy(zs[a].at[:, 0], zs[a].at[:, 0], local_sems.at[a]).wait()

    return _split_call(body, name, thru, (4 * n, 4 * n, n), extra=(*sems, after), with_token=False)[n:]


def forward_start(zones, after, name):
    n = len(zones)

    def body(*refs):
        zs = refs[:n]
        send_sems, recv_sems = refs[n + 1:n + 3]
        token = refs[-1]
        x, y, c = _mesh_pos()
        for a in range(n):
            for j, chip in enumerate([(1 - x, y), (x, 1 - y), (1 - x, 1 - y)]):
                blk = zs[a].at[:, 4 * chip[0] + 2 * chip[1] + c]
                pltpu.make_async_remote_copy(
                    src_ref=blk, dst_ref=blk, send_sem=send_sems.at[3 * a + j], recv_sem=recv_sems.at[3 * a + j],
                    device_id=(x, y, 1 - c), device_id_type=MESH).start()
        token[...] = jnp.zeros(token.shape, F32)

    sems, thru, token = _split_call(body, name, list(zones), (3 * n, 3 * n), extra=(after,))
    return (sems, thru, n), token


def forward_wait(started, after, name):
    sems, thru, n = started

    def body(*refs):
        zs = refs[:n]
        send_sems, recv_sems = refs[n:n + 2]
        x, y, c = _mesh_pos()
        for a in range(n):
            for j in range(3):
                cp = pltpu.make_async_remote_copy(
                    src_ref=zs[a].at[:, 0], dst_ref=zs[a].at[:, 0], send_sem=send_sems.at[3 * a + j],
                    recv_sem=recv_sems.at[3 * a + j], device_id=(x, y, 1 - c), device_id_type=MESH)
                cp.wait_send()
                cp.wait_recv()

    return _split_call(body, name, thru, (3 * n, 3 * n), extra=(*sems, after), with_token=False)


def scatter_start(groups, name):
    n = len(groups)
    flat = [g for grp in groups for g in grp]
    nf = len(flat)
    offs = np.cumsum([0] + [len(grp) for grp in groups])
    lands = [lax.empty((N_DEV, len(grp)) + grp[0].shape[1:], grp[0].dtype) for grp in groups]

    def body(*refs):
        ins, zones = refs[:nf], refs[nf:nf + n]
        send_sems, recv_sems, local_sems = refs[nf + n:nf + n + 3]
        token = refs[-1]
        me, peers = _peers()
        for a in range(n):
            for w in range(len(groups[a])):
                pltpu.make_async_copy(ins[offs[a] + w].at[me], zones[a].at[me, w], local_sems.at[a]).start()
        for k, peer in enumerate(peers):
            p_id = 4 * peer[0] + 2 * peer[1] + peer[2]
            for a in range(n):
                for w in range(len(groups[a])):
                    pltpu.make_async_remote_copy(
                        src_ref=ins[offs[a] + w].at[p_id], dst_ref=zones[a].at[me, w],
                        send_sem=send_sems.at[7 * a + k], recv_sem=recv_sems.at[7 * a + k],
                        device_id=peer, device_id_type=MESH).start()
        token[...] = jnp.zeros(token.shape, F32)

    hbm = lambda t: pltpu.with_memory_space_constraint(t, pltpu.HBM)
    outs = pl.pallas_call(
        body, name=name,
        in_specs=[HBM_SPEC] * (nf + n),
        out_specs=[SEM_SPEC] * 3 + [HBM_SPEC] * (nf + n) + [pl.BlockSpec(memory_space=pltpu.VMEM)],
        out_shape=[pltpu.SemaphoreType.DMA((7 * n,)), pltpu.SemaphoreType.DMA((7 * n,)), pltpu.SemaphoreType.DMA((n,))]
                  + [pltpu.HBM(t.shape, t.dtype) for t in flat + lands]
                  + [jax.ShapeDtypeStruct((8, LANES), F32)],
        input_output_aliases={i: 3 + i for i in range(nf + n)},
        compiler_params=pltpu.CompilerParams(has_side_effects=pltpu.SideEffectType.DATAFLOW_SIDE_EFFECTING),
    )(*[hbm(t) for t in flat], *[hbm(t) for t in lands])
    sems, thru, token = outs[:3], outs[3:3 + nf + n], outs[-1]
    return (sems, thru, [len(grp) for grp in groups]), token


def scatter_wait(started, after, name):
    (send_sems, recv_sems, local_sems), thru, sizes = started
    n = len(sizes)
    nf = len(thru) - n

    def body(*refs):
        zones = refs[nf:nf + n]
        s_sems, r_sems, l_sems = refs[nf + n:nf + n + 3]
        me, peers = _peers()
        for a in range(n):
            for k, peer in enumerate(peers):
                cp = pltpu.make_async_remote_copy(
                    src_ref=zones[a].at[0], dst_ref=zones[a].at[0],
                    send_sem=s_sems.at[7 * a + k], recv_sem=r_sems.at[7 * a + k], device_id=peer,
                    device_id_type=MESH)
                cp.wait_send()
                cp.wait_recv()
            pltpu.make_async_copy(zones[a].at[0], zones[a].at[0], l_sems.at[a]).wait()

    outs = pl.pallas_call(
        body, name=name,
        in_specs=[HBM_SPEC] * (nf + n) + [SEM_SPEC] * 3 + [pl.BlockSpec(memory_space=pl.ANY)],
        out_specs=[HBM_SPEC] * (nf + n),
        out_shape=[pltpu.HBM(t.shape, t.dtype) for t in thru],
        input_output_aliases={i: i for i in range(nf + n)},
        compiler_params=pltpu.CompilerParams(has_side_effects=pltpu.SideEffectType.DATAFLOW_SIDE_EFFECTING),
    )(*thru, send_sems, recv_sems, local_sems, after)
    return outs[nf:]


def pair_start(grads, after, name):
    nw = len(grads)
    land = lax.empty((4, nw) + grads[0].shape[1:], grads[0].dtype)

    def body(*refs):
        ins, zone = refs[:nw], refs[nw]
        send_sems, recv_sems = refs[nw + 2:nw + 4]
        x, y, c = _mesh_pos()
        for j in range(4):
            for w in range(nw):
                pltpu.make_async_remote_copy(
                    src_ref=ins[w].at[2 * j + (1 - c)], dst_ref=zone.at[j, w], send_sem=send_sems.at[0],
                    recv_sem=recv_sems.at[0], device_id=(x, y, 1 - c), device_id_type=MESH).start()
        refs[-1][...] = jnp.zeros(refs[-1].shape, F32)

    sems, thru, token = _split_call(body, name, list(grads) + [land], (1, 1), extra=(after,))
    return (sems, thru, nw), token


def pair_wait(started, after, name):
    sems, thru, nw = started

    def body(*refs):
        zone = refs[nw]
        send_sems, recv_sems = refs[nw + 1:nw + 3]
        x, y, c = _mesh_pos()
        cp = pltpu.make_async_remote_copy(src_ref=zone, dst_ref=zone, send_sem=send_sems.at[0],
                                          recv_sem=recv_sems.at[0], device_id=(x, y, 1 - c), device_id_type=MESH)
        cp.wait_send()
        cp.wait_recv()

    outs = _split_call(body, name, thru, (1, 1), extra=(*sems, after), with_token=False)
    return outs[:nw], outs[nw]


def pair_sum(grads, land, name):
    nw = len(grads)
    _, r, c_dim = grads[0].shape

    def body(*refs):
        g_refs, l_ref, o_ref = refs[:nw], refs[nw], refs[nw + 1]
        core = lax.axis_index("c")
        for w in range(nw):
            o_ref[0, w] = (g_refs[w][0, core].astype(F32) + l_ref[0, w].astype(F32)).astype(BF16)

    return pl.pallas_call(
        body, name=name, grid=(4,),
        in_specs=[pl.BlockSpec((1, 2, r, c_dim), lambda j: (j, 0, 0, 0))] * nw
                 + [pl.BlockSpec((1, nw, r, c_dim), lambda j: (j, 0, 0, 0))],
        out_specs=pl.BlockSpec((1, nw, r, c_dim), lambda j: (j, 0, 0, 0)),
        out_shape=jax.ShapeDtypeStruct((4, nw, r, c_dim), BF16),
        compiler_params=_params(),
    )(*[g.reshape(4, 2, r, c_dim) for g in grads], land)


def _other_chips():
    x, y, c = _mesh_pos()
    chips = []
    for rel in range(1, 4):
        px, py = (1 - x if rel & 2 else x), (1 - y if rel & 1 else y)
        chips.append((px, py, 2 * px + py))
    return 2 * x + y, c, chips


def chip_start(pair_sums, after, name):
    land = lax.empty(pair_sums.shape, pair_sums.dtype)

    def body(*refs):
        h_ref, zone = refs[0], refs[1]
        send_sems, recv_sems, local_sem = refs[3:6]
        mine, c, chips = _other_chips()
        pltpu.make_async_copy(h_ref.at[mine], zone.at[mine], local_sem.at[0]).start()
        for k, (px, py, j) in enumerate(chips):
            pltpu.make_async_remote_copy(
                src_ref=h_ref.at[j], dst_ref=zone.at[mine], send_sem=send_sems.at[k], recv_sem=recv_sems.at[k],
                device_id=(px, py, c), device_id_type=MESH).start()
        refs[-1][...] = jnp.zeros(refs[-1].shape, F32)

    sems, thru, token = _split_call(body, name, [pair_sums, land], (3, 3, 1), extra=(after,))
    return (sems, thru), token


def chip_wait(started, after, name):
    sems, thru = started

    def body(*refs):
        zone = refs[1]
        send_sems, recv_sems, local_sem = refs[2:5]
        _, c, chips = _other_chips()
        for k, (px, py, _) in enumerate(chips):
            cp = pltpu.make_async_remote_copy(
                src_ref=zone.at[0], dst_ref=zone.at[0], send_sem=send_sems.at[k], recv_sem=recv_sems.at[k],
                device_id=(px, py, c), device_id_type=MESH)
            cp.wait_send()
            cp.wait_recv()
        pltpu.make_async_copy(zone.at[0], zone.at[0], local_sem.at[0]).wait()

    return _split_call(body, name, thru, (3, 3, 1), extra=(*sems, after), with_token=False)[1]


def share_small(parts, after):
    n = len(parts)

    def body(*refs):
        ins, outs = refs[:n], refs[n + 1:2 * n + 1]
        send_sems, recv_sems, local_sems = refs[2 * n + 1:]
        me, peers = _peers()
        copies = []
        for i in range(n):
            copies.append(pltpu.make_async_copy(ins[i], outs[i].at[me], local_sems.at[i]))
            copies += [pltpu.make_async_remote_copy(
                src_ref=ins[i], dst_ref=outs[i].at[me], send_sem=send_sems.at[7 * i + k],
                recv_sem=recv_sems.at[7 * i + k], device_id=peer, device_id_type=MESH)
                for k, peer in enumerate(peers)]
        for cp in copies:
            cp.start()
        for cp in copies:
            cp.wait()

    vm = pl.BlockSpec(memory_space=pltpu.VMEM)
    return pl.pallas_call(
        body, name="share_small", in_specs=[vm] * n + [pl.BlockSpec(memory_space=pl.ANY)], out_specs=[vm] * n,
        out_shape=[jax.ShapeDtypeStruct((N_DEV,) + p.shape, p.dtype) for p in parts],
        scratch_shapes=[pltpu.SemaphoreType.DMA((7 * n,)), pltpu.SemaphoreType.DMA((7 * n,)),
                        pltpu.SemaphoreType.DMA((n,))],
    )(*parts, after)


def _adamw_math(w, g, m, v):
    m = ADAM_B1 * m + (1.0 - ADAM_B1) * g
    v = ADAM_B2 * v + (1.0 - ADAM_B2) * (g * g)
    m_hat = m / (1.0 - ADAM_B1 ** ADAM_STEP)
    v_hat = v / (1.0 - ADAM_B2 ** ADAM_STEP)
    delta = -ADAM_LR * (m_hat / (jnp.sqrt(v_hat) + ADAM_EPS) + ADAM_WD * w)
    return delta, m, v


def adamw_layer(zone, w_idx, layer, w, m, v, prev, after, name):
    n_src, _, r, c = zone.shape
    depth = w.shape[0]
    if prev is None:
        prev = tuple(lax.empty((depth, r, c), F32) for _ in range(4))
    tr = r // 2 if r % 16 == 0 else r

    def body(z_ref, w_ref, m_ref, v_ref, *rest):
        g_ref, d_ref, mo_ref, vo_ref = rest[5:]
        g = z_ref[0].astype(F32)
        for src in range(1, n_src):
            g = g + z_ref[src].astype(F32)
        g_ref[...] = g
        d_ref[...], mo_ref[...], vo_ref[...] = _adamw_math(w_ref[...], g, m_ref[...], v_ref[...])

    rows = pl.BlockSpec((None, tr, c), lambda i: (layer, i, 0))
    anywhere = pl.BlockSpec(memory_space=pl.ANY)
    return pl.pallas_call(
        body, name=name, grid=(r // tr,),
        in_specs=[pl.BlockSpec((n_src, None, tr, c), lambda i: (0, w_idx, i, 0)), rows, rows, rows]
                 + [anywhere] * 5,
        out_specs=[rows] * 4,
        out_shape=[jax.ShapeDtypeStruct((depth, r, c), F32)] * 4,
        input_output_aliases={4 + k: k for k in range(4)},
        compiler_params=_params(),
    )(zone, w, m, v, *prev, after)


def adamw_small(ws, recvs, ms, vs, name):
    n = len(ws)

    def body(*refs):
        w_refs, r_refs, m_refs, v_refs = (refs[i * n:(i + 1) * n] for i in range(4))
        g_refs, d_refs, mo_refs, vo_refs = (refs[(4 + i) * n:(5 + i) * n] for i in range(4))
        for i in range(n):
            g = r_refs[i][0]
            for src in range(1, N_DEV):
                g = g + r_refs[i][src]
            g_refs[i][...] = g
            d_refs[i][...], mo_refs[i][...], vo_refs[i][...] = _adamw_math(w_refs[i][...], g, m_refs[i][...],
                                                                            v_refs[i][...])

    vm = pl.BlockSpec(memory_space=pltpu.VMEM)
    outs = pl.pallas_call(
        body, name=name, in_specs=[vm] * (4 * n), out_specs=[vm] * (4 * n),
        out_shape=[jax.ShapeDtypeStruct(w.shape, F32) for w in ws] * 4,
        compiler_params=pltpu.CompilerParams(vmem_limit_bytes=V7X_VMEM_LIMIT),
    )(*ws, *recvs, *ms, *vs)
    return [outs[i * n:(i + 1) * n] for i in range(4)]


SMALL_NAMES = ("ffn1_norm", "mix_norm", "ffn2_norm", "b_gate", "na_q_norm", "na_k_norm", "sw_q_norm", "sw_k_norm",
               "na_rpb", "sw_sink", "t5_rel_table")


def kernel(x, ffn1_norm, ffn1_w_gate, ffn1_w_up, ffn1_w_down, mix_norm, w_in, b_gate, na_q_norm, na_k_norm, na_rpb, sw_q_norm, sw_k_norm, sw_sink, t5_rel_table, w_branch_na, w_branch_sw, w_out, ffn2_norm, ffn2_w_gate, ffn2_w_up, ffn2_w_down, loss_target, m_ffn1_norm, m_ffn1_w_gate, m_ffn1_w_up, m_ffn1_w_down, m_mix_norm, m_w_in, m_b_gate, m_na_q_norm, m_na_k_norm, m_na_rpb, m_sw_q_norm, m_sw_k_norm, m_sw_sink, m_t5_rel_table, m_w_branch_na, m_w_branch_sw, m_w_out, m_ffn2_norm, m_ffn2_w_gate, m_ffn2_w_up, m_ffn2_w_down, v_ffn1_norm, v_ffn1_w_gate, v_ffn1_w_up, v_ffn1_w_down, v_mix_norm, v_w_in, v_b_gate, v_na_q_norm, v_na_k_norm, v_na_rpb, v_sw_q_norm, v_sw_k_norm, v_sw_sink, v_t5_rel_table, v_w_branch_na, v_w_branch_sw, v_w_out, v_ffn2_norm, v_ffn2_w_gate, v_ffn2_w_up, v_ffn2_w_down):
    weights = dict(ffn1_norm=ffn1_norm, ffn1_w_gate=ffn1_w_gate, ffn1_w_up=ffn1_w_up, ffn1_w_down=ffn1_w_down,
                   mix_norm=mix_norm, w_in=w_in, b_gate=b_gate, na_q_norm=na_q_norm, na_k_norm=na_k_norm,
                   na_rpb=na_rpb, sw_q_norm=sw_q_norm, sw_k_norm=sw_k_norm, sw_sink=sw_sink,
                   t5_rel_table=t5_rel_table, w_branch_na=w_branch_na, w_branch_sw=w_branch_sw, w_out=w_out,
                   ffn2_norm=ffn2_norm, ffn2_w_gate=ffn2_w_gate, ffn2_w_up=ffn2_w_up, ffn2_w_down=ffn2_w_down)
    mom_m = dict(ffn1_norm=m_ffn1_norm, ffn1_w_gate=m_ffn1_w_gate, ffn1_w_up=m_ffn1_w_up, ffn1_w_down=m_ffn1_w_down,
                 mix_norm=m_mix_norm, w_in=m_w_in, b_gate=m_b_gate, na_q_norm=m_na_q_norm, na_k_norm=m_na_k_norm,
                 na_rpb=m_na_rpb, sw_q_norm=m_sw_q_norm, sw_k_norm=m_sw_k_norm, sw_sink=m_sw_sink,
                 t5_rel_table=m_t5_rel_table, w_branch_na=m_w_branch_na, w_branch_sw=m_w_branch_sw, w_out=m_w_out,
                 ffn2_norm=m_ffn2_norm, ffn2_w_gate=m_ffn2_w_gate, ffn2_w_up=m_ffn2_w_up, ffn2_w_down=m_ffn2_w_down)
    mom_v = dict(ffn1_norm=v_ffn1_norm, ffn1_w_gate=v_ffn1_w_gate, ffn1_w_up=v_ffn1_w_up, ffn1_w_down=v_ffn1_w_down,
                 mix_norm=v_mix_norm, w_in=v_w_in, b_gate=v_b_gate, na_q_norm=v_na_q_norm, na_k_norm=v_na_k_norm,
                 na_rpb=v_na_rpb, sw_q_norm=v_sw_q_norm, sw_k_norm=v_sw_k_norm, sw_sink=v_sw_sink,
                 t5_rel_table=v_t5_rel_table, w_branch_na=v_w_branch_na, w_branch_sw=v_w_branch_sw, w_out=v_w_out,
                 ffn2_norm=v_ffn2_norm, ffn2_w_gate=v_ffn2_w_gate, ffn2_w_up=v_ffn2_w_up, ffn2_w_down=v_ffn2_w_down)
    order = list(weights)

    depth = ffn1_norm.shape[0]
    s, d = x.shape[1], x.shape[2]
    xs = x[0]
    tr = lambda w: jnp.swapaxes(w, -1, -2)

    merge = lambda t: t.reshape(t.shape[0], N_DEV * t.shape[2], t.shape[3])
    no_dep = jnp.zeros((8, LANES), F32)

    def shards_of(kind, l):
        stack = lambda *ws: jnp.stack(ws).astype(BF16)
        if kind == "ffn1":
            return [stack(tr(ffn1_w_gate[l]), tr(ffn1_w_up[l]), ffn1_w_down[l])]
        if kind == "win":
            return [stack(tr(w_in[l]))]
        return [stack(tr(ffn2_w_gate[l]), tr(ffn2_w_up[l]), ffn2_w_down[l]), stack(w_out[l]),
                stack(tr(w_branch_na[l]), tr(w_branch_sw[l]))]

    def start(kind, l, after):
        return gather_start(shards_of(kind, l), after, f"gather_{kind}_{l}")

    def arrive(started, kind, l, after):
        zones = gather_wait(started, after, f"gather_{kind}_{l}_wait")
        return forward_start(zones, no_dep, f"forward_{kind}_{l}")

    def finish(fwd, kind, l, after):
        return [merge(z) for z in forward_wait(fwd, after, f"forward_{kind}_{l}_wait")]

    bd = jnp.asarray(np.kron(np.eye(MXU_TILE // HEAD_DIM), np.full((HEAD_DIM, HEAD_DIM), 1.0 / HEAD_DIM)), BF16)
    bmap = jnp.asarray(_t5_bucket_map())
    tile8 = lambda g: jnp.tile(g, NA_WIDTH // HEAD_DIM).reshape(1, NA_WIDTH)
    tile2 = lambda g: jnp.tile(g, SW_KV_WIDTH // HEAD_DIM).reshape(1, SW_KV_WIDTH)

    st_first, tok = start("ffn1", 0, no_dep)
    t5b = t5_expand(t5_rel_table, bmap, tok, "t5_expand").reshape(SW_STACK, 3 * SW_BLOCK)
    t2_tables = [rpb_expand(_rpb_rows(na_rpb[l]), tok, f"rpb_expand_{l}") for l in range(depth)]
    fwd, _ = arrive(st_first, "ffn1", 0, t2_tables[-1])
    st_win, dep = start("win", 0, t5b)
    (first,) = finish(fwd, "ffn1", 0, dep)

    saved = []
    layer_w = {0: dict(wg1=(first, 0), wu1=(first, 1), wd1=(first, 2))}
    cur = xs
    for l in range(depth):
        sv = {}
        lw = layer_w[l]
        sv["x0"] = cur
        cur, sv["xn1"], sv["hg1"], sv["hu1"], sv["act1"] = ffn_both(
            cur, ffn1_norm[l][None], lw["wg1"], lw["wu1"], lw["wd1"], dep, f"ffn1_{l}")
        sv["x1"] = cur
        fwd, _ = arrive(st_win, "win", l, cur)
        st_rest, tok = start("rest", l, cur)
        (zb,) = finish(fwd, "win", l, tok)
        lw["win"] = (zb, 0)
        sv["gains"] = (tile8(na_q_norm[l]), tile8(na_k_norm[l]), tile8(sw_q_norm[l]), tile2(sw_k_norm[l]))
        sv["hn"], sv["zq"], sv["qa"], sv["ka"], sv["qs"], sv["ks"], sv["gt"] = mix_in(
            cur, mix_norm[l][None], lw["win"], b_gate[l][None], *sv["gains"], bd, f"mix_in_{l}")
        sv["t2"] = t2_tables[l]
        sv["o_na"] = na_fwd(sv["qa"], sv["ka"], sv["zq"], sv["t2"], f"na_fwd_{l}")
        dep = no_dep
        if l + 1 < depth:
            st_ffn1, dep = start("ffn1", l + 1, sv["o_na"])
        sv["o_sw"] = sw_fwd(sv["qs"], sv["ks"], sv["zq"], t5b, sw_sink[l], dep, f"sw_fwd_{l}")
        fwd, tok = arrive(st_rest, "rest", l, sv["o_sw"])
        za, zc, zd = finish(fwd, "rest", l, tok)
        lw.update(wg2=(za, 0), wu2=(za, 1), wd2=(za, 2), wout=(zc, 0), wna=(zd, 0), wsw=(zd, 1))
        cur, sv["a_na"], sv["a_sw"], sv["merged"] = merge_out(
            cur, sv["o_na"], sv["o_sw"], sv["gt"], lw["wna"], lw["wsw"], lw["wout"], f"merge_out_{l}")
        sv["x2"] = cur
        dep = no_dep
        if l + 1 < depth:
            st_win, dep = start("win", l + 1, cur)
        sv["xn2"], sv["hg2"], sv["hu2"], sv["act2"] = ffn_up(cur, ffn2_norm[l][None], lw["wg2"], lw["wu2"], dep,
                                                             f"ffn2_up_{l}")
        dep = no_dep
        if l + 1 < depth:
            fwd, dep = arrive(st_ffn1, "ffn1", l + 1, sv["act2"])
        if l + 1 < depth:
            cur = ffn_down(cur, sv["act2"], lw["wd2"], dep, f"ffn2_down_{l}")
            (za,) = finish(fwd, "ffn1", l + 1, cur)
            layer_w[l + 1] = dict(wg1=(za, 0), wu1=(za, 1), wd1=(za, 2))
        else:
            dx, loss_acc = ffn_down(cur, sv["act2"], lw["wd2"], dep, f"ffn2_down_{l}", target=loss_target[0])
        dep = no_dep
        saved.append(sv)

    loss = lax.psum(jnp.sum(loss_acc) * (0.5 / d), ("x", "y", "c"))

    split = lambda t: t.reshape(N_DEV, t.shape[0] // N_DEV, t.shape[1])
    pending = {}
    last_key = "ffn1_0"
    two_level = {last_key}
    small = {k: [None] * depth for k in SMALL_NAMES if k != "t5_rel_table"}
    dbias_sw = []
    for l in reversed(range(depth)):
        sv = saved[l]
        lw = layer_w[l]
        wg1, wu1, wd1, wg2, wu2, wd2 = (lw[k] for k in ("wg1", "wu1", "wd1", "wg2", "wu2", "wd2"))
        win_t, wout_l, wna_t, wsw_t = lw["win"], lw["wout"], lw["wna"], lw["wsw"]
        blocks = ((2, "x2", "xn2", "hg2", "hu2", "act2", wg2, wu2, wd2, "ffn2_norm", 3),
                  (1, "x0", "xn1", "hg1", "hu1", "act1", wg1, wu1, wd1, "ffn1_norm", 0))

        def ffn_backward(dx, blk):
            tag, xk, xnk, hgk, huk, actk, wg, wu, wd, norm_name, slot = blk
            gains = weights[norm_name]
            dxb, dhg, dhu = ffn_bwd_act(dx, wd, sv[hgk], sv[huk], f"ffn{tag}_bwd_act_{l}")
            gwg, gwu, gwd = tn_matmul([(dhg, sv[xnk], 1.0), (dhu, sv[xnk], 1.0), (sv[actk], dxb, 0.5)],
                                      f"ffn{tag}_dw_{l}")
            key = f"ffn{tag}_{l}"
            blocks_of = [split(gwg), split(gwu), split(gwd)]
            if key in two_level:
                paired, token = pair_start(blocks_of, dxb, f"pair_{key}")
            else:
                pending[key], token = scatter_start([blocks_of], f"scatter_{key}")
            dx, dg = proj_bwd_norm([dhg, dhu], [wg, wu], sv[xk], gains[l][None], dx, token, f"ffn{tag}_bwd_x_{l}")
            token = no_dep
            if key in two_level:
                thru, land = pair_wait(paired, dx, f"pair_{key}_wait")
                pending[key], token = chip_start(pair_sum(thru, land, f"pair_sum_{key}"), dg, f"chips_{key}")
            small[norm_name][l] = dg[0]
            return dx, token

        dx, token = ffn_backward(dx, blocks[0])
        dxb, dzg, da_na, da_sw, do_na, do_sw, dbg = mix_bwd_out(
            dx, sv["gt"], sv["a_na"], sv["a_sw"], wna_t, wsw_t, wout_l, token, f"mix_bwd_out_{l}")
        small["b_gate"][l] = dbg[0]
        gwout, gwna, gwsw = tn_matmul([(sv["merged"], dxb, 1.0), (da_na, sv["o_na"], 1.0), (da_sw, sv["o_sw"], 1.0)],
                                      f"mix_dw_{l}")
        dqa, dka, dva, dt2 = na_bwd(sv["qa"], sv["ka"], sv["zq"], sv["t2"], sv["o_na"], do_na, f"na_bwd_{l}")
        dqs, dks, dvs, dbias, dsink = sw_bwd(sv["qs"], sv["ks"], sv["zq"], t5b, sw_sink[l], sv["o_sw"], do_sw,
                                             f"sw_bwd_{l}")
        dbias_sw.append(dbias.reshape(SW_HEADS, SW_BLOCK, 3 * SW_BLOCK))
        small["sw_sink"][l] = jnp.sum(dsink[:, 0].reshape(SW_HEADS, SW_BLOCK), axis=1)
        small["na_rpb"][l] = _rpb_from_rows(rpb_reduce(dt2, f"rpb_reduce_{l}"))
        dz, dgqa, dgka, dgqs, dgks = qk_norm_bwd(dqa, dka, dva, dqs, dks, dvs, sv["zq"], dzg, *sv["gains"], bd,
                                                 f"qk_norm_bwd_{l}")
        fold = lambda g: jnp.sum(g.reshape(-1, HEAD_DIM), axis=0)
        small["na_q_norm"][l], small["na_k_norm"][l] = fold(dgqa), fold(dgka)
        small["sw_q_norm"][l], small["sw_k_norm"][l] = fold(dgqs), fold(dgks)
        (gwin,) = tn_matmul([(dz, sv["hn"], 1.0)], f"dwin_{l}")
        pending[f"mix_{l}"], token = scatter_start([[split(gwout)], [split(gwna), split(gwsw)], [split(gwin)]],
                                                   f"scatter_mix_{l}")
        dx, dg = proj_bwd_norm([dz], [win_t], sv["x1"], mix_norm[l][None], dx, token, f"mix_bwd_x_{l}")
        small["mix_norm"][l] = dg[0]
        dx, tail = ffn_backward(dx, blocks[1])

    dtab = t5_reduce(dbias_sw, bmap, "t5_reduce")
    small_parts = {k: jnp.stack(v) for k, v in small.items()}
    small_parts["t5_rel_table"] = jnp.transpose(dtab[:, :, 0])

    grads, delta, new_m, new_v = {}, {}, {}, {}
    state = {}
    chain = [tail]
    members = {"ffn": lambda t: [(f"ffn{t}_w_gate", 0, 0, True), (f"ffn{t}_w_up", 0, 1, True),
                                 (f"ffn{t}_w_down", 0, 2, False)],
               "mix": lambda t: [("w_out", 0, 0, False), ("w_branch_na", 1, 0, True), ("w_branch_sw", 1, 1, True),
                                 ("w_in", 2, 0, True)]}

    def collect(key):
        if key in two_level:
            zones = [chip_wait(pending[key], chain[0], f"wait_{key}")]
        else:
            zones = scatter_wait(pending[key], chain[0], f"wait_{key}")
        kind, l = key.split("_")
        for k, zi, wi, transposed in members[kind[:3]](kind[3:]):
            view = tr if transposed else (lambda t: t)
            state[k] = adamw_layer(zones[zi], wi, int(l), view(weights[k]), view(mom_m[k]), view(mom_v[k]),
                                   state.get(k), chain[0], f"adamw_{k}_{l}")
            chain[0] = state[k][1]
            if all(f"{kind}_{j}" in done for j in range(depth) if j != int(l)):
                grads[k], delta[k], new_m[k], new_v[k] = (view(t) for t in state[k])
        done.add(key)

    done = set()
    for key in pending:
        if key != last_key:
            collect(key)
    collect(last_key)
    recvs = share_small([small_parts[k] for k in SMALL_NAMES], chain[0])
    results = adamw_small([weights[k] for k in SMALL_NAMES], recvs, [mom_m[k] for k in SMALL_NAMES],
                          [mom_v[k] for k in SMALL_NAMES], "adamw_small")
    for dst, outs in zip((grads, delta, new_m, new_v), results):
        dst.update(dict(zip(SMALL_NAMES, outs)))

    return (loss, dx[None], *[grads[k] for k in order], *[delta[k] for k in order],
            *[new_m[k] for k in order], *[new_v[k] for k in order])
```

```python
import functools
import math

import numpy as np
import jax
import jax.numpy as jnp
from jax import lax
from jax.experimental import pallas as pl
from jax.experimental.pallas import tpu as pltpu

F32 = jnp.float32
BF16 = jnp.bfloat16
MESH = pl.DeviceIdType.MESH

N_DEV = 8
EPS = 1e-6
NEG = -1e30
HEAD_DIM = 64
GRID_W = 64
NA_ROWS = 8
NA_COLS = 16
NA_WIDTH = 512
SW_Q_WIDTH = 512
SW_KV_WIDTH = 128
SW_BLOCK = 128
SW_HEADS = 8
SW_REP = 4
REL_BUCKETS = 32
REL_MAX_DIST = 128
QKV_WIDTH = 3 * NA_WIDTH + SW_Q_WIDTH + 2 * SW_KV_WIDTH
SCALE = 1.0 / math.sqrt(HEAD_DIM)

ADAM_LR = 0.001
ADAM_B1 = 0.9
ADAM_B2 = 0.999
ADAM_EPS = 1e-08
ADAM_WD = 0.01
ADAM_STEP = 10

V7X_VMEM_LIMIT = 56 * 1024 * 1024
LANES = 128
MXU_TILE = 256

NT = (((1,), (1,)), ((), ()))
TN = (((0,), (0,)), ((), ()))


def _params(n_grid=1):
    return pltpu.CompilerParams(dimension_semantics=("arbitrary",) * n_grid,
                                vmem_limit_bytes=V7X_VMEM_LIMIT)


def _row_tile(s):
    for t in (512, 256, 128, 64, 32, 16, 8):
        if s % t == 0:
            return t
    raise ValueError(s)


def _tn_tile(n):
    best = max(t for t in range(LANES, min(n, 2304) + 1, LANES) if n % t == 0) if n % LANES == 0 else n
    return best // 2 if best == n and n >= 1024 else best


ONCE = pl.Buffered(1)


def _col_chunk(n):
    return MXU_TILE if n % MXU_TILE == 0 else n


def _dot(a, b):
    return jnp.dot(a, b, preferred_element_type=F32)


def _dotg(a, b, dn):
    return lax.dot_general(a, b, dn, preferred_element_type=F32)


def _sigmoid(v):
    return 1.0 / (1.0 + jnp.exp(-v))


def _rstd(xv):
    return lax.rsqrt(jnp.mean(xv * xv, axis=-1, keepdims=True) + EPS)


def _full(shape):
    nd = len(shape)
    return pl.BlockSpec(shape, lambda i, _n=nd: (0,) * _n)


def _rows(tm, width):
    return pl.BlockSpec((tm, width), lambda i: (i, 0))


def _mat(stack, idx):
    return pl.BlockSpec((None,) + tuple(stack.shape[1:]), lambda i, _w=idx: (_w, 0, 0), pipeline_mode=ONCE)


def _group_mean(v, bd):
    w = bd.shape[0]
    if v.shape[1] > w:
        return jnp.concatenate([_group_mean(v[:, c0:c0 + w], bd) for c0 in range(0, v.shape[1], w)], axis=1)
    hi = v.astype(BF16)
    lo = (v - hi.astype(F32)).astype(BF16)
    return _dot(hi, bd) + _dot(lo, bd)


def _row_streams(hbm_refs, picks, vm_refs, sems, bounds):
    first = pl.program_id(0) == 0
    nch = len(bounds) - 1

    def piece(k, j):
        rows = pl.ds(bounds[j], bounds[j + 1] - bounds[j])
        return pltpu.make_async_copy(hbm_refs[k].at[picks[k], rows, :], vm_refs[k].at[rows, :], sems.at[k * nch + j])

    def start_all():
        @pl.when(first)
        def _():
            for j in range(nch):
                for k in range(len(hbm_refs)):
                    piece(k, j).start()

    def ready(j):
        @pl.when(first)
        def _():
            for k in range(len(hbm_refs)):
                piece(k, j).wait()

    return start_all, ready


def ffn_forward(x, gain, wg_t, wu_t, wd, dep, name):
    s, d = x.shape
    f = wg_t[0].shape[1]
    tm = _row_tile(s) if wd is None else min(_row_tile(s), 256)
    fc = _col_chunk(f)
    bounds = list(range(0, f + 1, fc))
    nw = 2 if wd is None else 3

    def body(x_ref, g_ref, *refs):
        w_hbm, outs, w_vm, sems = refs[:nw], refs[nw + 1:nw + 1 + nw + 2], refs[2 * nw + 3:3 * nw + 3], refs[-1]
        xn_ref, dg_ref, du_ref, act_ref = outs[-4:]
        start_all, ready = _row_streams(w_hbm[:2], (wg_t[1], wu_t[1]), w_vm[:2], sems, bounds)
        start_all()
        if wd is not None:
            down = pltpu.make_async_copy(w_hbm[2].at[wd[1]], w_vm[2], sems.at[2 * (len(bounds) - 1)])

            @pl.when(pl.program_id(0) == 0)
            def _():
                down.start()

        xv = x_ref[...]
        xn = (xv * _rstd(xv) * g_ref[...]).astype(BF16)
        xn_ref[...] = xn
        for j, c0 in enumerate(bounds[:-1]):
            ready(j)
            hg = _dotg(xn, w_vm[0][c0:c0 + fc, :], NT)
            hu = _dotg(xn, w_vm[1][c0:c0 + fc, :], NT)
            sg = _sigmoid(hg)
            silu = hg * sg
            du_ref[:, c0:c0 + fc] = silu.astype(BF16)
            dg_ref[:, c0:c0 + fc] = (hu * (sg + silu * (1.0 - sg))).astype(BF16)
            act_ref[:, c0:c0 + fc] = (silu * hu).astype(BF16)
        if wd is not None:
            @pl.when(pl.program_id(0) == 0)
            def _():
                down.wait()

            outs[0][...] = xv + 0.5 * _dot(act_ref[...], w_vm[2][...])

    weights = [wg_t, wu_t] + ([] if wd is None else [wd])
    out_specs = [_rows(tm, d), _rows(tm, f), _rows(tm, f), _rows(tm, f)]
    out_shape = [jax.ShapeDtypeStruct((s, d), BF16)] + [jax.ShapeDtypeStruct((s, f), BF16)] * 3
    if wd is not None:
        out_specs, out_shape = [_rows(tm, d)] + out_specs, [jax.ShapeDtypeStruct((s, d), F32)] + out_shape
    return pl.pallas_call(
        body, name=name, grid=(s // tm,),
        in_specs=[_rows(tm, d), _full((1, d))] + [pl.BlockSpec(memory_space=pl.ANY)] * nw + [_full(dep.shape)],
        out_specs=out_specs, out_shape=out_shape,
        scratch_shapes=[pltpu.VMEM((f, d), BF16)] * nw + [pltpu.SemaphoreType.DMA((2 * (len(bounds) - 1) + 1,))],
        compiler_params=_params(),
    )(x, gain, *[w[0] for w in weights], dep)


def ffn_down(x, act, wd, dep, name, target=None):
    s, d = x.shape
    f = act.shape[1]
    tm = _row_tile(s)

    def body(x_ref, a_ref, w_ref, dep_ref, *rest):
        y = x_ref[...] + 0.5 * _dot(a_ref[...], w_ref[...])
        if target is None:
            rest[0][...] = y
            return
        t_ref, dy_ref, acc_ref = rest

        @pl.when(pl.program_id(0) == 0)
        def _():
            acc_ref[...] = jnp.zeros(acc_ref.shape, F32)

        err = y - t_ref[...]
        dy_ref[...] = err * (1.0 / d)
        part = jnp.sum((err * err).reshape(tm // 8, 8, d), axis=0)
        acc = part[:, 0:LANES]
        for c0 in range(LANES, d, LANES):
            acc = acc + part[:, c0:c0 + LANES]
        acc_ref[...] = acc_ref[...] + acc

    ins = [_rows(tm, d), _rows(tm, f), _mat(*wd), _full(dep.shape)]
    if target is None:
        return pl.pallas_call(
            body, name=name, grid=(s // tm,), in_specs=ins, out_specs=_rows(tm, d),
            out_shape=jax.ShapeDtypeStruct((s, d), F32), compiler_params=_params(),
        )(x, act, wd[0], dep)
    return pl.pallas_call(
        body, name=name, grid=(s // tm,), in_specs=ins + [_rows(tm, d)],
        out_specs=[_rows(tm, d), _full((8, LANES))],
        out_shape=[jax.ShapeDtypeStruct((s, d), F32), jax.ShapeDtypeStruct((8, LANES), F32)],
        compiler_params=_params(),
    )(x, act, wd[0], dep, target)


def mix_in(x, gain, win_t, b_gate, gq_na, gk_na, gq_sw, gk_sw, bd, name):
    s, d = x.shape
    tm = _row_tile(s)
    gc = _col_chunk(2 * d)

    def body(x_ref, g_ref, w_ref, b_ref, gqa_ref, gka_ref, gqs_ref, gks_ref, bd_ref,
             hn_ref, zq_ref, qa_ref, ka_ref, qs_ref, ks_ref, gt_ref):
        xv = x_ref[...]
        hn = (xv * _rstd(xv) * g_ref[...]).astype(BF16)
        hn_ref[...] = hn

        def proj(c0, c1):
            return _dotg(hn, w_ref[c0:c1, :], NT)

        def headnorm(z, g, bdm):
            return z * lax.rsqrt(_group_mean(z * z, bdm) + EPS) * g

        bd512 = bd_ref[...]
        bd128 = bd_ref[0:SW_KV_WIDTH, 0:SW_KV_WIDTH]
        z = proj(0, 512)
        zq_ref[:, 0:512] = z.astype(BF16)
        qa_ref[...] = (headnorm(z, gqa_ref[...], bd512) * SCALE).astype(BF16)
        z = proj(512, 1024)
        zq_ref[:, 512:1024] = z.astype(BF16)
        ka_ref[...] = headnorm(z, gka_ref[...], bd512).astype(BF16)
        z = proj(1024, 1536)
        zq_ref[:, 1024:1536] = z.astype(BF16)
        z = proj(1536, 2048)
        zq_ref[:, 1536:2048] = z.astype(BF16)
        qs_ref[...] = (headnorm(z, gqs_ref[...], bd512) * SCALE).astype(BF16)
        z = proj(2048, 2176)
        zq_ref[:, 2048:2176] = z.astype(BF16)
        ks_ref[...] = headnorm(z, gks_ref[...], bd128).astype(BF16)
        z = proj(2176, 2304)
        zq_ref[:, 2176:2304] = z.astype(BF16)
        for c0 in range(0, 2 * d, gc):
            zg = proj(QKV_WIDTH + c0, QKV_WIDTH + c0 + gc) + b_ref[:, c0:c0 + gc]
            gt_ref[:, c0:c0 + gc] = _sigmoid(zg).astype(BF16)

    return pl.pallas_call(
        body, name=name, grid=(s // tm,),
        in_specs=[_rows(tm, d), _full((1, d)), _mat(*win_t), _full((1, 2 * d)),
                  _full((1, 512)), _full((1, 512)), _full((1, 512)), _full((1, 128)), _full((MXU_TILE, MXU_TILE))],
        out_specs=[_rows(tm, d), _rows(tm, QKV_WIDTH), _rows(tm, 512), _rows(tm, 512), _rows(tm, 512),
                   _rows(tm, 128), _rows(tm, 2 * d)],
        out_shape=[jax.ShapeDtypeStruct((s, d), BF16), jax.ShapeDtypeStruct((s, QKV_WIDTH), BF16),
                   jax.ShapeDtypeStruct((s, 512), BF16), jax.ShapeDtypeStruct((s, 512), BF16),
                   jax.ShapeDtypeStruct((s, 512), BF16), jax.ShapeDtypeStruct((s, 128), BF16),
                   jax.ShapeDtypeStruct((s, 2 * d), BF16)],
        compiler_params=_params(),
    )(x, gain, win_t[0], b_gate, gq_na, gk_na, gq_sw, gk_sw, bd)


def _na_iotas():
    qc = lax.broadcasted_iota(jnp.int32, (GRID_W, LANES), 0)
    ln = lax.broadcasted_iota(jnp.int32, (GRID_W, LANES), 1)
    low = ln < GRID_W
    kc = jnp.where(low, ln, ln - GRID_W)
    diff = kc - qc + (NA_COLS - 1)
    qcs = jnp.clip(qc - NA_COLS // 2, 0, GRID_W - NA_COLS)
    inwin = (kc >= qcs) & (kc < qcs + NA_COLS)
    return diff, low, inwin


NA_RI = 2 * NA_ROWS - 1
NA_CI = 2 * NA_COLS - 1
NA_T2 = NA_RI + 1


def _rpb_rows(rpb):
    h = rpb.shape[0]
    padded = jnp.pad(rpb, ((0, 0), (1, 1), (0, GRID_W - NA_CI)))
    return jnp.concatenate([padded[:, :NA_T2], padded[:, 1:NA_T2 + 1]], axis=2).reshape(h, NA_T2, LANES)


def _rpb_from_rows(rows):
    return rows[:, 1:, :NA_CI] + rows[:, :NA_RI, GRID_W:GRID_W + NA_CI]


def rpb_expand(rows, dep, name):
    n_heads = rows.shape[0]

    def body(r_ref, dep_ref, o_ref):
        for h in range(n_heads):
            for e in range(NA_T2):
                line = jnp.broadcast_to(r_ref[h, e:e + 1, :], (GRID_W, LANES))
                o_ref[h, e] = pltpu.roll(line, LANES - (NA_COLS - 1), 1, stride=1, stride_axis=0)

    return pl.pallas_call(
        body, name=name,
        in_specs=[pl.BlockSpec(memory_space=pltpu.VMEM), pl.BlockSpec(memory_space=pltpu.VMEM)],
        out_specs=pl.BlockSpec(memory_space=pltpu.VMEM),
        out_shape=jax.ShapeDtypeStruct((n_heads, NA_T2, GRID_W, LANES), F32),
        compiler_params=pltpu.CompilerParams(vmem_limit_bytes=V7X_VMEM_LIMIT),
    )(rows, dep)


def rpb_reduce(dt2, name):
    n_heads = dt2.shape[0]
    flip = jnp.asarray(np.eye(GRID_W)[::-1], BF16)

    def body(d_ref, j_ref, o_ref):
        jm = j_ref[...]
        for h in range(n_heads):
            for e in range(NA_T2):
                dv = d_ref[h, e]
                hi = dv.astype(BF16)
                mid = (dv - hi.astype(F32)).astype(BF16)
                lo = (dv - hi.astype(F32) - mid.astype(F32)).astype(BF16)
                rev = _dot(jm, hi) + _dot(jm, mid) + _dot(jm, lo)
                back = pltpu.roll(rev, LANES + (NA_COLS - 1) - (GRID_W - 1), 1, stride=1, stride_axis=0)
                o_ref[h, e:e + 1, :] = jnp.sum(back, axis=0, keepdims=True)

    return pl.pallas_call(
        body, name=name,
        in_specs=[pl.BlockSpec(memory_space=pltpu.VMEM)] * 2,
        out_specs=pl.BlockSpec(memory_space=pltpu.VMEM),
        out_shape=jax.ShapeDtypeStruct((n_heads, NA_T2, LANES), F32),
        compiler_params=pltpu.CompilerParams(vmem_limit_bytes=V7X_VMEM_LIMIT),
    )(dt2, flip)


NA_TQ = 4
NA_TK = NA_TQ + NA_ROWS
NA_KCH = NA_TK // 2


def _na_tile_geometry(t, rows):
    r = t * NA_TQ
    kbase = jnp.clip(r - NA_ROWS // 2, 0, rows - NA_TK)
    starts = [jnp.clip(r + a - NA_ROWS // 2, 0, rows - NA_ROWS) for a in range(NA_TQ)]
    return r, kbase, starts


def _na_tile_mask(kbase, starts, low, inwin):
    half = jnp.where(low, 0, 1)
    cols = []
    for c in range(NA_KCH):
        krow = kbase + 2 * c + half
        cols.append(jnp.concatenate(
            [jnp.where(inwin & (krow >= st) & (krow < st + NA_ROWS), 0.0, NEG) for st in starts], axis=0))
    return jnp.concatenate(cols, axis=1)


def _na_tile_index(r, kbase, a, c):
    return jnp.clip(kbase + 2 * c - (r + a) + NA_ROWS, 0, NA_T2 - 1)


def _na_tile_scores(q, k, t2_ref, hh, r, kbase, madd):
    bias = jnp.concatenate(
        [jnp.concatenate([t2_ref[hh, _na_tile_index(r, kbase, a, c)] for a in range(NA_TQ)], axis=0)
         for c in range(NA_KCH)], axis=1)
    return _dotg(q, k, NT) + bias + madd


def _softmax_rows(sc):
    e = jnp.exp(sc - jnp.max(sc, axis=1, keepdims=True))
    return e * (1.0 / jnp.sum(e, axis=1, keepdims=True))


def na_fwd(qa, ka, zq, t2, name):
    s = qa.shape[0]
    rows = s // GRID_W
    n_pairs = NA_WIDTH // LANES
    v_blk0 = (2 * NA_WIDTH) // LANES

    assert rows % NA_TQ == 0 and rows >= NA_TK
    tq, tk = NA_TQ * GRID_W, NA_TK * GRID_W

    def body(q_ref, k_ref, v_ref, t2_ref, o_ref, s_scr, p_scr):
        _, low, inwin = _na_iotas()

        def tile(t, carry):
            r, kbase, starts = _na_tile_geometry(t, rows)
            madd = _na_tile_mask(kbase, starts, low, inwin)
            qr = pl.ds(pl.multiple_of(r * GRID_W, tq), tq)
            kr = pl.ds(pl.multiple_of(kbase * GRID_W, tq), tk)
            for hh in range(2):
                lanes = slice(HEAD_DIM * hh, HEAD_DIM * (hh + 1))
                s_scr[tq * hh:tq * (hh + 1), :] = _na_tile_scores(q_ref[qr, lanes], k_ref[kr, lanes], t2_ref, hh, r,
                                                                  kbase, madd)
            p_scr[...] = _softmax_rows(s_scr[...]).astype(BF16)
            for hh in range(2):
                lanes = slice(HEAD_DIM * hh, HEAD_DIM * (hh + 1))
                o_ref[qr, lanes] = _dot(p_scr[tq * hh:tq * (hh + 1), :], v_ref[kr, lanes]).astype(BF16)
            return carry

        lax.fori_loop(0, rows // NA_TQ, tile, 0)

    col = lambda off: pl.BlockSpec((s, LANES), lambda p, _o=off: (0, _o + p))
    return pl.pallas_call(
        body, name=name, grid=(n_pairs,),
        in_specs=[col(0), col(0), col(v_blk0),
                  pl.BlockSpec((2, NA_T2, GRID_W, LANES), lambda p: (p, 0, 0, 0))],
        out_specs=col(0),
        out_shape=jax.ShapeDtypeStruct((s, NA_WIDTH), BF16),
        scratch_shapes=[pltpu.VMEM((2 * tq, tk), F32), pltpu.VMEM((2 * tq, tk), BF16)],
        compiler_params=_params(),
    )(qa, ka, zq, t2)


def na_bwd(qa, ka, zq, t2, o_na, do_na, name):
    s = qa.shape[0]
    rows = s // GRID_W
    n_pairs = NA_WIDTH // LANES
    v_blk0 = (2 * NA_WIDTH) // LANES

    tq, tk = NA_TQ * GRID_W, NA_TK * GRID_W

    def body(q_ref, k_ref, v_ref, t2_ref, o_ref, do_ref, dq_ref, dk_ref, dv_ref, dt2_ref):
        _, low, inwin = _na_iotas()
        dk_ref[...] = jnp.zeros(dk_ref.shape, F32)
        dv_ref[...] = jnp.zeros(dv_ref.shape, F32)
        dt2_ref[...] = jnp.zeros(dt2_ref.shape, F32)

        def tile(t, carry):
            r, kbase, starts = _na_tile_geometry(t, rows)
            madd = _na_tile_mask(kbase, starts, low, inwin)
            qr = pl.ds(pl.multiple_of(r * GRID_W, tq), tq)
            kr = pl.ds(pl.multiple_of(kbase * GRID_W, tq), tk)
            for hh in range(2):
                lanes = slice(HEAD_DIM * hh, HEAD_DIM * (hh + 1))
                q, k, v = q_ref[qr, lanes], k_ref[kr, lanes], v_ref[kr, lanes]
                p = _softmax_rows(_na_tile_scores(q, k, t2_ref, hh, r, kbase, madd))
                do = do_ref[qr, lanes]
                delta = jnp.sum(do.astype(F32) * o_ref[qr, lanes].astype(F32), axis=1, keepdims=True)
                ds = p * (_dotg(do, v, NT) - delta)
                shared = {}
                for a in range(NA_TQ):
                    for c in range(NA_KCH):
                        shared.setdefault(2 * c - a, []).append(
                            ds[GRID_W * a:GRID_W * (a + 1), LANES * c:LANES * (c + 1)])
                for offset, parts in shared.items():
                    e = jnp.clip(offset + kbase - r + NA_ROWS, 0, NA_T2 - 1)
                    dt2_ref[hh, e] = dt2_ref[hh, e] + functools.reduce(jnp.add, parts)
                dsb = ds.astype(BF16)
                dq_ref[qr, lanes] = _dot(dsb, k)
                dk_ref[kr, lanes] = dk_ref[kr, lanes] + _dotg(dsb, q, TN)
                dv_ref[kr, lanes] = dv_ref[kr, lanes] + _dotg(p.astype(BF16), do, TN)
            return carry

        lax.fori_loop(0, rows // NA_TQ, tile, 0)

    col = lambda off: pl.BlockSpec((s, LANES), lambda p, _o=off: (0, _o + p))
    t2spec = pl.BlockSpec((2, NA_T2, GRID_W, LANES), lambda p: (p, 0, 0, 0))
    return pl.pallas_call(
        body, name=name, grid=(n_pairs,),
        in_specs=[col(0), col(0), col(v_blk0), t2spec, col(0), col(0)],
        out_specs=[col(0), col(0), col(0), t2spec],
        out_shape=[jax.ShapeDtypeStruct((s, NA_WIDTH), F32)] * 3 + [jax.ShapeDtypeStruct(t2.shape, F32)],
        compiler_params=_params(),
    )(qa, ka, zq, t2, o_na, do_na)


def _t5_bucket_map():
    rel = np.arange(3 * SW_BLOCK)[None, :] - SW_BLOCK - np.arange(SW_BLOCK)[:, None]
    nb = REL_BUCKETS // 2
    max_exact = nb // 2
    n = np.abs(rel)
    large = max_exact + (np.log(np.maximum(n, 1) / max_exact)
                         / np.log(REL_MAX_DIST / max_exact) * (nb - max_exact)).astype(np.int32)
    large = np.minimum(large, nb - 1)
    return ((rel > 0) * nb + np.where(n < max_exact, n, large)).astype(np.int32)


def t5_expand(table, bmap, dep, name):
    def body(tab_ref, bm_ref, dep_ref, o_ref):
        bm = bm_ref[...]
        for h in range(SW_HEADS):
            t = jnp.zeros(bm.shape, F32)
            for b in range(REL_BUCKETS):
                t = jnp.where(bm == b, tab_ref[b, h], t)
            o_ref[h] = t

    return pl.pallas_call(
        body, name=name,
        in_specs=[pl.BlockSpec(memory_space=pltpu.SMEM), pl.BlockSpec(memory_space=pltpu.VMEM),
                  pl.BlockSpec(memory_space=pltpu.VMEM)],
        out_specs=pl.BlockSpec(memory_space=pltpu.VMEM),
        out_shape=jax.ShapeDtypeStruct((SW_HEADS,) + bmap.shape, F32),
        compiler_params=pltpu.CompilerParams(vmem_limit_bytes=V7X_VMEM_LIMIT),
    )(table, bmap, dep)


def t5_reduce(dbias_list, bmap, name):
    n = len(dbias_list)

    def body(*refs):
        d_refs, bm_ref, o_ref = refs[:n], refs[n], refs[n + 1]
        bm = bm_ref[...]
        for h in range(SW_HEADS):
            dv = d_refs[0][h]
            for other in d_refs[1:]:
                dv = dv + other[h]
            rows = [jnp.sum(jnp.where(bm == b, dv, 0.0), axis=0, keepdims=True) for b in range(REL_BUCKETS)]
            r = jnp.concatenate(rows, axis=0)
            o_ref[h] = jnp.broadcast_to(jnp.sum(r, axis=1, keepdims=True), (REL_BUCKETS, LANES))

    return pl.pallas_call(
        body, name=name,
        in_specs=[pl.BlockSpec(memory_space=pltpu.VMEM)] * (n + 1),
        out_specs=pl.BlockSpec(memory_space=pltpu.VMEM),
        out_shape=jax.ShapeDtypeStruct((SW_HEADS, REL_BUCKETS, LANES), F32),
        compiler_params=pltpu.CompilerParams(vmem_limit_bytes=V7X_VMEM_LIMIT),
    )(*dbias_list, bmap)


def _sw_mask_iotas():
    a = lax.broadcasted_iota(jnp.int32, (SW_BLOCK, 3 * SW_BLOCK), 0)
    j = lax.broadcasted_iota(jnp.int32, (SW_BLOCK, 3 * SW_BLOCK), 1)
    inwin = jnp.abs(j - SW_BLOCK - a) <= SW_BLOCK
    return j, inwin


SW_STACK = SW_HEADS * SW_BLOCK


def _sw_softmax(sc, sk):
    m = jnp.maximum(jnp.max(sc, axis=1, keepdims=True), sk)
    e = jnp.exp(sc - m)
    es = jnp.exp(sk - m)
    inv = 1.0 / (jnp.sum(e, axis=1, keepdims=True) + es)
    return e * inv, es * inv


def _sw_prologue(k_ref, v_ref, kp, vp, sink_ref, s):
    pad = s + 2 * SW_BLOCK
    zeros = jnp.zeros((SW_BLOCK, SW_KV_WIDTH), BF16)
    kp[0:SW_BLOCK, :] = zeros
    vp[0:SW_BLOCK, :] = zeros
    kp[SW_BLOCK + s:pad, :] = zeros
    vp[SW_BLOCK + s:pad, :] = zeros
    kp[SW_BLOCK:SW_BLOCK + s, :] = k_ref[...]
    vp[SW_BLOCK:SW_BLOCK + s, :] = v_ref[...]
    return jnp.concatenate([jnp.full((SW_BLOCK, 1), sink_ref[h], F32) for h in range(SW_HEADS)], axis=0)


def sw_fwd(qs, ks, zq, t5b, sink, dep, name):
    s = qs.shape[0]
    nb = s // SW_BLOCK
    v_blk = (3 * NA_WIDTH + SW_Q_WIDTH + SW_KV_WIDTH) // LANES
    pad = s + 2 * SW_BLOCK

    def body(q_ref, k_ref, v_ref, b_ref, sink_ref, dep_ref, o_ref, kp, vp, s_scr, p_scr):
        sink_col = _sw_prologue(k_ref, v_ref, kp, vp, sink_ref, s)
        j, inwin = _sw_mask_iotas()

        def blk(n, carry):
            kpos = n * SW_BLOCK - SW_BLOCK + j
            madd = jnp.where(inwin & (kpos >= 0) & (kpos < s), 0.0, NEG)
            q0 = pl.multiple_of(n * SW_BLOCK, SW_BLOCK)
            qr, kr = pl.ds(q0, SW_BLOCK), pl.ds(q0, 3 * SW_BLOCK)
            for h in range(SW_HEADS):
                g = h // SW_REP
                s_scr[SW_BLOCK * h:SW_BLOCK * (h + 1), :] = _dotg(
                    q_ref[qr, HEAD_DIM * h:HEAD_DIM * (h + 1)], kp[kr, HEAD_DIM * g:HEAD_DIM * (g + 1)], NT) + madd
            p, _ = _sw_softmax(s_scr[...] + b_ref[...], sink_col)
            p_scr[...] = p.astype(BF16)
            for h in range(SW_HEADS):
                g = h // SW_REP
                o_ref[qr, HEAD_DIM * h:HEAD_DIM * (h + 1)] = _dot(
                    p_scr[SW_BLOCK * h:SW_BLOCK * (h + 1), :], vp[kr, HEAD_DIM * g:HEAD_DIM * (g + 1)]).astype(BF16)
            return carry

        lax.fori_loop(0, nb, blk, 0)

    return pl.pallas_call(
        body, name=name, grid=(1,),
        in_specs=[_full((s, SW_Q_WIDTH)), _full((s, SW_KV_WIDTH)),
                  pl.BlockSpec((s, SW_KV_WIDTH), lambda i: (0, v_blk)),
                  _full((SW_STACK, 3 * SW_BLOCK)), pl.BlockSpec(memory_space=pltpu.SMEM),
                  _full(dep.shape)],
        out_specs=_full((s, SW_Q_WIDTH)),
        out_shape=jax.ShapeDtypeStruct((s, SW_Q_WIDTH), BF16),
        scratch_shapes=[pltpu.VMEM((pad, SW_KV_WIDTH), BF16), pltpu.VMEM((pad, SW_KV_WIDTH), BF16),
                        pltpu.VMEM((SW_STACK, 3 * SW_BLOCK), F32), pltpu.VMEM((SW_STACK, 3 * SW_BLOCK), BF16)],
        compiler_params=_params(),
    )(qs, ks, zq, t5b, sink, dep)


def sw_bwd(qs, ks, zq, t5b, sink, o_sw, do_sw, name):
    s = qs.shape[0]
    nb = s // SW_BLOCK
    v_blk = (3 * NA_WIDTH + SW_Q_WIDTH + SW_KV_WIDTH) // LANES
    pad = s + 2 * SW_BLOCK

    def body(q_ref, k_ref, v_ref, b_ref, sink_ref, o_ref, do_ref,
             dq_ref, dk_ref, dv_ref, db_ref, dsk_ref, kp, vp, dkp, dvp, s_scr, dp_scr, ds_scr, p_scr):
        sink_col = _sw_prologue(k_ref, v_ref, kp, vp, sink_ref, s)
        dkp[...] = jnp.zeros(dkp.shape, F32)
        dvp[...] = jnp.zeros(dvp.shape, F32)
        db_ref[...] = jnp.zeros(db_ref.shape, F32)
        dsk_ref[...] = jnp.zeros(dsk_ref.shape, F32)
        j, inwin = _sw_mask_iotas()

        def blk(n, carry):
            kpos = n * SW_BLOCK - SW_BLOCK + j
            madd = jnp.where(inwin & (kpos >= 0) & (kpos < s), 0.0, NEG)
            q0 = pl.multiple_of(n * SW_BLOCK, SW_BLOCK)
            qr, kr = pl.ds(q0, SW_BLOCK), pl.ds(q0, 3 * SW_BLOCK)
            deltas = []
            for h in range(SW_HEADS):
                g = h // SW_REP
                hl, kl = slice(HEAD_DIM * h, HEAD_DIM * (h + 1)), slice(HEAD_DIM * g, HEAD_DIM * (g + 1))
                rows = slice(SW_BLOCK * h, SW_BLOCK * (h + 1))
                do = do_ref[qr, hl]
                s_scr[rows, :] = _dotg(q_ref[qr, hl], kp[kr, kl], NT) + madd
                dp_scr[rows, :] = _dotg(do, vp[kr, kl], NT)
                deltas.append(jnp.sum(do.astype(F32) * o_ref[qr, hl].astype(F32), axis=1, keepdims=True))
            delta = jnp.concatenate(deltas, axis=0)
            p, ps = _sw_softmax(s_scr[...] + b_ref[...], sink_col)
            ds = p * (dp_scr[...] - delta)
            db_ref[...] = db_ref[...] + ds
            dsk_ref[...] = dsk_ref[...] - jnp.broadcast_to(ps * delta, (SW_STACK, LANES))
            ds_scr[...] = ds.astype(BF16)
            p_scr[...] = p.astype(BF16)
            for g in range(SW_HEADS // SW_REP):
                kl = slice(HEAD_DIM * g, HEAD_DIM * (g + 1))
                k = kp[kr, kl]
                dkw = jnp.zeros((3 * SW_BLOCK, HEAD_DIM), F32)
                dvw = jnp.zeros((3 * SW_BLOCK, HEAD_DIM), F32)
                for r in range(SW_REP):
                    h = g * SW_REP + r
                    hl, rows = slice(HEAD_DIM * h, HEAD_DIM * (h + 1)), slice(SW_BLOCK * h, SW_BLOCK * (h + 1))
                    dsb = ds_scr[rows, :]
                    dq_ref[qr, hl] = _dot(dsb, k)
                    dkw = dkw + _dotg(dsb, q_ref[qr, hl], TN)
                    dvw = dvw + _dotg(p_scr[rows, :], do_ref[qr, hl], TN)
                dkp[kr, kl] = dkp[kr, kl] + dkw
                dvp[kr, kl] = dvp[kr, kl] + dvw
            return carry

        lax.fori_loop(0, nb, blk, 0)
        dk_ref[...] = dkp[SW_BLOCK:SW_BLOCK + s, :]
        dv_ref[...] = dvp[SW_BLOCK:SW_BLOCK + s, :]

    bias_spec = _full((SW_STACK, 3 * SW_BLOCK))
    return pl.pallas_call(
        body, name=name, grid=(1,),
        in_specs=[_full((s, SW_Q_WIDTH)), _full((s, SW_KV_WIDTH)),
                  pl.BlockSpec((s, SW_KV_WIDTH), lambda i: (0, v_blk)),
                  bias_spec, pl.BlockSpec(memory_space=pltpu.SMEM),
                  _full((s, SW_Q_WIDTH)), _full((s, SW_Q_WIDTH))],
        out_specs=[_full((s, SW_Q_WIDTH)), _full((s, SW_KV_WIDTH)), _full((s, SW_KV_WIDTH)), bias_spec,
                   _full((SW_STACK, LANES))],
        out_shape=[jax.ShapeDtypeStruct((s, SW_Q_WIDTH), F32), jax.ShapeDtypeStruct((s, SW_KV_WIDTH), F32),
                   jax.ShapeDtypeStruct((s, SW_KV_WIDTH), F32),
                   jax.ShapeDtypeStruct((SW_STACK, 3 * SW_BLOCK), F32),
                   jax.ShapeDtypeStruct((SW_STACK, LANES), F32)],
        scratch_shapes=[pltpu.VMEM((pad, SW_KV_WIDTH), BF16), pltpu.VMEM((pad, SW_KV_WIDTH), BF16),
                        pltpu.VMEM((pad, SW_KV_WIDTH), F32), pltpu.VMEM((pad, SW_KV_WIDTH), F32),
                        pltpu.VMEM((SW_STACK, 3 * SW_BLOCK), F32), pltpu.VMEM((SW_STACK, 3 * SW_BLOCK), F32),
                        pltpu.VMEM((SW_STACK, 3 * SW_BLOCK), BF16), pltpu.VMEM((SW_STACK, 3 * SW_BLOCK), BF16)],
        compiler_params=_params(),
    )(qs, ks, zq, t5b, sink, o_sw, do_sw)


def merge_out(x, o_na, o_sw, gt, wbna_t, wbsw_t, wout, name):
    s, d = x.shape
    tm = _row_tile(s)

    def body(x_ref, ona_ref, osw_ref, gt_ref, wna_ref, wsw_ref, wo_ref, xo_ref, ana_ref, asw_ref, mg_ref):
        a_na = _dotg(ona_ref[...], wna_ref[...], NT)
        a_sw = _dotg(osw_ref[...], wsw_ref[...], NT)
        g_na, g_sw = gt_ref[:, 0:d].astype(F32), gt_ref[:, d:2 * d].astype(F32)
        ana_ref[...] = (a_na * g_na * (1.0 - g_na)).astype(BF16)
        asw_ref[...] = (a_sw * g_sw * (1.0 - g_sw)).astype(BF16)
        merged = (g_na * a_na + g_sw * a_sw).astype(BF16)
        mg_ref[...] = merged
        xo_ref[...] = x_ref[...] + _dot(merged, wo_ref[...])

    return pl.pallas_call(
        body, name=name, grid=(s // tm,),
        in_specs=[_rows(tm, d), _rows(tm, 512), _rows(tm, 512), _rows(tm, 2 * d),
                  _mat(*wbna_t), _mat(*wbsw_t), _mat(*wout)],
        out_specs=[_rows(tm, d)] * 4,
        out_shape=[jax.ShapeDtypeStruct((s, d), F32)] + [jax.ShapeDtypeStruct((s, d), BF16)] * 3,
        compiler_params=_params(),
    )(x, o_na, o_sw, gt, wbna_t[0], wbsw_t[0], wout[0])


def mix_bwd_out(dx, gt, a_na, a_sw, wbna_t, wbsw_t, wout, dep, name):
    s, d = dx.shape
    tm = _row_tile(s)

    def body(dx_ref, gt_ref, ana_ref, asw_ref, wna_ref, wsw_ref, wo_ref, dep_ref,
             dxb_ref, dzg_ref, dana_ref, dasw_ref, dona_ref, dosw_ref, dbg_ref):
        @pl.when(pl.program_id(0) == 0)
        def _():
            dbg_ref[...] = jnp.zeros(dbg_ref.shape, F32)

        dxb = dx_ref[...].astype(BF16)
        dxb_ref[...] = dxb
        dm = _dotg(dxb, wo_ref[...], NT)
        for i, (a_ref, da_ref, w_ref, do_ref) in enumerate(
                [(ana_ref, dana_ref, wna_ref, dona_ref), (asw_ref, dasw_ref, wsw_ref, dosw_ref)]):
            gi = gt_ref[:, i * d:(i + 1) * d].astype(F32)
            da = (dm * gi).astype(BF16)
            da_ref[...] = da
            do_ref[...] = _dot(da, w_ref[...]).astype(BF16)
            dzg = dm * a_ref[...].astype(F32)
            dzg_ref[:, i * d:(i + 1) * d] = dzg.astype(BF16)
            dbg_ref[:, i * d:(i + 1) * d] = dbg_ref[:, i * d:(i + 1) * d] + jnp.sum(dzg, axis=0, keepdims=True)

    return pl.pallas_call(
        body, name=name, grid=(s // tm,),
        in_specs=[_rows(tm, d), _rows(tm, 2 * d), _rows(tm, d), _rows(tm, d),
                  _mat(*wbna_t), _mat(*wbsw_t), _mat(*wout), _full(dep.shape)],
        out_specs=[_rows(tm, d), _rows(tm, 2 * d), _rows(tm, d), _rows(tm, d), _rows(tm, 512), _rows(tm, 512),
                   _full((1, 2 * d))],
        out_shape=[jax.ShapeDtypeStruct((s, d), BF16), jax.ShapeDtypeStruct((s, 2 * d), BF16),
                   jax.ShapeDtypeStruct((s, d), BF16), jax.ShapeDtypeStruct((s, d), BF16),
                   jax.ShapeDtypeStruct((s, 512), BF16), jax.ShapeDtypeStruct((s, 512), BF16),
                   jax.ShapeDtypeStruct((1, 2 * d), F32)],
        compiler_params=_params(),
    )(dx, gt, a_na, a_sw, wbna_t[0], wbsw_t[0], wout[0], dep)


def qk_norm_bwd(dqa, dka, dva, dqs, dks, dvs, zq, dzg, gq_na, gk_na, gq_sw, gk_sw, bd, name):
    s = zq.shape[0]
    d2 = dzg.shape[1]
    n_in = QKV_WIDTH + d2
    tm = _row_tile(s)

    def body(dqa_ref, dka_ref, dva_ref, dqs_ref, dks_ref, dvs_ref, zq_ref, dzg_ref,
             gqa_ref, gka_ref, gqs_ref, gks_ref, bd_ref, dz_ref, dgqa_ref, dgka_ref, dgqs_ref, dgks_ref):
        @pl.when(pl.program_id(0) == 0)
        def _():
            for r in (dgqa_ref, dgka_ref, dgqs_ref, dgks_ref):
                r[...] = jnp.zeros(r.shape, F32)

        bd512 = bd_ref[...]
        bd128 = bd_ref[0:SW_KV_WIDTH, 0:SW_KV_WIDTH]

        def one(c0, c1, dy_ref, g_ref, dg_ref, bdm, scale):
            z = zq_ref[:, c0:c1].astype(F32)
            r = lax.rsqrt(_group_mean(z * z, bdm) + EPS)
            zh = z * r
            dy = dy_ref[...] * scale
            dyg = dy * g_ref[...]
            dz = r * (dyg - zh * _group_mean(dyg * zh, bdm))
            dz_ref[:, c0:c1] = dz.astype(BF16)
            dg_ref[...] = dg_ref[...] + jnp.sum(dy * zh, axis=0, keepdims=True)

        one(0, 512, dqa_ref, gqa_ref, dgqa_ref, bd512, SCALE)
        one(512, 1024, dka_ref, gka_ref, dgka_ref, bd512, 1.0)
        dz_ref[:, 1024:1536] = dva_ref[...].astype(BF16)
        one(1536, 2048, dqs_ref, gqs_ref, dgqs_ref, bd512, SCALE)
        one(2048, 2176, dks_ref, gks_ref, dgks_ref, bd128, 1.0)
        dz_ref[:, 2176:2304] = dvs_ref[...].astype(BF16)
        dz_ref[:, QKV_WIDTH:n_in] = dzg_ref[...]

    return pl.pallas_call(
        body, name=name, grid=(s // tm,),
        in_specs=[_rows(tm, 512), _rows(tm, 512), _rows(tm, 512), _rows(tm, 512), _rows(tm, 128), _rows(tm, 128),
                  _rows(tm, QKV_WIDTH), _rows(tm, d2),
                  _full((1, 512)), _full((1, 512)), _full((1, 512)), _full((1, 128)), _full((MXU_TILE, MXU_TILE))],
        out_specs=[_rows(tm, n_in), _full((1, 512)), _full((1, 512)), _full((1, 512)), _full((1, 128))],
        out_shape=[jax.ShapeDtypeStruct((s, n_in), BF16)] + [jax.ShapeDtypeStruct((1, 512), F32)] * 3
                  + [jax.ShapeDtypeStruct((1, 128), F32)],
        compiler_params=_params(),
    )(dqa, dka, dva, dqs, dks, dvs, zq, dzg, gq_na, gk_na, gq_sw, gk_sw, bd)


def ffn_bwd_act(dx, wd, hg, hu, name):
    s, d = dx.shape
    f = wd[0].shape[1]
    tm = _row_tile(s)
    fc = _col_chunk(f)

    bounds = list(range(0, f + 1, fc))

    def body(dx_ref, w_hbm, hg_ref, hu_ref, dxb_ref, dhg_ref, dhu_ref, w_vm, sems):
        start_all, ready = _row_streams([w_hbm], (wd[1],), [w_vm], sems, bounds)
        start_all()
        dxv = dx_ref[...]
        dxb_ref[...] = dxv.astype(BF16)
        half = (0.5 * dxv).astype(BF16)
        for j, c0 in enumerate(bounds[:-1]):
            ready(j)
            dact = _dotg(half, w_vm[c0:c0 + fc, :], NT)
            dhu_ref[:, c0:c0 + fc] = (dact * hu_ref[:, c0:c0 + fc].astype(F32)).astype(BF16)
            dhg_ref[:, c0:c0 + fc] = (dact * hg_ref[:, c0:c0 + fc].astype(F32)).astype(BF16)

    return pl.pallas_call(
        body, name=name, grid=(s // tm,),
        in_specs=[_rows(tm, d), pl.BlockSpec(memory_space=pl.ANY), _rows(tm, f), _rows(tm, f)],
        out_specs=[_rows(tm, d), _rows(tm, f), _rows(tm, f)],
        out_shape=[jax.ShapeDtypeStruct((s, d), BF16), jax.ShapeDtypeStruct((s, f), BF16),
                   jax.ShapeDtypeStruct((s, f), BF16)],
        scratch_shapes=[pltpu.VMEM((f, d), BF16), pltpu.SemaphoreType.DMA((len(bounds) - 1,))],
        compiler_params=_params(),
    )(dx, wd[0], hg, hu)


def proj_bwd_norm(acts, weights, x, gain, dx, dep, name):
    s, d = x.shape
    tm = min(_row_tile(s), 256)
    n = len(acts)

    def body(*refs):
        a_refs, w_refs = refs[:n], refs[n:2 * n]
        x_ref, g_ref, dx_ref, _, o_ref, dg_ref = refs[2 * n:]

        @pl.when(pl.program_id(0) == 0)
        def _():
            dg_ref[...] = jnp.zeros(dg_ref.shape, F32)

        dxn = _dot(a_refs[0][...], w_refs[0][...])
        for a_ref, w_ref in zip(a_refs[1:], w_refs[1:]):
            dxn = dxn + _dot(a_ref[...], w_ref[...])
        xv = x_ref[...]
        r = _rstd(xv)
        xh = xv * r
        dxh = dxn * g_ref[...]
        o_ref[...] = dx_ref[...] + r * (dxh - xh * jnp.mean(dxh * xh, axis=-1, keepdims=True))
        dg_ref[...] = dg_ref[...] + jnp.sum(dxn * xh, axis=0, keepdims=True)

    return pl.pallas_call(
        body, name=name, grid=(s // tm,),
        in_specs=[_rows(tm, a.shape[1]) for a in acts] + [_mat(*w) for w in weights]
                 + [_rows(tm, d), _full((1, d)), _rows(tm, d), _full(dep.shape)],
        out_specs=[_rows(tm, d), _full((1, d))],
        out_shape=[jax.ShapeDtypeStruct((s, d), F32), jax.ShapeDtypeStruct((1, d), F32)],
        compiler_params=_params(),
    )(*acts, *[w[0] for w in weights], x, gain, dx, dep)


def tn_matmul(products, name):
    s, n = products[0][0].shape
    tn = _tn_tile(n) if len(products) == 1 else _col_chunk(n)
    rhs = []
    for _, b, _ in products:
        if not any(b is seen for seen in rhs):
            rhs.append(b)
    which = [next(i for i, seen in enumerate(rhs) if b is seen) for _, b, _ in products]
    npr, nr = len(products), len(rhs)

    def body(*refs):
        a_refs, b_refs, o_refs = refs[:npr], refs[npr:npr + nr], refs[npr + nr:]
        for i, (_, _, scale) in enumerate(products):
            o_refs[i][...] = (scale * _dotg(a_refs[i][...], b_refs[which[i]][...], TN)).astype(BF16)

    return pl.pallas_call(
        body, name=name, grid=(n // tn,),
        in_specs=[pl.BlockSpec((s, tn), lambda i: (0, i))] * npr
                 + [pl.BlockSpec(b.shape, lambda i: (0, 0), pipeline_mode=ONCE) for b in rhs],
        out_specs=[pl.BlockSpec((tn, b.shape[1]), lambda i: (i, 0)) for _, b, _ in products],
        out_shape=[jax.ShapeDtypeStruct((n, b.shape[1]), BF16) for _, b, _ in products],
        compiler_params=_params(),
    )(*[a for a, _, _ in products], *rhs)


def _mesh_pos():
    return lax.axis_index("x"), lax.axis_index("y"), lax.axis_index("c")


def _peers():
    x, y, c = _mesh_pos()
    peers = []
    for rel in range(1, N_DEV):
        peers.append((1 - x if rel & 4 else x, 1 - y if rel & 2 else y, 1 - c if rel & 1 else c))
    return 4 * x + 2 * y + c, peers


HBM_SPEC = pl.BlockSpec(memory_space=pltpu.HBM)
SEM_SPEC = pl.BlockSpec(memory_space=pltpu.SEMAPHORE)


def _split_call(body, name, thru, n_sems, extra=(), with_token=True):
    hbm = lambda t: pltpu.with_memory_space_constraint(t, pltpu.HBM)
    effect = pltpu.CompilerParams(has_side_effects=pltpu.SideEffectType.DATAFLOW_SIDE_EFFECTING)
    nt = len(thru)
    thru_shapes = [pltpu.HBM(t.shape, t.dtype) for t in thru]
    if with_token:
        (after,) = extra
        outs = pl.pallas_call(
            body, name=name, in_specs=[HBM_SPEC] * nt + [pl.BlockSpec(memory_space=pl.ANY)],
            out_specs=[SEM_SPEC] * len(n_sems) + [HBM_SPEC] * nt + [pl.BlockSpec(memory_space=pltpu.VMEM)],
            out_shape=[pltpu.SemaphoreType.DMA((k,)) for k in n_sems] + thru_shapes
                      + [jax.ShapeDtypeStruct((8, LANES), F32)],
            input_output_aliases={i: len(n_sems) + i for i in range(nt)}, compiler_params=effect,
        )(*[hbm(t) for t in thru], after)
        return outs[:len(n_sems)], outs[len(n_sems):-1], outs[-1]
    return pl.pallas_call(
        body, name=name,
        in_specs=[HBM_SPEC] * nt + [SEM_SPEC] * len(n_sems) + [pl.BlockSpec(memory_space=pl.ANY)],
        out_specs=[HBM_SPEC] * nt, out_shape=thru_shapes,
        input_output_aliases={i: i for i in range(nt)}, compiler_params=effect,
    )(*thru, *extra)


def _gather_targets():
    x, y, c = _mesh_pos()
    return 4 * x + 2 * y + c, [(x, y, 1 - c), (1 - x, y, c), (x, 1 - y, c), (1 - x, 1 - y, c)]


def gather_start(shards, after, name):
    n = len(shards)
    zones = [lax.empty((w.shape[0], N_DEV) + w.shape[1:], w.dtype) for w in shards]

    def body(*refs):
        ins, zs = refs[:n], refs[n:2 * n]
        send_sems, recv_sems, local_sems = refs[2 * n + 1:2 * n + 4]
        token = refs[-1]
        me, targets = _gather_targets()
        for a in range(n):
            pltpu.make_async_copy(ins[a], zs[a].at[:, me], local_sems.at[a]).start()
            for k, to in enumerate(targets):
                pltpu.make_async_remote_copy(
                    src_ref=ins[a], dst_ref=zs[a].at[:, me], send_sem=send_sems.at[4 * a + k],
                    recv_sem=recv_sems.at[4 * a + k], device_id=to, device_id_type=MESH).start()
        token[...] = jnp.zeros(token.shape, F32)

    sems, thru, token = _split_call(body, name, list(shards) + zones, (4 * n, 4 * n, n), extra=(after,))
    return (sems, thru, n), token


def gather_wait(started, after, name):
    sems, thru, n = started

    def body(*refs):
        zs = refs[n:2 * n]
        send_sems, recv_sems, local_sems = refs[2 * n:2 * n + 3]
        _, targets = _gather_targets()
        for a in range(n):
            for k, to in enumerate(targets):
                cp = pltpu.make_async_remote_copy(
                    src_ref=zs[a].at[:, 0], dst_ref=zs[a].at[:, 0], send_sem=send_sems.at[4 * a + k],
                    recv_sem=recv_sems.at[4 * a + k], device_id=to, device_id_type=MESH)
                cp.wait_send()
                cp.wait_recv()
            pltpu.make_async_copy(zs[a].at[:, 0], zs[a].at[:, 0], local_sems.at[a]).wait()

    return _split_call(body, name, thru, (4 * n, 4 * n, n), extra=(*sems, after), with_token=False)[n:]


def forward_start(zones, after, name):
    n = len(zones)

    def body(*refs):
        zs = refs[:n]
        send_sems, recv_sems = refs[n + 1:n + 3]
        token = refs[-1]
        x, y, c = _mesh_pos()
        for a in range(n):
            for j, chip in enumerate([(1 - x, y), (x, 1 - y), (1 - x, 1 - y)]):
                blk = zs[a].at[:, 4 * chip[0] + 2 * chip[1] + c]
                pltpu.make_async_remote_copy(
                    src_ref=blk, dst_ref=blk, send_sem=send_sems.at[3 * a + j], recv_sem=recv_sems.at[3 * a + j],
                    device_id=(x, y, 1 - c), device_id_type=MESH).start()
        token[...] = jnp.zeros(token.shape, F32)

    sems, thru, token = _split_call(body, name, list(zones), (3 * n, 3 * n), extra=(after,))
    return (sems, thru, n), token


def forward_wait(started, after, name):
    sems, thru, n = started

    def body(*refs):
        zs = refs[:n]
        send_sems, recv_sems = refs[n:n + 2]
        x, y, c = _mesh_pos()
        for a in range(n):
            for j in range(3):
                cp = pltpu.make_async_remote_copy(
                    src_ref=zs[a].at[:, 0], dst_ref=zs[a].at[:, 0], send_sem=send_sems.at[3 * a + j],
                    recv_sem=recv_sems.at[3 * a + j], device_id=(x, y, 1 - c), device_id_type=MESH)
                cp.wait_send()
                cp.wait_recv()

    return _split_call(body, name, thru, (3 * n, 3 * n), extra=(*sems, after), with_token=False)


def scatter_start(groups, name):
    n = len(groups)
    flat = [g for grp in groups for g in grp]
    nf = len(flat)
    offs = np.cumsum([0] + [len(grp) for grp in groups])
    lands = [lax.empty((N_DEV, len(grp)) + grp[0].shape[1:], grp[0].dtype) for grp in groups]

    def body(*refs):
        ins, zones = refs[:nf], refs[nf:nf + n]
        send_sems, recv_sems, local_sems = refs[nf + n:nf + n + 3]
        token = refs[-1]
        me, peers = _peers()
        for a in range(n):
            for w in range(len(groups[a])):
                pltpu.make_async_copy(ins[offs[a] + w].at[me], zones[a].at[me, w], local_sems.at[a]).start()
        for k, peer in enumerate(peers):
            p_id = 4 * peer[0] + 2 * peer[1] + peer[2]
            for a in range(n):
                for w in range(len(groups[a])):
                    pltpu.make_async_remote_copy(
                        src_ref=ins[offs[a] + w].at[p_id], dst_ref=zones[a].at[me, w],
                        send_sem=send_sems.at[7 * a + k], recv_sem=recv_sems.at[7 * a + k],
                        device_id=peer, device_id_type=MESH).start()
        token[...] = jnp.zeros(token.shape, F32)

    hbm = lambda t: pltpu.with_memory_space_constraint(t, pltpu.HBM)
    outs = pl.pallas_call(
        body, name=name,
        in_specs=[HBM_SPEC] * (nf + n),
        out_specs=[SEM_SPEC] * 3 + [HBM_SPEC] * (nf + n) + [pl.BlockSpec(memory_space=pltpu.VMEM)],
        out_shape=[pltpu.SemaphoreType.DMA((7 * n,)), pltpu.SemaphoreType.DMA((7 * n,)), pltpu.SemaphoreType.DMA((n,))]
                  + [pltpu.HBM(t.shape, t.dtype) for t in flat + lands]
                  + [jax.ShapeDtypeStruct((8, LANES), F32)],
        input_output_aliases={i: 3 + i for i in range(nf + n)},
        compiler_params=pltpu.CompilerParams(has_side_effects=pltpu.SideEffectType.DATAFLOW_SIDE_EFFECTING),
    )(*[hbm(t) for t in flat], *[hbm(t) for t in lands])
    sems, thru, token = outs[:3], outs[3:3 + nf + n], outs[-1]
    return (sems, thru, [len(grp) for grp in groups]), token


def scatter_wait(started, after, name):
    (send_sems, recv_sems, local_sems), thru, sizes = started
    n = len(sizes)
    nf = len(thru) - n

    def body(*refs):
        zones = refs[nf:nf + n]
        s_sems, r_sems, l_sems = refs[nf + n:nf + n + 3]
        me, peers = _peers()
        for a in range(n):
            for k, peer in enumerate(peers):
                cp = pltpu.make_async_remote_copy(
                    src_ref=zones[a].at[0], dst_ref=zones[a].at[0],
                    send_sem=s_sems.at[7 * a + k], recv_sem=r_sems.at[7 * a + k], device_id=peer,
                    device_id_type=MESH)
                cp.wait_send()
                cp.wait_recv()
            pltpu.make_async_copy(zones[a].at[0], zones[a].at[0], l_sems.at[a]).wait()

    outs = pl.pallas_call(
        body, name=name,
        in_specs=[HBM_SPEC] * (nf + n) + [SEM_SPEC] * 3 + [pl.BlockSpec(memory_space=pl.ANY)],
        out_specs=[HBM_SPEC] * (nf + n),
        out_shape=[pltpu.HBM(t.shape, t.dtype) for t in thru],
        input_output_aliases={i: i for i in range(nf + n)},
        compiler_params=pltpu.CompilerParams(has_side_effects=pltpu.SideEffectType.DATAFLOW_SIDE_EFFECTING),
    )(*thru, send_sems, recv_sems, local_sems, after)
    return outs[nf:]


def pair_start(grads, after, name):
    nw = len(grads)
    land = lax.empty((4, nw) + grads[0].shape[1:], grads[0].dtype)

    def body(*refs):
        ins, zone = refs[:nw], refs[nw]
        send_sems, recv_sems = refs[nw + 2:nw + 4]
        x, y, c = _mesh_pos()
        for j in range(4):
            for w in range(nw):
                pltpu.make_async_remote_copy(
                    src_ref=ins[w].at[2 * j + (1 - c)], dst_ref=zone.at[j, w], send_sem=send_sems.at[0],
                    recv_sem=recv_sems.at[0], device_id=(x, y, 1 - c), device_id_type=MESH).start()
        refs[-1][...] = jnp.zeros(refs[-1].shape, F32)

    sems, thru, token = _split_call(body, name, list(grads) + [land], (1, 1), extra=(after,))
    return (sems, thru, nw), token


def pair_wait(started, after, name):
    sems, thru, nw = started

    def body(*refs):
        zone = refs[nw]
        send_sems, recv_sems = refs[nw + 1:nw + 3]
        x, y, c = _mesh_pos()
        cp = pltpu.make_async_remote_copy(src_ref=zone, dst_ref=zone, send_sem=send_sems.at[0],
                                          recv_sem=recv_sems.at[0], device_id=(x, y, 1 - c), device_id_type=MESH)
        cp.wait_send()
        cp.wait_recv()

    outs = _split_call(body, name, thru, (1, 1), extra=(*sems, after), with_token=False)
    return outs[:nw], outs[nw]


def pair_sum(grads, land, name):
    nw = len(grads)
    _, r, c_dim = grads[0].shape

    def body(*refs):
        g_refs, l_ref, o_ref = refs[:nw], refs[nw], refs[nw + 1]
        core = lax.axis_index("c")
        for w in range(nw):
            o_ref[0, w] = (g_refs[w][0, core].astype(F32) + l_ref[0, w].astype(F32)).astype(BF16)

    return pl.pallas_call(
        body, name=name, grid=(4,),
        in_specs=[pl.BlockSpec((1, 2, r, c_dim), lambda j: (j, 0, 0, 0))] * nw
                 + [pl.BlockSpec((1, nw, r, c_dim), lambda j: (j, 0, 0, 0))],
        out_specs=pl.BlockSpec((1, nw, r, c_dim), lambda j: (j, 0, 0, 0)),
        out_shape=jax.ShapeDtypeStruct((4, nw, r, c_dim), BF16),
        compiler_params=_params(),
    )(*[g.reshape(4, 2, r, c_dim) for g in grads], land)


def _other_chips():
    x, y, c = _mesh_pos()
    chips = []
    for rel in range(1, 4):
        px, py = (1 - x if rel & 2 else x), (1 - y if rel & 1 else y)
        chips.append((px, py, 2 * px + py))
    return 2 * x + y, c, chips


def chip_start(pair_sums, after, name):
    land = lax.empty(pair_sums.shape, pair_sums.dtype)

    def body(*refs):
        h_ref, zone = refs[0], refs[1]
        send_sems, recv_sems, local_sem = refs[3:6]
        mine, c, chips = _other_chips()
        pltpu.make_async_copy(h_ref.at[mine], zone.at[mine], local_sem.at[0]).start()
        for k, (px, py, j) in enumerate(chips):
            pltpu.make_async_remote_copy(
                src_ref=h_ref.at[j], dst_ref=zone.at[mine], send_sem=send_sems.at[k], recv_sem=recv_sems.at[k],
                device_id=(px, py, c), device_id_type=MESH).start()
        refs[-1][...] = jnp.zeros(refs[-1].shape, F32)

    sems, thru, token = _split_call(body, name, [pair_sums, land], (3, 3, 1), extra=(after,))
    return (sems, thru), token


def chip_wait(started, after, name):
    sems, thru = started

    def body(*refs):
        zone = refs[1]
        send_sems, recv_sems, local_sem = refs[2:5]
        _, c, chips = _other_chips()
        for k, (px, py, _) in enumerate(chips):
            cp = pltpu.make_async_remote_copy(
                src_ref=zone.at[0], dst_ref=zone.at[0], send_sem=send_sems.at[k], recv_sem=recv_sems.at[k],
                device_id=(px, py, c), device_id_type=MESH)
            cp.wait_send()
            cp.wait_recv()
        pltpu.make_async_copy(zone.at[0], zone.at[0], local_sem.at[0]).wait()

    return _split_call(body, name, thru, (3, 3, 1), extra=(*sems, after), with_token=False)[1]


def share_small(parts, after):
    n = len(parts)

    def body(*refs):
        ins, outs = refs[:n], refs[n + 1:2 * n + 1]
        send_sems, recv_sems, local_sems = refs[2 * n + 1:]
        me, peers = _peers()
        copies = []
        for i in range(n):
            copies.append(pltpu.make_async_copy(ins[i], outs[i].at[me], local_sems.at[i]))
            copies += [pltpu.make_async_remote_copy(
                src_ref=ins[i], dst_ref=outs[i].at[me], send_sem=send_sems.at[7 * i + k],
                recv_sem=recv_sems.at[7 * i + k], device_id=peer, device_id_type=MESH)
                for k, peer in enumerate(peers)]
        for cp in copies:
            cp.start()
        for cp in copies:
            cp.wait()

    vm = pl.BlockSpec(memory_space=pltpu.VMEM)
    return pl.pallas_call(
        body, name="share_small", in_specs=[vm] * n + [pl.BlockSpec(memory_space=pl.ANY)], out_specs=[vm] * n,
        out_shape=[jax.ShapeDtypeStruct((N_DEV,) + p.shape, p.dtype) for p in parts],
        scratch_shapes=[pltpu.SemaphoreType.DMA((7 * n,)), pltpu.SemaphoreType.DMA((7 * n,)),
                        pltpu.SemaphoreType.DMA((n,))],
    )(*parts, after)


def _adamw_math(w, g, m, v):
    m = ADAM_B1 * m + (1.0 - ADAM_B1) * g
    v = ADAM_B2 * v + (1.0 - ADAM_B2) * (g * g)
    m_hat = m / (1.0 - ADAM_B1 ** ADAM_STEP)
    v_hat = v / (1.0 - ADAM_B2 ** ADAM_STEP)
    delta = -ADAM_LR * (m_hat / (jnp.sqrt(v_hat) + ADAM_EPS) + ADAM_WD * w)
    return delta, m, v


def adamw_layer(zone, w_idx, layer, w, m, v, prev, after, name):
    n_src, _, r, c = zone.shape
    depth = w.shape[0]
    if prev is None:
        prev = tuple(lax.empty((depth, r, c), F32) for _ in range(4))
    tr = r // 2 if r % 16 == 0 else r

    def body(z_ref, w_ref, m_ref, v_ref, *rest):
        g_ref, d_ref, mo_ref, vo_ref = rest[5:]
        g = z_ref[0].astype(F32)
        for src in range(1, n_src):
            g = g + z_ref[src].astype(F32)
        g_ref[...] = g
        d_ref[...], mo_ref[...], vo_ref[...] = _adamw_math(w_ref[...], g, m_ref[...], v_ref[...])

    rows = pl.BlockSpec((None, tr, c), lambda i: (layer, i, 0))
    anywhere = pl.BlockSpec(memory_space=pl.ANY)
    return pl.pallas_call(
        body, name=name, grid=(r // tr,),
        in_specs=[pl.BlockSpec((n_src, None, tr, c), lambda i: (0, w_idx, i, 0)), rows, rows, rows]
                 + [anywhere] * 5,
        out_specs=[rows] * 4,
        out_shape=[jax.ShapeDtypeStruct((depth, r, c), F32)] * 4,
        input_output_aliases={4 + k: k for k in range(4)},
        compiler_params=_params(),
    )(zone, w, m, v, *prev, after)


def adamw_small(ws, recvs, ms, vs, name):
    n = len(ws)

    def body(*refs):
        w_refs, r_refs, m_refs, v_refs = (refs[i * n:(i + 1) * n] for i in range(4))
        g_refs, d_refs, mo_refs, vo_refs = (refs[(4 + i) * n:(5 + i) * n] for i in range(4))
        for i in range(n):
            g = r_refs[i][0]
            for src in range(1, N_DEV):
                g = g + r_refs[i][src]
            g_refs[i][...] = g
            d_refs[i][...], mo_refs[i][...], vo_refs[i][...] = _adamw_math(w_refs[i][...], g, m_refs[i][...],
                                                                            v_refs[i][...])

    vm = pl.BlockSpec(memory_space=pltpu.VMEM)
    outs = pl.pallas_call(
        body, name=name, in_specs=[vm] * (4 * n), out_specs=[vm] * (4 * n),
        out_shape=[jax.ShapeDtypeStruct(w.shape, F32) for w in ws] * 4,
        compiler_params=pltpu.CompilerParams(vmem_limit_bytes=V7X_VMEM_LIMIT),
    )(*ws, *recvs, *ms, *vs)
    return [outs[i * n:(i + 1) * n] for i in range(4)]


SMALL_NAMES = ("ffn1_norm", "mix_norm", "ffn2_norm", "b_gate", "na_q_norm", "na_k_norm", "sw_q_norm", "sw_k_norm",
               "na_rpb", "sw_sink", "t5_rel_table")


def kernel(x, ffn1_norm, ffn1_w_gate, ffn1_w_up, ffn1_w_down, mix_norm, w_in, b_gate, na_q_norm, na_k_norm, na_rpb, sw_q_norm, sw_k_norm, sw_sink, t5_rel_table, w_branch_na, w_branch_sw, w_out, ffn2_norm, ffn2_w_gate, ffn2_w_up, ffn2_w_down, loss_target, m_ffn1_norm, m_ffn1_w_gate, m_ffn1_w_up, m_ffn1_w_down, m_mix_norm, m_w_in, m_b_gate, m_na_q_norm, m_na_k_norm, m_na_rpb, m_sw_q_norm, m_sw_k_norm, m_sw_sink, m_t5_rel_table, m_w_branch_na, m_w_branch_sw, m_w_out, m_ffn2_norm, m_ffn2_w_gate, m_ffn2_w_up, m_ffn2_w_down, v_ffn1_norm, v_ffn1_w_gate, v_ffn1_w_up, v_ffn1_w_down, v_mix_norm, v_w_in, v_b_gate, v_na_q_norm, v_na_k_norm, v_na_rpb, v_sw_q_norm, v_sw_k_norm, v_sw_sink, v_t5_rel_table, v_w_branch_na, v_w_branch_sw, v_w_out, v_ffn2_norm, v_ffn2_w_gate, v_ffn2_w_up, v_ffn2_w_down):
    weights = dict(ffn1_norm=ffn1_norm, ffn1_w_gate=ffn1_w_gate, ffn1_w_up=ffn1_w_up, ffn1_w_down=ffn1_w_down,
                   mix_norm=mix_norm, w_in=w_in, b_gate=b_gate, na_q_norm=na_q_norm, na_k_norm=na_k_norm,
                   na_rpb=na_rpb, sw_q_norm=sw_q_norm, sw_k_norm=sw_k_norm, sw_sink=sw_sink,
                   t5_rel_table=t5_rel_table, w_branch_na=w_branch_na, w_branch_sw=w_branch_sw, w_out=w_out,
                   ffn2_norm=ffn2_norm, ffn2_w_gate=ffn2_w_gate, ffn2_w_up=ffn2_w_up, ffn2_w_down=ffn2_w_down)
    mom_m = dict(ffn1_norm=m_ffn1_norm, ffn1_w_gate=m_ffn1_w_gate, ffn1_w_up=m_ffn1_w_up, ffn1_w_down=m_ffn1_w_down,
                 mix_norm=m_mix_norm, w_in=m_w_in, b_gate=m_b_gate, na_q_norm=m_na_q_norm, na_k_norm=m_na_k_norm,
                 na_rpb=m_na_rpb, sw_q_norm=m_sw_q_norm, sw_k_norm=m_sw_k_norm, sw_sink=m_sw_sink,
                 t5_rel_table=m_t5_rel_table, w_branch_na=m_w_branch_na, w_branch_sw=m_w_branch_sw, w_out=m_w_out,
                 ffn2_norm=m_ffn2_norm, ffn2_w_gate=m_ffn2_w_gate, ffn2_w_up=m_ffn2_w_up, ffn2_w_down=m_ffn2_w_down)
    mom_v = dict(ffn1_norm=v_ffn1_norm, ffn1_w_gate=v_ffn1_w_gate, ffn1_w_up=v_ffn1_w_up, ffn1_w_down=v_ffn1_w_down,
                 mix_norm=v_mix_norm, w_in=v_w_in, b_gate=v_b_gate, na_q_norm=v_na_q_norm, na_k_norm=v_na_k_norm,
                 na_rpb=v_na_rpb, sw_q_norm=v_sw_q_norm, sw_k_norm=v_sw_k_norm, sw_sink=v_sw_sink,
                 t5_rel_table=v_t5_rel_table, w_branch_na=v_w_branch_na, w_branch_sw=v_w_branch_sw, w_out=v_w_out,
                 ffn2_norm=v_ffn2_norm, ffn2_w_gate=v_ffn2_w_gate, ffn2_w_up=v_ffn2_w_up, ffn2_w_down=v_ffn2_w_down)
    order = list(weights)

    depth = ffn1_norm.shape[0]
    s, d = x.shape[1], x.shape[2]
    xs = x[0]
    tr = lambda w: jnp.swapaxes(w, -1, -2)

    merge = lambda t: t.reshape(t.shape[0], N_DEV * t.shape[2], t.shape[3])
    no_dep = jnp.zeros((8, LANES), F32)

    def shards_of(kind, l):
        stack = lambda *ws: jnp.stack(ws).astype(BF16)
        if kind == "ffn1":
            return [stack(tr(ffn1_w_gate[l]), tr(ffn1_w_up[l]), ffn1_w_down[l])]
        if kind == "win":
            return [stack(tr(w_in[l]))]
        return [stack(tr(ffn2_w_gate[l]), tr(ffn2_w_up[l]), ffn2_w_down[l]), stack(w_out[l]),
                stack(tr(w_branch_na[l]), tr(w_branch_sw[l]))]

    def start(kind, l, after):
        return gather_start(shards_of(kind, l), after, f"gather_{kind}_{l}")

    def arrive(started, kind, l, after):
        zones = gather_wait(started, after, f"gather_{kind}_{l}_wait")
        return forward_start(zones, no_dep, f"forward_{kind}_{l}")

    def finish(fwd, kind, l, after):
        return [merge(z) for z in forward_wait(fwd, after, f"forward_{kind}_{l}_wait")]

    bd = jnp.asarray(np.kron(np.eye(MXU_TILE // HEAD_DIM), np.full((HEAD_DIM, HEAD_DIM), 1.0 / HEAD_DIM)), BF16)
    bmap = jnp.asarray(_t5_bucket_map())
    tile8 = lambda g: jnp.tile(g, NA_WIDTH // HEAD_DIM).reshape(1, NA_WIDTH)
    tile2 = lambda g: jnp.tile(g, SW_KV_WIDTH // HEAD_DIM).reshape(1, SW_KV_WIDTH)

    st_first, tok = start("ffn1", 0, no_dep)
    t5b = t5_expand(t5_rel_table, bmap, tok, "t5_expand").reshape(SW_STACK, 3 * SW_BLOCK)
    t2_tables = [rpb_expand(_rpb_rows(na_rpb[l]), tok, f"rpb_expand_{l}") for l in range(depth)]
    fwd, _ = arrive(st_first, "ffn1", 0, t2_tables[-1])
    st_win, dep = start("win", 0, t5b)
    (first,) = finish(fwd, "ffn1", 0, dep)

    saved = []
    layer_w = {0: dict(wg1=(first, 0), wu1=(first, 1), wd1=(first, 2))}
    cur = xs
    for l in range(depth):
        sv = {}
        lw = layer_w[l]
        sv["x0"] = cur
        cur, sv["xn1"], sv["hg1"], sv["hu1"], sv["act1"] = ffn_forward(
            cur, ffn1_norm[l][None], lw["wg1"], lw["wu1"], lw["wd1"], dep, f"ffn1_{l}")
        sv["x1"] = cur
        fwd, _ = arrive(st_win, "win", l, cur)
        st_rest, tok = start("rest", l, cur)
        (zb,) = finish(fwd, "win", l, tok)
        lw["win"] = (zb, 0)
        sv["gains"] = (tile8(na_q_norm[l]), tile8(na_k_norm[l]), tile8(sw_q_norm[l]), tile2(sw_k_norm[l]))
        sv["hn"], sv["zq"], sv["qa"], sv["ka"], sv["qs"], sv["ks"], sv["gt"] = mix_in(
            cur, mix_norm[l][None], lw["win"], b_gate[l][None], *sv["gains"], bd, f"mix_in_{l}")
        sv["t2"] = t2_tables[l]
        sv["o_na"] = na_fwd(sv["qa"], sv["ka"], sv["zq"], sv["t2"], f"na_fwd_{l}")
        dep = no_dep
        if l + 1 < depth:
            st_ffn1, dep = start("ffn1", l + 1, sv["o_na"])
        sv["o_sw"] = sw_fwd(sv["qs"], sv["ks"], sv["zq"], t5b, sw_sink[l], dep, f"sw_fwd_{l}")
        fwd, tok = arrive(st_rest, "rest", l, sv["o_sw"])
        za, zc, zd = finish(fwd, "rest", l, tok)
        lw.update(wg2=(za, 0), wu2=(za, 1), wd2=(za, 2), wout=(zc, 0), wna=(zd, 0), wsw=(zd, 1))
        cur, sv["a_na"], sv["a_sw"], sv["merged"] = merge_out(
            cur, sv["o_na"], sv["o_sw"], sv["gt"], lw["wna"], lw["wsw"], lw["wout"], f"merge_out_{l}")
        sv["x2"] = cur
        dep = no_dep
        if l + 1 < depth:
            st_win, dep = start("win", l + 1, cur)
        sv["xn2"], sv["hg2"], sv["hu2"], sv["act2"] = ffn_forward(
            cur, ffn2_norm[l][None], lw["wg2"], lw["wu2"], None, dep, f"ffn2_up_{l}")
        dep = no_dep
        if l + 1 < depth:
            fwd, dep = arrive(st_ffn1, "ffn1", l + 1, sv["act2"])
        if l + 1 < depth:
            cur = ffn_down(cur, sv["act2"], lw["wd2"], dep, f"ffn2_down_{l}")
            (za,) = finish(fwd, "ffn1", l + 1, cur)
            layer_w[l + 1] = dict(wg1=(za, 0), wu1=(za, 1), wd1=(za, 2))
        else:
            dx, loss_acc = ffn_down(cur, sv["act2"], lw["wd2"], dep, f"ffn2_down_{l}", target=loss_target[0])
        dep = no_dep
        saved.append(sv)

    loss = lax.psum(jnp.sum(loss_acc) * (0.5 / d), ("x", "y", "c"))

    split = lambda t: t.reshape(N_DEV, t.shape[0] // N_DEV, t.shape[1])
    pending = {}
    last_key = "ffn1_0"
    two_level = {last_key}
    small = {k: [None] * depth for k in SMALL_NAMES if k != "t5_rel_table"}
    dbias_sw = []
    for l in reversed(range(depth)):
        sv = saved[l]
        lw = layer_w[l]
        wg1, wu1, wd1, wg2, wu2, wd2 = (lw[k] for k in ("wg1", "wu1", "wd1", "wg2", "wu2", "wd2"))
        win_t, wout_l, wna_t, wsw_t = lw["win"], lw["wout"], lw["wna"], lw["wsw"]
        blocks = ((2, "x2", "xn2", "hg2", "hu2", "act2", wg2, wu2, wd2, "ffn2_norm", 3),
                  (1, "x0", "xn1", "hg1", "hu1", "act1", wg1, wu1, wd1, "ffn1_norm", 0))

        def ffn_backward(dx, blk):
            tag, xk, xnk, hgk, huk, actk, wg, wu, wd, norm_name, slot = blk
            gains = weights[norm_name]
            dxb, dhg, dhu = ffn_bwd_act(dx, wd, sv[hgk], sv[huk], f"ffn{tag}_bwd_act_{l}")
            gwg, gwu, gwd = tn_matmul([(dhg, sv[xnk], 1.0), (dhu, sv[xnk], 1.0), (sv[actk], dxb, 0.5)],
                                      f"ffn{tag}_dw_{l}")
            key = f"ffn{tag}_{l}"
            blocks_of = [split(gwg), split(gwu), split(gwd)]
            if key in two_level:
                paired, token = pair_start(blocks_of, dxb, f"pair_{key}")
            else:
                pending[key], token = scatter_start([blocks_of], f"scatter_{key}")
            dx, dg = proj_bwd_norm([dhg, dhu], [wg, wu], sv[xk], gains[l][None], dx, token, f"ffn{tag}_bwd_x_{l}")
            token = no_dep
            if key in two_level:
                thru, land = pair_wait(paired, dx, f"pair_{key}_wait")
                pending[key], token = chip_start(pair_sum(thru, land, f"pair_sum_{key}"), dg, f"chips_{key}")
            small[norm_name][l] = dg[0]
            return dx, token

        dx, token = ffn_backward(dx, blocks[0])
        dxb, dzg, da_na, da_sw, do_na, do_sw, dbg = mix_bwd_out(
            dx, sv["gt"], sv["a_na"], sv["a_sw"], wna_t, wsw_t, wout_l, token, f"mix_bwd_out_{l}")
        small["b_gate"][l] = dbg[0]
        gwout, gwna, gwsw = tn_matmul([(sv["merged"], dxb, 1.0), (da_na, sv["o_na"], 1.0), (da_sw, sv["o_sw"], 1.0)],
                                      f"mix_dw_{l}")
        dqa, dka, dva, dt2 = na_bwd(sv["qa"], sv["ka"], sv["zq"], sv["t2"], sv["o_na"], do_na, f"na_bwd_{l}")
        dqs, dks, dvs, dbias, dsink = sw_bwd(sv["qs"], sv["ks"], sv["zq"], t5b, sw_sink[l], sv["o_sw"], do_sw,
                                             f"sw_bwd_{l}")
        dbias_sw.append(dbias.reshape(SW_HEADS, SW_BLOCK, 3 * SW_BLOCK))
        small["sw_sink"][l] = jnp.sum(dsink[:, 0].reshape(SW_HEADS, SW_BLOCK), axis=1)
        small["na_rpb"][l] = _rpb_from_rows(rpb_reduce(dt2, f"rpb_reduce_{l}"))
        dz, dgqa, dgka, dgqs, dgks = qk_norm_bwd(dqa, dka, dva, dqs, dks, dvs, sv["zq"], dzg, *sv["gains"], bd,
                                                 f"qk_norm_bwd_{l}")
        fold = lambda g: jnp.sum(g.reshape(-1, HEAD_DIM), axis=0)
        small["na_q_norm"][l], small["na_k_norm"][l] = fold(dgqa), fold(dgka)
        small["sw_q_norm"][l], small["sw_k_norm"][l] = fold(dgqs), fold(dgks)
        (gwin,) = tn_matmul([(dz, sv["hn"], 1.0)], f"dwin_{l}")
        pending[f"mix_{l}"], token = scatter_start([[split(gwout)], [split(gwna), split(gwsw)], [split(gwin)]],
                                                   f"scatter_mix_{l}")
        dx, dg = proj_bwd_norm([dz], [win_t], sv["x1"], mix_norm[l][None], dx, token, f"mix_bwd_x_{l}")
        small["mix_norm"][l] = dg[0]
        dx, tail = ffn_backward(dx, blocks[1])

    dtab = t5_reduce(dbias_sw, bmap, "t5_reduce")
    small_parts = {k: jnp.stack(v) for k, v in small.items()}
    small_parts["t5_rel_table"] = jnp.transpose(dtab[:, :, 0])

    grads, delta, new_m, new_v = {}, {}, {}, {}
    state = {}
    chain = [tail]
    members = {"ffn": lambda t: [(f"ffn{t}_w_gate", 0, 0, True), (f"ffn{t}_w_up", 0, 1, True),
                                 (f"ffn{t}_w_down", 0, 2, False)],
               "mix": lambda t: [("w_out", 0, 0, False), ("w_branch_na", 1, 0, True), ("w_branch_sw", 1, 1, True),
                                 ("w_in", 2, 0, True)]}

    def collect(key):
        if key in two_level:
            zones = [chip_wait(pending[key], chain[0], f"wait_{key}")]
        else:
            zones = scatter_wait(pending[key], chain[0], f"wait_{key}")
        kind, l = key.split("_")
        for k, zi, wi, transposed in members[kind[:3]](kind[3:]):
            view = tr if transposed else (lambda t: t)
            state[k] = adamw_layer(zones[zi], wi, int(l), view(weights[k]), view(mom_m[k]), view(mom_v[k]),
                                   state.get(k), chain[0], f"adamw_{k}_{l}")
            chain[0] = state[k][1]
            if all(f"{kind}_{j}" in done for j in range(depth) if j != int(l)):
                grads[k], delta[k], new_m[k], new_v[k] = (view(t) for t in state[k])
        done.add(key)

    done = set()
    for key in pending:
        if key != last_key:
            collect(key)
    collect(last_key)
    recvs = share_small([small_parts[k] for k in SMALL_NAMES], chain[0])
    results = adamw_small([weights[k] for k in SMALL_NAMES], recvs, [mom_m[k] for k in SMALL_NAMES],
                          [mom_v[k] for k in SMALL_NAMES], "adamw_small")
    for dst, outs in zip((grads, delta, new_m, new_v), results):
        dst.update(dict(zip(SMALL_NAMES, outs)))

    return (loss, dx[None], *[grads[k] for k in order], *[delta[k] for k in order],
            *[new_m[k] for k in order], *[new_v[k] for k in order])
```

```python
import functools
import math

import numpy as np
import jax
import jax.numpy as jnp
from jax import lax
from jax.experimental import pallas as pl
from jax.experimental.pallas import tpu as pltpu

F32 = jnp.float32
BF16 = jnp.bfloat16
MESH = pl.DeviceIdType.MESH

N_DEV = 8
EPS = 1e-6
NEG = -1e30
HEAD_DIM = 64
GRID_W = 64
NA_ROWS = 8
NA_COLS = 16
NA_WIDTH = 512
SW_Q_WIDTH = 512
SW_KV_WIDTH = 128
SW_BLOCK = 128
SW_HEADS = 8
SW_REP = 4
REL_BUCKETS = 32
REL_MAX_DIST = 128
QKV_WIDTH = 3 * NA_WIDTH + SW_Q_WIDTH + 2 * SW_KV_WIDTH
SCALE = 1.0 / math.sqrt(HEAD_DIM)

ADAM_LR = 0.001
ADAM_B1 = 0.9
ADAM_B2 = 0.999
ADAM_EPS = 1e-08
ADAM_WD = 0.01
ADAM_STEP = 10

V7X_VMEM_LIMIT = 56 * 1024 * 1024
LANES = 128
MXU_TILE = 256

NT = (((1,), (1,)), ((), ()))
TN = (((0,), (0,)), ((), ()))


def _params(n_grid=1):
    return pltpu.CompilerParams(dimension_semantics=("arbitrary",) * n_grid,
                                vmem_limit_bytes=V7X_VMEM_LIMIT)


def _row_tile(s):
    for t in (512, 256, 128, 64, 32, 16, 8):
        if s % t == 0:
            return t
    raise ValueError(s)


def _tn_tile(n):
    best = max(t for t in range(LANES, min(n, 2304) + 1, LANES) if n % t == 0) if n % LANES == 0 else n
    return best // 2 if best == n and n >= 1024 else best


ONCE = pl.Buffered(1)


def _col_chunk(n):
    return MXU_TILE if n % MXU_TILE == 0 else n


def _dot(a, b):
    return jnp.dot(a, b, preferred_element_type=F32)


def _dotg(a, b, dn):
    return lax.dot_general(a, b, dn, preferred_element_type=F32)


def _sigmoid(v):
    return 1.0 / (1.0 + jnp.exp(-v))


def _rstd(xv):
    return lax.rsqrt(jnp.mean(xv * xv, axis=-1, keepdims=True) + EPS)


def _full(shape):
    nd = len(shape)
    return pl.BlockSpec(shape, lambda i, _n=nd: (0,) * _n)


def _rows(tm, width):
    return pl.BlockSpec((tm, width), lambda i: (i, 0))


def _mat(stack, idx):
    return pl.BlockSpec((None,) + tuple(stack.shape[1:]), lambda i, _w=idx: (_w, 0, 0), pipeline_mode=ONCE)


def _group_mean(v, bd):
    w = bd.shape[0]
    if v.shape[1] > w:
        return jnp.concatenate([_group_mean(v[:, c0:c0 + w], bd) for c0 in range(0, v.shape[1], w)], axis=1)
    hi = v.astype(BF16)
    lo = (v - hi.astype(F32)).astype(BF16)
    return _dot(hi, bd) + _dot(lo, bd)


def ffn_forward(x, gain, wg_t, wu_t, wd, dep, name):
    s, d = x.shape
    f = wg_t[0].shape[1]
    tm = _row_tile(s) if wd is None else min(_row_tile(s), 256)
    fc = _col_chunk(f)
    nw = 2 if wd is None else 3

    def body(x_ref, g_ref, *refs):
        w_refs, outs = refs[:nw], refs[nw + 1:]
        xn_ref, dg_ref, du_ref, act_ref = outs[-4:]
        xv = x_ref[...]
        xn = (xv * _rstd(xv) * g_ref[...]).astype(BF16)
        xn_ref[...] = xn
        for c0 in range(0, f, fc):
            hg = _dotg(xn, w_refs[0][c0:c0 + fc, :], NT)
            hu = _dotg(xn, w_refs[1][c0:c0 + fc, :], NT)
            sg = _sigmoid(hg)
            silu = hg * sg
            du_ref[:, c0:c0 + fc] = silu.astype(BF16)
            dg_ref[:, c0:c0 + fc] = (hu * (sg + silu * (1.0 - sg))).astype(BF16)
            act_ref[:, c0:c0 + fc] = (silu * hu).astype(BF16)
        if wd is not None:
            outs[0][...] = xv + 0.5 * _dot(act_ref[...], w_refs[2][...])

    weights = [wg_t, wu_t] + ([] if wd is None else [wd])
    out_specs = [_rows(tm, d), _rows(tm, f), _rows(tm, f), _rows(tm, f)]
    out_shape = [jax.ShapeDtypeStruct((s, d), BF16)] + [jax.ShapeDtypeStruct((s, f), BF16)] * 3
    if wd is not None:
        out_specs, out_shape = [_rows(tm, d)] + out_specs, [jax.ShapeDtypeStruct((s, d), F32)] + out_shape
    return pl.pallas_call(
        body, name=name, grid=(s // tm,),
        in_specs=[_rows(tm, d), _full((1, d))] + [_mat(*w) for w in weights] + [_full(dep.shape)],
        out_specs=out_specs, out_shape=out_shape,
        compiler_params=_params(),
    )(x, gain, *[w[0] for w in weights], dep)


def ffn_down(x, act, wd, dep, name, target=None):
    s, d = x.shape
    f = act.shape[1]
    tm = _row_tile(s)

    def body(x_ref, a_ref, w_ref, dep_ref, *rest):
        y = x_ref[...] + 0.5 * _dot(a_ref[...], w_ref[...])
        if target is None:
            rest[0][...] = y
            return
        t_ref, dy_ref, acc_ref = rest

        @pl.when(pl.program_id(0) == 0)
        def _():
            acc_ref[...] = jnp.zeros(acc_ref.shape, F32)

        err = y - t_ref[...]
        dy_ref[...] = err * (1.0 / d)
        part = jnp.sum((err * err).reshape(tm // 8, 8, d), axis=0)
        acc = part[:, 0:LANES]
        for c0 in range(LANES, d, LANES):
            acc = acc + part[:, c0:c0 + LANES]
        acc_ref[...] = acc_ref[...] + acc

    ins = [_rows(tm, d), _rows(tm, f), _mat(*wd), _full(dep.shape)]
    if target is None:
        return pl.pallas_call(
            body, name=name, grid=(s // tm,), in_specs=ins, out_specs=_rows(tm, d),
            out_shape=jax.ShapeDtypeStruct((s, d), F32), compiler_params=_params(),
        )(x, act, wd[0], dep)
    return pl.pallas_call(
        body, name=name, grid=(s // tm,), in_specs=ins + [_rows(tm, d)],
        out_specs=[_rows(tm, d), _full((8, LANES))],
        out_shape=[jax.ShapeDtypeStruct((s, d), F32), jax.ShapeDtypeStruct((8, LANES), F32)],
        compiler_params=_params(),
    )(x, act, wd[0], dep, target)


def mix_in(x, gain, win_t, b_gate, gq_na, gk_na, gq_sw, gk_sw, bd, name):
    s, d = x.shape
    tm = _row_tile(s)
    gc = _col_chunk(2 * d)

    def body(x_ref, g_ref, w_ref, b_ref, gqa_ref, gka_ref, gqs_ref, gks_ref, bd_ref,
             hn_ref, zq_ref, qa_ref, ka_ref, qs_ref, ks_ref, gt_ref):
        xv = x_ref[...]
        hn = (xv * _rstd(xv) * g_ref[...]).astype(BF16)
        hn_ref[...] = hn

        def proj(c0, c1):
            return _dotg(hn, w_ref[c0:c1, :], NT)

        def headnorm(z, g, bdm):
            return z * lax.rsqrt(_group_mean(z * z, bdm) + EPS) * g

        bd512 = bd_ref[...]
        bd128 = bd_ref[0:SW_KV_WIDTH, 0:SW_KV_WIDTH]
        z = proj(0, 512)
        zq_ref[:, 0:512] = z.astype(BF16)
        qa_ref[...] = (headnorm(z, gqa_ref[...], bd512) * SCALE).astype(BF16)
        z = proj(512, 1024)
        zq_ref[:, 512:1024] = z.astype(BF16)
        ka_ref[...] = headnorm(z, gka_ref[...], bd512).astype(BF16)
        z = proj(1024, 1536)
        zq_ref[:, 1024:1536] = z.astype(BF16)
        z = proj(1536, 2048)
        zq_ref[:, 1536:2048] = z.astype(BF16)
        qs_ref[...] = (headnorm(z, gqs_ref[...], bd512) * SCALE).astype(BF16)
        z = proj(2048, 2176)
        zq_ref[:, 2048:2176] = z.astype(BF16)
        ks_ref[...] = headnorm(z, gks_ref[...], bd128).astype(BF16)
        z = proj(2176, 2304)
        zq_ref[:, 2176:2304] = z.astype(BF16)
        for c0 in range(0, 2 * d, gc):
            zg = proj(QKV_WIDTH + c0, QKV_WIDTH + c0 + gc) + b_ref[:, c0:c0 + gc]
            gt_ref[:, c0:c0 + gc] = _sigmoid(zg).astype(BF16)

    return pl.pallas_call(
        body, name=name, grid=(s // tm,),
        in_specs=[_rows(tm, d), _full((1, d)), _mat(*win_t), _full((1, 2 * d)),
                  _full((1, 512)), _full((1, 512)), _full((1, 512)), _full((1, 128)), _full((MXU_TILE, MXU_TILE))],
        out_specs=[_rows(tm, d), _rows(tm, QKV_WIDTH), _rows(tm, 512), _rows(tm, 512), _rows(tm, 512),
                   _rows(tm, 128), _rows(tm, 2 * d)],
        out_shape=[jax.ShapeDtypeStruct((s, d), BF16), jax.ShapeDtypeStruct((s, QKV_WIDTH), BF16),
                   jax.ShapeDtypeStruct((s, 512), BF16), jax.ShapeDtypeStruct((s, 512), BF16),
                   jax.ShapeDtypeStruct((s, 512), BF16), jax.ShapeDtypeStruct((s, 128), BF16),
                   jax.ShapeDtypeStruct((s, 2 * d), BF16)],
        compiler_params=_params(),
    )(x, gain, win_t[0], b_gate, gq_na, gk_na, gq_sw, gk_sw, bd)


def _na_iotas():
    qc = lax.broadcasted_iota(jnp.int32, (GRID_W, LANES), 0)
    ln = lax.broadcasted_iota(jnp.int32, (GRID_W, LANES), 1)
    low = ln < GRID_W
    kc = jnp.where(low, ln, ln - GRID_W)
    diff = kc - qc + (NA_COLS - 1)
    qcs = jnp.clip(qc - NA_COLS // 2, 0, GRID_W - NA_COLS)
    inwin = (kc >= qcs) & (kc < qcs + NA_COLS)
    return diff, low, inwin


NA_RI = 2 * NA_ROWS - 1
NA_CI = 2 * NA_COLS - 1
NA_T2 = NA_RI + 1


def _rpb_rows(rpb):
    h = rpb.shape[0]
    padded = jnp.pad(rpb, ((0, 0), (1, 1), (0, GRID_W - NA_CI)))
    return jnp.concatenate([padded[:, :NA_T2], padded[:, 1:NA_T2 + 1]], axis=2).reshape(h, NA_T2, LANES)


def _rpb_from_rows(rows):
    return rows[:, 1:, :NA_CI] + rows[:, :NA_RI, GRID_W:GRID_W + NA_CI]


def rpb_expand(rows, dep, name):
    n_heads = rows.shape[0]

    def body(r_ref, dep_ref, o_ref):
        for h in range(n_heads):
            for e in range(NA_T2):
                line = jnp.broadcast_to(r_ref[h, e:e + 1, :], (GRID_W, LANES))
                o_ref[h, e] = pltpu.roll(line, LANES - (NA_COLS - 1), 1, stride=1, stride_axis=0)

    return pl.pallas_call(
        body, name=name,
        in_specs=[pl.BlockSpec(memory_space=pltpu.VMEM), pl.BlockSpec(memory_space=pltpu.VMEM)],
        out_specs=pl.BlockSpec(memory_space=pltpu.VMEM),
        out_shape=jax.ShapeDtypeStruct((n_heads, NA_T2, GRID_W, LANES), F32),
        compiler_params=pltpu.CompilerParams(vmem_limit_bytes=V7X_VMEM_LIMIT),
    )(rows, dep)


def rpb_reduce(dt2, name):
    n_heads = dt2.shape[0]
    flip = jnp.asarray(np.eye(GRID_W)[::-1], BF16)

    def body(d_ref, j_ref, o_ref):
        jm = j_ref[...]
        for h in range(n_heads):
            for e in range(NA_T2):
                dv = d_ref[h, e]
                hi = dv.astype(BF16)
                mid = (dv - hi.astype(F32)).astype(BF16)
                lo = (dv - hi.astype(F32) - mid.astype(F32)).astype(BF16)
                rev = _dot(jm, hi) + _dot(jm, mid) + _dot(jm, lo)
                back = pltpu.roll(rev, LANES + (NA_COLS - 1) - (GRID_W - 1), 1, stride=1, stride_axis=0)
                o_ref[h, e:e + 1, :] = jnp.sum(back, axis=0, keepdims=True)

    return pl.pallas_call(
        body, name=name,
        in_specs=[pl.BlockSpec(memory_space=pltpu.VMEM)] * 2,
        out_specs=pl.BlockSpec(memory_space=pltpu.VMEM),
        out_shape=jax.ShapeDtypeStruct((n_heads, NA_T2, LANES), F32),
        compiler_params=pltpu.CompilerParams(vmem_limit_bytes=V7X_VMEM_LIMIT),
    )(dt2, flip)


NA_TQ = 4
NA_TK = NA_TQ + NA_ROWS
NA_KCH = NA_TK // 2


def _na_tile_geometry(t, rows):
    r = t * NA_TQ
    kbase = jnp.clip(r - NA_ROWS // 2, 0, rows - NA_TK)
    starts = [jnp.clip(r + a - NA_ROWS // 2, 0, rows - NA_ROWS) for a in range(NA_TQ)]
    return r, kbase, starts


def _na_tile_mask(kbase, starts, low, inwin):
    half = jnp.where(low, 0, 1)
    cols = []
    for c in range(NA_KCH):
        krow = kbase + 2 * c + half
        cols.append(jnp.concatenate(
            [jnp.where(inwin & (krow >= st) & (krow < st + NA_ROWS), 0.0, NEG) for st in starts], axis=0))
    return jnp.concatenate(cols, axis=1)


def _na_tile_index(r, kbase, a, c):
    return jnp.clip(kbase + 2 * c - (r + a) + NA_ROWS, 0, NA_T2 - 1)


def _na_tile_scores(q, k, t2_ref, hh, r, kbase, madd):
    bias = jnp.concatenate(
        [jnp.concatenate([t2_ref[hh, _na_tile_index(r, kbase, a, c)] for a in range(NA_TQ)], axis=0)
         for c in range(NA_KCH)], axis=1)
    return _dotg(q, k, NT) + bias + madd


def _softmax_rows(sc):
    e = jnp.exp(sc - jnp.max(sc, axis=1, keepdims=True))
    return e * (1.0 / jnp.sum(e, axis=1, keepdims=True))


def na_fwd(qa, ka, zq, t2, name):
    s = qa.shape[0]
    rows = s // GRID_W
    n_pairs = NA_WIDTH // LANES
    v_blk0 = (2 * NA_WIDTH) // LANES

    assert rows % NA_TQ == 0 and rows >= NA_TK
    tq, tk = NA_TQ * GRID_W, NA_TK * GRID_W

    def body(q_ref, k_ref, v_ref, t2_ref, o_ref, s_scr, p_scr):
        _, low, inwin = _na_iotas()

        def tile(t, carry):
            r, kbase, starts = _na_tile_geometry(t, rows)
            madd = _na_tile_mask(kbase, starts, low, inwin)
            qr = pl.ds(pl.multiple_of(r * GRID_W, tq), tq)
            kr = pl.ds(pl.multiple_of(kbase * GRID_W, tq), tk)
            for hh in range(2):
                lanes = slice(HEAD_DIM * hh, HEAD_DIM * (hh + 1))
                s_scr[tq * hh:tq * (hh + 1), :] = _na_tile_scores(q_ref[qr, lanes], k_ref[kr, lanes], t2_ref, hh, r,
                                                                  kbase, madd)
            sc = s_scr[...]
            p_scr[...] = jnp.exp((sc - jnp.max(sc, axis=1, keepdims=True)).astype(BF16))
            ones = jnp.ones((tk, LANES), BF16)
            for hh in range(2):
                lanes = slice(HEAD_DIM * hh, HEAD_DIM * (hh + 1))
                e = p_scr[tq * hh:tq * (hh + 1), :]
                inv = 1.0 / _dot(e, ones)[:, 0:1]
                o_ref[qr, lanes] = (_dot(e, v_ref[kr, lanes]) * inv).astype(BF16)
            return carry

        lax.fori_loop(0, rows // NA_TQ, tile, 0)

    col = lambda off: pl.BlockSpec((s, LANES), lambda p, _o=off: (0, _o + p))
    return pl.pallas_call(
        body, name=name, grid=(n_pairs,),
        in_specs=[col(0), col(0), col(v_blk0),
                  pl.BlockSpec((2, NA_T2, GRID_W, LANES), lambda p: (p, 0, 0, 0))],
        out_specs=col(0),
        out_shape=jax.ShapeDtypeStruct((s, NA_WIDTH), BF16),
        scratch_shapes=[pltpu.VMEM((2 * tq, tk), F32), pltpu.VMEM((2 * tq, tk), BF16)],
        compiler_params=_params(),
    )(qa, ka, zq, t2)


def na_bwd(qa, ka, zq, t2, o_na, do_na, name):
    s = qa.shape[0]
    rows = s // GRID_W
    n_pairs = NA_WIDTH // LANES
    v_blk0 = (2 * NA_WIDTH) // LANES

    tq, tk = NA_TQ * GRID_W, NA_TK * GRID_W

    def body(q_ref, k_ref, v_ref, t2_ref, o_ref, do_ref, dq_ref, dk_ref, dv_ref, dt2_ref):
        _, low, inwin = _na_iotas()
        dk_ref[...] = jnp.zeros(dk_ref.shape, F32)
        dv_ref[...] = jnp.zeros(dv_ref.shape, F32)
        dt2_ref[...] = jnp.zeros(dt2_ref.shape, F32)

        def tile(t, carry):
            r, kbase, starts = _na_tile_geometry(t, rows)
            madd = _na_tile_mask(kbase, starts, low, inwin)
            qr = pl.ds(pl.multiple_of(r * GRID_W, tq), tq)
            kr = pl.ds(pl.multiple_of(kbase * GRID_W, tq), tk)
            for hh in range(2):
                lanes = slice(HEAD_DIM * hh, HEAD_DIM * (hh + 1))
                q, k, v = q_ref[qr, lanes], k_ref[kr, lanes], v_ref[kr, lanes]
                p = _softmax_rows(_na_tile_scores(q, k, t2_ref, hh, r, kbase, madd))
                do = do_ref[qr, lanes]
                delta = jnp.sum(do.astype(F32) * o_ref[qr, lanes].astype(F32), axis=1, keepdims=True)
                ds = p * (_dotg(do, v, NT) - delta)
                shared = {}
                for a in range(NA_TQ):
                    for c in range(NA_KCH):
                        shared.setdefault(2 * c - a, []).append(
                            ds[GRID_W * a:GRID_W * (a + 1), LANES * c:LANES * (c + 1)])
                for offset, parts in shared.items():
                    e = jnp.clip(offset + kbase - r + NA_ROWS, 0, NA_T2 - 1)
                    dt2_ref[hh, e] = dt2_ref[hh, e] + functools.reduce(jnp.add, parts)
                dsb = ds.astype(BF16)
                dq_ref[qr, lanes] = _dot(dsb, k)
                dk_ref[kr, lanes] = dk_ref[kr, lanes] + _dotg(dsb, q, TN)
                dv_ref[kr, lanes] = dv_ref[kr, lanes] + _dotg(p.astype(BF16), do, TN)
            return carry

        lax.fori_loop(0, rows // NA_TQ, tile, 0)

    col = lambda off: pl.BlockSpec((s, LANES), lambda p, _o=off: (0, _o + p))
    t2spec = pl.BlockSpec((2, NA_T2, GRID_W, LANES), lambda p: (p, 0, 0, 0))
    return pl.pallas_call(
        body, name=name, grid=(n_pairs,),
        in_specs=[col(0), col(0), col(v_blk0), t2spec, col(0), col(0)],
        out_specs=[col(0), col(0), col(0), t2spec],
        out_shape=[jax.ShapeDtypeStruct((s, NA_WIDTH), F32)] * 3 + [jax.ShapeDtypeStruct(t2.shape, F32)],
        compiler_params=_params(),
    )(qa, ka, zq, t2, o_na, do_na)


def _t5_bucket_map():
    rel = np.arange(3 * SW_BLOCK)[None, :] - SW_BLOCK - np.arange(SW_BLOCK)[:, None]
    nb = REL_BUCKETS // 2
    max_exact = nb // 2
    n = np.abs(rel)
    large = max_exact + (np.log(np.maximum(n, 1) / max_exact)
                         / np.log(REL_MAX_DIST / max_exact) * (nb - max_exact)).astype(np.int32)
    large = np.minimum(large, nb - 1)
    return ((rel > 0) * nb + np.where(n < max_exact, n, large)).astype(np.int32)


def t5_expand(table, bmap, dep, name):
    def body(tab_ref, bm_ref, dep_ref, o_ref):
        bm = bm_ref[...]
        for h in range(SW_HEADS):
            t = jnp.zeros(bm.shape, F32)
            for b in range(REL_BUCKETS):
                t = jnp.where(bm == b, tab_ref[b, h], t)
            o_ref[h] = t

    return pl.pallas_call(
        body, name=name,
        in_specs=[pl.BlockSpec(memory_space=pltpu.SMEM), pl.BlockSpec(memory_space=pltpu.VMEM),
                  pl.BlockSpec(memory_space=pltpu.VMEM)],
        out_specs=pl.BlockSpec(memory_space=pltpu.VMEM),
        out_shape=jax.ShapeDtypeStruct((SW_HEADS,) + bmap.shape, F32),
        compiler_params=pltpu.CompilerParams(vmem_limit_bytes=V7X_VMEM_LIMIT),
    )(table, bmap, dep)


def t5_reduce(dbias_list, bmap, name):
    n = len(dbias_list)

    def body(*refs):
        d_refs, bm_ref, o_ref = refs[:n], refs[n], refs[n + 1]
        bm = bm_ref[...]
        for h in range(SW_HEADS):
            dv = d_refs[0][h]
            for other in d_refs[1:]:
                dv = dv + other[h]
            rows = [jnp.sum(jnp.where(bm == b, dv, 0.0), axis=0, keepdims=True) for b in range(REL_BUCKETS)]
            r = jnp.concatenate(rows, axis=0)
            o_ref[h] = jnp.broadcast_to(jnp.sum(r, axis=1, keepdims=True), (REL_BUCKETS, LANES))

    return pl.pallas_call(
        body, name=name,
        in_specs=[pl.BlockSpec(memory_space=pltpu.VMEM)] * (n + 1),
        out_specs=pl.BlockSpec(memory_space=pltpu.VMEM),
        out_shape=jax.ShapeDtypeStruct((SW_HEADS, REL_BUCKETS, LANES), F32),
        compiler_params=pltpu.CompilerParams(vmem_limit_bytes=V7X_VMEM_LIMIT),
    )(*dbias_list, bmap)


def _sw_mask_iotas():
    a = lax.broadcasted_iota(jnp.int32, (SW_BLOCK, 3 * SW_BLOCK), 0)
    j = lax.broadcasted_iota(jnp.int32, (SW_BLOCK, 3 * SW_BLOCK), 1)
    inwin = jnp.abs(j - SW_BLOCK - a) <= SW_BLOCK
    return j, inwin


SW_STACK = SW_HEADS * SW_BLOCK


def _sw_softmax(sc, sk):
    m = jnp.maximum(jnp.max(sc, axis=1, keepdims=True), sk)
    e = jnp.exp(sc - m)
    es = jnp.exp(sk - m)
    inv = 1.0 / (jnp.sum(e, axis=1, keepdims=True) + es)
    return e * inv, es * inv


def _sw_prologue(k_ref, v_ref, kp, vp, sink_ref, s):
    pad = s + 2 * SW_BLOCK
    zeros = jnp.zeros((SW_BLOCK, SW_KV_WIDTH), BF16)
    kp[0:SW_BLOCK, :] = zeros
    vp[0:SW_BLOCK, :] = zeros
    kp[SW_BLOCK + s:pad, :] = zeros
    vp[SW_BLOCK + s:pad, :] = zeros
    kp[SW_BLOCK:SW_BLOCK + s, :] = k_ref[...]
    vp[SW_BLOCK:SW_BLOCK + s, :] = v_ref[...]
    return jnp.concatenate([jnp.full((SW_BLOCK, 1), sink_ref[h], F32) for h in range(SW_HEADS)], axis=0)


def sw_fwd(qs, ks, zq, t5b, sink, dep, name):
    s = qs.shape[0]
    nb = s // SW_BLOCK
    v_blk = (3 * NA_WIDTH + SW_Q_WIDTH + SW_KV_WIDTH) // LANES
    pad = s + 2 * SW_BLOCK

    def body(q_ref, k_ref, v_ref, b_ref, sink_ref, dep_ref, o_ref, kp, vp, s_scr, p_scr):
        sink_col = _sw_prologue(k_ref, v_ref, kp, vp, sink_ref, s)
        j, inwin = _sw_mask_iotas()

        def blk(n, carry):
            kpos = n * SW_BLOCK - SW_BLOCK + j
            madd = jnp.where(inwin & (kpos >= 0) & (kpos < s), 0.0, NEG)
            q0 = pl.multiple_of(n * SW_BLOCK, SW_BLOCK)
            qr, kr = pl.ds(q0, SW_BLOCK), pl.ds(q0, 3 * SW_BLOCK)
            for h in range(SW_HEADS):
                g = h // SW_REP
                s_scr[SW_BLOCK * h:SW_BLOCK * (h + 1), :] = _dotg(
                    q_ref[qr, HEAD_DIM * h:HEAD_DIM * (h + 1)], kp[kr, HEAD_DIM * g:HEAD_DIM * (g + 1)], NT) + madd
            sc = s_scr[...] + b_ref[...]
            m = jnp.maximum(jnp.max(sc, axis=1, keepdims=True), sink_col)
            p_scr[...] = jnp.exp((sc - m).astype(BF16))
            sink_term = jnp.exp(sink_col - m)
            ones = jnp.ones((3 * SW_BLOCK, LANES), BF16)
            for h in range(SW_HEADS):
                g = h // SW_REP
                rows = slice(SW_BLOCK * h, SW_BLOCK * (h + 1))
                e = p_scr[rows, :]
                inv = 1.0 / (_dot(e, ones)[:, 0:1] + sink_term[rows, :])
                o_ref[qr, HEAD_DIM * h:HEAD_DIM * (h + 1)] = (
                    _dot(e, vp[kr, HEAD_DIM * g:HEAD_DIM * (g + 1)]) * inv).astype(BF16)
            return carry

        lax.fori_loop(0, nb, blk, 0)

    return pl.pallas_call(
        body, name=name, grid=(1,),
        in_specs=[_full((s, SW_Q_WIDTH)), _full((s, SW_KV_WIDTH)),
                  pl.BlockSpec((s, SW_KV_WIDTH), lambda i: (0, v_blk)),
                  _full((SW_STACK, 3 * SW_BLOCK)), pl.BlockSpec(memory_space=pltpu.SMEM),
                  _full(dep.shape)],
        out_specs=_full((s, SW_Q_WIDTH)),
        out_shape=jax.ShapeDtypeStruct((s, SW_Q_WIDTH), BF16),
        scratch_shapes=[pltpu.VMEM((pad, SW_KV_WIDTH), BF16), pltpu.VMEM((pad, SW_KV_WIDTH), BF16),
                        pltpu.VMEM((SW_STACK, 3 * SW_BLOCK), F32), pltpu.VMEM((SW_STACK, 3 * SW_BLOCK), BF16)],
        compiler_params=_params(),
    )(qs, ks, zq, t5b, sink, dep)


def sw_bwd(qs, ks, zq, t5b, sink, o_sw, do_sw, name):
    s = qs.shape[0]
    nb = s // SW_BLOCK
    v_blk = (3 * NA_WIDTH + SW_Q_WIDTH + SW_KV_WIDTH) // LANES
    pad = s + 2 * SW_BLOCK

    def body(q_ref, k_ref, v_ref, b_ref, sink_ref, o_ref, do_ref,
             dq_ref, dk_ref, dv_ref, db_ref, dsk_ref, kp, vp, dkp, dvp, s_scr, dp_scr, ds_scr, p_scr):
        sink_col = _sw_prologue(k_ref, v_ref, kp, vp, sink_ref, s)
        dkp[...] = jnp.zeros(dkp.shape, F32)
        dvp[...] = jnp.zeros(dvp.shape, F32)
        db_ref[...] = jnp.zeros(db_ref.shape, F32)
        dsk_ref[...] = jnp.zeros(dsk_ref.shape, F32)
        j, inwin = _sw_mask_iotas()

        def blk(n, carry):
            kpos = n * SW_BLOCK - SW_BLOCK + j
            madd = jnp.where(inwin & (kpos >= 0) & (kpos < s), 0.0, NEG)
            q0 = pl.multiple_of(n * SW_BLOCK, SW_BLOCK)
            qr, kr = pl.ds(q0, SW_BLOCK), pl.ds(q0, 3 * SW_BLOCK)
            deltas = []
            for h in range(SW_HEADS):
                g = h // SW_REP
                hl, kl = slice(HEAD_DIM * h, HEAD_DIM * (h + 1)), slice(HEAD_DIM * g, HEAD_DIM * (g + 1))
                rows = slice(SW_BLOCK * h, SW_BLOCK * (h + 1))
                do = do_ref[qr, hl]
                s_scr[rows, :] = _dotg(q_ref[qr, hl], kp[kr, kl], NT) + madd
                dp_scr[rows, :] = _dotg(do, vp[kr, kl], NT)
                deltas.append(jnp.sum(do.astype(F32) * o_ref[qr, hl].astype(F32), axis=1, keepdims=True))
            delta = jnp.concatenate(deltas, axis=0)
            p, ps = _sw_softmax(s_scr[...] + b_ref[...], sink_col)
            ds = p * (dp_scr[...] - delta)
            db_ref[...] = db_ref[...] + ds
            dsk_ref[...] = dsk_ref[...] - jnp.broadcast_to(ps * delta, (SW_STACK, LANES))
            ds_scr[...] = ds.astype(BF16)
            p_scr[...] = p.astype(BF16)
            for g in range(SW_HEADS // SW_REP):
                kl = slice(HEAD_DIM * g, HEAD_DIM * (g + 1))
                k = kp[kr, kl]
                dkw = jnp.zeros((3 * SW_BLOCK, HEAD_DIM), F32)
                dvw = jnp.zeros((3 * SW_BLOCK, HEAD_DIM), F32)
                for r in range(SW_REP):
                    h = g * SW_REP + r
                    hl, rows = slice(HEAD_DIM * h, HEAD_DIM * (h + 1)), slice(SW_BLOCK * h, SW_BLOCK * (h + 1))
                    dsb = ds_scr[rows, :]
                    dq_ref[qr, hl] = _dot(dsb, k)
                    dkw = dkw + _dotg(dsb, q_ref[qr, hl], TN)
                    dvw = dvw + _dotg(p_scr[rows, :], do_ref[qr, hl], TN)
                dkp[kr, kl] = dkp[kr, kl] + dkw
                dvp[kr, kl] = dvp[kr, kl] + dvw
            return carry

        lax.fori_loop(0, nb, blk, 0)
        dk_ref[...] = dkp[SW_BLOCK:SW_BLOCK + s, :]
        dv_ref[...] = dvp[SW_BLOCK:SW_BLOCK + s, :]

    bias_spec = _full((SW_STACK, 3 * SW_BLOCK))
    return pl.pallas_call(
        body, name=name, grid=(1,),
        in_specs=[_full((s, SW_Q_WIDTH)), _full((s, SW_KV_WIDTH)),
                  pl.BlockSpec((s, SW_KV_WIDTH), lambda i: (0, v_blk)),
                  bias_spec, pl.BlockSpec(memory_space=pltpu.SMEM),
                  _full((s, SW_Q_WIDTH)), _full((s, SW_Q_WIDTH))],
        out_specs=[_full((s, SW_Q_WIDTH)), _full((s, SW_KV_WIDTH)), _full((s, SW_KV_WIDTH)), bias_spec,
                   _full((SW_STACK, LANES))],
        out_shape=[jax.ShapeDtypeStruct((s, SW_Q_WIDTH), F32), jax.ShapeDtypeStruct((s, SW_KV_WIDTH), F32),
                   jax.ShapeDtypeStruct((s, SW_KV_WIDTH), F32),
                   jax.ShapeDtypeStruct((SW_STACK, 3 * SW_BLOCK), F32),
                   jax.ShapeDtypeStruct((SW_STACK, LANES), F32)],
        scratch_shapes=[pltpu.VMEM((pad, SW_KV_WIDTH), BF16), pltpu.VMEM((pad, SW_KV_WIDTH), BF16),
                        pltpu.VMEM((pad, SW_KV_WIDTH), F32), pltpu.VMEM((pad, SW_KV_WIDTH), F32),
                        pltpu.VMEM((SW_STACK, 3 * SW_BLOCK), F32), pltpu.VMEM((SW_STACK, 3 * SW_BLOCK), F32),
                        pltpu.VMEM((SW_STACK, 3 * SW_BLOCK), BF16), pltpu.VMEM((SW_STACK, 3 * SW_BLOCK), BF16)],
        compiler_params=_params(),
    )(qs, ks, zq, t5b, sink, o_sw, do_sw)


def merge_out(x, o_na, o_sw, gt, wbna_t, wbsw_t, wout, name):
    s, d = x.shape
    tm = _row_tile(s)

    def body(x_ref, ona_ref, osw_ref, gt_ref, wna_ref, wsw_ref, wo_ref, xo_ref, ana_ref, asw_ref, mg_ref):
        a_na = _dotg(ona_ref[...], wna_ref[...], NT)
        a_sw = _dotg(osw_ref[...], wsw_ref[...], NT)
        g_na, g_sw = gt_ref[:, 0:d].astype(F32), gt_ref[:, d:2 * d].astype(F32)
        ana_ref[...] = (a_na * g_na * (1.0 - g_na)).astype(BF16)
        asw_ref[...] = (a_sw * g_sw * (1.0 - g_sw)).astype(BF16)
        merged = (g_na * a_na + g_sw * a_sw).astype(BF16)
        mg_ref[...] = merged
        xo_ref[...] = x_ref[...] + _dot(merged, wo_ref[...])

    return pl.pallas_call(
        body, name=name, grid=(s // tm,),
        in_specs=[_rows(tm, d), _rows(tm, 512), _rows(tm, 512), _rows(tm, 2 * d),
                  _mat(*wbna_t), _mat(*wbsw_t), _mat(*wout)],
        out_specs=[_rows(tm, d)] * 4,
        out_shape=[jax.ShapeDtypeStruct((s, d), F32)] + [jax.ShapeDtypeStruct((s, d), BF16)] * 3,
        compiler_params=_params(),
    )(x, o_na, o_sw, gt, wbna_t[0], wbsw_t[0], wout[0])


def mix_bwd_out(dx, gt, a_na, a_sw, wbna_t, wbsw_t, wout, dep, name):
    s, d = dx.shape
    tm = _row_tile(s)

    def body(dx_ref, gt_ref, ana_ref, asw_ref, wna_ref, wsw_ref, wo_ref, dep_ref,
             dxb_ref, dzg_ref, dana_ref, dasw_ref, dona_ref, dosw_ref, dbg_ref):
        @pl.when(pl.program_id(0) == 0)
        def _():
            dbg_ref[...] = jnp.zeros(dbg_ref.shape, F32)

        dxb = dx_ref[...].astype(BF16)
        dxb_ref[...] = dxb
        dm = _dotg(dxb, wo_ref[...], NT)
        for i, (a_ref, da_ref, w_ref, do_ref) in enumerate(
                [(ana_ref, dana_ref, wna_ref, dona_ref), (asw_ref, dasw_ref, wsw_ref, dosw_ref)]):
            gi = gt_ref[:, i * d:(i + 1) * d].astype(F32)
            da = (dm * gi).astype(BF16)
            da_ref[...] = da
            do_ref[...] = _dot(da, w_ref[...]).astype(BF16)
            dzg = dm * a_ref[...].astype(F32)
            dzg_ref[:, i * d:(i + 1) * d] = dzg.astype(BF16)
            dbg_ref[:, i * d:(i + 1) * d] = dbg_ref[:, i * d:(i + 1) * d] + jnp.sum(dzg, axis=0, keepdims=True)

    return pl.pallas_call(
        body, name=name, grid=(s // tm,),
        in_specs=[_rows(tm, d), _rows(tm, 2 * d), _rows(tm, d), _rows(tm, d),
                  _mat(*wbna_t), _mat(*wbsw_t), _mat(*wout), _full(dep.shape)],
        out_specs=[_rows(tm, d), _rows(tm, 2 * d), _rows(tm, d), _rows(tm, d), _rows(tm, 512), _rows(tm, 512),
                   _full((1, 2 * d))],
        out_shape=[jax.ShapeDtypeStruct((s, d), BF16), jax.ShapeDtypeStruct((s, 2 * d), BF16),
                   jax.ShapeDtypeStruct((s, d), BF16), jax.ShapeDtypeStruct((s, d), BF16),
                   jax.ShapeDtypeStruct((s, 512), BF16), jax.ShapeDtypeStruct((s, 512), BF16),
                   jax.ShapeDtypeStruct((1, 2 * d), F32)],
        compiler_params=_params(),
    )(dx, gt, a_na, a_sw, wbna_t[0], wbsw_t[0], wout[0], dep)


def qk_norm_bwd(dqa, dka, dva, dqs, dks, dvs, zq, dzg, gq_na, gk_na, gq_sw, gk_sw, bd, name):
    s = zq.shape[0]
    d2 = dzg.shape[1]
    n_in = QKV_WIDTH + d2
    tm = _row_tile(s)

    def body(dqa_ref, dka_ref, dva_ref, dqs_ref, dks_ref, dvs_ref, zq_ref, dzg_ref,
             gqa_ref, gka_ref, gqs_ref, gks_ref, bd_ref, dz_ref, dgqa_ref, dgka_ref, dgqs_ref, dgks_ref):
        @pl.when(pl.program_id(0) == 0)
        def _():
            for r in (dgqa_ref, dgka_ref, dgqs_ref, dgks_ref):
                r[...] = jnp.zeros(r.shape, F32)

        bd512 = bd_ref[...]
        bd128 = bd_ref[0:SW_KV_WIDTH, 0:SW_KV_WIDTH]

        def one(c0, c1, dy_ref, g_ref, dg_ref, bdm, scale):
            z = zq_ref[:, c0:c1].astype(F32)
            r = lax.rsqrt(_group_mean(z * z, bdm) + EPS)
            zh = z * r
            dy = dy_ref[...] * scale
            dyg = dy * g_ref[...]
            dz = r * (dyg - zh * _group_mean(dyg * zh, bdm))
            dz_ref[:, c0:c1] = dz.astype(BF16)
            dg_ref[...] = dg_ref[...] + jnp.sum(dy * zh, axis=0, keepdims=True)

        one(0, 512, dqa_ref, gqa_ref, dgqa_ref, bd512, SCALE)
        one(512, 1024, dka_ref, gka_ref, dgka_ref, bd512, 1.0)
        dz_ref[:, 1024:1536] = dva_ref[...].astype(BF16)
        one(1536, 2048, dqs_ref, gqs_ref, dgqs_ref, bd512, SCALE)
        one(2048, 2176, dks_ref, gks_ref, dgks_ref, bd128, 1.0)
        dz_ref[:, 2176:2304] = dvs_ref[...].astype(BF16)
        dz_ref[:, QKV_WIDTH:n_in] = dzg_ref[...]

    return pl.pallas_call(
        body, name=name, grid=(s // tm,),
        in_specs=[_rows(tm, 512), _rows(tm, 512), _rows(tm, 512), _rows(tm, 512), _rows(tm, 128), _rows(tm, 128),
                  _rows(tm, QKV_WIDTH), _rows(tm, d2),
                  _full((1, 512)), _full((1, 512)), _full((1, 512)), _full((1, 128)), _full((MXU_TILE, MXU_TILE))],
        out_specs=[_rows(tm, n_in), _full((1, 512)), _full((1, 512)), _full((1, 512)), _full((1, 128))],
        out_shape=[jax.ShapeDtypeStruct((s, n_in), BF16)] + [jax.ShapeDtypeStruct((1, 512), F32)] * 3
                  + [jax.ShapeDtypeStruct((1, 128), F32)],
        compiler_params=_params(),
    )(dqa, dka, dva, dqs, dks, dvs, zq, dzg, gq_na, gk_na, gq_sw, gk_sw, bd)


def ffn_bwd_act(dx, wd, hg, hu, name):
    s, d = dx.shape
    f = wd[0].shape[1]
    tm = _row_tile(s)
    fc = _col_chunk(f)

    def body(dx_ref, w_ref, hg_ref, hu_ref, dxb_ref, dhg_ref, dhu_ref):
        dxv = dx_ref[...]
        dxb_ref[...] = dxv.astype(BF16)
        half = (0.5 * dxv).astype(BF16)
        for c0 in range(0, f, fc):
            dact = _dotg(half, w_ref[c0:c0 + fc, :], NT)
            dhu_ref[:, c0:c0 + fc] = (dact * hu_ref[:, c0:c0 + fc].astype(F32)).astype(BF16)
            dhg_ref[:, c0:c0 + fc] = (dact * hg_ref[:, c0:c0 + fc].astype(F32)).astype(BF16)

    return pl.pallas_call(
        body, name=name, grid=(s // tm,),
        in_specs=[_rows(tm, d), _mat(*wd), _rows(tm, f), _rows(tm, f)],
        out_specs=[_rows(tm, d), _rows(tm, f), _rows(tm, f)],
        out_shape=[jax.ShapeDtypeStruct((s, d), BF16), jax.ShapeDtypeStruct((s, f), BF16),
                   jax.ShapeDtypeStruct((s, f), BF16)],
        compiler_params=_params(),
    )(dx, wd[0], hg, hu)


def proj_bwd_norm(acts, weights, x, gain, dx, dep, name):
    s, d = x.shape
    tm = min(_row_tile(s), 256)
    n = len(acts)

    def body(*refs):
        a_refs, w_refs = refs[:n], refs[n:2 * n]
        x_ref, g_ref, dx_ref, _, o_ref, dg_ref = refs[2 * n:]

        @pl.when(pl.program_id(0) == 0)
        def _():
            dg_ref[...] = jnp.zeros(dg_ref.shape, F32)

        dxn = _dot(a_refs[0][...], w_refs[0][...])
        for a_ref, w_ref in zip(a_refs[1:], w_refs[1:]):
            dxn = dxn + _dot(a_ref[...], w_ref[...])
        xv = x_ref[...]
        r = _rstd(xv)
        xh = xv * r
        dxh = dxn * g_ref[...]
        o_ref[...] = dx_ref[...] + r * (dxh - xh * jnp.mean(dxh * xh, axis=-1, keepdims=True))
        dg_ref[...] = dg_ref[...] + jnp.sum(dxn * xh, axis=0, keepdims=True)

    return pl.pallas_call(
        body, name=name, grid=(s // tm,),
        in_specs=[_rows(tm, a.shape[1]) for a in acts] + [_mat(*w) for w in weights]
                 + [_rows(tm, d), _full((1, d)), _rows(tm, d), _full(dep.shape)],
        out_specs=[_rows(tm, d), _full((1, d))],
        out_shape=[jax.ShapeDtypeStruct((s, d), F32), jax.ShapeDtypeStruct((1, d), F32)],
        compiler_params=_params(),
    )(*acts, *[w[0] for w in weights], x, gain, dx, dep)


def tn_matmul(products, name):
    s, n = products[0][0].shape
    tn = _tn_tile(n) if len(products) == 1 else _col_chunk(n)
    rhs = []
    for _, b, _ in products:
        if not any(b is seen for seen in rhs):
            rhs.append(b)
    which = [next(i for i, seen in enumerate(rhs) if b is seen) for _, b, _ in products]
    npr, nr = len(products), len(rhs)

    def body(*refs):
        a_refs, b_refs, o_refs = refs[:npr], refs[npr:npr + nr], refs[npr + nr:]
        for i, (_, _, scale) in enumerate(products):
            o_refs[i][...] = (scale * _dotg(a_refs[i][...], b_refs[which[i]][...], TN)).astype(BF16)

    return pl.pallas_call(
        body, name=name, grid=(n // tn,),
        in_specs=[pl.BlockSpec((s, tn), lambda i: (0, i))] * npr
                 + [pl.BlockSpec(b.shape, lambda i: (0, 0), pipeline_mode=ONCE) for b in rhs],
        out_specs=[pl.BlockSpec((tn, b.shape[1]), lambda i: (i, 0)) for _, b, _ in products],
        out_shape=[jax.ShapeDtypeStruct((n, b.shape[1]), BF16) for _, b, _ in products],
        compiler_params=_params(),
    )(*[a for a, _, _ in products], *rhs)


def _mesh_pos():
    return lax.axis_index("x"), lax.axis_index("y"), lax.axis_index("c")


def _peers():
    x, y, c = _mesh_pos()
    peers = []
    for rel in range(1, N_DEV):
        peers.append((1 - x if rel & 4 else x, 1 - y if rel & 2 else y, 1 - c if rel & 1 else c))
    return 4 * x + 2 * y + c, peers


HBM_SPEC = pl.BlockSpec(memory_space=pltpu.HBM)
SEM_SPEC = pl.BlockSpec(memory_space=pltpu.SEMAPHORE)


def _split_call(body, name, thru, n_sems, extra=(), with_token=True):
    hbm = lambda t: pltpu.with_memory_space_constraint(t, pltpu.HBM)
    effect = pltpu.CompilerParams(has_side_effects=pltpu.SideEffectType.DATAFLOW_SIDE_EFFECTING)
    nt = len(thru)
    thru_shapes = [pltpu.HBM(t.shape, t.dtype) for t in thru]
    if with_token:
        (after,) = extra
        outs = pl.pallas_call(
            body, name=name, in_specs=[HBM_SPEC] * nt + [pl.BlockSpec(memory_space=pl.ANY)],
            out_specs=[SEM_SPEC] * len(n_sems) + [HBM_SPEC] * nt + [pl.BlockSpec(memory_space=pltpu.VMEM)],
            out_shape=[pltpu.SemaphoreType.DMA((k,)) for k in n_sems] + thru_shapes
                      + [jax.ShapeDtypeStruct((8, LANES), F32)],
            input_output_aliases={i: len(n_sems) + i for i in range(nt)}, compiler_params=effect,
        )(*[hbm(t) for t in thru], after)
        return outs[:len(n_sems)], outs[len(n_sems):-1], outs[-1]
    return pl.pallas_call(
        body, name=name,
        in_specs=[HBM_SPEC] * nt + [SEM_SPEC] * len(n_sems) + [pl.BlockSpec(memory_space=pl.ANY)],
        out_specs=[HBM_SPEC] * nt, out_shape=thru_shapes,
        input_output_aliases={i: i for i in range(nt)}, compiler_params=effect,
    )(*thru, *extra)


def _gather_targets():
    x, y, c = _mesh_pos()
    return 4 * x + 2 * y + c, [(x, y, 1 - c), (1 - x, y, c), (x, 1 - y, c), (1 - x, 1 - y, c)]


def gather_start(shards, after, name):
    n = len(shards)
    zones = [lax.empty((w.shape[0], N_DEV) + w.shape[1:], w.dtype) for w in shards]

    def body(*refs):
        ins, zs = refs[:n], refs[n:2 * n]
        send_sems, recv_sems, local_sems = refs[2 * n + 1:2 * n + 4]
        token = refs[-1]
        me, targets = _gather_targets()
        for a in range(n):
            pltpu.make_async_copy(ins[a], zs[a].at[:, me], local_sems.at[a]).start()
            for k, to in enumerate(targets):
                pltpu.make_async_remote_copy(
                    src_ref=ins[a], dst_ref=zs[a].at[:, me], send_sem=send_sems.at[4 * a + k],
                    recv_sem=recv_sems.at[4 * a + k], device_id=to, device_id_type=MESH).start()
        token[...] = jnp.zeros(token.shape, F32)

    sems, thru, token = _split_call(body, name, list(shards) + zones, (4 * n, 4 * n, n), extra=(after,))
    return (sems, thru, n), token


def gather_wait(started, after, name):
    sems, thru, n = started

    def body(*refs):
        zs = refs[n:2 * n]
        send_sems, recv_sems, local_sems = refs[2 * n:2 * n + 3]
        _, targets = _gather_targets()
        for a in range(n):
            for k, to in enumerate(targets):
                cp = pltpu.make_async_remote_copy(
                    src_ref=zs[a].at[:, 0], dst_ref=zs[a].at[:, 0], send_sem=send_sems.at[4 * a + k],
                    recv_sem=recv_sems.at[4 * a + k], device_id=to, device_id_type=MESH)
                cp.wait_send()
                cp.wait_recv()
            pltpu.make_async_copy(zs[a].at[:, 0], zs[a].at[:, 0], local_sems.at[a]).wait()

    return _split_call(body, name, thru, (4 * n, 4 * n, n), extra=(*sems, after), with_token=False)[n:]


def forward_start(zones, after, name):
    n = len(zones)

    def body(*refs):
        zs = refs[:n]
        send_sems, recv_sems = refs[n + 1:n + 3]
        token = refs[-1]
        x, y, c = _mesh_pos()
        for a in range(n):
            for j, chip in enumerate([(1 - x, y), (x, 1 - y), (1 - x, 1 - y)]):
                blk = zs[a].at[:, 4 * chip[0] + 2 * chip[1] + c]
                pltpu.make_async_remote_copy(
                    src_ref=blk, dst_ref=blk, send_sem=send_sems.at[3 * a + j], recv_sem=recv_sems.at[3 * a + j],
                    device_id=(x, y, 1 - c), device_id_type=MESH).start()
        token[...] = jnp.zeros(token.shape, F32)

    sems, thru, token = _split_call(body, name, list(zones), (3 * n, 3 * n), extra=(after,))
    return (sems, thru, n), token


def forward_wait(started, after, name):
    sems, thru, n = started

    def body(*refs):
        zs = refs[:n]
        send_sems, recv_sems = refs[n:n + 2]
        x, y, c = _mesh_pos()
        for a in range(n):
            for j in range(3):
                cp = pltpu.make_async_remote_copy(
                    src_ref=zs[a].at[:, 0], dst_ref=zs[a].at[:, 0], send_sem=send_sems.at[3 * a + j],
                    recv_sem=recv_sems.at[3 * a + j], device_id=(x, y, 1 - c), device_id_type=MESH)
                cp.wait_send()
                cp.wait_recv()

    return _split_call(body, name, thru, (3 * n, 3 * n), extra=(*sems, after), with_token=False)


def scatter_start(groups, name):
    n = len(groups)
    flat = [g for grp in groups for g in grp]
    nf = len(flat)
    offs = np.cumsum([0] + [len(grp) for grp in groups])
    lands = [lax.empty((N_DEV, len(grp)) + grp[0].shape[1:], grp[0].dtype) for grp in groups]

    def body(*refs):
        ins, zones = refs[:nf], refs[nf:nf + n]
        send_sems, recv_sems, local_sems = refs[nf + n:nf + n + 3]
        token = refs[-1]
        me, peers = _peers()
        for a in range(n):
            for w in range(len(groups[a])):
                pltpu.make_async_copy(ins[offs[a] + w].at[me], zones[a].at[me, w], local_sems.at[a]).start()
        for k, peer in enumerate(peers):
            p_id = 4 * peer[0] + 2 * peer[1] + peer[2]
            for a in range(n):
                for w in range(len(groups[a])):
                    pltpu.make_async_remote_copy(
                        src_ref=ins[offs[a] + w].at[p_id], dst_ref=zones[a].at[me, w],
                        send_sem=send_sems.at[7 * a + k], recv_sem=recv_sems.at[7 * a + k],
                        device_id=peer, device_id_type=MESH).start()
        token[...] = jnp.zeros(token.shape, F32)

    hbm = lambda t: pltpu.with_memory_space_constraint(t, pltpu.HBM)
    outs = pl.pallas_call(
        body, name=name,
        in_specs=[HBM_SPEC] * (nf + n),
        out_specs=[SEM_SPEC] * 3 + [HBM_SPEC] * (nf + n) + [pl.BlockSpec(memory_space=pltpu.VMEM)],
        out_shape=[pltpu.SemaphoreType.DMA((7 * n,)), pltpu.SemaphoreType.DMA((7 * n,)), pltpu.SemaphoreType.DMA((n,))]
                  + [pltpu.HBM(t.shape, t.dtype) for t in flat + lands]
                  + [jax.ShapeDtypeStruct((8, LANES), F32)],
        input_output_aliases={i: 3 + i for i in range(nf + n)},
        compiler_params=pltpu.CompilerParams(has_side_effects=pltpu.SideEffectType.DATAFLOW_SIDE_EFFECTING),
    )(*[hbm(t) for t in flat], *[hbm(t) for t in lands])
    sems, thru, token = outs[:3], outs[3:3 + nf + n], outs[-1]
    return (sems, thru, [len(grp) for grp in groups]), token


def scatter_wait(started, after, name):
    (send_sems, recv_sems, local_sems), thru, sizes = started
    n = len(sizes)
    nf = len(thru) - n

    def body(*refs):
        zones = refs[nf:nf + n]
        s_sems, r_sems, l_sems = refs[nf + n:nf + n + 3]
        me, peers = _peers()
        for a in range(n):
            for k, peer in enumerate(peers):
                cp = pltpu.make_async_remote_copy(
                    src_ref=zones[a].at[0], dst_ref=zones[a].at[0],
                    send_sem=s_sems.at[7 * a + k], recv_sem=r_sems.at[7 * a + k], device_id=peer,
                    device_id_type=MESH)
                cp.wait_send()
                cp.wait_recv()
            pltpu.make_async_copy(zones[a].at[0], zones[a].at[0], l_sems.at[a]).wait()

    outs = pl.pallas_call(
        body, name=name,
        in_specs=[HBM_SPEC] * (nf + n) + [SEM_SPEC] * 3 + [pl.BlockSpec(memory_space=pl.ANY)],
        out_specs=[HBM_SPEC] * (nf + n),
        out_shape=[pltpu.HBM(t.shape, t.dtype) for t in thru],
        input_output_aliases={i: i for i in range(nf + n)},
        compiler_params=pltpu.CompilerParams(has_side_effects=pltpu.SideEffectType.DATAFLOW_SIDE_EFFECTING),
    )(*thru, send_sems, recv_sems, local_sems, after)
    return outs[nf:]


def pair_start(grads, after, name):
    nw = len(grads)
    land = lax.empty((4, nw) + grads[0].shape[1:], grads[0].dtype)

    def body(*refs):
        ins, zone = refs[:nw], refs[nw]
        send_sems, recv_sems = refs[nw + 2:nw + 4]
        x, y, c = _mesh_pos()
        for j in range(4):
            for w in range(nw):
                pltpu.make_async_remote_copy(
                    src_ref=ins[w].at[2 * j + (1 - c)], dst_ref=zone.at[j, w], send_sem=send_sems.at[0],
                    recv_sem=recv_sems.at[0], device_id=(x, y, 1 - c), device_id_type=MESH).start()
        refs[-1][...] = jnp.zeros(refs[-1].shape, F32)

    sems, thru, token = _split_call(body, name, list(grads) + [land], (1, 1), extra=(after,))
    return (sems, thru, nw), token


def pair_wait(started, after, name):
    sems, thru, nw = started

    def body(*refs):
        zone = refs[nw]
        send_sems, recv_sems = refs[nw + 1:nw + 3]
        x, y, c = _mesh_pos()
        cp = pltpu.make_async_remote_copy(src_ref=zone, dst_ref=zone, send_sem=send_sems.at[0],
                                          recv_sem=recv_sems.at[0], device_id=(x, y, 1 - c), device_id_type=MESH)
        cp.wait_send()
        cp.wait_recv()

    outs = _split_call(body, name, thru, (1, 1), extra=(*sems, after), with_token=False)
    return outs[:nw], outs[nw]


def pair_sum(grads, land, name):
    nw = len(grads)
    _, r, c_dim = grads[0].shape

    def body(*refs):
        g_refs, l_ref, o_ref = refs[:nw], refs[nw], refs[nw + 1]
        core = lax.axis_index("c")
        for w in range(nw):
            o_ref[0, w] = (g_refs[w][0, core].astype(F32) + l_ref[0, w].astype(F32)).astype(BF16)

    return pl.pallas_call(
        body, name=name, grid=(4,),
        in_specs=[pl.BlockSpec((1, 2, r, c_dim), lambda j: (j, 0, 0, 0))] * nw
                 + [pl.BlockSpec((1, nw, r, c_dim), lambda j: (j, 0, 0, 0))],
        out_specs=pl.BlockSpec((1, nw, r, c_dim), lambda j: (j, 0, 0, 0)),
        out_shape=jax.ShapeDtypeStruct((4, nw, r, c_dim), BF16),
        compiler_params=_params(),
    )(*[g.reshape(4, 2, r, c_dim) for g in grads], land)


def _other_chips():
    x, y, c = _mesh_pos()
    chips = []
    for rel in range(1, 4):
        px, py = (1 - x if rel & 2 else x), (1 - y if rel & 1 else y)
        chips.append((px, py, 2 * px + py))
    return 2 * x + y, c, chips


def chip_start(pair_sums, after, name):
    land = lax.empty(pair_sums.shape, pair_sums.dtype)

    def body(*refs):
        h_ref, zone = refs[0], refs[1]
        send_sems, recv_sems, local_sem = refs[3:6]
        mine, c, chips = _other_chips()
        pltpu.make_async_copy(h_ref.at[mine], zone.at[mine], local_sem.at[0]).start()
        for k, (px, py, j) in enumerate(chips):
            pltpu.make_async_remote_copy(
                src_ref=h_ref.at[j], dst_ref=zone.at[mine], send_sem=send_sems.at[k], recv_sem=recv_sems.at[k],
                device_id=(px, py, c), device_id_type=MESH).start()
        refs[-1][...] = jnp.zeros(refs[-1].shape, F32)

    sems, thru, token = _split_call(body, name, [pair_sums, land], (3, 3, 1), extra=(after,))
    return (sems, thru), token


def chip_wait(started, after, name):
    sems, thru = started

    def body(*refs):
        zone = refs[1]
        send_sems, recv_sems, local_sem = refs[2:5]
        _, c, chips = _other_chips()
        for k, (px, py, _) in enumerate(chips):
            cp = pltpu.make_async_remote_copy(
                src_ref=zone.at[0], dst_ref=zone.at[0], send_sem=send_sems.at[k], recv_sem=recv_sems.at[k],
                device_id=(px, py, c), device_id_type=MESH)
            cp.wait_send()
            cp.wait_recv()
        pltpu.make_async_copy(zone.at[0], zone.at[0], local_sem.at[0]).wait()

    return _split_call(body, name, thru, (3, 3, 1), extra=(*sems, after), with_token=False)[1]


def share_small(parts, after):
    n = len(parts)

    def body(*refs):
        ins, outs = refs[:n], refs[n + 1:2 * n + 1]
        send_sems, recv_sems, local_sems = refs[2 * n + 1:]
        me, peers = _peers()
        copies = []
        for i in range(n):
            copies.append(pltpu.make_async_copy(ins[i], outs[i].at[me], local_sems.at[i]))
            copies += [pltpu.make_async_remote_copy(
                src_ref=ins[i], dst_ref=outs[i].at[me], send_sem=send_sems.at[7 * i + k],
                recv_sem=recv_sems.at[7 * i + k], device_id=peer, device_id_type=MESH)
                for k, peer in enumerate(peers)]
        for cp in copies:
            cp.start()
        for cp in copies:
            cp.wait()

    vm = pl.BlockSpec(memory_space=pltpu.VMEM)
    return pl.pallas_call(
        body, name="share_small", in_specs=[vm] * n + [pl.BlockSpec(memory_space=pl.ANY)], out_specs=[vm] * n,
        out_shape=[jax.ShapeDtypeStruct((N_DEV,) + p.shape, p.dtype) for p in parts],
        scratch_shapes=[pltpu.SemaphoreType.DMA((7 * n,)), pltpu.SemaphoreType.DMA((7 * n,)),
                        pltpu.SemaphoreType.DMA((n,))],
    )(*parts, after)


def _adamw_math(w, g, m, v):
    m = ADAM_B1 * m + (1.0 - ADAM_B1) * g
    v = ADAM_B2 * v + (1.0 - ADAM_B2) * (g * g)
    m_hat = m / (1.0 - ADAM_B1 ** ADAM_STEP)
    v_hat = v / (1.0 - ADAM_B2 ** ADAM_STEP)
    delta = -ADAM_LR * (m_hat / (jnp.sqrt(v_hat) + ADAM_EPS) + ADAM_WD * w)
    return delta, m, v


def adamw_layer(zone, w_idx, layer, w, m, v, prev, after, name):
    n_src, _, r, c = zone.shape
    depth = w.shape[0]
    if prev is None:
        prev = tuple(lax.empty((depth, r, c), F32) for _ in range(4))
    tr = r // 2 if r % 16 == 0 else r

    def body(z_ref, w_ref, m_ref, v_ref, *rest):
        g_ref, d_ref, mo_ref, vo_ref = rest[5:]
        g = z_ref[0].astype(F32)
        for src in range(1, n_src):
            g = g + z_ref[src].astype(F32)
        g_ref[...] = g
        d_ref[...], mo_ref[...], vo_ref[...] = _adamw_math(w_ref[...], g, m_ref[...], v_ref[...])

    rows = pl.BlockSpec((None, tr, c), lambda i: (layer, i, 0))
    anywhere = pl.BlockSpec(memory_space=pl.ANY)
    return pl.pallas_call(
        body, name=name, grid=(r // tr,),
        in_specs=[pl.BlockSpec((n_src, None, tr, c), lambda i: (0, w_idx, i, 0)), rows, rows, rows]
                 + [anywhere] * 5,
        out_specs=[rows] * 4,
        out_shape=[jax.ShapeDtypeStruct((depth, r, c), F32)] * 4,
        input_output_aliases={4 + k: k for k in range(4)},
        compiler_params=_params(),
    )(zone, w, m, v, *prev, after)


def adamw_small(ws, recvs, ms, vs, name):
    n = len(ws)

    def body(*refs):
        w_refs, r_refs, m_refs, v_refs = (refs[i * n:(i + 1) * n] for i in range(4))
        g_refs, d_refs, mo_refs, vo_refs = (refs[(4 + i) * n:(5 + i) * n] for i in range(4))
        for i in range(n):
            g = r_refs[i][0]
            for src in range(1, N_DEV):
                g = g + r_refs[i][src]
            g_refs[i][...] = g
            d_refs[i][...], mo_refs[i][...], vo_refs[i][...] = _adamw_math(w_refs[i][...], g, m_refs[i][...],
                                                                            v_refs[i][...])

    vm = pl.BlockSpec(memory_space=pltpu.VMEM)
    outs = pl.pallas_call(
        body, name=name, in_specs=[vm] * (4 * n), out_specs=[vm] * (4 * n),
        out_shape=[jax.ShapeDtypeStruct(w.shape, F32) for w in ws] * 4,
        compiler_params=pltpu.CompilerParams(vmem_limit_bytes=V7X_VMEM_LIMIT),
    )(*ws, *recvs, *ms, *vs)
    return [outs[i * n:(i + 1) * n] for i in range(4)]


SMALL_NAMES = ("ffn1_norm", "mix_norm", "ffn2_norm", "b_gate", "na_q_norm", "na_k_norm", "sw_q_norm", "sw_k_norm",
               "na_rpb", "sw_sink", "t5_rel_table")


def kernel(x, ffn1_norm, ffn1_w_gate, ffn1_w_up, ffn1_w_down, mix_norm, w_in, b_gate, na_q_norm, na_k_norm, na_rpb, sw_q_norm, sw_k_norm, sw_sink, t5_rel_table, w_branch_na, w_branch_sw, w_out, ffn2_norm, ffn2_w_gate, ffn2_w_up, ffn2_w_down, loss_target, m_ffn1_norm, m_ffn1_w_gate, m_ffn1_w_up, m_ffn1_w_down, m_mix_norm, m_w_in, m_b_gate, m_na_q_norm, m_na_k_norm, m_na_rpb, m_sw_q_norm, m_sw_k_norm, m_sw_sink, m_t5_rel_table, m_w_branch_na, m_w_branch_sw, m_w_out, m_ffn2_norm, m_ffn2_w_gate, m_ffn2_w_up, m_ffn2_w_down, v_ffn1_norm, v_ffn1_w_gate, v_ffn1_w_up, v_ffn1_w_down, v_mix_norm, v_w_in, v_b_gate, v_na_q_norm, v_na_k_norm, v_na_rpb, v_sw_q_norm, v_sw_k_norm, v_sw_sink, v_t5_rel_table, v_w_branch_na, v_w_branch_sw, v_w_out, v_ffn2_norm, v_ffn2_w_gate, v_ffn2_w_up, v_ffn2_w_down):
    weights = dict(ffn1_norm=ffn1_norm, ffn1_w_gate=ffn1_w_gate, ffn1_w_up=ffn1_w_up, ffn1_w_down=ffn1_w_down,
                   mix_norm=mix_norm, w_in=w_in, b_gate=b_gate, na_q_norm=na_q_norm, na_k_norm=na_k_norm,
                   na_rpb=na_rpb, sw_q_norm=sw_q_norm, sw_k_norm=sw_k_norm, sw_sink=sw_sink,
                   t5_rel_table=t5_rel_table, w_branch_na=w_branch_na, w_branch_sw=w_branch_sw, w_out=w_out,
                   ffn2_norm=ffn2_norm, ffn2_w_gate=ffn2_w_gate, ffn2_w_up=ffn2_w_up, ffn2_w_down=ffn2_w_down)
    mom_m = dict(ffn1_norm=m_ffn1_norm, ffn1_w_gate=m_ffn1_w_gate, ffn1_w_up=m_ffn1_w_up, ffn1_w_down=m_ffn1_w_down,
                 mix_norm=m_mix_norm, w_in=m_w_in, b_gate=m_b_gate, na_q_norm=m_na_q_norm, na_k_norm=m_na_k_norm,
                 na_rpb=m_na_rpb, sw_q_norm=m_sw_q_norm, sw_k_norm=m_sw_k_norm, sw_sink=m_sw_sink,
                 t5_rel_table=m_t5_rel_table, w_branch_na=m_w_branch_na, w_branch_sw=m_w_branch_sw, w_out=m_w_out,
                 ffn2_norm=m_ffn2_norm, ffn2_w_gate=m_ffn2_w_gate, ffn2_w_up=m_ffn2_w_up, ffn2_w_down=m_ffn2_w_down)
    mom_v = dict(ffn1_norm=v_ffn1_norm, ffn1_w_gate=v_ffn1_w_gate, ffn1_w_up=v_ffn1_w_up, ffn1_w_down=v_ffn1_w_down,
                 mix_norm=v_mix_norm, w_in=v_w_in, b_gate=v_b_gate, na_q_norm=v_na_q_norm, na_k_norm=v_na_k_norm,
                 na_rpb=v_na_rpb, sw_q_norm=v_sw_q_norm, sw_k_norm=v_sw_k_norm, sw_sink=v_sw_sink,
                 t5_rel_table=v_t5_rel_table, w_branch_na=v_w_branch_na, w_branch_sw=v_w_branch_sw, w_out=v_w_out,
                 ffn2_norm=v_ffn2_norm, ffn2_w_gate=v_ffn2_w_gate, ffn2_w_up=v_ffn2_w_up, ffn2_w_down=v_ffn2_w_down)
    order = list(weights)

    depth = ffn1_norm.shape[0]
    s, d = x.shape[1], x.shape[2]
    xs = x[0]
    tr = lambda w: jnp.swapaxes(w, -1, -2)

    merge = lambda t: t.reshape(t.shape[0], N_DEV * t.shape[2], t.shape[3])
    no_dep = jnp.zeros((8, LANES), F32)

    def shards_of(kind, l):
        stack = lambda *ws: jnp.stack(ws).astype(BF16)
        if kind == "ffn1":
            return [stack(tr(ffn1_w_gate[l]), tr(ffn1_w_up[l]), ffn1_w_down[l])]
        if kind == "win":
            return [stack(tr(w_in[l]))]
        return [stack(tr(ffn2_w_gate[l]), tr(ffn2_w_up[l]), ffn2_w_down[l]), stack(w_out[l]),
                stack(tr(w_branch_na[l]), tr(w_branch_sw[l]))]

    def start(kind, l, after):
        return gather_start(shards_of(kind, l), after, f"gather_{kind}_{l}")

    def arrive(started, kind, l, after):
        zones = gather_wait(started, after, f"gather_{kind}_{l}_wait")
        return forward_start(zones, no_dep, f"forward_{kind}_{l}")

    def finish(fwd, kind, l, after):
        return [merge(z) for z in forward_wait(fwd, after, f"forward_{kind}_{l}_wait")]

    bd = jnp.asarray(np.kron(np.eye(MXU_TILE // HEAD_DIM), np.full((HEAD_DIM, HEAD_DIM), 1.0 / HEAD_DIM)), BF16)
    bmap = jnp.asarray(_t5_bucket_map())
    tile8 = lambda g: jnp.tile(g, NA_WIDTH // HEAD_DIM).reshape(1, NA_WIDTH)
    tile2 = lambda g: jnp.tile(g, SW_KV_WIDTH // HEAD_DIM).reshape(1, SW_KV_WIDTH)

    st_first, tok = start("ffn1", 0, no_dep)
    t5b = t5_expand(t5_rel_table, bmap, tok, "t5_expand").reshape(SW_STACK, 3 * SW_BLOCK)
    t2_tables = [rpb_expand(_rpb_rows(na_rpb[l]), tok, f"rpb_expand_{l}") for l in range(depth)]
    fwd, _ = arrive(st_first, "ffn1", 0, t2_tables[-1])
    st_win, dep = start("win", 0, t5b)
    (first,) = finish(fwd, "ffn1", 0, dep)

    saved = []
    layer_w = {0: dict(wg1=(first, 0), wu1=(first, 1), wd1=(first, 2))}
    cur = xs
    for l in range(depth):
        sv = {}
        lw = layer_w[l]
        sv["x0"] = cur
        cur, sv["xn1"], sv["hg1"], sv["hu1"], sv["act1"] = ffn_forward(
            cur, ffn1_norm[l][None], lw["wg1"], lw["wu1"], lw["wd1"], dep, f"ffn1_{l}")
        sv["x1"] = cur
        fwd, _ = arrive(st_win, "win", l, cur)
        st_rest, tok = start("rest", l, cur)
        (zb,) = finish(fwd, "win", l, tok)
        lw["win"] = (zb, 0)
        sv["gains"] = (tile8(na_q_norm[l]), tile8(na_k_norm[l]), tile8(sw_q_norm[l]), tile2(sw_k_norm[l]))
        sv["hn"], sv["zq"], sv["qa"], sv["ka"], sv["qs"], sv["ks"], sv["gt"] = mix_in(
            cur, mix_norm[l][None], lw["win"], b_gate[l][None], *sv["gains"], bd, f"mix_in_{l}")
        sv["t2"] = t2_tables[l]
        sv["o_na"] = na_fwd(sv["qa"], sv["ka"], sv["zq"], sv["t2"], f"na_fwd_{l}")
        dep = no_dep
        if l + 1 < depth:
            st_ffn1, dep = start("ffn1", l + 1, sv["o_na"])
        sv["o_sw"] = sw_fwd(sv["qs"], sv["ks"], sv["zq"], t5b, sw_sink[l], dep, f"sw_fwd_{l}")
        fwd, tok = arrive(st_rest, "rest", l, sv["o_sw"])
        za, zc, zd = finish(fwd, "rest", l, tok)
        lw.update(wg2=(za, 0), wu2=(za, 1), wd2=(za, 2), wout=(zc, 0), wna=(zd, 0), wsw=(zd, 1))
        cur, sv["a_na"], sv["a_sw"], sv["merged"] = merge_out(
            cur, sv["o_na"], sv["o_sw"], sv["gt"], lw["wna"], lw["wsw"], lw["wout"], f"merge_out_{l}")
        sv["x2"] = cur
        dep = no_dep
        if l + 1 < depth:
            st_win, dep = start("win", l + 1, cur)
        sv["xn2"], sv["hg2"], sv["hu2"], sv["act2"] = ffn_forward(
            cur, ffn2_norm[l][None], lw["wg2"], lw["wu2"], None, dep, f"ffn2_up_{l}")
        dep = no_dep
        if l + 1 < depth:
            fwd, dep = arrive(st_ffn1, "ffn1", l + 1, sv["act2"])
        if l + 1 < depth:
            cur = ffn_down(cur, sv["act2"], lw["wd2"], dep, f"ffn2_down_{l}")
            (za,) = finish(fwd, "ffn1", l + 1, cur)
            layer_w[l + 1] = dict(wg1=(za, 0), wu1=(za, 1), wd1=(za, 2))
        else:
            dx, loss_acc = ffn_down(cur, sv["act2"], lw["wd2"], dep, f"ffn2_down_{l}", target=loss_target[0])
        dep = no_dep
        saved.append(sv)

    loss = lax.psum(jnp.sum(loss_acc) * (0.5 / d), ("x", "y", "c"))

    split = lambda t: t.reshape(N_DEV, t.shape[0] // N_DEV, t.shape[1])
    pending = {}
    last_key = "ffn1_0"
    two_level = {last_key}
    small = {k: [None] * depth for k in SMALL_NAMES if k != "t5_rel_table"}
    dbias_sw = []
    for l in reversed(range(depth)):
        sv = saved[l]
        lw = layer_w[l]
        wg1, wu1, wd1, wg2, wu2, wd2 = (lw[k] for k in ("wg1", "wu1", "wd1", "wg2", "wu2", "wd2"))
        win_t, wout_l, wna_t, wsw_t = lw["win"], lw["wout"], lw["wna"], lw["wsw"]
        blocks = ((2, "x2", "xn2", "hg2", "hu2", "act2", wg2, wu2, wd2, "ffn2_norm", 3),
                  (1, "x0", "xn1", "hg1", "hu1", "act1", wg1, wu1, wd1, "ffn1_norm", 0))

        def ffn_backward(dx, blk):
            tag, xk, xnk, hgk, huk, actk, wg, wu, wd, norm_name, slot = blk
            gains = weights[norm_name]
            dxb, dhg, dhu = ffn_bwd_act(dx, wd, sv[hgk], sv[huk], f"ffn{tag}_bwd_act_{l}")
            gwg, gwu, gwd = tn_matmul([(dhg, sv[xnk], 1.0), (dhu, sv[xnk], 1.0), (sv[actk], dxb, 0.5)],
                                      f"ffn{tag}_dw_{l}")
            key = f"ffn{tag}_{l}"
            blocks_of = [split(gwg), split(gwu), split(gwd)]
            if key in two_level:
                paired, token = pair_start(blocks_of, dxb, f"pair_{key}")
            else:
                pending[key], token = scatter_start([blocks_of], f"scatter_{key}")
            dx, dg = proj_bwd_norm([dhg, dhu], [wg, wu], sv[xk], gains[l][None], dx, token, f"ffn{tag}_bwd_x_{l}")
            token = no_dep
            if key in two_level:
                thru, land = pair_wait(paired, dx, f"pair_{key}_wait")
                pending[key], token = chip_start(pair_sum(thru, land, f"pair_sum_{key}"), dg, f"chips_{key}")
            small[norm_name][l] = dg[0]
            return dx, token

        dx, token = ffn_backward(dx, blocks[0])
        dxb, dzg, da_na, da_sw, do_na, do_sw, dbg = mix_bwd_out(
            dx, sv["gt"], sv["a_na"], sv["a_sw"], wna_t, wsw_t, wout_l, token, f"mix_bwd_out_{l}")
        small["b_gate"][l] = dbg[0]
        gwout, gwna, gwsw = tn_matmul([(sv["merged"], dxb, 1.0), (da_na, sv["o_na"], 1.0), (da_sw, sv["o_sw"], 1.0)],
                                      f"mix_dw_{l}")
        dqa, dka, dva, dt2 = na_bwd(sv["qa"], sv["ka"], sv["zq"], sv["t2"], sv["o_na"], do_na, f"na_bwd_{l}")
        dqs, dks, dvs, dbias, dsink = sw_bwd(sv["qs"], sv["ks"], sv["zq"], t5b, sw_sink[l], sv["o_sw"], do_sw,
                                             f"sw_bwd_{l}")
        dbias_sw.append(dbias.reshape(SW_HEADS, SW_BLOCK, 3 * SW_BLOCK))
        small["sw_sink"][l] = jnp.sum(dsink[:, 0].reshape(SW_HEADS, SW_BLOCK), axis=1)
        small["na_rpb"][l] = _rpb_from_rows(rpb_reduce(dt2, f"rpb_reduce_{l}"))
        dz, dgqa, dgka, dgqs, dgks = qk_norm_bwd(dqa, dka, dva, dqs, dks, dvs, sv["zq"], dzg, *sv["gains"], bd,
                                                 f"qk_norm_bwd_{l}")
        fold = lambda g: jnp.sum(g.reshape(-1, HEAD_DIM), axis=0)
        small["na_q_norm"][l], small["na_k_norm"][l] = fold(dgqa), fold(dgka)
        small["sw_q_norm"][l], small["sw_k_norm"][l] = fold(dgqs), fold(dgks)
        (gwin,) = tn_matmul([(dz, sv["hn"], 1.0)], f"dwin_{l}")
        pending[f"mix_{l}"], token = scatter_start([[split(gwout)], [split(gwna), split(gwsw)], [split(gwin)]],
                                                   f"scatter_mix_{l}")
        dx, dg = proj_bwd_norm([dz], [win_t], sv["x1"], mix_norm[l][None], dx, token, f"mix_bwd_x_{l}")
        small["mix_norm"][l] = dg[0]
        dx, tail = ffn_backward(dx, blocks[1])

    dtab = t5_reduce(dbias_sw, bmap, "t5_reduce")
    small_parts = {k: jnp.stack(v) for k, v in small.items()}
    small_parts["t5_rel_table"] = jnp.transpose(dtab[:, :, 0])

    grads, delta, new_m, new_v = {}, {}, {}, {}
    state = {}
    chain = [tail]
    members = {"ffn": lambda t: [(f"ffn{t}_w_gate", 0, 0, True), (f"ffn{t}_w_up", 0, 1, True),
                                 (f"ffn{t}_w_down", 0, 2, False)],
               "mix": lambda t: [("w_out", 0, 0, False), ("w_branch_na", 1, 0, True), ("w_branch_sw", 1, 1, True),
                                 ("w_in", 2, 0, True)]}

    def collect(key):
        if key in two_level:
            zones = [chip_wait(pending[key], chain[0], f"wait_{key}")]
        else:
            zones = scatter_wait(pending[key], chain[0], f"wait_{key}")
        kind, l = key.split("_")
        for k, zi, wi, transposed in members[kind[:3]](kind[3:]):
            view = tr if transposed else (lambda t: t)
            state[k] = adamw_layer(zones[zi], wi, int(l), view(weights[k]), view(mom_m[k]), view(mom_v[k]),
                                   state.get(k), chain[0], f"adamw_{k}_{l}")
            chain[0] = state[k][1]
            if all(f"{kind}_{j}" in done for j in range(depth) if j != int(l)):
                grads[k], delta[k], new_m[k], new_v[k] = (view(t) for t in state[k])
        done.add(key)

    done = set()
    for key in pending:
        if key != last_key:
            collect(key)
    collect(last_key)
    recvs = share_small([small_parts[k] for k in SMALL_NAMES], chain[0])
    results = adamw_small([weights[k] for k in SMALL_NAMES], recvs, [mom_m[k] for k in SMALL_NAMES],
                          [mom_v[k] for k in SMALL_NAMES], "adamw_small")
    for dst, outs in zip((grads, delta, new_m, new_v), results):
        dst.update(dict(zip(SMALL_NAMES, outs)))

    return (loss, dx[None], *[grads[k] for k in order], *[delta[k] for k in order],
            *[new_m[k] for k in order], *[new_v[k] for k in order])
```

```python
import functools
import math

import numpy as np
import jax
import jax.numpy as jnp
from jax import lax
from jax.experimental import pallas as pl
from jax.experimental.pallas import tpu as pltpu

F32 = jnp.float32
BF16 = jnp.bfloat16
MESH = pl.DeviceIdType.MESH

N_DEV = 8
EPS = 1e-6
NEG = -1e30
HEAD_DIM = 64
GRID_W = 64
NA_ROWS = 8
NA_COLS = 16
NA_WIDTH = 512
SW_Q_WIDTH = 512
SW_KV_WIDTH = 128
SW_BLOCK = 128
SW_HEADS = 8
SW_REP = 4
REL_BUCKETS = 32
REL_MAX_DIST = 128
QKV_WIDTH = 3 * NA_WIDTH + SW_Q_WIDTH + 2 * SW_KV_WIDTH
SCALE = 1.0 / math.sqrt(HEAD_DIM)

ADAM_LR = 0.001
ADAM_B1 = 0.9
ADAM_B2 = 0.999
ADAM_EPS = 1e-08
ADAM_WD = 0.01
ADAM_STEP = 10

V7X_VMEM_LIMIT = 56 * 1024 * 1024
LANES = 128
MXU_TILE = 256

NT = (((1,), (1,)), ((), ()))
TN = (((0,), (0,)), ((), ()))


def _params(n_grid=1):
    return pltpu.CompilerParams(dimension_semantics=("arbitrary",) * n_grid,
                                vmem_limit_bytes=V7X_VMEM_LIMIT)


def _row_tile(s):
    for t in (512, 256, 128, 64, 32, 16, 8):
        if s % t == 0:
            return t
    raise ValueError(s)


def _tn_tile(n):
    best = max(t for t in range(LANES, min(n, 2304) + 1, LANES) if n % t == 0) if n % LANES == 0 else n
    return best // 2 if best == n and n >= 1024 else best


ONCE = pl.Buffered(1)


def _col_chunk(n):
    return MXU_TILE if n % MXU_TILE == 0 else n


def _dot(a, b):
    return jnp.dot(a, b, preferred_element_type=F32)


def _dotg(a, b, dn):
    return lax.dot_general(a, b, dn, preferred_element_type=F32)


def _sigmoid(v):
    return 1.0 / (1.0 + jnp.exp(-v))


def _rstd(xv):
    return lax.rsqrt(jnp.mean(xv * xv, axis=-1, keepdims=True) + EPS)


def _full(shape):
    nd = len(shape)
    return pl.BlockSpec(shape, lambda i, _n=nd: (0,) * _n)


def _rows(tm, width):
    return pl.BlockSpec((tm, width), lambda i: (i, 0))


def _mat(stack, idx):
    return pl.BlockSpec((None,) + tuple(stack.shape[1:]), lambda i, _w=idx: (_w, 0, 0), pipeline_mode=ONCE)


def _group_mean(v, bd):
    w = bd.shape[0]
    if v.shape[1] > w:
        return jnp.concatenate([_group_mean(v[:, c0:c0 + w], bd) for c0 in range(0, v.shape[1], w)], axis=1)
    hi = v.astype(BF16)
    lo = (v - hi.astype(F32)).astype(BF16)
    return _dot(hi, bd) + _dot(lo, bd)


def ffn_forward(x, gain, wg_t, wu_t, wd, dep, name):
    s, d = x.shape
    f = wg_t[0].shape[1]
    tm = _row_tile(s) if wd is None else min(_row_tile(s), 256)
    fc = _col_chunk(f)
    nw = 2 if wd is None else 3

    def body(x_ref, g_ref, *refs):
        w_refs, outs = refs[:nw], refs[nw + 1:]
        xn_ref, dg_ref, du_ref, act_ref = outs[-4:]
        xv = x_ref[...]
        xn = (xv * _rstd(xv) * g_ref[...]).astype(BF16)
        xn_ref[...] = xn
        for c0 in range(0, f, fc):
            hg = _dotg(xn, w_refs[0][c0:c0 + fc, :], NT)
            hu = _dotg(xn, w_refs[1][c0:c0 + fc, :], NT)
            sg = _sigmoid(hg)
            silu = hg * sg
            du_ref[:, c0:c0 + fc] = silu.astype(BF16)
            dg_ref[:, c0:c0 + fc] = (hu * (sg + silu * (1.0 - sg))).astype(BF16)
            act_ref[:, c0:c0 + fc] = (silu * hu).astype(BF16)
        if wd is not None:
            outs[0][...] = xv + 0.5 * _dot(act_ref[...], w_refs[2][...])

    weights = [wg_t, wu_t] + ([] if wd is None else [wd])
    out_specs = [_rows(tm, d), _rows(tm, f), _rows(tm, f), _rows(tm, f)]
    out_shape = [jax.ShapeDtypeStruct((s, d), BF16)] + [jax.ShapeDtypeStruct((s, f), BF16)] * 3
    if wd is not None:
        out_specs, out_shape = [_rows(tm, d)] + out_specs, [jax.ShapeDtypeStruct((s, d), F32)] + out_shape
    return pl.pallas_call(
        body, name=name, grid=(s // tm,),
        in_specs=[_rows(tm, d), _full((1, d))] + [_mat(*w) for w in weights] + [_full(dep.shape)],
        out_specs=out_specs, out_shape=out_shape,
        compiler_params=_params(),
    )(x, gain, *[w[0] for w in weights], dep)


def ffn_down(x, act, wd, dep, name, target=None):
    s, d = x.shape
    f = act.shape[1]
    tm = _row_tile(s)

    def body(x_ref, a_ref, w_ref, dep_ref, *rest):
        y = x_ref[...] + 0.5 * _dot(a_ref[...], w_ref[...])
        if target is None:
            rest[0][...] = y
            return
        t_ref, dy_ref, acc_ref = rest

        @pl.when(pl.program_id(0) == 0)
        def _():
            acc_ref[...] = jnp.zeros(acc_ref.shape, F32)

        err = y - t_ref[...]
        dy_ref[...] = err * (1.0 / d)
        part = jnp.sum((err * err).reshape(tm // 8, 8, d), axis=0)
        acc = part[:, 0:LANES]
        for c0 in range(LANES, d, LANES):
            acc = acc + part[:, c0:c0 + LANES]
        acc_ref[...] = acc_ref[...] + acc

    ins = [_rows(tm, d), _rows(tm, f), _mat(*wd), _full(dep.shape)]
    if target is None:
        return pl.pallas_call(
            body, name=name, grid=(s // tm,), in_specs=ins, out_specs=_rows(tm, d),
            out_shape=jax.ShapeDtypeStruct((s, d), F32), compiler_params=_params(),
        )(x, act, wd[0], dep)
    return pl.pallas_call(
        body, name=name, grid=(s // tm,), in_specs=ins + [_rows(tm, d)],
        out_specs=[_rows(tm, d), _full((8, LANES))],
        out_shape=[jax.ShapeDtypeStruct((s, d), F32), jax.ShapeDtypeStruct((8, LANES), F32)],
        compiler_params=_params(),
    )(x, act, wd[0], dep, target)


def mix_in(x, gain, win_t, b_gate, gq_na, gk_na, gq_sw, gk_sw, bd, name):
    s, d = x.shape
    tm = _row_tile(s)
    gc = _col_chunk(2 * d)

    def body(x_ref, g_ref, w_ref, b_ref, gqa_ref, gka_ref, gqs_ref, gks_ref, bd_ref,
             hn_ref, zq_ref, qa_ref, ka_ref, qs_ref, ks_ref, gt_ref):
        xv = x_ref[...]
        hn = (xv * _rstd(xv) * g_ref[...]).astype(BF16)
        hn_ref[...] = hn

        def proj(c0, c1):
            return _dotg(hn, w_ref[c0:c1, :], NT)

        def headnorm(z, g, bdm):
            return z * lax.rsqrt(_group_mean(z * z, bdm) + EPS) * g

        bd512 = bd_ref[...]
        bd128 = bd_ref[0:SW_KV_WIDTH, 0:SW_KV_WIDTH]
        z = proj(0, 512)
        zq_ref[:, 0:512] = z.astype(BF16)
        qa_ref[...] = (headnorm(z, gqa_ref[...], bd512) * SCALE).astype(BF16)
        z = proj(512, 1024)
        zq_ref[:, 512:1024] = z.astype(BF16)
        ka_ref[...] = headnorm(z, gka_ref[...], bd512).astype(BF16)
        z = proj(1024, 1536)
        zq_ref[:, 1024:1536] = z.astype(BF16)
        z = proj(1536, 2048)
        zq_ref[:, 1536:2048] = z.astype(BF16)
        qs_ref[...] = (headnorm(z, gqs_ref[...], bd512) * SCALE).astype(BF16)
        z = proj(2048, 2176)
        zq_ref[:, 2048:2176] = z.astype(BF16)
        ks_ref[...] = headnorm(z, gks_ref[...], bd128).astype(BF16)
        z = proj(2176, 2304)
        zq_ref[:, 2176:2304] = z.astype(BF16)
        for c0 in range(0, 2 * d, gc):
            zg = proj(QKV_WIDTH + c0, QKV_WIDTH + c0 + gc) + b_ref[:, c0:c0 + gc]
            gt_ref[:, c0:c0 + gc] = _sigmoid(zg).astype(BF16)

    return pl.pallas_call(
        body, name=name, grid=(s // tm,),
        in_specs=[_rows(tm, d), _full((1, d)), _mat(*win_t), _full((1, 2 * d)),
                  _full((1, 512)), _full((1, 512)), _full((1, 512)), _full((1, 128)), _full((MXU_TILE, MXU_TILE))],
        out_specs=[_rows(tm, d), _rows(tm, QKV_WIDTH), _rows(tm, 512), _rows(tm, 512), _rows(tm, 512),
                   _rows(tm, 128), _rows(tm, 2 * d)],
        out_shape=[jax.ShapeDtypeStruct((s, d), BF16), jax.ShapeDtypeStruct((s, QKV_WIDTH), BF16),
                   jax.ShapeDtypeStruct((s, 512), BF16), jax.ShapeDtypeStruct((s, 512), BF16),
                   jax.ShapeDtypeStruct((s, 512), BF16), jax.ShapeDtypeStruct((s, 128), BF16),
                   jax.ShapeDtypeStruct((s, 2 * d), BF16)],
        compiler_params=_params(),
    )(x, gain, win_t[0], b_gate, gq_na, gk_na, gq_sw, gk_sw, bd)


def _na_iotas():
    qc = lax.broadcasted_iota(jnp.int32, (GRID_W, LANES), 0)
    ln = lax.broadcasted_iota(jnp.int32, (GRID_W, LANES), 1)
    low = ln < GRID_W
    kc = jnp.where(low, ln, ln - GRID_W)
    diff = kc - qc + (NA_COLS - 1)
    qcs = jnp.clip(qc - NA_COLS // 2, 0, GRID_W - NA_COLS)
    inwin = (kc >= qcs) & (kc < qcs + NA_COLS)
    return diff, low, inwin


NA_RI = 2 * NA_ROWS - 1
NA_CI = 2 * NA_COLS - 1
NA_T2 = NA_RI + 1


def _rpb_rows(rpb):
    h = rpb.shape[0]
    padded = jnp.pad(rpb, ((0, 0), (1, 1), (0, GRID_W - NA_CI)))
    return jnp.concatenate([padded[:, :NA_T2], padded[:, 1:NA_T2 + 1]], axis=2).reshape(h, NA_T2, LANES)


def _rpb_from_rows(rows):
    return rows[:, 1:, :NA_CI] + rows[:, :NA_RI, GRID_W:GRID_W + NA_CI]


def rpb_expand(rows, dep, name):
    n_heads = rows.shape[0]

    def body(r_ref, dep_ref, o_ref):
        for h in range(n_heads):
            for e in range(NA_T2):
                line = jnp.broadcast_to(r_ref[h, e:e + 1, :], (GRID_W, LANES))
                o_ref[h, e] = pltpu.roll(line, LANES - (NA_COLS - 1), 1, stride=1, stride_axis=0)

    return pl.pallas_call(
        body, name=name,
        in_specs=[pl.BlockSpec(memory_space=pltpu.VMEM), pl.BlockSpec(memory_space=pltpu.VMEM)],
        out_specs=pl.BlockSpec(memory_space=pltpu.VMEM),
        out_shape=jax.ShapeDtypeStruct((n_heads, NA_T2, GRID_W, LANES), F32),
        compiler_params=pltpu.CompilerParams(vmem_limit_bytes=V7X_VMEM_LIMIT),
    )(rows, dep)


def rpb_reduce(dt2, name):
    n_heads = dt2.shape[0]
    flip = jnp.asarray(np.eye(GRID_W)[::-1], BF16)

    def body(d_ref, j_ref, o_ref):
        jm = j_ref[...]
        for h in range(n_heads):
            for e in range(NA_T2):
                dv = d_ref[h, e]
                hi = dv.astype(BF16)
                mid = (dv - hi.astype(F32)).astype(BF16)
                lo = (dv - hi.astype(F32) - mid.astype(F32)).astype(BF16)
                rev = _dot(jm, hi) + _dot(jm, mid) + _dot(jm, lo)
                back = pltpu.roll(rev, LANES + (NA_COLS - 1) - (GRID_W - 1), 1, stride=1, stride_axis=0)
                o_ref[h, e:e + 1, :] = jnp.sum(back, axis=0, keepdims=True)

    return pl.pallas_call(
        body, name=name,
        in_specs=[pl.BlockSpec(memory_space=pltpu.VMEM)] * 2,
        out_specs=pl.BlockSpec(memory_space=pltpu.VMEM),
        out_shape=jax.ShapeDtypeStruct((n_heads, NA_T2, LANES), F32),
        compiler_params=pltpu.CompilerParams(vmem_limit_bytes=V7X_VMEM_LIMIT),
    )(dt2, flip)


NA_TQ = 4
NA_TK = NA_TQ + NA_ROWS
NA_KCH = NA_TK // 2


def _na_tile_geometry(t, rows):
    r = t * NA_TQ
    kbase = jnp.clip(r - NA_ROWS // 2, 0, rows - NA_TK)
    starts = [jnp.clip(r + a - NA_ROWS // 2, 0, rows - NA_ROWS) for a in range(NA_TQ)]
    return r, kbase, starts


def _na_tile_mask(kbase, starts, low, inwin):
    half = jnp.where(low, 0, 1)
    cols = []
    for c in range(NA_KCH):
        krow = kbase + 2 * c + half
        cols.append(jnp.concatenate(
            [jnp.where(inwin & (krow >= st) & (krow < st + NA_ROWS), 0.0, NEG) for st in starts], axis=0))
    return jnp.concatenate(cols, axis=1)


def _na_tile_index(r, kbase, a, c):
    return jnp.clip(kbase + 2 * c - (r + a) + NA_ROWS, 0, NA_T2 - 1)


def _na_tile_scores(q, k, t2_ref, hh, r, kbase, madd):
    bias = jnp.concatenate(
        [jnp.concatenate([t2_ref[hh, _na_tile_index(r, kbase, a, c)] for a in range(NA_TQ)], axis=0)
         for c in range(NA_KCH)], axis=1)
    return _dotg(q, k, NT) + bias + madd


def _softmax_rows(sc):
    e = jnp.exp(sc - jnp.max(sc, axis=1, keepdims=True))
    return e * (1.0 / jnp.sum(e, axis=1, keepdims=True))


def na_fwd(qa, ka, zq, t2, name):
    s = qa.shape[0]
    rows = s // GRID_W
    n_pairs = NA_WIDTH // LANES
    v_blk0 = (2 * NA_WIDTH) // LANES

    assert rows % NA_TQ == 0 and rows >= NA_TK
    tq, tk = NA_TQ * GRID_W, NA_TK * GRID_W

    def body(q_ref, k_ref, v_ref, t2_ref, o_ref, s_scr, p_scr):
        _, low, inwin = _na_iotas()

        def tile(t, carry):
            r, kbase, starts = _na_tile_geometry(t, rows)
            madd = _na_tile_mask(kbase, starts, low, inwin)
            qr = pl.ds(pl.multiple_of(r * GRID_W, tq), tq)
            kr = pl.ds(pl.multiple_of(kbase * GRID_W, tq), tk)
            for hh in range(2):
                lanes = slice(HEAD_DIM * hh, HEAD_DIM * (hh + 1))
                s_scr[tq * hh:tq * (hh + 1), :] = _na_tile_scores(q_ref[qr, lanes], k_ref[kr, lanes], t2_ref, hh, r,
                                                                  kbase, madd)
            p_scr[...] = _softmax_rows(s_scr[...]).astype(BF16)
            for hh in range(2):
                lanes = slice(HEAD_DIM * hh, HEAD_DIM * (hh + 1))
                o_ref[qr, lanes] = _dot(p_scr[tq * hh:tq * (hh + 1), :], v_ref[kr, lanes]).astype(BF16)
            return carry

        lax.fori_loop(0, rows // NA_TQ, tile, 0)

    col = lambda off: pl.BlockSpec((s, LANES), lambda p, _o=off: (0, _o + p))
    return pl.pallas_call(
        body, name=name, grid=(n_pairs,),
        in_specs=[col(0), col(0), col(v_blk0),
                  pl.BlockSpec((2, NA_T2, GRID_W, LANES), lambda p: (p, 0, 0, 0))],
        out_specs=col(0),
        out_shape=jax.ShapeDtypeStruct((s, NA_WIDTH), BF16),
        scratch_shapes=[pltpu.VMEM((2 * tq, tk), F32), pltpu.VMEM((2 * tq, tk), BF16)],
        compiler_params=_params(),
    )(qa, ka, zq, t2)


def na_bwd(qa, ka, zq, t2, o_na, do_na, name):
    s = qa.shape[0]
    rows = s // GRID_W
    n_pairs = NA_WIDTH // LANES
    v_blk0 = (2 * NA_WIDTH) // LANES

    tq, tk = NA_TQ * GRID_W, NA_TK * GRID_W

    def body(q_ref, k_ref, v_ref, t2_ref, o_ref, do_ref, dq_ref, dk_ref, dv_ref, dt2_ref):
        _, low, inwin = _na_iotas()
        dk_ref[...] = jnp.zeros(dk_ref.shape, F32)
        dv_ref[...] = jnp.zeros(dv_ref.shape, F32)
        dt2_ref[...] = jnp.zeros(dt2_ref.shape, F32)

        def tile(t, carry):
            r, kbase, starts = _na_tile_geometry(t, rows)
            madd = _na_tile_mask(kbase, starts, low, inwin)
            qr = pl.ds(pl.multiple_of(r * GRID_W, tq), tq)
            kr = pl.ds(pl.multiple_of(kbase * GRID_W, tq), tk)
            for hh in range(2):
                lanes = slice(HEAD_DIM * hh, HEAD_DIM * (hh + 1))
                q, k, v = q_ref[qr, lanes], k_ref[kr, lanes], v_ref[kr, lanes]
                p = _softmax_rows(_na_tile_scores(q, k, t2_ref, hh, r, kbase, madd))
                do = do_ref[qr, lanes]
                delta = jnp.sum(do.astype(F32) * o_ref[qr, lanes].astype(F32), axis=1, keepdims=True)
                ds = p * (_dotg(do, v, NT) - delta)
                shared = {}
                for a in range(NA_TQ):
                    for c in range(NA_KCH):
                        shared.setdefault(2 * c - a, []).append(
                            ds[GRID_W * a:GRID_W * (a + 1), LANES * c:LANES * (c + 1)])
                for offset, parts in shared.items():
                    e = jnp.clip(offset + kbase - r + NA_ROWS, 0, NA_T2 - 1)
                    dt2_ref[hh, e] = dt2_ref[hh, e] + functools.reduce(jnp.add, parts)
                dsb = ds.astype(BF16)
                dq_ref[qr, lanes] = _dot(dsb, k)
                dk_ref[kr, lanes] = dk_ref[kr, lanes] + _dotg(dsb, q, TN)
                dv_ref[kr, lanes] = dv_ref[kr, lanes] + _dotg(p.astype(BF16), do, TN)
            return carry

        lax.fori_loop(0, rows // NA_TQ, tile, 0)

    col = lambda off: pl.BlockSpec((s, LANES), lambda p, _o=off: (0, _o + p))
    t2spec = pl.BlockSpec((2, NA_T2, GRID_W, LANES), lambda p: (p, 0, 0, 0))
    return pl.pallas_call(
        body, name=name, grid=(n_pairs,),
        in_specs=[col(0), col(0), col(v_blk0), t2spec, col(0), col(0)],
        out_specs=[col(0), col(0), col(0), t2spec],
        out_shape=[jax.ShapeDtypeStruct((s, NA_WIDTH), F32)] * 3 + [jax.ShapeDtypeStruct(t2.shape, F32)],
        compiler_params=_params(),
    )(qa, ka, zq, t2, o_na, do_na)


def _t5_bucket_map():
    rel = np.arange(3 * SW_BLOCK)[None, :] - SW_BLOCK - np.arange(SW_BLOCK)[:, None]
    nb = REL_BUCKETS // 2
    max_exact = nb // 2
    n = np.abs(rel)
    large = max_exact + (np.log(np.maximum(n, 1) / max_exact)
                         / np.log(REL_MAX_DIST / max_exact) * (nb - max_exact)).astype(np.int32)
    large = np.minimum(large, nb - 1)
    return ((rel > 0) * nb + np.where(n < max_exact, n, large)).astype(np.int32)


def t5_expand(table, bmap, dep, name):
    def body(tab_ref, bm_ref, dep_ref, o_ref):
        bm = bm_ref[...]
        for h in range(SW_HEADS):
            t = jnp.zeros(bm.shape, F32)
            for b in range(REL_BUCKETS):
                t = jnp.where(bm == b, tab_ref[b, h], t)
            o_ref[h] = t

    return pl.pallas_call(
        body, name=name,
        in_specs=[pl.BlockSpec(memory_space=pltpu.SMEM), pl.BlockSpec(memory_space=pltpu.VMEM),
                  pl.BlockSpec(memory_space=pltpu.VMEM)],
        out_specs=pl.BlockSpec(memory_space=pltpu.VMEM),
        out_shape=jax.ShapeDtypeStruct((SW_HEADS,) + bmap.shape, F32),
        compiler_params=pltpu.CompilerParams(vmem_limit_bytes=V7X_VMEM_LIMIT),
    )(table, bmap, dep)


def t5_reduce(dbias_list, bmap, name):
    n = len(dbias_list)

    def body(*refs):
        d_refs, bm_ref, o_ref = refs[:n], refs[n], refs[n + 1]
        bm = bm_ref[...]
        for h in range(SW_HEADS):
            dv = d_refs[0][h]
            for other in d_refs[1:]:
                dv = dv + other[h]
            rows = [jnp.sum(jnp.where(bm == b, dv, 0.0), axis=0, keepdims=True) for b in range(REL_BUCKETS)]
            r = jnp.concatenate(rows, axis=0)
            o_ref[h] = jnp.broadcast_to(jnp.sum(r, axis=1, keepdims=True), (REL_BUCKETS, LANES))

    return pl.pallas_call(
        body, name=name,
        in_specs=[pl.BlockSpec(memory_space=pltpu.VMEM)] * (n + 1),
        out_specs=pl.BlockSpec(memory_space=pltpu.VMEM),
        out_shape=jax.ShapeDtypeStruct((SW_HEADS, REL_BUCKETS, LANES), F32),
        compiler_params=pltpu.CompilerParams(vmem_limit_bytes=V7X_VMEM_LIMIT),
    )(*dbias_list, bmap)


def _sw_mask_iotas():
    a = lax.broadcasted_iota(jnp.int32, (SW_BLOCK, 3 * SW_BLOCK), 0)
    j = lax.broadcasted_iota(jnp.int32, (SW_BLOCK, 3 * SW_BLOCK), 1)
    inwin = jnp.abs(j - SW_BLOCK - a) <= SW_BLOCK
    return j, inwin


SW_STACK = SW_HEADS * SW_BLOCK


def _sw_softmax(sc, sk):
    m = jnp.maximum(jnp.max(sc, axis=1, keepdims=True), sk)
    e = jnp.exp(sc - m)
    es = jnp.exp(sk - m)
    inv = 1.0 / (jnp.sum(e, axis=1, keepdims=True) + es)
    return e * inv, es * inv


def _sw_prologue(k_ref, v_ref, kp, vp, sink_ref, s):
    pad = s + 2 * SW_BLOCK
    zeros = jnp.zeros((SW_BLOCK, SW_KV_WIDTH), BF16)
    kp[0:SW_BLOCK, :] = zeros
    vp[0:SW_BLOCK, :] = zeros
    kp[SW_BLOCK + s:pad, :] = zeros
    vp[SW_BLOCK + s:pad, :] = zeros
    kp[SW_BLOCK:SW_BLOCK + s, :] = k_ref[...]
    vp[SW_BLOCK:SW_BLOCK + s, :] = v_ref[...]
    return jnp.concatenate([jnp.full((SW_BLOCK, 1), sink_ref[h], F32) for h in range(SW_HEADS)], axis=0)


def sw_fwd(qs, ks, zq, t5b, sink, dep, name):
    s = qs.shape[0]
    nb = s // SW_BLOCK
    v_blk = (3 * NA_WIDTH + SW_Q_WIDTH + SW_KV_WIDTH) // LANES
    pad = s + 2 * SW_BLOCK

    def body(q_ref, k_ref, v_ref, b_ref, sink_ref, dep_ref, o_ref, kp, vp, s_scr, p_scr):
        sink_col = _sw_prologue(k_ref, v_ref, kp, vp, sink_ref, s)
        j, inwin = _sw_mask_iotas()

        def blk(n, carry):
            kpos = n * SW_BLOCK - SW_BLOCK + j
            madd = jnp.where(inwin & (kpos >= 0) & (kpos < s), 0.0, NEG)
            q0 = pl.multiple_of(n * SW_BLOCK, SW_BLOCK)
            qr, kr = pl.ds(q0, SW_BLOCK), pl.ds(q0, 3 * SW_BLOCK)
            for h in range(SW_HEADS):
                g = h // SW_REP
                s_scr[SW_BLOCK * h:SW_BLOCK * (h + 1), :] = _dotg(
                    q_ref[qr, HEAD_DIM * h:HEAD_DIM * (h + 1)], kp[kr, HEAD_DIM * g:HEAD_DIM * (g + 1)], NT) + madd
            p, _ = _sw_softmax(s_scr[...] + b_ref[...], sink_col)
            p_scr[...] = p.astype(BF16)
            for h in range(SW_HEADS):
                g = h // SW_REP
                o_ref[qr, HEAD_DIM * h:HEAD_DIM * (h + 1)] = _dot(
                    p_scr[SW_BLOCK * h:SW_BLOCK * (h + 1), :], vp[kr, HEAD_DIM * g:HEAD_DIM * (g + 1)]).astype(BF16)
            return carry

        lax.fori_loop(0, nb, blk, 0)

    return pl.pallas_call(
        body, name=name, grid=(1,),
        in_specs=[_full((s, SW_Q_WIDTH)), _full((s, SW_KV_WIDTH)),
                  pl.BlockSpec((s, SW_KV_WIDTH), lambda i: (0, v_blk)),
                  _full((SW_STACK, 3 * SW_BLOCK)), pl.BlockSpec(memory_space=pltpu.SMEM),
                  _full(dep.shape)],
        out_specs=_full((s, SW_Q_WIDTH)),
        out_shape=jax.ShapeDtypeStruct((s, SW_Q_WIDTH), BF16),
        scratch_shapes=[pltpu.VMEM((pad, SW_KV_WIDTH), BF16), pltpu.VMEM((pad, SW_KV_WIDTH), BF16),
                        pltpu.VMEM((SW_STACK, 3 * SW_BLOCK), F32), pltpu.VMEM((SW_STACK, 3 * SW_BLOCK), BF16)],
        compiler_params=_params(),
    )(qs, ks, zq, t5b, sink, dep)


def sw_bwd(qs, ks, zq, t5b, sink, o_sw, do_sw, name):
    s = qs.shape[0]
    nb = s // SW_BLOCK
    v_blk = (3 * NA_WIDTH + SW_Q_WIDTH + SW_KV_WIDTH) // LANES
    pad = s + 2 * SW_BLOCK

    def body(q_ref, k_ref, v_ref, b_ref, sink_ref, o_ref, do_ref,
             dq_ref, dk_ref, dv_ref, db_ref, dsk_ref, kp, vp, dkp, dvp, s_scr, dp_scr, ds_scr, p_scr):
        sink_col = _sw_prologue(k_ref, v_ref, kp, vp, sink_ref, s)
        dkp[...] = jnp.zeros(dkp.shape, F32)
        dvp[...] = jnp.zeros(dvp.shape, F32)
        db_ref[...] = jnp.zeros(db_ref.shape, F32)
        dsk_ref[...] = jnp.zeros(dsk_ref.shape, F32)
        j, inwin = _sw_mask_iotas()

        def blk(n, carry):
            kpos = n * SW_BLOCK - SW_BLOCK + j
            madd = jnp.where(inwin & (kpos >= 0) & (kpos < s), 0.0, NEG)
            q0 = pl.multiple_of(n * SW_BLOCK, SW_BLOCK)
            qr, kr = pl.ds(q0, SW_BLOCK), pl.ds(q0, 3 * SW_BLOCK)
            deltas = []
            for h in range(SW_HEADS):
                g = h // SW_REP
                hl, kl = slice(HEAD_DIM * h, HEAD_DIM * (h + 1)), slice(HEAD_DIM * g, HEAD_DIM * (g + 1))
                rows = slice(SW_BLOCK * h, SW_BLOCK * (h + 1))
                do = do_ref[qr, hl]
                s_scr[rows, :] = _dotg(q_ref[qr, hl], kp[kr, kl], NT) + madd
                dp_scr[rows, :] = _dotg(do, vp[kr, kl], NT)
                deltas.append(jnp.sum(do.astype(F32) * o_ref[qr, hl].astype(F32), axis=1, keepdims=True))
            delta = jnp.concatenate(deltas, axis=0)
            p, ps = _sw_softmax(s_scr[...] + b_ref[...], sink_col)
            ds = p * (dp_scr[...] - delta)
            db_ref[...] = db_ref[...] + ds
            dsk_ref[...] = dsk_ref[...] - jnp.broadcast_to(ps * delta, (SW_STACK, LANES))
            ds_scr[...] = ds.astype(BF16)
            p_scr[...] = p.astype(BF16)
            for g in range(SW_HEADS // SW_REP):
                kl = slice(HEAD_DIM * g, HEAD_DIM * (g + 1))
                k = kp[kr, kl]
                dkw = jnp.zeros((3 * SW_BLOCK, HEAD_DIM), F32)
                dvw = jnp.zeros((3 * SW_BLOCK, HEAD_DIM), F32)
                for r in range(SW_REP):
                    h = g * SW_REP + r
                    hl, rows = slice(HEAD_DIM * h, HEAD_DIM * (h + 1)), slice(SW_BLOCK * h, SW_BLOCK * (h + 1))
                    dsb = ds_scr[rows, :]
                    dq_ref[qr, hl] = _dot(dsb, k)
                    dkw = dkw + _dotg(dsb, q_ref[qr, hl], TN)
                    dvw = dvw + _dotg(p_scr[rows, :], do_ref[qr, hl], TN)
                dkp[kr, kl] = dkp[kr, kl] + dkw
                dvp[kr, kl] = dvp[kr, kl] + dvw
            return carry

        lax.fori_loop(0, nb, blk, 0)
        dk_ref[...] = dkp[SW_BLOCK:SW_BLOCK + s, :]
        dv_ref[...] = dvp[SW_BLOCK:SW_BLOCK + s, :]

    bias_spec = _full((SW_STACK, 3 * SW_BLOCK))
    return pl.pallas_call(
        body, name=name, grid=(1,),
        in_specs=[_full((s, SW_Q_WIDTH)), _full((s, SW_KV_WIDTH)),
                  pl.BlockSpec((s, SW_KV_WIDTH), lambda i: (0, v_blk)),
                  bias_spec, pl.BlockSpec(memory_space=pltpu.SMEM),
                  _full((s, SW_Q_WIDTH)), _full((s, SW_Q_WIDTH))],
        out_specs=[_full((s, SW_Q_WIDTH)), _full((s, SW_KV_WIDTH)), _full((s, SW_KV_WIDTH)), bias_spec,
                   _full((SW_STACK, LANES))],
        out_shape=[jax.ShapeDtypeStruct((s, SW_Q_WIDTH), F32), jax.ShapeDtypeStruct((s, SW_KV_WIDTH), F32),
                   jax.ShapeDtypeStruct((s, SW_KV_WIDTH), F32),
                   jax.ShapeDtypeStruct((SW_STACK, 3 * SW_BLOCK), F32),
                   jax.ShapeDtypeStruct((SW_STACK, LANES), F32)],
        scratch_shapes=[pltpu.VMEM((pad, SW_KV_WIDTH), BF16), pltpu.VMEM((pad, SW_KV_WIDTH), BF16),
                        pltpu.VMEM((pad, SW_KV_WIDTH), F32), pltpu.VMEM((pad, SW_KV_WIDTH), F32),
                        pltpu.VMEM((SW_STACK, 3 * SW_BLOCK), F32), pltpu.VMEM((SW_STACK, 3 * SW_BLOCK), F32),
                        pltpu.VMEM((SW_STACK, 3 * SW_BLOCK), BF16), pltpu.VMEM((SW_STACK, 3 * SW_BLOCK), BF16)],
        compiler_params=_params(),
    )(qs, ks, zq, t5b, sink, o_sw, do_sw)


def merge_out(x, o_na, o_sw, gt, wbna_t, wbsw_t, wout, name):
    s, d = x.shape
    tm = _row_tile(s)

    def body(x_ref, ona_ref, osw_ref, gt_ref, wna_ref, wsw_ref, wo_ref, xo_ref, ana_ref, asw_ref, mg_ref):
        a_na = _dotg(ona_ref[...], wna_ref[...], NT)
        a_sw = _dotg(osw_ref[...], wsw_ref[...], NT)
        g_na, g_sw = gt_ref[:, 0:d].astype(F32), gt_ref[:, d:2 * d].astype(F32)
        ana_ref[...] = (a_na * g_na * (1.0 - g_na)).astype(BF16)
        asw_ref[...] = (a_sw * g_sw * (1.0 - g_sw)).astype(BF16)
        merged = (g_na * a_na + g_sw * a_sw).astype(BF16)
        mg_ref[...] = merged
        xo_ref[...] = x_ref[...] + _dot(merged, wo_ref[...])

    return pl.pallas_call(
        body, name=name, grid=(s // tm,),
        in_specs=[_rows(tm, d), _rows(tm, 512), _rows(tm, 512), _rows(tm, 2 * d),
                  _mat(*wbna_t), _mat(*wbsw_t), _mat(*wout)],
        out_specs=[_rows(tm, d)] * 4,
        out_shape=[jax.ShapeDtypeStruct((s, d), F32)] + [jax.ShapeDtypeStruct((s, d), BF16)] * 3,
        compiler_params=_params(),
    )(x, o_na, o_sw, gt, wbna_t[0], wbsw_t[0], wout[0])


def mix_bwd_out(dx, gt, a_na, a_sw, wbna_t, wbsw_t, wout, dep, name):
    s, d = dx.shape
    tm = _row_tile(s)

    def body(dx_ref, gt_ref, ana_ref, asw_ref, wna_ref, wsw_ref, wo_ref, dep_ref,
             dxb_ref, dzg_ref, dana_ref, dasw_ref, dona_ref, dosw_ref, dbg_ref):
        @pl.when(pl.program_id(0) == 0)
        def _():
            dbg_ref[...] = jnp.zeros(dbg_ref.shape, F32)

        dxb = dx_ref[...].astype(BF16)
        dxb_ref[...] = dxb
        dm = _dotg(dxb, wo_ref[...], NT)
        for i, (a_ref, da_ref, w_ref, do_ref) in enumerate(
                [(ana_ref, dana_ref, wna_ref, dona_ref), (asw_ref, dasw_ref, wsw_ref, dosw_ref)]):
            gi = gt_ref[:, i * d:(i + 1) * d].astype(F32)
            da = (dm * gi).astype(BF16)
            da_ref[...] = da
            do_ref[...] = _dot(da, w_ref[...]).astype(BF16)
            dzg = dm * a_ref[...].astype(F32)
            dzg_ref[:, i * d:(i + 1) * d] = dzg.astype(BF16)
            dbg_ref[:, i * d:(i + 1) * d] = dbg_ref[:, i * d:(i + 1) * d] + jnp.sum(dzg, axis=0, keepdims=True)

    return pl.pallas_call(
        body, name=name, grid=(s // tm,),
        in_specs=[_rows(tm, d), _rows(tm, 2 * d), _rows(tm, d), _rows(tm, d),
                  _mat(*wbna_t), _mat(*wbsw_t), _mat(*wout), _full(dep.shape)],
        out_specs=[_rows(tm, d), _rows(tm, 2 * d), _rows(tm, d), _rows(tm, d), _rows(tm, 512), _rows(tm, 512),
                   _full((1, 2 * d))],
        out_shape=[jax.ShapeDtypeStruct((s, d), BF16), jax.ShapeDtypeStruct((s, 2 * d), BF16),
                   jax.ShapeDtypeStruct((s, d), BF16), jax.ShapeDtypeStruct((s, d), BF16),
                   jax.ShapeDtypeStruct((s, 512), BF16), jax.ShapeDtypeStruct((s, 512), BF16),
                   jax.ShapeDtypeStruct((1, 2 * d), F32)],
        compiler_params=_params(),
    )(dx, gt, a_na, a_sw, wbna_t[0], wbsw_t[0], wout[0], dep)


def qk_norm_bwd(dqa, dka, dva, dqs, dks, dvs, zq, dzg, gq_na, gk_na, gq_sw, gk_sw, bd, name):
    s = zq.shape[0]
    d2 = dzg.shape[1]
    n_in = QKV_WIDTH + d2
    tm = _row_tile(s)

    def body(dqa_ref, dka_ref, dva_ref, dqs_ref, dks_ref, dvs_ref, zq_ref, dzg_ref,
             gqa_ref, gka_ref, gqs_ref, gks_ref, bd_ref, dz_ref, dgqa_ref, dgka_ref, dgqs_ref, dgks_ref):
        @pl.when(pl.program_id(0) == 0)
        def _():
            for r in (dgqa_ref, dgka_ref, dgqs_ref, dgks_ref):
                r[...] = jnp.zeros(r.shape, F32)

        bd512 = bd_ref[...]
        bd128 = bd_ref[0:SW_KV_WIDTH, 0:SW_KV_WIDTH]

        def one(c0, c1, dy_ref, g_ref, dg_ref, bdm, scale):
            z = zq_ref[:, c0:c1].astype(F32)
            r = lax.rsqrt(_group_mean(z * z, bdm) + EPS)
            zh = z * r
            dy = dy_ref[...] * scale
            dyg = dy * g_ref[...]
            dz = r * (dyg - zh * _group_mean(dyg * zh, bdm))
            dz_ref[:, c0:c1] = dz.astype(BF16)
            dg_ref[...] = dg_ref[...] + jnp.sum(dy * zh, axis=0, keepdims=True)

        one(0, 512, dqa_ref, gqa_ref, dgqa_ref, bd512, SCALE)
        one(512, 1024, dka_ref, gka_ref, dgka_ref, bd512, 1.0)
        dz_ref[:, 1024:1536] = dva_ref[...].astype(BF16)
        one(1536, 2048, dqs_ref, gqs_ref, dgqs_ref, bd512, SCALE)
        one(2048, 2176, dks_ref, gks_ref, dgks_ref, bd128, 1.0)
        dz_ref[:, 2176:2304] = dvs_ref[...].astype(BF16)
        dz_ref[:, QKV_WIDTH:n_in] = dzg_ref[...]

    return pl.pallas_call(
        body, name=name, grid=(s // tm,),
        in_specs=[_rows(tm, 512), _rows(tm, 512), _rows(tm, 512), _rows(tm, 512), _rows(tm, 128), _rows(tm, 128),
                  _rows(tm, QKV_WIDTH), _rows(tm, d2),
                  _full((1, 512)), _full((1, 512)), _full((1, 512)), _full((1, 128)), _full((MXU_TILE, MXU_TILE))],
        out_specs=[_rows(tm, n_in), _full((1, 512)), _full((1, 512)), _full((1, 512)), _full((1, 128))],
        out_shape=[jax.ShapeDtypeStruct((s, n_in), BF16)] + [jax.ShapeDtypeStruct((1, 512), F32)] * 3
                  + [jax.ShapeDtypeStruct((1, 128), F32)],
        compiler_params=_params(),
    )(dqa, dka, dva, dqs, dks, dvs, zq, dzg, gq_na, gk_na, gq_sw, gk_sw, bd)


def ffn_bwd_act(dx, wd, hg, hu, name):
    s, d = dx.shape
    f = wd[0].shape[1]
    tm = _row_tile(s)
    fc = _col_chunk(f)

    def body(dx_ref, w_ref, hg_ref, hu_ref, dxb_ref, dhg_ref, dhu_ref):
        dxv = dx_ref[...]
        dxb_ref[...] = dxv.astype(BF16)
        half = (0.5 * dxv).astype(BF16)
        for c0 in range(0, f, fc):
            dact = _dotg(half, w_ref[c0:c0 + fc, :], NT)
            dhu_ref[:, c0:c0 + fc] = (dact * hu_ref[:, c0:c0 + fc].astype(F32)).astype(BF16)
            dhg_ref[:, c0:c0 + fc] = (dact * hg_ref[:, c0:c0 + fc].astype(F32)).astype(BF16)

    return pl.pallas_call(
        body, name=name, grid=(s // tm,),
        in_specs=[_rows(tm, d), _mat(*wd), _rows(tm, f), _rows(tm, f)],
        out_specs=[_rows(tm, d), _rows(tm, f), _rows(tm, f)],
        out_shape=[jax.ShapeDtypeStruct((s, d), BF16), jax.ShapeDtypeStruct((s, f), BF16),
                   jax.ShapeDtypeStruct((s, f), BF16)],
        compiler_params=_params(),
    )(dx, wd[0], hg, hu)


def proj_bwd_norm(acts, weights, x, gain, dx, dep, name):
    s, d = x.shape
    tm = min(_row_tile(s), 256)
    n = len(acts)

    def body(*refs):
        a_refs, w_refs = refs[:n], refs[n:2 * n]
        x_ref, g_ref, dx_ref, _, o_ref, dg_ref = refs[2 * n:]

        @pl.when(pl.program_id(0) == 0)
        def _():
            dg_ref[...] = jnp.zeros(dg_ref.shape, F32)

        dxn = _dot(a_refs[0][...], w_refs[0][...])
        for a_ref, w_ref in zip(a_refs[1:], w_refs[1:]):
            dxn = dxn + _dot(a_ref[...], w_ref[...])
        xv = x_ref[...]
        r = _rstd(xv)
        xh = xv * r
        dxh = dxn * g_ref[...]
        o_ref[...] = dx_ref[...] + r * (dxh - xh * jnp.mean(dxh * xh, axis=-1, keepdims=True))
        dg_ref[...] = dg_ref[...] + jnp.sum(dxn * xh, axis=0, keepdims=True)

    return pl.pallas_call(
        body, name=name, grid=(s // tm,),
        in_specs=[_rows(tm, a.shape[1]) for a in acts] + [_mat(*w) for w in weights]
                 + [_rows(tm, d), _full((1, d)), _rows(tm, d), _full(dep.shape)],
        out_specs=[_rows(tm, d), _full((1, d))],
        out_shape=[jax.ShapeDtypeStruct((s, d), F32), jax.ShapeDtypeStruct((1, d), F32)],
        compiler_params=_params(),
    )(*acts, *[w[0] for w in weights], x, gain, dx, dep)


def tn_matmul(products, name):
    s, n = products[0][0].shape
    tn = _tn_tile(n) if len(products) == 1 else _col_chunk(n)
    rhs = []
    for _, b, _ in products:
        if not any(b is seen for seen in rhs):
            rhs.append(b)
    which = [next(i for i, seen in enumerate(rhs) if b is seen) for _, b, _ in products]
    npr, nr = len(products), len(rhs)

    def body(*refs):
        a_refs, b_refs, o_refs = refs[:npr], refs[npr:npr + nr], refs[npr + nr:]
        for i, (_, _, scale) in enumerate(products):
            o_refs[i][...] = (scale * _dotg(a_refs[i][...], b_refs[which[i]][...], TN)).astype(BF16)

    return pl.pallas_call(
        body, name=name, grid=(n // tn,),
        in_specs=[pl.BlockSpec((s, tn), lambda i: (0, i))] * npr
                 + [pl.BlockSpec(b.shape, lambda i: (0, 0), pipeline_mode=ONCE) for b in rhs],
        out_specs=[pl.BlockSpec((tn, b.shape[1]), lambda i: (i, 0)) for _, b, _ in products],
        out_shape=[jax.ShapeDtypeStruct((n, b.shape[1]), BF16) for _, b, _ in products],
        compiler_params=_params(),
    )(*[a for a, _, _ in products], *rhs)


def _mesh_pos():
    return lax.axis_index("x"), lax.axis_index("y"), lax.axis_index("c")


def _peers():
    x, y, c = _mesh_pos()
    peers = []
    for rel in range(1, N_DEV):
        peers.append((1 - x if rel & 4 else x, 1 - y if rel & 2 else y, 1 - c if rel & 1 else c))
    return 4 * x + 2 * y + c, peers


HBM_SPEC = pl.BlockSpec(memory_space=pltpu.HBM)
SEM_SPEC = pl.BlockSpec(memory_space=pltpu.SEMAPHORE)


def _split_call(body, name, thru, n_sems, extra=(), with_token=True):
    hbm = lambda t: pltpu.with_memory_space_constraint(t, pltpu.HBM)
    effect = pltpu.CompilerParams(has_side_effects=pltpu.SideEffectType.DATAFLOW_SIDE_EFFECTING)
    nt = len(thru)
    thru_shapes = [pltpu.HBM(t.shape, t.dtype) for t in thru]
    if with_token:
        (after,) = extra
        outs = pl.pallas_call(
            body, name=name, in_specs=[HBM_SPEC] * nt + [pl.BlockSpec(memory_space=pl.ANY)],
            out_specs=[SEM_SPEC] * len(n_sems) + [HBM_SPEC] * nt + [pl.BlockSpec(memory_space=pltpu.VMEM)],
            out_shape=[pltpu.SemaphoreType.DMA((k,)) for k in n_sems] + thru_shapes
                      + [jax.ShapeDtypeStruct((8, LANES), F32)],
            input_output_aliases={i: len(n_sems) + i for i in range(nt)}, compiler_params=effect,
        )(*[hbm(t) for t in thru], after)
        return outs[:len(n_sems)], outs[len(n_sems):-1], outs[-1]
    return pl.pallas_call(
        body, name=name,
        in_specs=[HBM_SPEC] * nt + [SEM_SPEC] * len(n_sems) + [pl.BlockSpec(memory_space=pl.ANY)],
        out_specs=[HBM_SPEC] * nt, out_shape=thru_shapes,
        input_output_aliases={i: i for i in range(nt)}, compiler_params=effect,
    )(*thru, *extra)


def _gather_targets():
    x, y, c = _mesh_pos()
    return 4 * x + 2 * y + c, [(x, y, 1 - c), (1 - x, y, c), (x, 1 - y, c), (1 - x, 1 - y, c)]


def gather_start(shards, after, name):
    n = len(shards)
    zones = [lax.empty((w.shape[0], N_DEV) + w.shape[1:], w.dtype) for w in shards]

    def body(*refs):
        ins, zs = refs[:n], refs[n:2 * n]
        send_sems, recv_sems, local_sems = refs[2 * n + 1:2 * n + 4]
        token = refs[-1]
        me, targets = _gather_targets()
        for a in range(n):
            pltpu.make_async_copy(ins[a], zs[a].at[:, me], local_sems.at[a]).start()
            for k, to in enumerate(targets):
                pltpu.make_async_remote_copy(
                    src_ref=ins[a], dst_ref=zs[a].at[:, me], send_sem=send_sems.at[4 * a + k],
                    recv_sem=recv_sems.at[4 * a + k], device_id=to, device_id_type=MESH).start()
        token[...] = jnp.zeros(token.shape, F32)

    sems, thru, token = _split_call(body, name, list(shards) + zones, (4 * n, 4 * n, n), extra=(after,))
    return (sems, thru, n), token


def gather_wait(started, after, name):
    sems, thru, n = started

    def body(*refs):
        zs = refs[n:2 * n]
        send_sems, recv_sems, local_sems = refs[2 * n:2 * n + 3]
        _, targets = _gather_targets()
        for a in range(n):
            for k, to in enumerate(targets):
                cp = pltpu.make_async_remote_copy(
                    src_ref=zs[a].at[:, 0], dst_ref=zs[a].at[:, 0], send_sem=send_sems.at[4 * a + k],
                    recv_sem=recv_sems.at[4 * a + k], device_id=to, device_id_type=MESH)
                cp.wait_send()
                cp.wait_recv()
            pltpu.make_async_copy(zs[a].at[:, 0], zs[a].at[:, 0], local_sems.at[a]).wait()

    return _split_call(body, name, thru, (4 * n, 4 * n, n), extra=(*sems, after), with_token=False)[n:]


def forward_start(zones, after, name):
    n = len(zones)

    def body(*refs):
        zs = refs[:n]
        send_sems, recv_sems = refs[n + 1:n + 3]
        token = refs[-1]
        x, y, c = _mesh_pos()
        for a in range(n):
            for j, chip in enumerate([(1 - x, y), (x, 1 - y), (1 - x, 1 - y)]):
                blk = zs[a].at[:, 4 * chip[0] + 2 * chip[1] + c]
                pltpu.make_async_remote_copy(
                    src_ref=blk, dst_ref=blk, send_sem=send_sems.at[3 * a + j], recv_sem=recv_sems.at[3 * a + j],
                    device_id=(x, y, 1 - c), device_id_type=MESH).start()
        token[...] = jnp.zeros(token.shape, F32)

    sems, thru, token = _split_call(body, name, list(zones), (3 * n, 3 * n), extra=(after,))
    return (sems, thru, n), token


def forward_wait(started, after, name):
    sems, thru, n = started

    def body(*refs):
        zs = refs[:n]
        send_sems, recv_sems = refs[n:n + 2]
        x, y, c = _mesh_pos()
        for a in range(n):
            for j in range(3):
                cp = pltpu.make_async_remote_copy(
                    src_ref=zs[a].at[:, 0], dst_ref=zs[a].at[:, 0], send_sem=send_sems.at[3 * a + j],
                    recv_sem=recv_sems.at[3 * a + j], device_id=(x, y, 1 - c), device_id_type=MESH)
                cp.wait_send()
                cp.wait_recv()

    return _split_call(body, name, thru, (3 * n, 3 * n), extra=(*sems, after), with_token=False)


def scatter_start(groups, name):
    n = len(groups)
    flat = [g for grp in groups for g in grp]
    nf = len(flat)
    offs = np.cumsum([0] + [len(grp) for grp in groups])
    lands = [lax.empty((N_DEV, len(grp)) + grp[0].shape[1:], grp[0].dtype) for grp in groups]

    def body(*refs):
        ins, zones = refs[:nf], refs[nf:nf + n]
        send_sems, recv_sems, local_sems = refs[nf + n:nf + n + 3]
        token = refs[-1]
        me, peers = _peers()
        for a in range(n):
            for w in range(len(groups[a])):
                pltpu.make_async_copy(ins[offs[a] + w].at[me], zones[a].at[me, w], local_sems.at[a]).start()
        for k, peer in enumerate(peers):
            p_id = 4 * peer[0] + 2 * peer[1] + peer[2]
            for a in range(n):
                for w in range(len(groups[a])):
                    pltpu.make_async_remote_copy(
                        src_ref=ins[offs[a] + w].at[p_id], dst_ref=zones[a].at[me, w],
                        send_sem=send_sems.at[7 * a + k], recv_sem=recv_sems.at[7 * a + k],
                        device_id=peer, device_id_type=MESH).start()
        token[...] = jnp.zeros(token.shape, F32)

    hbm = lambda t: pltpu.with_memory_space_constraint(t, pltpu.HBM)
    outs = pl.pallas_call(
        body, name=name,
        in_specs=[HBM_SPEC] * (nf + n),
        out_specs=[SEM_SPEC] * 3 + [HBM_SPEC] * (nf + n) + [pl.BlockSpec(memory_space=pltpu.VMEM)],
        out_shape=[pltpu.SemaphoreType.DMA((7 * n,)), pltpu.SemaphoreType.DMA((7 * n,)), pltpu.SemaphoreType.DMA((n,))]
                  + [pltpu.HBM(t.shape, t.dtype) for t in flat + lands]
                  + [jax.ShapeDtypeStruct((8, LANES), F32)],
        input_output_aliases={i: 3 + i for i in range(nf + n)},
        compiler_params=pltpu.CompilerParams(has_side_effects=pltpu.SideEffectType.DATAFLOW_SIDE_EFFECTING),
    )(*[hbm(t) for t in flat], *[hbm(t) for t in lands])
    sems, thru, token = outs[:3], outs[3:3 + nf + n], outs[-1]
    return (sems, thru, [len(grp) for grp in groups]), token


def scatter_wait(started, after, name):
    (send_sems, recv_sems, local_sems), thru, sizes = started
    n = len(sizes)
    nf = len(thru) - n

    def body(*refs):
        zones = refs[nf:nf + n]
        s_sems, r_sems, l_sems = refs[nf + n:nf + n + 3]
        me, peers = _peers()
        for a in range(n):
            for k, peer in enumerate(peers):
                cp = pltpu.make_async_remote_copy(
                    src_ref=zones[a].at[0], dst_ref=zones[a].at[0],
                    send_sem=s_sems.at[7 * a + k], recv_sem=r_sems.at[7 * a + k], device_id=peer,
                    device_id_type=MESH)
                cp.wait_send()
                cp.wait_recv()
            pltpu.make_async_copy(zones[a].at[0], zones[a].at[0], l_sems.at[a]).wait()

    outs = pl.pallas_call(
        body, name=name,
        in_specs=[HBM_SPEC] * (nf + n) + [SEM_SPEC] * 3 + [pl.BlockSpec(memory_space=pl.ANY)],
        out_specs=[HBM_SPEC] * (nf + n),
        out_shape=[pltpu.HBM(t.shape, t.dtype) for t in thru],
        input_output_aliases={i: i for i in range(nf + n)},
        compiler_params=pltpu.CompilerParams(has_side_effects=pltpu.SideEffectType.DATAFLOW_SIDE_EFFECTING),
    )(*thru, send_sems, recv_sems, local_sems, after)
    return outs[nf:]


def pair_start(grads, after, name):
    nw = len(grads)
    land = lax.empty((4, nw) + grads[0].shape[1:], grads[0].dtype)

    def body(*refs):
        ins, zone = refs[:nw], refs[nw]
        send_sems, recv_sems = refs[nw + 2:nw + 4]
        x, y, c = _mesh_pos()
        for j in range(4):
            for w in range(nw):
                pltpu.make_async_remote_copy(
                    src_ref=ins[w].at[2 * j + (1 - c)], dst_ref=zone.at[j, w], send_sem=send_sems.at[0],
                    recv_sem=recv_sems.at[0], device_id=(x, y, 1 - c), device_id_type=MESH).start()
        refs[-1][...] = jnp.zeros(refs[-1].shape, F32)

    sems, thru, token = _split_call(body, name, list(grads) + [land], (1, 1), extra=(after,))
    return (sems, thru, nw), token


def pair_wait(started, after, name):
    sems, thru, nw = started

    def body(*refs):
        zone = refs[nw]
        send_sems, recv_sems = refs[nw + 1:nw + 3]
        x, y, c = _mesh_pos()
        cp = pltpu.make_async_remote_copy(src_ref=zone, dst_ref=zone, send_sem=send_sems.at[0],
                                          recv_sem=recv_sems.at[0], device_id=(x, y, 1 - c), device_id_type=MESH)
        cp.wait_send()
        cp.wait_recv()

    outs = _split_call(body, name, thru, (1, 1), extra=(*sems, after), with_token=False)
    return outs[:nw], outs[nw]


def pair_sum(grads, land, name):
    nw = len(grads)
    _, r, c_dim = grads[0].shape

    def body(*refs):
        g_refs, l_ref, o_ref = refs[:nw], refs[nw], refs[nw + 1]
        core = lax.axis_index("c")
        for w in range(nw):
            o_ref[0, w] = (g_refs[w][0, core].astype(F32) + l_ref[0, w].astype(F32)).astype(BF16)

    return pl.pallas_call(
        body, name=name, grid=(4,),
        in_specs=[pl.BlockSpec((1, 2, r, c_dim), lambda j: (j, 0, 0, 0))] * nw
                 + [pl.BlockSpec((1, nw, r, c_dim), lambda j: (j, 0, 0, 0))],
        out_specs=pl.BlockSpec((1, nw, r, c_dim), lambda j: (j, 0, 0, 0)),
        out_shape=jax.ShapeDtypeStruct((4, nw, r, c_dim), BF16),
        compiler_params=_params(),
    )(*[g.reshape(4, 2, r, c_dim) for g in grads], land)


def _other_chips():
    x, y, c = _mesh_pos()
    chips = []
    for rel in range(1, 4):
        px, py = (1 - x if rel & 2 else x), (1 - y if rel & 1 else y)
        chips.append((px, py, 2 * px + py))
    return 2 * x + y, c, chips


def chip_start(pair_sums, after, name):
    land = lax.empty(pair_sums.shape, pair_sums.dtype)

    def body(*refs):
        h_ref, zone = refs[0], refs[1]
        send_sems, recv_sems, local_sem = refs[3:6]
        mine, c, chips = _other_chips()
        pltpu.make_async_copy(h_ref.at[mine], zone.at[mine], local_sem.at[0]).start()
        for k, (px, py, j) in enumerate(chips):
            pltpu.make_async_remote_copy(
                src_ref=h_ref.at[j], dst_ref=zone.at[mine], send_sem=send_sems.at[k], recv_sem=recv_sems.at[k],
                device_id=(px, py, c), device_id_type=MESH).start()
        refs[-1][...] = jnp.zeros(refs[-1].shape, F32)

    sems, thru, token = _split_call(body, name, [pair_sums, land], (3, 3, 1), extra=(after,))
    return (sems, thru), token


def chip_wait(started, after, name):
    sems, thru = started

    def body(*refs):
        zone = refs[1]
        send_sems, recv_sems, local_sem = refs[2:5]
        _, c, chips = _other_chips()
        for k, (px, py, _) in enumerate(chips):
            cp = pltpu.make_async_remote_copy(
                src_ref=zone.at[0], dst_ref=zone.at[0], send_sem=send_sems.at[k], recv_sem=recv_sems.at[k],
                device_id=(px, py, c), device_id_type=MESH)
            cp.wait_send()
            cp.wait_recv()
        pltpu.make_async_copy(zone.at[0], zone.at[0], local_sem.at[0]).wait()

    return _split_call(body, name, thru, (3, 3, 1), extra=(*sems, after), with_token=False)[1]


def share_small(parts, after):
    n = len(parts)

    def body(*refs):
        ins, outs = refs[:n], refs[n + 1:2 * n + 1]
        send_sems, recv_sems, local_sems = refs[2 * n + 1:]
        me, peers = _peers()
        copies = []
        for i in range(n):
            copies.append(pltpu.make_async_copy(ins[i], outs[i].at[me], local_sems.at[i]))
            copies += [pltpu.make_async_remote_copy(
                src_ref=ins[i], dst_ref=outs[i].at[me], send_sem=send_sems.at[7 * i + k],
                recv_sem=recv_sems.at[7 * i + k], device_id=peer, device_id_type=MESH)
                for k, peer in enumerate(peers)]
        for cp in copies:
            cp.start()
        for cp in copies:
            cp.wait()

    vm = pl.BlockSpec(memory_space=pltpu.VMEM)
    return pl.pallas_call(
        body, name="share_small", in_specs=[vm] * n + [pl.BlockSpec(memory_space=pl.ANY)], out_specs=[vm] * n,
        out_shape=[jax.ShapeDtypeStruct((N_DEV,) + p.shape, p.dtype) for p in parts],
        scratch_shapes=[pltpu.SemaphoreType.DMA((7 * n,)), pltpu.SemaphoreType.DMA((7 * n,)),
                        pltpu.SemaphoreType.DMA((n,))],
    )(*parts, after)


def _adamw_math(w, g, m, v):
    m = ADAM_B1 * m + (1.0 - ADAM_B1) * g
    v = ADAM_B2 * v + (1.0 - ADAM_B2) * (g * g)
    m_hat = m / (1.0 - ADAM_B1 ** ADAM_STEP)
    v_hat = v / (1.0 - ADAM_B2 ** ADAM_STEP)
    delta = -ADAM_LR * (m_hat / (jnp.sqrt(v_hat) + ADAM_EPS) + ADAM_WD * w)
    return delta, m, v


def adamw_layer(zone, w_idx, layer, w, m, v, prev, after, name):
    n_src, _, r, c = zone.shape
    depth = w.shape[0]
    if prev is None:
        prev = tuple(lax.empty((depth, r, c), F32) for _ in range(4))
    tr = r // 2 if r % 16 == 0 else r

    def body(z_ref, w_ref, m_ref, v_ref, *rest):
        g_ref, d_ref, mo_ref, vo_ref = rest[5:]
        g = z_ref[0].astype(F32)
        for src in range(1, n_src):
            g = g + z_ref[src].astype(F32)
        g_ref[...] = g
        d_ref[...], mo_ref[...], vo_ref[...] = _adamw_math(w_ref[...], g, m_ref[...], v_ref[...])

    rows = pl.BlockSpec((None, tr, c), lambda i: (layer, i, 0))
    anywhere = pl.BlockSpec(memory_space=pl.ANY)
    return pl.pallas_call(
        body, name=name, grid=(r // tr,),
        in_specs=[pl.BlockSpec((n_src, None, tr, c), lambda i: (0, w_idx, i, 0)), rows, rows, rows]
                 + [anywhere] * 5,
        out_specs=[rows] * 4,
        out_shape=[jax.ShapeDtypeStruct((depth, r, c), F32)] * 4,
        input_output_aliases={4 + k: k for k in range(4)},
        compiler_params=_params(),
    )(zone, w, m, v, *prev, after)


def adamw_small(ws, recvs, ms, vs, name):
    n = len(ws)

    def body(*refs):
        w_refs, r_refs, m_refs, v_refs = (refs[i * n:(i + 1) * n] for i in range(4))
        g_refs, d_refs, mo_refs, vo_refs = (refs[(4 + i) * n:(5 + i) * n] for i in range(4))
        for i in range(n):
            g = r_refs[i][0]
            for src in range(1, N_DEV):
                g = g + r_refs[i][src]
            g_refs[i][...] = g
            d_refs[i][...], mo_refs[i][...], vo_refs[i][...] = _adamw_math(w_refs[i][...], g, m_refs[i][...],
                                                                            v_refs[i][...])

    vm = pl.BlockSpec(memory_space=pltpu.VMEM)
    outs = pl.pallas_call(
        body, name=name, in_specs=[vm] * (4 * n), out_specs=[vm] * (4 * n),
        out_shape=[jax.ShapeDtypeStruct(w.shape, F32) for w in ws] * 4,
        compiler_params=pltpu.CompilerParams(vmem_limit_bytes=V7X_VMEM_LIMIT),
    )(*ws, *recvs, *ms, *vs)
    return [outs[i * n:(i + 1) * n] for i in range(4)]


SMALL_NAMES = ("ffn1_norm", "mix_norm", "ffn2_norm", "b_gate", "na_q_norm", "na_k_norm", "sw_q_norm", "sw_k_norm",
               "na_rpb", "sw_sink", "t5_rel_table")


def kernel(x, ffn1_norm, ffn1_w_gate, ffn1_w_up, ffn1_w_down, mix_norm, w_in, b_gate, na_q_norm, na_k_norm, na_rpb, sw_q_norm, sw_k_norm, sw_sink, t5_rel_table, w_branch_na, w_branch_sw, w_out, ffn2_norm, ffn2_w_gate, ffn2_w_up, ffn2_w_down, loss_target, m_ffn1_norm, m_ffn1_w_gate, m_ffn1_w_up, m_ffn1_w_down, m_mix_norm, m_w_in, m_b_gate, m_na_q_norm, m_na_k_norm, m_na_rpb, m_sw_q_norm, m_sw_k_norm, m_sw_sink, m_t5_rel_table, m_w_branch_na, m_w_branch_sw, m_w_out, m_ffn2_norm, m_ffn2_w_gate, m_ffn2_w_up, m_ffn2_w_down, v_ffn1_norm, v_ffn1_w_gate, v_ffn1_w_up, v_ffn1_w_down, v_mix_norm, v_w_in, v_b_gate, v_na_q_norm, v_na_k_norm, v_na_rpb, v_sw_q_norm, v_sw_k_norm, v_sw_sink, v_t5_rel_table, v_w_branch_na, v_w_branch_sw, v_w_out, v_ffn2_norm, v_ffn2_w_gate, v_ffn2_w_up, v_ffn2_w_down):
    weights = dict(ffn1_norm=ffn1_norm, ffn1_w_gate=ffn1_w_gate, ffn1_w_up=ffn1_w_up, ffn1_w_down=ffn1_w_down,
                   mix_norm=mix_norm, w_in=w_in, b_gate=b_gate, na_q_norm=na_q_norm, na_k_norm=na_k_norm,
                   na_rpb=na_rpb, sw_q_norm=sw_q_norm, sw_k_norm=sw_k_norm, sw_sink=sw_sink,
                   t5_rel_table=t5_rel_table, w_branch_na=w_branch_na, w_branch_sw=w_branch_sw, w_out=w_out,
                   ffn2_norm=ffn2_norm, ffn2_w_gate=ffn2_w_gate, ffn2_w_up=ffn2_w_up, ffn2_w_down=ffn2_w_down)
    mom_m = dict(ffn1_norm=m_ffn1_norm, ffn1_w_gate=m_ffn1_w_gate, ffn1_w_up=m_ffn1_w_up, ffn1_w_down=m_ffn1_w_down,
                 mix_norm=m_mix_norm, w_in=m_w_in, b_gate=m_b_gate, na_q_norm=m_na_q_norm, na_k_norm=m_na_k_norm,
                 na_rpb=m_na_rpb, sw_q_norm=m_sw_q_norm, sw_k_norm=m_sw_k_norm, sw_sink=m_sw_sink,
                 t5_rel_table=m_t5_rel_table, w_branch_na=m_w_branch_na, w_branch_sw=m_w_branch_sw, w_out=m_w_out,
                 ffn2_norm=m_ffn2_norm, ffn2_w_gate=m_ffn2_w_gate, ffn2_w_up=m_ffn2_w_up, ffn2_w_down=m_ffn2_w_down)
    mom_v = dict(ffn1_norm=v_ffn1_norm, ffn1_w_gate=v_ffn1_w_gate, ffn1_w_up=v_ffn1_w_up, ffn1_w_down=v_ffn1_w_down,
                 mix_norm=v_mix_norm, w_in=v_w_in, b_gate=v_b_gate, na_q_norm=v_na_q_norm, na_k_norm=v_na_k_norm,
                 na_rpb=v_na_rpb, sw_q_norm=v_sw_q_norm, sw_k_norm=v_sw_k_norm, sw_sink=v_sw_sink,
                 t5_rel_table=v_t5_rel_table, w_branch_na=v_w_branch_na, w_branch_sw=v_w_branch_sw, w_out=v_w_out,
                 ffn2_norm=v_ffn2_norm, ffn2_w_gate=v_ffn2_w_gate, ffn2_w_up=v_ffn2_w_up, ffn2_w_down=v_ffn2_w_down)
    order = list(weights)

    depth = ffn1_norm.shape[0]
    s, d = x.shape[1], x.shape[2]
    xs = x[0]
    tr = lambda w: jnp.swapaxes(w, -1, -2)

    merge = lambda t: t.reshape(t.shape[0], N_DEV * t.shape[2], t.shape[3])
    no_dep = jnp.zeros((8, LANES), F32)

    def shards_of(kind, l):
        stack = lambda *ws: jnp.stack(ws).astype(BF16)
        if kind == "ffn1":
            return [stack(tr(ffn1_w_gate[l]), tr(ffn1_w_up[l]), ffn1_w_down[l])]
        if kind == "win":
            return [stack(tr(w_in[l]))]
        return [stack(tr(ffn2_w_gate[l]), tr(ffn2_w_up[l]), ffn2_w_down[l]), stack(w_out[l]),
                stack(tr(w_branch_na[l]), tr(w_branch_sw[l]))]

    def start(kind, l, after):
        return gather_start(shards_of(kind, l), after, f"gather_{kind}_{l}")

    def arrive(started, kind, l, after):
        zones = gather_wait(started, after, f"gather_{kind}_{l}_wait")
        return forward_start(zones, no_dep, f"forward_{kind}_{l}")

    def finish(fwd, kind, l, after):
        return [merge(z) for z in forward_wait(fwd, after, f"forward_{kind}_{l}_wait")]

    bd = jnp.asarray(np.kron(np.eye(MXU_TILE // HEAD_DIM), np.full((HEAD_DIM, HEAD_DIM), 1.0 / HEAD_DIM)), BF16)
    bmap = jnp.asarray(_t5_bucket_map())
    tile8 = lambda g: jnp.tile(g, NA_WIDTH // HEAD_DIM).reshape(1, NA_WIDTH)
    tile2 = lambda g: jnp.tile(g, SW_KV_WIDTH // HEAD_DIM).reshape(1, SW_KV_WIDTH)

    st_first, tok = start("ffn1", 0, no_dep)
    t5b = t5_expand(t5_rel_table, bmap, tok, "t5_expand").reshape(SW_STACK, 3 * SW_BLOCK)
    t2_tables = [rpb_expand(_rpb_rows(na_rpb[l]), tok, f"rpb_expand_{l}") for l in range(depth)]
    fwd, _ = arrive(st_first, "ffn1", 0, t2_tables[-1])
    st_win, dep = start("win", 0, t5b)
    (first,) = finish(fwd, "ffn1", 0, dep)

    saved = []
    layer_w = {0: dict(wg1=(first, 0), wu1=(first, 1), wd1=(first, 2))}
    cur = xs
    for l in range(depth):
        sv = {}
        lw = layer_w[l]
        sv["x0"] = cur
        cur, sv["xn1"], sv["hg1"], sv["hu1"], sv["act1"] = ffn_forward(
            cur, ffn1_norm[l][None], lw["wg1"], lw["wu1"], lw["wd1"], dep, f"ffn1_{l}")
        sv["x1"] = cur
        fwd, _ = arrive(st_win, "win", l, cur)
        st_rest, tok = start("rest", l, cur)
        (zb,) = finish(fwd, "win", l, tok)
        lw["win"] = (zb, 0)
        sv["gains"] = (tile8(na_q_norm[l]), tile8(na_k_norm[l]), tile8(sw_q_norm[l]), tile2(sw_k_norm[l]))
        sv["hn"], sv["zq"], sv["qa"], sv["ka"], sv["qs"], sv["ks"], sv["gt"] = mix_in(
            cur, mix_norm[l][None], lw["win"], b_gate[l][None], *sv["gains"], bd, f"mix_in_{l}")
        sv["t2"] = t2_tables[l]
        sv["o_na"] = na_fwd(sv["qa"], sv["ka"], sv["zq"], sv["t2"], f"na_fwd_{l}")
        dep = no_dep
        if l + 1 < depth:
            st_ffn1, dep = start("ffn1", l + 1, sv["o_na"])
        sv["o_sw"] = sw_fwd(sv["qs"], sv["ks"], sv["zq"], t5b, sw_sink[l], dep, f"sw_fwd_{l}")
        fwd, tok = arrive(st_rest, "rest", l, sv["o_sw"])
        za, zc, zd = finish(fwd, "rest", l, tok)
        lw.update(wg2=(za, 0), wu2=(za, 1), wd2=(za, 2), wout=(zc, 0), wna=(zd, 0), wsw=(zd, 1))
        cur, sv["a_na"], sv["a_sw"], sv["merged"] = merge_out(
            cur, sv["o_na"], sv["o_sw"], sv["gt"], lw["wna"], lw["wsw"], lw["wout"], f"merge_out_{l}")
        sv["x2"] = cur
        dep = no_dep
        if l + 1 < depth:
            st_win, dep = start("win", l + 1, cur)
        sv["xn2"], sv["hg2"], sv["hu2"], sv["act2"] = ffn_forward(
            cur, ffn2_norm[l][None], lw["wg2"], lw["wu2"], None, dep, f"ffn2_up_{l}")
        dep = no_dep
        if l + 1 < depth:
            fwd, dep = arrive(st_ffn1, "ffn1", l + 1, sv["act2"])
        if l + 1 < depth:
            cur = ffn_down(cur, sv["act2"], lw["wd2"], dep, f"ffn2_down_{l}")
            (za,) = finish(fwd, "ffn1", l + 1, cur)
            layer_w[l + 1] = dict(wg1=(za, 0), wu1=(za, 1), wd1=(za, 2))
        else:
            dx, loss_acc = ffn_down(cur, sv["act2"], lw["wd2"], dep, f"ffn2_down_{l}", target=loss_target[0])
        dep = no_dep
        saved.append(sv)

    loss = lax.psum(jnp.sum(loss_acc) * (0.5 / d), ("x", "y", "c"))

    split = lambda t: t.reshape(N_DEV, t.shape[0] // N_DEV, t.shape[1])
    pending = {}
    last_key = "ffn1_0"
    two_level = set()
    small = {k: [None] * depth for k in SMALL_NAMES if k != "t5_rel_table"}
    dbias_sw = []
    for l in reversed(range(depth)):
        sv = saved[l]
        lw = layer_w[l]
        wg1, wu1, wd1, wg2, wu2, wd2 = (lw[k] for k in ("wg1", "wu1", "wd1", "wg2", "wu2", "wd2"))
        win_t, wout_l, wna_t, wsw_t = lw["win"], lw["wout"], lw["wna"], lw["wsw"]
        blocks = ((2, "x2", "xn2", "hg2", "hu2", "act2", wg2, wu2, wd2, "ffn2_norm", 3),
                  (1, "x0", "xn1", "hg1", "hu1", "act1", wg1, wu1, wd1, "ffn1_norm", 0))

        def ffn_backward(dx, blk):
            tag, xk, xnk, hgk, huk, actk, wg, wu, wd, norm_name, slot = blk
            gains = weights[norm_name]
            dxb, dhg, dhu = ffn_bwd_act(dx, wd, sv[hgk], sv[huk], f"ffn{tag}_bwd_act_{l}")
            gwg, gwu, gwd = tn_matmul([(dhg, sv[xnk], 1.0), (dhu, sv[xnk], 1.0), (sv[actk], dxb, 0.5)],
                                      f"ffn{tag}_dw_{l}")
            key = f"ffn{tag}_{l}"
            blocks_of = [split(gwg), split(gwu), split(gwd)]
            if key in two_level:
                paired, token = pair_start(blocks_of, dxb, f"pair_{key}")
            else:
                pending[key], token = scatter_start([blocks_of], f"scatter_{key}")
            dx, dg = proj_bwd_norm([dhg, dhu], [wg, wu], sv[xk], gains[l][None], dx, token, f"ffn{tag}_bwd_x_{l}")
            token = no_dep
            if key in two_level:
                thru, land = pair_wait(paired, dx, f"pair_{key}_wait")
                pending[key], token = chip_start(pair_sum(thru, land, f"pair_sum_{key}"), dg, f"chips_{key}")
            small[norm_name][l] = dg[0]
            return dx, token

        dx, token = ffn_backward(dx, blocks[0])
        dxb, dzg, da_na, da_sw, do_na, do_sw, dbg = mix_bwd_out(
            dx, sv["gt"], sv["a_na"], sv["a_sw"], wna_t, wsw_t, wout_l, token, f"mix_bwd_out_{l}")
        small["b_gate"][l] = dbg[0]
        gwout, gwna, gwsw = tn_matmul([(sv["merged"], dxb, 1.0), (da_na, sv["o_na"], 1.0), (da_sw, sv["o_sw"], 1.0)],
                                      f"mix_dw_{l}")
        dqa, dka, dva, dt2 = na_bwd(sv["qa"], sv["ka"], sv["zq"], sv["t2"], sv["o_na"], do_na, f"na_bwd_{l}")
        dqs, dks, dvs, dbias, dsink = sw_bwd(sv["qs"], sv["ks"], sv["zq"], t5b, sw_sink[l], sv["o_sw"], do_sw,
                                             f"sw_bwd_{l}")
        dbias_sw.append(dbias.reshape(SW_HEADS, SW_BLOCK, 3 * SW_BLOCK))
        small["sw_sink"][l] = jnp.sum(dsink[:, 0].reshape(SW_HEADS, SW_BLOCK), axis=1)
        small["na_rpb"][l] = _rpb_from_rows(rpb_reduce(dt2, f"rpb_reduce_{l}"))
        dz, dgqa, dgka, dgqs, dgks = qk_norm_bwd(dqa, dka, dva, dqs, dks, dvs, sv["zq"], dzg, *sv["gains"], bd,
                                                 f"qk_norm_bwd_{l}")
        fold = lambda g: jnp.sum(g.reshape(-1, HEAD_DIM), axis=0)
        small["na_q_norm"][l], small["na_k_norm"][l] = fold(dgqa), fold(dgka)
        small["sw_q_norm"][l], small["sw_k_norm"][l] = fold(dgqs), fold(dgks)
        (gwin,) = tn_matmul([(dz, sv["hn"], 1.0)], f"dwin_{l}")
        pending[f"mix_{l}"], token = scatter_start([[split(gwout)], [split(gwna), split(gwsw)], [split(gwin)]],
                                                   f"scatter_mix_{l}")
        dx, dg = proj_bwd_norm([dz], [win_t], sv["x1"], mix_norm[l][None], dx, token, f"mix_bwd_x_{l}")
        small["mix_norm"][l] = dg[0]
        dx, tail = ffn_backward(dx, blocks[1])

    dtab = t5_reduce(dbias_sw, bmap, "t5_reduce")
    small_parts = {k: jnp.stack(v) for k, v in small.items()}
    small_parts["t5_rel_table"] = jnp.transpose(dtab[:, :, 0])

    grads, delta, new_m, new_v = {}, {}, {}, {}
    state = {}
    chain = [tail if last_key in two_level else dx]
    members = {"ffn": lambda t: [(f"ffn{t}_w_gate", 0, 0, True), (f"ffn{t}_w_up", 0, 1, True),
                                 (f"ffn{t}_w_down", 0, 2, False)],
               "mix": lambda t: [("w_out", 0, 0, False), ("w_branch_na", 1, 0, True), ("w_branch_sw", 1, 1, True),
                                 ("w_in", 2, 0, True)]}

    def collect(key):
        if key in two_level:
            zones = [chip_wait(pending[key], chain[0], f"wait_{key}")]
        else:
            zones = scatter_wait(pending[key], chain[0], f"wait_{key}")
        kind, l = key.split("_")
        for k, zi, wi, transposed in members[kind[:3]](kind[3:]):
            view = tr if transposed else (lambda t: t)
            state[k] = adamw_layer(zones[zi], wi, int(l), view(weights[k]), view(mom_m[k]), view(mom_v[k]),
                                   state.get(k), chain[0], f"adamw_{k}_{l}")
            chain[0] = state[k][1]
            if all(f"{kind}_{j}" in done for j in range(depth) if j != int(l)):
                grads[k], delta[k], new_m[k], new_v[k] = (view(t) for t in state[k])
        done.add(key)

    done = set()
    for key in pending:
        if key != last_key:
            collect(key)
    collect(last_key)
    recvs = share_small([small_parts[k] for k in SMALL_NAMES], chain[0])
    results = adamw_small([weights[k] for k in SMALL_NAMES], recvs, [mom_m[k] for k in SMALL_NAMES],
                          [mom_v[k] for k in SMALL_NAMES], "adamw_small")
    for dst, outs in zip((grads, delta, new_m, new_v), results):
        dst.update(dict(zip(SMALL_NAMES, outs)))

    return (loss, dx[None], *[grads[k] for k in order], *[delta[k] for k in order],
            *[new_m[k] for k in order], *[new_v[k] for k in order])
```

```python
import functools
import math

import numpy as np
import jax
import jax.numpy as jnp
from jax import lax
from jax.experimental import pallas as pl
from jax.experimental.pallas import tpu as pltpu

F32 = jnp.float32
BF16 = jnp.bfloat16
MESH = pl.DeviceIdType.MESH

N_DEV = 8
EPS = 1e-6
NEG = -1e30
HEAD_DIM = 64
GRID_W = 64
NA_ROWS = 8
NA_COLS = 16
NA_WIDTH = 512
SW_Q_WIDTH = 512
SW_KV_WIDTH = 128
SW_BLOCK = 128
SW_HEADS = 8
SW_REP = 4
REL_BUCKETS = 32
REL_MAX_DIST = 128
QKV_WIDTH = 3 * NA_WIDTH + SW_Q_WIDTH + 2 * SW_KV_WIDTH
SCALE = 1.0 / math.sqrt(HEAD_DIM)

ADAM_LR = 0.001
ADAM_B1 = 0.9
ADAM_B2 = 0.999
ADAM_EPS = 1e-08
ADAM_WD = 0.01
ADAM_STEP = 10

V7X_VMEM_LIMIT = 56 * 1024 * 1024
LANES = 128
MXU_TILE = 256

NT = (((1,), (1,)), ((), ()))
TN = (((0,), (0,)), ((), ()))


def _params(n_grid=1):
    return pltpu.CompilerParams(dimension_semantics=("arbitrary",) * n_grid,
                                vmem_limit_bytes=V7X_VMEM_LIMIT)


def _row_tile(s):
    for t in (512, 256, 128, 64, 32, 16, 8):
        if s % t == 0:
            return t
    raise ValueError(s)


def _tn_tile(n):
    best = max(t for t in range(LANES, min(n, 2304) + 1, LANES) if n % t == 0) if n % LANES == 0 else n
    return best // 2 if best == n and n >= 1024 else best


ONCE = pl.Buffered(1)


def _col_chunk(n):
    return MXU_TILE if n % MXU_TILE == 0 else n


def _dot(a, b):
    return jnp.dot(a, b, preferred_element_type=F32)


def _dotg(a, b, dn):
    return lax.dot_general(a, b, dn, preferred_element_type=F32)


def _sigmoid(v):
    return 1.0 / (1.0 + jnp.exp(-v))


def _rstd(xv):
    return lax.rsqrt(jnp.mean(xv * xv, axis=-1, keepdims=True) + EPS)


def _full(shape):
    nd = len(shape)
    return pl.BlockSpec(shape, lambda i, _n=nd: (0,) * _n)


def _rows(tm, width):
    return pl.BlockSpec((tm, width), lambda i: (i, 0))


def _mat(stack, idx):
    return pl.BlockSpec((None,) + tuple(stack.shape[1:]), lambda i, _w=idx: (_w, 0, 0), pipeline_mode=ONCE)


def _group_mean(v, bd):
    w = bd.shape[0]
    if v.shape[1] > w:
        return jnp.concatenate([_group_mean(v[:, c0:c0 + w], bd) for c0 in range(0, v.shape[1], w)], axis=1)
    hi = v.astype(BF16)
    lo = (v - hi.astype(F32)).astype(BF16)
    return _dot(hi, bd) + _dot(lo, bd)


def ffn_forward(x, gain, wg_t, wu_t, wd, dep, name):
    s, d = x.shape
    f = wg_t[0].shape[1]
    tm = _row_tile(s) if wd is None else min(_row_tile(s), 256)
    fc = _col_chunk(f)
    nw = 2 if wd is None else 3

    def body(x_ref, g_ref, *refs):
        w_refs, outs = refs[:nw], refs[nw + 1:]
        xn_ref, dg_ref, du_ref, act_ref = outs[-4:]
        xv = x_ref[...]
        xn = (xv * _rstd(xv) * g_ref[...]).astype(BF16)
        xn_ref[...] = xn
        for c0 in range(0, f, fc):
            hg = _dotg(xn, w_refs[0][c0:c0 + fc, :], NT)
            hu = _dotg(xn, w_refs[1][c0:c0 + fc, :], NT)
            sg = _sigmoid(hg)
            silu = hg * sg
            du_ref[:, c0:c0 + fc] = silu.astype(BF16)
            dg_ref[:, c0:c0 + fc] = (hu * (sg + silu * (1.0 - sg))).astype(BF16)
            act_ref[:, c0:c0 + fc] = (silu * hu).astype(BF16)
        if wd is not None:
            outs[0][...] = xv + 0.5 * _dot(act_ref[...], w_refs[2][...])

    weights = [wg_t, wu_t] + ([] if wd is None else [wd])
    out_specs = [_rows(tm, d), _rows(tm, f), _rows(tm, f), _rows(tm, f)]
    out_shape = [jax.ShapeDtypeStruct((s, d), BF16)] + [jax.ShapeDtypeStruct((s, f), BF16)] * 3
    if wd is not None:
        out_specs, out_shape = [_rows(tm, d)] + out_specs, [jax.ShapeDtypeStruct((s, d), F32)] + out_shape
    return pl.pallas_call(
        body, name=name, grid=(s // tm,),
        in_specs=[_rows(tm, d), _full((1, d))] + [_mat(*w) for w in weights] + [_full(dep.shape)],
        out_specs=out_specs, out_shape=out_shape,
        compiler_params=_params(),
    )(x, gain, *[w[0] for w in weights], dep)


def ffn_down(x, act, wd, dep, name, target=None):
    s, d = x.shape
    f = act.shape[1]
    tm = _row_tile(s)

    def body(x_ref, a_ref, w_ref, dep_ref, *rest):
        y = x_ref[...] + 0.5 * _dot(a_ref[...], w_ref[...])
        if target is None:
            rest[0][...] = y
            return
        t_ref, dy_ref, acc_ref = rest

        @pl.when(pl.program_id(0) == 0)
        def _():
            acc_ref[...] = jnp.zeros(acc_ref.shape, F32)

        err = y - t_ref[...]
        dy_ref[...] = err * (1.0 / d)
        part = jnp.sum((err * err).reshape(tm // 8, 8, d), axis=0)
        acc = part[:, 0:LANES]
        for c0 in range(LANES, d, LANES):
            acc = acc + part[:, c0:c0 + LANES]
        acc_ref[...] = acc_ref[...] + acc

    ins = [_rows(tm, d), _rows(tm, f), _mat(*wd), _full(dep.shape)]
    if target is None:
        return pl.pallas_call(
            body, name=name, grid=(s // tm,), in_specs=ins, out_specs=_rows(tm, d),
            out_shape=jax.ShapeDtypeStruct((s, d), F32), compiler_params=_params(),
        )(x, act, wd[0], dep)
    return pl.pallas_call(
        body, name=name, grid=(s // tm,), in_specs=ins + [_rows(tm, d)],
        out_specs=[_rows(tm, d), _full((8, LANES))],
        out_shape=[jax.ShapeDtypeStruct((s, d), F32), jax.ShapeDtypeStruct((8, LANES), F32)],
        compiler_params=_params(),
    )(x, act, wd[0], dep, target)


def mix_in(x, gain, win_t, b_gate, gq_na, gk_na, gq_sw, gk_sw, bd, name):
    s, d = x.shape
    tm = _row_tile(s)
    gc = _col_chunk(2 * d)

    def body(x_ref, g_ref, w_ref, b_ref, gqa_ref, gka_ref, gqs_ref, gks_ref, bd_ref,
             hn_ref, zq_ref, qa_ref, ka_ref, qs_ref, ks_ref, gt_ref):
        xv = x_ref[...]
        hn = (xv * _rstd(xv) * g_ref[...]).astype(BF16)
        hn_ref[...] = hn

        def proj(c0, c1):
            return _dotg(hn, w_ref[c0:c1, :], NT)

        def headnorm(z, g, bdm):
            return z * lax.rsqrt(_group_mean(z * z, bdm) + EPS) * g

        bd512 = bd_ref[...]
        bd128 = bd_ref[0:SW_KV_WIDTH, 0:SW_KV_WIDTH]
        z = proj(0, 512)
        zq_ref[:, 0:512] = z.astype(BF16)
        qa_ref[...] = (headnorm(z, gqa_ref[...], bd512) * SCALE).astype(BF16)
        z = proj(512, 1024)
        zq_ref[:, 512:1024] = z.astype(BF16)
        ka_ref[...] = headnorm(z, gka_ref[...], bd512).astype(BF16)
        z = proj(1024, 1536)
        zq_ref[:, 1024:1536] = z.astype(BF16)
        z = proj(1536, 2048)
        zq_ref[:, 1536:2048] = z.astype(BF16)
        qs_ref[...] = (headnorm(z, gqs_ref[...], bd512) * SCALE).astype(BF16)
        z = proj(2048, 2176)
        zq_ref[:, 2048:2176] = z.astype(BF16)
        ks_ref[...] = headnorm(z, gks_ref[...], bd128).astype(BF16)
        z = proj(2176, 2304)
        zq_ref[:, 2176:2304] = z.astype(BF16)
        for c0 in range(0, 2 * d, gc):
            zg = proj(QKV_WIDTH + c0, QKV_WIDTH + c0 + gc) + b_ref[:, c0:c0 + gc]
            gt_ref[:, c0:c0 + gc] = _sigmoid(zg).astype(BF16)

    return pl.pallas_call(
        body, name=name, grid=(s // tm,),
        in_specs=[_rows(tm, d), _full((1, d)), _mat(*win_t), _full((1, 2 * d)),
                  _full((1, 512)), _full((1, 512)), _full((1, 512)), _full((1, 128)), _full((MXU_TILE, MXU_TILE))],
        out_specs=[_rows(tm, d), _rows(tm, QKV_WIDTH), _rows(tm, 512), _rows(tm, 512), _rows(tm, 512),
                   _rows(tm, 128), _rows(tm, 2 * d)],
        out_shape=[jax.ShapeDtypeStruct((s, d), BF16), jax.ShapeDtypeStruct((s, QKV_WIDTH), BF16),
                   jax.ShapeDtypeStruct((s, 512), BF16), jax.ShapeDtypeStruct((s, 512), BF16),
                   jax.ShapeDtypeStruct((s, 512), BF16), jax.ShapeDtypeStruct((s, 128), BF16),
                   jax.ShapeDtypeStruct((s, 2 * d), BF16)],
        compiler_params=_params(),
    )(x, gain, win_t[0], b_gate, gq_na, gk_na, gq_sw, gk_sw, bd)


def _na_iotas():
    qc = lax.broadcasted_iota(jnp.int32, (GRID_W, LANES), 0)
    ln = lax.broadcasted_iota(jnp.int32, (GRID_W, LANES), 1)
    low = ln < GRID_W
    kc = jnp.where(low, ln, ln - GRID_W)
    diff = kc - qc + (NA_COLS - 1)
    qcs = jnp.clip(qc - NA_COLS // 2, 0, GRID_W - NA_COLS)
    inwin = (kc >= qcs) & (kc < qcs + NA_COLS)
    return diff, low, inwin


NA_RI = 2 * NA_ROWS - 1
NA_CI = 2 * NA_COLS - 1
NA_T2 = NA_RI + 1


def _rpb_rows(rpb):
    h = rpb.shape[0]
    padded = jnp.pad(rpb, ((0, 0), (1, 1), (0, GRID_W - NA_CI)))
    return jnp.concatenate([padded[:, :NA_T2], padded[:, 1:NA_T2 + 1]], axis=2).reshape(h, NA_T2, LANES)


def _rpb_from_rows(rows):
    return rows[:, 1:, :NA_CI] + rows[:, :NA_RI, GRID_W:GRID_W + NA_CI]


def rpb_expand(rows, dep, name):
    n_heads = rows.shape[0]

    def body(r_ref, dep_ref, o_ref):
        for h in range(n_heads):
            for e in range(NA_T2):
                line = jnp.broadcast_to(r_ref[h, e:e + 1, :], (GRID_W, LANES))
                o_ref[h, e] = pltpu.roll(line, LANES - (NA_COLS - 1), 1, stride=1, stride_axis=0)

    return pl.pallas_call(
        body, name=name,
        in_specs=[pl.BlockSpec(memory_space=pltpu.VMEM), pl.BlockSpec(memory_space=pltpu.VMEM)],
        out_specs=pl.BlockSpec(memory_space=pltpu.VMEM),
        out_shape=jax.ShapeDtypeStruct((n_heads, NA_T2, GRID_W, LANES), F32),
        compiler_params=pltpu.CompilerParams(vmem_limit_bytes=V7X_VMEM_LIMIT),
    )(rows, dep)


def rpb_reduce(dt2, name):
    n_heads = dt2.shape[0]
    flip = jnp.asarray(np.eye(GRID_W)[::-1], BF16)

    def body(d_ref, j_ref, o_ref):
        jm = j_ref[...]
        for h in range(n_heads):
            for e in range(NA_T2):
                dv = d_ref[h, e]
                hi = dv.astype(BF16)
                mid = (dv - hi.astype(F32)).astype(BF16)
                lo = (dv - hi.astype(F32) - mid.astype(F32)).astype(BF16)
                rev = _dot(jm, hi) + _dot(jm, mid) + _dot(jm, lo)
                back = pltpu.roll(rev, LANES + (NA_COLS - 1) - (GRID_W - 1), 1, stride=1, stride_axis=0)
                o_ref[h, e:e + 1, :] = jnp.sum(back, axis=0, keepdims=True)

    return pl.pallas_call(
        body, name=name,
        in_specs=[pl.BlockSpec(memory_space=pltpu.VMEM)] * 2,
        out_specs=pl.BlockSpec(memory_space=pltpu.VMEM),
        out_shape=jax.ShapeDtypeStruct((n_heads, NA_T2, LANES), F32),
        compiler_params=pltpu.CompilerParams(vmem_limit_bytes=V7X_VMEM_LIMIT),
    )(dt2, flip)


NA_TQ = 4
NA_TK = NA_TQ + NA_ROWS
NA_KCH = NA_TK // 2


def _na_tile_geometry(t, rows):
    r = t * NA_TQ
    kbase = jnp.clip(r - NA_ROWS // 2, 0, rows - NA_TK)
    starts = [jnp.clip(r + a - NA_ROWS // 2, 0, rows - NA_ROWS) for a in range(NA_TQ)]
    return r, kbase, starts


def _na_tile_mask(kbase, starts, low, inwin):
    half = jnp.where(low, 0, 1)
    cols = []
    for c in range(NA_KCH):
        krow = kbase + 2 * c + half
        cols.append(jnp.concatenate(
            [jnp.where(inwin & (krow >= st) & (krow < st + NA_ROWS), 0.0, NEG) for st in starts], axis=0))
    return jnp.concatenate(cols, axis=1)


def _na_tile_index(r, kbase, a, c):
    return jnp.clip(kbase + 2 * c - (r + a) + NA_ROWS, 0, NA_T2 - 1)


def _na_tile_scores(q, k, t2_ref, hh, r, kbase, madd):
    bias = jnp.concatenate(
        [jnp.concatenate([t2_ref[hh, _na_tile_index(r, kbase, a, c)] for a in range(NA_TQ)], axis=0)
         for c in range(NA_KCH)], axis=1)
    return _dotg(q, k, NT) + bias + madd


def _softmax_rows(sc):
    e = jnp.exp(sc - jnp.max(sc, axis=1, keepdims=True))
    return e * (1.0 / jnp.sum(e, axis=1, keepdims=True))


def na_fwd(qa, ka, zq, t2, name):
    s = qa.shape[0]
    rows = s // GRID_W
    n_pairs = NA_WIDTH // LANES
    v_blk0 = (2 * NA_WIDTH) // LANES

    assert rows % NA_TQ == 0 and rows >= NA_TK
    tq, tk = NA_TQ * GRID_W, NA_TK * GRID_W

    def body(q_ref, k_ref, v_ref, t2_ref, o_ref, s_scr, p_scr):
        _, low, inwin = _na_iotas()

        def tile(t, carry):
            r, kbase, starts = _na_tile_geometry(t, rows)
            madd = _na_tile_mask(kbase, starts, low, inwin)
            qr = pl.ds(pl.multiple_of(r * GRID_W, tq), tq)
            kr = pl.ds(pl.multiple_of(kbase * GRID_W, tq), tk)
            for hh in range(2):
                lanes = slice(HEAD_DIM * hh, HEAD_DIM * (hh + 1))
                s_scr[tq * hh:tq * (hh + 1), :] = _na_tile_scores(q_ref[qr, lanes], k_ref[kr, lanes], t2_ref, hh, r,
                                                                  kbase, madd)
            p_scr[...] = _softmax_rows(s_scr[...]).astype(BF16)
            for hh in range(2):
                lanes = slice(HEAD_DIM * hh, HEAD_DIM * (hh + 1))
                o_ref[qr, lanes] = _dot(p_scr[tq * hh:tq * (hh + 1), :], v_ref[kr, lanes]).astype(BF16)
            return carry

        lax.fori_loop(0, rows // NA_TQ, tile, 0)

    col = lambda off: pl.BlockSpec((s, LANES), lambda p, _o=off: (0, _o + p))
    return pl.pallas_call(
        body, name=name, grid=(n_pairs,),
        in_specs=[col(0), col(0), col(v_blk0),
                  pl.BlockSpec((2, NA_T2, GRID_W, LANES), lambda p: (p, 0, 0, 0))],
        out_specs=col(0),
        out_shape=jax.ShapeDtypeStruct((s, NA_WIDTH), BF16),
        scratch_shapes=[pltpu.VMEM((2 * tq, tk), F32), pltpu.VMEM((2 * tq, tk), BF16)],
        compiler_params=_params(),
    )(qa, ka, zq, t2)


def na_bwd(qa, ka, zq, t2, o_na, do_na, name):
    s = qa.shape[0]
    rows = s // GRID_W
    n_pairs = NA_WIDTH // LANES
    v_blk0 = (2 * NA_WIDTH) // LANES

    tq, tk = NA_TQ * GRID_W, NA_TK * GRID_W

    def body(q_ref, k_ref, v_ref, t2_ref, o_ref, do_ref, dq_ref, dk_ref, dv_ref, dt2_ref):
        _, low, inwin = _na_iotas()
        dk_ref[...] = jnp.zeros(dk_ref.shape, F32)
        dv_ref[...] = jnp.zeros(dv_ref.shape, F32)
        dt2_ref[...] = jnp.zeros(dt2_ref.shape, F32)

        def tile(t, carry):
            r, kbase, starts = _na_tile_geometry(t, rows)
            madd = _na_tile_mask(kbase, starts, low, inwin)
            qr = pl.ds(pl.multiple_of(r * GRID_W, tq), tq)
            kr = pl.ds(pl.multiple_of(kbase * GRID_W, tq), tk)
            for hh in range(2):
                lanes = slice(HEAD_DIM * hh, HEAD_DIM * (hh + 1))
                q, k, v = q_ref[qr, lanes], k_ref[kr, lanes], v_ref[kr, lanes]
                p = _softmax_rows(_na_tile_scores(q, k, t2_ref, hh, r, kbase, madd))
                do = do_ref[qr, lanes]
                delta = jnp.sum(do.astype(F32) * o_ref[qr, lanes].astype(F32), axis=1, keepdims=True)
                ds = p * (_dotg(do, v, NT) - delta)
                shared = {}
                for a in range(NA_TQ):
                    for c in range(NA_KCH):
                        shared.setdefault(2 * c - a, []).append(
                            ds[GRID_W * a:GRID_W * (a + 1), LANES * c:LANES * (c + 1)])
                for offset, parts in shared.items():
                    e = jnp.clip(offset + kbase - r + NA_ROWS, 0, NA_T2 - 1)
                    dt2_ref[hh, e] = dt2_ref[hh, e] + functools.reduce(jnp.add, parts)
                dsb = ds.astype(BF16)
                dq_ref[qr, lanes] = _dot(dsb, k)
                dk_ref[kr, lanes] = dk_ref[kr, lanes] + _dotg(dsb, q, TN)
                dv_ref[kr, lanes] = dv_ref[kr, lanes] + _dotg(p.astype(BF16), do, TN)
            return carry

        lax.fori_loop(0, rows // NA_TQ, tile, 0)

    col = lambda off: pl.BlockSpec((s, LANES), lambda p, _o=off: (0, _o + p))
    t2spec = pl.BlockSpec((2, NA_T2, GRID_W, LANES), lambda p: (p, 0, 0, 0))
    return pl.pallas_call(
        body, name=name, grid=(n_pairs,),
        in_specs=[col(0), col(0), col(v_blk0), t2spec, col(0), col(0)],
        out_specs=[col(0), col(0), col(0), t2spec],
        out_shape=[jax.ShapeDtypeStruct((s, NA_WIDTH), F32)] * 3 + [jax.ShapeDtypeStruct(t2.shape, F32)],
        compiler_params=_params(),
    )(qa, ka, zq, t2, o_na, do_na)


def _t5_bucket_map():
    rel = np.arange(3 * SW_BLOCK)[None, :] - SW_BLOCK - np.arange(SW_BLOCK)[:, None]
    nb = REL_BUCKETS // 2
    max_exact = nb // 2
    n = np.abs(rel)
    large = max_exact + (np.log(np.maximum(n, 1) / max_exact)
                         / np.log(REL_MAX_DIST / max_exact) * (nb - max_exact)).astype(np.int32)
    large = np.minimum(large, nb - 1)
    return ((rel > 0) * nb + np.where(n < max_exact, n, large)).astype(np.int32)


def t5_expand(table, bmap, dep, name):
    def body(tab_ref, bm_ref, dep_ref, o_ref):
        bm = bm_ref[...]
        for h in range(SW_HEADS):
            t = jnp.zeros(bm.shape, F32)
            for b in range(REL_BUCKETS):
                t = jnp.where(bm == b, tab_ref[b, h], t)
            o_ref[h] = t

    return pl.pallas_call(
        body, name=name,
        in_specs=[pl.BlockSpec(memory_space=pltpu.SMEM), pl.BlockSpec(memory_space=pltpu.VMEM),
                  pl.BlockSpec(memory_space=pltpu.VMEM)],
        out_specs=pl.BlockSpec(memory_space=pltpu.VMEM),
        out_shape=jax.ShapeDtypeStruct((SW_HEADS,) + bmap.shape, F32),
        compiler_params=pltpu.CompilerParams(vmem_limit_bytes=V7X_VMEM_LIMIT),
    )(table, bmap, dep)


def t5_reduce(dbias_list, bmap, name):
    n = len(dbias_list)

    def body(*refs):
        d_refs, bm_ref, o_ref = refs[:n], refs[n], refs[n + 1]
        bm = bm_ref[...]
        for h in range(SW_HEADS):
            dv = d_refs[0][h]
            for other in d_refs[1:]:
                dv = dv + other[h]
            rows = [jnp.sum(jnp.where(bm == b, dv, 0.0), axis=0, keepdims=True) for b in range(REL_BUCKETS)]
            r = jnp.concatenate(rows, axis=0)
            o_ref[h] = jnp.broadcast_to(jnp.sum(r, axis=1, keepdims=True), (REL_BUCKETS, LANES))

    return pl.pallas_call(
        body, name=name,
        in_specs=[pl.BlockSpec(memory_space=pltpu.VMEM)] * (n + 1),
        out_specs=pl.BlockSpec(memory_space=pltpu.VMEM),
        out_shape=jax.ShapeDtypeStruct((SW_HEADS, REL_BUCKETS, LANES), F32),
        compiler_params=pltpu.CompilerParams(vmem_limit_bytes=V7X_VMEM_LIMIT),
    )(*dbias_list, bmap)


def _sw_mask_iotas():
    a = lax.broadcasted_iota(jnp.int32, (SW_BLOCK, 3 * SW_BLOCK), 0)
    j = lax.broadcasted_iota(jnp.int32, (SW_BLOCK, 3 * SW_BLOCK), 1)
    inwin = jnp.abs(j - SW_BLOCK - a) <= SW_BLOCK
    return j, inwin


SW_STACK = SW_HEADS * SW_BLOCK


def _sw_softmax(sc, sk):
    m = jnp.maximum(jnp.max(sc, axis=1, keepdims=True), sk)
    e = jnp.exp(sc - m)
    es = jnp.exp(sk - m)
    inv = 1.0 / (jnp.sum(e, axis=1, keepdims=True) + es)
    return e * inv, es * inv


def _sw_prologue(k_ref, v_ref, kp, vp, sink_ref, s):
    pad = s + 2 * SW_BLOCK
    zeros = jnp.zeros((SW_BLOCK, SW_KV_WIDTH), BF16)
    kp[0:SW_BLOCK, :] = zeros
    vp[0:SW_BLOCK, :] = zeros
    kp[SW_BLOCK + s:pad, :] = zeros
    vp[SW_BLOCK + s:pad, :] = zeros
    kp[SW_BLOCK:SW_BLOCK + s, :] = k_ref[...]
    vp[SW_BLOCK:SW_BLOCK + s, :] = v_ref[...]
    return jnp.concatenate([jnp.full((SW_BLOCK, 1), sink_ref[h], F32) for h in range(SW_HEADS)], axis=0)


def sw_fwd(qs, ks, zq, t5b, sink, dep, name):
    s = qs.shape[0]
    nb = s // SW_BLOCK
    v_blk = (3 * NA_WIDTH + SW_Q_WIDTH + SW_KV_WIDTH) // LANES
    pad = s + 2 * SW_BLOCK

    def body(q_ref, k_ref, v_ref, b_ref, sink_ref, dep_ref, o_ref, kp, vp, s_scr, p_scr):
        sink_col = _sw_prologue(k_ref, v_ref, kp, vp, sink_ref, s)
        j, inwin = _sw_mask_iotas()

        def blk(n, carry):
            kpos = n * SW_BLOCK - SW_BLOCK + j
            madd = jnp.where(inwin & (kpos >= 0) & (kpos < s), 0.0, NEG)
            q0 = pl.multiple_of(n * SW_BLOCK, SW_BLOCK)
            qr, kr = pl.ds(q0, SW_BLOCK), pl.ds(q0, 3 * SW_BLOCK)
            for h in range(SW_HEADS):
                g = h // SW_REP
                s_scr[SW_BLOCK * h:SW_BLOCK * (h + 1), :] = _dotg(
                    q_ref[qr, HEAD_DIM * h:HEAD_DIM * (h + 1)], kp[kr, HEAD_DIM * g:HEAD_DIM * (g + 1)], NT) + madd
            p, _ = _sw_softmax(s_scr[...] + b_ref[...], sink_col)
            p_scr[...] = p.astype(BF16)
            for h in range(SW_HEADS):
                g = h // SW_REP
                o_ref[qr, HEAD_DIM * h:HEAD_DIM * (h + 1)] = _dot(
                    p_scr[SW_BLOCK * h:SW_BLOCK * (h + 1), :], vp[kr, HEAD_DIM * g:HEAD_DIM * (g + 1)]).astype(BF16)
            return carry

        lax.fori_loop(0, nb, blk, 0)

    return pl.pallas_call(
        body, name=name, grid=(1,),
        in_specs=[_full((s, SW_Q_WIDTH)), _full((s, SW_KV_WIDTH)),
                  pl.BlockSpec((s, SW_KV_WIDTH), lambda i: (0, v_blk)),
                  _full((SW_STACK, 3 * SW_BLOCK)), pl.BlockSpec(memory_space=pltpu.SMEM),
                  _full(dep.shape)],
        out_specs=_full((s, SW_Q_WIDTH)),
        out_shape=jax.ShapeDtypeStruct((s, SW_Q_WIDTH), BF16),
        scratch_shapes=[pltpu.VMEM((pad, SW_KV_WIDTH), BF16), pltpu.VMEM((pad, SW_KV_WIDTH), BF16),
                        pltpu.VMEM((SW_STACK, 3 * SW_BLOCK), F32), pltpu.VMEM((SW_STACK, 3 * SW_BLOCK), BF16)],
        compiler_params=_params(),
    )(qs, ks, zq, t5b, sink, dep)


def sw_bwd(qs, ks, zq, t5b, sink, o_sw, do_sw, name):
    s = qs.shape[0]
    nb = s // SW_BLOCK
    v_blk = (3 * NA_WIDTH + SW_Q_WIDTH + SW_KV_WIDTH) // LANES
    pad = s + 2 * SW_BLOCK

    def body(q_ref, k_ref, v_ref, b_ref, sink_ref, o_ref, do_ref,
             dq_ref, dk_ref, dv_ref, db_ref, dsk_ref, kp, vp, dkp, dvp, s_scr, dp_scr, ds_scr, p_scr):
        sink_col = _sw_prologue(k_ref, v_ref, kp, vp, sink_ref, s)
        dkp[...] = jnp.zeros(dkp.shape, F32)
        dvp[...] = jnp.zeros(dvp.shape, F32)
        db_ref[...] = jnp.zeros(db_ref.shape, F32)
        dsk_ref[...] = jnp.zeros(dsk_ref.shape, F32)
        j, inwin = _sw_mask_iotas()

        def blk(n, carry):
            kpos = n * SW_BLOCK - SW_BLOCK + j
            madd = jnp.where(inwin & (kpos >= 0) & (kpos < s), 0.0, NEG)
            q0 = pl.multiple_of(n * SW_BLOCK, SW_BLOCK)
            qr, kr = pl.ds(q0, SW_BLOCK), pl.ds(q0, 3 * SW_BLOCK)
            deltas = []
            for h in range(SW_HEADS):
                g = h // SW_REP
                hl, kl = slice(HEAD_DIM * h, HEAD_DIM * (h + 1)), slice(HEAD_DIM * g, HEAD_DIM * (g + 1))
                rows = slice(SW_BLOCK * h, SW_BLOCK * (h + 1))
                do = do_ref[qr, hl]
                s_scr[rows, :] = _dotg(q_ref[qr, hl], kp[kr, kl], NT) + madd
                dp_scr[rows, :] = _dotg(do, vp[kr, kl], NT)
                deltas.append(jnp.sum(do.astype(F32) * o_ref[qr, hl].astype(F32), axis=1, keepdims=True))
            delta = jnp.concatenate(deltas, axis=0)
            p, ps = _sw_softmax(s_scr[...] + b_ref[...], sink_col)
            ds = p * (dp_scr[...] - delta)
            db_ref[...] = db_ref[...] + ds
            dsk_ref[...] = dsk_ref[...] - jnp.broadcast_to(ps * delta, (SW_STACK, LANES))
            ds_scr[...] = ds.astype(BF16)
            p_scr[...] = p.astype(BF16)
            for g in range(SW_HEADS // SW_REP):
                kl = slice(HEAD_DIM * g, HEAD_DIM * (g + 1))
                k = kp[kr, kl]
                dkw = jnp.zeros((3 * SW_BLOCK, HEAD_DIM), F32)
                dvw = jnp.zeros((3 * SW_BLOCK, HEAD_DIM), F32)
                for r in range(SW_REP):
                    h = g * SW_REP + r
                    hl, rows = slice(HEAD_DIM * h, HEAD_DIM * (h + 1)), slice(SW_BLOCK * h, SW_BLOCK * (h + 1))
                    dsb = ds_scr[rows, :]
                    dq_ref[qr, hl] = _dot(dsb, k)
                    dkw = dkw + _dotg(dsb, q_ref[qr, hl], TN)
                    dvw = dvw + _dotg(p_scr[rows, :], do_ref[qr, hl], TN)
                dkp[kr, kl] = dkp[kr, kl] + dkw
                dvp[kr, kl] = dvp[kr, kl] + dvw
            return carry

        lax.fori_loop(0, nb, blk, 0)
        dk_ref[...] = dkp[SW_BLOCK:SW_BLOCK + s, :]
        dv_ref[...] = dvp[SW_BLOCK:SW_BLOCK + s, :]

    bias_spec = _full((SW_STACK, 3 * SW_BLOCK))
    return pl.pallas_call(
        body, name=name, grid=(1,),
        in_specs=[_full((s, SW_Q_WIDTH)), _full((s, SW_KV_WIDTH)),
                  pl.BlockSpec((s, SW_KV_WIDTH), lambda i: (0, v_blk)),
                  bias_spec, pl.BlockSpec(memory_space=pltpu.SMEM),
                  _full((s, SW_Q_WIDTH)), _full((s, SW_Q_WIDTH))],
        out_specs=[_full((s, SW_Q_WIDTH)), _full((s, SW_KV_WIDTH)), _full((s, SW_KV_WIDTH)), bias_spec,
                   _full((SW_STACK, LANES))],
        out_shape=[jax.ShapeDtypeStruct((s, SW_Q_WIDTH), F32), jax.ShapeDtypeStruct((s, SW_KV_WIDTH), F32),
                   jax.ShapeDtypeStruct((s, SW_KV_WIDTH), F32),
                   jax.ShapeDtypeStruct((SW_STACK, 3 * SW_BLOCK), F32),
                   jax.ShapeDtypeStruct((SW_STACK, LANES), F32)],
        scratch_shapes=[pltpu.VMEM((pad, SW_KV_WIDTH), BF16), pltpu.VMEM((pad, SW_KV_WIDTH), BF16),
                        pltpu.VMEM((pad, SW_KV_WIDTH), F32), pltpu.VMEM((pad, SW_KV_WIDTH), F32),
                        pltpu.VMEM((SW_STACK, 3 * SW_BLOCK), F32), pltpu.VMEM((SW_STACK, 3 * SW_BLOCK), F32),
                        pltpu.VMEM((SW_STACK, 3 * SW_BLOCK), BF16), pltpu.VMEM((SW_STACK, 3 * SW_BLOCK), BF16)],
        compiler_params=_params(),
    )(qs, ks, zq, t5b, sink, o_sw, do_sw)


def merge_out(x, o_na, o_sw, gt, wbna_t, wbsw_t, wout, name):
    s, d = x.shape
    tm = _row_tile(s)

    def body(x_ref, ona_ref, osw_ref, gt_ref, wna_ref, wsw_ref, wo_ref, xo_ref, ana_ref, asw_ref, mg_ref):
        a_na = _dotg(ona_ref[...], wna_ref[...], NT)
        a_sw = _dotg(osw_ref[...], wsw_ref[...], NT)
        g_na, g_sw = gt_ref[:, 0:d].astype(F32), gt_ref[:, d:2 * d].astype(F32)
        ana_ref[...] = (a_na * g_na * (1.0 - g_na)).astype(BF16)
        asw_ref[...] = (a_sw * g_sw * (1.0 - g_sw)).astype(BF16)
        merged = (g_na * a_na + g_sw * a_sw).astype(BF16)
        mg_ref[...] = merged
        xo_ref[...] = x_ref[...] + _dot(merged, wo_ref[...])

    return pl.pallas_call(
        body, name=name, grid=(s // tm,),
        in_specs=[_rows(tm, d), _rows(tm, 512), _rows(tm, 512), _rows(tm, 2 * d),
                  _mat(*wbna_t), _mat(*wbsw_t), _mat(*wout)],
        out_specs=[_rows(tm, d)] * 4,
        out_shape=[jax.ShapeDtypeStruct((s, d), F32)] + [jax.ShapeDtypeStruct((s, d), BF16)] * 3,
        compiler_params=_params(),
    )(x, o_na, o_sw, gt, wbna_t[0], wbsw_t[0], wout[0])


def mix_bwd_out(dx, gt, a_na, a_sw, wbna_t, wbsw_t, wout, dep, name):
    s, d = dx.shape
    tm = _row_tile(s)

    def body(dx_ref, gt_ref, ana_ref, asw_ref, wna_ref, wsw_ref, wo_ref, dep_ref,
             dxb_ref, dzg_ref, dana_ref, dasw_ref, dona_ref, dosw_ref, dbg_ref):
        @pl.when(pl.program_id(0) == 0)
        def _():
            dbg_ref[...] = jnp.zeros(dbg_ref.shape, F32)

        dxb = dx_ref[...].astype(BF16)
        dxb_ref[...] = dxb
        dm = _dotg(dxb, wo_ref[...], NT)
        for i, (a_ref, da_ref, w_ref, do_ref) in enumerate(
                [(ana_ref, dana_ref, wna_ref, dona_ref), (asw_ref, dasw_ref, wsw_ref, dosw_ref)]):
            gi = gt_ref[:, i * d:(i + 1) * d].astype(F32)
            da = (dm * gi).astype(BF16)
            da_ref[...] = da
            do_ref[...] = _dot(da, w_ref[...]).astype(BF16)
            dzg = dm * a_ref[...].astype(F32)
            dzg_ref[:, i * d:(i + 1) * d] = dzg.astype(BF16)
            dbg_ref[:, i * d:(i + 1) * d] = dbg_ref[:, i * d:(i + 1) * d] + jnp.sum(dzg, axis=0, keepdims=True)

    return pl.pallas_call(
        body, name=name, grid=(s // tm,),
        in_specs=[_rows(tm, d), _rows(tm, 2 * d), _rows(tm, d), _rows(tm, d),
                  _mat(*wbna_t), _mat(*wbsw_t), _mat(*wout), _full(dep.shape)],
        out_specs=[_rows(tm, d), _rows(tm, 2 * d), _rows(tm, d), _rows(tm, d), _rows(tm, 512), _rows(tm, 512),
                   _full((1, 2 * d))],
        out_shape=[jax.ShapeDtypeStruct((s, d), BF16), jax.ShapeDtypeStruct((s, 2 * d), BF16),
                   jax.ShapeDtypeStruct((s, d), BF16), jax.ShapeDtypeStruct((s, d), BF16),
                   jax.ShapeDtypeStruct((s, 512), BF16), jax.ShapeDtypeStruct((s, 512), BF16),
                   jax.ShapeDtypeStruct((1, 2 * d), F32)],
        compiler_params=_params(),
    )(dx, gt, a_na, a_sw, wbna_t[0], wbsw_t[0], wout[0], dep)


def qk_norm_bwd(dqa, dka, dva, dqs, dks, dvs, zq, dzg, gq_na, gk_na, gq_sw, gk_sw, bd, name):
    s = zq.shape[0]
    d2 = dzg.shape[1]
    n_in = QKV_WIDTH + d2
    tm = _row_tile(s)

    def body(dqa_ref, dka_ref, dva_ref, dqs_ref, dks_ref, dvs_ref, zq_ref, dzg_ref,
             gqa_ref, gka_ref, gqs_ref, gks_ref, bd_ref, dz_ref, dgqa_ref, dgka_ref, dgqs_ref, dgks_ref):
        @pl.when(pl.program_id(0) == 0)
        def _():
            for r in (dgqa_ref, dgka_ref, dgqs_ref, dgks_ref):
                r[...] = jnp.zeros(r.shape, F32)

        bd512 = bd_ref[...]
        bd128 = bd_ref[0:SW_KV_WIDTH, 0:SW_KV_WIDTH]

        def one(c0, c1, dy_ref, g_ref, dg_ref, bdm, scale):
            z = zq_ref[:, c0:c1].astype(F32)
            r = lax.rsqrt(_group_mean(z * z, bdm) + EPS)
            zh = z * r
            dy = dy_ref[...] * scale
            dyg = dy * g_ref[...]
            dz = r * (dyg - zh * _group_mean(dyg * zh, bdm))
            dz_ref[:, c0:c1] = dz.astype(BF16)
            dg_ref[...] = dg_ref[...] + jnp.sum(dy * zh, axis=0, keepdims=True)

        one(0, 512, dqa_ref, gqa_ref, dgqa_ref, bd512, SCALE)
        one(512, 1024, dka_ref, gka_ref, dgka_ref, bd512, 1.0)
        dz_ref[:, 1024:1536] = dva_ref[...].astype(BF16)
        one(1536, 2048, dqs_ref, gqs_ref, dgqs_ref, bd512, SCALE)
        one(2048, 2176, dks_ref, gks_ref, dgks_ref, bd128, 1.0)
        dz_ref[:, 2176:2304] = dvs_ref[...].astype(BF16)
        dz_ref[:, QKV_WIDTH:n_in] = dzg_ref[...]

    return pl.pallas_call(
        body, name=name, grid=(s // tm,),
        in_specs=[_rows(tm, 512), _rows(tm, 512), _rows(tm, 512), _rows(tm, 512), _rows(tm, 128), _rows(tm, 128),
                  _rows(tm, QKV_WIDTH), _rows(tm, d2),
                  _full((1, 512)), _full((1, 512)), _full((1, 512)), _full((1, 128)), _full((MXU_TILE, MXU_TILE))],
        out_specs=[_rows(tm, n_in), _full((1, 512)), _full((1, 512)), _full((1, 512)), _full((1, 128))],
        out_shape=[jax.ShapeDtypeStruct((s, n_in), BF16)] + [jax.ShapeDtypeStruct((1, 512), F32)] * 3
                  + [jax.ShapeDtypeStruct((1, 128), F32)],
        compiler_params=_params(),
    )(dqa, dka, dva, dqs, dks, dvs, zq, dzg, gq_na, gk_na, gq_sw, gk_sw, bd)


def ffn_bwd_act(dx, wd, hg, hu, name):
    s, d = dx.shape
    f = wd[0].shape[1]
    tm = _row_tile(s)
    fc = _col_chunk(f)

    def body(dx_ref, w_ref, hg_ref, hu_ref, dxb_ref, dhg_ref, dhu_ref):
        dxv = dx_ref[...]
        dxb_ref[...] = dxv.astype(BF16)
        half = (0.5 * dxv).astype(BF16)
        for c0 in range(0, f, fc):
            dact = _dotg(half, w_ref[c0:c0 + fc, :], NT)
            dhu_ref[:, c0:c0 + fc] = (dact * hu_ref[:, c0:c0 + fc].astype(F32)).astype(BF16)
            dhg_ref[:, c0:c0 + fc] = (dact * hg_ref[:, c0:c0 + fc].astype(F32)).astype(BF16)

    return pl.pallas_call(
        body, name=name, grid=(s // tm,),
        in_specs=[_rows(tm, d), _mat(*wd), _rows(tm, f), _rows(tm, f)],
        out_specs=[_rows(tm, d), _rows(tm, f), _rows(tm, f)],
        out_shape=[jax.ShapeDtypeStruct((s, d), BF16), jax.ShapeDtypeStruct((s, f), BF16),
                   jax.ShapeDtypeStruct((s, f), BF16)],
        compiler_params=_params(),
    )(dx, wd[0], hg, hu)


def proj_bwd_norm(acts, weights, x, gain, dx, dep, name):
    s, d = x.shape
    tm = min(_row_tile(s), 256)
    n = len(acts)

    def body(*refs):
        a_refs, w_refs = refs[:n], refs[n:2 * n]
        x_ref, g_ref, dx_ref, _, o_ref, dg_ref = refs[2 * n:]

        @pl.when(pl.program_id(0) == 0)
        def _():
            dg_ref[...] = jnp.zeros(dg_ref.shape, F32)

        dxn = _dot(a_refs[0][...], w_refs[0][...])
        for a_ref, w_ref in zip(a_refs[1:], w_refs[1:]):
            dxn = dxn + _dot(a_ref[...], w_ref[...])
        xv = x_ref[...]
        r = _rstd(xv)
        xh = xv * r
        dxh = dxn * g_ref[...]
        o_ref[...] = dx_ref[...] + r * (dxh - xh * jnp.mean(dxh * xh, axis=-1, keepdims=True))
        dg_ref[...] = dg_ref[...] + jnp.sum(dxn * xh, axis=0, keepdims=True)

    return pl.pallas_call(
        body, name=name, grid=(s // tm,),
        in_specs=[_rows(tm, a.shape[1]) for a in acts] + [_mat(*w) for w in weights]
                 + [_rows(tm, d), _full((1, d)), _rows(tm, d), _full(dep.shape)],
        out_specs=[_rows(tm, d), _full((1, d))],
        out_shape=[jax.ShapeDtypeStruct((s, d), F32), jax.ShapeDtypeStruct((1, d), F32)],
        compiler_params=_params(),
    )(*acts, *[w[0] for w in weights], x, gain, dx, dep)


def tn_matmul(products, name):
    s, n = products[0][0].shape
    tn = _tn_tile(n) if len(products) == 1 else _col_chunk(n)
    rhs = []
    for _, b, _ in products:
        if not any(b is seen for seen in rhs):
            rhs.append(b)
    which = [next(i for i, seen in enumerate(rhs) if b is seen) for _, b, _ in products]
    npr, nr = len(products), len(rhs)

    def body(*refs):
        a_refs, b_refs, o_refs = refs[:npr], refs[npr:npr + nr], refs[npr + nr:]
        for i, (_, _, scale) in enumerate(products):
            o_refs[i][...] = (scale * _dotg(a_refs[i][...], b_refs[which[i]][...], TN)).astype(BF16)

    return pl.pallas_call(
        body, name=name, grid=(n // tn,),
        in_specs=[pl.BlockSpec((s, tn), lambda i: (0, i))] * npr
                 + [pl.BlockSpec(b.shape, lambda i: (0, 0), pipeline_mode=ONCE) for b in rhs],
        out_specs=[pl.BlockSpec((tn, b.shape[1]), lambda i: (i, 0)) for _, b, _ in products],
        out_shape=[jax.ShapeDtypeStruct((n, b.shape[1]), BF16) for _, b, _ in products],
        compiler_params=_params(),
    )(*[a for a, _, _ in products], *rhs)


def _mesh_pos():
    return lax.axis_index("x"), lax.axis_index("y"), lax.axis_index("c")


def _peers():
    x, y, c = _mesh_pos()
    peers = []
    for rel in range(1, N_DEV):
        peers.append((1 - x if rel & 4 else x, 1 - y if rel & 2 else y, 1 - c if rel & 1 else c))
    return 4 * x + 2 * y + c, peers


HBM_SPEC = pl.BlockSpec(memory_space=pltpu.HBM)
SEM_SPEC = pl.BlockSpec(memory_space=pltpu.SEMAPHORE)


def _split_call(body, name, thru, n_sems, extra=(), with_token=True):
    hbm = lambda t: pltpu.with_memory_space_constraint(t, pltpu.HBM)
    effect = pltpu.CompilerParams(has_side_effects=pltpu.SideEffectType.DATAFLOW_SIDE_EFFECTING)
    nt = len(thru)
    thru_shapes = [pltpu.HBM(t.shape, t.dtype) for t in thru]
    if with_token:
        (after,) = extra
        outs = pl.pallas_call(
            body, name=name, in_specs=[HBM_SPEC] * nt + [pl.BlockSpec(memory_space=pl.ANY)],
            out_specs=[SEM_SPEC] * len(n_sems) + [HBM_SPEC] * nt + [pl.BlockSpec(memory_space=pltpu.VMEM)],
            out_shape=[pltpu.SemaphoreType.DMA((k,)) for k in n_sems] + thru_shapes
                      + [jax.ShapeDtypeStruct((8, LANES), F32)],
            input_output_aliases={i: len(n_sems) + i for i in range(nt)}, compiler_params=effect,
        )(*[hbm(t) for t in thru], after)
        return outs[:len(n_sems)], outs[len(n_sems):-1], outs[-1]
    return pl.pallas_call(
        body, name=name,
        in_specs=[HBM_SPEC] * nt + [SEM_SPEC] * len(n_sems) + [pl.BlockSpec(memory_space=pl.ANY)],
        out_specs=[HBM_SPEC] * nt, out_shape=thru_shapes,
        input_output_aliases={i: i for i in range(nt)}, compiler_params=effect,
    )(*thru, *extra)


def _gather_targets():
    x, y, c = _mesh_pos()
    return 4 * x + 2 * y + c, [(x, y, 1 - c), (1 - x, y, c), (x, 1 - y, c), (1 - x, 1 - y, c)]


def gather_start(shards, after, name):
    n = len(shards)
    zones = [lax.empty((w.shape[0], N_DEV) + w.shape[1:], w.dtype) for w in shards]

    def body(*refs):
        ins, zs = refs[:n], refs[n:2 * n]
        send_sems, recv_sems, local_sems = refs[2 * n + 1:2 * n + 4]
        token = refs[-1]
        me, targets = _gather_targets()
        for a in range(n):
            pltpu.make_async_copy(ins[a], zs[a].at[:, me], local_sems.at[a]).start()
            for k, to in enumerate(targets):
                pltpu.make_async_remote_copy(
                    src_ref=ins[a], dst_ref=zs[a].at[:, me], send_sem=send_sems.at[4 * a + k],
                    recv_sem=recv_sems.at[4 * a + k], device_id=to, device_id_type=MESH).start()
        token[...] = jnp.zeros(token.shape, F32)

    sems, thru, token = _split_call(body, name, list(shards) + zones, (4 * n, 4 * n, n), extra=(after,))
    return (sems, thru, n), token


def gather_wait(started, after, name):
    sems, thru, n = started

    def body(*refs):
        zs = refs[n:2 * n]
        send_sems, recv_sems, local_sems = refs[2 * n:2 * n + 3]
        _, targets = _gather_targets()
        for a in range(n):
            for k, to in enumerate(targets):
                cp = pltpu.make_async_remote_copy(
                    src_ref=zs[a].at[:, 0], dst_ref=zs[a].at[:, 0], send_sem=send_sems.at[4 * a + k],
                    recv_sem=recv_sems.at[4 * a + k], device_id=to, device_id_type=MESH)
                cp.wait_send()
                cp.wait_recv()
            pltpu.make_async_copy(zs[a].at[:, 0], zs[a].at[:, 0], local_sems.at[a]).wait()

    return _split_call(body, name, thru, (4 * n, 4 * n, n), extra=(*sems, after), with_token=False)[n:]


def forward_start(zones, after, name):
    n = len(zones)

    def body(*refs):
        zs = refs[:n]
        send_sems, recv_sems = refs[n + 1:n + 3]
        token = refs[-1]
        x, y, c = _mesh_pos()
        for a in range(n):
            for j, chip in enumerate([(1 - x, y), (x, 1 - y), (1 - x, 1 - y)]):
                blk = zs[a].at[:, 4 * chip[0] + 2 * chip[1] + c]
                pltpu.make_async_remote_copy(
                    src_ref=blk, dst_ref=blk, send_sem=send_sems.at[3 * a + j], recv_sem=recv_sems.at[3 * a + j],
                    device_id=(x, y, 1 - c), device_id_type=MESH).start()
        token[...] = jnp.zeros(token.shape, F32)

    sems, thru, token = _split_call(body, name, list(zones), (3 * n, 3 * n), extra=(after,))
    return (sems, thru, n), token


def forward_wait(started, after, name):
    sems, thru, n = started

    def body(*refs):
        zs = refs[:n]
        send_sems, recv_sems = refs[n:n + 2]
        x, y, c = _mesh_pos()
        for a in range(n):
            for j in range(3):
                cp = pltpu.make_async_remote_copy(
                    src_ref=zs[a].at[:, 0], dst_ref=zs[a].at[:, 0], send_sem=send_sems.at[3 * a + j],
                    recv_sem=recv_sems.at[3 * a + j], device_id=(x, y, 1 - c), device_id_type=MESH)
                cp.wait_send()
                cp.wait_recv()

    return _split_call(body, name, thru, (3 * n, 3 * n), extra=(*sems, after), with_token=False)


def scatter_start(groups, name):
    n = len(groups)
    flat = [g for grp in groups for g in grp]
    nf = len(flat)
    offs = np.cumsum([0] + [len(grp) for grp in groups])
    lands = [lax.empty((N_DEV, len(grp)) + grp[0].shape[1:], grp[0].dtype) for grp in groups]

    def body(*refs):
        ins, zones = refs[:nf], refs[nf:nf + n]
        send_sems, recv_sems, local_sems = refs[nf + n:nf + n + 3]
        token = refs[-1]
        me, peers = _peers()
        for a in range(n):
            for w in range(len(groups[a])):
                pltpu.make_async_copy(ins[offs[a] + w].at[me], zones[a].at[me, w], local_sems.at[a]).start()
        for k, peer in enumerate(peers):
            p_id = 4 * peer[0] + 2 * peer[1] + peer[2]
            for a in range(n):
                for w in range(len(groups[a])):
                    pltpu.make_async_remote_copy(
                        src_ref=ins[offs[a] + w].at[p_id], dst_ref=zones[a].at[me, w],
                        send_sem=send_sems.at[7 * a + k], recv_sem=recv_sems.at[7 * a + k],
                        device_id=peer, device_id_type=MESH).start()
        token[...] = jnp.zeros(token.shape, F32)

    hbm = lambda t: pltpu.with_memory_space_constraint(t, pltpu.HBM)
    outs = pl.pallas_call(
        body, name=name,
        in_specs=[HBM_SPEC] * (nf + n),
        out_specs=[SEM_SPEC] * 3 + [HBM_SPEC] * (nf + n) + [pl.BlockSpec(memory_space=pltpu.VMEM)],
        out_shape=[pltpu.SemaphoreType.DMA((7 * n,)), pltpu.SemaphoreType.DMA((7 * n,)), pltpu.SemaphoreType.DMA((n,))]
                  + [pltpu.HBM(t.shape, t.dtype) for t in flat + lands]
                  + [jax.ShapeDtypeStruct((8, LANES), F32)],
        input_output_aliases={i: 3 + i for i in range(nf + n)},
        compiler_params=pltpu.CompilerParams(has_side_effects=pltpu.SideEffectType.DATAFLOW_SIDE_EFFECTING),
    )(*[hbm(t) for t in flat], *[hbm(t) for t in lands])
    sems, thru, token = outs[:3], outs[3:3 + nf + n], outs[-1]
    return (sems, thru, [len(grp) for grp in groups]), token


def scatter_wait(started, after, name):
    (send_sems, recv_sems, local_sems), thru, sizes = started
    n = len(sizes)
    nf = len(thru) - n

    def body(*refs):
        zones = refs[nf:nf + n]
        s_sems, r_sems, l_sems = refs[nf + n:nf + n + 3]
        me, peers = _peers()
        for a in range(n):
            for k, peer in enumerate(peers):
                cp = pltpu.make_async_remote_copy(
                    src_ref=zones[a].at[0], dst_ref=zones[a].at[0],
                    send_sem=s_sems.at[7 * a + k], recv_sem=r_sems.at[7 * a + k], device_id=peer,
                    device_id_type=MESH)
                cp.wait_send()
                cp.wait_recv()
            pltpu.make_async_copy(zones[a].at[0], zones[a].at[0], l_sems.at[a]).wait()

    outs = pl.pallas_call(
        body, name=name,
        in_specs=[HBM_SPEC] * (nf + n) + [SEM_SPEC] * 3 + [pl.BlockSpec(memory_space=pl.ANY)],
        out_specs=[HBM_SPEC] * (nf + n),
        out_shape=[pltpu.HBM(t.shape, t.dtype) for t in thru],
        input_output_aliases={i: i for i in range(nf + n)},
        compiler_params=pltpu.CompilerParams(has_side_effects=pltpu.SideEffectType.DATAFLOW_SIDE_EFFECTING),
    )(*thru, send_sems, recv_sems, local_sems, after)
    return outs[nf:]


def pair_start(grads, after, name):
    nw = len(grads)
    land = lax.empty((4, nw) + grads[0].shape[1:], grads[0].dtype)

    def body(*refs):
        ins, zone = refs[:nw], refs[nw]
        send_sems, recv_sems = refs[nw + 2:nw + 4]
        x, y, c = _mesh_pos()
        for j in range(4):
            for w in range(nw):
                pltpu.make_async_remote_copy(
                    src_ref=ins[w].at[2 * j + (1 - c)], dst_ref=zone.at[j, w], send_sem=send_sems.at[0],
                    recv_sem=recv_sems.at[0], device_id=(x, y, 1 - c), device_id_type=MESH).start()
        refs[-1][...] = jnp.zeros(refs[-1].shape, F32)

    sems, thru, token = _split_call(body, name, list(grads) + [land], (1, 1), extra=(after,))
    return (sems, thru, nw), token


def pair_wait(started, after, name):
    sems, thru, nw = started

    def body(*refs):
        zone = refs[nw]
        send_sems, recv_sems = refs[nw + 1:nw + 3]
        x, y, c = _mesh_pos()
        cp = pltpu.make_async_remote_copy(src_ref=zone, dst_ref=zone, send_sem=send_sems.at[0],
                                          recv_sem=recv_sems.at[0], device_id=(x, y, 1 - c), device_id_type=MESH)
        cp.wait_send()
        cp.wait_recv()

    outs = _split_call(body, name, thru, (1, 1), extra=(*sems, after), with_token=False)
    return outs[:nw], outs[nw]


def pair_sum(grads, land, name):
    nw = len(grads)
    _, r, c_dim = grads[0].shape

    def body(*refs):
        g_refs, l_ref, o_ref = refs[:nw], refs[nw], refs[nw + 1]
        core = lax.axis_index("c")
        for w in range(nw):
            o_ref[0, w] = (g_refs[w][0, core].astype(F32) + l_ref[0, w].astype(F32)).astype(BF16)

    return pl.pallas_call(
        body, name=name, grid=(4,),
        in_specs=[pl.BlockSpec((1, 2, r, c_dim), lambda j: (j, 0, 0, 0))] * nw
                 + [pl.BlockSpec((1, nw, r, c_dim), lambda j: (j, 0, 0, 0))],
        out_specs=pl.BlockSpec((1, nw, r, c_dim), lambda j: (j, 0, 0, 0)),
        out_shape=jax.ShapeDtypeStruct((4, nw, r, c_dim), BF16),
        compiler_params=_params(),
    )(*[g.reshape(4, 2, r, c_dim) for g in grads], land)


def _other_chips():
    x, y, c = _mesh_pos()
    chips = []
    for rel in range(1, 4):
        px, py = (1 - x if rel & 2 else x), (1 - y if rel & 1 else y)
        chips.append((px, py, 2 * px + py))
    return 2 * x + y, c, chips


def chip_start(pair_sums, after, name):
    land = lax.empty(pair_sums.shape, pair_sums.dtype)

    def body(*refs):
        h_ref, zone = refs[0], refs[1]
        send_sems, recv_sems, local_sem = refs[3:6]
        mine, c, chips = _other_chips()
        pltpu.make_async_copy(h_ref.at[mine], zone.at[mine], local_sem.at[0]).start()
        for k, (px, py, j) in enumerate(chips):
            pltpu.make_async_remote_copy(
                src_ref=h_ref.at[j], dst_ref=zone.at[mine], send_sem=send_sems.at[k], recv_sem=recv_sems.at[k],
                device_id=(px, py, c), device_id_type=MESH).start()
        refs[-1][...] = jnp.zeros(refs[-1].shape, F32)

    sems, thru, token = _split_call(body, name, [pair_sums, land], (3, 3, 1), extra=(after,))
    return (sems, thru), token


def chip_wait(started, after, name):
    sems, thru = started

    def body(*refs):
        zone = refs[1]
        send_sems, recv_sems, local_sem = refs[2:5]
        _, c, chips = _other_chips()
        for k, (px, py, _) in enumerate(chips):
            cp = pltpu.make_async_remote_copy(
                src_ref=zone.at[0], dst_ref=zone.at[0], send_sem=send_sems.at[k], recv_sem=recv_sems.at[k],
                device_id=(px, py, c), device_id_type=MESH)
            cp.wait_send()
            cp.wait_recv()
        pltpu.make_async_copy(zone.at[0], zone.at[0], local_sem.at[0]).wait()

    return _split_call(body, name, thru, (3, 3, 1), extra=(*sems, after), with_token=False)[1]


def share_small(parts, after):
    n = len(parts)

    def body(*refs):
        ins, outs = refs[:n], refs[n + 1:2 * n + 1]
        send_sems, recv_sems, local_sems = refs[2 * n + 1:]
        me, peers = _peers()
        copies = []
        for i in range(n):
            copies.append(pltpu.make_async_copy(ins[i], outs[i].at[me], local_sems.at[i]))
            copies += [pltpu.make_async_remote_copy(
                src_ref=ins[i], dst_ref=outs[i].at[me], send_sem=send_sems.at[7 * i + k],
                recv_sem=recv_sems.at[7 * i + k], device_id=peer, device_id_type=MESH)
                for k, peer in enumerate(peers)]
        for cp in copies:
            cp.start()
        for cp in copies:
            cp.wait()

    vm = pl.BlockSpec(memory_space=pltpu.VMEM)
    return pl.pallas_call(
        body, name="share_small", in_specs=[vm] * n + [pl.BlockSpec(memory_space=pl.ANY)], out_specs=[vm] * n,
        out_shape=[jax.ShapeDtypeStruct((N_DEV,) + p.shape, p.dtype) for p in parts],
        scratch_shapes=[pltpu.SemaphoreType.DMA((7 * n,)), pltpu.SemaphoreType.DMA((7 * n,)),
                        pltpu.SemaphoreType.DMA((n,))],
    )(*parts, after)


def _adamw_math(w, g, m, v):
    m = ADAM_B1 * m + (1.0 - ADAM_B1) * g
    v = ADAM_B2 * v + (1.0 - ADAM_B2) * (g * g)
    m_hat = m / (1.0 - ADAM_B1 ** ADAM_STEP)
    v_hat = v / (1.0 - ADAM_B2 ** ADAM_STEP)
    delta = -ADAM_LR * (m_hat / (jnp.sqrt(v_hat) + ADAM_EPS) + ADAM_WD * w)
    return delta, m, v


ADAMW_BLOCK_BYTES = 24 * 1024 * 1024


def adamw_layer(zone, layer, items, after, name):
    n_src, nw, r, c = zone.shape
    depth = items[0][0].shape[0]
    prevs = [p if p is not None else tuple(lax.empty((depth, r, c), F32) for _ in range(4)) for _, _, _, p in items]
    row_bytes = 2 * nw * c * (2 * n_src + 4 * 7)
    tr = max(t for t in range(8, r + 1, 8) if r % t == 0 and t * row_bytes <= ADAMW_BLOCK_BYTES)

    def body(z_ref, *rest):
        ins, outs = rest[:3 * nw], rest[7 * nw + 1:]
        for i in range(nw):
            g = z_ref[0, i].astype(F32)
            for src in range(1, n_src):
                g = g + z_ref[src, i].astype(F32)
            g_ref, d_ref, mo_ref, vo_ref = outs[4 * i:4 * i + 4]
            w_ref, m_ref, v_ref = ins[3 * i:3 * i + 3]
            g_ref[...] = g
            d_ref[...], mo_ref[...], vo_ref[...] = _adamw_math(w_ref[...], g, m_ref[...], v_ref[...])

    rows = pl.BlockSpec((None, tr, c), lambda i: (layer, i, 0))
    anywhere = pl.BlockSpec(memory_space=pl.ANY)
    outs = pl.pallas_call(
        body, name=name, grid=(r // tr,),
        in_specs=[pl.BlockSpec((n_src, nw, tr, c), lambda i: (0, 0, i, 0))] + [rows] * (3 * nw)
                 + [anywhere] * (4 * nw + 1),
        out_specs=[rows] * (4 * nw),
        out_shape=[jax.ShapeDtypeStruct((depth, r, c), F32)] * (4 * nw),
        input_output_aliases={1 + 3 * nw + k: k for k in range(4 * nw)},
        compiler_params=_params(),
    )(zone, *[t for w, m, v, _ in items for t in (w, m, v)], *[t for p in prevs for t in p], after)
    return [tuple(outs[4 * i:4 * i + 4]) for i in range(nw)]


def adamw_small(ws, recvs, ms, vs, name):
    n = len(ws)

    def body(*refs):
        w_refs, r_refs, m_refs, v_refs = (refs[i * n:(i + 1) * n] for i in range(4))
        g_refs, d_refs, mo_refs, vo_refs = (refs[(4 + i) * n:(5 + i) * n] for i in range(4))
        for i in range(n):
            g = r_refs[i][0]
            for src in range(1, N_DEV):
                g = g + r_refs[i][src]
            g_refs[i][...] = g
            d_refs[i][...], mo_refs[i][...], vo_refs[i][...] = _adamw_math(w_refs[i][...], g, m_refs[i][...],
                                                                            v_refs[i][...])

    vm = pl.BlockSpec(memory_space=pltpu.VMEM)
    outs = pl.pallas_call(
        body, name=name, in_specs=[vm] * (4 * n), out_specs=[vm] * (4 * n),
        out_shape=[jax.ShapeDtypeStruct(w.shape, F32) for w in ws] * 4,
        compiler_params=pltpu.CompilerParams(vmem_limit_bytes=V7X_VMEM_LIMIT),
    )(*ws, *recvs, *ms, *vs)
    return [outs[i * n:(i + 1) * n] for i in range(4)]


SMALL_NAMES = ("ffn1_norm", "mix_norm", "ffn2_norm", "b_gate", "na_q_norm", "na_k_norm", "sw_q_norm", "sw_k_norm",
               "na_rpb", "sw_sink", "t5_rel_table")


def kernel(x, ffn1_norm, ffn1_w_gate, ffn1_w_up, ffn1_w_down, mix_norm, w_in, b_gate, na_q_norm, na_k_norm, na_rpb, sw_q_norm, sw_k_norm, sw_sink, t5_rel_table, w_branch_na, w_branch_sw, w_out, ffn2_norm, ffn2_w_gate, ffn2_w_up, ffn2_w_down, loss_target, m_ffn1_norm, m_ffn1_w_gate, m_ffn1_w_up, m_ffn1_w_down, m_mix_norm, m_w_in, m_b_gate, m_na_q_norm, m_na_k_norm, m_na_rpb, m_sw_q_norm, m_sw_k_norm, m_sw_sink, m_t5_rel_table, m_w_branch_na, m_w_branch_sw, m_w_out, m_ffn2_norm, m_ffn2_w_gate, m_ffn2_w_up, m_ffn2_w_down, v_ffn1_norm, v_ffn1_w_gate, v_ffn1_w_up, v_ffn1_w_down, v_mix_norm, v_w_in, v_b_gate, v_na_q_norm, v_na_k_norm, v_na_rpb, v_sw_q_norm, v_sw_k_norm, v_sw_sink, v_t5_rel_table, v_w_branch_na, v_w_branch_sw, v_w_out, v_ffn2_norm, v_ffn2_w_gate, v_ffn2_w_up, v_ffn2_w_down):
    weights = dict(ffn1_norm=ffn1_norm, ffn1_w_gate=ffn1_w_gate, ffn1_w_up=ffn1_w_up, ffn1_w_down=ffn1_w_down,
                   mix_norm=mix_norm, w_in=w_in, b_gate=b_gate, na_q_norm=na_q_norm, na_k_norm=na_k_norm,
                   na_rpb=na_rpb, sw_q_norm=sw_q_norm, sw_k_norm=sw_k_norm, sw_sink=sw_sink,
                   t5_rel_table=t5_rel_table, w_branch_na=w_branch_na, w_branch_sw=w_branch_sw, w_out=w_out,
                   ffn2_norm=ffn2_norm, ffn2_w_gate=ffn2_w_gate, ffn2_w_up=ffn2_w_up, ffn2_w_down=ffn2_w_down)
    mom_m = dict(ffn1_norm=m_ffn1_norm, ffn1_w_gate=m_ffn1_w_gate, ffn1_w_up=m_ffn1_w_up, ffn1_w_down=m_ffn1_w_down,
                 mix_norm=m_mix_norm, w_in=m_w_in, b_gate=m_b_gate, na_q_norm=m_na_q_norm, na_k_norm=m_na_k_norm,
                 na_rpb=m_na_rpb, sw_q_norm=m_sw_q_norm, sw_k_norm=m_sw_k_norm, sw_sink=m_sw_sink,
                 t5_rel_table=m_t5_rel_table, w_branch_na=m_w_branch_na, w_branch_sw=m_w_branch_sw, w_out=m_w_out,
                 ffn2_norm=m_ffn2_norm, ffn2_w_gate=m_ffn2_w_gate, ffn2_w_up=m_ffn2_w_up, ffn2_w_down=m_ffn2_w_down)
    mom_v = dict(ffn1_norm=v_ffn1_norm, ffn1_w_gate=v_ffn1_w_gate, ffn1_w_up=v_ffn1_w_up, ffn1_w_down=v_ffn1_w_down,
                 mix_norm=v_mix_norm, w_in=v_w_in, b_gate=v_b_gate, na_q_norm=v_na_q_norm, na_k_norm=v_na_k_norm,
                 na_rpb=v_na_rpb, sw_q_norm=v_sw_q_norm, sw_k_norm=v_sw_k_norm, sw_sink=v_sw_sink,
                 t5_rel_table=v_t5_rel_table, w_branch_na=v_w_branch_na, w_branch_sw=v_w_branch_sw, w_out=v_w_out,
                 ffn2_norm=v_ffn2_norm, ffn2_w_gate=v_ffn2_w_gate, ffn2_w_up=v_ffn2_w_up, ffn2_w_down=v_ffn2_w_down)
    order = list(weights)

    depth = ffn1_norm.shape[0]
    s, d = x.shape[1], x.shape[2]
    xs = x[0]
    tr = lambda w: jnp.swapaxes(w, -1, -2)

    merge = lambda t: t.reshape(t.shape[0], N_DEV * t.shape[2], t.shape[3])
    no_dep = jnp.zeros((8, LANES), F32)

    def shards_of(kind, l):
        stack = lambda *ws: jnp.stack(ws).astype(BF16)
        if kind == "ffn1":
            return [stack(tr(ffn1_w_gate[l]), tr(ffn1_w_up[l]), ffn1_w_down[l])]
        if kind == "win":
            return [stack(tr(w_in[l]))]
        return [stack(tr(ffn2_w_gate[l]), tr(ffn2_w_up[l]), ffn2_w_down[l]), stack(w_out[l]),
                stack(tr(w_branch_na[l]), tr(w_branch_sw[l]))]

    def start(kind, l, after):
        return gather_start(shards_of(kind, l), after, f"gather_{kind}_{l}")

    def arrive(started, kind, l, after):
        zones = gather_wait(started, after, f"gather_{kind}_{l}_wait")
        return forward_start(zones, no_dep, f"forward_{kind}_{l}")

    def finish(fwd, kind, l, after):
        return [merge(z) for z in forward_wait(fwd, after, f"forward_{kind}_{l}_wait")]

    bd = jnp.asarray(np.kron(np.eye(MXU_TILE // HEAD_DIM), np.full((HEAD_DIM, HEAD_DIM), 1.0 / HEAD_DIM)), BF16)
    bmap = jnp.asarray(_t5_bucket_map())
    tile8 = lambda g: jnp.tile(g, NA_WIDTH // HEAD_DIM).reshape(1, NA_WIDTH)
    tile2 = lambda g: jnp.tile(g, SW_KV_WIDTH // HEAD_DIM).reshape(1, SW_KV_WIDTH)

    st_first, tok = start("ffn1", 0, no_dep)
    t5b = t5_expand(t5_rel_table, bmap, tok, "t5_expand").reshape(SW_STACK, 3 * SW_BLOCK)
    t2_tables = [rpb_expand(_rpb_rows(na_rpb[l]), tok, f"rpb_expand_{l}") for l in range(depth)]
    fwd, _ = arrive(st_first, "ffn1", 0, t2_tables[-1])
    st_win, dep = start("win", 0, t5b)
    (first,) = finish(fwd, "ffn1", 0, dep)

    saved = []
    layer_w = {0: dict(wg1=(first, 0), wu1=(first, 1), wd1=(first, 2))}
    cur = xs
    for l in range(depth):
        sv = {}
        lw = layer_w[l]
        sv["x0"] = cur
        cur, sv["xn1"], sv["hg1"], sv["hu1"], sv["act1"] = ffn_forward(
            cur, ffn1_norm[l][None], lw["wg1"], lw["wu1"], lw["wd1"], dep, f"ffn1_{l}")
        sv["x1"] = cur
        fwd, _ = arrive(st_win, "win", l, cur)
        st_rest, tok = start("rest", l, cur)
        (zb,) = finish(fwd, "win", l, tok)
        lw["win"] = (zb, 0)
        sv["gains"] = (tile8(na_q_norm[l]), tile8(na_k_norm[l]), tile8(sw_q_norm[l]), tile2(sw_k_norm[l]))
        sv["hn"], sv["zq"], sv["qa"], sv["ka"], sv["qs"], sv["ks"], sv["gt"] = mix_in(
            cur, mix_norm[l][None], lw["win"], b_gate[l][None], *sv["gains"], bd, f"mix_in_{l}")
        sv["t2"] = t2_tables[l]
        sv["o_na"] = na_fwd(sv["qa"], sv["ka"], sv["zq"], sv["t2"], f"na_fwd_{l}")
        dep = no_dep
        if l + 1 < depth:
            st_ffn1, dep = start("ffn1", l + 1, sv["o_na"])
        sv["o_sw"] = sw_fwd(sv["qs"], sv["ks"], sv["zq"], t5b, sw_sink[l], dep, f"sw_fwd_{l}")
        fwd, tok = arrive(st_rest, "rest", l, sv["o_sw"])
        za, zc, zd = finish(fwd, "rest", l, tok)
        lw.update(wg2=(za, 0), wu2=(za, 1), wd2=(za, 2), wout=(zc, 0), wna=(zd, 0), wsw=(zd, 1))
        cur, sv["a_na"], sv["a_sw"], sv["merged"] = merge_out(
            cur, sv["o_na"], sv["o_sw"], sv["gt"], lw["wna"], lw["wsw"], lw["wout"], f"merge_out_{l}")
        sv["x2"] = cur
        dep = no_dep
        if l + 1 < depth:
            st_win, dep = start("win", l + 1, cur)
        sv["xn2"], sv["hg2"], sv["hu2"], sv["act2"] = ffn_forward(
            cur, ffn2_norm[l][None], lw["wg2"], lw["wu2"], None, dep, f"ffn2_up_{l}")
        dep = no_dep
        if l + 1 < depth:
            fwd, dep = arrive(st_ffn1, "ffn1", l + 1, sv["act2"])
        if l + 1 < depth:
            cur = ffn_down(cur, sv["act2"], lw["wd2"], dep, f"ffn2_down_{l}")
            (za,) = finish(fwd, "ffn1", l + 1, cur)
            layer_w[l + 1] = dict(wg1=(za, 0), wu1=(za, 1), wd1=(za, 2))
        else:
            dx, loss_acc = ffn_down(cur, sv["act2"], lw["wd2"], dep, f"ffn2_down_{l}", target=loss_target[0])
        dep = no_dep
        saved.append(sv)

    split = lambda t: t.reshape(N_DEV, t.shape[0] // N_DEV, t.shape[1])
    pending = {}
    last_key = "ffn1_0"
    two_level = {last_key}
    small = {k: [None] * depth for k in SMALL_NAMES if k != "t5_rel_table"}
    dbias_sw = []
    for l in reversed(range(depth)):
        sv = saved[l]
        lw = layer_w[l]
        wg1, wu1, wd1, wg2, wu2, wd2 = (lw[k] for k in ("wg1", "wu1", "wd1", "wg2", "wu2", "wd2"))
        win_t, wout_l, wna_t, wsw_t = lw["win"], lw["wout"], lw["wna"], lw["wsw"]
        blocks = ((2, "x2", "xn2", "hg2", "hu2", "act2", wg2, wu2, wd2, "ffn2_norm", 3),
                  (1, "x0", "xn1", "hg1", "hu1", "act1", wg1, wu1, wd1, "ffn1_norm", 0))

        def ffn_backward(dx, blk):
            tag, xk, xnk, hgk, huk, actk, wg, wu, wd, norm_name, slot = blk
            gains = weights[norm_name]
            dxb, dhg, dhu = ffn_bwd_act(dx, wd, sv[hgk], sv[huk], f"ffn{tag}_bwd_act_{l}")
            gwg, gwu, gwd = tn_matmul([(dhg, sv[xnk], 1.0), (dhu, sv[xnk], 1.0), (sv[actk], dxb, 0.5)],
                                      f"ffn{tag}_dw_{l}")
            key = f"ffn{tag}_{l}"
            blocks_of = [split(gwg), split(gwu), split(gwd)]
            if key in two_level:
                paired, token = pair_start(blocks_of, dxb, f"pair_{key}")
            else:
                pending[key], token = scatter_start([blocks_of], f"scatter_{key}")
            dx, dg = proj_bwd_norm([dhg, dhu], [wg, wu], sv[xk], gains[l][None], dx, token, f"ffn{tag}_bwd_x_{l}")
            token = no_dep
            if key in two_level:
                thru, land = pair_wait(paired, dx, f"pair_{key}_wait")
                pending[key], token = chip_start(pair_sum(thru, land, f"pair_sum_{key}"), dg, f"chips_{key}")
            small[norm_name][l] = dg[0]
            return dx, token

        dx, token = ffn_backward(dx, blocks[0])
        dxb, dzg, da_na, da_sw, do_na, do_sw, dbg = mix_bwd_out(
            dx, sv["gt"], sv["a_na"], sv["a_sw"], wna_t, wsw_t, wout_l, token, f"mix_bwd_out_{l}")
        small["b_gate"][l] = dbg[0]
        gwout, gwna, gwsw = tn_matmul([(sv["merged"], dxb, 1.0), (da_na, sv["o_na"], 1.0), (da_sw, sv["o_sw"], 1.0)],
                                      f"mix_dw_{l}")
        dqa, dka, dva, dt2 = na_bwd(sv["qa"], sv["ka"], sv["zq"], sv["t2"], sv["o_na"], do_na, f"na_bwd_{l}")
        dqs, dks, dvs, dbias, dsink = sw_bwd(sv["qs"], sv["ks"], sv["zq"], t5b, sw_sink[l], sv["o_sw"], do_sw,
                                             f"sw_bwd_{l}")
        dbias_sw.append(dbias.reshape(SW_HEADS, SW_BLOCK, 3 * SW_BLOCK))
        small["sw_sink"][l] = jnp.sum(dsink[:, 0].reshape(SW_HEADS, SW_BLOCK), axis=1)
        small["na_rpb"][l] = _rpb_from_rows(rpb_reduce(dt2, f"rpb_reduce_{l}"))
        dz, dgqa, dgka, dgqs, dgks = qk_norm_bwd(dqa, dka, dva, dqs, dks, dvs, sv["zq"], dzg, *sv["gains"], bd,
                                                 f"qk_norm_bwd_{l}")
        fold = lambda g: jnp.sum(g.reshape(-1, HEAD_DIM), axis=0)
        small["na_q_norm"][l], small["na_k_norm"][l] = fold(dgqa), fold(dgka)
        small["sw_q_norm"][l], small["sw_k_norm"][l] = fold(dgqs), fold(dgks)
        (gwin,) = tn_matmul([(dz, sv["hn"], 1.0)], f"dwin_{l}")
        pending[f"mix_{l}"], token = scatter_start([[split(gwout)], [split(gwna), split(gwsw)], [split(gwin)]],
                                                   f"scatter_mix_{l}")
        dx, dg = proj_bwd_norm([dz], [win_t], sv["x1"], mix_norm[l][None], dx, token, f"mix_bwd_x_{l}")
        small["mix_norm"][l] = dg[0]
        dx, tail = ffn_backward(dx, blocks[1])

    dtab = t5_reduce(dbias_sw, bmap, "t5_reduce")
    small_parts = {k: jnp.stack(v) for k, v in small.items()}
    small_parts["t5_rel_table"] = jnp.transpose(dtab[:, :, 0])

    grads, delta, new_m, new_v = {}, {}, {}, {}
    state = {}
    chain = [tail]
    members = {"ffn": lambda t: [(f"ffn{t}_w_gate", 0, 0, True), (f"ffn{t}_w_up", 0, 1, True),
                                 (f"ffn{t}_w_down", 0, 2, False)],
               "mix": lambda t: [("w_out", 0, 0, False), ("w_branch_na", 1, 0, True), ("w_branch_sw", 1, 1, True),
                                 ("w_in", 2, 0, True)]}

    def collect(key):
        if key in two_level:
            zones = [chip_wait(pending[key], chain[0], f"wait_{key}")]
        else:
            zones = scatter_wait(pending[key], chain[0], f"wait_{key}")
        kind, l = key.split("_")
        group = members[kind[:3]](kind[3:])
        complete = all(f"{kind}_{j}" in done for j in range(depth) if j != int(l))
        for zi, zone in enumerate(zones):
            mine = sorted((wi, k, transposed) for k, z, wi, transposed in group if z == zi)
            views = [tr if transposed else (lambda t: t) for _, _, transposed in mine]
            items = [(view(weights[k]), view(mom_m[k]), view(mom_v[k]), state.get(k))
                     for (_, k, _), view in zip(mine, views)]
            results = adamw_layer(zone, int(l), items, chain[0], f"adamw_{key}_{zi}")
            chain[0] = results[-1][1]
            for (_, k, _), view, res in zip(mine, views, results):
                state[k] = res
                if complete:
                    grads[k], delta[k], new_m[k], new_v[k] = (view(t) for t in res)
        done.add(key)

    done = set()
    for key in pending:
        if key != last_key:
            collect(key)
    collect(last_key)
    *recvs, all_losses = share_small([small_parts[k] for k in SMALL_NAMES] + [loss_acc], chain[0])
    loss = jnp.sum(all_losses) * (0.5 / d)
    results = adamw_small([weights[k] for k in SMALL_NAMES], recvs, [mom_m[k] for k in SMALL_NAMES],
                          [mom_v[k] for k in SMALL_NAMES], "adamw_small")
    for dst, outs in zip((grads, delta, new_m, new_v), results):
        dst.update(dict(zip(SMALL_NAMES, outs)))

    return (loss, dx[None], *[grads[k] for k in order], *[delta[k] for k in order],
            *[new_m[k] for k in order], *[new_v[k] for k in order])
```

```python
import functools
import math

import numpy as np
import jax
import jax.numpy as jnp
from jax import lax
from jax.experimental import pallas as pl
from jax.experimental.pallas import tpu as pltpu

F32 = jnp.float32
BF16 = jnp.bfloat16
MESH = pl.DeviceIdType.MESH

N_DEV = 8
EPS = 1e-6
NEG = -1e30
HEAD_DIM = 64
GRID_W = 64
NA_ROWS = 8
NA_COLS = 16
NA_WIDTH = 512
SW_Q_WIDTH = 512
SW_KV_WIDTH = 128
SW_BLOCK = 128
SW_HEADS = 8
SW_REP = 4
REL_BUCKETS = 32
REL_MAX_DIST = 128
QKV_WIDTH = 3 * NA_WIDTH + SW_Q_WIDTH + 2 * SW_KV_WIDTH
SCALE = 1.0 / math.sqrt(HEAD_DIM)

ADAM_LR = 0.001
ADAM_B1 = 0.9
ADAM_B2 = 0.999
ADAM_EPS = 1e-08
ADAM_WD = 0.01
ADAM_STEP = 10

V7X_VMEM_LIMIT = 56 * 1024 * 1024
LANES = 128
MXU_TILE = 256

NT = (((1,), (1,)), ((), ()))
TN = (((0,), (0,)), ((), ()))


def _params(n_grid=1):
    return pltpu.CompilerParams(dimension_semantics=("arbitrary",) * n_grid,
                                vmem_limit_bytes=V7X_VMEM_LIMIT)


def _row_tile(s):
    for t in (512, 256, 128, 64, 32, 16, 8):
        if s % t == 0:
            return t
    raise ValueError(s)


def _tn_tile(n):
    best = max(t for t in range(LANES, min(n, 2304) + 1, LANES) if n % t == 0) if n % LANES == 0 else n
    return best // 2 if best == n and n >= 1024 else best


ONCE = pl.Buffered(1)


def _col_chunk(n):
    return MXU_TILE if n % MXU_TILE == 0 else n


def _dot(a, b):
    return jnp.dot(a, b, preferred_element_type=F32)


def _dotg(a, b, dn):
    return lax.dot_general(a, b, dn, preferred_element_type=F32)


def _sigmoid(v):
    return 1.0 / (1.0 + jnp.exp(-v))


def _rstd(xv):
    return lax.rsqrt(jnp.mean(xv * xv, axis=-1, keepdims=True) + EPS)


def _full(shape):
    nd = len(shape)
    return pl.BlockSpec(shape, lambda i, _n=nd: (0,) * _n)


def _rows(tm, width):
    return pl.BlockSpec((tm, width), lambda i: (i, 0))


def _mat(stack, idx):
    return pl.BlockSpec((None,) + tuple(stack.shape[1:]), lambda i, _w=idx: (_w, 0, 0), pipeline_mode=ONCE)


def _group_mean(v, bd):
    w = bd.shape[0]
    if v.shape[1] > w:
        return jnp.concatenate([_group_mean(v[:, c0:c0 + w], bd) for c0 in range(0, v.shape[1], w)], axis=1)
    hi = v.astype(BF16)
    lo = (v - hi.astype(F32)).astype(BF16)
    return _dot(hi, bd) + _dot(lo, bd)


def ffn_forward(x, gain, wg_t, wu_t, wd, dep, name):
    s, d = x.shape
    f = wg_t[0].shape[1]
    tm = _row_tile(s) if wd is None else min(_row_tile(s), 256)
    fc = _col_chunk(f)
    nw = 2 if wd is None else 3

    def body(x_ref, g_ref, *refs):
        w_refs, outs = refs[:nw], refs[nw + 1:]
        xn_ref, dg_ref, du_ref, act_ref = outs[-4:]
        xv = x_ref[...]
        xn = (xv * _rstd(xv) * g_ref[...]).astype(BF16)
        xn_ref[...] = xn
        for c0 in range(0, f, fc):
            hg = _dotg(xn, w_refs[0][c0:c0 + fc, :], NT)
            hu = _dotg(xn, w_refs[1][c0:c0 + fc, :], NT)
            sg = _sigmoid(hg)
            silu = hg * sg
            du_ref[:, c0:c0 + fc] = silu.astype(BF16)
            dg_ref[:, c0:c0 + fc] = (hu * (sg + silu * (1.0 - sg))).astype(BF16)
            act_ref[:, c0:c0 + fc] = (silu * hu).astype(BF16)
        if wd is not None:
            outs[0][...] = xv + 0.5 * _dot(act_ref[...], w_refs[2][...])

    weights = [wg_t, wu_t] + ([] if wd is None else [wd])
    out_specs = [_rows(tm, d), _rows(tm, f), _rows(tm, f), _rows(tm, f)]
    out_shape = [jax.ShapeDtypeStruct((s, d), BF16)] + [jax.ShapeDtypeStruct((s, f), BF16)] * 3
    if wd is not None:
        out_specs, out_shape = [_rows(tm, d)] + out_specs, [jax.ShapeDtypeStruct((s, d), F32)] + out_shape
    return pl.pallas_call(
        body, name=name, grid=(s // tm,),
        in_specs=[_rows(tm, d), _full((1, d))] + [_mat(*w) for w in weights] + [_full(dep.shape)],
        out_specs=out_specs, out_shape=out_shape,
        compiler_params=_params(),
    )(x, gain, *[w[0] for w in weights], dep)


def ffn_down(x, act, wd, dep, name, target=None):
    s, d = x.shape
    f = act.shape[1]
    tm = _row_tile(s)

    def body(x_ref, a_ref, w_ref, dep_ref, *rest):
        y = x_ref[...] + 0.5 * _dot(a_ref[...], w_ref[...])
        if target is None:
            rest[0][...] = y
            return
        t_ref, dy_ref, acc_ref = rest

        @pl.when(pl.program_id(0) == 0)
        def _():
            acc_ref[...] = jnp.zeros(acc_ref.shape, F32)

        err = y - t_ref[...]
        dy_ref[...] = err * (1.0 / d)
        part = jnp.sum((err * err).reshape(tm // 8, 8, d), axis=0)
        acc = part[:, 0:LANES]
        for c0 in range(LANES, d, LANES):
            acc = acc + part[:, c0:c0 + LANES]
        acc_ref[...] = acc_ref[...] + acc

    ins = [_rows(tm, d), _rows(tm, f), _mat(*wd), _full(dep.shape)]
    if target is None:
        return pl.pallas_call(
            body, name=name, grid=(s // tm,), in_specs=ins, out_specs=_rows(tm, d),
            out_shape=jax.ShapeDtypeStruct((s, d), F32), compiler_params=_params(),
        )(x, act, wd[0], dep)
    return pl.pallas_call(
        body, name=name, grid=(s // tm,), in_specs=ins + [_rows(tm, d)],
        out_specs=[_rows(tm, d), _full((8, LANES))],
        out_shape=[jax.ShapeDtypeStruct((s, d), F32), jax.ShapeDtypeStruct((8, LANES), F32)],
        compiler_params=_params(),
    )(x, act, wd[0], dep, target)


def mix_in(x, gain, win_t, b_gate, gq_na, gk_na, gq_sw, gk_sw, bd, name):
    s, d = x.shape
    tm = _row_tile(s)
    gc = _col_chunk(2 * d)

    def body(x_ref, g_ref, w_ref, b_ref, gqa_ref, gka_ref, gqs_ref, gks_ref, bd_ref,
             hn_ref, zq_ref, qa_ref, ka_ref, qs_ref, ks_ref, gt_ref):
        xv = x_ref[...]
        hn = (xv * _rstd(xv) * g_ref[...]).astype(BF16)
        hn_ref[...] = hn

        def proj(c0, c1):
            return _dotg(hn, w_ref[c0:c1, :], NT)

        def headnorm(z, g, bdm):
            return z * lax.rsqrt(_group_mean(z * z, bdm) + EPS) * g

        bd512 = bd_ref[...]
        bd128 = bd_ref[0:SW_KV_WIDTH, 0:SW_KV_WIDTH]
        z = proj(0, 512)
        zq_ref[:, 0:512] = z.astype(BF16)
        qa_ref[...] = (headnorm(z, gqa_ref[...], bd512) * SCALE).astype(BF16)
        z = proj(512, 1024)
        zq_ref[:, 512:1024] = z.astype(BF16)
        ka_ref[...] = headnorm(z, gka_ref[...], bd512).astype(BF16)
        z = proj(1024, 1536)
        zq_ref[:, 1024:1536] = z.astype(BF16)
        z = proj(1536, 2048)
        zq_ref[:, 1536:2048] = z.astype(BF16)
        qs_ref[...] = (headnorm(z, gqs_ref[...], bd512) * SCALE).astype(BF16)
        z = proj(2048, 2176)
        zq_ref[:, 2048:2176] = z.astype(BF16)
        ks_ref[...] = headnorm(z, gks_ref[...], bd128).astype(BF16)
        z = proj(2176, 2304)
        zq_ref[:, 2176:2304] = z.astype(BF16)
        for c0 in range(0, 2 * d, gc):
            zg = proj(QKV_WIDTH + c0, QKV_WIDTH + c0 + gc) + b_ref[:, c0:c0 + gc]
            gt_ref[:, c0:c0 + gc] = _sigmoid(zg).astype(BF16)

    return pl.pallas_call(
        body, name=name, grid=(s // tm,),
        in_specs=[_rows(tm, d), _full((1, d)), _mat(*win_t), _full((1, 2 * d)),
                  _full((1, 512)), _full((1, 512)), _full((1, 512)), _full((1, 128)), _full((MXU_TILE, MXU_TILE))],
        out_specs=[_rows(tm, d), _rows(tm, QKV_WIDTH), _rows(tm, 512), _rows(tm, 512), _rows(tm, 512),
                   _rows(tm, 128), _rows(tm, 2 * d)],
        out_shape=[jax.ShapeDtypeStruct((s, d), BF16), jax.ShapeDtypeStruct((s, QKV_WIDTH), BF16),
                   jax.ShapeDtypeStruct((s, 512), BF16), jax.ShapeDtypeStruct((s, 512), BF16),
                   jax.ShapeDtypeStruct((s, 512), BF16), jax.ShapeDtypeStruct((s, 128), BF16),
                   jax.ShapeDtypeStruct((s, 2 * d), BF16)],
        compiler_params=_params(),
    )(x, gain, win_t[0], b_gate, gq_na, gk_na, gq_sw, gk_sw, bd)


def _na_iotas():
    qc = lax.broadcasted_iota(jnp.int32, (GRID_W, LANES), 0)
    ln = lax.broadcasted_iota(jnp.int32, (GRID_W, LANES), 1)
    low = ln < GRID_W
    kc = jnp.where(low, ln, ln - GRID_W)
    diff = kc - qc + (NA_COLS - 1)
    qcs = jnp.clip(qc - NA_COLS // 2, 0, GRID_W - NA_COLS)
    inwin = (kc >= qcs) & (kc < qcs + NA_COLS)
    return diff, low, inwin


NA_RI = 2 * NA_ROWS - 1
NA_CI = 2 * NA_COLS - 1
NA_T2 = NA_RI + 1


def _rpb_rows(rpb):
    h = rpb.shape[0]
    padded = jnp.pad(rpb, ((0, 0), (1, 1), (0, GRID_W - NA_CI)))
    return jnp.concatenate([padded[:, :NA_T2], padded[:, 1:NA_T2 + 1]], axis=2).reshape(h, NA_T2, LANES)


def _rpb_from_rows(rows):
    return rows[:, 1:, :NA_CI] + rows[:, :NA_RI, GRID_W:GRID_W + NA_CI]


def rpb_expand(rows, dep, name):
    n_heads = rows.shape[0]

    def body(r_ref, dep_ref, o_ref):
        for h in range(n_heads):
            for e in range(NA_T2):
                line = jnp.broadcast_to(r_ref[h, e:e + 1, :], (GRID_W, LANES))
                o_ref[h, e] = pltpu.roll(line, LANES - (NA_COLS - 1), 1, stride=1, stride_axis=0)

    return pl.pallas_call(
        body, name=name,
        in_specs=[pl.BlockSpec(memory_space=pltpu.VMEM), pl.BlockSpec(memory_space=pltpu.VMEM)],
        out_specs=pl.BlockSpec(memory_space=pltpu.VMEM),
        out_shape=jax.ShapeDtypeStruct((n_heads, NA_T2, GRID_W, LANES), F32),
        compiler_params=pltpu.CompilerParams(vmem_limit_bytes=V7X_VMEM_LIMIT),
    )(rows, dep)


def rpb_reduce(dt2, name):
    n_heads = dt2.shape[0]
    flip = jnp.asarray(np.eye(GRID_W)[::-1], BF16)

    def body(d_ref, j_ref, o_ref):
        jm = j_ref[...]
        for h in range(n_heads):
            for e in range(NA_T2):
                dv = d_ref[h, e]
                hi = dv.astype(BF16)
                mid = (dv - hi.astype(F32)).astype(BF16)
                lo = (dv - hi.astype(F32) - mid.astype(F32)).astype(BF16)
                rev = _dot(jm, hi) + _dot(jm, mid) + _dot(jm, lo)
                back = pltpu.roll(rev, LANES + (NA_COLS - 1) - (GRID_W - 1), 1, stride=1, stride_axis=0)
                o_ref[h, e:e + 1, :] = jnp.sum(back, axis=0, keepdims=True)

    return pl.pallas_call(
        body, name=name,
        in_specs=[pl.BlockSpec(memory_space=pltpu.VMEM)] * 2,
        out_specs=pl.BlockSpec(memory_space=pltpu.VMEM),
        out_shape=jax.ShapeDtypeStruct((n_heads, NA_T2, LANES), F32),
        compiler_params=pltpu.CompilerParams(vmem_limit_bytes=V7X_VMEM_LIMIT),
    )(dt2, flip)


NA_TQ = 4
NA_TK = NA_TQ + NA_ROWS
NA_KCH = NA_TK // 2


def _na_tile_geometry(t, rows):
    r = t * NA_TQ
    kbase = jnp.clip(r - NA_ROWS // 2, 0, rows - NA_TK)
    starts = [jnp.clip(r + a - NA_ROWS // 2, 0, rows - NA_ROWS) for a in range(NA_TQ)]
    return r, kbase, starts


def _na_tile_mask(kbase, starts, low, inwin):
    half = jnp.where(low, 0, 1)
    cols = []
    for c in range(NA_KCH):
        krow = kbase + 2 * c + half
        cols.append(jnp.concatenate(
            [jnp.where(inwin & (krow >= st) & (krow < st + NA_ROWS), 0.0, NEG) for st in starts], axis=0))
    return jnp.concatenate(cols, axis=1)


def _na_tile_index(r, kbase, a, c):
    return jnp.clip(kbase + 2 * c - (r + a) + NA_ROWS, 0, NA_T2 - 1)


def _na_tile_scores(q, k, t2_ref, hh, r, kbase, madd):
    bias = jnp.concatenate(
        [jnp.concatenate([t2_ref[hh, _na_tile_index(r, kbase, a, c)] for a in range(NA_TQ)], axis=0)
         for c in range(NA_KCH)], axis=1)
    return _dotg(q, k, NT) + bias + madd


def _softmax_rows(sc):
    e = jnp.exp(sc - jnp.max(sc, axis=1, keepdims=True))
    return e * (1.0 / jnp.sum(e, axis=1, keepdims=True))


def na_fwd(qa, ka, zq, t2, name):
    s = qa.shape[0]
    rows = s // GRID_W
    n_pairs = NA_WIDTH // LANES
    v_blk0 = (2 * NA_WIDTH) // LANES

    assert rows % NA_TQ == 0 and rows >= NA_TK
    tq, tk = NA_TQ * GRID_W, NA_TK * GRID_W

    def body(q_ref, k_ref, v_ref, t2_ref, o_ref, s_scr, p_scr):
        _, low, inwin = _na_iotas()

        def tile(t, carry):
            r, kbase, starts = _na_tile_geometry(t, rows)
            madd = _na_tile_mask(kbase, starts, low, inwin)
            qr = pl.ds(pl.multiple_of(r * GRID_W, tq), tq)
            kr = pl.ds(pl.multiple_of(kbase * GRID_W, tq), tk)
            for hh in range(2):
                lanes = slice(HEAD_DIM * hh, HEAD_DIM * (hh + 1))
                s_scr[tq * hh:tq * (hh + 1), :] = _na_tile_scores(q_ref[qr, lanes], k_ref[kr, lanes], t2_ref, hh, r,
                                                                  kbase, madd)
            p_scr[...] = _softmax_rows(s_scr[...]).astype(BF16)
            for hh in range(2):
                lanes = slice(HEAD_DIM * hh, HEAD_DIM * (hh + 1))
                o_ref[qr, lanes] = _dot(p_scr[tq * hh:tq * (hh + 1), :], v_ref[kr, lanes]).astype(BF16)
            return carry

        lax.fori_loop(0, rows // NA_TQ, tile, 0)

    col = lambda off: pl.BlockSpec((s, LANES), lambda p, _o=off: (0, _o + p))
    return pl.pallas_call(
        body, name=name, grid=(n_pairs,),
        in_specs=[col(0), col(0), col(v_blk0),
                  pl.BlockSpec((2, NA_T2, GRID_W, LANES), lambda p: (p, 0, 0, 0))],
        out_specs=col(0),
        out_shape=jax.ShapeDtypeStruct((s, NA_WIDTH), BF16),
        scratch_shapes=[pltpu.VMEM((2 * tq, tk), F32), pltpu.VMEM((2 * tq, tk), BF16)],
        compiler_params=_params(),
    )(qa, ka, zq, t2)


def na_bwd(qa, ka, zq, t2, o_na, do_na, name):
    s = qa.shape[0]
    rows = s // GRID_W
    n_pairs = NA_WIDTH // LANES
    v_blk0 = (2 * NA_WIDTH) // LANES

    tq, tk = NA_TQ * GRID_W, NA_TK * GRID_W

    def body(q_ref, k_ref, v_ref, t2_ref, o_ref, do_ref, dq_ref, dk_ref, dv_ref, dt2_ref):
        _, low, inwin = _na_iotas()
        dk_ref[...] = jnp.zeros(dk_ref.shape, F32)
        dv_ref[...] = jnp.zeros(dv_ref.shape, F32)
        dt2_ref[...] = jnp.zeros(dt2_ref.shape, F32)

        def tile(t, carry):
            r, kbase, starts = _na_tile_geometry(t, rows)
            madd = _na_tile_mask(kbase, starts, low, inwin)
            qr = pl.ds(pl.multiple_of(r * GRID_W, tq), tq)
            kr = pl.ds(pl.multiple_of(kbase * GRID_W, tq), tk)
            for hh in range(2):
                lanes = slice(HEAD_DIM * hh, HEAD_DIM * (hh + 1))
                q, k, v = q_ref[qr, lanes], k_ref[kr, lanes], v_ref[kr, lanes]
                p = _softmax_rows(_na_tile_scores(q, k, t2_ref, hh, r, kbase, madd))
                do = do_ref[qr, lanes]
                delta = jnp.sum(do.astype(F32) * o_ref[qr, lanes].astype(F32), axis=1, keepdims=True)
                ds = p * (_dotg(do, v, NT) - delta)
                shared = {}
                for a in range(NA_TQ):
                    for c in range(NA_KCH):
                        shared.setdefault(2 * c - a, []).append(
                            ds[GRID_W * a:GRID_W * (a + 1), LANES * c:LANES * (c + 1)])
                for offset, parts in shared.items():
                    e = jnp.clip(offset + kbase - r + NA_ROWS, 0, NA_T2 - 1)
                    dt2_ref[hh, e] = dt2_ref[hh, e] + functools.reduce(jnp.add, parts)
                dsb = ds.astype(BF16)
                dq_ref[qr, lanes] = _dot(dsb, k)
                dk_ref[kr, lanes] = dk_ref[kr, lanes] + _dotg(dsb, q, TN)
                dv_ref[kr, lanes] = dv_ref[kr, lanes] + _dotg(p.astype(BF16), do, TN)
            return carry

        lax.fori_loop(0, rows // NA_TQ, tile, 0)

    col = lambda off: pl.BlockSpec((s, LANES), lambda p, _o=off: (0, _o + p))
    t2spec = pl.BlockSpec((2, NA_T2, GRID_W, LANES), lambda p: (p, 0, 0, 0))
    return pl.pallas_call(
        body, name=name, grid=(n_pairs,),
        in_specs=[col(0), col(0), col(v_blk0), t2spec, col(0), col(0)],
        out_specs=[col(0), col(0), col(0), t2spec],
        out_shape=[jax.ShapeDtypeStruct((s, NA_WIDTH), F32)] * 3 + [jax.ShapeDtypeStruct(t2.shape, F32)],
        compiler_params=_params(),
    )(qa, ka, zq, t2, o_na, do_na)


def _t5_bucket_map():
    rel = np.arange(3 * SW_BLOCK)[None, :] - SW_BLOCK - np.arange(SW_BLOCK)[:, None]
    nb = REL_BUCKETS // 2
    max_exact = nb // 2
    n = np.abs(rel)
    large = max_exact + (np.log(np.maximum(n, 1) / max_exact)
                         / np.log(REL_MAX_DIST / max_exact) * (nb - max_exact)).astype(np.int32)
    large = np.minimum(large, nb - 1)
    return ((rel > 0) * nb + np.where(n < max_exact, n, large)).astype(np.int32)


def t5_expand(table, bmap, dep, name):
    def body(tab_ref, bm_ref, dep_ref, o_ref):
        bm = bm_ref[...]
        for h in range(SW_HEADS):
            t = jnp.zeros(bm.shape, F32)
            for b in range(REL_BUCKETS):
                t = jnp.where(bm == b, tab_ref[b, h], t)
            o_ref[h] = t

    return pl.pallas_call(
        body, name=name,
        in_specs=[pl.BlockSpec(memory_space=pltpu.SMEM), pl.BlockSpec(memory_space=pltpu.VMEM),
                  pl.BlockSpec(memory_space=pltpu.VMEM)],
        out_specs=pl.BlockSpec(memory_space=pltpu.VMEM),
        out_shape=jax.ShapeDtypeStruct((SW_HEADS,) + bmap.shape, F32),
        compiler_params=pltpu.CompilerParams(vmem_limit_bytes=V7X_VMEM_LIMIT),
    )(table, bmap, dep)


def t5_reduce(dbias_list, bmap, name):
    n = len(dbias_list)

    def body(*refs):
        d_refs, bm_ref, o_ref = refs[:n], refs[n], refs[n + 1]
        bm = bm_ref[...]
        for h in range(SW_HEADS):
            dv = d_refs[0][h]
            for other in d_refs[1:]:
                dv = dv + other[h]
            rows = [jnp.sum(jnp.where(bm == b, dv, 0.0), axis=0, keepdims=True) for b in range(REL_BUCKETS)]
            r = jnp.concatenate(rows, axis=0)
            o_ref[h] = jnp.broadcast_to(jnp.sum(r, axis=1, keepdims=True), (REL_BUCKETS, LANES))

    return pl.pallas_call(
        body, name=name,
        in_specs=[pl.BlockSpec(memory_space=pltpu.VMEM)] * (n + 1),
        out_specs=pl.BlockSpec(memory_space=pltpu.VMEM),
        out_shape=jax.ShapeDtypeStruct((SW_HEADS, REL_BUCKETS, LANES), F32),
        compiler_params=pltpu.CompilerParams(vmem_limit_bytes=V7X_VMEM_LIMIT),
    )(*dbias_list, bmap)


def _sw_mask_iotas():
    a = lax.broadcasted_iota(jnp.int32, (SW_BLOCK, 3 * SW_BLOCK), 0)
    j = lax.broadcasted_iota(jnp.int32, (SW_BLOCK, 3 * SW_BLOCK), 1)
    inwin = jnp.abs(j - SW_BLOCK - a) <= SW_BLOCK
    return j, inwin


SW_STACK = SW_HEADS * SW_BLOCK


def _sw_softmax(sc, sk):
    m = jnp.maximum(jnp.max(sc, axis=1, keepdims=True), sk)
    e = jnp.exp(sc - m)
    es = jnp.exp(sk - m)
    inv = 1.0 / (jnp.sum(e, axis=1, keepdims=True) + es)
    return e * inv, es * inv


def _sw_prologue(k_ref, v_ref, kp, vp, sink_ref, s):
    pad = s + 2 * SW_BLOCK
    zeros = jnp.zeros((SW_BLOCK, SW_KV_WIDTH), BF16)
    kp[0:SW_BLOCK, :] = zeros
    vp[0:SW_BLOCK, :] = zeros
    kp[SW_BLOCK + s:pad, :] = zeros
    vp[SW_BLOCK + s:pad, :] = zeros
    kp[SW_BLOCK:SW_BLOCK + s, :] = k_ref[...]
    vp[SW_BLOCK:SW_BLOCK + s, :] = v_ref[...]
    return jnp.concatenate([jnp.full((SW_BLOCK, 1), sink_ref[h], F32) for h in range(SW_HEADS)], axis=0)


def sw_fwd(qs, ks, zq, t5b, sink, dep, name):
    s = qs.shape[0]
    nb = s // SW_BLOCK
    v_blk = (3 * NA_WIDTH + SW_Q_WIDTH + SW_KV_WIDTH) // LANES
    pad = s + 2 * SW_BLOCK

    def body(q_ref, k_ref, v_ref, b_ref, sink_ref, dep_ref, o_ref, kp, vp, s_scr, p_scr):
        sink_col = _sw_prologue(k_ref, v_ref, kp, vp, sink_ref, s)
        j, inwin = _sw_mask_iotas()

        def blk(n, carry):
            kpos = n * SW_BLOCK - SW_BLOCK + j
            madd = jnp.where(inwin & (kpos >= 0) & (kpos < s), 0.0, NEG)
            q0 = pl.multiple_of(n * SW_BLOCK, SW_BLOCK)
            qr, kr = pl.ds(q0, SW_BLOCK), pl.ds(q0, 3 * SW_BLOCK)
            for h in range(SW_HEADS):
                g = h // SW_REP
                s_scr[SW_BLOCK * h:SW_BLOCK * (h + 1), :] = _dotg(
                    q_ref[qr, HEAD_DIM * h:HEAD_DIM * (h + 1)], kp[kr, HEAD_DIM * g:HEAD_DIM * (g + 1)], NT) + madd
            p, _ = _sw_softmax(s_scr[...] + b_ref[...], sink_col)
            p_scr[...] = p.astype(BF16)
            for h in range(SW_HEADS):
                g = h // SW_REP
                o_ref[qr, HEAD_DIM * h:HEAD_DIM * (h + 1)] = _dot(
                    p_scr[SW_BLOCK * h:SW_BLOCK * (h + 1), :], vp[kr, HEAD_DIM * g:HEAD_DIM * (g + 1)]).astype(BF16)
            return carry

        lax.fori_loop(0, nb, blk, 0)

    return pl.pallas_call(
        body, name=name, grid=(1,),
        in_specs=[_full((s, SW_Q_WIDTH)), _full((s, SW_KV_WIDTH)),
                  pl.BlockSpec((s, SW_KV_WIDTH), lambda i: (0, v_blk)),
                  _full((SW_STACK, 3 * SW_BLOCK)), pl.BlockSpec(memory_space=pltpu.SMEM),
                  _full(dep.shape)],
        out_specs=_full((s, SW_Q_WIDTH)),
        out_shape=jax.ShapeDtypeStruct((s, SW_Q_WIDTH), BF16),
        scratch_shapes=[pltpu.VMEM((pad, SW_KV_WIDTH), BF16), pltpu.VMEM((pad, SW_KV_WIDTH), BF16),
                        pltpu.VMEM((SW_STACK, 3 * SW_BLOCK), F32), pltpu.VMEM((SW_STACK, 3 * SW_BLOCK), BF16)],
        compiler_params=_params(),
    )(qs, ks, zq, t5b, sink, dep)


def sw_bwd(qs, ks, zq, t5b, sink, o_sw, do_sw, name):
    s = qs.shape[0]
    nb = s // SW_BLOCK
    v_blk = (3 * NA_WIDTH + SW_Q_WIDTH + SW_KV_WIDTH) // LANES
    pad = s + 2 * SW_BLOCK

    def body(q_ref, k_ref, v_ref, b_ref, sink_ref, o_ref, do_ref,
             dq_ref, dk_ref, dv_ref, db_ref, dsk_ref, kp, vp, dkp, dvp, s_scr, dp_scr, ds_scr, p_scr):
        sink_col = _sw_prologue(k_ref, v_ref, kp, vp, sink_ref, s)
        dkp[...] = jnp.zeros(dkp.shape, F32)
        dvp[...] = jnp.zeros(dvp.shape, F32)
        db_ref[...] = jnp.zeros(db_ref.shape, F32)
        dsk_ref[...] = jnp.zeros(dsk_ref.shape, F32)
        j, inwin = _sw_mask_iotas()

        def blk(n, carry):
            kpos = n * SW_BLOCK - SW_BLOCK + j
            madd = jnp.where(inwin & (kpos >= 0) & (kpos < s), 0.0, NEG)
            q0 = pl.multiple_of(n * SW_BLOCK, SW_BLOCK)
            qr, kr = pl.ds(q0, SW_BLOCK), pl.ds(q0, 3 * SW_BLOCK)
            deltas = []
            for h in range(SW_HEADS):
                g = h // SW_REP
                hl, kl = slice(HEAD_DIM * h, HEAD_DIM * (h + 1)), slice(HEAD_DIM * g, HEAD_DIM * (g + 1))
                rows = slice(SW_BLOCK * h, SW_BLOCK * (h + 1))
                do = do_ref[qr, hl]
                s_scr[rows, :] = _dotg(q_ref[qr, hl], kp[kr, kl], NT) + madd
                dp_scr[rows, :] = _dotg(do, vp[kr, kl], NT)
                deltas.append(jnp.sum(do.astype(F32) * o_ref[qr, hl].astype(F32), axis=1, keepdims=True))
            delta = jnp.concatenate(deltas, axis=0)
            p, ps = _sw_softmax(s_scr[...] + b_ref[...], sink_col)
            ds = p * (dp_scr[...] - delta)
            db_ref[...] = db_ref[...] + ds
            dsk_ref[...] = dsk_ref[...] - jnp.broadcast_to(ps * delta, (SW_STACK, LANES))
            ds_scr[...] = ds.astype(BF16)
            p_scr[...] = p.astype(BF16)
            for g in range(SW_HEADS // SW_REP):
                kl = slice(HEAD_DIM * g, HEAD_DIM * (g + 1))
                k = kp[kr, kl]
                dkw = jnp.zeros((3 * SW_BLOCK, HEAD_DIM), F32)
                dvw = jnp.zeros((3 * SW_BLOCK, HEAD_DIM), F32)
                for r in range(SW_REP):
                    h = g * SW_REP + r
                    hl, rows = slice(HEAD_DIM * h, HEAD_DIM * (h + 1)), slice(SW_BLOCK * h, SW_BLOCK * (h + 1))
                    dsb = ds_scr[rows, :]
                    dq_ref[qr, hl] = _dot(dsb, k)
                    dkw = dkw + _dotg(dsb, q_ref[qr, hl], TN)
                    dvw = dvw + _dotg(p_scr[rows, :], do_ref[qr, hl], TN)
                dkp[kr, kl] = dkp[kr, kl] + dkw
                dvp[kr, kl] = dvp[kr, kl] + dvw
            return carry

        lax.fori_loop(0, nb, blk, 0)
        dk_ref[...] = dkp[SW_BLOCK:SW_BLOCK + s, :]
        dv_ref[...] = dvp[SW_BLOCK:SW_BLOCK + s, :]

    bias_spec = _full((SW_STACK, 3 * SW_BLOCK))
    return pl.pallas_call(
        body, name=name, grid=(1,),
        in_specs=[_full((s, SW_Q_WIDTH)), _full((s, SW_KV_WIDTH)),
                  pl.BlockSpec((s, SW_KV_WIDTH), lambda i: (0, v_blk)),
                  bias_spec, pl.BlockSpec(memory_space=pltpu.SMEM),
                  _full((s, SW_Q_WIDTH)), _full((s, SW_Q_WIDTH))],
        out_specs=[_full((s, SW_Q_WIDTH)), _full((s, SW_KV_WIDTH)), _full((s, SW_KV_WIDTH)), bias_spec,
                   _full((SW_STACK, LANES))],
        out_shape=[jax.ShapeDtypeStruct((s, SW_Q_WIDTH), F32), jax.ShapeDtypeStruct((s, SW_KV_WIDTH), F32),
                   jax.ShapeDtypeStruct((s, SW_KV_WIDTH), F32),
                   jax.ShapeDtypeStruct((SW_STACK, 3 * SW_BLOCK), F32),
                   jax.ShapeDtypeStruct((SW_STACK, LANES), F32)],
        scratch_shapes=[pltpu.VMEM((pad, SW_KV_WIDTH), BF16), pltpu.VMEM((pad, SW_KV_WIDTH), BF16),
                        pltpu.VMEM((pad, SW_KV_WIDTH), F32), pltpu.VMEM((pad, SW_KV_WIDTH), F32),
                        pltpu.VMEM((SW_STACK, 3 * SW_BLOCK), F32), pltpu.VMEM((SW_STACK, 3 * SW_BLOCK), F32),
                        pltpu.VMEM((SW_STACK, 3 * SW_BLOCK), BF16), pltpu.VMEM((SW_STACK, 3 * SW_BLOCK), BF16)],
        compiler_params=_params(),
    )(qs, ks, zq, t5b, sink, o_sw, do_sw)


def merge_out(x, o_na, o_sw, gt, wbna_t, wbsw_t, wout, name):
    s, d = x.shape
    tm = _row_tile(s)

    def body(x_ref, ona_ref, osw_ref, gt_ref, wna_ref, wsw_ref, wo_ref, xo_ref, ana_ref, asw_ref, mg_ref):
        a_na = _dotg(ona_ref[...], wna_ref[...], NT)
        a_sw = _dotg(osw_ref[...], wsw_ref[...], NT)
        g_na, g_sw = gt_ref[:, 0:d].astype(F32), gt_ref[:, d:2 * d].astype(F32)
        ana_ref[...] = (a_na * g_na * (1.0 - g_na)).astype(BF16)
        asw_ref[...] = (a_sw * g_sw * (1.0 - g_sw)).astype(BF16)
        merged = (g_na * a_na + g_sw * a_sw).astype(BF16)
        mg_ref[...] = merged
        xo_ref[...] = x_ref[...] + _dot(merged, wo_ref[...])

    return pl.pallas_call(
        body, name=name, grid=(s // tm,),
        in_specs=[_rows(tm, d), _rows(tm, 512), _rows(tm, 512), _rows(tm, 2 * d),
                  _mat(*wbna_t), _mat(*wbsw_t), _mat(*wout)],
        out_specs=[_rows(tm, d)] * 4,
        out_shape=[jax.ShapeDtypeStruct((s, d), F32)] + [jax.ShapeDtypeStruct((s, d), BF16)] * 3,
        compiler_params=_params(),
    )(x, o_na, o_sw, gt, wbna_t[0], wbsw_t[0], wout[0])


def mix_bwd_out(dx, gt, a_na, a_sw, wbna_t, wbsw_t, wout, dep, name):
    s, d = dx.shape
    tm = _row_tile(s)

    def body(dx_ref, gt_ref, ana_ref, asw_ref, wna_ref, wsw_ref, wo_ref, dep_ref,
             dxb_ref, dzg_ref, dana_ref, dasw_ref, dona_ref, dosw_ref, dbg_ref):
        @pl.when(pl.program_id(0) == 0)
        def _():
            dbg_ref[...] = jnp.zeros(dbg_ref.shape, F32)

        dxb = dx_ref[...].astype(BF16)
        dxb_ref[...] = dxb
        dm = _dotg(dxb, wo_ref[...], NT)
        for i, (a_ref, da_ref, w_ref, do_ref) in enumerate(
                [(ana_ref, dana_ref, wna_ref, dona_ref), (asw_ref, dasw_ref, wsw_ref, dosw_ref)]):
            gi = gt_ref[:, i * d:(i + 1) * d].astype(F32)
            da = (dm * gi).astype(BF16)
            da_ref[...] = da
            do_ref[...] = _dot(da, w_ref[...]).astype(BF16)
            dzg = dm * a_ref[...].astype(F32)
            dzg_ref[:, i * d:(i + 1) * d] = dzg.astype(BF16)
            dbg_ref[:, i * d:(i + 1) * d] = dbg_ref[:, i * d:(i + 1) * d] + jnp.sum(dzg, axis=0, keepdims=True)

    return pl.pallas_call(
        body, name=name, grid=(s // tm,),
        in_specs=[_rows(tm, d), _rows(tm, 2 * d), _rows(tm, d), _rows(tm, d),
                  _mat(*wbna_t), _mat(*wbsw_t), _mat(*wout), _full(dep.shape)],
        out_specs=[_rows(tm, d), _rows(tm, 2 * d), _rows(tm, d), _rows(tm, d), _rows(tm, 512), _rows(tm, 512),
                   _full((1, 2 * d))],
        out_shape=[jax.ShapeDtypeStruct((s, d), BF16), jax.ShapeDtypeStruct((s, 2 * d), BF16),
                   jax.ShapeDtypeStruct((s, d), BF16), jax.ShapeDtypeStruct((s, d), BF16),
                   jax.ShapeDtypeStruct((s, 512), BF16), jax.ShapeDtypeStruct((s, 512), BF16),
                   jax.ShapeDtypeStruct((1, 2 * d), F32)],
        compiler_params=_params(),
    )(dx, gt, a_na, a_sw, wbna_t[0], wbsw_t[0], wout[0], dep)


def qk_norm_bwd(dqa, dka, dva, dqs, dks, dvs, zq, dzg, gq_na, gk_na, gq_sw, gk_sw, bd, name):
    s = zq.shape[0]
    d2 = dzg.shape[1]
    n_in = QKV_WIDTH + d2
    tm = _row_tile(s)

    def body(dqa_ref, dka_ref, dva_ref, dqs_ref, dks_ref, dvs_ref, zq_ref, dzg_ref,
             gqa_ref, gka_ref, gqs_ref, gks_ref, bd_ref, dz_ref, dgqa_ref, dgka_ref, dgqs_ref, dgks_ref):
        @pl.when(pl.program_id(0) == 0)
        def _():
            for r in (dgqa_ref, dgka_ref, dgqs_ref, dgks_ref):
                r[...] = jnp.zeros(r.shape, F32)

        bd512 = bd_ref[...]
        bd128 = bd_ref[0:SW_KV_WIDTH, 0:SW_KV_WIDTH]

        def one(c0, c1, dy_ref, g_ref, dg_ref, bdm, scale):
            z = zq_ref[:, c0:c1].astype(F32)
            r = lax.rsqrt(_group_mean(z * z, bdm) + EPS)
            zh = z * r
            dy = dy_ref[...] * scale
            dyg = dy * g_ref[...]
            dz = r * (dyg - zh * _group_mean(dyg * zh, bdm))
            dz_ref[:, c0:c1] = dz.astype(BF16)
            dg_ref[...] = dg_ref[...] + jnp.sum(dy * zh, axis=0, keepdims=True)

        one(0, 512, dqa_ref, gqa_ref, dgqa_ref, bd512, SCALE)
        one(512, 1024, dka_ref, gka_ref, dgka_ref, bd512, 1.0)
        dz_ref[:, 1024:1536] = dva_ref[...].astype(BF16)
        one(1536, 2048, dqs_ref, gqs_ref, dgqs_ref, bd512, SCALE)
        one(2048, 2176, dks_ref, gks_ref, dgks_ref, bd128, 1.0)
        dz_ref[:, 2176:2304] = dvs_ref[...].astype(BF16)
        dz_ref[:, QKV_WIDTH:n_in] = dzg_ref[...]

    return pl.pallas_call(
        body, name=name, grid=(s // tm,),
        in_specs=[_rows(tm, 512), _rows(tm, 512), _rows(tm, 512), _rows(tm, 512), _rows(tm, 128), _rows(tm, 128),
                  _rows(tm, QKV_WIDTH), _rows(tm, d2),
                  _full((1, 512)), _full((1, 512)), _full((1, 512)), _full((1, 128)), _full((MXU_TILE, MXU_TILE))],
        out_specs=[_rows(tm, n_in), _full((1, 512)), _full((1, 512)), _full((1, 512)), _full((1, 128))],
        out_shape=[jax.ShapeDtypeStruct((s, n_in), BF16)] + [jax.ShapeDtypeStruct((1, 512), F32)] * 3
                  + [jax.ShapeDtypeStruct((1, 128), F32)],
        compiler_params=_params(),
    )(dqa, dka, dva, dqs, dks, dvs, zq, dzg, gq_na, gk_na, gq_sw, gk_sw, bd)


def ffn_bwd_act(dx, wd, hg, hu, name):
    s, d = dx.shape
    f = wd[0].shape[1]
    tm = _row_tile(s)
    fc = _col_chunk(f)

    def body(dx_ref, w_ref, hg_ref, hu_ref, dxb_ref, dhg_ref, dhu_ref):
        dxv = dx_ref[...]
        dxb_ref[...] = dxv.astype(BF16)
        half = (0.5 * dxv).astype(BF16)
        for c0 in range(0, f, fc):
            dact = _dotg(half, w_ref[c0:c0 + fc, :], NT)
            dhu_ref[:, c0:c0 + fc] = (dact * hu_ref[:, c0:c0 + fc].astype(F32)).astype(BF16)
            dhg_ref[:, c0:c0 + fc] = (dact * hg_ref[:, c0:c0 + fc].astype(F32)).astype(BF16)

    return pl.pallas_call(
        body, name=name, grid=(s // tm,),
        in_specs=[_rows(tm, d), _mat(*wd), _rows(tm, f), _rows(tm, f)],
        out_specs=[_rows(tm, d), _rows(tm, f), _rows(tm, f)],
        out_shape=[jax.ShapeDtypeStruct((s, d), BF16), jax.ShapeDtypeStruct((s, f), BF16),
                   jax.ShapeDtypeStruct((s, f), BF16)],
        compiler_params=_params(),
    )(dx, wd[0], hg, hu)


def proj_bwd_norm(acts, weights, x, gain, dx, dep, name):
    s, d = x.shape
    tm = min(_row_tile(s), 256)
    n = len(acts)

    def body(*refs):
        a_refs, w_refs = refs[:n], refs[n:2 * n]
        x_ref, g_ref, dx_ref, _, o_ref, dg_ref = refs[2 * n:]

        @pl.when(pl.program_id(0) == 0)
        def _():
            dg_ref[...] = jnp.zeros(dg_ref.shape, F32)

        dxn = _dot(a_refs[0][...], w_refs[0][...])
        for a_ref, w_ref in zip(a_refs[1:], w_refs[1:]):
            dxn = dxn + _dot(a_ref[...], w_ref[...])
        xv = x_ref[...]
        r = _rstd(xv)
        xh = xv * r
        dxh = dxn * g_ref[...]
        o_ref[...] = dx_ref[...] + r * (dxh - xh * jnp.mean(dxh * xh, axis=-1, keepdims=True))
        dg_ref[...] = dg_ref[...] + jnp.sum(dxn * xh, axis=0, keepdims=True)

    return pl.pallas_call(
        body, name=name, grid=(s // tm,),
        in_specs=[_rows(tm, a.shape[1]) for a in acts] + [_mat(*w) for w in weights]
                 + [_rows(tm, d), _full((1, d)), _rows(tm, d), _full(dep.shape)],
        out_specs=[_rows(tm, d), _full((1, d))],
        out_shape=[jax.ShapeDtypeStruct((s, d), F32), jax.ShapeDtypeStruct((1, d), F32)],
        compiler_params=_params(),
    )(*acts, *[w[0] for w in weights], x, gain, dx, dep)


def tn_matmul(products, name):
    s, n = products[0][0].shape
    tn = _tn_tile(n) if len(products) == 1 else _col_chunk(n)
    rhs = []
    for _, b, _ in products:
        if not any(b is seen for seen in rhs):
            rhs.append(b)
    which = [next(i for i, seen in enumerate(rhs) if b is seen) for _, b, _ in products]
    npr, nr = len(products), len(rhs)

    def body(*refs):
        a_refs, b_refs, o_refs = refs[:npr], refs[npr:npr + nr], refs[npr + nr:]
        for i, (_, _, scale) in enumerate(products):
            o_refs[i][...] = (scale * _dotg(a_refs[i][...], b_refs[which[i]][...], TN)).astype(BF16)

    return pl.pallas_call(
        body, name=name, grid=(n // tn,),
        in_specs=[pl.BlockSpec((s, tn), lambda i: (0, i))] * npr
                 + [pl.BlockSpec(b.shape, lambda i: (0, 0), pipeline_mode=ONCE) for b in rhs],
        out_specs=[pl.BlockSpec((tn, b.shape[1]), lambda i: (i, 0)) for _, b, _ in products],
        out_shape=[jax.ShapeDtypeStruct((n, b.shape[1]), BF16) for _, b, _ in products],
        compiler_params=_params(),
    )(*[a for a, _, _ in products], *rhs)


def _mesh_pos():
    return lax.axis_index("x"), lax.axis_index("y"), lax.axis_index("c")


def _peers():
    x, y, c = _mesh_pos()
    peers = []
    for rel in range(1, N_DEV):
        peers.append((1 - x if rel & 4 else x, 1 - y if rel & 2 else y, 1 - c if rel & 1 else c))
    return 4 * x + 2 * y + c, peers


HBM_SPEC = pl.BlockSpec(memory_space=pltpu.HBM)
SEM_SPEC = pl.BlockSpec(memory_space=pltpu.SEMAPHORE)


def _split_call(body, name, thru, n_sems, extra=(), with_token=True):
    hbm = lambda t: pltpu.with_memory_space_constraint(t, pltpu.HBM)
    effect = pltpu.CompilerParams(has_side_effects=pltpu.SideEffectType.DATAFLOW_SIDE_EFFECTING)
    nt = len(thru)
    thru_shapes = [pltpu.HBM(t.shape, t.dtype) for t in thru]
    if with_token:
        (after,) = extra
        outs = pl.pallas_call(
            body, name=name, in_specs=[HBM_SPEC] * nt + [pl.BlockSpec(memory_space=pl.ANY)],
            out_specs=[SEM_SPEC] * len(n_sems) + [HBM_SPEC] * nt + [pl.BlockSpec(memory_space=pltpu.VMEM)],
            out_shape=[pltpu.SemaphoreType.DMA((k,)) for k in n_sems] + thru_shapes
                      + [jax.ShapeDtypeStruct((8, LANES), F32)],
            input_output_aliases={i: len(n_sems) + i for i in range(nt)}, compiler_params=effect,
        )(*[hbm(t) for t in thru], after)
        return outs[:len(n_sems)], outs[len(n_sems):-1], outs[-1]
    return pl.pallas_call(
        body, name=name,
        in_specs=[HBM_SPEC] * nt + [SEM_SPEC] * len(n_sems) + [pl.BlockSpec(memory_space=pl.ANY)],
        out_specs=[HBM_SPEC] * nt, out_shape=thru_shapes,
        input_output_aliases={i: i for i in range(nt)}, compiler_params=effect,
    )(*thru, *extra)


def _gather_targets():
    x, y, c = _mesh_pos()
    return 4 * x + 2 * y + c, [(x, y, 1 - c), (1 - x, y, c), (x, 1 - y, c), (1 - x, 1 - y, c)]


def gather_start(shards, after, name):
    n = len(shards)
    zones = [lax.empty((w.shape[0], N_DEV) + w.shape[1:], w.dtype) for w in shards]

    def body(*refs):
        ins, zs = refs[:n], refs[n:2 * n]
        send_sems, recv_sems, local_sems = refs[2 * n + 1:2 * n + 4]
        token = refs[-1]
        me, targets = _gather_targets()
        for a in range(n):
            pltpu.make_async_copy(ins[a], zs[a].at[:, me], local_sems.at[a]).start()
            for k, to in enumerate(targets):
                pltpu.make_async_remote_copy(
                    src_ref=ins[a], dst_ref=zs[a].at[:, me], send_sem=send_sems.at[4 * a + k],
                    recv_sem=recv_sems.at[4 * a + k], device_id=to, device_id_type=MESH).start()
        token[...] = jnp.zeros(token.shape, F32)

    sems, thru, token = _split_call(body, name, list(shards) + zones, (4 * n, 4 * n, n), extra=(after,))
    return (sems, thru, n), token


def gather_wait(started, after, name):
    sems, thru, n = started

    def body(*refs):
        zs = refs[n:2 * n]
        send_sems, recv_sems, local_sems = refs[2 * n:2 * n + 3]
        _, targets = _gather_targets()
        for a in range(n):
            for k, to in enumerate(targets):
                cp = pltpu.make_async_remote_copy(
                    src_ref=zs[a].at[:, 0], dst_ref=zs[a].at[:, 0], send_sem=send_sems.at[4 * a + k],
                    recv_sem=recv_sems.at[4 * a + k], device_id=to, device_id_type=MESH)
                cp.wait_send()
                cp.wait_recv()
            pltpu.make_async_copy(zs[a].at[:, 0], zs[a].at[:, 0], local_sems.at[a]).wait()

    return _split_call(body, name, thru, (4 * n, 4 * n, n), extra=(*sems, after), with_token=False)[n:]


def forward_start(zones, after, name):
    n = len(zones)

    def body(*refs):
        zs = refs[:n]
        send_sems, recv_sems = refs[n + 1:n + 3]
        token = refs[-1]
        x, y, c = _mesh_pos()
        for a in range(n):
            for j, chip in enumerate([(1 - x, y), (x, 1 - y), (1 - x, 1 - y)]):
                blk = zs[a].at[:, 4 * chip[0] + 2 * chip[1] + c]
                pltpu.make_async_remote_copy(
                    src_ref=blk, dst_ref=blk, send_sem=send_sems.at[3 * a + j], recv_sem=recv_sems.at[3 * a + j],
                    device_id=(x, y, 1 - c), device_id_type=MESH).start()
        token[...] = jnp.zeros(token.shape, F32)

    sems, thru, token = _split_call(body, name, list(zones), (3 * n, 3 * n), extra=(after,))
    return (sems, thru, n), token


def forward_wait(started, after, name):
    sems, thru, n = started

    def body(*refs):
        zs = refs[:n]
        send_sems, recv_sems = refs[n:n + 2]
        x, y, c = _mesh_pos()
        for a in range(n):
            for j in range(3):
                cp = pltpu.make_async_remote_copy(
                    src_ref=zs[a].at[:, 0], dst_ref=zs[a].at[:, 0], send_sem=send_sems.at[3 * a + j],
                    recv_sem=recv_sems.at[3 * a + j], device_id=(x, y, 1 - c), device_id_type=MESH)
                cp.wait_send()
                cp.wait_recv()

    return _split_call(body, name, thru, (3 * n, 3 * n), extra=(*sems, after), with_token=False)


def scatter_start(groups, name):
    n = len(groups)
    flat = [g for grp in groups for g in grp]
    nf = len(flat)
    offs = np.cumsum([0] + [len(grp) for grp in groups])
    lands = [lax.empty((N_DEV, len(grp)) + grp[0].shape[1:], grp[0].dtype) for grp in groups]

    def body(*refs):
        ins, zones = refs[:nf], refs[nf:nf + n]
        send_sems, recv_sems, local_sems = refs[nf + n:nf + n + 3]
        token = refs[-1]
        me, peers = _peers()
        for a in range(n):
            for w in range(len(groups[a])):
                pltpu.make_async_copy(ins[offs[a] + w].at[me], zones[a].at[me, w], local_sems.at[a]).start()
        for k, peer in enumerate(peers):
            p_id = 4 * peer[0] + 2 * peer[1] + peer[2]
            for a in range(n):
                for w in range(len(groups[a])):
                    pltpu.make_async_remote_copy(
                        src_ref=ins[offs[a] + w].at[p_id], dst_ref=zones[a].at[me, w],
                        send_sem=send_sems.at[7 * a + k], recv_sem=recv_sems.at[7 * a + k],
                        device_id=peer, device_id_type=MESH).start()
        token[...] = jnp.zeros(token.shape, F32)

    hbm = lambda t: pltpu.with_memory_space_constraint(t, pltpu.HBM)
    outs = pl.pallas_call(
        body, name=name,
        in_specs=[HBM_SPEC] * (nf + n),
        out_specs=[SEM_SPEC] * 3 + [HBM_SPEC] * (nf + n) + [pl.BlockSpec(memory_space=pltpu.VMEM)],
        out_shape=[pltpu.SemaphoreType.DMA((7 * n,)), pltpu.SemaphoreType.DMA((7 * n,)), pltpu.SemaphoreType.DMA((n,))]
                  + [pltpu.HBM(t.shape, t.dtype) for t in flat + lands]
                  + [jax.ShapeDtypeStruct((8, LANES), F32)],
        input_output_aliases={i: 3 + i for i in range(nf + n)},
        compiler_params=pltpu.CompilerParams(has_side_effects=pltpu.SideEffectType.DATAFLOW_SIDE_EFFECTING),
    )(*[hbm(t) for t in flat], *[hbm(t) for t in lands])
    sems, thru, token = outs[:3], outs[3:3 + nf + n], outs[-1]
    return (sems, thru, [len(grp) for grp in groups]), token


def scatter_wait(started, after, name):
    (send_sems, recv_sems, local_sems), thru, sizes = started
    n = len(sizes)
    nf = len(thru) - n

    def body(*refs):
        zones = refs[nf:nf + n]
        s_sems, r_sems, l_sems = refs[nf + n:nf + n + 3]
        me, peers = _peers()
        for a in range(n):
            for k, peer in enumerate(peers):
                cp = pltpu.make_async_remote_copy(
                    src_ref=zones[a].at[0], dst_ref=zones[a].at[0],
                    send_sem=s_sems.at[7 * a + k], recv_sem=r_sems.at[7 * a + k], device_id=peer,
                    device_id_type=MESH)
                cp.wait_send()
                cp.wait_recv()
            pltpu.make_async_copy(zones[a].at[0], zones[a].at[0], l_sems.at[a]).wait()

    outs = pl.pallas_call(
        body, name=name,
        in_specs=[HBM_SPEC] * (nf + n) + [SEM_SPEC] * 3 + [pl.BlockSpec(memory_space=pl.ANY)],
        out_specs=[HBM_SPEC] * (nf + n),
        out_shape=[pltpu.HBM(t.shape, t.dtype) for t in thru],
        input_output_aliases={i: i for i in range(nf + n)},
        compiler_params=pltpu.CompilerParams(has_side_effects=pltpu.SideEffectType.DATAFLOW_SIDE_EFFECTING),
    )(*thru, send_sems, recv_sems, local_sems, after)
    return outs[nf:]


def pair_start(grads, after, name):
    nw = len(grads)
    land = lax.empty((4, nw) + grads[0].shape[1:], grads[0].dtype)

    def body(*refs):
        ins, zone = refs[:nw], refs[nw]
        send_sems, recv_sems = refs[nw + 2:nw + 4]
        x, y, c = _mesh_pos()
        for j in range(4):
            for w in range(nw):
                pltpu.make_async_remote_copy(
                    src_ref=ins[w].at[2 * j + (1 - c)], dst_ref=zone.at[j, w], send_sem=send_sems.at[0],
                    recv_sem=recv_sems.at[0], device_id=(x, y, 1 - c), device_id_type=MESH).start()
        refs[-1][...] = jnp.zeros(refs[-1].shape, F32)

    sems, thru, token = _split_call(body, name, list(grads) + [land], (1, 1), extra=(after,))
    return (sems, thru, nw), token


def pair_wait(started, after, name):
    sems, thru, nw = started

    def body(*refs):
        zone = refs[nw]
        send_sems, recv_sems = refs[nw + 1:nw + 3]
        x, y, c = _mesh_pos()
        cp = pltpu.make_async_remote_copy(src_ref=zone, dst_ref=zone, send_sem=send_sems.at[0],
                                          recv_sem=recv_sems.at[0], device_id=(x, y, 1 - c), device_id_type=MESH)
        cp.wait_send()
        cp.wait_recv()

    outs = _split_call(body, name, thru, (1, 1), extra=(*sems, after), with_token=False)
    return outs[:nw], outs[nw]


def pair_sum(grads, land, name):
    nw = len(grads)
    _, r, c_dim = grads[0].shape

    def body(*refs):
        g_refs, l_ref, o_ref = refs[:nw], refs[nw], refs[nw + 1]
        core = lax.axis_index("c")
        for w in range(nw):
            o_ref[0, w] = (g_refs[w][0, core].astype(F32) + l_ref[0, w].astype(F32)).astype(BF16)

    return pl.pallas_call(
        body, name=name, grid=(4,),
        in_specs=[pl.BlockSpec((1, 2, r, c_dim), lambda j: (j, 0, 0, 0))] * nw
                 + [pl.BlockSpec((1, nw, r, c_dim), lambda j: (j, 0, 0, 0))],
        out_specs=pl.BlockSpec((1, nw, r, c_dim), lambda j: (j, 0, 0, 0)),
        out_shape=jax.ShapeDtypeStruct((4, nw, r, c_dim), BF16),
        compiler_params=_params(),
    )(*[g.reshape(4, 2, r, c_dim) for g in grads], land)


def _other_chips():
    x, y, c = _mesh_pos()
    chips = []
    for rel in range(1, 4):
        px, py = (1 - x if rel & 2 else x), (1 - y if rel & 1 else y)
        chips.append((px, py, 2 * px + py))
    return 2 * x + y, c, chips


def chip_start(pair_sums, after, name):
    land = lax.empty(pair_sums.shape, pair_sums.dtype)

    def body(*refs):
        h_ref, zone = refs[0], refs[1]
        send_sems, recv_sems, local_sem = refs[3:6]
        mine, c, chips = _other_chips()
        pltpu.make_async_copy(h_ref.at[mine], zone.at[mine], local_sem.at[0]).start()
        for k, (px, py, j) in enumerate(chips):
            pltpu.make_async_remote_copy(
                src_ref=h_ref.at[j], dst_ref=zone.at[mine], send_sem=send_sems.at[k], recv_sem=recv_sems.at[k],
                device_id=(px, py, c), device_id_type=MESH).start()
        refs[-1][...] = jnp.zeros(refs[-1].shape, F32)

    sems, thru, token = _split_call(body, name, [pair_sums, land], (3, 3, 1), extra=(after,))
    return (sems, thru), token


def chip_wait(started, after, name):
    sems, thru = started

    def body(*refs):
        zone = refs[1]
        send_sems, recv_sems, local_sem = refs[2:5]
        _, c, chips = _other_chips()
        for k, (px, py, _) in enumerate(chips):
            cp = pltpu.make_async_remote_copy(
                src_ref=zone.at[0], dst_ref=zone.at[0], send_sem=send_sems.at[k], recv_sem=recv_sems.at[k],
                device_id=(px, py, c), device_id_type=MESH)
            cp.wait_send()
            cp.wait_recv()
        pltpu.make_async_copy(zone.at[0], zone.at[0], local_sem.at[0]).wait()

    return _split_call(body, name, thru, (3, 3, 1), extra=(*sems, after), with_token=False)[1]


def share_small(parts, after):
    n = len(parts)

    def body(*refs):
        ins, outs = refs[:n], refs[n + 1:2 * n + 1]
        send_sems, recv_sems, local_sems = refs[2 * n + 1:]
        me, peers = _peers()
        copies = []
        for i in range(n):
            copies.append(pltpu.make_async_copy(ins[i], outs[i].at[me], local_sems.at[i]))
            copies += [pltpu.make_async_remote_copy(
                src_ref=ins[i], dst_ref=outs[i].at[me], send_sem=send_sems.at[7 * i + k],
                recv_sem=recv_sems.at[7 * i + k], device_id=peer, device_id_type=MESH)
                for k, peer in enumerate(peers)]
        for cp in copies:
            cp.start()
        for cp in copies:
            cp.wait()

    vm = pl.BlockSpec(memory_space=pltpu.VMEM)
    return pl.pallas_call(
        body, name="share_small", in_specs=[vm] * n + [pl.BlockSpec(memory_space=pl.ANY)], out_specs=[vm] * n,
        out_shape=[jax.ShapeDtypeStruct((N_DEV,) + p.shape, p.dtype) for p in parts],
        scratch_shapes=[pltpu.SemaphoreType.DMA((7 * n,)), pltpu.SemaphoreType.DMA((7 * n,)),
                        pltpu.SemaphoreType.DMA((n,))],
    )(*parts, after)


def _adamw_math(w, g, m, v):
    m = ADAM_B1 * m + (1.0 - ADAM_B1) * g
    v = ADAM_B2 * v + (1.0 - ADAM_B2) * (g * g)
    m_hat = m / (1.0 - ADAM_B1 ** ADAM_STEP)
    v_hat = v / (1.0 - ADAM_B2 ** ADAM_STEP)
    delta = -ADAM_LR * (m_hat / (jnp.sqrt(v_hat) + ADAM_EPS) + ADAM_WD * w)
    return delta, m, v


ADAMW_BLOCK_BYTES = 24 * 1024 * 1024


def adamw_layer(zone, layer, items, after, name):
    n_src, nw, r, c = zone.shape
    depth = items[0][0].shape[0]
    prevs = [p if p is not None else tuple(lax.empty((depth, r, c), F32) for _ in range(4)) for _, _, _, p in items]
    row_bytes = 2 * nw * c * (2 * n_src + 4 * 7)
    tr = max(t for t in range(8, r + 1, 8) if r % t == 0 and t * row_bytes <= ADAMW_BLOCK_BYTES)

    def body(z_ref, *rest):
        ins, outs = rest[:3 * nw], rest[7 * nw + 1:]
        for i in range(nw):
            g = z_ref[0, i].astype(F32)
            for src in range(1, n_src):
                g = g + z_ref[src, i].astype(F32)
            g_ref, d_ref, mo_ref, vo_ref = outs[4 * i:4 * i + 4]
            w_ref, m_ref, v_ref = ins[3 * i:3 * i + 3]
            g_ref[...] = g
            d_ref[...], mo_ref[...], vo_ref[...] = _adamw_math(w_ref[...], g, m_ref[...], v_ref[...])

    rows = pl.BlockSpec((None, tr, c), lambda i: (layer, i, 0))
    anywhere = pl.BlockSpec(memory_space=pl.ANY)
    outs = pl.pallas_call(
        body, name=name, grid=(r // tr,),
        in_specs=[pl.BlockSpec((n_src, nw, tr, c), lambda i: (0, 0, i, 0))] + [rows] * (3 * nw)
                 + [anywhere] * (4 * nw + 1),
        out_specs=[rows] * (4 * nw),
        out_shape=[jax.ShapeDtypeStruct((depth, r, c), F32)] * (4 * nw),
        input_output_aliases={1 + 3 * nw + k: k for k in range(4 * nw)},
        compiler_params=_params(),
    )(zone, *[t for w, m, v, _ in items for t in (w, m, v)], *[t for p in prevs for t in p], after)
    return [tuple(outs[4 * i:4 * i + 4]) for i in range(nw)]


def adamw_small(ws, recvs, ms, vs, name):
    n = len(ws)

    def body(*refs):
        w_refs, r_refs, m_refs, v_refs = (refs[i * n:(i + 1) * n] for i in range(4))
        g_refs, d_refs, mo_refs, vo_refs = (refs[(4 + i) * n:(5 + i) * n] for i in range(4))
        for i in range(n):
            g = r_refs[i][0]
            for src in range(1, N_DEV):
                g = g + r_refs[i][src]
            g_refs[i][...] = g
            d_refs[i][...], mo_refs[i][...], vo_refs[i][...] = _adamw_math(w_refs[i][...], g, m_refs[i][...],
                                                                            v_refs[i][...])

    vm = pl.BlockSpec(memory_space=pltpu.VMEM)
    outs = pl.pallas_call(
        body, name=name, in_specs=[vm] * (4 * n), out_specs=[vm] * (4 * n),
        out_shape=[jax.ShapeDtypeStruct(w.shape, F32) for w in ws] * 4,
        compiler_params=pltpu.CompilerParams(vmem_limit_bytes=V7X_VMEM_LIMIT),
    )(*ws, *recvs, *ms, *vs)
    return [outs[i * n:(i + 1) * n] for i in range(4)]


SMALL_NAMES = ("ffn1_norm", "mix_norm", "ffn2_norm", "b_gate", "na_q_norm", "na_k_norm", "sw_q_norm", "sw_k_norm",
               "na_rpb", "sw_sink", "t5_rel_table")


def kernel(x, ffn1_norm, ffn1_w_gate, ffn1_w_up, ffn1_w_down, mix_norm, w_in, b_gate, na_q_norm, na_k_norm, na_rpb, sw_q_norm, sw_k_norm, sw_sink, t5_rel_table, w_branch_na, w_branch_sw, w_out, ffn2_norm, ffn2_w_gate, ffn2_w_up, ffn2_w_down, loss_target, m_ffn1_norm, m_ffn1_w_gate, m_ffn1_w_up, m_ffn1_w_down, m_mix_norm, m_w_in, m_b_gate, m_na_q_norm, m_na_k_norm, m_na_rpb, m_sw_q_norm, m_sw_k_norm, m_sw_sink, m_t5_rel_table, m_w_branch_na, m_w_branch_sw, m_w_out, m_ffn2_norm, m_ffn2_w_gate, m_ffn2_w_up, m_ffn2_w_down, v_ffn1_norm, v_ffn1_w_gate, v_ffn1_w_up, v_ffn1_w_down, v_mix_norm, v_w_in, v_b_gate, v_na_q_norm, v_na_k_norm, v_na_rpb, v_sw_q_norm, v_sw_k_norm, v_sw_sink, v_t5_rel_table, v_w_branch_na, v_w_branch_sw, v_w_out, v_ffn2_norm, v_ffn2_w_gate, v_ffn2_w_up, v_ffn2_w_down):
    weights = dict(ffn1_norm=ffn1_norm, ffn1_w_gate=ffn1_w_gate, ffn1_w_up=ffn1_w_up, ffn1_w_down=ffn1_w_down,
                   mix_norm=mix_norm, w_in=w_in, b_gate=b_gate, na_q_norm=na_q_norm, na_k_norm=na_k_norm,
                   na_rpb=na_rpb, sw_q_norm=sw_q_norm, sw_k_norm=sw_k_norm, sw_sink=sw_sink,
                   t5_rel_table=t5_rel_table, w_branch_na=w_branch_na, w_branch_sw=w_branch_sw, w_out=w_out,
                   ffn2_norm=ffn2_norm, ffn2_w_gate=ffn2_w_gate, ffn2_w_up=ffn2_w_up, ffn2_w_down=ffn2_w_down)
    mom_m = dict(ffn1_norm=m_ffn1_norm, ffn1_w_gate=m_ffn1_w_gate, ffn1_w_up=m_ffn1_w_up, ffn1_w_down=m_ffn1_w_down,
                 mix_norm=m_mix_norm, w_in=m_w_in, b_gate=m_b_gate, na_q_norm=m_na_q_norm, na_k_norm=m_na_k_norm,
                 na_rpb=m_na_rpb, sw_q_norm=m_sw_q_norm, sw_k_norm=m_sw_k_norm, sw_sink=m_sw_sink,
                 t5_rel_table=m_t5_rel_table, w_branch_na=m_w_branch_na, w_branch_sw=m_w_branch_sw, w_out=m_w_out,
                 ffn2_norm=m_ffn2_norm, ffn2_w_gate=m_ffn2_w_gate, ffn2_w_up=m_ffn2_w_up, ffn2_w_down=m_ffn2_w_down)
    mom_v = dict(ffn1_norm=v_ffn1_norm, ffn1_w_gate=v_ffn1_w_gate, ffn1_w_up=v_ffn1_w_up, ffn1_w_down=v_ffn1_w_down,
                 mix_norm=v_mix_norm, w_in=v_w_in, b_gate=v_b_gate, na_q_norm=v_na_q_norm, na_k_norm=v_na_k_norm,
                 na_rpb=v_na_rpb, sw_q_norm=v_sw_q_norm, sw_k_norm=v_sw_k_norm, sw_sink=v_sw_sink,
                 t5_rel_table=v_t5_rel_table, w_branch_na=v_w_branch_na, w_branch_sw=v_w_branch_sw, w_out=v_w_out,
                 ffn2_norm=v_ffn2_norm, ffn2_w_gate=v_ffn2_w_gate, ffn2_w_up=v_ffn2_w_up, ffn2_w_down=v_ffn2_w_down)
    order = list(weights)

    depth = ffn1_norm.shape[0]
    s, d = x.shape[1], x.shape[2]
    xs = x[0]
    tr = lambda w: jnp.swapaxes(w, -1, -2)

    merge = lambda t: t.reshape(t.shape[0], N_DEV * t.shape[2], t.shape[3])
    no_dep = jnp.zeros((8, LANES), F32)

    def shards_of(kind, l):
        stack = lambda *ws: jnp.stack(ws).astype(BF16)
        if kind == "ffn1":
            return [stack(tr(ffn1_w_gate[l]), tr(ffn1_w_up[l]), ffn1_w_down[l])]
        if kind == "win":
            return [stack(tr(w_in[l]))]
        return [stack(tr(ffn2_w_gate[l]), tr(ffn2_w_up[l]), ffn2_w_down[l]), stack(w_out[l]),
                stack(tr(w_branch_na[l]), tr(w_branch_sw[l]))]

    def start(kind, l, after):
        return gather_start(shards_of(kind, l), after, f"gather_{kind}_{l}")

    def arrive(started, kind, l, after):
        zones = gather_wait(started, after, f"gather_{kind}_{l}_wait")
        return forward_start(zones, no_dep, f"forward_{kind}_{l}")

    def finish(fwd, kind, l, after):
        return [merge(z) for z in forward_wait(fwd, after, f"forward_{kind}_{l}_wait")]

    bd = jnp.asarray(np.kron(np.eye(MXU_TILE // HEAD_DIM), np.full((HEAD_DIM, HEAD_DIM), 1.0 / HEAD_DIM)), BF16)
    bmap = jnp.asarray(_t5_bucket_map())
    tile8 = lambda g: jnp.tile(g, NA_WIDTH // HEAD_DIM).reshape(1, NA_WIDTH)
    tile2 = lambda g: jnp.tile(g, SW_KV_WIDTH // HEAD_DIM).reshape(1, SW_KV_WIDTH)

    st_first, tok = start("ffn1", 0, no_dep)
    t5b = t5_expand(t5_rel_table, bmap, tok, "t5_expand").reshape(SW_STACK, 3 * SW_BLOCK)
    t2_tables = [rpb_expand(_rpb_rows(na_rpb[l]), tok, f"rpb_expand_{l}") for l in range(depth)]
    fwd, _ = arrive(st_first, "ffn1", 0, t2_tables[-1])
    st_win, dep = start("win", 0, t5b)
    (first,) = finish(fwd, "ffn1", 0, dep)

    saved = []
    layer_w = {0: dict(wg1=(first, 0), wu1=(first, 1), wd1=(first, 2))}
    cur = xs
    for l in range(depth):
        sv = {}
        lw = layer_w[l]
        sv["x0"] = cur
        cur, sv["xn1"], sv["hg1"], sv["hu1"], sv["act1"] = ffn_forward(
            cur, ffn1_norm[l][None], lw["wg1"], lw["wu1"], lw["wd1"], dep, f"ffn1_{l}")
        sv["x1"] = cur
        fwd, _ = arrive(st_win, "win", l, cur)
        st_rest, tok = start("rest", l, cur)
        (zb,) = finish(fwd, "win", l, tok)
        lw["win"] = (zb, 0)
        sv["gains"] = (tile8(na_q_norm[l]), tile8(na_k_norm[l]), tile8(sw_q_norm[l]), tile2(sw_k_norm[l]))
        sv["hn"], sv["zq"], sv["qa"], sv["ka"], sv["qs"], sv["ks"], sv["gt"] = mix_in(
            cur, mix_norm[l][None], lw["win"], b_gate[l][None], *sv["gains"], bd, f"mix_in_{l}")
        sv["t2"] = t2_tables[l]
        sv["o_na"] = na_fwd(sv["qa"], sv["ka"], sv["zq"], sv["t2"], f"na_fwd_{l}")
        dep = no_dep
        if l + 1 < depth:
            st_ffn1, dep = start("ffn1", l + 1, sv["o_na"])
        sv["o_sw"] = sw_fwd(sv["qs"], sv["ks"], sv["zq"], t5b, sw_sink[l], dep, f"sw_fwd_{l}")
        fwd, tok = arrive(st_rest, "rest", l, sv["o_sw"][0:8, 0:LANES] + sv["o_na"][0:8, 0:LANES])
        za, zc, zd = finish(fwd, "rest", l, tok)
        lw.update(wg2=(za, 0), wu2=(za, 1), wd2=(za, 2), wout=(zc, 0), wna=(zd, 0), wsw=(zd, 1))
        cur, sv["a_na"], sv["a_sw"], sv["merged"] = merge_out(
            cur, sv["o_na"], sv["o_sw"], sv["gt"], lw["wna"], lw["wsw"], lw["wout"], f"merge_out_{l}")
        sv["x2"] = cur
        dep = no_dep
        if l + 1 < depth:
            st_win, dep = start("win", l + 1, cur)
        sv["xn2"], sv["hg2"], sv["hu2"], sv["act2"] = ffn_forward(
            cur, ffn2_norm[l][None], lw["wg2"], lw["wu2"], None, dep, f"ffn2_up_{l}")
        dep = no_dep
        if l + 1 < depth:
            fwd, dep = arrive(st_ffn1, "ffn1", l + 1, sv["act2"])
        if l + 1 < depth:
            cur = ffn_down(cur, sv["act2"], lw["wd2"], dep, f"ffn2_down_{l}")
            (za,) = finish(fwd, "ffn1", l + 1, cur)
            layer_w[l + 1] = dict(wg1=(za, 0), wu1=(za, 1), wd1=(za, 2))
        else:
            dx, loss_acc = ffn_down(cur, sv["act2"], lw["wd2"], dep, f"ffn2_down_{l}", target=loss_target[0])
        dep = no_dep
        saved.append(sv)

    split = lambda t: t.reshape(N_DEV, t.shape[0] // N_DEV, t.shape[1])
    pending = {}
    last_key = "ffn1_0"
    two_level = {last_key}
    small = {k: [None] * depth for k in SMALL_NAMES if k != "t5_rel_table"}
    dbias_sw = []
    for l in reversed(range(depth)):
        sv = saved[l]
        lw = layer_w[l]
        wg1, wu1, wd1, wg2, wu2, wd2 = (lw[k] for k in ("wg1", "wu1", "wd1", "wg2", "wu2", "wd2"))
        win_t, wout_l, wna_t, wsw_t = lw["win"], lw["wout"], lw["wna"], lw["wsw"]
        blocks = ((2, "x2", "xn2", "hg2", "hu2", "act2", wg2, wu2, wd2, "ffn2_norm", 3),
                  (1, "x0", "xn1", "hg1", "hu1", "act1", wg1, wu1, wd1, "ffn1_norm", 0))

        def ffn_backward(dx, blk):
            tag, xk, xnk, hgk, huk, actk, wg, wu, wd, norm_name, slot = blk
            gains = weights[norm_name]
            dxb, dhg, dhu = ffn_bwd_act(dx, wd, sv[hgk], sv[huk], f"ffn{tag}_bwd_act_{l}")
            gwg, gwu, gwd = tn_matmul([(dhg, sv[xnk], 1.0), (dhu, sv[xnk], 1.0), (sv[actk], dxb, 0.5)],
                                      f"ffn{tag}_dw_{l}")
            key = f"ffn{tag}_{l}"
            blocks_of = [split(gwg), split(gwu), split(gwd)]
            if key in two_level:
                paired, token = pair_start(blocks_of, dxb, f"pair_{key}")
            else:
                pending[key], token = scatter_start([blocks_of], f"scatter_{key}")
            dx, dg = proj_bwd_norm([dhg, dhu], [wg, wu], sv[xk], gains[l][None], dx, token, f"ffn{tag}_bwd_x_{l}")
            token = no_dep
            if key in two_level:
                thru, land = pair_wait(paired, dx, f"pair_{key}_wait")
                pending[key], token = chip_start(pair_sum(thru, land, f"pair_sum_{key}"), dg, f"chips_{key}")
            small[norm_name][l] = dg[0]
            return dx, token

        dx, token = ffn_backward(dx, blocks[0])
        dxb, dzg, da_na, da_sw, do_na, do_sw, dbg = mix_bwd_out(
            dx, sv["gt"], sv["a_na"], sv["a_sw"], wna_t, wsw_t, wout_l, token, f"mix_bwd_out_{l}")
        small["b_gate"][l] = dbg[0]
        gwout, gwna, gwsw = tn_matmul([(sv["merged"], dxb, 1.0), (da_na, sv["o_na"], 1.0), (da_sw, sv["o_sw"], 1.0)],
                                      f"mix_dw_{l}")
        dqa, dka, dva, dt2 = na_bwd(sv["qa"], sv["ka"], sv["zq"], sv["t2"], sv["o_na"], do_na, f"na_bwd_{l}")
        dqs, dks, dvs, dbias, dsink = sw_bwd(sv["qs"], sv["ks"], sv["zq"], t5b, sw_sink[l], sv["o_sw"], do_sw,
                                             f"sw_bwd_{l}")
        dbias_sw.append(dbias.reshape(SW_HEADS, SW_BLOCK, 3 * SW_BLOCK))
        small["sw_sink"][l] = jnp.sum(dsink[:, 0].reshape(SW_HEADS, SW_BLOCK), axis=1)
        small["na_rpb"][l] = _rpb_from_rows(rpb_reduce(dt2, f"rpb_reduce_{l}"))
        dz, dgqa, dgka, dgqs, dgks = qk_norm_bwd(dqa, dka, dva, dqs, dks, dvs, sv["zq"], dzg, *sv["gains"], bd,
                                                 f"qk_norm_bwd_{l}")
        fold = lambda g: jnp.sum(g.reshape(-1, HEAD_DIM), axis=0)
        small["na_q_norm"][l], small["na_k_norm"][l] = fold(dgqa), fold(dgka)
        small["sw_q_norm"][l], small["sw_k_norm"][l] = fold(dgqs), fold(dgks)
        (gwin,) = tn_matmul([(dz, sv["hn"], 1.0)], f"dwin_{l}")
        pending[f"mix_{l}"], token = scatter_start([[split(gwout)], [split(gwna), split(gwsw)], [split(gwin)]],
                                                   f"scatter_mix_{l}")
        dx, dg = proj_bwd_norm([dz], [win_t], sv["x1"], mix_norm[l][None], dx, token, f"mix_bwd_x_{l}")
        small["mix_norm"][l] = dg[0]
        dx, tail = ffn_backward(dx, blocks[1])

    dtab = t5_reduce(dbias_sw, bmap, "t5_reduce")
    small_parts = {k: jnp.stack(v) for k, v in small.items()}
    small_parts["t5_rel_table"] = jnp.transpose(dtab[:, :, 0])

    grads, delta, new_m, new_v = {}, {}, {}, {}
    state = {}
    chain = [tail]
    members = {"ffn": lambda t: [(f"ffn{t}_w_gate", 0, 0, True), (f"ffn{t}_w_up", 0, 1, True),
                                 (f"ffn{t}_w_down", 0, 2, False)],
               "mix": lambda t: [("w_out", 0, 0, False), ("w_branch_na", 1, 0, True), ("w_branch_sw", 1, 1, True),
                                 ("w_in", 2, 0, True)]}

    def collect(key):
        if key in two_level:
            zones = [chip_wait(pending[key], chain[0], f"wait_{key}")]
        else:
            zones = scatter_wait(pending[key], chain[0], f"wait_{key}")
        kind, l = key.split("_")
        group = members[kind[:3]](kind[3:])
        complete = all(f"{kind}_{j}" in done for j in range(depth) if j != int(l))
        for zi, zone in enumerate(zones):
            mine = sorted((wi, k, transposed) for k, z, wi, transposed in group if z == zi)
            views = [tr if transposed else (lambda t: t) for _, _, transposed in mine]
            items = [(view(weights[k]), view(mom_m[k]), view(mom_v[k]), state.get(k))
                     for (_, k, _), view in zip(mine, views)]
            results = adamw_layer(zone, int(l), items, chain[0], f"adamw_{key}_{zi}")
            chain[0] = results[-1][1]
            for (_, k, _), view, res in zip(mine, views, results):
                state[k] = res
                if complete:
                    grads[k], delta[k], new_m[k], new_v[k] = (view(t) for t in res)
        done.add(key)

    done = set()
    for key in pending:
        if key != last_key:
            collect(key)
    collect(last_key)
    *recvs, all_losses = share_small([small_parts[k] for k in SMALL_NAMES] + [loss_acc], chain[0])
    loss = jnp.sum(all_losses) * (0.5 / d)
    results = adamw_small([weights[k] for k in SMALL_NAMES], recvs, [mom_m[k] for k in SMALL_NAMES],
                          [mom_v[k] for k in SMALL_NAMES], "adamw_small")
    for dst, outs in zip((grads, delta, new_m, new_v), results):
        dst.update(dict(zip(SMALL_NAMES, outs)))

    return (loss, dx[None], *[grads[k] for k in order], *[delta[k] for k in order],
            *[new_m[k] for k in order], *[new_v[k] for k in order])
```

```python
import functools
import math

import numpy as np
import jax
import jax.numpy as jnp
from jax import lax
from jax.experimental import pallas as pl
from jax.experimental.pallas import tpu as pltpu

F32 = jnp.float32
BF16 = jnp.bfloat16
MESH = pl.DeviceIdType.MESH

N_DEV = 8
EPS = 1e-6
NEG = -1e30
HEAD_DIM = 64
GRID_W = 64
NA_ROWS = 8
NA_COLS = 16
NA_WIDTH = 512
SW_Q_WIDTH = 512
SW_KV_WIDTH = 128
SW_BLOCK = 128
SW_HEADS = 8
SW_REP = 4
REL_BUCKETS = 32
REL_MAX_DIST = 128
QKV_WIDTH = 3 * NA_WIDTH + SW_Q_WIDTH + 2 * SW_KV_WIDTH
SCALE = 1.0 / math.sqrt(HEAD_DIM)

ADAM_LR = 0.001
ADAM_B1 = 0.9
ADAM_B2 = 0.999
ADAM_EPS = 1e-08
ADAM_WD = 0.01
ADAM_STEP = 10

V7X_VMEM_LIMIT = 56 * 1024 * 1024
LANES = 128
MXU_TILE = 256

NT = (((1,), (1,)), ((), ()))
TN = (((0,), (0,)), ((), ()))


def _params(n_grid=1):
    return pltpu.CompilerParams(dimension_semantics=("arbitrary",) * n_grid,
                                vmem_limit_bytes=V7X_VMEM_LIMIT)


def _row_tile(s):
    for t in (512, 256, 128, 64, 32, 16, 8):
        if s % t == 0:
            return t
    raise ValueError(s)


def _tn_tile(n):
    best = max(t for t in range(LANES, min(n, 2304) + 1, LANES) if n % t == 0) if n % LANES == 0 else n
    return best // 2 if best == n and n >= 1024 else best


ONCE = pl.Buffered(1)


def _col_chunk(n):
    return MXU_TILE if n % MXU_TILE == 0 else n


def _dot(a, b):
    return jnp.dot(a, b, preferred_element_type=F32)


def _dotg(a, b, dn):
    return lax.dot_general(a, b, dn, preferred_element_type=F32)


def _sigmoid(v):
    return 1.0 / (1.0 + jnp.exp(-v))


def _rstd(xv):
    return lax.rsqrt(jnp.mean(xv * xv, axis=-1, keepdims=True) + EPS)


def _full(shape):
    nd = len(shape)
    return pl.BlockSpec(shape, lambda i, _n=nd: (0,) * _n)


def _rows(tm, width):
    return pl.BlockSpec((tm, width), lambda i: (i, 0))


def _mat(stack, idx):
    return pl.BlockSpec((None,) + tuple(stack.shape[1:]), lambda i, _w=idx: (_w, 0, 0), pipeline_mode=ONCE)


def _group_mean(v, bd):
    w = bd.shape[0]
    if v.shape[1] > w:
        return jnp.concatenate([_group_mean(v[:, c0:c0 + w], bd) for c0 in range(0, v.shape[1], w)], axis=1)
    hi = v.astype(BF16)
    lo = (v - hi.astype(F32)).astype(BF16)
    return _dot(hi, bd) + _dot(lo, bd)


def ffn_forward(x, gain, wg_t, wu_t, wd, dep, name):
    s, d = x.shape
    f = wg_t[0].shape[1]
    tm = _row_tile(s) if wd is None else min(_row_tile(s), 256)
    fc = _col_chunk(f)
    nw = 2 if wd is None else 3

    def body(x_ref, g_ref, *refs):
        w_refs, outs = refs[:nw], refs[nw + 1:]
        xn_ref, dg_ref, du_ref, act_ref = outs[-4:]
        xv = x_ref[...]
        xn = (xv * _rstd(xv) * g_ref[...]).astype(BF16)
        xn_ref[...] = xn
        for c0 in range(0, f, fc):
            hg = _dotg(xn, w_refs[0][c0:c0 + fc, :], NT)
            hu = _dotg(xn, w_refs[1][c0:c0 + fc, :], NT)
            sg = _sigmoid(hg)
            silu = hg * sg
            du_ref[:, c0:c0 + fc] = silu.astype(BF16)
            dg_ref[:, c0:c0 + fc] = (hu * (sg + silu * (1.0 - sg))).astype(BF16)
            act_ref[:, c0:c0 + fc] = (silu * hu).astype(BF16)
        if wd is not None:
            outs[0][...] = xv + 0.5 * _dot(act_ref[...], w_refs[2][...])

    weights = [wg_t, wu_t] + ([] if wd is None else [wd])
    out_specs = [_rows(tm, d), _rows(tm, f), _rows(tm, f), _rows(tm, f)]
    out_shape = [jax.ShapeDtypeStruct((s, d), BF16)] + [jax.ShapeDtypeStruct((s, f), BF16)] * 3
    if wd is not None:
        out_specs, out_shape = [_rows(tm, d)] + out_specs, [jax.ShapeDtypeStruct((s, d), F32)] + out_shape
    return pl.pallas_call(
        body, name=name, grid=(s // tm,),
        in_specs=[_rows(tm, d), _full((1, d))] + [_mat(*w) for w in weights] + [_full(dep.shape)],
        out_specs=out_specs, out_shape=out_shape,
        compiler_params=_params(),
    )(x, gain, *[w[0] for w in weights], dep)


def ffn_down(x, act, wd, dep, name, target=None):
    s, d = x.shape
    f = act.shape[1]
    tm = _row_tile(s)

    def body(x_ref, a_ref, w_ref, dep_ref, *rest):
        y = x_ref[...] + 0.5 * _dot(a_ref[...], w_ref[...])
        if target is None:
            rest[0][...] = y
            return
        t_ref, dy_ref, acc_ref = rest

        @pl.when(pl.program_id(0) == 0)
        def _():
            acc_ref[...] = jnp.zeros(acc_ref.shape, F32)

        err = y - t_ref[...]
        dy_ref[...] = err * (1.0 / d)
        part = jnp.sum((err * err).reshape(tm // 8, 8, d), axis=0)
        acc = part[:, 0:LANES]
        for c0 in range(LANES, d, LANES):
            acc = acc + part[:, c0:c0 + LANES]
        acc_ref[...] = acc_ref[...] + acc

    ins = [_rows(tm, d), _rows(tm, f), _mat(*wd), _full(dep.shape)]
    if target is None:
        return pl.pallas_call(
            body, name=name, grid=(s // tm,), in_specs=ins, out_specs=_rows(tm, d),
            out_shape=jax.ShapeDtypeStruct((s, d), F32), compiler_params=_params(),
        )(x, act, wd[0], dep)
    return pl.pallas_call(
        body, name=name, grid=(s // tm,), in_specs=ins + [_rows(tm, d)],
        out_specs=[_rows(tm, d), _full((8, LANES))],
        out_shape=[jax.ShapeDtypeStruct((s, d), F32), jax.ShapeDtypeStruct((8, LANES), F32)],
        compiler_params=_params(),
    )(x, act, wd[0], dep, target)


def mix_in(x, gain, win_t, b_gate, gq_na, gk_na, gq_sw, gk_sw, bd, name):
    s, d = x.shape
    tm = _row_tile(s)
    gc = _col_chunk(2 * d)

    def body(x_ref, g_ref, w_ref, b_ref, gqa_ref, gka_ref, gqs_ref, gks_ref, bd_ref,
             hn_ref, zq_ref, qa_ref, ka_ref, qs_ref, ks_ref, gt_ref):
        xv = x_ref[...]
        hn = (xv * _rstd(xv) * g_ref[...]).astype(BF16)
        hn_ref[...] = hn

        def proj(c0, c1):
            return _dotg(hn, w_ref[c0:c1, :], NT)

        def headnorm(z, g, bdm):
            return z * lax.rsqrt(_group_mean(z * z, bdm) + EPS) * g

        bd512 = bd_ref[...]
        bd128 = bd_ref[0:SW_KV_WIDTH, 0:SW_KV_WIDTH]
        z = proj(0, 512)
        zq_ref[:, 0:512] = z.astype(BF16)
        qa_ref[...] = (headnorm(z, gqa_ref[...], bd512) * SCALE).astype(BF16)
        z = proj(512, 1024)
        zq_ref[:, 512:1024] = z.astype(BF16)
        ka_ref[...] = headnorm(z, gka_ref[...], bd512).astype(BF16)
        z = proj(1024, 1536)
        zq_ref[:, 1024:1536] = z.astype(BF16)
        z = proj(1536, 2048)
        zq_ref[:, 1536:2048] = z.astype(BF16)
        qs_ref[...] = (headnorm(z, gqs_ref[...], bd512) * SCALE).astype(BF16)
        z = proj(2048, 2176)
        zq_ref[:, 2048:2176] = z.astype(BF16)
        ks_ref[...] = headnorm(z, gks_ref[...], bd128).astype(BF16)
        z = proj(2176, 2304)
        zq_ref[:, 2176:2304] = z.astype(BF16)
        for c0 in range(0, 2 * d, gc):
            zg = proj(QKV_WIDTH + c0, QKV_WIDTH + c0 + gc) + b_ref[:, c0:c0 + gc]
            gt_ref[:, c0:c0 + gc] = _sigmoid(zg).astype(BF16)

    return pl.pallas_call(
        body, name=name, grid=(s // tm,),
        in_specs=[_rows(tm, d), _full((1, d)), _mat(*win_t), _full((1, 2 * d)),
                  _full((1, 512)), _full((1, 512)), _full((1, 512)), _full((1, 128)), _full((MXU_TILE, MXU_TILE))],
        out_specs=[_rows(tm, d), _rows(tm, QKV_WIDTH), _rows(tm, 512), _rows(tm, 512), _rows(tm, 512),
                   _rows(tm, 128), _rows(tm, 2 * d)],
        out_shape=[jax.ShapeDtypeStruct((s, d), BF16), jax.ShapeDtypeStruct((s, QKV_WIDTH), BF16),
                   jax.ShapeDtypeStruct((s, 512), BF16), jax.ShapeDtypeStruct((s, 512), BF16),
                   jax.ShapeDtypeStruct((s, 512), BF16), jax.ShapeDtypeStruct((s, 128), BF16),
                   jax.ShapeDtypeStruct((s, 2 * d), BF16)],
        compiler_params=_params(),
    )(x, gain, win_t[0], b_gate, gq_na, gk_na, gq_sw, gk_sw, bd)


def _na_iotas():
    qc = lax.broadcasted_iota(jnp.int32, (GRID_W, LANES), 0)
    ln = lax.broadcasted_iota(jnp.int32, (GRID_W, LANES), 1)
    low = ln < GRID_W
    kc = jnp.where(low, ln, ln - GRID_W)
    diff = kc - qc + (NA_COLS - 1)
    qcs = jnp.clip(qc - NA_COLS // 2, 0, GRID_W - NA_COLS)
    inwin = (kc >= qcs) & (kc < qcs + NA_COLS)
    return diff, low, inwin


NA_RI = 2 * NA_ROWS - 1
NA_CI = 2 * NA_COLS - 1
NA_T2 = NA_RI + 1


def _rpb_rows(rpb):
    h = rpb.shape[0]
    padded = jnp.pad(rpb, ((0, 0), (1, 1), (0, GRID_W - NA_CI)))
    return jnp.concatenate([padded[:, :NA_T2], padded[:, 1:NA_T2 + 1]], axis=2).reshape(h, NA_T2, LANES)


def _rpb_from_rows(rows):
    return rows[:, 1:, :NA_CI] + rows[:, :NA_RI, GRID_W:GRID_W + NA_CI]


def rpb_expand(rows, dep, name):
    n_heads = rows.shape[0]

    def body(r_ref, dep_ref, o_ref):
        for h in range(n_heads):
            for e in range(NA_T2):
                line = jnp.broadcast_to(r_ref[h, e:e + 1, :], (GRID_W, LANES))
                o_ref[h, e] = pltpu.roll(line, LANES - (NA_COLS - 1), 1, stride=1, stride_axis=0)

    return pl.pallas_call(
        body, name=name,
        in_specs=[pl.BlockSpec(memory_space=pltpu.VMEM), pl.BlockSpec(memory_space=pltpu.VMEM)],
        out_specs=pl.BlockSpec(memory_space=pltpu.VMEM),
        out_shape=jax.ShapeDtypeStruct((n_heads, NA_T2, GRID_W, LANES), F32),
        compiler_params=pltpu.CompilerParams(vmem_limit_bytes=V7X_VMEM_LIMIT),
    )(rows, dep)


def rpb_reduce(dt2, name):
    n_heads = dt2.shape[0]
    flip = jnp.asarray(np.eye(GRID_W)[::-1], BF16)

    def body(d_ref, j_ref, o_ref):
        jm = j_ref[...]
        for h in range(n_heads):
            for e in range(NA_T2):
                dv = d_ref[h, e]
                hi = dv.astype(BF16)
                mid = (dv - hi.astype(F32)).astype(BF16)
                lo = (dv - hi.astype(F32) - mid.astype(F32)).astype(BF16)
                rev = _dot(jm, hi) + _dot(jm, mid) + _dot(jm, lo)
                back = pltpu.roll(rev, LANES + (NA_COLS - 1) - (GRID_W - 1), 1, stride=1, stride_axis=0)
                o_ref[h, e:e + 1, :] = jnp.sum(back, axis=0, keepdims=True)

    return pl.pallas_call(
        body, name=name,
        in_specs=[pl.BlockSpec(memory_space=pltpu.VMEM)] * 2,
        out_specs=pl.BlockSpec(memory_space=pltpu.VMEM),
        out_shape=jax.ShapeDtypeStruct((n_heads, NA_T2, LANES), F32),
        compiler_params=pltpu.CompilerParams(vmem_limit_bytes=V7X_VMEM_LIMIT),
    )(dt2, flip)


NA_TQ = 4
NA_TK = NA_TQ + NA_ROWS
NA_KCH = NA_TK // 2


def _na_tile_geometry(t, rows):
    r = t * NA_TQ
    kbase = jnp.clip(r - NA_ROWS // 2, 0, rows - NA_TK)
    starts = [jnp.clip(r + a - NA_ROWS // 2, 0, rows - NA_ROWS) for a in range(NA_TQ)]
    return r, kbase, starts


def _na_tile_mask(kbase, starts, low, inwin):
    half = jnp.where(low, 0, 1)
    cols = []
    for c in range(NA_KCH):
        krow = kbase + 2 * c + half
        cols.append(jnp.concatenate(
            [jnp.where(inwin & (krow >= st) & (krow < st + NA_ROWS), 0.0, NEG) for st in starts], axis=0))
    return jnp.concatenate(cols, axis=1)


def _na_tile_index(r, kbase, a, c):
    return jnp.clip(kbase + 2 * c - (r + a) + NA_ROWS, 0, NA_T2 - 1)


def _na_tile_scores(q, k, t2_ref, hh, r, kbase, madd):
    bias = jnp.concatenate(
        [jnp.concatenate([t2_ref[hh, _na_tile_index(r, kbase, a, c)] for a in range(NA_TQ)], axis=0)
         for c in range(NA_KCH)], axis=1)
    return _dotg(q, k, NT) + bias + madd


def _softmax_rows(sc):
    e = jnp.exp(sc - jnp.max(sc, axis=1, keepdims=True))
    return e * (1.0 / jnp.sum(e, axis=1, keepdims=True))


def na_fwd(qa, ka, zq, t2, name):
    s = qa.shape[0]
    rows = s // GRID_W
    n_pairs = NA_WIDTH // LANES
    v_blk0 = (2 * NA_WIDTH) // LANES

    assert rows % NA_TQ == 0 and rows >= NA_TK
    tq, tk = NA_TQ * GRID_W, NA_TK * GRID_W

    def body(q_ref, k_ref, v_ref, t2_ref, o_ref, s_scr, p_scr):
        _, low, inwin = _na_iotas()

        def tile(t, carry):
            r, kbase, starts = _na_tile_geometry(t, rows)
            madd = _na_tile_mask(kbase, starts, low, inwin)
            qr = pl.ds(pl.multiple_of(r * GRID_W, tq), tq)
            kr = pl.ds(pl.multiple_of(kbase * GRID_W, tq), tk)
            for hh in range(2):
                lanes = slice(HEAD_DIM * hh, HEAD_DIM * (hh + 1))
                s_scr[tq * hh:tq * (hh + 1), :] = _na_tile_scores(q_ref[qr, lanes], k_ref[kr, lanes], t2_ref, hh, r,
                                                                  kbase, madd)
            p_scr[...] = _softmax_rows(s_scr[...]).astype(BF16)
            for hh in range(2):
                lanes = slice(HEAD_DIM * hh, HEAD_DIM * (hh + 1))
                o_ref[qr, lanes] = _dot(p_scr[tq * hh:tq * (hh + 1), :], v_ref[kr, lanes]).astype(BF16)
            return carry

        lax.fori_loop(0, rows // NA_TQ, tile, 0)

    col = lambda off: pl.BlockSpec((s, LANES), lambda p, _o=off: (0, _o + p))
    return pl.pallas_call(
        body, name=name, grid=(n_pairs,),
        in_specs=[col(0), col(0), col(v_blk0),
                  pl.BlockSpec((2, NA_T2, GRID_W, LANES), lambda p: (p, 0, 0, 0))],
        out_specs=col(0),
        out_shape=jax.ShapeDtypeStruct((s, NA_WIDTH), BF16),
        scratch_shapes=[pltpu.VMEM((2 * tq, tk), F32), pltpu.VMEM((2 * tq, tk), BF16)],
        compiler_params=_params(),
    )(qa, ka, zq, t2)


def na_bwd(qa, ka, zq, t2, o_na, do_na, name):
    s = qa.shape[0]
    rows = s // GRID_W
    n_pairs = NA_WIDTH // LANES
    v_blk0 = (2 * NA_WIDTH) // LANES

    tq, tk = NA_TQ * GRID_W, NA_TK * GRID_W

    def body(q_ref, k_ref, v_ref, t2_ref, o_ref, do_ref, dq_ref, dk_ref, dv_ref, dt2_ref):
        _, low, inwin = _na_iotas()
        dk_ref[...] = jnp.zeros(dk_ref.shape, F32)
        dv_ref[...] = jnp.zeros(dv_ref.shape, F32)
        dt2_ref[...] = jnp.zeros(dt2_ref.shape, F32)

        def tile(t, carry):
            r, kbase, starts = _na_tile_geometry(t, rows)
            madd = _na_tile_mask(kbase, starts, low, inwin)
            qr = pl.ds(pl.multiple_of(r * GRID_W, tq), tq)
            kr = pl.ds(pl.multiple_of(kbase * GRID_W, tq), tk)
            for hh in range(2):
                lanes = slice(HEAD_DIM * hh, HEAD_DIM * (hh + 1))
                q, k, v = q_ref[qr, lanes], k_ref[kr, lanes], v_ref[kr, lanes]
                p = _softmax_rows(_na_tile_scores(q, k, t2_ref, hh, r, kbase, madd))
                do = do_ref[qr, lanes]
                delta = jnp.sum(do.astype(F32) * o_ref[qr, lanes].astype(F32), axis=1, keepdims=True)
                ds = p * (_dotg(do, v, NT) - delta)
                shared = {}
                for a in range(NA_TQ):
                    for c in range(NA_KCH):
                        shared.setdefault(2 * c - a, []).append(
                            ds[GRID_W * a:GRID_W * (a + 1), LANES * c:LANES * (c + 1)])
                for offset, parts in shared.items():
                    e = jnp.clip(offset + kbase - r + NA_ROWS, 0, NA_T2 - 1)
                    dt2_ref[hh, e] = dt2_ref[hh, e] + functools.reduce(jnp.add, parts)
                dsb = ds.astype(BF16)
                dq_ref[qr, lanes] = _dot(dsb, k)
                dk_ref[kr, lanes] = dk_ref[kr, lanes] + _dotg(dsb, q, TN)
                dv_ref[kr, lanes] = dv_ref[kr, lanes] + _dotg(p.astype(BF16), do, TN)
            return carry

        lax.fori_loop(0, rows // NA_TQ, tile, 0)

    col = lambda off: pl.BlockSpec((s, LANES), lambda p, _o=off: (0, _o + p))
    t2spec = pl.BlockSpec((2, NA_T2, GRID_W, LANES), lambda p: (p, 0, 0, 0))
    return pl.pallas_call(
        body, name=name, grid=(n_pairs,),
        in_specs=[col(0), col(0), col(v_blk0), t2spec, col(0), col(0)],
        out_specs=[col(0), col(0), col(0), t2spec],
        out_shape=[jax.ShapeDtypeStruct((s, NA_WIDTH), F32)] * 3 + [jax.ShapeDtypeStruct(t2.shape, F32)],
        compiler_params=_params(),
    )(qa, ka, zq, t2, o_na, do_na)


def _t5_bucket_map():
    rel = np.arange(3 * SW_BLOCK)[None, :] - SW_BLOCK - np.arange(SW_BLOCK)[:, None]
    nb = REL_BUCKETS // 2
    max_exact = nb // 2
    n = np.abs(rel)
    large = max_exact + (np.log(np.maximum(n, 1) / max_exact)
                         / np.log(REL_MAX_DIST / max_exact) * (nb - max_exact)).astype(np.int32)
    large = np.minimum(large, nb - 1)
    return ((rel > 0) * nb + np.where(n < max_exact, n, large)).astype(np.int32)


def t5_expand(table, bmap, dep, name):
    def body(tab_ref, bm_ref, dep_ref, o_ref):
        bm = bm_ref[...]
        for h in range(SW_HEADS):
            t = jnp.zeros(bm.shape, F32)
            for b in range(REL_BUCKETS):
                t = jnp.where(bm == b, tab_ref[b, h], t)
            o_ref[h] = t

    return pl.pallas_call(
        body, name=name,
        in_specs=[pl.BlockSpec(memory_space=pltpu.SMEM), pl.BlockSpec(memory_space=pltpu.VMEM),
                  pl.BlockSpec(memory_space=pltpu.VMEM)],
        out_specs=pl.BlockSpec(memory_space=pltpu.VMEM),
        out_shape=jax.ShapeDtypeStruct((SW_HEADS,) + bmap.shape, F32),
        compiler_params=pltpu.CompilerParams(vmem_limit_bytes=V7X_VMEM_LIMIT),
    )(table, bmap, dep)


def t5_reduce(dbias_list, bmap, name):
    n = len(dbias_list)

    def body(*refs):
        d_refs, bm_ref, o_ref = refs[:n], refs[n], refs[n + 1]
        bm = bm_ref[...]
        for h in range(SW_HEADS):
            dv = d_refs[0][h]
            for other in d_refs[1:]:
                dv = dv + other[h]
            rows = [jnp.sum(jnp.where(bm == b, dv, 0.0), axis=0, keepdims=True) for b in range(REL_BUCKETS)]
            r = jnp.concatenate(rows, axis=0)
            o_ref[h] = jnp.broadcast_to(jnp.sum(r, axis=1, keepdims=True), (REL_BUCKETS, LANES))

    return pl.pallas_call(
        body, name=name,
        in_specs=[pl.BlockSpec(memory_space=pltpu.VMEM)] * (n + 1),
        out_specs=pl.BlockSpec(memory_space=pltpu.VMEM),
        out_shape=jax.ShapeDtypeStruct((SW_HEADS, REL_BUCKETS, LANES), F32),
        compiler_params=pltpu.CompilerParams(vmem_limit_bytes=V7X_VMEM_LIMIT),
    )(*dbias_list, bmap)


def _sw_mask_iotas():
    a = lax.broadcasted_iota(jnp.int32, (SW_BLOCK, 3 * SW_BLOCK), 0)
    j = lax.broadcasted_iota(jnp.int32, (SW_BLOCK, 3 * SW_BLOCK), 1)
    inwin = jnp.abs(j - SW_BLOCK - a) <= SW_BLOCK
    return j, inwin


SW_STACK = SW_HEADS * SW_BLOCK


def _sw_softmax(sc, sk):
    m = jnp.maximum(jnp.max(sc, axis=1, keepdims=True), sk)
    e = jnp.exp(sc - m)
    es = jnp.exp(sk - m)
    inv = 1.0 / (jnp.sum(e, axis=1, keepdims=True) + es)
    return e * inv, es * inv


def _sw_prologue(k_ref, v_ref, kp, vp, sink_ref, s):
    pad = s + 2 * SW_BLOCK
    zeros = jnp.zeros((SW_BLOCK, SW_KV_WIDTH), BF16)
    kp[0:SW_BLOCK, :] = zeros
    vp[0:SW_BLOCK, :] = zeros
    kp[SW_BLOCK + s:pad, :] = zeros
    vp[SW_BLOCK + s:pad, :] = zeros
    kp[SW_BLOCK:SW_BLOCK + s, :] = k_ref[...]
    vp[SW_BLOCK:SW_BLOCK + s, :] = v_ref[...]
    return jnp.concatenate([jnp.full((SW_BLOCK, 1), sink_ref[h], F32) for h in range(SW_HEADS)], axis=0)


def sw_fwd(qs, ks, zq, t5b, sink, dep, name):
    s = qs.shape[0]
    nb = s // SW_BLOCK
    v_blk = (3 * NA_WIDTH + SW_Q_WIDTH + SW_KV_WIDTH) // LANES
    pad = s + 2 * SW_BLOCK

    def body(q_ref, k_ref, v_ref, b_ref, sink_ref, dep_ref, o_ref, kp, vp, s_scr, p_scr):
        sink_col = _sw_prologue(k_ref, v_ref, kp, vp, sink_ref, s)
        j, inwin = _sw_mask_iotas()

        def blk(n, carry):
            kpos = n * SW_BLOCK - SW_BLOCK + j
            madd = jnp.where(inwin & (kpos >= 0) & (kpos < s), 0.0, NEG)
            q0 = pl.multiple_of(n * SW_BLOCK, SW_BLOCK)
            qr, kr = pl.ds(q0, SW_BLOCK), pl.ds(q0, 3 * SW_BLOCK)
            for h in range(SW_HEADS):
                g = h // SW_REP
                s_scr[SW_BLOCK * h:SW_BLOCK * (h + 1), :] = _dotg(
                    q_ref[qr, HEAD_DIM * h:HEAD_DIM * (h + 1)], kp[kr, HEAD_DIM * g:HEAD_DIM * (g + 1)], NT) + madd
            p, _ = _sw_softmax(s_scr[...] + b_ref[...], sink_col)
            p_scr[...] = p.astype(BF16)
            for h in range(SW_HEADS):
                g = h // SW_REP
                o_ref[qr, HEAD_DIM * h:HEAD_DIM * (h + 1)] = _dot(
                    p_scr[SW_BLOCK * h:SW_BLOCK * (h + 1), :], vp[kr, HEAD_DIM * g:HEAD_DIM * (g + 1)]).astype(BF16)
            return carry

        lax.fori_loop(0, nb, blk, 0)

    return pl.pallas_call(
        body, name=name, grid=(1,),
        in_specs=[_full((s, SW_Q_WIDTH)), _full((s, SW_KV_WIDTH)),
                  pl.BlockSpec((s, SW_KV_WIDTH), lambda i: (0, v_blk)),
                  _full((SW_STACK, 3 * SW_BLOCK)), pl.BlockSpec(memory_space=pltpu.SMEM),
                  _full(dep.shape)],
        out_specs=_full((s, SW_Q_WIDTH)),
        out_shape=jax.ShapeDtypeStruct((s, SW_Q_WIDTH), BF16),
        scratch_shapes=[pltpu.VMEM((pad, SW_KV_WIDTH), BF16), pltpu.VMEM((pad, SW_KV_WIDTH), BF16),
                        pltpu.VMEM((SW_STACK, 3 * SW_BLOCK), F32), pltpu.VMEM((SW_STACK, 3 * SW_BLOCK), BF16)],
        compiler_params=_params(),
    )(qs, ks, zq, t5b, sink, dep)


def sw_bwd(qs, ks, zq, t5b, sink, o_sw, do_sw, name):
    s = qs.shape[0]
    nb = s // SW_BLOCK
    v_blk = (3 * NA_WIDTH + SW_Q_WIDTH + SW_KV_WIDTH) // LANES
    pad = s + 2 * SW_BLOCK

    def body(q_ref, k_ref, v_ref, b_ref, sink_ref, o_ref, do_ref,
             dq_ref, dk_ref, dv_ref, db_ref, dsk_ref, kp, vp, dkp, dvp, s_scr, dp_scr, ds_scr, p_scr):
        sink_col = _sw_prologue(k_ref, v_ref, kp, vp, sink_ref, s)
        dkp[...] = jnp.zeros(dkp.shape, F32)
        dvp[...] = jnp.zeros(dvp.shape, F32)
        db_ref[...] = jnp.zeros(db_ref.shape, F32)
        dsk_ref[...] = jnp.zeros(dsk_ref.shape, F32)
        j, inwin = _sw_mask_iotas()

        def blk(n, carry):
            kpos = n * SW_BLOCK - SW_BLOCK + j
            madd = jnp.where(inwin & (kpos >= 0) & (kpos < s), 0.0, NEG)
            q0 = pl.multiple_of(n * SW_BLOCK, SW_BLOCK)
            qr, kr = pl.ds(q0, SW_BLOCK), pl.ds(q0, 3 * SW_BLOCK)
            deltas = []
            for h in range(SW_HEADS):
                g = h // SW_REP
                hl, kl = slice(HEAD_DIM * h, HEAD_DIM * (h + 1)), slice(HEAD_DIM * g, HEAD_DIM * (g + 1))
                rows = slice(SW_BLOCK * h, SW_BLOCK * (h + 1))
                do = do_ref[qr, hl]
                s_scr[rows, :] = _dotg(q_ref[qr, hl], kp[kr, kl], NT) + madd
                dp_scr[rows, :] = _dotg(do, vp[kr, kl], NT)
                deltas.append(jnp.sum(do.astype(F32) * o_ref[qr, hl].astype(F32), axis=1, keepdims=True))
            delta = jnp.concatenate(deltas, axis=0)
            p, ps = _sw_softmax(s_scr[...] + b_ref[...], sink_col)
            ds = p * (dp_scr[...] - delta)
            db_ref[...] = db_ref[...] + ds
            dsk_ref[...] = dsk_ref[...] - jnp.broadcast_to(ps * delta, (SW_STACK, LANES))
            ds_scr[...] = ds.astype(BF16)
            p_scr[...] = p.astype(BF16)
            for g in range(SW_HEADS // SW_REP):
                kl = slice(HEAD_DIM * g, HEAD_DIM * (g + 1))
                k = kp[kr, kl]
                dkw = jnp.zeros((3 * SW_BLOCK, HEAD_DIM), F32)
                dvw = jnp.zeros((3 * SW_BLOCK, HEAD_DIM), F32)
                for r in range(SW_REP):
                    h = g * SW_REP + r
                    hl, rows = slice(HEAD_DIM * h, HEAD_DIM * (h + 1)), slice(SW_BLOCK * h, SW_BLOCK * (h + 1))
                    dsb = ds_scr[rows, :]
                    dq_ref[qr, hl] = _dot(dsb, k)
                    dkw = dkw + _dotg(dsb, q_ref[qr, hl], TN)
                    dvw = dvw + _dotg(p_scr[rows, :], do_ref[qr, hl], TN)
                dkp[kr, kl] = dkp[kr, kl] + dkw
                dvp[kr, kl] = dvp[kr, kl] + dvw
            return carry

        lax.fori_loop(0, nb, blk, 0)
        dk_ref[...] = dkp[SW_BLOCK:SW_BLOCK + s, :]
        dv_ref[...] = dvp[SW_BLOCK:SW_BLOCK + s, :]

    bias_spec = _full((SW_STACK, 3 * SW_BLOCK))
    return pl.pallas_call(
        body, name=name, grid=(1,),
        in_specs=[_full((s, SW_Q_WIDTH)), _full((s, SW_KV_WIDTH)),
                  pl.BlockSpec((s, SW_KV_WIDTH), lambda i: (0, v_blk)),
                  bias_spec, pl.BlockSpec(memory_space=pltpu.SMEM),
                  _full((s, SW_Q_WIDTH)), _full((s, SW_Q_WIDTH))],
        out_specs=[_full((s, SW_Q_WIDTH)), _full((s, SW_KV_WIDTH)), _full((s, SW_KV_WIDTH)), bias_spec,
                   _full((SW_STACK, LANES))],
        out_shape=[jax.ShapeDtypeStruct((s, SW_Q_WIDTH), F32), jax.ShapeDtypeStruct((s, SW_KV_WIDTH), F32),
                   jax.ShapeDtypeStruct((s, SW_KV_WIDTH), F32),
                   jax.ShapeDtypeStruct((SW_STACK, 3 * SW_BLOCK), F32),
                   jax.ShapeDtypeStruct((SW_STACK, LANES), F32)],
        scratch_shapes=[pltpu.VMEM((pad, SW_KV_WIDTH), BF16), pltpu.VMEM((pad, SW_KV_WIDTH), BF16),
                        pltpu.VMEM((pad, SW_KV_WIDTH), F32), pltpu.VMEM((pad, SW_KV_WIDTH), F32),
                        pltpu.VMEM((SW_STACK, 3 * SW_BLOCK), F32), pltpu.VMEM((SW_STACK, 3 * SW_BLOCK), F32),
                        pltpu.VMEM((SW_STACK, 3 * SW_BLOCK), BF16), pltpu.VMEM((SW_STACK, 3 * SW_BLOCK), BF16)],
        compiler_params=_params(),
    )(qs, ks, zq, t5b, sink, o_sw, do_sw)


def merge_out(x, o_na, o_sw, gt, wbna_t, wbsw_t, wout, name):
    s, d = x.shape
    tm = _row_tile(s)

    def body(x_ref, ona_ref, osw_ref, gt_ref, wna_ref, wsw_ref, wo_ref, xo_ref, ana_ref, asw_ref, mg_ref):
        a_na = _dotg(ona_ref[...], wna_ref[...], NT)
        a_sw = _dotg(osw_ref[...], wsw_ref[...], NT)
        g_na, g_sw = gt_ref[:, 0:d].astype(F32), gt_ref[:, d:2 * d].astype(F32)
        ana_ref[...] = (a_na * g_na * (1.0 - g_na)).astype(BF16)
        asw_ref[...] = (a_sw * g_sw * (1.0 - g_sw)).astype(BF16)
        merged = (g_na * a_na + g_sw * a_sw).astype(BF16)
        mg_ref[...] = merged
        xo_ref[...] = x_ref[...] + _dot(merged, wo_ref[...])

    return pl.pallas_call(
        body, name=name, grid=(s // tm,),
        in_specs=[_rows(tm, d), _rows(tm, 512), _rows(tm, 512), _rows(tm, 2 * d),
                  _mat(*wbna_t), _mat(*wbsw_t), _mat(*wout)],
        out_specs=[_rows(tm, d)] * 4,
        out_shape=[jax.ShapeDtypeStruct((s, d), F32)] + [jax.ShapeDtypeStruct((s, d), BF16)] * 3,
        compiler_params=_params(),
    )(x, o_na, o_sw, gt, wbna_t[0], wbsw_t[0], wout[0])


def mix_bwd_out(dx, gt, a_na, a_sw, wbna_t, wbsw_t, wout, dep, name):
    s, d = dx.shape
    tm = _row_tile(s)

    def body(dx_ref, gt_ref, ana_ref, asw_ref, wna_ref, wsw_ref, wo_ref, dep_ref,
             dxb_ref, dzg_ref, dana_ref, dasw_ref, dona_ref, dosw_ref, dbg_ref):
        @pl.when(pl.program_id(0) == 0)
        def _():
            dbg_ref[...] = jnp.zeros(dbg_ref.shape, F32)

        dxb = dx_ref[...].astype(BF16)
        dxb_ref[...] = dxb
        dm = _dotg(dxb, wo_ref[...], NT)
        for i, (a_ref, da_ref, w_ref, do_ref) in enumerate(
                [(ana_ref, dana_ref, wna_ref, dona_ref), (asw_ref, dasw_ref, wsw_ref, dosw_ref)]):
            gi = gt_ref[:, i * d:(i + 1) * d].astype(F32)
            da = (dm * gi).astype(BF16)
            da_ref[...] = da
            do_ref[...] = _dot(da, w_ref[...]).astype(BF16)
            dzg = dm * a_ref[...].astype(F32)
            dzg_ref[:, i * d:(i + 1) * d] = dzg.astype(BF16)
            dbg_ref[:, i * d:(i + 1) * d] = dbg_ref[:, i * d:(i + 1) * d] + jnp.sum(dzg, axis=0, keepdims=True)

    return pl.pallas_call(
        body, name=name, grid=(s // tm,),
        in_specs=[_rows(tm, d), _rows(tm, 2 * d), _rows(tm, d), _rows(tm, d),
                  _mat(*wbna_t), _mat(*wbsw_t), _mat(*wout), _full(dep.shape)],
        out_specs=[_rows(tm, d), _rows(tm, 2 * d), _rows(tm, d), _rows(tm, d), _rows(tm, 512), _rows(tm, 512),
                   _full((1, 2 * d))],
        out_shape=[jax.ShapeDtypeStruct((s, d), BF16), jax.ShapeDtypeStruct((s, 2 * d), BF16),
                   jax.ShapeDtypeStruct((s, d), BF16), jax.ShapeDtypeStruct((s, d), BF16),
                   jax.ShapeDtypeStruct((s, 512), BF16), jax.ShapeDtypeStruct((s, 512), BF16),
                   jax.ShapeDtypeStruct((1, 2 * d), F32)],
        compiler_params=_params(),
    )(dx, gt, a_na, a_sw, wbna_t[0], wbsw_t[0], wout[0], dep)


def qk_norm_bwd(dqa, dka, dva, dqs, dks, dvs, zq, dzg, gq_na, gk_na, gq_sw, gk_sw, bd, name):
    s = zq.shape[0]
    d2 = dzg.shape[1]
    n_in = QKV_WIDTH + d2
    tm = _row_tile(s)

    def body(dqa_ref, dka_ref, dva_ref, dqs_ref, dks_ref, dvs_ref, zq_ref, dzg_ref,
             gqa_ref, gka_ref, gqs_ref, gks_ref, bd_ref, dz_ref, dgqa_ref, dgka_ref, dgqs_ref, dgks_ref):
        @pl.when(pl.program_id(0) == 0)
        def _():
            for r in (dgqa_ref, dgka_ref, dgqs_ref, dgks_ref):
                r[...] = jnp.zeros(r.shape, F32)

        bd512 = bd_ref[...]
        bd128 = bd_ref[0:SW_KV_WIDTH, 0:SW_KV_WIDTH]

        def one(c0, c1, dy_ref, g_ref, dg_ref, bdm, scale):
            z = zq_ref[:, c0:c1].astype(F32)
            r = lax.rsqrt(_group_mean(z * z, bdm) + EPS)
            zh = z * r
            dy = dy_ref[...] * scale
            dyg = dy * g_ref[...]
            dz = r * (dyg - zh * _group_mean(dyg * zh, bdm))
            dz_ref[:, c0:c1] = dz.astype(BF16)
            dg_ref[...] = dg_ref[...] + jnp.sum(dy * zh, axis=0, keepdims=True)

        one(0, 512, dqa_ref, gqa_ref, dgqa_ref, bd512, SCALE)
        one(512, 1024, dka_ref, gka_ref, dgka_ref, bd512, 1.0)
        dz_ref[:, 1024:1536] = dva_ref[...].astype(BF16)
        one(1536, 2048, dqs_ref, gqs_ref, dgqs_ref, bd512, SCALE)
        one(2048, 2176, dks_ref, gks_ref, dgks_ref, bd128, 1.0)
        dz_ref[:, 2176:2304] = dvs_ref[...].astype(BF16)
        dz_ref[:, QKV_WIDTH:n_in] = dzg_ref[...]

    return pl.pallas_call(
        body, name=name, grid=(s // tm,),
        in_specs=[_rows(tm, 512), _rows(tm, 512), _rows(tm, 512), _rows(tm, 512), _rows(tm, 128), _rows(tm, 128),
                  _rows(tm, QKV_WIDTH), _rows(tm, d2),
                  _full((1, 512)), _full((1, 512)), _full((1, 512)), _full((1, 128)), _full((MXU_TILE, MXU_TILE))],
        out_specs=[_rows(tm, n_in), _full((1, 512)), _full((1, 512)), _full((1, 512)), _full((1, 128))],
        out_shape=[jax.ShapeDtypeStruct((s, n_in), BF16)] + [jax.ShapeDtypeStruct((1, 512), F32)] * 3
                  + [jax.ShapeDtypeStruct((1, 128), F32)],
        compiler_params=_params(),
    )(dqa, dka, dva, dqs, dks, dvs, zq, dzg, gq_na, gk_na, gq_sw, gk_sw, bd)


def ffn_bwd_act(dx, wd, hg, hu, name):
    s, d = dx.shape
    f = wd[0].shape[1]
    tm = _row_tile(s)
    fc = _col_chunk(f)

    def body(dx_ref, w_ref, hg_ref, hu_ref, dxb_ref, dhg_ref, dhu_ref):
        dxv = dx_ref[...]
        dxb_ref[...] = dxv.astype(BF16)
        half = (0.5 * dxv).astype(BF16)
        for c0 in range(0, f, fc):
            dact = _dotg(half, w_ref[c0:c0 + fc, :], NT)
            dhu_ref[:, c0:c0 + fc] = (dact * hu_ref[:, c0:c0 + fc].astype(F32)).astype(BF16)
            dhg_ref[:, c0:c0 + fc] = (dact * hg_ref[:, c0:c0 + fc].astype(F32)).astype(BF16)

    return pl.pallas_call(
        body, name=name, grid=(s // tm,),
        in_specs=[_rows(tm, d), _mat(*wd), _rows(tm, f), _rows(tm, f)],
        out_specs=[_rows(tm, d), _rows(tm, f), _rows(tm, f)],
        out_shape=[jax.ShapeDtypeStruct((s, d), BF16), jax.ShapeDtypeStruct((s, f), BF16),
                   jax.ShapeDtypeStruct((s, f), BF16)],
        compiler_params=_params(),
    )(dx, wd[0], hg, hu)


def proj_bwd_norm(acts, weights, x, gain, dx, dep, name):
    s, d = x.shape
    tm = min(_row_tile(s), 256)
    n = len(acts)

    def body(*refs):
        a_refs, w_refs = refs[:n], refs[n:2 * n]
        x_ref, g_ref, dx_ref, _, o_ref, dg_ref = refs[2 * n:]

        @pl.when(pl.program_id(0) == 0)
        def _():
            dg_ref[...] = jnp.zeros(dg_ref.shape, F32)

        dxn = _dot(a_refs[0][...], w_refs[0][...])
        for a_ref, w_ref in zip(a_refs[1:], w_refs[1:]):
            dxn = dxn + _dot(a_ref[...], w_ref[...])
        xv = x_ref[...]
        r = _rstd(xv)
        xh = xv * r
        dxh = dxn * g_ref[...]
        o_ref[...] = dx_ref[...] + r * (dxh - xh * jnp.mean(dxh * xh, axis=-1, keepdims=True))
        dg_ref[...] = dg_ref[...] + jnp.sum(dxn * xh, axis=0, keepdims=True)

    return pl.pallas_call(
        body, name=name, grid=(s // tm,),
        in_specs=[_rows(tm, a.shape[1]) for a in acts] + [_mat(*w) for w in weights]
                 + [_rows(tm, d), _full((1, d)), _rows(tm, d), _full(dep.shape)],
        out_specs=[_rows(tm, d), _full((1, d))],
        out_shape=[jax.ShapeDtypeStruct((s, d), F32), jax.ShapeDtypeStruct((1, d), F32)],
        compiler_params=_params(),
    )(*acts, *[w[0] for w in weights], x, gain, dx, dep)


def tn_matmul(products, name):
    s, n = products[0][0].shape
    tn = _tn_tile(n) if len(products) == 1 else _col_chunk(n)
    rhs = []
    for _, b, _ in products:
        if not any(b is seen for seen in rhs):
            rhs.append(b)
    which = [next(i for i, seen in enumerate(rhs) if b is seen) for _, b, _ in products]
    npr, nr = len(products), len(rhs)

    def body(*refs):
        a_refs, b_refs, o_refs = refs[:npr], refs[npr:npr + nr], refs[npr + nr:]
        for i, (_, _, scale) in enumerate(products):
            o_refs[i][...] = (scale * _dotg(a_refs[i][...], b_refs[which[i]][...], TN)).astype(BF16)

    return pl.pallas_call(
        body, name=name, grid=(n // tn,),
        in_specs=[pl.BlockSpec((s, tn), lambda i: (0, i))] * npr
                 + [pl.BlockSpec(b.shape, lambda i: (0, 0), pipeline_mode=ONCE) for b in rhs],
        out_specs=[pl.BlockSpec((tn, b.shape[1]), lambda i: (i, 0)) for _, b, _ in products],
        out_shape=[jax.ShapeDtypeStruct((n, b.shape[1]), BF16) for _, b, _ in products],
        compiler_params=_params(),
    )(*[a for a, _, _ in products], *rhs)


def _mesh_pos():
    return lax.axis_index("x"), lax.axis_index("y"), lax.axis_index("c")


def _peers():
    x, y, c = _mesh_pos()
    peers = []
    for rel in range(1, N_DEV):
        peers.append((1 - x if rel & 4 else x, 1 - y if rel & 2 else y, 1 - c if rel & 1 else c))
    return 4 * x + 2 * y + c, peers


HBM_SPEC = pl.BlockSpec(memory_space=pltpu.HBM)
SEM_SPEC = pl.BlockSpec(memory_space=pltpu.SEMAPHORE)


def _split_call(body, name, thru, n_sems, extra=(), with_token=True):
    hbm = lambda t: pltpu.with_memory_space_constraint(t, pltpu.HBM)
    effect = pltpu.CompilerParams(has_side_effects=pltpu.SideEffectType.DATAFLOW_SIDE_EFFECTING)
    nt = len(thru)
    thru_shapes = [pltpu.HBM(t.shape, t.dtype) for t in thru]
    if with_token:
        (after,) = extra
        outs = pl.pallas_call(
            body, name=name, in_specs=[HBM_SPEC] * nt + [pl.BlockSpec(memory_space=pl.ANY)],
            out_specs=[SEM_SPEC] * len(n_sems) + [HBM_SPEC] * nt + [pl.BlockSpec(memory_space=pltpu.VMEM)],
            out_shape=[pltpu.SemaphoreType.DMA((k,)) for k in n_sems] + thru_shapes
                      + [jax.ShapeDtypeStruct((8, LANES), F32)],
            input_output_aliases={i: len(n_sems) + i for i in range(nt)}, compiler_params=effect,
        )(*[hbm(t) for t in thru], after)
        return outs[:len(n_sems)], outs[len(n_sems):-1], outs[-1]
    return pl.pallas_call(
        body, name=name,
        in_specs=[HBM_SPEC] * nt + [SEM_SPEC] * len(n_sems) + [pl.BlockSpec(memory_space=pl.ANY)],
        out_specs=[HBM_SPEC] * nt, out_shape=thru_shapes,
        input_output_aliases={i: i for i in range(nt)}, compiler_params=effect,
    )(*thru, *extra)


def _gather_targets():
    x, y, c = _mesh_pos()
    return 4 * x + 2 * y + c, [(x, y, 1 - c), (1 - x, y, c), (x, 1 - y, c), (1 - x, 1 - y, c)]


def gather_start(shards, after, name):
    n = len(shards)
    zones = [lax.empty((w.shape[0], N_DEV) + w.shape[1:], w.dtype) for w in shards]

    def body(*refs):
        ins, zs = refs[:n], refs[n:2 * n]
        send_sems, recv_sems, local_sems = refs[2 * n + 1:2 * n + 4]
        token = refs[-1]
        me, targets = _gather_targets()
        for a in range(n):
            pltpu.make_async_copy(ins[a], zs[a].at[:, me], local_sems.at[a]).start()
            for k, to in enumerate(targets):
                pltpu.make_async_remote_copy(
                    src_ref=ins[a], dst_ref=zs[a].at[:, me], send_sem=send_sems.at[4 * a + k],
                    recv_sem=recv_sems.at[4 * a + k], device_id=to, device_id_type=MESH).start()
        token[...] = jnp.zeros(token.shape, F32)

    sems, thru, token = _split_call(body, name, list(shards) + zones, (4 * n, 4 * n, n), extra=(after,))
    return (sems, thru, n), token


def gather_wait(started, after, name):
    sems, thru, n = started

    def body(*refs):
        zs = refs[n:2 * n]
        send_sems, recv_sems, local_sems = refs[2 * n:2 * n + 3]
        _, targets = _gather_targets()
        for a in range(n):
            for k, to in enumerate(targets):
                cp = pltpu.make_async_remote_copy(
                    src_ref=zs[a].at[:, 0], dst_ref=zs[a].at[:, 0], send_sem=send_sems.at[4 * a + k],
                    recv_sem=recv_sems.at[4 * a + k], device_id=to, device_id_type=MESH)
                cp.wait_send()
                cp.wait_recv()
            pltpu.make_async_copy(zs[a].at[:, 0], zs[a].at[:, 0], local_sems.at[a]).wait()

    return _split_call(body, name, thru, (4 * n, 4 * n, n), extra=(*sems, after), with_token=False)[n:]


def forward_start(zones, after, name):
    n = len(zones)

    def body(*refs):
        zs = refs[:n]
        send_sems, recv_sems = refs[n + 1:n + 3]
        token = refs[-1]
        x, y, c = _mesh_pos()
        for a in range(n):
            for j, chip in enumerate([(1 - x, y), (x, 1 - y), (1 - x, 1 - y)]):
                blk = zs[a].at[:, 4 * chip[0] + 2 * chip[1] + c]
                pltpu.make_async_remote_copy(
                    src_ref=blk, dst_ref=blk, send_sem=send_sems.at[3 * a + j], recv_sem=recv_sems.at[3 * a + j],
                    device_id=(x, y, 1 - c), device_id_type=MESH).start()
        token[...] = jnp.zeros(token.shape, F32)

    sems, thru, token = _split_call(body, name, list(zones), (3 * n, 3 * n), extra=(after,))
    return (sems, thru, n), token


def forward_wait(started, after, name):
    sems, thru, n = started

    def body(*refs):
        zs = refs[:n]
        send_sems, recv_sems = refs[n:n + 2]
        x, y, c = _mesh_pos()
        for a in range(n):
            for j in range(3):
                cp = pltpu.make_async_remote_copy(
                    src_ref=zs[a].at[:, 0], dst_ref=zs[a].at[:, 0], send_sem=send_sems.at[3 * a + j],
                    recv_sem=recv_sems.at[3 * a + j], device_id=(x, y, 1 - c), device_id_type=MESH)
                cp.wait_send()
                cp.wait_recv()

    return _split_call(body, name, thru, (3 * n, 3 * n), extra=(*sems, after), with_token=False)


def scatter_start(groups, name):
    n = len(groups)
    flat = [g for grp in groups for g in grp]
    nf = len(flat)
    offs = np.cumsum([0] + [len(grp) for grp in groups])
    lands = [lax.empty((N_DEV, len(grp)) + grp[0].shape[1:], grp[0].dtype) for grp in groups]

    def body(*refs):
        ins, zones = refs[:nf], refs[nf:nf + n]
        send_sems, recv_sems, local_sems = refs[nf + n:nf + n + 3]
        token = refs[-1]
        me, peers = _peers()
        for a in range(n):
            for w in range(len(groups[a])):
                pltpu.make_async_copy(ins[offs[a] + w].at[me], zones[a].at[me, w], local_sems.at[a]).start()
        for k, peer in enumerate(peers):
            p_id = 4 * peer[0] + 2 * peer[1] + peer[2]
            for a in range(n):
                for w in range(len(groups[a])):
                    pltpu.make_async_remote_copy(
                        src_ref=ins[offs[a] + w].at[p_id], dst_ref=zones[a].at[me, w],
                        send_sem=send_sems.at[7 * a + k], recv_sem=recv_sems.at[7 * a + k],
                        device_id=peer, device_id_type=MESH).start()
        token[...] = jnp.zeros(token.shape, F32)

    hbm = lambda t: pltpu.with_memory_space_constraint(t, pltpu.HBM)
    outs = pl.pallas_call(
        body, name=name,
        in_specs=[HBM_SPEC] * (nf + n),
        out_specs=[SEM_SPEC] * 3 + [HBM_SPEC] * (nf + n) + [pl.BlockSpec(memory_space=pltpu.VMEM)],
        out_shape=[pltpu.SemaphoreType.DMA((7 * n,)), pltpu.SemaphoreType.DMA((7 * n,)), pltpu.SemaphoreType.DMA((n,))]
                  + [pltpu.HBM(t.shape, t.dtype) for t in flat + lands]
                  + [jax.ShapeDtypeStruct((8, LANES), F32)],
        input_output_aliases={i: 3 + i for i in range(nf + n)},
        compiler_params=pltpu.CompilerParams(has_side_effects=pltpu.SideEffectType.DATAFLOW_SIDE_EFFECTING),
    )(*[hbm(t) for t in flat], *[hbm(t) for t in lands])
    sems, thru, token = outs[:3], outs[3:3 + nf + n], outs[-1]
    return (sems, thru, [len(grp) for grp in groups]), token


def scatter_wait(started, after, name):
    (send_sems, recv_sems, local_sems), thru, sizes = started
    n = len(sizes)
    nf = len(thru) - n

    def body(*refs):
        zones = refs[nf:nf + n]
        s_sems, r_sems, l_sems = refs[nf + n:nf + n + 3]
        me, peers = _peers()
        for a in range(n):
            for k, peer in enumerate(peers):
                cp = pltpu.make_async_remote_copy(
                    src_ref=zones[a].at[0], dst_ref=zones[a].at[0],
                    send_sem=s_sems.at[7 * a + k], recv_sem=r_sems.at[7 * a + k], device_id=peer,
                    device_id_type=MESH)
                cp.wait_send()
                cp.wait_recv()
            pltpu.make_async_copy(zones[a].at[0], zones[a].at[0], l_sems.at[a]).wait()

    outs = pl.pallas_call(
        body, name=name,
        in_specs=[HBM_SPEC] * (nf + n) + [SEM_SPEC] * 3 + [pl.BlockSpec(memory_space=pl.ANY)],
        out_specs=[HBM_SPEC] * (nf + n),
        out_shape=[pltpu.HBM(t.shape, t.dtype) for t in thru],
        input_output_aliases={i: i for i in range(nf + n)},
        compiler_params=pltpu.CompilerParams(has_side_effects=pltpu.SideEffectType.DATAFLOW_SIDE_EFFECTING),
    )(*thru, send_sems, recv_sems, local_sems, after)
    return outs[nf:]


def pair_start(grads, after, name):
    nw = len(grads)
    land = lax.empty((4, nw) + grads[0].shape[1:], grads[0].dtype)

    def body(*refs):
        ins, zone = refs[:nw], refs[nw]
        send_sems, recv_sems = refs[nw + 2:nw + 4]
        x, y, c = _mesh_pos()
        for j in range(4):
            for w in range(nw):
                pltpu.make_async_remote_copy(
                    src_ref=ins[w].at[2 * j + (1 - c)], dst_ref=zone.at[j, w], send_sem=send_sems.at[0],
                    recv_sem=recv_sems.at[0], device_id=(x, y, 1 - c), device_id_type=MESH).start()
        refs[-1][...] = jnp.zeros(refs[-1].shape, F32)

    sems, thru, token = _split_call(body, name, list(grads) + [land], (1, 1), extra=(after,))
    return (sems, thru, nw), token


def pair_wait(started, after, name):
    sems, thru, nw = started

    def body(*refs):
        zone = refs[nw]
        send_sems, recv_sems = refs[nw + 1:nw + 3]
        x, y, c = _mesh_pos()
        cp = pltpu.make_async_remote_copy(src_ref=zone, dst_ref=zone, send_sem=send_sems.at[0],
                                          recv_sem=recv_sems.at[0], device_id=(x, y, 1 - c), device_id_type=MESH)
        cp.wait_send()
        cp.wait_recv()

    outs = _split_call(body, name, thru, (1, 1), extra=(*sems, after), with_token=False)
    return outs[:nw], outs[nw]


def pair_sum(grads, land, name):
    nw = len(grads)
    _, r, c_dim = grads[0].shape

    def body(*refs):
        g_refs, l_ref, o_ref = refs[:nw], refs[nw], refs[nw + 1]
        core = lax.axis_index("c")
        for w in range(nw):
            o_ref[0, w] = (g_refs[w][0, core].astype(F32) + l_ref[0, w].astype(F32)).astype(BF16)

    return pl.pallas_call(
        body, name=name, grid=(4,),
        in_specs=[pl.BlockSpec((1, 2, r, c_dim), lambda j: (j, 0, 0, 0))] * nw
                 + [pl.BlockSpec((1, nw, r, c_dim), lambda j: (j, 0, 0, 0))],
        out_specs=pl.BlockSpec((1, nw, r, c_dim), lambda j: (j, 0, 0, 0)),
        out_shape=jax.ShapeDtypeStruct((4, nw, r, c_dim), BF16),
        compiler_params=_params(),
    )(*[g.reshape(4, 2, r, c_dim) for g in grads], land)


def _other_chips():
    x, y, c = _mesh_pos()
    chips = []
    for rel in range(1, 4):
        px, py = (1 - x if rel & 2 else x), (1 - y if rel & 1 else y)
        chips.append((px, py, 2 * px + py))
    return 2 * x + y, c, chips


def chip_start(pair_sums, after, name):
    land = lax.empty(pair_sums.shape, pair_sums.dtype)

    def body(*refs):
        h_ref, zone = refs[0], refs[1]
        send_sems, recv_sems, local_sem = refs[3:6]
        mine, c, chips = _other_chips()
        pltpu.make_async_copy(h_ref.at[mine], zone.at[mine], local_sem.at[0]).start()
        for k, (px, py, j) in enumerate(chips):
            pltpu.make_async_remote_copy(
                src_ref=h_ref.at[j], dst_ref=zone.at[mine], send_sem=send_sems.at[k], recv_sem=recv_sems.at[k],
                device_id=(px, py, c), device_id_type=MESH).start()
        refs[-1][...] = jnp.zeros(refs[-1].shape, F32)

    sems, thru, token = _split_call(body, name, [pair_sums, land], (3, 3, 1), extra=(after,))
    return (sems, thru), token


def chip_wait(started, after, name):
    sems, thru = started

    def body(*refs):
        zone = refs[1]
        send_sems, recv_sems, local_sem = refs[2:5]
        _, c, chips = _other_chips()
        for k, (px, py, _) in enumerate(chips):
            cp = pltpu.make_async_remote_copy(
                src_ref=zone.at[0], dst_ref=zone.at[0], send_sem=send_sems.at[k], recv_sem=recv_sems.at[k],
                device_id=(px, py, c), device_id_type=MESH)
            cp.wait_send()
            cp.wait_recv()
        pltpu.make_async_copy(zone.at[0], zone.at[0], local_sem.at[0]).wait()

    return _split_call(body, name, thru, (3, 3, 1), extra=(*sems, after), with_token=False)[1]


def share_start(parts, after, name):
    n = len(parts)
    zones = [lax.empty((N_DEV,) + p.shape, p.dtype) for p in parts]

    def body(*refs):
        ins, zs = refs[:n], refs[n:2 * n]
        send_sems, recv_sems, local_sems = refs[2 * n + 1:2 * n + 4]
        me, peers = _peers()
        for i in range(n):
            pltpu.make_async_copy(ins[i], zs[i].at[me], local_sems.at[i]).start()
            for k, peer in enumerate(peers):
                pltpu.make_async_remote_copy(
                    src_ref=ins[i], dst_ref=zs[i].at[me], send_sem=send_sems.at[7 * i + k],
                    recv_sem=recv_sems.at[7 * i + k], device_id=peer, device_id_type=MESH).start()
        refs[-1][...] = jnp.zeros(refs[-1].shape, F32)

    sems, thru, token = _split_call(body, name, list(parts) + zones, (7 * n, 7 * n, n), extra=(after,))
    return (sems, thru, n), token


def share_wait(started, after, name):
    sems, thru, n = started

    def body(*refs):
        zs = refs[n:2 * n]
        send_sems, recv_sems, local_sems = refs[2 * n:2 * n + 3]
        _, peers = _peers()
        for i in range(n):
            for k, peer in enumerate(peers):
                cp = pltpu.make_async_remote_copy(
                    src_ref=zs[i].at[0], dst_ref=zs[i].at[0], send_sem=send_sems.at[7 * i + k],
                    recv_sem=recv_sems.at[7 * i + k], device_id=peer, device_id_type=MESH)
                cp.wait_send()
                cp.wait_recv()
            pltpu.make_async_copy(zs[i].at[0], zs[i].at[0], local_sems.at[i]).wait()

    return _split_call(body, name, thru, (7 * n, 7 * n, n), extra=(*sems, after), with_token=False)[n:]


def _adamw_math(w, g, m, v):
    m = ADAM_B1 * m + (1.0 - ADAM_B1) * g
    v = ADAM_B2 * v + (1.0 - ADAM_B2) * (g * g)
    m_hat = m / (1.0 - ADAM_B1 ** ADAM_STEP)
    v_hat = v / (1.0 - ADAM_B2 ** ADAM_STEP)
    delta = -ADAM_LR * (m_hat / (jnp.sqrt(v_hat) + ADAM_EPS) + ADAM_WD * w)
    return delta, m, v


ADAMW_BLOCK_BYTES = 24 * 1024 * 1024


def adamw_layer(zone, layer, items, after, name):
    n_src, nw, r, c = zone.shape
    depth = items[0][0].shape[0]
    prevs = [p if p is not None else tuple(lax.empty((depth, r, c), F32) for _ in range(4)) for _, _, _, p in items]
    row_bytes = 2 * nw * c * (2 * n_src + 4 * 7)
    tr = max(t for t in range(8, r + 1, 8) if r % t == 0 and t * row_bytes <= ADAMW_BLOCK_BYTES)

    def body(z_ref, *rest):
        ins, outs = rest[:3 * nw], rest[7 * nw + 1:]
        for i in range(nw):
            g = z_ref[0, i].astype(F32)
            for src in range(1, n_src):
                g = g + z_ref[src, i].astype(F32)
            g_ref, d_ref, mo_ref, vo_ref = outs[4 * i:4 * i + 4]
            w_ref, m_ref, v_ref = ins[3 * i:3 * i + 3]
            g_ref[...] = g
            d_ref[...], mo_ref[...], vo_ref[...] = _adamw_math(w_ref[...], g, m_ref[...], v_ref[...])

    rows = pl.BlockSpec((None, tr, c), lambda i: (layer, i, 0))
    anywhere = pl.BlockSpec(memory_space=pl.ANY)
    outs = pl.pallas_call(
        body, name=name, grid=(r // tr,),
        in_specs=[pl.BlockSpec((n_src, nw, tr, c), lambda i: (0, 0, i, 0))] + [rows] * (3 * nw)
                 + [anywhere] * (4 * nw + 1),
        out_specs=[rows] * (4 * nw),
        out_shape=[jax.ShapeDtypeStruct((depth, r, c), F32)] * (4 * nw),
        input_output_aliases={1 + 3 * nw + k: k for k in range(4 * nw)},
        compiler_params=_params(),
    )(zone, *[t for w, m, v, _ in items for t in (w, m, v)], *[t for p in prevs for t in p], after)
    return [tuple(outs[4 * i:4 * i + 4]) for i in range(nw)]


def adamw_small(ws, recvs, ms, vs, name):
    n = len(ws)

    def body(*refs):
        w_refs, r_refs, m_refs, v_refs = (refs[i * n:(i + 1) * n] for i in range(4))
        g_refs, d_refs, mo_refs, vo_refs = (refs[(4 + i) * n:(5 + i) * n] for i in range(4))
        for i in range(n):
            g = r_refs[i][0]
            for src in range(1, N_DEV):
                g = g + r_refs[i][src]
            g_refs[i][...] = g
            d_refs[i][...], mo_refs[i][...], vo_refs[i][...] = _adamw_math(w_refs[i][...], g, m_refs[i][...],
                                                                            v_refs[i][...])

    vm = pl.BlockSpec(memory_space=pltpu.VMEM)
    outs = pl.pallas_call(
        body, name=name, in_specs=[vm] * (4 * n), out_specs=[vm] * (4 * n),
        out_shape=[jax.ShapeDtypeStruct(w.shape, F32) for w in ws] * 4,
        compiler_params=pltpu.CompilerParams(vmem_limit_bytes=V7X_VMEM_LIMIT),
    )(*ws, *recvs, *ms, *vs)
    return [outs[i * n:(i + 1) * n] for i in range(4)]


SMALL_NAMES = ("ffn1_norm", "mix_norm", "ffn2_norm", "b_gate", "na_q_norm", "na_k_norm", "sw_q_norm", "sw_k_norm",
               "na_rpb", "sw_sink", "t5_rel_table")


def kernel(x, ffn1_norm, ffn1_w_gate, ffn1_w_up, ffn1_w_down, mix_norm, w_in, b_gate, na_q_norm, na_k_norm, na_rpb, sw_q_norm, sw_k_norm, sw_sink, t5_rel_table, w_branch_na, w_branch_sw, w_out, ffn2_norm, ffn2_w_gate, ffn2_w_up, ffn2_w_down, loss_target, m_ffn1_norm, m_ffn1_w_gate, m_ffn1_w_up, m_ffn1_w_down, m_mix_norm, m_w_in, m_b_gate, m_na_q_norm, m_na_k_norm, m_na_rpb, m_sw_q_norm, m_sw_k_norm, m_sw_sink, m_t5_rel_table, m_w_branch_na, m_w_branch_sw, m_w_out, m_ffn2_norm, m_ffn2_w_gate, m_ffn2_w_up, m_ffn2_w_down, v_ffn1_norm, v_ffn1_w_gate, v_ffn1_w_up, v_ffn1_w_down, v_mix_norm, v_w_in, v_b_gate, v_na_q_norm, v_na_k_norm, v_na_rpb, v_sw_q_norm, v_sw_k_norm, v_sw_sink, v_t5_rel_table, v_w_branch_na, v_w_branch_sw, v_w_out, v_ffn2_norm, v_ffn2_w_gate, v_ffn2_w_up, v_ffn2_w_down):
    weights = dict(ffn1_norm=ffn1_norm, ffn1_w_gate=ffn1_w_gate, ffn1_w_up=ffn1_w_up, ffn1_w_down=ffn1_w_down,
                   mix_norm=mix_norm, w_in=w_in, b_gate=b_gate, na_q_norm=na_q_norm, na_k_norm=na_k_norm,
                   na_rpb=na_rpb, sw_q_norm=sw_q_norm, sw_k_norm=sw_k_norm, sw_sink=sw_sink,
                   t5_rel_table=t5_rel_table, w_branch_na=w_branch_na, w_branch_sw=w_branch_sw, w_out=w_out,
                   ffn2_norm=ffn2_norm, ffn2_w_gate=ffn2_w_gate, ffn2_w_up=ffn2_w_up, ffn2_w_down=ffn2_w_down)
    mom_m = dict(ffn1_norm=m_ffn1_norm, ffn1_w_gate=m_ffn1_w_gate, ffn1_w_up=m_ffn1_w_up, ffn1_w_down=m_ffn1_w_down,
                 mix_norm=m_mix_norm, w_in=m_w_in, b_gate=m_b_gate, na_q_norm=m_na_q_norm, na_k_norm=m_na_k_norm,
                 na_rpb=m_na_rpb, sw_q_norm=m_sw_q_norm, sw_k_norm=m_sw_k_norm, sw_sink=m_sw_sink,
                 t5_rel_table=m_t5_rel_table, w_branch_na=m_w_branch_na, w_branch_sw=m_w_branch_sw, w_out=m_w_out,
                 ffn2_norm=m_ffn2_norm, ffn2_w_gate=m_ffn2_w_gate, ffn2_w_up=m_ffn2_w_up, ffn2_w_down=m_ffn2_w_down)
    mom_v = dict(ffn1_norm=v_ffn1_norm, ffn1_w_gate=v_ffn1_w_gate, ffn1_w_up=v_ffn1_w_up, ffn1_w_down=v_ffn1_w_down,
                 mix_norm=v_mix_norm, w_in=v_w_in, b_gate=v_b_gate, na_q_norm=v_na_q_norm, na_k_norm=v_na_k_norm,
                 na_rpb=v_na_rpb, sw_q_norm=v_sw_q_norm, sw_k_norm=v_sw_k_norm, sw_sink=v_sw_sink,
                 t5_rel_table=v_t5_rel_table, w_branch_na=v_w_branch_na, w_branch_sw=v_w_branch_sw, w_out=v_w_out,
                 ffn2_norm=v_ffn2_norm, ffn2_w_gate=v_ffn2_w_gate, ffn2_w_up=v_ffn2_w_up, ffn2_w_down=v_ffn2_w_down)
    order = list(weights)

    depth = ffn1_norm.shape[0]
    s, d = x.shape[1], x.shape[2]
    xs = x[0]
    tr = lambda w: jnp.swapaxes(w, -1, -2)

    merge = lambda t: t.reshape(t.shape[0], N_DEV * t.shape[2], t.shape[3])
    no_dep = jnp.zeros((8, LANES), F32)

    def shards_of(kind, l):
        stack = lambda *ws: jnp.stack(ws).astype(BF16)
        if kind == "ffn1":
            return [stack(tr(ffn1_w_gate[l]), tr(ffn1_w_up[l]), ffn1_w_down[l])]
        if kind == "win":
            return [stack(tr(w_in[l]))]
        return [stack(tr(ffn2_w_gate[l]), tr(ffn2_w_up[l]), ffn2_w_down[l]), stack(w_out[l]),
                stack(tr(w_branch_na[l]), tr(w_branch_sw[l]))]

    def start(kind, l, after):
        return gather_start(shards_of(kind, l), after, f"gather_{kind}_{l}")

    def arrive(started, kind, l, after):
        zones = gather_wait(started, after, f"gather_{kind}_{l}_wait")
        return forward_start(zones, no_dep, f"forward_{kind}_{l}")

    def finish(fwd, kind, l, after):
        return [merge(z) for z in forward_wait(fwd, after, f"forward_{kind}_{l}_wait")]

    bd = jnp.asarray(np.kron(np.eye(MXU_TILE // HEAD_DIM), np.full((HEAD_DIM, HEAD_DIM), 1.0 / HEAD_DIM)), BF16)
    bmap = jnp.asarray(_t5_bucket_map())
    tile8 = lambda g: jnp.tile(g, NA_WIDTH // HEAD_DIM).reshape(1, NA_WIDTH)
    tile2 = lambda g: jnp.tile(g, SW_KV_WIDTH // HEAD_DIM).reshape(1, SW_KV_WIDTH)

    st_first, tok = start("ffn1", 0, no_dep)
    t5b = t5_expand(t5_rel_table, bmap, tok, "t5_expand").reshape(SW_STACK, 3 * SW_BLOCK)
    t2_tables = [rpb_expand(_rpb_rows(na_rpb[l]), tok, f"rpb_expand_{l}") for l in range(depth)]
    fwd, _ = arrive(st_first, "ffn1", 0, t2_tables[-1])
    st_win, dep = start("win", 0, t5b)
    (first,) = finish(fwd, "ffn1", 0, dep)

    saved = []
    layer_w = {0: dict(wg1=(first, 0), wu1=(first, 1), wd1=(first, 2))}
    cur = xs
    for l in range(depth):
        sv = {}
        lw = layer_w[l]
        sv["x0"] = cur
        cur, sv["xn1"], sv["hg1"], sv["hu1"], sv["act1"] = ffn_forward(
            cur, ffn1_norm[l][None], lw["wg1"], lw["wu1"], lw["wd1"], dep, f"ffn1_{l}")
        sv["x1"] = cur
        fwd, _ = arrive(st_win, "win", l, cur)
        st_rest, tok = start("rest", l, cur)
        (zb,) = finish(fwd, "win", l, tok)
        lw["win"] = (zb, 0)
        sv["gains"] = (tile8(na_q_norm[l]), tile8(na_k_norm[l]), tile8(sw_q_norm[l]), tile2(sw_k_norm[l]))
        sv["hn"], sv["zq"], sv["qa"], sv["ka"], sv["qs"], sv["ks"], sv["gt"] = mix_in(
            cur, mix_norm[l][None], lw["win"], b_gate[l][None], *sv["gains"], bd, f"mix_in_{l}")
        sv["t2"] = t2_tables[l]
        sv["o_na"] = na_fwd(sv["qa"], sv["ka"], sv["zq"], sv["t2"], f"na_fwd_{l}")
        dep = no_dep
        if l + 1 < depth:
            st_ffn1, dep = start("ffn1", l + 1, sv["o_na"])
        sv["o_sw"] = sw_fwd(sv["qs"], sv["ks"], sv["zq"], t5b, sw_sink[l], dep, f"sw_fwd_{l}")
        fwd, tok = arrive(st_rest, "rest", l, sv["o_sw"][0:8, 0:LANES] + sv["o_na"][0:8, 0:LANES])
        za, zc, zd = finish(fwd, "rest", l, tok)
        lw.update(wg2=(za, 0), wu2=(za, 1), wd2=(za, 2), wout=(zc, 0), wna=(zd, 0), wsw=(zd, 1))
        cur, sv["a_na"], sv["a_sw"], sv["merged"] = merge_out(
            cur, sv["o_na"], sv["o_sw"], sv["gt"], lw["wna"], lw["wsw"], lw["wout"], f"merge_out_{l}")
        sv["x2"] = cur
        dep = no_dep
        if l + 1 < depth:
            st_win, dep = start("win", l + 1, cur)
        sv["xn2"], sv["hg2"], sv["hu2"], sv["act2"] = ffn_forward(
            cur, ffn2_norm[l][None], lw["wg2"], lw["wu2"], None, dep, f"ffn2_up_{l}")
        dep = no_dep
        if l + 1 < depth:
            fwd, dep = arrive(st_ffn1, "ffn1", l + 1, sv["act2"])
        if l + 1 < depth:
            cur = ffn_down(cur, sv["act2"], lw["wd2"], dep, f"ffn2_down_{l}")
            (za,) = finish(fwd, "ffn1", l + 1, cur)
            layer_w[l + 1] = dict(wg1=(za, 0), wu1=(za, 1), wd1=(za, 2))
        else:
            dx, loss_acc = ffn_down(cur, sv["act2"], lw["wd2"], dep, f"ffn2_down_{l}", target=loss_target[0])
        dep = no_dep
        saved.append(sv)

    split = lambda t: t.reshape(N_DEV, t.shape[0] // N_DEV, t.shape[1])
    pending = {}
    last_key = "ffn1_0"
    two_level = {last_key}
    small = {k: [None] * depth for k in SMALL_NAMES if k != "t5_rel_table"}
    dbias_sw = []
    for l in reversed(range(depth)):
        sv = saved[l]
        lw = layer_w[l]
        wg1, wu1, wd1, wg2, wu2, wd2 = (lw[k] for k in ("wg1", "wu1", "wd1", "wg2", "wu2", "wd2"))
        win_t, wout_l, wna_t, wsw_t = lw["win"], lw["wout"], lw["wna"], lw["wsw"]
        blocks = ((2, "x2", "xn2", "hg2", "hu2", "act2", wg2, wu2, wd2, "ffn2_norm", 3),
                  (1, "x0", "xn1", "hg1", "hu1", "act1", wg1, wu1, wd1, "ffn1_norm", 0))

        def ffn_backward(dx, blk):
            tag, xk, xnk, hgk, huk, actk, wg, wu, wd, norm_name, slot = blk
            gains = weights[norm_name]
            dxb, dhg, dhu = ffn_bwd_act(dx, wd, sv[hgk], sv[huk], f"ffn{tag}_bwd_act_{l}")
            gwg, gwu, gwd = tn_matmul([(dhg, sv[xnk], 1.0), (dhu, sv[xnk], 1.0), (sv[actk], dxb, 0.5)],
                                      f"ffn{tag}_dw_{l}")
            key = f"ffn{tag}_{l}"
            blocks_of = [split(gwg), split(gwu), split(gwd)]
            if key in two_level:
                paired, token = pair_start(blocks_of, dxb, f"pair_{key}")
            else:
                pending[key], token = scatter_start([blocks_of], f"scatter_{key}")
            dx, dg = proj_bwd_norm([dhg, dhu], [wg, wu], sv[xk], gains[l][None], dx, token, f"ffn{tag}_bwd_x_{l}")
            token = no_dep
            if key in two_level:
                thru, land = pair_wait(paired, dx, f"pair_{key}_wait")
                pending[key], token = chip_start(pair_sum(thru, land, f"pair_sum_{key}"), dg, f"chips_{key}")
            small[norm_name][l] = dg[0]
            return dx, token

        dx, token = ffn_backward(dx, blocks[0])
        dxb, dzg, da_na, da_sw, do_na, do_sw, dbg = mix_bwd_out(
            dx, sv["gt"], sv["a_na"], sv["a_sw"], wna_t, wsw_t, wout_l, token, f"mix_bwd_out_{l}")
        small["b_gate"][l] = dbg[0]
        gwout, gwna, gwsw = tn_matmul([(sv["merged"], dxb, 1.0), (da_na, sv["o_na"], 1.0), (da_sw, sv["o_sw"], 1.0)],
                                      f"mix_dw_{l}")
        dqa, dka, dva, dt2 = na_bwd(sv["qa"], sv["ka"], sv["zq"], sv["t2"], sv["o_na"], do_na, f"na_bwd_{l}")
        dqs, dks, dvs, dbias, dsink = sw_bwd(sv["qs"], sv["ks"], sv["zq"], t5b, sw_sink[l], sv["o_sw"], do_sw,
                                             f"sw_bwd_{l}")
        dbias_sw.append(dbias.reshape(SW_HEADS, SW_BLOCK, 3 * SW_BLOCK))
        small["sw_sink"][l] = jnp.sum(dsink[:, 0].reshape(SW_HEADS, SW_BLOCK), axis=1)
        small["na_rpb"][l] = _rpb_from_rows(rpb_reduce(dt2, f"rpb_reduce_{l}"))
        dz, dgqa, dgka, dgqs, dgks = qk_norm_bwd(dqa, dka, dva, dqs, dks, dvs, sv["zq"], dzg, *sv["gains"], bd,
                                                 f"qk_norm_bwd_{l}")
        fold = lambda g: jnp.sum(g.reshape(-1, HEAD_DIM), axis=0)
        small["na_q_norm"][l], small["na_k_norm"][l] = fold(dgqa), fold(dgka)
        small["sw_q_norm"][l], small["sw_k_norm"][l] = fold(dgqs), fold(dgks)
        (gwin,) = tn_matmul([(dz, sv["hn"], 1.0)], f"dwin_{l}")
        pending[f"mix_{l}"], token = scatter_start([[split(gwout)], [split(gwna), split(gwsw)], [split(gwin)]],
                                                   f"scatter_mix_{l}")
        dx, dg = proj_bwd_norm([dz], [win_t], sv["x1"], mix_norm[l][None], dx, token, f"mix_bwd_x_{l}")
        small["mix_norm"][l] = dg[0]
        dx, tail = ffn_backward(dx, blocks[1])

    dtab = t5_reduce(dbias_sw, bmap, "t5_reduce")
    small_parts = {k: jnp.stack(v) for k, v in small.items()}
    small_parts["t5_rel_table"] = jnp.transpose(dtab[:, :, 0])

    grads, delta, new_m, new_v = {}, {}, {}, {}
    state = {}
    sharing, token = share_start([small_parts[k] for k in SMALL_NAMES] + [loss_acc], tail, "share_small")
    chain = [token]
    members = {"ffn": lambda t: [(f"ffn{t}_w_gate", 0, 0, True), (f"ffn{t}_w_up", 0, 1, True),
                                 (f"ffn{t}_w_down", 0, 2, False)],
               "mix": lambda t: [("w_out", 0, 0, False), ("w_branch_na", 1, 0, True), ("w_branch_sw", 1, 1, True),
                                 ("w_in", 2, 0, True)]}

    def collect(key):
        if key in two_level:
            zones = [chip_wait(pending[key], chain[0], f"wait_{key}")]
        else:
            zones = scatter_wait(pending[key], chain[0], f"wait_{key}")
        kind, l = key.split("_")
        group = members[kind[:3]](kind[3:])
        complete = all(f"{kind}_{j}" in done for j in range(depth) if j != int(l))
        for zi, zone in enumerate(zones):
            mine = sorted((wi, k, transposed) for k, z, wi, transposed in group if z == zi)
            views = [tr if transposed else (lambda t: t) for _, _, transposed in mine]
            items = [(view(weights[k]), view(mom_m[k]), view(mom_v[k]), state.get(k))
                     for (_, k, _), view in zip(mine, views)]
            results = adamw_layer(zone, int(l), items, chain[0], f"adamw_{key}_{zi}")
            chain[0] = results[-1][1]
            for (_, k, _), view, res in zip(mine, views, results):
                state[k] = res
                if complete:
                    grads[k], delta[k], new_m[k], new_v[k] = (view(t) for t in res)
        done.add(key)

    done = set()
    for key in pending:
        if key != last_key:
            collect(key)
    collect(last_key)
    *recvs, all_losses = share_wait(sharing, chain[0], "share_small_wait")
    loss = jnp.sum(all_losses) * (0.5 / d)
    results = adamw_small([weights[k] for k in SMALL_NAMES], recvs, [mom_m[k] for k in SMALL_NAMES],
                          [mom_v[k] for k in SMALL_NAMES], "adamw_small")
    for dst, outs in zip((grads, delta, new_m, new_v), results):
        dst.update(dict(zip(SMALL_NAMES, outs)))

    return (loss, dx[None], *[grads[k] for k in order], *[delta[k] for k in order],
            *[new_m[k] for k in order], *[new_v[k] for k in order])
```

```python
import functools
import math

import numpy as np
import jax
import jax.numpy as jnp
from jax import lax
from jax.experimental import pallas as pl
from jax.experimental.pallas import tpu as pltpu

F32 = jnp.float32
BF16 = jnp.bfloat16
MESH = pl.DeviceIdType.MESH

N_DEV = 8
EPS = 1e-6
NEG = -1e30
HEAD_DIM = 64
GRID_W = 64
NA_ROWS = 8
NA_COLS = 16
NA_WIDTH = 512
SW_Q_WIDTH = 512
SW_KV_WIDTH = 128
SW_BLOCK = 128
SW_HEADS = 8
SW_REP = 4
REL_BUCKETS = 32
REL_MAX_DIST = 128
QKV_WIDTH = 3 * NA_WIDTH + SW_Q_WIDTH + 2 * SW_KV_WIDTH
SCALE = 1.0 / math.sqrt(HEAD_DIM)

ADAM_LR = 0.001
ADAM_B1 = 0.9
ADAM_B2 = 0.999
ADAM_EPS = 1e-08
ADAM_WD = 0.01
ADAM_STEP = 10

V7X_VMEM_LIMIT = 56 * 1024 * 1024
LANES = 128
MXU_TILE = 256

NT = (((1,), (1,)), ((), ()))
TN = (((0,), (0,)), ((), ()))


def _params(n_grid=1):
    return pltpu.CompilerParams(dimension_semantics=("arbitrary",) * n_grid,
                                vmem_limit_bytes=V7X_VMEM_LIMIT)


def _row_tile(s):
    for t in (512, 256, 128, 64, 32, 16, 8):
        if s % t == 0:
            return t
    raise ValueError(s)


def _tn_tile(n):
    best = max(t for t in range(LANES, min(n, 2304) + 1, LANES) if n % t == 0) if n % LANES == 0 else n
    return best // 2 if best == n and n >= 1024 else best


ONCE = pl.Buffered(1)


def _col_chunk(n):
    return MXU_TILE if n % MXU_TILE == 0 else n


def _dot(a, b):
    return jnp.dot(a, b, preferred_element_type=F32)


def _dotg(a, b, dn):
    return lax.dot_general(a, b, dn, preferred_element_type=F32)


def _sigmoid(v):
    return 1.0 / (1.0 + jnp.exp(-v))


def _rstd(xv):
    return lax.rsqrt(jnp.mean(xv * xv, axis=-1, keepdims=True) + EPS)


def _full(shape):
    nd = len(shape)
    return pl.BlockSpec(shape, lambda i, _n=nd: (0,) * _n)


def _rows(tm, width):
    return pl.BlockSpec((tm, width), lambda i: (i, 0))


def _mat(stack, idx):
    return pl.BlockSpec((None,) + tuple(stack.shape[1:]), lambda i, _w=idx: (_w, 0, 0), pipeline_mode=ONCE)


def _group_mean(v, bd):
    w = bd.shape[0]
    if v.shape[1] > w:
        return jnp.concatenate([_group_mean(v[:, c0:c0 + w], bd) for c0 in range(0, v.shape[1], w)], axis=1)
    hi = v.astype(BF16)
    lo = (v - hi.astype(F32)).astype(BF16)
    return _dot(hi, bd) + _dot(lo, bd)


def ffn_forward(x, gain, wg_t, wu_t, wd, dep, name):
    s, d = x.shape
    f = wg_t[0].shape[1]
    tm = _row_tile(s) if wd is None else min(_row_tile(s), 256)
    fc = _col_chunk(f)
    nw = 2 if wd is None else 3

    def body(x_ref, g_ref, *refs):
        w_refs, outs = refs[:nw], refs[nw + 1:]
        xn_ref, dg_ref, du_ref, act_ref = outs[-4:]
        xv = x_ref[...]
        xn = (xv * _rstd(xv) * g_ref[...]).astype(BF16)
        xn_ref[...] = xn
        for c0 in range(0, f, fc):
            hg = _dotg(xn, w_refs[0][c0:c0 + fc, :], NT)
            hu = _dotg(xn, w_refs[1][c0:c0 + fc, :], NT)
            sg = _sigmoid(hg)
            silu = hg * sg
            du_ref[:, c0:c0 + fc] = silu.astype(BF16)
            dg_ref[:, c0:c0 + fc] = (hu * (sg + silu * (1.0 - sg))).astype(BF16)
            act_ref[:, c0:c0 + fc] = (silu * hu).astype(BF16)
        if wd is not None:
            outs[0][...] = xv + 0.5 * _dot(act_ref[...], w_refs[2][...])

    weights = [wg_t, wu_t] + ([] if wd is None else [wd])
    out_specs = [_rows(tm, d), _rows(tm, f), _rows(tm, f), _rows(tm, f)]
    out_shape = [jax.ShapeDtypeStruct((s, d), BF16)] + [jax.ShapeDtypeStruct((s, f), BF16)] * 3
    if wd is not None:
        out_specs, out_shape = [_rows(tm, d)] + out_specs, [jax.ShapeDtypeStruct((s, d), F32)] + out_shape
    return pl.pallas_call(
        body, name=name, grid=(s // tm,),
        in_specs=[_rows(tm, d), _full((1, d))] + [_mat(*w) for w in weights] + [_full(dep.shape)],
        out_specs=out_specs, out_shape=out_shape,
        compiler_params=_params(),
    )(x, gain, *[w[0] for w in weights], dep)


def ffn_down(x, act, wd, dep, name, target=None):
    s, d = x.shape
    f = act.shape[1]
    tm = _row_tile(s)

    def body(x_ref, a_ref, w_ref, dep_ref, *rest):
        y = x_ref[...] + 0.5 * _dot(a_ref[...], w_ref[...])
        if target is None:
            rest[0][...] = y
            return
        t_ref, dy_ref, acc_ref = rest

        @pl.when(pl.program_id(0) == 0)
        def _():
            acc_ref[...] = jnp.zeros(acc_ref.shape, F32)

        err = y - t_ref[...]
        dy_ref[...] = err * (1.0 / d)
        part = jnp.sum((err * err).reshape(tm // 8, 8, d), axis=0)
        acc = part[:, 0:LANES]
        for c0 in range(LANES, d, LANES):
            acc = acc + part[:, c0:c0 + LANES]
        acc_ref[...] = acc_ref[...] + acc

    ins = [_rows(tm, d), _rows(tm, f), _mat(*wd), _full(dep.shape)]
    if target is None:
        return pl.pallas_call(
            body, name=name, grid=(s // tm,), in_specs=ins, out_specs=_rows(tm, d),
            out_shape=jax.ShapeDtypeStruct((s, d), F32), compiler_params=_params(),
        )(x, act, wd[0], dep)
    return pl.pallas_call(
        body, name=name, grid=(s // tm,), in_specs=ins + [_rows(tm, d)],
        out_specs=[_rows(tm, d), _full((8, LANES))],
        out_shape=[jax.ShapeDtypeStruct((s, d), F32), jax.ShapeDtypeStruct((8, LANES), F32)],
        compiler_params=_params(),
    )(x, act, wd[0], dep, target)


def mix_in(x, gain, win_t, b_gate, gq_na, gk_na, gq_sw, gk_sw, bd, name):
    s, d = x.shape
    tm = _row_tile(s)
    gc = _col_chunk(2 * d)

    def body(x_ref, g_ref, w_ref, b_ref, gqa_ref, gka_ref, gqs_ref, gks_ref, bd_ref,
             hn_ref, zq_ref, qa_ref, ka_ref, qs_ref, ks_ref, gt_ref):
        xv = x_ref[...]
        hn = (xv * _rstd(xv) * g_ref[...]).astype(BF16)
        hn_ref[...] = hn

        def proj(c0, c1):
            return _dotg(hn, w_ref[c0:c1, :], NT)

        def headnorm(z, g, bdm):
            return z * lax.rsqrt(_group_mean(z * z, bdm) + EPS) * g

        bd512 = bd_ref[...]
        bd128 = bd_ref[0:SW_KV_WIDTH, 0:SW_KV_WIDTH]
        z = proj(0, 512)
        zq_ref[:, 0:512] = z.astype(BF16)
        qa_ref[...] = (headnorm(z, gqa_ref[...], bd512) * SCALE).astype(BF16)
        z = proj(512, 1024)
        zq_ref[:, 512:1024] = z.astype(BF16)
        ka_ref[...] = headnorm(z, gka_ref[...], bd512).astype(BF16)
        z = proj(1024, 1536)
        zq_ref[:, 1024:1536] = z.astype(BF16)
        z = proj(1536, 2048)
        zq_ref[:, 1536:2048] = z.astype(BF16)
        qs_ref[...] = (headnorm(z, gqs_ref[...], bd512) * SCALE).astype(BF16)
        z = proj(2048, 2176)
        zq_ref[:, 2048:2176] = z.astype(BF16)
        ks_ref[...] = headnorm(z, gks_ref[...], bd128).astype(BF16)
        z = proj(2176, 2304)
        zq_ref[:, 2176:2304] = z.astype(BF16)
        for c0 in range(0, 2 * d, gc):
            zg = proj(QKV_WIDTH + c0, QKV_WIDTH + c0 + gc) + b_ref[:, c0:c0 + gc]
            gt_ref[:, c0:c0 + gc] = _sigmoid(zg).astype(BF16)

    return pl.pallas_call(
        body, name=name, grid=(s // tm,),
        in_specs=[_rows(tm, d), _full((1, d)), _mat(*win_t), _full((1, 2 * d)),
                  _full((1, 512)), _full((1, 512)), _full((1, 512)), _full((1, 128)), _full((MXU_TILE, MXU_TILE))],
        out_specs=[_rows(tm, d), _rows(tm, QKV_WIDTH), _rows(tm, 512), _rows(tm, 512), _rows(tm, 512),
                   _rows(tm, 128), _rows(tm, 2 * d)],
        out_shape=[jax.ShapeDtypeStruct((s, d), BF16), jax.ShapeDtypeStruct((s, QKV_WIDTH), BF16),
                   jax.ShapeDtypeStruct((s, 512), BF16), jax.ShapeDtypeStruct((s, 512), BF16),
                   jax.ShapeDtypeStruct((s, 512), BF16), jax.ShapeDtypeStruct((s, 128), BF16),
                   jax.ShapeDtypeStruct((s, 2 * d), BF16)],
        compiler_params=_params(),
    )(x, gain, win_t[0], b_gate, gq_na, gk_na, gq_sw, gk_sw, bd)


def _na_iotas():
    qc = lax.broadcasted_iota(jnp.int32, (GRID_W, LANES), 0)
    ln = lax.broadcasted_iota(jnp.int32, (GRID_W, LANES), 1)
    low = ln < GRID_W
    kc = jnp.where(low, ln, ln - GRID_W)
    diff = kc - qc + (NA_COLS - 1)
    qcs = jnp.clip(qc - NA_COLS // 2, 0, GRID_W - NA_COLS)
    inwin = (kc >= qcs) & (kc < qcs + NA_COLS)
    return diff, low, inwin


NA_RI = 2 * NA_ROWS - 1
NA_CI = 2 * NA_COLS - 1
NA_T2 = NA_RI + 1


def _rpb_rows(rpb):
    h = rpb.shape[0]
    padded = jnp.pad(rpb, ((0, 0), (1, 1), (0, GRID_W - NA_CI)))
    return jnp.concatenate([padded[:, :NA_T2], padded[:, 1:NA_T2 + 1]], axis=2).reshape(h, NA_T2, LANES)


def _rpb_from_rows(rows):
    return rows[:, 1:, :NA_CI] + rows[:, :NA_RI, GRID_W:GRID_W + NA_CI]


def rpb_expand(rows, dep, name):
    n_heads = rows.shape[0]

    def body(r_ref, dep_ref, o_ref):
        for h in range(n_heads):
            for e in range(NA_T2):
                line = jnp.broadcast_to(r_ref[h, e:e + 1, :], (GRID_W, LANES))
                o_ref[h, e] = pltpu.roll(line, LANES - (NA_COLS - 1), 1, stride=1, stride_axis=0)

    return pl.pallas_call(
        body, name=name,
        in_specs=[pl.BlockSpec(memory_space=pltpu.VMEM), pl.BlockSpec(memory_space=pltpu.VMEM)],
        out_specs=pl.BlockSpec(memory_space=pltpu.VMEM),
        out_shape=jax.ShapeDtypeStruct((n_heads, NA_T2, GRID_W, LANES), F32),
        compiler_params=pltpu.CompilerParams(vmem_limit_bytes=V7X_VMEM_LIMIT),
    )(rows, dep)


def rpb_reduce(dt2, name):
    n_heads = dt2.shape[0]
    flip = jnp.asarray(np.eye(GRID_W)[::-1], BF16)

    def body(d_ref, j_ref, o_ref):
        jm = j_ref[...]
        for h in range(n_heads):
            for e in range(NA_T2):
                dv = d_ref[h, e]
                hi = dv.astype(BF16)
                mid = (dv - hi.astype(F32)).astype(BF16)
                lo = (dv - hi.astype(F32) - mid.astype(F32)).astype(BF16)
                rev = _dot(jm, hi) + _dot(jm, mid) + _dot(jm, lo)
                back = pltpu.roll(rev, LANES + (NA_COLS - 1) - (GRID_W - 1), 1, stride=1, stride_axis=0)
                o_ref[h, e:e + 1, :] = jnp.sum(back, axis=0, keepdims=True)

    return pl.pallas_call(
        body, name=name,
        in_specs=[pl.BlockSpec(memory_space=pltpu.VMEM)] * 2,
        out_specs=pl.BlockSpec(memory_space=pltpu.VMEM),
        out_shape=jax.ShapeDtypeStruct((n_heads, NA_T2, LANES), F32),
        compiler_params=pltpu.CompilerParams(vmem_limit_bytes=V7X_VMEM_LIMIT),
    )(dt2, flip)


NA_TQ = 4
NA_TK = NA_TQ + NA_ROWS
NA_KCH = NA_TK // 2


def _na_tile_geometry(t, rows):
    r = t * NA_TQ
    kbase = jnp.clip(r - NA_ROWS // 2, 0, rows - NA_TK)
    starts = [jnp.clip(r + a - NA_ROWS // 2, 0, rows - NA_ROWS) for a in range(NA_TQ)]
    return r, kbase, starts


def _na_tile_mask(kbase, starts, low, inwin):
    half = jnp.where(low, 0, 1)
    cols = []
    for c in range(NA_KCH):
        krow = kbase + 2 * c + half
        cols.append(jnp.concatenate(
            [jnp.where(inwin & (krow >= st) & (krow < st + NA_ROWS), 0.0, NEG) for st in starts], axis=0))
    return jnp.concatenate(cols, axis=1)


def _na_tile_index(r, kbase, a, c):
    return jnp.clip(kbase + 2 * c - (r + a) + NA_ROWS, 0, NA_T2 - 1)


def _na_tile_scores(q, k, t2_ref, hh, r, kbase, madd):
    bias = jnp.concatenate(
        [jnp.concatenate([t2_ref[hh, _na_tile_index(r, kbase, a, c)] for a in range(NA_TQ)], axis=0)
         for c in range(NA_KCH)], axis=1)
    return _dotg(q, k, NT) + bias + madd


def _softmax_rows(sc):
    e = jnp.exp(sc - jnp.max(sc, axis=1, keepdims=True))
    return e * (1.0 / jnp.sum(e, axis=1, keepdims=True))


def na_fwd(qa, ka, zq, t2, name):
    s = qa.shape[0]
    rows = s // GRID_W
    n_pairs = NA_WIDTH // LANES
    v_blk0 = (2 * NA_WIDTH) // LANES

    assert rows % NA_TQ == 0 and rows >= NA_TK
    tq, tk = NA_TQ * GRID_W, NA_TK * GRID_W

    def body(q_ref, k_ref, v_ref, t2_ref, o_ref, s_scr, p_scr):
        _, low, inwin = _na_iotas()

        def tile(t, carry):
            r, kbase, starts = _na_tile_geometry(t, rows)
            madd = _na_tile_mask(kbase, starts, low, inwin)
            qr = pl.ds(pl.multiple_of(r * GRID_W, tq), tq)
            kr = pl.ds(pl.multiple_of(kbase * GRID_W, tq), tk)
            for hh in range(2):
                lanes = slice(HEAD_DIM * hh, HEAD_DIM * (hh + 1))
                s_scr[tq * hh:tq * (hh + 1), :] = _na_tile_scores(q_ref[qr, lanes], k_ref[kr, lanes], t2_ref, hh, r,
                                                                  kbase, madd)
            p_scr[...] = _softmax_rows(s_scr[...]).astype(BF16)
            for hh in range(2):
                lanes = slice(HEAD_DIM * hh, HEAD_DIM * (hh + 1))
                o_ref[qr, lanes] = _dot(p_scr[tq * hh:tq * (hh + 1), :], v_ref[kr, lanes]).astype(BF16)
            return carry

        lax.fori_loop(0, rows // NA_TQ, tile, 0)

    col = lambda off: pl.BlockSpec((s, LANES), lambda p, _o=off: (0, _o + p))
    return pl.pallas_call(
        body, name=name, grid=(n_pairs,),
        in_specs=[col(0), col(0), col(v_blk0),
                  pl.BlockSpec((2, NA_T2, GRID_W, LANES), lambda p: (p, 0, 0, 0))],
        out_specs=col(0),
        out_shape=jax.ShapeDtypeStruct((s, NA_WIDTH), BF16),
        scratch_shapes=[pltpu.VMEM((2 * tq, tk), F32), pltpu.VMEM((2 * tq, tk), BF16)],
        compiler_params=_params(),
    )(qa, ka, zq, t2)


def na_bwd(qa, ka, zq, t2, o_na, do_na, name):
    s = qa.shape[0]
    rows = s // GRID_W
    n_pairs = NA_WIDTH // LANES
    v_blk0 = (2 * NA_WIDTH) // LANES

    tq, tk = NA_TQ * GRID_W, NA_TK * GRID_W

    def body(q_ref, k_ref, v_ref, t2_ref, o_ref, do_ref, dq_ref, dk_ref, dv_ref, dt2_ref):
        _, low, inwin = _na_iotas()
        dk_ref[...] = jnp.zeros(dk_ref.shape, F32)
        dv_ref[...] = jnp.zeros(dv_ref.shape, F32)
        dt2_ref[...] = jnp.zeros(dt2_ref.shape, F32)

        def tile(t, carry):
            r, kbase, starts = _na_tile_geometry(t, rows)
            madd = _na_tile_mask(kbase, starts, low, inwin)
            qr = pl.ds(pl.multiple_of(r * GRID_W, tq), tq)
            kr = pl.ds(pl.multiple_of(kbase * GRID_W, tq), tk)
            for hh in range(2):
                lanes = slice(HEAD_DIM * hh, HEAD_DIM * (hh + 1))
                q, k, v = q_ref[qr, lanes], k_ref[kr, lanes], v_ref[kr, lanes]
                p = _softmax_rows(_na_tile_scores(q, k, t2_ref, hh, r, kbase, madd))
                do = do_ref[qr, lanes]
                delta = jnp.sum(do.astype(F32) * o_ref[qr, lanes].astype(F32), axis=1, keepdims=True)
                ds = p * (_dotg(do, v, NT) - delta)
                shared = {}
                for a in range(NA_TQ):
                    for c in range(NA_KCH):
                        shared.setdefault(2 * c - a, []).append(
                            ds[GRID_W * a:GRID_W * (a + 1), LANES * c:LANES * (c + 1)])
                for offset, parts in shared.items():
                    e = jnp.clip(offset + kbase - r + NA_ROWS, 0, NA_T2 - 1)
                    dt2_ref[hh, e] = dt2_ref[hh, e] + functools.reduce(jnp.add, parts)
                dsb = ds.astype(BF16)
                dq_ref[qr, lanes] = _dot(dsb, k)
                dk_ref[kr, lanes] = dk_ref[kr, lanes] + _dotg(dsb, q, TN)
                dv_ref[kr, lanes] = dv_ref[kr, lanes] + _dotg(p.astype(BF16), do, TN)
            return carry

        lax.fori_loop(0, rows // NA_TQ, tile, 0)

    col = lambda off: pl.BlockSpec((s, LANES), lambda p, _o=off: (0, _o + p))
    t2spec = pl.BlockSpec((2, NA_T2, GRID_W, LANES), lambda p: (p, 0, 0, 0))
    return pl.pallas_call(
        body, name=name, grid=(n_pairs,),
        in_specs=[col(0), col(0), col(v_blk0), t2spec, col(0), col(0)],
        out_specs=[col(0), col(0), col(0), t2spec],
        out_shape=[jax.ShapeDtypeStruct((s, NA_WIDTH), F32)] * 3 + [jax.ShapeDtypeStruct(t2.shape, F32)],
        compiler_params=_params(),
    )(qa, ka, zq, t2, o_na, do_na)


def _t5_bucket_map():
    rel = np.arange(3 * SW_BLOCK)[None, :] - SW_BLOCK - np.arange(SW_BLOCK)[:, None]
    nb = REL_BUCKETS // 2
    max_exact = nb // 2
    n = np.abs(rel)
    large = max_exact + (np.log(np.maximum(n, 1) / max_exact)
                         / np.log(REL_MAX_DIST / max_exact) * (nb - max_exact)).astype(np.int32)
    large = np.minimum(large, nb - 1)
    return ((rel > 0) * nb + np.where(n < max_exact, n, large)).astype(np.int32)


def t5_expand(table, bmap, dep, name):
    def body(tab_ref, bm_ref, dep_ref, o_ref):
        bm = bm_ref[...]
        for h in range(SW_HEADS):
            t = jnp.zeros(bm.shape, F32)
            for b in range(REL_BUCKETS):
                t = jnp.where(bm == b, tab_ref[b, h], t)
            o_ref[h] = t

    return pl.pallas_call(
        body, name=name,
        in_specs=[pl.BlockSpec(memory_space=pltpu.SMEM), pl.BlockSpec(memory_space=pltpu.VMEM),
                  pl.BlockSpec(memory_space=pltpu.VMEM)],
        out_specs=pl.BlockSpec(memory_space=pltpu.VMEM),
        out_shape=jax.ShapeDtypeStruct((SW_HEADS,) + bmap.shape, F32),
        compiler_params=pltpu.CompilerParams(vmem_limit_bytes=V7X_VMEM_LIMIT),
    )(table, bmap, dep)


def t5_reduce(dbias_list, bmap, name):
    n = len(dbias_list)

    def body(*refs):
        d_refs, bm_ref, o_ref = refs[:n], refs[n], refs[n + 1]
        bm = bm_ref[...]
        for h in range(SW_HEADS):
            dv = d_refs[0][h]
            for other in d_refs[1:]:
                dv = dv + other[h]
            rows = [jnp.sum(jnp.where(bm == b, dv, 0.0), axis=0, keepdims=True) for b in range(REL_BUCKETS)]
            r = jnp.concatenate(rows, axis=0)
            o_ref[h] = jnp.broadcast_to(jnp.sum(r, axis=1, keepdims=True), (REL_BUCKETS, LANES))

    return pl.pallas_call(
        body, name=name,
        in_specs=[pl.BlockSpec(memory_space=pltpu.VMEM)] * (n + 1),
        out_specs=pl.BlockSpec(memory_space=pltpu.VMEM),
        out_shape=jax.ShapeDtypeStruct((SW_HEADS, REL_BUCKETS, LANES), F32),
        compiler_params=pltpu.CompilerParams(vmem_limit_bytes=V7X_VMEM_LIMIT),
    )(*dbias_list, bmap)


def _sw_mask_iotas():
    a = lax.broadcasted_iota(jnp.int32, (SW_BLOCK, 3 * SW_BLOCK), 0)
    j = lax.broadcasted_iota(jnp.int32, (SW_BLOCK, 3 * SW_BLOCK), 1)
    inwin = jnp.abs(j - SW_BLOCK - a) <= SW_BLOCK
    return j, inwin


SW_STACK = SW_HEADS * SW_BLOCK


def _sw_softmax(sc, sk):
    m = jnp.maximum(jnp.max(sc, axis=1, keepdims=True), sk)
    e = jnp.exp(sc - m)
    es = jnp.exp(sk - m)
    inv = 1.0 / (jnp.sum(e, axis=1, keepdims=True) + es)
    return e * inv, es * inv


def _sw_prologue(k_ref, v_ref, kp, vp, sink_ref, s):
    pad = s + 2 * SW_BLOCK
    zeros = jnp.zeros((SW_BLOCK, SW_KV_WIDTH), BF16)
    kp[0:SW_BLOCK, :] = zeros
    vp[0:SW_BLOCK, :] = zeros
    kp[SW_BLOCK + s:pad, :] = zeros
    vp[SW_BLOCK + s:pad, :] = zeros
    kp[SW_BLOCK:SW_BLOCK + s, :] = k_ref[...]
    vp[SW_BLOCK:SW_BLOCK + s, :] = v_ref[...]
    return jnp.concatenate([jnp.full((SW_BLOCK, 1), sink_ref[h], F32) for h in range(SW_HEADS)], axis=0)


def sw_fwd(qs, ks, zq, t5b, sink, dep, name):
    s = qs.shape[0]
    nb = s // SW_BLOCK
    v_blk = (3 * NA_WIDTH + SW_Q_WIDTH + SW_KV_WIDTH) // LANES
    pad = s + 2 * SW_BLOCK

    def body(q_ref, k_ref, v_ref, b_ref, sink_ref, dep_ref, o_ref, kp, vp, s_scr, p_scr):
        sink_col = _sw_prologue(k_ref, v_ref, kp, vp, sink_ref, s)
        j, inwin = _sw_mask_iotas()

        def blk(n, carry):
            kpos = n * SW_BLOCK - SW_BLOCK + j
            madd = jnp.where(inwin & (kpos >= 0) & (kpos < s), 0.0, NEG)
            q0 = pl.multiple_of(n * SW_BLOCK, SW_BLOCK)
            qr, kr = pl.ds(q0, SW_BLOCK), pl.ds(q0, 3 * SW_BLOCK)
            for h in range(SW_HEADS):
                g = h // SW_REP
                s_scr[SW_BLOCK * h:SW_BLOCK * (h + 1), :] = _dotg(
                    q_ref[qr, HEAD_DIM * h:HEAD_DIM * (h + 1)], kp[kr, HEAD_DIM * g:HEAD_DIM * (g + 1)], NT) + madd
            p, _ = _sw_softmax(s_scr[...] + b_ref[...], sink_col)
            p_scr[...] = p.astype(BF16)
            for h in range(SW_HEADS):
                g = h // SW_REP
                o_ref[qr, HEAD_DIM * h:HEAD_DIM * (h + 1)] = _dot(
                    p_scr[SW_BLOCK * h:SW_BLOCK * (h + 1), :], vp[kr, HEAD_DIM * g:HEAD_DIM * (g + 1)]).astype(BF16)
            return carry

        lax.fori_loop(0, nb, blk, 0)

    return pl.pallas_call(
        body, name=name, grid=(1,),
        in_specs=[_full((s, SW_Q_WIDTH)), _full((s, SW_KV_WIDTH)),
                  pl.BlockSpec((s, SW_KV_WIDTH), lambda i: (0, v_blk)),
                  _full((SW_STACK, 3 * SW_BLOCK)), pl.BlockSpec(memory_space=pltpu.SMEM),
                  _full(dep.shape)],
        out_specs=_full((s, SW_Q_WIDTH)),
        out_shape=jax.ShapeDtypeStruct((s, SW_Q_WIDTH), BF16),
        scratch_shapes=[pltpu.VMEM((pad, SW_KV_WIDTH), BF16), pltpu.VMEM((pad, SW_KV_WIDTH), BF16),
                        pltpu.VMEM((SW_STACK, 3 * SW_BLOCK), F32), pltpu.VMEM((SW_STACK, 3 * SW_BLOCK), BF16)],
        compiler_params=_params(),
    )(qs, ks, zq, t5b, sink, dep)


def sw_bwd(qs, ks, zq, t5b, sink, o_sw, do_sw, name):
    s = qs.shape[0]
    nb = s // SW_BLOCK
    v_blk = (3 * NA_WIDTH + SW_Q_WIDTH + SW_KV_WIDTH) // LANES
    pad = s + 2 * SW_BLOCK

    def body(q_ref, k_ref, v_ref, b_ref, sink_ref, o_ref, do_ref,
             dq_ref, dk_ref, dv_ref, db_ref, dsk_ref, kp, vp, dkp, dvp, s_scr, dp_scr, ds_scr, p_scr):
        sink_col = _sw_prologue(k_ref, v_ref, kp, vp, sink_ref, s)
        dkp[...] = jnp.zeros(dkp.shape, F32)
        dvp[...] = jnp.zeros(dvp.shape, F32)
        db_ref[...] = jnp.zeros(db_ref.shape, F32)
        dsk_ref[...] = jnp.zeros(dsk_ref.shape, F32)
        j, inwin = _sw_mask_iotas()

        def blk(n, carry):
            kpos = n * SW_BLOCK - SW_BLOCK + j
            madd = jnp.where(inwin & (kpos >= 0) & (kpos < s), 0.0, NEG)
            q0 = pl.multiple_of(n * SW_BLOCK, SW_BLOCK)
            qr, kr = pl.ds(q0, SW_BLOCK), pl.ds(q0, 3 * SW_BLOCK)
            deltas = []
            for h in range(SW_HEADS):
                g = h // SW_REP
                hl, kl = slice(HEAD_DIM * h, HEAD_DIM * (h + 1)), slice(HEAD_DIM * g, HEAD_DIM * (g + 1))
                rows = slice(SW_BLOCK * h, SW_BLOCK * (h + 1))
                do = do_ref[qr, hl]
                s_scr[rows, :] = _dotg(q_ref[qr, hl], kp[kr, kl], NT) + madd
                dp_scr[rows, :] = _dotg(do, vp[kr, kl], NT)
                deltas.append(jnp.sum(do.astype(F32) * o_ref[qr, hl].astype(F32), axis=1, keepdims=True))
            delta = jnp.concatenate(deltas, axis=0)
            p, ps = _sw_softmax(s_scr[...] + b_ref[...], sink_col)
            ds = p * (dp_scr[...] - delta)
            db_ref[...] = db_ref[...] + ds
            dsk_ref[...] = dsk_ref[...] - jnp.broadcast_to(ps * delta, (SW_STACK, LANES))
            ds_scr[...] = ds.astype(BF16)
            p_scr[...] = p.astype(BF16)
            for g in range(SW_HEADS // SW_REP):
                kl = slice(HEAD_DIM * g, HEAD_DIM * (g + 1))
                k = kp[kr, kl]
                dkw = jnp.zeros((3 * SW_BLOCK, HEAD_DIM), F32)
                dvw = jnp.zeros((3 * SW_BLOCK, HEAD_DIM), F32)
                for r in range(SW_REP):
                    h = g * SW_REP + r
                    hl, rows = slice(HEAD_DIM * h, HEAD_DIM * (h + 1)), slice(SW_BLOCK * h, SW_BLOCK * (h + 1))
                    dsb = ds_scr[rows, :]
                    dq_ref[qr, hl] = _dot(dsb, k)
                    dkw = dkw + _dotg(dsb, q_ref[qr, hl], TN)
                    dvw = dvw + _dotg(p_scr[rows, :], do_ref[qr, hl], TN)
                dkp[kr, kl] = dkp[kr, kl] + dkw
                dvp[kr, kl] = dvp[kr, kl] + dvw
            return carry

        lax.fori_loop(0, nb, blk, 0)
        dk_ref[...] = dkp[SW_BLOCK:SW_BLOCK + s, :]
        dv_ref[...] = dvp[SW_BLOCK:SW_BLOCK + s, :]

    bias_spec = _full((SW_STACK, 3 * SW_BLOCK))
    return pl.pallas_call(
        body, name=name, grid=(1,),
        in_specs=[_full((s, SW_Q_WIDTH)), _full((s, SW_KV_WIDTH)),
                  pl.BlockSpec((s, SW_KV_WIDTH), lambda i: (0, v_blk)),
                  bias_spec, pl.BlockSpec(memory_space=pltpu.SMEM),
                  _full((s, SW_Q_WIDTH)), _full((s, SW_Q_WIDTH))],
        out_specs=[_full((s, SW_Q_WIDTH)), _full((s, SW_KV_WIDTH)), _full((s, SW_KV_WIDTH)), bias_spec,
                   _full((SW_STACK, LANES))],
        out_shape=[jax.ShapeDtypeStruct((s, SW_Q_WIDTH), F32), jax.ShapeDtypeStruct((s, SW_KV_WIDTH), F32),
                   jax.ShapeDtypeStruct((s, SW_KV_WIDTH), F32),
                   jax.ShapeDtypeStruct((SW_STACK, 3 * SW_BLOCK), F32),
                   jax.ShapeDtypeStruct((SW_STACK, LANES), F32)],
        scratch_shapes=[pltpu.VMEM((pad, SW_KV_WIDTH), BF16), pltpu.VMEM((pad, SW_KV_WIDTH), BF16),
                        pltpu.VMEM((pad, SW_KV_WIDTH), F32), pltpu.VMEM((pad, SW_KV_WIDTH), F32),
                        pltpu.VMEM((SW_STACK, 3 * SW_BLOCK), F32), pltpu.VMEM((SW_STACK, 3 * SW_BLOCK), F32),
                        pltpu.VMEM((SW_STACK, 3 * SW_BLOCK), BF16), pltpu.VMEM((SW_STACK, 3 * SW_BLOCK), BF16)],
        compiler_params=_params(),
    )(qs, ks, zq, t5b, sink, o_sw, do_sw)


def merge_out(x, o_na, o_sw, gt, wbna_t, wbsw_t, wout, name):
    s, d = x.shape
    tm = _row_tile(s)

    def body(x_ref, ona_ref, osw_ref, gt_ref, wna_ref, wsw_ref, wo_ref, xo_ref, ana_ref, asw_ref, mg_ref):
        a_na = _dotg(ona_ref[...], wna_ref[...], NT)
        a_sw = _dotg(osw_ref[...], wsw_ref[...], NT)
        g_na, g_sw = gt_ref[:, 0:d].astype(F32), gt_ref[:, d:2 * d].astype(F32)
        ana_ref[...] = (a_na * g_na * (1.0 - g_na)).astype(BF16)
        asw_ref[...] = (a_sw * g_sw * (1.0 - g_sw)).astype(BF16)
        merged = (g_na * a_na + g_sw * a_sw).astype(BF16)
        mg_ref[...] = merged
        xo_ref[...] = x_ref[...] + _dot(merged, wo_ref[...])

    return pl.pallas_call(
        body, name=name, grid=(s // tm,),
        in_specs=[_rows(tm, d), _rows(tm, 512), _rows(tm, 512), _rows(tm, 2 * d),
                  _mat(*wbna_t), _mat(*wbsw_t), _mat(*wout)],
        out_specs=[_rows(tm, d)] * 4,
        out_shape=[jax.ShapeDtypeStruct((s, d), F32)] + [jax.ShapeDtypeStruct((s, d), BF16)] * 3,
        compiler_params=_params(),
    )(x, o_na, o_sw, gt, wbna_t[0], wbsw_t[0], wout[0])


def mix_bwd_out(dx, gt, a_na, a_sw, wbna_t, wbsw_t, wout, dep, name):
    s, d = dx.shape
    tm = _row_tile(s)

    def body(dx_ref, gt_ref, ana_ref, asw_ref, wna_ref, wsw_ref, wo_ref, dep_ref,
             dxb_ref, dzg_ref, dana_ref, dasw_ref, dona_ref, dosw_ref, dbg_ref):
        @pl.when(pl.program_id(0) == 0)
        def _():
            dbg_ref[...] = jnp.zeros(dbg_ref.shape, F32)

        dxb = dx_ref[...].astype(BF16)
        dxb_ref[...] = dxb
        dm = _dotg(dxb, wo_ref[...], NT)
        for i, (a_ref, da_ref, w_ref, do_ref) in enumerate(
                [(ana_ref, dana_ref, wna_ref, dona_ref), (asw_ref, dasw_ref, wsw_ref, dosw_ref)]):
            gi = gt_ref[:, i * d:(i + 1) * d].astype(F32)
            da = (dm * gi).astype(BF16)
            da_ref[...] = da
            do_ref[...] = _dot(da, w_ref[...]).astype(BF16)
            dzg = dm * a_ref[...].astype(F32)
            dzg_ref[:, i * d:(i + 1) * d] = dzg.astype(BF16)
            dbg_ref[:, i * d:(i + 1) * d] = dbg_ref[:, i * d:(i + 1) * d] + jnp.sum(dzg, axis=0, keepdims=True)

    return pl.pallas_call(
        body, name=name, grid=(s // tm,),
        in_specs=[_rows(tm, d), _rows(tm, 2 * d), _rows(tm, d), _rows(tm, d),
                  _mat(*wbna_t), _mat(*wbsw_t), _mat(*wout), _full(dep.shape)],
        out_specs=[_rows(tm, d), _rows(tm, 2 * d), _rows(tm, d), _rows(tm, d), _rows(tm, 512), _rows(tm, 512),
                   _full((1, 2 * d))],
        out_shape=[jax.ShapeDtypeStruct((s, d), BF16), jax.ShapeDtypeStruct((s, 2 * d), BF16),
                   jax.ShapeDtypeStruct((s, d), BF16), jax.ShapeDtypeStruct((s, d), BF16),
                   jax.ShapeDtypeStruct((s, 512), BF16), jax.ShapeDtypeStruct((s, 512), BF16),
                   jax.ShapeDtypeStruct((1, 2 * d), F32)],
        compiler_params=_params(),
    )(dx, gt, a_na, a_sw, wbna_t[0], wbsw_t[0], wout[0], dep)


def qk_norm_bwd(dqa, dka, dva, dqs, dks, dvs, zq, dzg, gq_na, gk_na, gq_sw, gk_sw, bd, name):
    s = zq.shape[0]
    d2 = dzg.shape[1]
    n_in = QKV_WIDTH + d2
    tm = _row_tile(s)

    def body(dqa_ref, dka_ref, dva_ref, dqs_ref, dks_ref, dvs_ref, zq_ref, dzg_ref,
             gqa_ref, gka_ref, gqs_ref, gks_ref, bd_ref, dz_ref, dgqa_ref, dgka_ref, dgqs_ref, dgks_ref):
        @pl.when(pl.program_id(0) == 0)
        def _():
            for r in (dgqa_ref, dgka_ref, dgqs_ref, dgks_ref):
                r[...] = jnp.zeros(r.shape, F32)

        bd512 = bd_ref[...]
        bd128 = bd_ref[0:SW_KV_WIDTH, 0:SW_KV_WIDTH]

        def one(c0, c1, dy_ref, g_ref, dg_ref, bdm, scale):
            z = zq_ref[:, c0:c1].astype(F32)
            r = lax.rsqrt(_group_mean(z * z, bdm) + EPS)
            zh = z * r
            dy = dy_ref[...] * scale
            dyg = dy * g_ref[...]
            dz = r * (dyg - zh * _group_mean(dyg * zh, bdm))
            dz_ref[:, c0:c1] = dz.astype(BF16)
            dg_ref[...] = dg_ref[...] + jnp.sum(dy * zh, axis=0, keepdims=True)

        one(0, 512, dqa_ref, gqa_ref, dgqa_ref, bd512, SCALE)
        one(512, 1024, dka_ref, gka_ref, dgka_ref, bd512, 1.0)
        dz_ref[:, 1024:1536] = dva_ref[...].astype(BF16)
        one(1536, 2048, dqs_ref, gqs_ref, dgqs_ref, bd512, SCALE)
        one(2048, 2176, dks_ref, gks_ref, dgks_ref, bd128, 1.0)
        dz_ref[:, 2176:2304] = dvs_ref[...].astype(BF16)
        dz_ref[:, QKV_WIDTH:n_in] = dzg_ref[...]

    return pl.pallas_call(
        body, name=name, grid=(s // tm,),
        in_specs=[_rows(tm, 512), _rows(tm, 512), _rows(tm, 512), _rows(tm, 512), _rows(tm, 128), _rows(tm, 128),
                  _rows(tm, QKV_WIDTH), _rows(tm, d2),
                  _full((1, 512)), _full((1, 512)), _full((1, 512)), _full((1, 128)), _full((MXU_TILE, MXU_TILE))],
        out_specs=[_rows(tm, n_in), _full((1, 512)), _full((1, 512)), _full((1, 512)), _full((1, 128))],
        out_shape=[jax.ShapeDtypeStruct((s, n_in), BF16)] + [jax.ShapeDtypeStruct((1, 512), F32)] * 3
                  + [jax.ShapeDtypeStruct((1, 128), F32)],
        compiler_params=_params(),
    )(dqa, dka, dva, dqs, dks, dvs, zq, dzg, gq_na, gk_na, gq_sw, gk_sw, bd)


def ffn_bwd_act(dx, wd, hg, hu, name):
    s, d = dx.shape
    f = wd[0].shape[1]
    tm = _row_tile(s)
    fc = _col_chunk(f)

    def body(dx_ref, w_ref, hg_ref, hu_ref, dxb_ref, dhg_ref, dhu_ref):
        dxv = dx_ref[...]
        dxb_ref[...] = dxv.astype(BF16)
        half = (0.5 * dxv).astype(BF16)
        for c0 in range(0, f, fc):
            dact = _dotg(half, w_ref[c0:c0 + fc, :], NT)
            dhu_ref[:, c0:c0 + fc] = (dact * hu_ref[:, c0:c0 + fc].astype(F32)).astype(BF16)
            dhg_ref[:, c0:c0 + fc] = (dact * hg_ref[:, c0:c0 + fc].astype(F32)).astype(BF16)

    return pl.pallas_call(
        body, name=name, grid=(s // tm,),
        in_specs=[_rows(tm, d), _mat(*wd), _rows(tm, f), _rows(tm, f)],
        out_specs=[_rows(tm, d), _rows(tm, f), _rows(tm, f)],
        out_shape=[jax.ShapeDtypeStruct((s, d), BF16), jax.ShapeDtypeStruct((s, f), BF16),
                   jax.ShapeDtypeStruct((s, f), BF16)],
        compiler_params=_params(),
    )(dx, wd[0], hg, hu)


def proj_bwd_norm(acts, weights, x, gain, dx, dep, name):
    s, d = x.shape
    tm = min(_row_tile(s), 256)
    n = len(acts)

    def body(*refs):
        a_refs, w_refs = refs[:n], refs[n:2 * n]
        x_ref, g_ref, dx_ref, _, o_ref, dg_ref = refs[2 * n:]

        @pl.when(pl.program_id(0) == 0)
        def _():
            dg_ref[...] = jnp.zeros(dg_ref.shape, F32)

        dxn = _dot(a_refs[0][...], w_refs[0][...])
        for a_ref, w_ref in zip(a_refs[1:], w_refs[1:]):
            dxn = dxn + _dot(a_ref[...], w_ref[...])
        xv = x_ref[...]
        r = _rstd(xv)
        xh = xv * r
        dxh = dxn * g_ref[...]
        o_ref[...] = dx_ref[...] + r * (dxh - xh * jnp.mean(dxh * xh, axis=-1, keepdims=True))
        dg_ref[...] = dg_ref[...] + jnp.sum(dxn * xh, axis=0, keepdims=True)

    return pl.pallas_call(
        body, name=name, grid=(s // tm,),
        in_specs=[_rows(tm, a.shape[1]) for a in acts] + [_mat(*w) for w in weights]
                 + [_rows(tm, d), _full((1, d)), _rows(tm, d), _full(dep.shape)],
        out_specs=[_rows(tm, d), _full((1, d))],
        out_shape=[jax.ShapeDtypeStruct((s, d), F32), jax.ShapeDtypeStruct((1, d), F32)],
        compiler_params=_params(),
    )(*acts, *[w[0] for w in weights], x, gain, dx, dep)


def tn_matmul(products, name):
    s, n = products[0][0].shape
    tn = _tn_tile(n) if len(products) == 1 else _col_chunk(n)
    rhs = []
    for _, b, _ in products:
        if not any(b is seen for seen in rhs):
            rhs.append(b)
    which = [next(i for i, seen in enumerate(rhs) if b is seen) for _, b, _ in products]
    npr, nr = len(products), len(rhs)

    def body(*refs):
        a_refs, b_refs, o_refs = refs[:npr], refs[npr:npr + nr], refs[npr + nr:]
        for i, (_, _, scale) in enumerate(products):
            o_refs[i][...] = (scale * _dotg(a_refs[i][...], b_refs[which[i]][...], TN)).astype(BF16)

    return pl.pallas_call(
        body, name=name, grid=(n // tn,),
        in_specs=[pl.BlockSpec((s, tn), lambda i: (0, i))] * npr
                 + [pl.BlockSpec(b.shape, lambda i: (0, 0), pipeline_mode=ONCE) for b in rhs],
        out_specs=[pl.BlockSpec((tn, b.shape[1]), lambda i: (i, 0)) for _, b, _ in products],
        out_shape=[jax.ShapeDtypeStruct((n, b.shape[1]), BF16) for _, b, _ in products],
        compiler_params=_params(),
    )(*[a for a, _, _ in products], *rhs)


def _mesh_pos():
    return lax.axis_index("x"), lax.axis_index("y"), lax.axis_index("c")


def _peers():
    x, y, c = _mesh_pos()
    peers = []
    for rel in range(1, N_DEV):
        peers.append((1 - x if rel & 4 else x, 1 - y if rel & 2 else y, 1 - c if rel & 1 else c))
    return 4 * x + 2 * y + c, peers


HBM_SPEC = pl.BlockSpec(memory_space=pltpu.HBM)
SEM_SPEC = pl.BlockSpec(memory_space=pltpu.SEMAPHORE)


def _split_call(body, name, thru, n_sems, extra=(), with_token=True):
    hbm = lambda t: pltpu.with_memory_space_constraint(t, pltpu.HBM)
    effect = pltpu.CompilerParams(has_side_effects=pltpu.SideEffectType.DATAFLOW_SIDE_EFFECTING)
    nt = len(thru)
    thru_shapes = [pltpu.HBM(t.shape, t.dtype) for t in thru]
    if with_token:
        (after,) = extra
        outs = pl.pallas_call(
            body, name=name, in_specs=[HBM_SPEC] * nt + [pl.BlockSpec(memory_space=pl.ANY)],
            out_specs=[SEM_SPEC] * len(n_sems) + [HBM_SPEC] * nt + [pl.BlockSpec(memory_space=pltpu.VMEM)],
            out_shape=[pltpu.SemaphoreType.DMA((k,)) for k in n_sems] + thru_shapes
                      + [jax.ShapeDtypeStruct((8, LANES), F32)],
            input_output_aliases={i: len(n_sems) + i for i in range(nt)}, compiler_params=effect,
        )(*[hbm(t) for t in thru], after)
        return outs[:len(n_sems)], outs[len(n_sems):-1], outs[-1]
    return pl.pallas_call(
        body, name=name,
        in_specs=[HBM_SPEC] * nt + [SEM_SPEC] * len(n_sems) + [pl.BlockSpec(memory_space=pl.ANY)],
        out_specs=[HBM_SPEC] * nt, out_shape=thru_shapes,
        input_output_aliases={i: i for i in range(nt)}, compiler_params=effect,
    )(*thru, *extra)


def _gather_targets():
    x, y, c = _mesh_pos()
    return 4 * x + 2 * y + c, [(x, y, 1 - c), (1 - x, y, c), (x, 1 - y, c), (1 - x, 1 - y, c)]


def gather_start(shards, after, name):
    n = len(shards)
    zones = [lax.empty((w.shape[0], N_DEV) + w.shape[1:], w.dtype) for w in shards]

    def body(*refs):
        ins, zs = refs[:n], refs[n:2 * n]
        send_sems, recv_sems, local_sems = refs[2 * n + 1:2 * n + 4]
        token = refs[-1]
        me, targets = _gather_targets()
        for a in range(n):
            pltpu.make_async_copy(ins[a], zs[a].at[:, me], local_sems.at[a]).start()
            for k, to in enumerate(targets):
                pltpu.make_async_remote_copy(
                    src_ref=ins[a], dst_ref=zs[a].at[:, me], send_sem=send_sems.at[4 * a + k],
                    recv_sem=recv_sems.at[4 * a + k], device_id=to, device_id_type=MESH).start()
        token[...] = jnp.zeros(token.shape, F32)

    sems, thru, token = _split_call(body, name, list(shards) + zones, (4 * n, 4 * n, n), extra=(after,))
    return (sems, thru, n), token


def gather_wait(started, after, name):
    sems, thru, n = started

    def body(*refs):
        zs = refs[n:2 * n]
        send_sems, recv_sems, local_sems = refs[2 * n:2 * n + 3]
        _, targets = _gather_targets()
        for a in range(n):
            for k, to in enumerate(targets):
                cp = pltpu.make_async_remote_copy(
                    src_ref=zs[a].at[:, 0], dst_ref=zs[a].at[:, 0], send_sem=send_sems.at[4 * a + k],
                    recv_sem=recv_sems.at[4 * a + k], device_id=to, device_id_type=MESH)
                cp.wait_send()
                cp.wait_recv()
            pltpu.make_async_copy(zs[a].at[:, 0], zs[a].at[:, 0], local_sems.at[a]).wait()

    return _split_call(body, name, thru, (4 * n, 4 * n, n), extra=(*sems, after), with_token=False)[n:]


def forward_start(zones, after, name):
    n = len(zones)

    def body(*refs):
        zs = refs[:n]
        send_sems, recv_sems = refs[n + 1:n + 3]
        token = refs[-1]
        x, y, c = _mesh_pos()
        for a in range(n):
            for j, chip in enumerate([(1 - x, y), (x, 1 - y), (1 - x, 1 - y)]):
                blk = zs[a].at[:, 4 * chip[0] + 2 * chip[1] + c]
                pltpu.make_async_remote_copy(
                    src_ref=blk, dst_ref=blk, send_sem=send_sems.at[3 * a + j], recv_sem=recv_sems.at[3 * a + j],
                    device_id=(x, y, 1 - c), device_id_type=MESH).start()
        token[...] = jnp.zeros(token.shape, F32)

    sems, thru, token = _split_call(body, name, list(zones), (3 * n, 3 * n), extra=(after,))
    return (sems, thru, n), token


def forward_wait(started, after, name):
    sems, thru, n = started

    def body(*refs):
        zs = refs[:n]
        send_sems, recv_sems = refs[n:n + 2]
        x, y, c = _mesh_pos()
        for a in range(n):
            for j in range(3):
                cp = pltpu.make_async_remote_copy(
                    src_ref=zs[a].at[:, 0], dst_ref=zs[a].at[:, 0], send_sem=send_sems.at[3 * a + j],
                    recv_sem=recv_sems.at[3 * a + j], device_id=(x, y, 1 - c), device_id_type=MESH)
                cp.wait_send()
                cp.wait_recv()

    return _split_call(body, name, thru, (3 * n, 3 * n), extra=(*sems, after), with_token=False)


def scatter_start(groups, name):
    n = len(groups)
    flat = [g for grp in groups for g in grp]
    nf = len(flat)
    offs = np.cumsum([0] + [len(grp) for grp in groups])
    lands = [lax.empty((N_DEV, len(grp)) + grp[0].shape[1:], grp[0].dtype) for grp in groups]

    def body(*refs):
        ins, zones = refs[:nf], refs[nf:nf + n]
        send_sems, recv_sems, local_sems = refs[nf + n:nf + n + 3]
        token = refs[-1]
        me, peers = _peers()
        for a in range(n):
            for w in range(len(groups[a])):
                pltpu.make_async_copy(ins[offs[a] + w].at[me], zones[a].at[me, w], local_sems.at[a]).start()
        for k, peer in enumerate(peers):
            p_id = 4 * peer[0] + 2 * peer[1] + peer[2]
            for a in range(n):
                for w in range(len(groups[a])):
                    pltpu.make_async_remote_copy(
                        src_ref=ins[offs[a] + w].at[p_id], dst_ref=zones[a].at[me, w],
                        send_sem=send_sems.at[7 * a + k], recv_sem=recv_sems.at[7 * a + k],
                        device_id=peer, device_id_type=MESH).start()
        token[...] = jnp.zeros(token.shape, F32)

    hbm = lambda t: pltpu.with_memory_space_constraint(t, pltpu.HBM)
    outs = pl.pallas_call(
        body, name=name,
        in_specs=[HBM_SPEC] * (nf + n),
        out_specs=[SEM_SPEC] * 3 + [HBM_SPEC] * (nf + n) + [pl.BlockSpec(memory_space=pltpu.VMEM)],
        out_shape=[pltpu.SemaphoreType.DMA((7 * n,)), pltpu.SemaphoreType.DMA((7 * n,)), pltpu.SemaphoreType.DMA((n,))]
                  + [pltpu.HBM(t.shape, t.dtype) for t in flat + lands]
                  + [jax.ShapeDtypeStruct((8, LANES), F32)],
        input_output_aliases={i: 3 + i for i in range(nf + n)},
        compiler_params=pltpu.CompilerParams(has_side_effects=pltpu.SideEffectType.DATAFLOW_SIDE_EFFECTING),
    )(*[hbm(t) for t in flat], *[hbm(t) for t in lands])
    sems, thru, token = outs[:3], outs[3:3 + nf + n], outs[-1]
    return (sems, thru, [len(grp) for grp in groups]), token


def scatter_wait(started, after, name):
    (send_sems, recv_sems, local_sems), thru, sizes = started
    n = len(sizes)
    nf = len(thru) - n

    def body(*refs):
        zones = refs[nf:nf + n]
        s_sems, r_sems, l_sems = refs[nf + n:nf + n + 3]
        me, peers = _peers()
        for a in range(n):
            for k, peer in enumerate(peers):
                cp = pltpu.make_async_remote_copy(
                    src_ref=zones[a].at[0], dst_ref=zones[a].at[0],
                    send_sem=s_sems.at[7 * a + k], recv_sem=r_sems.at[7 * a + k], device_id=peer,
                    device_id_type=MESH)
                cp.wait_send()
                cp.wait_recv()
            pltpu.make_async_copy(zones[a].at[0], zones[a].at[0], l_sems.at[a]).wait()

    outs = pl.pallas_call(
        body, name=name,
        in_specs=[HBM_SPEC] * (nf + n) + [SEM_SPEC] * 3 + [pl.BlockSpec(memory_space=pl.ANY)],
        out_specs=[HBM_SPEC] * (nf + n),
        out_shape=[pltpu.HBM(t.shape, t.dtype) for t in thru],
        input_output_aliases={i: i for i in range(nf + n)},
        compiler_params=pltpu.CompilerParams(has_side_effects=pltpu.SideEffectType.DATAFLOW_SIDE_EFFECTING),
    )(*thru, send_sems, recv_sems, local_sems, after)
    return outs[nf:]


def pair_start(grads, after, name):
    nw = len(grads)
    land = lax.empty((4, nw) + grads[0].shape[1:], grads[0].dtype)

    def body(*refs):
        ins, zone = refs[:nw], refs[nw]
        send_sems, recv_sems = refs[nw + 2:nw + 4]
        x, y, c = _mesh_pos()
        for j in range(4):
            for w in range(nw):
                pltpu.make_async_remote_copy(
                    src_ref=ins[w].at[2 * j + (1 - c)], dst_ref=zone.at[j, w], send_sem=send_sems.at[0],
                    recv_sem=recv_sems.at[0], device_id=(x, y, 1 - c), device_id_type=MESH).start()
        refs[-1][...] = jnp.zeros(refs[-1].shape, F32)

    sems, thru, token = _split_call(body, name, list(grads) + [land], (1, 1), extra=(after,))
    return (sems, thru, nw), token


def pair_wait(started, after, name):
    sems, thru, nw = started

    def body(*refs):
        zone = refs[nw]
        send_sems, recv_sems = refs[nw + 1:nw + 3]
        x, y, c = _mesh_pos()
        cp = pltpu.make_async_remote_copy(src_ref=zone, dst_ref=zone, send_sem=send_sems.at[0],
                                          recv_sem=recv_sems.at[0], device_id=(x, y, 1 - c), device_id_type=MESH)
        cp.wait_send()
        cp.wait_recv()

    outs = _split_call(body, name, thru, (1, 1), extra=(*sems, after), with_token=False)
    return outs[:nw], outs[nw]


def pair_sum(grads, land, name):
    nw = len(grads)
    _, r, c_dim = grads[0].shape

    def body(*refs):
        g_refs, l_ref, o_ref = refs[:nw], refs[nw], refs[nw + 1]
        core = lax.axis_index("c")
        for w in range(nw):
            o_ref[0, w] = (g_refs[w][0, core].astype(F32) + l_ref[0, w].astype(F32)).astype(BF16)

    return pl.pallas_call(
        body, name=name, grid=(4,),
        in_specs=[pl.BlockSpec((1, 2, r, c_dim), lambda j: (j, 0, 0, 0))] * nw
                 + [pl.BlockSpec((1, nw, r, c_dim), lambda j: (j, 0, 0, 0))],
        out_specs=pl.BlockSpec((1, nw, r, c_dim), lambda j: (j, 0, 0, 0)),
        out_shape=jax.ShapeDtypeStruct((4, nw, r, c_dim), BF16),
        compiler_params=_params(),
    )(*[g.reshape(4, 2, r, c_dim) for g in grads], land)


def _other_chips():
    x, y, c = _mesh_pos()
    chips = []
    for rel in range(1, 4):
        px, py = (1 - x if rel & 2 else x), (1 - y if rel & 1 else y)
        chips.append((px, py, 2 * px + py))
    return 2 * x + y, c, chips


def chip_start(pair_sums, after, name):
    land = lax.empty(pair_sums.shape, pair_sums.dtype)

    def body(*refs):
        h_ref, zone = refs[0], refs[1]
        send_sems, recv_sems, local_sem = refs[3:6]
        mine, c, chips = _other_chips()
        pltpu.make_async_copy(h_ref.at[mine], zone.at[mine], local_sem.at[0]).start()
        for k, (px, py, j) in enumerate(chips):
            pltpu.make_async_remote_copy(
                src_ref=h_ref.at[j], dst_ref=zone.at[mine], send_sem=send_sems.at[k], recv_sem=recv_sems.at[k],
                device_id=(px, py, c), device_id_type=MESH).start()
        refs[-1][...] = jnp.zeros(refs[-1].shape, F32)

    sems, thru, token = _split_call(body, name, [pair_sums, land], (3, 3, 1), extra=(after,))
    return (sems, thru), token


def chip_wait(started, after, name):
    sems, thru = started

    def body(*refs):
        zone = refs[1]
        send_sems, recv_sems, local_sem = refs[2:5]
        _, c, chips = _other_chips()
        for k, (px, py, _) in enumerate(chips):
            cp = pltpu.make_async_remote_copy(
                src_ref=zone.at[0], dst_ref=zone.at[0], send_sem=send_sems.at[k], recv_sem=recv_sems.at[k],
                device_id=(px, py, c), device_id_type=MESH)
            cp.wait_send()
            cp.wait_recv()
        pltpu.make_async_copy(zone.at[0], zone.at[0], local_sem.at[0]).wait()

    return _split_call(body, name, thru, (3, 3, 1), extra=(*sems, after), with_token=False)[1]


def share_start(parts, after, name):
    n = len(parts)
    zones = [lax.empty((N_DEV,) + p.shape, p.dtype) for p in parts]

    def body(*refs):
        ins, zs = refs[:n], refs[n:2 * n]
        send_sems, recv_sems, local_sems = refs[2 * n + 1:2 * n + 4]
        me, peers = _peers()
        for i in range(n):
            pltpu.make_async_copy(ins[i], zs[i].at[me], local_sems.at[i]).start()
            for k, peer in enumerate(peers):
                pltpu.make_async_remote_copy(
                    src_ref=ins[i], dst_ref=zs[i].at[me], send_sem=send_sems.at[7 * i + k],
                    recv_sem=recv_sems.at[7 * i + k], device_id=peer, device_id_type=MESH).start()
        refs[-1][...] = jnp.zeros(refs[-1].shape, F32)

    sems, thru, token = _split_call(body, name, list(parts) + zones, (7 * n, 7 * n, n), extra=(after,))
    return (sems, thru, n), token


def share_wait(started, after, name):
    sems, thru, n = started

    def body(*refs):
        zs = refs[n:2 * n]
        send_sems, recv_sems, local_sems = refs[2 * n:2 * n + 3]
        _, peers = _peers()
        for i in range(n):
            for k, peer in enumerate(peers):
                cp = pltpu.make_async_remote_copy(
                    src_ref=zs[i].at[0], dst_ref=zs[i].at[0], send_sem=send_sems.at[7 * i + k],
                    recv_sem=recv_sems.at[7 * i + k], device_id=peer, device_id_type=MESH)
                cp.wait_send()
                cp.wait_recv()
            pltpu.make_async_copy(zs[i].at[0], zs[i].at[0], local_sems.at[i]).wait()

    return _split_call(body, name, thru, (7 * n, 7 * n, n), extra=(*sems, after), with_token=False)[n:]


def _adamw_math(w, g, m, v):
    m = ADAM_B1 * m + (1.0 - ADAM_B1) * g
    v = ADAM_B2 * v + (1.0 - ADAM_B2) * (g * g)
    m_hat = m / (1.0 - ADAM_B1 ** ADAM_STEP)
    v_hat = v / (1.0 - ADAM_B2 ** ADAM_STEP)
    delta = -ADAM_LR * (m_hat / (jnp.sqrt(v_hat) + ADAM_EPS) + ADAM_WD * w)
    return delta, m, v


ADAMW_BLOCK_BYTES = 24 * 1024 * 1024


def adamw_layer(zone, layer, items, after, name):
    n_src, nw, r, c = zone.shape
    depth = items[0][0].shape[0]
    prevs = [p if p is not None else tuple(lax.empty((depth, r, c), F32) for _ in range(4)) for _, _, _, p in items]
    row_bytes = 2 * nw * c * (2 * n_src + 4 * 7)
    tr = max(t for t in range(8, r + 1, 8) if r % t == 0 and t * row_bytes <= ADAMW_BLOCK_BYTES)

    def body(z_ref, *rest):
        ins, outs = rest[:3 * nw], rest[7 * nw + 1:]
        for i in range(nw):
            g = z_ref[0, i].astype(F32)
            for src in range(1, n_src):
                g = g + z_ref[src, i].astype(F32)
            g_ref, d_ref, mo_ref, vo_ref = outs[4 * i:4 * i + 4]
            w_ref, m_ref, v_ref = ins[3 * i:3 * i + 3]
            g_ref[...] = g
            d_ref[...], mo_ref[...], vo_ref[...] = _adamw_math(w_ref[...], g, m_ref[...], v_ref[...])

    rows = pl.BlockSpec((None, tr, c), lambda i: (layer, i, 0))
    anywhere = pl.BlockSpec(memory_space=pl.ANY)
    outs = pl.pallas_call(
        body, name=name, grid=(r // tr,),
        in_specs=[pl.BlockSpec((n_src, nw, tr, c), lambda i: (0, 0, i, 0))] + [rows] * (3 * nw)
                 + [anywhere] * (4 * nw + 1),
        out_specs=[rows] * (4 * nw),
        out_shape=[jax.ShapeDtypeStruct((depth, r, c), F32)] * (4 * nw),
        input_output_aliases={1 + 3 * nw + k: k for k in range(4 * nw)},
        compiler_params=_params(),
    )(zone, *[t for w, m, v, _ in items for t in (w, m, v)], *[t for p in prevs for t in p], after)
    return [tuple(outs[4 * i:4 * i + 4]) for i in range(nw)]


def adamw_small(ws, recvs, ms, vs, name):
    n = len(ws)

    def body(*refs):
        w_refs, r_refs, m_refs, v_refs = (refs[i * n:(i + 1) * n] for i in range(4))
        g_refs, d_refs, mo_refs, vo_refs = (refs[(4 + i) * n:(5 + i) * n] for i in range(4))
        for i in range(n):
            g = r_refs[i][0]
            for src in range(1, N_DEV):
                g = g + r_refs[i][src]
            g_refs[i][...] = g
            d_refs[i][...], mo_refs[i][...], vo_refs[i][...] = _adamw_math(w_refs[i][...], g, m_refs[i][...],
                                                                            v_refs[i][...])

    vm = pl.BlockSpec(memory_space=pltpu.VMEM)
    outs = pl.pallas_call(
        body, name=name, in_specs=[vm] * (4 * n), out_specs=[vm] * (4 * n),
        out_shape=[jax.ShapeDtypeStruct(w.shape, F32) for w in ws] * 4,
        compiler_params=pltpu.CompilerParams(vmem_limit_bytes=V7X_VMEM_LIMIT),
    )(*ws, *recvs, *ms, *vs)
    return [outs[i * n:(i + 1) * n] for i in range(4)]


SMALL_NAMES = ("ffn1_norm", "mix_norm", "ffn2_norm", "b_gate", "na_q_norm", "na_k_norm", "sw_q_norm", "sw_k_norm",
               "na_rpb", "sw_sink", "t5_rel_table")


def kernel(x, ffn1_norm, ffn1_w_gate, ffn1_w_up, ffn1_w_down, mix_norm, w_in, b_gate, na_q_norm, na_k_norm, na_rpb, sw_q_norm, sw_k_norm, sw_sink, t5_rel_table, w_branch_na, w_branch_sw, w_out, ffn2_norm, ffn2_w_gate, ffn2_w_up, ffn2_w_down, loss_target, m_ffn1_norm, m_ffn1_w_gate, m_ffn1_w_up, m_ffn1_w_down, m_mix_norm, m_w_in, m_b_gate, m_na_q_norm, m_na_k_norm, m_na_rpb, m_sw_q_norm, m_sw_k_norm, m_sw_sink, m_t5_rel_table, m_w_branch_na, m_w_branch_sw, m_w_out, m_ffn2_norm, m_ffn2_w_gate, m_ffn2_w_up, m_ffn2_w_down, v_ffn1_norm, v_ffn1_w_gate, v_ffn1_w_up, v_ffn1_w_down, v_mix_norm, v_w_in, v_b_gate, v_na_q_norm, v_na_k_norm, v_na_rpb, v_sw_q_norm, v_sw_k_norm, v_sw_sink, v_t5_rel_table, v_w_branch_na, v_w_branch_sw, v_w_out, v_ffn2_norm, v_ffn2_w_gate, v_ffn2_w_up, v_ffn2_w_down):
    weights = dict(ffn1_norm=ffn1_norm, ffn1_w_gate=ffn1_w_gate, ffn1_w_up=ffn1_w_up, ffn1_w_down=ffn1_w_down,
                   mix_norm=mix_norm, w_in=w_in, b_gate=b_gate, na_q_norm=na_q_norm, na_k_norm=na_k_norm,
                   na_rpb=na_rpb, sw_q_norm=sw_q_norm, sw_k_norm=sw_k_norm, sw_sink=sw_sink,
                   t5_rel_table=t5_rel_table, w_branch_na=w_branch_na, w_branch_sw=w_branch_sw, w_out=w_out,
                   ffn2_norm=ffn2_norm, ffn2_w_gate=ffn2_w_gate, ffn2_w_up=ffn2_w_up, ffn2_w_down=ffn2_w_down)
    mom_m = dict(ffn1_norm=m_ffn1_norm, ffn1_w_gate=m_ffn1_w_gate, ffn1_w_up=m_ffn1_w_up, ffn1_w_down=m_ffn1_w_down,
                 mix_norm=m_mix_norm, w_in=m_w_in, b_gate=m_b_gate, na_q_norm=m_na_q_norm, na_k_norm=m_na_k_norm,
                 na_rpb=m_na_rpb, sw_q_norm=m_sw_q_norm, sw_k_norm=m_sw_k_norm, sw_sink=m_sw_sink,
                 t5_rel_table=m_t5_rel_table, w_branch_na=m_w_branch_na, w_branch_sw=m_w_branch_sw, w_out=m_w_out,
                 ffn2_norm=m_ffn2_norm, ffn2_w_gate=m_ffn2_w_gate, ffn2_w_up=m_ffn2_w_up, ffn2_w_down=m_ffn2_w_down)
    mom_v = dict(ffn1_norm=v_ffn1_norm, ffn1_w_gate=v_ffn1_w_gate, ffn1_w_up=v_ffn1_w_up, ffn1_w_down=v_ffn1_w_down,
                 mix_norm=v_mix_norm, w_in=v_w_in, b_gate=v_b_gate, na_q_norm=v_na_q_norm, na_k_norm=v_na_k_norm,
                 na_rpb=v_na_rpb, sw_q_norm=v_sw_q_norm, sw_k_norm=v_sw_k_norm, sw_sink=v_sw_sink,
                 t5_rel_table=v_t5_rel_table, w_branch_na=v_w_branch_na, w_branch_sw=v_w_branch_sw, w_out=v_w_out,
                 ffn2_norm=v_ffn2_norm, ffn2_w_gate=v_ffn2_w_gate, ffn2_w_up=v_ffn2_w_up, ffn2_w_down=v_ffn2_w_down)
    order = list(weights)

    depth = ffn1_norm.shape[0]
    s, d = x.shape[1], x.shape[2]
    xs = x[0]
    tr = lambda w: jnp.swapaxes(w, -1, -2)

    merge = lambda t: t.reshape(t.shape[0], N_DEV * t.shape[2], t.shape[3])
    no_dep = jnp.zeros((8, LANES), F32)

    def shards_of(kind, l):
        stack = lambda *ws: jnp.stack(ws).astype(BF16)
        if kind == "ffn1":
            return [stack(tr(ffn1_w_gate[l]), tr(ffn1_w_up[l]), ffn1_w_down[l])]
        if kind == "win":
            return [stack(tr(w_in[l]))]
        return [stack(tr(ffn2_w_gate[l]), tr(ffn2_w_up[l]), ffn2_w_down[l]), stack(w_out[l]),
                stack(tr(w_branch_na[l]), tr(w_branch_sw[l]))]

    def start(kind, l, after):
        return gather_start(shards_of(kind, l), after, f"gather_{kind}_{l}")

    def arrive(started, kind, l, after):
        zones = gather_wait(started, after, f"gather_{kind}_{l}_wait")
        return forward_start(zones, no_dep, f"forward_{kind}_{l}")

    def finish(fwd, kind, l, after):
        return [merge(z) for z in forward_wait(fwd, after, f"forward_{kind}_{l}_wait")]

    bd = jnp.asarray(np.kron(np.eye(MXU_TILE // HEAD_DIM), np.full((HEAD_DIM, HEAD_DIM), 1.0 / HEAD_DIM)), BF16)
    bmap = jnp.asarray(_t5_bucket_map())
    tile8 = lambda g: jnp.tile(g, NA_WIDTH // HEAD_DIM).reshape(1, NA_WIDTH)
    tile2 = lambda g: jnp.tile(g, SW_KV_WIDTH // HEAD_DIM).reshape(1, SW_KV_WIDTH)

    st_first, tok = start("ffn1", 0, no_dep)
    t5b = t5_expand(t5_rel_table, bmap, tok, "t5_expand").reshape(SW_STACK, 3 * SW_BLOCK)
    t2_tables = [rpb_expand(_rpb_rows(na_rpb[l]), tok, f"rpb_expand_{l}") for l in range(depth)]
    tables_done = functools.reduce(jnp.add, [t[0, 0, 0:8, :] for t in t2_tables], t5b[0:8, 0:LANES])
    fwd, _ = arrive(st_first, "ffn1", 0, tables_done)
    st_win, dep = start("win", 0, t5b)
    (first,) = finish(fwd, "ffn1", 0, dep)

    saved = []
    layer_w = {0: dict(wg1=(first, 0), wu1=(first, 1), wd1=(first, 2))}
    cur = xs
    for l in range(depth):
        sv = {}
        lw = layer_w[l]
        sv["x0"] = cur
        cur, sv["xn1"], sv["hg1"], sv["hu1"], sv["act1"] = ffn_forward(
            cur, ffn1_norm[l][None], lw["wg1"], lw["wu1"], lw["wd1"], dep, f"ffn1_{l}")
        sv["x1"] = cur
        fwd, _ = arrive(st_win, "win", l, cur)
        st_rest, tok = start("rest", l, cur)
        (zb,) = finish(fwd, "win", l, tok)
        lw["win"] = (zb, 0)
        sv["gains"] = (tile8(na_q_norm[l]), tile8(na_k_norm[l]), tile8(sw_q_norm[l]), tile2(sw_k_norm[l]))
        sv["hn"], sv["zq"], sv["qa"], sv["ka"], sv["qs"], sv["ks"], sv["gt"] = mix_in(
            cur, mix_norm[l][None], lw["win"], b_gate[l][None], *sv["gains"], bd, f"mix_in_{l}")
        sv["t2"] = t2_tables[l]
        sv["o_na"] = na_fwd(sv["qa"], sv["ka"], sv["zq"], sv["t2"], f"na_fwd_{l}")
        dep = no_dep
        if l + 1 < depth:
            st_ffn1, dep = start("ffn1", l + 1, sv["o_na"])
        sv["o_sw"] = sw_fwd(sv["qs"], sv["ks"], sv["zq"], t5b, sw_sink[l], dep, f"sw_fwd_{l}")
        fwd, tok = arrive(st_rest, "rest", l, sv["o_sw"][0:8, 0:LANES] + sv["o_na"][0:8, 0:LANES])
        za, zc, zd = finish(fwd, "rest", l, tok)
        lw.update(wg2=(za, 0), wu2=(za, 1), wd2=(za, 2), wout=(zc, 0), wna=(zd, 0), wsw=(zd, 1))
        cur, sv["a_na"], sv["a_sw"], sv["merged"] = merge_out(
            cur, sv["o_na"], sv["o_sw"], sv["gt"], lw["wna"], lw["wsw"], lw["wout"], f"merge_out_{l}")
        sv["x2"] = cur
        dep = no_dep
        if l + 1 < depth:
            st_win, dep = start("win", l + 1, cur)
        sv["xn2"], sv["hg2"], sv["hu2"], sv["act2"] = ffn_forward(
            cur, ffn2_norm[l][None], lw["wg2"], lw["wu2"], None, dep, f"ffn2_up_{l}")
        dep = no_dep
        if l + 1 < depth:
            fwd, dep = arrive(st_ffn1, "ffn1", l + 1, sv["act2"])
        if l + 1 < depth:
            cur = ffn_down(cur, sv["act2"], lw["wd2"], dep, f"ffn2_down_{l}")
            (za,) = finish(fwd, "ffn1", l + 1, cur)
            layer_w[l + 1] = dict(wg1=(za, 0), wu1=(za, 1), wd1=(za, 2))
        else:
            dx, loss_acc = ffn_down(cur, sv["act2"], lw["wd2"], dep, f"ffn2_down_{l}", target=loss_target[0])
        dep = no_dep
        saved.append(sv)

    split = lambda t: t.reshape(N_DEV, t.shape[0] // N_DEV, t.shape[1])
    pending = {}
    last_key = "ffn1_0"
    two_level = {last_key}
    small = {k: [None] * depth for k in SMALL_NAMES if k != "t5_rel_table"}
    dbias_sw = []
    for l in reversed(range(depth)):
        sv = saved[l]
        lw = layer_w[l]
        wg1, wu1, wd1, wg2, wu2, wd2 = (lw[k] for k in ("wg1", "wu1", "wd1", "wg2", "wu2", "wd2"))
        win_t, wout_l, wna_t, wsw_t = lw["win"], lw["wout"], lw["wna"], lw["wsw"]
        blocks = ((2, "x2", "xn2", "hg2", "hu2", "act2", wg2, wu2, wd2, "ffn2_norm", 3),
                  (1, "x0", "xn1", "hg1", "hu1", "act1", wg1, wu1, wd1, "ffn1_norm", 0))

        def ffn_backward(dx, blk):
            tag, xk, xnk, hgk, huk, actk, wg, wu, wd, norm_name, slot = blk
            gains = weights[norm_name]
            dxb, dhg, dhu = ffn_bwd_act(dx, wd, sv[hgk], sv[huk], f"ffn{tag}_bwd_act_{l}")
            gwg, gwu, gwd = tn_matmul([(dhg, sv[xnk], 1.0), (dhu, sv[xnk], 1.0), (sv[actk], dxb, 0.5)],
                                      f"ffn{tag}_dw_{l}")
            key = f"ffn{tag}_{l}"
            blocks_of = [split(gwg), split(gwu), split(gwd)]
            if key in two_level:
                paired, token = pair_start(blocks_of, dxb, f"pair_{key}")
            else:
                pending[key], token = scatter_start([blocks_of], f"scatter_{key}")
            dx, dg = proj_bwd_norm([dhg, dhu], [wg, wu], sv[xk], gains[l][None], dx, token, f"ffn{tag}_bwd_x_{l}")
            token = no_dep
            if key in two_level:
                thru, land = pair_wait(paired, dx, f"pair_{key}_wait")
                pending[key], token = chip_start(pair_sum(thru, land, f"pair_sum_{key}"), dg, f"chips_{key}")
            small[norm_name][l] = dg[0]
            return dx, token

        dx, token = ffn_backward(dx, blocks[0])
        dxb, dzg, da_na, da_sw, do_na, do_sw, dbg = mix_bwd_out(
            dx, sv["gt"], sv["a_na"], sv["a_sw"], wna_t, wsw_t, wout_l, token, f"mix_bwd_out_{l}")
        small["b_gate"][l] = dbg[0]
        gwout, gwna, gwsw = tn_matmul([(sv["merged"], dxb, 1.0), (da_na, sv["o_na"], 1.0), (da_sw, sv["o_sw"], 1.0)],
                                      f"mix_dw_{l}")
        dqa, dka, dva, dt2 = na_bwd(sv["qa"], sv["ka"], sv["zq"], sv["t2"], sv["o_na"], do_na, f"na_bwd_{l}")
        dqs, dks, dvs, dbias, dsink = sw_bwd(sv["qs"], sv["ks"], sv["zq"], t5b, sw_sink[l], sv["o_sw"], do_sw,
                                             f"sw_bwd_{l}")
        dbias_sw.append(dbias.reshape(SW_HEADS, SW_BLOCK, 3 * SW_BLOCK))
        small["sw_sink"][l] = jnp.sum(dsink[:, 0].reshape(SW_HEADS, SW_BLOCK), axis=1)
        small["na_rpb"][l] = _rpb_from_rows(rpb_reduce(dt2, f"rpb_reduce_{l}"))
        dz, dgqa, dgka, dgqs, dgks = qk_norm_bwd(dqa, dka, dva, dqs, dks, dvs, sv["zq"], dzg, *sv["gains"], bd,
                                                 f"qk_norm_bwd_{l}")
        fold = lambda g: jnp.sum(g.reshape(-1, HEAD_DIM), axis=0)
        small["na_q_norm"][l], small["na_k_norm"][l] = fold(dgqa), fold(dgka)
        small["sw_q_norm"][l], small["sw_k_norm"][l] = fold(dgqs), fold(dgks)
        (gwin,) = tn_matmul([(dz, sv["hn"], 1.0)], f"dwin_{l}")
        pending[f"mix_{l}"], token = scatter_start([[split(gwout)], [split(gwna), split(gwsw)], [split(gwin)]],
                                                   f"scatter_mix_{l}")
        dx, dg = proj_bwd_norm([dz], [win_t], sv["x1"], mix_norm[l][None], dx, token, f"mix_bwd_x_{l}")
        small["mix_norm"][l] = dg[0]
        dx, tail = ffn_backward(dx, blocks[1])

    dtab = t5_reduce(dbias_sw, bmap, "t5_reduce")
    small_parts = {k: jnp.stack(v) for k, v in small.items()}
    small_parts["t5_rel_table"] = jnp.transpose(dtab[:, :, 0])

    grads, delta, new_m, new_v = {}, {}, {}, {}
    state = {}
    sharing, token = share_start([small_parts[k] for k in SMALL_NAMES] + [loss_acc], tail, "share_small")
    chain = [token]
    members = {"ffn": lambda t: [(f"ffn{t}_w_gate", 0, 0, True), (f"ffn{t}_w_up", 0, 1, True),
                                 (f"ffn{t}_w_down", 0, 2, False)],
               "mix": lambda t: [("w_out", 0, 0, False), ("w_branch_na", 1, 0, True), ("w_branch_sw", 1, 1, True),
                                 ("w_in", 2, 0, True)]}

    def collect(key):
        if key in two_level:
            zones = [chip_wait(pending[key], chain[0], f"wait_{key}")]
        else:
            zones = scatter_wait(pending[key], chain[0], f"wait_{key}")
        kind, l = key.split("_")
        group = members[kind[:3]](kind[3:])
        complete = all(f"{kind}_{j}" in done for j in range(depth) if j != int(l))
        for zi, zone in enumerate(zones):
            mine = sorted((wi, k, transposed) for k, z, wi, transposed in group if z == zi)
            views = [tr if transposed else (lambda t: t) for _, _, transposed in mine]
            items = [(view(weights[k]), view(mom_m[k]), view(mom_v[k]), state.get(k))
                     for (_, k, _), view in zip(mine, views)]
            results = adamw_layer(zone, int(l), items, chain[0], f"adamw_{key}_{zi}")
            chain[0] = results[-1][1]
            for (_, k, _), view, res in zip(mine, views, results):
                state[k] = res
                if complete:
                    grads[k], delta[k], new_m[k], new_v[k] = (view(t) for t in res)
        done.add(key)

    done = set()
    for key in pending:
        if key != last_key:
            collect(key)
    collect(last_key)
    *recvs, all_losses = share_wait(sharing, chain[0], "share_small_wait")
    loss = jnp.sum(all_losses) * (0.5 / d)
    results = adamw_small([weights[k] for k in SMALL_NAMES], recvs, [mom_m[k] for k in SMALL_NAMES],
                          [mom_v[k] for k in SMALL_NAMES], "adamw_small")
    for dst, outs in zip((grads, delta, new_m, new_v), results):
        dst.update(dict(zip(SMALL_NAMES, outs)))

    return (loss, dx[None], *[grads[k] for k in order], *[delta[k] for k in order],
            *[new_m[k] for k in order], *[new_v[k] for k in order])
```

```python
import functools
import math

import numpy as np
import jax
import jax.numpy as jnp
from jax import lax
from jax.experimental import pallas as pl
from jax.experimental.pallas import tpu as pltpu

F32 = jnp.float32
BF16 = jnp.bfloat16
MESH = pl.DeviceIdType.MESH

N_DEV = 8
EPS = 1e-6
NEG = -1e30
HEAD_DIM = 64
GRID_W = 64
NA_ROWS = 8
NA_COLS = 16
NA_WIDTH = 512
SW_Q_WIDTH = 512
SW_KV_WIDTH = 128
SW_BLOCK = 128
SW_HEADS = 8
SW_REP = 4
REL_BUCKETS = 32
REL_MAX_DIST = 128
QKV_WIDTH = 3 * NA_WIDTH + SW_Q_WIDTH + 2 * SW_KV_WIDTH
SCALE = 1.0 / math.sqrt(HEAD_DIM)

ADAM_LR = 0.001
ADAM_B1 = 0.9
ADAM_B2 = 0.999
ADAM_EPS = 1e-08
ADAM_WD = 0.01
ADAM_STEP = 10

V7X_VMEM_LIMIT = 56 * 1024 * 1024
LANES = 128
MXU_TILE = 256

NT = (((1,), (1,)), ((), ()))
TN = (((0,), (0,)), ((), ()))


def _params(n_grid=1):
    return pltpu.CompilerParams(dimension_semantics=("arbitrary",) * n_grid,
                                vmem_limit_bytes=V7X_VMEM_LIMIT)


def _row_tile(s):
    for t in (512, 256, 128, 64, 32, 16, 8):
        if s % t == 0:
            return t
    raise ValueError(s)


def _tn_tile(n):
    best = max(t for t in range(LANES, min(n, 2304) + 1, LANES) if n % t == 0) if n % LANES == 0 else n
    return best // 2 if best == n and n >= 1024 else best


ONCE = pl.Buffered(1)


def _col_chunk(n):
    return MXU_TILE if n % MXU_TILE == 0 else n


def _dot(a, b):
    return jnp.dot(a, b, preferred_element_type=F32)


def _dotg(a, b, dn):
    return lax.dot_general(a, b, dn, preferred_element_type=F32)


def _sigmoid(v):
    return 1.0 / (1.0 + jnp.exp(-v))


def _rstd(xv):
    return lax.rsqrt(jnp.mean(xv * xv, axis=-1, keepdims=True) + EPS)


def _full(shape):
    nd = len(shape)
    return pl.BlockSpec(shape, lambda i, _n=nd: (0,) * _n)


def _rows(tm, width):
    return pl.BlockSpec((tm, width), lambda i: (i, 0))


def _mat(stack, idx):
    return pl.BlockSpec((None,) + tuple(stack.shape[1:]), lambda i, _w=idx: (_w, 0, 0), pipeline_mode=ONCE)


def _group_mean(v, bd):
    w = bd.shape[0]
    if v.shape[1] > w:
        return jnp.concatenate([_group_mean(v[:, c0:c0 + w], bd) for c0 in range(0, v.shape[1], w)], axis=1)
    hi = v.astype(BF16)
    lo = (v - hi.astype(F32)).astype(BF16)
    return _dot(hi, bd) + _dot(lo, bd)


def _loss_tile(y, t_ref, dy_ref, acc_ref):
    tm, d = y.shape

    @pl.when(pl.program_id(0) == 0)
    def _():
        acc_ref[...] = jnp.zeros(acc_ref.shape, F32)

    err = y - t_ref[...]
    dy_ref[...] = err * (1.0 / d)
    part = jnp.sum((err * err).reshape(tm // 8, 8, d), axis=0)
    acc = part[:, 0:LANES]
    for c0 in range(LANES, d, LANES):
        acc = acc + part[:, c0:c0 + LANES]
    acc_ref[...] = acc_ref[...] + acc


def ffn_forward(x, gain, wg_t, wu_t, wd, dep, name, target=None):
    s, d = x.shape
    f = wg_t[0].shape[1]
    tm = _row_tile(s) if wd is None else min(_row_tile(s), 256)
    fc = _col_chunk(f)
    nw = 2 if wd is None else 3
    n_in = nw + (1 if target is None else 2)

    def body(x_ref, g_ref, *refs):
        w_refs, outs = refs[:nw], refs[n_in:]
        xn_ref, dg_ref, du_ref, act_ref = outs[-4:]
        xv = x_ref[...]
        xn = (xv * _rstd(xv) * g_ref[...]).astype(BF16)
        xn_ref[...] = xn
        for c0 in range(0, f, fc):
            hg = _dotg(xn, w_refs[0][c0:c0 + fc, :], NT)
            hu = _dotg(xn, w_refs[1][c0:c0 + fc, :], NT)
            sg = _sigmoid(hg)
            silu = hg * sg
            du_ref[:, c0:c0 + fc] = silu.astype(BF16)
            dg_ref[:, c0:c0 + fc] = (hu * (sg + silu * (1.0 - sg))).astype(BF16)
            act_ref[:, c0:c0 + fc] = (silu * hu).astype(BF16)
        if wd is not None:
            y = xv + 0.5 * _dot(act_ref[...], w_refs[2][...])
            if target is None:
                outs[0][...] = y
            else:
                _loss_tile(y, refs[nw + 1], outs[0], outs[1])

    weights = [wg_t, wu_t] + ([] if wd is None else [wd])
    in_specs = [_rows(tm, d), _full((1, d))] + [_mat(*w) for w in weights] + [_full(dep.shape)]
    operands = [x, gain, *[w[0] for w in weights], dep]
    out_specs = [_rows(tm, d), _rows(tm, f), _rows(tm, f), _rows(tm, f)]
    out_shape = [jax.ShapeDtypeStruct((s, d), BF16)] + [jax.ShapeDtypeStruct((s, f), BF16)] * 3
    if wd is not None:
        out_specs, out_shape = [_rows(tm, d)] + out_specs, [jax.ShapeDtypeStruct((s, d), F32)] + out_shape
    if target is not None:
        in_specs, operands = in_specs + [_rows(tm, d)], operands + [target]
        out_specs = out_specs[:1] + [_full((8, LANES))] + out_specs[1:]
        out_shape = out_shape[:1] + [jax.ShapeDtypeStruct((8, LANES), F32)] + out_shape[1:]
    return pl.pallas_call(
        body, name=name, grid=(s // tm,), in_specs=in_specs, out_specs=out_specs, out_shape=out_shape,
        compiler_params=_params(),
    )(*operands)


def ffn_down(x, act, wd, dep, name):
    s, d = x.shape
    f = act.shape[1]
    tm = _row_tile(s)

    def body(x_ref, a_ref, w_ref, dep_ref, o_ref):
        o_ref[...] = x_ref[...] + 0.5 * _dot(a_ref[...], w_ref[...])

    return pl.pallas_call(
        body, name=name, grid=(s // tm,),
        in_specs=[_rows(tm, d), _rows(tm, f), _mat(*wd), _full(dep.shape)],
        out_specs=_rows(tm, d),
        out_shape=jax.ShapeDtypeStruct((s, d), F32),
        compiler_params=_params(),
    )(x, act, wd[0], dep)


def mix_in(x, gain, win_t, b_gate, gq_na, gk_na, gq_sw, gk_sw, bd, name):
    s, d = x.shape
    tm = _row_tile(s)
    gc = _col_chunk(2 * d)

    def body(x_ref, g_ref, w_ref, b_ref, gqa_ref, gka_ref, gqs_ref, gks_ref, bd_ref,
             hn_ref, zq_ref, qa_ref, ka_ref, qs_ref, ks_ref, gt_ref):
        xv = x_ref[...]
        hn = (xv * _rstd(xv) * g_ref[...]).astype(BF16)
        hn_ref[...] = hn

        def proj(c0, c1):
            return _dotg(hn, w_ref[c0:c1, :], NT)

        def headnorm(z, g, bdm):
            return z * lax.rsqrt(_group_mean(z * z, bdm) + EPS) * g

        bd512 = bd_ref[...]
        bd128 = bd_ref[0:SW_KV_WIDTH, 0:SW_KV_WIDTH]
        z = proj(0, 512)
        zq_ref[:, 0:512] = z.astype(BF16)
        qa_ref[...] = (headnorm(z, gqa_ref[...], bd512) * SCALE).astype(BF16)
        z = proj(512, 1024)
        zq_ref[:, 512:1024] = z.astype(BF16)
        ka_ref[...] = headnorm(z, gka_ref[...], bd512).astype(BF16)
        z = proj(1024, 1536)
        zq_ref[:, 1024:1536] = z.astype(BF16)
        z = proj(1536, 2048)
        zq_ref[:, 1536:2048] = z.astype(BF16)
        qs_ref[...] = (headnorm(z, gqs_ref[...], bd512) * SCALE).astype(BF16)
        z = proj(2048, 2176)
        zq_ref[:, 2048:2176] = z.astype(BF16)
        ks_ref[...] = headnorm(z, gks_ref[...], bd128).astype(BF16)
        z = proj(2176, 2304)
        zq_ref[:, 2176:2304] = z.astype(BF16)
        for c0 in range(0, 2 * d, gc):
            zg = proj(QKV_WIDTH + c0, QKV_WIDTH + c0 + gc) + b_ref[:, c0:c0 + gc]
            gt_ref[:, c0:c0 + gc] = _sigmoid(zg).astype(BF16)

    return pl.pallas_call(
        body, name=name, grid=(s // tm,),
        in_specs=[_rows(tm, d), _full((1, d)), _mat(*win_t), _full((1, 2 * d)),
                  _full((1, 512)), _full((1, 512)), _full((1, 512)), _full((1, 128)), _full((MXU_TILE, MXU_TILE))],
        out_specs=[_rows(tm, d), _rows(tm, QKV_WIDTH), _rows(tm, 512), _rows(tm, 512), _rows(tm, 512),
                   _rows(tm, 128), _rows(tm, 2 * d)],
        out_shape=[jax.ShapeDtypeStruct((s, d), BF16), jax.ShapeDtypeStruct((s, QKV_WIDTH), BF16),
                   jax.ShapeDtypeStruct((s, 512), BF16), jax.ShapeDtypeStruct((s, 512), BF16),
                   jax.ShapeDtypeStruct((s, 512), BF16), jax.ShapeDtypeStruct((s, 128), BF16),
                   jax.ShapeDtypeStruct((s, 2 * d), BF16)],
        compiler_params=_params(),
    )(x, gain, win_t[0], b_gate, gq_na, gk_na, gq_sw, gk_sw, bd)


def _na_iotas():
    qc = lax.broadcasted_iota(jnp.int32, (GRID_W, LANES), 0)
    ln = lax.broadcasted_iota(jnp.int32, (GRID_W, LANES), 1)
    low = ln < GRID_W
    kc = jnp.where(low, ln, ln - GRID_W)
    diff = kc - qc + (NA_COLS - 1)
    qcs = jnp.clip(qc - NA_COLS // 2, 0, GRID_W - NA_COLS)
    inwin = (kc >= qcs) & (kc < qcs + NA_COLS)
    return diff, low, inwin


NA_RI = 2 * NA_ROWS - 1
NA_CI = 2 * NA_COLS - 1
NA_T2 = NA_RI + 1


def _rpb_rows(rpb):
    h = rpb.shape[0]
    padded = jnp.pad(rpb, ((0, 0), (1, 1), (0, GRID_W - NA_CI)))
    return jnp.concatenate([padded[:, :NA_T2], padded[:, 1:NA_T2 + 1]], axis=2).reshape(h, NA_T2, LANES)


def _rpb_from_rows(rows):
    return rows[:, 1:, :NA_CI] + rows[:, :NA_RI, GRID_W:GRID_W + NA_CI]


def rpb_expand(rows, dep, name):
    n_heads = rows.shape[0]

    def body(r_ref, dep_ref, o_ref):
        for h in range(n_heads):
            for e in range(NA_T2):
                line = jnp.broadcast_to(r_ref[h, e:e + 1, :], (GRID_W, LANES))
                o_ref[h, e] = pltpu.roll(line, LANES - (NA_COLS - 1), 1, stride=1, stride_axis=0)

    return pl.pallas_call(
        body, name=name,
        in_specs=[pl.BlockSpec(memory_space=pltpu.VMEM), pl.BlockSpec(memory_space=pltpu.VMEM)],
        out_specs=pl.BlockSpec(memory_space=pltpu.VMEM),
        out_shape=jax.ShapeDtypeStruct((n_heads, NA_T2, GRID_W, LANES), F32),
        compiler_params=pltpu.CompilerParams(vmem_limit_bytes=V7X_VMEM_LIMIT),
    )(rows, dep)


def rpb_reduce(dt2, name):
    n_heads = dt2.shape[0]
    flip = jnp.asarray(np.eye(GRID_W)[::-1], BF16)

    def body(d_ref, j_ref, o_ref):
        jm = j_ref[...]
        for h in range(n_heads):
            for e in range(NA_T2):
                dv = d_ref[h, e]
                hi = dv.astype(BF16)
                mid = (dv - hi.astype(F32)).astype(BF16)
                lo = (dv - hi.astype(F32) - mid.astype(F32)).astype(BF16)
                rev = _dot(jm, hi) + _dot(jm, mid) + _dot(jm, lo)
                back = pltpu.roll(rev, LANES + (NA_COLS - 1) - (GRID_W - 1), 1, stride=1, stride_axis=0)
                o_ref[h, e:e + 1, :] = jnp.sum(back, axis=0, keepdims=True)

    return pl.pallas_call(
        body, name=name,
        in_specs=[pl.BlockSpec(memory_space=pltpu.VMEM)] * 2,
        out_specs=pl.BlockSpec(memory_space=pltpu.VMEM),
        out_shape=jax.ShapeDtypeStruct((n_heads, NA_T2, LANES), F32),
        compiler_params=pltpu.CompilerParams(vmem_limit_bytes=V7X_VMEM_LIMIT),
    )(dt2, flip)


NA_TQ = 4
NA_TK = NA_TQ + NA_ROWS
NA_KCH = NA_TK // 2


def _na_tile_geometry(t, rows):
    r = t * NA_TQ
    kbase = jnp.clip(r - NA_ROWS // 2, 0, rows - NA_TK)
    starts = [jnp.clip(r + a - NA_ROWS // 2, 0, rows - NA_ROWS) for a in range(NA_TQ)]
    return r, kbase, starts


def _na_tile_mask(kbase, starts, low, inwin):
    half = jnp.where(low, 0, 1)
    cols = []
    for c in range(NA_KCH):
        krow = kbase + 2 * c + half
        cols.append(jnp.concatenate(
            [jnp.where(inwin & (krow >= st) & (krow < st + NA_ROWS), 0.0, NEG) for st in starts], axis=0))
    return jnp.concatenate(cols, axis=1)


def _na_tile_index(r, kbase, a, c):
    return jnp.clip(kbase + 2 * c - (r + a) + NA_ROWS, 0, NA_T2 - 1)


def _na_tile_scores(q, k, t2_ref, hh, r, kbase, madd):
    bias = jnp.concatenate(
        [jnp.concatenate([t2_ref[hh, _na_tile_index(r, kbase, a, c)] for a in range(NA_TQ)], axis=0)
         for c in range(NA_KCH)], axis=1)
    return _dotg(q, k, NT) + bias + madd


def _softmax_rows(sc):
    e = jnp.exp(sc - jnp.max(sc, axis=1, keepdims=True))
    return e * (1.0 / jnp.sum(e, axis=1, keepdims=True))


def na_fwd(qa, ka, zq, t2, name):
    s = qa.shape[0]
    rows = s // GRID_W
    n_pairs = NA_WIDTH // LANES
    v_blk0 = (2 * NA_WIDTH) // LANES

    assert rows % NA_TQ == 0 and rows >= NA_TK
    tq, tk = NA_TQ * GRID_W, NA_TK * GRID_W

    def body(q_ref, k_ref, v_ref, t2_ref, o_ref, s_scr, p_scr):
        _, low, inwin = _na_iotas()

        def tile(t, carry):
            r, kbase, starts = _na_tile_geometry(t, rows)
            madd = _na_tile_mask(kbase, starts, low, inwin)
            qr = pl.ds(pl.multiple_of(r * GRID_W, tq), tq)
            kr = pl.ds(pl.multiple_of(kbase * GRID_W, tq), tk)
            for hh in range(2):
                lanes = slice(HEAD_DIM * hh, HEAD_DIM * (hh + 1))
                s_scr[tq * hh:tq * (hh + 1), :] = _na_tile_scores(q_ref[qr, lanes], k_ref[kr, lanes], t2_ref, hh, r,
                                                                  kbase, madd)
            p_scr[...] = _softmax_rows(s_scr[...]).astype(BF16)
            for hh in range(2):
                lanes = slice(HEAD_DIM * hh, HEAD_DIM * (hh + 1))
                o_ref[qr, lanes] = _dot(p_scr[tq * hh:tq * (hh + 1), :], v_ref[kr, lanes]).astype(BF16)
            return carry

        lax.fori_loop(0, rows // NA_TQ, tile, 0)

    col = lambda off: pl.BlockSpec((s, LANES), lambda p, _o=off: (0, _o + p))
    return pl.pallas_call(
        body, name=name, grid=(n_pairs,),
        in_specs=[col(0), col(0), col(v_blk0),
                  pl.BlockSpec((2, NA_T2, GRID_W, LANES), lambda p: (p, 0, 0, 0))],
        out_specs=col(0),
        out_shape=jax.ShapeDtypeStruct((s, NA_WIDTH), BF16),
        scratch_shapes=[pltpu.VMEM((2 * tq, tk), F32), pltpu.VMEM((2 * tq, tk), BF16)],
        compiler_params=_params(),
    )(qa, ka, zq, t2)


def na_bwd(qa, ka, zq, t2, o_na, do_na, name):
    s = qa.shape[0]
    rows = s // GRID_W
    n_pairs = NA_WIDTH // LANES
    v_blk0 = (2 * NA_WIDTH) // LANES

    tq, tk = NA_TQ * GRID_W, NA_TK * GRID_W

    def body(q_ref, k_ref, v_ref, t2_ref, o_ref, do_ref, dq_ref, dk_ref, dv_ref, dt2_ref):
        _, low, inwin = _na_iotas()
        dk_ref[...] = jnp.zeros(dk_ref.shape, F32)
        dv_ref[...] = jnp.zeros(dv_ref.shape, F32)
        dt2_ref[...] = jnp.zeros(dt2_ref.shape, F32)

        def tile(t, carry):
            r, kbase, starts = _na_tile_geometry(t, rows)
            madd = _na_tile_mask(kbase, starts, low, inwin)
            qr = pl.ds(pl.multiple_of(r * GRID_W, tq), tq)
            kr = pl.ds(pl.multiple_of(kbase * GRID_W, tq), tk)
            for hh in range(2):
                lanes = slice(HEAD_DIM * hh, HEAD_DIM * (hh + 1))
                q, k, v = q_ref[qr, lanes], k_ref[kr, lanes], v_ref[kr, lanes]
                p = _softmax_rows(_na_tile_scores(q, k, t2_ref, hh, r, kbase, madd))
                do = do_ref[qr, lanes]
                delta = jnp.sum(do.astype(F32) * o_ref[qr, lanes].astype(F32), axis=1, keepdims=True)
                ds = p * (_dotg(do, v, NT) - delta)
                shared = {}
                for a in range(NA_TQ):
                    for c in range(NA_KCH):
                        shared.setdefault(2 * c - a, []).append(
                            ds[GRID_W * a:GRID_W * (a + 1), LANES * c:LANES * (c + 1)])
                for offset, parts in shared.items():
                    e = jnp.clip(offset + kbase - r + NA_ROWS, 0, NA_T2 - 1)
                    dt2_ref[hh, e] = dt2_ref[hh, e] + functools.reduce(jnp.add, parts)
                dsb = ds.astype(BF16)
                dq_ref[qr, lanes] = _dot(dsb, k)
                dk_ref[kr, lanes] = dk_ref[kr, lanes] + _dotg(dsb, q, TN)
                dv_ref[kr, lanes] = dv_ref[kr, lanes] + _dotg(p.astype(BF16), do, TN)
            return carry

        lax.fori_loop(0, rows // NA_TQ, tile, 0)

    col = lambda off: pl.BlockSpec((s, LANES), lambda p, _o=off: (0, _o + p))
    t2spec = pl.BlockSpec((2, NA_T2, GRID_W, LANES), lambda p: (p, 0, 0, 0))
    return pl.pallas_call(
        body, name=name, grid=(n_pairs,),
        in_specs=[col(0), col(0), col(v_blk0), t2spec, col(0), col(0)],
        out_specs=[col(0), col(0), col(0), t2spec],
        out_shape=[jax.ShapeDtypeStruct((s, NA_WIDTH), F32)] * 3 + [jax.ShapeDtypeStruct(t2.shape, F32)],
        compiler_params=_params(),
    )(qa, ka, zq, t2, o_na, do_na)


def _t5_bucket_map():
    rel = np.arange(3 * SW_BLOCK)[None, :] - SW_BLOCK - np.arange(SW_BLOCK)[:, None]
    nb = REL_BUCKETS // 2
    max_exact = nb // 2
    n = np.abs(rel)
    large = max_exact + (np.log(np.maximum(n, 1) / max_exact)
                         / np.log(REL_MAX_DIST / max_exact) * (nb - max_exact)).astype(np.int32)
    large = np.minimum(large, nb - 1)
    return ((rel > 0) * nb + np.where(n < max_exact, n, large)).astype(np.int32)


def t5_expand(table, bmap, dep, name):
    def body(tab_ref, bm_ref, dep_ref, o_ref):
        bm = bm_ref[...]
        for h in range(SW_HEADS):
            t = jnp.zeros(bm.shape, F32)
            for b in range(REL_BUCKETS):
                t = jnp.where(bm == b, tab_ref[b, h], t)
            o_ref[h] = t

    return pl.pallas_call(
        body, name=name,
        in_specs=[pl.BlockSpec(memory_space=pltpu.SMEM), pl.BlockSpec(memory_space=pltpu.VMEM),
                  pl.BlockSpec(memory_space=pltpu.VMEM)],
        out_specs=pl.BlockSpec(memory_space=pltpu.VMEM),
        out_shape=jax.ShapeDtypeStruct((SW_HEADS,) + bmap.shape, F32),
        compiler_params=pltpu.CompilerParams(vmem_limit_bytes=V7X_VMEM_LIMIT),
    )(table, bmap, dep)


def t5_reduce(dbias_list, bmap, name):
    n = len(dbias_list)

    def body(*refs):
        d_refs, bm_ref, o_ref = refs[:n], refs[n], refs[n + 1]
        bm = bm_ref[...]
        for h in range(SW_HEADS):
            dv = d_refs[0][h]
            for other in d_refs[1:]:
                dv = dv + other[h]
            rows = [jnp.sum(jnp.where(bm == b, dv, 0.0), axis=0, keepdims=True) for b in range(REL_BUCKETS)]
            r = jnp.concatenate(rows, axis=0)
            o_ref[h] = jnp.broadcast_to(jnp.sum(r, axis=1, keepdims=True), (REL_BUCKETS, LANES))

    return pl.pallas_call(
        body, name=name,
        in_specs=[pl.BlockSpec(memory_space=pltpu.VMEM)] * (n + 1),
        out_specs=pl.BlockSpec(memory_space=pltpu.VMEM),
        out_shape=jax.ShapeDtypeStruct((SW_HEADS, REL_BUCKETS, LANES), F32),
        compiler_params=pltpu.CompilerParams(vmem_limit_bytes=V7X_VMEM_LIMIT),
    )(*dbias_list, bmap)


def _sw_mask_iotas():
    a = lax.broadcasted_iota(jnp.int32, (SW_BLOCK, 3 * SW_BLOCK), 0)
    j = lax.broadcasted_iota(jnp.int32, (SW_BLOCK, 3 * SW_BLOCK), 1)
    inwin = jnp.abs(j - SW_BLOCK - a) <= SW_BLOCK
    return j, inwin


SW_STACK = SW_HEADS * SW_BLOCK


def _sw_softmax(sc, sk):
    m = jnp.maximum(jnp.max(sc, axis=1, keepdims=True), sk)
    e = jnp.exp(sc - m)
    es = jnp.exp(sk - m)
    inv = 1.0 / (jnp.sum(e, axis=1, keepdims=True) + es)
    return e * inv, es * inv


def _sw_prologue(k_ref, v_ref, kp, vp, sink_ref, s):
    pad = s + 2 * SW_BLOCK
    zeros = jnp.zeros((SW_BLOCK, SW_KV_WIDTH), BF16)
    kp[0:SW_BLOCK, :] = zeros
    vp[0:SW_BLOCK, :] = zeros
    kp[SW_BLOCK + s:pad, :] = zeros
    vp[SW_BLOCK + s:pad, :] = zeros
    kp[SW_BLOCK:SW_BLOCK + s, :] = k_ref[...]
    vp[SW_BLOCK:SW_BLOCK + s, :] = v_ref[...]
    return jnp.concatenate([jnp.full((SW_BLOCK, 1), sink_ref[h], F32) for h in range(SW_HEADS)], axis=0)


def sw_fwd(qs, ks, zq, t5b, sink, dep, name):
    s = qs.shape[0]
    nb = s // SW_BLOCK
    v_blk = (3 * NA_WIDTH + SW_Q_WIDTH + SW_KV_WIDTH) // LANES
    pad = s + 2 * SW_BLOCK

    def body(q_ref, k_ref, v_ref, b_ref, sink_ref, dep_ref, o_ref, kp, vp, s_scr, p_scr):
        sink_col = _sw_prologue(k_ref, v_ref, kp, vp, sink_ref, s)
        j, inwin = _sw_mask_iotas()

        def blk(n, carry):
            kpos = n * SW_BLOCK - SW_BLOCK + j
            madd = jnp.where(inwin & (kpos >= 0) & (kpos < s), 0.0, NEG)
            q0 = pl.multiple_of(n * SW_BLOCK, SW_BLOCK)
            qr, kr = pl.ds(q0, SW_BLOCK), pl.ds(q0, 3 * SW_BLOCK)
            for h in range(SW_HEADS):
                g = h // SW_REP
                s_scr[SW_BLOCK * h:SW_BLOCK * (h + 1), :] = _dotg(
                    q_ref[qr, HEAD_DIM * h:HEAD_DIM * (h + 1)], kp[kr, HEAD_DIM * g:HEAD_DIM * (g + 1)], NT) + madd
            p, _ = _sw_softmax(s_scr[...] + b_ref[...], sink_col)
            p_scr[...] = p.astype(BF16)
            for h in range(SW_HEADS):
                g = h // SW_REP
                o_ref[qr, HEAD_DIM * h:HEAD_DIM * (h + 1)] = _dot(
                    p_scr[SW_BLOCK * h:SW_BLOCK * (h + 1), :], vp[kr, HEAD_DIM * g:HEAD_DIM * (g + 1)]).astype(BF16)
            return carry

        lax.fori_loop(0, nb, blk, 0)

    return pl.pallas_call(
        body, name=name, grid=(1,),
        in_specs=[_full((s, SW_Q_WIDTH)), _full((s, SW_KV_WIDTH)),
                  pl.BlockSpec((s, SW_KV_WIDTH), lambda i: (0, v_blk)),
                  _full((SW_STACK, 3 * SW_BLOCK)), pl.BlockSpec(memory_space=pltpu.SMEM),
                  _full(dep.shape)],
        out_specs=_full((s, SW_Q_WIDTH)),
        out_shape=jax.ShapeDtypeStruct((s, SW_Q_WIDTH), BF16),
        scratch_shapes=[pltpu.VMEM((pad, SW_KV_WIDTH), BF16), pltpu.VMEM((pad, SW_KV_WIDTH), BF16),
                        pltpu.VMEM((SW_STACK, 3 * SW_BLOCK), F32), pltpu.VMEM((SW_STACK, 3 * SW_BLOCK), BF16)],
        compiler_params=_params(),
    )(qs, ks, zq, t5b, sink, dep)


def sw_bwd(qs, ks, zq, t5b, sink, o_sw, do_sw, name):
    s = qs.shape[0]
    nb = s // SW_BLOCK
    v_blk = (3 * NA_WIDTH + SW_Q_WIDTH + SW_KV_WIDTH) // LANES
    pad = s + 2 * SW_BLOCK

    def body(q_ref, k_ref, v_ref, b_ref, sink_ref, o_ref, do_ref,
             dq_ref, dk_ref, dv_ref, db_ref, dsk_ref, kp, vp, dkp, dvp, s_scr, dp_scr, ds_scr, p_scr):
        sink_col = _sw_prologue(k_ref, v_ref, kp, vp, sink_ref, s)
        dkp[...] = jnp.zeros(dkp.shape, F32)
        dvp[...] = jnp.zeros(dvp.shape, F32)
        db_ref[...] = jnp.zeros(db_ref.shape, F32)
        dsk_ref[...] = jnp.zeros(dsk_ref.shape, F32)
        j, inwin = _sw_mask_iotas()

        def blk(n, carry):
            kpos = n * SW_BLOCK - SW_BLOCK + j
            madd = jnp.where(inwin & (kpos >= 0) & (kpos < s), 0.0, NEG)
            q0 = pl.multiple_of(n * SW_BLOCK, SW_BLOCK)
            qr, kr = pl.ds(q0, SW_BLOCK), pl.ds(q0, 3 * SW_BLOCK)
            deltas = []
            for h in range(SW_HEADS):
                g = h // SW_REP
                hl, kl = slice(HEAD_DIM * h, HEAD_DIM * (h + 1)), slice(HEAD_DIM * g, HEAD_DIM * (g + 1))
                rows = slice(SW_BLOCK * h, SW_BLOCK * (h + 1))
                do = do_ref[qr, hl]
                s_scr[rows, :] = _dotg(q_ref[qr, hl], kp[kr, kl], NT) + madd
                dp_scr[rows, :] = _dotg(do, vp[kr, kl], NT)
                deltas.append(jnp.sum(do.astype(F32) * o_ref[qr, hl].astype(F32), axis=1, keepdims=True))
            delta = jnp.concatenate(deltas, axis=0)
            p, ps = _sw_softmax(s_scr[...] + b_ref[...], sink_col)
            ds = p * (dp_scr[...] - delta)
            db_ref[...] = db_ref[...] + ds
            dsk_ref[...] = dsk_ref[...] - jnp.broadcast_to(ps * delta, (SW_STACK, LANES))
            ds_scr[...] = ds.astype(BF16)
            p_scr[...] = p.astype(BF16)
            for g in range(SW_HEADS // SW_REP):
                kl = slice(HEAD_DIM * g, HEAD_DIM * (g + 1))
                k = kp[kr, kl]
                dkw = jnp.zeros((3 * SW_BLOCK, HEAD_DIM), F32)
                dvw = jnp.zeros((3 * SW_BLOCK, HEAD_DIM), F32)
                for r in range(SW_REP):
                    h = g * SW_REP + r
                    hl, rows = slice(HEAD_DIM * h, HEAD_DIM * (h + 1)), slice(SW_BLOCK * h, SW_BLOCK * (h + 1))
                    dsb = ds_scr[rows, :]
                    dq_ref[qr, hl] = _dot(dsb, k)
                    dkw = dkw + _dotg(dsb, q_ref[qr, hl], TN)
                    dvw = dvw + _dotg(p_scr[rows, :], do_ref[qr, hl], TN)
                dkp[kr, kl] = dkp[kr, kl] + dkw
                dvp[kr, kl] = dvp[kr, kl] + dvw
            return carry

        lax.fori_loop(0, nb, blk, 0)
        dk_ref[...] = dkp[SW_BLOCK:SW_BLOCK + s, :]
        dv_ref[...] = dvp[SW_BLOCK:SW_BLOCK + s, :]

    bias_spec = _full((SW_STACK, 3 * SW_BLOCK))
    return pl.pallas_call(
        body, name=name, grid=(1,),
        in_specs=[_full((s, SW_Q_WIDTH)), _full((s, SW_KV_WIDTH)),
                  pl.BlockSpec((s, SW_KV_WIDTH), lambda i: (0, v_blk)),
                  bias_spec, pl.BlockSpec(memory_space=pltpu.SMEM),
                  _full((s, SW_Q_WIDTH)), _full((s, SW_Q_WIDTH))],
        out_specs=[_full((s, SW_Q_WIDTH)), _full((s, SW_KV_WIDTH)), _full((s, SW_KV_WIDTH)), bias_spec,
                   _full((SW_STACK, LANES))],
        out_shape=[jax.ShapeDtypeStruct((s, SW_Q_WIDTH), F32), jax.ShapeDtypeStruct((s, SW_KV_WIDTH), F32),
                   jax.ShapeDtypeStruct((s, SW_KV_WIDTH), F32),
                   jax.ShapeDtypeStruct((SW_STACK, 3 * SW_BLOCK), F32),
                   jax.ShapeDtypeStruct((SW_STACK, LANES), F32)],
        scratch_shapes=[pltpu.VMEM((pad, SW_KV_WIDTH), BF16), pltpu.VMEM((pad, SW_KV_WIDTH), BF16),
                        pltpu.VMEM((pad, SW_KV_WIDTH), F32), pltpu.VMEM((pad, SW_KV_WIDTH), F32),
                        pltpu.VMEM((SW_STACK, 3 * SW_BLOCK), F32), pltpu.VMEM((SW_STACK, 3 * SW_BLOCK), F32),
                        pltpu.VMEM((SW_STACK, 3 * SW_BLOCK), BF16), pltpu.VMEM((SW_STACK, 3 * SW_BLOCK), BF16)],
        compiler_params=_params(),
    )(qs, ks, zq, t5b, sink, o_sw, do_sw)


def merge_out(x, o_na, o_sw, gt, wbna_t, wbsw_t, wout, name):
    s, d = x.shape
    tm = _row_tile(s)

    def body(x_ref, ona_ref, osw_ref, gt_ref, wna_ref, wsw_ref, wo_ref, xo_ref, ana_ref, asw_ref, mg_ref):
        a_na = _dotg(ona_ref[...], wna_ref[...], NT)
        a_sw = _dotg(osw_ref[...], wsw_ref[...], NT)
        g_na, g_sw = gt_ref[:, 0:d].astype(F32), gt_ref[:, d:2 * d].astype(F32)
        ana_ref[...] = (a_na * g_na * (1.0 - g_na)).astype(BF16)
        asw_ref[...] = (a_sw * g_sw * (1.0 - g_sw)).astype(BF16)
        merged = (g_na * a_na + g_sw * a_sw).astype(BF16)
        mg_ref[...] = merged
        xo_ref[...] = x_ref[...] + _dot(merged, wo_ref[...])

    return pl.pallas_call(
        body, name=name, grid=(s // tm,),
        in_specs=[_rows(tm, d), _rows(tm, 512), _rows(tm, 512), _rows(tm, 2 * d),
                  _mat(*wbna_t), _mat(*wbsw_t), _mat(*wout)],
        out_specs=[_rows(tm, d)] * 4,
        out_shape=[jax.ShapeDtypeStruct((s, d), F32)] + [jax.ShapeDtypeStruct((s, d), BF16)] * 3,
        compiler_params=_params(),
    )(x, o_na, o_sw, gt, wbna_t[0], wbsw_t[0], wout[0])


def mix_bwd_out(dx, gt, a_na, a_sw, wbna_t, wbsw_t, wout, dep, name):
    s, d = dx.shape
    tm = _row_tile(s)

    def body(dx_ref, gt_ref, ana_ref, asw_ref, wna_ref, wsw_ref, wo_ref, dep_ref,
             dxb_ref, dzg_ref, dana_ref, dasw_ref, dona_ref, dosw_ref, dbg_ref):
        @pl.when(pl.program_id(0) == 0)
        def _():
            dbg_ref[...] = jnp.zeros(dbg_ref.shape, F32)

        dxb = dx_ref[...].astype(BF16)
        dxb_ref[...] = dxb
        dm = _dotg(dxb, wo_ref[...], NT)
        for i, (a_ref, da_ref, w_ref, do_ref) in enumerate(
                [(ana_ref, dana_ref, wna_ref, dona_ref), (asw_ref, dasw_ref, wsw_ref, dosw_ref)]):
            gi = gt_ref[:, i * d:(i + 1) * d].astype(F32)
            da = (dm * gi).astype(BF16)
            da_ref[...] = da
            do_ref[...] = _dot(da, w_ref[...]).astype(BF16)
            dzg = dm * a_ref[...].astype(F32)
            dzg_ref[:, i * d:(i + 1) * d] = dzg.astype(BF16)
            dbg_ref[:, i * d:(i + 1) * d] = dbg_ref[:, i * d:(i + 1) * d] + jnp.sum(dzg, axis=0, keepdims=True)

    return pl.pallas_call(
        body, name=name, grid=(s // tm,),
        in_specs=[_rows(tm, d), _rows(tm, 2 * d), _rows(tm, d), _rows(tm, d),
                  _mat(*wbna_t), _mat(*wbsw_t), _mat(*wout), _full(dep.shape)],
        out_specs=[_rows(tm, d), _rows(tm, 2 * d), _rows(tm, d), _rows(tm, d), _rows(tm, 512), _rows(tm, 512),
                   _full((1, 2 * d))],
        out_shape=[jax.ShapeDtypeStruct((s, d), BF16), jax.ShapeDtypeStruct((s, 2 * d), BF16),
                   jax.ShapeDtypeStruct((s, d), BF16), jax.ShapeDtypeStruct((s, d), BF16),
                   jax.ShapeDtypeStruct((s, 512), BF16), jax.ShapeDtypeStruct((s, 512), BF16),
                   jax.ShapeDtypeStruct((1, 2 * d), F32)],
        compiler_params=_params(),
    )(dx, gt, a_na, a_sw, wbna_t[0], wbsw_t[0], wout[0], dep)


def qk_norm_bwd(dqa, dka, dva, dqs, dks, dvs, zq, dzg, gq_na, gk_na, gq_sw, gk_sw, bd, name):
    s = zq.shape[0]
    d2 = dzg.shape[1]
    n_in = QKV_WIDTH + d2
    tm = _row_tile(s)

    def body(dqa_ref, dka_ref, dva_ref, dqs_ref, dks_ref, dvs_ref, zq_ref, dzg_ref,
             gqa_ref, gka_ref, gqs_ref, gks_ref, bd_ref, dz_ref, dgqa_ref, dgka_ref, dgqs_ref, dgks_ref):
        @pl.when(pl.program_id(0) == 0)
        def _():
            for r in (dgqa_ref, dgka_ref, dgqs_ref, dgks_ref):
                r[...] = jnp.zeros(r.shape, F32)

        bd512 = bd_ref[...]
        bd128 = bd_ref[0:SW_KV_WIDTH, 0:SW_KV_WIDTH]

        def one(c0, c1, dy_ref, g_ref, dg_ref, bdm, scale):
            z = zq_ref[:, c0:c1].astype(F32)
            r = lax.rsqrt(_group_mean(z * z, bdm) + EPS)
            zh = z * r
            dy = dy_ref[...] * scale
            dyg = dy * g_ref[...]
            dz = r * (dyg - zh * _group_mean(dyg * zh, bdm))
            dz_ref[:, c0:c1] = dz.astype(BF16)
            dg_ref[...] = dg_ref[...] + jnp.sum(dy * zh, axis=0, keepdims=True)

        one(0, 512, dqa_ref, gqa_ref, dgqa_ref, bd512, SCALE)
        one(512, 1024, dka_ref, gka_ref, dgka_ref, bd512, 1.0)
        dz_ref[:, 1024:1536] = dva_ref[...].astype(BF16)
        one(1536, 2048, dqs_ref, gqs_ref, dgqs_ref, bd512, SCALE)
        one(2048, 2176, dks_ref, gks_ref, dgks_ref, bd128, 1.0)
        dz_ref[:, 2176:2304] = dvs_ref[...].astype(BF16)
        dz_ref[:, QKV_WIDTH:n_in] = dzg_ref[...]

    return pl.pallas_call(
        body, name=name, grid=(s // tm,),
        in_specs=[_rows(tm, 512), _rows(tm, 512), _rows(tm, 512), _rows(tm, 512), _rows(tm, 128), _rows(tm, 128),
                  _rows(tm, QKV_WIDTH), _rows(tm, d2),
                  _full((1, 512)), _full((1, 512)), _full((1, 512)), _full((1, 128)), _full((MXU_TILE, MXU_TILE))],
        out_specs=[_rows(tm, n_in), _full((1, 512)), _full((1, 512)), _full((1, 512)), _full((1, 128))],
        out_shape=[jax.ShapeDtypeStruct((s, n_in), BF16)] + [jax.ShapeDtypeStruct((1, 512), F32)] * 3
                  + [jax.ShapeDtypeStruct((1, 128), F32)],
        compiler_params=_params(),
    )(dqa, dka, dva, dqs, dks, dvs, zq, dzg, gq_na, gk_na, gq_sw, gk_sw, bd)


def ffn_bwd_act(dx, wd, hg, hu, name):
    s, d = dx.shape
    f = wd[0].shape[1]
    tm = _row_tile(s)
    fc = _col_chunk(f)

    def body(dx_ref, w_ref, hg_ref, hu_ref, dxb_ref, dhg_ref, dhu_ref):
        dxv = dx_ref[...]
        dxb_ref[...] = dxv.astype(BF16)
        half = (0.5 * dxv).astype(BF16)
        for c0 in range(0, f, fc):
            dact = _dotg(half, w_ref[c0:c0 + fc, :], NT)
            dhu_ref[:, c0:c0 + fc] = (dact * hu_ref[:, c0:c0 + fc].astype(F32)).astype(BF16)
            dhg_ref[:, c0:c0 + fc] = (dact * hg_ref[:, c0:c0 + fc].astype(F32)).astype(BF16)

    return pl.pallas_call(
        body, name=name, grid=(s // tm,),
        in_specs=[_rows(tm, d), _mat(*wd), _rows(tm, f), _rows(tm, f)],
        out_specs=[_rows(tm, d), _rows(tm, f), _rows(tm, f)],
        out_shape=[jax.ShapeDtypeStruct((s, d), BF16), jax.ShapeDtypeStruct((s, f), BF16),
                   jax.ShapeDtypeStruct((s, f), BF16)],
        compiler_params=_params(),
    )(dx, wd[0], hg, hu)


def proj_bwd_norm(acts, weights, x, gain, dx, dep, name):
    s, d = x.shape
    tm = min(_row_tile(s), 256)
    n = len(acts)

    def body(*refs):
        a_refs, w_refs = refs[:n], refs[n:2 * n]
        x_ref, g_ref, dx_ref, _, o_ref, dg_ref = refs[2 * n:]

        @pl.when(pl.program_id(0) == 0)
        def _():
            dg_ref[...] = jnp.zeros(dg_ref.shape, F32)

        dxn = _dot(a_refs[0][...], w_refs[0][...])
        for a_ref, w_ref in zip(a_refs[1:], w_refs[1:]):
            dxn = dxn + _dot(a_ref[...], w_ref[...])
        xv = x_ref[...]
        r = _rstd(xv)
        xh = xv * r
        dxh = dxn * g_ref[...]
        o_ref[...] = dx_ref[...] + r * (dxh - xh * jnp.mean(dxh * xh, axis=-1, keepdims=True))
        dg_ref[...] = dg_ref[...] + jnp.sum(dxn * xh, axis=0, keepdims=True)

    return pl.pallas_call(
        body, name=name, grid=(s // tm,),
        in_specs=[_rows(tm, a.shape[1]) for a in acts] + [_mat(*w) for w in weights]
                 + [_rows(tm, d), _full((1, d)), _rows(tm, d), _full(dep.shape)],
        out_specs=[_rows(tm, d), _full((1, d))],
        out_shape=[jax.ShapeDtypeStruct((s, d), F32), jax.ShapeDtypeStruct((1, d), F32)],
        compiler_params=_params(),
    )(*acts, *[w[0] for w in weights], x, gain, dx, dep)


def tn_matmul(products, name):
    s, n = products[0][0].shape
    tn = _tn_tile(n) if len(products) == 1 else _col_chunk(n)
    rhs = []
    for _, b, _ in products:
        if not any(b is seen for seen in rhs):
            rhs.append(b)
    which = [next(i for i, seen in enumerate(rhs) if b is seen) for _, b, _ in products]
    npr, nr = len(products), len(rhs)

    def body(*refs):
        a_refs, b_refs, o_refs = refs[:npr], refs[npr:npr + nr], refs[npr + nr:]
        for i, (_, _, scale) in enumerate(products):
            o_refs[i][...] = (scale * _dotg(a_refs[i][...], b_refs[which[i]][...], TN)).astype(BF16)

    return pl.pallas_call(
        body, name=name, grid=(n // tn,),
        in_specs=[pl.BlockSpec((s, tn), lambda i: (0, i))] * npr
                 + [pl.BlockSpec(b.shape, lambda i: (0, 0), pipeline_mode=ONCE) for b in rhs],
        out_specs=[pl.BlockSpec((tn, b.shape[1]), lambda i: (i, 0)) for _, b, _ in products],
        out_shape=[jax.ShapeDtypeStruct((n, b.shape[1]), BF16) for _, b, _ in products],
        compiler_params=_params(),
    )(*[a for a, _, _ in products], *rhs)


def _mesh_pos():
    return lax.axis_index("x"), lax.axis_index("y"), lax.axis_index("c")


def _peers():
    x, y, c = _mesh_pos()
    peers = []
    for rel in range(1, N_DEV):
        peers.append((1 - x if rel & 4 else x, 1 - y if rel & 2 else y, 1 - c if rel & 1 else c))
    return 4 * x + 2 * y + c, peers


HBM_SPEC = pl.BlockSpec(memory_space=pltpu.HBM)
SEM_SPEC = pl.BlockSpec(memory_space=pltpu.SEMAPHORE)


def _split_call(body, name, thru, n_sems, extra=(), with_token=True):
    hbm = lambda t: pltpu.with_memory_space_constraint(t, pltpu.HBM)
    effect = pltpu.CompilerParams(has_side_effects=pltpu.SideEffectType.DATAFLOW_SIDE_EFFECTING)
    nt = len(thru)
    thru_shapes = [pltpu.HBM(t.shape, t.dtype) for t in thru]
    if with_token:
        (after,) = extra
        outs = pl.pallas_call(
            body, name=name, in_specs=[HBM_SPEC] * nt + [pl.BlockSpec(memory_space=pl.ANY)],
            out_specs=[SEM_SPEC] * len(n_sems) + [HBM_SPEC] * nt + [pl.BlockSpec(memory_space=pltpu.VMEM)],
            out_shape=[pltpu.SemaphoreType.DMA((k,)) for k in n_sems] + thru_shapes
                      + [jax.ShapeDtypeStruct((8, LANES), F32)],
            input_output_aliases={i: len(n_sems) + i for i in range(nt)}, compiler_params=effect,
        )(*[hbm(t) for t in thru], after)
        return outs[:len(n_sems)], outs[len(n_sems):-1], outs[-1]
    return pl.pallas_call(
        body, name=name,
        in_specs=[HBM_SPEC] * nt + [SEM_SPEC] * len(n_sems) + [pl.BlockSpec(memory_space=pl.ANY)],
        out_specs=[HBM_SPEC] * nt, out_shape=thru_shapes,
        input_output_aliases={i: i for i in range(nt)}, compiler_params=effect,
    )(*thru, *extra)


def _gather_targets():
    x, y, c = _mesh_pos()
    return 4 * x + 2 * y + c, [(x, y, 1 - c), (1 - x, y, c), (x, 1 - y, c), (1 - x, 1 - y, c)]


def gather_start(shards, after, name):
    n = len(shards)
    zones = [lax.empty((w.shape[0], N_DEV) + w.shape[1:], w.dtype) for w in shards]

    def body(*refs):
        ins, zs = refs[:n], refs[n:2 * n]
        send_sems, recv_sems, local_sems = refs[2 * n + 1:2 * n + 4]
        token = refs[-1]
        me, targets = _gather_targets()
        for a in range(n):
            pltpu.make_async_copy(ins[a], zs[a].at[:, me], local_sems.at[a]).start()
            for k, to in enumerate(targets):
                pltpu.make_async_remote_copy(
                    src_ref=ins[a], dst_ref=zs[a].at[:, me], send_sem=send_sems.at[4 * a + k],
                    recv_sem=recv_sems.at[4 * a + k], device_id=to, device_id_type=MESH).start()
        token[...] = jnp.zeros(token.shape, F32)

    sems, thru, token = _split_call(body, name, list(shards) + zones, (4 * n, 4 * n, n), extra=(after,))
    return (sems, thru, n), token


def gather_wait(started, after, name):
    sems, thru, n = started

    def body(*refs):
        zs = refs[n:2 * n]
        send_sems, recv_sems, local_sems = refs[2 * n:2 * n + 3]
        _, targets = _gather_targets()
        for a in range(n):
            for k, to in enumerate(targets):
                cp = pltpu.make_async_remote_copy(
                    src_ref=zs[a].at[:, 0], dst_ref=zs[a].at[:, 0], send_sem=send_sems.at[4 * a + k],
                    recv_sem=recv_sems.at[4 * a + k], device_id=to, device_id_type=MESH)
                cp.wait_send()
                cp.wait_recv()
            pltpu.make_async_copy(zs[a].at[:, 0], zs[a].at[:, 0], local_sems.at[a]).wait()

    return _split_call(body, name, thru, (4 * n, 4 * n, n), extra=(*sems, after), with_token=False)[n:]


def forward_start(zones, after, name):
    n = len(zones)

    def body(*refs):
        zs = refs[:n]
        send_sems, recv_sems = refs[n + 1:n + 3]
        token = refs[-1]
        x, y, c = _mesh_pos()
        for a in range(n):
            for j, chip in enumerate([(1 - x, y), (x, 1 - y), (1 - x, 1 - y)]):
                blk = zs[a].at[:, 4 * chip[0] + 2 * chip[1] + c]
                pltpu.make_async_remote_copy(
                    src_ref=blk, dst_ref=blk, send_sem=send_sems.at[3 * a + j], recv_sem=recv_sems.at[3 * a + j],
                    device_id=(x, y, 1 - c), device_id_type=MESH).start()
        token[...] = jnp.zeros(token.shape, F32)

    sems, thru, token = _split_call(body, name, list(zones), (3 * n, 3 * n), extra=(after,))
    return (sems, thru, n), token


def forward_wait(started, after, name):
    sems, thru, n = started

    def body(*refs):
        zs = refs[:n]
        send_sems, recv_sems = refs[n:n + 2]
        x, y, c = _mesh_pos()
        for a in range(n):
            for j in range(3):
                cp = pltpu.make_async_remote_copy(
                    src_ref=zs[a].at[:, 0], dst_ref=zs[a].at[:, 0], send_sem=send_sems.at[3 * a + j],
                    recv_sem=recv_sems.at[3 * a + j], device_id=(x, y, 1 - c), device_id_type=MESH)
                cp.wait_send()
                cp.wait_recv()

    return _split_call(body, name, thru, (3 * n, 3 * n), extra=(*sems, after), with_token=False)


def scatter_start(groups, name):
    n = len(groups)
    flat = [g for grp in groups for g in grp]
    nf = len(flat)
    offs = np.cumsum([0] + [len(grp) for grp in groups])
    lands = [lax.empty((N_DEV, len(grp)) + grp[0].shape[1:], grp[0].dtype) for grp in groups]

    def body(*refs):
        ins, zones = refs[:nf], refs[nf:nf + n]
        send_sems, recv_sems, local_sems = refs[nf + n:nf + n + 3]
        token = refs[-1]
        me, peers = _peers()
        for a in range(n):
            for w in range(len(groups[a])):
                pltpu.make_async_copy(ins[offs[a] + w].at[me], zones[a].at[me, w], local_sems.at[a]).start()
        for k, peer in enumerate(peers):
            p_id = 4 * peer[0] + 2 * peer[1] + peer[2]
            for a in range(n):
                for w in range(len(groups[a])):
                    pltpu.make_async_remote_copy(
                        src_ref=ins[offs[a] + w].at[p_id], dst_ref=zones[a].at[me, w],
                        send_sem=send_sems.at[7 * a + k], recv_sem=recv_sems.at[7 * a + k],
                        device_id=peer, device_id_type=MESH).start()
        token[...] = jnp.zeros(token.shape, F32)

    hbm = lambda t: pltpu.with_memory_space_constraint(t, pltpu.HBM)
    outs = pl.pallas_call(
        body, name=name,
        in_specs=[HBM_SPEC] * (nf + n),
        out_specs=[SEM_SPEC] * 3 + [HBM_SPEC] * (nf + n) + [pl.BlockSpec(memory_space=pltpu.VMEM)],
        out_shape=[pltpu.SemaphoreType.DMA((7 * n,)), pltpu.SemaphoreType.DMA((7 * n,)), pltpu.SemaphoreType.DMA((n,))]
                  + [pltpu.HBM(t.shape, t.dtype) for t in flat + lands]
                  + [jax.ShapeDtypeStruct((8, LANES), F32)],
        input_output_aliases={i: 3 + i for i in range(nf + n)},
        compiler_params=pltpu.CompilerParams(has_side_effects=pltpu.SideEffectType.DATAFLOW_SIDE_EFFECTING),
    )(*[hbm(t) for t in flat], *[hbm(t) for t in lands])
    sems, thru, token = outs[:3], outs[3:3 + nf + n], outs[-1]
    return (sems, thru, [len(grp) for grp in groups]), token


def scatter_wait(started, after, name):
    (send_sems, recv_sems, local_sems), thru, sizes = started
    n = len(sizes)
    nf = len(thru) - n

    def body(*refs):
        zones = refs[nf:nf + n]
        s_sems, r_sems, l_sems = refs[nf + n:nf + n + 3]
        me, peers = _peers()
        for a in range(n):
            for k, peer in enumerate(peers):
                cp = pltpu.make_async_remote_copy(
                    src_ref=zones[a].at[0], dst_ref=zones[a].at[0],
                    send_sem=s_sems.at[7 * a + k], recv_sem=r_sems.at[7 * a + k], device_id=peer,
                    device_id_type=MESH)
                cp.wait_send()
                cp.wait_recv()
            pltpu.make_async_copy(zones[a].at[0], zones[a].at[0], l_sems.at[a]).wait()

    outs = pl.pallas_call(
        body, name=name,
        in_specs=[HBM_SPEC] * (nf + n) + [SEM_SPEC] * 3 + [pl.BlockSpec(memory_space=pl.ANY)],
        out_specs=[HBM_SPEC] * (nf + n),
        out_shape=[pltpu.HBM(t.shape, t.dtype) for t in thru],
        input_output_aliases={i: i for i in range(nf + n)},
        compiler_params=pltpu.CompilerParams(has_side_effects=pltpu.SideEffectType.DATAFLOW_SIDE_EFFECTING),
    )(*thru, send_sems, recv_sems, local_sems, after)
    return outs[nf:]


def pair_start(grads, after, name):
    nw = len(grads)
    land = lax.empty((4, nw) + grads[0].shape[1:], grads[0].dtype)

    def body(*refs):
        ins, zone = refs[:nw], refs[nw]
        send_sems, recv_sems = refs[nw + 2:nw + 4]
        x, y, c = _mesh_pos()
        for j in range(4):
            for w in range(nw):
                pltpu.make_async_remote_copy(
                    src_ref=ins[w].at[2 * j + (1 - c)], dst_ref=zone.at[j, w], send_sem=send_sems.at[0],
                    recv_sem=recv_sems.at[0], device_id=(x, y, 1 - c), device_id_type=MESH).start()
        refs[-1][...] = jnp.zeros(refs[-1].shape, F32)

    sems, thru, token = _split_call(body, name, list(grads) + [land], (1, 1), extra=(after,))
    return (sems, thru, nw), token


def pair_wait(started, after, name):
    sems, thru, nw = started

    def body(*refs):
        zone = refs[nw]
        send_sems, recv_sems = refs[nw + 1:nw + 3]
        x, y, c = _mesh_pos()
        cp = pltpu.make_async_remote_copy(src_ref=zone, dst_ref=zone, send_sem=send_sems.at[0],
                                          recv_sem=recv_sems.at[0], device_id=(x, y, 1 - c), device_id_type=MESH)
        cp.wait_send()
        cp.wait_recv()

    outs = _split_call(body, name, thru, (1, 1), extra=(*sems, after), with_token=False)
    return outs[:nw], outs[nw]


def pair_sum(grads, land, name):
    nw = len(grads)
    _, r, c_dim = grads[0].shape

    def body(*refs):
        g_refs, l_ref, o_ref = refs[:nw], refs[nw], refs[nw + 1]
        core = lax.axis_index("c")
        for w in range(nw):
            o_ref[0, w] = (g_refs[w][0, core].astype(F32) + l_ref[0, w].astype(F32)).astype(BF16)

    return pl.pallas_call(
        body, name=name, grid=(4,),
        in_specs=[pl.BlockSpec((1, 2, r, c_dim), lambda j: (j, 0, 0, 0))] * nw
                 + [pl.BlockSpec((1, nw, r, c_dim), lambda j: (j, 0, 0, 0))],
        out_specs=pl.BlockSpec((1, nw, r, c_dim), lambda j: (j, 0, 0, 0)),
        out_shape=jax.ShapeDtypeStruct((4, nw, r, c_dim), BF16),
        compiler_params=_params(),
    )(*[g.reshape(4, 2, r, c_dim) for g in grads], land)


def _other_chips():
    x, y, c = _mesh_pos()
    chips = []
    for rel in range(1, 4):
        px, py = (1 - x if rel & 2 else x), (1 - y if rel & 1 else y)
        chips.append((px, py, 2 * px + py))
    return 2 * x + y, c, chips


def chip_start(pair_sums, after, name):
    land = lax.empty(pair_sums.shape, pair_sums.dtype)

    def body(*refs):
        h_ref, zone = refs[0], refs[1]
        send_sems, recv_sems, local_sem = refs[3:6]
        mine, c, chips = _other_chips()
        pltpu.make_async_copy(h_ref.at[mine], zone.at[mine], local_sem.at[0]).start()
        for k, (px, py, j) in enumerate(chips):
            pltpu.make_async_remote_copy(
                src_ref=h_ref.at[j], dst_ref=zone.at[mine], send_sem=send_sems.at[k], recv_sem=recv_sems.at[k],
                device_id=(px, py, c), device_id_type=MESH).start()
        refs[-1][...] = jnp.zeros(refs[-1].shape, F32)

    sems, thru, token = _split_call(body, name, [pair_sums, land], (3, 3, 1), extra=(after,))
    return (sems, thru), token


def chip_wait(started, after, name):
    sems, thru = started

    def body(*refs):
        zone = refs[1]
        send_sems, recv_sems, local_sem = refs[2:5]
        _, c, chips = _other_chips()
        for k, (px, py, _) in enumerate(chips):
            cp = pltpu.make_async_remote_copy(
                src_ref=zone.at[0], dst_ref=zone.at[0], send_sem=send_sems.at[k], recv_sem=recv_sems.at[k],
                device_id=(px, py, c), device_id_type=MESH)
            cp.wait_send()
            cp.wait_recv()
        pltpu.make_async_copy(zone.at[0], zone.at[0], local_sem.at[0]).wait()

    return _split_call(body, name, thru, (3, 3, 1), extra=(*sems, after), with_token=False)[1]


def share_start(parts, after, name):
    n = len(parts)
    zones = [lax.empty((N_DEV,) + p.shape, p.dtype) for p in parts]

    def body(*refs):
        ins, zs = refs[:n], refs[n:2 * n]
        send_sems, recv_sems, local_sems = refs[2 * n + 1:2 * n + 4]
        me, peers = _peers()
        for i in range(n):
            pltpu.make_async_copy(ins[i], zs[i].at[me], local_sems.at[i]).start()
            for k, peer in enumerate(peers):
                pltpu.make_async_remote_copy(
                    src_ref=ins[i], dst_ref=zs[i].at[me], send_sem=send_sems.at[7 * i + k],
                    recv_sem=recv_sems.at[7 * i + k], device_id=peer, device_id_type=MESH).start()
        refs[-1][...] = jnp.zeros(refs[-1].shape, F32)

    sems, thru, token = _split_call(body, name, list(parts) + zones, (7 * n, 7 * n, n), extra=(after,))
    return (sems, thru, n), token


def share_wait(started, after, name):
    sems, thru, n = started

    def body(*refs):
        zs = refs[n:2 * n]
        send_sems, recv_sems, local_sems = refs[2 * n:2 * n + 3]
        _, peers = _peers()
        for i in range(n):
            for k, peer in enumerate(peers):
                cp = pltpu.make_async_remote_copy(
                    src_ref=zs[i].at[0], dst_ref=zs[i].at[0], send_sem=send_sems.at[7 * i + k],
                    recv_sem=recv_sems.at[7 * i + k], device_id=peer, device_id_type=MESH)
                cp.wait_send()
                cp.wait_recv()
            pltpu.make_async_copy(zs[i].at[0], zs[i].at[0], local_sems.at[i]).wait()

    return _split_call(body, name, thru, (7 * n, 7 * n, n), extra=(*sems, after), with_token=False)[n:]


def _adamw_math(w, g, m, v):
    m = ADAM_B1 * m + (1.0 - ADAM_B1) * g
    v = ADAM_B2 * v + (1.0 - ADAM_B2) * (g * g)
    m_hat = m / (1.0 - ADAM_B1 ** ADAM_STEP)
    v_hat = v / (1.0 - ADAM_B2 ** ADAM_STEP)
    delta = -ADAM_LR * (m_hat / (jnp.sqrt(v_hat) + ADAM_EPS) + ADAM_WD * w)
    return delta, m, v


ADAMW_BLOCK_BYTES = 24 * 1024 * 1024


def adamw_layer(zone, layer, items, after, name):
    n_src, nw, r, c = zone.shape
    depth = items[0][0].shape[0]
    prevs = [p if p is not None else tuple(lax.empty((depth, r, c), F32) for _ in range(4)) for _, _, _, p in items]
    row_bytes = 2 * nw * c * (2 * n_src + 4 * 7)
    tr = max(t for t in range(8, r + 1, 8) if r % t == 0 and t * row_bytes <= ADAMW_BLOCK_BYTES)

    def body(z_ref, *rest):
        ins, outs = rest[:3 * nw], rest[7 * nw + 1:]
        for i in range(nw):
            g = z_ref[0, i].astype(F32)
            for src in range(1, n_src):
                g = g + z_ref[src, i].astype(F32)
            g_ref, d_ref, mo_ref, vo_ref = outs[4 * i:4 * i + 4]
            w_ref, m_ref, v_ref = ins[3 * i:3 * i + 3]
            g_ref[...] = g
            d_ref[...], mo_ref[...], vo_ref[...] = _adamw_math(w_ref[...], g, m_ref[...], v_ref[...])

    rows = pl.BlockSpec((None, tr, c), lambda i: (layer, i, 0))
    anywhere = pl.BlockSpec(memory_space=pl.ANY)
    outs = pl.pallas_call(
        body, name=name, grid=(r // tr,),
        in_specs=[pl.BlockSpec((n_src, nw, tr, c), lambda i: (0, 0, i, 0))] + [rows] * (3 * nw)
                 + [anywhere] * (4 * nw + 1),
        out_specs=[rows] * (4 * nw),
        out_shape=[jax.ShapeDtypeStruct((depth, r, c), F32)] * (4 * nw),
        input_output_aliases={1 + 3 * nw + k: k for k in range(4 * nw)},
        compiler_params=_params(),
    )(zone, *[t for w, m, v, _ in items for t in (w, m, v)], *[t for p in prevs for t in p], after)
    return [tuple(outs[4 * i:4 * i + 4]) for i in range(nw)]


def adamw_small(ws, recvs, ms, vs, name):
    n = len(ws)

    def body(*refs):
        w_refs, r_refs, m_refs, v_refs = (refs[i * n:(i + 1) * n] for i in range(4))
        g_refs, d_refs, mo_refs, vo_refs = (refs[(4 + i) * n:(5 + i) * n] for i in range(4))
        for i in range(n):
            g = r_refs[i][0]
            for src in range(1, N_DEV):
                g = g + r_refs[i][src]
            g_refs[i][...] = g
            d_refs[i][...], mo_refs[i][...], vo_refs[i][...] = _adamw_math(w_refs[i][...], g, m_refs[i][...],
                                                                            v_refs[i][...])

    vm = pl.BlockSpec(memory_space=pltpu.VMEM)
    outs = pl.pallas_call(
        body, name=name, in_specs=[vm] * (4 * n), out_specs=[vm] * (4 * n),
        out_shape=[jax.ShapeDtypeStruct(w.shape, F32) for w in ws] * 4,
        compiler_params=pltpu.CompilerParams(vmem_limit_bytes=V7X_VMEM_LIMIT),
    )(*ws, *recvs, *ms, *vs)
    return [outs[i * n:(i + 1) * n] for i in range(4)]


SMALL_NAMES = ("ffn1_norm", "mix_norm", "ffn2_norm", "b_gate", "na_q_norm", "na_k_norm", "sw_q_norm", "sw_k_norm",
               "na_rpb", "sw_sink", "t5_rel_table")


def kernel(x, ffn1_norm, ffn1_w_gate, ffn1_w_up, ffn1_w_down, mix_norm, w_in, b_gate, na_q_norm, na_k_norm, na_rpb, sw_q_norm, sw_k_norm, sw_sink, t5_rel_table, w_branch_na, w_branch_sw, w_out, ffn2_norm, ffn2_w_gate, ffn2_w_up, ffn2_w_down, loss_target, m_ffn1_norm, m_ffn1_w_gate, m_ffn1_w_up, m_ffn1_w_down, m_mix_norm, m_w_in, m_b_gate, m_na_q_norm, m_na_k_norm, m_na_rpb, m_sw_q_norm, m_sw_k_norm, m_sw_sink, m_t5_rel_table, m_w_branch_na, m_w_branch_sw, m_w_out, m_ffn2_norm, m_ffn2_w_gate, m_ffn2_w_up, m_ffn2_w_down, v_ffn1_norm, v_ffn1_w_gate, v_ffn1_w_up, v_ffn1_w_down, v_mix_norm, v_w_in, v_b_gate, v_na_q_norm, v_na_k_norm, v_na_rpb, v_sw_q_norm, v_sw_k_norm, v_sw_sink, v_t5_rel_table, v_w_branch_na, v_w_branch_sw, v_w_out, v_ffn2_norm, v_ffn2_w_gate, v_ffn2_w_up, v_ffn2_w_down):
    weights = dict(ffn1_norm=ffn1_norm, ffn1_w_gate=ffn1_w_gate, ffn1_w_up=ffn1_w_up, ffn1_w_down=ffn1_w_down,
                   mix_norm=mix_norm, w_in=w_in, b_gate=b_gate, na_q_norm=na_q_norm, na_k_norm=na_k_norm,
                   na_rpb=na_rpb, sw_q_norm=sw_q_norm, sw_k_norm=sw_k_norm, sw_sink=sw_sink,
                   t5_rel_table=t5_rel_table, w_branch_na=w_branch_na, w_branch_sw=w_branch_sw, w_out=w_out,
                   ffn2_norm=ffn2_norm, ffn2_w_gate=ffn2_w_gate, ffn2_w_up=ffn2_w_up, ffn2_w_down=ffn2_w_down)
    mom_m = dict(ffn1_norm=m_ffn1_norm, ffn1_w_gate=m_ffn1_w_gate, ffn1_w_up=m_ffn1_w_up, ffn1_w_down=m_ffn1_w_down,
                 mix_norm=m_mix_norm, w_in=m_w_in, b_gate=m_b_gate, na_q_norm=m_na_q_norm, na_k_norm=m_na_k_norm,
                 na_rpb=m_na_rpb, sw_q_norm=m_sw_q_norm, sw_k_norm=m_sw_k_norm, sw_sink=m_sw_sink,
                 t5_rel_table=m_t5_rel_table, w_branch_na=m_w_branch_na, w_branch_sw=m_w_branch_sw, w_out=m_w_out,
                 ffn2_norm=m_ffn2_norm, ffn2_w_gate=m_ffn2_w_gate, ffn2_w_up=m_ffn2_w_up, ffn2_w_down=m_ffn2_w_down)
    mom_v = dict(ffn1_norm=v_ffn1_norm, ffn1_w_gate=v_ffn1_w_gate, ffn1_w_up=v_ffn1_w_up, ffn1_w_down=v_ffn1_w_down,
                 mix_norm=v_mix_norm, w_in=v_w_in, b_gate=v_b_gate, na_q_norm=v_na_q_norm, na_k_norm=v_na_k_norm,
                 na_rpb=v_na_rpb, sw_q_norm=v_sw_q_norm, sw_k_norm=v_sw_k_norm, sw_sink=v_sw_sink,
                 t5_rel_table=v_t5_rel_table, w_branch_na=v_w_branch_na, w_branch_sw=v_w_branch_sw, w_out=v_w_out,
                 ffn2_norm=v_ffn2_norm, ffn2_w_gate=v_ffn2_w_gate, ffn2_w_up=v_ffn2_w_up, ffn2_w_down=v_ffn2_w_down)
    order = list(weights)

    depth = ffn1_norm.shape[0]
    s, d = x.shape[1], x.shape[2]
    xs = x[0]
    tr = lambda w: jnp.swapaxes(w, -1, -2)

    merge = lambda t: t.reshape(t.shape[0], N_DEV * t.shape[2], t.shape[3])
    no_dep = jnp.zeros((8, LANES), F32)

    def shards_of(kind, l):
        stack = lambda *ws: jnp.stack(ws).astype(BF16)
        if kind == "ffn1":
            return [stack(tr(ffn1_w_gate[l]), tr(ffn1_w_up[l]), ffn1_w_down[l])]
        if kind == "win":
            return [stack(tr(w_in[l]))]
        return [stack(tr(ffn2_w_gate[l]), tr(ffn2_w_up[l]), ffn2_w_down[l]), stack(w_out[l]),
                stack(tr(w_branch_na[l]), tr(w_branch_sw[l]))]

    def start(kind, l, after):
        return gather_start(shards_of(kind, l), after, f"gather_{kind}_{l}")

    def arrive(started, kind, l, after):
        zones = gather_wait(started, after, f"gather_{kind}_{l}_wait")
        return forward_start(zones, no_dep, f"forward_{kind}_{l}")

    def finish(fwd, kind, l, after):
        return [merge(z) for z in forward_wait(fwd, after, f"forward_{kind}_{l}_wait")]

    bd = jnp.asarray(np.kron(np.eye(MXU_TILE // HEAD_DIM), np.full((HEAD_DIM, HEAD_DIM), 1.0 / HEAD_DIM)), BF16)
    bmap = jnp.asarray(_t5_bucket_map())
    tile8 = lambda g: jnp.tile(g, NA_WIDTH // HEAD_DIM).reshape(1, NA_WIDTH)
    tile2 = lambda g: jnp.tile(g, SW_KV_WIDTH // HEAD_DIM).reshape(1, SW_KV_WIDTH)

    st_first, tok = start("ffn1", 0, no_dep)
    t5b = t5_expand(t5_rel_table, bmap, tok, "t5_expand").reshape(SW_STACK, 3 * SW_BLOCK)
    t2_tables = [rpb_expand(_rpb_rows(na_rpb[l]), tok, f"rpb_expand_{l}") for l in range(depth)]
    tables_done = functools.reduce(jnp.add, [t[0, 0, 0:8, :] for t in t2_tables], t5b[0:8, 0:LANES])
    fwd, _ = arrive(st_first, "ffn1", 0, tables_done)
    st_win, dep = start("win", 0, t5b)
    (first,) = finish(fwd, "ffn1", 0, dep)

    saved = []
    layer_w = {0: dict(wg1=(first, 0), wu1=(first, 1), wd1=(first, 2))}
    cur = xs
    for l in range(depth):
        sv = {}
        lw = layer_w[l]
        sv["x0"] = cur
        cur, sv["xn1"], sv["hg1"], sv["hu1"], sv["act1"] = ffn_forward(
            cur, ffn1_norm[l][None], lw["wg1"], lw["wu1"], lw["wd1"], dep, f"ffn1_{l}")
        sv["x1"] = cur
        fwd, _ = arrive(st_win, "win", l, cur)
        st_rest, tok = start("rest", l, cur)
        (zb,) = finish(fwd, "win", l, tok)
        lw["win"] = (zb, 0)
        sv["gains"] = (tile8(na_q_norm[l]), tile8(na_k_norm[l]), tile8(sw_q_norm[l]), tile2(sw_k_norm[l]))
        sv["hn"], sv["zq"], sv["qa"], sv["ka"], sv["qs"], sv["ks"], sv["gt"] = mix_in(
            cur, mix_norm[l][None], lw["win"], b_gate[l][None], *sv["gains"], bd, f"mix_in_{l}")
        sv["t2"] = t2_tables[l]
        sv["o_na"] = na_fwd(sv["qa"], sv["ka"], sv["zq"], sv["t2"], f"na_fwd_{l}")
        dep = no_dep
        if l + 1 < depth:
            st_ffn1, dep = start("ffn1", l + 1, sv["o_na"])
        sv["o_sw"] = sw_fwd(sv["qs"], sv["ks"], sv["zq"], t5b, sw_sink[l], dep, f"sw_fwd_{l}")
        fwd, tok = arrive(st_rest, "rest", l, sv["o_sw"][0:8, 0:LANES] + sv["o_na"][0:8, 0:LANES])
        za, zc, zd = finish(fwd, "rest", l, tok)
        lw.update(wg2=(za, 0), wu2=(za, 1), wd2=(za, 2), wout=(zc, 0), wna=(zd, 0), wsw=(zd, 1))
        cur, sv["a_na"], sv["a_sw"], sv["merged"] = merge_out(
            cur, sv["o_na"], sv["o_sw"], sv["gt"], lw["wna"], lw["wsw"], lw["wout"], f"merge_out_{l}")
        sv["x2"] = cur
        if l + 1 < depth:
            st_win, dep = start("win", l + 1, cur)
            sv["xn2"], sv["hg2"], sv["hu2"], sv["act2"] = ffn_forward(
                cur, ffn2_norm[l][None], lw["wg2"], lw["wu2"], None, dep, f"ffn2_up_{l}")
            fwd, dep = arrive(st_ffn1, "ffn1", l + 1, sv["act2"])
            cur = ffn_down(cur, sv["act2"], lw["wd2"], dep, f"ffn2_down_{l}")
            (za,) = finish(fwd, "ffn1", l + 1, cur)
            layer_w[l + 1] = dict(wg1=(za, 0), wu1=(za, 1), wd1=(za, 2))
        else:
            dx, loss_acc, sv["xn2"], sv["hg2"], sv["hu2"], sv["act2"] = ffn_forward(
                cur, ffn2_norm[l][None], lw["wg2"], lw["wu2"], lw["wd2"], no_dep, f"ffn2_{l}",
                target=loss_target[0])
        dep = no_dep
        saved.append(sv)

    split = lambda t: t.reshape(N_DEV, t.shape[0] // N_DEV, t.shape[1])
    pending = {}
    last_key = "ffn1_0"
    two_level = {last_key}
    small = {k: [None] * depth for k in SMALL_NAMES if k != "t5_rel_table"}
    dbias_sw = []
    for l in reversed(range(depth)):
        sv = saved[l]
        lw = layer_w[l]
        wg1, wu1, wd1, wg2, wu2, wd2 = (lw[k] for k in ("wg1", "wu1", "wd1", "wg2", "wu2", "wd2"))
        win_t, wout_l, wna_t, wsw_t = lw["win"], lw["wout"], lw["wna"], lw["wsw"]
        blocks = ((2, "x2", "xn2", "hg2", "hu2", "act2", wg2, wu2, wd2, "ffn2_norm", 3),
                  (1, "x0", "xn1", "hg1", "hu1", "act1", wg1, wu1, wd1, "ffn1_norm", 0))

        def ffn_backward(dx, blk):
            tag, xk, xnk, hgk, huk, actk, wg, wu, wd, norm_name, slot = blk
            gains = weights[norm_name]
            dxb, dhg, dhu = ffn_bwd_act(dx, wd, sv[hgk], sv[huk], f"ffn{tag}_bwd_act_{l}")
            gwg, gwu, gwd = tn_matmul([(dhg, sv[xnk], 1.0), (dhu, sv[xnk], 1.0), (sv[actk], dxb, 0.5)],
                                      f"ffn{tag}_dw_{l}")
            key = f"ffn{tag}_{l}"
            blocks_of = [split(gwg), split(gwu), split(gwd)]
            if key in two_level:
                paired, token = pair_start(blocks_of, dxb, f"pair_{key}")
            else:
                pending[key], token = scatter_start([blocks_of], f"scatter_{key}")
            dx, dg = proj_bwd_norm([dhg, dhu], [wg, wu], sv[xk], gains[l][None], dx, token, f"ffn{tag}_bwd_x_{l}")
            token = no_dep
            if key in two_level:
                thru, land = pair_wait(paired, dx, f"pair_{key}_wait")
                pending[key], token = chip_start(pair_sum(thru, land, f"pair_sum_{key}"), dg, f"chips_{key}")
            small[norm_name][l] = dg[0]
            return dx, token

        dx, token = ffn_backward(dx, blocks[0])
        dxb, dzg, da_na, da_sw, do_na, do_sw, dbg = mix_bwd_out(
            dx, sv["gt"], sv["a_na"], sv["a_sw"], wna_t, wsw_t, wout_l, token, f"mix_bwd_out_{l}")
        small["b_gate"][l] = dbg[0]
        gwout, gwna, gwsw = tn_matmul([(sv["merged"], dxb, 1.0), (da_na, sv["o_na"], 1.0), (da_sw, sv["o_sw"], 1.0)],
                                      f"mix_dw_{l}")
        dqa, dka, dva, dt2 = na_bwd(sv["qa"], sv["ka"], sv["zq"], sv["t2"], sv["o_na"], do_na, f"na_bwd_{l}")
        dqs, dks, dvs, dbias, dsink = sw_bwd(sv["qs"], sv["ks"], sv["zq"], t5b, sw_sink[l], sv["o_sw"], do_sw,
                                             f"sw_bwd_{l}")
        dbias_sw.append(dbias.reshape(SW_HEADS, SW_BLOCK, 3 * SW_BLOCK))
        small["sw_sink"][l] = jnp.sum(dsink[:, 0].reshape(SW_HEADS, SW_BLOCK), axis=1)
        small["na_rpb"][l] = _rpb_from_rows(rpb_reduce(dt2, f"rpb_reduce_{l}"))
        dz, dgqa, dgka, dgqs, dgks = qk_norm_bwd(dqa, dka, dva, dqs, dks, dvs, sv["zq"], dzg, *sv["gains"], bd,
                                                 f"qk_norm_bwd_{l}")
        fold = lambda g: jnp.sum(g.reshape(-1, HEAD_DIM), axis=0)
        small["na_q_norm"][l], small["na_k_norm"][l] = fold(dgqa), fold(dgka)
        small["sw_q_norm"][l], small["sw_k_norm"][l] = fold(dgqs), fold(dgks)
        (gwin,) = tn_matmul([(dz, sv["hn"], 1.0)], f"dwin_{l}")
        pending[f"mix_{l}"], token = scatter_start([[split(gwout)], [split(gwna), split(gwsw)], [split(gwin)]],
                                                   f"scatter_mix_{l}")
        dx, dg = proj_bwd_norm([dz], [win_t], sv["x1"], mix_norm[l][None], dx, token, f"mix_bwd_x_{l}")
        small["mix_norm"][l] = dg[0]
        dx, tail = ffn_backward(dx, blocks[1])

    dtab = t5_reduce(dbias_sw, bmap, "t5_reduce")
    small_parts = {k: jnp.stack(v) for k, v in small.items()}
    small_parts["t5_rel_table"] = jnp.transpose(dtab[:, :, 0])

    grads, delta, new_m, new_v = {}, {}, {}, {}
    state = {}
    sharing, token = share_start([small_parts[k] for k in SMALL_NAMES] + [loss_acc], tail, "share_small")
    chain = [token]
    members = {"ffn": lambda t: [(f"ffn{t}_w_gate", 0, 0, True), (f"ffn{t}_w_up", 0, 1, True),
                                 (f"ffn{t}_w_down", 0, 2, False)],
               "mix": lambda t: [("w_out", 0, 0, False), ("w_branch_na", 1, 0, True), ("w_branch_sw", 1, 1, True),
                                 ("w_in", 2, 0, True)]}

    def collect(key):
        if key in two_level:
            zones = [chip_wait(pending[key], chain[0], f"wait_{key}")]
        else:
            zones = scatter_wait(pending[key], chain[0], f"wait_{key}")
        kind, l = key.split("_")
        group = members[kind[:3]](kind[3:])
        complete = all(f"{kind}_{j}" in done for j in range(depth) if j != int(l))
        for zi, zone in enumerate(zones):
            mine = sorted((wi, k, transposed) for k, z, wi, transposed in group if z == zi)
            views = [tr if transposed else (lambda t: t) for _, _, transposed in mine]
            items = [(view(weights[k]), view(mom_m[k]), view(mom_v[k]), state.get(k))
                     for (_, k, _), view in zip(mine, views)]
            results = adamw_layer(zone, int(l), items, chain[0], f"adamw_{key}_{zi}")
            chain[0] = results[-1][1]
            for (_, k, _), view, res in zip(mine, views, results):
                state[k] = res
                if complete:
                    grads[k], delta[k], new_m[k], new_v[k] = (view(t) for t in res)
        done.add(key)

    done = set()
    for key in pending:
        if key != last_key:
            collect(key)
    collect(last_key)
    *recvs, all_losses = share_wait(sharing, chain[0], "share_small_wait")
    loss = jnp.sum(all_losses) * (0.5 / d)
    results = adamw_small([weights[k] for k in SMALL_NAMES], recvs, [mom_m[k] for k in SMALL_NAMES],
                          [mom_v[k] for k in SMALL_NAMES], "adamw_small")
    for dst, outs in zip((grads, delta, new_m, new_v), results):
        dst.update(dict(zip(SMALL_NAMES, outs)))

    return (loss, dx[None], *[grads[k] for k in order], *[delta[k] for k in order],
            *[new_m[k] for k in order], *[new_v[k] for k in order])
```

```python
import functools
import math

import numpy as np
import jax
import jax.numpy as jnp
from jax import lax
from jax.experimental import pallas as pl
from jax.experimental.pallas import tpu as pltpu

F32 = jnp.float32
BF16 = jnp.bfloat16
MESH = pl.DeviceIdType.MESH

N_DEV = 8
EPS = 1e-6
NEG = -1e30
HEAD_DIM = 64
GRID_W = 64
NA_ROWS = 8
NA_COLS = 16
NA_WIDTH = 512
SW_Q_WIDTH = 512
SW_KV_WIDTH = 128
SW_BLOCK = 128
SW_HEADS = 8
SW_REP = 4
REL_BUCKETS = 32
REL_MAX_DIST = 128
QKV_WIDTH = 3 * NA_WIDTH + SW_Q_WIDTH + 2 * SW_KV_WIDTH
SCALE = 1.0 / math.sqrt(HEAD_DIM)

ADAM_LR = 0.001
ADAM_B1 = 0.9
ADAM_B2 = 0.999
ADAM_EPS = 1e-08
ADAM_WD = 0.01
ADAM_STEP = 10

V7X_VMEM_LIMIT = 56 * 1024 * 1024
LANES = 128
MXU_TILE = 256

NT = (((1,), (1,)), ((), ()))
TN = (((0,), (0,)), ((), ()))


def _params(n_grid=1):
    return pltpu.CompilerParams(dimension_semantics=("arbitrary",) * n_grid,
                                vmem_limit_bytes=V7X_VMEM_LIMIT)


def _row_tile(s):
    for t in (512, 256, 128, 64, 32, 16, 8):
        if s % t == 0:
            return t
    raise ValueError(s)


def _tn_tile(n):
    best = max(t for t in range(LANES, min(n, 2304) + 1, LANES) if n % t == 0) if n % LANES == 0 else n
    return best // 2 if best == n and n >= 1024 else best


ONCE = pl.Buffered(1)


def _col_chunk(n):
    return MXU_TILE if n % MXU_TILE == 0 else n


def _dot(a, b):
    return jnp.dot(a, b, preferred_element_type=F32)


def _dotg(a, b, dn):
    return lax.dot_general(a, b, dn, preferred_element_type=F32)


def _sigmoid(v):
    return 1.0 / (1.0 + jnp.exp(-v))


def _rstd(xv):
    return lax.rsqrt(jnp.mean(xv * xv, axis=-1, keepdims=True) + EPS)


def _full(shape):
    nd = len(shape)
    return pl.BlockSpec(shape, lambda i, _n=nd: (0,) * _n)


def _rows(tm, width):
    return pl.BlockSpec((tm, width), lambda i: (i, 0))


def _mat(stack, idx):
    return pl.BlockSpec((None,) + tuple(stack.shape[1:]), lambda i, _w=idx: (_w, 0, 0), pipeline_mode=ONCE)


def _group_mean(v, bd):
    w = bd.shape[0]
    if v.shape[1] > w:
        return jnp.concatenate([_group_mean(v[:, c0:c0 + w], bd) for c0 in range(0, v.shape[1], w)], axis=1)
    hi = v.astype(BF16)
    lo = (v - hi.astype(F32)).astype(BF16)
    return _dot(hi, bd) + _dot(lo, bd)


def _loss_tile(y, t_ref, dy_ref, acc_ref):
    tm, d = y.shape

    @pl.when(pl.program_id(0) == 0)
    def _():
        acc_ref[...] = jnp.zeros(acc_ref.shape, F32)

    err = y - t_ref[...]
    dy_ref[...] = err * (1.0 / d)
    part = jnp.sum((err * err).reshape(tm // 8, 8, d), axis=0)
    acc = part[:, 0:LANES]
    for c0 in range(LANES, d, LANES):
        acc = acc + part[:, c0:c0 + LANES]
    acc_ref[...] = acc_ref[...] + acc


def ffn_forward(x, gain, wg_t, wu_t, wd, dep, name, target=None):
    s, d = x.shape
    f = wg_t[0].shape[1]
    tm = _row_tile(s) if wd is None else min(_row_tile(s), 256)
    fc = _col_chunk(f)
    nw = 2 if wd is None else 3
    n_in = nw + (1 if target is None else 2)

    def body(x_ref, g_ref, *refs):
        w_refs, outs = refs[:nw], refs[n_in:]
        xn_ref, dg_ref, du_ref, act_ref = outs[-4:]
        xv = x_ref[...]
        xn = (xv * _rstd(xv) * g_ref[...]).astype(BF16)
        xn_ref[...] = xn
        for c0 in range(0, f, fc):
            hg = _dotg(xn, w_refs[0][c0:c0 + fc, :], NT)
            hu = _dotg(xn, w_refs[1][c0:c0 + fc, :], NT)
            sg = _sigmoid(hg)
            silu = hg * sg
            du_ref[:, c0:c0 + fc] = silu.astype(BF16)
            dg_ref[:, c0:c0 + fc] = (hu * (sg + silu * (1.0 - sg))).astype(BF16)
            act_ref[:, c0:c0 + fc] = (silu * hu).astype(BF16)
        if wd is not None:
            y = xv + 0.5 * _dot(act_ref[...], w_refs[2][...])
            if target is None:
                outs[0][...] = y
            else:
                _loss_tile(y, refs[nw + 1], outs[0], outs[1])

    weights = [wg_t, wu_t] + ([] if wd is None else [wd])
    in_specs = [_rows(tm, d), _full((1, d))] + [_mat(*w) for w in weights] + [_full(dep.shape)]
    operands = [x, gain, *[w[0] for w in weights], dep]
    out_specs = [_rows(tm, d), _rows(tm, f), _rows(tm, f), _rows(tm, f)]
    out_shape = [jax.ShapeDtypeStruct((s, d), BF16)] + [jax.ShapeDtypeStruct((s, f), BF16)] * 3
    if wd is not None:
        out_specs, out_shape = [_rows(tm, d)] + out_specs, [jax.ShapeDtypeStruct((s, d), F32)] + out_shape
    if target is not None:
        in_specs, operands = in_specs + [_rows(tm, d)], operands + [target]
        out_specs = out_specs[:1] + [_full((8, LANES))] + out_specs[1:]
        out_shape = out_shape[:1] + [jax.ShapeDtypeStruct((8, LANES), F32)] + out_shape[1:]
    return pl.pallas_call(
        body, name=name, grid=(s // tm,), in_specs=in_specs, out_specs=out_specs, out_shape=out_shape,
        compiler_params=_params(),
    )(*operands)


def ffn_down(x, act, wd, dep, name):
    s, d = x.shape
    f = act.shape[1]
    tm = _row_tile(s)

    def body(x_ref, a_ref, w_ref, dep_ref, o_ref):
        o_ref[...] = x_ref[...] + 0.5 * _dot(a_ref[...], w_ref[...])

    return pl.pallas_call(
        body, name=name, grid=(s // tm,),
        in_specs=[_rows(tm, d), _rows(tm, f), _mat(*wd), _full(dep.shape)],
        out_specs=_rows(tm, d),
        out_shape=jax.ShapeDtypeStruct((s, d), F32),
        compiler_params=_params(),
    )(x, act, wd[0], dep)


def mix_in(x, gain, win_t, b_gate, gq_na, gk_na, gq_sw, gk_sw, bd, name):
    s, d = x.shape
    tm = _row_tile(s)
    gc = _col_chunk(2 * d)

    def body(x_ref, g_ref, w_ref, b_ref, gqa_ref, gka_ref, gqs_ref, gks_ref, bd_ref,
             hn_ref, zq_ref, qa_ref, ka_ref, qs_ref, ks_ref, gt_ref):
        xv = x_ref[...]
        hn = (xv * _rstd(xv) * g_ref[...]).astype(BF16)
        hn_ref[...] = hn

        def proj(c0, c1):
            return _dotg(hn, w_ref[c0:c1, :], NT)

        def headnorm(z, g, bdm):
            return z * lax.rsqrt(_group_mean(z * z, bdm) + EPS) * g

        bd512 = bd_ref[...]
        bd128 = bd_ref[0:SW_KV_WIDTH, 0:SW_KV_WIDTH]
        z = proj(0, 512)
        zq_ref[:, 0:512] = z.astype(BF16)
        qa_ref[...] = (headnorm(z, gqa_ref[...], bd512) * SCALE).astype(BF16)
        z = proj(512, 1024)
        zq_ref[:, 512:1024] = z.astype(BF16)
        ka_ref[...] = headnorm(z, gka_ref[...], bd512).astype(BF16)
        z = proj(1024, 1536)
        zq_ref[:, 1024:1536] = z.astype(BF16)
        z = proj(1536, 2048)
        zq_ref[:, 1536:2048] = z.astype(BF16)
        qs_ref[...] = (headnorm(z, gqs_ref[...], bd512) * SCALE).astype(BF16)
        z = proj(2048, 2176)
        zq_ref[:, 2048:2176] = z.astype(BF16)
        ks_ref[...] = headnorm(z, gks_ref[...], bd128).astype(BF16)
        z = proj(2176, 2304)
        zq_ref[:, 2176:2304] = z.astype(BF16)
        for c0 in range(0, 2 * d, gc):
            zg = proj(QKV_WIDTH + c0, QKV_WIDTH + c0 + gc) + b_ref[:, c0:c0 + gc]
            gt_ref[:, c0:c0 + gc] = _sigmoid(zg).astype(BF16)

    return pl.pallas_call(
        body, name=name, grid=(s // tm,),
        in_specs=[_rows(tm, d), _full((1, d)), _mat(*win_t), _full((1, 2 * d)),
                  _full((1, 512)), _full((1, 512)), _full((1, 512)), _full((1, 128)), _full((MXU_TILE, MXU_TILE))],
        out_specs=[_rows(tm, d), _rows(tm, QKV_WIDTH), _rows(tm, 512), _rows(tm, 512), _rows(tm, 512),
                   _rows(tm, 128), _rows(tm, 2 * d)],
        out_shape=[jax.ShapeDtypeStruct((s, d), BF16), jax.ShapeDtypeStruct((s, QKV_WIDTH), BF16),
                   jax.ShapeDtypeStruct((s, 512), BF16), jax.ShapeDtypeStruct((s, 512), BF16),
                   jax.ShapeDtypeStruct((s, 512), BF16), jax.ShapeDtypeStruct((s, 128), BF16),
                   jax.ShapeDtypeStruct((s, 2 * d), BF16)],
        compiler_params=_params(),
    )(x, gain, win_t[0], b_gate, gq_na, gk_na, gq_sw, gk_sw, bd)


def _na_iotas():
    qc = lax.broadcasted_iota(jnp.int32, (GRID_W, LANES), 0)
    ln = lax.broadcasted_iota(jnp.int32, (GRID_W, LANES), 1)
    low = ln < GRID_W
    kc = jnp.where(low, ln, ln - GRID_W)
    diff = kc - qc + (NA_COLS - 1)
    qcs = jnp.clip(qc - NA_COLS // 2, 0, GRID_W - NA_COLS)
    inwin = (kc >= qcs) & (kc < qcs + NA_COLS)
    return diff, low, inwin


NA_RI = 2 * NA_ROWS - 1
NA_CI = 2 * NA_COLS - 1
NA_T2 = NA_RI + 1


def _rpb_rows(rpb):
    h = rpb.shape[0]
    padded = jnp.pad(rpb, ((0, 0), (1, 1), (0, GRID_W - NA_CI)))
    return jnp.concatenate([padded[:, :NA_T2], padded[:, 1:NA_T2 + 1]], axis=2).reshape(h, NA_T2, LANES)


def _rpb_from_rows(rows):
    return rows[:, 1:, :NA_CI] + rows[:, :NA_RI, GRID_W:GRID_W + NA_CI]


def rpb_expand(rows, dep, name):
    n_heads = rows.shape[0]

    def body(r_ref, dep_ref, o_ref):
        for h in range(n_heads):
            for e in range(NA_T2):
                line = jnp.broadcast_to(r_ref[h, e:e + 1, :], (GRID_W, LANES))
                o_ref[h, e] = pltpu.roll(line, LANES - (NA_COLS - 1), 1, stride=1, stride_axis=0)

    return pl.pallas_call(
        body, name=name,
        in_specs=[pl.BlockSpec(memory_space=pltpu.VMEM), pl.BlockSpec(memory_space=pltpu.VMEM)],
        out_specs=pl.BlockSpec(memory_space=pltpu.VMEM),
        out_shape=jax.ShapeDtypeStruct((n_heads, NA_T2, GRID_W, LANES), F32),
        compiler_params=pltpu.CompilerParams(vmem_limit_bytes=V7X_VMEM_LIMIT),
    )(rows, dep)


def rpb_reduce(dt2, name):
    n_heads = dt2.shape[0]
    flip = jnp.asarray(np.eye(GRID_W)[::-1], BF16)

    def body(d_ref, j_ref, o_ref):
        jm = j_ref[...]
        for h in range(n_heads):
            for e in range(NA_T2):
                dv = d_ref[h, e]
                hi = dv.astype(BF16)
                mid = (dv - hi.astype(F32)).astype(BF16)
                lo = (dv - hi.astype(F32) - mid.astype(F32)).astype(BF16)
                rev = _dot(jm, hi) + _dot(jm, mid) + _dot(jm, lo)
                back = pltpu.roll(rev, LANES + (NA_COLS - 1) - (GRID_W - 1), 1, stride=1, stride_axis=0)
                o_ref[h, e:e + 1, :] = jnp.sum(back, axis=0, keepdims=True)

    return pl.pallas_call(
        body, name=name,
        in_specs=[pl.BlockSpec(memory_space=pltpu.VMEM)] * 2,
        out_specs=pl.BlockSpec(memory_space=pltpu.VMEM),
        out_shape=jax.ShapeDtypeStruct((n_heads, NA_T2, LANES), F32),
        compiler_params=pltpu.CompilerParams(vmem_limit_bytes=V7X_VMEM_LIMIT),
    )(dt2, flip)


NA_TQ = 4
NA_TK = NA_TQ + NA_ROWS
NA_KCH = NA_TK // 2


def _na_tile_geometry(t, rows):
    r = t * NA_TQ
    kbase = jnp.clip(r - NA_ROWS // 2, 0, rows - NA_TK)
    starts = [jnp.clip(r + a - NA_ROWS // 2, 0, rows - NA_ROWS) for a in range(NA_TQ)]
    return r, kbase, starts


def _na_tile_mask(kbase, starts, low, inwin):
    half = jnp.where(low, 0, 1)
    cols = []
    for c in range(NA_KCH):
        krow = kbase + 2 * c + half
        cols.append(jnp.concatenate(
            [jnp.where(inwin & (krow >= st) & (krow < st + NA_ROWS), 0.0, NEG) for st in starts], axis=0))
    return jnp.concatenate(cols, axis=1)


def _na_tile_index(r, kbase, a, c):
    return jnp.clip(kbase + 2 * c - (r + a) + NA_ROWS, 0, NA_T2 - 1)


def _na_tile_scores(q, k, t2_ref, hh, r, kbase, madd):
    bias = jnp.concatenate(
        [jnp.concatenate([t2_ref[hh, _na_tile_index(r, kbase, a, c)] for a in range(NA_TQ)], axis=0)
         for c in range(NA_KCH)], axis=1)
    return _dotg(q, k, NT) + bias + madd


def _softmax_rows(sc):
    e = jnp.exp(sc - jnp.max(sc, axis=1, keepdims=True))
    return e * (1.0 / jnp.sum(e, axis=1, keepdims=True))


def na_fwd(qa, ka, zq, t2, name):
    s = qa.shape[0]
    rows = s // GRID_W
    n_pairs = NA_WIDTH // LANES
    v_blk0 = (2 * NA_WIDTH) // LANES

    assert rows % NA_TQ == 0 and rows >= NA_TK
    tq, tk = NA_TQ * GRID_W, NA_TK * GRID_W

    def body(q_ref, k_ref, v_ref, t2_ref, o_ref, s_scr, p_scr):
        _, low, inwin = _na_iotas()

        def tile(t, carry):
            r, kbase, starts = _na_tile_geometry(t, rows)
            madd = _na_tile_mask(kbase, starts, low, inwin)
            qr = pl.ds(pl.multiple_of(r * GRID_W, tq), tq)
            kr = pl.ds(pl.multiple_of(kbase * GRID_W, tq), tk)
            for hh in range(2):
                lanes = slice(HEAD_DIM * hh, HEAD_DIM * (hh + 1))
                s_scr[tq * hh:tq * (hh + 1), :] = _na_tile_scores(q_ref[qr, lanes], k_ref[kr, lanes], t2_ref, hh, r,
                                                                  kbase, madd)
            p_scr[...] = _softmax_rows(s_scr[...]).astype(BF16)
            for hh in range(2):
                lanes = slice(HEAD_DIM * hh, HEAD_DIM * (hh + 1))
                o_ref[qr, lanes] = _dot(p_scr[tq * hh:tq * (hh + 1), :], v_ref[kr, lanes]).astype(BF16)
            return carry

        lax.fori_loop(0, rows // NA_TQ, tile, 0)

    col = lambda off: pl.BlockSpec((s, LANES), lambda p, _o=off: (0, _o + p))
    return pl.pallas_call(
        body, name=name, grid=(n_pairs,),
        in_specs=[col(0), col(0), col(v_blk0),
                  pl.BlockSpec((2, NA_T2, GRID_W, LANES), lambda p: (p, 0, 0, 0))],
        out_specs=col(0),
        out_shape=jax.ShapeDtypeStruct((s, NA_WIDTH), BF16),
        scratch_shapes=[pltpu.VMEM((2 * tq, tk), F32), pltpu.VMEM((2 * tq, tk), BF16)],
        compiler_params=_params(),
    )(qa, ka, zq, t2)


def na_bwd(qa, ka, zq, t2, o_na, do_na, name):
    s = qa.shape[0]
    rows = s // GRID_W
    n_pairs = NA_WIDTH // LANES
    v_blk0 = (2 * NA_WIDTH) // LANES

    tq, tk = NA_TQ * GRID_W, NA_TK * GRID_W

    def body(q_ref, k_ref, v_ref, t2_ref, o_ref, do_ref, dq_ref, dk_ref, dv_ref, dt2_ref):
        _, low, inwin = _na_iotas()
        dk_ref[...] = jnp.zeros(dk_ref.shape, F32)
        dv_ref[...] = jnp.zeros(dv_ref.shape, F32)
        dt2_ref[...] = jnp.zeros(dt2_ref.shape, F32)

        def tile(t, carry):
            r, kbase, starts = _na_tile_geometry(t, rows)
            madd = _na_tile_mask(kbase, starts, low, inwin)
            qr = pl.ds(pl.multiple_of(r * GRID_W, tq), tq)
            kr = pl.ds(pl.multiple_of(kbase * GRID_W, tq), tk)
            for hh in range(2):
                lanes = slice(HEAD_DIM * hh, HEAD_DIM * (hh + 1))
                q, k, v = q_ref[qr, lanes], k_ref[kr, lanes], v_ref[kr, lanes]
                p = _softmax_rows(_na_tile_scores(q, k, t2_ref, hh, r, kbase, madd))
                do = do_ref[qr, lanes]
                delta = jnp.sum(do.astype(F32) * o_ref[qr, lanes].astype(F32), axis=1, keepdims=True)
                ds = p * (_dotg(do, v, NT) - delta)
                shared = {}
                for a in range(NA_TQ):
                    for c in range(NA_KCH):
                        shared.setdefault(2 * c - a, []).append(
                            ds[GRID_W * a:GRID_W * (a + 1), LANES * c:LANES * (c + 1)])
                for offset, parts in shared.items():
                    e = jnp.clip(offset + kbase - r + NA_ROWS, 0, NA_T2 - 1)
                    dt2_ref[hh, e] = dt2_ref[hh, e] + functools.reduce(jnp.add, parts)
                dsb = ds.astype(BF16)
                dq_ref[qr, lanes] = _dot(dsb, k)
                dk_ref[kr, lanes] = dk_ref[kr, lanes] + _dotg(dsb, q, TN)
                dv_ref[kr, lanes] = dv_ref[kr, lanes] + _dotg(p.astype(BF16), do, TN)
            return carry

        lax.fori_loop(0, rows // NA_TQ, tile, 0)

    col = lambda off: pl.BlockSpec((s, LANES), lambda p, _o=off: (0, _o + p))
    t2spec = pl.BlockSpec((2, NA_T2, GRID_W, LANES), lambda p: (p, 0, 0, 0))
    return pl.pallas_call(
        body, name=name, grid=(n_pairs,),
        in_specs=[col(0), col(0), col(v_blk0), t2spec, col(0), col(0)],
        out_specs=[col(0), col(0), col(0), t2spec],
        out_shape=[jax.ShapeDtypeStruct((s, NA_WIDTH), F32)] * 3 + [jax.ShapeDtypeStruct(t2.shape, F32)],
        compiler_params=_params(),
    )(qa, ka, zq, t2, o_na, do_na)


def _t5_bucket_map():
    rel = np.arange(3 * SW_BLOCK)[None, :] - SW_BLOCK - np.arange(SW_BLOCK)[:, None]
    nb = REL_BUCKETS // 2
    max_exact = nb // 2
    n = np.abs(rel)
    large = max_exact + (np.log(np.maximum(n, 1) / max_exact)
                         / np.log(REL_MAX_DIST / max_exact) * (nb - max_exact)).astype(np.int32)
    large = np.minimum(large, nb - 1)
    return ((rel > 0) * nb + np.where(n < max_exact, n, large)).astype(np.int32)


def t5_expand(table, bmap, dep, name):
    def body(tab_ref, bm_ref, dep_ref, o_ref):
        bm = bm_ref[...]
        for h in range(SW_HEADS):
            t = jnp.zeros(bm.shape, F32)
            for b in range(REL_BUCKETS):
                t = jnp.where(bm == b, tab_ref[b, h], t)
            o_ref[h] = t

    return pl.pallas_call(
        body, name=name,
        in_specs=[pl.BlockSpec(memory_space=pltpu.SMEM), pl.BlockSpec(memory_space=pltpu.VMEM),
                  pl.BlockSpec(memory_space=pltpu.VMEM)],
        out_specs=pl.BlockSpec(memory_space=pltpu.VMEM),
        out_shape=jax.ShapeDtypeStruct((SW_HEADS,) + bmap.shape, F32),
        compiler_params=pltpu.CompilerParams(vmem_limit_bytes=V7X_VMEM_LIMIT),
    )(table, bmap, dep)


def t5_reduce(dbias_list, bmap, name):
    n = len(dbias_list)

    def body(*refs):
        d_refs, bm_ref, o_ref = refs[:n], refs[n], refs[n + 1]
        bm = bm_ref[...]
        for h in range(SW_HEADS):
            dv = d_refs[0][h]
            for other in d_refs[1:]:
                dv = dv + other[h]
            rows = [jnp.sum(jnp.where(bm == b, dv, 0.0), axis=0, keepdims=True) for b in range(REL_BUCKETS)]
            r = jnp.concatenate(rows, axis=0)
            o_ref[h] = jnp.broadcast_to(jnp.sum(r, axis=1, keepdims=True), (REL_BUCKETS, LANES))

    return pl.pallas_call(
        body, name=name,
        in_specs=[pl.BlockSpec(memory_space=pltpu.VMEM)] * (n + 1),
        out_specs=pl.BlockSpec(memory_space=pltpu.VMEM),
        out_shape=jax.ShapeDtypeStruct((SW_HEADS, REL_BUCKETS, LANES), F32),
        compiler_params=pltpu.CompilerParams(vmem_limit_bytes=V7X_VMEM_LIMIT),
    )(*dbias_list, bmap)


def _sw_mask_iotas():
    a = lax.broadcasted_iota(jnp.int32, (SW_BLOCK, 3 * SW_BLOCK), 0)
    j = lax.broadcasted_iota(jnp.int32, (SW_BLOCK, 3 * SW_BLOCK), 1)
    inwin = jnp.abs(j - SW_BLOCK - a) <= SW_BLOCK
    return j, inwin


SW_STACK = SW_HEADS * SW_BLOCK


def _sw_softmax(sc, sk):
    m = jnp.maximum(jnp.max(sc, axis=1, keepdims=True), sk)
    e = jnp.exp(sc - m)
    es = jnp.exp(sk - m)
    inv = 1.0 / (jnp.sum(e, axis=1, keepdims=True) + es)
    return e * inv, es * inv


def _sw_prologue(k_ref, v_ref, kp, vp, sink_ref, s):
    pad = s + 2 * SW_BLOCK
    zeros = jnp.zeros((SW_BLOCK, SW_KV_WIDTH), BF16)
    kp[0:SW_BLOCK, :] = zeros
    vp[0:SW_BLOCK, :] = zeros
    kp[SW_BLOCK + s:pad, :] = zeros
    vp[SW_BLOCK + s:pad, :] = zeros
    kp[SW_BLOCK:SW_BLOCK + s, :] = k_ref[...]
    vp[SW_BLOCK:SW_BLOCK + s, :] = v_ref[...]
    return jnp.concatenate([jnp.full((SW_BLOCK, 1), sink_ref[h], F32) for h in range(SW_HEADS)], axis=0)


def sw_fwd(qs, ks, zq, t5b, sink, dep, name):
    s = qs.shape[0]
    nb = s // SW_BLOCK
    v_blk = (3 * NA_WIDTH + SW_Q_WIDTH + SW_KV_WIDTH) // LANES
    pad = s + 2 * SW_BLOCK

    def body(q_ref, k_ref, v_ref, b_ref, sink_ref, dep_ref, o_ref, kp, vp, s_scr, p_scr):
        sink_col = _sw_prologue(k_ref, v_ref, kp, vp, sink_ref, s)
        j, inwin = _sw_mask_iotas()

        def blk(n, carry):
            kpos = n * SW_BLOCK - SW_BLOCK + j
            madd = jnp.where(inwin & (kpos >= 0) & (kpos < s), 0.0, NEG)
            q0 = pl.multiple_of(n * SW_BLOCK, SW_BLOCK)
            qr, kr = pl.ds(q0, SW_BLOCK), pl.ds(q0, 3 * SW_BLOCK)
            for h in range(SW_HEADS):
                g = h // SW_REP
                s_scr[SW_BLOCK * h:SW_BLOCK * (h + 1), :] = _dotg(
                    q_ref[qr, HEAD_DIM * h:HEAD_DIM * (h + 1)], kp[kr, HEAD_DIM * g:HEAD_DIM * (g + 1)], NT) + madd
            p, _ = _sw_softmax(s_scr[...] + b_ref[...], sink_col)
            p_scr[...] = p.astype(BF16)
            for h in range(SW_HEADS):
                g = h // SW_REP
                o_ref[qr, HEAD_DIM * h:HEAD_DIM * (h + 1)] = _dot(
                    p_scr[SW_BLOCK * h:SW_BLOCK * (h + 1), :], vp[kr, HEAD_DIM * g:HEAD_DIM * (g + 1)]).astype(BF16)
            return carry

        lax.fori_loop(0, nb, blk, 0)

    return pl.pallas_call(
        body, name=name, grid=(1,),
        in_specs=[_full((s, SW_Q_WIDTH)), _full((s, SW_KV_WIDTH)),
                  pl.BlockSpec((s, SW_KV_WIDTH), lambda i: (0, v_blk)),
                  _full((SW_STACK, 3 * SW_BLOCK)), pl.BlockSpec(memory_space=pltpu.SMEM),
                  _full(dep.shape)],
        out_specs=_full((s, SW_Q_WIDTH)),
        out_shape=jax.ShapeDtypeStruct((s, SW_Q_WIDTH), BF16),
        scratch_shapes=[pltpu.VMEM((pad, SW_KV_WIDTH), BF16), pltpu.VMEM((pad, SW_KV_WIDTH), BF16),
                        pltpu.VMEM((SW_STACK, 3 * SW_BLOCK), F32), pltpu.VMEM((SW_STACK, 3 * SW_BLOCK), BF16)],
        compiler_params=_params(),
    )(qs, ks, zq, t5b, sink, dep)


def sw_bwd(qs, ks, zq, t5b, sink, o_sw, do_sw, name):
    s = qs.shape[0]
    nb = s // SW_BLOCK
    v_blk = (3 * NA_WIDTH + SW_Q_WIDTH + SW_KV_WIDTH) // LANES
    pad = s + 2 * SW_BLOCK

    def body(q_ref, k_ref, v_ref, b_ref, sink_ref, o_ref, do_ref,
             dq_ref, dk_ref, dv_ref, db_ref, dsk_ref, kp, vp, dkp, dvp, s_scr, dp_scr, ds_scr, p_scr):
        sink_col = _sw_prologue(k_ref, v_ref, kp, vp, sink_ref, s)
        dkp[...] = jnp.zeros(dkp.shape, F32)
        dvp[...] = jnp.zeros(dvp.shape, F32)
        db_ref[...] = jnp.zeros(db_ref.shape, F32)
        dsk_ref[...] = jnp.zeros(dsk_ref.shape, F32)
        j, inwin = _sw_mask_iotas()

        def blk(n, carry):
            kpos = n * SW_BLOCK - SW_BLOCK + j
            madd = jnp.where(inwin & (kpos >= 0) & (kpos < s), 0.0, NEG)
            q0 = pl.multiple_of(n * SW_BLOCK, SW_BLOCK)
            qr, kr = pl.ds(q0, SW_BLOCK), pl.ds(q0, 3 * SW_BLOCK)
            deltas = []
            for h in range(SW_HEADS):
                g = h // SW_REP
                hl, kl = slice(HEAD_DIM * h, HEAD_DIM * (h + 1)), slice(HEAD_DIM * g, HEAD_DIM * (g + 1))
                rows = slice(SW_BLOCK * h, SW_BLOCK * (h + 1))
                do = do_ref[qr, hl]
                s_scr[rows, :] = _dotg(q_ref[qr, hl], kp[kr, kl], NT) + madd
                dp_scr[rows, :] = _dotg(do, vp[kr, kl], NT)
                deltas.append(jnp.sum(do.astype(F32) * o_ref[qr, hl].astype(F32), axis=1, keepdims=True))
            delta = jnp.concatenate(deltas, axis=0)
            p, ps = _sw_softmax(s_scr[...] + b_ref[...], sink_col)
            ds = p * (dp_scr[...] - delta)
            db_ref[...] = db_ref[...] + ds
            dsk_ref[...] = dsk_ref[...] - jnp.broadcast_to(ps * delta, (SW_STACK, LANES))
            ds_scr[...] = ds.astype(BF16)
            p_scr[...] = p.astype(BF16)
            for g in range(SW_HEADS // SW_REP):
                kl = slice(HEAD_DIM * g, HEAD_DIM * (g + 1))
                k = kp[kr, kl]
                dkw = jnp.zeros((3 * SW_BLOCK, HEAD_DIM), F32)
                dvw = jnp.zeros((3 * SW_BLOCK, HEAD_DIM), F32)
                for r in range(SW_REP):
                    h = g * SW_REP + r
                    hl, rows = slice(HEAD_DIM * h, HEAD_DIM * (h + 1)), slice(SW_BLOCK * h, SW_BLOCK * (h + 1))
                    dsb = ds_scr[rows, :]
                    dq_ref[qr, hl] = _dot(dsb, k)
                    dkw = dkw + _dotg(dsb, q_ref[qr, hl], TN)
                    dvw = dvw + _dotg(p_scr[rows, :], do_ref[qr, hl], TN)
                dkp[kr, kl] = dkp[kr, kl] + dkw
                dvp[kr, kl] = dvp[kr, kl] + dvw
            return carry

        lax.fori_loop(0, nb, blk, 0)
        dk_ref[...] = dkp[SW_BLOCK:SW_BLOCK + s, :]
        dv_ref[...] = dvp[SW_BLOCK:SW_BLOCK + s, :]

    bias_spec = _full((SW_STACK, 3 * SW_BLOCK))
    return pl.pallas_call(
        body, name=name, grid=(1,),
        in_specs=[_full((s, SW_Q_WIDTH)), _full((s, SW_KV_WIDTH)),
                  pl.BlockSpec((s, SW_KV_WIDTH), lambda i: (0, v_blk)),
                  bias_spec, pl.BlockSpec(memory_space=pltpu.SMEM),
                  _full((s, SW_Q_WIDTH)), _full((s, SW_Q_WIDTH))],
        out_specs=[_full((s, SW_Q_WIDTH)), _full((s, SW_KV_WIDTH)), _full((s, SW_KV_WIDTH)), bias_spec,
                   _full((SW_STACK, LANES))],
        out_shape=[jax.ShapeDtypeStruct((s, SW_Q_WIDTH), F32), jax.ShapeDtypeStruct((s, SW_KV_WIDTH), F32),
                   jax.ShapeDtypeStruct((s, SW_KV_WIDTH), F32),
                   jax.ShapeDtypeStruct((SW_STACK, 3 * SW_BLOCK), F32),
                   jax.ShapeDtypeStruct((SW_STACK, LANES), F32)],
        scratch_shapes=[pltpu.VMEM((pad, SW_KV_WIDTH), BF16), pltpu.VMEM((pad, SW_KV_WIDTH), BF16),
                        pltpu.VMEM((pad, SW_KV_WIDTH), F32), pltpu.VMEM((pad, SW_KV_WIDTH), F32),
                        pltpu.VMEM((SW_STACK, 3 * SW_BLOCK), F32), pltpu.VMEM((SW_STACK, 3 * SW_BLOCK), F32),
                        pltpu.VMEM((SW_STACK, 3 * SW_BLOCK), BF16), pltpu.VMEM((SW_STACK, 3 * SW_BLOCK), BF16)],
        compiler_params=_params(),
    )(qs, ks, zq, t5b, sink, o_sw, do_sw)


def merge_out(x, o_na, o_sw, gt, wbna_t, wbsw_t, wout, name):
    s, d = x.shape
    tm = _row_tile(s)

    def body(x_ref, ona_ref, osw_ref, gt_ref, wna_ref, wsw_ref, wo_ref, xo_ref, ana_ref, asw_ref, mg_ref):
        a_na = _dotg(ona_ref[...], wna_ref[...], NT)
        a_sw = _dotg(osw_ref[...], wsw_ref[...], NT)
        g_na, g_sw = gt_ref[:, 0:d].astype(F32), gt_ref[:, d:2 * d].astype(F32)
        ana_ref[...] = (a_na * g_na * (1.0 - g_na)).astype(BF16)
        asw_ref[...] = (a_sw * g_sw * (1.0 - g_sw)).astype(BF16)
        merged = (g_na * a_na + g_sw * a_sw).astype(BF16)
        mg_ref[...] = merged
        xo_ref[...] = x_ref[...] + _dot(merged, wo_ref[...])

    return pl.pallas_call(
        body, name=name, grid=(s // tm,),
        in_specs=[_rows(tm, d), _rows(tm, 512), _rows(tm, 512), _rows(tm, 2 * d),
                  _mat(*wbna_t), _mat(*wbsw_t), _mat(*wout)],
        out_specs=[_rows(tm, d)] * 4,
        out_shape=[jax.ShapeDtypeStruct((s, d), F32)] + [jax.ShapeDtypeStruct((s, d), BF16)] * 3,
        compiler_params=_params(),
    )(x, o_na, o_sw, gt, wbna_t[0], wbsw_t[0], wout[0])


def mix_bwd_out(dx, gt, a_na, a_sw, wbna_t, wbsw_t, wout, dep, name):
    s, d = dx.shape
    tm = _row_tile(s)

    def body(dx_ref, gt_ref, ana_ref, asw_ref, wna_ref, wsw_ref, wo_ref, dep_ref,
             dxb_ref, dzg_ref, dana_ref, dasw_ref, dona_ref, dosw_ref, dbg_ref):
        @pl.when(pl.program_id(0) == 0)
        def _():
            dbg_ref[...] = jnp.zeros(dbg_ref.shape, F32)

        dxb = dx_ref[...].astype(BF16)
        dxb_ref[...] = dxb
        dm = _dotg(dxb, wo_ref[...], NT)
        for i, (a_ref, da_ref, w_ref, do_ref) in enumerate(
                [(ana_ref, dana_ref, wna_ref, dona_ref), (asw_ref, dasw_ref, wsw_ref, dosw_ref)]):
            gi = gt_ref[:, i * d:(i + 1) * d].astype(F32)
            da = (dm * gi).astype(BF16)
            da_ref[...] = da
            do_ref[...] = _dot(da, w_ref[...]).astype(BF16)
            dzg = dm * a_ref[...].astype(F32)
            dzg_ref[:, i * d:(i + 1) * d] = dzg.astype(BF16)
            dbg_ref[:, i * d:(i + 1) * d] = dbg_ref[:, i * d:(i + 1) * d] + jnp.sum(dzg, axis=0, keepdims=True)

    return pl.pallas_call(
        body, name=name, grid=(s // tm,),
        in_specs=[_rows(tm, d), _rows(tm, 2 * d), _rows(tm, d), _rows(tm, d),
                  _mat(*wbna_t), _mat(*wbsw_t), _mat(*wout), _full(dep.shape)],
        out_specs=[_rows(tm, d), _rows(tm, 2 * d), _rows(tm, d), _rows(tm, d), _rows(tm, 512), _rows(tm, 512),
                   _full((1, 2 * d))],
        out_shape=[jax.ShapeDtypeStruct((s, d), BF16), jax.ShapeDtypeStruct((s, 2 * d), BF16),
                   jax.ShapeDtypeStruct((s, d), BF16), jax.ShapeDtypeStruct((s, d), BF16),
                   jax.ShapeDtypeStruct((s, 512), BF16), jax.ShapeDtypeStruct((s, 512), BF16),
                   jax.ShapeDtypeStruct((1, 2 * d), F32)],
        compiler_params=_params(),
    )(dx, gt, a_na, a_sw, wbna_t[0], wbsw_t[0], wout[0], dep)


def qk_norm_bwd(dqa, dka, dva, dqs, dks, dvs, zq, dzg, gq_na, gk_na, gq_sw, gk_sw, bd, name):
    s = zq.shape[0]
    d2 = dzg.shape[1]
    n_in = QKV_WIDTH + d2
    tm = _row_tile(s)

    def body(dqa_ref, dka_ref, dva_ref, dqs_ref, dks_ref, dvs_ref, zq_ref, dzg_ref,
             gqa_ref, gka_ref, gqs_ref, gks_ref, bd_ref, dz_ref, dgqa_ref, dgka_ref, dgqs_ref, dgks_ref):
        @pl.when(pl.program_id(0) == 0)
        def _():
            for r in (dgqa_ref, dgka_ref, dgqs_ref, dgks_ref):
                r[...] = jnp.zeros(r.shape, F32)

        bd512 = bd_ref[...]
        bd128 = bd_ref[0:SW_KV_WIDTH, 0:SW_KV_WIDTH]

        def one(c0, c1, dy_ref, g_ref, dg_ref, bdm, scale):
            z = zq_ref[:, c0:c1].astype(F32)
            r = lax.rsqrt(_group_mean(z * z, bdm) + EPS)
            zh = z * r
            dy = dy_ref[...] * scale
            dyg = dy * g_ref[...]
            dz = r * (dyg - zh * _group_mean(dyg * zh, bdm))
            dz_ref[:, c0:c1] = dz.astype(BF16)
            dg_ref[...] = dg_ref[...] + jnp.sum(dy * zh, axis=0, keepdims=True)

        one(0, 512, dqa_ref, gqa_ref, dgqa_ref, bd512, SCALE)
        one(512, 1024, dka_ref, gka_ref, dgka_ref, bd512, 1.0)
        dz_ref[:, 1024:1536] = dva_ref[...].astype(BF16)
        one(1536, 2048, dqs_ref, gqs_ref, dgqs_ref, bd512, SCALE)
        one(2048, 2176, dks_ref, gks_ref, dgks_ref, bd128, 1.0)
        dz_ref[:, 2176:2304] = dvs_ref[...].astype(BF16)
        dz_ref[:, QKV_WIDTH:n_in] = dzg_ref[...]

    return pl.pallas_call(
        body, name=name, grid=(s // tm,),
        in_specs=[_rows(tm, 512), _rows(tm, 512), _rows(tm, 512), _rows(tm, 512), _rows(tm, 128), _rows(tm, 128),
                  _rows(tm, QKV_WIDTH), _rows(tm, d2),
                  _full((1, 512)), _full((1, 512)), _full((1, 512)), _full((1, 128)), _full((MXU_TILE, MXU_TILE))],
        out_specs=[_rows(tm, n_in), _full((1, 512)), _full((1, 512)), _full((1, 512)), _full((1, 128))],
        out_shape=[jax.ShapeDtypeStruct((s, n_in), BF16)] + [jax.ShapeDtypeStruct((1, 512), F32)] * 3
                  + [jax.ShapeDtypeStruct((1, 128), F32)],
        compiler_params=_params(),
    )(dqa, dka, dva, dqs, dks, dvs, zq, dzg, gq_na, gk_na, gq_sw, gk_sw, bd)


def ffn_bwd_act(dx, wd, hg, hu, name):
    s, d = dx.shape
    f = wd[0].shape[1]
    tm = _row_tile(s)
    fc = _col_chunk(f)

    def body(dx_ref, w_ref, hg_ref, hu_ref, dxb_ref, dhg_ref, dhu_ref):
        dxv = dx_ref[...]
        dxb_ref[...] = dxv.astype(BF16)
        half = (0.5 * dxv).astype(BF16)
        for c0 in range(0, f, fc):
            dact = _dotg(half, w_ref[c0:c0 + fc, :], NT)
            dhu_ref[:, c0:c0 + fc] = (dact * hu_ref[:, c0:c0 + fc].astype(F32)).astype(BF16)
            dhg_ref[:, c0:c0 + fc] = (dact * hg_ref[:, c0:c0 + fc].astype(F32)).astype(BF16)

    return pl.pallas_call(
        body, name=name, grid=(s // tm,),
        in_specs=[_rows(tm, d), _mat(*wd), _rows(tm, f), _rows(tm, f)],
        out_specs=[_rows(tm, d), _rows(tm, f), _rows(tm, f)],
        out_shape=[jax.ShapeDtypeStruct((s, d), BF16), jax.ShapeDtypeStruct((s, f), BF16),
                   jax.ShapeDtypeStruct((s, f), BF16)],
        compiler_params=_params(),
    )(dx, wd[0], hg, hu)


def proj_bwd_norm(acts, weights, x, gain, dx, dep, name):
    s, d = x.shape
    tm = min(_row_tile(s), 256)
    n = len(acts)

    def body(*refs):
        a_refs, w_refs = refs[:n], refs[n:2 * n]
        x_ref, g_ref, dx_ref, _, o_ref, dg_ref = refs[2 * n:]

        @pl.when(pl.program_id(0) == 0)
        def _():
            dg_ref[...] = jnp.zeros(dg_ref.shape, F32)

        dxn = _dot(a_refs[0][...], w_refs[0][...])
        for a_ref, w_ref in zip(a_refs[1:], w_refs[1:]):
            dxn = dxn + _dot(a_ref[...], w_ref[...])
        xv = x_ref[...]
        r = _rstd(xv)
        xh = xv * r
        dxh = dxn * g_ref[...]
        o_ref[...] = dx_ref[...] + r * (dxh - xh * jnp.mean(dxh * xh, axis=-1, keepdims=True))
        dg_ref[...] = dg_ref[...] + jnp.sum(dxn * xh, axis=0, keepdims=True)

    return pl.pallas_call(
        body, name=name, grid=(s // tm,),
        in_specs=[_rows(tm, a.shape[1]) for a in acts] + [_mat(*w) for w in weights]
                 + [_rows(tm, d), _full((1, d)), _rows(tm, d), _full(dep.shape)],
        out_specs=[_rows(tm, d), _full((1, d))],
        out_shape=[jax.ShapeDtypeStruct((s, d), F32), jax.ShapeDtypeStruct((1, d), F32)],
        compiler_params=_params(),
    )(*acts, *[w[0] for w in weights], x, gain, dx, dep)


def tn_matmul(products, name):
    s, n = products[0][0].shape
    tn = _tn_tile(n) if len(products) == 1 else _col_chunk(n)
    rhs = []
    for _, b, _ in products:
        if not any(b is seen for seen in rhs):
            rhs.append(b)
    which = [next(i for i, seen in enumerate(rhs) if b is seen) for _, b, _ in products]
    npr, nr = len(products), len(rhs)

    def body(*refs):
        a_refs, b_refs, o_refs = refs[:npr], refs[npr:npr + nr], refs[npr + nr:]
        for i, (_, _, scale) in enumerate(products):
            o_refs[i][...] = (scale * _dotg(a_refs[i][...], b_refs[which[i]][...], TN)).astype(BF16)

    return pl.pallas_call(
        body, name=name, grid=(n // tn,),
        in_specs=[pl.BlockSpec((s, tn), lambda i: (0, i))] * npr
                 + [pl.BlockSpec(b.shape, lambda i: (0, 0), pipeline_mode=ONCE) for b in rhs],
        out_specs=[pl.BlockSpec((tn, b.shape[1]), lambda i: (i, 0)) for _, b, _ in products],
        out_shape=[jax.ShapeDtypeStruct((n, b.shape[1]), BF16) for _, b, _ in products],
        compiler_params=_params(),
    )(*[a for a, _, _ in products], *rhs)


def _mesh_pos():
    return lax.axis_index("x"), lax.axis_index("y"), lax.axis_index("c")


def _peers():
    x, y, c = _mesh_pos()
    peers = []
    for rel in range(1, N_DEV):
        peers.append((1 - x if rel & 4 else x, 1 - y if rel & 2 else y, 1 - c if rel & 1 else c))
    return 4 * x + 2 * y + c, peers


HBM_SPEC = pl.BlockSpec(memory_space=pltpu.HBM)
SEM_SPEC = pl.BlockSpec(memory_space=pltpu.SEMAPHORE)


def _split_call(body, name, thru, n_sems, extra=(), with_token=True):
    hbm = lambda t: pltpu.with_memory_space_constraint(t, pltpu.HBM)
    effect = pltpu.CompilerParams(has_side_effects=pltpu.SideEffectType.DATAFLOW_SIDE_EFFECTING)
    nt = len(thru)
    thru_shapes = [pltpu.HBM(t.shape, t.dtype) for t in thru]
    if with_token:
        (after,) = extra
        outs = pl.pallas_call(
            body, name=name, in_specs=[HBM_SPEC] * nt + [pl.BlockSpec(memory_space=pl.ANY)],
            out_specs=[SEM_SPEC] * len(n_sems) + [HBM_SPEC] * nt + [pl.BlockSpec(memory_space=pltpu.VMEM)],
            out_shape=[pltpu.SemaphoreType.DMA((k,)) for k in n_sems] + thru_shapes
                      + [jax.ShapeDtypeStruct((8, LANES), F32)],
            input_output_aliases={i: len(n_sems) + i for i in range(nt)}, compiler_params=effect,
        )(*[hbm(t) for t in thru], after)
        return outs[:len(n_sems)], outs[len(n_sems):-1], outs[-1]
    return pl.pallas_call(
        body, name=name,
        in_specs=[HBM_SPEC] * nt + [SEM_SPEC] * len(n_sems) + [pl.BlockSpec(memory_space=pl.ANY)],
        out_specs=[HBM_SPEC] * nt, out_shape=thru_shapes,
        input_output_aliases={i: i for i in range(nt)}, compiler_params=effect,
    )(*thru, *extra)


def _gather_targets():
    x, y, c = _mesh_pos()
    return 4 * x + 2 * y + c, [(x, y, 1 - c), (1 - x, y, c), (x, 1 - y, c), (1 - x, 1 - y, c)]


def gather_start(shards, after, name):
    n = len(shards)
    zones = [lax.empty((w.shape[0], N_DEV) + w.shape[1:], w.dtype) for w in shards]

    def body(*refs):
        ins, zs = refs[:n], refs[n:2 * n]
        send_sems, recv_sems, local_sems = refs[2 * n + 1:2 * n + 4]
        token = refs[-1]
        me, targets = _gather_targets()
        for a in range(n):
            pltpu.make_async_copy(ins[a], zs[a].at[:, me], local_sems.at[a]).start()
            for k, to in enumerate(targets):
                pltpu.make_async_remote_copy(
                    src_ref=ins[a], dst_ref=zs[a].at[:, me], send_sem=send_sems.at[4 * a + k],
                    recv_sem=recv_sems.at[4 * a + k], device_id=to, device_id_type=MESH).start()
        token[...] = jnp.zeros(token.shape, F32)

    sems, thru, token = _split_call(body, name, list(shards) + zones, (4 * n, 4 * n, n), extra=(after,))
    return (sems, thru, n), token


def gather_wait(started, after, name):
    sems, thru, n = started

    def body(*refs):
        zs = refs[n:2 * n]
        send_sems, recv_sems, local_sems = refs[2 * n:2 * n + 3]
        _, targets = _gather_targets()
        for a in range(n):
            for k, to in enumerate(targets):
                cp = pltpu.make_async_remote_copy(
                    src_ref=zs[a].at[:, 0], dst_ref=zs[a].at[:, 0], send_sem=send_sems.at[4 * a + k],
                    recv_sem=recv_sems.at[4 * a + k], device_id=to, device_id_type=MESH)
                cp.wait_send()
                cp.wait_recv()
            pltpu.make_async_copy(zs[a].at[:, 0], zs[a].at[:, 0], local_sems.at[a]).wait()

    return _split_call(body, name, thru, (4 * n, 4 * n, n), extra=(*sems, after), with_token=False)[n:]


def forward_start(zones, after, name):
    n = len(zones)

    def body(*refs):
        zs = refs[:n]
        send_sems, recv_sems = refs[n + 1:n + 3]
        token = refs[-1]
        x, y, c = _mesh_pos()
        for a in range(n):
            for j, chip in enumerate([(1 - x, y), (x, 1 - y), (1 - x, 1 - y)]):
                blk = zs[a].at[:, 4 * chip[0] + 2 * chip[1] + c]
                pltpu.make_async_remote_copy(
                    src_ref=blk, dst_ref=blk, send_sem=send_sems.at[3 * a + j], recv_sem=recv_sems.at[3 * a + j],
                    device_id=(x, y, 1 - c), device_id_type=MESH).start()
        token[...] = jnp.zeros(token.shape, F32)

    sems, thru, token = _split_call(body, name, list(zones), (3 * n, 3 * n), extra=(after,))
    return (sems, thru, n), token


def forward_wait(started, after, name):
    sems, thru, n = started

    def body(*refs):
        zs = refs[:n]
        send_sems, recv_sems = refs[n:n + 2]
        x, y, c = _mesh_pos()
        for a in range(n):
            for j in range(3):
                cp = pltpu.make_async_remote_copy(
                    src_ref=zs[a].at[:, 0], dst_ref=zs[a].at[:, 0], send_sem=send_sems.at[3 * a + j],
                    recv_sem=recv_sems.at[3 * a + j], device_id=(x, y, 1 - c), device_id_type=MESH)
                cp.wait_send()
                cp.wait_recv()

    return _split_call(body, name, thru, (3 * n, 3 * n), extra=(*sems, after), with_token=False)


def scatter_start(groups, name):
    n = len(groups)
    flat = [g for grp in groups for g in grp]
    nf = len(flat)
    offs = np.cumsum([0] + [len(grp) for grp in groups])
    lands = [lax.empty((N_DEV, len(grp)) + grp[0].shape[1:], grp[0].dtype) for grp in groups]

    def body(*refs):
        ins, zones = refs[:nf], refs[nf:nf + n]
        send_sems, recv_sems, local_sems = refs[nf + n:nf + n + 3]
        token = refs[-1]
        me, peers = _peers()
        for a in range(n):
            for w in range(len(groups[a])):
                pltpu.make_async_copy(ins[offs[a] + w].at[me], zones[a].at[me, w], local_sems.at[a]).start()
        for k, peer in enumerate(peers):
            p_id = 4 * peer[0] + 2 * peer[1] + peer[2]
            for a in range(n):
                for w in range(len(groups[a])):
                    pltpu.make_async_remote_copy(
                        src_ref=ins[offs[a] + w].at[p_id], dst_ref=zones[a].at[me, w],
                        send_sem=send_sems.at[7 * a + k], recv_sem=recv_sems.at[7 * a + k],
                        device_id=peer, device_id_type=MESH).start()
        token[...] = jnp.zeros(token.shape, F32)

    hbm = lambda t: pltpu.with_memory_space_constraint(t, pltpu.HBM)
    outs = pl.pallas_call(
        body, name=name,
        in_specs=[HBM_SPEC] * (nf + n),
        out_specs=[SEM_SPEC] * 3 + [HBM_SPEC] * (nf + n) + [pl.BlockSpec(memory_space=pltpu.VMEM)],
        out_shape=[pltpu.SemaphoreType.DMA((7 * n,)), pltpu.SemaphoreType.DMA((7 * n,)), pltpu.SemaphoreType.DMA((n,))]
                  + [pltpu.HBM(t.shape, t.dtype) for t in flat + lands]
                  + [jax.ShapeDtypeStruct((8, LANES), F32)],
        input_output_aliases={i: 3 + i for i in range(nf + n)},
        compiler_params=pltpu.CompilerParams(has_side_effects=pltpu.SideEffectType.DATAFLOW_SIDE_EFFECTING),
    )(*[hbm(t) for t in flat], *[hbm(t) for t in lands])
    sems, thru, token = outs[:3], outs[3:3 + nf + n], outs[-1]
    return (sems, thru, [len(grp) for grp in groups]), token


def scatter_wait(started, after, name):
    (send_sems, recv_sems, local_sems), thru, sizes = started
    n = len(sizes)
    nf = len(thru) - n

    def body(*refs):
        zones = refs[nf:nf + n]
        s_sems, r_sems, l_sems = refs[nf + n:nf + n + 3]
        me, peers = _peers()
        for a in range(n):
            for k, peer in enumerate(peers):
                cp = pltpu.make_async_remote_copy(
                    src_ref=zones[a].at[0], dst_ref=zones[a].at[0],
                    send_sem=s_sems.at[7 * a + k], recv_sem=r_sems.at[7 * a + k], device_id=peer,
                    device_id_type=MESH)
                cp.wait_send()
                cp.wait_recv()
            pltpu.make_async_copy(zones[a].at[0], zones[a].at[0], l_sems.at[a]).wait()

    outs = pl.pallas_call(
        body, name=name,
        in_specs=[HBM_SPEC] * (nf + n) + [SEM_SPEC] * 3 + [pl.BlockSpec(memory_space=pl.ANY)],
        out_specs=[HBM_SPEC] * (nf + n),
        out_shape=[pltpu.HBM(t.shape, t.dtype) for t in thru],
        input_output_aliases={i: i for i in range(nf + n)},
        compiler_params=pltpu.CompilerParams(has_side_effects=pltpu.SideEffectType.DATAFLOW_SIDE_EFFECTING),
    )(*thru, send_sems, recv_sems, local_sems, after)
    return outs[nf:]


def pair_start(grads, after, name):
    nw = len(grads)
    land = lax.empty((4, nw) + grads[0].shape[1:], grads[0].dtype)

    def body(*refs):
        ins, zone = refs[:nw], refs[nw]
        send_sems, recv_sems = refs[nw + 2:nw + 4]
        x, y, c = _mesh_pos()
        for j in range(4):
            for w in range(nw):
                pltpu.make_async_remote_copy(
                    src_ref=ins[w].at[2 * j + (1 - c)], dst_ref=zone.at[j, w], send_sem=send_sems.at[0],
                    recv_sem=recv_sems.at[0], device_id=(x, y, 1 - c), device_id_type=MESH).start()
        refs[-1][...] = jnp.zeros(refs[-1].shape, F32)

    sems, thru, token = _split_call(body, name, list(grads) + [land], (1, 1), extra=(after,))
    return (sems, thru, nw), token


def pair_wait(started, after, name):
    sems, thru, nw = started

    def body(*refs):
        zone = refs[nw]
        send_sems, recv_sems = refs[nw + 1:nw + 3]
        x, y, c = _mesh_pos()
        cp = pltpu.make_async_remote_copy(src_ref=zone, dst_ref=zone, send_sem=send_sems.at[0],
                                          recv_sem=recv_sems.at[0], device_id=(x, y, 1 - c), device_id_type=MESH)
        cp.wait_send()
        cp.wait_recv()

    outs = _split_call(body, name, thru, (1, 1), extra=(*sems, after), with_token=False)
    return outs[:nw], outs[nw]


def pair_sum(grads, land, name):
    nw = len(grads)
    _, r, c_dim = grads[0].shape

    def body(*refs):
        g_refs, l_ref, o_ref = refs[:nw], refs[nw], refs[nw + 1]
        core = lax.axis_index("c")
        for w in range(nw):
            o_ref[0, w] = (g_refs[w][0, core].astype(F32) + l_ref[0, w].astype(F32)).astype(BF16)

    return pl.pallas_call(
        body, name=name, grid=(4,),
        in_specs=[pl.BlockSpec((1, 2, r, c_dim), lambda j: (j, 0, 0, 0))] * nw
                 + [pl.BlockSpec((1, nw, r, c_dim), lambda j: (j, 0, 0, 0))],
        out_specs=pl.BlockSpec((1, nw, r, c_dim), lambda j: (j, 0, 0, 0)),
        out_shape=jax.ShapeDtypeStruct((4, nw, r, c_dim), BF16),
        compiler_params=_params(),
    )(*[g.reshape(4, 2, r, c_dim) for g in grads], land)


def _other_chips():
    x, y, c = _mesh_pos()
    chips = []
    for rel in range(1, 4):
        px, py = (1 - x if rel & 2 else x), (1 - y if rel & 1 else y)
        chips.append((px, py, 2 * px + py))
    return 2 * x + y, c, chips


def chip_start(pair_sums, after, name):
    land = lax.empty(pair_sums.shape, pair_sums.dtype)

    def body(*refs):
        h_ref, zone = refs[0], refs[1]
        send_sems, recv_sems, local_sem = refs[3:6]
        mine, c, chips = _other_chips()
        pltpu.make_async_copy(h_ref.at[mine], zone.at[mine], local_sem.at[0]).start()
        for k, (px, py, j) in enumerate(chips):
            pltpu.make_async_remote_copy(
                src_ref=h_ref.at[j], dst_ref=zone.at[mine], send_sem=send_sems.at[k], recv_sem=recv_sems.at[k],
                device_id=(px, py, c), device_id_type=MESH).start()
        refs[-1][...] = jnp.zeros(refs[-1].shape, F32)

    sems, thru, token = _split_call(body, name, [pair_sums, land], (3, 3, 1), extra=(after,))
    return (sems, thru), token


def chip_wait(started, after, name):
    sems, thru = started

    def body(*refs):
        zone = refs[1]
        send_sems, recv_sems, local_sem = refs[2:5]
        _, c, chips = _other_chips()
        for k, (px, py, _) in enumerate(chips):
            cp = pltpu.make_async_remote_copy(
                src_ref=zone.at[0], dst_ref=zone.at[0], send_sem=send_sems.at[k], recv_sem=recv_sems.at[k],
                device_id=(px, py, c), device_id_type=MESH)
            cp.wait_send()
            cp.wait_recv()
        pltpu.make_async_copy(zone.at[0], zone.at[0], local_sem.at[0]).wait()

    return _split_call(body, name, thru, (3, 3, 1), extra=(*sems, after), with_token=False)[1]


def share_start(parts, after, name):
    n = len(parts)
    zones = [lax.empty((N_DEV,) + p.shape, p.dtype) for p in parts]

    def body(*refs):
        ins, zs = refs[:n], refs[n:2 * n]
        send_sems, recv_sems, local_sems = refs[2 * n + 1:2 * n + 4]
        me, peers = _peers()
        for i in range(n):
            pltpu.make_async_copy(ins[i], zs[i].at[me], local_sems.at[i]).start()
            for k, peer in enumerate(peers):
                pltpu.make_async_remote_copy(
                    src_ref=ins[i], dst_ref=zs[i].at[me], send_sem=send_sems.at[7 * i + k],
                    recv_sem=recv_sems.at[7 * i + k], device_id=peer, device_id_type=MESH).start()
        refs[-1][...] = jnp.zeros(refs[-1].shape, F32)

    sems, thru, token = _split_call(body, name, list(parts) + zones, (7 * n, 7 * n, n), extra=(after,))
    return (sems, thru, n), token


def share_wait(started, after, name):
    sems, thru, n = started

    def body(*refs):
        zs = refs[n:2 * n]
        send_sems, recv_sems, local_sems = refs[2 * n:2 * n + 3]
        _, peers = _peers()
        for i in range(n):
            for k, peer in enumerate(peers):
                cp = pltpu.make_async_remote_copy(
                    src_ref=zs[i].at[0], dst_ref=zs[i].at[0], send_sem=send_sems.at[7 * i + k],
                    recv_sem=recv_sems.at[7 * i + k], device_id=peer, device_id_type=MESH)
                cp.wait_send()
                cp.wait_recv()
            pltpu.make_async_copy(zs[i].at[0], zs[i].at[0], local_sems.at[i]).wait()

    return _split_call(body, name, thru, (7 * n, 7 * n, n), extra=(*sems, after), with_token=False)[n:]


def _adamw_math(w, g, m, v):
    m = ADAM_B1 * m + (1.0 - ADAM_B1) * g
    v = ADAM_B2 * v + (1.0 - ADAM_B2) * (g * g)
    m_hat = m / (1.0 - ADAM_B1 ** ADAM_STEP)
    v_hat = v / (1.0 - ADAM_B2 ** ADAM_STEP)
    delta = -ADAM_LR * (m_hat / (jnp.sqrt(v_hat) + ADAM_EPS) + ADAM_WD * w)
    return delta, m, v


ADAMW_BLOCK_BYTES = 24 * 1024 * 1024


def adamw_layer(zone, layer, items, after, name):
    n_src, nw, r, c = zone.shape
    depth = items[0][0].shape[0]
    prevs = [p if p is not None else tuple(lax.empty((depth, r, c), F32) for _ in range(4)) for _, _, _, p in items]
    row_bytes = 2 * nw * c * (2 * n_src + 4 * 7)
    tr = max(t for t in range(8, r + 1, 8) if r % t == 0 and t * row_bytes <= ADAMW_BLOCK_BYTES)

    def body(z_ref, *rest):
        ins, outs = rest[:3 * nw], rest[7 * nw + 1:]
        for i in range(nw):
            g = z_ref[0, i].astype(F32)
            for src in range(1, n_src):
                g = g + z_ref[src, i].astype(F32)
            g_ref, d_ref, mo_ref, vo_ref = outs[4 * i:4 * i + 4]
            w_ref, m_ref, v_ref = ins[3 * i:3 * i + 3]
            g_ref[...] = g
            d_ref[...], mo_ref[...], vo_ref[...] = _adamw_math(w_ref[...], g, m_ref[...], v_ref[...])

    rows = pl.BlockSpec((None, tr, c), lambda i: (layer, i, 0))
    anywhere = pl.BlockSpec(memory_space=pl.ANY)
    outs = pl.pallas_call(
        body, name=name, grid=(r // tr,),
        in_specs=[pl.BlockSpec((n_src, nw, tr, c), lambda i: (0, 0, i, 0))] + [rows] * (3 * nw)
                 + [anywhere] * (4 * nw + 1),
        out_specs=[rows] * (4 * nw),
        out_shape=[jax.ShapeDtypeStruct((depth, r, c), F32)] * (4 * nw),
        input_output_aliases={1 + 3 * nw + k: k for k in range(4 * nw)},
        compiler_params=_params(),
    )(zone, *[t for w, m, v, _ in items for t in (w, m, v)], *[t for p in prevs for t in p], after)
    return [tuple(outs[4 * i:4 * i + 4]) for i in range(nw)]


def adamw_small(ws, recvs, ms, vs, name):
    n = len(ws)

    def body(*refs):
        w_refs, r_refs, m_refs, v_refs = (refs[i * n:(i + 1) * n] for i in range(4))
        g_refs, d_refs, mo_refs, vo_refs = (refs[(4 + i) * n:(5 + i) * n] for i in range(4))
        for i in range(n):
            g = r_refs[i][0]
            for src in range(1, N_DEV):
                g = g + r_refs[i][src]
            g_refs[i][...] = g
            d_refs[i][...], mo_refs[i][...], vo_refs[i][...] = _adamw_math(w_refs[i][...], g, m_refs[i][...],
                                                                            v_refs[i][...])

    vm = pl.BlockSpec(memory_space=pltpu.VMEM)
    outs = pl.pallas_call(
        body, name=name, in_specs=[vm] * (4 * n), out_specs=[vm] * (4 * n),
        out_shape=[jax.ShapeDtypeStruct(w.shape, F32) for w in ws] * 4,
        compiler_params=pltpu.CompilerParams(vmem_limit_bytes=V7X_VMEM_LIMIT),
    )(*ws, *recvs, *ms, *vs)
    return [outs[i * n:(i + 1) * n] for i in range(4)]


SMALL_NAMES = ("ffn1_norm", "mix_norm", "ffn2_norm", "b_gate", "na_q_norm", "na_k_norm", "sw_q_norm", "sw_k_norm",
               "na_rpb", "sw_sink", "t5_rel_table")


def kernel(x, ffn1_norm, ffn1_w_gate, ffn1_w_up, ffn1_w_down, mix_norm, w_in, b_gate, na_q_norm, na_k_norm, na_rpb, sw_q_norm, sw_k_norm, sw_sink, t5_rel_table, w_branch_na, w_branch_sw, w_out, ffn2_norm, ffn2_w_gate, ffn2_w_up, ffn2_w_down, loss_target, m_ffn1_norm, m_ffn1_w_gate, m_ffn1_w_up, m_ffn1_w_down, m_mix_norm, m_w_in, m_b_gate, m_na_q_norm, m_na_k_norm, m_na_rpb, m_sw_q_norm, m_sw_k_norm, m_sw_sink, m_t5_rel_table, m_w_branch_na, m_w_branch_sw, m_w_out, m_ffn2_norm, m_ffn2_w_gate, m_ffn2_w_up, m_ffn2_w_down, v_ffn1_norm, v_ffn1_w_gate, v_ffn1_w_up, v_ffn1_w_down, v_mix_norm, v_w_in, v_b_gate, v_na_q_norm, v_na_k_norm, v_na_rpb, v_sw_q_norm, v_sw_k_norm, v_sw_sink, v_t5_rel_table, v_w_branch_na, v_w_branch_sw, v_w_out, v_ffn2_norm, v_ffn2_w_gate, v_ffn2_w_up, v_ffn2_w_down):
    weights = dict(ffn1_norm=ffn1_norm, ffn1_w_gate=ffn1_w_gate, ffn1_w_up=ffn1_w_up, ffn1_w_down=ffn1_w_down,
                   mix_norm=mix_norm, w_in=w_in, b_gate=b_gate, na_q_norm=na_q_norm, na_k_norm=na_k_norm,
                   na_rpb=na_rpb, sw_q_norm=sw_q_norm, sw_k_norm=sw_k_norm, sw_sink=sw_sink,
                   t5_rel_table=t5_rel_table, w_branch_na=w_branch_na, w_branch_sw=w_branch_sw, w_out=w_out,
                   ffn2_norm=ffn2_norm, ffn2_w_gate=ffn2_w_gate, ffn2_w_up=ffn2_w_up, ffn2_w_down=ffn2_w_down)
    mom_m = dict(ffn1_norm=m_ffn1_norm, ffn1_w_gate=m_ffn1_w_gate, ffn1_w_up=m_ffn1_w_up, ffn1_w_down=m_ffn1_w_down,
                 mix_norm=m_mix_norm, w_in=m_w_in, b_gate=m_b_gate, na_q_norm=m_na_q_norm, na_k_norm=m_na_k_norm,
                 na_rpb=m_na_rpb, sw_q_norm=m_sw_q_norm, sw_k_norm=m_sw_k_norm, sw_sink=m_sw_sink,
                 t5_rel_table=m_t5_rel_table, w_branch_na=m_w_branch_na, w_branch_sw=m_w_branch_sw, w_out=m_w_out,
                 ffn2_norm=m_ffn2_norm, ffn2_w_gate=m_ffn2_w_gate, ffn2_w_up=m_ffn2_w_up, ffn2_w_down=m_ffn2_w_down)
    mom_v = dict(ffn1_norm=v_ffn1_norm, ffn1_w_gate=v_ffn1_w_gate, ffn1_w_up=v_ffn1_w_up, ffn1_w_down=v_ffn1_w_down,
                 mix_norm=v_mix_norm, w_in=v_w_in, b_gate=v_b_gate, na_q_norm=v_na_q_norm, na_k_norm=v_na_k_norm,
                 na_rpb=v_na_rpb, sw_q_norm=v_sw_q_norm, sw_k_norm=v_sw_k_norm, sw_sink=v_sw_sink,
                 t5_rel_table=v_t5_rel_table, w_branch_na=v_w_branch_na, w_branch_sw=v_w_branch_sw, w_out=v_w_out,
                 ffn2_norm=v_ffn2_norm, ffn2_w_gate=v_ffn2_w_gate, ffn2_w_up=v_ffn2_w_up, ffn2_w_down=v_ffn2_w_down)
    order = list(weights)

    depth = ffn1_norm.shape[0]
    s, d = x.shape[1], x.shape[2]
    xs = x[0]
    tr = lambda w: jnp.swapaxes(w, -1, -2)

    merge = lambda t: t.reshape(t.shape[0], N_DEV * t.shape[2], t.shape[3])
    no_dep = jnp.zeros((8, LANES), F32)

    def shards_of(kind, l):
        stack = lambda *ws: jnp.stack(ws).astype(BF16)
        if kind == "ffn1":
            return [stack(tr(ffn1_w_gate[l]), tr(ffn1_w_up[l]), ffn1_w_down[l])]
        if kind == "win":
            return [stack(tr(w_in[l]))]
        return [stack(tr(ffn2_w_gate[l]), tr(ffn2_w_up[l]), ffn2_w_down[l]), stack(w_out[l]),
                stack(tr(w_branch_na[l]), tr(w_branch_sw[l]))]

    shards = {(kind, l): shards_of(kind, l) for l in range(depth) for kind in ("ffn1", "win", "rest")}

    def start(kind, l, after):
        return gather_start(shards[kind, l], after, f"gather_{kind}_{l}")

    def arrive(started, kind, l, after):
        zones = gather_wait(started, after, f"gather_{kind}_{l}_wait")
        return forward_start(zones, no_dep, f"forward_{kind}_{l}")

    def finish(fwd, kind, l, after):
        return [merge(z) for z in forward_wait(fwd, after, f"forward_{kind}_{l}_wait")]

    bd = jnp.asarray(np.kron(np.eye(MXU_TILE // HEAD_DIM), np.full((HEAD_DIM, HEAD_DIM), 1.0 / HEAD_DIM)), BF16)
    bmap = jnp.asarray(_t5_bucket_map())
    tile8 = lambda g: jnp.tile(g, NA_WIDTH // HEAD_DIM).reshape(1, NA_WIDTH)
    tile2 = lambda g: jnp.tile(g, SW_KV_WIDTH // HEAD_DIM).reshape(1, SW_KV_WIDTH)

    st_first, tok = start("ffn1", 0, no_dep)
    t5b = t5_expand(t5_rel_table, bmap, tok, "t5_expand").reshape(SW_STACK, 3 * SW_BLOCK)
    t2_tables = [rpb_expand(_rpb_rows(na_rpb[l]), tok, f"rpb_expand_{l}") for l in range(depth)]
    qk_gains = [(tile8(na_q_norm[l]), tile8(na_k_norm[l]), tile8(sw_q_norm[l]), tile2(sw_k_norm[l]))
                for l in range(depth)]
    early = ([t[0, 0, 0:8, :] for t in t2_tables] + [t[0, 0:8, 0:LANES].astype(F32) for v in shards.values() for t in v]
             + [g[:, 0:LANES] for gs in qk_gains for g in gs])
    fwd, _ = arrive(st_first, "ffn1", 0, functools.reduce(jnp.add, early, t5b[0:8, 0:LANES]))
    st_win, dep = start("win", 0, t5b)
    (first,) = finish(fwd, "ffn1", 0, dep)

    saved = []
    layer_w = {0: dict(wg1=(first, 0), wu1=(first, 1), wd1=(first, 2))}
    cur = xs
    for l in range(depth):
        sv = {}
        lw = layer_w[l]
        sv["x0"] = cur
        cur, sv["xn1"], sv["hg1"], sv["hu1"], sv["act1"] = ffn_forward(
            cur, ffn1_norm[l][None], lw["wg1"], lw["wu1"], lw["wd1"], dep, f"ffn1_{l}")
        sv["x1"] = cur
        fwd, _ = arrive(st_win, "win", l, cur)
        st_rest, tok = start("rest", l, cur)
        (zb,) = finish(fwd, "win", l, tok)
        lw["win"] = (zb, 0)
        sv["gains"] = qk_gains[l]
        sv["hn"], sv["zq"], sv["qa"], sv["ka"], sv["qs"], sv["ks"], sv["gt"] = mix_in(
            cur, mix_norm[l][None], lw["win"], b_gate[l][None], *sv["gains"], bd, f"mix_in_{l}")
        sv["t2"] = t2_tables[l]
        sv["o_na"] = na_fwd(sv["qa"], sv["ka"], sv["zq"], sv["t2"], f"na_fwd_{l}")
        dep = no_dep
        if l + 1 < depth:
            st_ffn1, dep = start("ffn1", l + 1, sv["o_na"])
        sv["o_sw"] = sw_fwd(sv["qs"], sv["ks"], sv["zq"], t5b, sw_sink[l], dep, f"sw_fwd_{l}")
        fwd, tok = arrive(st_rest, "rest", l, sv["o_sw"][0:8, 0:LANES] + sv["o_na"][0:8, 0:LANES])
        za, zc, zd = finish(fwd, "rest", l, tok)
        lw.update(wg2=(za, 0), wu2=(za, 1), wd2=(za, 2), wout=(zc, 0), wna=(zd, 0), wsw=(zd, 1))
        cur, sv["a_na"], sv["a_sw"], sv["merged"] = merge_out(
            cur, sv["o_na"], sv["o_sw"], sv["gt"], lw["wna"], lw["wsw"], lw["wout"], f"merge_out_{l}")
        sv["x2"] = cur
        if l + 1 < depth:
            st_win, dep = start("win", l + 1, cur)
            sv["xn2"], sv["hg2"], sv["hu2"], sv["act2"] = ffn_forward(
                cur, ffn2_norm[l][None], lw["wg2"], lw["wu2"], None, dep, f"ffn2_up_{l}")
            fwd, dep = arrive(st_ffn1, "ffn1", l + 1, sv["act2"])
            cur = ffn_down(cur, sv["act2"], lw["wd2"], dep, f"ffn2_down_{l}")
            (za,) = finish(fwd, "ffn1", l + 1, cur)
            layer_w[l + 1] = dict(wg1=(za, 0), wu1=(za, 1), wd1=(za, 2))
        else:
            dx, loss_acc, sv["xn2"], sv["hg2"], sv["hu2"], sv["act2"] = ffn_forward(
                cur, ffn2_norm[l][None], lw["wg2"], lw["wu2"], lw["wd2"], no_dep, f"ffn2_{l}",
                target=loss_target[0])
        dep = no_dep
        saved.append(sv)

    split = lambda t: t.reshape(N_DEV, t.shape[0] // N_DEV, t.shape[1])
    pending = {}
    last_key = "ffn1_0"
    two_level = {last_key}
    small = {k: [None] * depth for k in SMALL_NAMES if k != "t5_rel_table"}
    dbias_sw = []
    for l in reversed(range(depth)):
        sv = saved[l]
        lw = layer_w[l]
        wg1, wu1, wd1, wg2, wu2, wd2 = (lw[k] for k in ("wg1", "wu1", "wd1", "wg2", "wu2", "wd2"))
        win_t, wout_l, wna_t, wsw_t = lw["win"], lw["wout"], lw["wna"], lw["wsw"]
        blocks = ((2, "x2", "xn2", "hg2", "hu2", "act2", wg2, wu2, wd2, "ffn2_norm", 3),
                  (1, "x0", "xn1", "hg1", "hu1", "act1", wg1, wu1, wd1, "ffn1_norm", 0))

        def ffn_backward(dx, blk):
            tag, xk, xnk, hgk, huk, actk, wg, wu, wd, norm_name, slot = blk
            gains = weights[norm_name]
            dxb, dhg, dhu = ffn_bwd_act(dx, wd, sv[hgk], sv[huk], f"ffn{tag}_bwd_act_{l}")
            gwg, gwu, gwd = tn_matmul([(dhg, sv[xnk], 1.0), (dhu, sv[xnk], 1.0), (sv[actk], dxb, 0.5)],
                                      f"ffn{tag}_dw_{l}")
            key = f"ffn{tag}_{l}"
            blocks_of = [split(gwg), split(gwu), split(gwd)]
            if key in two_level:
                paired, token = pair_start(blocks_of, dxb, f"pair_{key}")
            else:
                pending[key], token = scatter_start([blocks_of], f"scatter_{key}")
            dx, dg = proj_bwd_norm([dhg, dhu], [wg, wu], sv[xk], gains[l][None], dx, token, f"ffn{tag}_bwd_x_{l}")
            token = no_dep
            if key in two_level:
                thru, land = pair_wait(paired, dx, f"pair_{key}_wait")
                pending[key], token = chip_start(pair_sum(thru, land, f"pair_sum_{key}"), dg, f"chips_{key}")
            small[norm_name][l] = dg[0]
            return dx, token

        dx, token = ffn_backward(dx, blocks[0])
        dxb, dzg, da_na, da_sw, do_na, do_sw, dbg = mix_bwd_out(
            dx, sv["gt"], sv["a_na"], sv["a_sw"], wna_t, wsw_t, wout_l, token, f"mix_bwd_out_{l}")
        small["b_gate"][l] = dbg[0]
        gwout, gwna, gwsw = tn_matmul([(sv["merged"], dxb, 1.0), (da_na, sv["o_na"], 1.0), (da_sw, sv["o_sw"], 1.0)],
                                      f"mix_dw_{l}")
        dqa, dka, dva, dt2 = na_bwd(sv["qa"], sv["ka"], sv["zq"], sv["t2"], sv["o_na"], do_na, f"na_bwd_{l}")
        dqs, dks, dvs, dbias, dsink = sw_bwd(sv["qs"], sv["ks"], sv["zq"], t5b, sw_sink[l], sv["o_sw"], do_sw,
                                             f"sw_bwd_{l}")
        dbias_sw.append(dbias.reshape(SW_HEADS, SW_BLOCK, 3 * SW_BLOCK))
        small["sw_sink"][l] = jnp.sum(dsink[:, 0].reshape(SW_HEADS, SW_BLOCK), axis=1)
        small["na_rpb"][l] = _rpb_from_rows(rpb_reduce(dt2, f"rpb_reduce_{l}"))
        dz, dgqa, dgka, dgqs, dgks = qk_norm_bwd(dqa, dka, dva, dqs, dks, dvs, sv["zq"], dzg, *sv["gains"], bd,
                                                 f"qk_norm_bwd_{l}")
        fold = lambda g: jnp.sum(g.reshape(-1, HEAD_DIM), axis=0)
        small["na_q_norm"][l], small["na_k_norm"][l] = fold(dgqa), fold(dgka)
        small["sw_q_norm"][l], small["sw_k_norm"][l] = fold(dgqs), fold(dgks)
        (gwin,) = tn_matmul([(dz, sv["hn"], 1.0)], f"dwin_{l}")
        pending[f"mix_{l}"], token = scatter_start([[split(gwout)], [split(gwna), split(gwsw)], [split(gwin)]],
                                                   f"scatter_mix_{l}")
        dx, dg = proj_bwd_norm([dz], [win_t], sv["x1"], mix_norm[l][None], dx, token, f"mix_bwd_x_{l}")
        small["mix_norm"][l] = dg[0]
        dx, tail = ffn_backward(dx, blocks[1])

    dtab = t5_reduce(dbias_sw, bmap, "t5_reduce")
    small_parts = {k: jnp.stack(v) for k, v in small.items()}
    small_parts["t5_rel_table"] = jnp.transpose(dtab[:, :, 0])

    grads, delta, new_m, new_v = {}, {}, {}, {}
    state = {}
    sharing, token = share_start([small_parts[k] for k in SMALL_NAMES] + [loss_acc], tail, "share_small")
    chain = [token]
    members = {"ffn": lambda t: [(f"ffn{t}_w_gate", 0, 0, True), (f"ffn{t}_w_up", 0, 1, True),
                                 (f"ffn{t}_w_down", 0, 2, False)],
               "mix": lambda t: [("w_out", 0, 0, False), ("w_branch_na", 1, 0, True), ("w_branch_sw", 1, 1, True),
                                 ("w_in", 2, 0, True)]}

    def collect(key):
        if key in two_level:
            zones = [chip_wait(pending[key], chain[0], f"wait_{key}")]
        else:
            zones = scatter_wait(pending[key], chain[0], f"wait_{key}")
        kind, l = key.split("_")
        group = members[kind[:3]](kind[3:])
        complete = all(f"{kind}_{j}" in done for j in range(depth) if j != int(l))
        for zi, zone in enumerate(zones):
            mine = sorted((wi, k, transposed) for k, z, wi, transposed in group if z == zi)
            views = [tr if transposed else (lambda t: t) for _, _, transposed in mine]
            items = [(view(weights[k]), view(mom_m[k]), view(mom_v[k]), state.get(k))
                     for (_, k, _), view in zip(mine, views)]
            results = adamw_layer(zone, int(l), items, chain[0], f"adamw_{key}_{zi}")
            chain[0] = results[-1][1]
            for (_, k, _), view, res in zip(mine, views, results):
                state[k] = res
                if complete:
                    grads[k], delta[k], new_m[k], new_v[k] = (view(t) for t in res)
        done.add(key)

    done = set()
    for key in pending:
        if key != last_key:
            collect(key)
    collect(last_key)
    *recvs, all_losses = share_wait(sharing, chain[0], "share_small_wait")
    loss = jnp.sum(all_losses) * (0.5 / d)
    results = adamw_small([weights[k] for k in SMALL_NAMES], recvs, [mom_m[k] for k in SMALL_NAMES],
                          [mom_v[k] for k in SMALL_NAMES], "adamw_small")
    for dst, outs in zip((grads, delta, new_m, new_v), results):
        dst.update(dict(zip(SMALL_NAMES, outs)))

    return (loss, dx[None], *[grads[k] for k in order], *[delta[k] for k in order],
            *[new_m[k] for k in order], *[new_v[k] for k in order])
```

```python
import functools
import math

import numpy as np
import jax
import jax.numpy as jnp
from jax import lax
from jax.experimental import pallas as pl
from jax.experimental.pallas import tpu as pltpu

F32 = jnp.float32
BF16 = jnp.bfloat16
MESH = pl.DeviceIdType.MESH

N_DEV = 8
EPS = 1e-6
NEG = -1e30
HEAD_DIM = 64
GRID_W = 64
NA_ROWS = 8
NA_COLS = 16
NA_WIDTH = 512
SW_Q_WIDTH = 512
SW_KV_WIDTH = 128
SW_BLOCK = 128
SW_HEADS = 8
SW_REP = 4
REL_BUCKETS = 32
REL_MAX_DIST = 128
QKV_WIDTH = 3 * NA_WIDTH + SW_Q_WIDTH + 2 * SW_KV_WIDTH
SCALE = 1.0 / math.sqrt(HEAD_DIM)

ADAM_LR = 0.001
ADAM_B1 = 0.9
ADAM_B2 = 0.999
ADAM_EPS = 1e-08
ADAM_WD = 0.01
ADAM_STEP = 10

V7X_VMEM_LIMIT = 56 * 1024 * 1024
LANES = 128
MXU_TILE = 256

NT = (((1,), (1,)), ((), ()))
TN = (((0,), (0,)), ((), ()))


def _params(n_grid=1):
    return pltpu.CompilerParams(dimension_semantics=("arbitrary",) * n_grid,
                                vmem_limit_bytes=V7X_VMEM_LIMIT)


def _row_tile(s):
    for t in (512, 256, 128, 64, 32, 16, 8):
        if s % t == 0:
            return t
    raise ValueError(s)


ONCE = pl.Buffered(1)


def _col_chunk(n):
    return MXU_TILE if n % MXU_TILE == 0 else n


def _dot(a, b):
    return jnp.dot(a, b, preferred_element_type=F32)


def _dotg(a, b, dn):
    return lax.dot_general(a, b, dn, preferred_element_type=F32)


def _sigmoid(v):
    return 1.0 / (1.0 + jnp.exp(-v))


def _rstd(xv):
    return lax.rsqrt(jnp.mean(xv * xv, axis=-1, keepdims=True) + EPS)


def _full(shape):
    nd = len(shape)
    return pl.BlockSpec(shape, lambda i, _n=nd: (0,) * _n)


def _rows(tm, width):
    return pl.BlockSpec((tm, width), lambda i: (i, 0))


def _mat(stack, idx):
    return pl.BlockSpec((None,) + tuple(stack.shape[1:]), lambda i, _w=idx: (_w, 0, 0), pipeline_mode=ONCE)


def _group_mean(v, bd):
    w = bd.shape[0]
    if v.shape[1] > w:
        return jnp.concatenate([_group_mean(v[:, c0:c0 + w], bd) for c0 in range(0, v.shape[1], w)], axis=1)
    hi = v.astype(BF16)
    lo = (v - hi.astype(F32)).astype(BF16)
    return _dot(hi, bd) + _dot(lo, bd)


def _loss_tile(y, t_ref, dy_ref, acc_ref):
    tm, d = y.shape

    @pl.when(pl.program_id(0) == 0)
    def _():
        acc_ref[...] = jnp.zeros(acc_ref.shape, F32)

    err = y - t_ref[...]
    dy_ref[...] = err * (1.0 / d)
    part = jnp.sum((err * err).reshape(tm // 8, 8, d), axis=0)
    acc = part[:, 0:LANES]
    for c0 in range(LANES, d, LANES):
        acc = acc + part[:, c0:c0 + LANES]
    acc_ref[...] = acc_ref[...] + acc


def ffn_forward(x, gain, wg_t, wu_t, wd, dep, name, target=None):
    s, d = x.shape
    f = wg_t[0].shape[1]
    tm = _row_tile(s) if wd is None else min(_row_tile(s), 256)
    fc = _col_chunk(f)
    nw = 2 if wd is None else 3
    n_in = nw + (1 if target is None else 2)

    def body(x_ref, g_ref, *refs):
        w_refs, outs = refs[:nw], refs[n_in:]
        xn_ref, dg_ref, du_ref, act_ref = outs[-4:]
        xv = x_ref[...]
        xn = (xv * _rstd(xv) * g_ref[...]).astype(BF16)
        xn_ref[...] = xn
        for c0 in range(0, f, fc):
            hg = _dotg(xn, w_refs[0][c0:c0 + fc, :], NT)
            hu = _dotg(xn, w_refs[1][c0:c0 + fc, :], NT)
            sg = _sigmoid(hg)
            silu = hg * sg
            du_ref[:, c0:c0 + fc] = silu.astype(BF16)
            dg_ref[:, c0:c0 + fc] = (hu * (sg + silu * (1.0 - sg))).astype(BF16)
            act_ref[:, c0:c0 + fc] = (silu * hu).astype(BF16)
        if wd is not None:
            y = xv + 0.5 * _dot(act_ref[...], w_refs[2][...])
            if target is None:
                outs[0][...] = y
            else:
                _loss_tile(y, refs[nw + 1], outs[0], outs[1])

    weights = [wg_t, wu_t] + ([] if wd is None else [wd])
    in_specs = [_rows(tm, d), _full((1, d))] + [_mat(*w) for w in weights] + [_full(dep.shape)]
    operands = [x, gain, *[w[0] for w in weights], dep]
    out_specs = [_rows(tm, d), _rows(tm, f), _rows(tm, f), _rows(tm, f)]
    out_shape = [jax.ShapeDtypeStruct((s, d), BF16)] + [jax.ShapeDtypeStruct((s, f), BF16)] * 3
    if wd is not None:
        out_specs, out_shape = [_rows(tm, d)] + out_specs, [jax.ShapeDtypeStruct((s, d), F32)] + out_shape
    if target is not None:
        in_specs, operands = in_specs + [_rows(tm, d)], operands + [target]
        out_specs = out_specs[:1] + [_full((8, LANES))] + out_specs[1:]
        out_shape = out_shape[:1] + [jax.ShapeDtypeStruct((8, LANES), F32)] + out_shape[1:]
    return pl.pallas_call(
        body, name=name, grid=(s // tm,), in_specs=in_specs, out_specs=out_specs, out_shape=out_shape,
        compiler_params=_params(),
    )(*operands)


def ffn_down(x, act, wd, dep, name):
    s, d = x.shape
    f = act.shape[1]
    tm = _row_tile(s)

    def body(x_ref, a_ref, w_ref, dep_ref, o_ref):
        o_ref[...] = x_ref[...] + 0.5 * _dot(a_ref[...], w_ref[...])

    return pl.pallas_call(
        body, name=name, grid=(s // tm,),
        in_specs=[_rows(tm, d), _rows(tm, f), _mat(*wd), _full(dep.shape)],
        out_specs=_rows(tm, d),
        out_shape=jax.ShapeDtypeStruct((s, d), F32),
        compiler_params=_params(),
    )(x, act, wd[0], dep)


def mix_in(x, gain, win_t, b_gate, gq_na, gk_na, gq_sw, gk_sw, bd, name):
    s, d = x.shape
    tm = _row_tile(s)
    gc = _col_chunk(2 * d)

    def body(x_ref, g_ref, w_ref, b_ref, gqa_ref, gka_ref, gqs_ref, gks_ref, bd_ref,
             hn_ref, zq_ref, qa_ref, ka_ref, qs_ref, ks_ref, gt_ref):
        xv = x_ref[...]
        hn = (xv * _rstd(xv) * g_ref[...]).astype(BF16)
        hn_ref[...] = hn

        def proj(c0, c1):
            return _dotg(hn, w_ref[c0:c1, :], NT)

        def headnorm(z, g, bdm):
            return z * lax.rsqrt(_group_mean(z * z, bdm) + EPS) * g

        bd512 = bd_ref[...]
        bd128 = bd_ref[0:SW_KV_WIDTH, 0:SW_KV_WIDTH]
        z = proj(0, 512)
        zq_ref[:, 0:512] = z.astype(BF16)
        qa_ref[...] = (headnorm(z, gqa_ref[...], bd512) * SCALE).astype(BF16)
        z = proj(512, 1024)
        zq_ref[:, 512:1024] = z.astype(BF16)
        ka_ref[...] = headnorm(z, gka_ref[...], bd512).astype(BF16)
        z = proj(1024, 1536)
        zq_ref[:, 1024:1536] = z.astype(BF16)
        z = proj(1536, 2048)
        zq_ref[:, 1536:2048] = z.astype(BF16)
        qs_ref[...] = (headnorm(z, gqs_ref[...], bd512) * SCALE).astype(BF16)
        z = proj(2048, 2176)
        zq_ref[:, 2048:2176] = z.astype(BF16)
        ks_ref[...] = headnorm(z, gks_ref[...], bd128).astype(BF16)
        z = proj(2176, 2304)
        zq_ref[:, 2176:2304] = z.astype(BF16)
        for c0 in range(0, 2 * d, gc):
            zg = proj(QKV_WIDTH + c0, QKV_WIDTH + c0 + gc) + b_ref[:, c0:c0 + gc]
            gt_ref[:, c0:c0 + gc] = _sigmoid(zg).astype(BF16)

    return pl.pallas_call(
        body, name=name, grid=(s // tm,),
        in_specs=[_rows(tm, d), _full((1, d)), _mat(*win_t), _full((1, 2 * d)),
                  _full((1, 512)), _full((1, 512)), _full((1, 512)), _full((1, 128)), _full((MXU_TILE, MXU_TILE))],
        out_specs=[_rows(tm, d), _rows(tm, QKV_WIDTH), _rows(tm, 512), _rows(tm, 512), _rows(tm, 512),
                   _rows(tm, 128), _rows(tm, 2 * d)],
        out_shape=[jax.ShapeDtypeStruct((s, d), BF16), jax.ShapeDtypeStruct((s, QKV_WIDTH), BF16),
                   jax.ShapeDtypeStruct((s, 512), BF16), jax.ShapeDtypeStruct((s, 512), BF16),
                   jax.ShapeDtypeStruct((s, 512), BF16), jax.ShapeDtypeStruct((s, 128), BF16),
                   jax.ShapeDtypeStruct((s, 2 * d), BF16)],
        compiler_params=_params(),
    )(x, gain, win_t[0], b_gate, gq_na, gk_na, gq_sw, gk_sw, bd)


def _na_iotas():
    qc = lax.broadcasted_iota(jnp.int32, (GRID_W, LANES), 0)
    ln = lax.broadcasted_iota(jnp.int32, (GRID_W, LANES), 1)
    low = ln < GRID_W
    kc = jnp.where(low, ln, ln - GRID_W)
    diff = kc - qc + (NA_COLS - 1)
    qcs = jnp.clip(qc - NA_COLS // 2, 0, GRID_W - NA_COLS)
    inwin = (kc >= qcs) & (kc < qcs + NA_COLS)
    return diff, low, inwin


NA_RI = 2 * NA_ROWS - 1
NA_CI = 2 * NA_COLS - 1
NA_T2 = NA_RI + 1


def _rpb_rows(rpb):
    h = rpb.shape[0]
    padded = jnp.pad(rpb, ((0, 0), (1, 1), (0, GRID_W - NA_CI)))
    return jnp.concatenate([padded[:, :NA_T2], padded[:, 1:NA_T2 + 1]], axis=2).reshape(h, NA_T2, LANES)


def _rpb_from_rows(rows):
    return rows[:, 1:, :NA_CI] + rows[:, :NA_RI, GRID_W:GRID_W + NA_CI]


def rpb_expand(rows, dep, name):
    n_heads = rows.shape[0]

    def body(r_ref, dep_ref, o_ref):
        for h in range(n_heads):
            for e in range(NA_T2):
                line = jnp.broadcast_to(r_ref[h, e:e + 1, :], (GRID_W, LANES))
                o_ref[h, e] = pltpu.roll(line, LANES - (NA_COLS - 1), 1, stride=1, stride_axis=0)

    return pl.pallas_call(
        body, name=name,
        in_specs=[pl.BlockSpec(memory_space=pltpu.VMEM), pl.BlockSpec(memory_space=pltpu.VMEM)],
        out_specs=pl.BlockSpec(memory_space=pltpu.VMEM),
        out_shape=jax.ShapeDtypeStruct((n_heads, NA_T2, GRID_W, LANES), F32),
        compiler_params=pltpu.CompilerParams(vmem_limit_bytes=V7X_VMEM_LIMIT),
    )(rows, dep)


def rpb_reduce(dt2, name):
    n_heads = dt2.shape[0]
    flip = jnp.asarray(np.eye(GRID_W)[::-1], BF16)

    def body(d_ref, j_ref, o_ref):
        jm = j_ref[...]
        for h in range(n_heads):
            for e in range(NA_T2):
                dv = d_ref[h, e]
                hi = dv.astype(BF16)
                mid = (dv - hi.astype(F32)).astype(BF16)
                lo = (dv - hi.astype(F32) - mid.astype(F32)).astype(BF16)
                rev = _dot(jm, hi) + _dot(jm, mid) + _dot(jm, lo)
                back = pltpu.roll(rev, LANES + (NA_COLS - 1) - (GRID_W - 1), 1, stride=1, stride_axis=0)
                o_ref[h, e:e + 1, :] = jnp.sum(back, axis=0, keepdims=True)

    return pl.pallas_call(
        body, name=name,
        in_specs=[pl.BlockSpec(memory_space=pltpu.VMEM)] * 2,
        out_specs=pl.BlockSpec(memory_space=pltpu.VMEM),
        out_shape=jax.ShapeDtypeStruct((n_heads, NA_T2, LANES), F32),
        compiler_params=pltpu.CompilerParams(vmem_limit_bytes=V7X_VMEM_LIMIT),
    )(dt2, flip)


NA_TQ = 4
NA_TK = NA_TQ + NA_ROWS
NA_KCH = NA_TK // 2


def _na_tile_geometry(t, rows):
    r = t * NA_TQ
    kbase = jnp.clip(r - NA_ROWS // 2, 0, rows - NA_TK)
    starts = [jnp.clip(r + a - NA_ROWS // 2, 0, rows - NA_ROWS) for a in range(NA_TQ)]
    return r, kbase, starts


def _na_tile_mask(kbase, starts, low, inwin):
    half = jnp.where(low, 0, 1)
    cols = []
    for c in range(NA_KCH):
        krow = kbase + 2 * c + half
        cols.append(jnp.concatenate(
            [jnp.where(inwin & (krow >= st) & (krow < st + NA_ROWS), 0.0, NEG) for st in starts], axis=0))
    return jnp.concatenate(cols, axis=1)


def _na_tile_index(r, kbase, a, c):
    return jnp.clip(kbase + 2 * c - (r + a) + NA_ROWS, 0, NA_T2 - 1)


def _na_tile_scores(q, k, t2_ref, hh, r, kbase, madd):
    bias = jnp.concatenate(
        [jnp.concatenate([t2_ref[hh, _na_tile_index(r, kbase, a, c)] for a in range(NA_TQ)], axis=0)
         for c in range(NA_KCH)], axis=1)
    return _dotg(q, k, NT) + bias + madd


def _softmax_rows(sc):
    e = jnp.exp(sc - jnp.max(sc, axis=1, keepdims=True))
    return e * (1.0 / jnp.sum(e, axis=1, keepdims=True))


def na_fwd(qa, ka, zq, t2, name):
    s = qa.shape[0]
    rows = s // GRID_W
    n_pairs = NA_WIDTH // LANES
    v_blk0 = (2 * NA_WIDTH) // LANES

    assert rows % NA_TQ == 0 and rows >= NA_TK
    tq, tk = NA_TQ * GRID_W, NA_TK * GRID_W

    def body(q_ref, k_ref, v_ref, t2_ref, o_ref, s_scr, p_scr):
        _, low, inwin = _na_iotas()

        def tile(t, carry):
            r, kbase, starts = _na_tile_geometry(t, rows)
            madd = _na_tile_mask(kbase, starts, low, inwin)
            qr = pl.ds(pl.multiple_of(r * GRID_W, tq), tq)
            kr = pl.ds(pl.multiple_of(kbase * GRID_W, tq), tk)
            for hh in range(2):
                lanes = slice(HEAD_DIM * hh, HEAD_DIM * (hh + 1))
                s_scr[tq * hh:tq * (hh + 1), :] = _na_tile_scores(q_ref[qr, lanes], k_ref[kr, lanes], t2_ref, hh, r,
                                                                  kbase, madd)
            p_scr[...] = _softmax_rows(s_scr[...]).astype(BF16)
            for hh in range(2):
                lanes = slice(HEAD_DIM * hh, HEAD_DIM * (hh + 1))
                o_ref[qr, lanes] = _dot(p_scr[tq * hh:tq * (hh + 1), :], v_ref[kr, lanes]).astype(BF16)
            return carry

        lax.fori_loop(0, rows // NA_TQ, tile, 0)

    col = lambda off: pl.BlockSpec((s, LANES), lambda p, _o=off: (0, _o + p))
    return pl.pallas_call(
        body, name=name, grid=(n_pairs,),
        in_specs=[col(0), col(0), col(v_blk0),
                  pl.BlockSpec((2, NA_T2, GRID_W, LANES), lambda p: (p, 0, 0, 0))],
        out_specs=col(0),
        out_shape=jax.ShapeDtypeStruct((s, NA_WIDTH), BF16),
        scratch_shapes=[pltpu.VMEM((2 * tq, tk), F32), pltpu.VMEM((2 * tq, tk), BF16)],
        compiler_params=_params(),
    )(qa, ka, zq, t2)


def na_bwd(qa, ka, zq, t2, o_na, do_na, name):
    s = qa.shape[0]
    rows = s // GRID_W
    n_pairs = NA_WIDTH // LANES
    v_blk0 = (2 * NA_WIDTH) // LANES

    tq, tk = NA_TQ * GRID_W, NA_TK * GRID_W

    def body(q_ref, k_ref, v_ref, t2_ref, o_ref, do_ref, dq_ref, dk_ref, dv_ref, dt2_ref):
        _, low, inwin = _na_iotas()
        dk_ref[...] = jnp.zeros(dk_ref.shape, F32)
        dv_ref[...] = jnp.zeros(dv_ref.shape, F32)
        dt2_ref[...] = jnp.zeros(dt2_ref.shape, F32)

        def tile(t, carry):
            r, kbase, starts = _na_tile_geometry(t, rows)
            madd = _na_tile_mask(kbase, starts, low, inwin)
            qr = pl.ds(pl.multiple_of(r * GRID_W, tq), tq)
            kr = pl.ds(pl.multiple_of(kbase * GRID_W, tq), tk)
            for hh in range(2):
                lanes = slice(HEAD_DIM * hh, HEAD_DIM * (hh + 1))
                q, k, v = q_ref[qr, lanes], k_ref[kr, lanes], v_ref[kr, lanes]
                p = _softmax_rows(_na_tile_scores(q, k, t2_ref, hh, r, kbase, madd))
                do = do_ref[qr, lanes]
                delta = jnp.sum(do.astype(F32) * o_ref[qr, lanes].astype(F32), axis=1, keepdims=True)
                ds = p * (_dotg(do, v, NT) - delta)
                shared = {}
                for a in range(NA_TQ):
                    for c in range(NA_KCH):
                        shared.setdefault(2 * c - a, []).append(
                            ds[GRID_W * a:GRID_W * (a + 1), LANES * c:LANES * (c + 1)])
                for offset, parts in shared.items():
                    e = jnp.clip(offset + kbase - r + NA_ROWS, 0, NA_T2 - 1)
                    dt2_ref[hh, e] = dt2_ref[hh, e] + functools.reduce(jnp.add, parts)
                dsb = ds.astype(BF16)
                dq_ref[qr, lanes] = _dot(dsb, k)
                dk_ref[kr, lanes] = dk_ref[kr, lanes] + _dotg(dsb, q, TN)
                dv_ref[kr, lanes] = dv_ref[kr, lanes] + _dotg(p.astype(BF16), do, TN)
            return carry

        lax.fori_loop(0, rows // NA_TQ, tile, 0)

    col = lambda off: pl.BlockSpec((s, LANES), lambda p, _o=off: (0, _o + p))
    t2spec = pl.BlockSpec((2, NA_T2, GRID_W, LANES), lambda p: (p, 0, 0, 0))
    return pl.pallas_call(
        body, name=name, grid=(n_pairs,),
        in_specs=[col(0), col(0), col(v_blk0), t2spec, col(0), col(0)],
        out_specs=[col(0), col(0), col(0), t2spec],
        out_shape=[jax.ShapeDtypeStruct((s, NA_WIDTH), F32)] * 3 + [jax.ShapeDtypeStruct(t2.shape, F32)],
        compiler_params=_params(),
    )(qa, ka, zq, t2, o_na, do_na)


def _t5_bucket_map():
    rel = np.arange(3 * SW_BLOCK)[None, :] - SW_BLOCK - np.arange(SW_BLOCK)[:, None]
    nb = REL_BUCKETS // 2
    max_exact = nb // 2
    n = np.abs(rel)
    large = max_exact + (np.log(np.maximum(n, 1) / max_exact)
                         / np.log(REL_MAX_DIST / max_exact) * (nb - max_exact)).astype(np.int32)
    large = np.minimum(large, nb - 1)
    return ((rel > 0) * nb + np.where(n < max_exact, n, large)).astype(np.int32)


def t5_expand(table, bmap, dep, name):
    def body(tab_ref, bm_ref, dep_ref, o_ref):
        bm = bm_ref[...]
        for h in range(SW_HEADS):
            t = jnp.zeros(bm.shape, F32)
            for b in range(REL_BUCKETS):
                t = jnp.where(bm == b, tab_ref[b, h], t)
            o_ref[h] = t

    return pl.pallas_call(
        body, name=name,
        in_specs=[pl.BlockSpec(memory_space=pltpu.SMEM), pl.BlockSpec(memory_space=pltpu.VMEM),
                  pl.BlockSpec(memory_space=pltpu.VMEM)],
        out_specs=pl.BlockSpec(memory_space=pltpu.VMEM),
        out_shape=jax.ShapeDtypeStruct((SW_HEADS,) + bmap.shape, F32),
        compiler_params=pltpu.CompilerParams(vmem_limit_bytes=V7X_VMEM_LIMIT),
    )(table, bmap, dep)


def t5_reduce(dbias_list, bmap, name):
    n = len(dbias_list)

    def body(*refs):
        d_refs, bm_ref, o_ref = refs[:n], refs[n], refs[n + 1]
        bm = bm_ref[...]
        for h in range(SW_HEADS):
            dv = d_refs[0][h]
            for other in d_refs[1:]:
                dv = dv + other[h]
            rows = [jnp.sum(jnp.where(bm == b, dv, 0.0), axis=0, keepdims=True) for b in range(REL_BUCKETS)]
            r = jnp.concatenate(rows, axis=0)
            o_ref[h] = jnp.broadcast_to(jnp.sum(r, axis=1, keepdims=True), (REL_BUCKETS, LANES))

    return pl.pallas_call(
        body, name=name,
        in_specs=[pl.BlockSpec(memory_space=pltpu.VMEM)] * (n + 1),
        out_specs=pl.BlockSpec(memory_space=pltpu.VMEM),
        out_shape=jax.ShapeDtypeStruct((SW_HEADS, REL_BUCKETS, LANES), F32),
        compiler_params=pltpu.CompilerParams(vmem_limit_bytes=V7X_VMEM_LIMIT),
    )(*dbias_list, bmap)


def _sw_mask_iotas():
    a = lax.broadcasted_iota(jnp.int32, (SW_BLOCK, 3 * SW_BLOCK), 0)
    j = lax.broadcasted_iota(jnp.int32, (SW_BLOCK, 3 * SW_BLOCK), 1)
    inwin = jnp.abs(j - SW_BLOCK - a) <= SW_BLOCK
    return j, inwin


SW_STACK = SW_HEADS * SW_BLOCK


def _sw_softmax(sc, sk):
    m = jnp.maximum(jnp.max(sc, axis=1, keepdims=True), sk)
    e = jnp.exp(sc - m)
    es = jnp.exp(sk - m)
    inv = 1.0 / (jnp.sum(e, axis=1, keepdims=True) + es)
    return e * inv, es * inv


def _sw_prologue(k_ref, v_ref, kp, vp, sink_ref, s):
    pad = s + 2 * SW_BLOCK
    zeros = jnp.zeros((SW_BLOCK, SW_KV_WIDTH), BF16)
    kp[0:SW_BLOCK, :] = zeros
    vp[0:SW_BLOCK, :] = zeros
    kp[SW_BLOCK + s:pad, :] = zeros
    vp[SW_BLOCK + s:pad, :] = zeros
    kp[SW_BLOCK:SW_BLOCK + s, :] = k_ref[...]
    vp[SW_BLOCK:SW_BLOCK + s, :] = v_ref[...]
    return jnp.concatenate([jnp.full((SW_BLOCK, 1), sink_ref[h], F32) for h in range(SW_HEADS)], axis=0)


def sw_fwd(qs, ks, zq, t5b, sink, dep, name):
    s = qs.shape[0]
    nb = s // SW_BLOCK
    v_blk = (3 * NA_WIDTH + SW_Q_WIDTH + SW_KV_WIDTH) // LANES
    pad = s + 2 * SW_BLOCK

    def body(q_ref, k_ref, v_ref, b_ref, sink_ref, dep_ref, o_ref, kp, vp, s_scr, p_scr):
        sink_col = _sw_prologue(k_ref, v_ref, kp, vp, sink_ref, s)
        j, inwin = _sw_mask_iotas()

        def blk(n, carry):
            kpos = n * SW_BLOCK - SW_BLOCK + j
            madd = jnp.where(inwin & (kpos >= 0) & (kpos < s), 0.0, NEG)
            q0 = pl.multiple_of(n * SW_BLOCK, SW_BLOCK)
            qr, kr = pl.ds(q0, SW_BLOCK), pl.ds(q0, 3 * SW_BLOCK)
            for h in range(SW_HEADS):
                g = h // SW_REP
                s_scr[SW_BLOCK * h:SW_BLOCK * (h + 1), :] = _dotg(
                    q_ref[qr, HEAD_DIM * h:HEAD_DIM * (h + 1)], kp[kr, HEAD_DIM * g:HEAD_DIM * (g + 1)], NT) + madd
            p, _ = _sw_softmax(s_scr[...] + b_ref[...], sink_col)
            p_scr[...] = p.astype(BF16)
            for h in range(SW_HEADS):
                g = h // SW_REP
                o_ref[qr, HEAD_DIM * h:HEAD_DIM * (h + 1)] = _dot(
                    p_scr[SW_BLOCK * h:SW_BLOCK * (h + 1), :], vp[kr, HEAD_DIM * g:HEAD_DIM * (g + 1)]).astype(BF16)
            return carry

        lax.fori_loop(0, nb, blk, 0)

    return pl.pallas_call(
        body, name=name, grid=(1,),
        in_specs=[_full((s, SW_Q_WIDTH)), _full((s, SW_KV_WIDTH)),
                  pl.BlockSpec((s, SW_KV_WIDTH), lambda i: (0, v_blk)),
                  _full((SW_STACK, 3 * SW_BLOCK)), pl.BlockSpec(memory_space=pltpu.SMEM),
                  _full(dep.shape)],
        out_specs=_full((s, SW_Q_WIDTH)),
        out_shape=jax.ShapeDtypeStruct((s, SW_Q_WIDTH), BF16),
        scratch_shapes=[pltpu.VMEM((pad, SW_KV_WIDTH), BF16), pltpu.VMEM((pad, SW_KV_WIDTH), BF16),
                        pltpu.VMEM((SW_STACK, 3 * SW_BLOCK), F32), pltpu.VMEM((SW_STACK, 3 * SW_BLOCK), BF16)],
        compiler_params=_params(),
    )(qs, ks, zq, t5b, sink, dep)


def sw_bwd(qs, ks, zq, t5b, sink, o_sw, do_sw, name):
    s = qs.shape[0]
    nb = s // SW_BLOCK
    v_blk = (3 * NA_WIDTH + SW_Q_WIDTH + SW_KV_WIDTH) // LANES
    pad = s + 2 * SW_BLOCK

    def body(q_ref, k_ref, v_ref, b_ref, sink_ref, o_ref, do_ref,
             dq_ref, dk_ref, dv_ref, db_ref, dsk_ref, kp, vp, dkp, dvp, s_scr, dp_scr, ds_scr, p_scr):
        sink_col = _sw_prologue(k_ref, v_ref, kp, vp, sink_ref, s)
        dkp[...] = jnp.zeros(dkp.shape, F32)
        dvp[...] = jnp.zeros(dvp.shape, F32)
        db_ref[...] = jnp.zeros(db_ref.shape, F32)
        dsk_ref[...] = jnp.zeros(dsk_ref.shape, F32)
        j, inwin = _sw_mask_iotas()

        def blk(n, carry):
            kpos = n * SW_BLOCK - SW_BLOCK + j
            madd = jnp.where(inwin & (kpos >= 0) & (kpos < s), 0.0, NEG)
            q0 = pl.multiple_of(n * SW_BLOCK, SW_BLOCK)
            qr, kr = pl.ds(q0, SW_BLOCK), pl.ds(q0, 3 * SW_BLOCK)
            deltas = []
            for h in range(SW_HEADS):
                g = h // SW_REP
                hl, kl = slice(HEAD_DIM * h, HEAD_DIM * (h + 1)), slice(HEAD_DIM * g, HEAD_DIM * (g + 1))
                rows = slice(SW_BLOCK * h, SW_BLOCK * (h + 1))
                do = do_ref[qr, hl]
                s_scr[rows, :] = _dotg(q_ref[qr, hl], kp[kr, kl], NT) + madd
                dp_scr[rows, :] = _dotg(do, vp[kr, kl], NT)
                deltas.append(jnp.sum(do.astype(F32) * o_ref[qr, hl].astype(F32), axis=1, keepdims=True))
            delta = jnp.concatenate(deltas, axis=0)
            p, ps = _sw_softmax(s_scr[...] + b_ref[...], sink_col)
            ds = p * (dp_scr[...] - delta)
            db_ref[...] = db_ref[...] + ds
            dsk_ref[...] = dsk_ref[...] - jnp.broadcast_to(ps * delta, (SW_STACK, LANES))
            ds_scr[...] = ds.astype(BF16)
            p_scr[...] = p.astype(BF16)
            for g in range(SW_HEADS // SW_REP):
                kl = slice(HEAD_DIM * g, HEAD_DIM * (g + 1))
                k = kp[kr, kl]
                dkw = jnp.zeros((3 * SW_BLOCK, HEAD_DIM), F32)
                dvw = jnp.zeros((3 * SW_BLOCK, HEAD_DIM), F32)
                for r in range(SW_REP):
                    h = g * SW_REP + r
                    hl, rows = slice(HEAD_DIM * h, HEAD_DIM * (h + 1)), slice(SW_BLOCK * h, SW_BLOCK * (h + 1))
                    dsb = ds_scr[rows, :]
                    dq_ref[qr, hl] = _dot(dsb, k)
                    dkw = dkw + _dotg(dsb, q_ref[qr, hl], TN)
                    dvw = dvw + _dotg(p_scr[rows, :], do_ref[qr, hl], TN)
                dkp[kr, kl] = dkp[kr, kl] + dkw
                dvp[kr, kl] = dvp[kr, kl] + dvw
            return carry

        lax.fori_loop(0, nb, blk, 0)
        dk_ref[...] = dkp[SW_BLOCK:SW_BLOCK + s, :]
        dv_ref[...] = dvp[SW_BLOCK:SW_BLOCK + s, :]

    bias_spec = _full((SW_STACK, 3 * SW_BLOCK))
    return pl.pallas_call(
        body, name=name, grid=(1,),
        in_specs=[_full((s, SW_Q_WIDTH)), _full((s, SW_KV_WIDTH)),
                  pl.BlockSpec((s, SW_KV_WIDTH), lambda i: (0, v_blk)),
                  bias_spec, pl.BlockSpec(memory_space=pltpu.SMEM),
                  _full((s, SW_Q_WIDTH)), _full((s, SW_Q_WIDTH))],
        out_specs=[_full((s, SW_Q_WIDTH)), _full((s, SW_KV_WIDTH)), _full((s, SW_KV_WIDTH)), bias_spec,
                   _full((SW_STACK, LANES))],
        out_shape=[jax.ShapeDtypeStruct((s, SW_Q_WIDTH), F32), jax.ShapeDtypeStruct((s, SW_KV_WIDTH), F32),
                   jax.ShapeDtypeStruct((s, SW_KV_WIDTH), F32),
                   jax.ShapeDtypeStruct((SW_STACK, 3 * SW_BLOCK), F32),
                   jax.ShapeDtypeStruct((SW_STACK, LANES), F32)],
        scratch_shapes=[pltpu.VMEM((pad, SW_KV_WIDTH), BF16), pltpu.VMEM((pad, SW_KV_WIDTH), BF16),
                        pltpu.VMEM((pad, SW_KV_WIDTH), F32), pltpu.VMEM((pad, SW_KV_WIDTH), F32),
                        pltpu.VMEM((SW_STACK, 3 * SW_BLOCK), F32), pltpu.VMEM((SW_STACK, 3 * SW_BLOCK), F32),
                        pltpu.VMEM((SW_STACK, 3 * SW_BLOCK), BF16), pltpu.VMEM((SW_STACK, 3 * SW_BLOCK), BF16)],
        compiler_params=_params(),
    )(qs, ks, zq, t5b, sink, o_sw, do_sw)


def merge_out(x, o_na, o_sw, gt, wbna_t, wbsw_t, wout, name):
    s, d = x.shape
    tm = _row_tile(s)

    def body(x_ref, ona_ref, osw_ref, gt_ref, wna_ref, wsw_ref, wo_ref, xo_ref, ana_ref, asw_ref, mg_ref):
        a_na = _dotg(ona_ref[...], wna_ref[...], NT)
        a_sw = _dotg(osw_ref[...], wsw_ref[...], NT)
        g_na, g_sw = gt_ref[:, 0:d].astype(F32), gt_ref[:, d:2 * d].astype(F32)
        ana_ref[...] = (a_na * g_na * (1.0 - g_na)).astype(BF16)
        asw_ref[...] = (a_sw * g_sw * (1.0 - g_sw)).astype(BF16)
        merged = (g_na * a_na + g_sw * a_sw).astype(BF16)
        mg_ref[...] = merged
        xo_ref[...] = x_ref[...] + _dot(merged, wo_ref[...])

    return pl.pallas_call(
        body, name=name, grid=(s // tm,),
        in_specs=[_rows(tm, d), _rows(tm, 512), _rows(tm, 512), _rows(tm, 2 * d),
                  _mat(*wbna_t), _mat(*wbsw_t), _mat(*wout)],
        out_specs=[_rows(tm, d)] * 4,
        out_shape=[jax.ShapeDtypeStruct((s, d), F32)] + [jax.ShapeDtypeStruct((s, d), BF16)] * 3,
        compiler_params=_params(),
    )(x, o_na, o_sw, gt, wbna_t[0], wbsw_t[0], wout[0])


def mix_bwd_out(dx, gt, a_na, a_sw, wbna_t, wbsw_t, wout, dep, name):
    s, d = dx.shape
    tm = _row_tile(s)

    def body(dx_ref, gt_ref, ana_ref, asw_ref, wna_ref, wsw_ref, wo_ref, dep_ref,
             dxb_ref, dzg_ref, dana_ref, dasw_ref, dona_ref, dosw_ref, dbg_ref):
        @pl.when(pl.program_id(0) == 0)
        def _():
            dbg_ref[...] = jnp.zeros(dbg_ref.shape, F32)

        dxb = dx_ref[...].astype(BF16)
        dxb_ref[...] = dxb
        dm = _dotg(dxb, wo_ref[...], NT)
        for i, (a_ref, da_ref, w_ref, do_ref) in enumerate(
                [(ana_ref, dana_ref, wna_ref, dona_ref), (asw_ref, dasw_ref, wsw_ref, dosw_ref)]):
            gi = gt_ref[:, i * d:(i + 1) * d].astype(F32)
            da = (dm * gi).astype(BF16)
            da_ref[...] = da
            do_ref[...] = _dot(da, w_ref[...]).astype(BF16)
            dzg = dm * a_ref[...].astype(F32)
            dzg_ref[:, i * d:(i + 1) * d] = dzg.astype(BF16)
            dbg_ref[:, i * d:(i + 1) * d] = dbg_ref[:, i * d:(i + 1) * d] + jnp.sum(dzg, axis=0, keepdims=True)

    return pl.pallas_call(
        body, name=name, grid=(s // tm,),
        in_specs=[_rows(tm, d), _rows(tm, 2 * d), _rows(tm, d), _rows(tm, d),
                  _mat(*wbna_t), _mat(*wbsw_t), _mat(*wout), _full(dep.shape)],
        out_specs=[_rows(tm, d), _rows(tm, 2 * d), _rows(tm, d), _rows(tm, d), _rows(tm, 512), _rows(tm, 512),
                   _full((1, 2 * d))],
        out_shape=[jax.ShapeDtypeStruct((s, d), BF16), jax.ShapeDtypeStruct((s, 2 * d), BF16),
                   jax.ShapeDtypeStruct((s, d), BF16), jax.ShapeDtypeStruct((s, d), BF16),
                   jax.ShapeDtypeStruct((s, 512), BF16), jax.ShapeDtypeStruct((s, 512), BF16),
                   jax.ShapeDtypeStruct((1, 2 * d), F32)],
        compiler_params=_params(),
    )(dx, gt, a_na, a_sw, wbna_t[0], wbsw_t[0], wout[0], dep)


def qk_norm_bwd(dqa, dka, dva, dqs, dks, dvs, zq, dzg, gq_na, gk_na, gq_sw, gk_sw, bd, name):
    s = zq.shape[0]
    d2 = dzg.shape[1]
    n_in = QKV_WIDTH + d2
    tm = _row_tile(s)

    def body(dqa_ref, dka_ref, dva_ref, dqs_ref, dks_ref, dvs_ref, zq_ref, dzg_ref,
             gqa_ref, gka_ref, gqs_ref, gks_ref, bd_ref, dz_ref, dgqa_ref, dgka_ref, dgqs_ref, dgks_ref):
        @pl.when(pl.program_id(0) == 0)
        def _():
            for r in (dgqa_ref, dgka_ref, dgqs_ref, dgks_ref):
                r[...] = jnp.zeros(r.shape, F32)

        bd512 = bd_ref[...]
        bd128 = bd_ref[0:SW_KV_WIDTH, 0:SW_KV_WIDTH]

        def one(c0, c1, dy_ref, g_ref, dg_ref, bdm, scale):
            z = zq_ref[:, c0:c1].astype(F32)
            r = lax.rsqrt(_group_mean(z * z, bdm) + EPS)
            zh = z * r
            dy = dy_ref[...] * scale
            dyg = dy * g_ref[...]
            dz = r * (dyg - zh * _group_mean(dyg * zh, bdm))
            dz_ref[:, c0:c1] = dz.astype(BF16)
            dg_ref[...] = dg_ref[...] + jnp.sum(dy * zh, axis=0, keepdims=True)

        one(0, 512, dqa_ref, gqa_ref, dgqa_ref, bd512, SCALE)
        one(512, 1024, dka_ref, gka_ref, dgka_ref, bd512, 1.0)
        dz_ref[:, 1024:1536] = dva_ref[...].astype(BF16)
        one(1536, 2048, dqs_ref, gqs_ref, dgqs_ref, bd512, SCALE)
        one(2048, 2176, dks_ref, gks_ref, dgks_ref, bd128, 1.0)
        dz_ref[:, 2176:2304] = dvs_ref[...].astype(BF16)
        dz_ref[:, QKV_WIDTH:n_in] = dzg_ref[...]

    return pl.pallas_call(
        body, name=name, grid=(s // tm,),
        in_specs=[_rows(tm, 512), _rows(tm, 512), _rows(tm, 512), _rows(tm, 512), _rows(tm, 128), _rows(tm, 128),
                  _rows(tm, QKV_WIDTH), _rows(tm, d2),
                  _full((1, 512)), _full((1, 512)), _full((1, 512)), _full((1, 128)), _full((MXU_TILE, MXU_TILE))],
        out_specs=[_rows(tm, n_in), _full((1, 512)), _full((1, 512)), _full((1, 512)), _full((1, 128))],
        out_shape=[jax.ShapeDtypeStruct((s, n_in), BF16)] + [jax.ShapeDtypeStruct((1, 512), F32)] * 3
                  + [jax.ShapeDtypeStruct((1, 128), F32)],
        compiler_params=_params(),
    )(dqa, dka, dva, dqs, dks, dvs, zq, dzg, gq_na, gk_na, gq_sw, gk_sw, bd)


def ffn_bwd_act(dx, wd, hg, hu, name):
    s, d = dx.shape
    f = wd[0].shape[1]
    tm = min(_row_tile(s), 256)
    fc = _col_chunk(f)

    def body(dx_ref, w_ref, hg_ref, hu_ref, dxb_ref, dhg_ref, dhu_ref):
        dxv = dx_ref[...]
        dxb_ref[...] = dxv.astype(BF16)
        half = (0.5 * dxv).astype(BF16)
        for c0 in range(0, f, fc):
            dact = _dotg(half, w_ref[c0:c0 + fc, :], NT)
            dhu_ref[:, c0:c0 + fc] = (dact * hu_ref[:, c0:c0 + fc].astype(F32)).astype(BF16)
            dhg_ref[:, c0:c0 + fc] = (dact * hg_ref[:, c0:c0 + fc].astype(F32)).astype(BF16)

    return pl.pallas_call(
        body, name=name, grid=(s // tm,),
        in_specs=[_rows(tm, d), _mat(*wd), _rows(tm, f), _rows(tm, f)],
        out_specs=[_rows(tm, d), _rows(tm, f), _rows(tm, f)],
        out_shape=[jax.ShapeDtypeStruct((s, d), BF16), jax.ShapeDtypeStruct((s, f), BF16),
                   jax.ShapeDtypeStruct((s, f), BF16)],
        compiler_params=_params(),
    )(dx, wd[0], hg, hu)


def proj_bwd_norm(acts, weights, x, gain, dx, dep, name):
    s, d = x.shape
    tm = min(_row_tile(s), 256)
    n = len(acts)

    def body(*refs):
        a_refs, w_refs = refs[:n], refs[n:2 * n]
        x_ref, g_ref, dx_ref, _, o_ref, dg_ref = refs[2 * n:]

        @pl.when(pl.program_id(0) == 0)
        def _():
            dg_ref[...] = jnp.zeros(dg_ref.shape, F32)

        dxn = _dot(a_refs[0][...], w_refs[0][...])
        for a_ref, w_ref in zip(a_refs[1:], w_refs[1:]):
            dxn = dxn + _dot(a_ref[...], w_ref[...])
        xv = x_ref[...]
        r = _rstd(xv)
        xh = xv * r
        dxh = dxn * g_ref[...]
        o_ref[...] = dx_ref[...] + r * (dxh - xh * jnp.mean(dxh * xh, axis=-1, keepdims=True))
        dg_ref[...] = dg_ref[...] + jnp.sum(dxn * xh, axis=0, keepdims=True)

    return pl.pallas_call(
        body, name=name, grid=(s // tm,),
        in_specs=[_rows(tm, a.shape[1]) for a in acts] + [_mat(*w) for w in weights]
                 + [_rows(tm, d), _full((1, d)), _rows(tm, d), _full(dep.shape)],
        out_specs=[_rows(tm, d), _full((1, d))],
        out_shape=[jax.ShapeDtypeStruct((s, d), F32), jax.ShapeDtypeStruct((1, d), F32)],
        compiler_params=_params(),
    )(*acts, *[w[0] for w in weights], x, gain, dx, dep)


def tn_matmul(products, name):
    s, n = products[0][0].shape
    tn = _col_chunk(n)
    rhs = []
    for _, b, _ in products:
        if not any(b is seen for seen in rhs):
            rhs.append(b)
    which = [next(i for i, seen in enumerate(rhs) if b is seen) for _, b, _ in products]
    npr, nr = len(products), len(rhs)

    def body(*refs):
        a_refs, b_refs, o_refs = refs[:npr], refs[npr:npr + nr], refs[npr + nr:]
        for i, (_, _, scale) in enumerate(products):
            o_refs[i][...] = (scale * _dotg(a_refs[i][...], b_refs[which[i]][...], TN)).astype(BF16)

    return pl.pallas_call(
        body, name=name, grid=(n // tn,),
        in_specs=[pl.BlockSpec((s, tn), lambda i: (0, i))] * npr
                 + [pl.BlockSpec(b.shape, lambda i: (0, 0), pipeline_mode=ONCE) for b in rhs],
        out_specs=[pl.BlockSpec((tn, b.shape[1]), lambda i: (i, 0)) for _, b, _ in products],
        out_shape=[jax.ShapeDtypeStruct((n, b.shape[1]), BF16) for _, b, _ in products],
        compiler_params=_params(),
    )(*[a for a, _, _ in products], *rhs)


def _mesh_pos():
    return lax.axis_index("x"), lax.axis_index("y"), lax.axis_index("c")


def _peers():
    x, y, c = _mesh_pos()
    peers = []
    for rel in range(1, N_DEV):
        peers.append((1 - x if rel & 4 else x, 1 - y if rel & 2 else y, 1 - c if rel & 1 else c))
    return 4 * x + 2 * y + c, peers


HBM_SPEC = pl.BlockSpec(memory_space=pltpu.HBM)
SEM_SPEC = pl.BlockSpec(memory_space=pltpu.SEMAPHORE)


def _split_call(body, name, thru, n_sems, extra=(), with_token=True):
    hbm = lambda t: pltpu.with_memory_space_constraint(t, pltpu.HBM)
    effect = pltpu.CompilerParams(has_side_effects=pltpu.SideEffectType.DATAFLOW_SIDE_EFFECTING)
    nt = len(thru)
    thru_shapes = [pltpu.HBM(t.shape, t.dtype) for t in thru]
    if with_token:
        (after,) = extra
        outs = pl.pallas_call(
            body, name=name, in_specs=[HBM_SPEC] * nt + [pl.BlockSpec(memory_space=pl.ANY)],
            out_specs=[SEM_SPEC] * len(n_sems) + [HBM_SPEC] * nt + [pl.BlockSpec(memory_space=pltpu.VMEM)],
            out_shape=[pltpu.SemaphoreType.DMA((k,)) for k in n_sems] + thru_shapes
                      + [jax.ShapeDtypeStruct((8, LANES), F32)],
            input_output_aliases={i: len(n_sems) + i for i in range(nt)}, compiler_params=effect,
        )(*[hbm(t) for t in thru], after)
        return outs[:len(n_sems)], outs[len(n_sems):-1], outs[-1]
    return pl.pallas_call(
        body, name=name,
        in_specs=[HBM_SPEC] * nt + [SEM_SPEC] * len(n_sems) + [pl.BlockSpec(memory_space=pl.ANY)],
        out_specs=[HBM_SPEC] * nt, out_shape=thru_shapes,
        input_output_aliases={i: i for i in range(nt)}, compiler_params=effect,
    )(*thru, *extra)


def _gather_targets():
    x, y, c = _mesh_pos()
    return 4 * x + 2 * y + c, [(x, y, 1 - c), (1 - x, y, c), (x, 1 - y, c), (1 - x, 1 - y, c)]


def gather_start(shards, after, name):
    n = len(shards)
    zones = [lax.empty((w.shape[0], N_DEV) + w.shape[1:], w.dtype) for w in shards]

    def body(*refs):
        ins, zs = refs[:n], refs[n:2 * n]
        send_sems, recv_sems, local_sems = refs[2 * n + 1:2 * n + 4]
        token = refs[-1]
        me, targets = _gather_targets()
        for a in range(n):
            pltpu.make_async_copy(ins[a], zs[a].at[:, me], local_sems.at[a]).start()
            for k, to in enumerate(targets):
                pltpu.make_async_remote_copy(
                    src_ref=ins[a], dst_ref=zs[a].at[:, me], send_sem=send_sems.at[4 * a + k],
                    recv_sem=recv_sems.at[4 * a + k], device_id=to, device_id_type=MESH).start()
        token[...] = jnp.zeros(token.shape, F32)

    sems, thru, token = _split_call(body, name, list(shards) + zones, (4 * n, 4 * n, n), extra=(after,))
    return (sems, thru, n), token


def gather_wait(started, after, name):
    sems, thru, n = started

    def body(*refs):
        zs = refs[n:2 * n]
        send_sems, recv_sems, local_sems = refs[2 * n:2 * n + 3]
        _, targets = _gather_targets()
        for a in range(n):
            for k, to in enumerate(targets):
                cp = pltpu.make_async_remote_copy(
                    src_ref=zs[a].at[:, 0], dst_ref=zs[a].at[:, 0], send_sem=send_sems.at[4 * a + k],
                    recv_sem=recv_sems.at[4 * a + k], device_id=to, device_id_type=MESH)
                cp.wait_send()
                cp.wait_recv()
            pltpu.make_async_copy(zs[a].at[:, 0], zs[a].at[:, 0], local_sems.at[a]).wait()

    return _split_call(body, name, thru, (4 * n, 4 * n, n), extra=(*sems, after), with_token=False)[n:]


def forward_start(zones, after, name):
    n = len(zones)

    def body(*refs):
        zs = refs[:n]
        send_sems, recv_sems = refs[n + 1:n + 3]
        token = refs[-1]
        x, y, c = _mesh_pos()
        for a in range(n):
            for j, chip in enumerate([(1 - x, y), (x, 1 - y), (1 - x, 1 - y)]):
                blk = zs[a].at[:, 4 * chip[0] + 2 * chip[1] + c]
                pltpu.make_async_remote_copy(
                    src_ref=blk, dst_ref=blk, send_sem=send_sems.at[3 * a + j], recv_sem=recv_sems.at[3 * a + j],
                    device_id=(x, y, 1 - c), device_id_type=MESH).start()
        token[...] = jnp.zeros(token.shape, F32)

    sems, thru, token = _split_call(body, name, list(zones), (3 * n, 3 * n), extra=(after,))
    return (sems, thru, n), token


def forward_wait(started, after, name):
    sems, thru, n = started

    def body(*refs):
        zs = refs[:n]
        send_sems, recv_sems = refs[n:n + 2]
        x, y, c = _mesh_pos()
        for a in range(n):
            for j in range(3):
                cp = pltpu.make_async_remote_copy(
                    src_ref=zs[a].at[:, 0], dst_ref=zs[a].at[:, 0], send_sem=send_sems.at[3 * a + j],
                    recv_sem=recv_sems.at[3 * a + j], device_id=(x, y, 1 - c), device_id_type=MESH)
                cp.wait_send()
                cp.wait_recv()

    return _split_call(body, name, thru, (3 * n, 3 * n), extra=(*sems, after), with_token=False)


def scatter_start(groups, name):
    n = len(groups)
    flat = [g for grp in groups for g in grp]
    nf = len(flat)
    offs = np.cumsum([0] + [len(grp) for grp in groups])
    lands = [lax.empty((N_DEV, len(grp)) + grp[0].shape[1:], grp[0].dtype) for grp in groups]

    def body(*refs):
        ins, zones = refs[:nf], refs[nf:nf + n]
        send_sems, recv_sems, local_sems = refs[nf + n:nf + n + 3]
        token = refs[-1]
        me, peers = _peers()
        for a in range(n):
            for w in range(len(groups[a])):
                pltpu.make_async_copy(ins[offs[a] + w].at[me], zones[a].at[me, w], local_sems.at[a]).start()
        for k, peer in enumerate(peers):
            p_id = 4 * peer[0] + 2 * peer[1] + peer[2]
            for a in range(n):
                for w in range(len(groups[a])):
                    pltpu.make_async_remote_copy(
                        src_ref=ins[offs[a] + w].at[p_id], dst_ref=zones[a].at[me, w],
                        send_sem=send_sems.at[7 * a + k], recv_sem=recv_sems.at[7 * a + k],
                        device_id=peer, device_id_type=MESH).start()
        token[...] = jnp.zeros(token.shape, F32)

    hbm = lambda t: pltpu.with_memory_space_constraint(t, pltpu.HBM)
    outs = pl.pallas_call(
        body, name=name,
        in_specs=[HBM_SPEC] * (nf + n),
        out_specs=[SEM_SPEC] * 3 + [HBM_SPEC] * (nf + n) + [pl.BlockSpec(memory_space=pltpu.VMEM)],
        out_shape=[pltpu.SemaphoreType.DMA((7 * n,)), pltpu.SemaphoreType.DMA((7 * n,)), pltpu.SemaphoreType.DMA((n,))]
                  + [pltpu.HBM(t.shape, t.dtype) for t in flat + lands]
                  + [jax.ShapeDtypeStruct((8, LANES), F32)],
        input_output_aliases={i: 3 + i for i in range(nf + n)},
        compiler_params=pltpu.CompilerParams(has_side_effects=pltpu.SideEffectType.DATAFLOW_SIDE_EFFECTING),
    )(*[hbm(t) for t in flat], *[hbm(t) for t in lands])
    sems, thru, token = outs[:3], outs[3:3 + nf + n], outs[-1]
    return (sems, thru, [len(grp) for grp in groups]), token


def scatter_wait(started, after, name):
    (send_sems, recv_sems, local_sems), thru, sizes = started
    n = len(sizes)
    nf = len(thru) - n

    def body(*refs):
        zones = refs[nf:nf + n]
        s_sems, r_sems, l_sems = refs[nf + n:nf + n + 3]
        me, peers = _peers()
        for a in range(n):
            for k, peer in enumerate(peers):
                cp = pltpu.make_async_remote_copy(
                    src_ref=zones[a].at[0], dst_ref=zones[a].at[0],
                    send_sem=s_sems.at[7 * a + k], recv_sem=r_sems.at[7 * a + k], device_id=peer,
                    device_id_type=MESH)
                cp.wait_send()
                cp.wait_recv()
            pltpu.make_async_copy(zones[a].at[0], zones[a].at[0], l_sems.at[a]).wait()

    outs = pl.pallas_call(
        body, name=name,
        in_specs=[HBM_SPEC] * (nf + n) + [SEM_SPEC] * 3 + [pl.BlockSpec(memory_space=pl.ANY)],
        out_specs=[HBM_SPEC] * (nf + n),
        out_shape=[pltpu.HBM(t.shape, t.dtype) for t in thru],
        input_output_aliases={i: i for i in range(nf + n)},
        compiler_params=pltpu.CompilerParams(has_side_effects=pltpu.SideEffectType.DATAFLOW_SIDE_EFFECTING),
    )(*thru, send_sems, recv_sems, local_sems, after)
    return outs[nf:]


def pair_start(grads, after, name):
    nw = len(grads)
    land = lax.empty((4, nw) + grads[0].shape[1:], grads[0].dtype)

    def body(*refs):
        ins, zone = refs[:nw], refs[nw]
        send_sems, recv_sems = refs[nw + 2:nw + 4]
        x, y, c = _mesh_pos()
        for j in range(4):
            for w in range(nw):
                pltpu.make_async_remote_copy(
                    src_ref=ins[w].at[2 * j + (1 - c)], dst_ref=zone.at[j, w], send_sem=send_sems.at[0],
                    recv_sem=recv_sems.at[0], device_id=(x, y, 1 - c), device_id_type=MESH).start()
        refs[-1][...] = jnp.zeros(refs[-1].shape, F32)

    sems, thru, token = _split_call(body, name, list(grads) + [land], (1, 1), extra=(after,))
    return (sems, thru, nw), token


def pair_wait(started, after, name):
    sems, thru, nw = started

    def body(*refs):
        zone = refs[nw]
        send_sems, recv_sems = refs[nw + 1:nw + 3]
        x, y, c = _mesh_pos()
        cp = pltpu.make_async_remote_copy(src_ref=zone, dst_ref=zone, send_sem=send_sems.at[0],
                                          recv_sem=recv_sems.at[0], device_id=(x, y, 1 - c), device_id_type=MESH)
        cp.wait_send()
        cp.wait_recv()

    outs = _split_call(body, name, thru, (1, 1), extra=(*sems, after), with_token=False)
    return outs[:nw], outs[nw]


def pair_sum(grads, land, name):
    nw = len(grads)
    _, r, c_dim = grads[0].shape

    def body(*refs):
        g_refs, l_ref, o_ref = refs[:nw], refs[nw], refs[nw + 1]
        core = lax.axis_index("c")
        for w in range(nw):
            o_ref[0, w] = (g_refs[w][0, core].astype(F32) + l_ref[0, w].astype(F32)).astype(BF16)

    return pl.pallas_call(
        body, name=name, grid=(4,),
        in_specs=[pl.BlockSpec((1, 2, r, c_dim), lambda j: (j, 0, 0, 0))] * nw
                 + [pl.BlockSpec((1, nw, r, c_dim), lambda j: (j, 0, 0, 0))],
        out_specs=pl.BlockSpec((1, nw, r, c_dim), lambda j: (j, 0, 0, 0)),
        out_shape=jax.ShapeDtypeStruct((4, nw, r, c_dim), BF16),
        compiler_params=_params(),
    )(*[g.reshape(4, 2, r, c_dim) for g in grads], land)


def _other_chips():
    x, y, c = _mesh_pos()
    chips = []
    for rel in range(1, 4):
        px, py = (1 - x if rel & 2 else x), (1 - y if rel & 1 else y)
        chips.append((px, py, 2 * px + py))
    return 2 * x + y, c, chips


def chip_start(pair_sums, after, name):
    land = lax.empty(pair_sums.shape, pair_sums.dtype)

    def body(*refs):
        h_ref, zone = refs[0], refs[1]
        send_sems, recv_sems, local_sem = refs[3:6]
        mine, c, chips = _other_chips()
        pltpu.make_async_copy(h_ref.at[mine], zone.at[mine], local_sem.at[0]).start()
        for k, (px, py, j) in enumerate(chips):
            pltpu.make_async_remote_copy(
                src_ref=h_ref.at[j], dst_ref=zone.at[mine], send_sem=send_sems.at[k], recv_sem=recv_sems.at[k],
                device_id=(px, py, c), device_id_type=MESH).start()
        refs[-1][...] = jnp.zeros(refs[-1].shape, F32)

    sems, thru, token = _split_call(body, name, [pair_sums, land], (3, 3, 1), extra=(after,))
    return (sems, thru), token


def chip_wait(started, after, name):
    sems, thru = started

    def body(*refs):
        zone = refs[1]
        send_sems, recv_sems, local_sem = refs[2:5]
        _, c, chips = _other_chips()
        for k, (px, py, _) in enumerate(chips):
            cp = pltpu.make_async_remote_copy(
                src_ref=zone.at[0], dst_ref=zone.at[0], send_sem=send_sems.at[k], recv_sem=recv_sems.at[k],
                device_id=(px, py, c), device_id_type=MESH)
            cp.wait_send()
            cp.wait_recv()
        pltpu.make_async_copy(zone.at[0], zone.at[0], local_sem.at[0]).wait()

    return _split_call(body, name, thru, (3, 3, 1), extra=(*sems, after), with_token=False)[1]


def share_start(parts, after, name):
    n = len(parts)
    zones = [lax.empty((N_DEV,) + p.shape, p.dtype) for p in parts]

    def body(*refs):
        ins, zs = refs[:n], refs[n:2 * n]
        send_sems, recv_sems, local_sems = refs[2 * n + 1:2 * n + 4]
        me, peers = _peers()
        for i in range(n):
            pltpu.make_async_copy(ins[i], zs[i].at[me], local_sems.at[i]).start()
            for k, peer in enumerate(peers):
                pltpu.make_async_remote_copy(
                    src_ref=ins[i], dst_ref=zs[i].at[me], send_sem=send_sems.at[7 * i + k],
                    recv_sem=recv_sems.at[7 * i + k], device_id=peer, device_id_type=MESH).start()
        refs[-1][...] = jnp.zeros(refs[-1].shape, F32)

    sems, thru, token = _split_call(body, name, list(parts) + zones, (7 * n, 7 * n, n), extra=(after,))
    return (sems, thru, n), token


def share_wait(started, after, name):
    sems, thru, n = started

    def body(*refs):
        zs = refs[n:2 * n]
        send_sems, recv_sems, local_sems = refs[2 * n:2 * n + 3]
        _, peers = _peers()
        for i in range(n):
            for k, peer in enumerate(peers):
                cp = pltpu.make_async_remote_copy(
                    src_ref=zs[i].at[0], dst_ref=zs[i].at[0], send_sem=send_sems.at[7 * i + k],
                    recv_sem=recv_sems.at[7 * i + k], device_id=peer, device_id_type=MESH)
                cp.wait_send()
                cp.wait_recv()
            pltpu.make_async_copy(zs[i].at[0], zs[i].at[0], local_sems.at[i]).wait()

    return _split_call(body, name, thru, (7 * n, 7 * n, n), extra=(*sems, after), with_token=False)[n:]


def _adamw_math(w, g, m, v):
    m = ADAM_B1 * m + (1.0 - ADAM_B1) * g
    v = ADAM_B2 * v + (1.0 - ADAM_B2) * (g * g)
    m_hat = m / (1.0 - ADAM_B1 ** ADAM_STEP)
    v_hat = v / (1.0 - ADAM_B2 ** ADAM_STEP)
    delta = -ADAM_LR * (m_hat / (jnp.sqrt(v_hat) + ADAM_EPS) + ADAM_WD * w)
    return delta, m, v


ADAMW_BLOCK_BYTES = 24 * 1024 * 1024


def adamw_layer(zone, layer, items, after, name):
    n_src, nw, r, c = zone.shape
    depth = items[0][0].shape[0]
    prevs = [p if p is not None else tuple(lax.empty((depth, r, c), F32) for _ in range(4)) for _, _, _, p in items]
    row_bytes = 2 * nw * c * (2 * n_src + 4 * 7)
    tr = max(t for t in range(8, r + 1, 8) if r % t == 0 and t * row_bytes <= ADAMW_BLOCK_BYTES)

    def body(z_ref, *rest):
        ins, outs = rest[:3 * nw], rest[7 * nw + 1:]
        for i in range(nw):
            g = z_ref[0, i].astype(F32)
            for src in range(1, n_src):
                g = g + z_ref[src, i].astype(F32)
            g_ref, d_ref, mo_ref, vo_ref = outs[4 * i:4 * i + 4]
            w_ref, m_ref, v_ref = ins[3 * i:3 * i + 3]
            g_ref[...] = g
            d_ref[...], mo_ref[...], vo_ref[...] = _adamw_math(w_ref[...], g, m_ref[...], v_ref[...])

    rows = pl.BlockSpec((None, tr, c), lambda i: (layer, i, 0))
    anywhere = pl.BlockSpec(memory_space=pl.ANY)
    outs = pl.pallas_call(
        body, name=name, grid=(r // tr,),
        in_specs=[pl.BlockSpec((n_src, nw, tr, c), lambda i: (0, 0, i, 0))] + [rows] * (3 * nw)
                 + [anywhere] * (4 * nw + 1),
        out_specs=[rows] * (4 * nw),
        out_shape=[jax.ShapeDtypeStruct((depth, r, c), F32)] * (4 * nw),
        input_output_aliases={1 + 3 * nw + k: k for k in range(4 * nw)},
        compiler_params=_params(),
    )(zone, *[t for w, m, v, _ in items for t in (w, m, v)], *[t for p in prevs for t in p], after)
    return [tuple(outs[4 * i:4 * i + 4]) for i in range(nw)]


def adamw_small(ws, recvs, ms, vs, name):
    n = len(ws)

    def body(*refs):
        w_refs, r_refs, m_refs, v_refs = (refs[i * n:(i + 1) * n] for i in range(4))
        g_refs, d_refs, mo_refs, vo_refs = (refs[(4 + i) * n:(5 + i) * n] for i in range(4))
        for i in range(n):
            g = r_refs[i][0]
            for src in range(1, N_DEV):
                g = g + r_refs[i][src]
            g_refs[i][...] = g
            d_refs[i][...], mo_refs[i][...], vo_refs[i][...] = _adamw_math(w_refs[i][...], g, m_refs[i][...],
                                                                            v_refs[i][...])

    vm = pl.BlockSpec(memory_space=pltpu.VMEM)
    outs = pl.pallas_call(
        body, name=name, in_specs=[vm] * (4 * n), out_specs=[vm] * (4 * n),
        out_shape=[jax.ShapeDtypeStruct(w.shape, F32) for w in ws] * 4,
        compiler_params=pltpu.CompilerParams(vmem_limit_bytes=V7X_VMEM_LIMIT),
    )(*ws, *recvs, *ms, *vs)
    return [outs[i * n:(i + 1) * n] for i in range(4)]


SMALL_NAMES = ("ffn1_norm", "mix_norm", "ffn2_norm", "b_gate", "na_q_norm", "na_k_norm", "sw_q_norm", "sw_k_norm",
               "na_rpb", "sw_sink", "t5_rel_table")


def kernel(x, ffn1_norm, ffn1_w_gate, ffn1_w_up, ffn1_w_down, mix_norm, w_in, b_gate, na_q_norm, na_k_norm, na_rpb, sw_q_norm, sw_k_norm, sw_sink, t5_rel_table, w_branch_na, w_branch_sw, w_out, ffn2_norm, ffn2_w_gate, ffn2_w_up, ffn2_w_down, loss_target, m_ffn1_norm, m_ffn1_w_gate, m_ffn1_w_up, m_ffn1_w_down, m_mix_norm, m_w_in, m_b_gate, m_na_q_norm, m_na_k_norm, m_na_rpb, m_sw_q_norm, m_sw_k_norm, m_sw_sink, m_t5_rel_table, m_w_branch_na, m_w_branch_sw, m_w_out, m_ffn2_norm, m_ffn2_w_gate, m_ffn2_w_up, m_ffn2_w_down, v_ffn1_norm, v_ffn1_w_gate, v_ffn1_w_up, v_ffn1_w_down, v_mix_norm, v_w_in, v_b_gate, v_na_q_norm, v_na_k_norm, v_na_rpb, v_sw_q_norm, v_sw_k_norm, v_sw_sink, v_t5_rel_table, v_w_branch_na, v_w_branch_sw, v_w_out, v_ffn2_norm, v_ffn2_w_gate, v_ffn2_w_up, v_ffn2_w_down):
    weights = dict(ffn1_norm=ffn1_norm, ffn1_w_gate=ffn1_w_gate, ffn1_w_up=ffn1_w_up, ffn1_w_down=ffn1_w_down,
                   mix_norm=mix_norm, w_in=w_in, b_gate=b_gate, na_q_norm=na_q_norm, na_k_norm=na_k_norm,
                   na_rpb=na_rpb, sw_q_norm=sw_q_norm, sw_k_norm=sw_k_norm, sw_sink=sw_sink,
                   t5_rel_table=t5_rel_table, w_branch_na=w_branch_na, w_branch_sw=w_branch_sw, w_out=w_out,
                   ffn2_norm=ffn2_norm, ffn2_w_gate=ffn2_w_gate, ffn2_w_up=ffn2_w_up, ffn2_w_down=ffn2_w_down)
    mom_m = dict(ffn1_norm=m_ffn1_norm, ffn1_w_gate=m_ffn1_w_gate, ffn1_w_up=m_ffn1_w_up, ffn1_w_down=m_ffn1_w_down,
                 mix_norm=m_mix_norm, w_in=m_w_in, b_gate=m_b_gate, na_q_norm=m_na_q_norm, na_k_norm=m_na_k_norm,
                 na_rpb=m_na_rpb, sw_q_norm=m_sw_q_norm, sw_k_norm=m_sw_k_norm, sw_sink=m_sw_sink,
                 t5_rel_table=m_t5_rel_table, w_branch_na=m_w_branch_na, w_branch_sw=m_w_branch_sw, w_out=m_w_out,
                 ffn2_norm=m_ffn2_norm, ffn2_w_gate=m_ffn2_w_gate, ffn2_w_up=m_ffn2_w_up, ffn2_w_down=m_ffn2_w_down)
    mom_v = dict(ffn1_norm=v_ffn1_norm, ffn1_w_gate=v_ffn1_w_gate, ffn1_w_up=v_ffn1_w_up, ffn1_w_down=v_ffn1_w_down,
                 mix_norm=v_mix_norm, w_in=v_w_in, b_gate=v_b_gate, na_q_norm=v_na_q_norm, na_k_norm=v_na_k_norm,
                 na_rpb=v_na_rpb, sw_q_norm=v_sw_q_norm, sw_k_norm=v_sw_k_norm, sw_sink=v_sw_sink,
                 t5_rel_table=v_t5_rel_table, w_branch_na=v_w_branch_na, w_branch_sw=v_w_branch_sw, w_out=v_w_out,
                 ffn2_norm=v_ffn2_norm, ffn2_w_gate=v_ffn2_w_gate, ffn2_w_up=v_ffn2_w_up, ffn2_w_down=v_ffn2_w_down)
    order = list(weights)

    depth = ffn1_norm.shape[0]
    s, d = x.shape[1], x.shape[2]
    xs = x[0]
    tr = lambda w: jnp.swapaxes(w, -1, -2)

    merge = lambda t: t.reshape(t.shape[0], N_DEV * t.shape[2], t.shape[3])
    no_dep = jnp.zeros((8, LANES), F32)

    def shards_of(kind, l):
        stack = lambda *ws: jnp.stack(ws).astype(BF16)
        if kind == "ffn1":
            return [stack(tr(ffn1_w_gate[l]), tr(ffn1_w_up[l]), ffn1_w_down[l])]
        if kind == "win":
            return [stack(tr(w_in[l]))]
        return [stack(tr(ffn2_w_gate[l]), tr(ffn2_w_up[l]), ffn2_w_down[l]), stack(w_out[l]),
                stack(tr(w_branch_na[l]), tr(w_branch_sw[l]))]

    shards = {(kind, l): shards_of(kind, l) for l in range(depth) for kind in ("ffn1", "win", "rest")}

    def start(kind, l, after):
        return gather_start(shards[kind, l], after, f"gather_{kind}_{l}")

    def arrive(started, kind, l, after):
        zones = gather_wait(started, after, f"gather_{kind}_{l}_wait")
        return forward_start(zones, no_dep, f"forward_{kind}_{l}")

    def finish(fwd, kind, l, after):
        return [merge(z) for z in forward_wait(fwd, after, f"forward_{kind}_{l}_wait")]

    bd = jnp.asarray(np.kron(np.eye(MXU_TILE // HEAD_DIM), np.full((HEAD_DIM, HEAD_DIM), 1.0 / HEAD_DIM)), BF16)
    bmap = jnp.asarray(_t5_bucket_map())
    tile8 = lambda g: jnp.tile(g, NA_WIDTH // HEAD_DIM).reshape(1, NA_WIDTH)
    tile2 = lambda g: jnp.tile(g, SW_KV_WIDTH // HEAD_DIM).reshape(1, SW_KV_WIDTH)

    st_first, tok = start("ffn1", 0, no_dep)
    t5b = t5_expand(t5_rel_table, bmap, tok, "t5_expand").reshape(SW_STACK, 3 * SW_BLOCK)
    t2_tables = [rpb_expand(_rpb_rows(na_rpb[l]), tok, f"rpb_expand_{l}") for l in range(depth)]
    qk_gains = [(tile8(na_q_norm[l]), tile8(na_k_norm[l]), tile8(sw_q_norm[l]), tile2(sw_k_norm[l]))
                for l in range(depth)]
    early = ([t[0, 0, 0:8, :] for t in t2_tables] + [t[0, 0:8, 0:LANES].astype(F32) for v in shards.values() for t in v]
             + [g[:, 0:LANES] for gs in qk_gains for g in gs])
    fwd, _ = arrive(st_first, "ffn1", 0, functools.reduce(jnp.add, early, t5b[0:8, 0:LANES]))
    st_win, dep = start("win", 0, t5b)
    (first,) = finish(fwd, "ffn1", 0, dep)

    saved = []
    layer_w = {0: dict(wg1=(first, 0), wu1=(first, 1), wd1=(first, 2))}
    cur = xs
    for l in range(depth):
        sv = {}
        lw = layer_w[l]
        sv["x0"] = cur
        cur, sv["xn1"], sv["hg1"], sv["hu1"], sv["act1"] = ffn_forward(
            cur, ffn1_norm[l][None], lw["wg1"], lw["wu1"], lw["wd1"], dep, f"ffn1_{l}")
        sv["x1"] = cur
        fwd, _ = arrive(st_win, "win", l, cur)
        st_rest, tok = start("rest", l, cur)
        (zb,) = finish(fwd, "win", l, tok)
        lw["win"] = (zb, 0)
        sv["gains"] = qk_gains[l]
        sv["hn"], sv["zq"], sv["qa"], sv["ka"], sv["qs"], sv["ks"], sv["gt"] = mix_in(
            cur, mix_norm[l][None], lw["win"], b_gate[l][None], *sv["gains"], bd, f"mix_in_{l}")
        sv["t2"] = t2_tables[l]
        sv["o_na"] = na_fwd(sv["qa"], sv["ka"], sv["zq"], sv["t2"], f"na_fwd_{l}")
        dep = no_dep
        if l + 1 < depth:
            st_ffn1, dep = start("ffn1", l + 1, sv["o_na"])
        sv["o_sw"] = sw_fwd(sv["qs"], sv["ks"], sv["zq"], t5b, sw_sink[l], dep, f"sw_fwd_{l}")
        fwd, tok = arrive(st_rest, "rest", l, sv["o_sw"][0:8, 0:LANES] + sv["o_na"][0:8, 0:LANES])
        za, zc, zd = finish(fwd, "rest", l, tok)
        lw.update(wg2=(za, 0), wu2=(za, 1), wd2=(za, 2), wout=(zc, 0), wna=(zd, 0), wsw=(zd, 1))
        cur, sv["a_na"], sv["a_sw"], sv["merged"] = merge_out(
            cur, sv["o_na"], sv["o_sw"], sv["gt"], lw["wna"], lw["wsw"], lw["wout"], f"merge_out_{l}")
        sv["x2"] = cur
        if l + 1 < depth:
            st_win, dep = start("win", l + 1, cur)
            sv["xn2"], sv["hg2"], sv["hu2"], sv["act2"] = ffn_forward(
                cur, ffn2_norm[l][None], lw["wg2"], lw["wu2"], None, dep, f"ffn2_up_{l}")
            fwd, dep = arrive(st_ffn1, "ffn1", l + 1, sv["act2"])
            cur = ffn_down(cur, sv["act2"], lw["wd2"], dep, f"ffn2_down_{l}")
            (za,) = finish(fwd, "ffn1", l + 1, cur)
            layer_w[l + 1] = dict(wg1=(za, 0), wu1=(za, 1), wd1=(za, 2))
        else:
            dx, loss_acc, sv["xn2"], sv["hg2"], sv["hu2"], sv["act2"] = ffn_forward(
                cur, ffn2_norm[l][None], lw["wg2"], lw["wu2"], lw["wd2"], no_dep, f"ffn2_{l}",
                target=loss_target[0])
        dep = no_dep
        saved.append(sv)

    split = lambda t: t.reshape(N_DEV, t.shape[0] // N_DEV, t.shape[1])
    pending = {}
    last_key = "ffn1_0"
    two_level = {last_key}
    small = {k: [None] * depth for k in SMALL_NAMES if k != "t5_rel_table"}
    dbias_sw = []
    for l in reversed(range(depth)):
        sv = saved[l]
        lw = layer_w[l]
        wg1, wu1, wd1, wg2, wu2, wd2 = (lw[k] for k in ("wg1", "wu1", "wd1", "wg2", "wu2", "wd2"))
        win_t, wout_l, wna_t, wsw_t = lw["win"], lw["wout"], lw["wna"], lw["wsw"]
        blocks = ((2, "x2", "xn2", "hg2", "hu2", "act2", wg2, wu2, wd2, "ffn2_norm", 3),
                  (1, "x0", "xn1", "hg1", "hu1", "act1", wg1, wu1, wd1, "ffn1_norm", 0))

        def ffn_backward(dx, blk):
            tag, xk, xnk, hgk, huk, actk, wg, wu, wd, norm_name, slot = blk
            gains = weights[norm_name]
            dxb, dhg, dhu = ffn_bwd_act(dx, wd, sv[hgk], sv[huk], f"ffn{tag}_bwd_act_{l}")
            gwg, gwu, gwd = tn_matmul([(dhg, sv[xnk], 1.0), (dhu, sv[xnk], 1.0), (sv[actk], dxb, 0.5)],
                                      f"ffn{tag}_dw_{l}")
            key = f"ffn{tag}_{l}"
            blocks_of = [split(gwg), split(gwu), split(gwd)]
            if key in two_level:
                paired, token = pair_start(blocks_of, dxb, f"pair_{key}")
            else:
                pending[key], token = scatter_start([blocks_of], f"scatter_{key}")
            dx, dg = proj_bwd_norm([dhg, dhu], [wg, wu], sv[xk], gains[l][None], dx, token, f"ffn{tag}_bwd_x_{l}")
            token = no_dep
            if key in two_level:
                thru, land = pair_wait(paired, dx, f"pair_{key}_wait")
                pending[key], token = chip_start(pair_sum(thru, land, f"pair_sum_{key}"), dg, f"chips_{key}")
            small[norm_name][l] = dg[0]
            return dx, token

        dx, token = ffn_backward(dx, blocks[0])
        dxb, dzg, da_na, da_sw, do_na, do_sw, dbg = mix_bwd_out(
            dx, sv["gt"], sv["a_na"], sv["a_sw"], wna_t, wsw_t, wout_l, token, f"mix_bwd_out_{l}")
        small["b_gate"][l] = dbg[0]
        gwout, gwna, gwsw = tn_matmul([(sv["merged"], dxb, 1.0), (da_na, sv["o_na"], 1.0), (da_sw, sv["o_sw"], 1.0)],
                                      f"mix_dw_{l}")
        dqa, dka, dva, dt2 = na_bwd(sv["qa"], sv["ka"], sv["zq"], sv["t2"], sv["o_na"], do_na, f"na_bwd_{l}")
        dqs, dks, dvs, dbias, dsink = sw_bwd(sv["qs"], sv["ks"], sv["zq"], t5b, sw_sink[l], sv["o_sw"], do_sw,
                                             f"sw_bwd_{l}")
        dbias_sw.append(dbias.reshape(SW_HEADS, SW_BLOCK, 3 * SW_BLOCK))
        small["sw_sink"][l] = jnp.sum(dsink[:, 0].reshape(SW_HEADS, SW_BLOCK), axis=1)
        small["na_rpb"][l] = _rpb_from_rows(rpb_reduce(dt2, f"rpb_reduce_{l}"))
        dz, dgqa, dgka, dgqs, dgks = qk_norm_bwd(dqa, dka, dva, dqs, dks, dvs, sv["zq"], dzg, *sv["gains"], bd,
                                                 f"qk_norm_bwd_{l}")
        fold = lambda g: jnp.sum(g.reshape(-1, HEAD_DIM), axis=0)
        small["na_q_norm"][l], small["na_k_norm"][l] = fold(dgqa), fold(dgka)
        small["sw_q_norm"][l], small["sw_k_norm"][l] = fold(dgqs), fold(dgks)
        (gwin,) = tn_matmul([(dz, sv["hn"], 1.0)], f"dwin_{l}")
        pending[f"mix_{l}"], token = scatter_start([[split(gwout)], [split(gwna), split(gwsw)], [split(gwin)]],
                                                   f"scatter_mix_{l}")
        dx, dg = proj_bwd_norm([dz], [win_t], sv["x1"], mix_norm[l][None], dx, token, f"mix_bwd_x_{l}")
        small["mix_norm"][l] = dg[0]
        dx, tail = ffn_backward(dx, blocks[1])

    dtab = t5_reduce(dbias_sw, bmap, "t5_reduce")
    small_parts = {k: jnp.stack(v) for k, v in small.items()}
    small_parts["t5_rel_table"] = jnp.transpose(dtab[:, :, 0])

    grads, delta, new_m, new_v = {}, {}, {}, {}
    state = {}
    sharing, token = share_start([small_parts[k] for k in SMALL_NAMES] + [loss_acc], tail, "share_small")
    chain = [token]
    members = {"ffn": lambda t: [(f"ffn{t}_w_gate", 0, 0, True), (f"ffn{t}_w_up", 0, 1, True),
                                 (f"ffn{t}_w_down", 0, 2, False)],
               "mix": lambda t: [("w_out", 0, 0, False), ("w_branch_na", 1, 0, True), ("w_branch_sw", 1, 1, True),
                                 ("w_in", 2, 0, True)]}

    def collect(key):
        if key in two_level:
            zones = [chip_wait(pending[key], chain[0], f"wait_{key}")]
        else:
            zones = scatter_wait(pending[key], chain[0], f"wait_{key}")
        kind, l = key.split("_")
        group = members[kind[:3]](kind[3:])
        complete = all(f"{kind}_{j}" in done for j in range(depth) if j != int(l))
        for zi, zone in enumerate(zones):
            mine = sorted((wi, k, transposed) for k, z, wi, transposed in group if z == zi)
            views = [tr if transposed else (lambda t: t) for _, _, transposed in mine]
            items = [(view(weights[k]), view(mom_m[k]), view(mom_v[k]), state.get(k))
                     for (_, k, _), view in zip(mine, views)]
            results = adamw_layer(zone, int(l), items, chain[0], f"adamw_{key}_{zi}")
            chain[0] = results[-1][1]
            for (_, k, _), view, res in zip(mine, views, results):
                state[k] = res
                if complete:
                    grads[k], delta[k], new_m[k], new_v[k] = (view(t) for t in res)
        done.add(key)

    done = set()
    for key in pending:
        if key != last_key:
            collect(key)
    collect(last_key)
    *recvs, all_losses = share_wait(sharing, chain[0], "share_small_wait")
    loss = jnp.sum(all_losses) * (0.5 / d)
    results = adamw_small([weights[k] for k in SMALL_NAMES], recvs, [mom_m[k] for k in SMALL_NAMES],
                          [mom_v[k] for k in SMALL_NAMES], "adamw_small")
    for dst, outs in zip((grads, delta, new_m, new_v), results):
        dst.update(dict(zip(SMALL_NAMES, outs)))

    return (loss, dx[None], *[grads[k] for k in order], *[delta[k] for k in order],
            *[new_m[k] for k in order], *[new_v[k] for k in order])
```

```python
import functools
import math

import numpy as np
import jax
import jax.numpy as jnp
from jax import lax
from jax.experimental import pallas as pl
from jax.experimental.pallas import tpu as pltpu

F32 = jnp.float32
BF16 = jnp.bfloat16
MESH = pl.DeviceIdType.MESH

N_DEV = 8
EPS = 1e-6
NEG = -1e30
HEAD_DIM = 64
GRID_W = 64
NA_ROWS = 8
NA_COLS = 16
NA_WIDTH = 512
SW_Q_WIDTH = 512
SW_KV_WIDTH = 128
SW_BLOCK = 128
SW_HEADS = 8
SW_REP = 4
REL_BUCKETS = 32
REL_MAX_DIST = 128
QKV_WIDTH = 3 * NA_WIDTH + SW_Q_WIDTH + 2 * SW_KV_WIDTH
SCALE = 1.0 / math.sqrt(HEAD_DIM)

ADAM_LR = 0.001
ADAM_B1 = 0.9
ADAM_B2 = 0.999
ADAM_EPS = 1e-08
ADAM_WD = 0.01
ADAM_STEP = 10

V7X_VMEM_LIMIT = 56 * 1024 * 1024
LANES = 128
MXU_TILE = 256

NT = (((1,), (1,)), ((), ()))
TN = (((0,), (0,)), ((), ()))


def _params(n_grid=1):
    return pltpu.CompilerParams(dimension_semantics=("arbitrary",) * n_grid,
                                vmem_limit_bytes=V7X_VMEM_LIMIT)


def _row_tile(s):
    for t in (512, 256, 128, 64, 32, 16, 8):
        if s % t == 0:
            return t
    raise ValueError(s)


def _tn_tile(n):
    best = max(t for t in range(LANES, min(n, 2304) + 1, LANES) if n % t == 0) if n % LANES == 0 else n
    return best // 2 if best == n and n >= 1024 else best


ONCE = pl.Buffered(1)


def _col_chunk(n):
    return MXU_TILE if n % MXU_TILE == 0 else n


def _dot(a, b):
    return jnp.dot(a, b, preferred_element_type=F32)


def _dotg(a, b, dn):
    return lax.dot_general(a, b, dn, preferred_element_type=F32)


def _sigmoid(v):
    return 1.0 / (1.0 + jnp.exp(-v))


def _rstd(xv):
    return lax.rsqrt(jnp.mean(xv * xv, axis=-1, keepdims=True) + EPS)


def _full(shape):
    nd = len(shape)
    return pl.BlockSpec(shape, lambda i, _n=nd: (0,) * _n)


def _rows(tm, width):
    return pl.BlockSpec((tm, width), lambda i: (i, 0))


def _mat(stack, idx):
    return pl.BlockSpec((None,) + tuple(stack.shape[1:]), lambda i, _w=idx: (_w, 0, 0), pipeline_mode=ONCE)


def _group_mean(v, bd):
    w = bd.shape[0]
    if v.shape[1] > w:
        return jnp.concatenate([_group_mean(v[:, c0:c0 + w], bd) for c0 in range(0, v.shape[1], w)], axis=1)
    hi = v.astype(BF16)
    lo = (v - hi.astype(F32)).astype(BF16)
    return _dot(hi, bd) + _dot(lo, bd)


def _loss_tile(y, t_ref, dy_ref, acc_ref):
    tm, d = y.shape

    @pl.when(pl.program_id(0) == 0)
    def _():
        acc_ref[...] = jnp.zeros(acc_ref.shape, F32)

    err = y - t_ref[...]
    dy_ref[...] = err * (1.0 / d)
    part = jnp.sum((err * err).reshape(tm // 8, 8, d), axis=0)
    acc = part[:, 0:LANES]
    for c0 in range(LANES, d, LANES):
        acc = acc + part[:, c0:c0 + LANES]
    acc_ref[...] = acc_ref[...] + acc


def ffn_forward(x, gain, wg_t, wu_t, wd, dep, name, target=None):
    s, d = x.shape
    f = wg_t[0].shape[1]
    tm = _row_tile(s) if wd is None else min(_row_tile(s), 256)
    fc = _col_chunk(f)
    nw = 2 if wd is None else 3
    n_in = nw + (1 if target is None else 2)

    def body(x_ref, g_ref, *refs):
        w_refs, outs = refs[:nw], refs[n_in:]
        xn_ref, dg_ref, du_ref, act_ref = outs[-4:]
        xv = x_ref[...]
        xn = (xv * _rstd(xv) * g_ref[...]).astype(BF16)
        xn_ref[...] = xn
        for c0 in range(0, f, fc):
            hg = _dotg(xn, w_refs[0][c0:c0 + fc, :], NT)
            hu = _dotg(xn, w_refs[1][c0:c0 + fc, :], NT)
            sg = _sigmoid(hg)
            silu = hg * sg
            du_ref[:, c0:c0 + fc] = silu.astype(BF16)
            dg_ref[:, c0:c0 + fc] = (hu * (sg + silu * (1.0 - sg))).astype(BF16)
            act_ref[:, c0:c0 + fc] = (silu * hu).astype(BF16)
        if wd is not None:
            y = xv + 0.5 * _dot(act_ref[...], w_refs[2][...])
            if target is None:
                outs[0][...] = y
            else:
                _loss_tile(y, refs[nw + 1], outs[0], outs[1])

    weights = [wg_t, wu_t] + ([] if wd is None else [wd])
    in_specs = [_rows(tm, d), _full((1, d))] + [_mat(*w) for w in weights] + [_full(dep.shape)]
    operands = [x, gain, *[w[0] for w in weights], dep]
    out_specs = [_rows(tm, d), _rows(tm, f), _rows(tm, f), _rows(tm, f)]
    out_shape = [jax.ShapeDtypeStruct((s, d), BF16)] + [jax.ShapeDtypeStruct((s, f), BF16)] * 3
    if wd is not None:
        out_specs, out_shape = [_rows(tm, d)] + out_specs, [jax.ShapeDtypeStruct((s, d), F32)] + out_shape
    if target is not None:
        in_specs, operands = in_specs + [_rows(tm, d)], operands + [target]
        out_specs = out_specs[:1] + [_full((8, LANES))] + out_specs[1:]
        out_shape = out_shape[:1] + [jax.ShapeDtypeStruct((8, LANES), F32)] + out_shape[1:]
    return pl.pallas_call(
        body, name=name, grid=(s // tm,), in_specs=in_specs, out_specs=out_specs, out_shape=out_shape,
        compiler_params=_params(),
    )(*operands)


def ffn_down(x, act, wd, dep, name):
    s, d = x.shape
    f = act.shape[1]
    tm = _row_tile(s)

    def body(x_ref, a_ref, w_ref, dep_ref, o_ref):
        o_ref[...] = x_ref[...] + 0.5 * _dot(a_ref[...], w_ref[...])

    return pl.pallas_call(
        body, name=name, grid=(s // tm,),
        in_specs=[_rows(tm, d), _rows(tm, f), _mat(*wd), _full(dep.shape)],
        out_specs=_rows(tm, d),
        out_shape=jax.ShapeDtypeStruct((s, d), F32),
        compiler_params=_params(),
    )(x, act, wd[0], dep)


def mix_in(x, gain, win_t, b_gate, gq_na, gk_na, gq_sw, gk_sw, bd, name):
    s, d = x.shape
    tm = _row_tile(s)
    gc = _col_chunk(2 * d)

    def body(x_ref, g_ref, w_ref, b_ref, gqa_ref, gka_ref, gqs_ref, gks_ref, bd_ref,
             hn_ref, zq_ref, qa_ref, ka_ref, qs_ref, ks_ref, gt_ref):
        xv = x_ref[...]
        hn = (xv * _rstd(xv) * g_ref[...]).astype(BF16)
        hn_ref[...] = hn

        def proj(c0, c1):
            return _dotg(hn, w_ref[c0:c1, :], NT)

        def headnorm(z, g, bdm):
            return z * lax.rsqrt(_group_mean(z * z, bdm) + EPS) * g

        bd512 = bd_ref[...]
        bd128 = bd_ref[0:SW_KV_WIDTH, 0:SW_KV_WIDTH]
        z = proj(0, 512)
        zq_ref[:, 0:512] = z.astype(BF16)
        qa_ref[...] = (headnorm(z, gqa_ref[...], bd512) * SCALE).astype(BF16)
        z = proj(512, 1024)
        zq_ref[:, 512:1024] = z.astype(BF16)
        ka_ref[...] = headnorm(z, gka_ref[...], bd512).astype(BF16)
        z = proj(1024, 1536)
        zq_ref[:, 1024:1536] = z.astype(BF16)
        z = proj(1536, 2048)
        zq_ref[:, 1536:2048] = z.astype(BF16)
        qs_ref[...] = (headnorm(z, gqs_ref[...], bd512) * SCALE).astype(BF16)
        z = proj(2048, 2176)
        zq_ref[:, 2048:2176] = z.astype(BF16)
        ks_ref[...] = headnorm(z, gks_ref[...], bd128).astype(BF16)
        z = proj(2176, 2304)
        zq_ref[:, 2176:2304] = z.astype(BF16)
        for c0 in range(0, 2 * d, gc):
            zg = proj(QKV_WIDTH + c0, QKV_WIDTH + c0 + gc) + b_ref[:, c0:c0 + gc]
            gt_ref[:, c0:c0 + gc] = _sigmoid(zg).astype(BF16)

    return pl.pallas_call(
        body, name=name, grid=(s // tm,),
        in_specs=[_rows(tm, d), _full((1, d)), _mat(*win_t), _full((1, 2 * d)),
                  _full((1, 512)), _full((1, 512)), _full((1, 512)), _full((1, 128)), _full((MXU_TILE, MXU_TILE))],
        out_specs=[_rows(tm, d), _rows(tm, QKV_WIDTH), _rows(tm, 512), _rows(tm, 512), _rows(tm, 512),
                   _rows(tm, 128), _rows(tm, 2 * d)],
        out_shape=[jax.ShapeDtypeStruct((s, d), BF16), jax.ShapeDtypeStruct((s, QKV_WIDTH), BF16),
                   jax.ShapeDtypeStruct((s, 512), BF16), jax.ShapeDtypeStruct((s, 512), BF16),
                   jax.ShapeDtypeStruct((s, 512), BF16), jax.ShapeDtypeStruct((s, 128), BF16),
                   jax.ShapeDtypeStruct((s, 2 * d), BF16)],
        compiler_params=_params(),
    )(x, gain, win_t[0], b_gate, gq_na, gk_na, gq_sw, gk_sw, bd)


def _na_iotas():
    qc = lax.broadcasted_iota(jnp.int32, (GRID_W, LANES), 0)
    ln = lax.broadcasted_iota(jnp.int32, (GRID_W, LANES), 1)
    low = ln < GRID_W
    kc = jnp.where(low, ln, ln - GRID_W)
    diff = kc - qc + (NA_COLS - 1)
    qcs = jnp.clip(qc - NA_COLS // 2, 0, GRID_W - NA_COLS)
    inwin = (kc >= qcs) & (kc < qcs + NA_COLS)
    return diff, low, inwin


NA_RI = 2 * NA_ROWS - 1
NA_CI = 2 * NA_COLS - 1
NA_T2 = NA_RI + 1


def _rpb_rows(rpb):
    h = rpb.shape[0]
    padded = jnp.pad(rpb, ((0, 0), (1, 1), (0, GRID_W - NA_CI)))
    return jnp.concatenate([padded[:, :NA_T2], padded[:, 1:NA_T2 + 1]], axis=2).reshape(h, NA_T2, LANES)


def _rpb_from_rows(rows):
    return rows[:, 1:, :NA_CI] + rows[:, :NA_RI, GRID_W:GRID_W + NA_CI]


def rpb_expand(rows, dep, name):
    n_heads = rows.shape[0]

    def body(r_ref, dep_ref, o_ref):
        for h in range(n_heads):
            for e in range(NA_T2):
                line = jnp.broadcast_to(r_ref[h, e:e + 1, :], (GRID_W, LANES))
                o_ref[h, e] = pltpu.roll(line, LANES - (NA_COLS - 1), 1, stride=1, stride_axis=0)

    return pl.pallas_call(
        body, name=name,
        in_specs=[pl.BlockSpec(memory_space=pltpu.VMEM), pl.BlockSpec(memory_space=pltpu.VMEM)],
        out_specs=pl.BlockSpec(memory_space=pltpu.VMEM),
        out_shape=jax.ShapeDtypeStruct((n_heads, NA_T2, GRID_W, LANES), F32),
        compiler_params=pltpu.CompilerParams(vmem_limit_bytes=V7X_VMEM_LIMIT),
    )(rows, dep)


def rpb_reduce(dt2, name):
    n_heads = dt2.shape[0]
    flip = jnp.asarray(np.eye(GRID_W)[::-1], BF16)

    def body(d_ref, j_ref, o_ref):
        jm = j_ref[...]
        for h in range(n_heads):
            for e in range(NA_T2):
                dv = d_ref[h, e]
                hi = dv.astype(BF16)
                mid = (dv - hi.astype(F32)).astype(BF16)
                lo = (dv - hi.astype(F32) - mid.astype(F32)).astype(BF16)
                rev = _dot(jm, hi) + _dot(jm, mid) + _dot(jm, lo)
                back = pltpu.roll(rev, LANES + (NA_COLS - 1) - (GRID_W - 1), 1, stride=1, stride_axis=0)
                o_ref[h, e:e + 1, :] = jnp.sum(back, axis=0, keepdims=True)

    return pl.pallas_call(
        body, name=name,
        in_specs=[pl.BlockSpec(memory_space=pltpu.VMEM)] * 2,
        out_specs=pl.BlockSpec(memory_space=pltpu.VMEM),
        out_shape=jax.ShapeDtypeStruct((n_heads, NA_T2, LANES), F32),
        compiler_params=pltpu.CompilerParams(vmem_limit_bytes=V7X_VMEM_LIMIT),
    )(dt2, flip)


NA_TQ = 4
NA_TK = NA_TQ + NA_ROWS
NA_KCH = NA_TK // 2


def _na_tile_geometry(t, rows):
    r = t * NA_TQ
    kbase = jnp.clip(r - NA_ROWS // 2, 0, rows - NA_TK)
    starts = [jnp.clip(r + a - NA_ROWS // 2, 0, rows - NA_ROWS) for a in range(NA_TQ)]
    return r, kbase, starts


def _na_tile_mask(kbase, starts, low, inwin):
    half = jnp.where(low, 0, 1)
    cols = []
    for c in range(NA_KCH):
        krow = kbase + 2 * c + half
        cols.append(jnp.concatenate(
            [jnp.where(inwin & (krow >= st) & (krow < st + NA_ROWS), 0.0, NEG) for st in starts], axis=0))
    return jnp.concatenate(cols, axis=1)


def _na_tile_index(r, kbase, a, c):
    return jnp.clip(kbase + 2 * c - (r + a) + NA_ROWS, 0, NA_T2 - 1)


def _na_tile_scores(q, k, t2_ref, hh, r, kbase, madd):
    bias = jnp.concatenate(
        [jnp.concatenate([t2_ref[hh, _na_tile_index(r, kbase, a, c)] for a in range(NA_TQ)], axis=0)
         for c in range(NA_KCH)], axis=1)
    return _dotg(q, k, NT) + bias + madd


def _softmax_rows(sc):
    e = jnp.exp(sc - jnp.max(sc, axis=1, keepdims=True))
    return e * (1.0 / jnp.sum(e, axis=1, keepdims=True))


def na_fwd(qa, ka, zq, t2, name):
    s = qa.shape[0]
    rows = s // GRID_W
    n_pairs = NA_WIDTH // LANES
    v_blk0 = (2 * NA_WIDTH) // LANES

    assert rows % NA_TQ == 0 and rows >= NA_TK
    tq, tk = NA_TQ * GRID_W, NA_TK * GRID_W

    def body(q_ref, k_ref, v_ref, t2_ref, o_ref, s_scr, p_scr):
        _, low, inwin = _na_iotas()

        def tile(t, carry):
            r, kbase, starts = _na_tile_geometry(t, rows)
            madd = _na_tile_mask(kbase, starts, low, inwin)
            qr = pl.ds(pl.multiple_of(r * GRID_W, tq), tq)
            kr = pl.ds(pl.multiple_of(kbase * GRID_W, tq), tk)
            for hh in range(2):
                lanes = slice(HEAD_DIM * hh, HEAD_DIM * (hh + 1))
                s_scr[tq * hh:tq * (hh + 1), :] = _na_tile_scores(q_ref[qr, lanes], k_ref[kr, lanes], t2_ref, hh, r,
                                                                  kbase, madd)
            p_scr[...] = _softmax_rows(s_scr[...]).astype(BF16)
            for hh in range(2):
                lanes = slice(HEAD_DIM * hh, HEAD_DIM * (hh + 1))
                o_ref[qr, lanes] = _dot(p_scr[tq * hh:tq * (hh + 1), :], v_ref[kr, lanes]).astype(BF16)
            return carry

        lax.fori_loop(0, rows // NA_TQ, tile, 0, unroll=2)

    col = lambda off: pl.BlockSpec((s, LANES), lambda p, _o=off: (0, _o + p))
    return pl.pallas_call(
        body, name=name, grid=(n_pairs,),
        in_specs=[col(0), col(0), col(v_blk0),
                  pl.BlockSpec((2, NA_T2, GRID_W, LANES), lambda p: (p, 0, 0, 0))],
        out_specs=col(0),
        out_shape=jax.ShapeDtypeStruct((s, NA_WIDTH), BF16),
        scratch_shapes=[pltpu.VMEM((2 * tq, tk), F32), pltpu.VMEM((2 * tq, tk), BF16)],
        compiler_params=_params(),
    )(qa, ka, zq, t2)


def na_bwd(qa, ka, zq, t2, o_na, do_na, name):
    s = qa.shape[0]
    rows = s // GRID_W
    n_pairs = NA_WIDTH // LANES
    v_blk0 = (2 * NA_WIDTH) // LANES

    tq, tk = NA_TQ * GRID_W, NA_TK * GRID_W

    def body(q_ref, k_ref, v_ref, t2_ref, o_ref, do_ref, dq_ref, dk_ref, dv_ref, dt2_ref):
        _, low, inwin = _na_iotas()
        dk_ref[...] = jnp.zeros(dk_ref.shape, F32)
        dv_ref[...] = jnp.zeros(dv_ref.shape, F32)
        dt2_ref[...] = jnp.zeros(dt2_ref.shape, F32)

        def tile(t, carry):
            r, kbase, starts = _na_tile_geometry(t, rows)
            madd = _na_tile_mask(kbase, starts, low, inwin)
            qr = pl.ds(pl.multiple_of(r * GRID_W, tq), tq)
            kr = pl.ds(pl.multiple_of(kbase * GRID_W, tq), tk)
            for hh in range(2):
                lanes = slice(HEAD_DIM * hh, HEAD_DIM * (hh + 1))
                q, k, v = q_ref[qr, lanes], k_ref[kr, lanes], v_ref[kr, lanes]
                p = _softmax_rows(_na_tile_scores(q, k, t2_ref, hh, r, kbase, madd))
                do = do_ref[qr, lanes]
                delta = jnp.sum(do.astype(F32) * o_ref[qr, lanes].astype(F32), axis=1, keepdims=True)
                ds = p * (_dotg(do, v, NT) - delta)
                shared = {}
                for a in range(NA_TQ):
                    for c in range(NA_KCH):
                        shared.setdefault(2 * c - a, []).append(
                            ds[GRID_W * a:GRID_W * (a + 1), LANES * c:LANES * (c + 1)])
                for offset, parts in shared.items():
                    e = jnp.clip(offset + kbase - r + NA_ROWS, 0, NA_T2 - 1)
                    dt2_ref[hh, e] = dt2_ref[hh, e] + functools.reduce(jnp.add, parts)
                dsb = ds.astype(BF16)
                dq_ref[qr, lanes] = _dot(dsb, k)
                dk_ref[kr, lanes] = dk_ref[kr, lanes] + _dotg(dsb, q, TN)
                dv_ref[kr, lanes] = dv_ref[kr, lanes] + _dotg(p.astype(BF16), do, TN)
            return carry

        lax.fori_loop(0, rows // NA_TQ, tile, 0, unroll=2)

    col = lambda off: pl.BlockSpec((s, LANES), lambda p, _o=off: (0, _o + p))
    t2spec = pl.BlockSpec((2, NA_T2, GRID_W, LANES), lambda p: (p, 0, 0, 0))
    return pl.pallas_call(
        body, name=name, grid=(n_pairs,),
        in_specs=[col(0), col(0), col(v_blk0), t2spec, col(0), col(0)],
        out_specs=[col(0), col(0), col(0), t2spec],
        out_shape=[jax.ShapeDtypeStruct((s, NA_WIDTH), F32)] * 3 + [jax.ShapeDtypeStruct(t2.shape, F32)],
        compiler_params=_params(),
    )(qa, ka, zq, t2, o_na, do_na)


def _t5_bucket_map():
    rel = np.arange(3 * SW_BLOCK)[None, :] - SW_BLOCK - np.arange(SW_BLOCK)[:, None]
    nb = REL_BUCKETS // 2
    max_exact = nb // 2
    n = np.abs(rel)
    large = max_exact + (np.log(np.maximum(n, 1) / max_exact)
                         / np.log(REL_MAX_DIST / max_exact) * (nb - max_exact)).astype(np.int32)
    large = np.minimum(large, nb - 1)
    return ((rel > 0) * nb + np.where(n < max_exact, n, large)).astype(np.int32)


def t5_expand(table, bmap, dep, name):
    def body(tab_ref, bm_ref, dep_ref, o_ref):
        bm = bm_ref[...]
        for h in range(SW_HEADS):
            t = jnp.zeros(bm.shape, F32)
            for b in range(REL_BUCKETS):
                t = jnp.where(bm == b, tab_ref[b, h], t)
            o_ref[h] = t

    return pl.pallas_call(
        body, name=name,
        in_specs=[pl.BlockSpec(memory_space=pltpu.SMEM), pl.BlockSpec(memory_space=pltpu.VMEM),
                  pl.BlockSpec(memory_space=pltpu.VMEM)],
        out_specs=pl.BlockSpec(memory_space=pltpu.VMEM),
        out_shape=jax.ShapeDtypeStruct((SW_HEADS,) + bmap.shape, F32),
        compiler_params=pltpu.CompilerParams(vmem_limit_bytes=V7X_VMEM_LIMIT),
    )(table, bmap, dep)


def t5_reduce(dbias_list, bmap, name):
    n = len(dbias_list)

    def body(*refs):
        d_refs, bm_ref, o_ref = refs[:n], refs[n], refs[n + 1]
        bm = bm_ref[...]
        for h in range(SW_HEADS):
            dv = d_refs[0][h]
            for other in d_refs[1:]:
                dv = dv + other[h]
            rows = [jnp.sum(jnp.where(bm == b, dv, 0.0), axis=0, keepdims=True) for b in range(REL_BUCKETS)]
            r = jnp.concatenate(rows, axis=0)
            o_ref[h] = jnp.broadcast_to(jnp.sum(r, axis=1, keepdims=True), (REL_BUCKETS, LANES))

    return pl.pallas_call(
        body, name=name,
        in_specs=[pl.BlockSpec(memory_space=pltpu.VMEM)] * (n + 1),
        out_specs=pl.BlockSpec(memory_space=pltpu.VMEM),
        out_shape=jax.ShapeDtypeStruct((SW_HEADS, REL_BUCKETS, LANES), F32),
        compiler_params=pltpu.CompilerParams(vmem_limit_bytes=V7X_VMEM_LIMIT),
    )(*dbias_list, bmap)


def _sw_mask_iotas():
    a = lax.broadcasted_iota(jnp.int32, (SW_BLOCK, 3 * SW_BLOCK), 0)
    j = lax.broadcasted_iota(jnp.int32, (SW_BLOCK, 3 * SW_BLOCK), 1)
    inwin = jnp.abs(j - SW_BLOCK - a) <= SW_BLOCK
    return j, inwin


SW_STACK = SW_HEADS * SW_BLOCK


def _sw_softmax(sc, sk):
    m = jnp.maximum(jnp.max(sc, axis=1, keepdims=True), sk)
    e = jnp.exp(sc - m)
    es = jnp.exp(sk - m)
    inv = 1.0 / (jnp.sum(e, axis=1, keepdims=True) + es)
    return e * inv, es * inv


def _sw_prologue(k_ref, v_ref, kp, vp, sink_ref, s):
    pad = s + 2 * SW_BLOCK
    zeros = jnp.zeros((SW_BLOCK, SW_KV_WIDTH), BF16)
    kp[0:SW_BLOCK, :] = zeros
    vp[0:SW_BLOCK, :] = zeros
    kp[SW_BLOCK + s:pad, :] = zeros
    vp[SW_BLOCK + s:pad, :] = zeros
    kp[SW_BLOCK:SW_BLOCK + s, :] = k_ref[...]
    vp[SW_BLOCK:SW_BLOCK + s, :] = v_ref[...]
    return jnp.concatenate([jnp.full((SW_BLOCK, 1), sink_ref[h], F32) for h in range(SW_HEADS)], axis=0)


def sw_fwd(qs, ks, zq, t5b, sink, dep, name):
    s = qs.shape[0]
    nb = s // SW_BLOCK
    v_blk = (3 * NA_WIDTH + SW_Q_WIDTH + SW_KV_WIDTH) // LANES
    pad = s + 2 * SW_BLOCK

    def body(q_ref, k_ref, v_ref, b_ref, sink_ref, dep_ref, o_ref, kp, vp, s_scr, p_scr):
        sink_col = _sw_prologue(k_ref, v_ref, kp, vp, sink_ref, s)
        j, inwin = _sw_mask_iotas()

        def blk(n, carry):
            kpos = n * SW_BLOCK - SW_BLOCK + j
            madd = jnp.where(inwin & (kpos >= 0) & (kpos < s), 0.0, NEG)
            q0 = pl.multiple_of(n * SW_BLOCK, SW_BLOCK)
            qr, kr = pl.ds(q0, SW_BLOCK), pl.ds(q0, 3 * SW_BLOCK)
            for h in range(SW_HEADS):
                g = h // SW_REP
                s_scr[SW_BLOCK * h:SW_BLOCK * (h + 1), :] = _dotg(
                    q_ref[qr, HEAD_DIM * h:HEAD_DIM * (h + 1)], kp[kr, HEAD_DIM * g:HEAD_DIM * (g + 1)], NT) + madd
            p, _ = _sw_softmax(s_scr[...] + b_ref[...], sink_col)
            p_scr[...] = p.astype(BF16)
            for h in range(SW_HEADS):
                g = h // SW_REP
                o_ref[qr, HEAD_DIM * h:HEAD_DIM * (h + 1)] = _dot(
                    p_scr[SW_BLOCK * h:SW_BLOCK * (h + 1), :], vp[kr, HEAD_DIM * g:HEAD_DIM * (g + 1)]).astype(BF16)
            return carry

        lax.fori_loop(0, nb, blk, 0, unroll=2)

    return pl.pallas_call(
        body, name=name, grid=(1,),
        in_specs=[_full((s, SW_Q_WIDTH)), _full((s, SW_KV_WIDTH)),
                  pl.BlockSpec((s, SW_KV_WIDTH), lambda i: (0, v_blk)),
                  _full((SW_STACK, 3 * SW_BLOCK)), pl.BlockSpec(memory_space=pltpu.SMEM),
                  _full(dep.shape)],
        out_specs=_full((s, SW_Q_WIDTH)),
        out_shape=jax.ShapeDtypeStruct((s, SW_Q_WIDTH), BF16),
        scratch_shapes=[pltpu.VMEM((pad, SW_KV_WIDTH), BF16), pltpu.VMEM((pad, SW_KV_WIDTH), BF16),
                        pltpu.VMEM((SW_STACK, 3 * SW_BLOCK), F32), pltpu.VMEM((SW_STACK, 3 * SW_BLOCK), BF16)],
        compiler_params=_params(),
    )(qs, ks, zq, t5b, sink, dep)


def sw_bwd(qs, ks, zq, t5b, sink, o_sw, do_sw, name):
    s = qs.shape[0]
    nb = s // SW_BLOCK
    v_blk = (3 * NA_WIDTH + SW_Q_WIDTH + SW_KV_WIDTH) // LANES
    pad = s + 2 * SW_BLOCK

    def body(q_ref, k_ref, v_ref, b_ref, sink_ref, o_ref, do_ref,
             dq_ref, dk_ref, dv_ref, db_ref, dsk_ref, kp, vp, dkp, dvp, s_scr, dp_scr, ds_scr, p_scr):
        sink_col = _sw_prologue(k_ref, v_ref, kp, vp, sink_ref, s)
        dkp[...] = jnp.zeros(dkp.shape, F32)
        dvp[...] = jnp.zeros(dvp.shape, F32)
        db_ref[...] = jnp.zeros(db_ref.shape, F32)
        dsk_ref[...] = jnp.zeros(dsk_ref.shape, F32)
        j, inwin = _sw_mask_iotas()

        def blk(n, carry):
            kpos = n * SW_BLOCK - SW_BLOCK + j
            madd = jnp.where(inwin & (kpos >= 0) & (kpos < s), 0.0, NEG)
            q0 = pl.multiple_of(n * SW_BLOCK, SW_BLOCK)
            qr, kr = pl.ds(q0, SW_BLOCK), pl.ds(q0, 3 * SW_BLOCK)
            deltas = []
            for h in range(SW_HEADS):
                g = h // SW_REP
                hl, kl = slice(HEAD_DIM * h, HEAD_DIM * (h + 1)), slice(HEAD_DIM * g, HEAD_DIM * (g + 1))
                rows = slice(SW_BLOCK * h, SW_BLOCK * (h + 1))
                do = do_ref[qr, hl]
                s_scr[rows, :] = _dotg(q_ref[qr, hl], kp[kr, kl], NT) + madd
                dp_scr[rows, :] = _dotg(do, vp[kr, kl], NT)
                deltas.append(jnp.sum(do.astype(F32) * o_ref[qr, hl].astype(F32), axis=1, keepdims=True))
            delta = jnp.concatenate(deltas, axis=0)
            p, ps = _sw_softmax(s_scr[...] + b_ref[...], sink_col)
            ds = p * (dp_scr[...] - delta)
            db_ref[...] = db_ref[...] + ds
            dsk_ref[...] = dsk_ref[...] - jnp.broadcast_to(ps * delta, (SW_STACK, LANES))
            ds_scr[...] = ds.astype(BF16)
            p_scr[...] = p.astype(BF16)
            for g in range(SW_HEADS // SW_REP):
                kl = slice(HEAD_DIM * g, HEAD_DIM * (g + 1))
                k = kp[kr, kl]
                dkw = jnp.zeros((3 * SW_BLOCK, HEAD_DIM), F32)
                dvw = jnp.zeros((3 * SW_BLOCK, HEAD_DIM), F32)
                for r in range(SW_REP):
                    h = g * SW_REP + r
                    hl, rows = slice(HEAD_DIM * h, HEAD_DIM * (h + 1)), slice(SW_BLOCK * h, SW_BLOCK * (h + 1))
                    dsb = ds_scr[rows, :]
                    dq_ref[qr, hl] = _dot(dsb, k)
                    dkw = dkw + _dotg(dsb, q_ref[qr, hl], TN)
                    dvw = dvw + _dotg(p_scr[rows, :], do_ref[qr, hl], TN)
                dkp[kr, kl] = dkp[kr, kl] + dkw
                dvp[kr, kl] = dvp[kr, kl] + dvw
            return carry

        lax.fori_loop(0, nb, blk, 0)
        dk_ref[...] = dkp[SW_BLOCK:SW_BLOCK + s, :]
        dv_ref[...] = dvp[SW_BLOCK:SW_BLOCK + s, :]

    bias_spec = _full((SW_STACK, 3 * SW_BLOCK))
    return pl.pallas_call(
        body, name=name, grid=(1,),
        in_specs=[_full((s, SW_Q_WIDTH)), _full((s, SW_KV_WIDTH)),
                  pl.BlockSpec((s, SW_KV_WIDTH), lambda i: (0, v_blk)),
                  bias_spec, pl.BlockSpec(memory_space=pltpu.SMEM),
                  _full((s, SW_Q_WIDTH)), _full((s, SW_Q_WIDTH))],
        out_specs=[_full((s, SW_Q_WIDTH)), _full((s, SW_KV_WIDTH)), _full((s, SW_KV_WIDTH)), bias_spec,
                   _full((SW_STACK, LANES))],
        out_shape=[jax.ShapeDtypeStruct((s, SW_Q_WIDTH), F32), jax.ShapeDtypeStruct((s, SW_KV_WIDTH), F32),
                   jax.ShapeDtypeStruct((s, SW_KV_WIDTH), F32),
                   jax.ShapeDtypeStruct((SW_STACK, 3 * SW_BLOCK), F32),
                   jax.ShapeDtypeStruct((SW_STACK, LANES), F32)],
        scratch_shapes=[pltpu.VMEM((pad, SW_KV_WIDTH), BF16), pltpu.VMEM((pad, SW_KV_WIDTH), BF16),
                        pltpu.VMEM((pad, SW_KV_WIDTH), F32), pltpu.VMEM((pad, SW_KV_WIDTH), F32),
                        pltpu.VMEM((SW_STACK, 3 * SW_BLOCK), F32), pltpu.VMEM((SW_STACK, 3 * SW_BLOCK), F32),
                        pltpu.VMEM((SW_STACK, 3 * SW_BLOCK), BF16), pltpu.VMEM((SW_STACK, 3 * SW_BLOCK), BF16)],
        compiler_params=_params(),
    )(qs, ks, zq, t5b, sink, o_sw, do_sw)


def merge_out(x, o_na, o_sw, gt, wbna_t, wbsw_t, wout, name):
    s, d = x.shape
    tm = _row_tile(s)

    def body(x_ref, ona_ref, osw_ref, gt_ref, wna_ref, wsw_ref, wo_ref, xo_ref, ana_ref, asw_ref, mg_ref):
        a_na = _dotg(ona_ref[...], wna_ref[...], NT)
        a_sw = _dotg(osw_ref[...], wsw_ref[...], NT)
        g_na, g_sw = gt_ref[:, 0:d].astype(F32), gt_ref[:, d:2 * d].astype(F32)
        ana_ref[...] = (a_na * g_na * (1.0 - g_na)).astype(BF16)
        asw_ref[...] = (a_sw * g_sw * (1.0 - g_sw)).astype(BF16)
        merged = (g_na * a_na + g_sw * a_sw).astype(BF16)
        mg_ref[...] = merged
        xo_ref[...] = x_ref[...] + _dot(merged, wo_ref[...])

    return pl.pallas_call(
        body, name=name, grid=(s // tm,),
        in_specs=[_rows(tm, d), _rows(tm, 512), _rows(tm, 512), _rows(tm, 2 * d),
                  _mat(*wbna_t), _mat(*wbsw_t), _mat(*wout)],
        out_specs=[_rows(tm, d)] * 4,
        out_shape=[jax.ShapeDtypeStruct((s, d), F32)] + [jax.ShapeDtypeStruct((s, d), BF16)] * 3,
        compiler_params=_params(),
    )(x, o_na, o_sw, gt, wbna_t[0], wbsw_t[0], wout[0])


def mix_bwd_out(dx, gt, a_na, a_sw, wbna_t, wbsw_t, wout, dep, name):
    s, d = dx.shape
    tm = _row_tile(s)

    def body(dx_ref, gt_ref, ana_ref, asw_ref, wna_ref, wsw_ref, wo_ref, dep_ref,
             dxb_ref, dzg_ref, dana_ref, dasw_ref, dona_ref, dosw_ref, dbg_ref):
        @pl.when(pl.program_id(0) == 0)
        def _():
            dbg_ref[...] = jnp.zeros(dbg_ref.shape, F32)

        dxb = dx_ref[...].astype(BF16)
        dxb_ref[...] = dxb
        dm = _dotg(dxb, wo_ref[...], NT)
        for i, (a_ref, da_ref, w_ref, do_ref) in enumerate(
                [(ana_ref, dana_ref, wna_ref, dona_ref), (asw_ref, dasw_ref, wsw_ref, dosw_ref)]):
            gi = gt_ref[:, i * d:(i + 1) * d].astype(F32)
            da = (dm * gi).astype(BF16)
            da_ref[...] = da
            do_ref[...] = _dot(da, w_ref[...]).astype(BF16)
            dzg = dm * a_ref[...].astype(F32)
            dzg_ref[:, i * d:(i + 1) * d] = dzg.astype(BF16)
            dbg_ref[:, i * d:(i + 1) * d] = dbg_ref[:, i * d:(i + 1) * d] + jnp.sum(dzg, axis=0, keepdims=True)

    return pl.pallas_call(
        body, name=name, grid=(s // tm,),
        in_specs=[_rows(tm, d), _rows(tm, 2 * d), _rows(tm, d), _rows(tm, d),
                  _mat(*wbna_t), _mat(*wbsw_t), _mat(*wout), _full(dep.shape)],
        out_specs=[_rows(tm, d), _rows(tm, 2 * d), _rows(tm, d), _rows(tm, d), _rows(tm, 512), _rows(tm, 512),
                   _full((1, 2 * d))],
        out_shape=[jax.ShapeDtypeStruct((s, d), BF16), jax.ShapeDtypeStruct((s, 2 * d), BF16),
                   jax.ShapeDtypeStruct((s, d), BF16), jax.ShapeDtypeStruct((s, d), BF16),
                   jax.ShapeDtypeStruct((s, 512), BF16), jax.ShapeDtypeStruct((s, 512), BF16),
                   jax.ShapeDtypeStruct((1, 2 * d), F32)],
        compiler_params=_params(),
    )(dx, gt, a_na, a_sw, wbna_t[0], wbsw_t[0], wout[0], dep)


def qk_norm_bwd(dqa, dka, dva, dqs, dks, dvs, zq, dzg, gq_na, gk_na, gq_sw, gk_sw, bd, name):
    s = zq.shape[0]
    d2 = dzg.shape[1]
    n_in = QKV_WIDTH + d2
    tm = _row_tile(s)

    def body(dqa_ref, dka_ref, dva_ref, dqs_ref, dks_ref, dvs_ref, zq_ref, dzg_ref,
             gqa_ref, gka_ref, gqs_ref, gks_ref, bd_ref, dz_ref, dgqa_ref, dgka_ref, dgqs_ref, dgks_ref):
        @pl.when(pl.program_id(0) == 0)
        def _():
            for r in (dgqa_ref, dgka_ref, dgqs_ref, dgks_ref):
                r[...] = jnp.zeros(r.shape, F32)

        bd512 = bd_ref[...]
        bd128 = bd_ref[0:SW_KV_WIDTH, 0:SW_KV_WIDTH]

        def one(c0, c1, dy_ref, g_ref, dg_ref, bdm, scale):
            z = zq_ref[:, c0:c1].astype(F32)
            r = lax.rsqrt(_group_mean(z * z, bdm) + EPS)
            zh = z * r
            dy = dy_ref[...] * scale
            dyg = dy * g_ref[...]
            dz = r * (dyg - zh * _group_mean(dyg * zh, bdm))
            dz_ref[:, c0:c1] = dz.astype(BF16)
            dg_ref[...] = dg_ref[...] + jnp.sum(dy * zh, axis=0, keepdims=True)

        one(0, 512, dqa_ref, gqa_ref, dgqa_ref, bd512, SCALE)
        one(512, 1024, dka_ref, gka_ref, dgka_ref, bd512, 1.0)
        dz_ref[:, 1024:1536] = dva_ref[...].astype(BF16)
        one(1536, 2048, dqs_ref, gqs_ref, dgqs_ref, bd512, SCALE)
        one(2048, 2176, dks_ref, gks_ref, dgks_ref, bd128, 1.0)
        dz_ref[:, 2176:2304] = dvs_ref[...].astype(BF16)
        dz_ref[:, QKV_WIDTH:n_in] = dzg_ref[...]

    return pl.pallas_call(
        body, name=name, grid=(s // tm,),
        in_specs=[_rows(tm, 512), _rows(tm, 512), _rows(tm, 512), _rows(tm, 512), _rows(tm, 128), _rows(tm, 128),
                  _rows(tm, QKV_WIDTH), _rows(tm, d2),
                  _full((1, 512)), _full((1, 512)), _full((1, 512)), _full((1, 128)), _full((MXU_TILE, MXU_TILE))],
        out_specs=[_rows(tm, n_in), _full((1, 512)), _full((1, 512)), _full((1, 512)), _full((1, 128))],
        out_shape=[jax.ShapeDtypeStruct((s, n_in), BF16)] + [jax.ShapeDtypeStruct((1, 512), F32)] * 3
                  + [jax.ShapeDtypeStruct((1, 128), F32)],
        compiler_params=_params(),
    )(dqa, dka, dva, dqs, dks, dvs, zq, dzg, gq_na, gk_na, gq_sw, gk_sw, bd)


def ffn_bwd_act(dx, wd, hg, hu, name):
    s, d = dx.shape
    f = wd[0].shape[1]
    tm = _row_tile(s)
    fc = _col_chunk(f)

    def body(dx_ref, w_ref, hg_ref, hu_ref, dxb_ref, dhg_ref, dhu_ref):
        dxv = dx_ref[...]
        dxb_ref[...] = dxv.astype(BF16)
        half = (0.5 * dxv).astype(BF16)
        for c0 in range(0, f, fc):
            dact = _dotg(half, w_ref[c0:c0 + fc, :], NT)
            dhu_ref[:, c0:c0 + fc] = (dact * hu_ref[:, c0:c0 + fc].astype(F32)).astype(BF16)
            dhg_ref[:, c0:c0 + fc] = (dact * hg_ref[:, c0:c0 + fc].astype(F32)).astype(BF16)

    return pl.pallas_call(
        body, name=name, grid=(s // tm,),
        in_specs=[_rows(tm, d), _mat(*wd), _rows(tm, f), _rows(tm, f)],
        out_specs=[_rows(tm, d), _rows(tm, f), _rows(tm, f)],
        out_shape=[jax.ShapeDtypeStruct((s, d), BF16), jax.ShapeDtypeStruct((s, f), BF16),
                   jax.ShapeDtypeStruct((s, f), BF16)],
        compiler_params=_params(),
    )(dx, wd[0], hg, hu)


def proj_bwd_norm(acts, weights, x, gain, dx, dep, name):
    s, d = x.shape
    tm = min(_row_tile(s), 256)
    n = len(acts)

    def body(*refs):
        a_refs, w_refs = refs[:n], refs[n:2 * n]
        x_ref, g_ref, dx_ref, _, o_ref, dg_ref = refs[2 * n:]

        @pl.when(pl.program_id(0) == 0)
        def _():
            dg_ref[...] = jnp.zeros(dg_ref.shape, F32)

        dxn = _dot(a_refs[0][...], w_refs[0][...])
        for a_ref, w_ref in zip(a_refs[1:], w_refs[1:]):
            dxn = dxn + _dot(a_ref[...], w_ref[...])
        xv = x_ref[...]
        r = _rstd(xv)
        xh = xv * r
        dxh = dxn * g_ref[...]
        o_ref[...] = dx_ref[...] + r * (dxh - xh * jnp.mean(dxh * xh, axis=-1, keepdims=True))
        dg_ref[...] = dg_ref[...] + jnp.sum(dxn * xh, axis=0, keepdims=True)

    return pl.pallas_call(
        body, name=name, grid=(s // tm,),
        in_specs=[_rows(tm, a.shape[1]) for a in acts] + [_mat(*w) for w in weights]
                 + [_rows(tm, d), _full((1, d)), _rows(tm, d), _full(dep.shape)],
        out_specs=[_rows(tm, d), _full((1, d))],
        out_shape=[jax.ShapeDtypeStruct((s, d), F32), jax.ShapeDtypeStruct((1, d), F32)],
        compiler_params=_params(),
    )(*acts, *[w[0] for w in weights], x, gain, dx, dep)


def tn_matmul(products, name):
    s, n = products[0][0].shape
    tn = _tn_tile(n) if len(products) == 1 else _col_chunk(n)
    rhs = []
    for _, b, _ in products:
        if not any(b is seen for seen in rhs):
            rhs.append(b)
    which = [next(i for i, seen in enumerate(rhs) if b is seen) for _, b, _ in products]
    npr, nr = len(products), len(rhs)

    def body(*refs):
        a_refs, b_refs, o_refs = refs[:npr], refs[npr:npr + nr], refs[npr + nr:]
        for i, (_, _, scale) in enumerate(products):
            o_refs[i][...] = (scale * _dotg(a_refs[i][...], b_refs[which[i]][...], TN)).astype(BF16)

    return pl.pallas_call(
        body, name=name, grid=(n // tn,),
        in_specs=[pl.BlockSpec((s, tn), lambda i: (0, i))] * npr
                 + [pl.BlockSpec(b.shape, lambda i: (0, 0), pipeline_mode=ONCE) for b in rhs],
        out_specs=[pl.BlockSpec((tn, b.shape[1]), lambda i: (i, 0)) for _, b, _ in products],
        out_shape=[jax.ShapeDtypeStruct((n, b.shape[1]), BF16) for _, b, _ in products],
        compiler_params=_params(),
    )(*[a for a, _, _ in products], *rhs)


def _mesh_pos():
    return lax.axis_index("x"), lax.axis_index("y"), lax.axis_index("c")


def _peers():
    x, y, c = _mesh_pos()
    peers = []
    for rel in range(1, N_DEV):
        peers.append((1 - x if rel & 4 else x, 1 - y if rel & 2 else y, 1 - c if rel & 1 else c))
    return 4 * x + 2 * y + c, peers


HBM_SPEC = pl.BlockSpec(memory_space=pltpu.HBM)
SEM_SPEC = pl.BlockSpec(memory_space=pltpu.SEMAPHORE)


def _split_call(body, name, thru, n_sems, extra=(), with_token=True):
    hbm = lambda t: pltpu.with_memory_space_constraint(t, pltpu.HBM)
    effect = pltpu.CompilerParams(has_side_effects=pltpu.SideEffectType.DATAFLOW_SIDE_EFFECTING)
    nt = len(thru)
    thru_shapes = [pltpu.HBM(t.shape, t.dtype) for t in thru]
    if with_token:
        (after,) = extra
        outs = pl.pallas_call(
            body, name=name, in_specs=[HBM_SPEC] * nt + [pl.BlockSpec(memory_space=pl.ANY)],
            out_specs=[SEM_SPEC] * len(n_sems) + [HBM_SPEC] * nt + [pl.BlockSpec(memory_space=pltpu.VMEM)],
            out_shape=[pltpu.SemaphoreType.DMA((k,)) for k in n_sems] + thru_shapes
                      + [jax.ShapeDtypeStruct((8, LANES), F32)],
            input_output_aliases={i: len(n_sems) + i for i in range(nt)}, compiler_params=effect,
        )(*[hbm(t) for t in thru], after)
        return outs[:len(n_sems)], outs[len(n_sems):-1], outs[-1]
    return pl.pallas_call(
        body, name=name,
        in_specs=[HBM_SPEC] * nt + [SEM_SPEC] * len(n_sems) + [pl.BlockSpec(memory_space=pl.ANY)],
        out_specs=[HBM_SPEC] * nt, out_shape=thru_shapes,
        input_output_aliases={i: i for i in range(nt)}, compiler_params=effect,
    )(*thru, *extra)


def _gather_targets():
    x, y, c = _mesh_pos()
    return 4 * x + 2 * y + c, [(x, y, 1 - c), (1 - x, y, c), (x, 1 - y, c), (1 - x, 1 - y, c)]


def gather_start(shards, after, name):
    n = len(shards)
    zones = [lax.empty((w.shape[0], N_DEV) + w.shape[1:], w.dtype) for w in shards]

    def body(*refs):
        ins, zs = refs[:n], refs[n:2 * n]
        send_sems, recv_sems, local_sems = refs[2 * n + 1:2 * n + 4]
        token = refs[-1]
        me, targets = _gather_targets()
        for a in range(n):
            pltpu.make_async_copy(ins[a], zs[a].at[:, me], local_sems.at[a]).start()
            for k, to in enumerate(targets):
                pltpu.make_async_remote_copy(
                    src_ref=ins[a], dst_ref=zs[a].at[:, me], send_sem=send_sems.at[4 * a + k],
                    recv_sem=recv_sems.at[4 * a + k], device_id=to, device_id_type=MESH).start()
        token[...] = jnp.zeros(token.shape, F32)

    sems, thru, token = _split_call(body, name, list(shards) + zones, (4 * n, 4 * n, n), extra=(after,))
    return (sems, thru, n), token


def gather_wait(started, after, name):
    sems, thru, n = started

    def body(*refs):
        zs = refs[n:2 * n]
        send_sems, recv_sems, local_sems = refs[2 * n:2 * n + 3]
        _, targets = _gather_targets()
        for a in range(n):
            for k, to in enumerate(targets):
                cp = pltpu.make_async_remote_copy(
                    src_ref=zs[a].at[:, 0], dst_ref=zs[a].at[:, 0], send_sem=send_sems.at[4 * a + k],
                    recv_sem=recv_sems.at[4 * a + k], device_id=to, device_id_type=MESH)
                cp.wait_send()
                cp.wait_recv()
            pltpu.make_async_copy(zs[a].at[:, 0], zs[a].at[:, 0], local_sems.at[a]).wait()

    return _split_call(body, name, thru, (4 * n, 4 * n, n), extra=(*sems, after), with_token=False)[n:]


def forward_start(zones, after, name):
    n = len(zones)

    def body(*refs):
        zs = refs[:n]
        send_sems, recv_sems = refs[n + 1:n + 3]
        token = refs[-1]
        x, y, c = _mesh_pos()
        for a in range(n):
            for j, chip in enumerate([(1 - x, y), (x, 1 - y), (1 - x, 1 - y)]):
                blk = zs[a].at[:, 4 * chip[0] + 2 * chip[1] + c]
                pltpu.make_async_remote_copy(
                    src_ref=blk, dst_ref=blk, send_sem=send_sems.at[3 * a + j], recv_sem=recv_sems.at[3 * a + j],
                    device_id=(x, y, 1 - c), device_id_type=MESH).start()
        token[...] = jnp.zeros(token.shape, F32)

    sems, thru, token = _split_call(body, name, list(zones), (3 * n, 3 * n), extra=(after,))
    return (sems, thru, n), token


def forward_wait(started, after, name):
    sems, thru, n = started

    def body(*refs):
        zs = refs[:n]
        send_sems, recv_sems = refs[n:n + 2]
        x, y, c = _mesh_pos()
        for a in range(n):
            for j in range(3):
                cp = pltpu.make_async_remote_copy(
                    src_ref=zs[a].at[:, 0], dst_ref=zs[a].at[:, 0], send_sem=send_sems.at[3 * a + j],
                    recv_sem=recv_sems.at[3 * a + j], device_id=(x, y, 1 - c), device_id_type=MESH)
                cp.wait_send()
                cp.wait_recv()

    return _split_call(body, name, thru, (3 * n, 3 * n), extra=(*sems, after), with_token=False)


def scatter_start(groups, name):
    n = len(groups)
    flat = [g for grp in groups for g in grp]
    nf = len(flat)
    offs = np.cumsum([0] + [len(grp) for grp in groups])
    lands = [lax.empty((N_DEV, len(grp)) + grp[0].shape[1:], grp[0].dtype) for grp in groups]

    def body(*refs):
        ins, zones = refs[:nf], refs[nf:nf + n]
        send_sems, recv_sems, local_sems = refs[nf + n:nf + n + 3]
        token = refs[-1]
        me, peers = _peers()
        for a in range(n):
            for w in range(len(groups[a])):
                pltpu.make_async_copy(ins[offs[a] + w].at[me], zones[a].at[me, w], local_sems.at[a]).start()
        for k, peer in enumerate(peers):
            p_id = 4 * peer[0] + 2 * peer[1] + peer[2]
            for a in range(n):
                for w in range(len(groups[a])):
                    pltpu.make_async_remote_copy(
                        src_ref=ins[offs[a] + w].at[p_id], dst_ref=zones[a].at[me, w],
                        send_sem=send_sems.at[7 * a + k], recv_sem=recv_sems.at[7 * a + k],
                        device_id=peer, device_id_type=MESH).start()
        token[...] = jnp.zeros(token.shape, F32)

    hbm = lambda t: pltpu.with_memory_space_constraint(t, pltpu.HBM)
    outs = pl.pallas_call(
        body, name=name,
        in_specs=[HBM_SPEC] * (nf + n),
        out_specs=[SEM_SPEC] * 3 + [HBM_SPEC] * (nf + n) + [pl.BlockSpec(memory_space=pltpu.VMEM)],
        out_shape=[pltpu.SemaphoreType.DMA((7 * n,)), pltpu.SemaphoreType.DMA((7 * n,)), pltpu.SemaphoreType.DMA((n,))]
                  + [pltpu.HBM(t.shape, t.dtype) for t in flat + lands]
                  + [jax.ShapeDtypeStruct((8, LANES), F32)],
        input_output_aliases={i: 3 + i for i in range(nf + n)},
        compiler_params=pltpu.CompilerParams(has_side_effects=pltpu.SideEffectType.DATAFLOW_SIDE_EFFECTING),
    )(*[hbm(t) for t in flat], *[hbm(t) for t in lands])
    sems, thru, token = outs[:3], outs[3:3 + nf + n], outs[-1]
    return (sems, thru, [len(grp) for grp in groups]), token


def scatter_wait(started, after, name):
    (send_sems, recv_sems, local_sems), thru, sizes = started
    n = len(sizes)
    nf = len(thru) - n

    def body(*refs):
        zones = refs[nf:nf + n]
        s_sems, r_sems, l_sems = refs[nf + n:nf + n + 3]
        me, peers = _peers()
        for a in range(n):
            for k, peer in enumerate(peers):
                cp = pltpu.make_async_remote_copy(
                    src_ref=zones[a].at[0], dst_ref=zones[a].at[0],
                    send_sem=s_sems.at[7 * a + k], recv_sem=r_sems.at[7 * a + k], device_id=peer,
                    device_id_type=MESH)
                cp.wait_send()
                cp.wait_recv()
            pltpu.make_async_copy(zones[a].at[0], zones[a].at[0], l_sems.at[a]).wait()

    outs = pl.pallas_call(
        body, name=name,
        in_specs=[HBM_SPEC] * (nf + n) + [SEM_SPEC] * 3 + [pl.BlockSpec(memory_space=pl.ANY)],
        out_specs=[HBM_SPEC] * (nf + n),
        out_shape=[pltpu.HBM(t.shape, t.dtype) for t in thru],
        input_output_aliases={i: i for i in range(nf + n)},
        compiler_params=pltpu.CompilerParams(has_side_effects=pltpu.SideEffectType.DATAFLOW_SIDE_EFFECTING),
    )(*thru, send_sems, recv_sems, local_sems, after)
    return outs[nf:]


def pair_start(grads, after, name):
    nw = len(grads)
    land = lax.empty((4, nw) + grads[0].shape[1:], grads[0].dtype)

    def body(*refs):
        ins, zone = refs[:nw], refs[nw]
        send_sems, recv_sems = refs[nw + 2:nw + 4]
        x, y, c = _mesh_pos()
        for j in range(4):
            for w in range(nw):
                pltpu.make_async_remote_copy(
                    src_ref=ins[w].at[2 * j + (1 - c)], dst_ref=zone.at[j, w], send_sem=send_sems.at[0],
                    recv_sem=recv_sems.at[0], device_id=(x, y, 1 - c), device_id_type=MESH).start()
        refs[-1][...] = jnp.zeros(refs[-1].shape, F32)

    sems, thru, token = _split_call(body, name, list(grads) + [land], (1, 1), extra=(after,))
    return (sems, thru, nw), token


def pair_wait(started, after, name):
    sems, thru, nw = started

    def body(*refs):
        zone = refs[nw]
        send_sems, recv_sems = refs[nw + 1:nw + 3]
        x, y, c = _mesh_pos()
        cp = pltpu.make_async_remote_copy(src_ref=zone, dst_ref=zone, send_sem=send_sems.at[0],
                                          recv_sem=recv_sems.at[0], device_id=(x, y, 1 - c), device_id_type=MESH)
        cp.wait_send()
        cp.wait_recv()

    outs = _split_call(body, name, thru, (1, 1), extra=(*sems, after), with_token=False)
    return outs[:nw], outs[nw]


def pair_sum(grads, land, name):
    nw = len(grads)
    _, r, c_dim = grads[0].shape

    def body(*refs):
        g_refs, l_ref, o_ref = refs[:nw], refs[nw], refs[nw + 1]
        core = lax.axis_index("c")
        for w in range(nw):
            o_ref[0, w] = (g_refs[w][0, core].astype(F32) + l_ref[0, w].astype(F32)).astype(BF16)

    return pl.pallas_call(
        body, name=name, grid=(4,),
        in_specs=[pl.BlockSpec((1, 2, r, c_dim), lambda j: (j, 0, 0, 0))] * nw
                 + [pl.BlockSpec((1, nw, r, c_dim), lambda j: (j, 0, 0, 0))],
        out_specs=pl.BlockSpec((1, nw, r, c_dim), lambda j: (j, 0, 0, 0)),
        out_shape=jax.ShapeDtypeStruct((4, nw, r, c_dim), BF16),
        compiler_params=_params(),
    )(*[g.reshape(4, 2, r, c_dim) for g in grads], land)


def _other_chips():
    x, y, c = _mesh_pos()
    chips = []
    for rel in range(1, 4):
        px, py = (1 - x if rel & 2 else x), (1 - y if rel & 1 else y)
        chips.append((px, py, 2 * px + py))
    return 2 * x + y, c, chips


def chip_start(pair_sums, after, name):
    land = lax.empty(pair_sums.shape, pair_sums.dtype)

    def body(*refs):
        h_ref, zone = refs[0], refs[1]
        send_sems, recv_sems, local_sem = refs[3:6]
        mine, c, chips = _other_chips()
        pltpu.make_async_copy(h_ref.at[mine], zone.at[mine], local_sem.at[0]).start()
        for k, (px, py, j) in enumerate(chips):
            pltpu.make_async_remote_copy(
                src_ref=h_ref.at[j], dst_ref=zone.at[mine], send_sem=send_sems.at[k], recv_sem=recv_sems.at[k],
                device_id=(px, py, c), device_id_type=MESH).start()
        refs[-1][...] = jnp.zeros(refs[-1].shape, F32)

    sems, thru, token = _split_call(body, name, [pair_sums, land], (3, 3, 1), extra=(after,))
    return (sems, thru), token


def chip_wait(started, after, name):
    sems, thru = started

    def body(*refs):
        zone = refs[1]
        send_sems, recv_sems, local_sem = refs[2:5]
        _, c, chips = _other_chips()
        for k, (px, py, _) in enumerate(chips):
            cp = pltpu.make_async_remote_copy(
                src_ref=zone.at[0], dst_ref=zone.at[0], send_sem=send_sems.at[k], recv_sem=recv_sems.at[k],
                device_id=(px, py, c), device_id_type=MESH)
            cp.wait_send()
            cp.wait_recv()
        pltpu.make_async_copy(zone.at[0], zone.at[0], local_sem.at[0]).wait()

    return _split_call(body, name, thru, (3, 3, 1), extra=(*sems, after), with_token=False)[1]


def share_start(parts, after, name):
    n = len(parts)
    zones = [lax.empty((N_DEV,) + p.shape, p.dtype) for p in parts]

    def body(*refs):
        ins, zs = refs[:n], refs[n:2 * n]
        send_sems, recv_sems, local_sems = refs[2 * n + 1:2 * n + 4]
        me, peers = _peers()
        for i in range(n):
            pltpu.make_async_copy(ins[i], zs[i].at[me], local_sems.at[i]).start()
            for k, peer in enumerate(peers):
                pltpu.make_async_remote_copy(
                    src_ref=ins[i], dst_ref=zs[i].at[me], send_sem=send_sems.at[7 * i + k],
                    recv_sem=recv_sems.at[7 * i + k], device_id=peer, device_id_type=MESH).start()
        refs[-1][...] = jnp.zeros(refs[-1].shape, F32)

    sems, thru, token = _split_call(body, name, list(parts) + zones, (7 * n, 7 * n, n), extra=(after,))
    return (sems, thru, n), token


def share_wait(started, after, name):
    sems, thru, n = started

    def body(*refs):
        zs = refs[n:2 * n]
        send_sems, recv_sems, local_sems = refs[2 * n:2 * n + 3]
        _, peers = _peers()
        for i in range(n):
            for k, peer in enumerate(peers):
                cp = pltpu.make_async_remote_copy(
                    src_ref=zs[i].at[0], dst_ref=zs[i].at[0], send_sem=send_sems.at[7 * i + k],
                    recv_sem=recv_sems.at[7 * i + k], device_id=peer, device_id_type=MESH)
                cp.wait_send()
                cp.wait_recv()
            pltpu.make_async_copy(zs[i].at[0], zs[i].at[0], local_sems.at[i]).wait()

    return _split_call(body, name, thru, (7 * n, 7 * n, n), extra=(*sems, after), with_token=False)[n:]


def _adamw_math(w, g, m, v):
    m = ADAM_B1 * m + (1.0 - ADAM_B1) * g
    v = ADAM_B2 * v + (1.0 - ADAM_B2) * (g * g)
    m_hat = m / (1.0 - ADAM_B1 ** ADAM_STEP)
    v_hat = v / (1.0 - ADAM_B2 ** ADAM_STEP)
    delta = -ADAM_LR * (m_hat / (jnp.sqrt(v_hat) + ADAM_EPS) + ADAM_WD * w)
    return delta, m, v


ADAMW_BLOCK_BYTES = 24 * 1024 * 1024


def adamw_layer(zone, layer, items, after, name):
    n_src, nw, r, c = zone.shape
    depth = items[0][0].shape[0]
    prevs = [p if p is not None else tuple(lax.empty((depth, r, c), F32) for _ in range(4)) for _, _, _, p in items]
    row_bytes = 2 * nw * c * (2 * n_src + 4 * 7)
    tr = max(t for t in range(8, r + 1, 8) if r % t == 0 and t * row_bytes <= ADAMW_BLOCK_BYTES)

    def body(z_ref, *rest):
        ins, outs = rest[:3 * nw], rest[7 * nw + 1:]
        for i in range(nw):
            g = z_ref[0, i].astype(F32)
            for src in range(1, n_src):
                g = g + z_ref[src, i].astype(F32)
            g_ref, d_ref, mo_ref, vo_ref = outs[4 * i:4 * i + 4]
            w_ref, m_ref, v_ref = ins[3 * i:3 * i + 3]
            g_ref[...] = g
            d_ref[...], mo_ref[...], vo_ref[...] = _adamw_math(w_ref[...], g, m_ref[...], v_ref[...])

    rows = pl.BlockSpec((None, tr, c), lambda i: (layer, i, 0))
    anywhere = pl.BlockSpec(memory_space=pl.ANY)
    outs = pl.pallas_call(
        body, name=name, grid=(r // tr,),
        in_specs=[pl.BlockSpec((n_src, nw, tr, c), lambda i: (0, 0, i, 0))] + [rows] * (3 * nw)
                 + [anywhere] * (4 * nw + 1),
        out_specs=[rows] * (4 * nw),
        out_shape=[jax.ShapeDtypeStruct((depth, r, c), F32)] * (4 * nw),
        input_output_aliases={1 + 3 * nw + k: k for k in range(4 * nw)},
        compiler_params=_params(),
    )(zone, *[t for w, m, v, _ in items for t in (w, m, v)], *[t for p in prevs for t in p], after)
    return [tuple(outs[4 * i:4 * i + 4]) for i in range(nw)]


def adamw_small(ws, recvs, ms, vs, name):
    n = len(ws)

    def body(*refs):
        w_refs, r_refs, m_refs, v_refs = (refs[i * n:(i + 1) * n] for i in range(4))
        g_refs, d_refs, mo_refs, vo_refs = (refs[(4 + i) * n:(5 + i) * n] for i in range(4))
        for i in range(n):
            g = r_refs[i][0]
            for src in range(1, N_DEV):
                g = g + r_refs[i][src]
            g_refs[i][...] = g
            d_refs[i][...], mo_refs[i][...], vo_refs[i][...] = _adamw_math(w_refs[i][...], g, m_refs[i][...],
                                                                            v_refs[i][...])

    vm = pl.BlockSpec(memory_space=pltpu.VMEM)
    outs = pl.pallas_call(
        body, name=name, in_specs=[vm] * (4 * n), out_specs=[vm] * (4 * n),
        out_shape=[jax.ShapeDtypeStruct(w.shape, F32) for w in ws] * 4,
        compiler_params=pltpu.CompilerParams(vmem_limit_bytes=V7X_VMEM_LIMIT),
    )(*ws, *recvs, *ms, *vs)
    return [outs[i * n:(i + 1) * n] for i in range(4)]


SMALL_NAMES = ("ffn1_norm", "mix_norm", "ffn2_norm", "b_gate", "na_q_norm", "na_k_norm", "sw_q_norm", "sw_k_norm",
               "na_rpb", "sw_sink", "t5_rel_table")


def kernel(x, ffn1_norm, ffn1_w_gate, ffn1_w_up, ffn1_w_down, mix_norm, w_in, b_gate, na_q_norm, na_k_norm, na_rpb, sw_q_norm, sw_k_norm, sw_sink, t5_rel_table, w_branch_na, w_branch_sw, w_out, ffn2_norm, ffn2_w_gate, ffn2_w_up, ffn2_w_down, loss_target, m_ffn1_norm, m_ffn1_w_gate, m_ffn1_w_up, m_ffn1_w_down, m_mix_norm, m_w_in, m_b_gate, m_na_q_norm, m_na_k_norm, m_na_rpb, m_sw_q_norm, m_sw_k_norm, m_sw_sink, m_t5_rel_table, m_w_branch_na, m_w_branch_sw, m_w_out, m_ffn2_norm, m_ffn2_w_gate, m_ffn2_w_up, m_ffn2_w_down, v_ffn1_norm, v_ffn1_w_gate, v_ffn1_w_up, v_ffn1_w_down, v_mix_norm, v_w_in, v_b_gate, v_na_q_norm, v_na_k_norm, v_na_rpb, v_sw_q_norm, v_sw_k_norm, v_sw_sink, v_t5_rel_table, v_w_branch_na, v_w_branch_sw, v_w_out, v_ffn2_norm, v_ffn2_w_gate, v_ffn2_w_up, v_ffn2_w_down):
    weights = dict(ffn1_norm=ffn1_norm, ffn1_w_gate=ffn1_w_gate, ffn1_w_up=ffn1_w_up, ffn1_w_down=ffn1_w_down,
                   mix_norm=mix_norm, w_in=w_in, b_gate=b_gate, na_q_norm=na_q_norm, na_k_norm=na_k_norm,
                   na_rpb=na_rpb, sw_q_norm=sw_q_norm, sw_k_norm=sw_k_norm, sw_sink=sw_sink,
                   t5_rel_table=t5_rel_table, w_branch_na=w_branch_na, w_branch_sw=w_branch_sw, w_out=w_out,
                   ffn2_norm=ffn2_norm, ffn2_w_gate=ffn2_w_gate, ffn2_w_up=ffn2_w_up, ffn2_w_down=ffn2_w_down)
    mom_m = dict(ffn1_norm=m_ffn1_norm, ffn1_w_gate=m_ffn1_w_gate, ffn1_w_up=m_ffn1_w_up, ffn1_w_down=m_ffn1_w_down,
                 mix_norm=m_mix_norm, w_in=m_w_in, b_gate=m_b_gate, na_q_norm=m_na_q_norm, na_k_norm=m_na_k_norm,
                 na_rpb=m_na_rpb, sw_q_norm=m_sw_q_norm, sw_k_norm=m_sw_k_norm, sw_sink=m_sw_sink,
                 t5_rel_table=m_t5_rel_table, w_branch_na=m_w_branch_na, w_branch_sw=m_w_branch_sw, w_out=m_w_out,
                 ffn2_norm=m_ffn2_norm, ffn2_w_gate=m_ffn2_w_gate, ffn2_w_up=m_ffn2_w_up, ffn2_w_down=m_ffn2_w_down)
    mom_v = dict(ffn1_norm=v_ffn1_norm, ffn1_w_gate=v_ffn1_w_gate, ffn1_w_up=v_ffn1_w_up, ffn1_w_down=v_ffn1_w_down,
                 mix_norm=v_mix_norm, w_in=v_w_in, b_gate=v_b_gate, na_q_norm=v_na_q_norm, na_k_norm=v_na_k_norm,
                 na_rpb=v_na_rpb, sw_q_norm=v_sw_q_norm, sw_k_norm=v_sw_k_norm, sw_sink=v_sw_sink,
                 t5_rel_table=v_t5_rel_table, w_branch_na=v_w_branch_na, w_branch_sw=v_w_branch_sw, w_out=v_w_out,
                 ffn2_norm=v_ffn2_norm, ffn2_w_gate=v_ffn2_w_gate, ffn2_w_up=v_ffn2_w_up, ffn2_w_down=v_ffn2_w_down)
    order = list(weights)

    depth = ffn1_norm.shape[0]
    s, d = x.shape[1], x.shape[2]
    xs = x[0]
    tr = lambda w: jnp.swapaxes(w, -1, -2)

    merge = lambda t: t.reshape(t.shape[0], N_DEV * t.shape[2], t.shape[3])
    no_dep = jnp.zeros((8, LANES), F32)

    def shards_of(kind, l):
        stack = lambda *ws: jnp.stack(ws).astype(BF16)
        if kind == "ffn1":
            return [stack(tr(ffn1_w_gate[l]), tr(ffn1_w_up[l]), ffn1_w_down[l])]
        if kind == "win":
            return [stack(tr(w_in[l]))]
        return [stack(tr(ffn2_w_gate[l]), tr(ffn2_w_up[l]), ffn2_w_down[l]), stack(w_out[l]),
                stack(tr(w_branch_na[l]), tr(w_branch_sw[l]))]

    shards = {(kind, l): shards_of(kind, l) for l in range(depth) for kind in ("ffn1", "win", "rest")}

    def start(kind, l, after):
        return gather_start(shards[kind, l], after, f"gather_{kind}_{l}")

    def arrive(started, kind, l, after):
        zones = gather_wait(started, after, f"gather_{kind}_{l}_wait")
        return forward_start(zones, no_dep, f"forward_{kind}_{l}")

    def finish(fwd, kind, l, after):
        return [merge(z) for z in forward_wait(fwd, after, f"forward_{kind}_{l}_wait")]

    bd = jnp.asarray(np.kron(np.eye(MXU_TILE // HEAD_DIM), np.full((HEAD_DIM, HEAD_DIM), 1.0 / HEAD_DIM)), BF16)
    bmap = jnp.asarray(_t5_bucket_map())
    tile8 = lambda g: jnp.tile(g, NA_WIDTH // HEAD_DIM).reshape(1, NA_WIDTH)
    tile2 = lambda g: jnp.tile(g, SW_KV_WIDTH // HEAD_DIM).reshape(1, SW_KV_WIDTH)

    st_first, tok = start("ffn1", 0, no_dep)
    t5b = t5_expand(t5_rel_table, bmap, tok, "t5_expand").reshape(SW_STACK, 3 * SW_BLOCK)
    t2_tables = [rpb_expand(_rpb_rows(na_rpb[l]), tok, f"rpb_expand_{l}") for l in range(depth)]
    qk_gains = [(tile8(na_q_norm[l]), tile8(na_k_norm[l]), tile8(sw_q_norm[l]), tile2(sw_k_norm[l]))
                for l in range(depth)]
    early = ([t[0, 0, 0:8, :] for t in t2_tables] + [t[0, 0:8, 0:LANES].astype(F32) for v in shards.values() for t in v]
             + [g[:, 0:LANES] for gs in qk_gains for g in gs])
    fwd, _ = arrive(st_first, "ffn1", 0, functools.reduce(jnp.add, early, t5b[0:8, 0:LANES]))
    st_win, dep = start("win", 0, t5b)
    (first,) = finish(fwd, "ffn1", 0, dep)

    saved = []
    layer_w = {0: dict(wg1=(first, 0), wu1=(first, 1), wd1=(first, 2))}
    cur = xs
    for l in range(depth):
        sv = {}
        lw = layer_w[l]
        sv["x0"] = cur
        cur, sv["xn1"], sv["hg1"], sv["hu1"], sv["act1"] = ffn_forward(
            cur, ffn1_norm[l][None], lw["wg1"], lw["wu1"], lw["wd1"], dep, f"ffn1_{l}")
        sv["x1"] = cur
        fwd, _ = arrive(st_win, "win", l, cur)
        st_rest, tok = start("rest", l, cur)
        (zb,) = finish(fwd, "win", l, tok)
        lw["win"] = (zb, 0)
        sv["gains"] = qk_gains[l]
        sv["hn"], sv["zq"], sv["qa"], sv["ka"], sv["qs"], sv["ks"], sv["gt"] = mix_in(
            cur, mix_norm[l][None], lw["win"], b_gate[l][None], *sv["gains"], bd, f"mix_in_{l}")
        sv["t2"] = t2_tables[l]
        sv["o_na"] = na_fwd(sv["qa"], sv["ka"], sv["zq"], sv["t2"], f"na_fwd_{l}")
        dep = no_dep
        if l + 1 < depth:
            st_ffn1, dep = start("ffn1", l + 1, sv["o_na"])
        sv["o_sw"] = sw_fwd(sv["qs"], sv["ks"], sv["zq"], t5b, sw_sink[l], dep, f"sw_fwd_{l}")
        fwd, tok = arrive(st_rest, "rest", l, sv["o_sw"][0:8, 0:LANES] + sv["o_na"][0:8, 0:LANES])
        za, zc, zd = finish(fwd, "rest", l, tok)
        lw.update(wg2=(za, 0), wu2=(za, 1), wd2=(za, 2), wout=(zc, 0), wna=(zd, 0), wsw=(zd, 1))
        cur, sv["a_na"], sv["a_sw"], sv["merged"] = merge_out(
            cur, sv["o_na"], sv["o_sw"], sv["gt"], lw["wna"], lw["wsw"], lw["wout"], f"merge_out_{l}")
        sv["x2"] = cur
        if l + 1 < depth:
            st_win, dep = start("win", l + 1, cur)
            sv["xn2"], sv["hg2"], sv["hu2"], sv["act2"] = ffn_forward(
                cur, ffn2_norm[l][None], lw["wg2"], lw["wu2"], None, dep, f"ffn2_up_{l}")
            fwd, dep = arrive(st_ffn1, "ffn1", l + 1, sv["act2"])
            cur = ffn_down(cur, sv["act2"], lw["wd2"], dep, f"ffn2_down_{l}")
            (za,) = finish(fwd, "ffn1", l + 1, cur)
            layer_w[l + 1] = dict(wg1=(za, 0), wu1=(za, 1), wd1=(za, 2))
        else:
            dx, loss_acc, sv["xn2"], sv["hg2"], sv["hu2"], sv["act2"] = ffn_forward(
                cur, ffn2_norm[l][None], lw["wg2"], lw["wu2"], lw["wd2"], no_dep, f"ffn2_{l}",
                target=loss_target[0])
        dep = no_dep
        saved.append(sv)

    split = lambda t: t.reshape(N_DEV, t.shape[0] // N_DEV, t.shape[1])
    pending = {}
    last_key = "ffn1_0"
    two_level = {last_key}
    small = {k: [None] * depth for k in SMALL_NAMES if k != "t5_rel_table"}
    dbias_sw = []
    for l in reversed(range(depth)):
        sv = saved[l]
        lw = layer_w[l]
        wg1, wu1, wd1, wg2, wu2, wd2 = (lw[k] for k in ("wg1", "wu1", "wd1", "wg2", "wu2", "wd2"))
        win_t, wout_l, wna_t, wsw_t = lw["win"], lw["wout"], lw["wna"], lw["wsw"]
        blocks = ((2, "x2", "xn2", "hg2", "hu2", "act2", wg2, wu2, wd2, "ffn2_norm", 3),
                  (1, "x0", "xn1", "hg1", "hu1", "act1", wg1, wu1, wd1, "ffn1_norm", 0))

        def ffn_backward(dx, blk):
            tag, xk, xnk, hgk, huk, actk, wg, wu, wd, norm_name, slot = blk
            gains = weights[norm_name]
            dxb, dhg, dhu = ffn_bwd_act(dx, wd, sv[hgk], sv[huk], f"ffn{tag}_bwd_act_{l}")
            gwg, gwu, gwd = tn_matmul([(dhg, sv[xnk], 1.0), (dhu, sv[xnk], 1.0), (sv[actk], dxb, 0.5)],
                                      f"ffn{tag}_dw_{l}")
            key = f"ffn{tag}_{l}"
            blocks_of = [split(gwg), split(gwu), split(gwd)]
            if key in two_level:
                paired, token = pair_start(blocks_of, dxb, f"pair_{key}")
            else:
                pending[key], token = scatter_start([blocks_of], f"scatter_{key}")
            dx, dg = proj_bwd_norm([dhg, dhu], [wg, wu], sv[xk], gains[l][None], dx, token, f"ffn{tag}_bwd_x_{l}")
            token = no_dep
            if key in two_level:
                thru, land = pair_wait(paired, dx, f"pair_{key}_wait")
                pending[key], token = chip_start(pair_sum(thru, land, f"pair_sum_{key}"), dg, f"chips_{key}")
            small[norm_name][l] = dg[0]
            return dx, token

        dx, token = ffn_backward(dx, blocks[0])
        dxb, dzg, da_na, da_sw, do_na, do_sw, dbg = mix_bwd_out(
            dx, sv["gt"], sv["a_na"], sv["a_sw"], wna_t, wsw_t, wout_l, token, f"mix_bwd_out_{l}")
        small["b_gate"][l] = dbg[0]
        gwout, gwna, gwsw = tn_matmul([(sv["merged"], dxb, 1.0), (da_na, sv["o_na"], 1.0), (da_sw, sv["o_sw"], 1.0)],
                                      f"mix_dw_{l}")
        dqa, dka, dva, dt2 = na_bwd(sv["qa"], sv["ka"], sv["zq"], sv["t2"], sv["o_na"], do_na, f"na_bwd_{l}")
        dqs, dks, dvs, dbias, dsink = sw_bwd(sv["qs"], sv["ks"], sv["zq"], t5b, sw_sink[l], sv["o_sw"], do_sw,
                                             f"sw_bwd_{l}")
        dbias_sw.append(dbias.reshape(SW_HEADS, SW_BLOCK, 3 * SW_BLOCK))
        small["sw_sink"][l] = jnp.sum(dsink[:, 0].reshape(SW_HEADS, SW_BLOCK), axis=1)
        small["na_rpb"][l] = _rpb_from_rows(rpb_reduce(dt2, f"rpb_reduce_{l}"))
        dz, dgqa, dgka, dgqs, dgks = qk_norm_bwd(dqa, dka, dva, dqs, dks, dvs, sv["zq"], dzg, *sv["gains"], bd,
                                                 f"qk_norm_bwd_{l}")
        fold = lambda g: jnp.sum(g.reshape(-1, HEAD_DIM), axis=0)
        small["na_q_norm"][l], small["na_k_norm"][l] = fold(dgqa), fold(dgka)
        small["sw_q_norm"][l], small["sw_k_norm"][l] = fold(dgqs), fold(dgks)
        (gwin,) = tn_matmul([(dz, sv["hn"], 1.0)], f"dwin_{l}")
        pending[f"mix_{l}"], token = scatter_start([[split(gwout)], [split(gwna), split(gwsw)], [split(gwin)]],
                                                   f"scatter_mix_{l}")
        dx, dg = proj_bwd_norm([dz], [win_t], sv["x1"], mix_norm[l][None], dx, token, f"mix_bwd_x_{l}")
        small["mix_norm"][l] = dg[0]
        dx, tail = ffn_backward(dx, blocks[1])

    dtab = t5_reduce(dbias_sw, bmap, "t5_reduce")
    small_parts = {k: jnp.stack(v) for k, v in small.items()}
    small_parts["t5_rel_table"] = jnp.transpose(dtab[:, :, 0])

    grads, delta, new_m, new_v = {}, {}, {}, {}
    state = {}
    sharing, token = share_start([small_parts[k] for k in SMALL_NAMES] + [loss_acc], tail, "share_small")
    chain = [token]
    members = {"ffn": lambda t: [(f"ffn{t}_w_gate", 0, 0, True), (f"ffn{t}_w_up", 0, 1, True),
                                 (f"ffn{t}_w_down", 0, 2, False)],
               "mix": lambda t: [("w_out", 0, 0, False), ("w_branch_na", 1, 0, True), ("w_branch_sw", 1, 1, True),
                                 ("w_in", 2, 0, True)]}

    def collect(key):
        if key in two_level:
            zones = [chip_wait(pending[key], chain[0], f"wait_{key}")]
        else:
            zones = scatter_wait(pending[key], chain[0], f"wait_{key}")
        kind, l = key.split("_")
        group = members[kind[:3]](kind[3:])
        complete = all(f"{kind}_{j}" in done for j in range(depth) if j != int(l))
        for zi, zone in enumerate(zones):
            mine = sorted((wi, k, transposed) for k, z, wi, transposed in group if z == zi)
            views = [tr if transposed else (lambda t: t) for _, _, transposed in mine]
            items = [(view(weights[k]), view(mom_m[k]), view(mom_v[k]), state.get(k))
                     for (_, k, _), view in zip(mine, views)]
            results = adamw_layer(zone, int(l), items, chain[0], f"adamw_{key}_{zi}")
            chain[0] = results[-1][1]
            for (_, k, _), view, res in zip(mine, views, results):
                state[k] = res
                if complete:
                    grads[k], delta[k], new_m[k], new_v[k] = (view(t) for t in res)
        done.add(key)

    done = set()
    for key in pending:
        if key != last_key:
            collect(key)
    collect(last_key)
    *recvs, all_losses = share_wait(sharing, chain[0], "share_small_wait")
    loss = jnp.sum(all_losses) * (0.5 / d)
    results = adamw_small([weights[k] for k in SMALL_NAMES], recvs, [mom_m[k] for k in SMALL_NAMES],
                          [mom_v[k] for k in SMALL_NAMES], "adamw_small")
    for dst, outs in zip((grads, delta, new_m, new_v), results):
        dst.update(dict(zip(SMALL_NAMES, outs)))

    return (loss, dx[None], *[grads[k] for k in order], *[delta[k] for k in order],
            *[new_m[k] for k in order], *[new_v[k] for k in order])
```

```python
import functools
import math

import numpy as np
import jax
import jax.numpy as jnp
from jax import lax
from jax.experimental import pallas as pl
from jax.experimental.pallas import tpu as pltpu

F32 = jnp.float32
BF16 = jnp.bfloat16
MESH = pl.DeviceIdType.MESH

N_DEV = 8
EPS = 1e-6
NEG = -1e30
HEAD_DIM = 64
GRID_W = 64
NA_ROWS = 8
NA_COLS = 16
NA_WIDTH = 512
SW_Q_WIDTH = 512
SW_KV_WIDTH = 128
SW_BLOCK = 128
SW_HEADS = 8
SW_REP = 4
REL_BUCKETS = 32
REL_MAX_DIST = 128
QKV_WIDTH = 3 * NA_WIDTH + SW_Q_WIDTH + 2 * SW_KV_WIDTH
SCALE = 1.0 / math.sqrt(HEAD_DIM)

ADAM_LR = 0.001
ADAM_B1 = 0.9
ADAM_B2 = 0.999
ADAM_EPS = 1e-08
ADAM_WD = 0.01
ADAM_STEP = 10

V7X_VMEM_LIMIT = 56 * 1024 * 1024
LANES = 128
MXU_TILE = 256

NT = (((1,), (1,)), ((), ()))
TN = (((0,), (0,)), ((), ()))


def _params(n_grid=1):
    return pltpu.CompilerParams(dimension_semantics=("arbitrary",) * n_grid,
                                vmem_limit_bytes=V7X_VMEM_LIMIT)


def _row_tile(s):
    for t in (512, 256, 128, 64, 32, 16, 8):
        if s % t == 0:
            return t
    raise ValueError(s)


def _tn_tile(n):
    best = max(t for t in range(LANES, min(n, 2304) + 1, LANES) if n % t == 0) if n % LANES == 0 else n
    return best // 2 if best == n and n >= 1024 else best


ONCE = pl.Buffered(1)


def _col_chunk(n):
    return MXU_TILE if n % MXU_TILE == 0 else n


def _dot(a, b):
    return jnp.dot(a, b, preferred_element_type=F32)


def _dotg(a, b, dn):
    return lax.dot_general(a, b, dn, preferred_element_type=F32)


def _sigmoid(v):
    return 1.0 / (1.0 + jnp.exp(-v))


def _rstd(xv):
    return lax.rsqrt(jnp.mean(xv * xv, axis=-1, keepdims=True) + EPS)


def _full(shape):
    nd = len(shape)
    return pl.BlockSpec(shape, lambda i, _n=nd: (0,) * _n)


def _rows(tm, width):
    return pl.BlockSpec((tm, width), lambda i: (i, 0))


def _mat(stack, idx):
    return pl.BlockSpec((None,) + tuple(stack.shape[1:]), lambda i, _w=idx: (_w, 0, 0), pipeline_mode=ONCE)


def _group_mean(v, bd):
    w = bd.shape[0]
    if v.shape[1] > w:
        return jnp.concatenate([_group_mean(v[:, c0:c0 + w], bd) for c0 in range(0, v.shape[1], w)], axis=1)
    hi = v.astype(BF16)
    lo = (v - hi.astype(F32)).astype(BF16)
    return _dot(hi, bd) + _dot(lo, bd)


def _loss_tile(y, t_ref, dy_ref, acc_ref):
    tm, d = y.shape

    @pl.when(pl.program_id(0) == 0)
    def _():
        acc_ref[...] = jnp.zeros(acc_ref.shape, F32)

    err = y - t_ref[...]
    dy_ref[...] = err * (1.0 / d)
    part = jnp.sum((err * err).reshape(tm // 8, 8, d), axis=0)
    acc = part[:, 0:LANES]
    for c0 in range(LANES, d, LANES):
        acc = acc + part[:, c0:c0 + LANES]
    acc_ref[...] = acc_ref[...] + acc


def ffn_forward(x, gain, wg_t, wu_t, wd, dep, name, target=None):
    s, d = x.shape
    f = wg_t[0].shape[1]
    tm = _row_tile(s) if wd is None else min(_row_tile(s), 256)
    fc = _col_chunk(f)
    nw = 2 if wd is None else 3
    n_in = nw + (1 if target is None else 2)

    def body(x_ref, g_ref, *refs):
        w_refs, outs = refs[:nw], refs[n_in:]
        xn_ref, dg_ref, du_ref, act_ref = outs[-4:]
        xv = x_ref[...]
        xn = (xv * _rstd(xv) * g_ref[...]).astype(BF16)
        xn_ref[...] = xn
        for c0 in range(0, f, fc):
            hg = _dotg(xn, w_refs[0][c0:c0 + fc, :], NT)
            hu = _dotg(xn, w_refs[1][c0:c0 + fc, :], NT)
            sg = _sigmoid(hg)
            silu = hg * sg
            du_ref[:, c0:c0 + fc] = silu.astype(BF16)
            dg_ref[:, c0:c0 + fc] = (hu * (sg + silu * (1.0 - sg))).astype(BF16)
            act_ref[:, c0:c0 + fc] = (silu * hu).astype(BF16)
        if wd is not None:
            y = xv + 0.5 * _dot(act_ref[...], w_refs[2][...])
            if target is None:
                outs[0][...] = y
            else:
                _loss_tile(y, refs[nw + 1], outs[0], outs[1])

    weights = [wg_t, wu_t] + ([] if wd is None else [wd])
    in_specs = [_rows(tm, d), _full((1, d))] + [_mat(*w) for w in weights] + [_full(dep.shape)]
    operands = [x, gain, *[w[0] for w in weights], dep]
    out_specs = [_rows(tm, d), _rows(tm, f), _rows(tm, f), _rows(tm, f)]
    out_shape = [jax.ShapeDtypeStruct((s, d), BF16)] + [jax.ShapeDtypeStruct((s, f), BF16)] * 3
    if wd is not None:
        out_specs, out_shape = [_rows(tm, d)] + out_specs, [jax.ShapeDtypeStruct((s, d), F32)] + out_shape
    if target is not None:
        in_specs, operands = in_specs + [_rows(tm, d)], operands + [target]
        out_specs = out_specs[:1] + [_full((8, LANES))] + out_specs[1:]
        out_shape = out_shape[:1] + [jax.ShapeDtypeStruct((8, LANES), F32)] + out_shape[1:]
    return pl.pallas_call(
        body, name=name, grid=(s // tm,), in_specs=in_specs, out_specs=out_specs, out_shape=out_shape,
        compiler_params=_params(),
    )(*operands)


def ffn_down(x, act, wd, dep, name):
    s, d = x.shape
    f = act.shape[1]
    tm = _row_tile(s)

    def body(x_ref, a_ref, w_ref, dep_ref, o_ref):
        o_ref[...] = x_ref[...] + 0.5 * _dot(a_ref[...], w_ref[...])

    return pl.pallas_call(
        body, name=name, grid=(s // tm,),
        in_specs=[_rows(tm, d), _rows(tm, f), _mat(*wd), _full(dep.shape)],
        out_specs=_rows(tm, d),
        out_shape=jax.ShapeDtypeStruct((s, d), F32),
        compiler_params=_params(),
    )(x, act, wd[0], dep)


def mix_in(x, gain, win_t, b_gate, gq_na, gk_na, gq_sw, gk_sw, bd, name):
    s, d = x.shape
    tm = _row_tile(s)
    gc = _col_chunk(2 * d)

    def body(x_ref, g_ref, w_ref, b_ref, gqa_ref, gka_ref, gqs_ref, gks_ref, bd_ref,
             hn_ref, zq_ref, qa_ref, ka_ref, qs_ref, ks_ref, gt_ref):
        xv = x_ref[...]
        hn = (xv * _rstd(xv) * g_ref[...]).astype(BF16)
        hn_ref[...] = hn

        def proj(c0, c1):
            return _dotg(hn, w_ref[c0:c1, :], NT)

        def headnorm(z, g, bdm):
            return z * lax.rsqrt(_group_mean(z * z, bdm) + EPS) * g

        bd512 = bd_ref[...]
        bd128 = bd_ref[0:SW_KV_WIDTH, 0:SW_KV_WIDTH]
        z = proj(0, 512)
        zq_ref[:, 0:512] = z.astype(BF16)
        qa_ref[...] = (headnorm(z, gqa_ref[...], bd512) * SCALE).astype(BF16)
        z = proj(512, 1024)
        zq_ref[:, 512:1024] = z.astype(BF16)
        ka_ref[...] = headnorm(z, gka_ref[...], bd512).astype(BF16)
        z = proj(1024, 1536)
        zq_ref[:, 1024:1536] = z.astype(BF16)
        z = proj(1536, 2048)
        zq_ref[:, 1536:2048] = z.astype(BF16)
        qs_ref[...] = (headnorm(z, gqs_ref[...], bd512) * SCALE).astype(BF16)
        z = proj(2048, 2176)
        zq_ref[:, 2048:2176] = z.astype(BF16)
        ks_ref[...] = headnorm(z, gks_ref[...], bd128).astype(BF16)
        z = proj(2176, 2304)
        zq_ref[:, 2176:2304] = z.astype(BF16)
        for c0 in range(0, 2 * d, gc):
            zg = proj(QKV_WIDTH + c0, QKV_WIDTH + c0 + gc) + b_ref[:, c0:c0 + gc]
            gt_ref[:, c0:c0 + gc] = _sigmoid(zg).astype(BF16)

    return pl.pallas_call(
        body, name=name, grid=(s // tm,),
        in_specs=[_rows(tm, d), _full((1, d)), _mat(*win_t), _full((1, 2 * d)),
                  _full((1, 512)), _full((1, 512)), _full((1, 512)), _full((1, 128)), _full((MXU_TILE, MXU_TILE))],
        out_specs=[_rows(tm, d), _rows(tm, QKV_WIDTH), _rows(tm, 512), _rows(tm, 512), _rows(tm, 512),
                   _rows(tm, 128), _rows(tm, 2 * d)],
        out_shape=[jax.ShapeDtypeStruct((s, d), BF16), jax.ShapeDtypeStruct((s, QKV_WIDTH), BF16),
                   jax.ShapeDtypeStruct((s, 512), BF16), jax.ShapeDtypeStruct((s, 512), BF16),
                   jax.ShapeDtypeStruct((s, 512), BF16), jax.ShapeDtypeStruct((s, 128), BF16),
                   jax.ShapeDtypeStruct((s, 2 * d), BF16)],
        compiler_params=_params(),
    )(x, gain, win_t[0], b_gate, gq_na, gk_na, gq_sw, gk_sw, bd)


def _na_iotas():
    qc = lax.broadcasted_iota(jnp.int32, (GRID_W, LANES), 0)
    ln = lax.broadcasted_iota(jnp.int32, (GRID_W, LANES), 1)
    low = ln < GRID_W
    kc = jnp.where(low, ln, ln - GRID_W)
    diff = kc - qc + (NA_COLS - 1)
    qcs = jnp.clip(qc - NA_COLS // 2, 0, GRID_W - NA_COLS)
    inwin = (kc >= qcs) & (kc < qcs + NA_COLS)
    return diff, low, inwin


NA_RI = 2 * NA_ROWS - 1
NA_CI = 2 * NA_COLS - 1
NA_T2 = NA_RI + 1


def _rpb_rows(rpb):
    h = rpb.shape[0]
    padded = jnp.pad(rpb, ((0, 0), (1, 1), (0, GRID_W - NA_CI)))
    return jnp.concatenate([padded[:, :NA_T2], padded[:, 1:NA_T2 + 1]], axis=2).reshape(h, NA_T2, LANES)


def _rpb_from_rows(rows):
    return rows[:, 1:, :NA_CI] + rows[:, :NA_RI, GRID_W:GRID_W + NA_CI]


def rpb_expand(rows, dep, name):
    n_heads = rows.shape[0]

    def body(r_ref, dep_ref, o_ref):
        for h in range(n_heads):
            for e in range(NA_T2):
                line = jnp.broadcast_to(r_ref[h, e:e + 1, :], (GRID_W, LANES))
                o_ref[h, e] = pltpu.roll(line, LANES - (NA_COLS - 1), 1, stride=1, stride_axis=0)

    return pl.pallas_call(
        body, name=name,
        in_specs=[pl.BlockSpec(memory_space=pltpu.VMEM), pl.BlockSpec(memory_space=pltpu.VMEM)],
        out_specs=pl.BlockSpec(memory_space=pltpu.VMEM),
        out_shape=jax.ShapeDtypeStruct((n_heads, NA_T2, GRID_W, LANES), F32),
        compiler_params=pltpu.CompilerParams(vmem_limit_bytes=V7X_VMEM_LIMIT),
    )(rows, dep)


def rpb_reduce(dt2, name):
    n_heads = dt2.shape[0]
    flip = jnp.asarray(np.eye(GRID_W)[::-1], BF16)

    def body(d_ref, j_ref, o_ref):
        jm = j_ref[...]
        for h in range(n_heads):
            for e in range(NA_T2):
                dv = d_ref[h, e]
                hi = dv.astype(BF16)
                mid = (dv - hi.astype(F32)).astype(BF16)
                lo = (dv - hi.astype(F32) - mid.astype(F32)).astype(BF16)
                rev = _dot(jm, hi) + _dot(jm, mid) + _dot(jm, lo)
                back = pltpu.roll(rev, LANES + (NA_COLS - 1) - (GRID_W - 1), 1, stride=1, stride_axis=0)
                o_ref[h, e:e + 1, :] = jnp.sum(back, axis=0, keepdims=True)

    return pl.pallas_call(
        body, name=name,
        in_specs=[pl.BlockSpec(memory_space=pltpu.VMEM)] * 2,
        out_specs=pl.BlockSpec(memory_space=pltpu.VMEM),
        out_shape=jax.ShapeDtypeStruct((n_heads, NA_T2, LANES), F32),
        compiler_params=pltpu.CompilerParams(vmem_limit_bytes=V7X_VMEM_LIMIT),
    )(dt2, flip)


NA_TQ = 4
NA_TK = NA_TQ + NA_ROWS
NA_KCH = NA_TK // 2


def _na_tile_geometry(t, rows):
    r = t * NA_TQ
    kbase = jnp.clip(r - NA_ROWS // 2, 0, rows - NA_TK)
    starts = [jnp.clip(r + a - NA_ROWS // 2, 0, rows - NA_ROWS) for a in range(NA_TQ)]
    return r, kbase, starts


def _na_tile_mask(kbase, starts, low, inwin):
    half = jnp.where(low, 0, 1)
    cols = []
    for c in range(NA_KCH):
        krow = kbase + 2 * c + half
        cols.append(jnp.concatenate(
            [jnp.where(inwin & (krow >= st) & (krow < st + NA_ROWS), 0.0, NEG) for st in starts], axis=0))
    return jnp.concatenate(cols, axis=1)


def _na_tile_index(r, kbase, a, c):
    return jnp.clip(kbase + 2 * c - (r + a) + NA_ROWS, 0, NA_T2 - 1)


def _na_tile_scores(q, k, t2_ref, hh, r, kbase, madd):
    bias = jnp.concatenate(
        [jnp.concatenate([t2_ref[hh, _na_tile_index(r, kbase, a, c)] for a in range(NA_TQ)], axis=0)
         for c in range(NA_KCH)], axis=1)
    return _dotg(q, k, NT) + bias + madd


def _softmax_rows(sc):
    e = jnp.exp(sc - jnp.max(sc, axis=1, keepdims=True))
    return e * (1.0 / jnp.sum(e, axis=1, keepdims=True))


def na_fwd(qa, ka, zq, t2, name):
    s = qa.shape[0]
    rows = s // GRID_W
    n_pairs = NA_WIDTH // LANES
    v_blk0 = (2 * NA_WIDTH) // LANES

    assert rows % NA_TQ == 0 and rows >= NA_TK
    tq, tk = NA_TQ * GRID_W, NA_TK * GRID_W

    def body(q_ref, k_ref, v_ref, t2_ref, o_ref, s_scr, p_scr):
        _, low, inwin = _na_iotas()

        def tile(t, carry):
            r, kbase, starts = _na_tile_geometry(t, rows)
            madd = _na_tile_mask(kbase, starts, low, inwin)
            qr = pl.ds(pl.multiple_of(r * GRID_W, tq), tq)
            kr = pl.ds(pl.multiple_of(kbase * GRID_W, tq), tk)
            for hh in range(2):
                lanes = slice(HEAD_DIM * hh, HEAD_DIM * (hh + 1))
                s_scr[tq * hh:tq * (hh + 1), :] = _na_tile_scores(q_ref[qr, lanes], k_ref[kr, lanes], t2_ref, hh, r,
                                                                  kbase, madd)
            p_scr[...] = _softmax_rows(s_scr[...]).astype(BF16)
            for hh in range(2):
                lanes = slice(HEAD_DIM * hh, HEAD_DIM * (hh + 1))
                o_ref[qr, lanes] = _dot(p_scr[tq * hh:tq * (hh + 1), :], v_ref[kr, lanes]).astype(BF16)
            return carry

        lax.fori_loop(0, rows // NA_TQ, tile, 0, unroll=4)

    col = lambda off: pl.BlockSpec((s, LANES), lambda p, _o=off: (0, _o + p))
    return pl.pallas_call(
        body, name=name, grid=(n_pairs,),
        in_specs=[col(0), col(0), col(v_blk0),
                  pl.BlockSpec((2, NA_T2, GRID_W, LANES), lambda p: (p, 0, 0, 0))],
        out_specs=col(0),
        out_shape=jax.ShapeDtypeStruct((s, NA_WIDTH), BF16),
        scratch_shapes=[pltpu.VMEM((2 * tq, tk), F32), pltpu.VMEM((2 * tq, tk), BF16)],
        compiler_params=_params(),
    )(qa, ka, zq, t2)


def na_bwd(qa, ka, zq, t2, o_na, do_na, name):
    s = qa.shape[0]
    rows = s // GRID_W
    n_pairs = NA_WIDTH // LANES
    v_blk0 = (2 * NA_WIDTH) // LANES

    tq, tk = NA_TQ * GRID_W, NA_TK * GRID_W

    def body(q_ref, k_ref, v_ref, t2_ref, o_ref, do_ref, dq_ref, dk_ref, dv_ref, dt2_ref):
        _, low, inwin = _na_iotas()
        dk_ref[...] = jnp.zeros(dk_ref.shape, F32)
        dv_ref[...] = jnp.zeros(dv_ref.shape, F32)
        dt2_ref[...] = jnp.zeros(dt2_ref.shape, F32)

        def tile(t, carry):
            r, kbase, starts = _na_tile_geometry(t, rows)
            madd = _na_tile_mask(kbase, starts, low, inwin)
            qr = pl.ds(pl.multiple_of(r * GRID_W, tq), tq)
            kr = pl.ds(pl.multiple_of(kbase * GRID_W, tq), tk)
            for hh in range(2):
                lanes = slice(HEAD_DIM * hh, HEAD_DIM * (hh + 1))
                q, k, v = q_ref[qr, lanes], k_ref[kr, lanes], v_ref[kr, lanes]
                p = _softmax_rows(_na_tile_scores(q, k, t2_ref, hh, r, kbase, madd))
                do = do_ref[qr, lanes]
                delta = jnp.sum(do.astype(F32) * o_ref[qr, lanes].astype(F32), axis=1, keepdims=True)
                ds = p * (_dotg(do, v, NT) - delta)
                shared = {}
                for a in range(NA_TQ):
                    for c in range(NA_KCH):
                        shared.setdefault(2 * c - a, []).append(
                            ds[GRID_W * a:GRID_W * (a + 1), LANES * c:LANES * (c + 1)])
                for offset, parts in shared.items():
                    e = jnp.clip(offset + kbase - r + NA_ROWS, 0, NA_T2 - 1)
                    dt2_ref[hh, e] = dt2_ref[hh, e] + functools.reduce(jnp.add, parts)
                dsb = ds.astype(BF16)
                dq_ref[qr, lanes] = _dot(dsb, k)
                dk_ref[kr, lanes] = dk_ref[kr, lanes] + _dotg(dsb, q, TN)
                dv_ref[kr, lanes] = dv_ref[kr, lanes] + _dotg(p.astype(BF16), do, TN)
            return carry

        lax.fori_loop(0, rows // NA_TQ, tile, 0, unroll=4)

    col = lambda off: pl.BlockSpec((s, LANES), lambda p, _o=off: (0, _o + p))
    t2spec = pl.BlockSpec((2, NA_T2, GRID_W, LANES), lambda p: (p, 0, 0, 0))
    return pl.pallas_call(
        body, name=name, grid=(n_pairs,),
        in_specs=[col(0), col(0), col(v_blk0), t2spec, col(0), col(0)],
        out_specs=[col(0), col(0), col(0), t2spec],
        out_shape=[jax.ShapeDtypeStruct((s, NA_WIDTH), F32)] * 3 + [jax.ShapeDtypeStruct(t2.shape, F32)],
        compiler_params=_params(),
    )(qa, ka, zq, t2, o_na, do_na)


def _t5_bucket_map():
    rel = np.arange(3 * SW_BLOCK)[None, :] - SW_BLOCK - np.arange(SW_BLOCK)[:, None]
    nb = REL_BUCKETS // 2
    max_exact = nb // 2
    n = np.abs(rel)
    large = max_exact + (np.log(np.maximum(n, 1) / max_exact)
                         / np.log(REL_MAX_DIST / max_exact) * (nb - max_exact)).astype(np.int32)
    large = np.minimum(large, nb - 1)
    return ((rel > 0) * nb + np.where(n < max_exact, n, large)).astype(np.int32)


def t5_expand(table, bmap, dep, name):
    def body(tab_ref, bm_ref, dep_ref, o_ref):
        bm = bm_ref[...]
        for h in range(SW_HEADS):
            t = jnp.zeros(bm.shape, F32)
            for b in range(REL_BUCKETS):
                t = jnp.where(bm == b, tab_ref[b, h], t)
            o_ref[h] = t

    return pl.pallas_call(
        body, name=name,
        in_specs=[pl.BlockSpec(memory_space=pltpu.SMEM), pl.BlockSpec(memory_space=pltpu.VMEM),
                  pl.BlockSpec(memory_space=pltpu.VMEM)],
        out_specs=pl.BlockSpec(memory_space=pltpu.VMEM),
        out_shape=jax.ShapeDtypeStruct((SW_HEADS,) + bmap.shape, F32),
        compiler_params=pltpu.CompilerParams(vmem_limit_bytes=V7X_VMEM_LIMIT),
    )(table, bmap, dep)


def t5_reduce(dbias_list, bmap, name):
    n = len(dbias_list)

    def body(*refs):
        d_refs, bm_ref, o_ref = refs[:n], refs[n], refs[n + 1]
        bm = bm_ref[...]
        for h in range(SW_HEADS):
            dv = d_refs[0][h]
            for other in d_refs[1:]:
                dv = dv + other[h]
            rows = [jnp.sum(jnp.where(bm == b, dv, 0.0), axis=0, keepdims=True) for b in range(REL_BUCKETS)]
            r = jnp.concatenate(rows, axis=0)
            o_ref[h] = jnp.broadcast_to(jnp.sum(r, axis=1, keepdims=True), (REL_BUCKETS, LANES))

    return pl.pallas_call(
        body, name=name,
        in_specs=[pl.BlockSpec(memory_space=pltpu.VMEM)] * (n + 1),
        out_specs=pl.BlockSpec(memory_space=pltpu.VMEM),
        out_shape=jax.ShapeDtypeStruct((SW_HEADS, REL_BUCKETS, LANES), F32),
        compiler_params=pltpu.CompilerParams(vmem_limit_bytes=V7X_VMEM_LIMIT),
    )(*dbias_list, bmap)


def _sw_mask_iotas():
    a = lax.broadcasted_iota(jnp.int32, (SW_BLOCK, 3 * SW_BLOCK), 0)
    j = lax.broadcasted_iota(jnp.int32, (SW_BLOCK, 3 * SW_BLOCK), 1)
    inwin = jnp.abs(j - SW_BLOCK - a) <= SW_BLOCK
    return j, inwin


SW_STACK = SW_HEADS * SW_BLOCK


def _sw_softmax(sc, sk):
    m = jnp.maximum(jnp.max(sc, axis=1, keepdims=True), sk)
    e = jnp.exp(sc - m)
    es = jnp.exp(sk - m)
    inv = 1.0 / (jnp.sum(e, axis=1, keepdims=True) + es)
    return e * inv, es * inv


def _sw_prologue(k_ref, v_ref, kp, vp, sink_ref, s):
    pad = s + 2 * SW_BLOCK
    zeros = jnp.zeros((SW_BLOCK, SW_KV_WIDTH), BF16)
    kp[0:SW_BLOCK, :] = zeros
    vp[0:SW_BLOCK, :] = zeros
    kp[SW_BLOCK + s:pad, :] = zeros
    vp[SW_BLOCK + s:pad, :] = zeros
    kp[SW_BLOCK:SW_BLOCK + s, :] = k_ref[...]
    vp[SW_BLOCK:SW_BLOCK + s, :] = v_ref[...]
    return jnp.concatenate([jnp.full((SW_BLOCK, 1), sink_ref[h], F32) for h in range(SW_HEADS)], axis=0)


def sw_fwd(qs, ks, zq, t5b, sink, dep, name):
    s = qs.shape[0]
    nb = s // SW_BLOCK
    v_blk = (3 * NA_WIDTH + SW_Q_WIDTH + SW_KV_WIDTH) // LANES
    pad = s + 2 * SW_BLOCK

    def body(q_ref, k_ref, v_ref, b_ref, sink_ref, dep_ref, o_ref, kp, vp, s_scr, p_scr):
        sink_col = _sw_prologue(k_ref, v_ref, kp, vp, sink_ref, s)
        j, inwin = _sw_mask_iotas()

        def blk(n, carry):
            kpos = n * SW_BLOCK - SW_BLOCK + j
            madd = jnp.where(inwin & (kpos >= 0) & (kpos < s), 0.0, NEG)
            q0 = pl.multiple_of(n * SW_BLOCK, SW_BLOCK)
            qr, kr = pl.ds(q0, SW_BLOCK), pl.ds(q0, 3 * SW_BLOCK)
            for h in range(SW_HEADS):
                g = h // SW_REP
                s_scr[SW_BLOCK * h:SW_BLOCK * (h + 1), :] = _dotg(
                    q_ref[qr, HEAD_DIM * h:HEAD_DIM * (h + 1)], kp[kr, HEAD_DIM * g:HEAD_DIM * (g + 1)], NT) + madd
            p, _ = _sw_softmax(s_scr[...] + b_ref[...], sink_col)
            p_scr[...] = p.astype(BF16)
            for h in range(SW_HEADS):
                g = h // SW_REP
                o_ref[qr, HEAD_DIM * h:HEAD_DIM * (h + 1)] = _dot(
                    p_scr[SW_BLOCK * h:SW_BLOCK * (h + 1), :], vp[kr, HEAD_DIM * g:HEAD_DIM * (g + 1)]).astype(BF16)
            return carry

        lax.fori_loop(0, nb, blk, 0, unroll=4)

    return pl.pallas_call(
        body, name=name, grid=(1,),
        in_specs=[_full((s, SW_Q_WIDTH)), _full((s, SW_KV_WIDTH)),
                  pl.BlockSpec((s, SW_KV_WIDTH), lambda i: (0, v_blk)),
                  _full((SW_STACK, 3 * SW_BLOCK)), pl.BlockSpec(memory_space=pltpu.SMEM),
                  _full(dep.shape)],
        out_specs=_full((s, SW_Q_WIDTH)),
        out_shape=jax.ShapeDtypeStruct((s, SW_Q_WIDTH), BF16),
        scratch_shapes=[pltpu.VMEM((pad, SW_KV_WIDTH), BF16), pltpu.VMEM((pad, SW_KV_WIDTH), BF16),
                        pltpu.VMEM((SW_STACK, 3 * SW_BLOCK), F32), pltpu.VMEM((SW_STACK, 3 * SW_BLOCK), BF16)],
        compiler_params=_params(),
    )(qs, ks, zq, t5b, sink, dep)


def sw_bwd(qs, ks, zq, t5b, sink, o_sw, do_sw, name):
    s = qs.shape[0]
    nb = s // SW_BLOCK
    v_blk = (3 * NA_WIDTH + SW_Q_WIDTH + SW_KV_WIDTH) // LANES
    pad = s + 2 * SW_BLOCK

    def body(q_ref, k_ref, v_ref, b_ref, sink_ref, o_ref, do_ref,
             dq_ref, dk_ref, dv_ref, db_ref, dsk_ref, kp, vp, dkp, dvp, s_scr, dp_scr, ds_scr, p_scr):
        sink_col = _sw_prologue(k_ref, v_ref, kp, vp, sink_ref, s)
        dkp[...] = jnp.zeros(dkp.shape, F32)
        dvp[...] = jnp.zeros(dvp.shape, F32)
        db_ref[...] = jnp.zeros(db_ref.shape, F32)
        dsk_ref[...] = jnp.zeros(dsk_ref.shape, F32)
        j, inwin = _sw_mask_iotas()

        def blk(n, carry):
            kpos = n * SW_BLOCK - SW_BLOCK + j
            madd = jnp.where(inwin & (kpos >= 0) & (kpos < s), 0.0, NEG)
            q0 = pl.multiple_of(n * SW_BLOCK, SW_BLOCK)
            qr, kr = pl.ds(q0, SW_BLOCK), pl.ds(q0, 3 * SW_BLOCK)
            deltas = []
            for h in range(SW_HEADS):
                g = h // SW_REP
                hl, kl = slice(HEAD_DIM * h, HEAD_DIM * (h + 1)), slice(HEAD_DIM * g, HEAD_DIM * (g + 1))
                rows = slice(SW_BLOCK * h, SW_BLOCK * (h + 1))
                do = do_ref[qr, hl]
                s_scr[rows, :] = _dotg(q_ref[qr, hl], kp[kr, kl], NT) + madd
                dp_scr[rows, :] = _dotg(do, vp[kr, kl], NT)
                deltas.append(jnp.sum(do.astype(F32) * o_ref[qr, hl].astype(F32), axis=1, keepdims=True))
            delta = jnp.concatenate(deltas, axis=0)
            p, ps = _sw_softmax(s_scr[...] + b_ref[...], sink_col)
            ds = p * (dp_scr[...] - delta)
            db_ref[...] = db_ref[...] + ds
            dsk_ref[...] = dsk_ref[...] - jnp.broadcast_to(ps * delta, (SW_STACK, LANES))
            ds_scr[...] = ds.astype(BF16)
            p_scr[...] = p.astype(BF16)
            for g in range(SW_HEADS // SW_REP):
                kl = slice(HEAD_DIM * g, HEAD_DIM * (g + 1))
                k = kp[kr, kl]
                dkw = jnp.zeros((3 * SW_BLOCK, HEAD_DIM), F32)
                dvw = jnp.zeros((3 * SW_BLOCK, HEAD_DIM), F32)
                for r in range(SW_REP):
                    h = g * SW_REP + r
                    hl, rows = slice(HEAD_DIM * h, HEAD_DIM * (h + 1)), slice(SW_BLOCK * h, SW_BLOCK * (h + 1))
                    dsb = ds_scr[rows, :]
                    dq_ref[qr, hl] = _dot(dsb, k)
                    dkw = dkw + _dotg(dsb, q_ref[qr, hl], TN)
                    dvw = dvw + _dotg(p_scr[rows, :], do_ref[qr, hl], TN)
                dkp[kr, kl] = dkp[kr, kl] + dkw
                dvp[kr, kl] = dvp[kr, kl] + dvw
            return carry

        lax.fori_loop(0, nb, blk, 0)
        dk_ref[...] = dkp[SW_BLOCK:SW_BLOCK + s, :]
        dv_ref[...] = dvp[SW_BLOCK:SW_BLOCK + s, :]

    bias_spec = _full((SW_STACK, 3 * SW_BLOCK))
    return pl.pallas_call(
        body, name=name, grid=(1,),
        in_specs=[_full((s, SW_Q_WIDTH)), _full((s, SW_KV_WIDTH)),
                  pl.BlockSpec((s, SW_KV_WIDTH), lambda i: (0, v_blk)),
                  bias_spec, pl.BlockSpec(memory_space=pltpu.SMEM),
                  _full((s, SW_Q_WIDTH)), _full((s, SW_Q_WIDTH))],
        out_specs=[_full((s, SW_Q_WIDTH)), _full((s, SW_KV_WIDTH)), _full((s, SW_KV_WIDTH)), bias_spec,
                   _full((SW_STACK, LANES))],
        out_shape=[jax.ShapeDtypeStruct((s, SW_Q_WIDTH), F32), jax.ShapeDtypeStruct((s, SW_KV_WIDTH), F32),
                   jax.ShapeDtypeStruct((s, SW_KV_WIDTH), F32),
                   jax.ShapeDtypeStruct((SW_STACK, 3 * SW_BLOCK), F32),
                   jax.ShapeDtypeStruct((SW_STACK, LANES), F32)],
        scratch_shapes=[pltpu.VMEM((pad, SW_KV_WIDTH), BF16), pltpu.VMEM((pad, SW_KV_WIDTH), BF16),
                        pltpu.VMEM((pad, SW_KV_WIDTH), F32), pltpu.VMEM((pad, SW_KV_WIDTH), F32),
                        pltpu.VMEM((SW_STACK, 3 * SW_BLOCK), F32), pltpu.VMEM((SW_STACK, 3 * SW_BLOCK), F32),
                        pltpu.VMEM((SW_STACK, 3 * SW_BLOCK), BF16), pltpu.VMEM((SW_STACK, 3 * SW_BLOCK), BF16)],
        compiler_params=_params(),
    )(qs, ks, zq, t5b, sink, o_sw, do_sw)


def merge_out(x, o_na, o_sw, gt, wbna_t, wbsw_t, wout, name):
    s, d = x.shape
    tm = _row_tile(s)

    def body(x_ref, ona_ref, osw_ref, gt_ref, wna_ref, wsw_ref, wo_ref, xo_ref, ana_ref, asw_ref, mg_ref):
        a_na = _dotg(ona_ref[...], wna_ref[...], NT)
        a_sw = _dotg(osw_ref[...], wsw_ref[...], NT)
        g_na, g_sw = gt_ref[:, 0:d].astype(F32), gt_ref[:, d:2 * d].astype(F32)
        ana_ref[...] = (a_na * g_na * (1.0 - g_na)).astype(BF16)
        asw_ref[...] = (a_sw * g_sw * (1.0 - g_sw)).astype(BF16)
        merged = (g_na * a_na + g_sw * a_sw).astype(BF16)
        mg_ref[...] = merged
        xo_ref[...] = x_ref[...] + _dot(merged, wo_ref[...])

    return pl.pallas_call(
        body, name=name, grid=(s // tm,),
        in_specs=[_rows(tm, d), _rows(tm, 512), _rows(tm, 512), _rows(tm, 2 * d),
                  _mat(*wbna_t), _mat(*wbsw_t), _mat(*wout)],
        out_specs=[_rows(tm, d)] * 4,
        out_shape=[jax.ShapeDtypeStruct((s, d), F32)] + [jax.ShapeDtypeStruct((s, d), BF16)] * 3,
        compiler_params=_params(),
    )(x, o_na, o_sw, gt, wbna_t[0], wbsw_t[0], wout[0])


def mix_bwd_out(dx, gt, a_na, a_sw, wbna_t, wbsw_t, wout, dep, name):
    s, d = dx.shape
    tm = _row_tile(s)

    def body(dx_ref, gt_ref, ana_ref, asw_ref, wna_ref, wsw_ref, wo_ref, dep_ref,
             dxb_ref, dzg_ref, dana_ref, dasw_ref, dona_ref, dosw_ref, dbg_ref):
        @pl.when(pl.program_id(0) == 0)
        def _():
            dbg_ref[...] = jnp.zeros(dbg_ref.shape, F32)

        dxb = dx_ref[...].astype(BF16)
        dxb_ref[...] = dxb
        dm = _dotg(dxb, wo_ref[...], NT)
        for i, (a_ref, da_ref, w_ref, do_ref) in enumerate(
                [(ana_ref, dana_ref, wna_ref, dona_ref), (asw_ref, dasw_ref, wsw_ref, dosw_ref)]):
            gi = gt_ref[:, i * d:(i + 1) * d].astype(F32)
            da = (dm * gi).astype(BF16)
            da_ref[...] = da
            do_ref[...] = _dot(da, w_ref[...]).astype(BF16)
            dzg = dm * a_ref[...].astype(F32)
            dzg_ref[:, i * d:(i + 1) * d] = dzg.astype(BF16)
            dbg_ref[:, i * d:(i + 1) * d] = dbg_ref[:, i * d:(i + 1) * d] + jnp.sum(dzg, axis=0, keepdims=True)

    return pl.pallas_call(
        body, name=name, grid=(s // tm,),
        in_specs=[_rows(tm, d), _rows(tm, 2 * d), _rows(tm, d), _rows(tm, d),
                  _mat(*wbna_t), _mat(*wbsw_t), _mat(*wout), _full(dep.shape)],
        out_specs=[_rows(tm, d), _rows(tm, 2 * d), _rows(tm, d), _rows(tm, d), _rows(tm, 512), _rows(tm, 512),
                   _full((1, 2 * d))],
        out_shape=[jax.ShapeDtypeStruct((s, d), BF16), jax.ShapeDtypeStruct((s, 2 * d), BF16),
                   jax.ShapeDtypeStruct((s, d), BF16), jax.ShapeDtypeStruct((s, d), BF16),
                   jax.ShapeDtypeStruct((s, 512), BF16), jax.ShapeDtypeStruct((s, 512), BF16),
                   jax.ShapeDtypeStruct((1, 2 * d), F32)],
        compiler_params=_params(),
    )(dx, gt, a_na, a_sw, wbna_t[0], wbsw_t[0], wout[0], dep)


def qk_norm_bwd(dqa, dka, dva, dqs, dks, dvs, zq, dzg, gq_na, gk_na, gq_sw, gk_sw, bd, name):
    s = zq.shape[0]
    d2 = dzg.shape[1]
    n_in = QKV_WIDTH + d2
    tm = _row_tile(s)

    def body(dqa_ref, dka_ref, dva_ref, dqs_ref, dks_ref, dvs_ref, zq_ref, dzg_ref,
             gqa_ref, gka_ref, gqs_ref, gks_ref, bd_ref, dz_ref, dgqa_ref, dgka_ref, dgqs_ref, dgks_ref):
        @pl.when(pl.program_id(0) == 0)
        def _():
            for r in (dgqa_ref, dgka_ref, dgqs_ref, dgks_ref):
                r[...] = jnp.zeros(r.shape, F32)

        bd512 = bd_ref[...]
        bd128 = bd_ref[0:SW_KV_WIDTH, 0:SW_KV_WIDTH]

        def one(c0, c1, dy_ref, g_ref, dg_ref, bdm, scale):
            z = zq_ref[:, c0:c1].astype(F32)
            r = lax.rsqrt(_group_mean(z * z, bdm) + EPS)
            zh = z * r
            dy = dy_ref[...] * scale
            dyg = dy * g_ref[...]
            dz = r * (dyg - zh * _group_mean(dyg * zh, bdm))
            dz_ref[:, c0:c1] = dz.astype(BF16)
            dg_ref[...] = dg_ref[...] + jnp.sum(dy * zh, axis=0, keepdims=True)

        one(0, 512, dqa_ref, gqa_ref, dgqa_ref, bd512, SCALE)
        one(512, 1024, dka_ref, gka_ref, dgka_ref, bd512, 1.0)
        dz_ref[:, 1024:1536] = dva_ref[...].astype(BF16)
        one(1536, 2048, dqs_ref, gqs_ref, dgqs_ref, bd512, SCALE)
        one(2048, 2176, dks_ref, gks_ref, dgks_ref, bd128, 1.0)
        dz_ref[:, 2176:2304] = dvs_ref[...].astype(BF16)
        dz_ref[:, QKV_WIDTH:n_in] = dzg_ref[...]

    return pl.pallas_call(
        body, name=name, grid=(s // tm,),
        in_specs=[_rows(tm, 512), _rows(tm, 512), _rows(tm, 512), _rows(tm, 512), _rows(tm, 128), _rows(tm, 128),
                  _rows(tm, QKV_WIDTH), _rows(tm, d2),
                  _full((1, 512)), _full((1, 512)), _full((1, 512)), _full((1, 128)), _full((MXU_TILE, MXU_TILE))],
        out_specs=[_rows(tm, n_in), _full((1, 512)), _full((1, 512)), _full((1, 512)), _full((1, 128))],
        out_shape=[jax.ShapeDtypeStruct((s, n_in), BF16)] + [jax.ShapeDtypeStruct((1, 512), F32)] * 3
                  + [jax.ShapeDtypeStruct((1, 128), F32)],
        compiler_params=_params(),
    )(dqa, dka, dva, dqs, dks, dvs, zq, dzg, gq_na, gk_na, gq_sw, gk_sw, bd)


def ffn_bwd_act(dx, wd, hg, hu, name):
    s, d = dx.shape
    f = wd[0].shape[1]
    tm = _row_tile(s)
    fc = _col_chunk(f)

    def body(dx_ref, w_ref, hg_ref, hu_ref, dxb_ref, dhg_ref, dhu_ref):
        dxv = dx_ref[...]
        dxb_ref[...] = dxv.astype(BF16)
        half = (0.5 * dxv).astype(BF16)
        for c0 in range(0, f, fc):
            dact = _dotg(half, w_ref[c0:c0 + fc, :], NT)
            dhu_ref[:, c0:c0 + fc] = (dact * hu_ref[:, c0:c0 + fc].astype(F32)).astype(BF16)
            dhg_ref[:, c0:c0 + fc] = (dact * hg_ref[:, c0:c0 + fc].astype(F32)).astype(BF16)

    return pl.pallas_call(
        body, name=name, grid=(s // tm,),
        in_specs=[_rows(tm, d), _mat(*wd), _rows(tm, f), _rows(tm, f)],
        out_specs=[_rows(tm, d), _rows(tm, f), _rows(tm, f)],
        out_shape=[jax.ShapeDtypeStruct((s, d), BF16), jax.ShapeDtypeStruct((s, f), BF16),
                   jax.ShapeDtypeStruct((s, f), BF16)],
        compiler_params=_params(),
    )(dx, wd[0], hg, hu)


def proj_bwd_norm(acts, weights, x, gain, dx, dep, name):
    s, d = x.shape
    tm = min(_row_tile(s), 256)
    n = len(acts)

    def body(*refs):
        a_refs, w_refs = refs[:n], refs[n:2 * n]
        x_ref, g_ref, dx_ref, _, o_ref, dg_ref = refs[2 * n:]

        @pl.when(pl.program_id(0) == 0)
        def _():
            dg_ref[...] = jnp.zeros(dg_ref.shape, F32)

        dxn = _dot(a_refs[0][...], w_refs[0][...])
        for a_ref, w_ref in zip(a_refs[1:], w_refs[1:]):
            dxn = dxn + _dot(a_ref[...], w_ref[...])
        xv = x_ref[...]
        r = _rstd(xv)
        xh = xv * r
        dxh = dxn * g_ref[...]
        o_ref[...] = dx_ref[...] + r * (dxh - xh * jnp.mean(dxh * xh, axis=-1, keepdims=True))
        dg_ref[...] = dg_ref[...] + jnp.sum(dxn * xh, axis=0, keepdims=True)

    return pl.pallas_call(
        body, name=name, grid=(s // tm,),
        in_specs=[_rows(tm, a.shape[1]) for a in acts] + [_mat(*w) for w in weights]
                 + [_rows(tm, d), _full((1, d)), _rows(tm, d), _full(dep.shape)],
        out_specs=[_rows(tm, d), _full((1, d))],
        out_shape=[jax.ShapeDtypeStruct((s, d), F32), jax.ShapeDtypeStruct((1, d), F32)],
        compiler_params=_params(),
    )(*acts, *[w[0] for w in weights], x, gain, dx, dep)


def tn_matmul(products, name):
    s, n = products[0][0].shape
    tn = _tn_tile(n) if len(products) == 1 else _col_chunk(n)
    rhs = []
    for _, b, _ in products:
        if not any(b is seen for seen in rhs):
            rhs.append(b)
    which = [next(i for i, seen in enumerate(rhs) if b is seen) for _, b, _ in products]
    npr, nr = len(products), len(rhs)

    def body(*refs):
        a_refs, b_refs, o_refs = refs[:npr], refs[npr:npr + nr], refs[npr + nr:]
        for i, (_, _, scale) in enumerate(products):
            o_refs[i][...] = (scale * _dotg(a_refs[i][...], b_refs[which[i]][...], TN)).astype(BF16)

    return pl.pallas_call(
        body, name=name, grid=(n // tn,),
        in_specs=[pl.BlockSpec((s, tn), lambda i: (0, i))] * npr
                 + [pl.BlockSpec(b.shape, lambda i: (0, 0), pipeline_mode=ONCE) for b in rhs],
        out_specs=[pl.BlockSpec((tn, b.shape[1]), lambda i: (i, 0)) for _, b, _ in products],
        out_shape=[jax.ShapeDtypeStruct((n, b.shape[1]), BF16) for _, b, _ in products],
        compiler_params=_params(),
    )(*[a for a, _, _ in products], *rhs)


def _mesh_pos():
    return lax.axis_index("x"), lax.axis_index("y"), lax.axis_index("c")


def _peers():
    x, y, c = _mesh_pos()
    peers = []
    for rel in range(1, N_DEV):
        peers.append((1 - x if rel & 4 else x, 1 - y if rel & 2 else y, 1 - c if rel & 1 else c))
    return 4 * x + 2 * y + c, peers


HBM_SPEC = pl.BlockSpec(memory_space=pltpu.HBM)
SEM_SPEC = pl.BlockSpec(memory_space=pltpu.SEMAPHORE)


def _split_call(body, name, thru, n_sems, extra=(), with_token=True):
    hbm = lambda t: pltpu.with_memory_space_constraint(t, pltpu.HBM)
    effect = pltpu.CompilerParams(has_side_effects=pltpu.SideEffectType.DATAFLOW_SIDE_EFFECTING)
    nt = len(thru)
    thru_shapes = [pltpu.HBM(t.shape, t.dtype) for t in thru]
    if with_token:
        (after,) = extra
        outs = pl.pallas_call(
            body, name=name, in_specs=[HBM_SPEC] * nt + [pl.BlockSpec(memory_space=pl.ANY)],
            out_specs=[SEM_SPEC] * len(n_sems) + [HBM_SPEC] * nt + [pl.BlockSpec(memory_space=pltpu.VMEM)],
            out_shape=[pltpu.SemaphoreType.DMA((k,)) for k in n_sems] + thru_shapes
                      + [jax.ShapeDtypeStruct((8, LANES), F32)],
            input_output_aliases={i: len(n_sems) + i for i in range(nt)}, compiler_params=effect,
        )(*[hbm(t) for t in thru], after)
        return outs[:len(n_sems)], outs[len(n_sems):-1], outs[-1]
    return pl.pallas_call(
        body, name=name,
        in_specs=[HBM_SPEC] * nt + [SEM_SPEC] * len(n_sems) + [pl.BlockSpec(memory_space=pl.ANY)],
        out_specs=[HBM_SPEC] * nt, out_shape=thru_shapes,
        input_output_aliases={i: i for i in range(nt)}, compiler_params=effect,
    )(*thru, *extra)


def _gather_targets():
    x, y, c = _mesh_pos()
    return 4 * x + 2 * y + c, [(x, y, 1 - c), (1 - x, y, c), (x, 1 - y, c), (1 - x, 1 - y, c)]


def gather_start(shards, after, name):
    n = len(shards)
    zones = [lax.empty((w.shape[0], N_DEV) + w.shape[1:], w.dtype) for w in shards]

    def body(*refs):
        ins, zs = refs[:n], refs[n:2 * n]
        send_sems, recv_sems, local_sems = refs[2 * n + 1:2 * n + 4]
        token = refs[-1]
        me, targets = _gather_targets()
        for a in range(n):
            pltpu.make_async_copy(ins[a], zs[a].at[:, me], local_sems.at[a]).start()
            for k, to in enumerate(targets):
                pltpu.make_async_remote_copy(
                    src_ref=ins[a], dst_ref=zs[a].at[:, me], send_sem=send_sems.at[4 * a + k],
                    recv_sem=recv_sems.at[4 * a + k], device_id=to, device_id_type=MESH).start()
        token[...] = jnp.zeros(token.shape, F32)

    sems, thru, token = _split_call(body, name, list(shards) + zones, (4 * n, 4 * n, n), extra=(after,))
    return (sems, thru, n), token


def gather_wait(started, after, name):
    sems, thru, n = started

    def body(*refs):
        zs = refs[n:2 * n]
        send_sems, recv_sems, local_sems = refs[2 * n:2 * n + 3]
        _, targets = _gather_targets()
        for a in range(n):
            for k, to in enumerate(targets):
                cp = pltpu.make_async_remote_copy(
                    src_ref=zs[a].at[:, 0], dst_ref=zs[a].at[:, 0], send_sem=send_sems.at[4 * a + k],
                    recv_sem=recv_sems.at[4 * a + k], device_id=to, device_id_type=MESH)
                cp.wait_send()
                cp.wait_recv()
            pltpu.make_async_copy(zs[a].at[:, 0], zs[a].at[:, 0], local_sems.at[a]).wait()

    return _split_call(body, name, thru, (4 * n, 4 * n, n), extra=(*sems, after), with_token=False)[n:]


def forward_start(zones, after, name):
    n = len(zones)

    def body(*refs):
        zs = refs[:n]
        send_sems, recv_sems = refs[n + 1:n + 3]
        token = refs[-1]
        x, y, c = _mesh_pos()
        for a in range(n):
            for j, chip in enumerate([(1 - x, y), (x, 1 - y), (1 - x, 1 - y)]):
                blk = zs[a].at[:, 4 * chip[0] + 2 * chip[1] + c]
                pltpu.make_async_remote_copy(
                    src_ref=blk, dst_ref=blk, send_sem=send_sems.at[3 * a + j], recv_sem=recv_sems.at[3 * a + j],
                    device_id=(x, y, 1 - c), device_id_type=MESH).start()
        token[...] = jnp.zeros(token.shape, F32)

    sems, thru, token = _split_call(body, name, list(zones), (3 * n, 3 * n), extra=(after,))
    return (sems, thru, n), token


def forward_wait(started, after, name):
    sems, thru, n = started

    def body(*refs):
        zs = refs[:n]
        send_sems, recv_sems = refs[n:n + 2]
        x, y, c = _mesh_pos()
        for a in range(n):
            for j in range(3):
                cp = pltpu.make_async_remote_copy(
                    src_ref=zs[a].at[:, 0], dst_ref=zs[a].at[:, 0], send_sem=send_sems.at[3 * a + j],
                    recv_sem=recv_sems.at[3 * a + j], device_id=(x, y, 1 - c), device_id_type=MESH)
                cp.wait_send()
                cp.wait_recv()

    return _split_call(body, name, thru, (3 * n, 3 * n), extra=(*sems, after), with_token=False)


def scatter_start(groups, name):
    n = len(groups)
    flat = [g for grp in groups for g in grp]
    nf = len(flat)
    offs = np.cumsum([0] + [len(grp) for grp in groups])
    lands = [lax.empty((N_DEV, len(grp)) + grp[0].shape[1:], grp[0].dtype) for grp in groups]

    def body(*refs):
        ins, zones = refs[:nf], refs[nf:nf + n]
        send_sems, recv_sems, local_sems = refs[nf + n:nf + n + 3]
        token = refs[-1]
        me, peers = _peers()
        for a in range(n):
            for w in range(len(groups[a])):
                pltpu.make_async_copy(ins[offs[a] + w].at[me], zones[a].at[me, w], local_sems.at[a]).start()
        for k, peer in enumerate(peers):
            p_id = 4 * peer[0] + 2 * peer[1] + peer[2]
            for a in range(n):
                for w in range(len(groups[a])):
                    pltpu.make_async_remote_copy(
                        src_ref=ins[offs[a] + w].at[p_id], dst_ref=zones[a].at[me, w],
                        send_sem=send_sems.at[7 * a + k], recv_sem=recv_sems.at[7 * a + k],
                        device_id=peer, device_id_type=MESH).start()
        token[...] = jnp.zeros(token.shape, F32)

    hbm = lambda t: pltpu.with_memory_space_constraint(t, pltpu.HBM)
    outs = pl.pallas_call(
        body, name=name,
        in_specs=[HBM_SPEC] * (nf + n),
        out_specs=[SEM_SPEC] * 3 + [HBM_SPEC] * (nf + n) + [pl.BlockSpec(memory_space=pltpu.VMEM)],
        out_shape=[pltpu.SemaphoreType.DMA((7 * n,)), pltpu.SemaphoreType.DMA((7 * n,)), pltpu.SemaphoreType.DMA((n,))]
                  + [pltpu.HBM(t.shape, t.dtype) for t in flat + lands]
                  + [jax.ShapeDtypeStruct((8, LANES), F32)],
        input_output_aliases={i: 3 + i for i in range(nf + n)},
        compiler_params=pltpu.CompilerParams(has_side_effects=pltpu.SideEffectType.DATAFLOW_SIDE_EFFECTING),
    )(*[hbm(t) for t in flat], *[hbm(t) for t in lands])
    sems, thru, token = outs[:3], outs[3:3 + nf + n], outs[-1]
    return (sems, thru, [len(grp) for grp in groups]), token


def scatter_wait(started, after, name):
    (send_sems, recv_sems, local_sems), thru, sizes = started
    n = len(sizes)
    nf = len(thru) - n

    def body(*refs):
        zones = refs[nf:nf + n]
        s_sems, r_sems, l_sems = refs[nf + n:nf + n + 3]
        me, peers = _peers()
        for a in range(n):
            for k, peer in enumerate(peers):
                cp = pltpu.make_async_remote_copy(
                    src_ref=zones[a].at[0], dst_ref=zones[a].at[0],
                    send_sem=s_sems.at[7 * a + k], recv_sem=r_sems.at[7 * a + k], device_id=peer,
                    device_id_type=MESH)
                cp.wait_send()
                cp.wait_recv()
            pltpu.make_async_copy(zones[a].at[0], zones[a].at[0], l_sems.at[a]).wait()

    outs = pl.pallas_call(
        body, name=name,
        in_specs=[HBM_SPEC] * (nf + n) + [SEM_SPEC] * 3 + [pl.BlockSpec(memory_space=pl.ANY)],
        out_specs=[HBM_SPEC] * (nf + n),
        out_shape=[pltpu.HBM(t.shape, t.dtype) for t in thru],
        input_output_aliases={i: i for i in range(nf + n)},
        compiler_params=pltpu.CompilerParams(has_side_effects=pltpu.SideEffectType.DATAFLOW_SIDE_EFFECTING),
    )(*thru, send_sems, recv_sems, local_sems, after)
    return outs[nf:]


def pair_start(grads, after, name):
    nw = len(grads)
    land = lax.empty((4, nw) + grads[0].shape[1:], grads[0].dtype)

    def body(*refs):
        ins, zone = refs[:nw], refs[nw]
        send_sems, recv_sems = refs[nw + 2:nw + 4]
        x, y, c = _mesh_pos()
        for j in range(4):
            for w in range(nw):
                pltpu.make_async_remote_copy(
                    src_ref=ins[w].at[2 * j + (1 - c)], dst_ref=zone.at[j, w], send_sem=send_sems.at[0],
                    recv_sem=recv_sems.at[0], device_id=(x, y, 1 - c), device_id_type=MESH).start()
        refs[-1][...] = jnp.zeros(refs[-1].shape, F32)

    sems, thru, token = _split_call(body, name, list(grads) + [land], (1, 1), extra=(after,))
    return (sems, thru, nw), token


def pair_wait(started, after, name):
    sems, thru, nw = started

    def body(*refs):
        zone = refs[nw]
        send_sems, recv_sems = refs[nw + 1:nw + 3]
        x, y, c = _mesh_pos()
        cp = pltpu.make_async_remote_copy(src_ref=zone, dst_ref=zone, send_sem=send_sems.at[0],
                                          recv_sem=recv_sems.at[0], device_id=(x, y, 1 - c), device_id_type=MESH)
        cp.wait_send()
        cp.wait_recv()

    outs = _split_call(body, name, thru, (1, 1), extra=(*sems, after), with_token=False)
    return outs[:nw], outs[nw]


def pair_sum(grads, land, name):
    nw = len(grads)
    _, r, c_dim = grads[0].shape

    def body(*refs):
        g_refs, l_ref, o_ref = refs[:nw], refs[nw], refs[nw + 1]
        core = lax.axis_index("c")
        for w in range(nw):
            o_ref[0, w] = (g_refs[w][0, core].astype(F32) + l_ref[0, w].astype(F32)).astype(BF16)

    return pl.pallas_call(
        body, name=name, grid=(4,),
        in_specs=[pl.BlockSpec((1, 2, r, c_dim), lambda j: (j, 0, 0, 0))] * nw
                 + [pl.BlockSpec((1, nw, r, c_dim), lambda j: (j, 0, 0, 0))],
        out_specs=pl.BlockSpec((1, nw, r, c_dim), lambda j: (j, 0, 0, 0)),
        out_shape=jax.ShapeDtypeStruct((4, nw, r, c_dim), BF16),
        compiler_params=_params(),
    )(*[g.reshape(4, 2, r, c_dim) for g in grads], land)


def _other_chips():
    x, y, c = _mesh_pos()
    chips = []
    for rel in range(1, 4):
        px, py = (1 - x if rel & 2 else x), (1 - y if rel & 1 else y)
        chips.append((px, py, 2 * px + py))
    return 2 * x + y, c, chips


def chip_start(pair_sums, after, name):
    land = lax.empty(pair_sums.shape, pair_sums.dtype)

    def body(*refs):
        h_ref, zone = refs[0], refs[1]
        send_sems, recv_sems, local_sem = refs[3:6]
        mine, c, chips = _other_chips()
        pltpu.make_async_copy(h_ref.at[mine], zone.at[mine], local_sem.at[0]).start()
        for k, (px, py, j) in enumerate(chips):
            pltpu.make_async_remote_copy(
                src_ref=h_ref.at[j], dst_ref=zone.at[mine], send_sem=send_sems.at[k], recv_sem=recv_sems.at[k],
                device_id=(px, py, c), device_id_type=MESH).start()
        refs[-1][...] = jnp.zeros(refs[-1].shape, F32)

    sems, thru, token = _split_call(body, name, [pair_sums, land], (3, 3, 1), extra=(after,))
    return (sems, thru), token


def chip_wait(started, after, name):
    sems, thru = started

    def body(*refs):
        zone = refs[1]
        send_sems, recv_sems, local_sem = refs[2:5]
        _, c, chips = _other_chips()
        for k, (px, py, _) in enumerate(chips):
            cp = pltpu.make_async_remote_copy(
                src_ref=zone.at[0], dst_ref=zone.at[0], send_sem=send_sems.at[k], recv_sem=recv_sems.at[k],
                device_id=(px, py, c), device_id_type=MESH)
            cp.wait_send()
            cp.wait_recv()
        pltpu.make_async_copy(zone.at[0], zone.at[0], local_sem.at[0]).wait()

    return _split_call(body, name, thru, (3, 3, 1), extra=(*sems, after), with_token=False)[1]


def share_start(parts, after, name):
    n = len(parts)
    zones = [lax.empty((N_DEV,) + p.shape, p.dtype) for p in parts]

    def body(*refs):
        ins, zs = refs[:n], refs[n:2 * n]
        send_sems, recv_sems, local_sems = refs[2 * n + 1:2 * n + 4]
        me, peers = _peers()
        for i in range(n):
            pltpu.make_async_copy(ins[i], zs[i].at[me], local_sems.at[i]).start()
            for k, peer in enumerate(peers):
                pltpu.make_async_remote_copy(
                    src_ref=ins[i], dst_ref=zs[i].at[me], send_sem=send_sems.at[7 * i + k],
                    recv_sem=recv_sems.at[7 * i + k], device_id=peer, device_id_type=MESH).start()
        refs[-1][...] = jnp.zeros(refs[-1].shape, F32)

    sems, thru, token = _split_call(body, name, list(parts) + zones, (7 * n, 7 * n, n), extra=(after,))
    return (sems, thru, n), token


def share_wait(started, after, name):
    sems, thru, n = started

    def body(*refs):
        zs = refs[n:2 * n]
        send_sems, recv_sems, local_sems = refs[2 * n:2 * n + 3]
        _, peers = _peers()
        for i in range(n):
            for k, peer in enumerate(peers):
                cp = pltpu.make_async_remote_copy(
                    src_ref=zs[i].at[0], dst_ref=zs[i].at[0], send_sem=send_sems.at[7 * i + k],
                    recv_sem=recv_sems.at[7 * i + k], device_id=peer, device_id_type=MESH)
                cp.wait_send()
                cp.wait_recv()
            pltpu.make_async_copy(zs[i].at[0], zs[i].at[0], local_sems.at[i]).wait()

    return _split_call(body, name, thru, (7 * n, 7 * n, n), extra=(*sems, after), with_token=False)[n:]


def _adamw_math(w, g, m, v):
    m = ADAM_B1 * m + (1.0 - ADAM_B1) * g
    v = ADAM_B2 * v + (1.0 - ADAM_B2) * (g * g)
    m_hat = m / (1.0 - ADAM_B1 ** ADAM_STEP)
    v_hat = v / (1.0 - ADAM_B2 ** ADAM_STEP)
    delta = -ADAM_LR * (m_hat / (jnp.sqrt(v_hat) + ADAM_EPS) + ADAM_WD * w)
    return delta, m, v


ADAMW_BLOCK_BYTES = 24 * 1024 * 1024


def adamw_layer(zone, layer, items, after, name):
    n_src, nw, r, c = zone.shape
    depth = items[0][0].shape[0]
    prevs = [p if p is not None else tuple(lax.empty((depth, r, c), F32) for _ in range(4)) for _, _, _, p in items]
    row_bytes = 2 * nw * c * (2 * n_src + 4 * 7)
    tr = max(t for t in range(8, r + 1, 8) if r % t == 0 and t * row_bytes <= ADAMW_BLOCK_BYTES)

    def body(z_ref, *rest):
        ins, outs = rest[:3 * nw], rest[7 * nw + 1:]
        for i in range(nw):
            g = z_ref[0, i].astype(F32)
            for src in range(1, n_src):
                g = g + z_ref[src, i].astype(F32)
            g_ref, d_ref, mo_ref, vo_ref = outs[4 * i:4 * i + 4]
            w_ref, m_ref, v_ref = ins[3 * i:3 * i + 3]
            g_ref[...] = g
            d_ref[...], mo_ref[...], vo_ref[...] = _adamw_math(w_ref[...], g, m_ref[...], v_ref[...])

    rows = pl.BlockSpec((None, tr, c), lambda i: (layer, i, 0))
    anywhere = pl.BlockSpec(memory_space=pl.ANY)
    outs = pl.pallas_call(
        body, name=name, grid=(r // tr,),
        in_specs=[pl.BlockSpec((n_src, nw, tr, c), lambda i: (0, 0, i, 0))] + [rows] * (3 * nw)
                 + [anywhere] * (4 * nw + 1),
        out_specs=[rows] * (4 * nw),
        out_shape=[jax.ShapeDtypeStruct((depth, r, c), F32)] * (4 * nw),
        input_output_aliases={1 + 3 * nw + k: k for k in range(4 * nw)},
        compiler_params=_params(),
    )(zone, *[t for w, m, v, _ in items for t in (w, m, v)], *[t for p in prevs for t in p], after)
    return [tuple(outs[4 * i:4 * i + 4]) for i in range(nw)]


def adamw_small(ws, recvs, ms, vs, name):
    n = len(ws)

    def body(*refs):
        w_refs, r_refs, m_refs, v_refs = (refs[i * n:(i + 1) * n] for i in range(4))
        g_refs, d_refs, mo_refs, vo_refs = (refs[(4 + i) * n:(5 + i) * n] for i in range(4))
        for i in range(n):
            g = r_refs[i][0]
            for src in range(1, N_DEV):
                g = g + r_refs[i][src]
            g_refs[i][...] = g
            d_refs[i][...], mo_refs[i][...], vo_refs[i][...] = _adamw_math(w_refs[i][...], g, m_refs[i][...],
                                                                            v_refs[i][...])

    vm = pl.BlockSpec(memory_space=pltpu.VMEM)
    outs = pl.pallas_call(
        body, name=name, in_specs=[vm] * (4 * n), out_specs=[vm] * (4 * n),
        out_shape=[jax.ShapeDtypeStruct(w.shape, F32) for w in ws] * 4,
        compiler_params=pltpu.CompilerParams(vmem_limit_bytes=V7X_VMEM_LIMIT),
    )(*ws, *recvs, *ms, *vs)
    return [outs[i * n:(i + 1) * n] for i in range(4)]


SMALL_NAMES = ("ffn1_norm", "mix_norm", "ffn2_norm", "b_gate", "na_q_norm", "na_k_norm", "sw_q_norm", "sw_k_norm",
               "na_rpb", "sw_sink", "t5_rel_table")


def kernel(x, ffn1_norm, ffn1_w_gate, ffn1_w_up, ffn1_w_down, mix_norm, w_in, b_gate, na_q_norm, na_k_norm, na_rpb, sw_q_norm, sw_k_norm, sw_sink, t5_rel_table, w_branch_na, w_branch_sw, w_out, ffn2_norm, ffn2_w_gate, ffn2_w_up, ffn2_w_down, loss_target, m_ffn1_norm, m_ffn1_w_gate, m_ffn1_w_up, m_ffn1_w_down, m_mix_norm, m_w_in, m_b_gate, m_na_q_norm, m_na_k_norm, m_na_rpb, m_sw_q_norm, m_sw_k_norm, m_sw_sink, m_t5_rel_table, m_w_branch_na, m_w_branch_sw, m_w_out, m_ffn2_norm, m_ffn2_w_gate, m_ffn2_w_up, m_ffn2_w_down, v_ffn1_norm, v_ffn1_w_gate, v_ffn1_w_up, v_ffn1_w_down, v_mix_norm, v_w_in, v_b_gate, v_na_q_norm, v_na_k_norm, v_na_rpb, v_sw_q_norm, v_sw_k_norm, v_sw_sink, v_t5_rel_table, v_w_branch_na, v_w_branch_sw, v_w_out, v_ffn2_norm, v_ffn2_w_gate, v_ffn2_w_up, v_ffn2_w_down):
    weights = dict(ffn1_norm=ffn1_norm, ffn1_w_gate=ffn1_w_gate, ffn1_w_up=ffn1_w_up, ffn1_w_down=ffn1_w_down,
                   mix_norm=mix_norm, w_in=w_in, b_gate=b_gate, na_q_norm=na_q_norm, na_k_norm=na_k_norm,
                   na_rpb=na_rpb, sw_q_norm=sw_q_norm, sw_k_norm=sw_k_norm, sw_sink=sw_sink,
                   t5_rel_table=t5_rel_table, w_branch_na=w_branch_na, w_branch_sw=w_branch_sw, w_out=w_out,
                   ffn2_norm=ffn2_norm, ffn2_w_gate=ffn2_w_gate, ffn2_w_up=ffn2_w_up, ffn2_w_down=ffn2_w_down)
    mom_m = dict(ffn1_norm=m_ffn1_norm, ffn1_w_gate=m_ffn1_w_gate, ffn1_w_up=m_ffn1_w_up, ffn1_w_down=m_ffn1_w_down,
                 mix_norm=m_mix_norm, w_in=m_w_in, b_gate=m_b_gate, na_q_norm=m_na_q_norm, na_k_norm=m_na_k_norm,
                 na_rpb=m_na_rpb, sw_q_norm=m_sw_q_norm, sw_k_norm=m_sw_k_norm, sw_sink=m_sw_sink,
                 t5_rel_table=m_t5_rel_table, w_branch_na=m_w_branch_na, w_branch_sw=m_w_branch_sw, w_out=m_w_out,
                 ffn2_norm=m_ffn2_norm, ffn2_w_gate=m_ffn2_w_gate, ffn2_w_up=m_ffn2_w_up, ffn2_w_down=m_ffn2_w_down)
    mom_v = dict(ffn1_norm=v_ffn1_norm, ffn1_w_gate=v_ffn1_w_gate, ffn1_w_up=v_ffn1_w_up, ffn1_w_down=v_ffn1_w_down,
                 mix_norm=v_mix_norm, w_in=v_w_in, b_gate=v_b_gate, na_q_norm=v_na_q_norm, na_k_norm=v_na_k_norm,
                 na_rpb=v_na_rpb, sw_q_norm=v_sw_q_norm, sw_k_norm=v_sw_k_norm, sw_sink=v_sw_sink,
                 t5_rel_table=v_t5_rel_table, w_branch_na=v_w_branch_na, w_branch_sw=v_w_branch_sw, w_out=v_w_out,
                 ffn2_norm=v_ffn2_norm, ffn2_w_gate=v_ffn2_w_gate, ffn2_w_up=v_ffn2_w_up, ffn2_w_down=v_ffn2_w_down)
    order = list(weights)

    depth = ffn1_norm.shape[0]
    s, d = x.shape[1], x.shape[2]
    xs = x[0]
    tr = lambda w: jnp.swapaxes(w, -1, -2)

    merge = lambda t: t.reshape(t.shape[0], N_DEV * t.shape[2], t.shape[3])
    no_dep = jnp.zeros((8, LANES), F32)

    def shards_of(kind, l):
        stack = lambda *ws: jnp.stack(ws).astype(BF16)
        if kind == "ffn1":
            return [stack(tr(ffn1_w_gate[l]), tr(ffn1_w_up[l]), ffn1_w_down[l])]
        if kind == "win":
            return [stack(tr(w_in[l]))]
        return [stack(tr(ffn2_w_gate[l]), tr(ffn2_w_up[l]), ffn2_w_down[l]), stack(w_out[l]),
                stack(tr(w_branch_na[l]), tr(w_branch_sw[l]))]

    shards = {(kind, l): shards_of(kind, l) for l in range(depth) for kind in ("ffn1", "win", "rest")}

    def start(kind, l, after):
        return gather_start(shards[kind, l], after, f"gather_{kind}_{l}")

    def arrive(started, kind, l, after):
        zones = gather_wait(started, after, f"gather_{kind}_{l}_wait")
        return forward_start(zones, no_dep, f"forward_{kind}_{l}")

    def finish(fwd, kind, l, after):
        return [merge(z) for z in forward_wait(fwd, after, f"forward_{kind}_{l}_wait")]

    bd = jnp.asarray(np.kron(np.eye(MXU_TILE // HEAD_DIM), np.full((HEAD_DIM, HEAD_DIM), 1.0 / HEAD_DIM)), BF16)
    bmap = jnp.asarray(_t5_bucket_map())
    tile8 = lambda g: jnp.tile(g, NA_WIDTH // HEAD_DIM).reshape(1, NA_WIDTH)
    tile2 = lambda g: jnp.tile(g, SW_KV_WIDTH // HEAD_DIM).reshape(1, SW_KV_WIDTH)

    st_first, tok = start("ffn1", 0, no_dep)
    t5b = t5_expand(t5_rel_table, bmap, tok, "t5_expand").reshape(SW_STACK, 3 * SW_BLOCK)
    t2_tables = [rpb_expand(_rpb_rows(na_rpb[l]), tok, f"rpb_expand_{l}") for l in range(depth)]
    qk_gains = [(tile8(na_q_norm[l]), tile8(na_k_norm[l]), tile8(sw_q_norm[l]), tile2(sw_k_norm[l]))
                for l in range(depth)]
    early = ([t[0, 0, 0:8, :] for t in t2_tables] + [t[0, 0:8, 0:LANES].astype(F32) for v in shards.values() for t in v]
             + [g[:, 0:LANES] for gs in qk_gains for g in gs])
    fwd, _ = arrive(st_first, "ffn1", 0, functools.reduce(jnp.add, early, t5b[0:8, 0:LANES]))
    st_win, dep = start("win", 0, t5b)
    (first,) = finish(fwd, "ffn1", 0, dep)

    saved = []
    layer_w = {0: dict(wg1=(first, 0), wu1=(first, 1), wd1=(first, 2))}
    cur = xs
    for l in range(depth):
        sv = {}
        lw = layer_w[l]
        sv["x0"] = cur
        cur, sv["xn1"], sv["hg1"], sv["hu1"], sv["act1"] = ffn_forward(
            cur, ffn1_norm[l][None], lw["wg1"], lw["wu1"], lw["wd1"], dep, f"ffn1_{l}")
        sv["x1"] = cur
        fwd, _ = arrive(st_win, "win", l, cur)
        st_rest, tok = start("rest", l, cur)
        (zb,) = finish(fwd, "win", l, tok)
        lw["win"] = (zb, 0)
        sv["gains"] = qk_gains[l]
        sv["hn"], sv["zq"], sv["qa"], sv["ka"], sv["qs"], sv["ks"], sv["gt"] = mix_in(
            cur, mix_norm[l][None], lw["win"], b_gate[l][None], *sv["gains"], bd, f"mix_in_{l}")
        sv["t2"] = t2_tables[l]
        sv["o_na"] = na_fwd(sv["qa"], sv["ka"], sv["zq"], sv["t2"], f"na_fwd_{l}")
        dep = no_dep
        if l + 1 < depth:
            st_ffn1, dep = start("ffn1", l + 1, sv["o_na"])
        sv["o_sw"] = sw_fwd(sv["qs"], sv["ks"], sv["zq"], t5b, sw_sink[l], dep, f"sw_fwd_{l}")
        fwd, tok = arrive(st_rest, "rest", l, sv["o_sw"][0:8, 0:LANES] + sv["o_na"][0:8, 0:LANES])
        za, zc, zd = finish(fwd, "rest", l, tok)
        lw.update(wg2=(za, 0), wu2=(za, 1), wd2=(za, 2), wout=(zc, 0), wna=(zd, 0), wsw=(zd, 1))
        cur, sv["a_na"], sv["a_sw"], sv["merged"] = merge_out(
            cur, sv["o_na"], sv["o_sw"], sv["gt"], lw["wna"], lw["wsw"], lw["wout"], f"merge_out_{l}")
        sv["x2"] = cur
        if l + 1 < depth:
            st_win, dep = start("win", l + 1, cur)
            sv["xn2"], sv["hg2"], sv["hu2"], sv["act2"] = ffn_forward(
                cur, ffn2_norm[l][None], lw["wg2"], lw["wu2"], None, dep, f"ffn2_up_{l}")
            fwd, dep = arrive(st_ffn1, "ffn1", l + 1, sv["act2"])
            cur = ffn_down(cur, sv["act2"], lw["wd2"], dep, f"ffn2_down_{l}")
            (za,) = finish(fwd, "ffn1", l + 1, cur)
            layer_w[l + 1] = dict(wg1=(za, 0), wu1=(za, 1), wd1=(za, 2))
        else:
            dx, loss_acc, sv["xn2"], sv["hg2"], sv["hu2"], sv["act2"] = ffn_forward(
                cur, ffn2_norm[l][None], lw["wg2"], lw["wu2"], lw["wd2"], no_dep, f"ffn2_{l}",
                target=loss_target[0])
        dep = no_dep
        saved.append(sv)

    split = lambda t: t.reshape(N_DEV, t.shape[0] // N_DEV, t.shape[1])
    pending = {}
    last_key = "ffn1_0"
    two_level = {last_key}
    small = {k: [None] * depth for k in SMALL_NAMES if k != "t5_rel_table"}
    dbias_sw = []
    for l in reversed(range(depth)):
        sv = saved[l]
        lw = layer_w[l]
        wg1, wu1, wd1, wg2, wu2, wd2 = (lw[k] for k in ("wg1", "wu1", "wd1", "wg2", "wu2", "wd2"))
        win_t, wout_l, wna_t, wsw_t = lw["win"], lw["wout"], lw["wna"], lw["wsw"]
        blocks = ((2, "x2", "xn2", "hg2", "hu2", "act2", wg2, wu2, wd2, "ffn2_norm", 3),
                  (1, "x0", "xn1", "hg1", "hu1", "act1", wg1, wu1, wd1, "ffn1_norm", 0))

        def ffn_backward(dx, blk):
            tag, xk, xnk, hgk, huk, actk, wg, wu, wd, norm_name, slot = blk
            gains = weights[norm_name]
            dxb, dhg, dhu = ffn_bwd_act(dx, wd, sv[hgk], sv[huk], f"ffn{tag}_bwd_act_{l}")
            gwg, gwu, gwd = tn_matmul([(dhg, sv[xnk], 1.0), (dhu, sv[xnk], 1.0), (sv[actk], dxb, 0.5)],
                                      f"ffn{tag}_dw_{l}")
            key = f"ffn{tag}_{l}"
            blocks_of = [split(gwg), split(gwu), split(gwd)]
            if key in two_level:
                paired, token = pair_start(blocks_of, dxb, f"pair_{key}")
            else:
                pending[key], token = scatter_start([blocks_of], f"scatter_{key}")
            dx, dg = proj_bwd_norm([dhg, dhu], [wg, wu], sv[xk], gains[l][None], dx, token, f"ffn{tag}_bwd_x_{l}")
            token = no_dep
            if key in two_level:
                thru, land = pair_wait(paired, dx, f"pair_{key}_wait")
                pending[key], token = chip_start(pair_sum(thru, land, f"pair_sum_{key}"), dg, f"chips_{key}")
            small[norm_name][l] = dg[0]
            return dx, token

        dx, token = ffn_backward(dx, blocks[0])
        dxb, dzg, da_na, da_sw, do_na, do_sw, dbg = mix_bwd_out(
            dx, sv["gt"], sv["a_na"], sv["a_sw"], wna_t, wsw_t, wout_l, token, f"mix_bwd_out_{l}")
        small["b_gate"][l] = dbg[0]
        gwout, gwna, gwsw = tn_matmul([(sv["merged"], dxb, 1.0), (da_na, sv["o_na"], 1.0), (da_sw, sv["o_sw"], 1.0)],
                                      f"mix_dw_{l}")
        dqa, dka, dva, dt2 = na_bwd(sv["qa"], sv["ka"], sv["zq"], sv["t2"], sv["o_na"], do_na, f"na_bwd_{l}")
        dqs, dks, dvs, dbias, dsink = sw_bwd(sv["qs"], sv["ks"], sv["zq"], t5b, sw_sink[l], sv["o_sw"], do_sw,
                                             f"sw_bwd_{l}")
        dbias_sw.append(dbias.reshape(SW_HEADS, SW_BLOCK, 3 * SW_BLOCK))
        small["sw_sink"][l] = jnp.sum(dsink[:, 0].reshape(SW_HEADS, SW_BLOCK), axis=1)
        small["na_rpb"][l] = _rpb_from_rows(rpb_reduce(dt2, f"rpb_reduce_{l}"))
        dz, dgqa, dgka, dgqs, dgks = qk_norm_bwd(dqa, dka, dva, dqs, dks, dvs, sv["zq"], dzg, *sv["gains"], bd,
                                                 f"qk_norm_bwd_{l}")
        fold = lambda g: jnp.sum(g.reshape(-1, HEAD_DIM), axis=0)
        small["na_q_norm"][l], small["na_k_norm"][l] = fold(dgqa), fold(dgka)
        small["sw_q_norm"][l], small["sw_k_norm"][l] = fold(dgqs), fold(dgks)
        (gwin,) = tn_matmul([(dz, sv["hn"], 1.0)], f"dwin_{l}")
        pending[f"mix_{l}"], token = scatter_start([[split(gwout)], [split(gwna), split(gwsw)], [split(gwin)]],
                                                   f"scatter_mix_{l}")
        dx, dg = proj_bwd_norm([dz], [win_t], sv["x1"], mix_norm[l][None], dx, token, f"mix_bwd_x_{l}")
        small["mix_norm"][l] = dg[0]
        dx, tail = ffn_backward(dx, blocks[1])

    dtab = t5_reduce(dbias_sw, bmap, "t5_reduce")
    small_parts = {k: jnp.stack(v) for k, v in small.items()}
    small_parts["t5_rel_table"] = jnp.transpose(dtab[:, :, 0])

    grads, delta, new_m, new_v = {}, {}, {}, {}
    state = {}
    sharing, token = share_start([small_parts[k] for k in SMALL_NAMES] + [loss_acc], tail, "share_small")
    chain = [token]
    members = {"ffn": lambda t: [(f"ffn{t}_w_gate", 0, 0, True), (f"ffn{t}_w_up", 0, 1, True),
                                 (f"ffn{t}_w_down", 0, 2, False)],
               "mix": lambda t: [("w_out", 0, 0, False), ("w_branch_na", 1, 0, True), ("w_branch_sw", 1, 1, True),
                                 ("w_in", 2, 0, True)]}

    def collect(key):
        if key in two_level:
            zones = [chip_wait(pending[key], chain[0], f"wait_{key}")]
        else:
            zones = scatter_wait(pending[key], chain[0], f"wait_{key}")
        kind, l = key.split("_")
        group = members[kind[:3]](kind[3:])
        complete = all(f"{kind}_{j}" in done for j in range(depth) if j != int(l))
        for zi, zone in enumerate(zones):
            mine = sorted((wi, k, transposed) for k, z, wi, transposed in group if z == zi)
            views = [tr if transposed else (lambda t: t) for _, _, transposed in mine]
            items = [(view(weights[k]), view(mom_m[k]), view(mom_v[k]), state.get(k))
                     for (_, k, _), view in zip(mine, views)]
            results = adamw_layer(zone, int(l), items, chain[0], f"adamw_{key}_{zi}")
            chain[0] = results[-1][1]
            for (_, k, _), view, res in zip(mine, views, results):
                state[k] = res
                if complete:
                    grads[k], delta[k], new_m[k], new_v[k] = (view(t) for t in res)
        done.add(key)

    done = set()
    for key in pending:
        if key != last_key:
            collect(key)
    collect(last_key)
    *recvs, all_losses = share_wait(sharing, chain[0], "share_small_wait")
    loss = jnp.sum(all_losses) * (0.5 / d)
    results = adamw_small([weights[k] for k in SMALL_NAMES], recvs, [mom_m[k] for k in SMALL_NAMES],
                          [mom_v[k] for k in SMALL_NAMES], "adamw_small")
    for dst, outs in zip((grads, delta, new_m, new_v), results):
        dst.update(dict(zip(SMALL_NAMES, outs)))

    return (loss, dx[None], *[grads[k] for k in order], *[delta[k] for k in order],
            *[new_m[k] for k in order], *[new_v[k] for k in order])
```

```python
import functools
import math

import numpy as np
import jax
import jax.numpy as jnp
from jax import lax
from jax.experimental import pallas as pl
from jax.experimental.pallas import tpu as pltpu

F32 = jnp.float32
BF16 = jnp.bfloat16
MESH = pl.DeviceIdType.MESH

N_DEV = 8
EPS = 1e-6
NEG = -1e30
HEAD_DIM = 64
GRID_W = 64
NA_ROWS = 8
NA_COLS = 16
NA_WIDTH = 512
SW_Q_WIDTH = 512
SW_KV_WIDTH = 128
SW_BLOCK = 128
SW_HEADS = 8
SW_REP = 4
REL_BUCKETS = 32
REL_MAX_DIST = 128
QKV_WIDTH = 3 * NA_WIDTH + SW_Q_WIDTH + 2 * SW_KV_WIDTH
SCALE = 1.0 / math.sqrt(HEAD_DIM)

ADAM_LR = 0.001
ADAM_B1 = 0.9
ADAM_B2 = 0.999
ADAM_EPS = 1e-08
ADAM_WD = 0.01
ADAM_STEP = 10

V7X_VMEM_LIMIT = 56 * 1024 * 1024
LANES = 128
MXU_TILE = 256

NT = (((1,), (1,)), ((), ()))
TN = (((0,), (0,)), ((), ()))


def _params(n_grid=1):
    return pltpu.CompilerParams(dimension_semantics=("arbitrary",) * n_grid,
                                vmem_limit_bytes=V7X_VMEM_LIMIT)


def _row_tile(s):
    for t in (512, 256, 128, 64, 32, 16, 8):
        if s % t == 0:
            return t
    raise ValueError(s)


def _tn_tile(n):
    best = max(t for t in range(LANES, min(n, 2304) + 1, LANES) if n % t == 0) if n % LANES == 0 else n
    return best // 2 if best == n and n >= 1024 else best


ONCE = pl.Buffered(1)


def _col_chunk(n):
    return MXU_TILE if n % MXU_TILE == 0 else n


def _dot(a, b):
    return jnp.dot(a, b, preferred_element_type=F32)


def _dotg(a, b, dn):
    return lax.dot_general(a, b, dn, preferred_element_type=F32)


def _sigmoid(v):
    return 1.0 / (1.0 + jnp.exp(-v))


def _rstd(xv):
    return lax.rsqrt(jnp.mean(xv * xv, axis=-1, keepdims=True) + EPS)


def _full(shape):
    nd = len(shape)
    return pl.BlockSpec(shape, lambda i, _n=nd: (0,) * _n)


def _rows(tm, width):
    return pl.BlockSpec((tm, width), lambda i: (i, 0))


def _mat(stack, idx):
    return pl.BlockSpec((None,) + tuple(stack.shape[1:]), lambda i, _w=idx: (_w, 0, 0), pipeline_mode=ONCE)


def _group_mean(v, bd):
    w = bd.shape[0]
    if v.shape[1] > w:
        return jnp.concatenate([_group_mean(v[:, c0:c0 + w], bd) for c0 in range(0, v.shape[1], w)], axis=1)
    hi = v.astype(BF16)
    lo = (v - hi.astype(F32)).astype(BF16)
    return _dot(hi, bd) + _dot(lo, bd)


def _loss_tile(y, t_ref, dy_ref, acc_ref):
    tm, d = y.shape

    @pl.when(pl.program_id(0) == 0)
    def _():
        acc_ref[...] = jnp.zeros(acc_ref.shape, F32)

    err = y - t_ref[...]
    dy_ref[...] = err * (1.0 / d)
    part = jnp.sum((err * err).reshape(tm // 8, 8, d), axis=0)
    acc = part[:, 0:LANES]
    for c0 in range(LANES, d, LANES):
        acc = acc + part[:, c0:c0 + LANES]
    acc_ref[...] = acc_ref[...] + acc


def ffn_forward(x, gain, wg_t, wu_t, wd, dep, name, target=None):
    s, d = x.shape
    f = wg_t[0].shape[1]
    tm = _row_tile(s) if wd is None else min(_row_tile(s), 256)
    fc = _col_chunk(f)
    nw = 2 if wd is None else 3
    n_in = nw + (1 if target is None else 2)

    def body(x_ref, g_ref, *refs):
        w_refs, outs = refs[:nw], refs[n_in:]
        xn_ref, dg_ref, du_ref, act_ref = outs[-4:]
        xv = x_ref[...]
        xn = (xv * _rstd(xv) * g_ref[...]).astype(BF16)
        xn_ref[...] = xn
        for c0 in range(0, f, fc):
            hg = _dotg(xn, w_refs[0][c0:c0 + fc, :], NT)
            hu = _dotg(xn, w_refs[1][c0:c0 + fc, :], NT)
            sg = _sigmoid(hg)
            silu = hg * sg
            du_ref[:, c0:c0 + fc] = silu.astype(BF16)
            dg_ref[:, c0:c0 + fc] = (hu * (sg + silu * (1.0 - sg))).astype(BF16)
            act_ref[:, c0:c0 + fc] = (silu * hu).astype(BF16)
        if wd is not None:
            y = xv + 0.5 * _dot(act_ref[...], w_refs[2][...])
            if target is None:
                outs[0][...] = y
            else:
                _loss_tile(y, refs[nw + 1], outs[0], outs[1])

    weights = [wg_t, wu_t] + ([] if wd is None else [wd])
    in_specs = [_rows(tm, d), _full((1, d))] + [_mat(*w) for w in weights] + [_full(dep.shape)]
    operands = [x, gain, *[w[0] for w in weights], dep]
    out_specs = [_rows(tm, d), _rows(tm, f), _rows(tm, f), _rows(tm, f)]
    out_shape = [jax.ShapeDtypeStruct((s, d), BF16)] + [jax.ShapeDtypeStruct((s, f), BF16)] * 3
    if wd is not None:
        out_specs, out_shape = [_rows(tm, d)] + out_specs, [jax.ShapeDtypeStruct((s, d), F32)] + out_shape
    if target is not None:
        in_specs, operands = in_specs + [_rows(tm, d)], operands + [target]
        out_specs = out_specs[:1] + [_full((8, LANES))] + out_specs[1:]
        out_shape = out_shape[:1] + [jax.ShapeDtypeStruct((8, LANES), F32)] + out_shape[1:]
    return pl.pallas_call(
        body, name=name, grid=(s // tm,), in_specs=in_specs, out_specs=out_specs, out_shape=out_shape,
        compiler_params=_params(),
    )(*operands)


def ffn_down(x, act, wd, dep, name):
    s, d = x.shape
    f = act.shape[1]
    tm = _row_tile(s)

    def body(x_ref, a_ref, w_ref, dep_ref, o_ref):
        o_ref[...] = x_ref[...] + 0.5 * _dot(a_ref[...], w_ref[...])

    return pl.pallas_call(
        body, name=name, grid=(s // tm,),
        in_specs=[_rows(tm, d), _rows(tm, f), _mat(*wd), _full(dep.shape)],
        out_specs=_rows(tm, d),
        out_shape=jax.ShapeDtypeStruct((s, d), F32),
        compiler_params=_params(),
    )(x, act, wd[0], dep)


def mix_in(x, gain, win_t, b_gate, gq_na, gk_na, gq_sw, gk_sw, bd, name):
    s, d = x.shape
    tm = _row_tile(s)
    gc = _col_chunk(2 * d)

    def body(x_ref, g_ref, w_ref, b_ref, gqa_ref, gka_ref, gqs_ref, gks_ref, bd_ref,
             hn_ref, zq_ref, qa_ref, ka_ref, qs_ref, ks_ref, gt_ref):
        xv = x_ref[...]
        hn = (xv * _rstd(xv) * g_ref[...]).astype(BF16)
        hn_ref[...] = hn

        def proj(c0, c1):
            return _dotg(hn, w_ref[c0:c1, :], NT)

        def headnorm(z, g, bdm):
            return z * lax.rsqrt(_group_mean(z * z, bdm) + EPS) * g

        bd512 = bd_ref[...]
        bd128 = bd_ref[0:SW_KV_WIDTH, 0:SW_KV_WIDTH]
        z = proj(0, 512)
        zq_ref[:, 0:512] = z.astype(BF16)
        qa_ref[...] = (headnorm(z, gqa_ref[...], bd512) * SCALE).astype(BF16)
        z = proj(512, 1024)
        zq_ref[:, 512:1024] = z.astype(BF16)
        ka_ref[...] = headnorm(z, gka_ref[...], bd512).astype(BF16)
        z = proj(1024, 1536)
        zq_ref[:, 1024:1536] = z.astype(BF16)
        z = proj(1536, 2048)
        zq_ref[:, 1536:2048] = z.astype(BF16)
        qs_ref[...] = (headnorm(z, gqs_ref[...], bd512) * SCALE).astype(BF16)
        z = proj(2048, 2176)
        zq_ref[:, 2048:2176] = z.astype(BF16)
        ks_ref[...] = headnorm(z, gks_ref[...], bd128).astype(BF16)
        z = proj(2176, 2304)
        zq_ref[:, 2176:2304] = z.astype(BF16)
        for c0 in range(0, 2 * d, gc):
            zg = proj(QKV_WIDTH + c0, QKV_WIDTH + c0 + gc) + b_ref[:, c0:c0 + gc]
            gt_ref[:, c0:c0 + gc] = _sigmoid(zg).astype(BF16)

    return pl.pallas_call(
        body, name=name, grid=(s // tm,),
        in_specs=[_rows(tm, d), _full((1, d)), _mat(*win_t), _full((1, 2 * d)),
                  _full((1, 512)), _full((1, 512)), _full((1, 512)), _full((1, 128)), _full((MXU_TILE, MXU_TILE))],
        out_specs=[_rows(tm, d), _rows(tm, QKV_WIDTH), _rows(tm, 512), _rows(tm, 512), _rows(tm, 512),
                   _rows(tm, 128), _rows(tm, 2 * d)],
        out_shape=[jax.ShapeDtypeStruct((s, d), BF16), jax.ShapeDtypeStruct((s, QKV_WIDTH), BF16),
                   jax.ShapeDtypeStruct((s, 512), BF16), jax.ShapeDtypeStruct((s, 512), BF16),
                   jax.ShapeDtypeStruct((s, 512), BF16), jax.ShapeDtypeStruct((s, 128), BF16),
                   jax.ShapeDtypeStruct((s, 2 * d), BF16)],
        compiler_params=_params(),
    )(x, gain, win_t[0], b_gate, gq_na, gk_na, gq_sw, gk_sw, bd)


def _na_iotas():
    qc = lax.broadcasted_iota(jnp.int32, (GRID_W, LANES), 0)
    ln = lax.broadcasted_iota(jnp.int32, (GRID_W, LANES), 1)
    low = ln < GRID_W
    kc = jnp.where(low, ln, ln - GRID_W)
    diff = kc - qc + (NA_COLS - 1)
    qcs = jnp.clip(qc - NA_COLS // 2, 0, GRID_W - NA_COLS)
    inwin = (kc >= qcs) & (kc < qcs + NA_COLS)
    return diff, low, inwin


NA_RI = 2 * NA_ROWS - 1
NA_CI = 2 * NA_COLS - 1
NA_T2 = NA_RI + 1


def _rpb_rows(rpb):
    h = rpb.shape[0]
    padded = jnp.pad(rpb, ((0, 0), (1, 1), (0, GRID_W - NA_CI)))
    return jnp.concatenate([padded[:, :NA_T2], padded[:, 1:NA_T2 + 1]], axis=2).reshape(h, NA_T2, LANES)


def _rpb_from_rows(rows):
    return rows[:, 1:, :NA_CI] + rows[:, :NA_RI, GRID_W:GRID_W + NA_CI]


def rpb_expand(rows, dep, name):
    n_heads = rows.shape[0]

    def body(r_ref, dep_ref, o_ref):
        for h in range(n_heads):
            for e in range(NA_T2):
                line = jnp.broadcast_to(r_ref[h, e:e + 1, :], (GRID_W, LANES))
                o_ref[h, e] = pltpu.roll(line, LANES - (NA_COLS - 1), 1, stride=1, stride_axis=0)

    return pl.pallas_call(
        body, name=name,
        in_specs=[pl.BlockSpec(memory_space=pltpu.VMEM), pl.BlockSpec(memory_space=pltpu.VMEM)],
        out_specs=pl.BlockSpec(memory_space=pltpu.VMEM),
        out_shape=jax.ShapeDtypeStruct((n_heads, NA_T2, GRID_W, LANES), F32),
        compiler_params=pltpu.CompilerParams(vmem_limit_bytes=V7X_VMEM_LIMIT),
    )(rows, dep)


def rpb_reduce(dt2, name):
    n_heads = dt2.shape[0]
    flip = jnp.asarray(np.eye(GRID_W)[::-1], BF16)

    def body(d_ref, j_ref, o_ref):
        jm = j_ref[...]
        for h in range(n_heads):
            for e in range(NA_T2):
                dv = d_ref[h, e]
                hi = dv.astype(BF16)
                mid = (dv - hi.astype(F32)).astype(BF16)
                lo = (dv - hi.astype(F32) - mid.astype(F32)).astype(BF16)
                rev = _dot(jm, hi) + _dot(jm, mid) + _dot(jm, lo)
                back = pltpu.roll(rev, LANES + (NA_COLS - 1) - (GRID_W - 1), 1, stride=1, stride_axis=0)
                o_ref[h, e:e + 1, :] = jnp.sum(back, axis=0, keepdims=True)

    return pl.pallas_call(
        body, name=name,
        in_specs=[pl.BlockSpec(memory_space=pltpu.VMEM)] * 2,
        out_specs=pl.BlockSpec(memory_space=pltpu.VMEM),
        out_shape=jax.ShapeDtypeStruct((n_heads, NA_T2, LANES), F32),
        compiler_params=pltpu.CompilerParams(vmem_limit_bytes=V7X_VMEM_LIMIT),
    )(dt2, flip)


NA_TQ = 4
NA_TK = NA_TQ + NA_ROWS
NA_KCH = NA_TK // 2


def _na_tile_geometry(t, rows):
    r = t * NA_TQ
    kbase = jnp.clip(r - NA_ROWS // 2, 0, rows - NA_TK)
    starts = [jnp.clip(r + a - NA_ROWS // 2, 0, rows - NA_ROWS) for a in range(NA_TQ)]
    return r, kbase, starts


def _na_tile_mask(kbase, starts, low, inwin):
    half = jnp.where(low, 0, 1)
    cols = []
    for c in range(NA_KCH):
        krow = kbase + 2 * c + half
        cols.append(jnp.concatenate(
            [jnp.where(inwin & (krow >= st) & (krow < st + NA_ROWS), 0.0, NEG) for st in starts], axis=0))
    return jnp.concatenate(cols, axis=1)


def na_masks(rows, dep, name):
    n_tiles = rows // NA_TQ

    def body(dep_ref, o_ref):
        _, low, inwin = _na_iotas()
        for t in range(n_tiles):
            _, kbase, starts = _na_tile_geometry(t, rows)
            o_ref[t] = _na_tile_mask(kbase, starts, low, inwin)

    return pl.pallas_call(
        body, name=name,
        in_specs=[pl.BlockSpec(memory_space=pltpu.VMEM)], out_specs=pl.BlockSpec(memory_space=pltpu.VMEM),
        out_shape=jax.ShapeDtypeStruct((n_tiles, NA_TQ * GRID_W, NA_TK * GRID_W), F32),
        compiler_params=pltpu.CompilerParams(vmem_limit_bytes=V7X_VMEM_LIMIT),
    )(dep)


def _na_tile_index(r, kbase, a, c):
    return jnp.clip(kbase + 2 * c - (r + a) + NA_ROWS, 0, NA_T2 - 1)


def _na_tile_scores(q, k, t2_ref, hh, r, kbase, madd):
    bias = jnp.concatenate(
        [jnp.concatenate([t2_ref[hh, _na_tile_index(r, kbase, a, c)] for a in range(NA_TQ)], axis=0)
         for c in range(NA_KCH)], axis=1)
    return _dotg(q, k, NT) + bias + madd


def _softmax_rows(sc):
    e = jnp.exp(sc - jnp.max(sc, axis=1, keepdims=True))
    return e * (1.0 / jnp.sum(e, axis=1, keepdims=True))


def na_fwd(qa, ka, zq, t2, masks, name):
    s = qa.shape[0]
    rows = s // GRID_W
    n_pairs = NA_WIDTH // LANES
    v_blk0 = (2 * NA_WIDTH) // LANES

    assert rows % NA_TQ == 0 and rows >= NA_TK
    tq, tk = NA_TQ * GRID_W, NA_TK * GRID_W

    def body(q_ref, k_ref, v_ref, t2_ref, m_ref, o_ref, s_scr, p_scr):
        def tile(t, carry):
            r, kbase, _ = _na_tile_geometry(t, rows)
            madd = m_ref[t]
            qr = pl.ds(pl.multiple_of(r * GRID_W, tq), tq)
            kr = pl.ds(pl.multiple_of(kbase * GRID_W, tq), tk)
            for hh in range(2):
                lanes = slice(HEAD_DIM * hh, HEAD_DIM * (hh + 1))
                s_scr[tq * hh:tq * (hh + 1), :] = _na_tile_scores(q_ref[qr, lanes], k_ref[kr, lanes], t2_ref, hh, r,
                                                                  kbase, madd)
            p_scr[...] = _softmax_rows(s_scr[...]).astype(BF16)
            for hh in range(2):
                lanes = slice(HEAD_DIM * hh, HEAD_DIM * (hh + 1))
                o_ref[qr, lanes] = _dot(p_scr[tq * hh:tq * (hh + 1), :], v_ref[kr, lanes]).astype(BF16)
            return carry

        lax.fori_loop(0, rows // NA_TQ, tile, 0, unroll=2)

    col = lambda off: pl.BlockSpec((s, LANES), lambda p, _o=off: (0, _o + p))
    return pl.pallas_call(
        body, name=name, grid=(n_pairs,),
        in_specs=[col(0), col(0), col(v_blk0),
                  pl.BlockSpec((2, NA_T2, GRID_W, LANES), lambda p: (p, 0, 0, 0)),
                  pl.BlockSpec(masks.shape, lambda p: (0, 0, 0), pipeline_mode=ONCE)],
        out_specs=col(0),
        out_shape=jax.ShapeDtypeStruct((s, NA_WIDTH), BF16),
        scratch_shapes=[pltpu.VMEM((2 * tq, tk), F32), pltpu.VMEM((2 * tq, tk), BF16)],
        compiler_params=_params(),
    )(qa, ka, zq, t2, masks)


def na_bwd(qa, ka, zq, t2, masks, o_na, do_na, name):
    s = qa.shape[0]
    rows = s // GRID_W
    n_pairs = NA_WIDTH // LANES
    v_blk0 = (2 * NA_WIDTH) // LANES

    tq, tk = NA_TQ * GRID_W, NA_TK * GRID_W

    def body(q_ref, k_ref, v_ref, t2_ref, m_ref, o_ref, do_ref, dq_ref, dk_ref, dv_ref, dt2_ref):
        dk_ref[...] = jnp.zeros(dk_ref.shape, F32)
        dv_ref[...] = jnp.zeros(dv_ref.shape, F32)
        dt2_ref[...] = jnp.zeros(dt2_ref.shape, F32)

        def tile(t, carry):
            r, kbase, _ = _na_tile_geometry(t, rows)
            madd = m_ref[t]
            qr = pl.ds(pl.multiple_of(r * GRID_W, tq), tq)
            kr = pl.ds(pl.multiple_of(kbase * GRID_W, tq), tk)
            for hh in range(2):
                lanes = slice(HEAD_DIM * hh, HEAD_DIM * (hh + 1))
                q, k, v = q_ref[qr, lanes], k_ref[kr, lanes], v_ref[kr, lanes]
                p = _softmax_rows(_na_tile_scores(q, k, t2_ref, hh, r, kbase, madd))
                do = do_ref[qr, lanes]
                delta = jnp.sum(do.astype(F32) * o_ref[qr, lanes].astype(F32), axis=1, keepdims=True)
                ds = p * (_dotg(do, v, NT) - delta)
                shared = {}
                for a in range(NA_TQ):
                    for c in range(NA_KCH):
                        shared.setdefault(2 * c - a, []).append(
                            ds[GRID_W * a:GRID_W * (a + 1), LANES * c:LANES * (c + 1)])
                for offset, parts in shared.items():
                    e = jnp.clip(offset + kbase - r + NA_ROWS, 0, NA_T2 - 1)
                    dt2_ref[hh, e] = dt2_ref[hh, e] + functools.reduce(jnp.add, parts)
                dsb = ds.astype(BF16)
                dq_ref[qr, lanes] = _dot(dsb, k)
                dk_ref[kr, lanes] = dk_ref[kr, lanes] + _dotg(dsb, q, TN)
                dv_ref[kr, lanes] = dv_ref[kr, lanes] + _dotg(p.astype(BF16), do, TN)
            return carry

        lax.fori_loop(0, rows // NA_TQ, tile, 0, unroll=2)

    col = lambda off: pl.BlockSpec((s, LANES), lambda p, _o=off: (0, _o + p))
    t2spec = pl.BlockSpec((2, NA_T2, GRID_W, LANES), lambda p: (p, 0, 0, 0))
    return pl.pallas_call(
        body, name=name, grid=(n_pairs,),
        in_specs=[col(0), col(0), col(v_blk0), t2spec,
                  pl.BlockSpec(masks.shape, lambda p: (0, 0, 0), pipeline_mode=ONCE), col(0), col(0)],
        out_specs=[col(0), col(0), col(0), t2spec],
        out_shape=[jax.ShapeDtypeStruct((s, NA_WIDTH), F32)] * 3 + [jax.ShapeDtypeStruct(t2.shape, F32)],
        compiler_params=_params(),
    )(qa, ka, zq, t2, masks, o_na, do_na)


def _t5_bucket_map():
    rel = np.arange(3 * SW_BLOCK)[None, :] - SW_BLOCK - np.arange(SW_BLOCK)[:, None]
    nb = REL_BUCKETS // 2
    max_exact = nb // 2
    n = np.abs(rel)
    large = max_exact + (np.log(np.maximum(n, 1) / max_exact)
                         / np.log(REL_MAX_DIST / max_exact) * (nb - max_exact)).astype(np.int32)
    large = np.minimum(large, nb - 1)
    return ((rel > 0) * nb + np.where(n < max_exact, n, large)).astype(np.int32)


def t5_expand(table, bmap, dep, name):
    def body(tab_ref, bm_ref, dep_ref, o_ref):
        bm = bm_ref[...]
        for h in range(SW_HEADS):
            t = jnp.zeros(bm.shape, F32)
            for b in range(REL_BUCKETS):
                t = jnp.where(bm == b, tab_ref[b, h], t)
            o_ref[h] = t

    return pl.pallas_call(
        body, name=name,
        in_specs=[pl.BlockSpec(memory_space=pltpu.SMEM), pl.BlockSpec(memory_space=pltpu.VMEM),
                  pl.BlockSpec(memory_space=pltpu.VMEM)],
        out_specs=pl.BlockSpec(memory_space=pltpu.VMEM),
        out_shape=jax.ShapeDtypeStruct((SW_HEADS,) + bmap.shape, F32),
        compiler_params=pltpu.CompilerParams(vmem_limit_bytes=V7X_VMEM_LIMIT),
    )(table, bmap, dep)


def t5_reduce(dbias_list, bmap, name):
    n = len(dbias_list)

    def body(*refs):
        d_refs, bm_ref, o_ref = refs[:n], refs[n], refs[n + 1]
        bm = bm_ref[...]
        for h in range(SW_HEADS):
            dv = d_refs[0][h]
            for other in d_refs[1:]:
                dv = dv + other[h]
            rows = [jnp.sum(jnp.where(bm == b, dv, 0.0), axis=0, keepdims=True) for b in range(REL_BUCKETS)]
            r = jnp.concatenate(rows, axis=0)
            o_ref[h] = jnp.broadcast_to(jnp.sum(r, axis=1, keepdims=True), (REL_BUCKETS, LANES))

    return pl.pallas_call(
        body, name=name,
        in_specs=[pl.BlockSpec(memory_space=pltpu.VMEM)] * (n + 1),
        out_specs=pl.BlockSpec(memory_space=pltpu.VMEM),
        out_shape=jax.ShapeDtypeStruct((SW_HEADS, REL_BUCKETS, LANES), F32),
        compiler_params=pltpu.CompilerParams(vmem_limit_bytes=V7X_VMEM_LIMIT),
    )(*dbias_list, bmap)


def _sw_mask_iotas():
    a = lax.broadcasted_iota(jnp.int32, (SW_BLOCK, 3 * SW_BLOCK), 0)
    j = lax.broadcasted_iota(jnp.int32, (SW_BLOCK, 3 * SW_BLOCK), 1)
    inwin = jnp.abs(j - SW_BLOCK - a) <= SW_BLOCK
    return j, inwin


SW_STACK = SW_HEADS * SW_BLOCK


def _sw_softmax(sc, sk):
    m = jnp.maximum(jnp.max(sc, axis=1, keepdims=True), sk)
    e = jnp.exp(sc - m)
    es = jnp.exp(sk - m)
    inv = 1.0 / (jnp.sum(e, axis=1, keepdims=True) + es)
    return e * inv, es * inv


def _sw_prologue(k_ref, v_ref, kp, vp, sink_ref, s):
    pad = s + 2 * SW_BLOCK
    zeros = jnp.zeros((SW_BLOCK, SW_KV_WIDTH), BF16)
    kp[0:SW_BLOCK, :] = zeros
    vp[0:SW_BLOCK, :] = zeros
    kp[SW_BLOCK + s:pad, :] = zeros
    vp[SW_BLOCK + s:pad, :] = zeros
    kp[SW_BLOCK:SW_BLOCK + s, :] = k_ref[...]
    vp[SW_BLOCK:SW_BLOCK + s, :] = v_ref[...]
    return jnp.concatenate([jnp.full((SW_BLOCK, 1), sink_ref[h], F32) for h in range(SW_HEADS)], axis=0)


def sw_fwd(qs, ks, zq, t5b, sink, dep, name):
    s = qs.shape[0]
    nb = s // SW_BLOCK
    v_blk = (3 * NA_WIDTH + SW_Q_WIDTH + SW_KV_WIDTH) // LANES
    pad = s + 2 * SW_BLOCK

    def body(q_ref, k_ref, v_ref, b_ref, sink_ref, dep_ref, o_ref, kp, vp, s_scr, p_scr):
        sink_col = _sw_prologue(k_ref, v_ref, kp, vp, sink_ref, s)
        j, inwin = _sw_mask_iotas()

        def blk(n, carry):
            kpos = n * SW_BLOCK - SW_BLOCK + j
            madd = jnp.where(inwin & (kpos >= 0) & (kpos < s), 0.0, NEG)
            q0 = pl.multiple_of(n * SW_BLOCK, SW_BLOCK)
            qr, kr = pl.ds(q0, SW_BLOCK), pl.ds(q0, 3 * SW_BLOCK)
            for h in range(SW_HEADS):
                g = h // SW_REP
                s_scr[SW_BLOCK * h:SW_BLOCK * (h + 1), :] = _dotg(
                    q_ref[qr, HEAD_DIM * h:HEAD_DIM * (h + 1)], kp[kr, HEAD_DIM * g:HEAD_DIM * (g + 1)], NT) + madd
            p, _ = _sw_softmax(s_scr[...] + b_ref[...], sink_col)
            p_scr[...] = p.astype(BF16)
            for h in range(SW_HEADS):
                g = h // SW_REP
                o_ref[qr, HEAD_DIM * h:HEAD_DIM * (h + 1)] = _dot(
                    p_scr[SW_BLOCK * h:SW_BLOCK * (h + 1), :], vp[kr, HEAD_DIM * g:HEAD_DIM * (g + 1)]).astype(BF16)
            return carry

        lax.fori_loop(0, nb, blk, 0, unroll=2)

    return pl.pallas_call(
        body, name=name, grid=(1,),
        in_specs=[_full((s, SW_Q_WIDTH)), _full((s, SW_KV_WIDTH)),
                  pl.BlockSpec((s, SW_KV_WIDTH), lambda i: (0, v_blk)),
                  _full((SW_STACK, 3 * SW_BLOCK)), pl.BlockSpec(memory_space=pltpu.SMEM),
                  _full(dep.shape)],
        out_specs=_full((s, SW_Q_WIDTH)),
        out_shape=jax.ShapeDtypeStruct((s, SW_Q_WIDTH), BF16),
        scratch_shapes=[pltpu.VMEM((pad, SW_KV_WIDTH), BF16), pltpu.VMEM((pad, SW_KV_WIDTH), BF16),
                        pltpu.VMEM((SW_STACK, 3 * SW_BLOCK), F32), pltpu.VMEM((SW_STACK, 3 * SW_BLOCK), BF16)],
        compiler_params=_params(),
    )(qs, ks, zq, t5b, sink, dep)


def sw_bwd(qs, ks, zq, t5b, sink, o_sw, do_sw, name):
    s = qs.shape[0]
    nb = s // SW_BLOCK
    v_blk = (3 * NA_WIDTH + SW_Q_WIDTH + SW_KV_WIDTH) // LANES
    pad = s + 2 * SW_BLOCK

    def body(q_ref, k_ref, v_ref, b_ref, sink_ref, o_ref, do_ref,
             dq_ref, dk_ref, dv_ref, db_ref, dsk_ref, kp, vp, dkp, dvp, s_scr, dp_scr, ds_scr, p_scr):
        sink_col = _sw_prologue(k_ref, v_ref, kp, vp, sink_ref, s)
        dkp[...] = jnp.zeros(dkp.shape, F32)
        dvp[...] = jnp.zeros(dvp.shape, F32)
        db_ref[...] = jnp.zeros(db_ref.shape, F32)
        dsk_ref[...] = jnp.zeros(dsk_ref.shape, F32)
        j, inwin = _sw_mask_iotas()

        def blk(n, carry):
            kpos = n * SW_BLOCK - SW_BLOCK + j
            madd = jnp.where(inwin & (kpos >= 0) & (kpos < s), 0.0, NEG)
            q0 = pl.multiple_of(n * SW_BLOCK, SW_BLOCK)
            qr, kr = pl.ds(q0, SW_BLOCK), pl.ds(q0, 3 * SW_BLOCK)
            deltas = []
            for h in range(SW_HEADS):
                g = h // SW_REP
                hl, kl = slice(HEAD_DIM * h, HEAD_DIM * (h + 1)), slice(HEAD_DIM * g, HEAD_DIM * (g + 1))
                rows = slice(SW_BLOCK * h, SW_BLOCK * (h + 1))
                do = do_ref[qr, hl]
                s_scr[rows, :] = _dotg(q_ref[qr, hl], kp[kr, kl], NT) + madd
                dp_scr[rows, :] = _dotg(do, vp[kr, kl], NT)
                deltas.append(jnp.sum(do.astype(F32) * o_ref[qr, hl].astype(F32), axis=1, keepdims=True))
            delta = jnp.concatenate(deltas, axis=0)
            p, ps = _sw_softmax(s_scr[...] + b_ref[...], sink_col)
            ds = p * (dp_scr[...] - delta)
            db_ref[...] = db_ref[...] + ds
            dsk_ref[...] = dsk_ref[...] - jnp.broadcast_to(ps * delta, (SW_STACK, LANES))
            ds_scr[...] = ds.astype(BF16)
            p_scr[...] = p.astype(BF16)
            for g in range(SW_HEADS // SW_REP):
                kl = slice(HEAD_DIM * g, HEAD_DIM * (g + 1))
                k = kp[kr, kl]
                dkw = jnp.zeros((3 * SW_BLOCK, HEAD_DIM), F32)
                dvw = jnp.zeros((3 * SW_BLOCK, HEAD_DIM), F32)
                for r in range(SW_REP):
                    h = g * SW_REP + r
                    hl, rows = slice(HEAD_DIM * h, HEAD_DIM * (h + 1)), slice(SW_BLOCK * h, SW_BLOCK * (h + 1))
                    dsb = ds_scr[rows, :]
                    dq_ref[qr, hl] = _dot(dsb, k)
                    dkw = dkw + _dotg(dsb, q_ref[qr, hl], TN)
                    dvw = dvw + _dotg(p_scr[rows, :], do_ref[qr, hl], TN)
                dkp[kr, kl] = dkp[kr, kl] + dkw
                dvp[kr, kl] = dvp[kr, kl] + dvw
            return carry

        lax.fori_loop(0, nb, blk, 0)
        dk_ref[...] = dkp[SW_BLOCK:SW_BLOCK + s, :]
        dv_ref[...] = dvp[SW_BLOCK:SW_BLOCK + s, :]

    bias_spec = _full((SW_STACK, 3 * SW_BLOCK))
    return pl.pallas_call(
        body, name=name, grid=(1,),
        in_specs=[_full((s, SW_Q_WIDTH)), _full((s, SW_KV_WIDTH)),
                  pl.BlockSpec((s, SW_KV_WIDTH), lambda i: (0, v_blk)),
                  bias_spec, pl.BlockSpec(memory_space=pltpu.SMEM),
                  _full((s, SW_Q_WIDTH)), _full((s, SW_Q_WIDTH))],
        out_specs=[_full((s, SW_Q_WIDTH)), _full((s, SW_KV_WIDTH)), _full((s, SW_KV_WIDTH)), bias_spec,
                   _full((SW_STACK, LANES))],
        out_shape=[jax.ShapeDtypeStruct((s, SW_Q_WIDTH), F32), jax.ShapeDtypeStruct((s, SW_KV_WIDTH), F32),
                   jax.ShapeDtypeStruct((s, SW_KV_WIDTH), F32),
                   jax.ShapeDtypeStruct((SW_STACK, 3 * SW_BLOCK), F32),
                   jax.ShapeDtypeStruct((SW_STACK, LANES), F32)],
        scratch_shapes=[pltpu.VMEM((pad, SW_KV_WIDTH), BF16), pltpu.VMEM((pad, SW_KV_WIDTH), BF16),
                        pltpu.VMEM((pad, SW_KV_WIDTH), F32), pltpu.VMEM((pad, SW_KV_WIDTH), F32),
                        pltpu.VMEM((SW_STACK, 3 * SW_BLOCK), F32), pltpu.VMEM((SW_STACK, 3 * SW_BLOCK), F32),
                        pltpu.VMEM((SW_STACK, 3 * SW_BLOCK), BF16), pltpu.VMEM((SW_STACK, 3 * SW_BLOCK), BF16)],
        compiler_params=_params(),
    )(qs, ks, zq, t5b, sink, o_sw, do_sw)


def merge_out(x, o_na, o_sw, gt, wbna_t, wbsw_t, wout, name):
    s, d = x.shape
    tm = _row_tile(s)

    def body(x_ref, ona_ref, osw_ref, gt_ref, wna_ref, wsw_ref, wo_ref, xo_ref, ana_ref, asw_ref, mg_ref):
        a_na = _dotg(ona_ref[...], wna_ref[...], NT)
        a_sw = _dotg(osw_ref[...], wsw_ref[...], NT)
        g_na, g_sw = gt_ref[:, 0:d].astype(F32), gt_ref[:, d:2 * d].astype(F32)
        ana_ref[...] = (a_na * g_na * (1.0 - g_na)).astype(BF16)
        asw_ref[...] = (a_sw * g_sw * (1.0 - g_sw)).astype(BF16)
        merged = (g_na * a_na + g_sw * a_sw).astype(BF16)
        mg_ref[...] = merged
        xo_ref[...] = x_ref[...] + _dot(merged, wo_ref[...])

    return pl.pallas_call(
        body, name=name, grid=(s // tm,),
        in_specs=[_rows(tm, d), _rows(tm, 512), _rows(tm, 512), _rows(tm, 2 * d),
                  _mat(*wbna_t), _mat(*wbsw_t), _mat(*wout)],
        out_specs=[_rows(tm, d)] * 4,
        out_shape=[jax.ShapeDtypeStruct((s, d), F32)] + [jax.ShapeDtypeStruct((s, d), BF16)] * 3,
        compiler_params=_params(),
    )(x, o_na, o_sw, gt, wbna_t[0], wbsw_t[0], wout[0])


def mix_bwd_out(dx, gt, a_na, a_sw, wbna_t, wbsw_t, wout, dep, name):
    s, d = dx.shape
    tm = _row_tile(s)

    def body(dx_ref, gt_ref, ana_ref, asw_ref, wna_ref, wsw_ref, wo_ref, dep_ref,
             dxb_ref, dzg_ref, dana_ref, dasw_ref, dona_ref, dosw_ref, dbg_ref):
        @pl.when(pl.program_id(0) == 0)
        def _():
            dbg_ref[...] = jnp.zeros(dbg_ref.shape, F32)

        dxb = dx_ref[...].astype(BF16)
        dxb_ref[...] = dxb
        dm = _dotg(dxb, wo_ref[...], NT)
        for i, (a_ref, da_ref, w_ref, do_ref) in enumerate(
                [(ana_ref, dana_ref, wna_ref, dona_ref), (asw_ref, dasw_ref, wsw_ref, dosw_ref)]):
            gi = gt_ref[:, i * d:(i + 1) * d].astype(F32)
            da = (dm * gi).astype(BF16)
            da_ref[...] = da
            do_ref[...] = _dot(da, w_ref[...]).astype(BF16)
            dzg = dm * a_ref[...].astype(F32)
            dzg_ref[:, i * d:(i + 1) * d] = dzg.astype(BF16)
            dbg_ref[:, i * d:(i + 1) * d] = dbg_ref[:, i * d:(i + 1) * d] + jnp.sum(dzg, axis=0, keepdims=True)

    return pl.pallas_call(
        body, name=name, grid=(s // tm,),
        in_specs=[_rows(tm, d), _rows(tm, 2 * d), _rows(tm, d), _rows(tm, d),
                  _mat(*wbna_t), _mat(*wbsw_t), _mat(*wout), _full(dep.shape)],
        out_specs=[_rows(tm, d), _rows(tm, 2 * d), _rows(tm, d), _rows(tm, d), _rows(tm, 512), _rows(tm, 512),
                   _full((1, 2 * d))],
        out_shape=[jax.ShapeDtypeStruct((s, d), BF16), jax.ShapeDtypeStruct((s, 2 * d), BF16),
                   jax.ShapeDtypeStruct((s, d), BF16), jax.ShapeDtypeStruct((s, d), BF16),
                   jax.ShapeDtypeStruct((s, 512), BF16), jax.ShapeDtypeStruct((s, 512), BF16),
                   jax.ShapeDtypeStruct((1, 2 * d), F32)],
        compiler_params=_params(),
    )(dx, gt, a_na, a_sw, wbna_t[0], wbsw_t[0], wout[0], dep)


def qk_norm_bwd(dqa, dka, dva, dqs, dks, dvs, zq, dzg, gq_na, gk_na, gq_sw, gk_sw, bd, name):
    s = zq.shape[0]
    d2 = dzg.shape[1]
    n_in = QKV_WIDTH + d2
    tm = _row_tile(s)

    def body(dqa_ref, dka_ref, dva_ref, dqs_ref, dks_ref, dvs_ref, zq_ref, dzg_ref,
             gqa_ref, gka_ref, gqs_ref, gks_ref, bd_ref, dz_ref, dgqa_ref, dgka_ref, dgqs_ref, dgks_ref):
        @pl.when(pl.program_id(0) == 0)
        def _():
            for r in (dgqa_ref, dgka_ref, dgqs_ref, dgks_ref):
                r[...] = jnp.zeros(r.shape, F32)

        bd512 = bd_ref[...]
        bd128 = bd_ref[0:SW_KV_WIDTH, 0:SW_KV_WIDTH]

        def one(c0, c1, dy_ref, g_ref, dg_ref, bdm, scale):
            z = zq_ref[:, c0:c1].astype(F32)
            r = lax.rsqrt(_group_mean(z * z, bdm) + EPS)
            zh = z * r
            dy = dy_ref[...] * scale
            dyg = dy * g_ref[...]
            dz = r * (dyg - zh * _group_mean(dyg * zh, bdm))
            dz_ref[:, c0:c1] = dz.astype(BF16)
            dg_ref[...] = dg_ref[...] + jnp.sum(dy * zh, axis=0, keepdims=True)

        one(0, 512, dqa_ref, gqa_ref, dgqa_ref, bd512, SCALE)
        one(512, 1024, dka_ref, gka_ref, dgka_ref, bd512, 1.0)
        dz_ref[:, 1024:1536] = dva_ref[...].astype(BF16)
        one(1536, 2048, dqs_ref, gqs_ref, dgqs_ref, bd512, SCALE)
        one(2048, 2176, dks_ref, gks_ref, dgks_ref, bd128, 1.0)
        dz_ref[:, 2176:2304] = dvs_ref[...].astype(BF16)
        dz_ref[:, QKV_WIDTH:n_in] = dzg_ref[...]

    return pl.pallas_call(
        body, name=name, grid=(s // tm,),
        in_specs=[_rows(tm, 512), _rows(tm, 512), _rows(tm, 512), _rows(tm, 512), _rows(tm, 128), _rows(tm, 128),
                  _rows(tm, QKV_WIDTH), _rows(tm, d2),
                  _full((1, 512)), _full((1, 512)), _full((1, 512)), _full((1, 128)), _full((MXU_TILE, MXU_TILE))],
        out_specs=[_rows(tm, n_in), _full((1, 512)), _full((1, 512)), _full((1, 512)), _full((1, 128))],
        out_shape=[jax.ShapeDtypeStruct((s, n_in), BF16)] + [jax.ShapeDtypeStruct((1, 512), F32)] * 3
                  + [jax.ShapeDtypeStruct((1, 128), F32)],
        compiler_params=_params(),
    )(dqa, dka, dva, dqs, dks, dvs, zq, dzg, gq_na, gk_na, gq_sw, gk_sw, bd)


def ffn_bwd_act(dx, wd, hg, hu, name):
    s, d = dx.shape
    f = wd[0].shape[1]
    tm = _row_tile(s)
    fc = _col_chunk(f)

    def body(dx_ref, w_ref, hg_ref, hu_ref, dxb_ref, dhg_ref, dhu_ref):
        dxv = dx_ref[...]
        dxb_ref[...] = dxv.astype(BF16)
        half = (0.5 * dxv).astype(BF16)
        for c0 in range(0, f, fc):
            dact = _dotg(half, w_ref[c0:c0 + fc, :], NT)
            dhu_ref[:, c0:c0 + fc] = (dact * hu_ref[:, c0:c0 + fc].astype(F32)).astype(BF16)
            dhg_ref[:, c0:c0 + fc] = (dact * hg_ref[:, c0:c0 + fc].astype(F32)).astype(BF16)

    return pl.pallas_call(
        body, name=name, grid=(s // tm,),
        in_specs=[_rows(tm, d), _mat(*wd), _rows(tm, f), _rows(tm, f)],
        out_specs=[_rows(tm, d), _rows(tm, f), _rows(tm, f)],
        out_shape=[jax.ShapeDtypeStruct((s, d), BF16), jax.ShapeDtypeStruct((s, f), BF16),
                   jax.ShapeDtypeStruct((s, f), BF16)],
        compiler_params=_params(),
    )(dx, wd[0], hg, hu)


def proj_bwd_norm(acts, weights, x, gain, dx, dep, name):
    s, d = x.shape
    tm = min(_row_tile(s), 256)
    n = len(acts)

    def body(*refs):
        a_refs, w_refs = refs[:n], refs[n:2 * n]
        x_ref, g_ref, dx_ref, _, o_ref, dg_ref = refs[2 * n:]

        @pl.when(pl.program_id(0) == 0)
        def _():
            dg_ref[...] = jnp.zeros(dg_ref.shape, F32)

        dxn = _dot(a_refs[0][...], w_refs[0][...])
        for a_ref, w_ref in zip(a_refs[1:], w_refs[1:]):
            dxn = dxn + _dot(a_ref[...], w_ref[...])
        xv = x_ref[...]
        r = _rstd(xv)
        xh = xv * r
        dxh = dxn * g_ref[...]
        o_ref[...] = dx_ref[...] + r * (dxh - xh * jnp.mean(dxh * xh, axis=-1, keepdims=True))
        dg_ref[...] = dg_ref[...] + jnp.sum(dxn * xh, axis=0, keepdims=True)

    return pl.pallas_call(
        body, name=name, grid=(s // tm,),
        in_specs=[_rows(tm, a.shape[1]) for a in acts] + [_mat(*w) for w in weights]
                 + [_rows(tm, d), _full((1, d)), _rows(tm, d), _full(dep.shape)],
        out_specs=[_rows(tm, d), _full((1, d))],
        out_shape=[jax.ShapeDtypeStruct((s, d), F32), jax.ShapeDtypeStruct((1, d), F32)],
        compiler_params=_params(),
    )(*acts, *[w[0] for w in weights], x, gain, dx, dep)


def tn_matmul(products, name):
    s, n = products[0][0].shape
    tn = _tn_tile(n) if len(products) == 1 else _col_chunk(n)
    rhs = []
    for _, b, _ in products:
        if not any(b is seen for seen in rhs):
            rhs.append(b)
    which = [next(i for i, seen in enumerate(rhs) if b is seen) for _, b, _ in products]
    npr, nr = len(products), len(rhs)

    def body(*refs):
        a_refs, b_refs, o_refs = refs[:npr], refs[npr:npr + nr], refs[npr + nr:]
        for i, (_, _, scale) in enumerate(products):
            o_refs[i][...] = (scale * _dotg(a_refs[i][...], b_refs[which[i]][...], TN)).astype(BF16)

    return pl.pallas_call(
        body, name=name, grid=(n // tn,),
        in_specs=[pl.BlockSpec((s, tn), lambda i: (0, i))] * npr
                 + [pl.BlockSpec(b.shape, lambda i: (0, 0), pipeline_mode=ONCE) for b in rhs],
        out_specs=[pl.BlockSpec((tn, b.shape[1]), lambda i: (i, 0)) for _, b, _ in products],
        out_shape=[jax.ShapeDtypeStruct((n, b.shape[1]), BF16) for _, b, _ in products],
        compiler_params=_params(),
    )(*[a for a, _, _ in products], *rhs)


def _mesh_pos():
    return lax.axis_index("x"), lax.axis_index("y"), lax.axis_index("c")


def _peers():
    x, y, c = _mesh_pos()
    peers = []
    for rel in range(1, N_DEV):
        peers.append((1 - x if rel & 4 else x, 1 - y if rel & 2 else y, 1 - c if rel & 1 else c))
    return 4 * x + 2 * y + c, peers


HBM_SPEC = pl.BlockSpec(memory_space=pltpu.HBM)
SEM_SPEC = pl.BlockSpec(memory_space=pltpu.SEMAPHORE)


def _split_call(body, name, thru, n_sems, extra=(), with_token=True):
    hbm = lambda t: pltpu.with_memory_space_constraint(t, pltpu.HBM)
    effect = pltpu.CompilerParams(has_side_effects=pltpu.SideEffectType.DATAFLOW_SIDE_EFFECTING)
    nt = len(thru)
    thru_shapes = [pltpu.HBM(t.shape, t.dtype) for t in thru]
    if with_token:
        (after,) = extra
        outs = pl.pallas_call(
            body, name=name, in_specs=[HBM_SPEC] * nt + [pl.BlockSpec(memory_space=pl.ANY)],
            out_specs=[SEM_SPEC] * len(n_sems) + [HBM_SPEC] * nt + [pl.BlockSpec(memory_space=pltpu.VMEM)],
            out_shape=[pltpu.SemaphoreType.DMA((k,)) for k in n_sems] + thru_shapes
                      + [jax.ShapeDtypeStruct((8, LANES), F32)],
            input_output_aliases={i: len(n_sems) + i for i in range(nt)}, compiler_params=effect,
        )(*[hbm(t) for t in thru], after)
        return outs[:len(n_sems)], outs[len(n_sems):-1], outs[-1]
    return pl.pallas_call(
        body, name=name,
        in_specs=[HBM_SPEC] * nt + [SEM_SPEC] * len(n_sems) + [pl.BlockSpec(memory_space=pl.ANY)],
        out_specs=[HBM_SPEC] * nt, out_shape=thru_shapes,
        input_output_aliases={i: i for i in range(nt)}, compiler_params=effect,
    )(*thru, *extra)


def _gather_targets():
    x, y, c = _mesh_pos()
    return 4 * x + 2 * y + c, [(x, y, 1 - c), (1 - x, y, c), (x, 1 - y, c), (1 - x, 1 - y, c)]


def gather_start(shards, after, name):
    n = len(shards)
    zones = [lax.empty((w.shape[0], N_DEV) + w.shape[1:], w.dtype) for w in shards]

    def body(*refs):
        ins, zs = refs[:n], refs[n:2 * n]
        send_sems, recv_sems, local_sems = refs[2 * n + 1:2 * n + 4]
        token = refs[-1]
        me, targets = _gather_targets()
        for a in range(n):
            pltpu.make_async_copy(ins[a], zs[a].at[:, me], local_sems.at[a]).start()
            for k, to in enumerate(targets):
                pltpu.make_async_remote_copy(
                    src_ref=ins[a], dst_ref=zs[a].at[:, me], send_sem=send_sems.at[4 * a + k],
                    recv_sem=recv_sems.at[4 * a + k], device_id=to, device_id_type=MESH).start()
        token[...] = jnp.zeros(token.shape, F32)

    sems, thru, token = _split_call(body, name, list(shards) + zones, (4 * n, 4 * n, n), extra=(after,))
    return (sems, thru, n), token


def gather_wait(started, after, name):
    sems, thru, n = started

    def body(*refs):
        zs = refs[n:2 * n]
        send_sems, recv_sems, local_sems = refs[2 * n:2 * n + 3]
        _, targets = _gather_targets()
        for a in range(n):
            for k, to in enumerate(targets):
                cp = pltpu.make_async_remote_copy(
                    src_ref=zs[a].at[:, 0], dst_ref=zs[a].at[:, 0], send_sem=send_sems.at[4 * a + k],
                    recv_sem=recv_sems.at[4 * a + k], device_id=to, device_id_type=MESH)
                cp.wait_send()
                cp.wait_recv()
            pltpu.make_async_copy(zs[a].at[:, 0], zs[a].at[:, 0], local_sems.at[a]).wait()

    return _split_call(body, name, thru, (4 * n, 4 * n, n), extra=(*sems, after), with_token=False)[n:]


def forward_start(zones, after, name):
    n = len(zones)

    def body(*refs):
        zs = refs[:n]
        send_sems, recv_sems = refs[n + 1:n + 3]
        token = refs[-1]
        x, y, c = _mesh_pos()
        for a in range(n):
            for j, chip in enumerate([(1 - x, y), (x, 1 - y), (1 - x, 1 - y)]):
                blk = zs[a].at[:, 4 * chip[0] + 2 * chip[1] + c]
                pltpu.make_async_remote_copy(
                    src_ref=blk, dst_ref=blk, send_sem=send_sems.at[3 * a + j], recv_sem=recv_sems.at[3 * a + j],
                    device_id=(x, y, 1 - c), device_id_type=MESH).start()
        token[...] = jnp.zeros(token.shape, F32)

    sems, thru, token = _split_call(body, name, list(zones), (3 * n, 3 * n), extra=(after,))
    return (sems, thru, n), token


def forward_wait(started, after, name):
    sems, thru, n = started

    def body(*refs):
        zs = refs[:n]
        send_sems, recv_sems = refs[n:n + 2]
        x, y, c = _mesh_pos()
        for a in range(n):
            for j in range(3):
                cp = pltpu.make_async_remote_copy(
                    src_ref=zs[a].at[:, 0], dst_ref=zs[a].at[:, 0], send_sem=send_sems.at[3 * a + j],
                    recv_sem=recv_sems.at[3 * a + j], device_id=(x, y, 1 - c), device_id_type=MESH)
                cp.wait_send()
                cp.wait_recv()

    return _split_call(body, name, thru, (3 * n, 3 * n), extra=(*sems, after), with_token=False)


def scatter_start(groups, name):
    n = len(groups)
    flat = [g for grp in groups for g in grp]
    nf = len(flat)
    offs = np.cumsum([0] + [len(grp) for grp in groups])
    lands = [lax.empty((N_DEV, len(grp)) + grp[0].shape[1:], grp[0].dtype) for grp in groups]

    def body(*refs):
        ins, zones = refs[:nf], refs[nf:nf + n]
        send_sems, recv_sems, local_sems = refs[nf + n:nf + n + 3]
        token = refs[-1]
        me, peers = _peers()
        for a in range(n):
            for w in range(len(groups[a])):
                pltpu.make_async_copy(ins[offs[a] + w].at[me], zones[a].at[me, w], local_sems.at[a]).start()
        for k, peer in enumerate(peers):
            p_id = 4 * peer[0] + 2 * peer[1] + peer[2]
            for a in range(n):
                for w in range(len(groups[a])):
                    pltpu.make_async_remote_copy(
                        src_ref=ins[offs[a] + w].at[p_id], dst_ref=zones[a].at[me, w],
                        send_sem=send_sems.at[7 * a + k], recv_sem=recv_sems.at[7 * a + k],
                        device_id=peer, device_id_type=MESH).start()
        token[...] = jnp.zeros(token.shape, F32)

    hbm = lambda t: pltpu.with_memory_space_constraint(t, pltpu.HBM)
    outs = pl.pallas_call(
        body, name=name,
        in_specs=[HBM_SPEC] * (nf + n),
        out_specs=[SEM_SPEC] * 3 + [HBM_SPEC] * (nf + n) + [pl.BlockSpec(memory_space=pltpu.VMEM)],
        out_shape=[pltpu.SemaphoreType.DMA((7 * n,)), pltpu.SemaphoreType.DMA((7 * n,)), pltpu.SemaphoreType.DMA((n,))]
                  + [pltpu.HBM(t.shape, t.dtype) for t in flat + lands]
                  + [jax.ShapeDtypeStruct((8, LANES), F32)],
        input_output_aliases={i: 3 + i for i in range(nf + n)},
        compiler_params=pltpu.CompilerParams(has_side_effects=pltpu.SideEffectType.DATAFLOW_SIDE_EFFECTING),
    )(*[hbm(t) for t in flat], *[hbm(t) for t in lands])
    sems, thru, token = outs[:3], outs[3:3 + nf + n], outs[-1]
    return (sems, thru, [len(grp) for grp in groups]), token


def scatter_wait(started, after, name):
    (send_sems, recv_sems, local_sems), thru, sizes = started
    n = len(sizes)
    nf = len(thru) - n

    def body(*refs):
        zones = refs[nf:nf + n]
        s_sems, r_sems, l_sems = refs[nf + n:nf + n + 3]
        me, peers = _peers()
        for a in range(n):
            for k, peer in enumerate(peers):
                cp = pltpu.make_async_remote_copy(
                    src_ref=zones[a].at[0], dst_ref=zones[a].at[0],
                    send_sem=s_sems.at[7 * a + k], recv_sem=r_sems.at[7 * a + k], device_id=peer,
                    device_id_type=MESH)
                cp.wait_send()
                cp.wait_recv()
            pltpu.make_async_copy(zones[a].at[0], zones[a].at[0], l_sems.at[a]).wait()

    outs = pl.pallas_call(
        body, name=name,
        in_specs=[HBM_SPEC] * (nf + n) + [SEM_SPEC] * 3 + [pl.BlockSpec(memory_space=pl.ANY)],
        out_specs=[HBM_SPEC] * (nf + n),
        out_shape=[pltpu.HBM(t.shape, t.dtype) for t in thru],
        input_output_aliases={i: i for i in range(nf + n)},
        compiler_params=pltpu.CompilerParams(has_side_effects=pltpu.SideEffectType.DATAFLOW_SIDE_EFFECTING),
    )(*thru, send_sems, recv_sems, local_sems, after)
    return outs[nf:]


def pair_start(grads, after, name):
    nw = len(grads)
    land = lax.empty((4, nw) + grads[0].shape[1:], grads[0].dtype)

    def body(*refs):
        ins, zone = refs[:nw], refs[nw]
        send_sems, recv_sems = refs[nw + 2:nw + 4]
        x, y, c = _mesh_pos()
        for j in range(4):
            for w in range(nw):
                pltpu.make_async_remote_copy(
                    src_ref=ins[w].at[2 * j + (1 - c)], dst_ref=zone.at[j, w], send_sem=send_sems.at[0],
                    recv_sem=recv_sems.at[0], device_id=(x, y, 1 - c), device_id_type=MESH).start()
        refs[-1][...] = jnp.zeros(refs[-1].shape, F32)

    sems, thru, token = _split_call(body, name, list(grads) + [land], (1, 1), extra=(after,))
    return (sems, thru, nw), token


def pair_wait(started, after, name):
    sems, thru, nw = started

    def body(*refs):
        zone = refs[nw]
        send_sems, recv_sems = refs[nw + 1:nw + 3]
        x, y, c = _mesh_pos()
        cp = pltpu.make_async_remote_copy(src_ref=zone, dst_ref=zone, send_sem=send_sems.at[0],
                                          recv_sem=recv_sems.at[0], device_id=(x, y, 1 - c), device_id_type=MESH)
        cp.wait_send()
        cp.wait_recv()

    outs = _split_call(body, name, thru, (1, 1), extra=(*sems, after), with_token=False)
    return outs[:nw], outs[nw]


def pair_sum(grads, land, name):
    nw = len(grads)
    _, r, c_dim = grads[0].shape

    def body(*refs):
        g_refs, l_ref, o_ref = refs[:nw], refs[nw], refs[nw + 1]
        core = lax.axis_index("c")
        for w in range(nw):
            o_ref[0, w] = (g_refs[w][0, core].astype(F32) + l_ref[0, w].astype(F32)).astype(BF16)

    return pl.pallas_call(
        body, name=name, grid=(4,),
        in_specs=[pl.BlockSpec((1, 2, r, c_dim), lambda j: (j, 0, 0, 0))] * nw
                 + [pl.BlockSpec((1, nw, r, c_dim), lambda j: (j, 0, 0, 0))],
        out_specs=pl.BlockSpec((1, nw, r, c_dim), lambda j: (j, 0, 0, 0)),
        out_shape=jax.ShapeDtypeStruct((4, nw, r, c_dim), BF16),
        compiler_params=_params(),
    )(*[g.reshape(4, 2, r, c_dim) for g in grads], land)


def _other_chips():
    x, y, c = _mesh_pos()
    chips = []
    for rel in range(1, 4):
        px, py = (1 - x if rel & 2 else x), (1 - y if rel & 1 else y)
        chips.append((px, py, 2 * px + py))
    return 2 * x + y, c, chips


def chip_start(pair_sums, after, name):
    land = lax.empty(pair_sums.shape, pair_sums.dtype)

    def body(*refs):
        h_ref, zone = refs[0], refs[1]
        send_sems, recv_sems, local_sem = refs[3:6]
        mine, c, chips = _other_chips()
        pltpu.make_async_copy(h_ref.at[mine], zone.at[mine], local_sem.at[0]).start()
        for k, (px, py, j) in enumerate(chips):
            pltpu.make_async_remote_copy(
                src_ref=h_ref.at[j], dst_ref=zone.at[mine], send_sem=send_sems.at[k], recv_sem=recv_sems.at[k],
                device_id=(px, py, c), device_id_type=MESH).start()
        refs[-1][...] = jnp.zeros(refs[-1].shape, F32)

    sems, thru, token = _split_call(body, name, [pair_sums, land], (3, 3, 1), extra=(after,))
    return (sems, thru), token


def chip_wait(started, after, name):
    sems, thru = started

    def body(*refs):
        zone = refs[1]
        send_sems, recv_sems, local_sem = refs[2:5]
        _, c, chips = _other_chips()
        for k, (px, py, _) in enumerate(chips):
            cp = pltpu.make_async_remote_copy(
                src_ref=zone.at[0], dst_ref=zone.at[0], send_sem=send_sems.at[k], recv_sem=recv_sems.at[k],
                device_id=(px, py, c), device_id_type=MESH)
            cp.wait_send()
            cp.wait_recv()
        pltpu.make_async_copy(zone.at[0], zone.at[0], local_sem.at[0]).wait()

    return _split_call(body, name, thru, (3, 3, 1), extra=(*sems, after), with_token=False)[1]


def share_start(parts, after, name):
    n = len(parts)
    zones = [lax.empty((N_DEV,) + p.shape, p.dtype) for p in parts]

    def body(*refs):
        ins, zs = refs[:n], refs[n:2 * n]
        send_sems, recv_sems, local_sems = refs[2 * n + 1:2 * n + 4]
        me, peers = _peers()
        for i in range(n):
            pltpu.make_async_copy(ins[i], zs[i].at[me], local_sems.at[i]).start()
            for k, peer in enumerate(peers):
                pltpu.make_async_remote_copy(
                    src_ref=ins[i], dst_ref=zs[i].at[me], send_sem=send_sems.at[7 * i + k],
                    recv_sem=recv_sems.at[7 * i + k], device_id=peer, device_id_type=MESH).start()
        refs[-1][...] = jnp.zeros(refs[-1].shape, F32)

    sems, thru, token = _split_call(body, name, list(parts) + zones, (7 * n, 7 * n, n), extra=(after,))
    return (sems, thru, n), token


def share_wait(started, after, name):
    sems, thru, n = started

    def body(*refs):
        zs = refs[n:2 * n]
        send_sems, recv_sems, local_sems = refs[2 * n:2 * n + 3]
        _, peers = _peers()
        for i in range(n):
            for k, peer in enumerate(peers):
                cp = pltpu.make_async_remote_copy(
                    src_ref=zs[i].at[0], dst_ref=zs[i].at[0], send_sem=send_sems.at[7 * i + k],
                    recv_sem=recv_sems.at[7 * i + k], device_id=peer, device_id_type=MESH)
                cp.wait_send()
                cp.wait_recv()
            pltpu.make_async_copy(zs[i].at[0], zs[i].at[0], local_sems.at[i]).wait()

    return _split_call(body, name, thru, (7 * n, 7 * n, n), extra=(*sems, after), with_token=False)[n:]


def _adamw_math(w, g, m, v):
    m = ADAM_B1 * m + (1.0 - ADAM_B1) * g
    v = ADAM_B2 * v + (1.0 - ADAM_B2) * (g * g)
    m_hat = m / (1.0 - ADAM_B1 ** ADAM_STEP)
    v_hat = v / (1.0 - ADAM_B2 ** ADAM_STEP)
    delta = -ADAM_LR * (m_hat / (jnp.sqrt(v_hat) + ADAM_EPS) + ADAM_WD * w)
    return delta, m, v


ADAMW_BLOCK_BYTES = 24 * 1024 * 1024


def adamw_layer(zone, layer, items, after, name):
    n_src, nw, r, c = zone.shape
    depth = items[0][0].shape[0]
    prevs = [p if p is not None else tuple(lax.empty((depth, r, c), F32) for _ in range(4)) for _, _, _, p in items]
    row_bytes = 2 * nw * c * (2 * n_src + 4 * 7)
    tr = max(t for t in range(8, r + 1, 8) if r % t == 0 and t * row_bytes <= ADAMW_BLOCK_BYTES)

    def body(z_ref, *rest):
        ins, outs = rest[:3 * nw], rest[7 * nw + 1:]
        for i in range(nw):
            g = z_ref[0, i].astype(F32)
            for src in range(1, n_src):
                g = g + z_ref[src, i].astype(F32)
            g_ref, d_ref, mo_ref, vo_ref = outs[4 * i:4 * i + 4]
            w_ref, m_ref, v_ref = ins[3 * i:3 * i + 3]
            g_ref[...] = g
            d_ref[...], mo_ref[...], vo_ref[...] = _adamw_math(w_ref[...], g, m_ref[...], v_ref[...])

    rows = pl.BlockSpec((None, tr, c), lambda i: (layer, i, 0))
    anywhere = pl.BlockSpec(memory_space=pl.ANY)
    outs = pl.pallas_call(
        body, name=name, grid=(r // tr,),
        in_specs=[pl.BlockSpec((n_src, nw, tr, c), lambda i: (0, 0, i, 0))] + [rows] * (3 * nw)
                 + [anywhere] * (4 * nw + 1),
        out_specs=[rows] * (4 * nw),
        out_shape=[jax.ShapeDtypeStruct((depth, r, c), F32)] * (4 * nw),
        input_output_aliases={1 + 3 * nw + k: k for k in range(4 * nw)},
        compiler_params=_params(),
    )(zone, *[t for w, m, v, _ in items for t in (w, m, v)], *[t for p in prevs for t in p], after)
    return [tuple(outs[4 * i:4 * i + 4]) for i in range(nw)]


def adamw_small(ws, recvs, ms, vs, name):
    n = len(ws)

    def body(*refs):
        w_refs, r_refs, m_refs, v_refs = (refs[i * n:(i + 1) * n] for i in range(4))
        g_refs, d_refs, mo_refs, vo_refs = (refs[(4 + i) * n:(5 + i) * n] for i in range(4))
        for i in range(n):
            g = r_refs[i][0]
            for src in range(1, N_DEV):
                g = g + r_refs[i][src]
            g_refs[i][...] = g
            d_refs[i][...], mo_refs[i][...], vo_refs[i][...] = _adamw_math(w_refs[i][...], g, m_refs[i][...],
                                                                            v_refs[i][...])

    vm = pl.BlockSpec(memory_space=pltpu.VMEM)
    outs = pl.pallas_call(
        body, name=name, in_specs=[vm] * (4 * n), out_specs=[vm] * (4 * n),
        out_shape=[jax.ShapeDtypeStruct(w.shape, F32) for w in ws] * 4,
        compiler_params=pltpu.CompilerParams(vmem_limit_bytes=V7X_VMEM_LIMIT),
    )(*ws, *recvs, *ms, *vs)
    return [outs[i * n:(i + 1) * n] for i in range(4)]


SMALL_NAMES = ("ffn1_norm", "mix_norm", "ffn2_norm", "b_gate", "na_q_norm", "na_k_norm", "sw_q_norm", "sw_k_norm",
               "na_rpb", "sw_sink", "t5_rel_table")


def kernel(x, ffn1_norm, ffn1_w_gate, ffn1_w_up, ffn1_w_down, mix_norm, w_in, b_gate, na_q_norm, na_k_norm, na_rpb, sw_q_norm, sw_k_norm, sw_sink, t5_rel_table, w_branch_na, w_branch_sw, w_out, ffn2_norm, ffn2_w_gate, ffn2_w_up, ffn2_w_down, loss_target, m_ffn1_norm, m_ffn1_w_gate, m_ffn1_w_up, m_ffn1_w_down, m_mix_norm, m_w_in, m_b_gate, m_na_q_norm, m_na_k_norm, m_na_rpb, m_sw_q_norm, m_sw_k_norm, m_sw_sink, m_t5_rel_table, m_w_branch_na, m_w_branch_sw, m_w_out, m_ffn2_norm, m_ffn2_w_gate, m_ffn2_w_up, m_ffn2_w_down, v_ffn1_norm, v_ffn1_w_gate, v_ffn1_w_up, v_ffn1_w_down, v_mix_norm, v_w_in, v_b_gate, v_na_q_norm, v_na_k_norm, v_na_rpb, v_sw_q_norm, v_sw_k_norm, v_sw_sink, v_t5_rel_table, v_w_branch_na, v_w_branch_sw, v_w_out, v_ffn2_norm, v_ffn2_w_gate, v_ffn2_w_up, v_ffn2_w_down):
    weights = dict(ffn1_norm=ffn1_norm, ffn1_w_gate=ffn1_w_gate, ffn1_w_up=ffn1_w_up, ffn1_w_down=ffn1_w_down,
                   mix_norm=mix_norm, w_in=w_in, b_gate=b_gate, na_q_norm=na_q_norm, na_k_norm=na_k_norm,
                   na_rpb=na_rpb, sw_q_norm=sw_q_norm, sw_k_norm=sw_k_norm, sw_sink=sw_sink,
                   t5_rel_table=t5_rel_table, w_branch_na=w_branch_na, w_branch_sw=w_branch_sw, w_out=w_out,
                   ffn2_norm=ffn2_norm, ffn2_w_gate=ffn2_w_gate, ffn2_w_up=ffn2_w_up, ffn2_w_down=ffn2_w_down)
    mom_m = dict(ffn1_norm=m_ffn1_norm, ffn1_w_gate=m_ffn1_w_gate, ffn1_w_up=m_ffn1_w_up, ffn1_w_down=m_ffn1_w_down,
                 mix_norm=m_mix_norm, w_in=m_w_in, b_gate=m_b_gate, na_q_norm=m_na_q_norm, na_k_norm=m_na_k_norm,
                 na_rpb=m_na_rpb, sw_q_norm=m_sw_q_norm, sw_k_norm=m_sw_k_norm, sw_sink=m_sw_sink,
                 t5_rel_table=m_t5_rel_table, w_branch_na=m_w_branch_na, w_branch_sw=m_w_branch_sw, w_out=m_w_out,
                 ffn2_norm=m_ffn2_norm, ffn2_w_gate=m_ffn2_w_gate, ffn2_w_up=m_ffn2_w_up, ffn2_w_down=m_ffn2_w_down)
    mom_v = dict(ffn1_norm=v_ffn1_norm, ffn1_w_gate=v_ffn1_w_gate, ffn1_w_up=v_ffn1_w_up, ffn1_w_down=v_ffn1_w_down,
                 mix_norm=v_mix_norm, w_in=v_w_in, b_gate=v_b_gate, na_q_norm=v_na_q_norm, na_k_norm=v_na_k_norm,
                 na_rpb=v_na_rpb, sw_q_norm=v_sw_q_norm, sw_k_norm=v_sw_k_norm, sw_sink=v_sw_sink,
                 t5_rel_table=v_t5_rel_table, w_branch_na=v_w_branch_na, w_branch_sw=v_w_branch_sw, w_out=v_w_out,
                 ffn2_norm=v_ffn2_norm, ffn2_w_gate=v_ffn2_w_gate, ffn2_w_up=v_ffn2_w_up, ffn2_w_down=v_ffn2_w_down)
    order = list(weights)

    depth = ffn1_norm.shape[0]
    s, d = x.shape[1], x.shape[2]
    xs = x[0]
    tr = lambda w: jnp.swapaxes(w, -1, -2)

    merge = lambda t: t.reshape(t.shape[0], N_DEV * t.shape[2], t.shape[3])
    no_dep = jnp.zeros((8, LANES), F32)

    def shards_of(kind, l):
        stack = lambda *ws: jnp.stack(ws).astype(BF16)
        if kind == "ffn1":
            return [stack(tr(ffn1_w_gate[l]), tr(ffn1_w_up[l]), ffn1_w_down[l])]
        if kind == "win":
            return [stack(tr(w_in[l]))]
        return [stack(tr(ffn2_w_gate[l]), tr(ffn2_w_up[l]), ffn2_w_down[l]), stack(w_out[l]),
                stack(tr(w_branch_na[l]), tr(w_branch_sw[l]))]

    shards = {(kind, l): shards_of(kind, l) for l in range(depth) for kind in ("ffn1", "win", "rest")}

    def start(kind, l, after):
        return gather_start(shards[kind, l], after, f"gather_{kind}_{l}")

    def arrive(started, kind, l, after):
        zones = gather_wait(started, after, f"gather_{kind}_{l}_wait")
        return forward_start(zones, no_dep, f"forward_{kind}_{l}")

    def finish(fwd, kind, l, after):
        return [merge(z) for z in forward_wait(fwd, after, f"forward_{kind}_{l}_wait")]

    bd = jnp.asarray(np.kron(np.eye(MXU_TILE // HEAD_DIM), np.full((HEAD_DIM, HEAD_DIM), 1.0 / HEAD_DIM)), BF16)
    bmap = jnp.asarray(_t5_bucket_map())
    tile8 = lambda g: jnp.tile(g, NA_WIDTH // HEAD_DIM).reshape(1, NA_WIDTH)
    tile2 = lambda g: jnp.tile(g, SW_KV_WIDTH // HEAD_DIM).reshape(1, SW_KV_WIDTH)

    st_first, tok = start("ffn1", 0, no_dep)
    t5b = t5_expand(t5_rel_table, bmap, tok, "t5_expand").reshape(SW_STACK, 3 * SW_BLOCK)
    t2_tables = [rpb_expand(_rpb_rows(na_rpb[l]), tok, f"rpb_expand_{l}") for l in range(depth)]
    masks = na_masks(s // GRID_W, tok, "na_masks")
    qk_gains = [(tile8(na_q_norm[l]), tile8(na_k_norm[l]), tile8(sw_q_norm[l]), tile2(sw_k_norm[l]))
                for l in range(depth)]
    early = ([t[0, 0, 0:8, :] for t in t2_tables] + [masks[0, 0:8, 0:LANES]] + [t[0, 0:8, 0:LANES].astype(F32) for v in shards.values() for t in v]
             + [g[:, 0:LANES] for gs in qk_gains for g in gs])
    fwd, _ = arrive(st_first, "ffn1", 0, functools.reduce(jnp.add, early, t5b[0:8, 0:LANES]))
    st_win, dep = start("win", 0, t5b)
    (first,) = finish(fwd, "ffn1", 0, dep)

    saved = []
    layer_w = {0: dict(wg1=(first, 0), wu1=(first, 1), wd1=(first, 2))}
    cur = xs
    for l in range(depth):
        sv = {}
        lw = layer_w[l]
        sv["x0"] = cur
        cur, sv["xn1"], sv["hg1"], sv["hu1"], sv["act1"] = ffn_forward(
            cur, ffn1_norm[l][None], lw["wg1"], lw["wu1"], lw["wd1"], dep, f"ffn1_{l}")
        sv["x1"] = cur
        fwd, _ = arrive(st_win, "win", l, cur)
        st_rest, tok = start("rest", l, cur)
        (zb,) = finish(fwd, "win", l, tok)
        lw["win"] = (zb, 0)
        sv["gains"] = qk_gains[l]
        sv["hn"], sv["zq"], sv["qa"], sv["ka"], sv["qs"], sv["ks"], sv["gt"] = mix_in(
            cur, mix_norm[l][None], lw["win"], b_gate[l][None], *sv["gains"], bd, f"mix_in_{l}")
        sv["t2"] = t2_tables[l]
        sv["o_na"] = na_fwd(sv["qa"], sv["ka"], sv["zq"], sv["t2"], masks, f"na_fwd_{l}")
        dep = no_dep
        if l + 1 < depth:
            st_ffn1, dep = start("ffn1", l + 1, sv["o_na"])
        sv["o_sw"] = sw_fwd(sv["qs"], sv["ks"], sv["zq"], t5b, sw_sink[l], dep, f"sw_fwd_{l}")
        fwd, tok = arrive(st_rest, "rest", l, sv["o_sw"][0:8, 0:LANES] + sv["o_na"][0:8, 0:LANES])
        za, zc, zd = finish(fwd, "rest", l, tok)
        lw.update(wg2=(za, 0), wu2=(za, 1), wd2=(za, 2), wout=(zc, 0), wna=(zd, 0), wsw=(zd, 1))
        cur, sv["a_na"], sv["a_sw"], sv["merged"] = merge_out(
            cur, sv["o_na"], sv["o_sw"], sv["gt"], lw["wna"], lw["wsw"], lw["wout"], f"merge_out_{l}")
        sv["x2"] = cur
        if l + 1 < depth:
            st_win, dep = start("win", l + 1, cur)
            sv["xn2"], sv["hg2"], sv["hu2"], sv["act2"] = ffn_forward(
                cur, ffn2_norm[l][None], lw["wg2"], lw["wu2"], None, dep, f"ffn2_up_{l}")
            fwd, dep = arrive(st_ffn1, "ffn1", l + 1, sv["act2"])
            cur = ffn_down(cur, sv["act2"], lw["wd2"], dep, f"ffn2_down_{l}")
            (za,) = finish(fwd, "ffn1", l + 1, cur)
            layer_w[l + 1] = dict(wg1=(za, 0), wu1=(za, 1), wd1=(za, 2))
        else:
            dx, loss_acc, sv["xn2"], sv["hg2"], sv["hu2"], sv["act2"] = ffn_forward(
                cur, ffn2_norm[l][None], lw["wg2"], lw["wu2"], lw["wd2"], no_dep, f"ffn2_{l}",
                target=loss_target[0])
        dep = no_dep
        saved.append(sv)

    split = lambda t: t.reshape(N_DEV, t.shape[0] // N_DEV, t.shape[1])
    pending = {}
    last_key = "ffn1_0"
    two_level = {last_key}
    small = {k: [None] * depth for k in SMALL_NAMES if k != "t5_rel_table"}
    dbias_sw = []
    for l in reversed(range(depth)):
        sv = saved[l]
        lw = layer_w[l]
        wg1, wu1, wd1, wg2, wu2, wd2 = (lw[k] for k in ("wg1", "wu1", "wd1", "wg2", "wu2", "wd2"))
        win_t, wout_l, wna_t, wsw_t = lw["win"], lw["wout"], lw["wna"], lw["wsw"]
        blocks = ((2, "x2", "xn2", "hg2", "hu2", "act2", wg2, wu2, wd2, "ffn2_norm", 3),
                  (1, "x0", "xn1", "hg1", "hu1", "act1", wg1, wu1, wd1, "ffn1_norm", 0))

        def ffn_backward(dx, blk):
            tag, xk, xnk, hgk, huk, actk, wg, wu, wd, norm_name, slot = blk
            gains = weights[norm_name]
            dxb, dhg, dhu = ffn_bwd_act(dx, wd, sv[hgk], sv[huk], f"ffn{tag}_bwd_act_{l}")
            gwg, gwu, gwd = tn_matmul([(dhg, sv[xnk], 1.0), (dhu, sv[xnk], 1.0), (sv[actk], dxb, 0.5)],
                                      f"ffn{tag}_dw_{l}")
            key = f"ffn{tag}_{l}"
            blocks_of = [split(gwg), split(gwu), split(gwd)]
            if key in two_level:
                paired, token = pair_start(blocks_of, dxb, f"pair_{key}")
            else:
                pending[key], token = scatter_start([blocks_of], f"scatter_{key}")
            dx, dg = proj_bwd_norm([dhg, dhu], [wg, wu], sv[xk], gains[l][None], dx, token, f"ffn{tag}_bwd_x_{l}")
            token = no_dep
            if key in two_level:
                thru, land = pair_wait(paired, dx, f"pair_{key}_wait")
                pending[key], token = chip_start(pair_sum(thru, land, f"pair_sum_{key}"), dg, f"chips_{key}")
            small[norm_name][l] = dg[0]
            return dx, token

        dx, token = ffn_backward(dx, blocks[0])
        dxb, dzg, da_na, da_sw, do_na, do_sw, dbg = mix_bwd_out(
            dx, sv["gt"], sv["a_na"], sv["a_sw"], wna_t, wsw_t, wout_l, token, f"mix_bwd_out_{l}")
        small["b_gate"][l] = dbg[0]
        gwout, gwna, gwsw = tn_matmul([(sv["merged"], dxb, 1.0), (da_na, sv["o_na"], 1.0), (da_sw, sv["o_sw"], 1.0)],
                                      f"mix_dw_{l}")
        dqa, dka, dva, dt2 = na_bwd(sv["qa"], sv["ka"], sv["zq"], sv["t2"], masks, sv["o_na"], do_na, f"na_bwd_{l}")
        dqs, dks, dvs, dbias, dsink = sw_bwd(sv["qs"], sv["ks"], sv["zq"], t5b, sw_sink[l], sv["o_sw"], do_sw,
                                             f"sw_bwd_{l}")
        dbias_sw.append(dbias.reshape(SW_HEADS, SW_BLOCK, 3 * SW_BLOCK))
        small["sw_sink"][l] = jnp.sum(dsink[:, 0].reshape(SW_HEADS, SW_BLOCK), axis=1)
        small["na_rpb"][l] = _rpb_from_rows(rpb_reduce(dt2, f"rpb_reduce_{l}"))
        dz, dgqa, dgka, dgqs, dgks = qk_norm_bwd(dqa, dka, dva, dqs, dks, dvs, sv["zq"], dzg, *sv["gains"], bd,
                                                 f"qk_norm_bwd_{l}")
        fold = lambda g: jnp.sum(g.reshape(-1, HEAD_DIM), axis=0)
        small["na_q_norm"][l], small["na_k_norm"][l] = fold(dgqa), fold(dgka)
        small["sw_q_norm"][l], small["sw_k_norm"][l] = fold(dgqs), fold(dgks)
        (gwin,) = tn_matmul([(dz, sv["hn"], 1.0)], f"dwin_{l}")
        pending[f"mix_{l}"], token = scatter_start([[split(gwout)], [split(gwna), split(gwsw)], [split(gwin)]],
                                                   f"scatter_mix_{l}")
        dx, dg = proj_bwd_norm([dz], [win_t], sv["x1"], mix_norm[l][None], dx, token, f"mix_bwd_x_{l}")
        small["mix_norm"][l] = dg[0]
        dx, tail = ffn_backward(dx, blocks[1])

    dtab = t5_reduce(dbias_sw, bmap, "t5_reduce")
    small_parts = {k: jnp.stack(v) for k, v in small.items()}
    small_parts["t5_rel_table"] = jnp.transpose(dtab[:, :, 0])

    grads, delta, new_m, new_v = {}, {}, {}, {}
    state = {}
    sharing, token = share_start([small_parts[k] for k in SMALL_NAMES] + [loss_acc], tail, "share_small")
    chain = [token]
    members = {"ffn": lambda t: [(f"ffn{t}_w_gate", 0, 0, True), (f"ffn{t}_w_up", 0, 1, True),
                                 (f"ffn{t}_w_down", 0, 2, False)],
               "mix": lambda t: [("w_out", 0, 0, False), ("w_branch_na", 1, 0, True), ("w_branch_sw", 1, 1, True),
                                 ("w_in", 2, 0, True)]}

    def collect(key):
        if key in two_level:
            zones = [chip_wait(pending[key], chain[0], f"wait_{key}")]
        else:
            zones = scatter_wait(pending[key], chain[0], f"wait_{key}")
        kind, l = key.split("_")
        group = members[kind[:3]](kind[3:])
        complete = all(f"{kind}_{j}" in done for j in range(depth) if j != int(l))
        for zi, zone in enumerate(zones):
            mine = sorted((wi, k, transposed) for k, z, wi, transposed in group if z == zi)
            views = [tr if transposed else (lambda t: t) for _, _, transposed in mine]
            items = [(view(weights[k]), view(mom_m[k]), view(mom_v[k]), state.get(k))
                     for (_, k, _), view in zip(mine, views)]
            results = adamw_layer(zone, int(l), items, chain[0], f"adamw_{key}_{zi}")
            chain[0] = results[-1][1]
            for (_, k, _), view, res in zip(mine, views, results):
                state[k] = res
                if complete:
                    grads[k], delta[k], new_m[k], new_v[k] = (view(t) for t in res)
        done.add(key)

    done = set()
    for key in pending:
        if key != last_key:
            collect(key)
    collect(last_key)
    *recvs, all_losses = share_wait(sharing, chain[0], "share_small_wait")
    loss = jnp.sum(all_losses) * (0.5 / d)
    results = adamw_small([weights[k] for k in SMALL_NAMES], recvs, [mom_m[k] for k in SMALL_NAMES],
                          [mom_v[k] for k in SMALL_NAMES], "adamw_small")
    for dst, outs in zip((grads, delta, new_m, new_v), results):
        dst.update(dict(zip(SMALL_NAMES, outs)))

    return (loss, dx[None], *[grads[k] for k in order], *[delta[k] for k in order],
            *[new_m[k] for k in order], *[new_v[k] for k in order])
```

```python
import functools
import math

import numpy as np
import jax
import jax.numpy as jnp
from jax import lax
from jax.experimental import pallas as pl
from jax.experimental.pallas import tpu as pltpu

F32 = jnp.float32
BF16 = jnp.bfloat16
MESH = pl.DeviceIdType.MESH

N_DEV = 8
EPS = 1e-6
NEG = -1e30
HEAD_DIM = 64
GRID_W = 64
NA_ROWS = 8
NA_COLS = 16
NA_WIDTH = 512
SW_Q_WIDTH = 512
SW_KV_WIDTH = 128
SW_BLOCK = 128
SW_HEADS = 8
SW_REP = 4
REL_BUCKETS = 32
REL_MAX_DIST = 128
QKV_WIDTH = 3 * NA_WIDTH + SW_Q_WIDTH + 2 * SW_KV_WIDTH
SCALE = 1.0 / math.sqrt(HEAD_DIM)

ADAM_LR = 0.001
ADAM_B1 = 0.9
ADAM_B2 = 0.999
ADAM_EPS = 1e-08
ADAM_WD = 0.01
ADAM_STEP = 10

V7X_VMEM_LIMIT = 56 * 1024 * 1024
LANES = 128
MXU_TILE = 256

NT = (((1,), (1,)), ((), ()))
TN = (((0,), (0,)), ((), ()))


def _params(n_grid=1):
    return pltpu.CompilerParams(dimension_semantics=("arbitrary",) * n_grid,
                                vmem_limit_bytes=V7X_VMEM_LIMIT)


def _row_tile(s):
    for t in (512, 256, 128, 64, 32, 16, 8):
        if s % t == 0:
            return t
    raise ValueError(s)


def _tn_tile(n):
    best = max(t for t in range(LANES, min(n, 2304) + 1, LANES) if n % t == 0) if n % LANES == 0 else n
    return best // 2 if best == n and n >= 1024 else best


ONCE = pl.Buffered(1)


def _col_chunk(n):
    return MXU_TILE if n % MXU_TILE == 0 else n


def _dot(a, b):
    return jnp.dot(a, b, preferred_element_type=F32)


def _dotg(a, b, dn):
    return lax.dot_general(a, b, dn, preferred_element_type=F32)


def _sigmoid(v):
    return 1.0 / (1.0 + jnp.exp(-v))


def _rstd(xv):
    return lax.rsqrt(jnp.mean(xv * xv, axis=-1, keepdims=True) + EPS)


def _full(shape):
    nd = len(shape)
    return pl.BlockSpec(shape, lambda i, _n=nd: (0,) * _n)


def _rows(tm, width):
    return pl.BlockSpec((tm, width), lambda i: (i, 0))


def _mat(stack, idx):
    return pl.BlockSpec((None,) + tuple(stack.shape[1:]), lambda i, _w=idx: (_w, 0, 0), pipeline_mode=ONCE)


def _group_mean(v, bd):
    w = bd.shape[0]
    if v.shape[1] > w:
        return jnp.concatenate([_group_mean(v[:, c0:c0 + w], bd) for c0 in range(0, v.shape[1], w)], axis=1)
    hi = v.astype(BF16)
    lo = (v - hi.astype(F32)).astype(BF16)
    return _dot(hi, bd) + _dot(lo, bd)


def _loss_tile(y, t_ref, dy_ref, acc_ref):
    tm, d = y.shape

    @pl.when(pl.program_id(0) == 0)
    def _():
        acc_ref[...] = jnp.zeros(acc_ref.shape, F32)

    err = y - t_ref[...]
    dy_ref[...] = err * (1.0 / d)
    part = jnp.sum((err * err).reshape(tm // 8, 8, d), axis=0)
    acc = part[:, 0:LANES]
    for c0 in range(LANES, d, LANES):
        acc = acc + part[:, c0:c0 + LANES]
    acc_ref[...] = acc_ref[...] + acc


def ffn_forward(x, gain, wg_t, wu_t, wd, dep, name, target=None):
    s, d = x.shape
    f = wg_t[0].shape[1]
    tm = _row_tile(s) if wd is None else min(_row_tile(s), 256)
    fc = _col_chunk(f)
    nw = 2 if wd is None else 3
    n_in = nw + (1 if target is None else 2)

    def body(x_ref, g_ref, *refs):
        w_refs, outs = refs[:nw], refs[n_in:]
        xn_ref, dg_ref, du_ref, act_ref = outs[-4:]
        xv = x_ref[...]
        xn = (xv * _rstd(xv) * g_ref[...]).astype(BF16)
        xn_ref[...] = xn
        for c0 in range(0, f, fc):
            hg = _dotg(xn, w_refs[0][c0:c0 + fc, :], NT)
            hu = _dotg(xn, w_refs[1][c0:c0 + fc, :], NT)
            sg = _sigmoid(hg)
            silu = hg * sg
            du_ref[:, c0:c0 + fc] = silu.astype(BF16)
            dg_ref[:, c0:c0 + fc] = (hu * (sg + silu * (1.0 - sg))).astype(BF16)
            act_ref[:, c0:c0 + fc] = (silu * hu).astype(BF16)
        if wd is not None:
            y = xv + 0.5 * _dot(act_ref[...], w_refs[2][...])
            if target is None:
                outs[0][...] = y
            else:
                _loss_tile(y, refs[nw + 1], outs[0], outs[1])

    weights = [wg_t, wu_t] + ([] if wd is None else [wd])
    in_specs = [_rows(tm, d), _full((1, d))] + [_mat(*w) for w in weights] + [_full(dep.shape)]
    operands = [x, gain, *[w[0] for w in weights], dep]
    out_specs = [_rows(tm, d), _rows(tm, f), _rows(tm, f), _rows(tm, f)]
    out_shape = [jax.ShapeDtypeStruct((s, d), BF16)] + [jax.ShapeDtypeStruct((s, f), BF16)] * 3
    if wd is not None:
        out_specs, out_shape = [_rows(tm, d)] + out_specs, [jax.ShapeDtypeStruct((s, d), F32)] + out_shape
    if target is not None:
        in_specs, operands = in_specs + [_rows(tm, d)], operands + [target]
        out_specs = out_specs[:1] + [_full((8, LANES))] + out_specs[1:]
        out_shape = out_shape[:1] + [jax.ShapeDtypeStruct((8, LANES), F32)] + out_shape[1:]
    return pl.pallas_call(
        body, name=name, grid=(s // tm,), in_specs=in_specs, out_specs=out_specs, out_shape=out_shape,
        compiler_params=_params(),
    )(*operands)


def ffn_down(x, act, wd, dep, name):
    s, d = x.shape
    f = act.shape[1]
    tm = _row_tile(s)

    def body(x_ref, a_ref, w_ref, dep_ref, o_ref):
        o_ref[...] = x_ref[...] + 0.5 * _dot(a_ref[...], w_ref[...])

    return pl.pallas_call(
        body, name=name, grid=(s // tm,),
        in_specs=[_rows(tm, d), _rows(tm, f), _mat(*wd), _full(dep.shape)],
        out_specs=_rows(tm, d),
        out_shape=jax.ShapeDtypeStruct((s, d), F32),
        compiler_params=_params(),
    )(x, act, wd[0], dep)


def mix_in(x, gain, win_t, b_gate, gq_na, gk_na, gq_sw, gk_sw, bd, name):
    s, d = x.shape
    tm = _row_tile(s)
    gc = _col_chunk(2 * d)

    def body(x_ref, g_ref, w_ref, b_ref, gqa_ref, gka_ref, gqs_ref, gks_ref, bd_ref,
             hn_ref, zq_ref, qa_ref, ka_ref, qs_ref, ks_ref, gt_ref):
        xv = x_ref[...]
        hn = (xv * _rstd(xv) * g_ref[...]).astype(BF16)
        hn_ref[...] = hn

        def proj(c0, c1):
            return _dotg(hn, w_ref[c0:c1, :], NT)

        def headnorm(z, g, bdm):
            return z * lax.rsqrt(_group_mean(z * z, bdm) + EPS) * g

        bd512 = bd_ref[...]
        bd128 = bd_ref[0:SW_KV_WIDTH, 0:SW_KV_WIDTH]
        z = proj(0, 512)
        zq_ref[:, 0:512] = z.astype(BF16)
        qa_ref[...] = (headnorm(z, gqa_ref[...], bd512) * SCALE).astype(BF16)
        z = proj(512, 1024)
        zq_ref[:, 512:1024] = z.astype(BF16)
        ka_ref[...] = headnorm(z, gka_ref[...], bd512).astype(BF16)
        z = proj(1024, 1536)
        zq_ref[:, 1024:1536] = z.astype(BF16)
        z = proj(1536, 2048)
        zq_ref[:, 1536:2048] = z.astype(BF16)
        qs_ref[...] = (headnorm(z, gqs_ref[...], bd512) * SCALE).astype(BF16)
        z = proj(2048, 2176)
        zq_ref[:, 2048:2176] = z.astype(BF16)
        ks_ref[...] = headnorm(z, gks_ref[...], bd128).astype(BF16)
        z = proj(2176, 2304)
        zq_ref[:, 2176:2304] = z.astype(BF16)
        for c0 in range(0, 2 * d, gc):
            zg = proj(QKV_WIDTH + c0, QKV_WIDTH + c0 + gc) + b_ref[:, c0:c0 + gc]
            gt_ref[:, c0:c0 + gc] = _sigmoid(zg).astype(BF16)

    return pl.pallas_call(
        body, name=name, grid=(s // tm,),
        in_specs=[_rows(tm, d), _full((1, d)), _mat(*win_t), _full((1, 2 * d)),
                  _full((1, 512)), _full((1, 512)), _full((1, 512)), _full((1, 128)), _full((MXU_TILE, MXU_TILE))],
        out_specs=[_rows(tm, d), _rows(tm, QKV_WIDTH), _rows(tm, 512), _rows(tm, 512), _rows(tm, 512),
                   _rows(tm, 128), _rows(tm, 2 * d)],
        out_shape=[jax.ShapeDtypeStruct((s, d), BF16), jax.ShapeDtypeStruct((s, QKV_WIDTH), BF16),
                   jax.ShapeDtypeStruct((s, 512), BF16), jax.ShapeDtypeStruct((s, 512), BF16),
                   jax.ShapeDtypeStruct((s, 512), BF16), jax.ShapeDtypeStruct((s, 128), BF16),
                   jax.ShapeDtypeStruct((s, 2 * d), BF16)],
        compiler_params=_params(),
    )(x, gain, win_t[0], b_gate, gq_na, gk_na, gq_sw, gk_sw, bd)


def _na_iotas():
    qc = lax.broadcasted_iota(jnp.int32, (GRID_W, LANES), 0)
    ln = lax.broadcasted_iota(jnp.int32, (GRID_W, LANES), 1)
    low = ln < GRID_W
    kc = jnp.where(low, ln, ln - GRID_W)
    diff = kc - qc + (NA_COLS - 1)
    qcs = jnp.clip(qc - NA_COLS // 2, 0, GRID_W - NA_COLS)
    inwin = (kc >= qcs) & (kc < qcs + NA_COLS)
    return diff, low, inwin


NA_RI = 2 * NA_ROWS - 1
NA_CI = 2 * NA_COLS - 1
NA_T2 = NA_RI + 1


def _rpb_rows(rpb):
    h = rpb.shape[0]
    padded = jnp.pad(rpb, ((0, 0), (1, 1), (0, GRID_W - NA_CI)))
    return jnp.concatenate([padded[:, :NA_T2], padded[:, 1:NA_T2 + 1]], axis=2).reshape(h, NA_T2, LANES)


def _rpb_from_rows(rows):
    return rows[:, 1:, :NA_CI] + rows[:, :NA_RI, GRID_W:GRID_W + NA_CI]


def rpb_expand(rows, dep, name):
    n_heads = rows.shape[0]

    def body(r_ref, dep_ref, o_ref):
        for h in range(n_heads):
            for e in range(NA_T2):
                line = jnp.broadcast_to(r_ref[h, e:e + 1, :], (GRID_W, LANES))
                o_ref[h, e] = pltpu.roll(line, LANES - (NA_COLS - 1), 1, stride=1, stride_axis=0)

    return pl.pallas_call(
        body, name=name,
        in_specs=[pl.BlockSpec(memory_space=pltpu.VMEM), pl.BlockSpec(memory_space=pltpu.VMEM)],
        out_specs=pl.BlockSpec(memory_space=pltpu.VMEM),
        out_shape=jax.ShapeDtypeStruct((n_heads, NA_T2, GRID_W, LANES), F32),
        compiler_params=pltpu.CompilerParams(vmem_limit_bytes=V7X_VMEM_LIMIT),
    )(rows, dep)


def rpb_reduce(dt2, name):
    n_heads = dt2.shape[0]
    flip = jnp.asarray(np.eye(GRID_W)[::-1], BF16)

    def body(d_ref, j_ref, o_ref):
        jm = j_ref[...]
        for h in range(n_heads):
            for e in range(NA_T2):
                dv = d_ref[h, e]
                hi = dv.astype(BF16)
                mid = (dv - hi.astype(F32)).astype(BF16)
                lo = (dv - hi.astype(F32) - mid.astype(F32)).astype(BF16)
                rev = _dot(jm, hi) + _dot(jm, mid) + _dot(jm, lo)
                back = pltpu.roll(rev, LANES + (NA_COLS - 1) - (GRID_W - 1), 1, stride=1, stride_axis=0)
                o_ref[h, e:e + 1, :] = jnp.sum(back, axis=0, keepdims=True)

    return pl.pallas_call(
        body, name=name,
        in_specs=[pl.BlockSpec(memory_space=pltpu.VMEM)] * 2,
        out_specs=pl.BlockSpec(memory_space=pltpu.VMEM),
        out_shape=jax.ShapeDtypeStruct((n_heads, NA_T2, LANES), F32),
        compiler_params=pltpu.CompilerParams(vmem_limit_bytes=V7X_VMEM_LIMIT),
    )(dt2, flip)


NA_TQ = 4
NA_TK = NA_TQ + NA_ROWS
NA_KCH = NA_TK // 2


def _na_tile_geometry(t, rows):
    r = t * NA_TQ
    kbase = jnp.clip(r - NA_ROWS // 2, 0, rows - NA_TK)
    starts = [jnp.clip(r + a - NA_ROWS // 2, 0, rows - NA_ROWS) for a in range(NA_TQ)]
    return r, kbase, starts


def _na_tile_mask(kbase, starts, low, inwin):
    half = jnp.where(low, 0, 1)
    cols = []
    for c in range(NA_KCH):
        krow = kbase + 2 * c + half
        cols.append(jnp.concatenate(
            [jnp.where(inwin & (krow >= st) & (krow < st + NA_ROWS), 0.0, NEG) for st in starts], axis=0))
    return jnp.concatenate(cols, axis=1)


def na_masks(rows, dep, name):
    n_tiles = rows // NA_TQ

    def body(dep_ref, o_ref):
        _, low, inwin = _na_iotas()
        for t in range(n_tiles):
            _, kbase, starts = _na_tile_geometry(t, rows)
            o_ref[t] = _na_tile_mask(kbase, starts, low, inwin)

    return pl.pallas_call(
        body, name=name,
        in_specs=[pl.BlockSpec(memory_space=pltpu.VMEM)], out_specs=pl.BlockSpec(memory_space=pltpu.VMEM),
        out_shape=jax.ShapeDtypeStruct((n_tiles, NA_TQ * GRID_W, NA_TK * GRID_W), F32),
        compiler_params=pltpu.CompilerParams(vmem_limit_bytes=V7X_VMEM_LIMIT),
    )(dep)


def _na_tile_index(r, kbase, a, c):
    return jnp.clip(kbase + 2 * c - (r + a) + NA_ROWS, 0, NA_T2 - 1)


def _na_tile_scores(q, k, t2_ref, hh, r, kbase, madd):
    bias = jnp.concatenate(
        [jnp.concatenate([t2_ref[hh, _na_tile_index(r, kbase, a, c)] for a in range(NA_TQ)], axis=0)
         for c in range(NA_KCH)], axis=1)
    return _dotg(q, k, NT) + bias + madd


def _softmax_rows(sc):
    e = jnp.exp(sc - jnp.max(sc, axis=1, keepdims=True))
    return e * (1.0 / jnp.sum(e, axis=1, keepdims=True))


def na_fwd(qa, ka, zq, t2, masks, name):
    s = qa.shape[0]
    rows = s // GRID_W
    n_pairs = NA_WIDTH // LANES
    v_blk0 = (2 * NA_WIDTH) // LANES

    assert rows % NA_TQ == 0 and rows >= NA_TK
    tq, tk = NA_TQ * GRID_W, NA_TK * GRID_W

    def body(q_ref, k_ref, v_ref, t2_ref, m_ref, o_ref, s_scr, p_scr):
        def tile(t, carry):
            r, kbase, _ = _na_tile_geometry(t, rows)
            madd = m_ref[t]
            qr = pl.ds(pl.multiple_of(r * GRID_W, tq), tq)
            kr = pl.ds(pl.multiple_of(kbase * GRID_W, tq), tk)
            for hh in range(2):
                lanes = slice(HEAD_DIM * hh, HEAD_DIM * (hh + 1))
                s_scr[tq * hh:tq * (hh + 1), :] = _na_tile_scores(q_ref[qr, lanes], k_ref[kr, lanes], t2_ref, hh, r,
                                                                  kbase, madd)
            p_scr[...] = _softmax_rows(s_scr[...]).astype(BF16)
            for hh in range(2):
                lanes = slice(HEAD_DIM * hh, HEAD_DIM * (hh + 1))
                o_ref[qr, lanes] = _dot(p_scr[tq * hh:tq * (hh + 1), :], v_ref[kr, lanes]).astype(BF16)
            return carry

        lax.fori_loop(0, rows // NA_TQ, tile, 0, unroll=2)

    col = lambda off: pl.BlockSpec((s, LANES), lambda p, _o=off: (0, _o + p))
    return pl.pallas_call(
        body, name=name, grid=(n_pairs,),
        in_specs=[col(0), col(0), col(v_blk0),
                  pl.BlockSpec((2, NA_T2, GRID_W, LANES), lambda p: (p, 0, 0, 0)),
                  pl.BlockSpec(masks.shape, lambda p: (0, 0, 0), pipeline_mode=ONCE)],
        out_specs=col(0),
        out_shape=jax.ShapeDtypeStruct((s, NA_WIDTH), BF16),
        scratch_shapes=[pltpu.VMEM((2 * tq, tk), F32), pltpu.VMEM((2 * tq, tk), BF16)],
        compiler_params=_params(),
    )(qa, ka, zq, t2, masks)


def na_bwd(qa, ka, zq, t2, masks, o_na, do_na, name):
    s = qa.shape[0]
    rows = s // GRID_W
    n_pairs = NA_WIDTH // LANES
    v_blk0 = (2 * NA_WIDTH) // LANES

    tq, tk = NA_TQ * GRID_W, NA_TK * GRID_W

    def body(q_ref, k_ref, v_ref, t2_ref, m_ref, o_ref, do_ref, dq_ref, dk_ref, dv_ref, dt2_ref):
        dk_ref[...] = jnp.zeros(dk_ref.shape, F32)
        dv_ref[...] = jnp.zeros(dv_ref.shape, F32)
        dt2_ref[...] = jnp.zeros(dt2_ref.shape, F32)

        def tile(t, carry):
            r, kbase, _ = _na_tile_geometry(t, rows)
            madd = m_ref[t]
            qr = pl.ds(pl.multiple_of(r * GRID_W, tq), tq)
            kr = pl.ds(pl.multiple_of(kbase * GRID_W, tq), tk)
            for hh in range(2):
                lanes = slice(HEAD_DIM * hh, HEAD_DIM * (hh + 1))
                q, k, v = q_ref[qr, lanes], k_ref[kr, lanes], v_ref[kr, lanes]
                p = _softmax_rows(_na_tile_scores(q, k, t2_ref, hh, r, kbase, madd))
                do = do_ref[qr, lanes]
                delta = jnp.sum(do.astype(F32) * o_ref[qr, lanes].astype(F32), axis=1, keepdims=True)
                ds = p * (_dotg(do, v, NT) - delta)
                shared = {}
                for a in range(NA_TQ):
                    for c in range(NA_KCH):
                        shared.setdefault(2 * c - a, []).append(
                            ds[GRID_W * a:GRID_W * (a + 1), LANES * c:LANES * (c + 1)])
                for offset, parts in shared.items():
                    e = jnp.clip(offset + kbase - r + NA_ROWS, 0, NA_T2 - 1)
                    dt2_ref[hh, e] = dt2_ref[hh, e] + functools.reduce(jnp.add, parts)
                dsb = ds.astype(BF16)
                dq_ref[qr, lanes] = _dot(dsb, k)
                dk_ref[kr, lanes] = dk_ref[kr, lanes] + _dotg(dsb, q, TN)
                dv_ref[kr, lanes] = dv_ref[kr, lanes] + _dotg(p.astype(BF16), do, TN)
            return carry

        lax.fori_loop(0, rows // NA_TQ, tile, 0, unroll=2)

    col = lambda off: pl.BlockSpec((s, LANES), lambda p, _o=off: (0, _o + p))
    t2spec = pl.BlockSpec((2, NA_T2, GRID_W, LANES), lambda p: (p, 0, 0, 0))
    return pl.pallas_call(
        body, name=name, grid=(n_pairs,),
        in_specs=[col(0), col(0), col(v_blk0), t2spec,
                  pl.BlockSpec(masks.shape, lambda p: (0, 0, 0), pipeline_mode=ONCE), col(0), col(0)],
        out_specs=[col(0), col(0), col(0), t2spec],
        out_shape=[jax.ShapeDtypeStruct((s, NA_WIDTH), F32)] * 3 + [jax.ShapeDtypeStruct(t2.shape, F32)],
        compiler_params=_params(),
    )(qa, ka, zq, t2, masks, o_na, do_na)


def _t5_bucket_map():
    rel = np.arange(3 * SW_BLOCK)[None, :] - SW_BLOCK - np.arange(SW_BLOCK)[:, None]
    nb = REL_BUCKETS // 2
    max_exact = nb // 2
    n = np.abs(rel)
    large = max_exact + (np.log(np.maximum(n, 1) / max_exact)
                         / np.log(REL_MAX_DIST / max_exact) * (nb - max_exact)).astype(np.int32)
    large = np.minimum(large, nb - 1)
    return ((rel > 0) * nb + np.where(n < max_exact, n, large)).astype(np.int32)


def t5_expand(table, bmap, dep, name):
    def body(tab_ref, bm_ref, dep_ref, o_ref):
        bm = bm_ref[...]
        j, inwin = _sw_mask_iotas()
        masks = [jnp.where(keep, 0.0, NEG) for keep in (inwin & (j >= SW_BLOCK), inwin, inwin & (j < 2 * SW_BLOCK))]
        for h in range(SW_HEADS):
            t = jnp.zeros(bm.shape, F32)
            for b in range(REL_BUCKETS):
                t = jnp.where(bm == b, tab_ref[b, h], t)
            for e, madd in enumerate(masks):
                o_ref[e, h] = t + madd

    return pl.pallas_call(
        body, name=name,
        in_specs=[pl.BlockSpec(memory_space=pltpu.SMEM), pl.BlockSpec(memory_space=pltpu.VMEM),
                  pl.BlockSpec(memory_space=pltpu.VMEM)],
        out_specs=pl.BlockSpec(memory_space=pltpu.VMEM),
        out_shape=jax.ShapeDtypeStruct((SW_EDGES, SW_HEADS) + bmap.shape, F32),
        compiler_params=pltpu.CompilerParams(vmem_limit_bytes=V7X_VMEM_LIMIT),
    )(table, bmap, dep)


def t5_reduce(dbias_list, bmap, name):
    n = len(dbias_list)

    def body(*refs):
        d_refs, bm_ref, o_ref = refs[:n], refs[n], refs[n + 1]
        bm = bm_ref[...]
        for h in range(SW_HEADS):
            dv = d_refs[0][h]
            for other in d_refs[1:]:
                dv = dv + other[h]
            rows = [jnp.sum(jnp.where(bm == b, dv, 0.0), axis=0, keepdims=True) for b in range(REL_BUCKETS)]
            r = jnp.concatenate(rows, axis=0)
            o_ref[h] = jnp.broadcast_to(jnp.sum(r, axis=1, keepdims=True), (REL_BUCKETS, LANES))

    return pl.pallas_call(
        body, name=name,
        in_specs=[pl.BlockSpec(memory_space=pltpu.VMEM)] * (n + 1),
        out_specs=pl.BlockSpec(memory_space=pltpu.VMEM),
        out_shape=jax.ShapeDtypeStruct((SW_HEADS, REL_BUCKETS, LANES), F32),
        compiler_params=pltpu.CompilerParams(vmem_limit_bytes=V7X_VMEM_LIMIT),
    )(*dbias_list, bmap)


def _sw_mask_iotas():
    a = lax.broadcasted_iota(jnp.int32, (SW_BLOCK, 3 * SW_BLOCK), 0)
    j = lax.broadcasted_iota(jnp.int32, (SW_BLOCK, 3 * SW_BLOCK), 1)
    inwin = jnp.abs(j - SW_BLOCK - a) <= SW_BLOCK
    return j, inwin


SW_STACK = SW_HEADS * SW_BLOCK
SW_EDGES = 3


def _sw_edge(n, nb):
    return jnp.where(n == 0, 0, jnp.where(n == nb - 1, 2, 1))


def _sw_softmax(sc, sk):
    m = jnp.maximum(jnp.max(sc, axis=1, keepdims=True), sk)
    e = jnp.exp(sc - m)
    es = jnp.exp(sk - m)
    inv = 1.0 / (jnp.sum(e, axis=1, keepdims=True) + es)
    return e * inv, es * inv


def _sw_prologue(k_ref, v_ref, kp, vp, sink_ref, s):
    pad = s + 2 * SW_BLOCK
    zeros = jnp.zeros((SW_BLOCK, SW_KV_WIDTH), BF16)
    kp[0:SW_BLOCK, :] = zeros
    vp[0:SW_BLOCK, :] = zeros
    kp[SW_BLOCK + s:pad, :] = zeros
    vp[SW_BLOCK + s:pad, :] = zeros
    kp[SW_BLOCK:SW_BLOCK + s, :] = k_ref[...]
    vp[SW_BLOCK:SW_BLOCK + s, :] = v_ref[...]
    return jnp.concatenate([jnp.full((SW_BLOCK, 1), sink_ref[h], F32) for h in range(SW_HEADS)], axis=0)


def sw_fwd(qs, ks, zq, t5b, sink, dep, name):
    s = qs.shape[0]
    nb = s // SW_BLOCK
    v_blk = (3 * NA_WIDTH + SW_Q_WIDTH + SW_KV_WIDTH) // LANES
    pad = s + 2 * SW_BLOCK
    assert nb >= 2

    def body(q_ref, k_ref, v_ref, b_ref, sink_ref, dep_ref, o_ref, kp, vp, s_scr, p_scr):
        sink_col = _sw_prologue(k_ref, v_ref, kp, vp, sink_ref, s)

        def blk(n, carry):
            q0 = pl.multiple_of(n * SW_BLOCK, SW_BLOCK)
            qr, kr = pl.ds(q0, SW_BLOCK), pl.ds(q0, 3 * SW_BLOCK)
            for h in range(SW_HEADS):
                g = h // SW_REP
                s_scr[SW_BLOCK * h:SW_BLOCK * (h + 1), :] = _dotg(
                    q_ref[qr, HEAD_DIM * h:HEAD_DIM * (h + 1)], kp[kr, HEAD_DIM * g:HEAD_DIM * (g + 1)], NT)
            p, _ = _sw_softmax(s_scr[...] + b_ref[_sw_edge(n, nb)], sink_col)
            p_scr[...] = p.astype(BF16)
            for h in range(SW_HEADS):
                g = h // SW_REP
                o_ref[qr, HEAD_DIM * h:HEAD_DIM * (h + 1)] = _dot(
                    p_scr[SW_BLOCK * h:SW_BLOCK * (h + 1), :], vp[kr, HEAD_DIM * g:HEAD_DIM * (g + 1)]).astype(BF16)
            return carry

        lax.fori_loop(0, nb, blk, 0, unroll=2)

    return pl.pallas_call(
        body, name=name, grid=(1,),
        in_specs=[_full((s, SW_Q_WIDTH)), _full((s, SW_KV_WIDTH)),
                  pl.BlockSpec((s, SW_KV_WIDTH), lambda i: (0, v_blk)),
                  pl.BlockSpec(t5b.shape, lambda i: (0, 0, 0), pipeline_mode=ONCE), pl.BlockSpec(memory_space=pltpu.SMEM),
                  _full(dep.shape)],
        out_specs=_full((s, SW_Q_WIDTH)),
        out_shape=jax.ShapeDtypeStruct((s, SW_Q_WIDTH), BF16),
        scratch_shapes=[pltpu.VMEM((pad, SW_KV_WIDTH), BF16), pltpu.VMEM((pad, SW_KV_WIDTH), BF16),
                        pltpu.VMEM((SW_STACK, 3 * SW_BLOCK), F32), pltpu.VMEM((SW_STACK, 3 * SW_BLOCK), BF16)],
        compiler_params=_params(),
    )(qs, ks, zq, t5b, sink, dep)


def sw_bwd(qs, ks, zq, t5b, sink, o_sw, do_sw, name):
    s = qs.shape[0]
    nb = s // SW_BLOCK
    v_blk = (3 * NA_WIDTH + SW_Q_WIDTH + SW_KV_WIDTH) // LANES
    pad = s + 2 * SW_BLOCK

    def body(q_ref, k_ref, v_ref, b_ref, sink_ref, o_ref, do_ref,
             dq_ref, dk_ref, dv_ref, db_ref, dsk_ref, kp, vp, dkp, dvp, s_scr, dp_scr, ds_scr, p_scr):
        sink_col = _sw_prologue(k_ref, v_ref, kp, vp, sink_ref, s)
        dkp[...] = jnp.zeros(dkp.shape, F32)
        dvp[...] = jnp.zeros(dvp.shape, F32)
        db_ref[...] = jnp.zeros(db_ref.shape, F32)
        dsk_ref[...] = jnp.zeros(dsk_ref.shape, F32)

        def blk(n, carry):
            q0 = pl.multiple_of(n * SW_BLOCK, SW_BLOCK)
            qr, kr = pl.ds(q0, SW_BLOCK), pl.ds(q0, 3 * SW_BLOCK)
            deltas = []
            for h in range(SW_HEADS):
                g = h // SW_REP
                hl, kl = slice(HEAD_DIM * h, HEAD_DIM * (h + 1)), slice(HEAD_DIM * g, HEAD_DIM * (g + 1))
                rows = slice(SW_BLOCK * h, SW_BLOCK * (h + 1))
                do = do_ref[qr, hl]
                s_scr[rows, :] = _dotg(q_ref[qr, hl], kp[kr, kl], NT)
                dp_scr[rows, :] = _dotg(do, vp[kr, kl], NT)
                deltas.append(jnp.sum(do.astype(F32) * o_ref[qr, hl].astype(F32), axis=1, keepdims=True))
            delta = jnp.concatenate(deltas, axis=0)
            p, ps = _sw_softmax(s_scr[...] + b_ref[_sw_edge(n, nb)], sink_col)
            ds = p * (dp_scr[...] - delta)
            db_ref[...] = db_ref[...] + ds
            dsk_ref[...] = dsk_ref[...] - jnp.broadcast_to(ps * delta, (SW_STACK, LANES))
            ds_scr[...] = ds.astype(BF16)
            p_scr[...] = p.astype(BF16)
            for g in range(SW_HEADS // SW_REP):
                kl = slice(HEAD_DIM * g, HEAD_DIM * (g + 1))
                k = kp[kr, kl]
                dkw = jnp.zeros((3 * SW_BLOCK, HEAD_DIM), F32)
                dvw = jnp.zeros((3 * SW_BLOCK, HEAD_DIM), F32)
                for r in range(SW_REP):
                    h = g * SW_REP + r
                    hl, rows = slice(HEAD_DIM * h, HEAD_DIM * (h + 1)), slice(SW_BLOCK * h, SW_BLOCK * (h + 1))
                    dsb = ds_scr[rows, :]
                    dq_ref[qr, hl] = _dot(dsb, k)
                    dkw = dkw + _dotg(dsb, q_ref[qr, hl], TN)
                    dvw = dvw + _dotg(p_scr[rows, :], do_ref[qr, hl], TN)
                dkp[kr, kl] = dkp[kr, kl] + dkw
                dvp[kr, kl] = dvp[kr, kl] + dvw
            return carry

        lax.fori_loop(0, nb, blk, 0)
        dk_ref[...] = dkp[SW_BLOCK:SW_BLOCK + s, :]
        dv_ref[...] = dvp[SW_BLOCK:SW_BLOCK + s, :]

    assert nb >= 2
    bias_spec = _full((SW_STACK, 3 * SW_BLOCK))
    return pl.pallas_call(
        body, name=name, grid=(1,),
        in_specs=[_full((s, SW_Q_WIDTH)), _full((s, SW_KV_WIDTH)),
                  pl.BlockSpec((s, SW_KV_WIDTH), lambda i: (0, v_blk)),
                  pl.BlockSpec(t5b.shape, lambda i: (0, 0, 0), pipeline_mode=ONCE), pl.BlockSpec(memory_space=pltpu.SMEM),
                  _full((s, SW_Q_WIDTH)), _full((s, SW_Q_WIDTH))],
        out_specs=[_full((s, SW_Q_WIDTH)), _full((s, SW_KV_WIDTH)), _full((s, SW_KV_WIDTH)), bias_spec,
                   _full((SW_STACK, LANES))],
        out_shape=[jax.ShapeDtypeStruct((s, SW_Q_WIDTH), F32), jax.ShapeDtypeStruct((s, SW_KV_WIDTH), F32),
                   jax.ShapeDtypeStruct((s, SW_KV_WIDTH), F32),
                   jax.ShapeDtypeStruct((SW_STACK, 3 * SW_BLOCK), F32),
                   jax.ShapeDtypeStruct((SW_STACK, LANES), F32)],
        scratch_shapes=[pltpu.VMEM((pad, SW_KV_WIDTH), BF16), pltpu.VMEM((pad, SW_KV_WIDTH), BF16),
                        pltpu.VMEM((pad, SW_KV_WIDTH), F32), pltpu.VMEM((pad, SW_KV_WIDTH), F32),
                        pltpu.VMEM((SW_STACK, 3 * SW_BLOCK), F32), pltpu.VMEM((SW_STACK, 3 * SW_BLOCK), F32),
                        pltpu.VMEM((SW_STACK, 3 * SW_BLOCK), BF16), pltpu.VMEM((SW_STACK, 3 * SW_BLOCK), BF16)],
        compiler_params=_params(),
    )(qs, ks, zq, t5b, sink, o_sw, do_sw)


def merge_out(x, o_na, o_sw, gt, wbna_t, wbsw_t, wout, name):
    s, d = x.shape
    tm = _row_tile(s)

    def body(x_ref, ona_ref, osw_ref, gt_ref, wna_ref, wsw_ref, wo_ref, xo_ref, ana_ref, asw_ref, mg_ref):
        a_na = _dotg(ona_ref[...], wna_ref[...], NT)
        a_sw = _dotg(osw_ref[...], wsw_ref[...], NT)
        g_na, g_sw = gt_ref[:, 0:d].astype(F32), gt_ref[:, d:2 * d].astype(F32)
        ana_ref[...] = (a_na * g_na * (1.0 - g_na)).astype(BF16)
        asw_ref[...] = (a_sw * g_sw * (1.0 - g_sw)).astype(BF16)
        merged = (g_na * a_na + g_sw * a_sw).astype(BF16)
        mg_ref[...] = merged
        xo_ref[...] = x_ref[...] + _dot(merged, wo_ref[...])

    return pl.pallas_call(
        body, name=name, grid=(s // tm,),
        in_specs=[_rows(tm, d), _rows(tm, 512), _rows(tm, 512), _rows(tm, 2 * d),
                  _mat(*wbna_t), _mat(*wbsw_t), _mat(*wout)],
        out_specs=[_rows(tm, d)] * 4,
        out_shape=[jax.ShapeDtypeStruct((s, d), F32)] + [jax.ShapeDtypeStruct((s, d), BF16)] * 3,
        compiler_params=_params(),
    )(x, o_na, o_sw, gt, wbna_t[0], wbsw_t[0], wout[0])


def mix_bwd_out(dx, gt, a_na, a_sw, wbna_t, wbsw_t, wout, dep, name):
    s, d = dx.shape
    tm = _row_tile(s)

    def body(dx_ref, gt_ref, ana_ref, asw_ref, wna_ref, wsw_ref, wo_ref, dep_ref,
             dxb_ref, dzg_ref, dana_ref, dasw_ref, dona_ref, dosw_ref, dbg_ref):
        @pl.when(pl.program_id(0) == 0)
        def _():
            dbg_ref[...] = jnp.zeros(dbg_ref.shape, F32)

        dxb = dx_ref[...].astype(BF16)
        dxb_ref[...] = dxb
        dm = _dotg(dxb, wo_ref[...], NT)
        for i, (a_ref, da_ref, w_ref, do_ref) in enumerate(
                [(ana_ref, dana_ref, wna_ref, dona_ref), (asw_ref, dasw_ref, wsw_ref, dosw_ref)]):
            gi = gt_ref[:, i * d:(i + 1) * d].astype(F32)
            da = (dm * gi).astype(BF16)
            da_ref[...] = da
            do_ref[...] = _dot(da, w_ref[...]).astype(BF16)
            dzg = dm * a_ref[...].astype(F32)
            dzg_ref[:, i * d:(i + 1) * d] = dzg.astype(BF16)
            dbg_ref[:, i * d:(i + 1) * d] = dbg_ref[:, i * d:(i + 1) * d] + jnp.sum(dzg, axis=0, keepdims=True)

    return pl.pallas_call(
        body, name=name, grid=(s // tm,),
        in_specs=[_rows(tm, d), _rows(tm, 2 * d), _rows(tm, d), _rows(tm, d),
                  _mat(*wbna_t), _mat(*wbsw_t), _mat(*wout), _full(dep.shape)],
        out_specs=[_rows(tm, d), _rows(tm, 2 * d), _rows(tm, d), _rows(tm, d), _rows(tm, 512), _rows(tm, 512),
                   _full((1, 2 * d))],
        out_shape=[jax.ShapeDtypeStruct((s, d), BF16), jax.ShapeDtypeStruct((s, 2 * d), BF16),
                   jax.ShapeDtypeStruct((s, d), BF16), jax.ShapeDtypeStruct((s, d), BF16),
                   jax.ShapeDtypeStruct((s, 512), BF16), jax.ShapeDtypeStruct((s, 512), BF16),
                   jax.ShapeDtypeStruct((1, 2 * d), F32)],
        compiler_params=_params(),
    )(dx, gt, a_na, a_sw, wbna_t[0], wbsw_t[0], wout[0], dep)


def qk_norm_bwd(dqa, dka, dva, dqs, dks, dvs, zq, dzg, gq_na, gk_na, gq_sw, gk_sw, bd, name):
    s = zq.shape[0]
    d2 = dzg.shape[1]
    n_in = QKV_WIDTH + d2
    tm = _row_tile(s)

    def body(dqa_ref, dka_ref, dva_ref, dqs_ref, dks_ref, dvs_ref, zq_ref, dzg_ref,
             gqa_ref, gka_ref, gqs_ref, gks_ref, bd_ref, dz_ref, dgqa_ref, dgka_ref, dgqs_ref, dgks_ref):
        @pl.when(pl.program_id(0) == 0)
        def _():
            for r in (dgqa_ref, dgka_ref, dgqs_ref, dgks_ref):
                r[...] = jnp.zeros(r.shape, F32)

        bd512 = bd_ref[...]
        bd128 = bd_ref[0:SW_KV_WIDTH, 0:SW_KV_WIDTH]

        def one(c0, c1, dy_ref, g_ref, dg_ref, bdm, scale):
            z = zq_ref[:, c0:c1].astype(F32)
            r = lax.rsqrt(_group_mean(z * z, bdm) + EPS)
            zh = z * r
            dy = dy_ref[...] * scale
            dyg = dy * g_ref[...]
            dz = r * (dyg - zh * _group_mean(dyg * zh, bdm))
            dz_ref[:, c0:c1] = dz.astype(BF16)
            dg_ref[...] = dg_ref[...] + jnp.sum(dy * zh, axis=0, keepdims=True)

        one(0, 512, dqa_ref, gqa_ref, dgqa_ref, bd512, SCALE)
        one(512, 1024, dka_ref, gka_ref, dgka_ref, bd512, 1.0)
        dz_ref[:, 1024:1536] = dva_ref[...].astype(BF16)
        one(1536, 2048, dqs_ref, gqs_ref, dgqs_ref, bd512, SCALE)
        one(2048, 2176, dks_ref, gks_ref, dgks_ref, bd128, 1.0)
        dz_ref[:, 2176:2304] = dvs_ref[...].astype(BF16)
        dz_ref[:, QKV_WIDTH:n_in] = dzg_ref[...]

    return pl.pallas_call(
        body, name=name, grid=(s // tm,),
        in_specs=[_rows(tm, 512), _rows(tm, 512), _rows(tm, 512), _rows(tm, 512), _rows(tm, 128), _rows(tm, 128),
                  _rows(tm, QKV_WIDTH), _rows(tm, d2),
                  _full((1, 512)), _full((1, 512)), _full((1, 512)), _full((1, 128)), _full((MXU_TILE, MXU_TILE))],
        out_specs=[_rows(tm, n_in), _full((1, 512)), _full((1, 512)), _full((1, 512)), _full((1, 128))],
        out_shape=[jax.ShapeDtypeStruct((s, n_in), BF16)] + [jax.ShapeDtypeStruct((1, 512), F32)] * 3
                  + [jax.ShapeDtypeStruct((1, 128), F32)],
        compiler_params=_params(),
    )(dqa, dka, dva, dqs, dks, dvs, zq, dzg, gq_na, gk_na, gq_sw, gk_sw, bd)


def ffn_bwd_act(dx, wd, hg, hu, name):
    s, d = dx.shape
    f = wd[0].shape[1]
    tm = _row_tile(s)
    fc = _col_chunk(f)

    def body(dx_ref, w_ref, hg_ref, hu_ref, dxb_ref, dhg_ref, dhu_ref):
        dxv = dx_ref[...]
        dxb_ref[...] = dxv.astype(BF16)
        half = (0.5 * dxv).astype(BF16)
        for c0 in range(0, f, fc):
            dact = _dotg(half, w_ref[c0:c0 + fc, :], NT)
            dhu_ref[:, c0:c0 + fc] = (dact * hu_ref[:, c0:c0 + fc].astype(F32)).astype(BF16)
            dhg_ref[:, c0:c0 + fc] = (dact * hg_ref[:, c0:c0 + fc].astype(F32)).astype(BF16)

    return pl.pallas_call(
        body, name=name, grid=(s // tm,),
        in_specs=[_rows(tm, d), _mat(*wd), _rows(tm, f), _rows(tm, f)],
        out_specs=[_rows(tm, d), _rows(tm, f), _rows(tm, f)],
        out_shape=[jax.ShapeDtypeStruct((s, d), BF16), jax.ShapeDtypeStruct((s, f), BF16),
                   jax.ShapeDtypeStruct((s, f), BF16)],
        compiler_params=_params(),
    )(dx, wd[0], hg, hu)


def proj_bwd_norm(acts, weights, x, gain, dx, dep, name):
    s, d = x.shape
    tm = min(_row_tile(s), 256)
    n = len(acts)

    def body(*refs):
        a_refs, w_refs = refs[:n], refs[n:2 * n]
        x_ref, g_ref, dx_ref, _, o_ref, dg_ref = refs[2 * n:]

        @pl.when(pl.program_id(0) == 0)
        def _():
            dg_ref[...] = jnp.zeros(dg_ref.shape, F32)

        dxn = _dot(a_refs[0][...], w_refs[0][...])
        for a_ref, w_ref in zip(a_refs[1:], w_refs[1:]):
            dxn = dxn + _dot(a_ref[...], w_ref[...])
        xv = x_ref[...]
        r = _rstd(xv)
        xh = xv * r
        dxh = dxn * g_ref[...]
        o_ref[...] = dx_ref[...] + r * (dxh - xh * jnp.mean(dxh * xh, axis=-1, keepdims=True))
        dg_ref[...] = dg_ref[...] + jnp.sum(dxn * xh, axis=0, keepdims=True)

    return pl.pallas_call(
        body, name=name, grid=(s // tm,),
        in_specs=[_rows(tm, a.shape[1]) for a in acts] + [_mat(*w) for w in weights]
                 + [_rows(tm, d), _full((1, d)), _rows(tm, d), _full(dep.shape)],
        out_specs=[_rows(tm, d), _full((1, d))],
        out_shape=[jax.ShapeDtypeStruct((s, d), F32), jax.ShapeDtypeStruct((1, d), F32)],
        compiler_params=_params(),
    )(*acts, *[w[0] for w in weights], x, gain, dx, dep)


def tn_matmul(products, name):
    s, n = products[0][0].shape
    tn = _tn_tile(n) if len(products) == 1 else _col_chunk(n)
    rhs = []
    for _, b, _ in products:
        if not any(b is seen for seen in rhs):
            rhs.append(b)
    which = [next(i for i, seen in enumerate(rhs) if b is seen) for _, b, _ in products]
    npr, nr = len(products), len(rhs)

    def body(*refs):
        a_refs, b_refs, o_refs = refs[:npr], refs[npr:npr + nr], refs[npr + nr:]
        for i, (_, _, scale) in enumerate(products):
            o_refs[i][...] = (scale * _dotg(a_refs[i][...], b_refs[which[i]][...], TN)).astype(BF16)

    return pl.pallas_call(
        body, name=name, grid=(n // tn,),
        in_specs=[pl.BlockSpec((s, tn), lambda i: (0, i))] * npr
                 + [pl.BlockSpec(b.shape, lambda i: (0, 0), pipeline_mode=ONCE) for b in rhs],
        out_specs=[pl.BlockSpec((tn, b.shape[1]), lambda i: (i, 0)) for _, b, _ in products],
        out_shape=[jax.ShapeDtypeStruct((n, b.shape[1]), BF16) for _, b, _ in products],
        compiler_params=_params(),
    )(*[a for a, _, _ in products], *rhs)


def _mesh_pos():
    return lax.axis_index("x"), lax.axis_index("y"), lax.axis_index("c")


def _peers():
    x, y, c = _mesh_pos()
    peers = []
    for rel in range(1, N_DEV):
        peers.append((1 - x if rel & 4 else x, 1 - y if rel & 2 else y, 1 - c if rel & 1 else c))
    return 4 * x + 2 * y + c, peers


HBM_SPEC = pl.BlockSpec(memory_space=pltpu.HBM)
SEM_SPEC = pl.BlockSpec(memory_space=pltpu.SEMAPHORE)


def _split_call(body, name, thru, n_sems, extra=(), with_token=True):
    hbm = lambda t: pltpu.with_memory_space_constraint(t, pltpu.HBM)
    effect = pltpu.CompilerParams(has_side_effects=pltpu.SideEffectType.DATAFLOW_SIDE_EFFECTING)
    nt = len(thru)
    thru_shapes = [pltpu.HBM(t.shape, t.dtype) for t in thru]
    if with_token:
        (after,) = extra
        outs = pl.pallas_call(
            body, name=name, in_specs=[HBM_SPEC] * nt + [pl.BlockSpec(memory_space=pl.ANY)],
            out_specs=[SEM_SPEC] * len(n_sems) + [HBM_SPEC] * nt + [pl.BlockSpec(memory_space=pltpu.VMEM)],
            out_shape=[pltpu.SemaphoreType.DMA((k,)) for k in n_sems] + thru_shapes
                      + [jax.ShapeDtypeStruct((8, LANES), F32)],
            input_output_aliases={i: len(n_sems) + i for i in range(nt)}, compiler_params=effect,
        )(*[hbm(t) for t in thru], after)
        return outs[:len(n_sems)], outs[len(n_sems):-1], outs[-1]
    return pl.pallas_call(
        body, name=name,
        in_specs=[HBM_SPEC] * nt + [SEM_SPEC] * len(n_sems) + [pl.BlockSpec(memory_space=pl.ANY)],
        out_specs=[HBM_SPEC] * nt, out_shape=thru_shapes,
        input_output_aliases={i: i for i in range(nt)}, compiler_params=effect,
    )(*thru, *extra)


def _gather_targets():
    x, y, c = _mesh_pos()
    return 4 * x + 2 * y + c, [(x, y, 1 - c), (1 - x, y, c), (x, 1 - y, c), (1 - x, 1 - y, c)]


def gather_start(shards, after, name):
    n = len(shards)
    zones = [lax.empty((w.shape[0], N_DEV) + w.shape[1:], w.dtype) for w in shards]

    def body(*refs):
        ins, zs = refs[:n], refs[n:2 * n]
        send_sems, recv_sems, local_sems = refs[2 * n + 1:2 * n + 4]
        token = refs[-1]
        me, targets = _gather_targets()
        for a in range(n):
            pltpu.make_async_copy(ins[a], zs[a].at[:, me], local_sems.at[a]).start()
            for k, to in enumerate(targets):
                pltpu.make_async_remote_copy(
                    src_ref=ins[a], dst_ref=zs[a].at[:, me], send_sem=send_sems.at[4 * a + k],
                    recv_sem=recv_sems.at[4 * a + k], device_id=to, device_id_type=MESH).start()
        token[...] = jnp.zeros(token.shape, F32)

    sems, thru, token = _split_call(body, name, list(shards) + zones, (4 * n, 4 * n, n), extra=(after,))
    return (sems, thru, n), token


def gather_wait(started, after, name):
    sems, thru, n = started

    def body(*refs):
        zs = refs[n:2 * n]
        send_sems, recv_sems, local_sems = refs[2 * n:2 * n + 3]
        _, targets = _gather_targets()
        for a in range(n):
            for k, to in enumerate(targets):
                cp = pltpu.make_async_remote_copy(
                    src_ref=zs[a].at[:, 0], dst_ref=zs[a].at[:, 0], send_sem=send_sems.at[4 * a + k],
                    recv_sem=recv_sems.at[4 * a + k], device_id=to, device_id_type=MESH)
                cp.wait_send()
                cp.wait_recv()
            pltpu.make_async_copy(zs[a].at[:, 0], zs[a].at[:, 0], local_sems.at[a]).wait()

    return _split_call(body, name, thru, (4 * n, 4 * n, n), extra=(*sems, after), with_token=False)[n:]


def forward_start(zones, after, name):
    n = len(zones)

    def body(*refs):
        zs = refs[:n]
        send_sems, recv_sems = refs[n + 1:n + 3]
        token = refs[-1]
        x, y, c = _mesh_pos()
        for a in range(n):
            for j, chip in enumerate([(1 - x, y), (x, 1 - y), (1 - x, 1 - y)]):
                blk = zs[a].at[:, 4 * chip[0] + 2 * chip[1] + c]
                pltpu.make_async_remote_copy(
                    src_ref=blk, dst_ref=blk, send_sem=send_sems.at[3 * a + j], recv_sem=recv_sems.at[3 * a + j],
                    device_id=(x, y, 1 - c), device_id_type=MESH).start()
        token[...] = jnp.zeros(token.shape, F32)

    sems, thru, token = _split_call(body, name, list(zones), (3 * n, 3 * n), extra=(after,))
    return (sems, thru, n), token


def forward_wait(started, after, name):
    sems, thru, n = started

    def body(*refs):
        zs = refs[:n]
        send_sems, recv_sems = refs[n:n + 2]
        x, y, c = _mesh_pos()
        for a in range(n):
            for j in range(3):
                cp = pltpu.make_async_remote_copy(
                    src_ref=zs[a].at[:, 0], dst_ref=zs[a].at[:, 0], send_sem=send_sems.at[3 * a + j],
                    recv_sem=recv_sems.at[3 * a + j], device_id=(x, y, 1 - c), device_id_type=MESH)
                cp.wait_send()
                cp.wait_recv()

    return _split_call(body, name, thru, (3 * n, 3 * n), extra=(*sems, after), with_token=False)


def scatter_start(groups, name):
    n = len(groups)
    flat = [g for grp in groups for g in grp]
    nf = len(flat)
    offs = np.cumsum([0] + [len(grp) for grp in groups])
    lands = [lax.empty((N_DEV, len(grp)) + grp[0].shape[1:], grp[0].dtype) for grp in groups]

    def body(*refs):
        ins, zones = refs[:nf], refs[nf:nf + n]
        send_sems, recv_sems, local_sems = refs[nf + n:nf + n + 3]
        token = refs[-1]
        me, peers = _peers()
        for a in range(n):
            for w in range(len(groups[a])):
                pltpu.make_async_copy(ins[offs[a] + w].at[me], zones[a].at[me, w], local_sems.at[a]).start()
        for k, peer in enumerate(peers):
            p_id = 4 * peer[0] + 2 * peer[1] + peer[2]
            for a in range(n):
                for w in range(len(groups[a])):
                    pltpu.make_async_remote_copy(
                        src_ref=ins[offs[a] + w].at[p_id], dst_ref=zones[a].at[me, w],
                        send_sem=send_sems.at[7 * a + k], recv_sem=recv_sems.at[7 * a + k],
                        device_id=peer, device_id_type=MESH).start()
        token[...] = jnp.zeros(token.shape, F32)

    hbm = lambda t: pltpu.with_memory_space_constraint(t, pltpu.HBM)
    outs = pl.pallas_call(
        body, name=name,
        in_specs=[HBM_SPEC] * (nf + n),
        out_specs=[SEM_SPEC] * 3 + [HBM_SPEC] * (nf + n) + [pl.BlockSpec(memory_space=pltpu.VMEM)],
        out_shape=[pltpu.SemaphoreType.DMA((7 * n,)), pltpu.SemaphoreType.DMA((7 * n,)), pltpu.SemaphoreType.DMA((n,))]
                  + [pltpu.HBM(t.shape, t.dtype) for t in flat + lands]
                  + [jax.ShapeDtypeStruct((8, LANES), F32)],
        input_output_aliases={i: 3 + i for i in range(nf + n)},
        compiler_params=pltpu.CompilerParams(has_side_effects=pltpu.SideEffectType.DATAFLOW_SIDE_EFFECTING),
    )(*[hbm(t) for t in flat], *[hbm(t) for t in lands])
    sems, thru, token = outs[:3], outs[3:3 + nf + n], outs[-1]
    return (sems, thru, [len(grp) for grp in groups]), token


def scatter_wait(started, after, name):
    (send_sems, recv_sems, local_sems), thru, sizes = started
    n = len(sizes)
    nf = len(thru) - n

    def body(*refs):
        zones = refs[nf:nf + n]
        s_sems, r_sems, l_sems = refs[nf + n:nf + n + 3]
        me, peers = _peers()
        for a in range(n):
            for k, peer in enumerate(peers):
                cp = pltpu.make_async_remote_copy(
                    src_ref=zones[a].at[0], dst_ref=zones[a].at[0],
                    send_sem=s_sems.at[7 * a + k], recv_sem=r_sems.at[7 * a + k], device_id=peer,
                    device_id_type=MESH)
                cp.wait_send()
                cp.wait_recv()
            pltpu.make_async_copy(zones[a].at[0], zones[a].at[0], l_sems.at[a]).wait()

    outs = pl.pallas_call(
        body, name=name,
        in_specs=[HBM_SPEC] * (nf + n) + [SEM_SPEC] * 3 + [pl.BlockSpec(memory_space=pl.ANY)],
        out_specs=[HBM_SPEC] * (nf + n),
        out_shape=[pltpu.HBM(t.shape, t.dtype) for t in thru],
        input_output_aliases={i: i for i in range(nf + n)},
        compiler_params=pltpu.CompilerParams(has_side_effects=pltpu.SideEffectType.DATAFLOW_SIDE_EFFECTING),
    )(*thru, send_sems, recv_sems, local_sems, after)
    return outs[nf:]


def pair_start(grads, after, name):
    nw = len(grads)
    land = lax.empty((4, nw) + grads[0].shape[1:], grads[0].dtype)

    def body(*refs):
        ins, zone = refs[:nw], refs[nw]
        send_sems, recv_sems = refs[nw + 2:nw + 4]
        x, y, c = _mesh_pos()
        for j in range(4):
            for w in range(nw):
                pltpu.make_async_remote_copy(
                    src_ref=ins[w].at[2 * j + (1 - c)], dst_ref=zone.at[j, w], send_sem=send_sems.at[0],
                    recv_sem=recv_sems.at[0], device_id=(x, y, 1 - c), device_id_type=MESH).start()
        refs[-1][...] = jnp.zeros(refs[-1].shape, F32)

    sems, thru, token = _split_call(body, name, list(grads) + [land], (1, 1), extra=(after,))
    return (sems, thru, nw), token


def pair_wait(started, after, name):
    sems, thru, nw = started

    def body(*refs):
        zone = refs[nw]
        send_sems, recv_sems = refs[nw + 1:nw + 3]
        x, y, c = _mesh_pos()
        cp = pltpu.make_async_remote_copy(src_ref=zone, dst_ref=zone, send_sem=send_sems.at[0],
                                          recv_sem=recv_sems.at[0], device_id=(x, y, 1 - c), device_id_type=MESH)
        cp.wait_send()
        cp.wait_recv()

    outs = _split_call(body, name, thru, (1, 1), extra=(*sems, after), with_token=False)
    return outs[:nw], outs[nw]


def pair_sum(grads, land, name):
    nw = len(grads)
    _, r, c_dim = grads[0].shape

    def body(*refs):
        g_refs, l_ref, o_ref = refs[:nw], refs[nw], refs[nw + 1]
        core = lax.axis_index("c")
        for w in range(nw):
            o_ref[0, w] = (g_refs[w][0, core].astype(F32) + l_ref[0, w].astype(F32)).astype(BF16)

    return pl.pallas_call(
        body, name=name, grid=(4,),
        in_specs=[pl.BlockSpec((1, 2, r, c_dim), lambda j: (j, 0, 0, 0))] * nw
                 + [pl.BlockSpec((1, nw, r, c_dim), lambda j: (j, 0, 0, 0))],
        out_specs=pl.BlockSpec((1, nw, r, c_dim), lambda j: (j, 0, 0, 0)),
        out_shape=jax.ShapeDtypeStruct((4, nw, r, c_dim), BF16),
        compiler_params=_params(),
    )(*[g.reshape(4, 2, r, c_dim) for g in grads], land)


def _other_chips():
    x, y, c = _mesh_pos()
    chips = []
    for rel in range(1, 4):
        px, py = (1 - x if rel & 2 else x), (1 - y if rel & 1 else y)
        chips.append((px, py, 2 * px + py))
    return 2 * x + y, c, chips


def chip_start(pair_sums, after, name):
    land = lax.empty(pair_sums.shape, pair_sums.dtype)

    def body(*refs):
        h_ref, zone = refs[0], refs[1]
        send_sems, recv_sems, local_sem = refs[3:6]
        mine, c, chips = _other_chips()
        pltpu.make_async_copy(h_ref.at[mine], zone.at[mine], local_sem.at[0]).start()
        for k, (px, py, j) in enumerate(chips):
            pltpu.make_async_remote_copy(
                src_ref=h_ref.at[j], dst_ref=zone.at[mine], send_sem=send_sems.at[k], recv_sem=recv_sems.at[k],
                device_id=(px, py, c), device_id_type=MESH).start()
        refs[-1][...] = jnp.zeros(refs[-1].shape, F32)

    sems, thru, token = _split_call(body, name, [pair_sums, land], (3, 3, 1), extra=(after,))
    return (sems, thru), token


def chip_wait(started, after, name):
    sems, thru = started

    def body(*refs):
        zone = refs[1]
        send_sems, recv_sems, local_sem = refs[2:5]
        _, c, chips = _other_chips()
        for k, (px, py, _) in enumerate(chips):
            cp = pltpu.make_async_remote_copy(
                src_ref=zone.at[0], dst_ref=zone.at[0], send_sem=send_sems.at[k], recv_sem=recv_sems.at[k],
                device_id=(px, py, c), device_id_type=MESH)
            cp.wait_send()
            cp.wait_recv()
        pltpu.make_async_copy(zone.at[0], zone.at[0], local_sem.at[0]).wait()

    return _split_call(body, name, thru, (3, 3, 1), extra=(*sems, after), with_token=False)[1]


def share_start(parts, after, name):
    n = len(parts)
    zones = [lax.empty((N_DEV,) + p.shape, p.dtype) for p in parts]

    def body(*refs):
        ins, zs = refs[:n], refs[n:2 * n]
        send_sems, recv_sems, local_sems = refs[2 * n + 1:2 * n + 4]
        me, peers = _peers()
        for i in range(n):
            pltpu.make_async_copy(ins[i], zs[i].at[me], local_sems.at[i]).start()
            for k, peer in enumerate(peers):
                pltpu.make_async_remote_copy(
                    src_ref=ins[i], dst_ref=zs[i].at[me], send_sem=send_sems.at[7 * i + k],
                    recv_sem=recv_sems.at[7 * i + k], device_id=peer, device_id_type=MESH).start()
        refs[-1][...] = jnp.zeros(refs[-1].shape, F32)

    sems, thru, token = _split_call(body, name, list(parts) + zones, (7 * n, 7 * n, n), extra=(after,))
    return (sems, thru, n), token


def share_wait(started, after, name):
    sems, thru, n = started

    def body(*refs):
        zs = refs[n:2 * n]
        send_sems, recv_sems, local_sems = refs[2 * n:2 * n + 3]
        _, peers = _peers()
        for i in range(n):
            for k, peer in enumerate(peers):
                cp = pltpu.make_async_remote_copy(
                    src_ref=zs[i].at[0], dst_ref=zs[i].at[0], send_sem=send_sems.at[7 * i + k],
                    recv_sem=recv_sems.at[7 * i + k], device_id=peer, device_id_type=MESH)
                cp.wait_send()
                cp.wait_recv()
            pltpu.make_async_copy(zs[i].at[0], zs[i].at[0], local_sems.at[i]).wait()

    return _split_call(body, name, thru, (7 * n, 7 * n, n), extra=(*sems, after), with_token=False)[n:]


def _adamw_math(w, g, m, v):
    m = ADAM_B1 * m + (1.0 - ADAM_B1) * g
    v = ADAM_B2 * v + (1.0 - ADAM_B2) * (g * g)
    m_hat = m / (1.0 - ADAM_B1 ** ADAM_STEP)
    v_hat = v / (1.0 - ADAM_B2 ** ADAM_STEP)
    delta = -ADAM_LR * (m_hat / (jnp.sqrt(v_hat) + ADAM_EPS) + ADAM_WD * w)
    return delta, m, v


ADAMW_BLOCK_BYTES = 24 * 1024 * 1024


def adamw_layer(zone, layer, items, after, name):
    n_src, nw, r, c = zone.shape
    depth = items[0][0].shape[0]
    prevs = [p if p is not None else tuple(lax.empty((depth, r, c), F32) for _ in range(4)) for _, _, _, p in items]
    row_bytes = 2 * nw * c * (2 * n_src + 4 * 7)
    tr = max(t for t in range(8, r + 1, 8) if r % t == 0 and t * row_bytes <= ADAMW_BLOCK_BYTES)

    def body(z_ref, *rest):
        ins, outs = rest[:3 * nw], rest[7 * nw + 1:]
        for i in range(nw):
            g = z_ref[0, i].astype(F32)
            for src in range(1, n_src):
                g = g + z_ref[src, i].astype(F32)
            g_ref, d_ref, mo_ref, vo_ref = outs[4 * i:4 * i + 4]
            w_ref, m_ref, v_ref = ins[3 * i:3 * i + 3]
            g_ref[...] = g
            d_ref[...], mo_ref[...], vo_ref[...] = _adamw_math(w_ref[...], g, m_ref[...], v_ref[...])

    rows = pl.BlockSpec((None, tr, c), lambda i: (layer, i, 0))
    anywhere = pl.BlockSpec(memory_space=pl.ANY)
    outs = pl.pallas_call(
        body, name=name, grid=(r // tr,),
        in_specs=[pl.BlockSpec((n_src, nw, tr, c), lambda i: (0, 0, i, 0))] + [rows] * (3 * nw)
                 + [anywhere] * (4 * nw + 1),
        out_specs=[rows] * (4 * nw),
        out_shape=[jax.ShapeDtypeStruct((depth, r, c), F32)] * (4 * nw),
        input_output_aliases={1 + 3 * nw + k: k for k in range(4 * nw)},
        compiler_params=_params(),
    )(zone, *[t for w, m, v, _ in items for t in (w, m, v)], *[t for p in prevs for t in p], after)
    return [tuple(outs[4 * i:4 * i + 4]) for i in range(nw)]


def adamw_small(ws, recvs, ms, vs, name):
    n = len(ws)

    def body(*refs):
        w_refs, r_refs, m_refs, v_refs = (refs[i * n:(i + 1) * n] for i in range(4))
        g_refs, d_refs, mo_refs, vo_refs = (refs[(4 + i) * n:(5 + i) * n] for i in range(4))
        for i in range(n):
            g = r_refs[i][0]
            for src in range(1, N_DEV):
                g = g + r_refs[i][src]
            g_refs[i][...] = g
            d_refs[i][...], mo_refs[i][...], vo_refs[i][...] = _adamw_math(w_refs[i][...], g, m_refs[i][...],
                                                                            v_refs[i][...])

    vm = pl.BlockSpec(memory_space=pltpu.VMEM)
    outs = pl.pallas_call(
        body, name=name, in_specs=[vm] * (4 * n), out_specs=[vm] * (4 * n),
        out_shape=[jax.ShapeDtypeStruct(w.shape, F32) for w in ws] * 4,
        compiler_params=pltpu.CompilerParams(vmem_limit_bytes=V7X_VMEM_LIMIT),
    )(*ws, *recvs, *ms, *vs)
    return [outs[i * n:(i + 1) * n] for i in range(4)]


SMALL_NAMES = ("ffn1_norm", "mix_norm", "ffn2_norm", "b_gate", "na_q_norm", "na_k_norm", "sw_q_norm", "sw_k_norm",
               "na_rpb", "sw_sink", "t5_rel_table")


def kernel(x, ffn1_norm, ffn1_w_gate, ffn1_w_up, ffn1_w_down, mix_norm, w_in, b_gate, na_q_norm, na_k_norm, na_rpb, sw_q_norm, sw_k_norm, sw_sink, t5_rel_table, w_branch_na, w_branch_sw, w_out, ffn2_norm, ffn2_w_gate, ffn2_w_up, ffn2_w_down, loss_target, m_ffn1_norm, m_ffn1_w_gate, m_ffn1_w_up, m_ffn1_w_down, m_mix_norm, m_w_in, m_b_gate, m_na_q_norm, m_na_k_norm, m_na_rpb, m_sw_q_norm, m_sw_k_norm, m_sw_sink, m_t5_rel_table, m_w_branch_na, m_w_branch_sw, m_w_out, m_ffn2_norm, m_ffn2_w_gate, m_ffn2_w_up, m_ffn2_w_down, v_ffn1_norm, v_ffn1_w_gate, v_ffn1_w_up, v_ffn1_w_down, v_mix_norm, v_w_in, v_b_gate, v_na_q_norm, v_na_k_norm, v_na_rpb, v_sw_q_norm, v_sw_k_norm, v_sw_sink, v_t5_rel_table, v_w_branch_na, v_w_branch_sw, v_w_out, v_ffn2_norm, v_ffn2_w_gate, v_ffn2_w_up, v_ffn2_w_down):
    weights = dict(ffn1_norm=ffn1_norm, ffn1_w_gate=ffn1_w_gate, ffn1_w_up=ffn1_w_up, ffn1_w_down=ffn1_w_down,
                   mix_norm=mix_norm, w_in=w_in, b_gate=b_gate, na_q_norm=na_q_norm, na_k_norm=na_k_norm,
                   na_rpb=na_rpb, sw_q_norm=sw_q_norm, sw_k_norm=sw_k_norm, sw_sink=sw_sink,
                   t5_rel_table=t5_rel_table, w_branch_na=w_branch_na, w_branch_sw=w_branch_sw, w_out=w_out,
                   ffn2_norm=ffn2_norm, ffn2_w_gate=ffn2_w_gate, ffn2_w_up=ffn2_w_up, ffn2_w_down=ffn2_w_down)
    mom_m = dict(ffn1_norm=m_ffn1_norm, ffn1_w_gate=m_ffn1_w_gate, ffn1_w_up=m_ffn1_w_up, ffn1_w_down=m_ffn1_w_down,
                 mix_norm=m_mix_norm, w_in=m_w_in, b_gate=m_b_gate, na_q_norm=m_na_q_norm, na_k_norm=m_na_k_norm,
                 na_rpb=m_na_rpb, sw_q_norm=m_sw_q_norm, sw_k_norm=m_sw_k_norm, sw_sink=m_sw_sink,
                 t5_rel_table=m_t5_rel_table, w_branch_na=m_w_branch_na, w_branch_sw=m_w_branch_sw, w_out=m_w_out,
                 ffn2_norm=m_ffn2_norm, ffn2_w_gate=m_ffn2_w_gate, ffn2_w_up=m_ffn2_w_up, ffn2_w_down=m_ffn2_w_down)
    mom_v = dict(ffn1_norm=v_ffn1_norm, ffn1_w_gate=v_ffn1_w_gate, ffn1_w_up=v_ffn1_w_up, ffn1_w_down=v_ffn1_w_down,
                 mix_norm=v_mix_norm, w_in=v_w_in, b_gate=v_b_gate, na_q_norm=v_na_q_norm, na_k_norm=v_na_k_norm,
                 na_rpb=v_na_rpb, sw_q_norm=v_sw_q_norm, sw_k_norm=v_sw_k_norm, sw_sink=v_sw_sink,
                 t5_rel_table=v_t5_rel_table, w_branch_na=v_w_branch_na, w_branch_sw=v_w_branch_sw, w_out=v_w_out,
                 ffn2_norm=v_ffn2_norm, ffn2_w_gate=v_ffn2_w_gate, ffn2_w_up=v_ffn2_w_up, ffn2_w_down=v_ffn2_w_down)
    order = list(weights)

    depth = ffn1_norm.shape[0]
    s, d = x.shape[1], x.shape[2]
    xs = x[0]
    tr = lambda w: jnp.swapaxes(w, -1, -2)

    merge = lambda t: t.reshape(t.shape[0], N_DEV * t.shape[2], t.shape[3])
    no_dep = jnp.zeros((8, LANES), F32)

    def shards_of(kind, l):
        stack = lambda *ws: jnp.stack(ws).astype(BF16)
        if kind == "ffn1":
            return [stack(tr(ffn1_w_gate[l]), tr(ffn1_w_up[l]), ffn1_w_down[l])]
        if kind == "win":
            return [stack(tr(w_in[l]))]
        return [stack(tr(ffn2_w_gate[l]), tr(ffn2_w_up[l]), ffn2_w_down[l]), stack(w_out[l]),
                stack(tr(w_branch_na[l]), tr(w_branch_sw[l]))]

    shards = {(kind, l): shards_of(kind, l) for l in range(depth) for kind in ("ffn1", "win", "rest")}

    def start(kind, l, after):
        return gather_start(shards[kind, l], after, f"gather_{kind}_{l}")

    def arrive(started, kind, l, after):
        zones = gather_wait(started, after, f"gather_{kind}_{l}_wait")
        return forward_start(zones, no_dep, f"forward_{kind}_{l}")

    def finish(fwd, kind, l, after):
        return [merge(z) for z in forward_wait(fwd, after, f"forward_{kind}_{l}_wait")]

    bd = jnp.asarray(np.kron(np.eye(MXU_TILE // HEAD_DIM), np.full((HEAD_DIM, HEAD_DIM), 1.0 / HEAD_DIM)), BF16)
    bmap = jnp.asarray(_t5_bucket_map())
    tile8 = lambda g: jnp.tile(g, NA_WIDTH // HEAD_DIM).reshape(1, NA_WIDTH)
    tile2 = lambda g: jnp.tile(g, SW_KV_WIDTH // HEAD_DIM).reshape(1, SW_KV_WIDTH)

    st_first, tok = start("ffn1", 0, no_dep)
    t5b = t5_expand(t5_rel_table, bmap, tok, "t5_expand").reshape(SW_EDGES, SW_STACK, 3 * SW_BLOCK)
    t2_tables = [rpb_expand(_rpb_rows(na_rpb[l]), tok, f"rpb_expand_{l}") for l in range(depth)]
    masks = na_masks(s // GRID_W, tok, "na_masks")
    qk_gains = [(tile8(na_q_norm[l]), tile8(na_k_norm[l]), tile8(sw_q_norm[l]), tile2(sw_k_norm[l]))
                for l in range(depth)]
    early = ([t[0, 0, 0:8, :] for t in t2_tables] + [masks[0, 0:8, 0:LANES]] + [t[0, 0:8, 0:LANES].astype(F32) for v in shards.values() for t in v]
             + [g[:, 0:LANES] for gs in qk_gains for g in gs])
    fwd, _ = arrive(st_first, "ffn1", 0, functools.reduce(jnp.add, early, t5b[0, 0:8, 0:LANES]))
    st_win, dep = start("win", 0, t5b)
    (first,) = finish(fwd, "ffn1", 0, dep)

    saved = []
    layer_w = {0: dict(wg1=(first, 0), wu1=(first, 1), wd1=(first, 2))}
    cur = xs
    for l in range(depth):
        sv = {}
        lw = layer_w[l]
        sv["x0"] = cur
        cur, sv["xn1"], sv["hg1"], sv["hu1"], sv["act1"] = ffn_forward(
            cur, ffn1_norm[l][None], lw["wg1"], lw["wu1"], lw["wd1"], dep, f"ffn1_{l}")
        sv["x1"] = cur
        fwd, _ = arrive(st_win, "win", l, cur)
        st_rest, tok = start("rest", l, cur)
        (zb,) = finish(fwd, "win", l, tok)
        lw["win"] = (zb, 0)
        sv["gains"] = qk_gains[l]
        sv["hn"], sv["zq"], sv["qa"], sv["ka"], sv["qs"], sv["ks"], sv["gt"] = mix_in(
            cur, mix_norm[l][None], lw["win"], b_gate[l][None], *sv["gains"], bd, f"mix_in_{l}")
        sv["t2"] = t2_tables[l]
        sv["o_na"] = na_fwd(sv["qa"], sv["ka"], sv["zq"], sv["t2"], masks, f"na_fwd_{l}")
        dep = no_dep
        if l + 1 < depth:
            st_ffn1, dep = start("ffn1", l + 1, sv["o_na"])
        sv["o_sw"] = sw_fwd(sv["qs"], sv["ks"], sv["zq"], t5b, sw_sink[l], dep, f"sw_fwd_{l}")
        fwd, tok = arrive(st_rest, "rest", l, sv["o_sw"][0:8, 0:LANES] + sv["o_na"][0:8, 0:LANES])
        za, zc, zd = finish(fwd, "rest", l, tok)
        lw.update(wg2=(za, 0), wu2=(za, 1), wd2=(za, 2), wout=(zc, 0), wna=(zd, 0), wsw=(zd, 1))
        cur, sv["a_na"], sv["a_sw"], sv["merged"] = merge_out(
            cur, sv["o_na"], sv["o_sw"], sv["gt"], lw["wna"], lw["wsw"], lw["wout"], f"merge_out_{l}")
        sv["x2"] = cur
        if l + 1 < depth:
            st_win, dep = start("win", l + 1, cur)
            sv["xn2"], sv["hg2"], sv["hu2"], sv["act2"] = ffn_forward(
                cur, ffn2_norm[l][None], lw["wg2"], lw["wu2"], None, dep, f"ffn2_up_{l}")
            fwd, dep = arrive(st_ffn1, "ffn1", l + 1, sv["act2"])
            cur = ffn_down(cur, sv["act2"], lw["wd2"], dep, f"ffn2_down_{l}")
            (za,) = finish(fwd, "ffn1", l + 1, cur)
            layer_w[l + 1] = dict(wg1=(za, 0), wu1=(za, 1), wd1=(za, 2))
        else:
            dx, loss_acc, sv["xn2"], sv["hg2"], sv["hu2"], sv["act2"] = ffn_forward(
                cur, ffn2_norm[l][None], lw["wg2"], lw["wu2"], lw["wd2"], no_dep, f"ffn2_{l}",
                target=loss_target[0])
        dep = no_dep
        saved.append(sv)

    split = lambda t: t.reshape(N_DEV, t.shape[0] // N_DEV, t.shape[1])
    pending = {}
    last_key = "ffn1_0"
    two_level = {last_key}
    small = {k: [None] * depth for k in SMALL_NAMES if k != "t5_rel_table"}
    dbias_sw = []
    for l in reversed(range(depth)):
        sv = saved[l]
        lw = layer_w[l]
        wg1, wu1, wd1, wg2, wu2, wd2 = (lw[k] for k in ("wg1", "wu1", "wd1", "wg2", "wu2", "wd2"))
        win_t, wout_l, wna_t, wsw_t = lw["win"], lw["wout"], lw["wna"], lw["wsw"]
        blocks = ((2, "x2", "xn2", "hg2", "hu2", "act2", wg2, wu2, wd2, "ffn2_norm", 3),
                  (1, "x0", "xn1", "hg1", "hu1", "act1", wg1, wu1, wd1, "ffn1_norm", 0))

        def ffn_backward(dx, blk):
            tag, xk, xnk, hgk, huk, actk, wg, wu, wd, norm_name, slot = blk
            gains = weights[norm_name]
            dxb, dhg, dhu = ffn_bwd_act(dx, wd, sv[hgk], sv[huk], f"ffn{tag}_bwd_act_{l}")
            gwg, gwu, gwd = tn_matmul([(dhg, sv[xnk], 1.0), (dhu, sv[xnk], 1.0), (sv[actk], dxb, 0.5)],
                                      f"ffn{tag}_dw_{l}")
            key = f"ffn{tag}_{l}"
            blocks_of = [split(gwg), split(gwu), split(gwd)]
            if key in two_level:
                paired, token = pair_start(blocks_of, dxb, f"pair_{key}")
            else:
                pending[key], token = scatter_start([blocks_of], f"scatter_{key}")
            dx, dg = proj_bwd_norm([dhg, dhu], [wg, wu], sv[xk], gains[l][None], dx, token, f"ffn{tag}_bwd_x_{l}")
            token = no_dep
            if key in two_level:
                thru, land = pair_wait(paired, dx, f"pair_{key}_wait")
                pending[key], token = chip_start(pair_sum(thru, land, f"pair_sum_{key}"), dg, f"chips_{key}")
            small[norm_name][l] = dg[0]
            return dx, token

        dx, token = ffn_backward(dx, blocks[0])
        dxb, dzg, da_na, da_sw, do_na, do_sw, dbg = mix_bwd_out(
            dx, sv["gt"], sv["a_na"], sv["a_sw"], wna_t, wsw_t, wout_l, token, f"mix_bwd_out_{l}")
        small["b_gate"][l] = dbg[0]
        gwout, gwna, gwsw = tn_matmul([(sv["merged"], dxb, 1.0), (da_na, sv["o_na"], 1.0), (da_sw, sv["o_sw"], 1.0)],
                                      f"mix_dw_{l}")
        dqa, dka, dva, dt2 = na_bwd(sv["qa"], sv["ka"], sv["zq"], sv["t2"], masks, sv["o_na"], do_na, f"na_bwd_{l}")
        dqs, dks, dvs, dbias, dsink = sw_bwd(sv["qs"], sv["ks"], sv["zq"], t5b, sw_sink[l], sv["o_sw"], do_sw,
                                             f"sw_bwd_{l}")
        dbias_sw.append(dbias.reshape(SW_HEADS, SW_BLOCK, 3 * SW_BLOCK))
        small["sw_sink"][l] = jnp.sum(dsink[:, 0].reshape(SW_HEADS, SW_BLOCK), axis=1)
        small["na_rpb"][l] = _rpb_from_rows(rpb_reduce(dt2, f"rpb_reduce_{l}"))
        dz, dgqa, dgka, dgqs, dgks = qk_norm_bwd(dqa, dka, dva, dqs, dks, dvs, sv["zq"], dzg, *sv["gains"], bd,
                                                 f"qk_norm_bwd_{l}")
        fold = lambda g: jnp.sum(g.reshape(-1, HEAD_DIM), axis=0)
        small["na_q_norm"][l], small["na_k_norm"][l] = fold(dgqa), fold(dgka)
        small["sw_q_norm"][l], small["sw_k_norm"][l] = fold(dgqs), fold(dgks)
        (gwin,) = tn_matmul([(dz, sv["hn"], 1.0)], f"dwin_{l}")
        pending[f"mix_{l}"], token = scatter_start([[split(gwout)], [split(gwna), split(gwsw)], [split(gwin)]],
                                                   f"scatter_mix_{l}")
        dx, dg = proj_bwd_norm([dz], [win_t], sv["x1"], mix_norm[l][None], dx, token, f"mix_bwd_x_{l}")
        small["mix_norm"][l] = dg[0]
        dx, tail = ffn_backward(dx, blocks[1])

    dtab = t5_reduce(dbias_sw, bmap, "t5_reduce")
    small_parts = {k: jnp.stack(v) for k, v in small.items()}
    small_parts["t5_rel_table"] = jnp.transpose(dtab[:, :, 0])

    grads, delta, new_m, new_v = {}, {}, {}, {}
    state = {}
    sharing, token = share_start([small_parts[k] for k in SMALL_NAMES] + [loss_acc], tail, "share_small")
    chain = [token]
    members = {"ffn": lambda t: [(f"ffn{t}_w_gate", 0, 0, True), (f"ffn{t}_w_up", 0, 1, True),
                                 (f"ffn{t}_w_down", 0, 2, False)],
               "mix": lambda t: [("w_out", 0, 0, False), ("w_branch_na", 1, 0, True), ("w_branch_sw", 1, 1, True),
                                 ("w_in", 2, 0, True)]}

    def collect(key):
        if key in two_level:
            zones = [chip_wait(pending[key], chain[0], f"wait_{key}")]
        else:
            zones = scatter_wait(pending[key], chain[0], f"wait_{key}")
        kind, l = key.split("_")
        group = members[kind[:3]](kind[3:])
        complete = all(f"{kind}_{j}" in done for j in range(depth) if j != int(l))
        for zi, zone in enumerate(zones):
            mine = sorted((wi, k, transposed) for k, z, wi, transposed in group if z == zi)
            views = [tr if transposed else (lambda t: t) for _, _, transposed in mine]
            items = [(view(weights[k]), view(mom_m[k]), view(mom_v[k]), state.get(k))
                     for (_, k, _), view in zip(mine, views)]
            results = adamw_layer(zone, int(l), items, chain[0], f"adamw_{key}_{zi}")
            chain[0] = results[-1][1]
            for (_, k, _), view, res in zip(mine, views, results):
                state[k] = res
                if complete:
                    grads[k], delta[k], new_m[k], new_v[k] = (view(t) for t in res)
        done.add(key)

    done = set()
    for key in pending:
        if key != last_key:
            collect(key)
    collect(last_key)
    *recvs, all_losses = share_wait(sharing, chain[0], "share_small_wait")
    loss = jnp.sum(all_losses) * (0.5 / d)
    results = adamw_small([weights[k] for k in SMALL_NAMES], recvs, [mom_m[k] for k in SMALL_NAMES],
                          [mom_v[k] for k in SMALL_NAMES], "adamw_small")
    for dst, outs in zip((grads, delta, new_m, new_v), results):
        dst.update(dict(zip(SMALL_NAMES, outs)))

    return (loss, dx[None], *[grads[k] for k in order], *[delta[k] for k in order],
            *[new_m[k] for k in order], *[new_v[k] for k in order])
```

```python
import functools
import math

import numpy as np
import jax
import jax.numpy as jnp
from jax import lax
from jax.experimental import pallas as pl
from jax.experimental.pallas import tpu as pltpu

F32 = jnp.float32
BF16 = jnp.bfloat16
MESH = pl.DeviceIdType.MESH

N_DEV = 8
EPS = 1e-6
NEG = -1e30
HEAD_DIM = 64
GRID_W = 64
NA_ROWS = 8
NA_COLS = 16
NA_WIDTH = 512
SW_Q_WIDTH = 512
SW_KV_WIDTH = 128
SW_BLOCK = 128
SW_HEADS = 8
SW_REP = 4
REL_BUCKETS = 32
REL_MAX_DIST = 128
QKV_WIDTH = 3 * NA_WIDTH + SW_Q_WIDTH + 2 * SW_KV_WIDTH
SCALE = 1.0 / math.sqrt(HEAD_DIM)

ADAM_LR = 0.001
ADAM_B1 = 0.9
ADAM_B2 = 0.999
ADAM_EPS = 1e-08
ADAM_WD = 0.01
ADAM_STEP = 10

V7X_VMEM_LIMIT = 56 * 1024 * 1024
LANES = 128
MXU_TILE = 256

NT = (((1,), (1,)), ((), ()))
TN = (((0,), (0,)), ((), ()))


def _params(n_grid=1):
    return pltpu.CompilerParams(dimension_semantics=("arbitrary",) * n_grid,
                                vmem_limit_bytes=V7X_VMEM_LIMIT)


def _row_tile(s):
    for t in (512, 256, 128, 64, 32, 16, 8):
        if s % t == 0:
            return t
    raise ValueError(s)


def _tn_tile(n):
    best = max(t for t in range(LANES, min(n, 2304) + 1, LANES) if n % t == 0) if n % LANES == 0 else n
    return best // 2 if best == n and n >= 1024 else best


ONCE = pl.Buffered(1)


def _col_chunk(n):
    return MXU_TILE if n % MXU_TILE == 0 else n


def _dot(a, b):
    return jnp.dot(a, b, preferred_element_type=F32)


def _dotg(a, b, dn):
    return lax.dot_general(a, b, dn, preferred_element_type=F32)


def _sigmoid(v):
    return 1.0 / (1.0 + jnp.exp(-v))


def _rstd(xv):
    return lax.rsqrt(jnp.mean(xv * xv, axis=-1, keepdims=True) + EPS)


def _full(shape):
    nd = len(shape)
    return pl.BlockSpec(shape, lambda i, _n=nd: (0,) * _n)


def _rows(tm, width):
    return pl.BlockSpec((tm, width), lambda i: (i, 0))


def _mat(stack, idx):
    return pl.BlockSpec((None,) + tuple(stack.shape[1:]), lambda i, _w=idx: (_w, 0, 0), pipeline_mode=ONCE)


def _group_mean(v, bd):
    w = bd.shape[0]
    if v.shape[1] > w:
        return jnp.concatenate([_group_mean(v[:, c0:c0 + w], bd) for c0 in range(0, v.shape[1], w)], axis=1)
    hi = v.astype(BF16)
    lo = (v - hi.astype(F32)).astype(BF16)
    return _dot(hi, bd) + _dot(lo, bd)


def _loss_tile(y, t_ref, dy_ref, acc_ref):
    tm, d = y.shape

    @pl.when(pl.program_id(0) == 0)
    def _():
        acc_ref[...] = jnp.zeros(acc_ref.shape, F32)

    err = y - t_ref[...]
    dy_ref[...] = err * (1.0 / d)
    part = jnp.sum((err * err).reshape(tm // 8, 8, d), axis=0)
    acc = part[:, 0:LANES]
    for c0 in range(LANES, d, LANES):
        acc = acc + part[:, c0:c0 + LANES]
    acc_ref[...] = acc_ref[...] + acc


def ffn_forward(x, gain, wg_t, wu_t, wd, dep, name, target=None):
    s, d = x.shape
    f = wg_t[0].shape[1]
    tm = _row_tile(s) if wd is None else min(_row_tile(s), 256)
    fc = _col_chunk(f)
    nw = 2 if wd is None else 3
    n_in = nw + (1 if target is None else 2)

    def body(x_ref, g_ref, *refs):
        w_refs, outs = refs[:nw], refs[n_in:]
        xn_ref, dg_ref, du_ref, act_ref = outs[-4:]
        xv = x_ref[...]
        xn = (xv * _rstd(xv) * g_ref[...]).astype(BF16)
        xn_ref[...] = xn
        for c0 in range(0, f, fc):
            hg = _dotg(xn, w_refs[0][c0:c0 + fc, :], NT)
            hu = _dotg(xn, w_refs[1][c0:c0 + fc, :], NT)
            sg = _sigmoid(hg)
            silu = hg * sg
            du_ref[:, c0:c0 + fc] = silu.astype(BF16)
            dg_ref[:, c0:c0 + fc] = (hu * (sg + silu * (1.0 - sg))).astype(BF16)
            act_ref[:, c0:c0 + fc] = (silu * hu).astype(BF16)
        if wd is not None:
            y = xv + 0.5 * _dot(act_ref[...], w_refs[2][...])
            if target is None:
                outs[0][...] = y
            else:
                _loss_tile(y, refs[nw + 1], outs[0], outs[1])

    weights = [wg_t, wu_t] + ([] if wd is None else [wd])
    in_specs = [_rows(tm, d), _full((1, d))] + [_mat(*w) for w in weights] + [_full(dep.shape)]
    operands = [x, gain, *[w[0] for w in weights], dep]
    out_specs = [_rows(tm, d), _rows(tm, f), _rows(tm, f), _rows(tm, f)]
    out_shape = [jax.ShapeDtypeStruct((s, d), BF16)] + [jax.ShapeDtypeStruct((s, f), BF16)] * 3
    if wd is not None:
        out_specs, out_shape = [_rows(tm, d)] + out_specs, [jax.ShapeDtypeStruct((s, d), F32)] + out_shape
    if target is not None:
        in_specs, operands = in_specs + [_rows(tm, d)], operands + [target]
        out_specs = out_specs[:1] + [_full((8, LANES))] + out_specs[1:]
        out_shape = out_shape[:1] + [jax.ShapeDtypeStruct((8, LANES), F32)] + out_shape[1:]
    return pl.pallas_call(
        body, name=name, grid=(s // tm,), in_specs=in_specs, out_specs=out_specs, out_shape=out_shape,
        compiler_params=_params(),
    )(*operands)


def ffn_down(x, act, wd, dep, name):
    s, d = x.shape
    f = act.shape[1]
    tm = _row_tile(s)

    def body(x_ref, a_ref, w_ref, dep_ref, o_ref):
        o_ref[...] = x_ref[...] + 0.5 * _dot(a_ref[...], w_ref[...])

    return pl.pallas_call(
        body, name=name, grid=(s // tm,),
        in_specs=[_rows(tm, d), _rows(tm, f), _mat(*wd), _full(dep.shape)],
        out_specs=_rows(tm, d),
        out_shape=jax.ShapeDtypeStruct((s, d), F32),
        compiler_params=_params(),
    )(x, act, wd[0], dep)


def mix_in(x, gain, win_t, b_gate, gq_na, gk_na, gq_sw, gk_sw, bd, name):
    s, d = x.shape
    tm = _row_tile(s)
    gc = _col_chunk(2 * d)

    def body(x_ref, g_ref, w_ref, b_ref, gqa_ref, gka_ref, gqs_ref, gks_ref, bd_ref,
             hn_ref, zq_ref, qa_ref, ka_ref, qs_ref, ks_ref, gt_ref):
        xv = x_ref[...]
        hn = (xv * _rstd(xv) * g_ref[...]).astype(BF16)
        hn_ref[...] = hn

        def proj(c0, c1):
            return _dotg(hn, w_ref[c0:c1, :], NT)

        def headnorm(z, g, bdm):
            return z * lax.rsqrt(_group_mean(z * z, bdm) + EPS) * g

        bd512 = bd_ref[...]
        bd128 = bd_ref[0:SW_KV_WIDTH, 0:SW_KV_WIDTH]
        z = proj(0, 512)
        zq_ref[:, 0:512] = z.astype(BF16)
        qa_ref[...] = (headnorm(z, gqa_ref[...], bd512) * SCALE).astype(BF16)
        z = proj(512, 1024)
        zq_ref[:, 512:1024] = z.astype(BF16)
        ka_ref[...] = headnorm(z, gka_ref[...], bd512).astype(BF16)
        z = proj(1024, 1536)
        zq_ref[:, 1024:1536] = z.astype(BF16)
        z = proj(1536, 2048)
        zq_ref[:, 1536:2048] = z.astype(BF16)
        qs_ref[...] = (headnorm(z, gqs_ref[...], bd512) * SCALE).astype(BF16)
        z = proj(2048, 2176)
        zq_ref[:, 2048:2176] = z.astype(BF16)
        ks_ref[...] = headnorm(z, gks_ref[...], bd128).astype(BF16)
        z = proj(2176, 2304)
        zq_ref[:, 2176:2304] = z.astype(BF16)
        for c0 in range(0, 2 * d, gc):
            zg = proj(QKV_WIDTH + c0, QKV_WIDTH + c0 + gc) + b_ref[:, c0:c0 + gc]
            gt_ref[:, c0:c0 + gc] = _sigmoid(zg).astype(BF16)

    return pl.pallas_call(
        body, name=name, grid=(s // tm,),
        in_specs=[_rows(tm, d), _full((1, d)), _mat(*win_t), _full((1, 2 * d)),
                  _full((1, 512)), _full((1, 512)), _full((1, 512)), _full((1, 128)), _full((MXU_TILE, MXU_TILE))],
        out_specs=[_rows(tm, d), _rows(tm, QKV_WIDTH), _rows(tm, 512), _rows(tm, 512), _rows(tm, 512),
                   _rows(tm, 128), _rows(tm, 2 * d)],
        out_shape=[jax.ShapeDtypeStruct((s, d), BF16), jax.ShapeDtypeStruct((s, QKV_WIDTH), BF16),
                   jax.ShapeDtypeStruct((s, 512), BF16), jax.ShapeDtypeStruct((s, 512), BF16),
                   jax.ShapeDtypeStruct((s, 512), BF16), jax.ShapeDtypeStruct((s, 128), BF16),
                   jax.ShapeDtypeStruct((s, 2 * d), BF16)],
        compiler_params=_params(),
    )(x, gain, win_t[0], b_gate, gq_na, gk_na, gq_sw, gk_sw, bd)


def _na_iotas():
    qc = lax.broadcasted_iota(jnp.int32, (GRID_W, LANES), 0)
    ln = lax.broadcasted_iota(jnp.int32, (GRID_W, LANES), 1)
    low = ln < GRID_W
    kc = jnp.where(low, ln, ln - GRID_W)
    diff = kc - qc + (NA_COLS - 1)
    qcs = jnp.clip(qc - NA_COLS // 2, 0, GRID_W - NA_COLS)
    inwin = (kc >= qcs) & (kc < qcs + NA_COLS)
    return diff, low, inwin


NA_RI = 2 * NA_ROWS - 1
NA_CI = 2 * NA_COLS - 1
NA_T2 = NA_RI + 1


def _rpb_rows(rpb):
    h = rpb.shape[0]
    padded = jnp.pad(rpb, ((0, 0), (1, 1), (0, GRID_W - NA_CI)))
    return jnp.concatenate([padded[:, :NA_T2], padded[:, 1:NA_T2 + 1]], axis=2).reshape(h, NA_T2, LANES)


def _rpb_from_rows(rows):
    return rows[:, 1:, :NA_CI] + rows[:, :NA_RI, GRID_W:GRID_W + NA_CI]


def rpb_expand(rows, dep, name):
    n_heads = rows.shape[0]

    def body(r_ref, dep_ref, o_ref):
        for h in range(n_heads):
            for e in range(NA_T2):
                line = jnp.broadcast_to(r_ref[h, e:e + 1, :], (GRID_W, LANES))
                o_ref[h, e] = pltpu.roll(line, LANES - (NA_COLS - 1), 1, stride=1, stride_axis=0)

    return pl.pallas_call(
        body, name=name,
        in_specs=[pl.BlockSpec(memory_space=pltpu.VMEM), pl.BlockSpec(memory_space=pltpu.VMEM)],
        out_specs=pl.BlockSpec(memory_space=pltpu.VMEM),
        out_shape=jax.ShapeDtypeStruct((n_heads, NA_T2, GRID_W, LANES), F32),
        compiler_params=pltpu.CompilerParams(vmem_limit_bytes=V7X_VMEM_LIMIT),
    )(rows, dep)


def rpb_reduce(dt2, name):
    n_heads = dt2.shape[0]
    flip = jnp.asarray(np.eye(GRID_W)[::-1], BF16)

    def body(d_ref, j_ref, o_ref):
        jm = j_ref[...]
        for h in range(n_heads):
            for e in range(NA_T2):
                dv = d_ref[h, e]
                hi = dv.astype(BF16)
                mid = (dv - hi.astype(F32)).astype(BF16)
                lo = (dv - hi.astype(F32) - mid.astype(F32)).astype(BF16)
                rev = _dot(jm, hi) + _dot(jm, mid) + _dot(jm, lo)
                back = pltpu.roll(rev, LANES + (NA_COLS - 1) - (GRID_W - 1), 1, stride=1, stride_axis=0)
                o_ref[h, e:e + 1, :] = jnp.sum(back, axis=0, keepdims=True)

    return pl.pallas_call(
        body, name=name,
        in_specs=[pl.BlockSpec(memory_space=pltpu.VMEM)] * 2,
        out_specs=pl.BlockSpec(memory_space=pltpu.VMEM),
        out_shape=jax.ShapeDtypeStruct((n_heads, NA_T2, LANES), F32),
        compiler_params=pltpu.CompilerParams(vmem_limit_bytes=V7X_VMEM_LIMIT),
    )(dt2, flip)


NA_TQ = 4
NA_TK = NA_TQ + NA_ROWS
NA_KCH = NA_TK // 2


def _na_tile_geometry(t, rows):
    r = t * NA_TQ
    kbase = jnp.clip(r - NA_ROWS // 2, 0, rows - NA_TK)
    starts = [jnp.clip(r + a - NA_ROWS // 2, 0, rows - NA_ROWS) for a in range(NA_TQ)]
    return r, kbase, starts


def _na_tile_mask(kbase, starts, low, inwin):
    half = jnp.where(low, 0, 1)
    cols = []
    for c in range(NA_KCH):
        krow = kbase + 2 * c + half
        cols.append(jnp.concatenate(
            [jnp.where(inwin & (krow >= st) & (krow < st + NA_ROWS), 0.0, NEG) for st in starts], axis=0))
    return jnp.concatenate(cols, axis=1)


def na_masks(rows, dep, name):
    n_tiles = rows // NA_TQ
    assert n_tiles >= 2 and NA_TQ >= NA_ROWS // 2

    def body(dep_ref, o_ref):
        _, low, inwin = _na_iotas()
        for e, t in enumerate((0, 1, n_tiles - 1)):
            _, kbase, starts = _na_tile_geometry(t, rows)
            o_ref[e] = _na_tile_mask(kbase, starts, low, inwin)

    return pl.pallas_call(
        body, name=name,
        in_specs=[pl.BlockSpec(memory_space=pltpu.VMEM)], out_specs=pl.BlockSpec(memory_space=pltpu.VMEM),
        out_shape=jax.ShapeDtypeStruct((3, NA_TQ * GRID_W, NA_TK * GRID_W), F32),
        compiler_params=pltpu.CompilerParams(vmem_limit_bytes=V7X_VMEM_LIMIT),
    )(dep)


def _na_tile_index(r, kbase, a, c):
    return jnp.clip(kbase + 2 * c - (r + a) + NA_ROWS, 0, NA_T2 - 1)


def _na_tile_scores(q, k, t2_ref, hh, r, kbase, madd):
    bias = jnp.concatenate(
        [jnp.concatenate([t2_ref[hh, _na_tile_index(r, kbase, a, c)] for a in range(NA_TQ)], axis=0)
         for c in range(NA_KCH)], axis=1)
    return _dotg(q, k, NT) + bias + madd


def _softmax_rows(sc):
    e = jnp.exp(sc - jnp.max(sc, axis=1, keepdims=True))
    return e * (1.0 / jnp.sum(e, axis=1, keepdims=True))


def na_fwd(qa, ka, zq, t2, masks, name):
    s = qa.shape[0]
    rows = s // GRID_W
    n_pairs = NA_WIDTH // LANES
    v_blk0 = (2 * NA_WIDTH) // LANES

    assert rows % NA_TQ == 0 and rows >= NA_TK
    tq, tk = NA_TQ * GRID_W, NA_TK * GRID_W

    def body(q_ref, k_ref, v_ref, t2_ref, m_ref, o_ref, s_scr, p_scr):
        def tile(t, carry):
            r, kbase, _ = _na_tile_geometry(t, rows)
            madd = m_ref[_edge_variant(t, rows // NA_TQ)]
            qr = pl.ds(pl.multiple_of(r * GRID_W, tq), tq)
            kr = pl.ds(pl.multiple_of(kbase * GRID_W, tq), tk)
            for hh in range(2):
                lanes = slice(HEAD_DIM * hh, HEAD_DIM * (hh + 1))
                s_scr[tq * hh:tq * (hh + 1), :] = _na_tile_scores(q_ref[qr, lanes], k_ref[kr, lanes], t2_ref, hh, r,
                                                                  kbase, madd)
            p_scr[...] = _softmax_rows(s_scr[...]).astype(BF16)
            for hh in range(2):
                lanes = slice(HEAD_DIM * hh, HEAD_DIM * (hh + 1))
                o_ref[qr, lanes] = _dot(p_scr[tq * hh:tq * (hh + 1), :], v_ref[kr, lanes]).astype(BF16)
            return carry

        lax.fori_loop(0, rows // NA_TQ, tile, 0, unroll=2)

    col = lambda off: pl.BlockSpec((s, LANES), lambda p, _o=off: (0, _o + p))
    return pl.pallas_call(
        body, name=name, grid=(n_pairs,),
        in_specs=[col(0), col(0), col(v_blk0),
                  pl.BlockSpec((2, NA_T2, GRID_W, LANES), lambda p: (p, 0, 0, 0)),
                  pl.BlockSpec(masks.shape, lambda p: (0, 0, 0), pipeline_mode=ONCE)],
        out_specs=col(0),
        out_shape=jax.ShapeDtypeStruct((s, NA_WIDTH), BF16),
        scratch_shapes=[pltpu.VMEM((2 * tq, tk), F32), pltpu.VMEM((2 * tq, tk), BF16)],
        compiler_params=_params(),
    )(qa, ka, zq, t2, masks)


def na_bwd(qa, ka, zq, t2, masks, o_na, do_na, name):
    s = qa.shape[0]
    rows = s // GRID_W
    n_pairs = NA_WIDTH // LANES
    v_blk0 = (2 * NA_WIDTH) // LANES

    tq, tk = NA_TQ * GRID_W, NA_TK * GRID_W

    def body(q_ref, k_ref, v_ref, t2_ref, m_ref, o_ref, do_ref, dq_ref, dk_ref, dv_ref, dt2_ref):
        dk_ref[...] = jnp.zeros(dk_ref.shape, F32)
        dv_ref[...] = jnp.zeros(dv_ref.shape, F32)
        dt2_ref[...] = jnp.zeros(dt2_ref.shape, F32)

        def tile(t, carry):
            r, kbase, _ = _na_tile_geometry(t, rows)
            madd = m_ref[_edge_variant(t, rows // NA_TQ)]
            qr = pl.ds(pl.multiple_of(r * GRID_W, tq), tq)
            kr = pl.ds(pl.multiple_of(kbase * GRID_W, tq), tk)
            for hh in range(2):
                lanes = slice(HEAD_DIM * hh, HEAD_DIM * (hh + 1))
                q, k, v = q_ref[qr, lanes], k_ref[kr, lanes], v_ref[kr, lanes]
                p = _softmax_rows(_na_tile_scores(q, k, t2_ref, hh, r, kbase, madd))
                do = do_ref[qr, lanes]
                delta = jnp.sum(do.astype(F32) * o_ref[qr, lanes].astype(F32), axis=1, keepdims=True)
                ds = p * (_dotg(do, v, NT) - delta)
                shared = {}
                for a in range(NA_TQ):
                    for c in range(NA_KCH):
                        shared.setdefault(2 * c - a, []).append(
                            ds[GRID_W * a:GRID_W * (a + 1), LANES * c:LANES * (c + 1)])
                for offset, parts in shared.items():
                    e = jnp.clip(offset + kbase - r + NA_ROWS, 0, NA_T2 - 1)
                    dt2_ref[hh, e] = dt2_ref[hh, e] + functools.reduce(jnp.add, parts)
                dsb = ds.astype(BF16)
                dq_ref[qr, lanes] = _dot(dsb, k)
                dk_ref[kr, lanes] = dk_ref[kr, lanes] + _dotg(dsb, q, TN)
                dv_ref[kr, lanes] = dv_ref[kr, lanes] + _dotg(p.astype(BF16), do, TN)
            return carry

        lax.fori_loop(0, rows // NA_TQ, tile, 0, unroll=2)

    col = lambda off: pl.BlockSpec((s, LANES), lambda p, _o=off: (0, _o + p))
    t2spec = pl.BlockSpec((2, NA_T2, GRID_W, LANES), lambda p: (p, 0, 0, 0))
    return pl.pallas_call(
        body, name=name, grid=(n_pairs,),
        in_specs=[col(0), col(0), col(v_blk0), t2spec,
                  pl.BlockSpec(masks.shape, lambda p: (0, 0, 0), pipeline_mode=ONCE), col(0), col(0)],
        out_specs=[col(0), col(0), col(0), t2spec],
        out_shape=[jax.ShapeDtypeStruct((s, NA_WIDTH), F32)] * 3 + [jax.ShapeDtypeStruct(t2.shape, F32)],
        compiler_params=_params(),
    )(qa, ka, zq, t2, masks, o_na, do_na)


def _t5_bucket_map():
    rel = np.arange(3 * SW_BLOCK)[None, :] - SW_BLOCK - np.arange(SW_BLOCK)[:, None]
    nb = REL_BUCKETS // 2
    max_exact = nb // 2
    n = np.abs(rel)
    large = max_exact + (np.log(np.maximum(n, 1) / max_exact)
                         / np.log(REL_MAX_DIST / max_exact) * (nb - max_exact)).astype(np.int32)
    large = np.minimum(large, nb - 1)
    return ((rel > 0) * nb + np.where(n < max_exact, n, large)).astype(np.int32)


def t5_expand(table, bmap, dep, name):
    def body(tab_ref, bm_ref, dep_ref, o_ref):
        bm = bm_ref[...]
        j, inwin = _sw_mask_iotas()
        masks = [jnp.where(keep, 0.0, NEG) for keep in (inwin & (j >= SW_BLOCK), inwin, inwin & (j < 2 * SW_BLOCK))]
        for h in range(SW_HEADS):
            t = jnp.zeros(bm.shape, F32)
            for b in range(REL_BUCKETS):
                t = jnp.where(bm == b, tab_ref[b, h], t)
            for e, madd in enumerate(masks):
                o_ref[e, h] = t + madd

    return pl.pallas_call(
        body, name=name,
        in_specs=[pl.BlockSpec(memory_space=pltpu.SMEM), pl.BlockSpec(memory_space=pltpu.VMEM),
                  pl.BlockSpec(memory_space=pltpu.VMEM)],
        out_specs=pl.BlockSpec(memory_space=pltpu.VMEM),
        out_shape=jax.ShapeDtypeStruct((SW_EDGES, SW_HEADS) + bmap.shape, F32),
        compiler_params=pltpu.CompilerParams(vmem_limit_bytes=V7X_VMEM_LIMIT),
    )(table, bmap, dep)


def t5_reduce(dbias_list, bmap, name):
    n = len(dbias_list)

    def body(*refs):
        d_refs, bm_ref, o_ref = refs[:n], refs[n], refs[n + 1]
        bm = bm_ref[...]
        for h in range(SW_HEADS):
            dv = d_refs[0][h]
            for other in d_refs[1:]:
                dv = dv + other[h]
            rows = [jnp.sum(jnp.where(bm == b, dv, 0.0), axis=0, keepdims=True) for b in range(REL_BUCKETS)]
            r = jnp.concatenate(rows, axis=0)
            o_ref[h] = jnp.broadcast_to(jnp.sum(r, axis=1, keepdims=True), (REL_BUCKETS, LANES))

    return pl.pallas_call(
        body, name=name,
        in_specs=[pl.BlockSpec(memory_space=pltpu.VMEM)] * (n + 1),
        out_specs=pl.BlockSpec(memory_space=pltpu.VMEM),
        out_shape=jax.ShapeDtypeStruct((SW_HEADS, REL_BUCKETS, LANES), F32),
        compiler_params=pltpu.CompilerParams(vmem_limit_bytes=V7X_VMEM_LIMIT),
    )(*dbias_list, bmap)


def _sw_mask_iotas():
    a = lax.broadcasted_iota(jnp.int32, (SW_BLOCK, 3 * SW_BLOCK), 0)
    j = lax.broadcasted_iota(jnp.int32, (SW_BLOCK, 3 * SW_BLOCK), 1)
    inwin = jnp.abs(j - SW_BLOCK - a) <= SW_BLOCK
    return j, inwin


SW_STACK = SW_HEADS * SW_BLOCK
SW_EDGES = 3


def _edge_variant(n, nb):
    return jnp.where(n == 0, 0, jnp.where(n == nb - 1, 2, 1))


def _sw_softmax(sc, sk):
    m = jnp.maximum(jnp.max(sc, axis=1, keepdims=True), sk)
    e = jnp.exp(sc - m)
    es = jnp.exp(sk - m)
    inv = 1.0 / (jnp.sum(e, axis=1, keepdims=True) + es)
    return e * inv, es * inv


def _sw_prologue(k_ref, v_ref, kp, vp, sink_ref, s):
    pad = s + 2 * SW_BLOCK
    zeros = jnp.zeros((SW_BLOCK, SW_KV_WIDTH), BF16)
    kp[0:SW_BLOCK, :] = zeros
    vp[0:SW_BLOCK, :] = zeros
    kp[SW_BLOCK + s:pad, :] = zeros
    vp[SW_BLOCK + s:pad, :] = zeros
    kp[SW_BLOCK:SW_BLOCK + s, :] = k_ref[...]
    vp[SW_BLOCK:SW_BLOCK + s, :] = v_ref[...]
    return jnp.concatenate([jnp.full((SW_BLOCK, 1), sink_ref[h], F32) for h in range(SW_HEADS)], axis=0)


def sw_fwd(qs, ks, zq, t5b, sink, dep, name):
    s = qs.shape[0]
    nb = s // SW_BLOCK
    v_blk = (3 * NA_WIDTH + SW_Q_WIDTH + SW_KV_WIDTH) // LANES
    pad = s + 2 * SW_BLOCK
    assert nb >= 2

    def body(q_ref, k_ref, v_ref, b_ref, sink_ref, dep_ref, o_ref, kp, vp, s_scr, p_scr):
        sink_col = _sw_prologue(k_ref, v_ref, kp, vp, sink_ref, s)

        def blk(n, carry):
            q0 = pl.multiple_of(n * SW_BLOCK, SW_BLOCK)
            qr, kr = pl.ds(q0, SW_BLOCK), pl.ds(q0, 3 * SW_BLOCK)
            for h in range(SW_HEADS):
                g = h // SW_REP
                s_scr[SW_BLOCK * h:SW_BLOCK * (h + 1), :] = _dotg(
                    q_ref[qr, HEAD_DIM * h:HEAD_DIM * (h + 1)], kp[kr, HEAD_DIM * g:HEAD_DIM * (g + 1)], NT)
            p, _ = _sw_softmax(s_scr[...] + b_ref[_edge_variant(n, nb)], sink_col)
            p_scr[...] = p.astype(BF16)
            for h in range(SW_HEADS):
                g = h // SW_REP
                o_ref[qr, HEAD_DIM * h:HEAD_DIM * (h + 1)] = _dot(
                    p_scr[SW_BLOCK * h:SW_BLOCK * (h + 1), :], vp[kr, HEAD_DIM * g:HEAD_DIM * (g + 1)]).astype(BF16)
            return carry

        lax.fori_loop(0, nb, blk, 0, unroll=2)

    return pl.pallas_call(
        body, name=name, grid=(1,),
        in_specs=[_full((s, SW_Q_WIDTH)), _full((s, SW_KV_WIDTH)),
                  pl.BlockSpec((s, SW_KV_WIDTH), lambda i: (0, v_blk)),
                  pl.BlockSpec(t5b.shape, lambda i: (0, 0, 0), pipeline_mode=ONCE), pl.BlockSpec(memory_space=pltpu.SMEM),
                  _full(dep.shape)],
        out_specs=_full((s, SW_Q_WIDTH)),
        out_shape=jax.ShapeDtypeStruct((s, SW_Q_WIDTH), BF16),
        scratch_shapes=[pltpu.VMEM((pad, SW_KV_WIDTH), BF16), pltpu.VMEM((pad, SW_KV_WIDTH), BF16),
                        pltpu.VMEM((SW_STACK, 3 * SW_BLOCK), F32), pltpu.VMEM((SW_STACK, 3 * SW_BLOCK), BF16)],
        compiler_params=_params(),
    )(qs, ks, zq, t5b, sink, dep)


def sw_bwd(qs, ks, zq, t5b, sink, o_sw, do_sw, name):
    s = qs.shape[0]
    nb = s // SW_BLOCK
    v_blk = (3 * NA_WIDTH + SW_Q_WIDTH + SW_KV_WIDTH) // LANES
    pad = s + 2 * SW_BLOCK

    def body(q_ref, k_ref, v_ref, b_ref, sink_ref, o_ref, do_ref,
             dq_ref, dk_ref, dv_ref, db_ref, dsk_ref, kp, vp, dkp, dvp, s_scr, dp_scr, ds_scr, p_scr):
        sink_col = _sw_prologue(k_ref, v_ref, kp, vp, sink_ref, s)
        dkp[...] = jnp.zeros(dkp.shape, F32)
        dvp[...] = jnp.zeros(dvp.shape, F32)
        db_ref[...] = jnp.zeros(db_ref.shape, F32)
        dsk_ref[...] = jnp.zeros(dsk_ref.shape, F32)

        def blk(n, carry):
            q0 = pl.multiple_of(n * SW_BLOCK, SW_BLOCK)
            qr, kr = pl.ds(q0, SW_BLOCK), pl.ds(q0, 3 * SW_BLOCK)
            deltas = []
            for h in range(SW_HEADS):
                g = h // SW_REP
                hl, kl = slice(HEAD_DIM * h, HEAD_DIM * (h + 1)), slice(HEAD_DIM * g, HEAD_DIM * (g + 1))
                rows = slice(SW_BLOCK * h, SW_BLOCK * (h + 1))
                do = do_ref[qr, hl]
                s_scr[rows, :] = _dotg(q_ref[qr, hl], kp[kr, kl], NT)
                dp_scr[rows, :] = _dotg(do, vp[kr, kl], NT)
                deltas.append(jnp.sum(do.astype(F32) * o_ref[qr, hl].astype(F32), axis=1, keepdims=True))
            delta = jnp.concatenate(deltas, axis=0)
            p, ps = _sw_softmax(s_scr[...] + b_ref[_edge_variant(n, nb)], sink_col)
            ds = p * (dp_scr[...] - delta)
            db_ref[...] = db_ref[...] + ds
            dsk_ref[...] = dsk_ref[...] - jnp.broadcast_to(ps * delta, (SW_STACK, LANES))
            ds_scr[...] = ds.astype(BF16)
            p_scr[...] = p.astype(BF16)
            for g in range(SW_HEADS // SW_REP):
                kl = slice(HEAD_DIM * g, HEAD_DIM * (g + 1))
                k = kp[kr, kl]
                dkw = jnp.zeros((3 * SW_BLOCK, HEAD_DIM), F32)
                dvw = jnp.zeros((3 * SW_BLOCK, HEAD_DIM), F32)
                for r in range(SW_REP):
                    h = g * SW_REP + r
                    hl, rows = slice(HEAD_DIM * h, HEAD_DIM * (h + 1)), slice(SW_BLOCK * h, SW_BLOCK * (h + 1))
                    dsb = ds_scr[rows, :]
                    dq_ref[qr, hl] = _dot(dsb, k)
                    dkw = dkw + _dotg(dsb, q_ref[qr, hl], TN)
                    dvw = dvw + _dotg(p_scr[rows, :], do_ref[qr, hl], TN)
                dkp[kr, kl] = dkp[kr, kl] + dkw
                dvp[kr, kl] = dvp[kr, kl] + dvw
            return carry

        lax.fori_loop(0, nb, blk, 0)
        dk_ref[...] = dkp[SW_BLOCK:SW_BLOCK + s, :]
        dv_ref[...] = dvp[SW_BLOCK:SW_BLOCK + s, :]

    assert nb >= 2
    bias_spec = _full((SW_STACK, 3 * SW_BLOCK))
    return pl.pallas_call(
        body, name=name, grid=(1,),
        in_specs=[_full((s, SW_Q_WIDTH)), _full((s, SW_KV_WIDTH)),
                  pl.BlockSpec((s, SW_KV_WIDTH), lambda i: (0, v_blk)),
                  pl.BlockSpec(t5b.shape, lambda i: (0, 0, 0), pipeline_mode=ONCE), pl.BlockSpec(memory_space=pltpu.SMEM),
                  _full((s, SW_Q_WIDTH)), _full((s, SW_Q_WIDTH))],
        out_specs=[_full((s, SW_Q_WIDTH)), _full((s, SW_KV_WIDTH)), _full((s, SW_KV_WIDTH)), bias_spec,
                   _full((SW_STACK, LANES))],
        out_shape=[jax.ShapeDtypeStruct((s, SW_Q_WIDTH), F32), jax.ShapeDtypeStruct((s, SW_KV_WIDTH), F32),
                   jax.ShapeDtypeStruct((s, SW_KV_WIDTH), F32),
                   jax.ShapeDtypeStruct((SW_STACK, 3 * SW_BLOCK), F32),
                   jax.ShapeDtypeStruct((SW_STACK, LANES), F32)],
        scratch_shapes=[pltpu.VMEM((pad, SW_KV_WIDTH), BF16), pltpu.VMEM((pad, SW_KV_WIDTH), BF16),
                        pltpu.VMEM((pad, SW_KV_WIDTH), F32), pltpu.VMEM((pad, SW_KV_WIDTH), F32),
                        pltpu.VMEM((SW_STACK, 3 * SW_BLOCK), F32), pltpu.VMEM((SW_STACK, 3 * SW_BLOCK), F32),
                        pltpu.VMEM((SW_STACK, 3 * SW_BLOCK), BF16), pltpu.VMEM((SW_STACK, 3 * SW_BLOCK), BF16)],
        compiler_params=_params(),
    )(qs, ks, zq, t5b, sink, o_sw, do_sw)


def merge_out(x, o_na, o_sw, gt, wbna_t, wbsw_t, wout, name):
    s, d = x.shape
    tm = _row_tile(s)

    def body(x_ref, ona_ref, osw_ref, gt_ref, wna_ref, wsw_ref, wo_ref, xo_ref, ana_ref, asw_ref, mg_ref):
        a_na = _dotg(ona_ref[...], wna_ref[...], NT)
        a_sw = _dotg(osw_ref[...], wsw_ref[...], NT)
        g_na, g_sw = gt_ref[:, 0:d].astype(F32), gt_ref[:, d:2 * d].astype(F32)
        ana_ref[...] = (a_na * g_na * (1.0 - g_na)).astype(BF16)
        asw_ref[...] = (a_sw * g_sw * (1.0 - g_sw)).astype(BF16)
        merged = (g_na * a_na + g_sw * a_sw).astype(BF16)
        mg_ref[...] = merged
        xo_ref[...] = x_ref[...] + _dot(merged, wo_ref[...])

    return pl.pallas_call(
        body, name=name, grid=(s // tm,),
        in_specs=[_rows(tm, d), _rows(tm, 512), _rows(tm, 512), _rows(tm, 2 * d),
                  _mat(*wbna_t), _mat(*wbsw_t), _mat(*wout)],
        out_specs=[_rows(tm, d)] * 4,
        out_shape=[jax.ShapeDtypeStruct((s, d), F32)] + [jax.ShapeDtypeStruct((s, d), BF16)] * 3,
        compiler_params=_params(),
    )(x, o_na, o_sw, gt, wbna_t[0], wbsw_t[0], wout[0])


def mix_bwd_out(dx, gt, a_na, a_sw, wbna_t, wbsw_t, wout, dep, name):
    s, d = dx.shape
    tm = _row_tile(s)

    def body(dx_ref, gt_ref, ana_ref, asw_ref, wna_ref, wsw_ref, wo_ref, dep_ref,
             dxb_ref, dzg_ref, dana_ref, dasw_ref, dona_ref, dosw_ref, dbg_ref):
        @pl.when(pl.program_id(0) == 0)
        def _():
            dbg_ref[...] = jnp.zeros(dbg_ref.shape, F32)

        dxb = dx_ref[...].astype(BF16)
        dxb_ref[...] = dxb
        dm = _dotg(dxb, wo_ref[...], NT)
        for i, (a_ref, da_ref, w_ref, do_ref) in enumerate(
                [(ana_ref, dana_ref, wna_ref, dona_ref), (asw_ref, dasw_ref, wsw_ref, dosw_ref)]):
            gi = gt_ref[:, i * d:(i + 1) * d].astype(F32)
            da = (dm * gi).astype(BF16)
            da_ref[...] = da
            do_ref[...] = _dot(da, w_ref[...]).astype(BF16)
            dzg = dm * a_ref[...].astype(F32)
            dzg_ref[:, i * d:(i + 1) * d] = dzg.astype(BF16)
            dbg_ref[:, i * d:(i + 1) * d] = dbg_ref[:, i * d:(i + 1) * d] + jnp.sum(dzg, axis=0, keepdims=True)

    return pl.pallas_call(
        body, name=name, grid=(s // tm,),
        in_specs=[_rows(tm, d), _rows(tm, 2 * d), _rows(tm, d), _rows(tm, d),
                  _mat(*wbna_t), _mat(*wbsw_t), _mat(*wout), _full(dep.shape)],
        out_specs=[_rows(tm, d), _rows(tm, 2 * d), _rows(tm, d), _rows(tm, d), _rows(tm, 512), _rows(tm, 512),
                   _full((1, 2 * d))],
        out_shape=[jax.ShapeDtypeStruct((s, d), BF16), jax.ShapeDtypeStruct((s, 2 * d), BF16),
                   jax.ShapeDtypeStruct((s, d), BF16), jax.ShapeDtypeStruct((s, d), BF16),
                   jax.ShapeDtypeStruct((s, 512), BF16), jax.ShapeDtypeStruct((s, 512), BF16),
                   jax.ShapeDtypeStruct((1, 2 * d), F32)],
        compiler_params=_params(),
    )(dx, gt, a_na, a_sw, wbna_t[0], wbsw_t[0], wout[0], dep)


def qk_norm_bwd(dqa, dka, dva, dqs, dks, dvs, zq, dzg, gq_na, gk_na, gq_sw, gk_sw, bd, name):
    s = zq.shape[0]
    d2 = dzg.shape[1]
    n_in = QKV_WIDTH + d2
    tm = _row_tile(s)

    def body(dqa_ref, dka_ref, dva_ref, dqs_ref, dks_ref, dvs_ref, zq_ref, dzg_ref,
             gqa_ref, gka_ref, gqs_ref, gks_ref, bd_ref, dz_ref, dgqa_ref, dgka_ref, dgqs_ref, dgks_ref):
        @pl.when(pl.program_id(0) == 0)
        def _():
            for r in (dgqa_ref, dgka_ref, dgqs_ref, dgks_ref):
                r[...] = jnp.zeros(r.shape, F32)

        bd512 = bd_ref[...]
        bd128 = bd_ref[0:SW_KV_WIDTH, 0:SW_KV_WIDTH]

        def one(c0, c1, dy_ref, g_ref, dg_ref, bdm, scale):
            z = zq_ref[:, c0:c1].astype(F32)
            r = lax.rsqrt(_group_mean(z * z, bdm) + EPS)
            zh = z * r
            dy = dy_ref[...] * scale
            dyg = dy * g_ref[...]
            dz = r * (dyg - zh * _group_mean(dyg * zh, bdm))
            dz_ref[:, c0:c1] = dz.astype(BF16)
            dg_ref[...] = dg_ref[...] + jnp.sum(dy * zh, axis=0, keepdims=True)

        one(0, 512, dqa_ref, gqa_ref, dgqa_ref, bd512, SCALE)
        one(512, 1024, dka_ref, gka_ref, dgka_ref, bd512, 1.0)
        dz_ref[:, 1024:1536] = dva_ref[...].astype(BF16)
        one(1536, 2048, dqs_ref, gqs_ref, dgqs_ref, bd512, SCALE)
        one(2048, 2176, dks_ref, gks_ref, dgks_ref, bd128, 1.0)
        dz_ref[:, 2176:2304] = dvs_ref[...].astype(BF16)
        dz_ref[:, QKV_WIDTH:n_in] = dzg_ref[...]

    return pl.pallas_call(
        body, name=name, grid=(s // tm,),
        in_specs=[_rows(tm, 512), _rows(tm, 512), _rows(tm, 512), _rows(tm, 512), _rows(tm, 128), _rows(tm, 128),
                  _rows(tm, QKV_WIDTH), _rows(tm, d2),
                  _full((1, 512)), _full((1, 512)), _full((1, 512)), _full((1, 128)), _full((MXU_TILE, MXU_TILE))],
        out_specs=[_rows(tm, n_in), _full((1, 512)), _full((1, 512)), _full((1, 512)), _full((1, 128))],
        out_shape=[jax.ShapeDtypeStruct((s, n_in), BF16)] + [jax.ShapeDtypeStruct((1, 512), F32)] * 3
                  + [jax.ShapeDtypeStruct((1, 128), F32)],
        compiler_params=_params(),
    )(dqa, dka, dva, dqs, dks, dvs, zq, dzg, gq_na, gk_na, gq_sw, gk_sw, bd)


def ffn_bwd_act(dx, wd, hg, hu, name):
    s, d = dx.shape
    f = wd[0].shape[1]
    tm = _row_tile(s)
    fc = _col_chunk(f)

    def body(dx_ref, w_ref, hg_ref, hu_ref, dxb_ref, dhg_ref, dhu_ref):
        dxv = dx_ref[...]
        dxb_ref[...] = dxv.astype(BF16)
        half = (0.5 * dxv).astype(BF16)
        for c0 in range(0, f, fc):
            dact = _dotg(half, w_ref[c0:c0 + fc, :], NT)
            dhu_ref[:, c0:c0 + fc] = (dact * hu_ref[:, c0:c0 + fc].astype(F32)).astype(BF16)
            dhg_ref[:, c0:c0 + fc] = (dact * hg_ref[:, c0:c0 + fc].astype(F32)).astype(BF16)

    return pl.pallas_call(
        body, name=name, grid=(s // tm,),
        in_specs=[_rows(tm, d), _mat(*wd), _rows(tm, f), _rows(tm, f)],
        out_specs=[_rows(tm, d), _rows(tm, f), _rows(tm, f)],
        out_shape=[jax.ShapeDtypeStruct((s, d), BF16), jax.ShapeDtypeStruct((s, f), BF16),
                   jax.ShapeDtypeStruct((s, f), BF16)],
        compiler_params=_params(),
    )(dx, wd[0], hg, hu)


def proj_bwd_norm(acts, weights, x, gain, dx, dep, name):
    s, d = x.shape
    tm = min(_row_tile(s), 256)
    n = len(acts)

    def body(*refs):
        a_refs, w_refs = refs[:n], refs[n:2 * n]
        x_ref, g_ref, dx_ref, _, o_ref, dg_ref = refs[2 * n:]

        @pl.when(pl.program_id(0) == 0)
        def _():
            dg_ref[...] = jnp.zeros(dg_ref.shape, F32)

        dxn = _dot(a_refs[0][...], w_refs[0][...])
        for a_ref, w_ref in zip(a_refs[1:], w_refs[1:]):
            dxn = dxn + _dot(a_ref[...], w_ref[...])
        xv = x_ref[...]
        r = _rstd(xv)
        xh = xv * r
        dxh = dxn * g_ref[...]
        o_ref[...] = dx_ref[...] + r * (dxh - xh * jnp.mean(dxh * xh, axis=-1, keepdims=True))
        dg_ref[...] = dg_ref[...] + jnp.sum(dxn * xh, axis=0, keepdims=True)

    return pl.pallas_call(
        body, name=name, grid=(s // tm,),
        in_specs=[_rows(tm, a.shape[1]) for a in acts] + [_mat(*w) for w in weights]
                 + [_rows(tm, d), _full((1, d)), _rows(tm, d), _full(dep.shape)],
        out_specs=[_rows(tm, d), _full((1, d))],
        out_shape=[jax.ShapeDtypeStruct((s, d), F32), jax.ShapeDtypeStruct((1, d), F32)],
        compiler_params=_params(),
    )(*acts, *[w[0] for w in weights], x, gain, dx, dep)


def tn_matmul(products, name):
    s, n = products[0][0].shape
    tn = _tn_tile(n) if len(products) == 1 else _col_chunk(n)
    rhs = []
    for _, b, _ in products:
        if not any(b is seen for seen in rhs):
            rhs.append(b)
    which = [next(i for i, seen in enumerate(rhs) if b is seen) for _, b, _ in products]
    npr, nr = len(products), len(rhs)

    def body(*refs):
        a_refs, b_refs, o_refs = refs[:npr], refs[npr:npr + nr], refs[npr + nr:]
        for i, (_, _, scale) in enumerate(products):
            o_refs[i][...] = (scale * _dotg(a_refs[i][...], b_refs[which[i]][...], TN)).astype(BF16)

    return pl.pallas_call(
        body, name=name, grid=(n // tn,),
        in_specs=[pl.BlockSpec((s, tn), lambda i: (0, i))] * npr
                 + [pl.BlockSpec(b.shape, lambda i: (0, 0), pipeline_mode=ONCE) for b in rhs],
        out_specs=[pl.BlockSpec((tn, b.shape[1]), lambda i: (i, 0)) for _, b, _ in products],
        out_shape=[jax.ShapeDtypeStruct((n, b.shape[1]), BF16) for _, b, _ in products],
        compiler_params=_params(),
    )(*[a for a, _, _ in products], *rhs)


def _mesh_pos():
    return lax.axis_index("x"), lax.axis_index("y"), lax.axis_index("c")


def _peers():
    x, y, c = _mesh_pos()
    peers = []
    for rel in range(1, N_DEV):
        peers.append((1 - x if rel & 4 else x, 1 - y if rel & 2 else y, 1 - c if rel & 1 else c))
    return 4 * x + 2 * y + c, peers


HBM_SPEC = pl.BlockSpec(memory_space=pltpu.HBM)
SEM_SPEC = pl.BlockSpec(memory_space=pltpu.SEMAPHORE)


def _split_call(body, name, thru, n_sems, extra=(), with_token=True):
    hbm = lambda t: pltpu.with_memory_space_constraint(t, pltpu.HBM)
    effect = pltpu.CompilerParams(has_side_effects=pltpu.SideEffectType.DATAFLOW_SIDE_EFFECTING)
    nt = len(thru)
    thru_shapes = [pltpu.HBM(t.shape, t.dtype) for t in thru]
    if with_token:
        (after,) = extra
        outs = pl.pallas_call(
            body, name=name, in_specs=[HBM_SPEC] * nt + [pl.BlockSpec(memory_space=pl.ANY)],
            out_specs=[SEM_SPEC] * len(n_sems) + [HBM_SPEC] * nt + [pl.BlockSpec(memory_space=pltpu.VMEM)],
            out_shape=[pltpu.SemaphoreType.DMA((k,)) for k in n_sems] + thru_shapes
                      + [jax.ShapeDtypeStruct((8, LANES), F32)],
            input_output_aliases={i: len(n_sems) + i for i in range(nt)}, compiler_params=effect,
        )(*[hbm(t) for t in thru], after)
        return outs[:len(n_sems)], outs[len(n_sems):-1], outs[-1]
    return pl.pallas_call(
        body, name=name,
        in_specs=[HBM_SPEC] * nt + [SEM_SPEC] * len(n_sems) + [pl.BlockSpec(memory_space=pl.ANY)],
        out_specs=[HBM_SPEC] * nt, out_shape=thru_shapes,
        input_output_aliases={i: i for i in range(nt)}, compiler_params=effect,
    )(*thru, *extra)


def _gather_targets():
    x, y, c = _mesh_pos()
    return 4 * x + 2 * y + c, [(x, y, 1 - c), (1 - x, y, c), (x, 1 - y, c), (1 - x, 1 - y, c)]


def gather_start(shards, after, name):
    n = len(shards)
    zones = [lax.empty((w.shape[0], N_DEV) + w.shape[1:], w.dtype) for w in shards]

    def body(*refs):
        ins, zs = refs[:n], refs[n:2 * n]
        send_sems, recv_sems, local_sems = refs[2 * n + 1:2 * n + 4]
        token = refs[-1]
        me, targets = _gather_targets()
        for a in range(n):
            pltpu.make_async_copy(ins[a], zs[a].at[:, me], local_sems.at[a]).start()
            for k, to in enumerate(targets):
                pltpu.make_async_remote_copy(
                    src_ref=ins[a], dst_ref=zs[a].at[:, me], send_sem=send_sems.at[4 * a + k],
                    recv_sem=recv_sems.at[4 * a + k], device_id=to, device_id_type=MESH).start()
        token[...] = jnp.zeros(token.shape, F32)

    sems, thru, token = _split_call(body, name, list(shards) + zones, (4 * n, 4 * n, n), extra=(after,))
    return (sems, thru, n), token


def gather_wait(started, after, name):
    sems, thru, n = started

    def body(*refs):
        zs = refs[n:2 * n]
        send_sems, recv_sems, local_sems = refs[2 * n:2 * n + 3]
        _, targets = _gather_targets()
        for a in range(n):
            for k, to in enumerate(targets):
                cp = pltpu.make_async_remote_copy(
                    src_ref=zs[a].at[:, 0], dst_ref=zs[a].at[:, 0], send_sem=send_sems.at[4 * a + k],
                    recv_sem=recv_sems.at[4 * a + k], device_id=to, device_id_type=MESH)
                cp.wait_send()
                cp.wait_recv()
            pltpu.make_async_copy(zs[a].at[:, 0], zs[a].at[:, 0], local_sems.at[a]).wait()

    return _split_call(body, name, thru, (4 * n, 4 * n, n), extra=(*sems, after), with_token=False)[n:]


def forward_start(zones, after, name):
    n = len(zones)

    def body(*refs):
        zs = refs[:n]
        send_sems, recv_sems = refs[n + 1:n + 3]
        token = refs[-1]
        x, y, c = _mesh_pos()
        for a in range(n):
            for j, chip in enumerate([(1 - x, y), (x, 1 - y), (1 - x, 1 - y)]):
                blk = zs[a].at[:, 4 * chip[0] + 2 * chip[1] + c]
                pltpu.make_async_remote_copy(
                    src_ref=blk, dst_ref=blk, send_sem=send_sems.at[3 * a + j], recv_sem=recv_sems.at[3 * a + j],
                    device_id=(x, y, 1 - c), device_id_type=MESH).start()
        token[...] = jnp.zeros(token.shape, F32)

    sems, thru, token = _split_call(body, name, list(zones), (3 * n, 3 * n), extra=(after,))
    return (sems, thru, n), token


def forward_wait(started, after, name):
    sems, thru, n = started

    def body(*refs):
        zs = refs[:n]
        send_sems, recv_sems = refs[n:n + 2]
        x, y, c = _mesh_pos()
        for a in range(n):
            for j in range(3):
                cp = pltpu.make_async_remote_copy(
                    src_ref=zs[a].at[:, 0], dst_ref=zs[a].at[:, 0], send_sem=send_sems.at[3 * a + j],
                    recv_sem=recv_sems.at[3 * a + j], device_id=(x, y, 1 - c), device_id_type=MESH)
                cp.wait_send()
                cp.wait_recv()

    return _split_call(body, name, thru, (3 * n, 3 * n), extra=(*sems, after), with_token=False)


def scatter_start(groups, name):
    n = len(groups)
    flat = [g for grp in groups for g in grp]
    nf = len(flat)
    offs = np.cumsum([0] + [len(grp) for grp in groups])
    lands = [lax.empty((N_DEV, len(grp)) + grp[0].shape[1:], grp[0].dtype) for grp in groups]

    def body(*refs):
        ins, zones = refs[:nf], refs[nf:nf + n]
        send_sems, recv_sems, local_sems = refs[nf + n:nf + n + 3]
        token = refs[-1]
        me, peers = _peers()
        for a in range(n):
            for w in range(len(groups[a])):
                pltpu.make_async_copy(ins[offs[a] + w].at[me], zones[a].at[me, w], local_sems.at[a]).start()
        for k, peer in enumerate(peers):
            p_id = 4 * peer[0] + 2 * peer[1] + peer[2]
            for a in range(n):
                for w in range(len(groups[a])):
                    pltpu.make_async_remote_copy(
                        src_ref=ins[offs[a] + w].at[p_id], dst_ref=zones[a].at[me, w],
                        send_sem=send_sems.at[7 * a + k], recv_sem=recv_sems.at[7 * a + k],
                        device_id=peer, device_id_type=MESH).start()
        token[...] = jnp.zeros(token.shape, F32)

    hbm = lambda t: pltpu.with_memory_space_constraint(t, pltpu.HBM)
    outs = pl.pallas_call(
        body, name=name,
        in_specs=[HBM_SPEC] * (nf + n),
        out_specs=[SEM_SPEC] * 3 + [HBM_SPEC] * (nf + n) + [pl.BlockSpec(memory_space=pltpu.VMEM)],
        out_shape=[pltpu.SemaphoreType.DMA((7 * n,)), pltpu.SemaphoreType.DMA((7 * n,)), pltpu.SemaphoreType.DMA((n,))]
                  + [pltpu.HBM(t.shape, t.dtype) for t in flat + lands]
                  + [jax.ShapeDtypeStruct((8, LANES), F32)],
        input_output_aliases={i: 3 + i for i in range(nf + n)},
        compiler_params=pltpu.CompilerParams(has_side_effects=pltpu.SideEffectType.DATAFLOW_SIDE_EFFECTING),
    )(*[hbm(t) for t in flat], *[hbm(t) for t in lands])
    sems, thru, token = outs[:3], outs[3:3 + nf + n], outs[-1]
    return (sems, thru, [len(grp) for grp in groups]), token


def scatter_wait(started, after, name):
    (send_sems, recv_sems, local_sems), thru, sizes = started
    n = len(sizes)
    nf = len(thru) - n

    def body(*refs):
        zones = refs[nf:nf + n]
        s_sems, r_sems, l_sems = refs[nf + n:nf + n + 3]
        me, peers = _peers()
        for a in range(n):
            for k, peer in enumerate(peers):
                cp = pltpu.make_async_remote_copy(
                    src_ref=zones[a].at[0], dst_ref=zones[a].at[0],
                    send_sem=s_sems.at[7 * a + k], recv_sem=r_sems.at[7 * a + k], device_id=peer,
                    device_id_type=MESH)
                cp.wait_send()
                cp.wait_recv()
            pltpu.make_async_copy(zones[a].at[0], zones[a].at[0], l_sems.at[a]).wait()

    outs = pl.pallas_call(
        body, name=name,
        in_specs=[HBM_SPEC] * (nf + n) + [SEM_SPEC] * 3 + [pl.BlockSpec(memory_space=pl.ANY)],
        out_specs=[HBM_SPEC] * (nf + n),
        out_shape=[pltpu.HBM(t.shape, t.dtype) for t in thru],
        input_output_aliases={i: i for i in range(nf + n)},
        compiler_params=pltpu.CompilerParams(has_side_effects=pltpu.SideEffectType.DATAFLOW_SIDE_EFFECTING),
    )(*thru, send_sems, recv_sems, local_sems, after)
    return outs[nf:]


def pair_start(grads, after, name):
    nw = len(grads)
    land = lax.empty((4, nw) + grads[0].shape[1:], grads[0].dtype)

    def body(*refs):
        ins, zone = refs[:nw], refs[nw]
        send_sems, recv_sems = refs[nw + 2:nw + 4]
        x, y, c = _mesh_pos()
        for j in range(4):
            for w in range(nw):
                pltpu.make_async_remote_copy(
                    src_ref=ins[w].at[2 * j + (1 - c)], dst_ref=zone.at[j, w], send_sem=send_sems.at[0],
                    recv_sem=recv_sems.at[0], device_id=(x, y, 1 - c), device_id_type=MESH).start()
        refs[-1][...] = jnp.zeros(refs[-1].shape, F32)

    sems, thru, token = _split_call(body, name, list(grads) + [land], (1, 1), extra=(after,))
    return (sems, thru, nw), token


def pair_wait(started, after, name):
    sems, thru, nw = started

    def body(*refs):
        zone = refs[nw]
        send_sems, recv_sems = refs[nw + 1:nw + 3]
        x, y, c = _mesh_pos()
        cp = pltpu.make_async_remote_copy(src_ref=zone, dst_ref=zone, send_sem=send_sems.at[0],
                                          recv_sem=recv_sems.at[0], device_id=(x, y, 1 - c), device_id_type=MESH)
        cp.wait_send()
        cp.wait_recv()

    outs = _split_call(body, name, thru, (1, 1), extra=(*sems, after), with_token=False)
    return outs[:nw], outs[nw]


def pair_sum(grads, land, name):
    nw = len(grads)
    _, r, c_dim = grads[0].shape

    def body(*refs):
        g_refs, l_ref, o_ref = refs[:nw], refs[nw], refs[nw + 1]
        core = lax.axis_index("c")
        for w in range(nw):
            o_ref[0, w] = (g_refs[w][0, core].astype(F32) + l_ref[0, w].astype(F32)).astype(BF16)

    return pl.pallas_call(
        body, name=name, grid=(4,),
        in_specs=[pl.BlockSpec((1, 2, r, c_dim), lambda j: (j, 0, 0, 0))] * nw
                 + [pl.BlockSpec((1, nw, r, c_dim), lambda j: (j, 0, 0, 0))],
        out_specs=pl.BlockSpec((1, nw, r, c_dim), lambda j: (j, 0, 0, 0)),
        out_shape=jax.ShapeDtypeStruct((4, nw, r, c_dim), BF16),
        compiler_params=_params(),
    )(*[g.reshape(4, 2, r, c_dim) for g in grads], land)


def _other_chips():
    x, y, c = _mesh_pos()
    chips = []
    for rel in range(1, 4):
        px, py = (1 - x if rel & 2 else x), (1 - y if rel & 1 else y)
        chips.append((px, py, 2 * px + py))
    return 2 * x + y, c, chips


def chip_start(pair_sums, after, name):
    land = lax.empty(pair_sums.shape, pair_sums.dtype)

    def body(*refs):
        h_ref, zone = refs[0], refs[1]
        send_sems, recv_sems, local_sem = refs[3:6]
        mine, c, chips = _other_chips()
        pltpu.make_async_copy(h_ref.at[mine], zone.at[mine], local_sem.at[0]).start()
        for k, (px, py, j) in enumerate(chips):
            pltpu.make_async_remote_copy(
                src_ref=h_ref.at[j], dst_ref=zone.at[mine], send_sem=send_sems.at[k], recv_sem=recv_sems.at[k],
                device_id=(px, py, c), device_id_type=MESH).start()
        refs[-1][...] = jnp.zeros(refs[-1].shape, F32)

    sems, thru, token = _split_call(body, name, [pair_sums, land], (3, 3, 1), extra=(after,))
    return (sems, thru), token


def chip_wait(started, after, name):
    sems, thru = started

    def body(*refs):
        zone = refs[1]
        send_sems, recv_sems, local_sem = refs[2:5]
        _, c, chips = _other_chips()
        for k, (px, py, _) in enumerate(chips):
            cp = pltpu.make_async_remote_copy(
                src_ref=zone.at[0], dst_ref=zone.at[0], send_sem=send_sems.at[k], recv_sem=recv_sems.at[k],
                device_id=(px, py, c), device_id_type=MESH)
            cp.wait_send()
            cp.wait_recv()
        pltpu.make_async_copy(zone.at[0], zone.at[0], local_sem.at[0]).wait()

    return _split_call(body, name, thru, (3, 3, 1), extra=(*sems, after), with_token=False)[1]


def share_start(parts, after, name):
    n = len(parts)
    zones = [lax.empty((N_DEV,) + p.shape, p.dtype) for p in parts]

    def body(*refs):
        ins, zs = refs[:n], refs[n:2 * n]
        send_sems, recv_sems, local_sems = refs[2 * n + 1:2 * n + 4]
        me, peers = _peers()
        for i in range(n):
            pltpu.make_async_copy(ins[i], zs[i].at[me], local_sems.at[i]).start()
            for k, peer in enumerate(peers):
                pltpu.make_async_remote_copy(
                    src_ref=ins[i], dst_ref=zs[i].at[me], send_sem=send_sems.at[7 * i + k],
                    recv_sem=recv_sems.at[7 * i + k], device_id=peer, device_id_type=MESH).start()
        refs[-1][...] = jnp.zeros(refs[-1].shape, F32)

    sems, thru, token = _split_call(body, name, list(parts) + zones, (7 * n, 7 * n, n), extra=(after,))
    return (sems, thru, n), token


def share_wait(started, after, name):
    sems, thru, n = started

    def body(*refs):
        zs = refs[n:2 * n]
        send_sems, recv_sems, local_sems = refs[2 * n:2 * n + 3]
        _, peers = _peers()
        for i in range(n):
            for k, peer in enumerate(peers):
                cp = pltpu.make_async_remote_copy(
                    src_ref=zs[i].at[0], dst_ref=zs[i].at[0], send_sem=send_sems.at[7 * i + k],
                    recv_sem=recv_sems.at[7 * i + k], device_id=peer, device_id_type=MESH)
                cp.wait_send()
                cp.wait_recv()
            pltpu.make_async_copy(zs[i].at[0], zs[i].at[0], local_sems.at[i]).wait()

    return _split_call(body, name, thru, (7 * n, 7 * n, n), extra=(*sems, after), with_token=False)[n:]


def _adamw_math(w, g, m, v):
    m = ADAM_B1 * m + (1.0 - ADAM_B1) * g
    v = ADAM_B2 * v + (1.0 - ADAM_B2) * (g * g)
    m_hat = m / (1.0 - ADAM_B1 ** ADAM_STEP)
    v_hat = v / (1.0 - ADAM_B2 ** ADAM_STEP)
    delta = -ADAM_LR * (m_hat / (jnp.sqrt(v_hat) + ADAM_EPS) + ADAM_WD * w)
    return delta, m, v


ADAMW_BLOCK_BYTES = 24 * 1024 * 1024


def adamw_layer(zone, layer, items, after, name):
    n_src, nw, r, c = zone.shape
    depth = items[0][0].shape[0]
    prevs = [p if p is not None else tuple(lax.empty((depth, r, c), F32) for _ in range(4)) for _, _, _, p in items]
    row_bytes = 2 * nw * c * (2 * n_src + 4 * 7)
    tr = max(t for t in range(8, r + 1, 8) if r % t == 0 and t * row_bytes <= ADAMW_BLOCK_BYTES)

    def body(z_ref, *rest):
        ins, outs = rest[:3 * nw], rest[7 * nw + 1:]
        for i in range(nw):
            g = z_ref[0, i].astype(F32)
            for src in range(1, n_src):
                g = g + z_ref[src, i].astype(F32)
            g_ref, d_ref, mo_ref, vo_ref = outs[4 * i:4 * i + 4]
            w_ref, m_ref, v_ref = ins[3 * i:3 * i + 3]
            g_ref[...] = g
            d_ref[...], mo_ref[...], vo_ref[...] = _adamw_math(w_ref[...], g, m_ref[...], v_ref[...])

    rows = pl.BlockSpec((None, tr, c), lambda i: (layer, i, 0))
    anywhere = pl.BlockSpec(memory_space=pl.ANY)
    outs = pl.pallas_call(
        body, name=name, grid=(r // tr,),
        in_specs=[pl.BlockSpec((n_src, nw, tr, c), lambda i: (0, 0, i, 0))] + [rows] * (3 * nw)
                 + [anywhere] * (4 * nw + 1),
        out_specs=[rows] * (4 * nw),
        out_shape=[jax.ShapeDtypeStruct((depth, r, c), F32)] * (4 * nw),
        input_output_aliases={1 + 3 * nw + k: k for k in range(4 * nw)},
        compiler_params=_params(),
    )(zone, *[t for w, m, v, _ in items for t in (w, m, v)], *[t for p in prevs for t in p], after)
    return [tuple(outs[4 * i:4 * i + 4]) for i in range(nw)]


def adamw_small(ws, recvs, ms, vs, name):
    n = len(ws)

    def body(*refs):
        w_refs, r_refs, m_refs, v_refs = (refs[i * n:(i + 1) * n] for i in range(4))
        g_refs, d_refs, mo_refs, vo_refs = (refs[(4 + i) * n:(5 + i) * n] for i in range(4))
        for i in range(n):
            g = r_refs[i][0]
            for src in range(1, N_DEV):
                g = g + r_refs[i][src]
            g_refs[i][...] = g
            d_refs[i][...], mo_refs[i][...], vo_refs[i][...] = _adamw_math(w_refs[i][...], g, m_refs[i][...],
                                                                            v_refs[i][...])

    vm = pl.BlockSpec(memory_space=pltpu.VMEM)
    outs = pl.pallas_call(
        body, name=name, in_specs=[vm] * (4 * n), out_specs=[vm] * (4 * n),
        out_shape=[jax.ShapeDtypeStruct(w.shape, F32) for w in ws] * 4,
        compiler_params=pltpu.CompilerParams(vmem_limit_bytes=V7X_VMEM_LIMIT),
    )(*ws, *recvs, *ms, *vs)
    return [outs[i * n:(i + 1) * n] for i in range(4)]


SMALL_NAMES = ("ffn1_norm", "mix_norm", "ffn2_norm", "b_gate", "na_q_norm", "na_k_norm", "sw_q_norm", "sw_k_norm",
               "na_rpb", "sw_sink", "t5_rel_table")


def kernel(x, ffn1_norm, ffn1_w_gate, ffn1_w_up, ffn1_w_down, mix_norm, w_in, b_gate, na_q_norm, na_k_norm, na_rpb, sw_q_norm, sw_k_norm, sw_sink, t5_rel_table, w_branch_na, w_branch_sw, w_out, ffn2_norm, ffn2_w_gate, ffn2_w_up, ffn2_w_down, loss_target, m_ffn1_norm, m_ffn1_w_gate, m_ffn1_w_up, m_ffn1_w_down, m_mix_norm, m_w_in, m_b_gate, m_na_q_norm, m_na_k_norm, m_na_rpb, m_sw_q_norm, m_sw_k_norm, m_sw_sink, m_t5_rel_table, m_w_branch_na, m_w_branch_sw, m_w_out, m_ffn2_norm, m_ffn2_w_gate, m_ffn2_w_up, m_ffn2_w_down, v_ffn1_norm, v_ffn1_w_gate, v_ffn1_w_up, v_ffn1_w_down, v_mix_norm, v_w_in, v_b_gate, v_na_q_norm, v_na_k_norm, v_na_rpb, v_sw_q_norm, v_sw_k_norm, v_sw_sink, v_t5_rel_table, v_w_branch_na, v_w_branch_sw, v_w_out, v_ffn2_norm, v_ffn2_w_gate, v_ffn2_w_up, v_ffn2_w_down):
    weights = dict(ffn1_norm=ffn1_norm, ffn1_w_gate=ffn1_w_gate, ffn1_w_up=ffn1_w_up, ffn1_w_down=ffn1_w_down,
                   mix_norm=mix_norm, w_in=w_in, b_gate=b_gate, na_q_norm=na_q_norm, na_k_norm=na_k_norm,
                   na_rpb=na_rpb, sw_q_norm=sw_q_norm, sw_k_norm=sw_k_norm, sw_sink=sw_sink,
                   t5_rel_table=t5_rel_table, w_branch_na=w_branch_na, w_branch_sw=w_branch_sw, w_out=w_out,
                   ffn2_norm=ffn2_norm, ffn2_w_gate=ffn2_w_gate, ffn2_w_up=ffn2_w_up, ffn2_w_down=ffn2_w_down)
    mom_m = dict(ffn1_norm=m_ffn1_norm, ffn1_w_gate=m_ffn1_w_gate, ffn1_w_up=m_ffn1_w_up, ffn1_w_down=m_ffn1_w_down,
                 mix_norm=m_mix_norm, w_in=m_w_in, b_gate=m_b_gate, na_q_norm=m_na_q_norm, na_k_norm=m_na_k_norm,
                 na_rpb=m_na_rpb, sw_q_norm=m_sw_q_norm, sw_k_norm=m_sw_k_norm, sw_sink=m_sw_sink,
                 t5_rel_table=m_t5_rel_table, w_branch_na=m_w_branch_na, w_branch_sw=m_w_branch_sw, w_out=m_w_out,
                 ffn2_norm=m_ffn2_norm, ffn2_w_gate=m_ffn2_w_gate, ffn2_w_up=m_ffn2_w_up, ffn2_w_down=m_ffn2_w_down)
    mom_v = dict(ffn1_norm=v_ffn1_norm, ffn1_w_gate=v_ffn1_w_gate, ffn1_w_up=v_ffn1_w_up, ffn1_w_down=v_ffn1_w_down,
                 mix_norm=v_mix_norm, w_in=v_w_in, b_gate=v_b_gate, na_q_norm=v_na_q_norm, na_k_norm=v_na_k_norm,
                 na_rpb=v_na_rpb, sw_q_norm=v_sw_q_norm, sw_k_norm=v_sw_k_norm, sw_sink=v_sw_sink,
                 t5_rel_table=v_t5_rel_table, w_branch_na=v_w_branch_na, w_branch_sw=v_w_branch_sw, w_out=v_w_out,
                 ffn2_norm=v_ffn2_norm, ffn2_w_gate=v_ffn2_w_gate, ffn2_w_up=v_ffn2_w_up, ffn2_w_down=v_ffn2_w_down)
    order = list(weights)

    depth = ffn1_norm.shape[0]
    s, d = x.shape[1], x.shape[2]
    xs = x[0]
    tr = lambda w: jnp.swapaxes(w, -1, -2)

    merge = lambda t: t.reshape(t.shape[0], N_DEV * t.shape[2], t.shape[3])
    no_dep = jnp.zeros((8, LANES), F32)

    def shards_of(kind, l):
        stack = lambda *ws: jnp.stack(ws).astype(BF16)
        if kind == "ffn1":
            return [stack(tr(ffn1_w_gate[l]), tr(ffn1_w_up[l]), ffn1_w_down[l])]
        if kind == "win":
            return [stack(tr(w_in[l]))]
        return [stack(tr(ffn2_w_gate[l]), tr(ffn2_w_up[l]), ffn2_w_down[l]), stack(w_out[l]),
                stack(tr(w_branch_na[l]), tr(w_branch_sw[l]))]

    shards = {(kind, l): shards_of(kind, l) for l in range(depth) for kind in ("ffn1", "win", "rest")}

    def start(kind, l, after):
        return gather_start(shards[kind, l], after, f"gather_{kind}_{l}")

    def arrive(started, kind, l, after):
        zones = gather_wait(started, after, f"gather_{kind}_{l}_wait")
        return forward_start(zones, no_dep, f"forward_{kind}_{l}")

    def finish(fwd, kind, l, after):
        return [merge(z) for z in forward_wait(fwd, after, f"forward_{kind}_{l}_wait")]

    bd = jnp.asarray(np.kron(np.eye(MXU_TILE // HEAD_DIM), np.full((HEAD_DIM, HEAD_DIM), 1.0 / HEAD_DIM)), BF16)
    bmap = jnp.asarray(_t5_bucket_map())
    tile8 = lambda g: jnp.tile(g, NA_WIDTH // HEAD_DIM).reshape(1, NA_WIDTH)
    tile2 = lambda g: jnp.tile(g, SW_KV_WIDTH // HEAD_DIM).reshape(1, SW_KV_WIDTH)

    st_first, tok = start("ffn1", 0, no_dep)
    t5b = t5_expand(t5_rel_table, bmap, tok, "t5_expand").reshape(SW_EDGES, SW_STACK, 3 * SW_BLOCK)
    t2_tables = [rpb_expand(_rpb_rows(na_rpb[l]), tok, f"rpb_expand_{l}") for l in range(depth)]
    masks = na_masks(s // GRID_W, tok, "na_masks")
    qk_gains = [(tile8(na_q_norm[l]), tile8(na_k_norm[l]), tile8(sw_q_norm[l]), tile2(sw_k_norm[l]))
                for l in range(depth)]
    early = ([t[0, 0, 0:8, :] for t in t2_tables] + [masks[0, 0:8, 0:LANES]] + [t[0, 0:8, 0:LANES].astype(F32) for v in shards.values() for t in v]
             + [g[:, 0:LANES] for gs in qk_gains for g in gs])
    fwd, _ = arrive(st_first, "ffn1", 0, functools.reduce(jnp.add, early, t5b[0, 0:8, 0:LANES]))
    st_win, dep = start("win", 0, t5b)
    (first,) = finish(fwd, "ffn1", 0, dep)

    saved = []
    layer_w = {0: dict(wg1=(first, 0), wu1=(first, 1), wd1=(first, 2))}
    cur = xs
    for l in range(depth):
        sv = {}
        lw = layer_w[l]
        sv["x0"] = cur
        cur, sv["xn1"], sv["hg1"], sv["hu1"], sv["act1"] = ffn_forward(
            cur, ffn1_norm[l][None], lw["wg1"], lw["wu1"], lw["wd1"], dep, f"ffn1_{l}")
        sv["x1"] = cur
        fwd, _ = arrive(st_win, "win", l, cur)
        st_rest, tok = start("rest", l, cur)
        (zb,) = finish(fwd, "win", l, tok)
        lw["win"] = (zb, 0)
        sv["gains"] = qk_gains[l]
        sv["hn"], sv["zq"], sv["qa"], sv["ka"], sv["qs"], sv["ks"], sv["gt"] = mix_in(
            cur, mix_norm[l][None], lw["win"], b_gate[l][None], *sv["gains"], bd, f"mix_in_{l}")
        sv["t2"] = t2_tables[l]
        sv["o_na"] = na_fwd(sv["qa"], sv["ka"], sv["zq"], sv["t2"], masks, f"na_fwd_{l}")
        dep = no_dep
        if l + 1 < depth:
            st_ffn1, dep = start("ffn1", l + 1, sv["o_na"])
        sv["o_sw"] = sw_fwd(sv["qs"], sv["ks"], sv["zq"], t5b, sw_sink[l], dep, f"sw_fwd_{l}")
        fwd, tok = arrive(st_rest, "rest", l, sv["o_sw"][0:8, 0:LANES] + sv["o_na"][0:8, 0:LANES])
        za, zc, zd = finish(fwd, "rest", l, tok)
        lw.update(wg2=(za, 0), wu2=(za, 1), wd2=(za, 2), wout=(zc, 0), wna=(zd, 0), wsw=(zd, 1))
        cur, sv["a_na"], sv["a_sw"], sv["merged"] = merge_out(
            cur, sv["o_na"], sv["o_sw"], sv["gt"], lw["wna"], lw["wsw"], lw["wout"], f"merge_out_{l}")
        sv["x2"] = cur
        if l + 1 < depth:
            st_win, dep = start("win", l + 1, cur)
            sv["xn2"], sv["hg2"], sv["hu2"], sv["act2"] = ffn_forward(
                cur, ffn2_norm[l][None], lw["wg2"], lw["wu2"], None, dep, f"ffn2_up_{l}")
            fwd, dep = arrive(st_ffn1, "ffn1", l + 1, sv["act2"])
            cur = ffn_down(cur, sv["act2"], lw["wd2"], dep, f"ffn2_down_{l}")
            (za,) = finish(fwd, "ffn1", l + 1, cur)
            layer_w[l + 1] = dict(wg1=(za, 0), wu1=(za, 1), wd1=(za, 2))
        else:
            dx, loss_acc, sv["xn2"], sv["hg2"], sv["hu2"], sv["act2"] = ffn_forward(
                cur, ffn2_norm[l][None], lw["wg2"], lw["wu2"], lw["wd2"], no_dep, f"ffn2_{l}",
                target=loss_target[0])
        dep = no_dep
        saved.append(sv)

    split = lambda t: t.reshape(N_DEV, t.shape[0] // N_DEV, t.shape[1])
    pending = {}
    last_key = "ffn1_0"
    two_level = {last_key}
    small = {k: [None] * depth for k in SMALL_NAMES if k != "t5_rel_table"}
    dbias_sw = []
    for l in reversed(range(depth)):
        sv = saved[l]
        lw = layer_w[l]
        wg1, wu1, wd1, wg2, wu2, wd2 = (lw[k] for k in ("wg1", "wu1", "wd1", "wg2", "wu2", "wd2"))
        win_t, wout_l, wna_t, wsw_t = lw["win"], lw["wout"], lw["wna"], lw["wsw"]
        blocks = ((2, "x2", "xn2", "hg2", "hu2", "act2", wg2, wu2, wd2, "ffn2_norm", 3),
                  (1, "x0", "xn1", "hg1", "hu1", "act1", wg1, wu1, wd1, "ffn1_norm", 0))

        def ffn_backward(dx, blk):
            tag, xk, xnk, hgk, huk, actk, wg, wu, wd, norm_name, slot = blk
            gains = weights[norm_name]
            dxb, dhg, dhu = ffn_bwd_act(dx, wd, sv[hgk], sv[huk], f"ffn{tag}_bwd_act_{l}")
            gwg, gwu, gwd = tn_matmul([(dhg, sv[xnk], 1.0), (dhu, sv[xnk], 1.0), (sv[actk], dxb, 0.5)],
                                      f"ffn{tag}_dw_{l}")
            key = f"ffn{tag}_{l}"
            blocks_of = [split(gwg), split(gwu), split(gwd)]
            if key in two_level:
                paired, token = pair_start(blocks_of, dxb, f"pair_{key}")
            else:
                pending[key], token = scatter_start([blocks_of], f"scatter_{key}")
            dx, dg = proj_bwd_norm([dhg, dhu], [wg, wu], sv[xk], gains[l][None], dx, token, f"ffn{tag}_bwd_x_{l}")
            token = no_dep
            if key in two_level:
                thru, land = pair_wait(paired, dx, f"pair_{key}_wait")
                pending[key], token = chip_start(pair_sum(thru, land, f"pair_sum_{key}"), dg, f"chips_{key}")
            small[norm_name][l] = dg[0]
            return dx, token

        dx, token = ffn_backward(dx, blocks[0])
        dxb, dzg, da_na, da_sw, do_na, do_sw, dbg = mix_bwd_out(
            dx, sv["gt"], sv["a_na"], sv["a_sw"], wna_t, wsw_t, wout_l, token, f"mix_bwd_out_{l}")
        small["b_gate"][l] = dbg[0]
        gwout, gwna, gwsw = tn_matmul([(sv["merged"], dxb, 1.0), (da_na, sv["o_na"], 1.0), (da_sw, sv["o_sw"], 1.0)],
                                      f"mix_dw_{l}")
        dqa, dka, dva, dt2 = na_bwd(sv["qa"], sv["ka"], sv["zq"], sv["t2"], masks, sv["o_na"], do_na, f"na_bwd_{l}")
        dqs, dks, dvs, dbias, dsink = sw_bwd(sv["qs"], sv["ks"], sv["zq"], t5b, sw_sink[l], sv["o_sw"], do_sw,
                                             f"sw_bwd_{l}")
        dbias_sw.append(dbias.reshape(SW_HEADS, SW_BLOCK, 3 * SW_BLOCK))
        small["sw_sink"][l] = jnp.sum(dsink[:, 0].reshape(SW_HEADS, SW_BLOCK), axis=1)
        small["na_rpb"][l] = _rpb_from_rows(rpb_reduce(dt2, f"rpb_reduce_{l}"))
        dz, dgqa, dgka, dgqs, dgks = qk_norm_bwd(dqa, dka, dva, dqs, dks, dvs, sv["zq"], dzg, *sv["gains"], bd,
                                                 f"qk_norm_bwd_{l}")
        fold = lambda g: jnp.sum(g.reshape(-1, HEAD_DIM), axis=0)
        small["na_q_norm"][l], small["na_k_norm"][l] = fold(dgqa), fold(dgka)
        small["sw_q_norm"][l], small["sw_k_norm"][l] = fold(dgqs), fold(dgks)
        (gwin,) = tn_matmul([(dz, sv["hn"], 1.0)], f"dwin_{l}")
        pending[f"mix_{l}"], token = scatter_start([[split(gwout)], [split(gwna), split(gwsw)], [split(gwin)]],
                                                   f"scatter_mix_{l}")
        dx, dg = proj_bwd_norm([dz], [win_t], sv["x1"], mix_norm[l][None], dx, token, f"mix_bwd_x_{l}")
        small["mix_norm"][l] = dg[0]
        dx, tail = ffn_backward(dx, blocks[1])

    dtab = t5_reduce(dbias_sw, bmap, "t5_reduce")
    small_parts = {k: jnp.stack(v) for k, v in small.items()}
    small_parts["t5_rel_table"] = jnp.transpose(dtab[:, :, 0])

    grads, delta, new_m, new_v = {}, {}, {}, {}
    state = {}
    sharing, token = share_start([small_parts[k] for k in SMALL_NAMES] + [loss_acc], tail, "share_small")
    chain = [token]
    members = {"ffn": lambda t: [(f"ffn{t}_w_gate", 0, 0, True), (f"ffn{t}_w_up", 0, 1, True),
                                 (f"ffn{t}_w_down", 0, 2, False)],
               "mix": lambda t: [("w_out", 0, 0, False), ("w_branch_na", 1, 0, True), ("w_branch_sw", 1, 1, True),
                                 ("w_in", 2, 0, True)]}

    def collect(key):
        if key in two_level:
            zones = [chip_wait(pending[key], chain[0], f"wait_{key}")]
        else:
            zones = scatter_wait(pending[key], chain[0], f"wait_{key}")
        kind, l = key.split("_")
        group = members[kind[:3]](kind[3:])
        complete = all(f"{kind}_{j}" in done for j in range(depth) if j != int(l))
        for zi, zone in enumerate(zones):
            mine = sorted((wi, k, transposed) for k, z, wi, transposed in group if z == zi)
            views = [tr if transposed else (lambda t: t) for _, _, transposed in mine]
            items = [(view(weights[k]), view(mom_m[k]), view(mom_v[k]), state.get(k))
                     for (_, k, _), view in zip(mine, views)]
            results = adamw_layer(zone, int(l), items, chain[0], f"adamw_{key}_{zi}")
            chain[0] = results[-1][1]
            for (_, k, _), view, res in zip(mine, views, results):
                state[k] = res
                if complete:
                    grads[k], delta[k], new_m[k], new_v[k] = (view(t) for t in res)
        done.add(key)

    done = set()
    for key in pending:
        if key != last_key:
            collect(key)
    collect(last_key)
    *recvs, all_losses = share_wait(sharing, chain[0], "share_small_wait")
    loss = jnp.sum(all_losses) * (0.5 / d)
    results = adamw_small([weights[k] for k in SMALL_NAMES], recvs, [mom_m[k] for k in SMALL_NAMES],
                          [mom_v[k] for k in SMALL_NAMES], "adamw_small")
    for dst, outs in zip((grads, delta, new_m, new_v), results):
        dst.update(dict(zip(SMALL_NAMES, outs)))

    return (loss, dx[None], *[grads[k] for k in order], *[delta[k] for k in order],
            *[new_m[k] for k in order], *[new_v[k] for k in order])
```

```python
import functools
import math

import numpy as np
import jax
import jax.numpy as jnp
from jax import lax
from jax.experimental import pallas as pl
from jax.experimental.pallas import tpu as pltpu

F32 = jnp.float32
BF16 = jnp.bfloat16
MESH = pl.DeviceIdType.MESH

N_DEV = 8
EPS = 1e-6
NEG = -1e30
HEAD_DIM = 64
GRID_W = 64
NA_ROWS = 8
NA_COLS = 16
NA_WIDTH = 512
SW_Q_WIDTH = 512
SW_KV_WIDTH = 128
SW_BLOCK = 128
SW_HEADS = 8
SW_REP = 4
REL_BUCKETS = 32
REL_MAX_DIST = 128
QKV_WIDTH = 3 * NA_WIDTH + SW_Q_WIDTH + 2 * SW_KV_WIDTH
SCALE = 1.0 / math.sqrt(HEAD_DIM)

ADAM_LR = 0.001
ADAM_B1 = 0.9
ADAM_B2 = 0.999
ADAM_EPS = 1e-08
ADAM_WD = 0.01
ADAM_STEP = 10

V7X_VMEM_LIMIT = 56 * 1024 * 1024
LANES = 128
MXU_TILE = 256

NT = (((1,), (1,)), ((), ()))
TN = (((0,), (0,)), ((), ()))


def _params(n_grid=1):
    return pltpu.CompilerParams(dimension_semantics=("arbitrary",) * n_grid,
                                vmem_limit_bytes=V7X_VMEM_LIMIT)


def _row_tile(s):
    for t in (512, 256, 128, 64, 32, 16, 8):
        if s % t == 0:
            return t
    raise ValueError(s)


def _tn_tile(n):
    best = max(t for t in range(LANES, min(n, 2304) + 1, LANES) if n % t == 0) if n % LANES == 0 else n
    return best // 2 if best == n and n >= 1024 else best


ONCE = pl.Buffered(1)


def _col_chunk(n):
    return MXU_TILE if n % MXU_TILE == 0 else n


def _dot(a, b):
    return jnp.dot(a, b, preferred_element_type=F32)


def _dotg(a, b, dn):
    return lax.dot_general(a, b, dn, preferred_element_type=F32)


def _sigmoid(v):
    return 1.0 / (1.0 + jnp.exp(-v))


def _rstd(xv):
    return lax.rsqrt(jnp.mean(xv * xv, axis=-1, keepdims=True) + EPS)


def _full(shape):
    nd = len(shape)
    return pl.BlockSpec(shape, lambda i, _n=nd: (0,) * _n)


def _rows(tm, width):
    return pl.BlockSpec((tm, width), lambda i: (i, 0))


def _mat(stack, idx):
    return pl.BlockSpec((None,) + tuple(stack.shape[1:]), lambda i, _w=idx: (_w, 0, 0), pipeline_mode=ONCE)


def _group_mean(v, bd):
    w = bd.shape[0]
    if v.shape[1] > w:
        return jnp.concatenate([_group_mean(v[:, c0:c0 + w], bd) for c0 in range(0, v.shape[1], w)], axis=1)
    hi = v.astype(BF16)
    lo = (v - hi.astype(F32)).astype(BF16)
    return _dot(hi, bd) + _dot(lo, bd)


def _loss_tile(y, t_ref, dy_ref, acc_ref):
    tm, d = y.shape

    @pl.when(pl.program_id(0) == 0)
    def _():
        acc_ref[...] = jnp.zeros(acc_ref.shape, F32)

    err = y - t_ref[...]
    dy_ref[...] = err * (1.0 / d)
    part = jnp.sum((err * err).reshape(tm // 8, 8, d), axis=0)
    acc = part[:, 0:LANES]
    for c0 in range(LANES, d, LANES):
        acc = acc + part[:, c0:c0 + LANES]
    acc_ref[...] = acc_ref[...] + acc


def ffn_forward(x, gain, wg_t, wu_t, wd, dep, name, target=None):
    s, d = x.shape
    f = wg_t[0].shape[1]
    tm = _row_tile(s) if wd is None else min(_row_tile(s), 256)
    fc = _col_chunk(f)
    nw = 2 if wd is None else 3
    n_in = nw + (1 if target is None else 2)

    def body(x_ref, g_ref, *refs):
        w_refs, outs = refs[:nw], refs[n_in:]
        xn_ref, dg_ref, du_ref, act_ref = outs[-4:]
        xv = x_ref[...]
        xn = (xv * _rstd(xv) * g_ref[...]).astype(BF16)
        xn_ref[...] = xn
        for c0 in range(0, f, fc):
            hg = _dotg(xn, w_refs[0][c0:c0 + fc, :], NT)
            hu = _dotg(xn, w_refs[1][c0:c0 + fc, :], NT)
            sg = _sigmoid(hg)
            silu = hg * sg
            du_ref[:, c0:c0 + fc] = silu.astype(BF16)
            dg_ref[:, c0:c0 + fc] = (hu * (sg + silu * (1.0 - sg))).astype(BF16)
            act_ref[:, c0:c0 + fc] = (silu * hu).astype(BF16)
        if wd is not None:
            y = xv + 0.5 * _dot(act_ref[...], w_refs[2][...])
            if target is None:
                outs[0][...] = y
            else:
                _loss_tile(y, refs[nw + 1], outs[0], outs[1])

    weights = [wg_t, wu_t] + ([] if wd is None else [wd])
    in_specs = [_rows(tm, d), _full((1, d))] + [_mat(*w) for w in weights] + [_full(dep.shape)]
    operands = [x, gain, *[w[0] for w in weights], dep]
    out_specs = [_rows(tm, d), _rows(tm, f), _rows(tm, f), _rows(tm, f)]
    out_shape = [jax.ShapeDtypeStruct((s, d), BF16)] + [jax.ShapeDtypeStruct((s, f), BF16)] * 3
    if wd is not None:
        out_specs, out_shape = [_rows(tm, d)] + out_specs, [jax.ShapeDtypeStruct((s, d), F32)] + out_shape
    if target is not None:
        in_specs, operands = in_specs + [_rows(tm, d)], operands + [target]
        out_specs = out_specs[:1] + [_full((8, LANES))] + out_specs[1:]
        out_shape = out_shape[:1] + [jax.ShapeDtypeStruct((8, LANES), F32)] + out_shape[1:]
    return pl.pallas_call(
        body, name=name, grid=(s // tm,), in_specs=in_specs, out_specs=out_specs, out_shape=out_shape,
        compiler_params=_params(),
    )(*operands)


def ffn_down(x, act, wd, dep, name):
    s, d = x.shape
    f = act.shape[1]
    tm = _row_tile(s)

    def body(x_ref, a_ref, w_ref, dep_ref, o_ref):
        o_ref[...] = x_ref[...] + 0.5 * _dot(a_ref[...], w_ref[...])

    return pl.pallas_call(
        body, name=name, grid=(s // tm,),
        in_specs=[_rows(tm, d), _rows(tm, f), _mat(*wd), _full(dep.shape)],
        out_specs=_rows(tm, d),
        out_shape=jax.ShapeDtypeStruct((s, d), F32),
        compiler_params=_params(),
    )(x, act, wd[0], dep)


def mix_in(x, gain, win_t, b_gate, gq_na, gk_na, gq_sw, gk_sw, bd, name):
    s, d = x.shape
    tm = _row_tile(s)
    gc = _col_chunk(2 * d)

    def body(x_ref, g_ref, w_ref, b_ref, gqa_ref, gka_ref, gqs_ref, gks_ref, bd_ref,
             hn_ref, zq_ref, qa_ref, ka_ref, qs_ref, ks_ref, gt_ref):
        xv = x_ref[...]
        hn = (xv * _rstd(xv) * g_ref[...]).astype(BF16)
        hn_ref[...] = hn

        def proj(c0, c1):
            return _dotg(hn, w_ref[c0:c1, :], NT)

        def headnorm(z, g, bdm):
            return z * lax.rsqrt(_group_mean(z * z, bdm) + EPS) * g

        bd512 = bd_ref[...]
        bd128 = bd_ref[0:SW_KV_WIDTH, 0:SW_KV_WIDTH]
        z = proj(0, 512)
        zq_ref[:, 0:512] = z.astype(BF16)
        qa_ref[...] = (headnorm(z, gqa_ref[...], bd512) * SCALE).astype(BF16)
        z = proj(512, 1024)
        zq_ref[:, 512:1024] = z.astype(BF16)
        ka_ref[...] = headnorm(z, gka_ref[...], bd512).astype(BF16)
        z = proj(1024, 1536)
        zq_ref[:, 1024:1536] = z.astype(BF16)
        z = proj(1536, 2048)
        zq_ref[:, 1536:2048] = z.astype(BF16)
        qs_ref[...] = (headnorm(z, gqs_ref[...], bd512) * SCALE).astype(BF16)
        z = proj(2048, 2176)
        zq_ref[:, 2048:2176] = z.astype(BF16)
        ks_ref[...] = headnorm(z, gks_ref[...], bd128).astype(BF16)
        z = proj(2176, 2304)
        zq_ref[:, 2176:2304] = z.astype(BF16)
        for c0 in range(0, 2 * d, gc):
            zg = proj(QKV_WIDTH + c0, QKV_WIDTH + c0 + gc) + b_ref[:, c0:c0 + gc]
            gt_ref[:, c0:c0 + gc] = _sigmoid(zg).astype(BF16)

    return pl.pallas_call(
        body, name=name, grid=(s // tm,),
        in_specs=[_rows(tm, d), _full((1, d)), _mat(*win_t), _full((1, 2 * d)),
                  _full((1, 512)), _full((1, 512)), _full((1, 512)), _full((1, 128)), _full((MXU_TILE, MXU_TILE))],
        out_specs=[_rows(tm, d), _rows(tm, QKV_WIDTH), _rows(tm, 512), _rows(tm, 512), _rows(tm, 512),
                   _rows(tm, 128), _rows(tm, 2 * d)],
        out_shape=[jax.ShapeDtypeStruct((s, d), BF16), jax.ShapeDtypeStruct((s, QKV_WIDTH), BF16),
                   jax.ShapeDtypeStruct((s, 512), BF16), jax.ShapeDtypeStruct((s, 512), BF16),
                   jax.ShapeDtypeStruct((s, 512), BF16), jax.ShapeDtypeStruct((s, 128), BF16),
                   jax.ShapeDtypeStruct((s, 2 * d), BF16)],
        compiler_params=_params(),
    )(x, gain, win_t[0], b_gate, gq_na, gk_na, gq_sw, gk_sw, bd)


def _na_iotas():
    qc = lax.broadcasted_iota(jnp.int32, (GRID_W, LANES), 0)
    ln = lax.broadcasted_iota(jnp.int32, (GRID_W, LANES), 1)
    low = ln < GRID_W
    kc = jnp.where(low, ln, ln - GRID_W)
    diff = kc - qc + (NA_COLS - 1)
    qcs = jnp.clip(qc - NA_COLS // 2, 0, GRID_W - NA_COLS)
    inwin = (kc >= qcs) & (kc < qcs + NA_COLS)
    return diff, low, inwin


NA_RI = 2 * NA_ROWS - 1
NA_CI = 2 * NA_COLS - 1
NA_T2 = NA_RI + 1


def _rpb_rows(rpb):
    h = rpb.shape[0]
    padded = jnp.pad(rpb, ((0, 0), (1, 1), (0, GRID_W - NA_CI)))
    return jnp.concatenate([padded[:, :NA_T2], padded[:, 1:NA_T2 + 1]], axis=2).reshape(h, NA_T2, LANES)


def _rpb_from_rows(rows):
    return rows[:, 1:, :NA_CI] + rows[:, :NA_RI, GRID_W:GRID_W + NA_CI]


def rpb_expand(rows, dep, name):
    n_heads = rows.shape[0]

    def body(r_ref, dep_ref, o_ref):
        for h in range(n_heads):
            for e in range(NA_T2):
                line = jnp.broadcast_to(r_ref[h, e:e + 1, :], (GRID_W, LANES))
                o_ref[h, e] = pltpu.roll(line, LANES - (NA_COLS - 1), 1, stride=1, stride_axis=0)

    return pl.pallas_call(
        body, name=name,
        in_specs=[pl.BlockSpec(memory_space=pltpu.VMEM), pl.BlockSpec(memory_space=pltpu.VMEM)],
        out_specs=pl.BlockSpec(memory_space=pltpu.VMEM),
        out_shape=jax.ShapeDtypeStruct((n_heads, NA_T2, GRID_W, LANES), F32),
        compiler_params=pltpu.CompilerParams(vmem_limit_bytes=V7X_VMEM_LIMIT),
    )(rows, dep)


def rpb_reduce(dt2, name):
    n_heads = dt2.shape[0]
    flip = jnp.asarray(np.eye(GRID_W)[::-1], BF16)

    def body(d_ref, j_ref, o_ref):
        jm = j_ref[...]
        for h in range(n_heads):
            for e in range(NA_T2):
                dv = d_ref[h, e]
                hi = dv.astype(BF16)
                mid = (dv - hi.astype(F32)).astype(BF16)
                lo = (dv - hi.astype(F32) - mid.astype(F32)).astype(BF16)
                rev = _dot(jm, hi) + _dot(jm, mid) + _dot(jm, lo)
                back = pltpu.roll(rev, LANES + (NA_COLS - 1) - (GRID_W - 1), 1, stride=1, stride_axis=0)
                o_ref[h, e:e + 1, :] = jnp.sum(back, axis=0, keepdims=True)

    return pl.pallas_call(
        body, name=name,
        in_specs=[pl.BlockSpec(memory_space=pltpu.VMEM)] * 2,
        out_specs=pl.BlockSpec(memory_space=pltpu.VMEM),
        out_shape=jax.ShapeDtypeStruct((n_heads, NA_T2, LANES), F32),
        compiler_params=pltpu.CompilerParams(vmem_limit_bytes=V7X_VMEM_LIMIT),
    )(dt2, flip)


NA_TQ = 4
NA_TK = NA_TQ + NA_ROWS
NA_KCH = NA_TK // 2


def _na_tile_geometry(t, rows):
    r = t * NA_TQ
    kbase = jnp.clip(r - NA_ROWS // 2, 0, rows - NA_TK)
    starts = [jnp.clip(r + a - NA_ROWS // 2, 0, rows - NA_ROWS) for a in range(NA_TQ)]
    return r, kbase, starts


def _na_tile_mask(kbase, starts, low, inwin):
    half = jnp.where(low, 0, 1)
    cols = []
    for c in range(NA_KCH):
        krow = kbase + 2 * c + half
        cols.append(jnp.concatenate(
            [jnp.where(inwin & (krow >= st) & (krow < st + NA_ROWS), 0.0, NEG) for st in starts], axis=0))
    return jnp.concatenate(cols, axis=1)


def na_masks(rows, dep, name):
    n_tiles = rows // NA_TQ
    assert n_tiles >= 2 and NA_TQ >= NA_ROWS // 2

    def body(dep_ref, o_ref):
        _, low, inwin = _na_iotas()
        for e, t in enumerate((0, 1, n_tiles - 1)):
            _, kbase, starts = _na_tile_geometry(t, rows)
            o_ref[e] = _na_tile_mask(kbase, starts, low, inwin)

    return pl.pallas_call(
        body, name=name,
        in_specs=[pl.BlockSpec(memory_space=pltpu.VMEM)], out_specs=pl.BlockSpec(memory_space=pltpu.VMEM),
        out_shape=jax.ShapeDtypeStruct((3, NA_TQ * GRID_W, NA_TK * GRID_W), F32),
        compiler_params=pltpu.CompilerParams(vmem_limit_bytes=V7X_VMEM_LIMIT),
    )(dep)


def _na_tile_index(r, kbase, a, c):
    return jnp.clip(kbase + 2 * c - (r + a) + NA_ROWS, 0, NA_T2 - 1)


def _na_tile_scores(q, k, t2_ref, hh, r, kbase, madd):
    bias = jnp.concatenate(
        [jnp.concatenate([t2_ref[hh, _na_tile_index(r, kbase, a, c)] for a in range(NA_TQ)], axis=0)
         for c in range(NA_KCH)], axis=1)
    return _dotg(q, k, NT) + bias + madd


def _softmax_rows(sc):
    e = jnp.exp(sc - jnp.max(sc, axis=1, keepdims=True))
    return e * (1.0 / jnp.sum(e, axis=1, keepdims=True))


def na_fwd(qa, ka, zq, t2, masks, name):
    s = qa.shape[0]
    rows = s // GRID_W
    n_pairs = NA_WIDTH // LANES
    v_blk0 = (2 * NA_WIDTH) // LANES

    assert rows % NA_TQ == 0 and rows >= NA_TK
    tq, tk = NA_TQ * GRID_W, NA_TK * GRID_W

    def body(q_ref, k_ref, v_ref, t2_ref, m_ref, o_ref, s_scr, p_scr):
        def tile(t, carry):
            r, kbase, _ = _na_tile_geometry(t, rows)
            madd = m_ref[_edge_variant(t, rows // NA_TQ)]
            qr = pl.ds(pl.multiple_of(r * GRID_W, tq), tq)
            kr = pl.ds(pl.multiple_of(kbase * GRID_W, tq), tk)
            for hh in range(2):
                lanes = slice(HEAD_DIM * hh, HEAD_DIM * (hh + 1))
                s_scr[tq * hh:tq * (hh + 1), :] = _na_tile_scores(q_ref[qr, lanes], k_ref[kr, lanes], t2_ref, hh, r,
                                                                  kbase, madd)
            p_scr[...] = _softmax_rows(s_scr[...]).astype(BF16)
            for hh in range(2):
                lanes = slice(HEAD_DIM * hh, HEAD_DIM * (hh + 1))
                o_ref[qr, lanes] = _dot(p_scr[tq * hh:tq * (hh + 1), :], v_ref[kr, lanes]).astype(BF16)
            return carry

        lax.fori_loop(0, rows // NA_TQ, tile, 0, unroll=2)

    col = lambda off: pl.BlockSpec((s, LANES), lambda p, _o=off: (0, _o + p))
    return pl.pallas_call(
        body, name=name, grid=(n_pairs,),
        in_specs=[col(0), col(0), col(v_blk0),
                  pl.BlockSpec((2, NA_T2, GRID_W, LANES), lambda p: (p, 0, 0, 0)),
                  pl.BlockSpec(masks.shape, lambda p: (0, 0, 0), pipeline_mode=ONCE)],
        out_specs=col(0),
        out_shape=jax.ShapeDtypeStruct((s, NA_WIDTH), BF16),
        scratch_shapes=[pltpu.VMEM((2 * tq, tk), F32), pltpu.VMEM((2 * tq, tk), BF16)],
        compiler_params=_params(),
    )(qa, ka, zq, t2, masks)


def na_bwd(qa, ka, zq, t2, masks, o_na, do_na, name):
    s = qa.shape[0]
    rows = s // GRID_W
    n_pairs = NA_WIDTH // LANES
    v_blk0 = (2 * NA_WIDTH) // LANES

    tq, tk = NA_TQ * GRID_W, NA_TK * GRID_W

    def body(q_ref, k_ref, v_ref, t2_ref, m_ref, o_ref, do_ref, dq_ref, dk_ref, dv_ref, dt2_ref):
        dk_ref[...] = jnp.zeros(dk_ref.shape, F32)
        dv_ref[...] = jnp.zeros(dv_ref.shape, F32)
        dt2_ref[...] = jnp.zeros(dt2_ref.shape, F32)

        def tile(t, carry):
            r, kbase, _ = _na_tile_geometry(t, rows)
            madd = m_ref[_edge_variant(t, rows // NA_TQ)]
            qr = pl.ds(pl.multiple_of(r * GRID_W, tq), tq)
            kr = pl.ds(pl.multiple_of(kbase * GRID_W, tq), tk)
            for hh in range(2):
                lanes = slice(HEAD_DIM * hh, HEAD_DIM * (hh + 1))
                q, k, v = q_ref[qr, lanes], k_ref[kr, lanes], v_ref[kr, lanes]
                p = _softmax_rows(_na_tile_scores(q, k, t2_ref, hh, r, kbase, madd))
                do = do_ref[qr, lanes]
                delta = jnp.sum(do.astype(F32) * o_ref[qr, lanes].astype(F32), axis=1, keepdims=True)
                ds = p * (_dotg(do, v, NT) - delta)
                shared = {}
                for a in range(NA_TQ):
                    for c in range(NA_KCH):
                        shared.setdefault(2 * c - a, []).append(
                            ds[GRID_W * a:GRID_W * (a + 1), LANES * c:LANES * (c + 1)])
                for offset, parts in shared.items():
                    e = jnp.clip(offset + kbase - r + NA_ROWS, 0, NA_T2 - 1)
                    dt2_ref[hh, e] = dt2_ref[hh, e] + functools.reduce(jnp.add, parts)
                dsb = ds.astype(BF16)
                dq_ref[qr, lanes] = _dot(dsb, k)
                dk_ref[kr, lanes] = dk_ref[kr, lanes] + _dotg(dsb, q, TN)
                dv_ref[kr, lanes] = dv_ref[kr, lanes] + _dotg(p.astype(BF16), do, TN)
            return carry

        lax.fori_loop(0, rows // NA_TQ, tile, 0, unroll=2)

    col = lambda off: pl.BlockSpec((s, LANES), lambda p, _o=off: (0, _o + p))
    t2spec = pl.BlockSpec((2, NA_T2, GRID_W, LANES), lambda p: (p, 0, 0, 0))
    return pl.pallas_call(
        body, name=name, grid=(n_pairs,),
        in_specs=[col(0), col(0), col(v_blk0), t2spec,
                  pl.BlockSpec(masks.shape, lambda p: (0, 0, 0), pipeline_mode=ONCE), col(0), col(0)],
        out_specs=[col(0), col(0), col(0), t2spec],
        out_shape=[jax.ShapeDtypeStruct((s, NA_WIDTH), F32)] * 3 + [jax.ShapeDtypeStruct(t2.shape, F32)],
        compiler_params=_params(),
    )(qa, ka, zq, t2, masks, o_na, do_na)


def _t5_bucket_map():
    rel = np.arange(3 * SW_BLOCK)[None, :] - SW_BLOCK - np.arange(SW_BLOCK)[:, None]
    nb = REL_BUCKETS // 2
    max_exact = nb // 2
    n = np.abs(rel)
    large = max_exact + (np.log(np.maximum(n, 1) / max_exact)
                         / np.log(REL_MAX_DIST / max_exact) * (nb - max_exact)).astype(np.int32)
    large = np.minimum(large, nb - 1)
    return ((rel > 0) * nb + np.where(n < max_exact, n, large)).astype(np.int32)


def t5_expand(table, bmap, dep, name):
    def body(tab_ref, bm_ref, dep_ref, o_ref):
        bm = bm_ref[...]
        j, inwin = _sw_mask_iotas()
        masks = [jnp.where(keep, 0.0, NEG) for keep in (inwin & (j >= SW_BLOCK), inwin, inwin & (j < 2 * SW_BLOCK))]
        for h in range(SW_HEADS):
            t = jnp.zeros(bm.shape, F32)
            for b in range(REL_BUCKETS):
                t = jnp.where(bm == b, tab_ref[b, h], t)
            for e, madd in enumerate(masks):
                o_ref[e, h] = t + madd

    return pl.pallas_call(
        body, name=name,
        in_specs=[pl.BlockSpec(memory_space=pltpu.SMEM), pl.BlockSpec(memory_space=pltpu.VMEM),
                  pl.BlockSpec(memory_space=pltpu.VMEM)],
        out_specs=pl.BlockSpec(memory_space=pltpu.VMEM),
        out_shape=jax.ShapeDtypeStruct((SW_EDGES, SW_HEADS) + bmap.shape, F32),
        compiler_params=pltpu.CompilerParams(vmem_limit_bytes=V7X_VMEM_LIMIT),
    )(table, bmap, dep)


def t5_reduce(dbias_list, bmap, name):
    n = len(dbias_list)

    def body(*refs):
        d_refs, bm_ref, o_ref = refs[:n], refs[n], refs[n + 1]
        bm = bm_ref[...]
        for h in range(SW_HEADS):
            dv = d_refs[0][h]
            for other in d_refs[1:]:
                dv = dv + other[h]
            rows = [jnp.sum(jnp.where(bm == b, dv, 0.0), axis=0, keepdims=True) for b in range(REL_BUCKETS)]
            r = jnp.concatenate(rows, axis=0)
            o_ref[h] = jnp.broadcast_to(jnp.sum(r, axis=1, keepdims=True), (REL_BUCKETS, LANES))

    return pl.pallas_call(
        body, name=name,
        in_specs=[pl.BlockSpec(memory_space=pltpu.VMEM)] * (n + 1),
        out_specs=pl.BlockSpec(memory_space=pltpu.VMEM),
        out_shape=jax.ShapeDtypeStruct((SW_HEADS, REL_BUCKETS, LANES), F32),
        compiler_params=pltpu.CompilerParams(vmem_limit_bytes=V7X_VMEM_LIMIT),
    )(*dbias_list, bmap)


def _sw_mask_iotas():
    a = lax.broadcasted_iota(jnp.int32, (SW_BLOCK, 3 * SW_BLOCK), 0)
    j = lax.broadcasted_iota(jnp.int32, (SW_BLOCK, 3 * SW_BLOCK), 1)
    inwin = jnp.abs(j - SW_BLOCK - a) <= SW_BLOCK
    return j, inwin


SW_STACK = SW_HEADS * SW_BLOCK
SW_EDGES = 3


def _edge_variant(n, nb):
    return jnp.where(n == 0, 0, jnp.where(n == nb - 1, 2, 1))


def _sw_softmax(sc, sk):
    m = jnp.maximum(jnp.max(sc, axis=1, keepdims=True), sk)
    e = jnp.exp(sc - m)
    es = jnp.exp(sk - m)
    inv = 1.0 / (jnp.sum(e, axis=1, keepdims=True) + es)
    return e * inv, es * inv


def _sw_prologue(k_ref, v_ref, kp, vp, sink_ref, s):
    pad = s + 2 * SW_BLOCK
    zeros = jnp.zeros((SW_BLOCK, SW_KV_WIDTH), BF16)
    kp[0:SW_BLOCK, :] = zeros
    vp[0:SW_BLOCK, :] = zeros
    kp[SW_BLOCK + s:pad, :] = zeros
    vp[SW_BLOCK + s:pad, :] = zeros
    kp[SW_BLOCK:SW_BLOCK + s, :] = k_ref[...]
    vp[SW_BLOCK:SW_BLOCK + s, :] = v_ref[...]
    return jnp.concatenate([jnp.full((SW_BLOCK, 1), sink_ref[h], F32) for h in range(SW_HEADS)], axis=0)


def sw_fwd(qs, ks, zq, t5b, sink, dep, name):
    s = qs.shape[0]
    nb = s // SW_BLOCK
    v_blk = (3 * NA_WIDTH + SW_Q_WIDTH + SW_KV_WIDTH) // LANES
    pad = s + 2 * SW_BLOCK
    assert nb >= 2

    def body(q_ref, k_ref, v_ref, b_ref, sink_ref, dep_ref, o_ref, kp, vp, s_scr, p_scr):
        sink_col = _sw_prologue(k_ref, v_ref, kp, vp, sink_ref, s)

        def blk(n, carry):
            q0 = pl.multiple_of(n * SW_BLOCK, SW_BLOCK)
            qr, kr = pl.ds(q0, SW_BLOCK), pl.ds(q0, 3 * SW_BLOCK)
            for h in range(SW_HEADS):
                g = h // SW_REP
                s_scr[SW_BLOCK * h:SW_BLOCK * (h + 1), :] = _dotg(
                    q_ref[qr, HEAD_DIM * h:HEAD_DIM * (h + 1)], kp[kr, HEAD_DIM * g:HEAD_DIM * (g + 1)], NT)
            p, _ = _sw_softmax(s_scr[...] + b_ref[_edge_variant(n, nb)], sink_col)
            p_scr[...] = p.astype(BF16)
            for h in range(SW_HEADS):
                g = h // SW_REP
                o_ref[qr, HEAD_DIM * h:HEAD_DIM * (h + 1)] = _dot(
                    p_scr[SW_BLOCK * h:SW_BLOCK * (h + 1), :], vp[kr, HEAD_DIM * g:HEAD_DIM * (g + 1)]).astype(BF16)
            return carry

        lax.fori_loop(0, nb, blk, 0, unroll=2)

    return pl.pallas_call(
        body, name=name, grid=(1,),
        in_specs=[_full((s, SW_Q_WIDTH)), _full((s, SW_KV_WIDTH)),
                  pl.BlockSpec((s, SW_KV_WIDTH), lambda i: (0, v_blk)),
                  pl.BlockSpec(t5b.shape, lambda i: (0, 0, 0), pipeline_mode=ONCE), pl.BlockSpec(memory_space=pltpu.SMEM),
                  _full(dep.shape)],
        out_specs=_full((s, SW_Q_WIDTH)),
        out_shape=jax.ShapeDtypeStruct((s, SW_Q_WIDTH), BF16),
        scratch_shapes=[pltpu.VMEM((pad, SW_KV_WIDTH), BF16), pltpu.VMEM((pad, SW_KV_WIDTH), BF16),
                        pltpu.VMEM((SW_STACK, 3 * SW_BLOCK), F32), pltpu.VMEM((SW_STACK, 3 * SW_BLOCK), BF16)],
        compiler_params=_params(),
    )(qs, ks, zq, t5b, sink, dep)


def sw_bwd(qs, ks, zq, t5b, sink, o_sw, do_sw, name):
    s = qs.shape[0]
    nb = s // SW_BLOCK
    v_blk = (3 * NA_WIDTH + SW_Q_WIDTH + SW_KV_WIDTH) // LANES
    pad = s + 2 * SW_BLOCK

    def body(q_ref, k_ref, v_ref, b_ref, sink_ref, o_ref, do_ref,
             dq_ref, dk_ref, dv_ref, db_ref, dsk_ref, kp, vp, dkp, dvp, s_scr, dp_scr, ds_scr, p_scr):
        sink_col = _sw_prologue(k_ref, v_ref, kp, vp, sink_ref, s)
        dkp[...] = jnp.zeros(dkp.shape, F32)
        dvp[...] = jnp.zeros(dvp.shape, F32)
        db_ref[...] = jnp.zeros(db_ref.shape, F32)
        dsk_ref[...] = jnp.zeros(dsk_ref.shape, F32)

        def blk(n, carry):
            q0 = pl.multiple_of(n * SW_BLOCK, SW_BLOCK)
            qr, kr = pl.ds(q0, SW_BLOCK), pl.ds(q0, 3 * SW_BLOCK)
            deltas = []
            for h in range(SW_HEADS):
                g = h // SW_REP
                hl, kl = slice(HEAD_DIM * h, HEAD_DIM * (h + 1)), slice(HEAD_DIM * g, HEAD_DIM * (g + 1))
                rows = slice(SW_BLOCK * h, SW_BLOCK * (h + 1))
                do = do_ref[qr, hl]
                s_scr[rows, :] = _dotg(q_ref[qr, hl], kp[kr, kl], NT)
                dp_scr[rows, :] = _dotg(do, vp[kr, kl], NT)
                deltas.append(jnp.sum(do.astype(F32) * o_ref[qr, hl].astype(F32), axis=1, keepdims=True))
            delta = jnp.concatenate(deltas, axis=0)
            p, ps = _sw_softmax(s_scr[...] + b_ref[_edge_variant(n, nb)], sink_col)
            ds = p * (dp_scr[...] - delta)
            db_ref[...] = db_ref[...] + ds
            dsk_ref[...] = dsk_ref[...] - jnp.broadcast_to(ps * delta, (SW_STACK, LANES))
            ds_scr[...] = ds.astype(BF16)
            p_scr[...] = p.astype(BF16)
            for g in range(SW_HEADS // SW_REP):
                kl = slice(HEAD_DIM * g, HEAD_DIM * (g + 1))
                k = kp[kr, kl]
                dkw = jnp.zeros((3 * SW_BLOCK, HEAD_DIM), F32)
                dvw = jnp.zeros((3 * SW_BLOCK, HEAD_DIM), F32)
                for r in range(SW_REP):
                    h = g * SW_REP + r
                    hl, rows = slice(HEAD_DIM * h, HEAD_DIM * (h + 1)), slice(SW_BLOCK * h, SW_BLOCK * (h + 1))
                    dsb = ds_scr[rows, :]
                    dq_ref[qr, hl] = _dot(dsb, k)
                    dkw = dkw + _dotg(dsb, q_ref[qr, hl], TN)
                    dvw = dvw + _dotg(p_scr[rows, :], do_ref[qr, hl], TN)
                dkp[kr, kl] = dkp[kr, kl] + dkw
                dvp[kr, kl] = dvp[kr, kl] + dvw
            return carry

        lax.fori_loop(0, nb, blk, 0, unroll=2)
        dk_ref[...] = dkp[SW_BLOCK:SW_BLOCK + s, :]
        dv_ref[...] = dvp[SW_BLOCK:SW_BLOCK + s, :]

    assert nb >= 2
    bias_spec = _full((SW_STACK, 3 * SW_BLOCK))
    return pl.pallas_call(
        body, name=name, grid=(1,),
        in_specs=[_full((s, SW_Q_WIDTH)), _full((s, SW_KV_WIDTH)),
                  pl.BlockSpec((s, SW_KV_WIDTH), lambda i: (0, v_blk)),
                  pl.BlockSpec(t5b.shape, lambda i: (0, 0, 0), pipeline_mode=ONCE), pl.BlockSpec(memory_space=pltpu.SMEM),
                  _full((s, SW_Q_WIDTH)), _full((s, SW_Q_WIDTH))],
        out_specs=[_full((s, SW_Q_WIDTH)), _full((s, SW_KV_WIDTH)), _full((s, SW_KV_WIDTH)), bias_spec,
                   _full((SW_STACK, LANES))],
        out_shape=[jax.ShapeDtypeStruct((s, SW_Q_WIDTH), F32), jax.ShapeDtypeStruct((s, SW_KV_WIDTH), F32),
                   jax.ShapeDtypeStruct((s, SW_KV_WIDTH), F32),
                   jax.ShapeDtypeStruct((SW_STACK, 3 * SW_BLOCK), F32),
                   jax.ShapeDtypeStruct((SW_STACK, LANES), F32)],
        scratch_shapes=[pltpu.VMEM((pad, SW_KV_WIDTH), BF16), pltpu.VMEM((pad, SW_KV_WIDTH), BF16),
                        pltpu.VMEM((pad, SW_KV_WIDTH), F32), pltpu.VMEM((pad, SW_KV_WIDTH), F32),
                        pltpu.VMEM((SW_STACK, 3 * SW_BLOCK), F32), pltpu.VMEM((SW_STACK, 3 * SW_BLOCK), F32),
                        pltpu.VMEM((SW_STACK, 3 * SW_BLOCK), BF16), pltpu.VMEM((SW_STACK, 3 * SW_BLOCK), BF16)],
        compiler_params=_params(),
    )(qs, ks, zq, t5b, sink, o_sw, do_sw)


def merge_out(x, o_na, o_sw, gt, wbna_t, wbsw_t, wout, name):
    s, d = x.shape
    tm = _row_tile(s)

    def body(x_ref, ona_ref, osw_ref, gt_ref, wna_ref, wsw_ref, wo_ref, xo_ref, ana_ref, asw_ref, mg_ref):
        a_na = _dotg(ona_ref[...], wna_ref[...], NT)
        a_sw = _dotg(osw_ref[...], wsw_ref[...], NT)
        g_na, g_sw = gt_ref[:, 0:d].astype(F32), gt_ref[:, d:2 * d].astype(F32)
        ana_ref[...] = (a_na * g_na * (1.0 - g_na)).astype(BF16)
        asw_ref[...] = (a_sw * g_sw * (1.0 - g_sw)).astype(BF16)
        merged = (g_na * a_na + g_sw * a_sw).astype(BF16)
        mg_ref[...] = merged
        xo_ref[...] = x_ref[...] + _dot(merged, wo_ref[...])

    return pl.pallas_call(
        body, name=name, grid=(s // tm,),
        in_specs=[_rows(tm, d), _rows(tm, 512), _rows(tm, 512), _rows(tm, 2 * d),
                  _mat(*wbna_t), _mat(*wbsw_t), _mat(*wout)],
        out_specs=[_rows(tm, d)] * 4,
        out_shape=[jax.ShapeDtypeStruct((s, d), F32)] + [jax.ShapeDtypeStruct((s, d), BF16)] * 3,
        compiler_params=_params(),
    )(x, o_na, o_sw, gt, wbna_t[0], wbsw_t[0], wout[0])


def mix_bwd_out(dx, gt, a_na, a_sw, wbna_t, wbsw_t, wout, dep, name):
    s, d = dx.shape
    tm = _row_tile(s)

    def body(dx_ref, gt_ref, ana_ref, asw_ref, wna_ref, wsw_ref, wo_ref, dep_ref,
             dxb_ref, dzg_ref, dana_ref, dasw_ref, dona_ref, dosw_ref, dbg_ref):
        @pl.when(pl.program_id(0) == 0)
        def _():
            dbg_ref[...] = jnp.zeros(dbg_ref.shape, F32)

        dxb = dx_ref[...].astype(BF16)
        dxb_ref[...] = dxb
        dm = _dotg(dxb, wo_ref[...], NT)
        for i, (a_ref, da_ref, w_ref, do_ref) in enumerate(
                [(ana_ref, dana_ref, wna_ref, dona_ref), (asw_ref, dasw_ref, wsw_ref, dosw_ref)]):
            gi = gt_ref[:, i * d:(i + 1) * d].astype(F32)
            da = (dm * gi).astype(BF16)
            da_ref[...] = da
            do_ref[...] = _dot(da, w_ref[...]).astype(BF16)
            dzg = dm * a_ref[...].astype(F32)
            dzg_ref[:, i * d:(i + 1) * d] = dzg.astype(BF16)
            dbg_ref[:, i * d:(i + 1) * d] = dbg_ref[:, i * d:(i + 1) * d] + jnp.sum(dzg, axis=0, keepdims=True)

    return pl.pallas_call(
        body, name=name, grid=(s // tm,),
        in_specs=[_rows(tm, d), _rows(tm, 2 * d), _rows(tm, d), _rows(tm, d),
                  _mat(*wbna_t), _mat(*wbsw_t), _mat(*wout), _full(dep.shape)],
        out_specs=[_rows(tm, d), _rows(tm, 2 * d), _rows(tm, d), _rows(tm, d), _rows(tm, 512), _rows(tm, 512),
                   _full((1, 2 * d))],
        out_shape=[jax.ShapeDtypeStruct((s, d), BF16), jax.ShapeDtypeStruct((s, 2 * d), BF16),
                   jax.ShapeDtypeStruct((s, d), BF16), jax.ShapeDtypeStruct((s, d), BF16),
                   jax.ShapeDtypeStruct((s, 512), BF16), jax.ShapeDtypeStruct((s, 512), BF16),
                   jax.ShapeDtypeStruct((1, 2 * d), F32)],
        compiler_params=_params(),
    )(dx, gt, a_na, a_sw, wbna_t[0], wbsw_t[0], wout[0], dep)


def qk_norm_bwd(dqa, dka, dva, dqs, dks, dvs, zq, dzg, gq_na, gk_na, gq_sw, gk_sw, bd, name):
    s = zq.shape[0]
    d2 = dzg.shape[1]
    n_in = QKV_WIDTH + d2
    tm = _row_tile(s)

    def body(dqa_ref, dka_ref, dva_ref, dqs_ref, dks_ref, dvs_ref, zq_ref, dzg_ref,
             gqa_ref, gka_ref, gqs_ref, gks_ref, bd_ref, dz_ref, dgqa_ref, dgka_ref, dgqs_ref, dgks_ref):
        @pl.when(pl.program_id(0) == 0)
        def _():
            for r in (dgqa_ref, dgka_ref, dgqs_ref, dgks_ref):
                r[...] = jnp.zeros(r.shape, F32)

        bd512 = bd_ref[...]
        bd128 = bd_ref[0:SW_KV_WIDTH, 0:SW_KV_WIDTH]

        def one(c0, c1, dy_ref, g_ref, dg_ref, bdm, scale):
            z = zq_ref[:, c0:c1].astype(F32)
            r = lax.rsqrt(_group_mean(z * z, bdm) + EPS)
            zh = z * r
            dy = dy_ref[...] * scale
            dyg = dy * g_ref[...]
            dz = r * (dyg - zh * _group_mean(dyg * zh, bdm))
            dz_ref[:, c0:c1] = dz.astype(BF16)
            dg_ref[...] = dg_ref[...] + jnp.sum(dy * zh, axis=0, keepdims=True)

        one(0, 512, dqa_ref, gqa_ref, dgqa_ref, bd512, SCALE)
        one(512, 1024, dka_ref, gka_ref, dgka_ref, bd512, 1.0)
        dz_ref[:, 1024:1536] = dva_ref[...].astype(BF16)
        one(1536, 2048, dqs_ref, gqs_ref, dgqs_ref, bd512, SCALE)
        one(2048, 2176, dks_ref, gks_ref, dgks_ref, bd128, 1.0)
        dz_ref[:, 2176:2304] = dvs_ref[...].astype(BF16)
        dz_ref[:, QKV_WIDTH:n_in] = dzg_ref[...]

    return pl.pallas_call(
        body, name=name, grid=(s // tm,),
        in_specs=[_rows(tm, 512), _rows(tm, 512), _rows(tm, 512), _rows(tm, 512), _rows(tm, 128), _rows(tm, 128),
                  _rows(tm, QKV_WIDTH), _rows(tm, d2),
                  _full((1, 512)), _full((1, 512)), _full((1, 512)), _full((1, 128)), _full((MXU_TILE, MXU_TILE))],
        out_specs=[_rows(tm, n_in), _full((1, 512)), _full((1, 512)), _full((1, 512)), _full((1, 128))],
        out_shape=[jax.ShapeDtypeStruct((s, n_in), BF16)] + [jax.ShapeDtypeStruct((1, 512), F32)] * 3
                  + [jax.ShapeDtypeStruct((1, 128), F32)],
        compiler_params=_params(),
    )(dqa, dka, dva, dqs, dks, dvs, zq, dzg, gq_na, gk_na, gq_sw, gk_sw, bd)


def ffn_bwd_act(dx, wd, hg, hu, name):
    s, d = dx.shape
    f = wd[0].shape[1]
    tm = _row_tile(s)
    fc = _col_chunk(f)

    def body(dx_ref, w_ref, hg_ref, hu_ref, dxb_ref, dhg_ref, dhu_ref):
        dxv = dx_ref[...]
        dxb_ref[...] = dxv.astype(BF16)
        half = (0.5 * dxv).astype(BF16)
        for c0 in range(0, f, fc):
            dact = _dotg(half, w_ref[c0:c0 + fc, :], NT)
            dhu_ref[:, c0:c0 + fc] = (dact * hu_ref[:, c0:c0 + fc].astype(F32)).astype(BF16)
            dhg_ref[:, c0:c0 + fc] = (dact * hg_ref[:, c0:c0 + fc].astype(F32)).astype(BF16)

    return pl.pallas_call(
        body, name=name, grid=(s // tm,),
        in_specs=[_rows(tm, d), _mat(*wd), _rows(tm, f), _rows(tm, f)],
        out_specs=[_rows(tm, d), _rows(tm, f), _rows(tm, f)],
        out_shape=[jax.ShapeDtypeStruct((s, d), BF16), jax.ShapeDtypeStruct((s, f), BF16),
                   jax.ShapeDtypeStruct((s, f), BF16)],
        compiler_params=_params(),
    )(dx, wd[0], hg, hu)


def proj_bwd_norm(acts, weights, x, gain, dx, dep, name):
    s, d = x.shape
    tm = min(_row_tile(s), 256)
    n = len(acts)

    def body(*refs):
        a_refs, w_refs = refs[:n], refs[n:2 * n]
        x_ref, g_ref, dx_ref, _, o_ref, dg_ref = refs[2 * n:]

        @pl.when(pl.program_id(0) == 0)
        def _():
            dg_ref[...] = jnp.zeros(dg_ref.shape, F32)

        dxn = _dot(a_refs[0][...], w_refs[0][...])
        for a_ref, w_ref in zip(a_refs[1:], w_refs[1:]):
            dxn = dxn + _dot(a_ref[...], w_ref[...])
        xv = x_ref[...]
        r = _rstd(xv)
        xh = xv * r
        dxh = dxn * g_ref[...]
        o_ref[...] = dx_ref[...] + r * (dxh - xh * jnp.mean(dxh * xh, axis=-1, keepdims=True))
        dg_ref[...] = dg_ref[...] + jnp.sum(dxn * xh, axis=0, keepdims=True)

    return pl.pallas_call(
        body, name=name, grid=(s // tm,),
        in_specs=[_rows(tm, a.shape[1]) for a in acts] + [_mat(*w) for w in weights]
                 + [_rows(tm, d), _full((1, d)), _rows(tm, d), _full(dep.shape)],
        out_specs=[_rows(tm, d), _full((1, d))],
        out_shape=[jax.ShapeDtypeStruct((s, d), F32), jax.ShapeDtypeStruct((1, d), F32)],
        compiler_params=_params(),
    )(*acts, *[w[0] for w in weights], x, gain, dx, dep)


def tn_matmul(products, name):
    s, n = products[0][0].shape
    tn = _tn_tile(n) if len(products) == 1 else _col_chunk(n)
    rhs = []
    for _, b, _ in products:
        if not any(b is seen for seen in rhs):
            rhs.append(b)
    which = [next(i for i, seen in enumerate(rhs) if b is seen) for _, b, _ in products]
    npr, nr = len(products), len(rhs)

    def body(*refs):
        a_refs, b_refs, o_refs = refs[:npr], refs[npr:npr + nr], refs[npr + nr:]
        for i, (_, _, scale) in enumerate(products):
            o_refs[i][...] = (scale * _dotg(a_refs[i][...], b_refs[which[i]][...], TN)).astype(BF16)

    return pl.pallas_call(
        body, name=name, grid=(n // tn,),
        in_specs=[pl.BlockSpec((s, tn), lambda i: (0, i))] * npr
                 + [pl.BlockSpec(b.shape, lambda i: (0, 0), pipeline_mode=ONCE) for b in rhs],
        out_specs=[pl.BlockSpec((tn, b.shape[1]), lambda i: (i, 0)) for _, b, _ in products],
        out_shape=[jax.ShapeDtypeStruct((n, b.shape[1]), BF16) for _, b, _ in products],
        compiler_params=_params(),
    )(*[a for a, _, _ in products], *rhs)


def _mesh_pos():
    return lax.axis_index("x"), lax.axis_index("y"), lax.axis_index("c")


def _peers():
    x, y, c = _mesh_pos()
    peers = []
    for rel in range(1, N_DEV):
        peers.append((1 - x if rel & 4 else x, 1 - y if rel & 2 else y, 1 - c if rel & 1 else c))
    return 4 * x + 2 * y + c, peers


HBM_SPEC = pl.BlockSpec(memory_space=pltpu.HBM)
SEM_SPEC = pl.BlockSpec(memory_space=pltpu.SEMAPHORE)


def _split_call(body, name, thru, n_sems, extra=(), with_token=True):
    hbm = lambda t: pltpu.with_memory_space_constraint(t, pltpu.HBM)
    effect = pltpu.CompilerParams(has_side_effects=pltpu.SideEffectType.DATAFLOW_SIDE_EFFECTING)
    nt = len(thru)
    thru_shapes = [pltpu.HBM(t.shape, t.dtype) for t in thru]
    if with_token:
        (after,) = extra
        outs = pl.pallas_call(
            body, name=name, in_specs=[HBM_SPEC] * nt + [pl.BlockSpec(memory_space=pl.ANY)],
            out_specs=[SEM_SPEC] * len(n_sems) + [HBM_SPEC] * nt + [pl.BlockSpec(memory_space=pltpu.VMEM)],
            out_shape=[pltpu.SemaphoreType.DMA((k,)) for k in n_sems] + thru_shapes
                      + [jax.ShapeDtypeStruct((8, LANES), F32)],
            input_output_aliases={i: len(n_sems) + i for i in range(nt)}, compiler_params=effect,
        )(*[hbm(t) for t in thru], after)
        return outs[:len(n_sems)], outs[len(n_sems):-1], outs[-1]
    return pl.pallas_call(
        body, name=name,
        in_specs=[HBM_SPEC] * nt + [SEM_SPEC] * len(n_sems) + [pl.BlockSpec(memory_space=pl.ANY)],
        out_specs=[HBM_SPEC] * nt, out_shape=thru_shapes,
        input_output_aliases={i: i for i in range(nt)}, compiler_params=effect,
    )(*thru, *extra)


def _gather_targets():
    x, y, c = _mesh_pos()
    return 4 * x + 2 * y + c, [(x, y, 1 - c), (1 - x, y, c), (x, 1 - y, c), (1 - x, 1 - y, c)]


def gather_start(shards, after, name):
    n = len(shards)
    zones = [lax.empty((w.shape[0], N_DEV) + w.shape[1:], w.dtype) for w in shards]

    def body(*refs):
        ins, zs = refs[:n], refs[n:2 * n]
        send_sems, recv_sems, local_sems = refs[2 * n + 1:2 * n + 4]
        token = refs[-1]
        me, targets = _gather_targets()
        for a in range(n):
            pltpu.make_async_copy(ins[a], zs[a].at[:, me], local_sems.at[a]).start()
            for k, to in enumerate(targets):
                pltpu.make_async_remote_copy(
                    src_ref=ins[a], dst_ref=zs[a].at[:, me], send_sem=send_sems.at[4 * a + k],
                    recv_sem=recv_sems.at[4 * a + k], device_id=to, device_id_type=MESH).start()
        token[...] = jnp.zeros(token.shape, F32)

    sems, thru, token = _split_call(body, name, list(shards) + zones, (4 * n, 4 * n, n), extra=(after,))
    return (sems, thru, n), token


def gather_wait(started, after, name):
    sems, thru, n = started

    def body(*refs):
        zs = refs[n:2 * n]
        send_sems, recv_sems, local_sems = refs[2 * n:2 * n + 3]
        _, targets = _gather_targets()
        for a in range(n):
            for k, to in enumerate(targets):
                cp = pltpu.make_async_remote_copy(
                    src_ref=zs[a].at[:, 0], dst_ref=zs[a].at[:, 0], send_sem=send_sems.at[4 * a + k],
                    recv_sem=recv_sems.at[4 * a + k], device_id=to, device_id_type=MESH)
                cp.wait_send()
                cp.wait_recv()
            pltpu.make_async_copy(zs[a].at[:, 0], zs[a].at[:, 0], local_sems.at[a]).wait()

    return _split_call(body, name, thru, (4 * n, 4 * n, n), extra=(*sems, after), with_token=False)[n:]


def forward_start(zones, after, name):
    n = len(zones)

    def body(*refs):
        zs = refs[:n]
        send_sems, recv_sems = refs[n + 1:n + 3]
        token = refs[-1]
        x, y, c = _mesh_pos()
        for a in range(n):
            for j, chip in enumerate([(1 - x, y), (x, 1 - y), (1 - x, 1 - y)]):
                blk = zs[a].at[:, 4 * chip[0] + 2 * chip[1] + c]
                pltpu.make_async_remote_copy(
                    src_ref=blk, dst_ref=blk, send_sem=send_sems.at[3 * a + j], recv_sem=recv_sems.at[3 * a + j],
                    device_id=(x, y, 1 - c), device_id_type=MESH).start()
        token[...] = jnp.zeros(token.shape, F32)

    sems, thru, token = _split_call(body, name, list(zones), (3 * n, 3 * n), extra=(after,))
    return (sems, thru, n), token


def forward_wait(started, after, name):
    sems, thru, n = started

    def body(*refs):
        zs = refs[:n]
        send_sems, recv_sems = refs[n:n + 2]
        x, y, c = _mesh_pos()
        for a in range(n):
            for j in range(3):
                cp = pltpu.make_async_remote_copy(
                    src_ref=zs[a].at[:, 0], dst_ref=zs[a].at[:, 0], send_sem=send_sems.at[3 * a + j],
                    recv_sem=recv_sems.at[3 * a + j], device_id=(x, y, 1 - c), device_id_type=MESH)
                cp.wait_send()
                cp.wait_recv()

    return _split_call(body, name, thru, (3 * n, 3 * n), extra=(*sems, after), with_token=False)


def scatter_start(groups, name):
    n = len(groups)
    flat = [g for grp in groups for g in grp]
    nf = len(flat)
    offs = np.cumsum([0] + [len(grp) for grp in groups])
    lands = [lax.empty((N_DEV, len(grp)) + grp[0].shape[1:], grp[0].dtype) for grp in groups]

    def body(*refs):
        ins, zones = refs[:nf], refs[nf:nf + n]
        send_sems, recv_sems, local_sems = refs[nf + n:nf + n + 3]
        token = refs[-1]
        me, peers = _peers()
        for a in range(n):
            for w in range(len(groups[a])):
                pltpu.make_async_copy(ins[offs[a] + w].at[me], zones[a].at[me, w], local_sems.at[a]).start()
        for k, peer in enumerate(peers):
            p_id = 4 * peer[0] + 2 * peer[1] + peer[2]
            for a in range(n):
                for w in range(len(groups[a])):
                    pltpu.make_async_remote_copy(
                        src_ref=ins[offs[a] + w].at[p_id], dst_ref=zones[a].at[me, w],
                        send_sem=send_sems.at[7 * a + k], recv_sem=recv_sems.at[7 * a + k],
                        device_id=peer, device_id_type=MESH).start()
        token[...] = jnp.zeros(token.shape, F32)

    hbm = lambda t: pltpu.with_memory_space_constraint(t, pltpu.HBM)
    outs = pl.pallas_call(
        body, name=name,
        in_specs=[HBM_SPEC] * (nf + n),
        out_specs=[SEM_SPEC] * 3 + [HBM_SPEC] * (nf + n) + [pl.BlockSpec(memory_space=pltpu.VMEM)],
        out_shape=[pltpu.SemaphoreType.DMA((7 * n,)), pltpu.SemaphoreType.DMA((7 * n,)), pltpu.SemaphoreType.DMA((n,))]
                  + [pltpu.HBM(t.shape, t.dtype) for t in flat + lands]
                  + [jax.ShapeDtypeStruct((8, LANES), F32)],
        input_output_aliases={i: 3 + i for i in range(nf + n)},
        compiler_params=pltpu.CompilerParams(has_side_effects=pltpu.SideEffectType.DATAFLOW_SIDE_EFFECTING),
    )(*[hbm(t) for t in flat], *[hbm(t) for t in lands])
    sems, thru, token = outs[:3], outs[3:3 + nf + n], outs[-1]
    return (sems, thru, [len(grp) for grp in groups]), token


def scatter_wait(started, after, name):
    (send_sems, recv_sems, local_sems), thru, sizes = started
    n = len(sizes)
    nf = len(thru) - n

    def body(*refs):
        zones = refs[nf:nf + n]
        s_sems, r_sems, l_sems = refs[nf + n:nf + n + 3]
        me, peers = _peers()
        for a in range(n):
            for k, peer in enumerate(peers):
                cp = pltpu.make_async_remote_copy(
                    src_ref=zones[a].at[0], dst_ref=zones[a].at[0],
                    send_sem=s_sems.at[7 * a + k], recv_sem=r_sems.at[7 * a + k], device_id=peer,
                    device_id_type=MESH)
                cp.wait_send()
                cp.wait_recv()
            pltpu.make_async_copy(zones[a].at[0], zones[a].at[0], l_sems.at[a]).wait()

    outs = pl.pallas_call(
        body, name=name,
        in_specs=[HBM_SPEC] * (nf + n) + [SEM_SPEC] * 3 + [pl.BlockSpec(memory_space=pl.ANY)],
        out_specs=[HBM_SPEC] * (nf + n),
        out_shape=[pltpu.HBM(t.shape, t.dtype) for t in thru],
        input_output_aliases={i: i for i in range(nf + n)},
        compiler_params=pltpu.CompilerParams(has_side_effects=pltpu.SideEffectType.DATAFLOW_SIDE_EFFECTING),
    )(*thru, send_sems, recv_sems, local_sems, after)
    return outs[nf:]


def pair_start(grads, after, name):
    nw = len(grads)
    land = lax.empty((4, nw) + grads[0].shape[1:], grads[0].dtype)

    def body(*refs):
        ins, zone = refs[:nw], refs[nw]
        send_sems, recv_sems = refs[nw + 2:nw + 4]
        x, y, c = _mesh_pos()
        for j in range(4):
            for w in range(nw):
                pltpu.make_async_remote_copy(
                    src_ref=ins[w].at[2 * j + (1 - c)], dst_ref=zone.at[j, w], send_sem=send_sems.at[0],
                    recv_sem=recv_sems.at[0], device_id=(x, y, 1 - c), device_id_type=MESH).start()
        refs[-1][...] = jnp.zeros(refs[-1].shape, F32)

    sems, thru, token = _split_call(body, name, list(grads) + [land], (1, 1), extra=(after,))
    return (sems, thru, nw), token


def pair_wait(started, after, name):
    sems, thru, nw = started

    def body(*refs):
        zone = refs[nw]
        send_sems, recv_sems = refs[nw + 1:nw + 3]
        x, y, c = _mesh_pos()
        cp = pltpu.make_async_remote_copy(src_ref=zone, dst_ref=zone, send_sem=send_sems.at[0],
                                          recv_sem=recv_sems.at[0], device_id=(x, y, 1 - c), device_id_type=MESH)
        cp.wait_send()
        cp.wait_recv()

    outs = _split_call(body, name, thru, (1, 1), extra=(*sems, after), with_token=False)
    return outs[:nw], outs[nw]


def pair_sum(grads, land, name):
    nw = len(grads)
    _, r, c_dim = grads[0].shape

    def body(*refs):
        g_refs, l_ref, o_ref = refs[:nw], refs[nw], refs[nw + 1]
        core = lax.axis_index("c")
        for w in range(nw):
            o_ref[0, w] = (g_refs[w][0, core].astype(F32) + l_ref[0, w].astype(F32)).astype(BF16)

    return pl.pallas_call(
        body, name=name, grid=(4,),
        in_specs=[pl.BlockSpec((1, 2, r, c_dim), lambda j: (j, 0, 0, 0))] * nw
                 + [pl.BlockSpec((1, nw, r, c_dim), lambda j: (j, 0, 0, 0))],
        out_specs=pl.BlockSpec((1, nw, r, c_dim), lambda j: (j, 0, 0, 0)),
        out_shape=jax.ShapeDtypeStruct((4, nw, r, c_dim), BF16),
        compiler_params=_params(),
    )(*[g.reshape(4, 2, r, c_dim) for g in grads], land)


def _other_chips():
    x, y, c = _mesh_pos()
    chips = []
    for rel in range(1, 4):
        px, py = (1 - x if rel & 2 else x), (1 - y if rel & 1 else y)
        chips.append((px, py, 2 * px + py))
    return 2 * x + y, c, chips


def chip_start(pair_sums, after, name):
    land = lax.empty(pair_sums.shape, pair_sums.dtype)

    def body(*refs):
        h_ref, zone = refs[0], refs[1]
        send_sems, recv_sems, local_sem = refs[3:6]
        mine, c, chips = _other_chips()
        pltpu.make_async_copy(h_ref.at[mine], zone.at[mine], local_sem.at[0]).start()
        for k, (px, py, j) in enumerate(chips):
            pltpu.make_async_remote_copy(
                src_ref=h_ref.at[j], dst_ref=zone.at[mine], send_sem=send_sems.at[k], recv_sem=recv_sems.at[k],
                device_id=(px, py, c), device_id_type=MESH).start()
        refs[-1][...] = jnp.zeros(refs[-1].shape, F32)

    sems, thru, token = _split_call(body, name, [pair_sums, land], (3, 3, 1), extra=(after,))
    return (sems, thru), token


def chip_wait(started, after, name):
    sems, thru = started

    def body(*refs):
        zone = refs[1]
        send_sems, recv_sems, local_sem = refs[2:5]
        _, c, chips = _other_chips()
        for k, (px, py, _) in enumerate(chips):
            cp = pltpu.make_async_remote_copy(
                src_ref=zone.at[0], dst_ref=zone.at[0], send_sem=send_sems.at[k], recv_sem=recv_sems.at[k],
                device_id=(px, py, c), device_id_type=MESH)
            cp.wait_send()
            cp.wait_recv()
        pltpu.make_async_copy(zone.at[0], zone.at[0], local_sem.at[0]).wait()

    return _split_call(body, name, thru, (3, 3, 1), extra=(*sems, after), with_token=False)[1]


def share_start(parts, after, name):
    n = len(parts)
    zones = [lax.empty((N_DEV,) + p.shape, p.dtype) for p in parts]

    def body(*refs):
        ins, zs = refs[:n], refs[n:2 * n]
        send_sems, recv_sems, local_sems = refs[2 * n + 1:2 * n + 4]
        me, peers = _peers()
        for i in range(n):
            pltpu.make_async_copy(ins[i], zs[i].at[me], local_sems.at[i]).start()
            for k, peer in enumerate(peers):
                pltpu.make_async_remote_copy(
                    src_ref=ins[i], dst_ref=zs[i].at[me], send_sem=send_sems.at[7 * i + k],
                    recv_sem=recv_sems.at[7 * i + k], device_id=peer, device_id_type=MESH).start()
        refs[-1][...] = jnp.zeros(refs[-1].shape, F32)

    sems, thru, token = _split_call(body, name, list(parts) + zones, (7 * n, 7 * n, n), extra=(after,))
    return (sems, thru, n), token


def share_wait(started, after, name):
    sems, thru, n = started

    def body(*refs):
        zs = refs[n:2 * n]
        send_sems, recv_sems, local_sems = refs[2 * n:2 * n + 3]
        _, peers = _peers()
        for i in range(n):
            for k, peer in enumerate(peers):
                cp = pltpu.make_async_remote_copy(
                    src_ref=zs[i].at[0], dst_ref=zs[i].at[0], send_sem=send_sems.at[7 * i + k],
                    recv_sem=recv_sems.at[7 * i + k], device_id=peer, device_id_type=MESH)
                cp.wait_send()
                cp.wait_recv()
            pltpu.make_async_copy(zs[i].at[0], zs[i].at[0], local_sems.at[i]).wait()

    return _split_call(body, name, thru, (7 * n, 7 * n, n), extra=(*sems, after), with_token=False)[n:]


def _adamw_math(w, g, m, v):
    m = ADAM_B1 * m + (1.0 - ADAM_B1) * g
    v = ADAM_B2 * v + (1.0 - ADAM_B2) * (g * g)
    m_hat = m / (1.0 - ADAM_B1 ** ADAM_STEP)
    v_hat = v / (1.0 - ADAM_B2 ** ADAM_STEP)
    delta = -ADAM_LR * (m_hat / (jnp.sqrt(v_hat) + ADAM_EPS) + ADAM_WD * w)
    return delta, m, v


ADAMW_BLOCK_BYTES = 24 * 1024 * 1024


def adamw_layer(zone, layer, items, after, name):
    n_src, nw, r, c = zone.shape
    depth = items[0][0].shape[0]
    prevs = [p if p is not None else tuple(lax.empty((depth, r, c), F32) for _ in range(4)) for _, _, _, p in items]
    row_bytes = 2 * nw * c * (2 * n_src + 4 * 7)
    tr = max(t for t in range(8, r + 1, 8) if r % t == 0 and t * row_bytes <= ADAMW_BLOCK_BYTES)

    def body(z_ref, *rest):
        ins, outs = rest[:3 * nw], rest[7 * nw + 1:]
        for i in range(nw):
            g = z_ref[0, i].astype(F32)
            for src in range(1, n_src):
                g = g + z_ref[src, i].astype(F32)
            g_ref, d_ref, mo_ref, vo_ref = outs[4 * i:4 * i + 4]
            w_ref, m_ref, v_ref = ins[3 * i:3 * i + 3]
            g_ref[...] = g
            d_ref[...], mo_ref[...], vo_ref[...] = _adamw_math(w_ref[...], g, m_ref[...], v_ref[...])

    rows = pl.BlockSpec((None, tr, c), lambda i: (layer, i, 0))
    anywhere = pl.BlockSpec(memory_space=pl.ANY)
    outs = pl.pallas_call(
        body, name=name, grid=(r // tr,),
        in_specs=[pl.BlockSpec((n_src, nw, tr, c), lambda i: (0, 0, i, 0))] + [rows] * (3 * nw)
                 + [anywhere] * (4 * nw + 1),
        out_specs=[rows] * (4 * nw),
        out_shape=[jax.ShapeDtypeStruct((depth, r, c), F32)] * (4 * nw),
        input_output_aliases={1 + 3 * nw + k: k for k in range(4 * nw)},
        compiler_params=_params(),
    )(zone, *[t for w, m, v, _ in items for t in (w, m, v)], *[t for p in prevs for t in p], after)
    return [tuple(outs[4 * i:4 * i + 4]) for i in range(nw)]


def adamw_small(ws, recvs, ms, vs, name):
    n = len(ws)

    def body(*refs):
        w_refs, r_refs, m_refs, v_refs = (refs[i * n:(i + 1) * n] for i in range(4))
        g_refs, d_refs, mo_refs, vo_refs = (refs[(4 + i) * n:(5 + i) * n] for i in range(4))
        for i in range(n):
            g = r_refs[i][0]
            for src in range(1, N_DEV):
                g = g + r_refs[i][src]
            g_refs[i][...] = g
            d_refs[i][...], mo_refs[i][...], vo_refs[i][...] = _adamw_math(w_refs[i][...], g, m_refs[i][...],
                                                                            v_refs[i][...])

    vm = pl.BlockSpec(memory_space=pltpu.VMEM)
    outs = pl.pallas_call(
        body, name=name, in_specs=[vm] * (4 * n), out_specs=[vm] * (4 * n),
        out_shape=[jax.ShapeDtypeStruct(w.shape, F32) for w in ws] * 4,
        compiler_params=pltpu.CompilerParams(vmem_limit_bytes=V7X_VMEM_LIMIT),
    )(*ws, *recvs, *ms, *vs)
    return [outs[i * n:(i + 1) * n] for i in range(4)]


SMALL_NAMES = ("ffn1_norm", "mix_norm", "ffn2_norm", "b_gate", "na_q_norm", "na_k_norm", "sw_q_norm", "sw_k_norm",
               "na_rpb", "sw_sink", "t5_rel_table")


def kernel(x, ffn1_norm, ffn1_w_gate, ffn1_w_up, ffn1_w_down, mix_norm, w_in, b_gate, na_q_norm, na_k_norm, na_rpb, sw_q_norm, sw_k_norm, sw_sink, t5_rel_table, w_branch_na, w_branch_sw, w_out, ffn2_norm, ffn2_w_gate, ffn2_w_up, ffn2_w_down, loss_target, m_ffn1_norm, m_ffn1_w_gate, m_ffn1_w_up, m_ffn1_w_down, m_mix_norm, m_w_in, m_b_gate, m_na_q_norm, m_na_k_norm, m_na_rpb, m_sw_q_norm, m_sw_k_norm, m_sw_sink, m_t5_rel_table, m_w_branch_na, m_w_branch_sw, m_w_out, m_ffn2_norm, m_ffn2_w_gate, m_ffn2_w_up, m_ffn2_w_down, v_ffn1_norm, v_ffn1_w_gate, v_ffn1_w_up, v_ffn1_w_down, v_mix_norm, v_w_in, v_b_gate, v_na_q_norm, v_na_k_norm, v_na_rpb, v_sw_q_norm, v_sw_k_norm, v_sw_sink, v_t5_rel_table, v_w_branch_na, v_w_branch_sw, v_w_out, v_ffn2_norm, v_ffn2_w_gate, v_ffn2_w_up, v_ffn2_w_down):
    weights = dict(ffn1_norm=ffn1_norm, ffn1_w_gate=ffn1_w_gate, ffn1_w_up=ffn1_w_up, ffn1_w_down=ffn1_w_down,
                   mix_norm=mix_norm, w_in=w_in, b_gate=b_gate, na_q_norm=na_q_norm, na_k_norm=na_k_norm,
                   na_rpb=na_rpb, sw_q_norm=sw_q_norm, sw_k_norm=sw_k_norm, sw_sink=sw_sink,
                   t5_rel_table=t5_rel_table, w_branch_na=w_branch_na, w_branch_sw=w_branch_sw, w_out=w_out,
                   ffn2_norm=ffn2_norm, ffn2_w_gate=ffn2_w_gate, ffn2_w_up=ffn2_w_up, ffn2_w_down=ffn2_w_down)
    mom_m = dict(ffn1_norm=m_ffn1_norm, ffn1_w_gate=m_ffn1_w_gate, ffn1_w_up=m_ffn1_w_up, ffn1_w_down=m_ffn1_w_down,
                 mix_norm=m_mix_norm, w_in=m_w_in, b_gate=m_b_gate, na_q_norm=m_na_q_norm, na_k_norm=m_na_k_norm,
                 na_rpb=m_na_rpb, sw_q_norm=m_sw_q_norm, sw_k_norm=m_sw_k_norm, sw_sink=m_sw_sink,
                 t5_rel_table=m_t5_rel_table, w_branch_na=m_w_branch_na, w_branch_sw=m_w_branch_sw, w_out=m_w_out,
                 ffn2_norm=m_ffn2_norm, ffn2_w_gate=m_ffn2_w_gate, ffn2_w_up=m_ffn2_w_up, ffn2_w_down=m_ffn2_w_down)
    mom_v = dict(ffn1_norm=v_ffn1_norm, ffn1_w_gate=v_ffn1_w_gate, ffn1_w_up=v_ffn1_w_up, ffn1_w_down=v_ffn1_w_down,
                 mix_norm=v_mix_norm, w_in=v_w_in, b_gate=v_b_gate, na_q_norm=v_na_q_norm, na_k_norm=v_na_k_norm,
                 na_rpb=v_na_rpb, sw_q_norm=v_sw_q_norm, sw_k_norm=v_sw_k_norm, sw_sink=v_sw_sink,
                 t5_rel_table=v_t5_rel_table, w_branch_na=v_w_branch_na, w_branch_sw=v_w_branch_sw, w_out=v_w_out,
                 ffn2_norm=v_ffn2_norm, ffn2_w_gate=v_ffn2_w_gate, ffn2_w_up=v_ffn2_w_up, ffn2_w_down=v_ffn2_w_down)
    order = list(weights)

    depth = ffn1_norm.shape[0]
    s, d = x.shape[1], x.shape[2]
    xs = x[0]
    tr = lambda w: jnp.swapaxes(w, -1, -2)

    merge = lambda t: t.reshape(t.shape[0], N_DEV * t.shape[2], t.shape[3])
    no_dep = jnp.zeros((8, LANES), F32)

    def shards_of(kind, l):
        stack = lambda *ws: jnp.stack(ws).astype(BF16)
        if kind == "ffn1":
            return [stack(tr(ffn1_w_gate[l]), tr(ffn1_w_up[l]), ffn1_w_down[l])]
        if kind == "win":
            return [stack(tr(w_in[l]))]
        return [stack(tr(ffn2_w_gate[l]), tr(ffn2_w_up[l]), ffn2_w_down[l]), stack(w_out[l]),
                stack(tr(w_branch_na[l]), tr(w_branch_sw[l]))]

    shards = {(kind, l): shards_of(kind, l) for l in range(depth) for kind in ("ffn1", "win", "rest")}

    def start(kind, l, after):
        return gather_start(shards[kind, l], after, f"gather_{kind}_{l}")

    def arrive(started, kind, l, after):
        zones = gather_wait(started, after, f"gather_{kind}_{l}_wait")
        return forward_start(zones, no_dep, f"forward_{kind}_{l}")

    def finish(fwd, kind, l, after):
        return [merge(z) for z in forward_wait(fwd, after, f"forward_{kind}_{l}_wait")]

    bd = jnp.asarray(np.kron(np.eye(MXU_TILE // HEAD_DIM), np.full((HEAD_DIM, HEAD_DIM), 1.0 / HEAD_DIM)), BF16)
    bmap = jnp.asarray(_t5_bucket_map())
    tile8 = lambda g: jnp.tile(g, NA_WIDTH // HEAD_DIM).reshape(1, NA_WIDTH)
    tile2 = lambda g: jnp.tile(g, SW_KV_WIDTH // HEAD_DIM).reshape(1, SW_KV_WIDTH)

    st_first, tok = start("ffn1", 0, no_dep)
    t5b = t5_expand(t5_rel_table, bmap, tok, "t5_expand").reshape(SW_EDGES, SW_STACK, 3 * SW_BLOCK)
    t2_tables = [rpb_expand(_rpb_rows(na_rpb[l]), tok, f"rpb_expand_{l}") for l in range(depth)]
    masks = na_masks(s // GRID_W, tok, "na_masks")
    qk_gains = [(tile8(na_q_norm[l]), tile8(na_k_norm[l]), tile8(sw_q_norm[l]), tile2(sw_k_norm[l]))
                for l in range(depth)]
    early = ([t[0, 0, 0:8, :] for t in t2_tables] + [masks[0, 0:8, 0:LANES]] + [t[0, 0:8, 0:LANES].astype(F32) for v in shards.values() for t in v]
             + [g[:, 0:LANES] for gs in qk_gains for g in gs])
    fwd, _ = arrive(st_first, "ffn1", 0, functools.reduce(jnp.add, early, t5b[0, 0:8, 0:LANES]))
    st_win, dep = start("win", 0, t5b)
    (first,) = finish(fwd, "ffn1", 0, dep)

    saved = []
    layer_w = {0: dict(wg1=(first, 0), wu1=(first, 1), wd1=(first, 2))}
    cur = xs
    for l in range(depth):
        sv = {}
        lw = layer_w[l]
        sv["x0"] = cur
        cur, sv["xn1"], sv["hg1"], sv["hu1"], sv["act1"] = ffn_forward(
            cur, ffn1_norm[l][None], lw["wg1"], lw["wu1"], lw["wd1"], dep, f"ffn1_{l}")
        sv["x1"] = cur
        fwd, _ = arrive(st_win, "win", l, cur)
        st_rest, tok = start("rest", l, cur)
        (zb,) = finish(fwd, "win", l, tok)
        lw["win"] = (zb, 0)
        sv["gains"] = qk_gains[l]
        sv["hn"], sv["zq"], sv["qa"], sv["ka"], sv["qs"], sv["ks"], sv["gt"] = mix_in(
            cur, mix_norm[l][None], lw["win"], b_gate[l][None], *sv["gains"], bd, f"mix_in_{l}")
        sv["t2"] = t2_tables[l]
        sv["o_na"] = na_fwd(sv["qa"], sv["ka"], sv["zq"], sv["t2"], masks, f"na_fwd_{l}")
        dep = no_dep
        if l + 1 < depth:
            st_ffn1, dep = start("ffn1", l + 1, sv["o_na"])
        sv["o_sw"] = sw_fwd(sv["qs"], sv["ks"], sv["zq"], t5b, sw_sink[l], dep, f"sw_fwd_{l}")
        fwd, tok = arrive(st_rest, "rest", l, sv["o_sw"][0:8, 0:LANES] + sv["o_na"][0:8, 0:LANES])
        za, zc, zd = finish(fwd, "rest", l, tok)
        lw.update(wg2=(za, 0), wu2=(za, 1), wd2=(za, 2), wout=(zc, 0), wna=(zd, 0), wsw=(zd, 1))
        cur, sv["a_na"], sv["a_sw"], sv["merged"] = merge_out(
            cur, sv["o_na"], sv["o_sw"], sv["gt"], lw["wna"], lw["wsw"], lw["wout"], f"merge_out_{l}")
        sv["x2"] = cur
        if l + 1 < depth:
            st_win, dep = start("win", l + 1, cur)
            sv["xn2"], sv["hg2"], sv["hu2"], sv["act2"] = ffn_forward(
                cur, ffn2_norm[l][None], lw["wg2"], lw["wu2"], None, dep, f"ffn2_up_{l}")
            fwd, dep = arrive(st_ffn1, "ffn1", l + 1, sv["act2"])
            cur = ffn_down(cur, sv["act2"], lw["wd2"], dep, f"ffn2_down_{l}")
            (za,) = finish(fwd, "ffn1", l + 1, cur)
            layer_w[l + 1] = dict(wg1=(za, 0), wu1=(za, 1), wd1=(za, 2))
        else:
            dx, loss_acc, sv["xn2"], sv["hg2"], sv["hu2"], sv["act2"] = ffn_forward(
                cur, ffn2_norm[l][None], lw["wg2"], lw["wu2"], lw["wd2"], no_dep, f"ffn2_{l}",
                target=loss_target[0])
        dep = no_dep
        saved.append(sv)

    split = lambda t: t.reshape(N_DEV, t.shape[0] // N_DEV, t.shape[1])
    pending = {}
    last_key = "ffn1_0"
    two_level = {last_key}
    small = {k: [None] * depth for k in SMALL_NAMES if k != "t5_rel_table"}
    dbias_sw = []
    for l in reversed(range(depth)):
        sv = saved[l]
        lw = layer_w[l]
        wg1, wu1, wd1, wg2, wu2, wd2 = (lw[k] for k in ("wg1", "wu1", "wd1", "wg2", "wu2", "wd2"))
        win_t, wout_l, wna_t, wsw_t = lw["win"], lw["wout"], lw["wna"], lw["wsw"]
        blocks = ((2, "x2", "xn2", "hg2", "hu2", "act2", wg2, wu2, wd2, "ffn2_norm", 3),
                  (1, "x0", "xn1", "hg1", "hu1", "act1", wg1, wu1, wd1, "ffn1_norm", 0))

        def ffn_backward(dx, blk):
            tag, xk, xnk, hgk, huk, actk, wg, wu, wd, norm_name, slot = blk
            gains = weights[norm_name]
            dxb, dhg, dhu = ffn_bwd_act(dx, wd, sv[hgk], sv[huk], f"ffn{tag}_bwd_act_{l}")
            gwg, gwu, gwd = tn_matmul([(dhg, sv[xnk], 1.0), (dhu, sv[xnk], 1.0), (sv[actk], dxb, 0.5)],
                                      f"ffn{tag}_dw_{l}")
            key = f"ffn{tag}_{l}"
            blocks_of = [split(gwg), split(gwu), split(gwd)]
            if key in two_level:
                paired, token = pair_start(blocks_of, dxb, f"pair_{key}")
            else:
                pending[key], token = scatter_start([blocks_of], f"scatter_{key}")
            dx, dg = proj_bwd_norm([dhg, dhu], [wg, wu], sv[xk], gains[l][None], dx, token, f"ffn{tag}_bwd_x_{l}")
            token = no_dep
            if key in two_level:
                thru, land = pair_wait(paired, dx, f"pair_{key}_wait")
                pending[key], token = chip_start(pair_sum(thru, land, f"pair_sum_{key}"), dg, f"chips_{key}")
            small[norm_name][l] = dg[0]
            return dx, token

        dx, token = ffn_backward(dx, blocks[0])
        dxb, dzg, da_na, da_sw, do_na, do_sw, dbg = mix_bwd_out(
            dx, sv["gt"], sv["a_na"], sv["a_sw"], wna_t, wsw_t, wout_l, token, f"mix_bwd_out_{l}")
        small["b_gate"][l] = dbg[0]
        gwout, gwna, gwsw = tn_matmul([(sv["merged"], dxb, 1.0), (da_na, sv["o_na"], 1.0), (da_sw, sv["o_sw"], 1.0)],
                                      f"mix_dw_{l}")
        dqa, dka, dva, dt2 = na_bwd(sv["qa"], sv["ka"], sv["zq"], sv["t2"], masks, sv["o_na"], do_na, f"na_bwd_{l}")
        dqs, dks, dvs, dbias, dsink = sw_bwd(sv["qs"], sv["ks"], sv["zq"], t5b, sw_sink[l], sv["o_sw"], do_sw,
                                             f"sw_bwd_{l}")
        dbias_sw.append(dbias.reshape(SW_HEADS, SW_BLOCK, 3 * SW_BLOCK))
        small["sw_sink"][l] = jnp.sum(dsink[:, 0].reshape(SW_HEADS, SW_BLOCK), axis=1)
        small["na_rpb"][l] = _rpb_from_rows(rpb_reduce(dt2, f"rpb_reduce_{l}"))
        dz, dgqa, dgka, dgqs, dgks = qk_norm_bwd(dqa, dka, dva, dqs, dks, dvs, sv["zq"], dzg, *sv["gains"], bd,
                                                 f"qk_norm_bwd_{l}")
        fold = lambda g: jnp.sum(g.reshape(-1, HEAD_DIM), axis=0)
        small["na_q_norm"][l], small["na_k_norm"][l] = fold(dgqa), fold(dgka)
        small["sw_q_norm"][l], small["sw_k_norm"][l] = fold(dgqs), fold(dgks)
        (gwin,) = tn_matmul([(dz, sv["hn"], 1.0)], f"dwin_{l}")
        pending[f"mix_{l}"], token = scatter_start([[split(gwout)], [split(gwna), split(gwsw)], [split(gwin)]],
                                                   f"scatter_mix_{l}")
        dx, dg = proj_bwd_norm([dz], [win_t], sv["x1"], mix_norm[l][None], dx, token, f"mix_bwd_x_{l}")
        small["mix_norm"][l] = dg[0]
        dx, tail = ffn_backward(dx, blocks[1])

    dtab = t5_reduce(dbias_sw, bmap, "t5_reduce")
    small_parts = {k: jnp.stack(v) for k, v in small.items()}
    small_parts["t5_rel_table"] = jnp.transpose(dtab[:, :, 0])

    grads, delta, new_m, new_v = {}, {}, {}, {}
    state = {}
    sharing, token = share_start([small_parts[k] for k in SMALL_NAMES] + [loss_acc], tail, "share_small")
    chain = [token]
    members = {"ffn": lambda t: [(f"ffn{t}_w_gate", 0, 0, True), (f"ffn{t}_w_up", 0, 1, True),
                                 (f"ffn{t}_w_down", 0, 2, False)],
               "mix": lambda t: [("w_out", 0, 0, False), ("w_branch_na", 1, 0, True), ("w_branch_sw", 1, 1, True),
                                 ("w_in", 2, 0, True)]}

    def collect(key):
        if key in two_level:
            zones = [chip_wait(pending[key], chain[0], f"wait_{key}")]
        else:
            zones = scatter_wait(pending[key], chain[0], f"wait_{key}")
        kind, l = key.split("_")
        group = members[kind[:3]](kind[3:])
        complete = all(f"{kind}_{j}" in done for j in range(depth) if j != int(l))
        for zi, zone in enumerate(zones):
            mine = sorted((wi, k, transposed) for k, z, wi, transposed in group if z == zi)
            views = [tr if transposed else (lambda t: t) for _, _, transposed in mine]
            items = [(view(weights[k]), view(mom_m[k]), view(mom_v[k]), state.get(k))
                     for (_, k, _), view in zip(mine, views)]
            results = adamw_layer(zone, int(l), items, chain[0], f"adamw_{key}_{zi}")
            chain[0] = results[-1][1]
            for (_, k, _), view, res in zip(mine, views, results):
                state[k] = res
                if complete:
                    grads[k], delta[k], new_m[k], new_v[k] = (view(t) for t in res)
        done.add(key)

    done = set()
    for key in pending:
        if key != last_key:
            collect(key)
    collect(last_key)
    *recvs, all_losses = share_wait(sharing, chain[0], "share_small_wait")
    loss = jnp.sum(all_losses) * (0.5 / d)
    results = adamw_small([weights[k] for k in SMALL_NAMES], recvs, [mom_m[k] for k in SMALL_NAMES],
                          [mom_v[k] for k in SMALL_NAMES], "adamw_small")
    for dst, outs in zip((grads, delta, new_m, new_v), results):
        dst.update(dict(zip(SMALL_NAMES, outs)))

    return (loss, dx[None], *[grads[k] for k in order], *[delta[k] for k in order],
            *[new_m[k] for k in order], *[new_v[k] for k in order])
```
